```python
import math
import jax, jax.numpy as jnp
from jax import lax
import numpy as np

D_MODEL = 1024
BATCH = 16
SEQ = 4096
DEPTH = 1

SSM_EXPAND = 2
SSM_D_INNER = SSM_EXPAND * D_MODEL
SSM_HEAD_DIM = 64
SSM_N_HEADS = SSM_D_INNER // SSM_HEAD_DIM
SSM_N_GROUPS = 4
SSM_D_STATE = 128
SSM_CONV = 4
SSM_CHUNK = 128
SSM_CONV_DIM = SSM_D_INNER + 2 * SSM_N_GROUPS * SSM_D_STATE

ATT_HEAD_DIM = 128
ATT_HEADS_PER_GROUP = 4
ATT_PATTERNS = ((128, 1), (512, 4), (2048, 16))
ATT_N_HEADS = ATT_HEADS_PER_GROUP * len(ATT_PATTERNS)
ATT_QKV_DIM = 3 * ATT_N_HEADS * ATT_HEAD_DIM
ATT_OUT_DIM = ATT_HEADS_PER_GROUP * ATT_HEAD_DIM
ATT_BLOCK = 128
ROPE_THETA = 10000.0

N_BRANCH = 2
D_FF = -(-8 * D_MODEL // (3 * 256)) * 256
IN_PROJ_SIZES = (SSM_D_INNER, SSM_CONV_DIM, SSM_N_HEADS, ATT_QKV_DIM, N_BRANCH * D_MODEL)
IN_PROJ_DIM = sum(IN_PROJ_SIZES)
EPS = 1e-6

kernel_name = "hybrid_ssd_dilated_swa_block"


def rmsnorm(x, g):
    xf = x.astype(jnp.float32)
    y = xf * lax.rsqrt(jnp.mean(xf * xf, axis=-1, keepdims=True) + EPS)
    return (y * g.astype(jnp.float32)).astype(x.dtype)


def rope(t, pos):
    half = t.shape[-1] // 2
    inv = ROPE_THETA ** (-jnp.arange(half, dtype=jnp.float32) / half)
    ang = pos.astype(jnp.float32)[:, None] * inv[None, :]
    cos = jnp.cos(ang)[None, :, None, :]
    sin = jnp.sin(ang)[None, :, None, :]
    t1, t2 = t[..., :half], t[..., half:]
    return jnp.concatenate([t1 * cos - t2 * sin, t2 * cos + t1 * sin], axis=-1)


def segsum(a):
    T = a.shape[-1]
    xx = jnp.broadcast_to(a[..., :, None], a.shape + (T,))
    cs = jnp.cumsum(jnp.where(jnp.tril(jnp.ones((T, T), bool), -1), xx, 0.0), axis=-2)
    return jnp.where(jnp.tril(jnp.ones((T, T), bool)), cs, -jnp.inf)


def causal_depthwise_conv(u, w, b):
    K, C = w.shape
    out = lax.conv_general_dilated(u, w[:, None, :], window_strides=(1,), padding=[(K - 1, 0)],
                                   dimension_numbers=('NWC', 'WIO', 'NWC'), feature_group_count=C)
    return out + b


def ssd_chunked(xs, dt, A, Bm, Cm):
    b, S, H, P = xs.shape
    G, N = Bm.shape[-2:]
    J = H // G
    Q = SSM_CHUNK
    nc = S // Q
    xdt = (xs * dt[..., None]).reshape(b, nc, Q, G, J, P)
    a = (dt * A).reshape(b, nc, Q, G, J).transpose(0, 1, 3, 4, 2)
    a_cs = jnp.cumsum(a, axis=-1)
    Br = Bm.reshape(b, nc, Q, G, N)
    Cr = Cm.reshape(b, nc, Q, G, N)
    tri = jnp.tril(jnp.ones((Q, Q), bool))
    Lmat = jnp.exp(jnp.where(tri, a_cs[..., :, None] - a_cs[..., None, :], -jnp.inf))
    CB = jnp.einsum('bclgn,bcsgn->bcgls', Cr, Br)
    y_diag = jnp.einsum('bcgjls,bcsgjp->bclgjp', CB[:, :, :, None] * Lmat, xdt)
    decay_states = jnp.exp(a_cs[..., -1:] - a_cs)
    states = jnp.einsum('bclgn,bcgjl,bclgjp->bcgjpn', Br, decay_states, xdt)
    chunk_tot = jnp.pad(a_cs[..., -1].transpose(0, 2, 3, 1), ((0, 0), (0, 0), (0, 0), (1, 0)))
    decay_chunk = jnp.exp(segsum(chunk_tot))
    states = jnp.concatenate([jnp.zeros_like(states[:, :1]), states], axis=1)
    new_states = jnp.einsum('bgjzc,bcgjpn->bzgjpn', decay_chunk, states)
    prev_states = new_states[:, :-1]
    y_off = jnp.einsum('bclgn,bcgjpn,bcgjl->bclgjp', Cr, prev_states, jnp.exp(a_cs))
    return (y_diag + y_off).reshape(b, S, H, P)


def mamba2_branch(z, xBC, dt_raw, conv_w, conv_b, dt_bias, a_log, d_skip, ssm_norm):
    b, S, _ = z.shape
    xBC = jax.nn.silu(causal_depthwise_conv(xBC, conv_w, conv_b)).astype(jnp.float32)
    gn = SSM_N_GROUPS * SSM_D_STATE
    xs = xBC[..., :SSM_D_INNER].reshape(b, S, SSM_N_HEADS, SSM_HEAD_DIM)
    Bm = xBC[..., SSM_D_INNER:SSM_D_INNER + gn].reshape(b, S, SSM_N_GROUPS, SSM_D_STATE)
    Cm = xBC[..., SSM_D_INNER + gn:].reshape(b, S, SSM_N_GROUPS, SSM_D_STATE)
    dt = jax.nn.softplus(dt_raw.astype(jnp.float32) + dt_bias.astype(jnp.float32))
    A = -jnp.exp(a_log.astype(jnp.float32))
    y = ssd_chunked(xs, dt, A, Bm, Cm) + d_skip.astype(jnp.float32)[:, None] * xs
    y = y.reshape(b, S, SSM_D_INNER) * jax.nn.silu(z.astype(jnp.float32))
    yg = y.reshape(b, S, SSM_N_GROUPS, SSM_D_INNER // SSM_N_GROUPS)
    yg = yg * lax.rsqrt(jnp.mean(yg * yg, axis=-1, keepdims=True) + EPS)
    y = yg.reshape(b, S, SSM_D_INNER) * ssm_norm.astype(jnp.float32)
    return y.astype(z.dtype)


def dilated_window_group(q, k, v, window, dilation):
    b, S, h, d = q.shape
    r = dilation
    w_sub = window // r
    L = S // r
    nb = -(-L // ATT_BLOCK)
    Lp = nb * ATT_BLOCK

    def to_blocks(t):
        t = t.reshape(b, L, r, h, d).transpose(0, 2, 1, 3, 4)
        t = jnp.pad(t, ((0, 0), (0, 0), (0, Lp - L), (0, 0), (0, 0)))
        return t.reshape(b, r, nb, ATT_BLOCK, h, d)

    def with_prev(t):
        prev = jnp.pad(t[:, :, :-1], ((0, 0), (0, 0), (1, 0), (0, 0), (0, 0), (0, 0)))
        return jnp.concatenate([prev, t], axis=3)

    qb = to_blocks(q)
    kk = with_prev(to_blocks(k))
    vv = with_prev(to_blocks(v))
    s = jnp.einsum('brnqhd,brnkhd->brnhqk', qb, kk) * (d ** -0.5)
    qi = jnp.arange(ATT_BLOCK)[:, None]
    kj = jnp.arange(2 * ATT_BLOCK)[None, :]
    dist = qi + ATT_BLOCK - kj
    band = (dist >= 0) & (dist <= w_sub)
    has_prev = (jnp.arange(nb) > 0)[:, None, None] | (kj >= ATT_BLOCK)[None]
    mask = band[None] & has_prev
    s = jnp.where(mask[None, None, :, None], s, -jnp.inf)
    m = jnp.max(s, axis=-1, keepdims=True)
    p = jnp.exp(s - m)
    den = jnp.sum(p, axis=-1, keepdims=True)
    o = jnp.einsum('brnhqk,brnkhd->brnqhd', p / den, vv)
    lse = (m + jnp.log(den))[..., 0].transpose(0, 1, 2, 4, 3)
    o = o.reshape(b, r, Lp, h, d)[:, :, :L].transpose(0, 2, 1, 3, 4).reshape(b, S, h, d)
    lse = lse.reshape(b, r, Lp, h)[:, :, :L].transpose(0, 2, 1, 3).reshape(b, S, h)
    return o, lse


def dilated_attention_branch(q, k, v):
    outs, lses = [], []
    for gi, (window, dilation) in enumerate(ATT_PATTERNS):
        sl = slice(gi * ATT_HEADS_PER_GROUP, (gi + 1) * ATT_HEADS_PER_GROUP)
        o, lse = dilated_window_group(q[:, :, sl], k[:, :, sl], v[:, :, sl], window, dilation)
        outs.append(o)
        lses.append(lse)
    o = jnp.stack(outs, axis=0)
    wts = jax.nn.softmax(jnp.stack(lses, axis=0), axis=0)
    return jnp.sum(wts[..., None] * o, axis=0)


def _fwd_setup_inputs(seed: int = 0) -> dict:
    key = jax.random.key(seed)
    ks = jax.random.split(key, 20)
    f32 = jnp.float32

    def nrm(k, shape, scale):
        return jax.random.normal(k, shape, f32) * scale

    dt0 = jnp.exp(jax.random.uniform(ks[6], (DEPTH, SSM_N_HEADS), f32,
                                     minval=math.log(1e-3), maxval=math.log(1e-1)))
    return {
        "x": nrm(ks[0], (BATCH, SEQ, D_MODEL), 1.0),
        "norm_mix": 1.0 + nrm(ks[1], (DEPTH, D_MODEL), 0.05),
        "w_in": nrm(ks[2], (DEPTH, D_MODEL, IN_PROJ_DIM), D_MODEL ** -0.5),
        "b_gate": nrm(ks[3], (DEPTH, N_BRANCH * D_MODEL), 0.01),
        "conv_w": nrm(ks[4], (DEPTH, SSM_CONV, SSM_CONV_DIM), SSM_CONV ** -0.5),
        "conv_b": nrm(ks[5], (DEPTH, SSM_CONV_DIM), 0.01),
        "dt_bias": dt0 + jnp.log(-jnp.expm1(-dt0)),
        "a_log": jnp.log(jax.random.uniform(ks[7], (DEPTH, SSM_N_HEADS), f32, minval=1.0, maxval=16.0)),
        "d_skip": 1.0 + nrm(ks[8], (DEPTH, SSM_N_HEADS), 0.1),
        "ssm_norm": 1.0 + nrm(ks[9], (DEPTH, SSM_D_INNER), 0.05),
        "w_ssm_out": nrm(ks[10], (DEPTH, SSM_D_INNER, D_MODEL), SSM_D_INNER ** -0.5),
        "w_att_out": nrm(ks[11], (DEPTH, ATT_OUT_DIM, D_MODEL), ATT_OUT_DIM ** -0.5),
        "w_mix_out": nrm(ks[12], (DEPTH, D_MODEL, D_MODEL), D_MODEL ** -0.5),
        "norm_ffn": 1.0 + nrm(ks[13], (DEPTH, D_MODEL), 0.05),
        "w_ffn_gate": nrm(ks[14], (DEPTH, D_MODEL, D_FF), D_MODEL ** -0.5),
        "w_ffn_up": nrm(ks[15], (DEPTH, D_MODEL, D_FF), D_MODEL ** -0.5),
        "w_ffn_down": nrm(ks[16], (DEPTH, D_FF, D_MODEL), D_FF ** -0.5),
        "norm_final": 1.0 + nrm(ks[17], (D_MODEL,), 0.05),
    }


def _fwd_reference(x, norm_mix, w_in, b_gate, conv_w, conv_b, dt_bias, a_log, d_skip, ssm_norm,
              w_ssm_out, w_att_out, w_mix_out, norm_ffn, w_ffn_gate, w_ffn_up, w_ffn_down, norm_final):
    b, S, _ = x.shape
    pos = jnp.arange(S)
    offs = [0]
    for sz in IN_PROJ_SIZES[:-1]:
        offs.append(offs[-1] + sz)
    for l in range(DEPTH):
        h = rmsnorm(x, norm_mix[l])
        proj = h @ w_in[l]
        z, xBC, dt_raw, qkv, gate_logits = jnp.split(proj, offs[1:], axis=-1)

        y_ssm = mamba2_branch(z, xBC, dt_raw, conv_w[l], conv_b[l], dt_bias[l], a_log[l],
                              d_skip[l], ssm_norm[l]) @ w_ssm_out[l]

        qkv = qkv.astype(jnp.float32).reshape(b, S, 3, ATT_N_HEADS, ATT_HEAD_DIM)
        q = rope(qkv[:, :, 0], pos)
        k = rope(qkv[:, :, 1], pos)
        v = qkv[:, :, 2]
        y_att = dilated_attention_branch(q, k, v).reshape(b, S, ATT_OUT_DIM).astype(x.dtype) @ w_att_out[l]

        gates = jax.nn.sigmoid((gate_logits + b_gate[l]).astype(jnp.float32))
        gates = gates.reshape(b, S, N_BRANCH, D_MODEL).astype(x.dtype)
        mixed = gates[:, :, 0] * y_ssm + gates[:, :, 1] * y_att
        x = x + mixed @ w_mix_out[l]

        h = rmsnorm(x, norm_ffn[l])
        x = x + (jax.nn.silu(h @ w_ffn_gate[l]) * (h @ w_ffn_up[l])) @ w_ffn_down[l]
    return rmsnorm(x, norm_final)


import jax as _jax
import jax.numpy as _jnp

TWIN_FORMAT = 'train_step'
FWD_PARAMS = ['x', 'norm_mix', 'w_in', 'b_gate', 'conv_w', 'conv_b', 'dt_bias', 'a_log', 'd_skip', 'ssm_norm', 'w_ssm_out', 'w_att_out', 'w_mix_out', 'norm_ffn', 'w_ffn_gate', 'w_ffn_up', 'w_ffn_down', 'norm_final']
TWIN_WEIGHTS = ['norm_mix', 'w_in', 'b_gate', 'conv_w', 'conv_b', 'dt_bias', 'a_log', 'd_skip', 'ssm_norm', 'w_ssm_out', 'w_att_out', 'w_mix_out', 'norm_ffn', 'w_ffn_gate', 'w_ffn_up', 'w_ffn_down', 'norm_final']
TWIN_DIFF_INPUT = 'x'
TWIN_INPUTS = ['x', 'norm_mix', 'w_in', 'b_gate', 'conv_w', 'conv_b', 'dt_bias', 'a_log', 'd_skip', 'ssm_norm', 'w_ssm_out', 'w_att_out', 'w_mix_out', 'norm_ffn', 'w_ffn_gate', 'w_ffn_up', 'w_ffn_down', 'norm_final', 'loss_target', 'm_norm_mix', 'm_w_in', 'm_b_gate', 'm_conv_w', 'm_conv_b', 'm_dt_bias', 'm_a_log', 'm_d_skip', 'm_ssm_norm', 'm_w_ssm_out', 'm_w_att_out', 'm_w_mix_out', 'm_norm_ffn', 'm_w_ffn_gate', 'm_w_ffn_up', 'm_w_ffn_down', 'm_norm_final', 'v_norm_mix', 'v_w_in', 'v_b_gate', 'v_conv_w', 'v_conv_b', 'v_dt_bias', 'v_a_log', 'v_d_skip', 'v_ssm_norm', 'v_w_ssm_out', 'v_w_att_out', 'v_w_mix_out', 'v_norm_ffn', 'v_w_ffn_gate', 'v_w_ffn_up', 'v_w_ffn_down', 'v_norm_final']
TWIN_OUTPUTS = ['loss', 'grad_x', 'grad_norm_mix', 'grad_w_in', 'grad_b_gate', 'grad_conv_w', 'grad_conv_b', 'grad_dt_bias', 'grad_a_log', 'grad_d_skip', 'grad_ssm_norm', 'grad_w_ssm_out', 'grad_w_att_out', 'grad_w_mix_out', 'grad_norm_ffn', 'grad_w_ffn_gate', 'grad_w_ffn_up', 'grad_w_ffn_down', 'grad_norm_final', 'delta_norm_mix', 'delta_w_in', 'delta_b_gate', 'delta_conv_w', 'delta_conv_b', 'delta_dt_bias', 'delta_a_log', 'delta_d_skip', 'delta_ssm_norm', 'delta_w_ssm_out', 'delta_w_att_out', 'delta_w_mix_out', 'delta_norm_ffn', 'delta_w_ffn_gate', 'delta_w_ffn_up', 'delta_w_ffn_down', 'delta_norm_final', 'new_m_norm_mix', 'new_m_w_in', 'new_m_b_gate', 'new_m_conv_w', 'new_m_conv_b', 'new_m_dt_bias', 'new_m_a_log', 'new_m_d_skip', 'new_m_ssm_norm', 'new_m_w_ssm_out', 'new_m_w_att_out', 'new_m_w_mix_out', 'new_m_norm_ffn', 'new_m_w_ffn_gate', 'new_m_w_ffn_up', 'new_m_w_ffn_down', 'new_m_norm_final', 'new_v_norm_mix', 'new_v_w_in', 'new_v_b_gate', 'new_v_conv_w', 'new_v_conv_b', 'new_v_dt_bias', 'new_v_a_log', 'new_v_d_skip', 'new_v_ssm_norm', 'new_v_w_ssm_out', 'new_v_w_att_out', 'new_v_w_mix_out', 'new_v_norm_ffn', 'new_v_w_ffn_gate', 'new_v_w_ffn_up', 'new_v_w_ffn_down', 'new_v_norm_final']
TWIN_LEAF_KINDS = {'loss': 'loss', 'grad_x': 'grad_x', 'grad_norm_mix': 'grad_w', 'grad_w_in': 'grad_w', 'grad_b_gate': 'grad_w', 'grad_conv_w': 'grad_w', 'grad_conv_b': 'grad_w', 'grad_dt_bias': 'grad_w', 'grad_a_log': 'grad_w', 'grad_d_skip': 'grad_w', 'grad_ssm_norm': 'grad_w', 'grad_w_ssm_out': 'grad_w', 'grad_w_att_out': 'grad_w', 'grad_w_mix_out': 'grad_w', 'grad_norm_ffn': 'grad_w', 'grad_w_ffn_gate': 'grad_w', 'grad_w_ffn_up': 'grad_w', 'grad_w_ffn_down': 'grad_w', 'grad_norm_final': 'grad_w', 'delta_norm_mix': 'delta_w', 'delta_w_in': 'delta_w', 'delta_b_gate': 'delta_w', 'delta_conv_w': 'delta_w', 'delta_conv_b': 'delta_w', 'delta_dt_bias': 'delta_w', 'delta_a_log': 'delta_w', 'delta_d_skip': 'delta_w', 'delta_ssm_norm': 'delta_w', 'delta_w_ssm_out': 'delta_w', 'delta_w_att_out': 'delta_w', 'delta_w_mix_out': 'delta_w', 'delta_norm_ffn': 'delta_w', 'delta_w_ffn_gate': 'delta_w', 'delta_w_ffn_up': 'delta_w', 'delta_w_ffn_down': 'delta_w', 'delta_norm_final': 'delta_w', 'new_m_norm_mix': 'new_m', 'new_m_w_in': 'new_m', 'new_m_b_gate': 'new_m', 'new_m_conv_w': 'new_m', 'new_m_conv_b': 'new_m', 'new_m_dt_bias': 'new_m', 'new_m_a_log': 'new_m', 'new_m_d_skip': 'new_m', 'new_m_ssm_norm': 'new_m', 'new_m_w_ssm_out': 'new_m', 'new_m_w_att_out': 'new_m', 'new_m_w_mix_out': 'new_m', 'new_m_norm_ffn': 'new_m', 'new_m_w_ffn_gate': 'new_m', 'new_m_w_ffn_up': 'new_m', 'new_m_w_ffn_down': 'new_m', 'new_m_norm_final': 'new_m', 'new_v_norm_mix': 'new_v', 'new_v_w_in': 'new_v', 'new_v_b_gate': 'new_v', 'new_v_conv_w': 'new_v', 'new_v_conv_b': 'new_v', 'new_v_dt_bias': 'new_v', 'new_v_a_log': 'new_v', 'new_v_d_skip': 'new_v', 'new_v_ssm_norm': 'new_v', 'new_v_w_ssm_out': 'new_v', 'new_v_w_att_out': 'new_v', 'new_v_w_mix_out': 'new_v', 'new_v_norm_ffn': 'new_v', 'new_v_w_ffn_gate': 'new_v', 'new_v_w_ffn_up': 'new_v', 'new_v_w_ffn_down': 'new_v', 'new_v_norm_final': 'new_v'}


def _forward(args):
    return _fwd_reference(*[args[k] for k in FWD_PARAMS])


def _output_shape():
    out = _jax.eval_shape(lambda: _forward(_fwd_setup_inputs(0)))
    return out.shape, out.dtype

N_MICROBATCH = 1
ADAM_LR = 0.001
ADAM_B1 = 0.9
ADAM_B2 = 0.999
ADAM_EPS = 1e-08
ADAM_WD = 0.01
ADAM_STEP = 10
PER_EXAMPLE_BATCH_AXIS = {'x': 0, 'loss_target': 0}
SHARED_INPUTS = []
_WEIGHT_DTYPES = {'norm_mix': _jnp.float32, 'w_in': _jnp.float32, 'b_gate': _jnp.float32, 'conv_w': _jnp.float32, 'conv_b': _jnp.float32, 'dt_bias': _jnp.float32, 'a_log': _jnp.float32, 'd_skip': _jnp.float32, 'ssm_norm': _jnp.float32, 'w_ssm_out': _jnp.float32, 'w_att_out': _jnp.float32, 'w_mix_out': _jnp.float32, 'norm_ffn': _jnp.float32, 'w_ffn_gate': _jnp.float32, 'w_ffn_up': _jnp.float32, 'w_ffn_down': _jnp.float32, 'norm_final': _jnp.float32}
MOMENT_SCALE = {'norm_mix': 2.074209e-01, 'w_in': 6.211306e-02, 'b_gate': 3.785716e-02, 'conv_w': 8.037365e-02, 'conv_b': 1.143570e-01, 'dt_bias': 4.147982e-01, 'a_log': 4.286097e-01, 'd_skip': 4.019025e-01, 'ssm_norm': 9.588224e-02, 'w_ssm_out': 1.370216e-01, 'w_att_out': 2.488059e-02, 'w_mix_out': 1.356207e-01, 'norm_ffn': 1.677647e-01, 'w_ffn_gate': 7.346289e-02, 'w_ffn_up': 7.155388e-02, 'w_ffn_down': 1.193232e-01, 'norm_final': 6.416957e+01}


def _to_microbatches(a, axis):
    t = _jnp.moveaxis(a, axis, 0)
    t = t.reshape((N_MICROBATCH, t.shape[0] // N_MICROBATCH) + t.shape[1:])
    return _jnp.moveaxis(t, 1, axis + 1)


def setup_inputs(seed: int = 0) -> dict:
    inp = _fwd_setup_inputs(seed)
    key = _jax.random.fold_in(_jax.random.key(seed), 7919)
    shape, _ = _output_shape()
    out = dict(inp)
    out["loss_target"] = _jax.random.normal(_jax.random.fold_in(key, 0), shape, _jnp.float32)
    for i, name in enumerate(TWIN_WEIGHTS):
        w = inp[name].astype(_jnp.float32)
        if MOMENT_SCALE is None:
            s = _jnp.sqrt(_jnp.mean(_jnp.square(w)) + 1e-30)
        else:
            s = MOMENT_SCALE[name]
        km, kv = _jax.random.split(_jax.random.fold_in(key, i + 1))
        out[name] = w
        out["m_" + name] = s * _jax.random.normal(km, w.shape, _jnp.float32)
        out["v_" + name] = (s * s) * _jax.random.uniform(kv, w.shape, _jnp.float32, 0.5, 1.5)
    if N_MICROBATCH > 1:
        for name, axis in PER_EXAMPLE_BATCH_AXIS.items():
            out[name] = _to_microbatches(out[name], axis)
    return {'x': out['x'], 'norm_mix': out['norm_mix'], 'w_in': out['w_in'], 'b_gate': out['b_gate'], 'conv_w': out['conv_w'], 'conv_b': out['conv_b'], 'dt_bias': out['dt_bias'], 'a_log': out['a_log'], 'd_skip': out['d_skip'], 'ssm_norm': out['ssm_norm'], 'w_ssm_out': out['w_ssm_out'], 'w_att_out': out['w_att_out'], 'w_mix_out': out['w_mix_out'], 'norm_ffn': out['norm_ffn'], 'w_ffn_gate': out['w_ffn_gate'], 'w_ffn_up': out['w_ffn_up'], 'w_ffn_down': out['w_ffn_down'], 'norm_final': out['norm_final'], 'loss_target': out['loss_target'], 'm_norm_mix': out['m_norm_mix'], 'm_w_in': out['m_w_in'], 'm_b_gate': out['m_b_gate'], 'm_conv_w': out['m_conv_w'], 'm_conv_b': out['m_conv_b'], 'm_dt_bias': out['m_dt_bias'], 'm_a_log': out['m_a_log'], 'm_d_skip': out['m_d_skip'], 'm_ssm_norm': out['m_ssm_norm'], 'm_w_ssm_out': out['m_w_ssm_out'], 'm_w_att_out': out['m_w_att_out'], 'm_w_mix_out': out['m_w_mix_out'], 'm_norm_ffn': out['m_norm_ffn'], 'm_w_ffn_gate': out['m_w_ffn_gate'], 'm_w_ffn_up': out['m_w_ffn_up'], 'm_w_ffn_down': out['m_w_ffn_down'], 'm_norm_final': out['m_norm_final'], 'v_norm_mix': out['v_norm_mix'], 'v_w_in': out['v_w_in'], 'v_b_gate': out['v_b_gate'], 'v_conv_w': out['v_conv_w'], 'v_conv_b': out['v_conv_b'], 'v_dt_bias': out['v_dt_bias'], 'v_a_log': out['v_a_log'], 'v_d_skip': out['v_d_skip'], 'v_ssm_norm': out['v_ssm_norm'], 'v_w_ssm_out': out['v_w_ssm_out'], 'v_w_att_out': out['v_w_att_out'], 'v_w_mix_out': out['v_w_mix_out'], 'v_norm_ffn': out['v_norm_ffn'], 'v_w_ffn_gate': out['v_w_ffn_gate'], 'v_w_ffn_up': out['v_w_ffn_up'], 'v_w_ffn_down': out['v_w_ffn_down'], 'v_norm_final': out['v_norm_final']}


def _loss(weights, diff, rest, loss_target):
    with _jax.named_scope("forward"):
        args = {**rest, TWIN_DIFF_INPUT: diff, **{k: w.astype(_WEIGHT_DTYPES[k]) for k, w in weights.items()}}
        y = _forward(args)
    with _jax.named_scope("loss_head"):
        err = _jnp.square(y.astype(_jnp.float32) - loss_target)
        return 0.5 * _jnp.sum(_jnp.mean(err, axis=-1)) if err.ndim else 0.5 * err


def _adamw(w, g, m, v):
    m = ADAM_B1 * m + (1.0 - ADAM_B1) * g
    v = ADAM_B2 * v + (1.0 - ADAM_B2) * _jnp.square(g)
    m_hat = m / (1.0 - ADAM_B1 ** ADAM_STEP)
    v_hat = v / (1.0 - ADAM_B2 ** ADAM_STEP)
    delta = -ADAM_LR * (m_hat / (_jnp.sqrt(v_hat) + ADAM_EPS) + ADAM_WD * w)
    return delta, m, v


def reference(x, norm_mix, w_in, b_gate, conv_w, conv_b, dt_bias, a_log, d_skip, ssm_norm, w_ssm_out, w_att_out, w_mix_out, norm_ffn, w_ffn_gate, w_ffn_up, w_ffn_down, norm_final, loss_target, m_norm_mix, m_w_in, m_b_gate, m_conv_w, m_conv_b, m_dt_bias, m_a_log, m_d_skip, m_ssm_norm, m_w_ssm_out, m_w_att_out, m_w_mix_out, m_norm_ffn, m_w_ffn_gate, m_w_ffn_up, m_w_ffn_down, m_norm_final, v_norm_mix, v_w_in, v_b_gate, v_conv_w, v_conv_b, v_dt_bias, v_a_log, v_d_skip, v_ssm_norm, v_w_ssm_out, v_w_att_out, v_w_mix_out, v_norm_ffn, v_w_ffn_gate, v_w_ffn_up, v_w_ffn_down, v_norm_final):
    given = dict(x=x, norm_mix=norm_mix, w_in=w_in, b_gate=b_gate, conv_w=conv_w, conv_b=conv_b, dt_bias=dt_bias, a_log=a_log, d_skip=d_skip, ssm_norm=ssm_norm, w_ssm_out=w_ssm_out, w_att_out=w_att_out, w_mix_out=w_mix_out, norm_ffn=norm_ffn, w_ffn_gate=w_ffn_gate, w_ffn_up=w_ffn_up, w_ffn_down=w_ffn_down, norm_final=norm_final, loss_target=loss_target, m_norm_mix=m_norm_mix, m_w_in=m_w_in, m_b_gate=m_b_gate, m_conv_w=m_conv_w, m_conv_b=m_conv_b, m_dt_bias=m_dt_bias, m_a_log=m_a_log, m_d_skip=m_d_skip, m_ssm_norm=m_ssm_norm, m_w_ssm_out=m_w_ssm_out, m_w_att_out=m_w_att_out, m_w_mix_out=m_w_mix_out, m_norm_ffn=m_norm_ffn, m_w_ffn_gate=m_w_ffn_gate, m_w_ffn_up=m_w_ffn_up, m_w_ffn_down=m_w_ffn_down, m_norm_final=m_norm_final, v_norm_mix=v_norm_mix, v_w_in=v_w_in, v_b_gate=v_b_gate, v_conv_w=v_conv_w, v_conv_b=v_conv_b, v_dt_bias=v_dt_bias, v_a_log=v_a_log, v_d_skip=v_d_skip, v_ssm_norm=v_ssm_norm, v_w_ssm_out=v_w_ssm_out, v_w_att_out=v_w_att_out, v_w_mix_out=v_w_mix_out, v_norm_ffn=v_norm_ffn, v_w_ffn_gate=v_w_ffn_gate, v_w_ffn_up=v_w_ffn_up, v_w_ffn_down=v_w_ffn_down, v_norm_final=v_norm_final)
    weights = {n: given[n] for n in TWIN_WEIGHTS}
    shared = {n: given[n] for n in SHARED_INPUTS}
    per_example = {n: given[n] for n in ['x']}
    grad_fn = _jax.value_and_grad(_loss, argnums=(0, 1))

    def one_microbatch(ex, loss_target):
        ex = dict(ex)
        diff = ex.pop(TWIN_DIFF_INPUT)
        return grad_fn(weights, diff, {**shared, **ex}, loss_target)

    if N_MICROBATCH == 1:
        loss, (grad_w, grad_x) = one_microbatch(per_example, given["loss_target"])
    else:
        def body(carry, xs):
            loss_sum, grad_sum = carry
            l_k, (gw_k, gx_k) = one_microbatch(xs[0], xs[1])
            with _jax.named_scope("update"):
                return (loss_sum + l_k, _jax.tree.map(_jnp.add, grad_sum, gw_k)), gx_k

        init = (_jnp.zeros((), _jnp.float32), _jax.tree.map(_jnp.zeros_like, weights))
        (loss, grad_w), grad_x = _jax.lax.scan(body, init, (per_example, given["loss_target"]))
    with _jax.named_scope("update"):
        delta_w, new_m, new_v = {}, {}, {}
        for n in TWIN_WEIGHTS:
            delta_w[n], new_m[n], new_v[n] = _adamw(weights[n], grad_w[n], given["m_" + n], given["v_" + n])
    return (loss, grad_x, *[grad_w[n] for n in TWIN_WEIGHTS], *[delta_w[n] for n in TWIN_WEIGHTS],
            *[new_m[n] for n in TWIN_WEIGHTS], *[new_v[n] for n in TWIN_WEIGHTS])
```

```python
import functools
import math

import jax
import jax.numpy as jnp
from jax import lax
from jax.experimental import pallas as pl
from jax.experimental.pallas import tpu as pltpu

F32 = jnp.float32
BF16 = jnp.bfloat16

N_DEV = 8
D_MODEL = 1024
SSM_D_INNER = 2048
SSM_HEAD_DIM = 64
HEAD_DIM_LOG2 = 6
SSM_N_HEADS = 32
SSM_N_GROUPS = 4
SSM_HEADS_PER_GROUP = SSM_N_HEADS // SSM_N_GROUPS
SSM_D_STATE = 128
SSM_CONV = 4
SSM_CHUNK = 128
SSM_CONV_DIM = 3072
ATT_HEAD_DIM = 128
ATT_HEADS_PER_GROUP = 4
ATT_DILATIONS = (1, 4, 16)
ATT_N_HEADS = 12
ATT_QKV_DIM = 4608
ATT_OUT_DIM = 512
ATT_BLOCK = 128
ROPE_THETA = 10000.0
D_FF = 2816
IN_PROJ_DIM = 11808
EPS = 1e-6
LANES = 128
DT_PAD = LANES

ADAM_LR = 0.001
ADAM_B1 = 0.9
ADAM_B2 = 0.999
ADAM_EPS = 1e-08
ADAM_WD = 0.01
ADAM_STEP = 10

VMEM_LIMIT = 56 * 1024 * 1024
MESH = pl.DeviceIdType.MESH
NEG_INF = float("-inf")


def _pick(n, candidates):
    for c in candidates:
        if n % c == 0:
            return c
    return n


def _params(*sem):
    return pltpu.CompilerParams(dimension_semantics=sem, vmem_limit_bytes=VMEM_LIMIT)


def _sigmoid(x):
    return 1.0 / (1.0 + jnp.exp(-x))


def _softplus(x):
    return jnp.maximum(x, 0.0) + jnp.log(1.0 + jnp.exp(-jnp.abs(x)))


def _dot(a, b, dims):
    return lax.dot_general(a.astype(BF16), b.astype(BF16), (dims, ((), ())), preferred_element_type=F32)


def _nn(a, b):
    return _dot(a, b, ((1,), (0,)))


def _nt(a, b):
    return _dot(a, b, ((1,), (1,)))


def _tn(a, b):
    return _dot(a, b, ((0,), (0,)))


def _split3(v):
    hi = v.astype(BF16)
    r1 = v - hi.astype(F32)
    mid = r1.astype(BF16)
    lo = (r1 - mid.astype(F32)).astype(BF16)
    return hi, mid, lo


def _mask_nn(mask, v):
    mb = mask.astype(BF16)
    hi, mid, lo = _split3(v)
    return _nn(mb, hi) + (_nn(mb, mid) + _nn(mb, lo))


def _nn_mask(v, mask):
    mb = mask.astype(BF16)
    hi, mid, lo = _split3(v)
    return _nn(hi, mb) + (_nn(mid, mb) + _nn(lo, mb))


def _mm(a, b, *, mode, name, out_dtype=F32, add=None):
    if mode == "nn":
        (m, k), n = a.shape, b.shape[1]
    elif mode == "nt":
        (m, k), n = a.shape, b.shape[0]
    else:
        (k, m), n = a.shape, b.shape[1]
    tm = _pick(m, (512, 256, 128))
    tn = _pick(n, (512, 384, 256, 128))
    tk = _pick(k, (1024, 512, 384, 256, 128))
    nk = k // tk
    dims = {"nn": ((1,), (0,)), "nt": ((1,), (1,)), "tn": ((0,), (0,))}[mode]
    a_spec = {"nn": pl.BlockSpec((tm, tk), lambda i, j, kk: (i, kk)),
              "nt": pl.BlockSpec((tm, tk), lambda i, j, kk: (i, kk)),
              "tn": pl.BlockSpec((tk, tm), lambda i, j, kk: (kk, i))}[mode]
    b_spec = {"nn": pl.BlockSpec((tk, tn), lambda i, j, kk: (kk, j)),
              "nt": pl.BlockSpec((tn, tk), lambda i, j, kk: (j, kk)),
              "tn": pl.BlockSpec((tk, tn), lambda i, j, kk: (kk, j))}[mode]
    o_spec = pl.BlockSpec((tm, tn), lambda i, j, kk: (i, j))
    has_add = add is not None

    def body(*refs):
        if has_add:
            a_ref, b_ref, c_ref, o_ref, acc = refs
        else:
            a_ref, b_ref, o_ref, acc = refs
        kk = pl.program_id(2)

        @pl.when(kk == 0)
        def _():
            acc[...] = jnp.zeros_like(acc)

        acc[...] += _dot(a_ref[...], b_ref[...], dims)

        @pl.when(kk == nk - 1)
        def _():
            r = acc[...]
            if has_add:
                r = r + c_ref[...]
            o_ref[...] = r.astype(out_dtype)

    in_specs = [a_spec, b_spec] + ([o_spec] if has_add else [])
    args = (a, b) + ((add,) if has_add else ())
    return pl.pallas_call(
        body, name=name, grid=(m // tm, n // tn, nk),
        in_specs=in_specs, out_specs=o_spec,
        out_shape=jax.ShapeDtypeStruct((m, n), out_dtype),
        scratch_shapes=[pltpu.VMEM((tm, tn), F32)],
        compiler_params=_params("parallel", "parallel", "arbitrary"),
    )(*args)


def _rmsnorm_fwd(x, g, name):
    t, d = x.shape
    tm = _pick(t, (512, 256, 128))

    def body(x_ref, g_ref, o_ref):
        xv = x_ref[...]
        r = lax.rsqrt(jnp.mean(xv * xv, axis=-1, keepdims=True) + EPS)
        o_ref[...] = ((xv * r) * g_ref[...]).astype(BF16)

    return pl.pallas_call(
        body, name=name, grid=(t // tm,),
        in_specs=[pl.BlockSpec((tm, d), lambda i: (i, 0)), pl.BlockSpec((1, d), lambda i: (0, 0))],
        out_specs=pl.BlockSpec((tm, d), lambda i: (i, 0)),
        out_shape=jax.ShapeDtypeStruct((t, d), BF16),
        compiler_params=_params("parallel"),
    )(x, g)


def _rmsnorm_bwd(x, g, dh, dres, name):
    t, d = x.shape
    tm = _pick(t, (512, 256, 128))

    def body(x_ref, g_ref, dh_ref, dres_ref, dx_ref, dg_ref):
        @pl.when(pl.program_id(0) == 0)
        def _():
            dg_ref[...] = jnp.zeros_like(dg_ref)

        xv = x_ref[...]
        r = lax.rsqrt(jnp.mean(xv * xv, axis=-1, keepdims=True) + EPS)
        xhat = xv * r
        dhv = dh_ref[...]
        dyg = dhv * g_ref[...]
        dx_ref[...] = dres_ref[...] + r * (dyg - xhat * jnp.mean(dyg * xhat, axis=-1, keepdims=True))
        dg_ref[...] += jnp.sum(dhv * xhat, axis=0, keepdims=True)

    row = pl.BlockSpec((tm, d), lambda i: (i, 0))
    vec = pl.BlockSpec((1, d), lambda i: (0, 0))
    return pl.pallas_call(
        body, name=name, grid=(t // tm,),
        in_specs=[row, vec, row, row], out_specs=[row, vec],
        out_shape=[jax.ShapeDtypeStruct((t, d), F32), jax.ShapeDtypeStruct((1, d), F32)],
        compiler_params=_params("arbitrary"),
    )(x, g, dh, dres)


def _loss_head(x, g, target, name):
    t, d = x.shape
    tm = _pick(t, (512, 256, 128))

    def body(x_ref, g_ref, t_ref, loss_ref, dx_ref, dg_ref):
        @pl.when(pl.program_id(0) == 0)
        def _():
            dg_ref[...] = jnp.zeros_like(dg_ref)
            loss_ref[...] = jnp.zeros_like(loss_ref)

        xv = x_ref[...]
        gv = g_ref[...]
        r = lax.rsqrt(jnp.mean(xv * xv, axis=-1, keepdims=True) + EPS)
        xhat = xv * r
        err = xhat * gv - t_ref[...]
        loss_ref[...] += jnp.sum(err * err) * (0.5 / d)
        dy = err * (1.0 / d)
        dyg = dy * gv
        dx_ref[...] = r * (dyg - xhat * jnp.mean(dyg * xhat, axis=-1, keepdims=True))
        dg_ref[...] += jnp.sum(dy * xhat, axis=0, keepdims=True)

    row = pl.BlockSpec((tm, d), lambda i: (i, 0))
    vec = pl.BlockSpec((1, d), lambda i: (0, 0))
    return pl.pallas_call(
        body, name=name, grid=(t // tm,),
        in_specs=[row, vec, row],
        out_specs=[pl.BlockSpec((1, LANES), lambda i: (0, 0)), row, vec],
        out_shape=[jax.ShapeDtypeStruct((1, LANES), F32), jax.ShapeDtypeStruct((t, d), F32),
                   jax.ShapeDtypeStruct((1, d), F32)],
        compiler_params=_params("arbitrary"),
    )(x, g, target)


CONV_HALO = 8


def _conv_fwd(u, w, bias, name):
    b, s, c = u.shape
    ts = _pick(s, (512, 256, 128))
    cb = _pick(c, (512, 384, 256, 128))
    hb = ts // CONV_HALO

    def body(u_ref, h_ref, w_ref, b_ref, o_ref):
        uv = u_ref[...]
        halo = jnp.where(pl.program_id(1) == 0, 0.0, h_ref[...])
        ext = jnp.concatenate([halo, uv], axis=0)
        wv = w_ref[...]
        acc = b_ref[...] + wv[SSM_CONV - 1:SSM_CONV, :] * uv
        for sh in range(1, SSM_CONV):
            kidx = SSM_CONV - 1 - sh
            acc = acc + wv[kidx:kidx + 1, :] * ext[CONV_HALO - sh:CONV_HALO - sh + ts, :]
        o_ref[...] = acc * _sigmoid(acc)

    return pl.pallas_call(
        body, name=name, grid=(b, s // ts, c // cb),
        in_specs=[pl.BlockSpec((None, ts, cb), lambda bi, i, j: (bi, i, j)),
                  pl.BlockSpec((None, CONV_HALO, cb), lambda bi, i, j: (bi, jnp.maximum(i * hb - 1, 0), j)),
                  pl.BlockSpec((SSM_CONV, cb), lambda bi, i, j: (0, j)),
                  pl.BlockSpec((1, cb), lambda bi, i, j: (0, j))],
        out_specs=pl.BlockSpec((None, ts, cb), lambda bi, i, j: (bi, i, j)),
        out_shape=jax.ShapeDtypeStruct((b, s, c), F32),
        compiler_params=_params("parallel", "parallel", "parallel"),
    )(u, u, w, bias)


def _conv_bwd(u, dout, w, bias, name):
    b, s, c = u.shape
    ts = _pick(s, (512, 256, 128))
    cb = _pick(c, (512, 384, 256, 128))
    hb = ts // CONV_HALO
    n_t = s // ts

    def pre_act_grad(ext_u, cur_u, dout_v, wv, bv, rows):
        acc = bv + wv[SSM_CONV - 1:SSM_CONV, :] * cur_u
        for sh in range(1, SSM_CONV):
            kidx = SSM_CONV - 1 - sh
            acc = acc + wv[kidx:kidx + 1, :] * ext_u[CONV_HALO - sh:CONV_HALO - sh + rows, :]
        sg = _sigmoid(acc)
        return dout_v * (sg * (1.0 + acc * (1.0 - sg)))

    def body(u_ref, up_ref, un_ref, d_ref, dn_ref, w_ref, b_ref, du_ref, dw_ref, db_ref):
        i = pl.program_id(2)
        first = jnp.logical_and(pl.program_id(1) == 0, i == 0)

        @pl.when(first)
        def _():
            dw_ref[...] = jnp.zeros_like(dw_ref)
            db_ref[...] = jnp.zeros_like(db_ref)

        wv = w_ref[...]
        bv = b_ref[...]
        uv = u_ref[...]
        u_prev = jnp.where(i == 0, 0.0, up_ref[...])
        ext_u = jnp.concatenate([u_prev, uv], axis=0)
        dpre = pre_act_grad(ext_u, uv, d_ref[...], wv, bv, ts)
        un = un_ref[...]
        ext_n = jnp.concatenate([uv[ts - CONV_HALO:, :], un], axis=0)
        dpre_n = pre_act_grad(ext_n, un, dn_ref[...], wv, bv, CONV_HALO)
        dpre_n = jnp.where(i == n_t - 1, 0.0, dpre_n)
        ext_d = jnp.concatenate([dpre, dpre_n], axis=0)
        du = wv[SSM_CONV - 1:SSM_CONV, :] * dpre
        dw_ref[SSM_CONV - 1:SSM_CONV, :] += jnp.sum(dpre * uv, axis=0, keepdims=True)
        for sh in range(1, SSM_CONV):
            kidx = SSM_CONV - 1 - sh
            du = du + wv[kidx:kidx + 1, :] * ext_d[sh:sh + ts, :]
            dw_ref[kidx:kidx + 1, :] += jnp.sum(dpre * ext_u[CONV_HALO - sh:CONV_HALO - sh + ts, :],
                                                axis=0, keepdims=True)
        du_ref[...] = du.astype(BF16)
        db_ref[...] += jnp.sum(dpre, axis=0, keepdims=True)

    last_hb = s // CONV_HALO - 1
    tile = pl.BlockSpec((None, ts, cb), lambda j, bi, i: (bi, i, j))
    prev = pl.BlockSpec((None, CONV_HALO, cb), lambda j, bi, i: (bi, jnp.maximum(i * hb - 1, 0), j))
    nxt = pl.BlockSpec((None, CONV_HALO, cb), lambda j, bi, i: (bi, jnp.minimum((i + 1) * hb, last_hb), j))
    return pl.pallas_call(
        body, name=name, grid=(c // cb, b, n_t),
        in_specs=[tile, prev, nxt, tile, nxt,
                  pl.BlockSpec((SSM_CONV, cb), lambda j, bi, i: (0, j)),
                  pl.BlockSpec((1, cb), lambda j, bi, i: (0, j))],
        out_specs=[tile, pl.BlockSpec((SSM_CONV, cb), lambda j, bi, i: (0, j)),
                   pl.BlockSpec((1, cb), lambda j, bi, i: (0, j))],
        out_shape=[jax.ShapeDtypeStruct((b, s, c), BF16), jax.ShapeDtypeStruct((SSM_CONV, c), F32),
                   jax.ShapeDtypeStruct((1, c), F32)],
        compiler_params=_params("parallel", "arbitrary", "arbitrary"),
    )(u, u, u, dout, dout, w, bias)


def _ssd_chunk_terms(dtr_ref, bias_ref, alog_ref):
    q = SSM_CHUNK
    dt = _softplus(dtr_ref[...] + bias_ref[...])
    a_neg = -jnp.exp(alog_ref[...])
    row = lax.broadcasted_iota(jnp.int32, (q, q), 0)
    col = lax.broadcasted_iota(jnp.int32, (q, q), 1)
    lower = row >= col
    s = _mask_nn(lower, dt * a_neg)
    return dt, a_neg, s, s.T, lower


def _ssd_fwd(xc, dtr, dt_bias, a_log, d_skip, name):
    b, s, _ = xc.shape
    q = SSM_CHUNK
    nc = s // q
    p, n = SSM_HEAD_DIM, SSM_D_STATE

    def body(xc_ref, dtr_ref, bias_ref, alog_ref, dsk_ref, y_ref, hs_ref, h_scr):
        @pl.when(pl.program_id(1) == 0)
        def _():
            h_scr[...] = jnp.zeros_like(h_scr)

        dt, _, s_col, s_row, lower = _ssd_chunk_terms(dtr_ref, bias_ref, alog_ref)
        tot = s_col[q - 1:q, :]
        dec = jnp.exp(tot - s_col)
        es = jnp.exp(s_col)
        etot = jnp.exp(tot)
        dsk = dsk_ref[...]
        for g in range(SSM_N_GROUPS):
            bg = xc_ref[:, SSM_D_INNER + n * g:SSM_D_INNER + n * (g + 1)].astype(BF16)
            cg = xc_ref[:, SSM_D_INNER + n * (SSM_N_GROUPS + g):SSM_D_INNER + n * (SSM_N_GROUPS + g + 1)].astype(BF16)
            gm = _nt(cg, bg)
            for j in range(SSM_HEADS_PER_GROUP):
                h = g * SSM_HEADS_PER_GROUP + j
                xh = xc_ref[:, p * h:p * (h + 1)]
                xdt = xh * dt[:, h:h + 1]
                lm = jnp.exp(jnp.where(lower, s_col[:, h:h + 1] - s_row[h:h + 1, :], NEG_INF))
                y_diag = _nn(gm * lm, xdt)
                hh = h_scr[h]
                hs_ref[h] = hh
                y_off = es[:, h:h + 1] * _nt(cg, hh)
                y_ref[:, p * h:p * (h + 1)] = y_diag + y_off + dsk[:, h:h + 1] * xh
                h_scr[h] = etot[:, h:h + 1] * hh + _tn(xdt * dec[:, h:h + 1], bg)

    vec = pl.BlockSpec((1, LANES), lambda bi, c: (0, 0))
    return pl.pallas_call(
        body, name=name, grid=(b, nc),
        in_specs=[pl.BlockSpec((None, q, SSM_CONV_DIM), lambda bi, c: (bi, c, 0)),
                  pl.BlockSpec((None, q, LANES), lambda bi, c: (bi, c, 0)), vec, vec, vec],
        out_specs=[pl.BlockSpec((None, q, SSM_D_INNER), lambda bi, c: (bi, c, 0)),
                   pl.BlockSpec((None, None, SSM_N_HEADS, p, n), lambda bi, c: (bi, c, 0, 0, 0))],
        out_shape=[jax.ShapeDtypeStruct((b, s, SSM_D_INNER), F32),
                   jax.ShapeDtypeStruct((b, nc, SSM_N_HEADS, p, n), F32)],
        scratch_shapes=[pltpu.VMEM((SSM_N_HEADS, p, n), F32)],
        compiler_params=_params("parallel", "arbitrary"),
    )(xc, dtr, dt_bias, a_log, d_skip)


def _ssd_bwd(xc, dtr, dy, hs, dt_bias, a_log, d_skip, name):
    b, s, _ = xc.shape
    q = SSM_CHUNK
    nc = s // q
    p, n = SSM_HEAD_DIM, SSM_D_STATE

    def body(xc_ref, dtr_ref, dy_ref, hs_ref, bias_ref, alog_ref, dsk_ref,
             dxc_ref, ddtr_ref, dalog_ref, ddsk_ref, dbias_ref, dh_scr, p_ds, p_dt, p_dd, p_tot):
        ci = pl.program_id(1)

        @pl.when(ci == 0)
        def _():
            dh_scr[...] = jnp.zeros_like(dh_scr)

        @pl.when(jnp.logical_and(pl.program_id(0) == 0, ci == 0))
        def _():
            dalog_ref[...] = jnp.zeros_like(dalog_ref)
            ddsk_ref[...] = jnp.zeros_like(ddsk_ref)
            dbias_ref[...] = jnp.zeros_like(dbias_ref)

        dt, a_neg, s_col, s_row, lower = _ssd_chunk_terms(dtr_ref, bias_ref, alog_ref)
        upper = jnp.logical_not(lower) | (lax.broadcasted_iota(jnp.int32, (q, q), 0)
                                          == lax.broadcasted_iota(jnp.int32, (q, q), 1))
        tot = s_col[q - 1:q, :]
        dec = jnp.exp(tot - s_col)
        es = jnp.exp(s_col)
        etot = jnp.exp(tot)
        dsk = dsk_ref[...]
        lane = lax.broadcasted_iota(jnp.int32, (1, LANES), 1)
        hdot = jnp.zeros((1, LANES), F32)
        ds_mask = jnp.zeros((q, LANES), F32)
        for g in range(SSM_N_GROUPS):
            b_lo = SSM_D_INNER + n * g
            c_lo = SSM_D_INNER + n * (SSM_N_GROUPS + g)
            bg = xc_ref[:, b_lo:b_lo + n].astype(BF16)
            cg = xc_ref[:, c_lo:c_lo + n].astype(BF16)
            gm = _nt(cg, bg)
            gmt = _nt(bg, cg)
            dg = jnp.zeros((q, q), F32)
            dgt = jnp.zeros((q, q), F32)
            dcg = jnp.zeros((q, n), F32)
            dbg = jnp.zeros((q, n), F32)
            for j in range(SSM_HEADS_PER_GROUP):
                h = g * SSM_HEADS_PER_GROUP + j
                sl = slice(p * h, p * (h + 1))
                xh = xc_ref[:, sl]
                dth = dt[:, h:h + 1]
                xdt = xh * dth
                dyh = dy_ref[:, sl]
                lm = jnp.exp(jnp.where(lower, s_col[:, h:h + 1] - s_row[h:h + 1, :], NEG_INF))
                lmt = jnp.exp(jnp.where(upper, s_row[h:h + 1, :] - s_col[:, h:h + 1], NEG_INF))
                dm = _nt(dyh, xdt)
                dmt = _nt(xdt, dyh)
                dg = dg + dm * lm
                dgt = dgt + dmt * lmt
                mt = gmt * lmt
                ds_h = (jnp.sum(dm * (gm * lm), axis=1, keepdims=True)
                        - jnp.sum(dmt * mt, axis=1, keepdims=True))
                ds_mask = jnp.where(lane == h, ds_h, ds_mask)
                dx_diag = _nn(mt, dyh)
                hh = hs_ref[h]
                dhn = dh_scr[h]
                dw = es[:, h:h + 1] * dyh
                dcg = dcg + _nn(dw, hh)
                dech = dec[:, h:h + 1]
                dx_state = dech * _nt(bg, dhn)
                dbg = dbg + _nn(xdt * dech, dhn)
                dxdt = dx_diag + dx_state
                dskh = dsk[:, h:h + 1]
                dxc_ref[:, sl] = dxdt * dth + dskh * dyh
                p_ds[:, sl] = dw * _nt(cg, hh) - xdt * dx_state
                p_dt[:, sl] = dxdt * xh
                p_dd[:, sl] = dyh * xh
                p_tot[:, sl] = xdt * dx_state
                hdot = jnp.where(lane == h, jnp.sum(dhn * hh), hdot)
                dh_scr[h] = _tn(dw, cg) + etot[:, h:h + 1] * dhn
            dxc_ref[:, b_lo:b_lo + n] = dbg + _nn(dgt, cg)
            dxc_ref[:, c_lo:c_lo + n] = dcg + _nn(dg, bg)
        ind = ((lax.broadcasted_iota(jnp.int32, (SSM_D_INNER, LANES), 0) >> HEAD_DIM_LOG2)
               == lax.broadcasted_iota(jnp.int32, (SSM_D_INNER, LANES), 1))
        r_ds = _nn_mask(p_ds[...], ind)
        r_dt = _nn_mask(p_dt[...], ind)
        r_dd = _nn_mask(p_dd[...], ind)
        r_tot = _nn_mask(p_tot[...], ind)
        dtot = jnp.sum(r_tot, axis=0, keepdims=True) + etot * hdot
        last = lax.broadcasted_iota(jnp.int32, (q, LANES), 0) == q - 1
        ds = ds_mask + r_ds + jnp.where(last, dtot, 0.0)
        da = _mask_nn(upper, ds)
        ddt = da * a_neg + r_dt
        live = lane < SSM_N_HEADS
        sg = _sigmoid(dtr_ref[...] + bias_ref[...])
        ddtr = jnp.where(live, ddt * sg, 0.0)
        ddtr_ref[...] = ddtr.astype(BF16)
        dalog_ref[...] += jnp.where(live, jnp.sum(da * dt, axis=0, keepdims=True) * a_neg, 0.0)
        ddsk_ref[...] += jnp.where(live, jnp.sum(r_dd, axis=0, keepdims=True), 0.0)
        dbias_ref[...] += jnp.sum(ddtr, axis=0, keepdims=True)

    rev = lambda bi, c: (bi, nc - 1 - c, 0)
    vec = pl.BlockSpec((1, LANES), lambda bi, c: (0, 0))
    wide = pl.BlockSpec((None, q, SSM_D_INNER), rev)
    return pl.pallas_call(
        body, name=name, grid=(b, nc),
        in_specs=[pl.BlockSpec((None, q, SSM_CONV_DIM), rev), pl.BlockSpec((None, q, LANES), rev), wide,
                  pl.BlockSpec((None, None, SSM_N_HEADS, p, n), lambda bi, c: (bi, nc - 1 - c, 0, 0, 0)),
                  vec, vec, vec],
        out_specs=[pl.BlockSpec((None, q, SSM_CONV_DIM), rev), pl.BlockSpec((None, q, LANES), rev), vec, vec, vec],
        out_shape=[jax.ShapeDtypeStruct((b, s, SSM_CONV_DIM), F32), jax.ShapeDtypeStruct((b, s, LANES), BF16),
                   jax.ShapeDtypeStruct((1, LANES), F32), jax.ShapeDtypeStruct((1, LANES), F32),
                   jax.ShapeDtypeStruct((1, LANES), F32)],
        scratch_shapes=[pltpu.VMEM((SSM_N_HEADS, p, n), F32)] + [pltpu.VMEM((q, SSM_D_INNER), F32)] * 4,
        compiler_params=_params("arbitrary", "arbitrary"),
    )(xc, dtr, dy, hs, dt_bias, a_log, d_skip)


SSM_GROUP_WIDTH = SSM_D_INNER // SSM_N_GROUPS


def _gate_norm_fwd(y, z, w, name):
    t, d = y.shape
    tm = _pick(t, (256, 128))

    def body(y_ref, z_ref, w_ref, o_ref):
        for g in range(SSM_N_GROUPS):
            sl = slice(SSM_GROUP_WIDTH * g, SSM_GROUP_WIDTH * (g + 1))
            zv = z_ref[:, sl]
            u = y_ref[:, sl] * (zv * _sigmoid(zv))
            r = lax.rsqrt(jnp.mean(u * u, axis=-1, keepdims=True) + EPS)
            o_ref[:, sl] = ((u * r) * w_ref[:, sl]).astype(BF16)

    row = pl.BlockSpec((tm, d), lambda i: (i, 0))
    return pl.pallas_call(
        body, name=name, grid=(t // tm,),
        in_specs=[row, row, pl.BlockSpec((1, d), lambda i: (0, 0))], out_specs=row,
        out_shape=jax.ShapeDtypeStruct((t, d), BF16),
        compiler_params=_params("parallel"),
    )(y, z, w)


def _gate_norm_bwd(y, z, w, dout, name):
    t, d = y.shape
    tm = _pick(t, (256, 128))

    def body(y_ref, z_ref, w_ref, do_ref, dy_ref, dz_ref, dw_ref):
        @pl.when(pl.program_id(0) == 0)
        def _():
            dw_ref[...] = jnp.zeros_like(dw_ref)

        for g in range(SSM_N_GROUPS):
            sl = slice(SSM_GROUP_WIDTH * g, SSM_GROUP_WIDTH * (g + 1))
            zv = z_ref[:, sl]
            yv = y_ref[:, sl]
            sg = _sigmoid(zv)
            silu = zv * sg
            u = yv * silu
            r = lax.rsqrt(jnp.mean(u * u, axis=-1, keepdims=True) + EPS)
            uh = u * r
            dov = do_ref[:, sl]
            dw_ref[:, sl] += jnp.sum(dov * uh, axis=0, keepdims=True)
            dyg = dov * w_ref[:, sl]
            du = r * (dyg - uh * jnp.mean(dyg * uh, axis=-1, keepdims=True))
            dy_ref[:, sl] = du * silu
            dz_ref[:, sl] = (du * yv * (sg * (1.0 + zv * (1.0 - sg)))).astype(BF16)

    row = pl.BlockSpec((tm, d), lambda i: (i, 0))
    vec = pl.BlockSpec((1, d), lambda i: (0, 0))
    return pl.pallas_call(
        body, name=name, grid=(t // tm,),
        in_specs=[row, row, vec, row], out_specs=[row, row, vec],
        out_shape=[jax.ShapeDtypeStruct((t, d), F32), jax.ShapeDtypeStruct((t, d), BF16),
                   jax.ShapeDtypeStruct((1, d), F32)],
        compiler_params=_params("arbitrary"),
    )(y, z, w, dout)


def _rope_tables(s):
    half = ATT_HEAD_DIM // 2
    inv = ROPE_THETA ** (-jnp.arange(half, dtype=F32) / half)
    ang = jnp.arange(s).astype(F32)[:, None] * inv[None, :]
    cos, sin = jnp.cos(ang), jnp.sin(ang)
    return jnp.concatenate([cos, cos], axis=-1), jnp.concatenate([-sin, sin], axis=-1)


def _rope_fwd(qkv, cosf, sinf, name):
    b, s, w = qkv.shape
    ts = _pick(s, (256, 128))
    d = ATT_HEAD_DIM

    def body(x_ref, c_ref, s_ref, o_ref):
        cv, sv = c_ref[...], s_ref[...]
        for hd in range(2 * ATT_N_HEADS):
            tv = x_ref[:, d * hd:d * (hd + 1)]
            o_ref[:, d * hd:d * (hd + 1)] = (tv * cv + pltpu.roll(tv, d // 2, 1) * sv).astype(BF16)
        o_ref[:, 2 * ATT_N_HEADS * d:] = x_ref[:, 2 * ATT_N_HEADS * d:].astype(BF16)

    tab = pl.BlockSpec((ts, d), lambda bi, i: (i, 0))
    row = pl.BlockSpec((None, ts, w), lambda bi, i: (bi, i, 0))
    return pl.pallas_call(
        body, name=name, grid=(b, s // ts), in_specs=[row, tab, tab], out_specs=row,
        out_shape=jax.ShapeDtypeStruct((b, s, w), BF16),
        compiler_params=_params("parallel", "parallel"),
    )(qkv, cosf, sinf)


def _rope_bwd(dq, dk, dv, cosf, sinf, name):
    b, s, gw = dq[0].shape
    ts = _pick(s, (256, 128))
    d = ATT_HEAD_DIM
    n_pat = len(ATT_DILATIONS)

    def body(*refs):
        ins, (c_ref, s_ref, o_ref) = refs[:3 * n_pat], refs[3 * n_pat:]
        cv, sv = c_ref[...], s_ref[...]
        for kind in range(3):
            for gi in range(n_pat):
                src = ins[kind * n_pat + gi]
                for j in range(ATT_HEADS_PER_GROUP):
                    tv = src[:, d * j:d * (j + 1)]
                    if kind < 2:
                        tv = tv * cv + pltpu.roll(tv * sv, d // 2, 1)
                    lo = d * (kind * ATT_N_HEADS + gi * ATT_HEADS_PER_GROUP + j)
                    o_ref[:, lo:lo + d] = tv.astype(BF16)

    tab = pl.BlockSpec((ts, d), lambda bi, i: (i, 0))
    part = pl.BlockSpec((None, ts, gw), lambda bi, i: (bi, i, 0))
    return pl.pallas_call(
        body, name=name, grid=(b, s // ts), in_specs=[part] * (3 * n_pat) + [tab, tab],
        out_specs=pl.BlockSpec((None, ts, ATT_QKV_DIM), lambda bi, i: (bi, i, 0)),
        out_shape=jax.ShapeDtypeStruct((b, s, ATT_QKV_DIM), BF16),
        compiler_params=_params("parallel", "parallel"),
    )(*dq, *dk, *dv, cosf, sinf)


ATT_SCALE = ATT_HEAD_DIM ** -0.5
QKV_BLOCKS = ATT_QKV_DIM // ATT_OUT_DIM


def _att_views(gi, r):
    def qkv_map(kind, shift):
        def index(bi, ri, nb_i):
            return (bi, jnp.maximum(nb_i + shift, 0), ri * QKV_BLOCKS + 3 * kind + gi)
        return index

    def out_map(shift):
        def index(bi, ri, nb_i):
            return (bi, jnp.maximum(nb_i + shift, 0), ri)
        return index
    return qkv_map, out_map


def _band_mask(shape, q_axis, has_prev):
    qi = lax.broadcasted_iota(jnp.int32, shape, q_axis)
    kj = lax.broadcasted_iota(jnp.int32, shape, 1 - q_axis)
    dist = qi + ATT_BLOCK - kj
    return (dist >= 0) & (dist <= ATT_BLOCK) & (has_prev | (kj >= ATT_BLOCK))


def _att_fwd(qkr, gi, r, name):
    b, s, w = qkr.shape
    l = s // r
    nb = l // ATT_BLOCK
    d = ATT_HEAD_DIM
    qkv_map, out_map = _att_views(gi, r)

    def body(q_ref, kp_ref, k_ref, vp_ref, v_ref, o_ref, lse_ref):
        mask = _band_mask((ATT_BLOCK, 2 * ATT_BLOCK), 0, pl.program_id(2) > 0)
        for j in range(ATT_HEADS_PER_GROUP):
            sl = slice(d * j, d * (j + 1))
            kcat = jnp.concatenate([kp_ref[:, sl], k_ref[:, sl]], axis=0)
            vcat = jnp.concatenate([vp_ref[:, sl], v_ref[:, sl]], axis=0)
            sc = jnp.where(mask, _nt(q_ref[:, sl], kcat) * ATT_SCALE, NEG_INF)
            m = jnp.max(sc, axis=-1, keepdims=True)
            pr = jnp.exp(sc - m)
            den = jnp.sum(pr, axis=-1, keepdims=True)
            o_ref[:, sl] = _nn(pr / den, vcat)
            lse_ref[:, sl] = jnp.broadcast_to(m + jnp.log(den), (ATT_BLOCK, d))

    blk = (None, ATT_BLOCK, ATT_OUT_DIM)
    out_spec = pl.BlockSpec(blk, out_map(0))
    o, lse = pl.pallas_call(
        body, name=name, grid=(b, r, nb),
        in_specs=[pl.BlockSpec(blk, qkv_map(0, 0)),
                  pl.BlockSpec(blk, qkv_map(1, -1)), pl.BlockSpec(blk, qkv_map(1, 0)),
                  pl.BlockSpec(blk, qkv_map(2, -1)), pl.BlockSpec(blk, qkv_map(2, 0))],
        out_specs=[out_spec, out_spec],
        out_shape=[jax.ShapeDtypeStruct((b, l, r * ATT_OUT_DIM), F32)] * 2,
        compiler_params=_params("parallel", "parallel", "parallel"),
    )(*([qkr.reshape(b, l, r * w)] * 5))
    return o.reshape(b, s, ATT_OUT_DIM), lse.reshape(b, s, ATT_OUT_DIM)


def _att_merge(os_, lses, name):
    b, s, w = os_[0].shape
    ts = _pick(s, (512, 256, 128))
    n_pat = len(os_)

    def body(*refs):
        o_refs, l_refs, (att_ref, lse_ref) = refs[:n_pat], refs[n_pat:2 * n_pat], refs[2 * n_pat:]
        ls = [r_[...] for r_ in l_refs]
        m = functools.reduce(jnp.maximum, ls)
        es = [jnp.exp(lv - m) for lv in ls]
        tot = functools.reduce(lambda u, v: u + v, es)
        acc = (es[0] / tot) * o_refs[0][...]
        for gi in range(1, n_pat):
            acc = acc + (es[gi] / tot) * o_refs[gi][...]
        att_ref[...] = acc
        lse_ref[...] = m + jnp.log(tot)

    row = pl.BlockSpec((None, ts, w), lambda bi, i: (bi, i, 0))
    return pl.pallas_call(
        body, name=name, grid=(b, s // ts), in_specs=[row] * (2 * n_pat), out_specs=[row, row],
        out_shape=[jax.ShapeDtypeStruct((b, s, w), F32)] * 2,
        compiler_params=_params("parallel", "parallel"),
    )(*os_, *lses)


def _att_delta(att, datt, name):
    b, s, w = att.shape
    ts = _pick(s, (512, 256, 128))
    d = ATT_HEAD_DIM

    def body(a_ref, d_ref, o_ref):
        for j in range(ATT_HEADS_PER_GROUP):
            sl = slice(d * j, d * (j + 1))
            o_ref[:, sl] = jnp.broadcast_to(jnp.sum(a_ref[:, sl] * d_ref[:, sl], axis=-1, keepdims=True), (ts, d))

    row = pl.BlockSpec((None, ts, w), lambda bi, i: (bi, i, 0))
    return pl.pallas_call(
        body, name=name, grid=(b, s // ts), in_specs=[row, row], out_specs=row,
        out_shape=jax.ShapeDtypeStruct((b, s, w), F32),
        compiler_params=_params("parallel", "parallel"),
    )(att, datt)


def _att_bwd_q(qkr, datt, lse, delta, gi, r, name):
    b, s, w = qkr.shape
    l = s // r
    nb = l // ATT_BLOCK
    d = ATT_HEAD_DIM
    qkv_map, out_map = _att_views(gi, r)

    def body(q_ref, kp_ref, k_ref, vp_ref, v_ref, do_ref, lse_ref, dl_ref, dq_ref):
        mask = _band_mask((ATT_BLOCK, 2 * ATT_BLOCK), 0, pl.program_id(2) > 0)
        for j in range(ATT_HEADS_PER_GROUP):
            sl = slice(d * j, d * (j + 1))
            kcat = jnp.concatenate([kp_ref[:, sl], k_ref[:, sl]], axis=0)
            vcat = jnp.concatenate([vp_ref[:, sl], v_ref[:, sl]], axis=0)
            sc = _nt(q_ref[:, sl], kcat) * ATT_SCALE
            pr = jnp.exp(jnp.where(mask, sc - lse_ref[:, d * j:d * j + 1], NEG_INF))
            dp = _nt(do_ref[:, sl], vcat)
            dsc = pr * (dp - dl_ref[:, d * j:d * j + 1])
            dq_ref[:, sl] = _nn(dsc, kcat) * ATT_SCALE

    blk = (None, ATT_BLOCK, ATT_OUT_DIM)
    tok = pl.BlockSpec(blk, out_map(0))
    qv = qkr.reshape(b, l, r * w)
    view = lambda a: a.reshape(b, l, r * ATT_OUT_DIM)
    dq = pl.pallas_call(
        body, name=name, grid=(b, r, nb),
        in_specs=[pl.BlockSpec(blk, qkv_map(0, 0)),
                  pl.BlockSpec(blk, qkv_map(1, -1)), pl.BlockSpec(blk, qkv_map(1, 0)),
                  pl.BlockSpec(blk, qkv_map(2, -1)), pl.BlockSpec(blk, qkv_map(2, 0)), tok, tok, tok],
        out_specs=tok,
        out_shape=jax.ShapeDtypeStruct((b, l, r * ATT_OUT_DIM), F32),
        compiler_params=_params("parallel", "parallel", "parallel"),
    )(qv, qv, qv, qv, qv, view(datt), view(lse), view(delta))
    return dq.reshape(b, s, ATT_OUT_DIM)


def _att_bwd_kv(qkr, datt, lse, delta, gi, r, name):
    b, s, w = qkr.shape
    l = s // r
    nb = l // ATT_BLOCK
    d = ATT_HEAD_DIM

    def qkv_map(kind, shift):
        def index(bi, ri, nb_i):
            return (bi, jnp.minimum(nb_i + shift, nb - 1), ri * QKV_BLOCKS + 3 * kind + gi)
        return index

    def tok_map(shift):
        def index(bi, ri, nb_i):
            return (bi, jnp.minimum(nb_i + shift, nb - 1), ri)
        return index

    def body(k_ref, v_ref, q_ref, qn_ref, do_ref, don_ref, lse_ref, lsen_ref, dl_ref, dln_ref, dk_ref, dv_ref):
        shape = (ATT_BLOCK, 2 * ATT_BLOCK)
        kj = lax.broadcasted_iota(jnp.int32, shape, 0)
        qi = lax.broadcasted_iota(jnp.int32, shape, 1)
        dist = qi - kj
        has_next = pl.program_id(2) < nb - 1
        mask = (dist >= 0) & (dist <= ATT_BLOCK) & (has_next | (qi < ATT_BLOCK))
        for j in range(ATT_HEADS_PER_GROUP):
            sl = slice(d * j, d * (j + 1))
            qcat = jnp.concatenate([q_ref[:, sl], qn_ref[:, sl]], axis=0)
            docat = jnp.concatenate([do_ref[:, sl], don_ref[:, sl]], axis=0)
            lse_t = jnp.concatenate([lse_ref[:, sl], lsen_ref[:, sl]], axis=0).T
            dl_t = jnp.concatenate([dl_ref[:, sl], dln_ref[:, sl]], axis=0).T
            sc_t = _nt(k_ref[:, sl], qcat) * ATT_SCALE
            pr_t = jnp.exp(jnp.where(mask, sc_t - lse_t, NEG_INF))
            dv_ref[:, sl] = _nn(pr_t, docat)
            dsc_t = pr_t * (_nt(v_ref[:, sl], docat) - dl_t)
            dk_ref[:, sl] = _nn(dsc_t, qcat) * ATT_SCALE

    blk = (None, ATT_BLOCK, ATT_OUT_DIM)
    tok, tok_n = pl.BlockSpec(blk, tok_map(0)), pl.BlockSpec(blk, tok_map(1))
    qv = qkr.reshape(b, l, r * w)
    view = lambda a: a.reshape(b, l, r * ATT_OUT_DIM)
    dk, dv = pl.pallas_call(
        body, name=name, grid=(b, r, nb),
        in_specs=[pl.BlockSpec(blk, qkv_map(1, 0)), pl.BlockSpec(blk, qkv_map(2, 0)),
                  pl.BlockSpec(blk, qkv_map(0, 0)), pl.BlockSpec(blk, qkv_map(0, 1)),
                  tok, tok_n, tok, tok_n, tok, tok_n],
        out_specs=[tok, tok],
        out_shape=[jax.ShapeDtypeStruct((b, l, r * ATT_OUT_DIM), F32)] * 2,
        compiler_params=_params("parallel", "parallel", "parallel"),
    )(qv, qv, qv, qv, view(datt), view(datt), view(lse), view(lse), view(delta), view(delta))
    return dk.reshape(b, s, ATT_OUT_DIM), dv.reshape(b, s, ATT_OUT_DIM)


def _mix_fwd(gl, bg, ys, ya, name):
    t, d = ys.shape
    tm = _pick(t, (512, 256, 128))

    def body(gl_ref, bg_ref, ys_ref, ya_ref, o_ref):
        g0 = _sigmoid(gl_ref[:, :d] + bg_ref[:, :d])
        g1 = _sigmoid(gl_ref[:, d:] + bg_ref[:, d:])
        o_ref[...] = (g0 * ys_ref[...] + g1 * ya_ref[...]).astype(BF16)

    row = pl.BlockSpec((tm, d), lambda i: (i, 0))
    return pl.pallas_call(
        body, name=name, grid=(t // tm,),
        in_specs=[pl.BlockSpec((tm, 2 * d), lambda i: (i, 0)), pl.BlockSpec((1, 2 * d), lambda i: (0, 0)), row, row],
        out_specs=row, out_shape=jax.ShapeDtypeStruct((t, d), BF16),
        compiler_params=_params("parallel"),
    )(gl, bg, ys, ya)


def _mix_bwd(gl, bg, ys, ya, dmixed, name):
    t, d = ys.shape
    tm = _pick(t, (512, 256, 128))

    def body(gl_ref, bg_ref, ys_ref, ya_ref, dm_ref, dys_ref, dya_ref, dgl_ref, dbg_ref):
        @pl.when(pl.program_id(0) == 0)
        def _():
            dbg_ref[...] = jnp.zeros_like(dbg_ref)

        dm = dm_ref[...]
        g0 = _sigmoid(gl_ref[:, :d] + bg_ref[:, :d])
        g1 = _sigmoid(gl_ref[:, d:] + bg_ref[:, d:])
        dys_ref[...] = (dm * g0).astype(BF16)
        dya_ref[...] = (dm * g1).astype(BF16)
        d0 = dm * ys_ref[...] * (g0 * (1.0 - g0))
        d1 = dm * ya_ref[...] * (g1 * (1.0 - g1))
        dgl_ref[:, :d] = d0.astype(BF16)
        dgl_ref[:, d:] = d1.astype(BF16)
        dbg_ref[:, :d] += jnp.sum(d0, axis=0, keepdims=True)
        dbg_ref[:, d:] += jnp.sum(d1, axis=0, keepdims=True)

    row = pl.BlockSpec((tm, d), lambda i: (i, 0))
    wide = pl.BlockSpec((tm, 2 * d), lambda i: (i, 0))
    vec = pl.BlockSpec((1, 2 * d), lambda i: (0, 0))
    return pl.pallas_call(
        body, name=name, grid=(t // tm,),
        in_specs=[wide, vec, row, row, row], out_specs=[row, row, wide, vec],
        out_shape=[jax.ShapeDtypeStruct((t, d), BF16), jax.ShapeDtypeStruct((t, d), BF16),
                   jax.ShapeDtypeStruct((t, 2 * d), BF16), jax.ShapeDtypeStruct((1, 2 * d), F32)],
        compiler_params=_params("arbitrary"),
    )(gl, bg, ys, ya, dmixed)


def _swiglu_fwd(gt, up, name):
    t, f = gt.shape
    tm = _pick(t, (512, 256, 128))

    def body(g_ref, u_ref, o_ref):
        gv = g_ref[...]
        o_ref[...] = ((gv * _sigmoid(gv)) * u_ref[...]).astype(BF16)

    row = pl.BlockSpec((tm, f), lambda i: (i, 0))
    return pl.pallas_call(
        body, name=name, grid=(t // tm,), in_specs=[row, row], out_specs=row,
        out_shape=jax.ShapeDtypeStruct((t, f), BF16), compiler_params=_params("parallel"),
    )(gt, up)


def _swiglu_bwd(gt, up, dact, name):
    t, f = gt.shape
    tm = _pick(t, (512, 256, 128))

    def body(g_ref, u_ref, d_ref, dg_ref, du_ref):
        gv = g_ref[...]
        dv = d_ref[...]
        sg = _sigmoid(gv)
        dg_ref[...] = (dv * u_ref[...] * (sg * (1.0 + gv * (1.0 - sg)))).astype(BF16)
        du_ref[...] = (dv * (gv * sg)).astype(BF16)

    row = pl.BlockSpec((tm, f), lambda i: (i, 0))
    return pl.pallas_call(
        body, name=name, grid=(t // tm,), in_specs=[row, row, row], out_specs=[row, row],
        out_shape=[jax.ShapeDtypeStruct((t, f), BF16)] * 2, compiler_params=_params("parallel"),
    )(gt, up, dact)


def _peer(k):
    x, y, c = lax.axis_index("x"), lax.axis_index("y"), lax.axis_index("c")
    px, py, pc = x ^ ((k >> 2) & 1), y ^ ((k >> 1) & 1), c ^ (k & 1)
    return (px, py, pc), 4 * px + 2 * py + pc


def _my_index():
    return 4 * lax.axis_index("x") + 2 * lax.axis_index("y") + lax.axis_index("c")


def _all_gather(parts, name):
    n_parts = len(parts)

    def body(*refs):
        ins, outs = refs[:n_parts], refs[n_parts:2 * n_parts]
        send_sems, recv_sems, local_sems = refs[2 * n_parts:]
        me = _my_index()
        local = [pltpu.make_async_copy(ins[i], outs[i].at[me], local_sems.at[i]) for i in range(n_parts)]
        for cp in local:
            cp.start()
        sends = []
        for k in range(1, N_DEV):
            peer, _ = _peer(k)
            for i in range(n_parts):
                cp = pltpu.make_async_remote_copy(
                    src_ref=ins[i], dst_ref=outs[i].at[me],
                    send_sem=send_sems.at[i * (N_DEV - 1) + k - 1], recv_sem=recv_sems.at[i * (N_DEV - 1) + k - 1],
                    device_id=peer, device_id_type=MESH)
                cp.start()
                sends.append(cp)
        for k in range(1, N_DEV):
            peer, pidx = _peer(k)
            for i in range(n_parts):
                pltpu.make_async_remote_copy(
                    src_ref=ins[i], dst_ref=outs[i].at[pidx],
                    send_sem=send_sems.at[i * (N_DEV - 1) + k - 1], recv_sem=recv_sems.at[i * (N_DEV - 1) + k - 1],
                    device_id=peer, device_id_type=MESH).wait_recv()
        for cp in sends:
            cp.wait_send()
        for cp in local:
            cp.wait()

    hbm = pl.BlockSpec(memory_space=pl.ANY)
    return pl.pallas_call(
        body, name=name, in_specs=[hbm] * n_parts, out_specs=[hbm] * n_parts,
        out_shape=[jax.ShapeDtypeStruct((N_DEV,) + p_.shape, p_.dtype) for p_ in parts],
        scratch_shapes=[pltpu.SemaphoreType.DMA((n_parts * (N_DEV - 1),)),
                        pltpu.SemaphoreType.DMA((n_parts * (N_DEV - 1),)),
                        pltpu.SemaphoreType.DMA((n_parts,))],
        compiler_params=pltpu.CompilerParams(has_side_effects=True),
    )(*parts)


def _exchange(slabs, shared, name):
    def body(slab_ref, sh_ref, got_ref, gsh_ref, send_sems, recv_sems, local_sems):
        me = _my_index()
        local = [pltpu.make_async_copy(slab_ref.at[me], got_ref.at[me], local_sems.at[0]),
                 pltpu.make_async_copy(sh_ref, gsh_ref.at[me], local_sems.at[1])]
        for cp in local:
            cp.start()
        sends = []
        for k in range(1, N_DEV):
            peer, pidx = _peer(k)
            for i, (src, dst) in enumerate(((slab_ref.at[pidx], got_ref.at[me]), (sh_ref, gsh_ref.at[me]))):
                cp = pltpu.make_async_remote_copy(
                    src_ref=src, dst_ref=dst, send_sem=send_sems.at[i * (N_DEV - 1) + k - 1], recv_sem=recv_sems.at[i * (N_DEV - 1) + k - 1],
                    device_id=peer, device_id_type=MESH)
                cp.start()
                sends.append(cp)
        for k in range(1, N_DEV):
            peer, pidx = _peer(k)
            for i, (src, dst) in enumerate(((slab_ref.at[me], got_ref.at[pidx]), (sh_ref, gsh_ref.at[pidx]))):
                pltpu.make_async_remote_copy(
                    src_ref=src, dst_ref=dst, send_sem=send_sems.at[i * (N_DEV - 1) + k - 1], recv_sem=recv_sems.at[i * (N_DEV - 1) + k - 1],
                    device_id=peer, device_id_type=MESH).wait_recv()
        for cp in sends:
            cp.wait_send()
        for cp in local:
            cp.wait()

    hbm = pl.BlockSpec(memory_space=pl.ANY)
    return pl.pallas_call(
        body, name=name, in_specs=[hbm, hbm], out_specs=[hbm, hbm],
        out_shape=[jax.ShapeDtypeStruct(slabs.shape, slabs.dtype),
                   jax.ShapeDtypeStruct((N_DEV,) + shared.shape, shared.dtype)],
        scratch_shapes=[pltpu.SemaphoreType.DMA((2 * (N_DEV - 1),)), pltpu.SemaphoreType.DMA((2 * (N_DEV - 1),)),
                        pltpu.SemaphoreType.DMA((2,))],
        compiler_params=pltpu.CompilerParams(has_side_effects=True),
    )(slabs, shared)


def _adamw(parts, w, m, v, name):
    _, rows, lanes = parts.shape
    tr = _pick(rows, (512, 256, 128, 64, 32, 16, 8))
    c1 = 1.0 - ADAM_B1 ** ADAM_STEP
    c2 = 1.0 - ADAM_B2 ** ADAM_STEP

    def body(p_ref, w_ref, m_ref, v_ref, g_ref, d_ref, nm_ref, nv_ref):
        g = p_ref[0]
        for j in range(1, N_DEV):
            g = g + p_ref[j]
        nm = ADAM_B1 * m_ref[...] + (1.0 - ADAM_B1) * g
        nv = ADAM_B2 * v_ref[...] + (1.0 - ADAM_B2) * (g * g)
        g_ref[...] = g
        nm_ref[...] = nm
        nv_ref[...] = nv
        d_ref[...] = -ADAM_LR * ((nm / c1) / (jnp.sqrt(nv / c2) + ADAM_EPS) + ADAM_WD * w_ref[...])

    row = pl.BlockSpec((tr, lanes), lambda i: (i, 0))
    return pl.pallas_call(
        body, name=name, grid=(rows // tr,),
        in_specs=[pl.BlockSpec((N_DEV, tr, lanes), lambda i: (0, i, 0)), row, row, row],
        out_specs=[row] * 4, out_shape=[jax.ShapeDtypeStruct((rows, lanes), F32)] * 4,
        compiler_params=_params("parallel"),
    )(parts, w, m, v)


MATRIX_SHARDS = (
    ("w_in", (D_MODEL, IN_PROJ_DIM // N_DEV), True),
    ("w_ssm_out", (SSM_D_INNER // N_DEV, D_MODEL), False),
    ("w_att_out", (ATT_OUT_DIM, D_MODEL // N_DEV), True),
    ("w_mix_out", (D_MODEL // N_DEV, D_MODEL), False),
    ("w_ffn_gate", (D_MODEL, D_FF // N_DEV), True),
    ("w_ffn_up", (D_MODEL, D_FF // N_DEV), True),
    ("w_ffn_down", (D_FF // N_DEV, D_MODEL), False),
)
CONV_SHARD = ("conv_w", (SSM_CONV, SSM_CONV_DIM // N_DEV), True)
SHARDED = MATRIX_SHARDS + (CONV_SHARD,)
REPLICATED = (("norm_mix", D_MODEL), ("b_gate", 2 * D_MODEL), ("conv_b", SSM_CONV_DIM), ("dt_bias", SSM_N_HEADS),
              ("a_log", SSM_N_HEADS), ("d_skip", SSM_N_HEADS), ("ssm_norm", SSM_D_INNER), ("norm_ffn", D_MODEL),
              ("norm_final", D_MODEL))


PACK_ROWS = 512


def _round_up(n, mult):
    return -(-n // mult) * mult


def _pack_rows(flat, row_mult):
    rows = _round_up(-(-flat.shape[0] // LANES), row_mult)
    return jnp.pad(flat, (0, rows * LANES - flat.shape[0])).reshape(rows, LANES)


def _pack_sharded(vals, specs, row_mult, dtype):
    return _pack_rows(jnp.concatenate([vals[name].reshape(-1).astype(dtype) for name, _, _ in specs]), row_mult)


def _unpack_sharded(packed, specs, lead=()):
    flat = packed.reshape(lead + (-1,))
    out, off = {}, 0
    for name, shape, _ in specs:
        size = shape[0] * shape[1]
        out[name] = flat[..., off:off + size].reshape(lead + shape)
        off += size
    return out


def _full_from_shards(stacked, by_cols):
    n, r, c = stacked.shape
    if by_cols:
        return stacked.transpose(1, 0, 2).reshape(r, n * c)
    return stacked.reshape(n * r, c)


def _shards_from_full(full, shape, by_cols):
    r, c = shape
    if by_cols:
        return full.reshape(r, N_DEV, c).transpose(1, 0, 2)
    return full.reshape(N_DEV, r, c)


def _pack_replicated(vals):
    rows = []
    for name, size in REPLICATED:
        v = vals[name].reshape(-1).astype(F32)
        rows.append(jnp.pad(v, (0, _round_up(size, LANES) - size)))
    return _pack_rows(jnp.concatenate(rows), 8)


def _unpack_replicated(packed, shapes):
    flat = packed.reshape(-1)
    out, off = {}, 0
    for name, size in REPLICATED:
        out[name] = flat[off:off + size].reshape(shapes[name])
        off += _round_up(size, LANES)
    return out


def _lane_row(v):
    v = v.reshape(-1).astype(F32)
    return jnp.pad(v, (0, LANES - v.shape[0])).reshape(1, LANES)


IN_SPLIT = (("z", SSM_D_INNER), ("xbc", SSM_CONV_DIM), ("dt", SSM_N_HEADS), ("qkv", ATT_QKV_DIM), ("gate", 2 * D_MODEL))


def _split_w_in(w_full):
    out, off = {}, 0
    for name, size in IN_SPLIT:
        out[name] = w_full[:, off:off + size]
        off += size
    out["dt"] = jnp.pad(out["dt"], ((0, 0), (0, DT_PAD - SSM_N_HEADS)))
    return out


def _join_w_in(parts):
    parts = dict(parts)
    parts["dt"] = parts["dt"][:, :SSM_N_HEADS]
    return jnp.concatenate([parts[name] for name, _ in IN_SPLIT], axis=1)


def kernel(x, norm_mix, w_in, b_gate, conv_w, conv_b, dt_bias, a_log, d_skip, ssm_norm, w_ssm_out, w_att_out, w_mix_out, norm_ffn, w_ffn_gate, w_ffn_up, w_ffn_down, norm_final, loss_target, m_norm_mix, m_w_in, m_b_gate, m_conv_w, m_conv_b, m_dt_bias, m_a_log, m_d_skip, m_ssm_norm, m_w_ssm_out, m_w_att_out, m_w_mix_out, m_norm_ffn, m_w_ffn_gate, m_w_ffn_up, m_w_ffn_down, m_norm_final, v_norm_mix, v_w_in, v_b_gate, v_conv_w, v_conv_b, v_dt_bias, v_a_log, v_d_skip, v_ssm_norm, v_w_ssm_out, v_w_att_out, v_w_mix_out, v_norm_ffn, v_w_ffn_gate, v_w_ffn_up, v_w_ffn_down, v_norm_final):
    given = dict(locals())
    weights = {name: given[name][0] for name, _, _ in SHARDED}
    b, s, d = x.shape
    t = b * s

    mat_local = _pack_sharded(weights, MATRIX_SHARDS, 16, BF16)
    conv_local = _pack_sharded(weights, (CONV_SHARD,), 8, F32)
    mat_all, conv_all = _all_gather([mat_local, conv_local], "weights_all_gather")
    shards = _unpack_sharded(mat_all, MATRIX_SHARDS, (N_DEV,))
    shards.update(_unpack_sharded(conv_all, (CONV_SHARD,), (N_DEV,)))
    full = {name: _full_from_shards(shards[name], by_cols) for name, _, by_cols in SHARDED}
    w_sec = _split_w_in(full["w_in"])

    g_mix, g_ffn, g_fin = norm_mix.reshape(1, d), norm_ffn.reshape(1, d), norm_final.reshape(1, d)
    bg_row = b_gate.reshape(1, 2 * d)
    convb_row = conv_b.reshape(1, SSM_CONV_DIM)
    ssmn_row = ssm_norm.reshape(1, SSM_D_INNER)
    dtb_row, alog_row, dsk_row = _lane_row(dt_bias), _lane_row(a_log), _lane_row(d_skip)
    cosf, sinf = _rope_tables(s)

    x2d = x.reshape(t, d)
    h1 = _rmsnorm_fwd(x2d, g_mix, "norm_mix_fwd")
    proj = {name: _mm(h1, w_sec[name], mode="nn", name="in_proj_" + name) for name, _ in IN_SPLIT}
    xbc3 = proj["xbc"].reshape(b, s, SSM_CONV_DIM)
    xc = _conv_fwd(xbc3, full["conv_w"], convb_row, "conv_fwd")
    dtr3 = proj["dt"].reshape(b, s, DT_PAD)
    y_ssd, h_states = _ssd_fwd(xc, dtr3, dtb_row, alog_row, dsk_row, "ssd_fwd")
    y_ssd2 = y_ssd.reshape(t, SSM_D_INNER)
    ynorm = _gate_norm_fwd(y_ssd2, proj["z"], ssmn_row, "ssd_gate_norm_fwd")
    y_ssm = _mm(ynorm, full["w_ssm_out"], mode="nn", name="ssm_out_proj")

    qkv3 = proj["qkv"].reshape(b, s, ATT_QKV_DIM)
    qkr = _rope_fwd(qkv3, cosf, sinf, "rope_fwd")
    att_parts = [_att_fwd(qkr, gi, r, "att_fwd_%d" % r) for gi, r in enumerate(ATT_DILATIONS)]
    att, lse = _att_merge([o for o, _ in att_parts], [l_ for _, l_ in att_parts], "att_merge")
    att2 = att.reshape(t, ATT_OUT_DIM)
    y_att = _mm(att2, full["w_att_out"], mode="nn", name="att_out_proj")

    mixed = _mix_fwd(proj["gate"], bg_row, y_ssm, y_att, "mix_fwd")
    x2 = _mm(mixed, full["w_mix_out"], mode="nn", name="mix_out_proj", add=x2d)
    h2 = _rmsnorm_fwd(x2, g_ffn, "norm_ffn_fwd")
    gt = _mm(h2, full["w_ffn_gate"], mode="nn", name="ffn_gate_proj")
    up = _mm(h2, full["w_ffn_up"], mode="nn", name="ffn_up_proj")
    act = _swiglu_fwd(gt, up, "swiglu_fwd")
    x3 = _mm(act, full["w_ffn_down"], mode="nn", name="ffn_down_proj", add=x2)

    loss_row, dx3, dg_fin = _loss_head(x3, g_fin, loss_target.reshape(t, d), "loss_head")
    grads = {}
    dact = _mm(dx3, full["w_ffn_down"], mode="nt", name="ffn_down_dx")
    grads["w_ffn_down"] = _mm(act, dx3, mode="tn", name="ffn_down_dw")
    dgt, dup = _swiglu_bwd(gt, up, dact, "swiglu_bwd")
    grads["w_ffn_gate"] = _mm(h2, dgt, mode="tn", name="ffn_gate_dw")
    grads["w_ffn_up"] = _mm(h2, dup, mode="tn", name="ffn_up_dw")
    dh2 = _mm(dgt, full["w_ffn_gate"], mode="nt", name="ffn_gate_dx")
    dh2 = _mm(dup, full["w_ffn_up"], mode="nt", name="ffn_up_dx", add=dh2)
    dx2, dg_ffn = _rmsnorm_bwd(x2, g_ffn, dh2, dx3, "norm_ffn_bwd")

    dmixed = _mm(dx2, full["w_mix_out"], mode="nt", name="mix_out_dx")
    grads["w_mix_out"] = _mm(mixed, dx2, mode="tn", name="mix_out_dw")
    dys, dya, dgl, dbg = _mix_bwd(proj["gate"], bg_row, y_ssm, y_att, dmixed, "mix_bwd")

    grads["w_ssm_out"] = _mm(ynorm, dys, mode="tn", name="ssm_out_dw")
    dynorm = _mm(dys, full["w_ssm_out"], mode="nt", name="ssm_out_dx")
    dy_ssd, dz, dssmn = _gate_norm_bwd(y_ssd2, proj["z"], ssmn_row, dynorm, "ssd_gate_norm_bwd")
    dxc, ddtr, dalog, ddsk, ddtb = _ssd_bwd(xc, dtr3, dy_ssd.reshape(b, s, SSM_D_INNER), h_states,
                                            dtb_row, alog_row, dsk_row, "ssd_bwd")
    dxbc, dconvw, dconvb = _conv_bwd(xbc3, dxc, full["conv_w"], convb_row, "conv_bwd")
    grads["conv_w"] = dconvw

    grads["w_att_out"] = _mm(att2, dya, mode="tn", name="att_out_dw")
    datt = _mm(dya, full["w_att_out"], mode="nt", name="att_out_dx").reshape(b, s, ATT_OUT_DIM)
    delta = _att_delta(att, datt, "att_delta")
    dqs, dks, dvs = [], [], []
    for gi, r in enumerate(ATT_DILATIONS):
        dqs.append(_att_bwd_q(qkr, datt, lse, delta, gi, r, "att_bwd_q_%d" % r))
        dk_g, dv_g = _att_bwd_kv(qkr, datt, lse, delta, gi, r, "att_bwd_kv_%d" % r)
        dks.append(dk_g)
        dvs.append(dv_g)
    dqkv = _rope_bwd(dqs, dks, dvs, cosf, sinf, "rope_bwd")

    dproj = {"z": dz, "xbc": dxbc.reshape(t, SSM_CONV_DIM), "dt": ddtr.reshape(t, DT_PAD),
             "qkv": dqkv.reshape(t, ATT_QKV_DIM), "gate": dgl}
    grads["w_in"] = _join_w_in({name: _mm(h1, dproj[name], mode="tn", name="in_proj_dw_" + name)
                                for name, _ in IN_SPLIT})
    k_all = sum(dproj[name].shape[1] for name, _ in IN_SPLIT)
    k_pad = _round_up(k_all, 1024) - k_all
    dproj_all = jnp.concatenate([dproj[name] for name, _ in IN_SPLIT] + [jnp.zeros((t, k_pad), BF16)], axis=1)
    w_in_all = jnp.concatenate([w_sec[name] for name, _ in IN_SPLIT] + [jnp.zeros((d, k_pad), BF16)], axis=1)
    dh1 = _mm(dproj_all, w_in_all, mode="nt", name="in_proj_dx")
    grad_x, dg_mix = _rmsnorm_bwd(x2d, g_mix, dh1, dx2, "norm_mix_bwd")

    slabs = jnp.concatenate([_shards_from_full(grads[name], shape, by_cols).reshape(N_DEV, -1)
                             for name, shape, by_cols in SHARDED], axis=1)
    slab_rows = _round_up(-(-slabs.shape[1] // LANES), PACK_ROWS)
    slabs = jnp.pad(slabs, ((0, 0), (0, slab_rows * LANES - slabs.shape[1]))).reshape(N_DEV, slab_rows, LANES)
    small = {"norm_mix": dg_mix, "b_gate": dbg, "conv_b": dconvb, "dt_bias": ddtb[:, :SSM_N_HEADS],
             "a_log": dalog[:, :SSM_N_HEADS], "d_skip": ddsk[:, :SSM_N_HEADS], "ssm_norm": dssmn,
             "norm_ffn": dg_ffn, "norm_final": dg_fin}
    got, got_small = _exchange(slabs, _pack_replicated(small), "grad_exchange")

    def packed(prefix):
        vals = {name: given[prefix + name][0] for name, _, _ in SHARDED}
        rep = {name: given[prefix + name] for name, _ in REPLICATED}
        return _pack_sharded(vals, SHARDED, PACK_ROWS, F32), _pack_replicated(rep)

    (w_big, w_small), (m_big, m_small), (v_big, v_small) = packed(""), packed("m_"), packed("v_")
    big = _adamw(got, w_big, m_big, v_big, "adamw_sharded")
    sml = _adamw(got_small, w_small, m_small, v_small, "adamw_replicated")

    loss = lax.psum(loss_row[0, 0], ("x", "y", "c"))
    outs = [loss, grad_x.reshape(b, s, d)]
    rep_shapes = {name: given[name].shape for name, _ in REPLICATED}
    order = ["norm_mix", "w_in", "b_gate", "conv_w", "conv_b", "dt_bias", "a_log", "d_skip", "ssm_norm", "w_ssm_out",
             "w_att_out", "w_mix_out", "norm_ffn", "w_ffn_gate", "w_ffn_up", "w_ffn_down", "norm_final"]
    for big_k, sml_k in zip(big, sml):
        sharded = _unpack_sharded(big_k, SHARDED)
        rep = _unpack_replicated(sml_k, rep_shapes)
        for name in order:
            outs.append(sharded[name][None] if name in sharded else rep[name])
    return tuple(outs)
```

```python
import functools
import math

import jax
import jax.numpy as jnp
from jax import lax
from jax.experimental import pallas as pl
from jax.experimental.pallas import tpu as pltpu

F32 = jnp.float32
BF16 = jnp.bfloat16

N_DEV = 8
N_CHIPS = 4
D_MODEL = 1024
SSM_D_INNER = 2048
SSM_HEAD_DIM = 64
HEAD_DIM_LOG2 = 6
SSM_N_HEADS = 32
SSM_N_GROUPS = 4
SSM_HEADS_PER_GROUP = SSM_N_HEADS // SSM_N_GROUPS
SSM_D_STATE = 128
SSM_CONV = 4
SSM_CHUNK = 128
SSM_CONV_DIM = 3072
ATT_HEAD_DIM = 128
ATT_HEADS_PER_GROUP = 4
ATT_DILATIONS = (1, 4, 16)
ATT_N_HEADS = 12
ATT_QKV_DIM = 4608
ATT_OUT_DIM = 512
ATT_BLOCK = 128
ROPE_THETA = 10000.0
D_FF = 2816
IN_PROJ_DIM = 11808
EPS = 1e-6
LANES = 128
DT_PAD = LANES

ADAM_LR = 0.001
ADAM_B1 = 0.9
ADAM_B2 = 0.999
ADAM_EPS = 1e-08
ADAM_WD = 0.01
ADAM_STEP = 10

VMEM_LIMIT = 56 * 1024 * 1024
MESH = pl.DeviceIdType.MESH
NEG_INF = float("-inf")


def _pick(n, candidates):
    for c in candidates:
        if n % c == 0:
            return c
    return n


def _params(*sem):
    return pltpu.CompilerParams(dimension_semantics=sem, vmem_limit_bytes=VMEM_LIMIT)


def _sigmoid(x):
    return 1.0 / (1.0 + jnp.exp(-x))


def _softplus(x):
    return jnp.maximum(x, 0.0) + jnp.log(1.0 + jnp.exp(-jnp.abs(x)))


def _dot(a, b, dims):
    return lax.dot_general(a.astype(BF16), b.astype(BF16), (dims, ((), ())), preferred_element_type=F32)


def _nn(a, b):
    return _dot(a, b, ((1,), (0,)))


def _nt(a, b):
    return _dot(a, b, ((1,), (1,)))


def _tn(a, b):
    return _dot(a, b, ((0,), (0,)))


def _split3(v):
    hi = v.astype(BF16)
    r1 = v - hi.astype(F32)
    mid = r1.astype(BF16)
    lo = (r1 - mid.astype(F32)).astype(BF16)
    return hi, mid, lo


def _mask_nn(mask, v):
    mb = mask.astype(BF16)
    hi, mid, lo = _split3(v)
    return _nn(mb, hi) + (_nn(mb, mid) + _nn(mb, lo))


def _nn_mask(v, mask):
    mb = mask.astype(BF16)
    hi, mid, lo = _split3(v)
    return _nn(hi, mb) + (_nn(mid, mb) + _nn(lo, mb))


MM_VMEM_BUDGET = 40 * 1024 * 1024
MM_FULL_K = 2816


def _mm_tiles(m, n, k, a_bytes, b_bytes, o_bytes, has_add):
    tk = k if k <= MM_FULL_K else _pick(k, (2048, 1024, 512, 256, 128))
    tn = 1408 if (n > 1024 and n % 1408 == 0) else _pick(n, (1024, 768, 512, 384, 256, 128))
    for tm in (1408, 1024, 768, 512, 384, 256, 128):
        if m % tm:
            continue
        buffers = 2 * (tm * tk * a_bytes + tk * tn * b_bytes + tm * tn * (o_bytes + (4 if has_add else 0)))
        if tk < k:
            buffers += tm * tn * 4
        if buffers <= MM_VMEM_BUDGET:
            return tm, tn, tk
    return _pick(m, (128,)), tn, tk


def _mm(a, b, *, mode, name, out_dtype=F32, add=None):
    if mode == "nn":
        (m, k), n = a.shape, b.shape[1]
    elif mode == "nt":
        (m, k), n = a.shape, b.shape[0]
    else:
        (k, m), n = a.shape, b.shape[1]
    has_add = add is not None
    tm, tn, tk = _mm_tiles(m, n, k, a.dtype.itemsize, b.dtype.itemsize, jnp.dtype(out_dtype).itemsize, has_add)
    nk = k // tk
    dims = {"nn": ((1,), (0,)), "nt": ((1,), (1,)), "tn": ((0,), (0,))}[mode]
    a_spec = {"nn": pl.BlockSpec((tm, tk), lambda i, j, kk: (i, kk)),
              "nt": pl.BlockSpec((tm, tk), lambda i, j, kk: (i, kk)),
              "tn": pl.BlockSpec((tk, tm), lambda i, j, kk: (kk, i))}[mode]
    b_spec = {"nn": pl.BlockSpec((tk, tn), lambda i, j, kk: (kk, j)),
              "nt": pl.BlockSpec((tn, tk), lambda i, j, kk: (j, kk)),
              "tn": pl.BlockSpec((tk, tn), lambda i, j, kk: (kk, j))}[mode]
    o_spec = pl.BlockSpec((tm, tn), lambda i, j, kk: (i, j))

    def finish(r, c_ref, o_ref):
        if has_add:
            r = r + c_ref[...]
        o_ref[...] = r.astype(out_dtype)

    def body_one(*refs):
        a_ref, b_ref = refs[:2]
        finish(_dot(a_ref[...], b_ref[...], dims), refs[2] if has_add else None, refs[-1])

    def body_acc(*refs):
        a_ref, b_ref = refs[:2]
        o_ref, acc = refs[-2:]
        kk = pl.program_id(2)

        @pl.when(kk == 0)
        def _():
            acc[...] = jnp.zeros_like(acc)

        acc[...] += _dot(a_ref[...], b_ref[...], dims)

        @pl.when(kk == nk - 1)
        def _():
            finish(acc[...], refs[2] if has_add else None, o_ref)

    in_specs = [a_spec, b_spec] + ([o_spec] if has_add else [])
    args = (a, b) + ((add,) if has_add else ())
    return pl.pallas_call(
        body_one if nk == 1 else body_acc, name=name, grid=(m // tm, n // tn, nk),
        in_specs=in_specs, out_specs=o_spec,
        out_shape=jax.ShapeDtypeStruct((m, n), out_dtype),
        scratch_shapes=[] if nk == 1 else [pltpu.VMEM((tm, tn), F32)],
        compiler_params=_params("parallel", "parallel", "arbitrary"),
    )(*args)


def _rmsnorm_fwd(x, g, name):
    t, d = x.shape
    tm = _pick(t, (512, 256, 128))

    def body(x_ref, g_ref, o_ref):
        xv = x_ref[...]
        r = lax.rsqrt(jnp.mean(xv * xv, axis=-1, keepdims=True) + EPS)
        o_ref[...] = ((xv * r) * g_ref[...]).astype(BF16)

    return pl.pallas_call(
        body, name=name, grid=(t // tm,),
        in_specs=[pl.BlockSpec((tm, d), lambda i: (i, 0)), pl.BlockSpec((1, d), lambda i: (0, 0))],
        out_specs=pl.BlockSpec((tm, d), lambda i: (i, 0)),
        out_shape=jax.ShapeDtypeStruct((t, d), BF16),
        compiler_params=_params("parallel"),
    )(x, g)


def _rmsnorm_bwd(x, g, dh, dres, name, with_bf16=False):
    t, d = x.shape
    tm = _pick(t, (512, 256, 128))

    def body(x_ref, g_ref, dh_ref, dres_ref, dx_ref, dg_ref, *dxb_ref):
        @pl.when(pl.program_id(0) == 0)
        def _():
            dg_ref[...] = jnp.zeros_like(dg_ref)

        xv = x_ref[...]
        r = lax.rsqrt(jnp.mean(xv * xv, axis=-1, keepdims=True) + EPS)
        xhat = xv * r
        dhv = dh_ref[...]
        dyg = dhv * g_ref[...]
        dx = dres_ref[...] + r * (dyg - xhat * jnp.mean(dyg * xhat, axis=-1, keepdims=True))
        dx_ref[...] = dx
        if with_bf16:
            dxb_ref[0][...] = dx.astype(BF16)
        dg_ref[...] += jnp.sum(dhv * xhat, axis=0, keepdims=True)

    row = pl.BlockSpec((tm, d), lambda i: (i, 0))
    vec = pl.BlockSpec((1, d), lambda i: (0, 0))
    extra = with_bf16 * [jax.ShapeDtypeStruct((t, d), BF16)]
    return pl.pallas_call(
        body, name=name, grid=(t // tm,),
        in_specs=[row, vec, row, row], out_specs=[row, vec] + with_bf16 * [row],
        out_shape=[jax.ShapeDtypeStruct((t, d), F32), jax.ShapeDtypeStruct((1, d), F32)] + extra,
        compiler_params=_params("arbitrary"),
    )(x, g, dh, dres)


def _loss_head(x, g, target, name):
    t, d = x.shape
    tm = _pick(t, (512, 256, 128))

    def body(x_ref, g_ref, t_ref, loss_ref, dx_ref, dg_ref, dxb_ref):
        @pl.when(pl.program_id(0) == 0)
        def _():
            dg_ref[...] = jnp.zeros_like(dg_ref)
            loss_ref[...] = jnp.zeros_like(loss_ref)

        xv = x_ref[...]
        gv = g_ref[...]
        r = lax.rsqrt(jnp.mean(xv * xv, axis=-1, keepdims=True) + EPS)
        xhat = xv * r
        err = xhat * gv - t_ref[...]
        loss_ref[...] += jnp.sum(err * err) * (0.5 / d)
        dy = err * (1.0 / d)
        dyg = dy * gv
        dx = r * (dyg - xhat * jnp.mean(dyg * xhat, axis=-1, keepdims=True))
        dx_ref[...] = dx
        dxb_ref[...] = dx.astype(BF16)
        dg_ref[...] += jnp.sum(dy * xhat, axis=0, keepdims=True)

    row = pl.BlockSpec((tm, d), lambda i: (i, 0))
    vec = pl.BlockSpec((1, d), lambda i: (0, 0))
    return pl.pallas_call(
        body, name=name, grid=(t // tm,),
        in_specs=[row, vec, row],
        out_specs=[pl.BlockSpec((1, LANES), lambda i: (0, 0)), row, vec, row],
        out_shape=[jax.ShapeDtypeStruct((1, LANES), F32), jax.ShapeDtypeStruct((t, d), F32),
                   jax.ShapeDtypeStruct((1, d), F32), jax.ShapeDtypeStruct((t, d), BF16)],
        compiler_params=_params("arbitrary"),
    )(x, g, target)


CONV_HALO = 8


def _conv_fwd(u, w, bias, name):
    b, s, c = u.shape
    ts = _pick(s, (512, 256, 128))
    cb = _pick(c, (512, 384, 256, 128))
    hb = ts // CONV_HALO

    def body(u_ref, h_ref, w_ref, b_ref, o_ref):
        uv = u_ref[...]
        halo = jnp.where(pl.program_id(1) == 0, 0.0, h_ref[...])
        ext = jnp.concatenate([halo, uv], axis=0)
        wv = w_ref[...]
        acc = b_ref[...] + wv[SSM_CONV - 1:SSM_CONV, :] * uv
        for sh in range(1, SSM_CONV):
            kidx = SSM_CONV - 1 - sh
            acc = acc + wv[kidx:kidx + 1, :] * ext[CONV_HALO - sh:CONV_HALO - sh + ts, :]
        o_ref[...] = acc * _sigmoid(acc)

    return pl.pallas_call(
        body, name=name, grid=(b, s // ts, c // cb),
        in_specs=[pl.BlockSpec((None, ts, cb), lambda bi, i, j: (bi, i, j)),
                  pl.BlockSpec((None, CONV_HALO, cb), lambda bi, i, j: (bi, jnp.maximum(i * hb - 1, 0), j)),
                  pl.BlockSpec((SSM_CONV, cb), lambda bi, i, j: (0, j)),
                  pl.BlockSpec((1, cb), lambda bi, i, j: (0, j))],
        out_specs=pl.BlockSpec((None, ts, cb), lambda bi, i, j: (bi, i, j)),
        out_shape=jax.ShapeDtypeStruct((b, s, c), F32),
        compiler_params=_params("parallel", "parallel", "parallel"),
    )(u, u, w, bias)


def _conv_bwd(u, dout, w, bias, name):
    b, s, c = u.shape
    ts = _pick(s, (512, 256, 128))
    cb = _pick(c, (512, 384, 256, 128))
    hb = ts // CONV_HALO
    n_t = s // ts

    def pre_act_grad(ext_u, cur_u, dout_v, wv, bv, rows):
        acc = bv + wv[SSM_CONV - 1:SSM_CONV, :] * cur_u
        for sh in range(1, SSM_CONV):
            kidx = SSM_CONV - 1 - sh
            acc = acc + wv[kidx:kidx + 1, :] * ext_u[CONV_HALO - sh:CONV_HALO - sh + rows, :]
        sg = _sigmoid(acc)
        return dout_v * (sg * (1.0 + acc * (1.0 - sg)))

    def body(u_ref, up_ref, un_ref, d_ref, dn_ref, w_ref, b_ref, du_ref, dw_ref, db_ref):
        i = pl.program_id(2)
        first = jnp.logical_and(pl.program_id(1) == 0, i == 0)

        @pl.when(first)
        def _():
            dw_ref[...] = jnp.zeros_like(dw_ref)
            db_ref[...] = jnp.zeros_like(db_ref)

        wv = w_ref[...]
        bv = b_ref[...]
        uv = u_ref[...]
        u_prev = jnp.where(i == 0, 0.0, up_ref[...])
        ext_u = jnp.concatenate([u_prev, uv], axis=0)
        dpre = pre_act_grad(ext_u, uv, d_ref[...], wv, bv, ts)
        un = un_ref[...]
        ext_n = jnp.concatenate([uv[ts - CONV_HALO:, :], un], axis=0)
        dpre_n = pre_act_grad(ext_n, un, dn_ref[...], wv, bv, CONV_HALO)
        dpre_n = jnp.where(i == n_t - 1, 0.0, dpre_n)
        ext_d = jnp.concatenate([dpre, dpre_n], axis=0)
        du = wv[SSM_CONV - 1:SSM_CONV, :] * dpre
        dw_ref[SSM_CONV - 1:SSM_CONV, :] += jnp.sum(dpre * uv, axis=0, keepdims=True)
        for sh in range(1, SSM_CONV):
            kidx = SSM_CONV - 1 - sh
            du = du + wv[kidx:kidx + 1, :] * ext_d[sh:sh + ts, :]
            dw_ref[kidx:kidx + 1, :] += jnp.sum(dpre * ext_u[CONV_HALO - sh:CONV_HALO - sh + ts, :],
                                                axis=0, keepdims=True)
        du_ref[...] = du.astype(BF16)
        db_ref[...] += jnp.sum(dpre, axis=0, keepdims=True)

    last_hb = s // CONV_HALO - 1
    tile = pl.BlockSpec((None, ts, cb), lambda j, bi, i: (bi, i, j))
    prev = pl.BlockSpec((None, CONV_HALO, cb), lambda j, bi, i: (bi, jnp.maximum(i * hb - 1, 0), j))
    nxt = pl.BlockSpec((None, CONV_HALO, cb), lambda j, bi, i: (bi, jnp.minimum((i + 1) * hb, last_hb), j))
    return pl.pallas_call(
        body, name=name, grid=(c // cb, b, n_t),
        in_specs=[tile, prev, nxt, tile, nxt,
                  pl.BlockSpec((SSM_CONV, cb), lambda j, bi, i: (0, j)),
                  pl.BlockSpec((1, cb), lambda j, bi, i: (0, j))],
        out_specs=[tile, pl.BlockSpec((SSM_CONV, cb), lambda j, bi, i: (0, j)),
                   pl.BlockSpec((1, cb), lambda j, bi, i: (0, j))],
        out_shape=[jax.ShapeDtypeStruct((b, s, c), BF16), jax.ShapeDtypeStruct((SSM_CONV, c), F32),
                   jax.ShapeDtypeStruct((1, c), F32)],
        compiler_params=_params("parallel", "arbitrary", "arbitrary"),
    )(u, u, u, dout, dout, w, bias)


def _ssd_chunk_terms(dtr_ref, bias_ref, alog_ref):
    q = SSM_CHUNK
    dt = _softplus(dtr_ref[...] + bias_ref[...])
    a_neg = -jnp.exp(alog_ref[...])
    row = lax.broadcasted_iota(jnp.int32, (q, q), 0)
    col = lax.broadcasted_iota(jnp.int32, (q, q), 1)
    lower = row >= col
    s = _mask_nn(lower, dt * a_neg)
    return dt, a_neg, s, s.T, lower


def _ssd_fwd(xc, dtr, dt_bias, a_log, d_skip, name):
    b, s, _ = xc.shape
    q = SSM_CHUNK
    nc = s // q
    p, n = SSM_HEAD_DIM, SSM_D_STATE

    def body(xc_ref, dtr_ref, bias_ref, alog_ref, dsk_ref, y_ref, hs_ref, h_scr):
        @pl.when(pl.program_id(1) == 0)
        def _():
            h_scr[...] = jnp.zeros_like(h_scr)

        dt, _, s_col, s_row, lower = _ssd_chunk_terms(dtr_ref, bias_ref, alog_ref)
        tot = s_col[q - 1:q, :]
        dec = jnp.exp(tot - s_col)
        es = jnp.exp(s_col)
        etot = jnp.exp(tot)
        dsk = dsk_ref[...]
        for g in range(SSM_N_GROUPS):
            bg = xc_ref[:, SSM_D_INNER + n * g:SSM_D_INNER + n * (g + 1)].astype(BF16)
            cg = xc_ref[:, SSM_D_INNER + n * (SSM_N_GROUPS + g):SSM_D_INNER + n * (SSM_N_GROUPS + g + 1)].astype(BF16)
            gm = _nt(cg, bg)
            for j in range(SSM_HEADS_PER_GROUP):
                h = g * SSM_HEADS_PER_GROUP + j
                xh = xc_ref[:, p * h:p * (h + 1)]
                xdt = xh * dt[:, h:h + 1]
                lm = jnp.exp(jnp.where(lower, s_col[:, h:h + 1] - s_row[h:h + 1, :], NEG_INF))
                y_diag = _nn(gm * lm, xdt)
                hh = h_scr[h]
                hs_ref[h] = hh
                y_off = es[:, h:h + 1] * _nt(cg, hh)
                y_ref[:, p * h:p * (h + 1)] = y_diag + y_off + dsk[:, h:h + 1] * xh
                h_scr[h] = etot[:, h:h + 1] * hh + _tn(xdt * dec[:, h:h + 1], bg)

    vec = pl.BlockSpec((1, LANES), lambda bi, c: (0, 0))
    return pl.pallas_call(
        body, name=name, grid=(b, nc),
        in_specs=[pl.BlockSpec((None, q, SSM_CONV_DIM), lambda bi, c: (bi, c, 0)),
                  pl.BlockSpec((None, q, LANES), lambda bi, c: (bi, c, 0)), vec, vec, vec],
        out_specs=[pl.BlockSpec((None, q, SSM_D_INNER), lambda bi, c: (bi, c, 0)),
                   pl.BlockSpec((None, None, SSM_N_HEADS, p, n), lambda bi, c: (bi, c, 0, 0, 0))],
        out_shape=[jax.ShapeDtypeStruct((b, s, SSM_D_INNER), F32),
                   jax.ShapeDtypeStruct((b, nc, SSM_N_HEADS, p, n), F32)],
        scratch_shapes=[pltpu.VMEM((SSM_N_HEADS, p, n), F32)],
        compiler_params=_params("parallel", "arbitrary"),
    )(xc, dtr, dt_bias, a_log, d_skip)


def _ssd_bwd(xc, dtr, dy, hs, dt_bias, a_log, d_skip, name):
    b, s, _ = xc.shape
    q = SSM_CHUNK
    nc = s // q
    p, n = SSM_HEAD_DIM, SSM_D_STATE

    def body(xc_ref, dtr_ref, dy_ref, hs_ref, bias_ref, alog_ref, dsk_ref,
             dxc_ref, ddtr_ref, dalog_ref, ddsk_ref, dbias_ref, dh_scr, p_ds, p_dt, p_dd, p_tot):
        ci = pl.program_id(1)

        @pl.when(ci == 0)
        def _():
            dh_scr[...] = jnp.zeros_like(dh_scr)

        @pl.when(jnp.logical_and(pl.program_id(0) == 0, ci == 0))
        def _():
            dalog_ref[...] = jnp.zeros_like(dalog_ref)
            ddsk_ref[...] = jnp.zeros_like(ddsk_ref)
            dbias_ref[...] = jnp.zeros_like(dbias_ref)

        dt, a_neg, s_col, s_row, lower = _ssd_chunk_terms(dtr_ref, bias_ref, alog_ref)
        upper = jnp.logical_not(lower) | (lax.broadcasted_iota(jnp.int32, (q, q), 0)
                                          == lax.broadcasted_iota(jnp.int32, (q, q), 1))
        tot = s_col[q - 1:q, :]
        dec = jnp.exp(tot - s_col)
        es = jnp.exp(s_col)
        etot = jnp.exp(tot)
        dsk = dsk_ref[...]
        lane = lax.broadcasted_iota(jnp.int32, (1, LANES), 1)
        hdot = jnp.zeros((1, LANES), F32)
        ds_mask = jnp.zeros((q, LANES), F32)
        for g in range(SSM_N_GROUPS):
            b_lo = SSM_D_INNER + n * g
            c_lo = SSM_D_INNER + n * (SSM_N_GROUPS + g)
            bg = xc_ref[:, b_lo:b_lo + n].astype(BF16)
            cg = xc_ref[:, c_lo:c_lo + n].astype(BF16)
            gm = _nt(cg, bg)
            gmt = _nt(bg, cg)
            dg = jnp.zeros((q, q), F32)
            dgt = jnp.zeros((q, q), F32)
            dcg = jnp.zeros((q, n), F32)
            dbg = jnp.zeros((q, n), F32)
            for j in range(SSM_HEADS_PER_GROUP):
                h = g * SSM_HEADS_PER_GROUP + j
                sl = slice(p * h, p * (h + 1))
                xh = xc_ref[:, sl]
                dth = dt[:, h:h + 1]
                xdt = xh * dth
                dyh = dy_ref[:, sl]
                lm = jnp.exp(jnp.where(lower, s_col[:, h:h + 1] - s_row[h:h + 1, :], NEG_INF))
                lmt = jnp.exp(jnp.where(upper, s_row[h:h + 1, :] - s_col[:, h:h + 1], NEG_INF))
                dm = _nt(dyh, xdt)
                dmt = _nt(xdt, dyh)
                dg = dg + dm * lm
                dgt = dgt + dmt * lmt
                mt = gmt * lmt
                ds_h = (jnp.sum(dm * (gm * lm), axis=1, keepdims=True)
                        - jnp.sum(dmt * mt, axis=1, keepdims=True))
                ds_mask = jnp.where(lane == h, ds_h, ds_mask)
                dx_diag = _nn(mt, dyh)
                hh = hs_ref[h]
                dhn = dh_scr[h]
                dw = es[:, h:h + 1] * dyh
                dcg = dcg + _nn(dw, hh)
                dech = dec[:, h:h + 1]
                dx_state = dech * _nt(bg, dhn)
                dbg = dbg + _nn(xdt * dech, dhn)
                dxdt = dx_diag + dx_state
                dskh = dsk[:, h:h + 1]
                dxc_ref[:, sl] = dxdt * dth + dskh * dyh
                p_ds[:, sl] = dw * _nt(cg, hh) - xdt * dx_state
                p_dt[:, sl] = dxdt * xh
                p_dd[:, sl] = dyh * xh
                p_tot[:, sl] = xdt * dx_state
                hdot = jnp.where(lane == h, jnp.sum(dhn * hh), hdot)
                dh_scr[h] = _tn(dw, cg) + etot[:, h:h + 1] * dhn
            dxc_ref[:, b_lo:b_lo + n] = dbg + _nn(dgt, cg)
            dxc_ref[:, c_lo:c_lo + n] = dcg + _nn(dg, bg)
        ind = ((lax.broadcasted_iota(jnp.int32, (SSM_D_INNER, LANES), 0) >> HEAD_DIM_LOG2)
               == lax.broadcasted_iota(jnp.int32, (SSM_D_INNER, LANES), 1))
        r_ds = _nn_mask(p_ds[...], ind)
        r_dt = _nn_mask(p_dt[...], ind)
        r_dd = _nn_mask(p_dd[...], ind)
        r_tot = _nn_mask(p_tot[...], ind)
        dtot = jnp.sum(r_tot, axis=0, keepdims=True) + etot * hdot
        last = lax.broadcasted_iota(jnp.int32, (q, LANES), 0) == q - 1
        ds = ds_mask + r_ds + jnp.where(last, dtot, 0.0)
        da = _mask_nn(upper, ds)
        ddt = da * a_neg + r_dt
        live = lane < SSM_N_HEADS
        sg = _sigmoid(dtr_ref[...] + bias_ref[...])
        ddtr = jnp.where(live, ddt * sg, 0.0)
        ddtr_ref[...] = ddtr.astype(BF16)
        dalog_ref[...] += jnp.where(live, jnp.sum(da * dt, axis=0, keepdims=True) * a_neg, 0.0)
        ddsk_ref[...] += jnp.where(live, jnp.sum(r_dd, axis=0, keepdims=True), 0.0)
        dbias_ref[...] += jnp.sum(ddtr, axis=0, keepdims=True)

    rev = lambda bi, c: (bi, nc - 1 - c, 0)
    vec = pl.BlockSpec((1, LANES), lambda bi, c: (0, 0))
    wide = pl.BlockSpec((None, q, SSM_D_INNER), rev)
    return pl.pallas_call(
        body, name=name, grid=(b, nc),
        in_specs=[pl.BlockSpec((None, q, SSM_CONV_DIM), rev), pl.BlockSpec((None, q, LANES), rev), wide,
                  pl.BlockSpec((None, None, SSM_N_HEADS, p, n), lambda bi, c: (bi, nc - 1 - c, 0, 0, 0)),
                  vec, vec, vec],
        out_specs=[pl.BlockSpec((None, q, SSM_CONV_DIM), rev), pl.BlockSpec((None, q, LANES), rev), vec, vec, vec],
        out_shape=[jax.ShapeDtypeStruct((b, s, SSM_CONV_DIM), F32), jax.ShapeDtypeStruct((b, s, LANES), BF16),
                   jax.ShapeDtypeStruct((1, LANES), F32), jax.ShapeDtypeStruct((1, LANES), F32),
                   jax.ShapeDtypeStruct((1, LANES), F32)],
        scratch_shapes=[pltpu.VMEM((SSM_N_HEADS, p, n), F32)] + [pltpu.VMEM((q, SSM_D_INNER), F32)] * 4,
        compiler_params=_params("arbitrary", "arbitrary"),
    )(xc, dtr, dy, hs, dt_bias, a_log, d_skip)


SSM_GROUP_WIDTH = SSM_D_INNER // SSM_N_GROUPS


def _gate_norm_fwd(y, z, w, name):
    t, d = y.shape
    tm = _pick(t, (256, 128))

    def body(y_ref, z_ref, w_ref, o_ref):
        for g in range(SSM_N_GROUPS):
            sl = slice(SSM_GROUP_WIDTH * g, SSM_GROUP_WIDTH * (g + 1))
            zv = z_ref[:, sl]
            u = y_ref[:, sl] * (zv * _sigmoid(zv))
            r = lax.rsqrt(jnp.mean(u * u, axis=-1, keepdims=True) + EPS)
            o_ref[:, sl] = ((u * r) * w_ref[:, sl]).astype(BF16)

    row = pl.BlockSpec((tm, d), lambda i: (i, 0))
    return pl.pallas_call(
        body, name=name, grid=(t // tm,),
        in_specs=[row, row, pl.BlockSpec((1, d), lambda i: (0, 0))], out_specs=row,
        out_shape=jax.ShapeDtypeStruct((t, d), BF16),
        compiler_params=_params("parallel"),
    )(y, z, w)


def _gate_norm_bwd(y, z, w, dout, name):
    t, d = y.shape
    tm = _pick(t, (256, 128))

    def body(y_ref, z_ref, w_ref, do_ref, dy_ref, dz_ref, dw_ref):
        @pl.when(pl.program_id(0) == 0)
        def _():
            dw_ref[...] = jnp.zeros_like(dw_ref)

        for g in range(SSM_N_GROUPS):
            sl = slice(SSM_GROUP_WIDTH * g, SSM_GROUP_WIDTH * (g + 1))
            zv = z_ref[:, sl]
            yv = y_ref[:, sl]
            sg = _sigmoid(zv)
            silu = zv * sg
            u = yv * silu
            r = lax.rsqrt(jnp.mean(u * u, axis=-1, keepdims=True) + EPS)
            uh = u * r
            dov = do_ref[:, sl]
            dw_ref[:, sl] += jnp.sum(dov * uh, axis=0, keepdims=True)
            dyg = dov * w_ref[:, sl]
            du = r * (dyg - uh * jnp.mean(dyg * uh, axis=-1, keepdims=True))
            dy_ref[:, sl] = du * silu
            dz_ref[:, sl] = (du * yv * (sg * (1.0 + zv * (1.0 - sg)))).astype(BF16)

    row = pl.BlockSpec((tm, d), lambda i: (i, 0))
    vec = pl.BlockSpec((1, d), lambda i: (0, 0))
    return pl.pallas_call(
        body, name=name, grid=(t // tm,),
        in_specs=[row, row, vec, row], out_specs=[row, row, vec],
        out_shape=[jax.ShapeDtypeStruct((t, d), F32), jax.ShapeDtypeStruct((t, d), BF16),
                   jax.ShapeDtypeStruct((1, d), F32)],
        compiler_params=_params("arbitrary"),
    )(y, z, w, dout)


def _rope_tables(s):
    half = ATT_HEAD_DIM // 2
    inv = ROPE_THETA ** (-jnp.arange(half, dtype=F32) / half)
    ang = jnp.arange(s).astype(F32)[:, None] * inv[None, :]
    cos, sin = jnp.cos(ang), jnp.sin(ang)
    return jnp.concatenate([cos, cos], axis=-1), jnp.concatenate([-sin, sin], axis=-1)


def _rope_fwd(qkv, cosf, sinf, name):
    b, s, w = qkv.shape
    ts = _pick(s, (256, 128))
    d = ATT_HEAD_DIM

    def body(x_ref, c_ref, s_ref, o_ref):
        cv, sv = c_ref[...], s_ref[...]
        for hd in range(2 * ATT_N_HEADS):
            tv = x_ref[:, d * hd:d * (hd + 1)]
            o_ref[:, d * hd:d * (hd + 1)] = (tv * cv + pltpu.roll(tv, d // 2, 1) * sv).astype(BF16)
        o_ref[:, 2 * ATT_N_HEADS * d:] = x_ref[:, 2 * ATT_N_HEADS * d:].astype(BF16)

    tab = pl.BlockSpec((ts, d), lambda bi, i: (i, 0))
    row = pl.BlockSpec((None, ts, w), lambda bi, i: (bi, i, 0))
    return pl.pallas_call(
        body, name=name, grid=(b, s // ts), in_specs=[row, tab, tab], out_specs=row,
        out_shape=jax.ShapeDtypeStruct((b, s, w), BF16),
        compiler_params=_params("parallel", "parallel"),
    )(qkv, cosf, sinf)


def _rope_bwd(dq, dk, dv, cosf, sinf, name):
    b, s, gw = dq[0].shape
    ts = _pick(s, (256, 128))
    d = ATT_HEAD_DIM
    n_pat = len(ATT_DILATIONS)

    def body(*refs):
        ins, (c_ref, s_ref, o_ref) = refs[:3 * n_pat], refs[3 * n_pat:]
        cv, sv = c_ref[...], s_ref[...]
        for kind in range(3):
            for gi in range(n_pat):
                src = ins[kind * n_pat + gi]
                for j in range(ATT_HEADS_PER_GROUP):
                    tv = src[:, d * j:d * (j + 1)]
                    if kind < 2:
                        tv = tv * cv + pltpu.roll(tv * sv, d // 2, 1)
                    lo = d * (kind * ATT_N_HEADS + gi * ATT_HEADS_PER_GROUP + j)
                    o_ref[:, lo:lo + d] = tv.astype(BF16)

    tab = pl.BlockSpec((ts, d), lambda bi, i: (i, 0))
    part = pl.BlockSpec((None, ts, gw), lambda bi, i: (bi, i, 0))
    return pl.pallas_call(
        body, name=name, grid=(b, s // ts), in_specs=[part] * (3 * n_pat) + [tab, tab],
        out_specs=pl.BlockSpec((None, ts, ATT_QKV_DIM), lambda bi, i: (bi, i, 0)),
        out_shape=jax.ShapeDtypeStruct((b, s, ATT_QKV_DIM), BF16),
        compiler_params=_params("parallel", "parallel"),
    )(*dq, *dk, *dv, cosf, sinf)


ATT_SCALE = ATT_HEAD_DIM ** -0.5
QKV_BLOCKS = ATT_QKV_DIM // ATT_OUT_DIM


def _att_views(gi, r):
    def qkv_map(kind, shift):
        def index(bi, ri, nb_i):
            return (bi, jnp.maximum(nb_i + shift, 0), ri * QKV_BLOCKS + 3 * kind + gi)
        return index

    def out_map(shift):
        def index(bi, ri, nb_i):
            return (bi, jnp.maximum(nb_i + shift, 0), ri)
        return index
    return qkv_map, out_map


def _band_mask(shape, q_axis, has_prev):
    qi = lax.broadcasted_iota(jnp.int32, shape, q_axis)
    kj = lax.broadcasted_iota(jnp.int32, shape, 1 - q_axis)
    dist = qi + ATT_BLOCK - kj
    return (dist >= 0) & (dist <= ATT_BLOCK) & (has_prev | (kj >= ATT_BLOCK))


def _att_fwd(qkr, gi, r, name):
    b, s, w = qkr.shape
    l = s // r
    nb = l // ATT_BLOCK
    d = ATT_HEAD_DIM
    qkv_map, out_map = _att_views(gi, r)

    def body(q_ref, kp_ref, k_ref, vp_ref, v_ref, o_ref, lse_ref):
        mask = _band_mask((ATT_BLOCK, 2 * ATT_BLOCK), 0, pl.program_id(2) > 0)
        for j in range(ATT_HEADS_PER_GROUP):
            sl = slice(d * j, d * (j + 1))
            kcat = jnp.concatenate([kp_ref[:, sl], k_ref[:, sl]], axis=0)
            vcat = jnp.concatenate([vp_ref[:, sl], v_ref[:, sl]], axis=0)
            sc = jnp.where(mask, _nt(q_ref[:, sl], kcat) * ATT_SCALE, NEG_INF)
            m = jnp.max(sc, axis=-1, keepdims=True)
            pr = jnp.exp(sc - m)
            den = jnp.sum(pr, axis=-1, keepdims=True)
            o_ref[:, sl] = _nn(pr / den, vcat)
            lse_ref[:, sl] = jnp.broadcast_to(m + jnp.log(den), (ATT_BLOCK, d))

    blk = (None, ATT_BLOCK, ATT_OUT_DIM)
    out_spec = pl.BlockSpec(blk, out_map(0))
    o, lse = pl.pallas_call(
        body, name=name, grid=(b, r, nb),
        in_specs=[pl.BlockSpec(blk, qkv_map(0, 0)),
                  pl.BlockSpec(blk, qkv_map(1, -1)), pl.BlockSpec(blk, qkv_map(1, 0)),
                  pl.BlockSpec(blk, qkv_map(2, -1)), pl.BlockSpec(blk, qkv_map(2, 0))],
        out_specs=[out_spec, out_spec],
        out_shape=[jax.ShapeDtypeStruct((b, l, r * ATT_OUT_DIM), F32)] * 2,
        compiler_params=_params("parallel", "parallel", "parallel"),
    )(*([qkr.reshape(b, l, r * w)] * 5))
    return o.reshape(b, s, ATT_OUT_DIM), lse.reshape(b, s, ATT_OUT_DIM)


def _att_merge(os_, lses, name):
    b, s, w = os_[0].shape
    ts = _pick(s, (512, 256, 128))
    n_pat = len(os_)

    def body(*refs):
        o_refs, l_refs, (att_ref, lse_ref) = refs[:n_pat], refs[n_pat:2 * n_pat], refs[2 * n_pat:]
        ls = [r_[...] for r_ in l_refs]
        m = functools.reduce(jnp.maximum, ls)
        es = [jnp.exp(lv - m) for lv in ls]
        tot = functools.reduce(lambda u, v: u + v, es)
        acc = (es[0] / tot) * o_refs[0][...]
        for gi in range(1, n_pat):
            acc = acc + (es[gi] / tot) * o_refs[gi][...]
        att_ref[...] = acc
        lse_ref[...] = m + jnp.log(tot)

    row = pl.BlockSpec((None, ts, w), lambda bi, i: (bi, i, 0))
    return pl.pallas_call(
        body, name=name, grid=(b, s // ts), in_specs=[row] * (2 * n_pat), out_specs=[row, row],
        out_shape=[jax.ShapeDtypeStruct((b, s, w), F32)] * 2,
        compiler_params=_params("parallel", "parallel"),
    )(*os_, *lses)


def _att_delta(att, datt, name):
    b, s, w = att.shape
    ts = _pick(s, (512, 256, 128))
    d = ATT_HEAD_DIM

    def body(a_ref, d_ref, o_ref):
        for j in range(ATT_HEADS_PER_GROUP):
            sl = slice(d * j, d * (j + 1))
            o_ref[:, sl] = jnp.broadcast_to(jnp.sum(a_ref[:, sl] * d_ref[:, sl], axis=-1, keepdims=True), (ts, d))

    row = pl.BlockSpec((None, ts, w), lambda bi, i: (bi, i, 0))
    return pl.pallas_call(
        body, name=name, grid=(b, s // ts), in_specs=[row, row], out_specs=row,
        out_shape=jax.ShapeDtypeStruct((b, s, w), F32),
        compiler_params=_params("parallel", "parallel"),
    )(att, datt)


def _att_bwd_q(qkr, datt, lse, delta, gi, r, name):
    b, s, w = qkr.shape
    l = s // r
    nb = l // ATT_BLOCK
    d = ATT_HEAD_DIM
    qkv_map, out_map = _att_views(gi, r)

    def body(q_ref, kp_ref, k_ref, vp_ref, v_ref, do_ref, lse_ref, dl_ref, dq_ref):
        mask = _band_mask((ATT_BLOCK, 2 * ATT_BLOCK), 0, pl.program_id(2) > 0)
        for j in range(ATT_HEADS_PER_GROUP):
            sl = slice(d * j, d * (j + 1))
            kcat = jnp.concatenate([kp_ref[:, sl], k_ref[:, sl]], axis=0)
            vcat = jnp.concatenate([vp_ref[:, sl], v_ref[:, sl]], axis=0)
            sc = _nt(q_ref[:, sl], kcat) * ATT_SCALE
            pr = jnp.exp(jnp.where(mask, sc - lse_ref[:, d * j:d * j + 1], NEG_INF))
            dp = _nt(do_ref[:, sl], vcat)
            dsc = pr * (dp - dl_ref[:, d * j:d * j + 1])
            dq_ref[:, sl] = _nn(dsc, kcat) * ATT_SCALE

    blk = (None, ATT_BLOCK, ATT_OUT_DIM)
    tok = pl.BlockSpec(blk, out_map(0))
    qv = qkr.reshape(b, l, r * w)
    view = lambda a: a.reshape(b, l, r * ATT_OUT_DIM)
    dq = pl.pallas_call(
        body, name=name, grid=(b, r, nb),
        in_specs=[pl.BlockSpec(blk, qkv_map(0, 0)),
                  pl.BlockSpec(blk, qkv_map(1, -1)), pl.BlockSpec(blk, qkv_map(1, 0)),
                  pl.BlockSpec(blk, qkv_map(2, -1)), pl.BlockSpec(blk, qkv_map(2, 0)), tok, tok, tok],
        out_specs=tok,
        out_shape=jax.ShapeDtypeStruct((b, l, r * ATT_OUT_DIM), F32),
        compiler_params=_params("parallel", "parallel", "parallel"),
    )(qv, qv, qv, qv, qv, view(datt), view(lse), view(delta))
    return dq.reshape(b, s, ATT_OUT_DIM)


def _att_bwd_kv(qkr, datt, lse, delta, gi, r, name):
    b, s, w = qkr.shape
    l = s // r
    nb = l // ATT_BLOCK
    d = ATT_HEAD_DIM

    def qkv_map(kind, shift):
        def index(bi, ri, nb_i):
            return (bi, jnp.minimum(nb_i + shift, nb - 1), ri * QKV_BLOCKS + 3 * kind + gi)
        return index

    def tok_map(shift):
        def index(bi, ri, nb_i):
            return (bi, jnp.minimum(nb_i + shift, nb - 1), ri)
        return index

    def body(k_ref, v_ref, q_ref, qn_ref, do_ref, don_ref, lse_ref, lsen_ref, dl_ref, dln_ref, dk_ref, dv_ref):
        shape = (ATT_BLOCK, 2 * ATT_BLOCK)
        kj = lax.broadcasted_iota(jnp.int32, shape, 0)
        qi = lax.broadcasted_iota(jnp.int32, shape, 1)
        dist = qi - kj
        has_next = pl.program_id(2) < nb - 1
        mask = (dist >= 0) & (dist <= ATT_BLOCK) & (has_next | (qi < ATT_BLOCK))
        for j in range(ATT_HEADS_PER_GROUP):
            sl = slice(d * j, d * (j + 1))
            qcat = jnp.concatenate([q_ref[:, sl], qn_ref[:, sl]], axis=0)
            docat = jnp.concatenate([do_ref[:, sl], don_ref[:, sl]], axis=0)
            lse_t = jnp.concatenate([lse_ref[:, sl], lsen_ref[:, sl]], axis=0).T
            dl_t = jnp.concatenate([dl_ref[:, sl], dln_ref[:, sl]], axis=0).T
            sc_t = _nt(k_ref[:, sl], qcat) * ATT_SCALE
            pr_t = jnp.exp(jnp.where(mask, sc_t - lse_t, NEG_INF))
            dv_ref[:, sl] = _nn(pr_t, docat)
            dsc_t = pr_t * (_nt(v_ref[:, sl], docat) - dl_t)
            dk_ref[:, sl] = _nn(dsc_t, qcat) * ATT_SCALE

    blk = (None, ATT_BLOCK, ATT_OUT_DIM)
    tok, tok_n = pl.BlockSpec(blk, tok_map(0)), pl.BlockSpec(blk, tok_map(1))
    qv = qkr.reshape(b, l, r * w)
    view = lambda a: a.reshape(b, l, r * ATT_OUT_DIM)
    dk, dv = pl.pallas_call(
        body, name=name, grid=(b, r, nb),
        in_specs=[pl.BlockSpec(blk, qkv_map(1, 0)), pl.BlockSpec(blk, qkv_map(2, 0)),
                  pl.BlockSpec(blk, qkv_map(0, 0)), pl.BlockSpec(blk, qkv_map(0, 1)),
                  tok, tok_n, tok, tok_n, tok, tok_n],
        out_specs=[tok, tok],
        out_shape=[jax.ShapeDtypeStruct((b, l, r * ATT_OUT_DIM), F32)] * 2,
        compiler_params=_params("parallel", "parallel", "parallel"),
    )(qv, qv, qv, qv, view(datt), view(datt), view(lse), view(lse), view(delta), view(delta))
    return dk.reshape(b, s, ATT_OUT_DIM), dv.reshape(b, s, ATT_OUT_DIM)


def _mix_fwd(gl, bg, ys, ya, name):
    t, d = ys.shape
    tm = _pick(t, (512, 256, 128))

    def body(gl_ref, bg_ref, ys_ref, ya_ref, o_ref):
        g0 = _sigmoid(gl_ref[:, :d] + bg_ref[:, :d])
        g1 = _sigmoid(gl_ref[:, d:] + bg_ref[:, d:])
        o_ref[...] = (g0 * ys_ref[...] + g1 * ya_ref[...]).astype(BF16)

    row = pl.BlockSpec((tm, d), lambda i: (i, 0))
    return pl.pallas_call(
        body, name=name, grid=(t // tm,),
        in_specs=[pl.BlockSpec((tm, 2 * d), lambda i: (i, 0)), pl.BlockSpec((1, 2 * d), lambda i: (0, 0)), row, row],
        out_specs=row, out_shape=jax.ShapeDtypeStruct((t, d), BF16),
        compiler_params=_params("parallel"),
    )(gl, bg, ys, ya)


def _mix_bwd(gl, bg, ys, ya, dmixed, name):
    t, d = ys.shape
    tm = _pick(t, (512, 256, 128))

    def body(gl_ref, bg_ref, ys_ref, ya_ref, dm_ref, dys_ref, dya_ref, dgl_ref, dbg_ref):
        @pl.when(pl.program_id(0) == 0)
        def _():
            dbg_ref[...] = jnp.zeros_like(dbg_ref)

        dm = dm_ref[...]
        g0 = _sigmoid(gl_ref[:, :d] + bg_ref[:, :d])
        g1 = _sigmoid(gl_ref[:, d:] + bg_ref[:, d:])
        dys_ref[...] = (dm * g0).astype(BF16)
        dya_ref[...] = (dm * g1).astype(BF16)
        d0 = dm * ys_ref[...] * (g0 * (1.0 - g0))
        d1 = dm * ya_ref[...] * (g1 * (1.0 - g1))
        dgl_ref[:, :d] = d0.astype(BF16)
        dgl_ref[:, d:] = d1.astype(BF16)
        dbg_ref[:, :d] += jnp.sum(d0, axis=0, keepdims=True)
        dbg_ref[:, d:] += jnp.sum(d1, axis=0, keepdims=True)

    row = pl.BlockSpec((tm, d), lambda i: (i, 0))
    wide = pl.BlockSpec((tm, 2 * d), lambda i: (i, 0))
    vec = pl.BlockSpec((1, 2 * d), lambda i: (0, 0))
    return pl.pallas_call(
        body, name=name, grid=(t // tm,),
        in_specs=[wide, vec, row, row, row], out_specs=[row, row, wide, vec],
        out_shape=[jax.ShapeDtypeStruct((t, d), BF16), jax.ShapeDtypeStruct((t, d), BF16),
                   jax.ShapeDtypeStruct((t, 2 * d), BF16), jax.ShapeDtypeStruct((1, 2 * d), F32)],
        compiler_params=_params("arbitrary"),
    )(gl, bg, ys, ya, dmixed)


def _swiglu_fwd(gt, up, name):
    t, f = gt.shape
    tm = _pick(t, (512, 256, 128))

    def body(g_ref, u_ref, o_ref):
        gv = g_ref[...]
        o_ref[...] = ((gv * _sigmoid(gv)) * u_ref[...]).astype(BF16)

    row = pl.BlockSpec((tm, f), lambda i: (i, 0))
    return pl.pallas_call(
        body, name=name, grid=(t // tm,), in_specs=[row, row], out_specs=row,
        out_shape=jax.ShapeDtypeStruct((t, f), BF16), compiler_params=_params("parallel"),
    )(gt, up)


def _swiglu_bwd(gt, up, dact, name):
    t, f = gt.shape
    tm = _pick(t, (512, 256, 128))

    def body(g_ref, u_ref, d_ref, dg_ref, du_ref):
        gv = g_ref[...]
        dv = d_ref[...]
        sg = _sigmoid(gv)
        dg_ref[...] = (dv * u_ref[...] * (sg * (1.0 + gv * (1.0 - sg)))).astype(BF16)
        du_ref[...] = (dv * (gv * sg)).astype(BF16)

    row = pl.BlockSpec((tm, f), lambda i: (i, 0))
    return pl.pallas_call(
        body, name=name, grid=(t // tm,), in_specs=[row, row, row], out_specs=[row, row],
        out_shape=[jax.ShapeDtypeStruct((t, f), BF16)] * 2, compiler_params=_params("parallel"),
    )(gt, up, dact)


def _peer(k):
    x, y, c = lax.axis_index("x"), lax.axis_index("y"), lax.axis_index("c")
    px, py, pc = x ^ ((k >> 2) & 1), y ^ ((k >> 1) & 1), c ^ (k & 1)
    return (px, py, pc), 4 * px + 2 * py + pc


def _my_index():
    return 4 * lax.axis_index("x") + 2 * lax.axis_index("y") + lax.axis_index("c")


def _all_gather(parts, name):
    n_parts = len(parts)

    def body(*refs):
        ins, outs = refs[:n_parts], refs[n_parts:2 * n_parts]
        send_sems, recv_sems, local_sems = refs[2 * n_parts:]
        here, me = _peer(0)
        sibling, sib_idx = _peer(1)
        chips = [_peer(2 * q) for q in range(1, N_CHIPS)]

        def copy(i, k, block, to, src=None):
            return pltpu.make_async_remote_copy(
                src_ref=outs[i].at[block] if src is None else src, dst_ref=outs[i].at[block],
                send_sem=send_sems.at[i * (N_DEV - 1) + k], recv_sem=recv_sems.at[i * (N_DEV - 1) + k],
                device_id=to, device_id_type=MESH)

        local = [pltpu.make_async_copy(ins[i], outs[i].at[me], local_sems.at[i]) for i in range(n_parts)]
        for cp in local:
            cp.start()
        sends = []
        for i in range(n_parts):
            sends.append(copy(i, 0, me, sibling, src=ins[i]))
            sends += [copy(i, q, me, chip, src=ins[i]) for q, (chip, _) in enumerate(chips, start=1)]
        for cp in sends:
            cp.start()
        for q, (chip, chip_idx) in enumerate(chips, start=1):
            for i in range(n_parts):
                copy(i, q, chip_idx, here).wait_recv()
                fwd = copy(i, N_CHIPS - 1 + q, chip_idx, sibling)
                fwd.start()
                sends.append(fwd)
        for i in range(n_parts):
            copy(i, 0, sib_idx, here).wait_recv()
        for q, (_, chip_idx) in enumerate(chips, start=1):
            for i in range(n_parts):
                copy(i, N_CHIPS - 1 + q, chip_idx ^ 1, here).wait_recv()
        for cp in sends:
            cp.wait_send()
        for cp in local:
            cp.wait()

    hbm = pl.BlockSpec(memory_space=pl.ANY)
    return pl.pallas_call(
        body, name=name, in_specs=[hbm] * n_parts, out_specs=[hbm] * n_parts,
        out_shape=[jax.ShapeDtypeStruct((N_DEV,) + p_.shape, p_.dtype) for p_ in parts],
        scratch_shapes=[pltpu.SemaphoreType.DMA((n_parts * (N_DEV - 1),)),
                        pltpu.SemaphoreType.DMA((n_parts * (N_DEV - 1),)),
                        pltpu.SemaphoreType.DMA((n_parts,))],
        compiler_params=pltpu.CompilerParams(has_side_effects=True),
    )(*parts)


def _pair_exchange(slabs, name):
    def body(slab_ref, got_ref, send_sems, recv_sems):
        c = lax.axis_index("c")
        sibling, _ = _peer(1)
        copies = [pltpu.make_async_remote_copy(
            src_ref=slab_ref.at[2 * q + 1 - c], dst_ref=got_ref.at[q], send_sem=send_sems.at[q],
            recv_sem=recv_sems.at[q], device_id=sibling, device_id_type=MESH) for q in range(N_CHIPS)]
        for cp in copies:
            cp.start()
        for cp in copies:
            cp.wait()

    hbm = pl.BlockSpec(memory_space=pl.ANY)
    return pl.pallas_call(
        body, name=name, in_specs=[hbm], out_specs=hbm,
        out_shape=jax.ShapeDtypeStruct((N_CHIPS,) + slabs.shape[1:], slabs.dtype),
        scratch_shapes=[pltpu.SemaphoreType.DMA((N_CHIPS,)), pltpu.SemaphoreType.DMA((N_CHIPS,))],
        compiler_params=pltpu.CompilerParams(has_side_effects=True),
    )(slabs)


def _chip_sum(slabs, got, core, name):
    _, rows, lanes = slabs.shape
    tr = _pick(rows, (512, 256, 128, 64, 32, 16, 8))

    def body(core_ref, mine_ref, got_ref, o_ref):
        o_ref[...] = mine_ref[...] + got_ref[...]

    return pl.pallas_call(
        body, name=name,
        grid_spec=pltpu.PrefetchScalarGridSpec(
            num_scalar_prefetch=1, grid=(N_CHIPS, rows // tr),
            in_specs=[pl.BlockSpec((None, tr, lanes), lambda q, i, core_ref: (2 * q + core_ref[0], i, 0)),
                      pl.BlockSpec((None, tr, lanes), lambda q, i, core_ref: (q, i, 0))],
            out_specs=pl.BlockSpec((None, tr, lanes), lambda q, i, core_ref: (q, i, 0))),
        out_shape=jax.ShapeDtypeStruct((N_CHIPS, rows, lanes), slabs.dtype),
        compiler_params=_params("parallel", "parallel"),
    )(core, slabs, got)


def _chip_exchange(chip_sums, shared, name):
    def body(sum_ref, sh_ref, got_ref, gsh_ref, send_sems, recv_sems, sh_send_sems, sh_recv_sems, local_sems):
        me = _my_index()
        my_chip = me >> 1
        local = [pltpu.make_async_copy(sum_ref.at[my_chip], got_ref.at[my_chip], local_sems.at[0]),
                 pltpu.make_async_copy(sh_ref, gsh_ref.at[me], local_sems.at[1])]
        for cp in local:
            cp.start()
        sends = []
        for q in range(1, N_CHIPS):
            peer, pidx = _peer(2 * q)
            cp = pltpu.make_async_remote_copy(
                src_ref=sum_ref.at[pidx >> 1], dst_ref=got_ref.at[my_chip], send_sem=send_sems.at[q - 1],
                recv_sem=recv_sems.at[q - 1], device_id=peer, device_id_type=MESH)
            cp.start()
            sends.append(cp)
        for k in range(1, N_DEV):
            peer, _ = _peer(k)
            cp = pltpu.make_async_remote_copy(
                src_ref=sh_ref, dst_ref=gsh_ref.at[me], send_sem=sh_send_sems.at[k - 1],
                recv_sem=sh_recv_sems.at[k - 1], device_id=peer, device_id_type=MESH)
            cp.start()
            sends.append(cp)
        for q in range(1, N_CHIPS):
            peer, pidx = _peer(2 * q)
            pltpu.make_async_remote_copy(
                src_ref=sum_ref.at[my_chip], dst_ref=got_ref.at[pidx >> 1], send_sem=send_sems.at[q - 1],
                recv_sem=recv_sems.at[q - 1], device_id=peer, device_id_type=MESH).wait_recv()
        for k in range(1, N_DEV):
            peer, pidx = _peer(k)
            pltpu.make_async_remote_copy(
                src_ref=sh_ref, dst_ref=gsh_ref.at[pidx], send_sem=sh_send_sems.at[k - 1],
                recv_sem=sh_recv_sems.at[k - 1], device_id=peer, device_id_type=MESH).wait_recv()
        for cp in sends:
            cp.wait_send()
        for cp in local:
            cp.wait()

    hbm = pl.BlockSpec(memory_space=pl.ANY)
    return pl.pallas_call(
        body, name=name, in_specs=[hbm, hbm], out_specs=[hbm, hbm],
        out_shape=[jax.ShapeDtypeStruct(chip_sums.shape, chip_sums.dtype),
                   jax.ShapeDtypeStruct((N_DEV,) + shared.shape, shared.dtype)],
        scratch_shapes=[pltpu.SemaphoreType.DMA((N_CHIPS - 1,)), pltpu.SemaphoreType.DMA((N_CHIPS - 1,)),
                        pltpu.SemaphoreType.DMA((N_DEV - 1,)), pltpu.SemaphoreType.DMA((N_DEV - 1,)),
                        pltpu.SemaphoreType.DMA((2,))],
        compiler_params=pltpu.CompilerParams(has_side_effects=True),
    )(chip_sums, shared)


def _adamw(parts, w, m, v, name):
    n_parts, rows, lanes = parts.shape
    tr = _pick(rows, (512, 256, 128, 64, 32, 16, 8))
    c1 = 1.0 - ADAM_B1 ** ADAM_STEP
    c2 = 1.0 - ADAM_B2 ** ADAM_STEP

    def body(p_ref, w_ref, m_ref, v_ref, g_ref, d_ref, nm_ref, nv_ref):
        g = p_ref[0]
        for j in range(1, n_parts):
            g = g + p_ref[j]
        nm = ADAM_B1 * m_ref[...] + (1.0 - ADAM_B1) * g
        nv = ADAM_B2 * v_ref[...] + (1.0 - ADAM_B2) * (g * g)
        g_ref[...] = g
        nm_ref[...] = nm
        nv_ref[...] = nv
        d_ref[...] = -ADAM_LR * ((nm / c1) / (jnp.sqrt(nv / c2) + ADAM_EPS) + ADAM_WD * w_ref[...])

    row = pl.BlockSpec((tr, lanes), lambda i: (i, 0))
    return pl.pallas_call(
        body, name=name, grid=(rows // tr,),
        in_specs=[pl.BlockSpec((n_parts, tr, lanes), lambda i: (0, i, 0)), row, row, row],
        out_specs=[row] * 4, out_shape=[jax.ShapeDtypeStruct((rows, lanes), F32)] * 4,
        compiler_params=_params("parallel"),
    )(parts, w, m, v)


MATRIX_SHARDS = (
    ("w_in", (D_MODEL, IN_PROJ_DIM // N_DEV), True),
    ("w_ssm_out", (SSM_D_INNER // N_DEV, D_MODEL), False),
    ("w_att_out", (ATT_OUT_DIM, D_MODEL // N_DEV), True),
    ("w_mix_out", (D_MODEL // N_DEV, D_MODEL), False),
    ("w_ffn_gate", (D_MODEL, D_FF // N_DEV), True),
    ("w_ffn_up", (D_MODEL, D_FF // N_DEV), True),
    ("w_ffn_down", (D_FF // N_DEV, D_MODEL), False),
)
CONV_SHARD = ("conv_w", (SSM_CONV, SSM_CONV_DIM // N_DEV), True)
SHARDED = MATRIX_SHARDS + (CONV_SHARD,)
REPLICATED = (("norm_mix", D_MODEL), ("b_gate", 2 * D_MODEL), ("conv_b", SSM_CONV_DIM), ("dt_bias", SSM_N_HEADS),
              ("a_log", SSM_N_HEADS), ("d_skip", SSM_N_HEADS), ("ssm_norm", SSM_D_INNER), ("norm_ffn", D_MODEL),
              ("norm_final", D_MODEL))


PACK_ROWS = 512


def _round_up(n, mult):
    return -(-n // mult) * mult


def _pack_rows(flat, row_mult):
    rows = _round_up(-(-flat.shape[0] // LANES), row_mult)
    return jnp.pad(flat, (0, rows * LANES - flat.shape[0])).reshape(rows, LANES)


def _pack_sharded(vals, specs, row_mult, dtype):
    return _pack_rows(jnp.concatenate([vals[name].reshape(-1).astype(dtype) for name, _, _ in specs]), row_mult)


def _unpack_sharded(packed, specs, lead=()):
    flat = packed.reshape(lead + (-1,))
    out, off = {}, 0
    for name, shape, _ in specs:
        size = shape[0] * shape[1]
        out[name] = flat[..., off:off + size].reshape(lead + shape)
        off += size
    return out


def _stacking(specs):
    return tuple((name, (shape[1], shape[0]) if by_cols else shape, by_cols) for name, shape, by_cols in specs)


def _to_stacking(vals, specs):
    return {name: (vals[name].T if by_cols else vals[name]) for name, _, by_cols in specs}


def _pack_replicated(vals):
    rows = []
    for name, size in REPLICATED:
        v = vals[name].reshape(-1).astype(F32)
        rows.append(jnp.pad(v, (0, _round_up(size, LANES) - size)))
    return _pack_rows(jnp.concatenate(rows), 8)


def _unpack_replicated(packed, shapes):
    flat = packed.reshape(-1)
    out, off = {}, 0
    for name, size in REPLICATED:
        out[name] = flat[off:off + size].reshape(shapes[name])
        off += _round_up(size, LANES)
    return out


def _lane_row(v):
    v = v.reshape(-1).astype(F32)
    return jnp.pad(v, (0, LANES - v.shape[0])).reshape(1, LANES)


IN_SPLIT = (("z", SSM_D_INNER), ("xbc", SSM_CONV_DIM), ("dt", SSM_N_HEADS), ("qkv", ATT_QKV_DIM), ("gate", 2 * D_MODEL))


def _split_w_in(w_t):
    out, off = {}, 0
    for name, size in IN_SPLIT:
        out[name] = w_t[off:off + size]
        off += size
    out["dt"] = jnp.pad(out["dt"], ((0, DT_PAD - SSM_N_HEADS), (0, 0)))
    return out


def _join_w_in(parts):
    parts = dict(parts)
    parts["dt"] = parts["dt"][:SSM_N_HEADS]
    return jnp.concatenate([parts[name] for name, _ in IN_SPLIT], axis=0)


def kernel(x, norm_mix, w_in, b_gate, conv_w, conv_b, dt_bias, a_log, d_skip, ssm_norm, w_ssm_out, w_att_out, w_mix_out, norm_ffn, w_ffn_gate, w_ffn_up, w_ffn_down, norm_final, loss_target, m_norm_mix, m_w_in, m_b_gate, m_conv_w, m_conv_b, m_dt_bias, m_a_log, m_d_skip, m_ssm_norm, m_w_ssm_out, m_w_att_out, m_w_mix_out, m_norm_ffn, m_w_ffn_gate, m_w_ffn_up, m_w_ffn_down, m_norm_final, v_norm_mix, v_w_in, v_b_gate, v_conv_w, v_conv_b, v_dt_bias, v_a_log, v_d_skip, v_ssm_norm, v_w_ssm_out, v_w_att_out, v_w_mix_out, v_norm_ffn, v_w_ffn_gate, v_w_ffn_up, v_w_ffn_down, v_norm_final):
    given = dict(locals())
    weights = {name: given[name][0] for name, _, _ in SHARDED}
    b, s, d = x.shape
    t = b * s

    mat_specs, conv_specs, all_specs = _stacking(MATRIX_SHARDS), _stacking((CONV_SHARD,)), _stacking(SHARDED)
    stacking = _to_stacking(weights, SHARDED)
    mat_local = _pack_sharded(stacking, mat_specs, 16, BF16)
    conv_local = _pack_sharded(stacking, conv_specs, 8, F32)
    mat_all, conv_all = _all_gather([mat_local, conv_local], "weights_all_gather")
    shards = _unpack_sharded(mat_all, mat_specs, (N_DEV,))
    shards.update(_unpack_sharded(conv_all, conv_specs, (N_DEV,)))
    full = {name: shards[name].reshape(N_DEV * shape[0], shape[1]) for name, shape, _ in all_specs}
    w_sec = _split_w_in(full["w_in"])
    conv_taps = full["conv_w"].T

    g_mix, g_ffn, g_fin = norm_mix.reshape(1, d), norm_ffn.reshape(1, d), norm_final.reshape(1, d)
    bg_row = b_gate.reshape(1, 2 * d)
    convb_row = conv_b.reshape(1, SSM_CONV_DIM)
    ssmn_row = ssm_norm.reshape(1, SSM_D_INNER)
    dtb_row, alog_row, dsk_row = _lane_row(dt_bias), _lane_row(a_log), _lane_row(d_skip)
    cosf, sinf = _rope_tables(s)

    x2d = x.reshape(t, d)
    h1 = _rmsnorm_fwd(x2d, g_mix, "norm_mix_fwd")
    proj = {name: _mm(h1, w_sec[name], mode="nt", name="in_proj_" + name) for name, _ in IN_SPLIT}
    xbc3 = proj["xbc"].reshape(b, s, SSM_CONV_DIM)
    xc = _conv_fwd(xbc3, conv_taps, convb_row, "conv_fwd")
    dtr3 = proj["dt"].reshape(b, s, DT_PAD)
    y_ssd, h_states = _ssd_fwd(xc, dtr3, dtb_row, alog_row, dsk_row, "ssd_fwd")
    y_ssd2 = y_ssd.reshape(t, SSM_D_INNER)
    ynorm = _gate_norm_fwd(y_ssd2, proj["z"], ssmn_row, "ssd_gate_norm_fwd")
    y_ssm = _mm(ynorm, full["w_ssm_out"], mode="nn", name="ssm_out_proj")

    qkv3 = proj["qkv"].reshape(b, s, ATT_QKV_DIM)
    qkr = _rope_fwd(qkv3, cosf, sinf, "rope_fwd")
    att_parts = [_att_fwd(qkr, gi, r, "att_fwd_%d" % r) for gi, r in enumerate(ATT_DILATIONS)]
    att, lse = _att_merge([o for o, _ in att_parts], [l_ for _, l_ in att_parts], "att_merge")
    att2 = att.reshape(t, ATT_OUT_DIM)
    y_att = _mm(att2, full["w_att_out"], mode="nt", name="att_out_proj")

    mixed = _mix_fwd(proj["gate"], bg_row, y_ssm, y_att, "mix_fwd")
    x2 = _mm(mixed, full["w_mix_out"], mode="nn", name="mix_out_proj", add=x2d)
    h2 = _rmsnorm_fwd(x2, g_ffn, "norm_ffn_fwd")
    gt = _mm(h2, full["w_ffn_gate"], mode="nt", name="ffn_gate_proj")
    up = _mm(h2, full["w_ffn_up"], mode="nt", name="ffn_up_proj")
    act = _swiglu_fwd(gt, up, "swiglu_fwd")
    x3 = _mm(act, full["w_ffn_down"], mode="nn", name="ffn_down_proj", add=x2)

    loss_row, dx3, dg_fin, dx3b = _loss_head(x3, g_fin, loss_target.reshape(t, d), "loss_head")
    grads = {}
    dact = _mm(dx3b, full["w_ffn_down"], mode="nt", name="ffn_down_dx")
    grads["w_ffn_down"] = _mm(act, dx3b, mode="tn", name="ffn_down_dw")
    dgt, dup = _swiglu_bwd(gt, up, dact, "swiglu_bwd")
    grads["w_ffn_gate"] = _mm(dgt, h2, mode="tn", name="ffn_gate_dw")
    grads["w_ffn_up"] = _mm(dup, h2, mode="tn", name="ffn_up_dw")
    dh2 = _mm(dgt, full["w_ffn_gate"], mode="nn", name="ffn_gate_dx")
    dh2 = _mm(dup, full["w_ffn_up"], mode="nn", name="ffn_up_dx", add=dh2)
    dx2, dg_ffn, dx2b = _rmsnorm_bwd(x2, g_ffn, dh2, dx3, "norm_ffn_bwd", with_bf16=True)

    dmixed = _mm(dx2b, full["w_mix_out"], mode="nt", name="mix_out_dx")
    grads["w_mix_out"] = _mm(mixed, dx2b, mode="tn", name="mix_out_dw")
    dys, dya, dgl, dbg = _mix_bwd(proj["gate"], bg_row, y_ssm, y_att, dmixed, "mix_bwd")

    grads["w_ssm_out"] = _mm(ynorm, dys, mode="tn", name="ssm_out_dw")
    dynorm = _mm(dys, full["w_ssm_out"], mode="nt", name="ssm_out_dx")
    dy_ssd, dz, dssmn = _gate_norm_bwd(y_ssd2, proj["z"], ssmn_row, dynorm, "ssd_gate_norm_bwd")
    dxc, ddtr, dalog, ddsk, ddtb = _ssd_bwd(xc, dtr3, dy_ssd.reshape(b, s, SSM_D_INNER), h_states,
                                            dtb_row, alog_row, dsk_row, "ssd_bwd")
    dxbc, dconvw, dconvb = _conv_bwd(xbc3, dxc, conv_taps, convb_row, "conv_bwd")
    grads["conv_w"] = dconvw.T

    grads["w_att_out"] = _mm(dya, att2, mode="tn", name="att_out_dw")
    datt = _mm(dya, full["w_att_out"], mode="nn", name="att_out_dx").reshape(b, s, ATT_OUT_DIM)
    delta = _att_delta(att, datt, "att_delta")
    dqs, dks, dvs = [], [], []
    for gi, r in enumerate(ATT_DILATIONS):
        dqs.append(_att_bwd_q(qkr, datt, lse, delta, gi, r, "att_bwd_q_%d" % r))
        dk_g, dv_g = _att_bwd_kv(qkr, datt, lse, delta, gi, r, "att_bwd_kv_%d" % r)
        dks.append(dk_g)
        dvs.append(dv_g)
    dqkv = _rope_bwd(dqs, dks, dvs, cosf, sinf, "rope_bwd")

    dproj = {"z": dz, "xbc": dxbc.reshape(t, SSM_CONV_DIM), "dt": ddtr.reshape(t, DT_PAD),
             "qkv": dqkv.reshape(t, ATT_QKV_DIM), "gate": dgl}
    grads["w_in"] = _join_w_in({name: _mm(dproj[name], h1, mode="tn", name="in_proj_dw_" + name)
                                for name, _ in IN_SPLIT})
    k_all = sum(dproj[name].shape[1] for name, _ in IN_SPLIT)
    k_pad = _round_up(k_all, 2048) - k_all
    dproj_all = jnp.concatenate([dproj[name] for name, _ in IN_SPLIT] + [jnp.zeros((t, k_pad), BF16)], axis=1)
    w_in_all = jnp.concatenate([w_sec[name] for name, _ in IN_SPLIT] + [jnp.zeros((k_pad, d), BF16)], axis=0)
    dh1 = _mm(dproj_all, w_in_all, mode="nn", name="in_proj_dx")
    grad_x, dg_mix = _rmsnorm_bwd(x2d, g_mix, dh1, dx2, "norm_mix_bwd")

    slabs = jnp.concatenate([grads[name].reshape(N_DEV, -1) for name, _, _ in all_specs], axis=1)
    slab_rows = _round_up(-(-slabs.shape[1] // LANES), PACK_ROWS)
    slabs = jnp.pad(slabs, ((0, 0), (0, slab_rows * LANES - slabs.shape[1]))).reshape(N_DEV, slab_rows, LANES)
    small = {"norm_mix": dg_mix, "b_gate": dbg, "conv_b": dconvb, "dt_bias": ddtb[:, :SSM_N_HEADS],
             "a_log": dalog[:, :SSM_N_HEADS], "d_skip": ddsk[:, :SSM_N_HEADS], "ssm_norm": dssmn,
             "norm_ffn": dg_ffn, "norm_final": dg_fin}
    core = lax.axis_index("c").astype(jnp.int32).reshape(1)
    chip_sums = _chip_sum(slabs, _pair_exchange(slabs, "grad_pair_exchange"), core, "grad_chip_sum")
    got, got_small = _chip_exchange(chip_sums, _pack_replicated(small), "grad_chip_exchange")

    def packed(prefix):
        vals = _to_stacking({name: given[prefix + name][0] for name, _, _ in SHARDED}, SHARDED)
        rep = {name: given[prefix + name] for name, _ in REPLICATED}
        return _pack_sharded(vals, all_specs, PACK_ROWS, F32), _pack_replicated(rep)

    (w_big, w_small), (m_big, m_small), (v_big, v_small) = packed(""), packed("m_"), packed("v_")
    big = _adamw(got, w_big, m_big, v_big, "adamw_sharded")
    sml = _adamw(got_small, w_small, m_small, v_small, "adamw_replicated")

    loss = lax.psum(loss_row[0, 0], ("x", "y", "c"))
    outs = [loss, grad_x.reshape(b, s, d)]
    rep_shapes = {name: given[name].shape for name, _ in REPLICATED}
    order = ["norm_mix", "w_in", "b_gate", "conv_w", "conv_b", "dt_bias", "a_log", "d_skip", "ssm_norm", "w_ssm_out",
             "w_att_out", "w_mix_out", "norm_ffn", "w_ffn_gate", "w_ffn_up", "w_ffn_down", "norm_final"]
    for big_k, sml_k in zip(big, sml):
        sharded = _to_stacking(_unpack_sharded(big_k, all_specs), SHARDED)
        rep = _unpack_replicated(sml_k, rep_shapes)
        for name in order:
            outs.append(sharded[name][None] if name in sharded else rep[name])
    return tuple(outs)
```

```python
import functools
import math

import jax
import jax.numpy as jnp
from jax import lax
from jax.experimental import pallas as pl
from jax.experimental.pallas import tpu as pltpu

F32 = jnp.float32
BF16 = jnp.bfloat16

N_DEV = 8
N_CHIPS = 4
D_MODEL = 1024
SSM_D_INNER = 2048
SSM_HEAD_DIM = 64
HEAD_DIM_LOG2 = 6
SSM_N_HEADS = 32
SSM_N_GROUPS = 4
SSM_HEADS_PER_GROUP = SSM_N_HEADS // SSM_N_GROUPS
SSM_D_STATE = 128
SSM_CONV = 4
SSM_CHUNK = 128
SSM_CONV_DIM = 3072
ATT_HEAD_DIM = 128
ATT_HEADS_PER_GROUP = 4
ATT_DILATIONS = (1, 4, 16)
ATT_N_HEADS = 12
ATT_QKV_DIM = 4608
ATT_OUT_DIM = 512
ATT_BLOCK = 128
ROPE_THETA = 10000.0
D_FF = 2816
IN_PROJ_DIM = 11808
EPS = 1e-6
LANES = 128
DT_PAD = LANES

ADAM_LR = 0.001
ADAM_B1 = 0.9
ADAM_B2 = 0.999
ADAM_EPS = 1e-08
ADAM_WD = 0.01
ADAM_STEP = 10

VMEM_LIMIT = 56 * 1024 * 1024
MESH = pl.DeviceIdType.MESH
NEG_INF = float("-inf")


def _pick(n, candidates):
    for c in candidates:
        if n % c == 0:
            return c
    return n


def _params(*sem):
    return pltpu.CompilerParams(dimension_semantics=sem, vmem_limit_bytes=VMEM_LIMIT)


def _sigmoid(x):
    return 1.0 / (1.0 + jnp.exp(-x))


def _softplus(x):
    return jnp.maximum(x, 0.0) + jnp.log(1.0 + jnp.exp(-jnp.abs(x)))


def _dot(a, b, dims):
    return lax.dot_general(a.astype(BF16), b.astype(BF16), (dims, ((), ())), preferred_element_type=F32)


def _nn(a, b):
    return _dot(a, b, ((1,), (0,)))


def _nt(a, b):
    return _dot(a, b, ((1,), (1,)))


def _tn(a, b):
    return _dot(a, b, ((0,), (0,)))


def _split3(v):
    hi = v.astype(BF16)
    r1 = v - hi.astype(F32)
    mid = r1.astype(BF16)
    lo = (r1 - mid.astype(F32)).astype(BF16)
    return hi, mid, lo


def _mask_nn(mask, v):
    mb = mask.astype(BF16)
    hi, mid, lo = _split3(v)
    return _nn(mb, hi) + (_nn(mb, mid) + _nn(mb, lo))


def _nn_mask(v, mask):
    mb = mask.astype(BF16)
    hi, mid, lo = _split3(v)
    return _nn(hi, mb) + (_nn(mid, mb) + _nn(lo, mb))


MM_VMEM_BUDGET = 40 * 1024 * 1024
MM_FULL_K = 2816


def _mm_tiles(m, n, k, a_bytes, b_bytes, o_bytes, has_add):
    tk = k if k <= MM_FULL_K else _pick(k, (2048, 1024, 512, 256, 128))
    tn = 1408 if (n > 1024 and n % 1408 == 0) else _pick(n, (1024, 768, 512, 384, 256, 128))
    for tm in (1408, 1024, 768, 512, 384, 256, 128):
        if m % tm:
            continue
        buffers = 2 * (tm * tk * a_bytes + tk * tn * b_bytes + tm * tn * (o_bytes + (4 if has_add else 0)))
        if tk < k:
            buffers += tm * tn * 4
        if buffers <= MM_VMEM_BUDGET:
            return tm, tn, tk
    return _pick(m, (128,)), tn, tk


def _mm(a, b, *, mode, name, out_dtype=F32, add=None):
    if mode == "nn":
        (m, k), n = a.shape, b.shape[1]
    elif mode == "nt":
        (m, k), n = a.shape, b.shape[0]
    else:
        (k, m), n = a.shape, b.shape[1]
    has_add = add is not None
    tm, tn, tk = _mm_tiles(m, n, k, a.dtype.itemsize, b.dtype.itemsize, jnp.dtype(out_dtype).itemsize, has_add)
    nk = k // tk
    dims = {"nn": ((1,), (0,)), "nt": ((1,), (1,)), "tn": ((0,), (0,))}[mode]
    a_spec = {"nn": pl.BlockSpec((tm, tk), lambda i, j, kk: (i, kk)),
              "nt": pl.BlockSpec((tm, tk), lambda i, j, kk: (i, kk)),
              "tn": pl.BlockSpec((tk, tm), lambda i, j, kk: (kk, i))}[mode]
    b_spec = {"nn": pl.BlockSpec((tk, tn), lambda i, j, kk: (kk, j)),
              "nt": pl.BlockSpec((tn, tk), lambda i, j, kk: (j, kk)),
              "tn": pl.BlockSpec((tk, tn), lambda i, j, kk: (kk, j))}[mode]
    o_spec = pl.BlockSpec((tm, tn), lambda i, j, kk: (i, j))

    def finish(r, c_ref, o_ref):
        if has_add:
            r = r + c_ref[...]
        o_ref[...] = r.astype(out_dtype)

    def body_one(*refs):
        a_ref, b_ref = refs[:2]
        finish(_dot(a_ref[...], b_ref[...], dims), refs[2] if has_add else None, refs[-1])

    def body_acc(*refs):
        a_ref, b_ref = refs[:2]
        o_ref, acc = refs[-2:]
        kk = pl.program_id(2)

        @pl.when(kk == 0)
        def _():
            acc[...] = jnp.zeros_like(acc)

        acc[...] += _dot(a_ref[...], b_ref[...], dims)

        @pl.when(kk == nk - 1)
        def _():
            finish(acc[...], refs[2] if has_add else None, o_ref)

    in_specs = [a_spec, b_spec] + ([o_spec] if has_add else [])
    args = (a, b) + ((add,) if has_add else ())
    return pl.pallas_call(
        body_one if nk == 1 else body_acc, name=name, grid=(m // tm, n // tn, nk),
        in_specs=in_specs, out_specs=o_spec,
        out_shape=jax.ShapeDtypeStruct((m, n), out_dtype),
        scratch_shapes=[] if nk == 1 else [pltpu.VMEM((tm, tn), F32)],
        compiler_params=_params("parallel", "parallel", "arbitrary"),
    )(*args)


def _rmsnorm_fwd(x, g, name):
    t, d = x.shape
    tm = _pick(t, (512, 256, 128))

    def body(x_ref, g_ref, o_ref):
        xv = x_ref[...]
        r = lax.rsqrt(jnp.mean(xv * xv, axis=-1, keepdims=True) + EPS)
        o_ref[...] = ((xv * r) * g_ref[...]).astype(BF16)

    return pl.pallas_call(
        body, name=name, grid=(t // tm,),
        in_specs=[pl.BlockSpec((tm, d), lambda i: (i, 0)), pl.BlockSpec((1, d), lambda i: (0, 0))],
        out_specs=pl.BlockSpec((tm, d), lambda i: (i, 0)),
        out_shape=jax.ShapeDtypeStruct((t, d), BF16),
        compiler_params=_params("parallel"),
    )(x, g)


def _rmsnorm_bwd(x, g, dh, dres, name, with_bf16=False):
    t, d = x.shape
    tm = _pick(t, (512, 256, 128))

    def body(x_ref, g_ref, dh_ref, dres_ref, dx_ref, dg_ref, *dxb_ref):
        @pl.when(pl.program_id(0) == 0)
        def _():
            dg_ref[...] = jnp.zeros_like(dg_ref)

        xv = x_ref[...]
        r = lax.rsqrt(jnp.mean(xv * xv, axis=-1, keepdims=True) + EPS)
        xhat = xv * r
        dhv = dh_ref[...]
        dyg = dhv * g_ref[...]
        dx = dres_ref[...] + r * (dyg - xhat * jnp.mean(dyg * xhat, axis=-1, keepdims=True))
        dx_ref[...] = dx
        if with_bf16:
            dxb_ref[0][...] = dx.astype(BF16)
        dg_ref[...] += jnp.sum(dhv * xhat, axis=0, keepdims=True)

    row = pl.BlockSpec((tm, d), lambda i: (i, 0))
    vec = pl.BlockSpec((1, d), lambda i: (0, 0))
    extra = with_bf16 * [jax.ShapeDtypeStruct((t, d), BF16)]
    return pl.pallas_call(
        body, name=name, grid=(t // tm,),
        in_specs=[row, vec, row, row], out_specs=[row, vec] + with_bf16 * [row],
        out_shape=[jax.ShapeDtypeStruct((t, d), F32), jax.ShapeDtypeStruct((1, d), F32)] + extra,
        compiler_params=_params("arbitrary"),
    )(x, g, dh, dres)


def _loss_head(x, g, target, name):
    t, d = x.shape
    tm = _pick(t, (512, 256, 128))

    def body(x_ref, g_ref, t_ref, loss_ref, dx_ref, dg_ref, dxb_ref):
        @pl.when(pl.program_id(0) == 0)
        def _():
            dg_ref[...] = jnp.zeros_like(dg_ref)
            loss_ref[...] = jnp.zeros_like(loss_ref)

        xv = x_ref[...]
        gv = g_ref[...]
        r = lax.rsqrt(jnp.mean(xv * xv, axis=-1, keepdims=True) + EPS)
        xhat = xv * r
        err = xhat * gv - t_ref[...]
        loss_ref[...] += jnp.sum(err * err) * (0.5 / d)
        dy = err * (1.0 / d)
        dyg = dy * gv
        dx = r * (dyg - xhat * jnp.mean(dyg * xhat, axis=-1, keepdims=True))
        dx_ref[...] = dx
        dxb_ref[...] = dx.astype(BF16)
        dg_ref[...] += jnp.sum(dy * xhat, axis=0, keepdims=True)

    row = pl.BlockSpec((tm, d), lambda i: (i, 0))
    vec = pl.BlockSpec((1, d), lambda i: (0, 0))
    return pl.pallas_call(
        body, name=name, grid=(t // tm,),
        in_specs=[row, vec, row],
        out_specs=[pl.BlockSpec((1, LANES), lambda i: (0, 0)), row, vec, row],
        out_shape=[jax.ShapeDtypeStruct((1, LANES), F32), jax.ShapeDtypeStruct((t, d), F32),
                   jax.ShapeDtypeStruct((1, d), F32), jax.ShapeDtypeStruct((t, d), BF16)],
        compiler_params=_params("arbitrary"),
    )(x, g, target)


CONV_HALO = 8


def _conv_fwd(u, w, bias, name):
    b, s, c = u.shape
    ts = _pick(s, (512, 256, 128))
    cb = _pick(c, (512, 384, 256, 128))
    hb = ts // CONV_HALO

    def body(u_ref, h_ref, w_ref, b_ref, o_ref):
        uv = u_ref[...]
        halo = jnp.where(pl.program_id(1) == 0, 0.0, h_ref[...])
        ext = jnp.concatenate([halo, uv], axis=0)
        wv = w_ref[...]
        acc = b_ref[...] + wv[SSM_CONV - 1:SSM_CONV, :] * uv
        for sh in range(1, SSM_CONV):
            kidx = SSM_CONV - 1 - sh
            acc = acc + wv[kidx:kidx + 1, :] * ext[CONV_HALO - sh:CONV_HALO - sh + ts, :]
        o_ref[...] = acc * _sigmoid(acc)

    return pl.pallas_call(
        body, name=name, grid=(b, s // ts, c // cb),
        in_specs=[pl.BlockSpec((None, ts, cb), lambda bi, i, j: (bi, i, j)),
                  pl.BlockSpec((None, CONV_HALO, cb), lambda bi, i, j: (bi, jnp.maximum(i * hb - 1, 0), j)),
                  pl.BlockSpec((SSM_CONV, cb), lambda bi, i, j: (0, j)),
                  pl.BlockSpec((1, cb), lambda bi, i, j: (0, j))],
        out_specs=pl.BlockSpec((None, ts, cb), lambda bi, i, j: (bi, i, j)),
        out_shape=jax.ShapeDtypeStruct((b, s, c), F32),
        compiler_params=_params("parallel", "parallel", "parallel"),
    )(u, u, w, bias)


def _conv_bwd(u, dout, w, bias, name):
    b, s, c = u.shape
    ts = _pick(s, (512, 256, 128))
    cb = _pick(c, (512, 384, 256, 128))
    hb = ts // CONV_HALO
    n_t = s // ts

    def pre_act_grad(ext_u, cur_u, dout_v, wv, bv, rows):
        acc = bv + wv[SSM_CONV - 1:SSM_CONV, :] * cur_u
        for sh in range(1, SSM_CONV):
            kidx = SSM_CONV - 1 - sh
            acc = acc + wv[kidx:kidx + 1, :] * ext_u[CONV_HALO - sh:CONV_HALO - sh + rows, :]
        sg = _sigmoid(acc)
        return dout_v * (sg * (1.0 + acc * (1.0 - sg)))

    def body(u_ref, up_ref, un_ref, d_ref, dn_ref, w_ref, b_ref, du_ref, dw_ref, db_ref):
        i = pl.program_id(2)
        first = jnp.logical_and(pl.program_id(1) == 0, i == 0)

        @pl.when(first)
        def _():
            dw_ref[...] = jnp.zeros_like(dw_ref)
            db_ref[...] = jnp.zeros_like(db_ref)

        wv = w_ref[...]
        bv = b_ref[...]
        uv = u_ref[...]
        u_prev = jnp.where(i == 0, 0.0, up_ref[...])
        ext_u = jnp.concatenate([u_prev, uv], axis=0)
        dpre = pre_act_grad(ext_u, uv, d_ref[...], wv, bv, ts)
        un = un_ref[...]
        ext_n = jnp.concatenate([uv[ts - CONV_HALO:, :], un], axis=0)
        dpre_n = pre_act_grad(ext_n, un, dn_ref[...], wv, bv, CONV_HALO)
        dpre_n = jnp.where(i == n_t - 1, 0.0, dpre_n)
        ext_d = jnp.concatenate([dpre, dpre_n], axis=0)
        du = wv[SSM_CONV - 1:SSM_CONV, :] * dpre
        dw_ref[SSM_CONV - 1:SSM_CONV, :] += jnp.sum(dpre * uv, axis=0, keepdims=True)
        for sh in range(1, SSM_CONV):
            kidx = SSM_CONV - 1 - sh
            du = du + wv[kidx:kidx + 1, :] * ext_d[sh:sh + ts, :]
            dw_ref[kidx:kidx + 1, :] += jnp.sum(dpre * ext_u[CONV_HALO - sh:CONV_HALO - sh + ts, :],
                                                axis=0, keepdims=True)
        du_ref[...] = du.astype(BF16)
        db_ref[...] += jnp.sum(dpre, axis=0, keepdims=True)

    last_hb = s // CONV_HALO - 1
    tile = pl.BlockSpec((None, ts, cb), lambda j, bi, i: (bi, i, j))
    prev = pl.BlockSpec((None, CONV_HALO, cb), lambda j, bi, i: (bi, jnp.maximum(i * hb - 1, 0), j))
    nxt = pl.BlockSpec((None, CONV_HALO, cb), lambda j, bi, i: (bi, jnp.minimum((i + 1) * hb, last_hb), j))
    return pl.pallas_call(
        body, name=name, grid=(c // cb, b, n_t),
        in_specs=[tile, prev, nxt, tile, nxt,
                  pl.BlockSpec((SSM_CONV, cb), lambda j, bi, i: (0, j)),
                  pl.BlockSpec((1, cb), lambda j, bi, i: (0, j))],
        out_specs=[tile, pl.BlockSpec((SSM_CONV, cb), lambda j, bi, i: (0, j)),
                   pl.BlockSpec((1, cb), lambda j, bi, i: (0, j))],
        out_shape=[jax.ShapeDtypeStruct((b, s, c), BF16), jax.ShapeDtypeStruct((SSM_CONV, c), F32),
                   jax.ShapeDtypeStruct((1, c), F32)],
        compiler_params=_params("parallel", "arbitrary", "arbitrary"),
    )(u, u, u, dout, dout, w, bias)


def _ssd_chunk_terms(dtr_ref, bias_ref, alog_ref):
    q = SSM_CHUNK
    dt = _softplus(dtr_ref[...] + bias_ref[...])
    a_neg = -jnp.exp(alog_ref[...])
    row = lax.broadcasted_iota(jnp.int32, (q, q), 0)
    col = lax.broadcasted_iota(jnp.int32, (q, q), 1)
    lower = row >= col
    s = _mask_nn(lower, dt * a_neg)
    return dt, a_neg, s, s.T, lower


def _ssd_fwd(xc, dtr, dt_bias, a_log, d_skip, name):
    b, s, _ = xc.shape
    q = SSM_CHUNK
    nc = s // q
    p, n = SSM_HEAD_DIM, SSM_D_STATE

    def body(xc_ref, dtr_ref, bias_ref, alog_ref, dsk_ref, y_ref, hs_ref, h_scr):
        @pl.when(pl.program_id(1) == 0)
        def _():
            h_scr[...] = jnp.zeros_like(h_scr)

        dt, _, s_col, s_row, lower = _ssd_chunk_terms(dtr_ref, bias_ref, alog_ref)
        tot = s_col[q - 1:q, :]
        dec = jnp.exp(tot - s_col)
        es = jnp.exp(s_col)
        etot = jnp.exp(tot)
        dsk = dsk_ref[...]
        for g in range(SSM_N_GROUPS):
            bg = xc_ref[:, SSM_D_INNER + n * g:SSM_D_INNER + n * (g + 1)].astype(BF16)
            cg = xc_ref[:, SSM_D_INNER + n * (SSM_N_GROUPS + g):SSM_D_INNER + n * (SSM_N_GROUPS + g + 1)].astype(BF16)
            gm = _nt(cg, bg)
            for j in range(SSM_HEADS_PER_GROUP):
                h = g * SSM_HEADS_PER_GROUP + j
                xh = xc_ref[:, p * h:p * (h + 1)]
                xdt = xh * dt[:, h:h + 1]
                lm = jnp.exp(jnp.where(lower, s_col[:, h:h + 1] - s_row[h:h + 1, :], NEG_INF))
                y_diag = _nn(gm * lm, xdt)
                hh = h_scr[h]
                hs_ref[h] = hh
                y_off = es[:, h:h + 1] * _nt(cg, hh)
                y_ref[:, p * h:p * (h + 1)] = y_diag + y_off + dsk[:, h:h + 1] * xh
                h_scr[h] = etot[:, h:h + 1] * hh + _tn(xdt * dec[:, h:h + 1], bg)

    vec = pl.BlockSpec((1, LANES), lambda bi, c: (0, 0))
    return pl.pallas_call(
        body, name=name, grid=(b, nc),
        in_specs=[pl.BlockSpec((None, q, SSM_CONV_DIM), lambda bi, c: (bi, c, 0)),
                  pl.BlockSpec((None, q, LANES), lambda bi, c: (bi, c, 0)), vec, vec, vec],
        out_specs=[pl.BlockSpec((None, q, SSM_D_INNER), lambda bi, c: (bi, c, 0)),
                   pl.BlockSpec((None, None, SSM_N_HEADS, p, n), lambda bi, c: (bi, c, 0, 0, 0))],
        out_shape=[jax.ShapeDtypeStruct((b, s, SSM_D_INNER), F32),
                   jax.ShapeDtypeStruct((b, nc, SSM_N_HEADS, p, n), F32)],
        scratch_shapes=[pltpu.VMEM((SSM_N_HEADS, p, n), F32)],
        compiler_params=_params("parallel", "arbitrary"),
    )(xc, dtr, dt_bias, a_log, d_skip)


def _ssd_bwd(xc, dtr, dy, hs, dt_bias, a_log, d_skip, name):
    b, s, _ = xc.shape
    q = SSM_CHUNK
    nc = s // q
    p, n = SSM_HEAD_DIM, SSM_D_STATE

    def body(xc_ref, dtr_ref, dy_ref, hs_ref, bias_ref, alog_ref, dsk_ref,
             dxc_ref, ddtr_ref, dalog_ref, ddsk_ref, dbias_ref, dh_scr, p_ds, p_dt, p_dd, p_tot):
        ci = pl.program_id(1)

        @pl.when(ci == 0)
        def _():
            dh_scr[...] = jnp.zeros_like(dh_scr)

        @pl.when(jnp.logical_and(pl.program_id(0) == 0, ci == 0))
        def _():
            dalog_ref[...] = jnp.zeros_like(dalog_ref)
            ddsk_ref[...] = jnp.zeros_like(ddsk_ref)
            dbias_ref[...] = jnp.zeros_like(dbias_ref)

        dt, a_neg, s_col, s_row, lower = _ssd_chunk_terms(dtr_ref, bias_ref, alog_ref)
        upper = jnp.logical_not(lower) | (lax.broadcasted_iota(jnp.int32, (q, q), 0)
                                          == lax.broadcasted_iota(jnp.int32, (q, q), 1))
        tot = s_col[q - 1:q, :]
        dec = jnp.exp(tot - s_col)
        es = jnp.exp(s_col)
        etot = jnp.exp(tot)
        dsk = dsk_ref[...]
        lane = lax.broadcasted_iota(jnp.int32, (1, LANES), 1)
        hdot = jnp.zeros((1, LANES), F32)
        ds_mask = jnp.zeros((q, LANES), F32)
        for g in range(SSM_N_GROUPS):
            b_lo = SSM_D_INNER + n * g
            c_lo = SSM_D_INNER + n * (SSM_N_GROUPS + g)
            bg = xc_ref[:, b_lo:b_lo + n].astype(BF16)
            cg = xc_ref[:, c_lo:c_lo + n].astype(BF16)
            gm = _nt(cg, bg)
            gmt = _nt(bg, cg)
            dg = jnp.zeros((q, q), F32)
            dgt = jnp.zeros((q, q), F32)
            dcg = jnp.zeros((q, n), F32)
            dbg = jnp.zeros((q, n), F32)
            for j in range(SSM_HEADS_PER_GROUP):
                h = g * SSM_HEADS_PER_GROUP + j
                sl = slice(p * h, p * (h + 1))
                xh = xc_ref[:, sl]
                dth = dt[:, h:h + 1]
                xdt = xh * dth
                dyh = dy_ref[:, sl]
                lm = jnp.exp(jnp.where(lower, s_col[:, h:h + 1] - s_row[h:h + 1, :], NEG_INF))
                lmt = jnp.exp(jnp.where(upper, s_row[h:h + 1, :] - s_col[:, h:h + 1], NEG_INF))
                dm = _nt(dyh, xdt)
                dmt = _nt(xdt, dyh)
                dg = dg + dm * lm
                dgt = dgt + dmt * lmt
                mt = gmt * lmt
                ds_h = (jnp.sum(dm * (gm * lm), axis=1, keepdims=True)
                        - jnp.sum(dmt * mt, axis=1, keepdims=True))
                ds_mask = jnp.where(lane == h, ds_h, ds_mask)
                dx_diag = _nn(mt, dyh)
                hh = hs_ref[h]
                dhn = dh_scr[h]
                dw = es[:, h:h + 1] * dyh
                dcg = dcg + _nn(dw, hh)
                dech = dec[:, h:h + 1]
                dx_state = dech * _nt(bg, dhn)
                dbg = dbg + _nn(xdt * dech, dhn)
                dxdt = dx_diag + dx_state
                dskh = dsk[:, h:h + 1]
                dxc_ref[:, sl] = dxdt * dth + dskh * dyh
                p_ds[:, sl] = dw * _nt(cg, hh) - xdt * dx_state
                p_dt[:, sl] = dxdt * xh
                p_dd[:, sl] = dyh * xh
                p_tot[:, sl] = xdt * dx_state
                hdot = jnp.where(lane == h, jnp.sum(dhn * hh), hdot)
                dh_scr[h] = _tn(dw, cg) + etot[:, h:h + 1] * dhn
            dxc_ref[:, b_lo:b_lo + n] = dbg + _nn(dgt, cg)
            dxc_ref[:, c_lo:c_lo + n] = dcg + _nn(dg, bg)
        ind = ((lax.broadcasted_iota(jnp.int32, (SSM_D_INNER, LANES), 0) >> HEAD_DIM_LOG2)
               == lax.broadcasted_iota(jnp.int32, (SSM_D_INNER, LANES), 1))
        r_ds = _nn_mask(p_ds[...], ind)
        r_dt = _nn_mask(p_dt[...], ind)
        r_dd = _nn_mask(p_dd[...], ind)
        r_tot = _nn_mask(p_tot[...], ind)
        dtot = jnp.sum(r_tot, axis=0, keepdims=True) + etot * hdot
        last = lax.broadcasted_iota(jnp.int32, (q, LANES), 0) == q - 1
        ds = ds_mask + r_ds + jnp.where(last, dtot, 0.0)
        da = _mask_nn(upper, ds)
        ddt = da * a_neg + r_dt
        live = lane < SSM_N_HEADS
        sg = _sigmoid(dtr_ref[...] + bias_ref[...])
        ddtr = jnp.where(live, ddt * sg, 0.0)
        ddtr_ref[...] = ddtr.astype(BF16)
        dalog_ref[...] += jnp.where(live, jnp.sum(da * dt, axis=0, keepdims=True) * a_neg, 0.0)
        ddsk_ref[...] += jnp.where(live, jnp.sum(r_dd, axis=0, keepdims=True), 0.0)
        dbias_ref[...] += jnp.sum(ddtr, axis=0, keepdims=True)

    rev = lambda bi, c: (bi, nc - 1 - c, 0)
    vec = pl.BlockSpec((1, LANES), lambda bi, c: (0, 0))
    wide = pl.BlockSpec((None, q, SSM_D_INNER), rev)
    return pl.pallas_call(
        body, name=name, grid=(b, nc),
        in_specs=[pl.BlockSpec((None, q, SSM_CONV_DIM), rev), pl.BlockSpec((None, q, LANES), rev), wide,
                  pl.BlockSpec((None, None, SSM_N_HEADS, p, n), lambda bi, c: (bi, nc - 1 - c, 0, 0, 0)),
                  vec, vec, vec],
        out_specs=[pl.BlockSpec((None, q, SSM_CONV_DIM), rev), pl.BlockSpec((None, q, LANES), rev), vec, vec, vec],
        out_shape=[jax.ShapeDtypeStruct((b, s, SSM_CONV_DIM), F32), jax.ShapeDtypeStruct((b, s, LANES), BF16),
                   jax.ShapeDtypeStruct((1, LANES), F32), jax.ShapeDtypeStruct((1, LANES), F32),
                   jax.ShapeDtypeStruct((1, LANES), F32)],
        scratch_shapes=[pltpu.VMEM((SSM_N_HEADS, p, n), F32)] + [pltpu.VMEM((q, SSM_D_INNER), F32)] * 4,
        compiler_params=_params("arbitrary", "arbitrary"),
    )(xc, dtr, dy, hs, dt_bias, a_log, d_skip)


SSM_GROUP_WIDTH = SSM_D_INNER // SSM_N_GROUPS


def _gate_norm_fwd(y, z, w, name):
    t, d = y.shape
    tm = _pick(t, (256, 128))

    def body(y_ref, z_ref, w_ref, o_ref):
        for g in range(SSM_N_GROUPS):
            sl = slice(SSM_GROUP_WIDTH * g, SSM_GROUP_WIDTH * (g + 1))
            zv = z_ref[:, sl]
            u = y_ref[:, sl] * (zv * _sigmoid(zv))
            r = lax.rsqrt(jnp.mean(u * u, axis=-1, keepdims=True) + EPS)
            o_ref[:, sl] = ((u * r) * w_ref[:, sl]).astype(BF16)

    row = pl.BlockSpec((tm, d), lambda i: (i, 0))
    return pl.pallas_call(
        body, name=name, grid=(t // tm,),
        in_specs=[row, row, pl.BlockSpec((1, d), lambda i: (0, 0))], out_specs=row,
        out_shape=jax.ShapeDtypeStruct((t, d), BF16),
        compiler_params=_params("parallel"),
    )(y, z, w)


def _gate_norm_bwd(y, z, w, dout, name):
    t, d = y.shape
    tm = _pick(t, (256, 128))

    def body(y_ref, z_ref, w_ref, do_ref, dy_ref, dz_ref, dw_ref):
        @pl.when(pl.program_id(0) == 0)
        def _():
            dw_ref[...] = jnp.zeros_like(dw_ref)

        for g in range(SSM_N_GROUPS):
            sl = slice(SSM_GROUP_WIDTH * g, SSM_GROUP_WIDTH * (g + 1))
            zv = z_ref[:, sl]
            yv = y_ref[:, sl]
            sg = _sigmoid(zv)
            silu = zv * sg
            u = yv * silu
            r = lax.rsqrt(jnp.mean(u * u, axis=-1, keepdims=True) + EPS)
            uh = u * r
            dov = do_ref[:, sl]
            dw_ref[:, sl] += jnp.sum(dov * uh, axis=0, keepdims=True)
            dyg = dov * w_ref[:, sl]
            du = r * (dyg - uh * jnp.mean(dyg * uh, axis=-1, keepdims=True))
            dy_ref[:, sl] = du * silu
            dz_ref[:, sl] = (du * yv * (sg * (1.0 + zv * (1.0 - sg)))).astype(BF16)

    row = pl.BlockSpec((tm, d), lambda i: (i, 0))
    vec = pl.BlockSpec((1, d), lambda i: (0, 0))
    return pl.pallas_call(
        body, name=name, grid=(t // tm,),
        in_specs=[row, row, vec, row], out_specs=[row, row, vec],
        out_shape=[jax.ShapeDtypeStruct((t, d), F32), jax.ShapeDtypeStruct((t, d), BF16),
                   jax.ShapeDtypeStruct((1, d), F32)],
        compiler_params=_params("arbitrary"),
    )(y, z, w, dout)


def _rope_tables(s):
    half = ATT_HEAD_DIM // 2
    inv = ROPE_THETA ** (-jnp.arange(half, dtype=F32) / half)
    ang = jnp.arange(s).astype(F32)[:, None] * inv[None, :]
    cos, sin = jnp.cos(ang), jnp.sin(ang)
    return jnp.concatenate([cos, cos], axis=-1), jnp.concatenate([-sin, sin], axis=-1)


ATT_TILE = 256


def _by_residue_spec(r, width):
    return pl.BlockSpec((None, r, ATT_TILE // r, width), lambda bi, i: (bi, 0, i, 0))


def _to_residues(tile, stage, r, store):
    if r == 1:
        store(0, tile)
        return
    stage[...] = tile
    for ri in range(r):
        store(ri, stage[pl.ds(ri, ATT_TILE // r, stride=r), :])


def _from_residues(load, stage, r):
    if r == 1:
        return load(0)
    for ri in range(r):
        stage[pl.ds(ri, ATT_TILE // r, stride=r), :] = load(ri)
    return stage[...]


def _rope_fwd(qkv, cosf, sinf, name):
    b, s, w = qkv.shape
    ts, d, gw = ATT_TILE, ATT_HEAD_DIM, ATT_OUT_DIM

    def body(x_ref, c_ref, s_ref, *rest):
        outs, stage = rest[:-1], rest[-1]
        cv, sv = c_ref[...], s_ref[...]
        for kind in range(3):
            for gi, r in enumerate(ATT_DILATIONS):
                for j in range(ATT_HEADS_PER_GROUP):
                    src = d * (kind * ATT_N_HEADS + gi * ATT_HEADS_PER_GROUP + j)
                    dst = slice(kind * gw + d * j, kind * gw + d * (j + 1))
                    tv = x_ref[:, src:src + d]
                    if kind < 2:
                        tv = tv * cv + pltpu.roll(tv, d // 2, 1) * sv

                    def store(ri, rows, o_ref=outs[gi], dst=dst):
                        o_ref[ri, :, dst] = rows.astype(BF16)

                    _to_residues(tv, stage, r, store)

    tab = pl.BlockSpec((ts, d), lambda bi, i: (i, 0))
    return pl.pallas_call(
        body, name=name, grid=(b, s // ts),
        in_specs=[pl.BlockSpec((None, ts, w), lambda bi, i: (bi, i, 0)), tab, tab],
        out_specs=[_by_residue_spec(r, 3 * gw) for r in ATT_DILATIONS],
        out_shape=[jax.ShapeDtypeStruct((b, r, s // r, 3 * gw), BF16) for r in ATT_DILATIONS],
        scratch_shapes=[pltpu.VMEM((ts, d), F32)],
        compiler_params=_params("parallel", "parallel"),
    )(qkv, cosf, sinf)


def _rope_bwd(dq, dk, dv, cosf, sinf, name):
    n_pat = len(ATT_DILATIONS)
    b, _, s, gw = dq[0].shape
    ts, d = ATT_TILE, ATT_HEAD_DIM

    def body(*refs):
        ins, (c_ref, s_ref, o_ref, stage) = refs[:3 * n_pat], refs[3 * n_pat:]
        cv, sv = c_ref[...], s_ref[...]
        for kind in range(3):
            for gi, r in enumerate(ATT_DILATIONS):
                src = ins[kind * n_pat + gi]
                for j in range(ATT_HEADS_PER_GROUP):
                    tv = _from_residues(lambda ri, src=src, j=j: src[ri, :, d * j:d * (j + 1)], stage, r)
                    if kind < 2:
                        tv = tv * cv + pltpu.roll(tv * sv, d // 2, 1)
                    lo = d * (kind * ATT_N_HEADS + gi * ATT_HEADS_PER_GROUP + j)
                    o_ref[:, lo:lo + d] = tv.astype(BF16)

    tab = pl.BlockSpec((ts, d), lambda bi, i: (i, 0))
    parts = [_by_residue_spec(r, gw) for r in ATT_DILATIONS]
    return pl.pallas_call(
        body, name=name, grid=(b, s // ts), in_specs=parts * 3 + [tab, tab],
        out_specs=pl.BlockSpec((None, ts, ATT_QKV_DIM), lambda bi, i: (bi, i, 0)),
        out_shape=jax.ShapeDtypeStruct((b, s, ATT_QKV_DIM), BF16),
        scratch_shapes=[pltpu.VMEM((ts, d), F32)],
        compiler_params=_params("parallel", "parallel"),
    )(*dq, *dk, *dv, cosf, sinf)


ATT_SCALE = ATT_HEAD_DIM ** -0.5
ATT_BLK = (None, None, ATT_BLOCK, ATT_OUT_DIM)


def _att_spec(col, shift, n_blocks):
    def index(bi, ri, nb_i):
        return (bi, ri, jnp.clip(nb_i + shift, 0, n_blocks - 1), col)
    return pl.BlockSpec(ATT_BLK, index)


def _band_mask(shape, q_axis, has_prev):
    qi = lax.broadcasted_iota(jnp.int32, shape, q_axis)
    kj = lax.broadcasted_iota(jnp.int32, shape, 1 - q_axis)
    dist = qi + ATT_BLOCK - kj
    return (dist >= 0) & (dist <= ATT_BLOCK) & (has_prev | (kj >= ATT_BLOCK))


def _att_fwd(qkr, name):
    b, r, l, _ = qkr.shape
    nb = l // ATT_BLOCK
    d = ATT_HEAD_DIM

    def body(q_ref, kp_ref, k_ref, vp_ref, v_ref, o_ref, lse_ref):
        mask = _band_mask((ATT_BLOCK, 2 * ATT_BLOCK), 0, pl.program_id(2) > 0)
        for j in range(ATT_HEADS_PER_GROUP):
            sl = slice(d * j, d * (j + 1))
            kcat = jnp.concatenate([kp_ref[:, sl], k_ref[:, sl]], axis=0)
            vcat = jnp.concatenate([vp_ref[:, sl], v_ref[:, sl]], axis=0)
            sc = jnp.where(mask, _nt(q_ref[:, sl], kcat) * ATT_SCALE, NEG_INF)
            m = jnp.max(sc, axis=-1, keepdims=True)
            pr = jnp.exp(sc - m)
            den = jnp.sum(pr, axis=-1, keepdims=True)
            o_ref[:, sl] = _nn(pr / den, vcat)
            lse_ref[:, sl] = jnp.broadcast_to(m + jnp.log(den), (ATT_BLOCK, d))

    out_spec = _att_spec(0, 0, nb)
    return pl.pallas_call(
        body, name=name, grid=(b, r, nb),
        in_specs=[_att_spec(0, 0, nb), _att_spec(1, -1, nb), _att_spec(1, 0, nb),
                  _att_spec(2, -1, nb), _att_spec(2, 0, nb)],
        out_specs=[out_spec, out_spec],
        out_shape=[jax.ShapeDtypeStruct((b, r, l, ATT_OUT_DIM), F32)] * 2,
        compiler_params=_params("parallel", "parallel", "parallel"),
    )(qkr, qkr, qkr, qkr, qkr)


def _att_merge(os_, lses, name):
    n_pat = len(os_)
    b, _, s, gw = os_[0].shape
    ts, d = ATT_TILE, ATT_HEAD_DIM

    def body(*refs):
        o_refs, l_refs = refs[:n_pat], refs[n_pat:2 * n_pat]
        att_ref, lse_outs, stage = refs[2 * n_pat], refs[2 * n_pat + 1:3 * n_pat + 1], refs[-1]
        for j in range(ATT_HEADS_PER_GROUP):
            sl = slice(d * j, d * (j + 1))
            ov = [_from_residues(lambda ri, g=g: o_refs[g][ri, :, sl], stage, r)
                  for g, r in enumerate(ATT_DILATIONS)]
            ls = [_from_residues(lambda ri, g=g: l_refs[g][ri, :, sl], stage, r)
                  for g, r in enumerate(ATT_DILATIONS)]
            m = functools.reduce(jnp.maximum, ls)
            es = [jnp.exp(lv - m) for lv in ls]
            tot = functools.reduce(lambda u, v: u + v, es)
            acc = (es[0] / tot) * ov[0]
            for g in range(1, n_pat):
                acc = acc + (es[g] / tot) * ov[g]
            att_ref[:, sl] = acc
            joint = m + jnp.log(tot)
            for g, r in enumerate(ATT_DILATIONS):
                def store(ri, rows, out=lse_outs[g]):
                    out[ri, :, sl] = rows
                _to_residues(joint, stage, r, store)

    parts = [_by_residue_spec(r, gw) for r in ATT_DILATIONS]
    return pl.pallas_call(
        body, name=name, grid=(b, s // ts), in_specs=parts * 2,
        out_specs=[pl.BlockSpec((None, ts, gw), lambda bi, i: (bi, i, 0))] + parts,
        out_shape=[jax.ShapeDtypeStruct((b, s, gw), F32)]
        + [jax.ShapeDtypeStruct((b, r, s // r, gw), F32) for r in ATT_DILATIONS],
        scratch_shapes=[pltpu.VMEM((ts, d), F32)],
        compiler_params=_params("parallel", "parallel"),
    )(*os_, *lses)


def _att_delta(att, datt, name):
    b, s, gw = att.shape
    ts, d = ATT_TILE, ATT_HEAD_DIM
    n_pat = len(ATT_DILATIONS)

    def body(a_ref, d_ref, *rest):
        do_outs, dl_outs, stage = rest[:n_pat], rest[n_pat:2 * n_pat], rest[-1]
        for j in range(ATT_HEADS_PER_GROUP):
            sl = slice(d * j, d * (j + 1))
            dv = d_ref[:, sl]
            delta = jnp.broadcast_to(jnp.sum(a_ref[:, sl] * dv, axis=-1, keepdims=True), (ts, d))
            for g, r in enumerate(ATT_DILATIONS):
                def store_do(ri, rows, out=do_outs[g]):
                    out[ri, :, sl] = rows.astype(BF16)

                def store_dl(ri, rows, out=dl_outs[g]):
                    out[ri, :, sl] = rows

                _to_residues(dv, stage, r, store_do)
                _to_residues(delta, stage, r, store_dl)

    row = pl.BlockSpec((None, ts, gw), lambda bi, i: (bi, i, 0))
    parts = [_by_residue_spec(r, gw) for r in ATT_DILATIONS]
    outs = pl.pallas_call(
        body, name=name, grid=(b, s // ts), in_specs=[row, row], out_specs=parts * 2,
        out_shape=[jax.ShapeDtypeStruct((b, r, s // r, gw), BF16) for r in ATT_DILATIONS]
        + [jax.ShapeDtypeStruct((b, r, s // r, gw), F32) for r in ATT_DILATIONS],
        scratch_shapes=[pltpu.VMEM((ts, d), F32)],
        compiler_params=_params("parallel", "parallel"),
    )(att, datt)
    return outs[:n_pat], outs[n_pat:]


def _att_bwd_q(qkr, datt, lse, delta, name):
    b, r, l, _ = qkr.shape
    nb = l // ATT_BLOCK
    d = ATT_HEAD_DIM

    def body(q_ref, kp_ref, k_ref, vp_ref, v_ref, do_ref, lse_ref, dl_ref, dq_ref):
        mask = _band_mask((ATT_BLOCK, 2 * ATT_BLOCK), 0, pl.program_id(2) > 0)
        for j in range(ATT_HEADS_PER_GROUP):
            sl = slice(d * j, d * (j + 1))
            kcat = jnp.concatenate([kp_ref[:, sl], k_ref[:, sl]], axis=0)
            vcat = jnp.concatenate([vp_ref[:, sl], v_ref[:, sl]], axis=0)
            sc = _nt(q_ref[:, sl], kcat) * ATT_SCALE
            pr = jnp.exp(jnp.where(mask, sc - lse_ref[:, d * j:d * j + 1], NEG_INF))
            dp = _nt(do_ref[:, sl], vcat)
            dsc = pr * (dp - dl_ref[:, d * j:d * j + 1])
            dq_ref[:, sl] = _nn(dsc, kcat) * ATT_SCALE

    tok = _att_spec(0, 0, nb)
    return pl.pallas_call(
        body, name=name, grid=(b, r, nb),
        in_specs=[_att_spec(0, 0, nb), _att_spec(1, -1, nb), _att_spec(1, 0, nb),
                  _att_spec(2, -1, nb), _att_spec(2, 0, nb), tok, tok, tok],
        out_specs=tok,
        out_shape=jax.ShapeDtypeStruct((b, r, l, ATT_OUT_DIM), F32),
        compiler_params=_params("parallel", "parallel", "parallel"),
    )(qkr, qkr, qkr, qkr, qkr, datt, lse, delta)


def _att_bwd_kv(qkr, datt, lse, delta, name):
    b, r, l, _ = qkr.shape
    nb = l // ATT_BLOCK
    d = ATT_HEAD_DIM

    def body(k_ref, v_ref, q_ref, qn_ref, do_ref, don_ref, lse_ref, lsen_ref, dl_ref, dln_ref, dk_ref, dv_ref):
        shape = (ATT_BLOCK, 2 * ATT_BLOCK)
        kj = lax.broadcasted_iota(jnp.int32, shape, 0)
        qi = lax.broadcasted_iota(jnp.int32, shape, 1)
        dist = qi - kj
        has_next = pl.program_id(2) < nb - 1
        mask = (dist >= 0) & (dist <= ATT_BLOCK) & (has_next | (qi < ATT_BLOCK))
        for j in range(ATT_HEADS_PER_GROUP):
            sl = slice(d * j, d * (j + 1))
            qcat = jnp.concatenate([q_ref[:, sl], qn_ref[:, sl]], axis=0)
            docat = jnp.concatenate([do_ref[:, sl], don_ref[:, sl]], axis=0)
            lse_t = jnp.concatenate([lse_ref[:, sl], lsen_ref[:, sl]], axis=0).T
            dl_t = jnp.concatenate([dl_ref[:, sl], dln_ref[:, sl]], axis=0).T
            sc_t = _nt(k_ref[:, sl], qcat) * ATT_SCALE
            pr_t = jnp.exp(jnp.where(mask, sc_t - lse_t, NEG_INF))
            dv_ref[:, sl] = _nn(pr_t, docat)
            dsc_t = pr_t * (_nt(v_ref[:, sl], docat) - dl_t)
            dk_ref[:, sl] = _nn(dsc_t, qcat) * ATT_SCALE

    tok, tok_n = _att_spec(0, 0, nb), _att_spec(0, 1, nb)
    return pl.pallas_call(
        body, name=name, grid=(b, r, nb),
        in_specs=[_att_spec(1, 0, nb), _att_spec(2, 0, nb), _att_spec(0, 0, nb), _att_spec(0, 1, nb),
                  tok, tok_n, tok, tok_n, tok, tok_n],
        out_specs=[tok, tok],
        out_shape=[jax.ShapeDtypeStruct((b, r, l, ATT_OUT_DIM), F32)] * 2,
        compiler_params=_params("parallel", "parallel", "parallel"),
    )(qkr, qkr, qkr, qkr, datt, datt, lse, lse, delta, delta)


def _mix_fwd(gl, bg, ys, ya, name):
    t, d = ys.shape
    tm = _pick(t, (512, 256, 128))

    def body(gl_ref, bg_ref, ys_ref, ya_ref, o_ref):
        g0 = _sigmoid(gl_ref[:, :d] + bg_ref[:, :d])
        g1 = _sigmoid(gl_ref[:, d:] + bg_ref[:, d:])
        o_ref[...] = (g0 * ys_ref[...] + g1 * ya_ref[...]).astype(BF16)

    row = pl.BlockSpec((tm, d), lambda i: (i, 0))
    return pl.pallas_call(
        body, name=name, grid=(t // tm,),
        in_specs=[pl.BlockSpec((tm, 2 * d), lambda i: (i, 0)), pl.BlockSpec((1, 2 * d), lambda i: (0, 0)), row, row],
        out_specs=row, out_shape=jax.ShapeDtypeStruct((t, d), BF16),
        compiler_params=_params("parallel"),
    )(gl, bg, ys, ya)


def _mix_bwd(gl, bg, ys, ya, dmixed, name):
    t, d = ys.shape
    tm = _pick(t, (512, 256, 128))

    def body(gl_ref, bg_ref, ys_ref, ya_ref, dm_ref, dys_ref, dya_ref, dgl_ref, dbg_ref):
        @pl.when(pl.program_id(0) == 0)
        def _():
            dbg_ref[...] = jnp.zeros_like(dbg_ref)

        dm = dm_ref[...]
        g0 = _sigmoid(gl_ref[:, :d] + bg_ref[:, :d])
        g1 = _sigmoid(gl_ref[:, d:] + bg_ref[:, d:])
        dys_ref[...] = (dm * g0).astype(BF16)
        dya_ref[...] = (dm * g1).astype(BF16)
        d0 = dm * ys_ref[...] * (g0 * (1.0 - g0))
        d1 = dm * ya_ref[...] * (g1 * (1.0 - g1))
        dgl_ref[:, :d] = d0.astype(BF16)
        dgl_ref[:, d:] = d1.astype(BF16)
        dbg_ref[:, :d] += jnp.sum(d0, axis=0, keepdims=True)
        dbg_ref[:, d:] += jnp.sum(d1, axis=0, keepdims=True)

    row = pl.BlockSpec((tm, d), lambda i: (i, 0))
    wide = pl.BlockSpec((tm, 2 * d), lambda i: (i, 0))
    vec = pl.BlockSpec((1, 2 * d), lambda i: (0, 0))
    return pl.pallas_call(
        body, name=name, grid=(t // tm,),
        in_specs=[wide, vec, row, row, row], out_specs=[row, row, wide, vec],
        out_shape=[jax.ShapeDtypeStruct((t, d), BF16), jax.ShapeDtypeStruct((t, d), BF16),
                   jax.ShapeDtypeStruct((t, 2 * d), BF16), jax.ShapeDtypeStruct((1, 2 * d), F32)],
        compiler_params=_params("arbitrary"),
    )(gl, bg, ys, ya, dmixed)


def _swiglu_fwd(gt, up, name):
    t, f = gt.shape
    tm = _pick(t, (512, 256, 128))

    def body(g_ref, u_ref, o_ref):
        gv = g_ref[...]
        o_ref[...] = ((gv * _sigmoid(gv)) * u_ref[...]).astype(BF16)

    row = pl.BlockSpec((tm, f), lambda i: (i, 0))
    return pl.pallas_call(
        body, name=name, grid=(t // tm,), in_specs=[row, row], out_specs=row,
        out_shape=jax.ShapeDtypeStruct((t, f), BF16), compiler_params=_params("parallel"),
    )(gt, up)


def _swiglu_bwd(gt, up, dact, name):
    t, f = gt.shape
    tm = _pick(t, (512, 256, 128))

    def body(g_ref, u_ref, d_ref, dg_ref, du_ref):
        gv = g_ref[...]
        dv = d_ref[...]
        sg = _sigmoid(gv)
        dg_ref[...] = (dv * u_ref[...] * (sg * (1.0 + gv * (1.0 - sg)))).astype(BF16)
        du_ref[...] = (dv * (gv * sg)).astype(BF16)

    row = pl.BlockSpec((tm, f), lambda i: (i, 0))
    return pl.pallas_call(
        body, name=name, grid=(t // tm,), in_specs=[row, row, row], out_specs=[row, row],
        out_shape=[jax.ShapeDtypeStruct((t, f), BF16)] * 2, compiler_params=_params("parallel"),
    )(gt, up, dact)


def _peer(k):
    x, y, c = lax.axis_index("x"), lax.axis_index("y"), lax.axis_index("c")
    px, py, pc = x ^ ((k >> 2) & 1), y ^ ((k >> 1) & 1), c ^ (k & 1)
    return (px, py, pc), 4 * px + 2 * py + pc


def _my_index():
    return 4 * lax.axis_index("x") + 2 * lax.axis_index("y") + lax.axis_index("c")


def _all_gather(parts, name):
    n_parts = len(parts)

    def body(*refs):
        ins, outs = refs[:n_parts], refs[n_parts:2 * n_parts]
        send_sems, recv_sems, local_sems = refs[2 * n_parts:]
        here, me = _peer(0)
        sibling, sib_idx = _peer(1)
        chips = [_peer(2 * q) for q in range(1, N_CHIPS)]

        def copy(i, k, block, to, src=None):
            return pltpu.make_async_remote_copy(
                src_ref=outs[i].at[block] if src is None else src, dst_ref=outs[i].at[block],
                send_sem=send_sems.at[i * (N_DEV - 1) + k], recv_sem=recv_sems.at[i * (N_DEV - 1) + k],
                device_id=to, device_id_type=MESH)

        local = [pltpu.make_async_copy(ins[i], outs[i].at[me], local_sems.at[i]) for i in range(n_parts)]
        for cp in local:
            cp.start()
        sends = []
        for i in range(n_parts):
            sends.append(copy(i, 0, me, sibling, src=ins[i]))
            sends += [copy(i, q, me, chip, src=ins[i]) for q, (chip, _) in enumerate(chips, start=1)]
        for cp in sends:
            cp.start()
        for q, (chip, chip_idx) in enumerate(chips, start=1):
            for i in range(n_parts):
                copy(i, q, chip_idx, here).wait_recv()
                fwd = copy(i, N_CHIPS - 1 + q, chip_idx, sibling)
                fwd.start()
                sends.append(fwd)
        for i in range(n_parts):
            copy(i, 0, sib_idx, here).wait_recv()
        for q, (_, chip_idx) in enumerate(chips, start=1):
            for i in range(n_parts):
                copy(i, N_CHIPS - 1 + q, chip_idx ^ 1, here).wait_recv()
        for cp in sends:
            cp.wait_send()
        for cp in local:
            cp.wait()

    hbm = pl.BlockSpec(memory_space=pl.ANY)
    return pl.pallas_call(
        body, name=name, in_specs=[hbm] * n_parts, out_specs=[hbm] * n_parts,
        out_shape=[jax.ShapeDtypeStruct((N_DEV,) + p_.shape, p_.dtype) for p_ in parts],
        scratch_shapes=[pltpu.SemaphoreType.DMA((n_parts * (N_DEV - 1),)),
                        pltpu.SemaphoreType.DMA((n_parts * (N_DEV - 1),)),
                        pltpu.SemaphoreType.DMA((n_parts,))],
        compiler_params=pltpu.CompilerParams(has_side_effects=True),
    )(*parts)


def _pair_exchange(slabs, name):
    def body(slab_ref, got_ref, send_sems, recv_sems):
        c = lax.axis_index("c")
        sibling, _ = _peer(1)
        copies = [pltpu.make_async_remote_copy(
            src_ref=slab_ref.at[2 * q + 1 - c], dst_ref=got_ref.at[q], send_sem=send_sems.at[q],
            recv_sem=recv_sems.at[q], device_id=sibling, device_id_type=MESH) for q in range(N_CHIPS)]
        for cp in copies:
            cp.start()
        for cp in copies:
            cp.wait()

    hbm = pl.BlockSpec(memory_space=pl.ANY)
    return pl.pallas_call(
        body, name=name, in_specs=[hbm], out_specs=hbm,
        out_shape=jax.ShapeDtypeStruct((N_CHIPS,) + slabs.shape[1:], slabs.dtype),
        scratch_shapes=[pltpu.SemaphoreType.DMA((N_CHIPS,)), pltpu.SemaphoreType.DMA((N_CHIPS,))],
        compiler_params=pltpu.CompilerParams(has_side_effects=True),
    )(slabs)


def _chip_sum(slabs, got, core, name):
    _, rows, lanes = slabs.shape
    tr = _pick(rows, (512, 256, 128, 64, 32, 16, 8))

    def body(core_ref, mine_ref, got_ref, o_ref):
        o_ref[...] = (mine_ref[...] + got_ref[...]).astype(BF16)

    return pl.pallas_call(
        body, name=name,
        grid_spec=pltpu.PrefetchScalarGridSpec(
            num_scalar_prefetch=1, grid=(N_CHIPS, rows // tr),
            in_specs=[pl.BlockSpec((None, tr, lanes), lambda q, i, core_ref: (2 * q + core_ref[0], i, 0)),
                      pl.BlockSpec((None, tr, lanes), lambda q, i, core_ref: (q, i, 0))],
            out_specs=pl.BlockSpec((None, tr, lanes), lambda q, i, core_ref: (q, i, 0))),
        out_shape=jax.ShapeDtypeStruct((N_CHIPS, rows, lanes), BF16),
        compiler_params=_params("parallel", "parallel"),
    )(core, slabs, got)


def _chip_exchange(chip_sums, shared, name):
    def body(sum_ref, sh_ref, got_ref, gsh_ref, send_sems, recv_sems, sh_send_sems, sh_recv_sems, local_sems):
        me = _my_index()
        my_chip = me >> 1
        local = [pltpu.make_async_copy(sum_ref.at[my_chip], got_ref.at[my_chip], local_sems.at[0]),
                 pltpu.make_async_copy(sh_ref, gsh_ref.at[me], local_sems.at[1])]
        for cp in local:
            cp.start()
        sends = []
        for q in range(1, N_CHIPS):
            peer, pidx = _peer(2 * q)
            cp = pltpu.make_async_remote_copy(
                src_ref=sum_ref.at[pidx >> 1], dst_ref=got_ref.at[my_chip], send_sem=send_sems.at[q - 1],
                recv_sem=recv_sems.at[q - 1], device_id=peer, device_id_type=MESH)
            cp.start()
            sends.append(cp)
        for k in range(1, N_DEV):
            peer, _ = _peer(k)
            cp = pltpu.make_async_remote_copy(
                src_ref=sh_ref, dst_ref=gsh_ref.at[me], send_sem=sh_send_sems.at[k - 1],
                recv_sem=sh_recv_sems.at[k - 1], device_id=peer, device_id_type=MESH)
            cp.start()
            sends.append(cp)
        for q in range(1, N_CHIPS):
            peer, pidx = _peer(2 * q)
            pltpu.make_async_remote_copy(
                src_ref=sum_ref.at[my_chip], dst_ref=got_ref.at[pidx >> 1], send_sem=send_sems.at[q - 1],
                recv_sem=recv_sems.at[q - 1], device_id=peer, device_id_type=MESH).wait_recv()
        for k in range(1, N_DEV):
            peer, pidx = _peer(k)
            pltpu.make_async_remote_copy(
                src_ref=sh_ref, dst_ref=gsh_ref.at[pidx], send_sem=sh_send_sems.at[k - 1],
                recv_sem=sh_recv_sems.at[k - 1], device_id=peer, device_id_type=MESH).wait_recv()
        for cp in sends:
            cp.wait_send()
        for cp in local:
            cp.wait()

    hbm = pl.BlockSpec(memory_space=pl.ANY)
    return pl.pallas_call(
        body, name=name, in_specs=[hbm, hbm], out_specs=[hbm, hbm],
        out_shape=[jax.ShapeDtypeStruct(chip_sums.shape, chip_sums.dtype),
                   jax.ShapeDtypeStruct((N_DEV,) + shared.shape, shared.dtype)],
        scratch_shapes=[pltpu.SemaphoreType.DMA((N_CHIPS - 1,)), pltpu.SemaphoreType.DMA((N_CHIPS - 1,)),
                        pltpu.SemaphoreType.DMA((N_DEV - 1,)), pltpu.SemaphoreType.DMA((N_DEV - 1,)),
                        pltpu.SemaphoreType.DMA((2,))],
        compiler_params=pltpu.CompilerParams(has_side_effects=True),
    )(chip_sums, shared)


def _adamw(parts, w, m, v, name):
    n_parts, rows, lanes = parts.shape
    tr = _pick(rows, (512, 256, 128, 64, 32, 16, 8))
    c1 = 1.0 - ADAM_B1 ** ADAM_STEP
    c2 = 1.0 - ADAM_B2 ** ADAM_STEP

    def body(p_ref, w_ref, m_ref, v_ref, g_ref, d_ref, nm_ref, nv_ref):
        g = p_ref[0].astype(F32)
        for j in range(1, n_parts):
            g = g + p_ref[j].astype(F32)
        nm = ADAM_B1 * m_ref[...] + (1.0 - ADAM_B1) * g
        nv = ADAM_B2 * v_ref[...] + (1.0 - ADAM_B2) * (g * g)
        g_ref[...] = g
        nm_ref[...] = nm
        nv_ref[...] = nv
        d_ref[...] = -ADAM_LR * ((nm / c1) / (jnp.sqrt(nv / c2) + ADAM_EPS) + ADAM_WD * w_ref[...])

    row = pl.BlockSpec((tr, lanes), lambda i: (i, 0))
    return pl.pallas_call(
        body, name=name, grid=(rows // tr,),
        in_specs=[pl.BlockSpec((n_parts, tr, lanes), lambda i: (0, i, 0)), row, row, row],
        out_specs=[row] * 4, out_shape=[jax.ShapeDtypeStruct((rows, lanes), F32)] * 4,
        compiler_params=_params("parallel"),
    )(parts, w, m, v)


MATRIX_SHARDS = (
    ("w_in", (D_MODEL, IN_PROJ_DIM // N_DEV), True),
    ("w_ssm_out", (SSM_D_INNER // N_DEV, D_MODEL), False),
    ("w_att_out", (ATT_OUT_DIM, D_MODEL // N_DEV), True),
    ("w_mix_out", (D_MODEL // N_DEV, D_MODEL), False),
    ("w_ffn_gate", (D_MODEL, D_FF // N_DEV), True),
    ("w_ffn_up", (D_MODEL, D_FF // N_DEV), True),
    ("w_ffn_down", (D_FF // N_DEV, D_MODEL), False),
)
CONV_SHARD = ("conv_w", (SSM_CONV, SSM_CONV_DIM // N_DEV), True)
SHARDED = MATRIX_SHARDS + (CONV_SHARD,)
REPLICATED = (("norm_mix", D_MODEL), ("b_gate", 2 * D_MODEL), ("conv_b", SSM_CONV_DIM), ("dt_bias", SSM_N_HEADS),
              ("a_log", SSM_N_HEADS), ("d_skip", SSM_N_HEADS), ("ssm_norm", SSM_D_INNER), ("norm_ffn", D_MODEL),
              ("norm_final", D_MODEL))


PACK_ROWS = 512


def _round_up(n, mult):
    return -(-n // mult) * mult


def _pack_rows(flat, row_mult):
    rows = _round_up(-(-flat.shape[0] // LANES), row_mult)
    return jnp.pad(flat, (0, rows * LANES - flat.shape[0])).reshape(rows, LANES)


def _pack_sharded(vals, specs, row_mult, dtype):
    return _pack_rows(jnp.concatenate([vals[name].reshape(-1).astype(dtype) for name, _, _ in specs]), row_mult)


def _unpack_sharded(packed, specs, lead=()):
    flat = packed.reshape(lead + (-1,))
    out, off = {}, 0
    for name, shape, _ in specs:
        size = shape[0] * shape[1]
        out[name] = flat[..., off:off + size].reshape(lead + shape)
        off += size
    return out


def _stacking(specs):
    return tuple((name, (shape[1], shape[0]) if by_cols else shape, by_cols) for name, shape, by_cols in specs)


def _to_stacking(vals, specs):
    return {name: (vals[name].T if by_cols else vals[name]) for name, _, by_cols in specs}


def _pack_replicated(vals):
    rows = []
    for name, size in REPLICATED:
        v = vals[name].reshape(-1).astype(F32)
        rows.append(jnp.pad(v, (0, _round_up(size, LANES) - size)))
    return _pack_rows(jnp.concatenate(rows), 8)


def _unpack_replicated(packed, shapes):
    flat = packed.reshape(-1)
    out, off = {}, 0
    for name, size in REPLICATED:
        out[name] = flat[off:off + size].reshape(shapes[name])
        off += _round_up(size, LANES)
    return out


def _lane_row(v):
    v = v.reshape(-1).astype(F32)
    return jnp.pad(v, (0, LANES - v.shape[0])).reshape(1, LANES)


IN_SPLIT = (("z", SSM_D_INNER), ("xbc", SSM_CONV_DIM), ("dt", SSM_N_HEADS), ("qkv", ATT_QKV_DIM), ("gate", 2 * D_MODEL))


def _split_w_in(w_t):
    out, off = {}, 0
    for name, size in IN_SPLIT:
        out[name] = w_t[off:off + size]
        off += size
    out["dt"] = jnp.pad(out["dt"], ((0, DT_PAD - SSM_N_HEADS), (0, 0)))
    return out


def _join_w_in(parts):
    parts = dict(parts)
    parts["dt"] = parts["dt"][:SSM_N_HEADS]
    return jnp.concatenate([parts[name] for name, _ in IN_SPLIT], axis=0)


def kernel(x, norm_mix, w_in, b_gate, conv_w, conv_b, dt_bias, a_log, d_skip, ssm_norm, w_ssm_out, w_att_out, w_mix_out, norm_ffn, w_ffn_gate, w_ffn_up, w_ffn_down, norm_final, loss_target, m_norm_mix, m_w_in, m_b_gate, m_conv_w, m_conv_b, m_dt_bias, m_a_log, m_d_skip, m_ssm_norm, m_w_ssm_out, m_w_att_out, m_w_mix_out, m_norm_ffn, m_w_ffn_gate, m_w_ffn_up, m_w_ffn_down, m_norm_final, v_norm_mix, v_w_in, v_b_gate, v_conv_w, v_conv_b, v_dt_bias, v_a_log, v_d_skip, v_ssm_norm, v_w_ssm_out, v_w_att_out, v_w_mix_out, v_norm_ffn, v_w_ffn_gate, v_w_ffn_up, v_w_ffn_down, v_norm_final):
    given = dict(locals())
    weights = {name: given[name][0] for name, _, _ in SHARDED}
    b, s, d = x.shape
    t = b * s

    mat_specs, conv_specs, all_specs = _stacking(MATRIX_SHARDS), _stacking((CONV_SHARD,)), _stacking(SHARDED)
    stacking = _to_stacking(weights, SHARDED)
    mat_local = _pack_sharded(stacking, mat_specs, 16, BF16)
    conv_local = _pack_sharded(stacking, conv_specs, 8, F32)
    mat_all, conv_all = _all_gather([mat_local, conv_local], "weights_all_gather")
    shards = _unpack_sharded(mat_all, mat_specs, (N_DEV,))
    shards.update(_unpack_sharded(conv_all, conv_specs, (N_DEV,)))
    full = {name: shards[name].reshape(N_DEV * shape[0], shape[1]) for name, shape, _ in all_specs}
    w_sec = _split_w_in(full["w_in"])
    conv_taps = full["conv_w"].T

    g_mix, g_ffn, g_fin = norm_mix.reshape(1, d), norm_ffn.reshape(1, d), norm_final.reshape(1, d)
    bg_row = b_gate.reshape(1, 2 * d)
    convb_row = conv_b.reshape(1, SSM_CONV_DIM)
    ssmn_row = ssm_norm.reshape(1, SSM_D_INNER)
    dtb_row, alog_row, dsk_row = _lane_row(dt_bias), _lane_row(a_log), _lane_row(d_skip)
    cosf, sinf = _rope_tables(s)

    x2d = x.reshape(t, d)
    h1 = _rmsnorm_fwd(x2d, g_mix, "norm_mix_fwd")
    proj = {name: _mm(h1, w_sec[name], mode="nt", name="in_proj_" + name) for name, _ in IN_SPLIT}
    xbc3 = proj["xbc"].reshape(b, s, SSM_CONV_DIM)
    xc = _conv_fwd(xbc3, conv_taps, convb_row, "conv_fwd")
    dtr3 = proj["dt"].reshape(b, s, DT_PAD)
    y_ssd, h_states = _ssd_fwd(xc, dtr3, dtb_row, alog_row, dsk_row, "ssd_fwd")
    y_ssd2 = y_ssd.reshape(t, SSM_D_INNER)
    ynorm = _gate_norm_fwd(y_ssd2, proj["z"], ssmn_row, "ssd_gate_norm_fwd")
    y_ssm = _mm(ynorm, full["w_ssm_out"], mode="nn", name="ssm_out_proj")

    qkv3 = proj["qkv"].reshape(b, s, ATT_QKV_DIM)
    qk_parts = _rope_fwd(qkv3, cosf, sinf, "rope_fwd")
    att_parts = [_att_fwd(qk_parts[gi], "att_fwd_%d" % r) for gi, r in enumerate(ATT_DILATIONS)]
    att, *lse_parts = _att_merge([o for o, _ in att_parts], [l_ for _, l_ in att_parts], "att_merge")
    att2 = att.reshape(t, ATT_OUT_DIM)
    y_att = _mm(att2, full["w_att_out"], mode="nt", name="att_out_proj")

    mixed = _mix_fwd(proj["gate"], bg_row, y_ssm, y_att, "mix_fwd")
    x2 = _mm(mixed, full["w_mix_out"], mode="nn", name="mix_out_proj", add=x2d)
    h2 = _rmsnorm_fwd(x2, g_ffn, "norm_ffn_fwd")
    gt = _mm(h2, full["w_ffn_gate"], mode="nt", name="ffn_gate_proj")
    up = _mm(h2, full["w_ffn_up"], mode="nt", name="ffn_up_proj")
    act = _swiglu_fwd(gt, up, "swiglu_fwd")
    x3 = _mm(act, full["w_ffn_down"], mode="nn", name="ffn_down_proj", add=x2)

    loss_row, dx3, dg_fin, dx3b = _loss_head(x3, g_fin, loss_target.reshape(t, d), "loss_head")
    grads = {}
    dact = _mm(dx3b, full["w_ffn_down"], mode="nt", name="ffn_down_dx")
    grads["w_ffn_down"] = _mm(act, dx3b, mode="tn", name="ffn_down_dw")
    dgt, dup = _swiglu_bwd(gt, up, dact, "swiglu_bwd")
    grads["w_ffn_gate"] = _mm(dgt, h2, mode="tn", name="ffn_gate_dw")
    grads["w_ffn_up"] = _mm(dup, h2, mode="tn", name="ffn_up_dw")
    dh2 = _mm(dgt, full["w_ffn_gate"], mode="nn", name="ffn_gate_dx")
    dh2 = _mm(dup, full["w_ffn_up"], mode="nn", name="ffn_up_dx", add=dh2)
    dx2, dg_ffn, dx2b = _rmsnorm_bwd(x2, g_ffn, dh2, dx3, "norm_ffn_bwd", with_bf16=True)

    dmixed = _mm(dx2b, full["w_mix_out"], mode="nt", name="mix_out_dx")
    grads["w_mix_out"] = _mm(mixed, dx2b, mode="tn", name="mix_out_dw")
    dys, dya, dgl, dbg = _mix_bwd(proj["gate"], bg_row, y_ssm, y_att, dmixed, "mix_bwd")

    grads["w_ssm_out"] = _mm(ynorm, dys, mode="tn", name="ssm_out_dw")
    dynorm = _mm(dys, full["w_ssm_out"], mode="nt", name="ssm_out_dx")
    dy_ssd, dz, dssmn = _gate_norm_bwd(y_ssd2, proj["z"], ssmn_row, dynorm, "ssd_gate_norm_bwd")
    dxc, ddtr, dalog, ddsk, ddtb = _ssd_bwd(xc, dtr3, dy_ssd.reshape(b, s, SSM_D_INNER), h_states,
                                            dtb_row, alog_row, dsk_row, "ssd_bwd")
    dxbc, dconvw, dconvb = _conv_bwd(xbc3, dxc, conv_taps, convb_row, "conv_bwd")
    grads["conv_w"] = dconvw.T

    grads["w_att_out"] = _mm(dya, att2, mode="tn", name="att_out_dw")
    datt = _mm(dya, full["w_att_out"], mode="nn", name="att_out_dx").reshape(b, s, ATT_OUT_DIM)
    do_parts, dl_parts = _att_delta(att, datt, "att_delta")
    dqs, dks, dvs = [], [], []
    for gi, r in enumerate(ATT_DILATIONS):
        operands = (qk_parts[gi], do_parts[gi], lse_parts[gi], dl_parts[gi])
        dqs.append(_att_bwd_q(*operands, "att_bwd_q_%d" % r))
        dk_g, dv_g = _att_bwd_kv(*operands, "att_bwd_kv_%d" % r)
        dks.append(dk_g)
        dvs.append(dv_g)
    dqkv = _rope_bwd(dqs, dks, dvs, cosf, sinf, "rope_bwd")

    dproj = {"z": dz, "xbc": dxbc.reshape(t, SSM_CONV_DIM), "dt": ddtr.reshape(t, DT_PAD),
             "qkv": dqkv.reshape(t, ATT_QKV_DIM), "gate": dgl}
    grads["w_in"] = _join_w_in({name: _mm(dproj[name], h1, mode="tn", name="in_proj_dw_" + name)
                                for name, _ in IN_SPLIT})
    k_all = sum(dproj[name].shape[1] for name, _ in IN_SPLIT)
    k_pad = _round_up(k_all, 2048) - k_all
    dproj_all = jnp.concatenate([dproj[name] for name, _ in IN_SPLIT] + [jnp.zeros((t, k_pad), BF16)], axis=1)
    w_in_all = jnp.concatenate([w_sec[name] for name, _ in IN_SPLIT] + [jnp.zeros((k_pad, d), BF16)], axis=0)
    dh1 = _mm(dproj_all, w_in_all, mode="nn", name="in_proj_dx")
    grad_x, dg_mix = _rmsnorm_bwd(x2d, g_mix, dh1, dx2, "norm_mix_bwd")

    slabs = jnp.concatenate([grads[name].reshape(N_DEV, -1) for name, _, _ in all_specs], axis=1)
    slab_rows = _round_up(-(-slabs.shape[1] // LANES), PACK_ROWS)
    slabs = jnp.pad(slabs, ((0, 0), (0, slab_rows * LANES - slabs.shape[1]))).reshape(N_DEV, slab_rows, LANES)
    small = {"norm_mix": dg_mix, "b_gate": dbg, "conv_b": dconvb, "dt_bias": ddtb[:, :SSM_N_HEADS],
             "a_log": dalog[:, :SSM_N_HEADS], "d_skip": ddsk[:, :SSM_N_HEADS], "ssm_norm": dssmn,
             "norm_ffn": dg_ffn, "norm_final": dg_fin}
    core = lax.axis_index("c").astype(jnp.int32).reshape(1)
    chip_sums = _chip_sum(slabs, _pair_exchange(slabs, "grad_pair_exchange"), core, "grad_chip_sum")
    got, got_small = _chip_exchange(chip_sums, _pack_replicated(small), "grad_chip_exchange")

    def packed(prefix):
        vals = _to_stacking({name: given[prefix + name][0] for name, _, _ in SHARDED}, SHARDED)
        rep = {name: given[prefix + name] for name, _ in REPLICATED}
        return _pack_sharded(vals, all_specs, PACK_ROWS, F32), _pack_replicated(rep)

    (w_big, w_small), (m_big, m_small), (v_big, v_small) = packed(""), packed("m_"), packed("v_")
    big = _adamw(got, w_big, m_big, v_big, "adamw_sharded")
    sml = _adamw(got_small, w_small, m_small, v_small, "adamw_replicated")

    loss = lax.psum(loss_row[0, 0], ("x", "y", "c"))
    outs = [loss, grad_x.reshape(b, s, d)]
    rep_shapes = {name: given[name].shape for name, _ in REPLICATED}
    order = ["norm_mix", "w_in", "b_gate", "conv_w", "conv_b", "dt_bias", "a_log", "d_skip", "ssm_norm", "w_ssm_out",
             "w_att_out", "w_mix_out", "norm_ffn", "w_ffn_gate", "w_ffn_up", "w_ffn_down", "norm_final"]
    for big_k, sml_k in zip(big, sml):
        sharded = _to_stacking(_unpack_sharded(big_k, all_specs), SHARDED)
        rep = _unpack_replicated(sml_k, rep_shapes)
        for name in order:
            outs.append(sharded[name][None] if name in sharded else rep[name])
    return tuple(outs)
```

```python
import functools
import math

import jax
import jax.numpy as jnp
from jax import lax
from jax.experimental import pallas as pl
from jax.experimental.pallas import tpu as pltpu

F32 = jnp.float32
BF16 = jnp.bfloat16

N_DEV = 8
N_CHIPS = 4
D_MODEL = 1024
SSM_D_INNER = 2048
SSM_HEAD_DIM = 64
SSM_N_HEADS = 32
SSM_N_GROUPS = 4
SSM_HEADS_PER_GROUP = SSM_N_HEADS // SSM_N_GROUPS
SSM_D_STATE = 128
SSM_CONV = 4
SSM_CHUNK = 128
SSM_CONV_DIM = 3072
ATT_HEAD_DIM = 128
ATT_HEADS_PER_GROUP = 4
ATT_DILATIONS = (1, 4, 16)
ATT_N_HEADS = 12
ATT_QKV_DIM = 4608
ATT_OUT_DIM = 512
ATT_BLOCK = 128
ROPE_THETA = 10000.0
D_FF = 2816
IN_PROJ_DIM = 11808
EPS = 1e-6
LANES = 128
DT_PAD = LANES

ADAM_LR = 0.001
ADAM_B1 = 0.9
ADAM_B2 = 0.999
ADAM_EPS = 1e-08
ADAM_WD = 0.01
ADAM_STEP = 10

VMEM_LIMIT = 56 * 1024 * 1024
MESH = pl.DeviceIdType.MESH
NEG_INF = float("-inf")


def _pick(n, candidates):
    for c in candidates:
        if n % c == 0:
            return c
    return n


def _params(*sem):
    return pltpu.CompilerParams(dimension_semantics=sem, vmem_limit_bytes=VMEM_LIMIT)


def _sigmoid(x):
    return 1.0 / (1.0 + jnp.exp(-x))


def _softplus(x):
    return jnp.maximum(x, 0.0) + jnp.log(1.0 + jnp.exp(-jnp.abs(x)))


def _dot(a, b, dims):
    return lax.dot_general(a.astype(BF16), b.astype(BF16), (dims, ((), ())), preferred_element_type=F32)


def _nn(a, b):
    return _dot(a, b, ((1,), (0,)))


def _nt(a, b):
    return _dot(a, b, ((1,), (1,)))


def _tn(a, b):
    return _dot(a, b, ((0,), (0,)))


def _split3(v):
    hi = v.astype(BF16)
    r1 = v - hi.astype(F32)
    mid = r1.astype(BF16)
    lo = (r1 - mid.astype(F32)).astype(BF16)
    return hi, mid, lo


def _mask_nn(mask, v):
    mb = mask.astype(BF16)
    hi, mid, lo = _split3(v)
    return _nn(mb, hi) + (_nn(mb, mid) + _nn(mb, lo))


MM_VMEM_BUDGET = 40 * 1024 * 1024
MM_FULL_K = 2816


def _mm_tiles(m, n, k, a_bytes, b_bytes, o_bytes, has_add):
    tk = k if k <= MM_FULL_K else _pick(k, (2048, 1024, 512, 256, 128))
    tn = 1408 if (n > 1024 and n % 1408 == 0) else _pick(n, (1024, 768, 512, 384, 256, 128))
    for tm in (1408, 1024, 768, 512, 384, 256, 128):
        if m % tm:
            continue
        buffers = 2 * (tm * tk * a_bytes + tk * tn * b_bytes + tm * tn * (o_bytes + (4 if has_add else 0)))
        if tk < k:
            buffers += tm * tn * 4
        if buffers <= MM_VMEM_BUDGET:
            return tm, tn, tk
    return _pick(m, (128,)), tn, tk


def _mm(a, b, *, mode, name, out_dtype=F32, add=None):
    if mode == "nn":
        (m, k), n = a.shape, b.shape[1]
    elif mode == "nt":
        (m, k), n = a.shape, b.shape[0]
    else:
        (k, m), n = a.shape, b.shape[1]
    has_add = add is not None
    tm, tn, tk = _mm_tiles(m, n, k, a.dtype.itemsize, b.dtype.itemsize, jnp.dtype(out_dtype).itemsize, has_add)
    nk = k // tk
    dims = {"nn": ((1,), (0,)), "nt": ((1,), (1,)), "tn": ((0,), (0,))}[mode]
    a_spec = {"nn": pl.BlockSpec((tm, tk), lambda i, j, kk: (i, kk)),
              "nt": pl.BlockSpec((tm, tk), lambda i, j, kk: (i, kk)),
              "tn": pl.BlockSpec((tk, tm), lambda i, j, kk: (kk, i))}[mode]
    b_spec = {"nn": pl.BlockSpec((tk, tn), lambda i, j, kk: (kk, j)),
              "nt": pl.BlockSpec((tn, tk), lambda i, j, kk: (j, kk)),
              "tn": pl.BlockSpec((tk, tn), lambda i, j, kk: (kk, j))}[mode]
    o_spec = pl.BlockSpec((tm, tn), lambda i, j, kk: (i, j))

    def finish(r, c_ref, o_ref):
        if has_add:
            r = r + c_ref[...]
        o_ref[...] = r.astype(out_dtype)

    def body_one(*refs):
        a_ref, b_ref = refs[:2]
        finish(_dot(a_ref[...], b_ref[...], dims), refs[2] if has_add else None, refs[-1])

    def body_acc(*refs):
        a_ref, b_ref = refs[:2]
        o_ref, acc = refs[-2:]
        kk = pl.program_id(2)

        @pl.when(kk == 0)
        def _():
            acc[...] = jnp.zeros_like(acc)

        acc[...] += _dot(a_ref[...], b_ref[...], dims)

        @pl.when(kk == nk - 1)
        def _():
            finish(acc[...], refs[2] if has_add else None, o_ref)

    in_specs = [a_spec, b_spec] + ([o_spec] if has_add else [])
    args = (a, b) + ((add,) if has_add else ())
    return pl.pallas_call(
        body_one if nk == 1 else body_acc, name=name, grid=(m // tm, n // tn, nk),
        in_specs=in_specs, out_specs=o_spec,
        out_shape=jax.ShapeDtypeStruct((m, n), out_dtype),
        scratch_shapes=[] if nk == 1 else [pltpu.VMEM((tm, tn), F32)],
        compiler_params=_params("parallel", "parallel", "arbitrary"),
    )(*args)


def _rmsnorm_fwd(x, g, name):
    t, d = x.shape
    tm = _pick(t, (512, 256, 128))

    def body(x_ref, g_ref, o_ref):
        xv = x_ref[...]
        r = lax.rsqrt(jnp.mean(xv * xv, axis=-1, keepdims=True) + EPS)
        o_ref[...] = ((xv * r) * g_ref[...]).astype(BF16)

    return pl.pallas_call(
        body, name=name, grid=(t // tm,),
        in_specs=[pl.BlockSpec((tm, d), lambda i: (i, 0)), pl.BlockSpec((1, d), lambda i: (0, 0))],
        out_specs=pl.BlockSpec((tm, d), lambda i: (i, 0)),
        out_shape=jax.ShapeDtypeStruct((t, d), BF16),
        compiler_params=_params("parallel"),
    )(x, g)


def _rmsnorm_bwd(x, g, dh, dres, name, with_bf16=False):
    t, d = x.shape
    tm = _pick(t, (512, 256, 128))

    def body(x_ref, g_ref, dh_ref, dres_ref, dx_ref, dg_ref, *dxb_ref):
        @pl.when(pl.program_id(0) == 0)
        def _():
            dg_ref[...] = jnp.zeros_like(dg_ref)

        xv = x_ref[...]
        r = lax.rsqrt(jnp.mean(xv * xv, axis=-1, keepdims=True) + EPS)
        xhat = xv * r
        dhv = dh_ref[...]
        dyg = dhv * g_ref[...]
        dx = dres_ref[...] + r * (dyg - xhat * jnp.mean(dyg * xhat, axis=-1, keepdims=True))
        dx_ref[...] = dx
        if with_bf16:
            dxb_ref[0][...] = dx.astype(BF16)
        dg_ref[...] += jnp.sum(dhv * xhat, axis=0, keepdims=True)

    row = pl.BlockSpec((tm, d), lambda i: (i, 0))
    vec = pl.BlockSpec((1, d), lambda i: (0, 0))
    extra = with_bf16 * [jax.ShapeDtypeStruct((t, d), BF16)]
    return pl.pallas_call(
        body, name=name, grid=(t // tm,),
        in_specs=[row, vec, row, row], out_specs=[row, vec] + with_bf16 * [row],
        out_shape=[jax.ShapeDtypeStruct((t, d), F32), jax.ShapeDtypeStruct((1, d), F32)] + extra,
        compiler_params=_params("arbitrary"),
    )(x, g, dh, dres)


def _loss_head(x, g, target, name):
    t, d = x.shape
    tm = _pick(t, (512, 256, 128))

    def body(x_ref, g_ref, t_ref, loss_ref, dx_ref, dg_ref, dxb_ref):
        @pl.when(pl.program_id(0) == 0)
        def _():
            dg_ref[...] = jnp.zeros_like(dg_ref)
            loss_ref[...] = jnp.zeros_like(loss_ref)

        xv = x_ref[...]
        gv = g_ref[...]
        r = lax.rsqrt(jnp.mean(xv * xv, axis=-1, keepdims=True) + EPS)
        xhat = xv * r
        err = xhat * gv - t_ref[...]
        loss_ref[...] += jnp.sum(err * err) * (0.5 / d)
        dy = err * (1.0 / d)
        dyg = dy * gv
        dx = r * (dyg - xhat * jnp.mean(dyg * xhat, axis=-1, keepdims=True))
        dx_ref[...] = dx
        dxb_ref[...] = dx.astype(BF16)
        dg_ref[...] += jnp.sum(dy * xhat, axis=0, keepdims=True)

    row = pl.BlockSpec((tm, d), lambda i: (i, 0))
    vec = pl.BlockSpec((1, d), lambda i: (0, 0))
    return pl.pallas_call(
        body, name=name, grid=(t // tm,),
        in_specs=[row, vec, row],
        out_specs=[pl.BlockSpec((1, LANES), lambda i: (0, 0)), row, vec, row],
        out_shape=[jax.ShapeDtypeStruct((1, LANES), F32), jax.ShapeDtypeStruct((t, d), F32),
                   jax.ShapeDtypeStruct((1, d), F32), jax.ShapeDtypeStruct((t, d), BF16)],
        compiler_params=_params("arbitrary"),
    )(x, g, target)


CONV_HALO = 8


def _conv_fwd(u, w, bias, name):
    b, s, c = u.shape
    ts = _pick(s, (512, 256, 128))
    cb = _pick(c, (512, 384, 256, 128))
    hb = ts // CONV_HALO

    def body(u_ref, h_ref, w_ref, b_ref, o_ref):
        uv = u_ref[...]
        halo = jnp.where(pl.program_id(1) == 0, 0.0, h_ref[...])
        ext = jnp.concatenate([halo, uv], axis=0)
        wv = w_ref[...]
        acc = b_ref[...] + wv[SSM_CONV - 1:SSM_CONV, :] * uv
        for sh in range(1, SSM_CONV):
            kidx = SSM_CONV - 1 - sh
            acc = acc + wv[kidx:kidx + 1, :] * ext[CONV_HALO - sh:CONV_HALO - sh + ts, :]
        o_ref[...] = acc * _sigmoid(acc)

    return pl.pallas_call(
        body, name=name, grid=(b, s // ts, c // cb),
        in_specs=[pl.BlockSpec((None, ts, cb), lambda bi, i, j: (bi, i, j)),
                  pl.BlockSpec((None, CONV_HALO, cb), lambda bi, i, j: (bi, jnp.maximum(i * hb - 1, 0), j)),
                  pl.BlockSpec((SSM_CONV, cb), lambda bi, i, j: (0, j)),
                  pl.BlockSpec((1, cb), lambda bi, i, j: (0, j))],
        out_specs=pl.BlockSpec((None, ts, cb), lambda bi, i, j: (bi, i, j)),
        out_shape=jax.ShapeDtypeStruct((b, s, c), F32),
        compiler_params=_params("parallel", "parallel", "parallel"),
    )(u, u, w, bias)


def _conv_bwd(u, dout, w, bias, name):
    b, s, c = u.shape
    ts = _pick(s, (512, 256, 128))
    cb = _pick(c, (512, 384, 256, 128))
    hb = ts // CONV_HALO
    n_t = s // ts

    def pre_act_grad(ext_u, cur_u, dout_v, wv, bv, rows):
        acc = bv + wv[SSM_CONV - 1:SSM_CONV, :] * cur_u
        for sh in range(1, SSM_CONV):
            kidx = SSM_CONV - 1 - sh
            acc = acc + wv[kidx:kidx + 1, :] * ext_u[CONV_HALO - sh:CONV_HALO - sh + rows, :]
        sg = _sigmoid(acc)
        return dout_v * (sg * (1.0 + acc * (1.0 - sg)))

    def body(u_ref, up_ref, un_ref, d_ref, dn_ref, w_ref, b_ref, du_ref, dw_ref, db_ref):
        i = pl.program_id(2)
        first = jnp.logical_and(pl.program_id(1) == 0, i == 0)

        @pl.when(first)
        def _():
            dw_ref[...] = jnp.zeros_like(dw_ref)
            db_ref[...] = jnp.zeros_like(db_ref)

        wv = w_ref[...]
        bv = b_ref[...]
        uv = u_ref[...]
        u_prev = jnp.where(i == 0, 0.0, up_ref[...])
        ext_u = jnp.concatenate([u_prev, uv], axis=0)
        dpre = pre_act_grad(ext_u, uv, d_ref[...], wv, bv, ts)
        un = un_ref[...]
        ext_n = jnp.concatenate([uv[ts - CONV_HALO:, :], un], axis=0)
        dpre_n = pre_act_grad(ext_n, un, dn_ref[...], wv, bv, CONV_HALO)
        dpre_n = jnp.where(i == n_t - 1, 0.0, dpre_n)
        ext_d = jnp.concatenate([dpre, dpre_n], axis=0)
        du = wv[SSM_CONV - 1:SSM_CONV, :] * dpre
        dw_ref[SSM_CONV - 1:SSM_CONV, :] += jnp.sum(dpre * uv, axis=0, keepdims=True)
        for sh in range(1, SSM_CONV):
            kidx = SSM_CONV - 1 - sh
            du = du + wv[kidx:kidx + 1, :] * ext_d[sh:sh + ts, :]
            dw_ref[kidx:kidx + 1, :] += jnp.sum(dpre * ext_u[CONV_HALO - sh:CONV_HALO - sh + ts, :],
                                                axis=0, keepdims=True)
        du_ref[...] = du.astype(BF16)
        db_ref[...] += jnp.sum(dpre, axis=0, keepdims=True)

    last_hb = s // CONV_HALO - 1
    tile = pl.BlockSpec((None, ts, cb), lambda j, bi, i: (bi, i, j))
    prev = pl.BlockSpec((None, CONV_HALO, cb), lambda j, bi, i: (bi, jnp.maximum(i * hb - 1, 0), j))
    nxt = pl.BlockSpec((None, CONV_HALO, cb), lambda j, bi, i: (bi, jnp.minimum((i + 1) * hb, last_hb), j))
    return pl.pallas_call(
        body, name=name, grid=(c // cb, b, n_t),
        in_specs=[tile, prev, nxt, tile, nxt,
                  pl.BlockSpec((SSM_CONV, cb), lambda j, bi, i: (0, j)),
                  pl.BlockSpec((1, cb), lambda j, bi, i: (0, j))],
        out_specs=[tile, pl.BlockSpec((SSM_CONV, cb), lambda j, bi, i: (0, j)),
                   pl.BlockSpec((1, cb), lambda j, bi, i: (0, j))],
        out_shape=[jax.ShapeDtypeStruct((b, s, c), BF16), jax.ShapeDtypeStruct((SSM_CONV, c), F32),
                   jax.ShapeDtypeStruct((1, c), F32)],
        compiler_params=_params("parallel", "arbitrary", "arbitrary"),
    )(u, u, u, dout, dout, w, bias)


def _ssd_chunk_terms(dtr_ref, bias_ref, alog_ref):
    q = SSM_CHUNK
    dt = _softplus(dtr_ref[...] + bias_ref[...])
    a_neg = -jnp.exp(alog_ref[...])
    row = lax.broadcasted_iota(jnp.int32, (q, q), 0)
    col = lax.broadcasted_iota(jnp.int32, (q, q), 1)
    lower = row >= col
    s = _mask_nn(lower, dt * a_neg)
    return dt, a_neg, s, s.T, lower


def _head_masks():
    heads = jnp.arange(LANES)[:, None]
    chans = jnp.arange(SSM_D_INNER)[None, :]
    to_channels = (chans // SSM_HEAD_DIM == heads).astype(BF16)
    return to_channels, to_channels.T


def _per_channel(v, to_channels):
    hi = v.astype(BF16)
    lo = (v - hi.astype(F32)).astype(BF16)
    return _nn(hi, to_channels) + _nn(lo, to_channels)


def _per_head(v, to_heads):
    hi = v.astype(BF16)
    lo = (v - hi.astype(F32)).astype(BF16)
    return _nn(hi, to_heads) + _nn(lo, to_heads)


def _decay_terms_per_channel(dt, s_col, to_channels):
    q = SSM_CHUNK
    tot = s_col[q - 1:q, :]
    stacked = jnp.concatenate([dt, jnp.exp(s_col), jnp.exp(tot - s_col)], axis=0)
    wide = _per_channel(stacked, to_channels)
    dtx, esx, decx = wide[:q], wide[q:2 * q], wide[2 * q:]
    return dtx, esx, decx, esx[0:1, :] * decx[0:1, :]


SSM_PAIRS_PER_GROUP = SSM_HEADS_PER_GROUP // 2
SSM_GROUP_CHANNELS = SSM_HEADS_PER_GROUP * SSM_HEAD_DIM


def _split_pair(v):
    first = lax.broadcasted_iota(jnp.int32, v.shape, 1) < SSM_HEAD_DIM
    return jnp.concatenate([jnp.where(first, v, 0.0), jnp.where(first, 0.0, v)], axis=0)


def _ssd_fwd(xc, dtr, dt_bias, a_log, dskx, to_channels, name):
    b, s, _ = xc.shape
    q = SSM_CHUNK
    nc = s // q
    n, gc = SSM_D_STATE, SSM_GROUP_CHANNELS

    def body(xc_ref, dtr_ref, bias_ref, alog_ref, dsk_ref, tc_ref, y_ref, hs_ref, h_scr):
        @pl.when(pl.program_id(1) == 0)
        def _():
            h_scr[...] = jnp.zeros_like(h_scr)

        dt, _, s_col, s_row, lower = _ssd_chunk_terms(dtr_ref, bias_ref, alog_ref)
        dtx, esx, decx, etotx = _decay_terms_per_channel(dt, s_col, tc_ref[...])
        x = xc_ref[:, :SSM_D_INNER]
        xdt = x * dtx
        xdec = xdt * decx
        skip = dsk_ref[...] * x
        for g in range(SSM_N_GROUPS):
            bg = xc_ref[:, SSM_D_INNER + n * g:SSM_D_INNER + n * (g + 1)].astype(BF16)
            cg = xc_ref[:, SSM_D_INNER + n * (SSM_N_GROUPS + g):SSM_D_INNER + n * (SSM_N_GROUPS + g + 1)].astype(BF16)
            gsl = slice(gc * g, gc * (g + 1))
            gm = _nt(cg, bg)
            hgt = h_scr[:, gsl]
            hs_ref[:, gsl] = hgt
            y_off = esx[:, gsl] * _nn(cg, hgt)
            h_scr[:, gsl] = etotx[:, gsl] * hgt + _tn(bg, xdec[:, gsl])
            for k in range(SSM_PAIRS_PER_GROUP):
                h0 = g * SSM_HEADS_PER_GROUP + 2 * k
                lo = gc * g + LANES * k
                ms = []
                for h in (h0, h0 + 1):
                    lm = jnp.exp(jnp.where(lower, s_col[:, h:h + 1] - s_row[h:h + 1, :], NEG_INF))
                    ms.append((gm * lm).astype(BF16))
                y_diag = _nn(jnp.concatenate(ms, axis=1), _split_pair(xdt[:, lo:lo + LANES]))
                y_ref[:, lo:lo + LANES] = y_diag + y_off[:, LANES * k:LANES * (k + 1)] + skip[:, lo:lo + LANES]

    vec = pl.BlockSpec((1, LANES), lambda bi, c: (0, 0))
    return pl.pallas_call(
        body, name=name, grid=(b, nc),
        in_specs=[pl.BlockSpec((None, q, SSM_CONV_DIM), lambda bi, c: (bi, c, 0)),
                  pl.BlockSpec((None, q, LANES), lambda bi, c: (bi, c, 0)), vec, vec,
                  pl.BlockSpec((1, SSM_D_INNER), lambda bi, c: (0, 0)),
                  pl.BlockSpec((LANES, SSM_D_INNER), lambda bi, c: (0, 0))],
        out_specs=[pl.BlockSpec((None, q, SSM_D_INNER), lambda bi, c: (bi, c, 0)),
                   pl.BlockSpec((None, None, n, SSM_D_INNER), lambda bi, c: (bi, c, 0, 0))],
        out_shape=[jax.ShapeDtypeStruct((b, s, SSM_D_INNER), F32),
                   jax.ShapeDtypeStruct((b, nc, n, SSM_D_INNER), F32)],
        scratch_shapes=[pltpu.VMEM((n, SSM_D_INNER), F32)],
        compiler_params=_params("parallel", "arbitrary"),
    )(xc, dtr, dt_bias, a_log, dskx, to_channels)


def _ssd_bwd(xc, dtr, dy, hs, dt_bias, a_log, dskx, to_channels, to_heads, name):
    b, s, _ = xc.shape
    q = SSM_CHUNK
    nc = s // q
    n, gc = SSM_D_STATE, SSM_GROUP_CHANNELS

    def colsum(v):
        return jnp.sum(v, axis=0, keepdims=True)

    def body(xc_ref, dtr_ref, dy_ref, hs_ref, bias_ref, alog_ref, dsk_ref, tc_ref, th_ref,
             dxc_ref, ddtr_ref, dalog_ref, ddsk_ref, dbias_ref, dh_scr, dxs_scr, dxd_scr, w_scr, dst_scr, rows_scr):
        ci = pl.program_id(1)

        @pl.when(ci == 0)
        def _():
            dh_scr[...] = jnp.zeros_like(dh_scr)

        @pl.when(jnp.logical_and(pl.program_id(0) == 0, ci == 0))
        def _():
            dalog_ref[...] = jnp.zeros_like(dalog_ref)
            ddsk_ref[...] = jnp.zeros_like(ddsk_ref)
            dbias_ref[...] = jnp.zeros_like(dbias_ref)
            dst_scr[...] = jnp.zeros_like(dst_scr)

        dt, a_neg, s_col, s_row, lower = _ssd_chunk_terms(dtr_ref, bias_ref, alog_ref)
        upper = jnp.logical_not(lower) | (lax.broadcasted_iota(jnp.int32, (q, q), 0)
                                          == lax.broadcasted_iota(jnp.int32, (q, q), 1))
        dtx, esx, decx, etotx = _decay_terms_per_channel(dt, s_col, tc_ref[...])
        x = xc_ref[:, :SSM_D_INNER]
        dyv = dy_ref[...]
        xdt = x * dtx
        xdec = xdt * decx
        dw = esx * dyv
        rows_scr[...] = jnp.zeros_like(rows_scr)
        for g in range(SSM_N_GROUPS):
            b_lo = SSM_D_INNER + n * g
            c_lo = SSM_D_INNER + n * (SSM_N_GROUPS + g)
            bg = xc_ref[:, b_lo:b_lo + n].astype(BF16)
            cg = xc_ref[:, c_lo:c_lo + n].astype(BF16)
            gsl = slice(gc * g, gc * (g + 1))
            gm = _nt(cg, bg)
            gmt = _nt(bg, cg)
            hgt = hs_ref[:, gsl]
            dhgt = dh_scr[:, gsl]
            w_scr[:, gsl] = _nn(cg, hgt)
            dcg = _nt(dw[:, gsl], hgt)
            dxs = decx[:, gsl] * _nn(bg, dhgt)
            dxs_scr[:, gsl] = dxs
            dbg = _nt(xdec[:, gsl], dhgt)
            rows_scr[2:3, gsl] = colsum(dhgt * hgt)
            dh_scr[:, gsl] = _tn(cg, dw[:, gsl]) + etotx[:, gsl] * dhgt
            dg = jnp.zeros((q, q), F32)
            dgt = jnp.zeros((q, q), F32)
            for k in range(SSM_PAIRS_PER_GROUP):
                h0 = g * SSM_HEADS_PER_GROUP + 2 * k
                lo = gc * g + LANES * k
                xp = xdt[:, lo:lo + LANES]
                dyp = dyv[:, lo:lo + LANES]
                dy2 = _split_pair(dyp)
                dm2 = _nt(dy2, xp)
                dmt2 = _nt(_split_pair(xp), dyp)
                mts = []
                for i, h in enumerate((h0, h0 + 1)):
                    lm = jnp.exp(jnp.where(lower, s_col[:, h:h + 1] - s_row[h:h + 1, :], NEG_INF))
                    lmt = jnp.exp(jnp.where(upper, s_row[h:h + 1, :] - s_col[:, h:h + 1], NEG_INF))
                    dm = dm2[q * i:q * (i + 1), :]
                    dmt = dmt2[q * i:q * (i + 1), :]
                    dg = dg + dm * lm
                    dgt = dgt + dmt * lmt
                    mt = gmt * lmt
                    dst_scr[h:h + 1, :] = colsum(dmt * mt) - colsum(dm * (gm * lm))
                    mts.append(mt.astype(BF16))
                dxd_scr[:, lo:lo + LANES] = _nn(jnp.concatenate(mts, axis=1), dy2)
            dxc_ref[:, b_lo:b_lo + n] = dbg + _nn(dgt, cg)
            dxc_ref[:, c_lo:c_lo + n] = dcg + _nn(dg, bg)
        dxs = dxs_scr[...]
        dxdt = dxd_scr[...] + dxs
        dxc_ref[:, :SSM_D_INNER] = dxdt * dtx + dsk_ref[...] * dyv
        state_part = xdt * dxs
        rows_scr[0:1, :] = colsum(dyv * x)
        rows_scr[1:2, :] = colsum(state_part)
        th = th_ref[...]
        per_head = _per_head(jnp.concatenate([dw * w_scr[...] - state_part, dxdt * x], axis=0), th)
        r_ds, r_dt = per_head[:q], per_head[q:]
        sums = _per_head(rows_scr[...], th)
        etot = jnp.exp(s_col[q - 1:q, :])
        dtot = sums[1:2, :] + etot * sums[2:3, :]
        last = lax.broadcasted_iota(jnp.int32, (q, LANES), 0) == q - 1
        ds = dst_scr[...].T + r_ds + jnp.where(last, dtot, 0.0)
        da = _mask_nn(upper, ds)
        ddt = da * a_neg + r_dt
        live = lax.broadcasted_iota(jnp.int32, (1, LANES), 1) < SSM_N_HEADS
        sg = _sigmoid(dtr_ref[...] + bias_ref[...])
        ddtr = jnp.where(live, ddt * sg, 0.0)
        ddtr_ref[...] = ddtr.astype(BF16)
        dalog_ref[...] += jnp.where(live, colsum(da * dt) * a_neg, 0.0)
        ddsk_ref[...] += jnp.where(live, sums[0:1, :], 0.0)
        dbias_ref[...] += colsum(ddtr)

    rev = lambda bi, c: (bi, nc - 1 - c, 0)
    vec = pl.BlockSpec((1, LANES), lambda bi, c: (0, 0))
    wide = pl.BlockSpec((None, q, SSM_D_INNER), rev)
    return pl.pallas_call(
        body, name=name, grid=(b, nc),
        in_specs=[pl.BlockSpec((None, q, SSM_CONV_DIM), rev), pl.BlockSpec((None, q, LANES), rev), wide,
                  pl.BlockSpec((None, None, n, SSM_D_INNER), lambda bi, c: (bi, nc - 1 - c, 0, 0)),
                  vec, vec, pl.BlockSpec((1, SSM_D_INNER), lambda bi, c: (0, 0)),
                  pl.BlockSpec((LANES, SSM_D_INNER), lambda bi, c: (0, 0)),
                  pl.BlockSpec((SSM_D_INNER, LANES), lambda bi, c: (0, 0))],
        out_specs=[pl.BlockSpec((None, q, SSM_CONV_DIM), rev), pl.BlockSpec((None, q, LANES), rev), vec, vec, vec],
        out_shape=[jax.ShapeDtypeStruct((b, s, SSM_CONV_DIM), F32), jax.ShapeDtypeStruct((b, s, LANES), BF16),
                   jax.ShapeDtypeStruct((1, LANES), F32), jax.ShapeDtypeStruct((1, LANES), F32),
                   jax.ShapeDtypeStruct((1, LANES), F32)],
        scratch_shapes=[pltpu.VMEM((n, SSM_D_INNER), F32)] + [pltpu.VMEM((q, SSM_D_INNER), F32)] * 3
        + [pltpu.VMEM((LANES, q), F32), pltpu.VMEM((8, SSM_D_INNER), F32)],
        compiler_params=_params("arbitrary", "arbitrary"),
    )(xc, dtr, dy, hs, dt_bias, a_log, dskx, to_channels, to_heads)


SSM_GROUP_WIDTH = SSM_D_INNER // SSM_N_GROUPS


def _gate_norm_fwd(y, z, w, name):
    t, d = y.shape
    tm = _pick(t, (256, 128))

    def body(y_ref, z_ref, w_ref, o_ref):
        for g in range(SSM_N_GROUPS):
            sl = slice(SSM_GROUP_WIDTH * g, SSM_GROUP_WIDTH * (g + 1))
            zv = z_ref[:, sl]
            u = y_ref[:, sl] * (zv * _sigmoid(zv))
            r = lax.rsqrt(jnp.mean(u * u, axis=-1, keepdims=True) + EPS)
            o_ref[:, sl] = ((u * r) * w_ref[:, sl]).astype(BF16)

    row = pl.BlockSpec((tm, d), lambda i: (i, 0))
    return pl.pallas_call(
        body, name=name, grid=(t // tm,),
        in_specs=[row, row, pl.BlockSpec((1, d), lambda i: (0, 0))], out_specs=row,
        out_shape=jax.ShapeDtypeStruct((t, d), BF16),
        compiler_params=_params("parallel"),
    )(y, z, w)


def _gate_norm_bwd(y, z, w, dout, name):
    t, d = y.shape
    tm = _pick(t, (256, 128))

    def body(y_ref, z_ref, w_ref, do_ref, dy_ref, dz_ref, dw_ref):
        @pl.when(pl.program_id(0) == 0)
        def _():
            dw_ref[...] = jnp.zeros_like(dw_ref)

        for g in range(SSM_N_GROUPS):
            sl = slice(SSM_GROUP_WIDTH * g, SSM_GROUP_WIDTH * (g + 1))
            zv = z_ref[:, sl]
            yv = y_ref[:, sl]
            sg = _sigmoid(zv)
            silu = zv * sg
            u = yv * silu
            r = lax.rsqrt(jnp.mean(u * u, axis=-1, keepdims=True) + EPS)
            uh = u * r
            dov = do_ref[:, sl]
            dw_ref[:, sl] += jnp.sum(dov * uh, axis=0, keepdims=True)
            dyg = dov * w_ref[:, sl]
            du = r * (dyg - uh * jnp.mean(dyg * uh, axis=-1, keepdims=True))
            dy_ref[:, sl] = du * silu
            dz_ref[:, sl] = (du * yv * (sg * (1.0 + zv * (1.0 - sg)))).astype(BF16)

    row = pl.BlockSpec((tm, d), lambda i: (i, 0))
    vec = pl.BlockSpec((1, d), lambda i: (0, 0))
    return pl.pallas_call(
        body, name=name, grid=(t // tm,),
        in_specs=[row, row, vec, row], out_specs=[row, row, vec],
        out_shape=[jax.ShapeDtypeStruct((t, d), F32), jax.ShapeDtypeStruct((t, d), BF16),
                   jax.ShapeDtypeStruct((1, d), F32)],
        compiler_params=_params("arbitrary"),
    )(y, z, w, dout)


def _rope_tables(s):
    half = ATT_HEAD_DIM // 2
    inv = ROPE_THETA ** (-jnp.arange(half, dtype=F32) / half)
    ang = jnp.arange(s).astype(F32)[:, None] * inv[None, :]
    cos, sin = jnp.cos(ang), jnp.sin(ang)
    return jnp.concatenate([cos, cos], axis=-1), jnp.concatenate([-sin, sin], axis=-1)


ATT_TILE = 256


def _by_residue_spec(r, width):
    return pl.BlockSpec((None, r, ATT_TILE // r, width), lambda bi, i: (bi, 0, i, 0))


def _to_residues(tile, stage, r, store):
    if r == 1:
        store(0, tile)
        return
    stage[...] = tile
    for ri in range(r):
        store(ri, stage[pl.ds(ri, ATT_TILE // r, stride=r), :])


def _from_residues(load, stage, r):
    if r == 1:
        return load(0)
    for ri in range(r):
        stage[pl.ds(ri, ATT_TILE // r, stride=r), :] = load(ri)
    return stage[...]


def _rope_fwd(qkv, cosf, sinf, name):
    b, s, w = qkv.shape
    ts, d, gw = ATT_TILE, ATT_HEAD_DIM, ATT_OUT_DIM

    def body(x_ref, c_ref, s_ref, *rest):
        outs, stage = rest[:-1], rest[-1]
        cv, sv = c_ref[...], s_ref[...]
        for kind in range(3):
            for gi, r in enumerate(ATT_DILATIONS):
                for j in range(ATT_HEADS_PER_GROUP):
                    src = d * (kind * ATT_N_HEADS + gi * ATT_HEADS_PER_GROUP + j)
                    dst = slice(kind * gw + d * j, kind * gw + d * (j + 1))
                    tv = x_ref[:, src:src + d]
                    if kind < 2:
                        tv = tv * cv + pltpu.roll(tv, d // 2, 1) * sv

                    def store(ri, rows, o_ref=outs[gi], dst=dst):
                        o_ref[ri, :, dst] = rows.astype(BF16)

                    _to_residues(tv, stage, r, store)

    tab = pl.BlockSpec((ts, d), lambda bi, i: (i, 0))
    return pl.pallas_call(
        body, name=name, grid=(b, s // ts),
        in_specs=[pl.BlockSpec((None, ts, w), lambda bi, i: (bi, i, 0)), tab, tab],
        out_specs=[_by_residue_spec(r, 3 * gw) for r in ATT_DILATIONS],
        out_shape=[jax.ShapeDtypeStruct((b, r, s // r, 3 * gw), BF16) for r in ATT_DILATIONS],
        scratch_shapes=[pltpu.VMEM((ts, d), F32)],
        compiler_params=_params("parallel", "parallel"),
    )(qkv, cosf, sinf)


def _rope_bwd(dq, dk, dv, cosf, sinf, name):
    n_pat = len(ATT_DILATIONS)
    b, _, s, gw = dq[0].shape
    ts, d = ATT_TILE, ATT_HEAD_DIM

    def body(*refs):
        ins, (c_ref, s_ref, o_ref, stage) = refs[:3 * n_pat], refs[3 * n_pat:]
        cv, sv = c_ref[...], s_ref[...]
        for kind in range(3):
            for gi, r in enumerate(ATT_DILATIONS):
                src = ins[kind * n_pat + gi]
                for j in range(ATT_HEADS_PER_GROUP):
                    tv = _from_residues(lambda ri, src=src, j=j: src[ri, :, d * j:d * (j + 1)], stage, r)
                    if kind < 2:
                        tv = tv * cv + pltpu.roll(tv * sv, d // 2, 1)
                    lo = d * (kind * ATT_N_HEADS + gi * ATT_HEADS_PER_GROUP + j)
                    o_ref[:, lo:lo + d] = tv.astype(BF16)

    tab = pl.BlockSpec((ts, d), lambda bi, i: (i, 0))
    parts = [_by_residue_spec(r, gw) for r in ATT_DILATIONS]
    return pl.pallas_call(
        body, name=name, grid=(b, s // ts), in_specs=parts * 3 + [tab, tab],
        out_specs=pl.BlockSpec((None, ts, ATT_QKV_DIM), lambda bi, i: (bi, i, 0)),
        out_shape=jax.ShapeDtypeStruct((b, s, ATT_QKV_DIM), BF16),
        scratch_shapes=[pltpu.VMEM((ts, d), F32)],
        compiler_params=_params("parallel", "parallel"),
    )(*dq, *dk, *dv, cosf, sinf)


ATT_SCALE = ATT_HEAD_DIM ** -0.5
ATT_BLK = (None, None, ATT_BLOCK, ATT_OUT_DIM)


def _att_spec(col, shift, n_blocks):
    def index(bi, ri, nb_i):
        return (bi, ri, jnp.clip(nb_i + shift, 0, n_blocks - 1), col)
    return pl.BlockSpec(ATT_BLK, index)


def _band_mask(shape, q_axis, has_prev):
    qi = lax.broadcasted_iota(jnp.int32, shape, q_axis)
    kj = lax.broadcasted_iota(jnp.int32, shape, 1 - q_axis)
    dist = qi + ATT_BLOCK - kj
    return (dist >= 0) & (dist <= ATT_BLOCK) & (has_prev | (kj >= ATT_BLOCK))


def _att_fwd(qkr, name):
    b, r, l, _ = qkr.shape
    nb = l // ATT_BLOCK
    d = ATT_HEAD_DIM

    def body(q_ref, kp_ref, k_ref, vp_ref, v_ref, o_ref, lse_ref):
        mask = _band_mask((ATT_BLOCK, 2 * ATT_BLOCK), 0, pl.program_id(2) > 0)
        for j in range(ATT_HEADS_PER_GROUP):
            sl = slice(d * j, d * (j + 1))
            kcat = jnp.concatenate([kp_ref[:, sl], k_ref[:, sl]], axis=0)
            vcat = jnp.concatenate([vp_ref[:, sl], v_ref[:, sl]], axis=0)
            sc = jnp.where(mask, _nt(q_ref[:, sl], kcat) * ATT_SCALE, NEG_INF)
            m = jnp.max(sc, axis=-1, keepdims=True)
            pr = jnp.exp(sc - m)
            den = jnp.sum(pr, axis=-1, keepdims=True)
            o_ref[:, sl] = _nn(pr / den, vcat)
            lse_ref[:, sl] = jnp.broadcast_to(m + jnp.log(den), (ATT_BLOCK, d))

    out_spec = _att_spec(0, 0, nb)
    return pl.pallas_call(
        body, name=name, grid=(b, r, nb),
        in_specs=[_att_spec(0, 0, nb), _att_spec(1, -1, nb), _att_spec(1, 0, nb),
                  _att_spec(2, -1, nb), _att_spec(2, 0, nb)],
        out_specs=[out_spec, out_spec],
        out_shape=[jax.ShapeDtypeStruct((b, r, l, ATT_OUT_DIM), F32)] * 2,
        compiler_params=_params("parallel", "parallel", "parallel"),
    )(qkr, qkr, qkr, qkr, qkr)


def _att_merge(os_, lses, name):
    n_pat = len(os_)
    b, _, s, gw = os_[0].shape
    ts, d = ATT_TILE, ATT_HEAD_DIM

    def body(*refs):
        o_refs, l_refs = refs[:n_pat], refs[n_pat:2 * n_pat]
        att_ref, lse_outs, stage = refs[2 * n_pat], refs[2 * n_pat + 1:3 * n_pat + 1], refs[-1]
        for j in range(ATT_HEADS_PER_GROUP):
            sl = slice(d * j, d * (j + 1))
            ov = [_from_residues(lambda ri, g=g: o_refs[g][ri, :, sl], stage, r)
                  for g, r in enumerate(ATT_DILATIONS)]
            ls = [_from_residues(lambda ri, g=g: l_refs[g][ri, :, sl], stage, r)
                  for g, r in enumerate(ATT_DILATIONS)]
            m = functools.reduce(jnp.maximum, ls)
            es = [jnp.exp(lv - m) for lv in ls]
            tot = functools.reduce(lambda u, v: u + v, es)
            acc = (es[0] / tot) * ov[0]
            for g in range(1, n_pat):
                acc = acc + (es[g] / tot) * ov[g]
            att_ref[:, sl] = acc
            joint = m + jnp.log(tot)
            for g, r in enumerate(ATT_DILATIONS):
                def store(ri, rows, out=lse_outs[g]):
                    out[ri, :, sl] = rows
                _to_residues(joint, stage, r, store)

    parts = [_by_residue_spec(r, gw) for r in ATT_DILATIONS]
    return pl.pallas_call(
        body, name=name, grid=(b, s // ts), in_specs=parts * 2,
        out_specs=[pl.BlockSpec((None, ts, gw), lambda bi, i: (bi, i, 0))] + parts,
        out_shape=[jax.ShapeDtypeStruct((b, s, gw), F32)]
        + [jax.ShapeDtypeStruct((b, r, s // r, gw), F32) for r in ATT_DILATIONS],
        scratch_shapes=[pltpu.VMEM((ts, d), F32)],
        compiler_params=_params("parallel", "parallel"),
    )(*os_, *lses)


def _att_delta(att, datt, name):
    b, s, gw = att.shape
    ts, d = ATT_TILE, ATT_HEAD_DIM
    n_pat = len(ATT_DILATIONS)

    def body(a_ref, d_ref, *rest):
        do_outs, dl_outs, stage = rest[:n_pat], rest[n_pat:2 * n_pat], rest[-1]
        for j in range(ATT_HEADS_PER_GROUP):
            sl = slice(d * j, d * (j + 1))
            dv = d_ref[:, sl]
            delta = jnp.broadcast_to(jnp.sum(a_ref[:, sl] * dv, axis=-1, keepdims=True), (ts, d))
            for g, r in enumerate(ATT_DILATIONS):
                def store_do(ri, rows, out=do_outs[g]):
                    out[ri, :, sl] = rows.astype(BF16)

                def store_dl(ri, rows, out=dl_outs[g]):
                    out[ri, :, sl] = rows

                _to_residues(dv, stage, r, store_do)
                _to_residues(delta, stage, r, store_dl)

    row = pl.BlockSpec((None, ts, gw), lambda bi, i: (bi, i, 0))
    parts = [_by_residue_spec(r, gw) for r in ATT_DILATIONS]
    outs = pl.pallas_call(
        body, name=name, grid=(b, s // ts), in_specs=[row, row], out_specs=parts * 2,
        out_shape=[jax.ShapeDtypeStruct((b, r, s // r, gw), BF16) for r in ATT_DILATIONS]
        + [jax.ShapeDtypeStruct((b, r, s // r, gw), F32) for r in ATT_DILATIONS],
        scratch_shapes=[pltpu.VMEM((ts, d), F32)],
        compiler_params=_params("parallel", "parallel"),
    )(att, datt)
    return outs[:n_pat], outs[n_pat:]


def _att_bwd_q(qkr, datt, lse, delta, name):
    b, r, l, _ = qkr.shape
    nb = l // ATT_BLOCK
    d = ATT_HEAD_DIM

    def body(q_ref, kp_ref, k_ref, vp_ref, v_ref, do_ref, lse_ref, dl_ref, dq_ref):
        mask = _band_mask((ATT_BLOCK, 2 * ATT_BLOCK), 0, pl.program_id(2) > 0)
        for j in range(ATT_HEADS_PER_GROUP):
            sl = slice(d * j, d * (j + 1))
            kcat = jnp.concatenate([kp_ref[:, sl], k_ref[:, sl]], axis=0)
            vcat = jnp.concatenate([vp_ref[:, sl], v_ref[:, sl]], axis=0)
            sc = _nt(q_ref[:, sl], kcat) * ATT_SCALE
            pr = jnp.exp(jnp.where(mask, sc - lse_ref[:, d * j:d * j + 1], NEG_INF))
            dp = _nt(do_ref[:, sl], vcat)
            dsc = pr * (dp - dl_ref[:, d * j:d * j + 1])
            dq_ref[:, sl] = _nn(dsc, kcat) * ATT_SCALE

    tok = _att_spec(0, 0, nb)
    return pl.pallas_call(
        body, name=name, grid=(b, r, nb),
        in_specs=[_att_spec(0, 0, nb), _att_spec(1, -1, nb), _att_spec(1, 0, nb),
                  _att_spec(2, -1, nb), _att_spec(2, 0, nb), tok, tok, tok],
        out_specs=tok,
        out_shape=jax.ShapeDtypeStruct((b, r, l, ATT_OUT_DIM), F32),
        compiler_params=_params("parallel", "parallel", "parallel"),
    )(qkr, qkr, qkr, qkr, qkr, datt, lse, delta)


def _att_bwd_kv(qkr, datt, lse, delta, name):
    b, r, l, _ = qkr.shape
    nb = l // ATT_BLOCK
    d = ATT_HEAD_DIM

    def body(k_ref, v_ref, q_ref, qn_ref, do_ref, don_ref, lse_ref, lsen_ref, dl_ref, dln_ref, dk_ref, dv_ref):
        shape = (ATT_BLOCK, 2 * ATT_BLOCK)
        kj = lax.broadcasted_iota(jnp.int32, shape, 0)
        qi = lax.broadcasted_iota(jnp.int32, shape, 1)
        dist = qi - kj
        has_next = pl.program_id(2) < nb - 1
        mask = (dist >= 0) & (dist <= ATT_BLOCK) & (has_next | (qi < ATT_BLOCK))
        for j in range(ATT_HEADS_PER_GROUP):
            sl = slice(d * j, d * (j + 1))
            qcat = jnp.concatenate([q_ref[:, sl], qn_ref[:, sl]], axis=0)
            docat = jnp.concatenate([do_ref[:, sl], don_ref[:, sl]], axis=0)
            lse_t = jnp.concatenate([lse_ref[:, sl], lsen_ref[:, sl]], axis=0).T
            dl_t = jnp.concatenate([dl_ref[:, sl], dln_ref[:, sl]], axis=0).T
            sc_t = _nt(k_ref[:, sl], qcat) * ATT_SCALE
            pr_t = jnp.exp(jnp.where(mask, sc_t - lse_t, NEG_INF))
            dv_ref[:, sl] = _nn(pr_t, docat)
            dsc_t = pr_t * (_nt(v_ref[:, sl], docat) - dl_t)
            dk_ref[:, sl] = _nn(dsc_t, qcat) * ATT_SCALE

    tok, tok_n = _att_spec(0, 0, nb), _att_spec(0, 1, nb)
    return pl.pallas_call(
        body, name=name, grid=(b, r, nb),
        in_specs=[_att_spec(1, 0, nb), _att_spec(2, 0, nb), _att_spec(0, 0, nb), _att_spec(0, 1, nb),
                  tok, tok_n, tok, tok_n, tok, tok_n],
        out_specs=[tok, tok],
        out_shape=[jax.ShapeDtypeStruct((b, r, l, ATT_OUT_DIM), F32)] * 2,
        compiler_params=_params("parallel", "parallel", "parallel"),
    )(qkr, qkr, qkr, qkr, datt, datt, lse, lse, delta, delta)


def _mix_fwd(gl, bg, ys, ya, name):
    t, d = ys.shape
    tm = _pick(t, (512, 256, 128))

    def body(gl_ref, bg_ref, ys_ref, ya_ref, o_ref):
        g0 = _sigmoid(gl_ref[:, :d] + bg_ref[:, :d])
        g1 = _sigmoid(gl_ref[:, d:] + bg_ref[:, d:])
        o_ref[...] = (g0 * ys_ref[...] + g1 * ya_ref[...]).astype(BF16)

    row = pl.BlockSpec((tm, d), lambda i: (i, 0))
    return pl.pallas_call(
        body, name=name, grid=(t // tm,),
        in_specs=[pl.BlockSpec((tm, 2 * d), lambda i: (i, 0)), pl.BlockSpec((1, 2 * d), lambda i: (0, 0)), row, row],
        out_specs=row, out_shape=jax.ShapeDtypeStruct((t, d), BF16),
        compiler_params=_params("parallel"),
    )(gl, bg, ys, ya)


def _mix_bwd(gl, bg, ys, ya, dmixed, name):
    t, d = ys.shape
    tm = _pick(t, (512, 256, 128))

    def body(gl_ref, bg_ref, ys_ref, ya_ref, dm_ref, dys_ref, dya_ref, dgl_ref, dbg_ref):
        @pl.when(pl.program_id(0) == 0)
        def _():
            dbg_ref[...] = jnp.zeros_like(dbg_ref)

        dm = dm_ref[...]
        g0 = _sigmoid(gl_ref[:, :d] + bg_ref[:, :d])
        g1 = _sigmoid(gl_ref[:, d:] + bg_ref[:, d:])
        dys_ref[...] = (dm * g0).astype(BF16)
        dya_ref[...] = (dm * g1).astype(BF16)
        d0 = dm * ys_ref[...] * (g0 * (1.0 - g0))
        d1 = dm * ya_ref[...] * (g1 * (1.0 - g1))
        dgl_ref[:, :d] = d0.astype(BF16)
        dgl_ref[:, d:] = d1.astype(BF16)
        dbg_ref[:, :d] += jnp.sum(d0, axis=0, keepdims=True)
        dbg_ref[:, d:] += jnp.sum(d1, axis=0, keepdims=True)

    row = pl.BlockSpec((tm, d), lambda i: (i, 0))
    wide = pl.BlockSpec((tm, 2 * d), lambda i: (i, 0))
    vec = pl.BlockSpec((1, 2 * d), lambda i: (0, 0))
    return pl.pallas_call(
        body, name=name, grid=(t // tm,),
        in_specs=[wide, vec, row, row, row], out_specs=[row, row, wide, vec],
        out_shape=[jax.ShapeDtypeStruct((t, d), BF16), jax.ShapeDtypeStruct((t, d), BF16),
                   jax.ShapeDtypeStruct((t, 2 * d), BF16), jax.ShapeDtypeStruct((1, 2 * d), F32)],
        compiler_params=_params("arbitrary"),
    )(gl, bg, ys, ya, dmixed)


def _swiglu_fwd(gt, up, name):
    t, f = gt.shape
    tm = _pick(t, (512, 256, 128))

    def body(g_ref, u_ref, o_ref):
        gv = g_ref[...]
        o_ref[...] = ((gv * _sigmoid(gv)) * u_ref[...]).astype(BF16)

    row = pl.BlockSpec((tm, f), lambda i: (i, 0))
    return pl.pallas_call(
        body, name=name, grid=(t // tm,), in_specs=[row, row], out_specs=row,
        out_shape=jax.ShapeDtypeStruct((t, f), BF16), compiler_params=_params("parallel"),
    )(gt, up)


def _swiglu_bwd(gt, up, dact, name):
    t, f = gt.shape
    tm = _pick(t, (512, 256, 128))

    def body(g_ref, u_ref, d_ref, dg_ref, du_ref):
        gv = g_ref[...]
        dv = d_ref[...]
        sg = _sigmoid(gv)
        dg_ref[...] = (dv * u_ref[...] * (sg * (1.0 + gv * (1.0 - sg)))).astype(BF16)
        du_ref[...] = (dv * (gv * sg)).astype(BF16)

    row = pl.BlockSpec((tm, f), lambda i: (i, 0))
    return pl.pallas_call(
        body, name=name, grid=(t // tm,), in_specs=[row, row, row], out_specs=[row, row],
        out_shape=[jax.ShapeDtypeStruct((t, f), BF16)] * 2, compiler_params=_params("parallel"),
    )(gt, up, dact)


def _peer(k):
    x, y, c = lax.axis_index("x"), lax.axis_index("y"), lax.axis_index("c")
    px, py, pc = x ^ ((k >> 2) & 1), y ^ ((k >> 1) & 1), c ^ (k & 1)
    return (px, py, pc), 4 * px + 2 * py + pc


def _my_index():
    return 4 * lax.axis_index("x") + 2 * lax.axis_index("y") + lax.axis_index("c")


def _all_gather(parts, name):
    n_parts = len(parts)

    def body(*refs):
        ins, outs = refs[:n_parts], refs[n_parts:2 * n_parts]
        send_sems, recv_sems, local_sems = refs[2 * n_parts:]
        here, me = _peer(0)
        sibling, sib_idx = _peer(1)
        chips = [_peer(2 * q) for q in range(1, N_CHIPS)]

        def copy(i, k, block, to, src=None):
            return pltpu.make_async_remote_copy(
                src_ref=outs[i].at[block] if src is None else src, dst_ref=outs[i].at[block],
                send_sem=send_sems.at[i * (N_DEV - 1) + k], recv_sem=recv_sems.at[i * (N_DEV - 1) + k],
                device_id=to, device_id_type=MESH)

        local = [pltpu.make_async_copy(ins[i], outs[i].at[me], local_sems.at[i]) for i in range(n_parts)]
        for cp in local:
            cp.start()
        sends = []
        for i in range(n_parts):
            sends.append(copy(i, 0, me, sibling, src=ins[i]))
            sends += [copy(i, q, me, chip, src=ins[i]) for q, (chip, _) in enumerate(chips, start=1)]
        for cp in sends:
            cp.start()
        for q, (chip, chip_idx) in enumerate(chips, start=1):
            for i in range(n_parts):
                copy(i, q, chip_idx, here).wait_recv()
                fwd = copy(i, N_CHIPS - 1 + q, chip_idx, sibling)
                fwd.start()
                sends.append(fwd)
        for i in range(n_parts):
            copy(i, 0, sib_idx, here).wait_recv()
        for q, (_, chip_idx) in enumerate(chips, start=1):
            for i in range(n_parts):
                copy(i, N_CHIPS - 1 + q, chip_idx ^ 1, here).wait_recv()
        for cp in sends:
            cp.wait_send()
        for cp in local:
            cp.wait()

    hbm = pl.BlockSpec(memory_space=pl.ANY)
    return pl.pallas_call(
        body, name=name, in_specs=[hbm] * n_parts, out_specs=[hbm] * n_parts,
        out_shape=[jax.ShapeDtypeStruct((N_DEV,) + p_.shape, p_.dtype) for p_ in parts],
        scratch_shapes=[pltpu.SemaphoreType.DMA((n_parts * (N_DEV - 1),)),
                        pltpu.SemaphoreType.DMA((n_parts * (N_DEV - 1),)),
                        pltpu.SemaphoreType.DMA((n_parts,))],
        compiler_params=pltpu.CompilerParams(has_side_effects=True),
    )(*parts)


def _pair_exchange(slabs, name):
    def body(slab_ref, got_ref, send_sems, recv_sems):
        c = lax.axis_index("c")
        sibling, _ = _peer(1)
        copies = [pltpu.make_async_remote_copy(
            src_ref=slab_ref.at[2 * q + 1 - c], dst_ref=got_ref.at[q], send_sem=send_sems.at[q],
            recv_sem=recv_sems.at[q], device_id=sibling, device_id_type=MESH) for q in range(N_CHIPS)]
        for cp in copies:
            cp.start()
        for cp in copies:
            cp.wait()

    hbm = pl.BlockSpec(memory_space=pl.ANY)
    return pl.pallas_call(
        body, name=name, in_specs=[hbm], out_specs=hbm,
        out_shape=jax.ShapeDtypeStruct((N_CHIPS,) + slabs.shape[1:], slabs.dtype),
        scratch_shapes=[pltpu.SemaphoreType.DMA((N_CHIPS,)), pltpu.SemaphoreType.DMA((N_CHIPS,))],
        compiler_params=pltpu.CompilerParams(has_side_effects=True),
    )(slabs)


def _chip_sum(slabs, got, core, name):
    _, rows, lanes = slabs.shape
    tr = _pick(rows, (512, 256, 128, 64, 32, 16, 8))

    def body(core_ref, mine_ref, got_ref, o_ref):
        o_ref[...] = (mine_ref[...] + got_ref[...]).astype(BF16)

    return pl.pallas_call(
        body, name=name,
        grid_spec=pltpu.PrefetchScalarGridSpec(
            num_scalar_prefetch=1, grid=(N_CHIPS, rows // tr),
            in_specs=[pl.BlockSpec((None, tr, lanes), lambda q, i, core_ref: (2 * q + core_ref[0], i, 0)),
                      pl.BlockSpec((None, tr, lanes), lambda q, i, core_ref: (q, i, 0))],
            out_specs=pl.BlockSpec((None, tr, lanes), lambda q, i, core_ref: (q, i, 0))),
        out_shape=jax.ShapeDtypeStruct((N_CHIPS, rows, lanes), BF16),
        compiler_params=_params("parallel", "parallel"),
    )(core, slabs, got)


def _chip_exchange(chip_sums, shared, name):
    def body(sum_ref, sh_ref, got_ref, gsh_ref, send_sems, recv_sems, sh_send_sems, sh_recv_sems, local_sems):
        me = _my_index()
        my_chip = me >> 1
        local = [pltpu.make_async_copy(sum_ref.at[my_chip], got_ref.at[my_chip], local_sems.at[0]),
                 pltpu.make_async_copy(sh_ref, gsh_ref.at[me], local_sems.at[1])]
        for cp in local:
            cp.start()
        sends = []
        for q in range(1, N_CHIPS):
            peer, pidx = _peer(2 * q)
            cp = pltpu.make_async_remote_copy(
                src_ref=sum_ref.at[pidx >> 1], dst_ref=got_ref.at[my_chip], send_sem=send_sems.at[q - 1],
                recv_sem=recv_sems.at[q - 1], device_id=peer, device_id_type=MESH)
            cp.start()
            sends.append(cp)
        for k in range(1, N_DEV):
            peer, _ = _peer(k)
            cp = pltpu.make_async_remote_copy(
                src_ref=sh_ref, dst_ref=gsh_ref.at[me], send_sem=sh_send_sems.at[k - 1],
                recv_sem=sh_recv_sems.at[k - 1], device_id=peer, device_id_type=MESH)
            cp.start()
            sends.append(cp)
        for q in range(1, N_CHIPS):
            peer, pidx = _peer(2 * q)
            pltpu.make_async_remote_copy(
                src_ref=sum_ref.at[my_chip], dst_ref=got_ref.at[pidx >> 1], send_sem=send_sems.at[q - 1],
                recv_sem=recv_sems.at[q - 1], device_id=peer, device_id_type=MESH).wait_recv()
        for k in range(1, N_DEV):
            peer, pidx = _peer(k)
            pltpu.make_async_remote_copy(
                src_ref=sh_ref, dst_ref=gsh_ref.at[pidx], send_sem=sh_send_sems.at[k - 1],
                recv_sem=sh_recv_sems.at[k - 1], device_id=peer, device_id_type=MESH).wait_recv()
        for cp in sends:
            cp.wait_send()
        for cp in local:
            cp.wait()

    hbm = pl.BlockSpec(memory_space=pl.ANY)
    return pl.pallas_call(
        body, name=name, in_specs=[hbm, hbm], out_specs=[hbm, hbm],
        out_shape=[jax.ShapeDtypeStruct(chip_sums.shape, chip_sums.dtype),
                   jax.ShapeDtypeStruct((N_DEV,) + shared.shape, shared.dtype)],
        scratch_shapes=[pltpu.SemaphoreType.DMA((N_CHIPS - 1,)), pltpu.SemaphoreType.DMA((N_CHIPS - 1,)),
                        pltpu.SemaphoreType.DMA((N_DEV - 1,)), pltpu.SemaphoreType.DMA((N_DEV - 1,)),
                        pltpu.SemaphoreType.DMA((2,))],
        compiler_params=pltpu.CompilerParams(has_side_effects=True),
    )(chip_sums, shared)


def _adamw(parts, w, m, v, name):
    n_parts, rows, lanes = parts.shape
    tr = _pick(rows, (512, 256, 128, 64, 32, 16, 8))
    c1 = 1.0 - ADAM_B1 ** ADAM_STEP
    c2 = 1.0 - ADAM_B2 ** ADAM_STEP

    def body(p_ref, w_ref, m_ref, v_ref, g_ref, d_ref, nm_ref, nv_ref):
        g = p_ref[0].astype(F32)
        for j in range(1, n_parts):
            g = g + p_ref[j].astype(F32)
        nm = ADAM_B1 * m_ref[...] + (1.0 - ADAM_B1) * g
        nv = ADAM_B2 * v_ref[...] + (1.0 - ADAM_B2) * (g * g)
        g_ref[...] = g
        nm_ref[...] = nm
        nv_ref[...] = nv
        d_ref[...] = -ADAM_LR * ((nm / c1) / (jnp.sqrt(nv / c2) + ADAM_EPS) + ADAM_WD * w_ref[...])

    row = pl.BlockSpec((tr, lanes), lambda i: (i, 0))
    return pl.pallas_call(
        body, name=name, grid=(rows // tr,),
        in_specs=[pl.BlockSpec((n_parts, tr, lanes), lambda i: (0, i, 0)), row, row, row],
        out_specs=[row] * 4, out_shape=[jax.ShapeDtypeStruct((rows, lanes), F32)] * 4,
        compiler_params=_params("parallel"),
    )(parts, w, m, v)


MATRIX_SHARDS = (
    ("w_in", (D_MODEL, IN_PROJ_DIM // N_DEV), True),
    ("w_ssm_out", (SSM_D_INNER // N_DEV, D_MODEL), False),
    ("w_att_out", (ATT_OUT_DIM, D_MODEL // N_DEV), True),
    ("w_mix_out", (D_MODEL // N_DEV, D_MODEL), False),
    ("w_ffn_gate", (D_MODEL, D_FF // N_DEV), True),
    ("w_ffn_up", (D_MODEL, D_FF // N_DEV), True),
    ("w_ffn_down", (D_FF // N_DEV, D_MODEL), False),
)
CONV_SHARD = ("conv_w", (SSM_CONV, SSM_CONV_DIM // N_DEV), True)
SHARDED = MATRIX_SHARDS + (CONV_SHARD,)
REPLICATED = (("norm_mix", D_MODEL), ("b_gate", 2 * D_MODEL), ("conv_b", SSM_CONV_DIM), ("dt_bias", SSM_N_HEADS),
              ("a_log", SSM_N_HEADS), ("d_skip", SSM_N_HEADS), ("ssm_norm", SSM_D_INNER), ("norm_ffn", D_MODEL),
              ("norm_final", D_MODEL))


PACK_ROWS = 512


def _round_up(n, mult):
    return -(-n // mult) * mult


def _pack_rows(flat, row_mult):
    rows = _round_up(-(-flat.shape[0] // LANES), row_mult)
    return jnp.pad(flat, (0, rows * LANES - flat.shape[0])).reshape(rows, LANES)


def _pack_sharded(vals, specs, row_mult, dtype):
    return _pack_rows(jnp.concatenate([vals[name].reshape(-1).astype(dtype) for name, _, _ in specs]), row_mult)


def _unpack_sharded(packed, specs, lead=()):
    flat = packed.reshape(lead + (-1,))
    out, off = {}, 0
    for name, shape, _ in specs:
        size = shape[0] * shape[1]
        out[name] = flat[..., off:off + size].reshape(lead + shape)
        off += size
    return out


def _stacking(specs):
    return tuple((name, (shape[1], shape[0]) if by_cols else shape, by_cols) for name, shape, by_cols in specs)


def _to_stacking(vals, specs):
    return {name: (vals[name].T if by_cols else vals[name]) for name, _, by_cols in specs}


def _pack_replicated(vals):
    rows = []
    for name, size in REPLICATED:
        v = vals[name].reshape(-1).astype(F32)
        rows.append(jnp.pad(v, (0, _round_up(size, LANES) - size)))
    return _pack_rows(jnp.concatenate(rows), 8)


def _unpack_replicated(packed, shapes):
    flat = packed.reshape(-1)
    out, off = {}, 0
    for name, size in REPLICATED:
        out[name] = flat[off:off + size].reshape(shapes[name])
        off += _round_up(size, LANES)
    return out


def _lane_row(v):
    v = v.reshape(-1).astype(F32)
    return jnp.pad(v, (0, LANES - v.shape[0])).reshape(1, LANES)


IN_SPLIT = (("z", SSM_D_INNER), ("xbc", SSM_CONV_DIM), ("dt", SSM_N_HEADS), ("qkv", ATT_QKV_DIM), ("gate", 2 * D_MODEL))


def _split_w_in(w_t):
    out, off = {}, 0
    for name, size in IN_SPLIT:
        out[name] = w_t[off:off + size]
        off += size
    out["dt"] = jnp.pad(out["dt"], ((0, DT_PAD - SSM_N_HEADS), (0, 0)))
    return out


def _join_w_in(parts):
    parts = dict(parts)
    parts["dt"] = parts["dt"][:SSM_N_HEADS]
    return jnp.concatenate([parts[name] for name, _ in IN_SPLIT], axis=0)


def kernel(x, norm_mix, w_in, b_gate, conv_w, conv_b, dt_bias, a_log, d_skip, ssm_norm, w_ssm_out, w_att_out, w_mix_out, norm_ffn, w_ffn_gate, w_ffn_up, w_ffn_down, norm_final, loss_target, m_norm_mix, m_w_in, m_b_gate, m_conv_w, m_conv_b, m_dt_bias, m_a_log, m_d_skip, m_ssm_norm, m_w_ssm_out, m_w_att_out, m_w_mix_out, m_norm_ffn, m_w_ffn_gate, m_w_ffn_up, m_w_ffn_down, m_norm_final, v_norm_mix, v_w_in, v_b_gate, v_conv_w, v_conv_b, v_dt_bias, v_a_log, v_d_skip, v_ssm_norm, v_w_ssm_out, v_w_att_out, v_w_mix_out, v_norm_ffn, v_w_ffn_gate, v_w_ffn_up, v_w_ffn_down, v_norm_final):
    given = dict(locals())
    weights = {name: given[name][0] for name, _, _ in SHARDED}
    b, s, d = x.shape
    t = b * s

    mat_specs, conv_specs, all_specs = _stacking(MATRIX_SHARDS), _stacking((CONV_SHARD,)), _stacking(SHARDED)
    stacking = _to_stacking(weights, SHARDED)
    mat_local = _pack_sharded(stacking, mat_specs, 16, BF16)
    conv_local = _pack_sharded(stacking, conv_specs, 8, F32)
    mat_all, conv_all = _all_gather([mat_local, conv_local], "weights_all_gather")
    shards = _unpack_sharded(mat_all, mat_specs, (N_DEV,))
    shards.update(_unpack_sharded(conv_all, conv_specs, (N_DEV,)))
    full = {name: shards[name].reshape(N_DEV * shape[0], shape[1]) for name, shape, _ in all_specs}
    w_sec = _split_w_in(full["w_in"])
    conv_taps = full["conv_w"].T

    g_mix, g_ffn, g_fin = norm_mix.reshape(1, d), norm_ffn.reshape(1, d), norm_final.reshape(1, d)
    bg_row = b_gate.reshape(1, 2 * d)
    convb_row = conv_b.reshape(1, SSM_CONV_DIM)
    ssmn_row = ssm_norm.reshape(1, SSM_D_INNER)
    dtb_row, alog_row = _lane_row(dt_bias), _lane_row(a_log)
    cosf, sinf = _rope_tables(s)

    x2d = x.reshape(t, d)
    h1 = _rmsnorm_fwd(x2d, g_mix, "norm_mix_fwd")
    proj = {name: _mm(h1, w_sec[name], mode="nt", name="in_proj_" + name) for name, _ in IN_SPLIT}
    xbc3 = proj["xbc"].reshape(b, s, SSM_CONV_DIM)
    xc = _conv_fwd(xbc3, conv_taps, convb_row, "conv_fwd")
    dtr3 = proj["dt"].reshape(b, s, DT_PAD)
    to_channels, to_heads = _head_masks()
    dskx = jnp.repeat(d_skip.reshape(-1).astype(F32), SSM_HEAD_DIM).reshape(1, SSM_D_INNER)
    y_ssd, h_states = _ssd_fwd(xc, dtr3, dtb_row, alog_row, dskx, to_channels, "ssd_fwd")
    y_ssd2 = y_ssd.reshape(t, SSM_D_INNER)
    ynorm = _gate_norm_fwd(y_ssd2, proj["z"], ssmn_row, "ssd_gate_norm_fwd")
    y_ssm = _mm(ynorm, full["w_ssm_out"], mode="nn", name="ssm_out_proj")

    qkv3 = proj["qkv"].reshape(b, s, ATT_QKV_DIM)
    qk_parts = _rope_fwd(qkv3, cosf, sinf, "rope_fwd")
    att_parts = [_att_fwd(qk_parts[gi], "att_fwd_%d" % r) for gi, r in enumerate(ATT_DILATIONS)]
    att, *lse_parts = _att_merge([o for o, _ in att_parts], [l_ for _, l_ in att_parts], "att_merge")
    att2 = att.reshape(t, ATT_OUT_DIM)
    y_att = _mm(att2, full["w_att_out"], mode="nt", name="att_out_proj")

    mixed = _mix_fwd(proj["gate"], bg_row, y_ssm, y_att, "mix_fwd")
    x2 = _mm(mixed, full["w_mix_out"], mode="nn", name="mix_out_proj", add=x2d)
    h2 = _rmsnorm_fwd(x2, g_ffn, "norm_ffn_fwd")
    gt = _mm(h2, full["w_ffn_gate"], mode="nt", name="ffn_gate_proj")
    up = _mm(h2, full["w_ffn_up"], mode="nt", name="ffn_up_proj")
    act = _swiglu_fwd(gt, up, "swiglu_fwd")
    x3 = _mm(act, full["w_ffn_down"], mode="nn", name="ffn_down_proj", add=x2)

    loss_row, dx3, dg_fin, dx3b = _loss_head(x3, g_fin, loss_target.reshape(t, d), "loss_head")
    grads = {}
    dact = _mm(dx3b, full["w_ffn_down"], mode="nt", name="ffn_down_dx")
    grads["w_ffn_down"] = _mm(act, dx3b, mode="tn", name="ffn_down_dw")
    dgt, dup = _swiglu_bwd(gt, up, dact, "swiglu_bwd")
    grads["w_ffn_gate"] = _mm(dgt, h2, mode="tn", name="ffn_gate_dw")
    grads["w_ffn_up"] = _mm(dup, h2, mode="tn", name="ffn_up_dw")
    dh2 = _mm(dgt, full["w_ffn_gate"], mode="nn", name="ffn_gate_dx")
    dh2 = _mm(dup, full["w_ffn_up"], mode="nn", name="ffn_up_dx", add=dh2)
    dx2, dg_ffn, dx2b = _rmsnorm_bwd(x2, g_ffn, dh2, dx3, "norm_ffn_bwd", with_bf16=True)

    dmixed = _mm(dx2b, full["w_mix_out"], mode="nt", name="mix_out_dx")
    grads["w_mix_out"] = _mm(mixed, dx2b, mode="tn", name="mix_out_dw")
    dys, dya, dgl, dbg = _mix_bwd(proj["gate"], bg_row, y_ssm, y_att, dmixed, "mix_bwd")

    grads["w_ssm_out"] = _mm(ynorm, dys, mode="tn", name="ssm_out_dw")
    dynorm = _mm(dys, full["w_ssm_out"], mode="nt", name="ssm_out_dx")
    dy_ssd, dz, dssmn = _gate_norm_bwd(y_ssd2, proj["z"], ssmn_row, dynorm, "ssd_gate_norm_bwd")
    dxc, ddtr, dalog, ddsk, ddtb = _ssd_bwd(xc, dtr3, dy_ssd.reshape(b, s, SSM_D_INNER), h_states,
                                            dtb_row, alog_row, dskx, to_channels, to_heads, "ssd_bwd")
    dxbc, dconvw, dconvb = _conv_bwd(xbc3, dxc, conv_taps, convb_row, "conv_bwd")
    grads["conv_w"] = dconvw.T

    grads["w_att_out"] = _mm(dya, att2, mode="tn", name="att_out_dw")
    datt = _mm(dya, full["w_att_out"], mode="nn", name="att_out_dx").reshape(b, s, ATT_OUT_DIM)
    do_parts, dl_parts = _att_delta(att, datt, "att_delta")
    dqs, dks, dvs = [], [], []
    for gi, r in enumerate(ATT_DILATIONS):
        operands = (qk_parts[gi], do_parts[gi], lse_parts[gi], dl_parts[gi])
        dqs.append(_att_bwd_q(*operands, "att_bwd_q_%d" % r))
        dk_g, dv_g = _att_bwd_kv(*operands, "att_bwd_kv_%d" % r)
        dks.append(dk_g)
        dvs.append(dv_g)
    dqkv = _rope_bwd(dqs, dks, dvs, cosf, sinf, "rope_bwd")

    dproj = {"z": dz, "xbc": dxbc.reshape(t, SSM_CONV_DIM), "dt": ddtr.reshape(t, DT_PAD),
             "qkv": dqkv.reshape(t, ATT_QKV_DIM), "gate": dgl}
    grads["w_in"] = _join_w_in({name: _mm(dproj[name], h1, mode="tn", name="in_proj_dw_" + name)
                                for name, _ in IN_SPLIT})
    k_all = sum(dproj[name].shape[1] for name, _ in IN_SPLIT)
    k_pad = _round_up(k_all, 2048) - k_all
    dproj_all = jnp.concatenate([dproj[name] for name, _ in IN_SPLIT] + [jnp.zeros((t, k_pad), BF16)], axis=1)
    w_in_all = jnp.concatenate([w_sec[name] for name, _ in IN_SPLIT] + [jnp.zeros((k_pad, d), BF16)], axis=0)
    dh1 = _mm(dproj_all, w_in_all, mode="nn", name="in_proj_dx")
    grad_x, dg_mix = _rmsnorm_bwd(x2d, g_mix, dh1, dx2, "norm_mix_bwd")

    slabs = jnp.concatenate([grads[name].reshape(N_DEV, -1) for name, _, _ in all_specs], axis=1)
    slab_rows = _round_up(-(-slabs.shape[1] // LANES), PACK_ROWS)
    slabs = jnp.pad(slabs, ((0, 0), (0, slab_rows * LANES - slabs.shape[1]))).reshape(N_DEV, slab_rows, LANES)
    small = {"norm_mix": dg_mix, "b_gate": dbg, "conv_b": dconvb, "dt_bias": ddtb[:, :SSM_N_HEADS],
             "a_log": dalog[:, :SSM_N_HEADS], "d_skip": ddsk[:, :SSM_N_HEADS], "ssm_norm": dssmn,
             "norm_ffn": dg_ffn, "norm_final": dg_fin}
    core = lax.axis_index("c").astype(jnp.int32).reshape(1)
    chip_sums = _chip_sum(slabs, _pair_exchange(slabs, "grad_pair_exchange"), core, "grad_chip_sum")
    got, got_small = _chip_exchange(chip_sums, _pack_replicated(small), "grad_chip_exchange")

    def packed(prefix):
        vals = _to_stacking({name: given[prefix + name][0] for name, _, _ in SHARDED}, SHARDED)
        rep = {name: given[prefix + name] for name, _ in REPLICATED}
        return _pack_sharded(vals, all_specs, PACK_ROWS, F32), _pack_replicated(rep)

    (w_big, w_small), (m_big, m_small), (v_big, v_small) = packed(""), packed("m_"), packed("v_")
    big = _adamw(got, w_big, m_big, v_big, "adamw_sharded")
    sml = _adamw(got_small, w_small, m_small, v_small, "adamw_replicated")

    loss = lax.psum(loss_row[0, 0], ("x", "y", "c"))
    outs = [loss, grad_x.reshape(b, s, d)]
    rep_shapes = {name: given[name].shape for name, _ in REPLICATED}
    order = ["norm_mix", "w_in", "b_gate", "conv_w", "conv_b", "dt_bias", "a_log", "d_skip", "ssm_norm", "w_ssm_out",
             "w_att_out", "w_mix_out", "norm_ffn", "w_ffn_gate", "w_ffn_up", "w_ffn_down", "norm_final"]
    for big_k, sml_k in zip(big, sml):
        sharded = _to_stacking(_unpack_sharded(big_k, all_specs), SHARDED)
        rep = _unpack_replicated(sml_k, rep_shapes)
        for name in order:
            outs.append(sharded[name][None] if name in sharded else rep[name])
    return tuple(outs)
```

```python
import functools
import math

import jax
import jax.numpy as jnp
from jax import lax
from jax.experimental import pallas as pl
from jax.experimental.pallas import tpu as pltpu

F32 = jnp.float32
BF16 = jnp.bfloat16

N_DEV = 8
N_CHIPS = 4
D_MODEL = 1024
SSM_D_INNER = 2048
SSM_HEAD_DIM = 64
SSM_N_HEADS = 32
SSM_N_GROUPS = 4
SSM_HEADS_PER_GROUP = SSM_N_HEADS // SSM_N_GROUPS
SSM_D_STATE = 128
SSM_CONV = 4
SSM_CHUNK = 128
SSM_CONV_DIM = 3072
ATT_HEAD_DIM = 128
ATT_HEADS_PER_GROUP = 4
ATT_DILATIONS = (1, 4, 16)
ATT_N_HEADS = 12
ATT_QKV_DIM = 4608
ATT_OUT_DIM = 512
ATT_BLOCK = 128
ROPE_THETA = 10000.0
D_FF = 2816
IN_PROJ_DIM = 11808
EPS = 1e-6
LANES = 128
DT_PAD = LANES

ADAM_LR = 0.001
ADAM_B1 = 0.9
ADAM_B2 = 0.999
ADAM_EPS = 1e-08
ADAM_WD = 0.01
ADAM_STEP = 10

VMEM_LIMIT = 56 * 1024 * 1024
MESH = pl.DeviceIdType.MESH
NEG_INF = float("-inf")


def _pick(n, candidates):
    for c in candidates:
        if n % c == 0:
            return c
    return n


def _params(*sem):
    return pltpu.CompilerParams(dimension_semantics=sem, vmem_limit_bytes=VMEM_LIMIT)


def _sigmoid(x):
    return 1.0 / (1.0 + jnp.exp(-x))


def _softplus(x):
    return jnp.maximum(x, 0.0) + jnp.log(1.0 + jnp.exp(-jnp.abs(x)))


def _dot(a, b, dims):
    return lax.dot_general(a.astype(BF16), b.astype(BF16), (dims, ((), ())), preferred_element_type=F32)


def _nn(a, b):
    return _dot(a, b, ((1,), (0,)))


def _nt(a, b):
    return _dot(a, b, ((1,), (1,)))


def _tn(a, b):
    return _dot(a, b, ((0,), (0,)))


def _split3(v):
    hi = v.astype(BF16)
    r1 = v - hi.astype(F32)
    mid = r1.astype(BF16)
    lo = (r1 - mid.astype(F32)).astype(BF16)
    return hi, mid, lo


def _mask_nn(mask, v):
    mb = mask.astype(BF16)
    hi, mid, lo = _split3(v)
    return _nn(mb, hi) + (_nn(mb, mid) + _nn(mb, lo))


MM_VMEM_BUDGET = 40 * 1024 * 1024
MM_FULL_K = 2816


def _mm_tiles(m, n, k, a_bytes, b_bytes, o_bytes, has_add):
    tk = k if k <= MM_FULL_K else _pick(k, (2048, 1024, 512, 256, 128))
    tn = 1408 if (n > 1024 and n % 1408 == 0) else _pick(n, (1024, 768, 512, 384, 256, 128))
    for tm in (1408, 1024, 768, 512, 384, 256, 128):
        if m % tm:
            continue
        buffers = 2 * (tm * tk * a_bytes + tk * tn * b_bytes + tm * tn * (o_bytes + (4 if has_add else 0)))
        if tk < k:
            buffers += tm * tn * 4
        if buffers <= MM_VMEM_BUDGET:
            return tm, tn, tk
    return _pick(m, (128,)), tn, tk


def _mm(a, b, *, mode, name, out_dtype=F32, add=None):
    if mode == "nn":
        (m, k), n = a.shape, b.shape[1]
    elif mode == "nt":
        (m, k), n = a.shape, b.shape[0]
    else:
        (k, m), n = a.shape, b.shape[1]
    has_add = add is not None
    tm, tn, tk = _mm_tiles(m, n, k, a.dtype.itemsize, b.dtype.itemsize, jnp.dtype(out_dtype).itemsize, has_add)
    nk = k // tk
    dims = {"nn": ((1,), (0,)), "nt": ((1,), (1,)), "tn": ((0,), (0,))}[mode]
    a_spec = {"nn": pl.BlockSpec((tm, tk), lambda i, j, kk: (i, kk)),
              "nt": pl.BlockSpec((tm, tk), lambda i, j, kk: (i, kk)),
              "tn": pl.BlockSpec((tk, tm), lambda i, j, kk: (kk, i))}[mode]
    b_spec = {"nn": pl.BlockSpec((tk, tn), lambda i, j, kk: (kk, j)),
              "nt": pl.BlockSpec((tn, tk), lambda i, j, kk: (j, kk)),
              "tn": pl.BlockSpec((tk, tn), lambda i, j, kk: (kk, j))}[mode]
    o_spec = pl.BlockSpec((tm, tn), lambda i, j, kk: (i, j))

    def finish(r, c_ref, o_ref):
        if has_add:
            r = r + c_ref[...]
        o_ref[...] = r.astype(out_dtype)

    def body_one(*refs):
        a_ref, b_ref = refs[:2]
        finish(_dot(a_ref[...], b_ref[...], dims), refs[2] if has_add else None, refs[-1])

    def body_acc(*refs):
        a_ref, b_ref = refs[:2]
        o_ref, acc = refs[-2:]
        kk = pl.program_id(2)

        @pl.when(kk == 0)
        def _():
            acc[...] = jnp.zeros_like(acc)

        acc[...] += _dot(a_ref[...], b_ref[...], dims)

        @pl.when(kk == nk - 1)
        def _():
            finish(acc[...], refs[2] if has_add else None, o_ref)

    in_specs = [a_spec, b_spec] + ([o_spec] if has_add else [])
    args = (a, b) + ((add,) if has_add else ())
    return pl.pallas_call(
        body_one if nk == 1 else body_acc, name=name, grid=(m // tm, n // tn, nk),
        in_specs=in_specs, out_specs=o_spec,
        out_shape=jax.ShapeDtypeStruct((m, n), out_dtype),
        scratch_shapes=[] if nk == 1 else [pltpu.VMEM((tm, tn), F32)],
        compiler_params=_params("parallel", "parallel", "arbitrary"),
    )(*args)


def _rmsnorm_fwd(x, g, name):
    t, d = x.shape
    tm = _pick(t, (512, 256, 128))

    def body(x_ref, g_ref, o_ref):
        xv = x_ref[...]
        r = lax.rsqrt(jnp.mean(xv * xv, axis=-1, keepdims=True) + EPS)
        o_ref[...] = ((xv * r) * g_ref[...]).astype(BF16)

    return pl.pallas_call(
        body, name=name, grid=(t // tm,),
        in_specs=[pl.BlockSpec((tm, d), lambda i: (i, 0)), pl.BlockSpec((1, d), lambda i: (0, 0))],
        out_specs=pl.BlockSpec((tm, d), lambda i: (i, 0)),
        out_shape=jax.ShapeDtypeStruct((t, d), BF16),
        compiler_params=_params("parallel"),
    )(x, g)


def _rmsnorm_bwd(x, g, dh, dres, name, with_bf16=False):
    t, d = x.shape
    tm = _pick(t, (512, 256, 128))

    def body(x_ref, g_ref, dh_ref, dres_ref, dx_ref, dg_ref, *dxb_ref):
        @pl.when(pl.program_id(0) == 0)
        def _():
            dg_ref[...] = jnp.zeros_like(dg_ref)

        xv = x_ref[...]
        r = lax.rsqrt(jnp.mean(xv * xv, axis=-1, keepdims=True) + EPS)
        xhat = xv * r
        dhv = dh_ref[...]
        dyg = dhv * g_ref[...]
        dx = dres_ref[...] + r * (dyg - xhat * jnp.mean(dyg * xhat, axis=-1, keepdims=True))
        dx_ref[...] = dx
        if with_bf16:
            dxb_ref[0][...] = dx.astype(BF16)
        dg_ref[...] += jnp.sum(dhv * xhat, axis=0, keepdims=True)

    row = pl.BlockSpec((tm, d), lambda i: (i, 0))
    vec = pl.BlockSpec((1, d), lambda i: (0, 0))
    extra = with_bf16 * [jax.ShapeDtypeStruct((t, d), BF16)]
    return pl.pallas_call(
        body, name=name, grid=(t // tm,),
        in_specs=[row, vec, row, row], out_specs=[row, vec] + with_bf16 * [row],
        out_shape=[jax.ShapeDtypeStruct((t, d), F32), jax.ShapeDtypeStruct((1, d), F32)] + extra,
        compiler_params=_params("arbitrary"),
    )(x, g, dh, dres)


def _loss_head(x, g, target, name):
    t, d = x.shape
    tm = _pick(t, (512, 256, 128))

    def body(x_ref, g_ref, t_ref, loss_ref, dx_ref, dg_ref, dxb_ref):
        @pl.when(pl.program_id(0) == 0)
        def _():
            dg_ref[...] = jnp.zeros_like(dg_ref)
            loss_ref[...] = jnp.zeros_like(loss_ref)

        xv = x_ref[...]
        gv = g_ref[...]
        r = lax.rsqrt(jnp.mean(xv * xv, axis=-1, keepdims=True) + EPS)
        xhat = xv * r
        err = xhat * gv - t_ref[...]
        loss_ref[...] += jnp.sum(err * err) * (0.5 / d)
        dy = err * (1.0 / d)
        dyg = dy * gv
        dx = r * (dyg - xhat * jnp.mean(dyg * xhat, axis=-1, keepdims=True))
        dx_ref[...] = dx
        dxb_ref[...] = dx.astype(BF16)
        dg_ref[...] += jnp.sum(dy * xhat, axis=0, keepdims=True)

    row = pl.BlockSpec((tm, d), lambda i: (i, 0))
    vec = pl.BlockSpec((1, d), lambda i: (0, 0))
    return pl.pallas_call(
        body, name=name, grid=(t // tm,),
        in_specs=[row, vec, row],
        out_specs=[pl.BlockSpec((1, LANES), lambda i: (0, 0)), row, vec, row],
        out_shape=[jax.ShapeDtypeStruct((1, LANES), F32), jax.ShapeDtypeStruct((t, d), F32),
                   jax.ShapeDtypeStruct((1, d), F32), jax.ShapeDtypeStruct((t, d), BF16)],
        compiler_params=_params("arbitrary"),
    )(x, g, target)


CONV_HALO = 8
CONV_ROWS = 32


def _conv_pre_activation(u_ext, wv, bv, rows):
    acc = bv + wv[SSM_CONV - 1:SSM_CONV, :] * u_ext[CONV_HALO:CONV_HALO + rows, :]
    for sh in range(1, SSM_CONV):
        kidx = SSM_CONV - 1 - sh
        acc = acc + wv[kidx:kidx + 1, :] * u_ext[CONV_HALO - sh:CONV_HALO - sh + rows, :]
    return acc


def _conv_fwd(u, w, bias, name):
    b, s, c = u.shape
    ts = _pick(s, (512, 256, 128))
    cb = _pick(c, (512, 384, 256, 128))
    hb = ts // CONV_HALO

    def body(u_ref, h_ref, w_ref, b_ref, o_ref, ext):
        ext[0:CONV_HALO, :] = jnp.where(pl.program_id(1) == 0, 0.0, h_ref[...])
        ext[CONV_HALO:, :] = u_ref[...]
        wv, bv = w_ref[...], b_ref[...]

        def step(ci, carry):
            r0 = pl.multiple_of(ci * CONV_ROWS, CONV_HALO)
            acc = _conv_pre_activation(ext[pl.ds(r0, CONV_ROWS + CONV_HALO), :], wv, bv, CONV_ROWS)
            o_ref[pl.ds(r0, CONV_ROWS), :] = acc * _sigmoid(acc)
            return carry

        lax.fori_loop(0, ts // CONV_ROWS, step, 0)

    return pl.pallas_call(
        body, name=name, grid=(b, s // ts, c // cb),
        in_specs=[pl.BlockSpec((None, ts, cb), lambda bi, i, j: (bi, i, j)),
                  pl.BlockSpec((None, CONV_HALO, cb), lambda bi, i, j: (bi, jnp.maximum(i * hb - 1, 0), j)),
                  pl.BlockSpec((SSM_CONV, cb), lambda bi, i, j: (0, j)),
                  pl.BlockSpec((1, cb), lambda bi, i, j: (0, j))],
        out_specs=pl.BlockSpec((None, ts, cb), lambda bi, i, j: (bi, i, j)),
        out_shape=jax.ShapeDtypeStruct((b, s, c), F32),
        scratch_shapes=[pltpu.VMEM((CONV_HALO + ts, cb), F32)],
        compiler_params=_params("parallel", "parallel", "parallel"),
    )(u, u, w, bias)


def _conv_bwd(u, dout, w, bias, name):
    b, s, c = u.shape
    ts = _pick(s, (512, 256, 128))
    cb = _pick(c, (256, 128))
    hb = ts // CONV_HALO
    n_t = s // ts
    rows = CONV_ROWS

    def body(u_ref, up_ref, un_ref, d_ref, dn_ref, w_ref, b_ref, du_ref, dw_ref, db_ref, ext_u, ext_d):
        i = pl.program_id(2)

        @pl.when(jnp.logical_and(pl.program_id(1) == 0, i == 0))
        def _():
            dw_ref[...] = jnp.zeros_like(dw_ref)
            db_ref[...] = jnp.zeros_like(db_ref)

        ext_u[0:CONV_HALO, :] = jnp.where(i == 0, 0.0, up_ref[...])
        ext_u[CONV_HALO:CONV_HALO + ts, :] = u_ref[...]
        ext_u[CONV_HALO + ts:, :] = un_ref[...]
        ext_d[0:ts, :] = d_ref[...]
        ext_d[ts:, :] = jnp.where(i == n_t - 1, 0.0, dn_ref[...])
        wv, bv = w_ref[...], b_ref[...]

        def fold(v):
            return jnp.sum(v.reshape(rows // CONV_HALO, CONV_HALO, cb), axis=0)

        def step(ci, carry):
            r0 = pl.multiple_of(ci * rows, CONV_HALO)
            u_ext = ext_u[pl.ds(r0, rows + 2 * CONV_HALO), :]
            acc = _conv_pre_activation(u_ext, wv, bv, rows + CONV_HALO)
            sg = _sigmoid(acc)
            dpre = ext_d[pl.ds(r0, rows + CONV_HALO), :] * (sg * (1.0 + acc * (1.0 - sg)))
            d0 = dpre[0:rows, :]
            du = wv[SSM_CONV - 1:SSM_CONV, :] * d0
            sums = [None] * SSM_CONV
            sums[SSM_CONV - 1] = fold(d0 * u_ext[CONV_HALO:CONV_HALO + rows, :])
            for sh in range(1, SSM_CONV):
                kidx = SSM_CONV - 1 - sh
                du = du + wv[kidx:kidx + 1, :] * dpre[sh:sh + rows, :]
                sums[kidx] = fold(d0 * u_ext[CONV_HALO - sh:CONV_HALO - sh + rows, :])
            du_ref[pl.ds(r0, rows), :] = du.astype(BF16)
            return tuple(cv + sv for cv, sv in zip(carry, sums + [fold(d0)]))

        zero = jnp.zeros((CONV_HALO, cb), F32)
        totals = lax.fori_loop(0, ts // rows, step, (zero,) * (SSM_CONV + 1))
        for kidx in range(SSM_CONV):
            dw_ref[kidx:kidx + 1, :] += jnp.sum(totals[kidx], axis=0, keepdims=True)
        db_ref[...] += jnp.sum(totals[SSM_CONV], axis=0, keepdims=True)

    last_hb = s // CONV_HALO - 1
    tile = pl.BlockSpec((None, ts, cb), lambda j, bi, i: (bi, i, j))
    prev = pl.BlockSpec((None, CONV_HALO, cb), lambda j, bi, i: (bi, jnp.maximum(i * hb - 1, 0), j))
    nxt = pl.BlockSpec((None, CONV_HALO, cb), lambda j, bi, i: (bi, jnp.minimum((i + 1) * hb, last_hb), j))
    return pl.pallas_call(
        body, name=name, grid=(c // cb, b, n_t),
        in_specs=[tile, prev, nxt, tile, nxt,
                  pl.BlockSpec((SSM_CONV, cb), lambda j, bi, i: (0, j)),
                  pl.BlockSpec((1, cb), lambda j, bi, i: (0, j))],
        out_specs=[tile, pl.BlockSpec((SSM_CONV, cb), lambda j, bi, i: (0, j)),
                   pl.BlockSpec((1, cb), lambda j, bi, i: (0, j))],
        out_shape=[jax.ShapeDtypeStruct((b, s, c), BF16), jax.ShapeDtypeStruct((SSM_CONV, c), F32),
                   jax.ShapeDtypeStruct((1, c), F32)],
        scratch_shapes=[pltpu.VMEM((ts + 2 * CONV_HALO, cb), F32), pltpu.VMEM((ts + CONV_HALO, cb), F32)],
        compiler_params=_params("parallel", "arbitrary", "arbitrary"),
    )(u, u, u, dout, dout, w, bias)


def _ssd_chunk_terms(dtr_ref, bias_ref, alog_ref):
    q = SSM_CHUNK
    dt = _softplus(dtr_ref[...] + bias_ref[...])
    a_neg = -jnp.exp(alog_ref[...])
    row = lax.broadcasted_iota(jnp.int32, (q, q), 0)
    col = lax.broadcasted_iota(jnp.int32, (q, q), 1)
    lower = row >= col
    s = _mask_nn(lower, dt * a_neg)
    return dt, a_neg, s, s.T, lower


def _head_masks():
    heads = jnp.arange(LANES)[:, None]
    chans = jnp.arange(SSM_D_INNER)[None, :]
    to_channels = (chans // SSM_HEAD_DIM == heads).astype(BF16)
    return to_channels, to_channels.T


def _per_channel(v, to_channels):
    hi = v.astype(BF16)
    lo = (v - hi.astype(F32)).astype(BF16)
    return _nn(hi, to_channels) + _nn(lo, to_channels)


def _per_head(v, to_heads):
    hi = v.astype(BF16)
    lo = (v - hi.astype(F32)).astype(BF16)
    return _nn(hi, to_heads) + _nn(lo, to_heads)


def _decay_terms_per_channel(dt, s_col, to_channels):
    q = SSM_CHUNK
    tot = s_col[q - 1:q, :]
    stacked = jnp.concatenate([dt, jnp.exp(s_col), jnp.exp(tot - s_col)], axis=0)
    wide = _per_channel(stacked, to_channels)
    dtx, esx, decx = wide[:q], wide[q:2 * q], wide[2 * q:]
    return dtx, esx, decx, esx[0:1, :] * decx[0:1, :]


SSM_PAIRS_PER_GROUP = SSM_HEADS_PER_GROUP // 2
SSM_GROUP_CHANNELS = SSM_HEADS_PER_GROUP * SSM_HEAD_DIM


def _split_pair(v):
    first = lax.broadcasted_iota(jnp.int32, v.shape, 1) < SSM_HEAD_DIM
    return jnp.concatenate([jnp.where(first, v, 0.0), jnp.where(first, 0.0, v)], axis=0)


def _ssd_fwd(xc, dtr, dt_bias, a_log, dskx, to_channels, name):
    b, s, _ = xc.shape
    q = SSM_CHUNK
    nc = s // q
    n, gc = SSM_D_STATE, SSM_GROUP_CHANNELS

    def body(xc_ref, dtr_ref, bias_ref, alog_ref, dsk_ref, tc_ref, y_ref, hs_ref, h_scr):
        @pl.when(pl.program_id(1) == 0)
        def _():
            h_scr[...] = jnp.zeros_like(h_scr)

        dt, _, s_col, s_row, lower = _ssd_chunk_terms(dtr_ref, bias_ref, alog_ref)
        dtx, esx, decx, etotx = _decay_terms_per_channel(dt, s_col, tc_ref[...])
        x = xc_ref[:, :SSM_D_INNER]
        xdt = x * dtx
        xdec = xdt * decx
        skip = dsk_ref[...] * x
        for g in range(SSM_N_GROUPS):
            bg = xc_ref[:, SSM_D_INNER + n * g:SSM_D_INNER + n * (g + 1)].astype(BF16)
            cg = xc_ref[:, SSM_D_INNER + n * (SSM_N_GROUPS + g):SSM_D_INNER + n * (SSM_N_GROUPS + g + 1)].astype(BF16)
            gsl = slice(gc * g, gc * (g + 1))
            gm = _nt(cg, bg)
            hgt = h_scr[:, gsl]
            hs_ref[:, gsl] = hgt
            y_off = esx[:, gsl] * _nn(cg, hgt)
            h_scr[:, gsl] = etotx[:, gsl] * hgt + _tn(bg, xdec[:, gsl])
            for k in range(SSM_PAIRS_PER_GROUP):
                h0 = g * SSM_HEADS_PER_GROUP + 2 * k
                lo = gc * g + LANES * k
                ms = []
                for h in (h0, h0 + 1):
                    lm = jnp.exp(jnp.where(lower, s_col[:, h:h + 1] - s_row[h:h + 1, :], NEG_INF))
                    ms.append((gm * lm).astype(BF16))
                y_diag = _nn(jnp.concatenate(ms, axis=1), _split_pair(xdt[:, lo:lo + LANES]))
                y_ref[:, lo:lo + LANES] = y_diag + y_off[:, LANES * k:LANES * (k + 1)] + skip[:, lo:lo + LANES]

    vec = pl.BlockSpec((1, LANES), lambda bi, c: (0, 0))
    return pl.pallas_call(
        body, name=name, grid=(b, nc),
        in_specs=[pl.BlockSpec((None, q, SSM_CONV_DIM), lambda bi, c: (bi, c, 0)),
                  pl.BlockSpec((None, q, LANES), lambda bi, c: (bi, c, 0)), vec, vec,
                  pl.BlockSpec((1, SSM_D_INNER), lambda bi, c: (0, 0)),
                  pl.BlockSpec((LANES, SSM_D_INNER), lambda bi, c: (0, 0))],
        out_specs=[pl.BlockSpec((None, q, SSM_D_INNER), lambda bi, c: (bi, c, 0)),
                   pl.BlockSpec((None, None, n, SSM_D_INNER), lambda bi, c: (bi, c, 0, 0))],
        out_shape=[jax.ShapeDtypeStruct((b, s, SSM_D_INNER), F32),
                   jax.ShapeDtypeStruct((b, nc, n, SSM_D_INNER), F32)],
        scratch_shapes=[pltpu.VMEM((n, SSM_D_INNER), F32)],
        compiler_params=_params("parallel", "arbitrary"),
    )(xc, dtr, dt_bias, a_log, dskx, to_channels)


def _ssd_bwd(xc, dtr, dy, hs, dt_bias, a_log, dskx, to_channels, to_heads, name):
    b, s, _ = xc.shape
    q = SSM_CHUNK
    nc = s // q
    n, gc = SSM_D_STATE, SSM_GROUP_CHANNELS

    def colsum(v):
        return jnp.sum(v, axis=0, keepdims=True)

    def body(xc_ref, dtr_ref, dy_ref, hs_ref, bias_ref, alog_ref, dsk_ref, tc_ref, th_ref,
             dxc_ref, ddtr_ref, dalog_ref, ddsk_ref, dbias_ref, dh_scr, dxs_scr, dxd_scr, w_scr, dst_scr, rows_scr):
        ci = pl.program_id(1)

        @pl.when(ci == 0)
        def _():
            dh_scr[...] = jnp.zeros_like(dh_scr)

        @pl.when(jnp.logical_and(pl.program_id(0) == 0, ci == 0))
        def _():
            dalog_ref[...] = jnp.zeros_like(dalog_ref)
            ddsk_ref[...] = jnp.zeros_like(ddsk_ref)
            dbias_ref[...] = jnp.zeros_like(dbias_ref)
            dst_scr[...] = jnp.zeros_like(dst_scr)

        dt, a_neg, s_col, s_row, lower = _ssd_chunk_terms(dtr_ref, bias_ref, alog_ref)
        upper = jnp.logical_not(lower) | (lax.broadcasted_iota(jnp.int32, (q, q), 0)
                                          == lax.broadcasted_iota(jnp.int32, (q, q), 1))
        dtx, esx, decx, etotx = _decay_terms_per_channel(dt, s_col, tc_ref[...])
        x = xc_ref[:, :SSM_D_INNER]
        dyv = dy_ref[...]
        xdt = x * dtx
        xdec = xdt * decx
        dw = esx * dyv
        rows_scr[...] = jnp.zeros_like(rows_scr)
        for g in range(SSM_N_GROUPS):
            b_lo = SSM_D_INNER + n * g
            c_lo = SSM_D_INNER + n * (SSM_N_GROUPS + g)
            bg = xc_ref[:, b_lo:b_lo + n].astype(BF16)
            cg = xc_ref[:, c_lo:c_lo + n].astype(BF16)
            gsl = slice(gc * g, gc * (g + 1))
            gm = _nt(cg, bg)
            gmt = _nt(bg, cg)
            hgt = hs_ref[:, gsl]
            dhgt = dh_scr[:, gsl]
            w_scr[:, gsl] = _nn(cg, hgt)
            dcg = _nt(dw[:, gsl], hgt)
            dxs = decx[:, gsl] * _nn(bg, dhgt)
            dxs_scr[:, gsl] = dxs
            dbg = _nt(xdec[:, gsl], dhgt)
            rows_scr[2:3, gsl] = colsum(dhgt * hgt)
            dh_scr[:, gsl] = _tn(cg, dw[:, gsl]) + etotx[:, gsl] * dhgt
            dg = jnp.zeros((q, q), F32)
            dgt = jnp.zeros((q, q), F32)
            for k in range(SSM_PAIRS_PER_GROUP):
                h0 = g * SSM_HEADS_PER_GROUP + 2 * k
                lo = gc * g + LANES * k
                xp = xdt[:, lo:lo + LANES]
                dyp = dyv[:, lo:lo + LANES]
                dy2 = _split_pair(dyp)
                dm2 = _nt(dy2, xp)
                dmt2 = _nt(_split_pair(xp), dyp)
                mts = []
                for i, h in enumerate((h0, h0 + 1)):
                    lm = jnp.exp(jnp.where(lower, s_col[:, h:h + 1] - s_row[h:h + 1, :], NEG_INF))
                    lmt = jnp.exp(jnp.where(upper, s_row[h:h + 1, :] - s_col[:, h:h + 1], NEG_INF))
                    dm = dm2[q * i:q * (i + 1), :]
                    dmt = dmt2[q * i:q * (i + 1), :]
                    dg = dg + dm * lm
                    dgt = dgt + dmt * lmt
                    mt = gmt * lmt
                    dst_scr[h:h + 1, :] = colsum(dmt * mt) - colsum(dm * (gm * lm))
                    mts.append(mt.astype(BF16))
                dxd_scr[:, lo:lo + LANES] = _nn(jnp.concatenate(mts, axis=1), dy2)
            dxc_ref[:, b_lo:b_lo + n] = dbg + _nn(dgt, cg)
            dxc_ref[:, c_lo:c_lo + n] = dcg + _nn(dg, bg)
        dxs = dxs_scr[...]
        dxdt = dxd_scr[...] + dxs
        dxc_ref[:, :SSM_D_INNER] = dxdt * dtx + dsk_ref[...] * dyv
        state_part = xdt * dxs
        rows_scr[0:1, :] = colsum(dyv * x)
        rows_scr[1:2, :] = colsum(state_part)
        th = th_ref[...]
        per_head = _per_head(jnp.concatenate([dw * w_scr[...] - state_part, dxdt * x], axis=0), th)
        r_ds, r_dt = per_head[:q], per_head[q:]
        sums = _per_head(rows_scr[...], th)
        etot = jnp.exp(s_col[q - 1:q, :])
        dtot = sums[1:2, :] + etot * sums[2:3, :]
        last = lax.broadcasted_iota(jnp.int32, (q, LANES), 0) == q - 1
        ds = dst_scr[...].T + r_ds + jnp.where(last, dtot, 0.0)
        da = _mask_nn(upper, ds)
        ddt = da * a_neg + r_dt
        live = lax.broadcasted_iota(jnp.int32, (1, LANES), 1) < SSM_N_HEADS
        sg = _sigmoid(dtr_ref[...] + bias_ref[...])
        ddtr = jnp.where(live, ddt * sg, 0.0)
        ddtr_ref[...] = ddtr.astype(BF16)
        dalog_ref[...] += jnp.where(live, colsum(da * dt) * a_neg, 0.0)
        ddsk_ref[...] += jnp.where(live, sums[0:1, :], 0.0)
        dbias_ref[...] += colsum(ddtr)

    rev = lambda bi, c: (bi, nc - 1 - c, 0)
    vec = pl.BlockSpec((1, LANES), lambda bi, c: (0, 0))
    wide = pl.BlockSpec((None, q, SSM_D_INNER), rev)
    return pl.pallas_call(
        body, name=name, grid=(b, nc),
        in_specs=[pl.BlockSpec((None, q, SSM_CONV_DIM), rev), pl.BlockSpec((None, q, LANES), rev), wide,
                  pl.BlockSpec((None, None, n, SSM_D_INNER), lambda bi, c: (bi, nc - 1 - c, 0, 0)),
                  vec, vec, pl.BlockSpec((1, SSM_D_INNER), lambda bi, c: (0, 0)),
                  pl.BlockSpec((LANES, SSM_D_INNER), lambda bi, c: (0, 0)),
                  pl.BlockSpec((SSM_D_INNER, LANES), lambda bi, c: (0, 0))],
        out_specs=[pl.BlockSpec((None, q, SSM_CONV_DIM), rev), pl.BlockSpec((None, q, LANES), rev), vec, vec, vec],
        out_shape=[jax.ShapeDtypeStruct((b, s, SSM_CONV_DIM), F32), jax.ShapeDtypeStruct((b, s, LANES), BF16),
                   jax.ShapeDtypeStruct((1, LANES), F32), jax.ShapeDtypeStruct((1, LANES), F32),
                   jax.ShapeDtypeStruct((1, LANES), F32)],
        scratch_shapes=[pltpu.VMEM((n, SSM_D_INNER), F32)] + [pltpu.VMEM((q, SSM_D_INNER), F32)] * 3
        + [pltpu.VMEM((LANES, q), F32), pltpu.VMEM((8, SSM_D_INNER), F32)],
        compiler_params=_params("arbitrary", "arbitrary"),
    )(xc, dtr, dy, hs, dt_bias, a_log, dskx, to_channels, to_heads)


SSM_GROUP_WIDTH = SSM_D_INNER // SSM_N_GROUPS


def _gate_norm_fwd(y, z, w, name):
    t, d = y.shape
    tm = _pick(t, (256, 128))

    def body(y_ref, z_ref, w_ref, o_ref):
        for g in range(SSM_N_GROUPS):
            sl = slice(SSM_GROUP_WIDTH * g, SSM_GROUP_WIDTH * (g + 1))
            zv = z_ref[:, sl]
            u = y_ref[:, sl] * (zv * _sigmoid(zv))
            r = lax.rsqrt(jnp.mean(u * u, axis=-1, keepdims=True) + EPS)
            o_ref[:, sl] = ((u * r) * w_ref[:, sl]).astype(BF16)

    row = pl.BlockSpec((tm, d), lambda i: (i, 0))
    return pl.pallas_call(
        body, name=name, grid=(t // tm,),
        in_specs=[row, row, pl.BlockSpec((1, d), lambda i: (0, 0))], out_specs=row,
        out_shape=jax.ShapeDtypeStruct((t, d), BF16),
        compiler_params=_params("parallel"),
    )(y, z, w)


def _gate_norm_bwd(y, z, w, dout, name):
    t, d = y.shape
    tm = _pick(t, (256, 128))

    def body(y_ref, z_ref, w_ref, do_ref, dy_ref, dz_ref, dw_ref):
        @pl.when(pl.program_id(0) == 0)
        def _():
            dw_ref[...] = jnp.zeros_like(dw_ref)

        for g in range(SSM_N_GROUPS):
            sl = slice(SSM_GROUP_WIDTH * g, SSM_GROUP_WIDTH * (g + 1))
            zv = z_ref[:, sl]
            yv = y_ref[:, sl]
            sg = _sigmoid(zv)
            silu = zv * sg
            u = yv * silu
            r = lax.rsqrt(jnp.mean(u * u, axis=-1, keepdims=True) + EPS)
            uh = u * r
            dov = do_ref[:, sl]
            dw_ref[:, sl] += jnp.sum(dov * uh, axis=0, keepdims=True)
            dyg = dov * w_ref[:, sl]
            du = r * (dyg - uh * jnp.mean(dyg * uh, axis=-1, keepdims=True))
            dy_ref[:, sl] = du * silu
            dz_ref[:, sl] = (du * yv * (sg * (1.0 + zv * (1.0 - sg)))).astype(BF16)

    row = pl.BlockSpec((tm, d), lambda i: (i, 0))
    vec = pl.BlockSpec((1, d), lambda i: (0, 0))
    return pl.pallas_call(
        body, name=name, grid=(t // tm,),
        in_specs=[row, row, vec, row], out_specs=[row, row, vec],
        out_shape=[jax.ShapeDtypeStruct((t, d), F32), jax.ShapeDtypeStruct((t, d), BF16),
                   jax.ShapeDtypeStruct((1, d), F32)],
        compiler_params=_params("arbitrary"),
    )(y, z, w, dout)


def _rope_tables(s):
    half = ATT_HEAD_DIM // 2
    inv = ROPE_THETA ** (-jnp.arange(half, dtype=F32) / half)
    ang = jnp.arange(s).astype(F32)[:, None] * inv[None, :]
    cos, sin = jnp.cos(ang), jnp.sin(ang)
    return jnp.concatenate([cos, cos], axis=-1), jnp.concatenate([-sin, sin], axis=-1)


ATT_TILE = 256


def _by_residue_spec(r, width):
    return pl.BlockSpec((None, r, ATT_TILE // r, width), lambda bi, i: (bi, 0, i, 0))


def _to_residues(tile, stage, r, store):
    if r == 1:
        store(0, tile)
        return
    stage[...] = tile
    for ri in range(r):
        store(ri, stage[pl.ds(ri, ATT_TILE // r, stride=r), :])


def _from_residues(load, stage, r):
    if r == 1:
        return load(0)
    for ri in range(r):
        stage[pl.ds(ri, ATT_TILE // r, stride=r), :] = load(ri)
    return stage[...]


def _rope_fwd(qkv, cosf, sinf, name):
    b, s, w = qkv.shape
    ts, d, gw = ATT_TILE, ATT_HEAD_DIM, ATT_OUT_DIM

    def body(x_ref, c_ref, s_ref, *rest):
        outs, stage = rest[:-1], rest[-1]
        cv, sv = c_ref[...], s_ref[...]
        for kind in range(3):
            for gi, r in enumerate(ATT_DILATIONS):
                for j in range(ATT_HEADS_PER_GROUP):
                    src = d * (kind * ATT_N_HEADS + gi * ATT_HEADS_PER_GROUP + j)
                    dst = slice(kind * gw + d * j, kind * gw + d * (j + 1))
                    tv = x_ref[:, src:src + d]
                    if kind < 2:
                        tv = tv * cv + pltpu.roll(tv, d // 2, 1) * sv

                    def store(ri, rows, o_ref=outs[gi], dst=dst):
                        o_ref[ri, :, dst] = rows.astype(BF16)

                    _to_residues(tv, stage, r, store)

    tab = pl.BlockSpec((ts, d), lambda bi, i: (i, 0))
    return pl.pallas_call(
        body, name=name, grid=(b, s // ts),
        in_specs=[pl.BlockSpec((None, ts, w), lambda bi, i: (bi, i, 0)), tab, tab],
        out_specs=[_by_residue_spec(r, 3 * gw) for r in ATT_DILATIONS],
        out_shape=[jax.ShapeDtypeStruct((b, r, s // r, 3 * gw), BF16) for r in ATT_DILATIONS],
        scratch_shapes=[pltpu.VMEM((ts, d), F32)],
        compiler_params=_params("parallel", "parallel"),
    )(qkv, cosf, sinf)


def _rope_bwd(dq, dk, dv, cosf, sinf, name):
    n_pat = len(ATT_DILATIONS)
    b, _, s, gw = dq[0].shape
    ts, d = ATT_TILE, ATT_HEAD_DIM

    def body(*refs):
        ins, (c_ref, s_ref, o_ref, stage) = refs[:3 * n_pat], refs[3 * n_pat:]
        cv, sv = c_ref[...], s_ref[...]
        for kind in range(3):
            for gi, r in enumerate(ATT_DILATIONS):
                src = ins[kind * n_pat + gi]
                for j in range(ATT_HEADS_PER_GROUP):
                    tv = _from_residues(lambda ri, src=src, j=j: src[ri, :, d * j:d * (j + 1)], stage, r)
                    if kind < 2:
                        tv = tv * cv + pltpu.roll(tv * sv, d // 2, 1)
                    lo = d * (kind * ATT_N_HEADS + gi * ATT_HEADS_PER_GROUP + j)
                    o_ref[:, lo:lo + d] = tv.astype(BF16)

    tab = pl.BlockSpec((ts, d), lambda bi, i: (i, 0))
    parts = [_by_residue_spec(r, gw) for r in ATT_DILATIONS]
    return pl.pallas_call(
        body, name=name, grid=(b, s // ts), in_specs=parts * 3 + [tab, tab],
        out_specs=pl.BlockSpec((None, ts, ATT_QKV_DIM), lambda bi, i: (bi, i, 0)),
        out_shape=jax.ShapeDtypeStruct((b, s, ATT_QKV_DIM), BF16),
        scratch_shapes=[pltpu.VMEM((ts, d), F32)],
        compiler_params=_params("parallel", "parallel"),
    )(*dq, *dk, *dv, cosf, sinf)


ATT_SCALE = ATT_HEAD_DIM ** -0.5
ATT_STEP = 2 * ATT_BLOCK


def _att_spec(col):
    return pl.BlockSpec((None, None, ATT_STEP, ATT_OUT_DIM), lambda bi, ri, i: (bi, ri, i, col))


def _att_edge_spec(col, side, n_steps):
    def index(bi, ri, i):
        blk = 2 * i - 1 if side < 0 else 2 * i + 2
        return (bi, ri, jnp.clip(blk, 0, 2 * n_steps - 1), col)
    return pl.BlockSpec((None, None, ATT_BLOCK, ATT_OUT_DIM), index)


def _band_mask(shape, q_axis, has_prev):
    qi = lax.broadcasted_iota(jnp.int32, shape, q_axis)
    kj = lax.broadcasted_iota(jnp.int32, shape, 1 - q_axis)
    dist = qi + ATT_BLOCK - kj
    return (dist >= 0) & (dist <= ATT_BLOCK) & (has_prev | (kj >= ATT_BLOCK))


def _att_fwd(qkr, name):
    b, r, l, _ = qkr.shape
    nb = l // ATT_STEP
    d = ATT_HEAD_DIM

    def body(q_ref, kp_ref, k_ref, vp_ref, v_ref, o_ref, lse_ref):
        mask = _band_mask((ATT_STEP, ATT_BLOCK + ATT_STEP), 0, pl.program_id(2) > 0)
        for j in range(ATT_HEADS_PER_GROUP):
            sl = slice(d * j, d * (j + 1))
            kcat = jnp.concatenate([kp_ref[:, sl], k_ref[:, sl]], axis=0)
            vcat = jnp.concatenate([vp_ref[:, sl], v_ref[:, sl]], axis=0)
            sc = jnp.where(mask, _nt(q_ref[:, sl], kcat) * ATT_SCALE, NEG_INF)
            m = jnp.max(sc, axis=-1, keepdims=True)
            pr = jnp.exp(sc - m)
            den = jnp.sum(pr, axis=-1, keepdims=True)
            o_ref[:, sl] = _nn(pr / den, vcat)
            lse_ref[:, sl] = jnp.broadcast_to(m + jnp.log(den), (ATT_STEP, d))

    out_spec = _att_spec(0)
    return pl.pallas_call(
        body, name=name, grid=(b, r, nb),
        in_specs=[_att_spec(0), _att_edge_spec(1, -1, nb), _att_spec(1), _att_edge_spec(2, -1, nb), _att_spec(2)],
        out_specs=[out_spec, out_spec],
        out_shape=[jax.ShapeDtypeStruct((b, r, l, ATT_OUT_DIM), F32)] * 2,
        compiler_params=_params("parallel", "parallel", "parallel"),
    )(qkr, qkr, qkr, qkr, qkr)


def _att_merge(os_, lses, name):
    n_pat = len(os_)
    b, _, s, gw = os_[0].shape
    ts, d = ATT_TILE, ATT_HEAD_DIM

    def body(*refs):
        o_refs, l_refs = refs[:n_pat], refs[n_pat:2 * n_pat]
        att_ref, lse_outs, stage = refs[2 * n_pat], refs[2 * n_pat + 1:3 * n_pat + 1], refs[-1]
        for j in range(ATT_HEADS_PER_GROUP):
            sl = slice(d * j, d * (j + 1))
            ov = [_from_residues(lambda ri, g=g: o_refs[g][ri, :, sl], stage, r)
                  for g, r in enumerate(ATT_DILATIONS)]
            ls = [_from_residues(lambda ri, g=g: l_refs[g][ri, :, sl], stage, r)
                  for g, r in enumerate(ATT_DILATIONS)]
            m = functools.reduce(jnp.maximum, ls)
            es = [jnp.exp(lv - m) for lv in ls]
            tot = functools.reduce(lambda u, v: u + v, es)
            acc = (es[0] / tot) * ov[0]
            for g in range(1, n_pat):
                acc = acc + (es[g] / tot) * ov[g]
            att_ref[:, sl] = acc
            joint = m + jnp.log(tot)
            for g, r in enumerate(ATT_DILATIONS):
                def store(ri, rows, out=lse_outs[g]):
                    out[ri, :, sl] = rows
                _to_residues(joint, stage, r, store)

    parts = [_by_residue_spec(r, gw) for r in ATT_DILATIONS]
    return pl.pallas_call(
        body, name=name, grid=(b, s // ts), in_specs=parts * 2,
        out_specs=[pl.BlockSpec((None, ts, gw), lambda bi, i: (bi, i, 0))] + parts,
        out_shape=[jax.ShapeDtypeStruct((b, s, gw), F32)]
        + [jax.ShapeDtypeStruct((b, r, s // r, gw), F32) for r in ATT_DILATIONS],
        scratch_shapes=[pltpu.VMEM((ts, d), F32)],
        compiler_params=_params("parallel", "parallel"),
    )(*os_, *lses)


def _att_delta(att, datt, name):
    b, s, gw = att.shape
    ts, d = ATT_TILE, ATT_HEAD_DIM
    n_pat = len(ATT_DILATIONS)

    def body(a_ref, d_ref, *rest):
        do_outs, dl_outs, stage = rest[:n_pat], rest[n_pat:2 * n_pat], rest[-1]
        for j in range(ATT_HEADS_PER_GROUP):
            sl = slice(d * j, d * (j + 1))
            dv = d_ref[:, sl]
            delta = jnp.broadcast_to(jnp.sum(a_ref[:, sl] * dv, axis=-1, keepdims=True), (ts, d))
            for g, r in enumerate(ATT_DILATIONS):
                def store_do(ri, rows, out=do_outs[g]):
                    out[ri, :, sl] = rows.astype(BF16)

                def store_dl(ri, rows, out=dl_outs[g]):
                    out[ri, :, sl] = rows

                _to_residues(dv, stage, r, store_do)
                _to_residues(delta, stage, r, store_dl)

    row = pl.BlockSpec((None, ts, gw), lambda bi, i: (bi, i, 0))
    parts = [_by_residue_spec(r, gw) for r in ATT_DILATIONS]
    outs = pl.pallas_call(
        body, name=name, grid=(b, s // ts), in_specs=[row, row], out_specs=parts * 2,
        out_shape=[jax.ShapeDtypeStruct((b, r, s // r, gw), BF16) for r in ATT_DILATIONS]
        + [jax.ShapeDtypeStruct((b, r, s // r, gw), F32) for r in ATT_DILATIONS],
        scratch_shapes=[pltpu.VMEM((ts, d), F32)],
        compiler_params=_params("parallel", "parallel"),
    )(att, datt)
    return outs[:n_pat], outs[n_pat:]


def _att_bwd_q(qkr, datt, lse, delta, name):
    b, r, l, _ = qkr.shape
    nb = l // ATT_STEP
    d = ATT_HEAD_DIM

    def body(q_ref, kp_ref, k_ref, vp_ref, v_ref, do_ref, lse_ref, dl_ref, dq_ref):
        mask = _band_mask((ATT_STEP, ATT_BLOCK + ATT_STEP), 0, pl.program_id(2) > 0)
        for j in range(ATT_HEADS_PER_GROUP):
            sl = slice(d * j, d * (j + 1))
            kcat = jnp.concatenate([kp_ref[:, sl], k_ref[:, sl]], axis=0)
            vcat = jnp.concatenate([vp_ref[:, sl], v_ref[:, sl]], axis=0)
            sc = _nt(q_ref[:, sl], kcat) * ATT_SCALE
            pr = jnp.exp(jnp.where(mask, sc - lse_ref[:, d * j:d * j + 1], NEG_INF))
            dp = _nt(do_ref[:, sl], vcat)
            dsc = pr * (dp - dl_ref[:, d * j:d * j + 1])
            dq_ref[:, sl] = _nn(dsc, kcat) * ATT_SCALE

    tok = _att_spec(0)
    return pl.pallas_call(
        body, name=name, grid=(b, r, nb),
        in_specs=[_att_spec(0), _att_edge_spec(1, -1, nb), _att_spec(1), _att_edge_spec(2, -1, nb), _att_spec(2),
                  tok, tok, tok],
        out_specs=tok,
        out_shape=jax.ShapeDtypeStruct((b, r, l, ATT_OUT_DIM), F32),
        compiler_params=_params("parallel", "parallel", "parallel"),
    )(qkr, qkr, qkr, qkr, qkr, datt, lse, delta)


def _att_bwd_kv(qkr, datt, lse, delta, name):
    b, r, l, _ = qkr.shape
    nb = l // ATT_STEP
    d = ATT_HEAD_DIM

    def body(k_ref, v_ref, q_ref, qn_ref, do_ref, don_ref, lse_ref, lsen_ref, dl_ref, dln_ref, dk_ref, dv_ref):
        shape = (ATT_STEP, ATT_STEP + ATT_BLOCK)
        kj = lax.broadcasted_iota(jnp.int32, shape, 0)
        qi = lax.broadcasted_iota(jnp.int32, shape, 1)
        dist = qi - kj
        has_next = pl.program_id(2) < nb - 1
        mask = (dist >= 0) & (dist <= ATT_BLOCK) & (has_next | (qi < ATT_STEP))
        for j in range(ATT_HEADS_PER_GROUP):
            sl = slice(d * j, d * (j + 1))
            qcat = jnp.concatenate([q_ref[:, sl], qn_ref[:, sl]], axis=0)
            docat = jnp.concatenate([do_ref[:, sl], don_ref[:, sl]], axis=0)
            lse_t = jnp.tile(jnp.concatenate([lse_ref[:, sl], lsen_ref[:, sl]], axis=0).T, (ATT_STEP // d, 1))
            dl_t = jnp.tile(jnp.concatenate([dl_ref[:, sl], dln_ref[:, sl]], axis=0).T, (ATT_STEP // d, 1))
            sc_t = _nt(k_ref[:, sl], qcat) * ATT_SCALE
            pr_t = jnp.exp(jnp.where(mask, sc_t - lse_t, NEG_INF))
            dv_ref[:, sl] = _nn(pr_t, docat)
            dsc_t = pr_t * (_nt(v_ref[:, sl], docat) - dl_t)
            dk_ref[:, sl] = _nn(dsc_t, qcat) * ATT_SCALE

    tok, tok_n = _att_spec(0), _att_edge_spec(0, 1, nb)
    return pl.pallas_call(
        body, name=name, grid=(b, r, nb),
        in_specs=[_att_spec(1), _att_spec(2), _att_spec(0), _att_edge_spec(0, 1, nb),
                  tok, tok_n, tok, tok_n, tok, tok_n],
        out_specs=[tok, tok],
        out_shape=[jax.ShapeDtypeStruct((b, r, l, ATT_OUT_DIM), F32)] * 2,
        compiler_params=_params("parallel", "parallel", "parallel"),
    )(qkr, qkr, qkr, qkr, datt, datt, lse, lse, delta, delta)


def _mix_fwd(gl, bg, ys, ya, name):
    t, d = ys.shape
    tm = _pick(t, (512, 256, 128))

    def body(gl_ref, bg_ref, ys_ref, ya_ref, o_ref):
        g0 = _sigmoid(gl_ref[:, :d] + bg_ref[:, :d])
        g1 = _sigmoid(gl_ref[:, d:] + bg_ref[:, d:])
        o_ref[...] = (g0 * ys_ref[...] + g1 * ya_ref[...]).astype(BF16)

    row = pl.BlockSpec((tm, d), lambda i: (i, 0))
    return pl.pallas_call(
        body, name=name, grid=(t // tm,),
        in_specs=[pl.BlockSpec((tm, 2 * d), lambda i: (i, 0)), pl.BlockSpec((1, 2 * d), lambda i: (0, 0)), row, row],
        out_specs=row, out_shape=jax.ShapeDtypeStruct((t, d), BF16),
        compiler_params=_params("parallel"),
    )(gl, bg, ys, ya)


def _mix_bwd(gl, bg, ys, ya, dmixed, name):
    t, d = ys.shape
    tm = _pick(t, (512, 256, 128))

    def body(gl_ref, bg_ref, ys_ref, ya_ref, dm_ref, dys_ref, dya_ref, dgl_ref, dbg_ref):
        @pl.when(pl.program_id(0) == 0)
        def _():
            dbg_ref[...] = jnp.zeros_like(dbg_ref)

        dm = dm_ref[...]
        g0 = _sigmoid(gl_ref[:, :d] + bg_ref[:, :d])
        g1 = _sigmoid(gl_ref[:, d:] + bg_ref[:, d:])
        dys_ref[...] = (dm * g0).astype(BF16)
        dya_ref[...] = (dm * g1).astype(BF16)
        d0 = dm * ys_ref[...] * (g0 * (1.0 - g0))
        d1 = dm * ya_ref[...] * (g1 * (1.0 - g1))
        dgl_ref[:, :d] = d0.astype(BF16)
        dgl_ref[:, d:] = d1.astype(BF16)
        dbg_ref[:, :d] += jnp.sum(d0, axis=0, keepdims=True)
        dbg_ref[:, d:] += jnp.sum(d1, axis=0, keepdims=True)

    row = pl.BlockSpec((tm, d), lambda i: (i, 0))
    wide = pl.BlockSpec((tm, 2 * d), lambda i: (i, 0))
    vec = pl.BlockSpec((1, 2 * d), lambda i: (0, 0))
    return pl.pallas_call(
        body, name=name, grid=(t // tm,),
        in_specs=[wide, vec, row, row, row], out_specs=[row, row, wide, vec],
        out_shape=[jax.ShapeDtypeStruct((t, d), BF16), jax.ShapeDtypeStruct((t, d), BF16),
                   jax.ShapeDtypeStruct((t, 2 * d), BF16), jax.ShapeDtypeStruct((1, 2 * d), F32)],
        compiler_params=_params("arbitrary"),
    )(gl, bg, ys, ya, dmixed)


def _swiglu_fwd(gt, up, name):
    t, f = gt.shape
    tm = _pick(t, (512, 256, 128))

    def body(g_ref, u_ref, o_ref):
        gv = g_ref[...]
        o_ref[...] = ((gv * _sigmoid(gv)) * u_ref[...]).astype(BF16)

    row = pl.BlockSpec((tm, f), lambda i: (i, 0))
    return pl.pallas_call(
        body, name=name, grid=(t // tm,), in_specs=[row, row], out_specs=row,
        out_shape=jax.ShapeDtypeStruct((t, f), BF16), compiler_params=_params("parallel"),
    )(gt, up)


def _swiglu_bwd(gt, up, dact, name):
    t, f = gt.shape
    tm = _pick(t, (512, 256, 128))

    def body(g_ref, u_ref, d_ref, dg_ref, du_ref):
        gv = g_ref[...]
        dv = d_ref[...]
        sg = _sigmoid(gv)
        dg_ref[...] = (dv * u_ref[...] * (sg * (1.0 + gv * (1.0 - sg)))).astype(BF16)
        du_ref[...] = (dv * (gv * sg)).astype(BF16)

    row = pl.BlockSpec((tm, f), lambda i: (i, 0))
    return pl.pallas_call(
        body, name=name, grid=(t // tm,), in_specs=[row, row, row], out_specs=[row, row],
        out_shape=[jax.ShapeDtypeStruct((t, f), BF16)] * 2, compiler_params=_params("parallel"),
    )(gt, up, dact)


def _peer(k):
    x, y, c = lax.axis_index("x"), lax.axis_index("y"), lax.axis_index("c")
    px, py, pc = x ^ ((k >> 2) & 1), y ^ ((k >> 1) & 1), c ^ (k & 1)
    return (px, py, pc), 4 * px + 2 * py + pc


def _my_index():
    return 4 * lax.axis_index("x") + 2 * lax.axis_index("y") + lax.axis_index("c")


def _all_gather(parts, name):
    n_parts = len(parts)

    def body(*refs):
        ins, outs = refs[:n_parts], refs[n_parts:2 * n_parts]
        send_sems, recv_sems, local_sems = refs[2 * n_parts:]
        here, me = _peer(0)
        sibling, sib_idx = _peer(1)
        chips = [_peer(2 * q) for q in range(1, N_CHIPS)]

        def copy(i, k, block, to, src=None):
            return pltpu.make_async_remote_copy(
                src_ref=outs[i].at[block] if src is None else src, dst_ref=outs[i].at[block],
                send_sem=send_sems.at[i * (N_DEV - 1) + k], recv_sem=recv_sems.at[i * (N_DEV - 1) + k],
                device_id=to, device_id_type=MESH)

        local = [pltpu.make_async_copy(ins[i], outs[i].at[me], local_sems.at[i]) for i in range(n_parts)]
        for cp in local:
            cp.start()
        sends = []
        for i in range(n_parts):
            sends.append(copy(i, 0, me, sibling, src=ins[i]))
            sends += [copy(i, q, me, chip, src=ins[i]) for q, (chip, _) in enumerate(chips, start=1)]
        for cp in sends:
            cp.start()
        for q, (chip, chip_idx) in enumerate(chips, start=1):
            for i in range(n_parts):
                copy(i, q, chip_idx, here).wait_recv()
                fwd = copy(i, N_CHIPS - 1 + q, chip_idx, sibling)
                fwd.start()
                sends.append(fwd)
        for i in range(n_parts):
            copy(i, 0, sib_idx, here).wait_recv()
        for q, (_, chip_idx) in enumerate(chips, start=1):
            for i in range(n_parts):
                copy(i, N_CHIPS - 1 + q, chip_idx ^ 1, here).wait_recv()
        for cp in sends:
            cp.wait_send()
        for cp in local:
            cp.wait()

    hbm = pl.BlockSpec(memory_space=pl.ANY)
    return pl.pallas_call(
        body, name=name, in_specs=[hbm] * n_parts, out_specs=[hbm] * n_parts,
        out_shape=[jax.ShapeDtypeStruct((N_DEV,) + p_.shape, p_.dtype) for p_ in parts],
        scratch_shapes=[pltpu.SemaphoreType.DMA((n_parts * (N_DEV - 1),)),
                        pltpu.SemaphoreType.DMA((n_parts * (N_DEV - 1),)),
                        pltpu.SemaphoreType.DMA((n_parts,))],
        compiler_params=pltpu.CompilerParams(has_side_effects=True),
    )(*parts)


def _pair_exchange(slabs, name):
    def body(slab_ref, got_ref, send_sems, recv_sems):
        c = lax.axis_index("c")
        sibling, _ = _peer(1)
        copies = [pltpu.make_async_remote_copy(
            src_ref=slab_ref.at[2 * q + 1 - c], dst_ref=got_ref.at[q], send_sem=send_sems.at[q],
            recv_sem=recv_sems.at[q], device_id=sibling, device_id_type=MESH) for q in range(N_CHIPS)]
        for cp in copies:
            cp.start()
        for cp in copies:
            cp.wait()

    hbm = pl.BlockSpec(memory_space=pl.ANY)
    return pl.pallas_call(
        body, name=name, in_specs=[hbm], out_specs=hbm,
        out_shape=jax.ShapeDtypeStruct((N_CHIPS,) + slabs.shape[1:], slabs.dtype),
        scratch_shapes=[pltpu.SemaphoreType.DMA((N_CHIPS,)), pltpu.SemaphoreType.DMA((N_CHIPS,))],
        compiler_params=pltpu.CompilerParams(has_side_effects=True),
    )(slabs)


def _chip_sum(slabs, got, core, name):
    _, rows, lanes = slabs.shape
    tr = _pick(rows, (512, 256, 128, 64, 32, 16, 8))

    def body(core_ref, mine_ref, got_ref, o_ref):
        o_ref[...] = (mine_ref[...].astype(F32) + got_ref[...].astype(F32)).astype(BF16)

    return pl.pallas_call(
        body, name=name,
        grid_spec=pltpu.PrefetchScalarGridSpec(
            num_scalar_prefetch=1, grid=(N_CHIPS, rows // tr),
            in_specs=[pl.BlockSpec((None, tr, lanes), lambda q, i, core_ref: (2 * q + core_ref[0], i, 0)),
                      pl.BlockSpec((None, tr, lanes), lambda q, i, core_ref: (q, i, 0))],
            out_specs=pl.BlockSpec((None, tr, lanes), lambda q, i, core_ref: (q, i, 0))),
        out_shape=jax.ShapeDtypeStruct((N_CHIPS, rows, lanes), BF16),
        compiler_params=_params("parallel", "parallel"),
    )(core, slabs, got)


def _chip_exchange(chip_sums, shared, name):
    def body(sum_ref, sh_ref, got_ref, gsh_ref, send_sems, recv_sems, sh_send_sems, sh_recv_sems, local_sems):
        me = _my_index()
        my_chip = me >> 1
        local = [pltpu.make_async_copy(sum_ref.at[my_chip], got_ref.at[my_chip], local_sems.at[0]),
                 pltpu.make_async_copy(sh_ref, gsh_ref.at[me], local_sems.at[1])]
        for cp in local:
            cp.start()
        sends = []
        for q in range(1, N_CHIPS):
            peer, pidx = _peer(2 * q)
            cp = pltpu.make_async_remote_copy(
                src_ref=sum_ref.at[pidx >> 1], dst_ref=got_ref.at[my_chip], send_sem=send_sems.at[q - 1],
                recv_sem=recv_sems.at[q - 1], device_id=peer, device_id_type=MESH)
            cp.start()
            sends.append(cp)
        for k in range(1, N_DEV):
            peer, _ = _peer(k)
            cp = pltpu.make_async_remote_copy(
                src_ref=sh_ref, dst_ref=gsh_ref.at[me], send_sem=sh_send_sems.at[k - 1],
                recv_sem=sh_recv_sems.at[k - 1], device_id=peer, device_id_type=MESH)
            cp.start()
            sends.append(cp)
        for q in range(1, N_CHIPS):
            peer, pidx = _peer(2 * q)
            pltpu.make_async_remote_copy(
                src_ref=sum_ref.at[my_chip], dst_ref=got_ref.at[pidx >> 1], send_sem=send_sems.at[q - 1],
                recv_sem=recv_sems.at[q - 1], device_id=peer, device_id_type=MESH).wait_recv()
        for k in range(1, N_DEV):
            peer, pidx = _peer(k)
            pltpu.make_async_remote_copy(
                src_ref=sh_ref, dst_ref=gsh_ref.at[pidx], send_sem=sh_send_sems.at[k - 1],
                recv_sem=sh_recv_sems.at[k - 1], device_id=peer, device_id_type=MESH).wait_recv()
        for cp in sends:
            cp.wait_send()
        for cp in local:
            cp.wait()

    hbm = pl.BlockSpec(memory_space=pl.ANY)
    return pl.pallas_call(
        body, name=name, in_specs=[hbm, hbm], out_specs=[hbm, hbm],
        out_shape=[jax.ShapeDtypeStruct(chip_sums.shape, chip_sums.dtype),
                   jax.ShapeDtypeStruct((N_DEV,) + shared.shape, shared.dtype)],
        scratch_shapes=[pltpu.SemaphoreType.DMA((N_CHIPS - 1,)), pltpu.SemaphoreType.DMA((N_CHIPS - 1,)),
                        pltpu.SemaphoreType.DMA((N_DEV - 1,)), pltpu.SemaphoreType.DMA((N_DEV - 1,)),
                        pltpu.SemaphoreType.DMA((2,))],
        compiler_params=pltpu.CompilerParams(has_side_effects=True),
    )(chip_sums, shared)


def _adamw(parts, w, m, v, name):
    n_parts, rows, lanes = parts.shape
    tr = _pick(rows, (512, 256, 128, 64, 32, 16, 8))
    c1 = 1.0 - ADAM_B1 ** ADAM_STEP
    c2 = 1.0 - ADAM_B2 ** ADAM_STEP

    def body(p_ref, w_ref, m_ref, v_ref, g_ref, d_ref, nm_ref, nv_ref):
        g = p_ref[0].astype(F32)
        for j in range(1, n_parts):
            g = g + p_ref[j].astype(F32)
        nm = ADAM_B1 * m_ref[...] + (1.0 - ADAM_B1) * g
        nv = ADAM_B2 * v_ref[...] + (1.0 - ADAM_B2) * (g * g)
        g_ref[...] = g
        nm_ref[...] = nm
        nv_ref[...] = nv
        d_ref[...] = -ADAM_LR * ((nm / c1) / (jnp.sqrt(nv / c2) + ADAM_EPS) + ADAM_WD * w_ref[...])

    row = pl.BlockSpec((tr, lanes), lambda i: (i, 0))
    return pl.pallas_call(
        body, name=name, grid=(rows // tr,),
        in_specs=[pl.BlockSpec((n_parts, tr, lanes), lambda i: (0, i, 0)), row, row, row],
        out_specs=[row] * 4, out_shape=[jax.ShapeDtypeStruct((rows, lanes), F32)] * 4,
        compiler_params=_params("parallel"),
    )(parts, w, m, v)


MATRIX_SHARDS = (
    ("w_in", (D_MODEL, IN_PROJ_DIM // N_DEV), True),
    ("w_ssm_out", (SSM_D_INNER // N_DEV, D_MODEL), False),
    ("w_att_out", (ATT_OUT_DIM, D_MODEL // N_DEV), True),
    ("w_mix_out", (D_MODEL // N_DEV, D_MODEL), False),
    ("w_ffn_gate", (D_MODEL, D_FF // N_DEV), True),
    ("w_ffn_up", (D_MODEL, D_FF // N_DEV), True),
    ("w_ffn_down", (D_FF // N_DEV, D_MODEL), False),
)
CONV_SHARD = ("conv_w", (SSM_CONV, SSM_CONV_DIM // N_DEV), True)
SHARDED = MATRIX_SHARDS + (CONV_SHARD,)
REPLICATED = (("norm_mix", D_MODEL), ("b_gate", 2 * D_MODEL), ("conv_b", SSM_CONV_DIM), ("dt_bias", SSM_N_HEADS),
              ("a_log", SSM_N_HEADS), ("d_skip", SSM_N_HEADS), ("ssm_norm", SSM_D_INNER), ("norm_ffn", D_MODEL),
              ("norm_final", D_MODEL))


PACK_ROWS = 512


def _round_up(n, mult):
    return -(-n // mult) * mult


def _pack_rows(flat, row_mult):
    rows = _round_up(-(-flat.shape[0] // LANES), row_mult)
    return jnp.pad(flat, (0, rows * LANES - flat.shape[0])).reshape(rows, LANES)


def _pack_sharded(vals, specs, row_mult, dtype):
    return _pack_rows(jnp.concatenate([vals[name].reshape(-1).astype(dtype) for name, _, _ in specs]), row_mult)


def _unpack_sharded(packed, specs, lead=()):
    flat = packed.reshape(lead + (-1,))
    out, off = {}, 0
    for name, shape, _ in specs:
        size = shape[0] * shape[1]
        out[name] = flat[..., off:off + size].reshape(lead + shape)
        off += size
    return out


def _stacking(specs):
    return tuple((name, (shape[1], shape[0]) if by_cols else shape, by_cols) for name, shape, by_cols in specs)


def _to_stacking(vals, specs):
    return {name: (vals[name].T if by_cols else vals[name]) for name, _, by_cols in specs}


REPLICATED_ROWS = sum(-(-size // LANES) for _, size in REPLICATED)
LOSS_ROW = REPLICATED_ROWS


def _pack_replicated(vals):
    rows = []
    for name, size in REPLICATED:
        v = vals[name].reshape(-1).astype(F32)
        rows.append(jnp.pad(v, (0, _round_up(size, LANES) - size)))
    return _pack_rows(jnp.concatenate(rows), 8)


def _unpack_replicated(packed, shapes):
    flat = packed.reshape(-1)
    out, off = {}, 0
    for name, size in REPLICATED:
        out[name] = flat[off:off + size].reshape(shapes[name])
        off += _round_up(size, LANES)
    return out


def _lane_row(v):
    v = v.reshape(-1).astype(F32)
    return jnp.pad(v, (0, LANES - v.shape[0])).reshape(1, LANES)


IN_SPLIT = (("z", SSM_D_INNER), ("xbc", SSM_CONV_DIM), ("dt", SSM_N_HEADS), ("qkv", ATT_QKV_DIM), ("gate", 2 * D_MODEL))


def _split_w_in(w_t):
    out, off = {}, 0
    for name, size in IN_SPLIT:
        out[name] = w_t[off:off + size]
        off += size
    out["dt"] = jnp.pad(out["dt"], ((0, DT_PAD - SSM_N_HEADS), (0, 0)))
    return out


def _join_w_in(parts):
    parts = dict(parts)
    parts["dt"] = parts["dt"][:SSM_N_HEADS]
    return jnp.concatenate([parts[name] for name, _ in IN_SPLIT], axis=0)


def kernel(x, norm_mix, w_in, b_gate, conv_w, conv_b, dt_bias, a_log, d_skip, ssm_norm, w_ssm_out, w_att_out, w_mix_out, norm_ffn, w_ffn_gate, w_ffn_up, w_ffn_down, norm_final, loss_target, m_norm_mix, m_w_in, m_b_gate, m_conv_w, m_conv_b, m_dt_bias, m_a_log, m_d_skip, m_ssm_norm, m_w_ssm_out, m_w_att_out, m_w_mix_out, m_norm_ffn, m_w_ffn_gate, m_w_ffn_up, m_w_ffn_down, m_norm_final, v_norm_mix, v_w_in, v_b_gate, v_conv_w, v_conv_b, v_dt_bias, v_a_log, v_d_skip, v_ssm_norm, v_w_ssm_out, v_w_att_out, v_w_mix_out, v_norm_ffn, v_w_ffn_gate, v_w_ffn_up, v_w_ffn_down, v_norm_final):
    given = dict(locals())
    weights = {name: given[name][0] for name, _, _ in SHARDED}
    b, s, d = x.shape
    t = b * s

    mat_specs, conv_specs, all_specs = _stacking(MATRIX_SHARDS), _stacking((CONV_SHARD,)), _stacking(SHARDED)
    stacking = _to_stacking(weights, SHARDED)
    mat_local = _pack_sharded(stacking, mat_specs, 16, BF16)
    conv_local = _pack_sharded(stacking, conv_specs, 8, F32)
    mat_all, conv_all = _all_gather([mat_local, conv_local], "weights_all_gather")
    shards = _unpack_sharded(mat_all, mat_specs, (N_DEV,))
    shards.update(_unpack_sharded(conv_all, conv_specs, (N_DEV,)))
    full = {name: shards[name].reshape(N_DEV * shape[0], shape[1]) for name, shape, _ in all_specs}
    w_sec = _split_w_in(full["w_in"])
    conv_taps = full["conv_w"].T

    g_mix, g_ffn, g_fin = norm_mix.reshape(1, d), norm_ffn.reshape(1, d), norm_final.reshape(1, d)
    bg_row = b_gate.reshape(1, 2 * d)
    convb_row = conv_b.reshape(1, SSM_CONV_DIM)
    ssmn_row = ssm_norm.reshape(1, SSM_D_INNER)
    dtb_row, alog_row = _lane_row(dt_bias), _lane_row(a_log)
    cosf, sinf = _rope_tables(s)

    x2d = x.reshape(t, d)
    h1 = _rmsnorm_fwd(x2d, g_mix, "norm_mix_fwd")
    proj = {name: _mm(h1, w_sec[name], mode="nt", name="in_proj_" + name) for name, _ in IN_SPLIT}
    xbc3 = proj["xbc"].reshape(b, s, SSM_CONV_DIM)
    xc = _conv_fwd(xbc3, conv_taps, convb_row, "conv_fwd")
    dtr3 = proj["dt"].reshape(b, s, DT_PAD)
    to_channels, to_heads = _head_masks()
    dskx = jnp.repeat(d_skip.reshape(-1).astype(F32), SSM_HEAD_DIM).reshape(1, SSM_D_INNER)
    y_ssd, h_states = _ssd_fwd(xc, dtr3, dtb_row, alog_row, dskx, to_channels, "ssd_fwd")
    y_ssd2 = y_ssd.reshape(t, SSM_D_INNER)
    ynorm = _gate_norm_fwd(y_ssd2, proj["z"], ssmn_row, "ssd_gate_norm_fwd")
    y_ssm = _mm(ynorm, full["w_ssm_out"], mode="nn", name="ssm_out_proj")

    qkv3 = proj["qkv"].reshape(b, s, ATT_QKV_DIM)
    qk_parts = _rope_fwd(qkv3, cosf, sinf, "rope_fwd")
    att_parts = [_att_fwd(qk_parts[gi], "att_fwd_%d" % r) for gi, r in enumerate(ATT_DILATIONS)]
    att, *lse_parts = _att_merge([o for o, _ in att_parts], [l_ for _, l_ in att_parts], "att_merge")
    att2 = att.reshape(t, ATT_OUT_DIM)
    y_att = _mm(att2, full["w_att_out"], mode="nt", name="att_out_proj")

    mixed = _mix_fwd(proj["gate"], bg_row, y_ssm, y_att, "mix_fwd")
    x2 = _mm(mixed, full["w_mix_out"], mode="nn", name="mix_out_proj", add=x2d)
    h2 = _rmsnorm_fwd(x2, g_ffn, "norm_ffn_fwd")
    gt = _mm(h2, full["w_ffn_gate"], mode="nt", name="ffn_gate_proj")
    up = _mm(h2, full["w_ffn_up"], mode="nt", name="ffn_up_proj")
    act = _swiglu_fwd(gt, up, "swiglu_fwd")
    x3 = _mm(act, full["w_ffn_down"], mode="nn", name="ffn_down_proj", add=x2)

    loss_row, dx3, dg_fin, dx3b = _loss_head(x3, g_fin, loss_target.reshape(t, d), "loss_head")
    grads = {}
    dact = _mm(dx3b, full["w_ffn_down"], mode="nt", name="ffn_down_dx")
    grads["w_ffn_down"] = _mm(act, dx3b, mode="tn", name="ffn_down_dw", out_dtype=BF16)
    dgt, dup = _swiglu_bwd(gt, up, dact, "swiglu_bwd")
    grads["w_ffn_gate"] = _mm(dgt, h2, mode="tn", name="ffn_gate_dw", out_dtype=BF16)
    grads["w_ffn_up"] = _mm(dup, h2, mode="tn", name="ffn_up_dw", out_dtype=BF16)
    dh2 = _mm(dgt, full["w_ffn_gate"], mode="nn", name="ffn_gate_dx")
    dh2 = _mm(dup, full["w_ffn_up"], mode="nn", name="ffn_up_dx", add=dh2)
    dx2, dg_ffn, dx2b = _rmsnorm_bwd(x2, g_ffn, dh2, dx3, "norm_ffn_bwd", with_bf16=True)

    dmixed = _mm(dx2b, full["w_mix_out"], mode="nt", name="mix_out_dx")
    grads["w_mix_out"] = _mm(mixed, dx2b, mode="tn", name="mix_out_dw", out_dtype=BF16)
    dys, dya, dgl, dbg = _mix_bwd(proj["gate"], bg_row, y_ssm, y_att, dmixed, "mix_bwd")

    grads["w_ssm_out"] = _mm(ynorm, dys, mode="tn", name="ssm_out_dw", out_dtype=BF16)
    dynorm = _mm(dys, full["w_ssm_out"], mode="nt", name="ssm_out_dx")
    dy_ssd, dz, dssmn = _gate_norm_bwd(y_ssd2, proj["z"], ssmn_row, dynorm, "ssd_gate_norm_bwd")
    dxc, ddtr, dalog, ddsk, ddtb = _ssd_bwd(xc, dtr3, dy_ssd.reshape(b, s, SSM_D_INNER), h_states,
                                            dtb_row, alog_row, dskx, to_channels, to_heads, "ssd_bwd")
    dxbc, dconvw, dconvb = _conv_bwd(xbc3, dxc, conv_taps, convb_row, "conv_bwd")
    grads["conv_w"] = dconvw.T.astype(BF16)

    grads["w_att_out"] = _mm(dya, att2, mode="tn", name="att_out_dw", out_dtype=BF16)
    datt = _mm(dya, full["w_att_out"], mode="nn", name="att_out_dx").reshape(b, s, ATT_OUT_DIM)
    do_parts, dl_parts = _att_delta(att, datt, "att_delta")
    dqs, dks, dvs = [], [], []
    for gi, r in enumerate(ATT_DILATIONS):
        operands = (qk_parts[gi], do_parts[gi], lse_parts[gi], dl_parts[gi])
        dqs.append(_att_bwd_q(*operands, "att_bwd_q_%d" % r))
        dk_g, dv_g = _att_bwd_kv(*operands, "att_bwd_kv_%d" % r)
        dks.append(dk_g)
        dvs.append(dv_g)
    dqkv = _rope_bwd(dqs, dks, dvs, cosf, sinf, "rope_bwd")

    dproj = {"z": dz, "xbc": dxbc.reshape(t, SSM_CONV_DIM), "dt": ddtr.reshape(t, DT_PAD),
             "qkv": dqkv.reshape(t, ATT_QKV_DIM), "gate": dgl}
    grads["w_in"] = _join_w_in({name: _mm(dproj[name], h1, mode="tn", name="in_proj_dw_" + name, out_dtype=BF16)
                                for name, _ in IN_SPLIT})
    k_all = sum(dproj[name].shape[1] for name, _ in IN_SPLIT)
    k_pad = _round_up(k_all, 2048) - k_all
    dproj_all = jnp.concatenate([dproj[name] for name, _ in IN_SPLIT] + [jnp.zeros((t, k_pad), BF16)], axis=1)
    w_in_all = jnp.concatenate([w_sec[name] for name, _ in IN_SPLIT] + [jnp.zeros((k_pad, d), BF16)], axis=0)
    dh1 = _mm(dproj_all, w_in_all, mode="nn", name="in_proj_dx")
    grad_x, dg_mix = _rmsnorm_bwd(x2d, g_mix, dh1, dx2, "norm_mix_bwd")

    slabs = jnp.concatenate([grads[name].reshape(N_DEV, -1) for name, _, _ in all_specs], axis=1)
    slab_rows = _round_up(-(-slabs.shape[1] // LANES), PACK_ROWS)
    slabs = jnp.pad(slabs, ((0, 0), (0, slab_rows * LANES - slabs.shape[1]))).reshape(N_DEV, slab_rows, LANES)
    small = {"norm_mix": dg_mix, "b_gate": dbg, "conv_b": dconvb, "dt_bias": ddtb[:, :SSM_N_HEADS],
             "a_log": dalog[:, :SSM_N_HEADS], "d_skip": ddsk[:, :SSM_N_HEADS], "ssm_norm": dssmn,
             "norm_ffn": dg_ffn, "norm_final": dg_fin}
    core = lax.axis_index("c").astype(jnp.int32).reshape(1)
    chip_sums = _chip_sum(slabs, _pair_exchange(slabs, "grad_pair_exchange"), core, "grad_chip_sum")
    shared = _pack_replicated(small)
    shared = shared.at[LOSS_ROW, 0].set(loss_row[0, 0])
    got, got_small = _chip_exchange(chip_sums, shared, "grad_chip_exchange")

    def packed(prefix):
        vals = _to_stacking({name: given[prefix + name][0] for name, _, _ in SHARDED}, SHARDED)
        rep = {name: given[prefix + name] for name, _ in REPLICATED}
        return _pack_sharded(vals, all_specs, PACK_ROWS, F32), _pack_replicated(rep)

    (w_big, w_small), (m_big, m_small), (v_big, v_small) = packed(""), packed("m_"), packed("v_")
    big = _adamw(got, w_big, m_big, v_big, "adamw_sharded")
    sml = _adamw(got_small, w_small, m_small, v_small, "adamw_replicated")

    outs = [sml[0][LOSS_ROW, 0], grad_x.reshape(b, s, d)]
    rep_shapes = {name: given[name].shape for name, _ in REPLICATED}
    order = ["norm_mix", "w_in", "b_gate", "conv_w", "conv_b", "dt_bias", "a_log", "d_skip", "ssm_norm", "w_ssm_out",
             "w_att_out", "w_mix_out", "norm_ffn", "w_ffn_gate", "w_ffn_up", "w_ffn_down", "norm_final"]
    for big_k, sml_k in zip(big, sml):
        sharded = _to_stacking(_unpack_sharded(big_k, all_specs), SHARDED)
        rep = _unpack_replicated(sml_k, rep_shapes)
        for name in order:
            outs.append(sharded[name][None] if name in sharded else rep[name])
    return tuple(outs)
```

```python
import functools
import math

import jax
import jax.numpy as jnp
from jax import lax
from jax.experimental import pallas as pl
from jax.experimental.pallas import tpu as pltpu

F32 = jnp.float32
BF16 = jnp.bfloat16

N_DEV = 8
N_CHIPS = 4
D_MODEL = 1024
SSM_D_INNER = 2048
SSM_HEAD_DIM = 64
SSM_N_HEADS = 32
SSM_N_GROUPS = 4
SSM_HEADS_PER_GROUP = SSM_N_HEADS // SSM_N_GROUPS
SSM_D_STATE = 128
SSM_CONV = 4
SSM_CHUNK = 128
SSM_CONV_DIM = 3072
ATT_HEAD_DIM = 128
ATT_HEADS_PER_GROUP = 4
ATT_DILATIONS = (1, 4, 16)
ATT_N_HEADS = 12
ATT_QKV_DIM = 4608
ATT_OUT_DIM = 512
ATT_BLOCK = 128
ROPE_THETA = 10000.0
D_FF = 2816
IN_PROJ_DIM = 11808
EPS = 1e-6
LANES = 128
DT_PAD = LANES

ADAM_LR = 0.001
ADAM_B1 = 0.9
ADAM_B2 = 0.999
ADAM_EPS = 1e-08
ADAM_WD = 0.01
ADAM_STEP = 10

VMEM_LIMIT = 56 * 1024 * 1024
MESH = pl.DeviceIdType.MESH
NEG_INF = float("-inf")


def _pick(n, candidates):
    for c in candidates:
        if n % c == 0:
            return c
    return n


def _params(*sem):
    return pltpu.CompilerParams(dimension_semantics=sem, vmem_limit_bytes=VMEM_LIMIT)


def _sigmoid(x):
    return 1.0 / (1.0 + jnp.exp(-x))


def _softplus(x):
    return jnp.maximum(x, 0.0) + jnp.log(1.0 + jnp.exp(-jnp.abs(x)))


def _dot(a, b, dims):
    return lax.dot_general(a.astype(BF16), b.astype(BF16), (dims, ((), ())), preferred_element_type=F32)


def _nn(a, b):
    return _dot(a, b, ((1,), (0,)))


def _nt(a, b):
    return _dot(a, b, ((1,), (1,)))


def _tn(a, b):
    return _dot(a, b, ((0,), (0,)))


def _split3(v):
    hi = v.astype(BF16)
    r1 = v - hi.astype(F32)
    mid = r1.astype(BF16)
    lo = (r1 - mid.astype(F32)).astype(BF16)
    return hi, mid, lo


def _mask_nn(mask, v):
    mb = mask.astype(BF16)
    hi, mid, lo = _split3(v)
    return _nn(mb, hi) + (_nn(mb, mid) + _nn(mb, lo))


MM_VMEM_BUDGET = 40 * 1024 * 1024
MM_FULL_K = 2816


def _mm_tiles(m, n, k, a_bytes, b_bytes, o_bytes, has_add):
    tk = k if k <= MM_FULL_K else _pick(k, (2048, 1024, 512, 256, 128))
    tn = 1408 if (n > 1024 and n % 1408 == 0) else _pick(n, (1024, 768, 512, 384, 256, 128))
    for tm in (1408, 1024, 768, 512, 384, 256, 128):
        if m % tm:
            continue
        buffers = 2 * (tm * tk * a_bytes + tk * tn * b_bytes + tm * tn * (o_bytes + (4 if has_add else 0)))
        if tk < k:
            buffers += tm * tn * 4
        if buffers <= MM_VMEM_BUDGET:
            return tm, tn, tk
    return _pick(m, (128,)), tn, tk


def _mm(a, b, *, mode, name, out_dtype=F32, add=None):
    if mode == "nn":
        (m, k), n = a.shape, b.shape[1]
    elif mode == "nt":
        (m, k), n = a.shape, b.shape[0]
    else:
        (k, m), n = a.shape, b.shape[1]
    has_add = add is not None
    tm, tn, tk = _mm_tiles(m, n, k, a.dtype.itemsize, b.dtype.itemsize, jnp.dtype(out_dtype).itemsize, has_add)
    nk = k // tk
    dims = {"nn": ((1,), (0,)), "nt": ((1,), (1,)), "tn": ((0,), (0,))}[mode]
    a_spec = {"nn": pl.BlockSpec((tm, tk), lambda i, j, kk: (i, kk)),
              "nt": pl.BlockSpec((tm, tk), lambda i, j, kk: (i, kk)),
              "tn": pl.BlockSpec((tk, tm), lambda i, j, kk: (kk, i))}[mode]
    b_spec = {"nn": pl.BlockSpec((tk, tn), lambda i, j, kk: (kk, j)),
              "nt": pl.BlockSpec((tn, tk), lambda i, j, kk: (j, kk)),
              "tn": pl.BlockSpec((tk, tn), lambda i, j, kk: (kk, j))}[mode]
    o_spec = pl.BlockSpec((tm, tn), lambda i, j, kk: (i, j))

    def finish(r, c_ref, o_ref):
        if has_add:
            r = r + c_ref[...]
        o_ref[...] = r.astype(out_dtype)

    def body_one(*refs):
        a_ref, b_ref = refs[:2]
        finish(_dot(a_ref[...], b_ref[...], dims), refs[2] if has_add else None, refs[-1])

    def body_acc(*refs):
        a_ref, b_ref = refs[:2]
        o_ref, acc = refs[-2:]
        kk = pl.program_id(2)

        @pl.when(kk == 0)
        def _():
            acc[...] = jnp.zeros_like(acc)

        acc[...] += _dot(a_ref[...], b_ref[...], dims)

        @pl.when(kk == nk - 1)
        def _():
            finish(acc[...], refs[2] if has_add else None, o_ref)

    in_specs = [a_spec, b_spec] + ([o_spec] if has_add else [])
    args = (a, b) + ((add,) if has_add else ())
    return pl.pallas_call(
        body_one if nk == 1 else body_acc, name=name, grid=(m // tm, n // tn, nk),
        in_specs=in_specs, out_specs=o_spec,
        out_shape=jax.ShapeDtypeStruct((m, n), out_dtype),
        scratch_shapes=[] if nk == 1 else [pltpu.VMEM((tm, tn), F32)],
        compiler_params=_params("parallel", "parallel", "arbitrary"),
    )(*args)


def _rmsnorm_fwd(x, g, name):
    t, d = x.shape
    tm = _pick(t, (512, 256, 128))

    def body(x_ref, g_ref, o_ref):
        xv = x_ref[...]
        r = lax.rsqrt(jnp.mean(xv * xv, axis=-1, keepdims=True) + EPS)
        o_ref[...] = ((xv * r) * g_ref[...]).astype(BF16)

    return pl.pallas_call(
        body, name=name, grid=(t // tm,),
        in_specs=[pl.BlockSpec((tm, d), lambda i: (i, 0)), pl.BlockSpec((1, d), lambda i: (0, 0))],
        out_specs=pl.BlockSpec((tm, d), lambda i: (i, 0)),
        out_shape=jax.ShapeDtypeStruct((t, d), BF16),
        compiler_params=_params("parallel"),
    )(x, g)


def _rmsnorm_bwd(x, g, dh, dres, name, with_bf16=False):
    t, d = x.shape
    tm = _pick(t, (512, 256, 128))

    def body(x_ref, g_ref, dh_ref, dres_ref, dx_ref, dg_ref, *dxb_ref):
        @pl.when(pl.program_id(0) == 0)
        def _():
            dg_ref[...] = jnp.zeros_like(dg_ref)

        xv = x_ref[...]
        r = lax.rsqrt(jnp.mean(xv * xv, axis=-1, keepdims=True) + EPS)
        xhat = xv * r
        dhv = dh_ref[...]
        dyg = dhv * g_ref[...]
        dx = dres_ref[...] + r * (dyg - xhat * jnp.mean(dyg * xhat, axis=-1, keepdims=True))
        dx_ref[...] = dx
        if with_bf16:
            dxb_ref[0][...] = dx.astype(BF16)
        dg_ref[...] += jnp.sum(dhv * xhat, axis=0, keepdims=True)

    row = pl.BlockSpec((tm, d), lambda i: (i, 0))
    vec = pl.BlockSpec((1, d), lambda i: (0, 0))
    extra = with_bf16 * [jax.ShapeDtypeStruct((t, d), BF16)]
    return pl.pallas_call(
        body, name=name, grid=(t // tm,),
        in_specs=[row, vec, row, row], out_specs=[row, vec] + with_bf16 * [row],
        out_shape=[jax.ShapeDtypeStruct((t, d), F32), jax.ShapeDtypeStruct((1, d), F32)] + extra,
        compiler_params=_params("arbitrary"),
    )(x, g, dh, dres)


def _loss_head(x, g, target, name):
    t, d = x.shape
    tm = _pick(t, (512, 256, 128))

    def body(x_ref, g_ref, t_ref, loss_ref, dx_ref, dg_ref, dxb_ref):
        @pl.when(pl.program_id(0) == 0)
        def _():
            dg_ref[...] = jnp.zeros_like(dg_ref)
            loss_ref[...] = jnp.zeros_like(loss_ref)

        xv = x_ref[...]
        gv = g_ref[...]
        r = lax.rsqrt(jnp.mean(xv * xv, axis=-1, keepdims=True) + EPS)
        xhat = xv * r
        err = xhat * gv - t_ref[...]
        loss_ref[...] += jnp.sum(err * err) * (0.5 / d)
        dy = err * (1.0 / d)
        dyg = dy * gv
        dx = r * (dyg - xhat * jnp.mean(dyg * xhat, axis=-1, keepdims=True))
        dx_ref[...] = dx
        dxb_ref[...] = dx.astype(BF16)
        dg_ref[...] += jnp.sum(dy * xhat, axis=0, keepdims=True)

    row = pl.BlockSpec((tm, d), lambda i: (i, 0))
    vec = pl.BlockSpec((1, d), lambda i: (0, 0))
    return pl.pallas_call(
        body, name=name, grid=(t // tm,),
        in_specs=[row, vec, row],
        out_specs=[pl.BlockSpec((1, LANES), lambda i: (0, 0)), row, vec, row],
        out_shape=[jax.ShapeDtypeStruct((1, LANES), F32), jax.ShapeDtypeStruct((t, d), F32),
                   jax.ShapeDtypeStruct((1, d), F32), jax.ShapeDtypeStruct((t, d), BF16)],
        compiler_params=_params("arbitrary"),
    )(x, g, target)


CONV_HALO = 8
CONV_ROWS = 64


def _conv_taps(window, wv, bv):
    acc = bv + wv[SSM_CONV - 1:SSM_CONV, :] * window(0)
    for sh in range(1, SSM_CONV):
        kidx = SSM_CONV - 1 - sh
        acc = acc + wv[kidx:kidx + 1, :] * window(sh)
    return acc


def _conv_fwd(u, w, bias, name):
    b, s, c = u.shape
    rows = CONV_ROWS

    def body(u_ref, w_ref, b_ref, o_ref, ext):
        ext[0:CONV_HALO, :] = jnp.zeros((CONV_HALO, LANES), F32)
        ext[CONV_HALO:, :] = u_ref[...]
        wv, bv = w_ref[...], b_ref[...]
        for r0 in range(0, s, rows):
            acc = _conv_taps(lambda sh: ext[CONV_HALO + r0 - sh:CONV_HALO + r0 - sh + rows, :], wv, bv)
            o_ref[r0:r0 + rows, :] = acc * _sigmoid(acc)

    strip = pl.BlockSpec((None, s, LANES), lambda bi, j: (bi, 0, j))
    return pl.pallas_call(
        body, name=name, grid=(b, c // LANES),
        in_specs=[strip, pl.BlockSpec((SSM_CONV, LANES), lambda bi, j: (0, j)),
                  pl.BlockSpec((1, LANES), lambda bi, j: (0, j))],
        out_specs=strip, out_shape=jax.ShapeDtypeStruct((b, s, c), F32),
        scratch_shapes=[pltpu.VMEM((CONV_HALO + s, LANES), F32)],
        compiler_params=_params("parallel", "parallel"),
    )(u, w, bias)


def _conv_bwd(u, dout, w, bias, name):
    b, s, c = u.shape
    rows = CONV_ROWS

    def fold(v):
        return jnp.sum(v.reshape(rows // CONV_HALO, CONV_HALO, LANES), axis=0)

    def body(u_ref, d_ref, w_ref, b_ref, du_ref, dw_ref, db_ref, ext, dpre):
        @pl.when(pl.program_id(1) == 0)
        def _():
            dw_ref[...] = jnp.zeros_like(dw_ref)
            db_ref[...] = jnp.zeros_like(db_ref)

        ext[0:CONV_HALO, :] = jnp.zeros((CONV_HALO, LANES), F32)
        ext[CONV_HALO:, :] = u_ref[...]
        dpre[s:, :] = jnp.zeros((CONV_HALO, LANES), F32)
        wv, bv = w_ref[...], b_ref[...]
        sums = [jnp.zeros((CONV_HALO, LANES), F32)] * (SSM_CONV + 1)
        for r0 in range(0, s, rows):
            window = lambda sh: ext[CONV_HALO + r0 - sh:CONV_HALO + r0 - sh + rows, :]
            acc = _conv_taps(window, wv, bv)
            sg = _sigmoid(acc)
            dp = d_ref[r0:r0 + rows, :] * (sg * (1.0 + acc * (1.0 - sg)))
            dpre[r0:r0 + rows, :] = dp
            taps = [sums[SSM_CONV - 1 - sh] + fold(dp * window(sh)) for sh in range(SSM_CONV)]
            sums = taps[::-1] + [sums[SSM_CONV] + fold(dp)]
        for r0 in range(0, s, rows):
            du = wv[SSM_CONV - 1:SSM_CONV, :] * dpre[r0:r0 + rows, :]
            for sh in range(1, SSM_CONV):
                kidx = SSM_CONV - 1 - sh
                du = du + wv[kidx:kidx + 1, :] * dpre[r0 + sh:r0 + sh + rows, :]
            du_ref[r0:r0 + rows, :] = du.astype(BF16)
        for kidx in range(SSM_CONV):
            dw_ref[kidx:kidx + 1, :] += jnp.sum(sums[kidx], axis=0, keepdims=True)
        db_ref[...] += jnp.sum(sums[SSM_CONV], axis=0, keepdims=True)

    strip = pl.BlockSpec((None, s, LANES), lambda j, bi: (bi, 0, j))
    taps = pl.BlockSpec((SSM_CONV, LANES), lambda j, bi: (0, j))
    vec = pl.BlockSpec((1, LANES), lambda j, bi: (0, j))
    return pl.pallas_call(
        body, name=name, grid=(c // LANES, b),
        in_specs=[strip, strip, taps, vec], out_specs=[strip, taps, vec],
        out_shape=[jax.ShapeDtypeStruct((b, s, c), BF16), jax.ShapeDtypeStruct((SSM_CONV, c), F32),
                   jax.ShapeDtypeStruct((1, c), F32)],
        scratch_shapes=[pltpu.VMEM((CONV_HALO + s, LANES), F32), pltpu.VMEM((s + CONV_HALO, LANES), F32)],
        compiler_params=_params("parallel", "arbitrary"),
    )(u, dout, w, bias)


def _ssd_chunk_terms(dtr_ref, bias_ref, alog_ref):
    q = SSM_CHUNK
    dt = _softplus(dtr_ref[...] + bias_ref[...])
    a_neg = -jnp.exp(alog_ref[...])
    row = lax.broadcasted_iota(jnp.int32, (q, q), 0)
    col = lax.broadcasted_iota(jnp.int32, (q, q), 1)
    lower = row >= col
    s = _mask_nn(lower, dt * a_neg)
    return dt, a_neg, s, s.T, lower


def _head_masks():
    heads = jnp.arange(LANES)[:, None]
    chans = jnp.arange(SSM_D_INNER)[None, :]
    to_channels = (chans // SSM_HEAD_DIM == heads).astype(BF16)
    return to_channels, to_channels.T


def _per_channel(v, to_channels):
    hi = v.astype(BF16)
    lo = (v - hi.astype(F32)).astype(BF16)
    return _nn(hi, to_channels) + _nn(lo, to_channels)


def _per_head(v, to_heads):
    hi = v.astype(BF16)
    lo = (v - hi.astype(F32)).astype(BF16)
    return _nn(hi, to_heads) + _nn(lo, to_heads)


def _decay_terms_per_channel(dt, s_col, to_channels):
    q = SSM_CHUNK
    tot = s_col[q - 1:q, :]
    stacked = jnp.concatenate([dt, jnp.exp(s_col), jnp.exp(tot - s_col)], axis=0)
    wide = _per_channel(stacked, to_channels)
    dtx, esx, decx = wide[:q], wide[q:2 * q], wide[2 * q:]
    return dtx, esx, decx, esx[0:1, :] * decx[0:1, :]


SSM_PAIRS_PER_GROUP = SSM_HEADS_PER_GROUP // 2
SSM_GROUP_CHANNELS = SSM_HEADS_PER_GROUP * SSM_HEAD_DIM


def _split_pair(v):
    first = lax.broadcasted_iota(jnp.int32, v.shape, 1) < SSM_HEAD_DIM
    return jnp.concatenate([jnp.where(first, v, 0.0), jnp.where(first, 0.0, v)], axis=0)


def _ssd_fwd(xc, dtr, dt_bias, a_log, dskx, to_channels, name):
    b, s, _ = xc.shape
    q = SSM_CHUNK
    nc = s // q
    n, gc = SSM_D_STATE, SSM_GROUP_CHANNELS

    def body(xc_ref, dtr_ref, bias_ref, alog_ref, dsk_ref, tc_ref, y_ref, hs_ref, h_scr):
        @pl.when(pl.program_id(1) == 0)
        def _():
            h_scr[...] = jnp.zeros_like(h_scr)

        dt, _, s_col, s_row, lower = _ssd_chunk_terms(dtr_ref, bias_ref, alog_ref)
        dtx, esx, decx, etotx = _decay_terms_per_channel(dt, s_col, tc_ref[...])
        x = xc_ref[:, :SSM_D_INNER]
        xdt = x * dtx
        xdec = xdt * decx
        skip = dsk_ref[...] * x
        for g in range(SSM_N_GROUPS):
            bg = xc_ref[:, SSM_D_INNER + n * g:SSM_D_INNER + n * (g + 1)].astype(BF16)
            cg = xc_ref[:, SSM_D_INNER + n * (SSM_N_GROUPS + g):SSM_D_INNER + n * (SSM_N_GROUPS + g + 1)].astype(BF16)
            gsl = slice(gc * g, gc * (g + 1))
            gm = _nt(cg, bg)
            hgt = h_scr[:, gsl]
            hs_ref[:, gsl] = hgt
            y_off = esx[:, gsl] * _nn(cg, hgt)
            h_scr[:, gsl] = etotx[:, gsl] * hgt + _tn(bg, xdec[:, gsl])
            for k in range(SSM_PAIRS_PER_GROUP):
                h0 = g * SSM_HEADS_PER_GROUP + 2 * k
                lo = gc * g + LANES * k
                ms = []
                for h in (h0, h0 + 1):
                    lm = jnp.exp(jnp.where(lower, s_col[:, h:h + 1] - s_row[h:h + 1, :], NEG_INF))
                    ms.append((gm * lm).astype(BF16))
                y_diag = _nn(jnp.concatenate(ms, axis=1), _split_pair(xdt[:, lo:lo + LANES]))
                y_ref[:, lo:lo + LANES] = y_diag + y_off[:, LANES * k:LANES * (k + 1)] + skip[:, lo:lo + LANES]

    vec = pl.BlockSpec((1, LANES), lambda bi, c: (0, 0))
    return pl.pallas_call(
        body, name=name, grid=(b, nc),
        in_specs=[pl.BlockSpec((None, q, SSM_CONV_DIM), lambda bi, c: (bi, c, 0)),
                  pl.BlockSpec((None, q, LANES), lambda bi, c: (bi, c, 0)), vec, vec,
                  pl.BlockSpec((1, SSM_D_INNER), lambda bi, c: (0, 0)),
                  pl.BlockSpec((LANES, SSM_D_INNER), lambda bi, c: (0, 0))],
        out_specs=[pl.BlockSpec((None, q, SSM_D_INNER), lambda bi, c: (bi, c, 0)),
                   pl.BlockSpec((None, None, n, SSM_D_INNER), lambda bi, c: (bi, c, 0, 0))],
        out_shape=[jax.ShapeDtypeStruct((b, s, SSM_D_INNER), F32),
                   jax.ShapeDtypeStruct((b, nc, n, SSM_D_INNER), F32)],
        scratch_shapes=[pltpu.VMEM((n, SSM_D_INNER), F32)],
        compiler_params=_params("parallel", "arbitrary"),
    )(xc, dtr, dt_bias, a_log, dskx, to_channels)


def _ssd_bwd(xc, dtr, dy, hs, dt_bias, a_log, dskx, to_channels, to_heads, name):
    b, s, _ = xc.shape
    q = SSM_CHUNK
    nc = s // q
    n, gc = SSM_D_STATE, SSM_GROUP_CHANNELS

    def colsum(v):
        return jnp.sum(v, axis=0, keepdims=True)

    def body(xc_ref, dtr_ref, dy_ref, hs_ref, bias_ref, alog_ref, dsk_ref, tc_ref, th_ref,
             dxc_ref, ddtr_ref, dalog_ref, ddsk_ref, dbias_ref, dh_scr, dxs_scr, dxd_scr, w_scr, dst_scr, rows_scr):
        ci = pl.program_id(1)

        @pl.when(ci == 0)
        def _():
            dh_scr[...] = jnp.zeros_like(dh_scr)

        @pl.when(jnp.logical_and(pl.program_id(0) == 0, ci == 0))
        def _():
            dalog_ref[...] = jnp.zeros_like(dalog_ref)
            ddsk_ref[...] = jnp.zeros_like(ddsk_ref)
            dbias_ref[...] = jnp.zeros_like(dbias_ref)
            dst_scr[...] = jnp.zeros_like(dst_scr)

        dt, a_neg, s_col, s_row, lower = _ssd_chunk_terms(dtr_ref, bias_ref, alog_ref)
        upper = jnp.logical_not(lower) | (lax.broadcasted_iota(jnp.int32, (q, q), 0)
                                          == lax.broadcasted_iota(jnp.int32, (q, q), 1))
        dtx, esx, decx, etotx = _decay_terms_per_channel(dt, s_col, tc_ref[...])
        x = xc_ref[:, :SSM_D_INNER]
        dyv = dy_ref[...]
        xdt = x * dtx
        xdec = xdt * decx
        dw = esx * dyv
        rows_scr[...] = jnp.zeros_like(rows_scr)
        for g in range(SSM_N_GROUPS):
            b_lo = SSM_D_INNER + n * g
            c_lo = SSM_D_INNER + n * (SSM_N_GROUPS + g)
            bg = xc_ref[:, b_lo:b_lo + n].astype(BF16)
            cg = xc_ref[:, c_lo:c_lo + n].astype(BF16)
            gsl = slice(gc * g, gc * (g + 1))
            gm = _nt(cg, bg)
            gmt = _nt(bg, cg)
            hgt = hs_ref[:, gsl]
            dhgt = dh_scr[:, gsl]
            w_scr[:, gsl] = _nn(cg, hgt)
            dcg = _nt(dw[:, gsl], hgt)
            dxs = decx[:, gsl] * _nn(bg, dhgt)
            dxs_scr[:, gsl] = dxs
            dbg = _nt(xdec[:, gsl], dhgt)
            rows_scr[2:3, gsl] = colsum(dhgt * hgt)
            dh_scr[:, gsl] = _tn(cg, dw[:, gsl]) + etotx[:, gsl] * dhgt
            dg = jnp.zeros((q, q), F32)
            dgt = jnp.zeros((q, q), F32)
            for k in range(SSM_PAIRS_PER_GROUP):
                h0 = g * SSM_HEADS_PER_GROUP + 2 * k
                lo = gc * g + LANES * k
                xp = xdt[:, lo:lo + LANES]
                dyp = dyv[:, lo:lo + LANES]
                dy2 = _split_pair(dyp)
                dm2 = _nt(dy2, xp)
                dmt2 = _nt(_split_pair(xp), dyp)
                mts = []
                for i, h in enumerate((h0, h0 + 1)):
                    lm = jnp.exp(jnp.where(lower, s_col[:, h:h + 1] - s_row[h:h + 1, :], NEG_INF))
                    lmt = jnp.exp(jnp.where(upper, s_row[h:h + 1, :] - s_col[:, h:h + 1], NEG_INF))
                    dm = dm2[q * i:q * (i + 1), :]
                    dmt = dmt2[q * i:q * (i + 1), :]
                    dg = dg + dm * lm
                    dgt = dgt + dmt * lmt
                    mt = gmt * lmt
                    dst_scr[h:h + 1, :] = colsum(dmt * mt) - colsum(dm * (gm * lm))
                    mts.append(mt.astype(BF16))
                dxd_scr[:, lo:lo + LANES] = _nn(jnp.concatenate(mts, axis=1), dy2)
            dxc_ref[:, b_lo:b_lo + n] = dbg + _nn(dgt, cg)
            dxc_ref[:, c_lo:c_lo + n] = dcg + _nn(dg, bg)
        dxs = dxs_scr[...]
        dxdt = dxd_scr[...] + dxs
        dxc_ref[:, :SSM_D_INNER] = dxdt * dtx + dsk_ref[...] * dyv
        state_part = xdt * dxs
        rows_scr[0:1, :] = colsum(dyv * x)
        rows_scr[1:2, :] = colsum(state_part)
        th = th_ref[...]
        per_head = _per_head(jnp.concatenate([dw * w_scr[...] - state_part, dxdt * x], axis=0), th)
        r_ds, r_dt = per_head[:q], per_head[q:]
        sums = _per_head(rows_scr[...], th)
        etot = jnp.exp(s_col[q - 1:q, :])
        dtot = sums[1:2, :] + etot * sums[2:3, :]
        last = lax.broadcasted_iota(jnp.int32, (q, LANES), 0) == q - 1
        ds = dst_scr[...].T + r_ds + jnp.where(last, dtot, 0.0)
        da = _mask_nn(upper, ds)
        ddt = da * a_neg + r_dt
        live = lax.broadcasted_iota(jnp.int32, (1, LANES), 1) < SSM_N_HEADS
        sg = _sigmoid(dtr_ref[...] + bias_ref[...])
        ddtr = jnp.where(live, ddt * sg, 0.0)
        ddtr_ref[...] = ddtr.astype(BF16)
        dalog_ref[...] += jnp.where(live, colsum(da * dt) * a_neg, 0.0)
        ddsk_ref[...] += jnp.where(live, sums[0:1, :], 0.0)
        dbias_ref[...] += colsum(ddtr)

    rev = lambda bi, c: (bi, nc - 1 - c, 0)
    vec = pl.BlockSpec((1, LANES), lambda bi, c: (0, 0))
    wide = pl.BlockSpec((None, q, SSM_D_INNER), rev)
    return pl.pallas_call(
        body, name=name, grid=(b, nc),
        in_specs=[pl.BlockSpec((None, q, SSM_CONV_DIM), rev), pl.BlockSpec((None, q, LANES), rev), wide,
                  pl.BlockSpec((None, None, n, SSM_D_INNER), lambda bi, c: (bi, nc - 1 - c, 0, 0)),
                  vec, vec, pl.BlockSpec((1, SSM_D_INNER), lambda bi, c: (0, 0)),
                  pl.BlockSpec((LANES, SSM_D_INNER), lambda bi, c: (0, 0)),
                  pl.BlockSpec((SSM_D_INNER, LANES), lambda bi, c: (0, 0))],
        out_specs=[pl.BlockSpec((None, q, SSM_CONV_DIM), rev), pl.BlockSpec((None, q, LANES), rev), vec, vec, vec],
        out_shape=[jax.ShapeDtypeStruct((b, s, SSM_CONV_DIM), F32), jax.ShapeDtypeStruct((b, s, LANES), BF16),
                   jax.ShapeDtypeStruct((1, LANES), F32), jax.ShapeDtypeStruct((1, LANES), F32),
                   jax.ShapeDtypeStruct((1, LANES), F32)],
        scratch_shapes=[pltpu.VMEM((n, SSM_D_INNER), F32)] + [pltpu.VMEM((q, SSM_D_INNER), F32)] * 3
        + [pltpu.VMEM((LANES, q), F32), pltpu.VMEM((8, SSM_D_INNER), F32)],
        compiler_params=_params("arbitrary", "arbitrary"),
    )(xc, dtr, dy, hs, dt_bias, a_log, dskx, to_channels, to_heads)


SSM_GROUP_WIDTH = SSM_D_INNER // SSM_N_GROUPS


def _gate_norm_fwd(y, z, w, name):
    t, d = y.shape
    tm = _pick(t, (256, 128))

    def body(y_ref, z_ref, w_ref, o_ref):
        for g in range(SSM_N_GROUPS):
            sl = slice(SSM_GROUP_WIDTH * g, SSM_GROUP_WIDTH * (g + 1))
            zv = z_ref[:, sl]
            u = y_ref[:, sl] * (zv * _sigmoid(zv))
            r = lax.rsqrt(jnp.mean(u * u, axis=-1, keepdims=True) + EPS)
            o_ref[:, sl] = ((u * r) * w_ref[:, sl]).astype(BF16)

    row = pl.BlockSpec((tm, d), lambda i: (i, 0))
    return pl.pallas_call(
        body, name=name, grid=(t // tm,),
        in_specs=[row, row, pl.BlockSpec((1, d), lambda i: (0, 0))], out_specs=row,
        out_shape=jax.ShapeDtypeStruct((t, d), BF16),
        compiler_params=_params("parallel"),
    )(y, z, w)


def _gate_norm_bwd(y, z, w, dout, name):
    t, d = y.shape
    tm = _pick(t, (256, 128))

    def body(y_ref, z_ref, w_ref, do_ref, dy_ref, dz_ref, dw_ref):
        @pl.when(pl.program_id(0) == 0)
        def _():
            dw_ref[...] = jnp.zeros_like(dw_ref)

        for g in range(SSM_N_GROUPS):
            sl = slice(SSM_GROUP_WIDTH * g, SSM_GROUP_WIDTH * (g + 1))
            zv = z_ref[:, sl]
            yv = y_ref[:, sl]
            sg = _sigmoid(zv)
            silu = zv * sg
            u = yv * silu
            r = lax.rsqrt(jnp.mean(u * u, axis=-1, keepdims=True) + EPS)
            uh = u * r
            dov = do_ref[:, sl]
            dw_ref[:, sl] += jnp.sum(dov * uh, axis=0, keepdims=True)
            dyg = dov * w_ref[:, sl]
            du = r * (dyg - uh * jnp.mean(dyg * uh, axis=-1, keepdims=True))
            dy_ref[:, sl] = du * silu
            dz_ref[:, sl] = (du * yv * (sg * (1.0 + zv * (1.0 - sg)))).astype(BF16)

    row = pl.BlockSpec((tm, d), lambda i: (i, 0))
    vec = pl.BlockSpec((1, d), lambda i: (0, 0))
    return pl.pallas_call(
        body, name=name, grid=(t // tm,),
        in_specs=[row, row, vec, row], out_specs=[row, row, vec],
        out_shape=[jax.ShapeDtypeStruct((t, d), F32), jax.ShapeDtypeStruct((t, d), BF16),
                   jax.ShapeDtypeStruct((1, d), F32)],
        compiler_params=_params("arbitrary"),
    )(y, z, w, dout)


def _rope_tables(s):
    half = ATT_HEAD_DIM // 2
    inv = ROPE_THETA ** (-jnp.arange(half, dtype=F32) / half)
    ang = jnp.arange(s).astype(F32)[:, None] * inv[None, :]
    cos, sin = jnp.cos(ang), jnp.sin(ang)
    return jnp.concatenate([cos, cos], axis=-1), jnp.concatenate([-sin, sin], axis=-1)


ATT_TILE = 256


def _by_residue_spec(r, width):
    return pl.BlockSpec((None, r, ATT_TILE // r, width), lambda bi, i: (bi, 0, i, 0))


def _to_residues(tile, stage, r, store):
    if r == 1:
        store(0, tile)
        return
    stage[...] = tile
    for ri in range(r):
        store(ri, stage[pl.ds(ri, ATT_TILE // r, stride=r), :])


def _from_residues(load, stage, r):
    if r == 1:
        return load(0)
    for ri in range(r):
        stage[pl.ds(ri, ATT_TILE // r, stride=r), :] = load(ri)
    return stage[...]


def _rope_fwd(qkv, cosf, sinf, name):
    b, s, w = qkv.shape
    ts, d, gw = ATT_TILE, ATT_HEAD_DIM, ATT_OUT_DIM

    def body(x_ref, c_ref, s_ref, *rest):
        outs, stage = rest[:-1], rest[-1]
        cv, sv = c_ref[...], s_ref[...]
        for kind in range(3):
            for gi, r in enumerate(ATT_DILATIONS):
                for j in range(ATT_HEADS_PER_GROUP):
                    src = d * (kind * ATT_N_HEADS + gi * ATT_HEADS_PER_GROUP + j)
                    dst = slice(kind * gw + d * j, kind * gw + d * (j + 1))
                    tv = x_ref[:, src:src + d]
                    if kind < 2:
                        tv = tv * cv + pltpu.roll(tv, d // 2, 1) * sv

                    def store(ri, rows, o_ref=outs[gi], dst=dst):
                        o_ref[ri, :, dst] = rows.astype(BF16)

                    _to_residues(tv, stage, r, store)

    tab = pl.BlockSpec((ts, d), lambda bi, i: (i, 0))
    return pl.pallas_call(
        body, name=name, grid=(b, s // ts),
        in_specs=[pl.BlockSpec((None, ts, w), lambda bi, i: (bi, i, 0)), tab, tab],
        out_specs=[_by_residue_spec(r, 3 * gw) for r in ATT_DILATIONS],
        out_shape=[jax.ShapeDtypeStruct((b, r, s // r, 3 * gw), BF16) for r in ATT_DILATIONS],
        scratch_shapes=[pltpu.VMEM((ts, d), F32)],
        compiler_params=_params("parallel", "parallel"),
    )(qkv, cosf, sinf)


def _rope_bwd(dq, dk, dv, cosf, sinf, name):
    n_pat = len(ATT_DILATIONS)
    b, _, s, gw = dq[0].shape
    ts, d = ATT_TILE, ATT_HEAD_DIM

    def body(*refs):
        ins, (c_ref, s_ref, o_ref, stage) = refs[:3 * n_pat], refs[3 * n_pat:]
        cv, sv = c_ref[...], s_ref[...]
        for kind in range(3):
            for gi, r in enumerate(ATT_DILATIONS):
                src = ins[kind * n_pat + gi]
                for j in range(ATT_HEADS_PER_GROUP):
                    tv = _from_residues(lambda ri, src=src, j=j: src[ri, :, d * j:d * (j + 1)], stage, r)
                    if kind < 2:
                        tv = tv * cv + pltpu.roll(tv * sv, d // 2, 1)
                    lo = d * (kind * ATT_N_HEADS + gi * ATT_HEADS_PER_GROUP + j)
                    o_ref[:, lo:lo + d] = tv.astype(BF16)

    tab = pl.BlockSpec((ts, d), lambda bi, i: (i, 0))
    parts = [_by_residue_spec(r, gw) for r in ATT_DILATIONS]
    return pl.pallas_call(
        body, name=name, grid=(b, s // ts), in_specs=parts * 3 + [tab, tab],
        out_specs=pl.BlockSpec((None, ts, ATT_QKV_DIM), lambda bi, i: (bi, i, 0)),
        out_shape=jax.ShapeDtypeStruct((b, s, ATT_QKV_DIM), BF16),
        scratch_shapes=[pltpu.VMEM((ts, d), F32)],
        compiler_params=_params("parallel", "parallel"),
    )(*dq, *dk, *dv, cosf, sinf)


ATT_SCALE = ATT_HEAD_DIM ** -0.5
ATT_STEP = 2 * ATT_BLOCK


def _att_spec(col):
    return pl.BlockSpec((None, None, ATT_STEP, ATT_OUT_DIM), lambda bi, ri, i: (bi, ri, i, col))


def _att_edge_spec(col, side, n_steps):
    def index(bi, ri, i):
        blk = 2 * i - 1 if side < 0 else 2 * i + 2
        return (bi, ri, jnp.clip(blk, 0, 2 * n_steps - 1), col)
    return pl.BlockSpec((None, None, ATT_BLOCK, ATT_OUT_DIM), index)


def _band_mask(shape, q_axis, has_prev):
    qi = lax.broadcasted_iota(jnp.int32, shape, q_axis)
    kj = lax.broadcasted_iota(jnp.int32, shape, 1 - q_axis)
    dist = qi + ATT_BLOCK - kj
    return (dist >= 0) & (dist <= ATT_BLOCK) & (has_prev | (kj >= ATT_BLOCK))


def _att_fwd(qkr, name):
    b, r, l, _ = qkr.shape
    nb = l // ATT_STEP
    d = ATT_HEAD_DIM

    def body(q_ref, kp_ref, k_ref, vp_ref, v_ref, o_ref, lse_ref):
        mask = _band_mask((ATT_STEP, ATT_BLOCK + ATT_STEP), 0, pl.program_id(2) > 0)
        for j in range(ATT_HEADS_PER_GROUP):
            sl = slice(d * j, d * (j + 1))
            kcat = jnp.concatenate([kp_ref[:, sl], k_ref[:, sl]], axis=0)
            vcat = jnp.concatenate([vp_ref[:, sl], v_ref[:, sl]], axis=0)
            sc = jnp.where(mask, _nt(q_ref[:, sl], kcat) * ATT_SCALE, NEG_INF)
            m = jnp.max(sc, axis=-1, keepdims=True)
            pr = jnp.exp(sc - m)
            den = jnp.sum(pr, axis=-1, keepdims=True)
            o_ref[:, sl] = _nn(pr / den, vcat)
            lse_ref[:, sl] = jnp.broadcast_to(m + jnp.log(den), (ATT_STEP, d))

    out_spec = _att_spec(0)
    return pl.pallas_call(
        body, name=name, grid=(b, r, nb),
        in_specs=[_att_spec(0), _att_edge_spec(1, -1, nb), _att_spec(1), _att_edge_spec(2, -1, nb), _att_spec(2)],
        out_specs=[out_spec, out_spec],
        out_shape=[jax.ShapeDtypeStruct((b, r, l, ATT_OUT_DIM), F32)] * 2,
        compiler_params=_params("parallel", "parallel", "parallel"),
    )(qkr, qkr, qkr, qkr, qkr)


def _att_merge(os_, lses, name):
    n_pat = len(os_)
    b, _, s, gw = os_[0].shape
    ts, d = ATT_TILE, ATT_HEAD_DIM

    def body(*refs):
        o_refs, l_refs = refs[:n_pat], refs[n_pat:2 * n_pat]
        att_ref, lse_outs, stage = refs[2 * n_pat], refs[2 * n_pat + 1:3 * n_pat + 1], refs[-1]
        for j in range(ATT_HEADS_PER_GROUP):
            sl = slice(d * j, d * (j + 1))
            ov = [_from_residues(lambda ri, g=g: o_refs[g][ri, :, sl], stage, r)
                  for g, r in enumerate(ATT_DILATIONS)]
            ls = [_from_residues(lambda ri, g=g: l_refs[g][ri, :, sl], stage, r)
                  for g, r in enumerate(ATT_DILATIONS)]
            m = functools.reduce(jnp.maximum, ls)
            es = [jnp.exp(lv - m) for lv in ls]
            tot = functools.reduce(lambda u, v: u + v, es)
            acc = (es[0] / tot) * ov[0]
            for g in range(1, n_pat):
                acc = acc + (es[g] / tot) * ov[g]
            att_ref[:, sl] = acc
            joint = m + jnp.log(tot)
            for g, r in enumerate(ATT_DILATIONS):
                def store(ri, rows, out=lse_outs[g]):
                    out[ri, :, sl] = rows
                _to_residues(joint, stage, r, store)

    parts = [_by_residue_spec(r, gw) for r in ATT_DILATIONS]
    return pl.pallas_call(
        body, name=name, grid=(b, s // ts), in_specs=parts * 2,
        out_specs=[pl.BlockSpec((None, ts, gw), lambda bi, i: (bi, i, 0))] + parts,
        out_shape=[jax.ShapeDtypeStruct((b, s, gw), F32)]
        + [jax.ShapeDtypeStruct((b, r, s // r, gw), F32) for r in ATT_DILATIONS],
        scratch_shapes=[pltpu.VMEM((ts, d), F32)],
        compiler_params=_params("parallel", "parallel"),
    )(*os_, *lses)


def _att_delta(att, datt, name):
    b, s, gw = att.shape
    ts, d = ATT_TILE, ATT_HEAD_DIM
    n_pat = len(ATT_DILATIONS)

    def body(a_ref, d_ref, *rest):
        do_outs, dl_outs, stage = rest[:n_pat], rest[n_pat:2 * n_pat], rest[-1]
        for j in range(ATT_HEADS_PER_GROUP):
            sl = slice(d * j, d * (j + 1))
            dv = d_ref[:, sl]
            delta = jnp.broadcast_to(jnp.sum(a_ref[:, sl] * dv, axis=-1, keepdims=True), (ts, d))
            for g, r in enumerate(ATT_DILATIONS):
                def store_do(ri, rows, out=do_outs[g]):
                    out[ri, :, sl] = rows.astype(BF16)

                def store_dl(ri, rows, out=dl_outs[g]):
                    out[ri, :, sl] = rows

                _to_residues(dv, stage, r, store_do)
                _to_residues(delta, stage, r, store_dl)

    row = pl.BlockSpec((None, ts, gw), lambda bi, i: (bi, i, 0))
    parts = [_by_residue_spec(r, gw) for r in ATT_DILATIONS]
    outs = pl.pallas_call(
        body, name=name, grid=(b, s // ts), in_specs=[row, row], out_specs=parts * 2,
        out_shape=[jax.ShapeDtypeStruct((b, r, s // r, gw), BF16) for r in ATT_DILATIONS]
        + [jax.ShapeDtypeStruct((b, r, s // r, gw), F32) for r in ATT_DILATIONS],
        scratch_shapes=[pltpu.VMEM((ts, d), F32)],
        compiler_params=_params("parallel", "parallel"),
    )(att, datt)
    return outs[:n_pat], outs[n_pat:]


def _att_bwd_q(qkr, datt, lse, delta, name):
    b, r, l, _ = qkr.shape
    nb = l // ATT_STEP
    d = ATT_HEAD_DIM

    def body(q_ref, kp_ref, k_ref, vp_ref, v_ref, do_ref, lse_ref, dl_ref, dq_ref):
        mask = _band_mask((ATT_STEP, ATT_BLOCK + ATT_STEP), 0, pl.program_id(2) > 0)
        for j in range(ATT_HEADS_PER_GROUP):
            sl = slice(d * j, d * (j + 1))
            kcat = jnp.concatenate([kp_ref[:, sl], k_ref[:, sl]], axis=0)
            vcat = jnp.concatenate([vp_ref[:, sl], v_ref[:, sl]], axis=0)
            sc = _nt(q_ref[:, sl], kcat) * ATT_SCALE
            pr = jnp.exp(jnp.where(mask, sc - lse_ref[:, d * j:d * j + 1], NEG_INF))
            dp = _nt(do_ref[:, sl], vcat)
            dsc = pr * (dp - dl_ref[:, d * j:d * j + 1])
            dq_ref[:, sl] = _nn(dsc, kcat) * ATT_SCALE

    tok = _att_spec(0)
    return pl.pallas_call(
        body, name=name, grid=(b, r, nb),
        in_specs=[_att_spec(0), _att_edge_spec(1, -1, nb), _att_spec(1), _att_edge_spec(2, -1, nb), _att_spec(2),
                  tok, tok, tok],
        out_specs=tok,
        out_shape=jax.ShapeDtypeStruct((b, r, l, ATT_OUT_DIM), F32),
        compiler_params=_params("parallel", "parallel", "parallel"),
    )(qkr, qkr, qkr, qkr, qkr, datt, lse, delta)


def _att_bwd_kv(qkr, datt, lse, delta, name):
    b, r, l, _ = qkr.shape
    nb = l // ATT_STEP
    d = ATT_HEAD_DIM

    def body(k_ref, v_ref, q_ref, qn_ref, do_ref, don_ref, lse_ref, lsen_ref, dl_ref, dln_ref, dk_ref, dv_ref):
        shape = (ATT_STEP, ATT_STEP + ATT_BLOCK)
        kj = lax.broadcasted_iota(jnp.int32, shape, 0)
        qi = lax.broadcasted_iota(jnp.int32, shape, 1)
        dist = qi - kj
        has_next = pl.program_id(2) < nb - 1
        mask = (dist >= 0) & (dist <= ATT_BLOCK) & (has_next | (qi < ATT_STEP))
        for j in range(ATT_HEADS_PER_GROUP):
            sl = slice(d * j, d * (j + 1))
            qcat = jnp.concatenate([q_ref[:, sl], qn_ref[:, sl]], axis=0)
            docat = jnp.concatenate([do_ref[:, sl], don_ref[:, sl]], axis=0)
            lse_t = jnp.tile(jnp.concatenate([lse_ref[:, sl], lsen_ref[:, sl]], axis=0).T, (ATT_STEP // d, 1))
            dl_t = jnp.tile(jnp.concatenate([dl_ref[:, sl], dln_ref[:, sl]], axis=0).T, (ATT_STEP // d, 1))
            sc_t = _nt(k_ref[:, sl], qcat) * ATT_SCALE
            pr_t = jnp.exp(jnp.where(mask, sc_t - lse_t, NEG_INF))
            dv_ref[:, sl] = _nn(pr_t, docat)
            dsc_t = pr_t * (_nt(v_ref[:, sl], docat) - dl_t)
            dk_ref[:, sl] = _nn(dsc_t, qcat) * ATT_SCALE

    tok, tok_n = _att_spec(0), _att_edge_spec(0, 1, nb)
    return pl.pallas_call(
        body, name=name, grid=(b, r, nb),
        in_specs=[_att_spec(1), _att_spec(2), _att_spec(0), _att_edge_spec(0, 1, nb),
                  tok, tok_n, tok, tok_n, tok, tok_n],
        out_specs=[tok, tok],
        out_shape=[jax.ShapeDtypeStruct((b, r, l, ATT_OUT_DIM), F32)] * 2,
        compiler_params=_params("parallel", "parallel", "parallel"),
    )(qkr, qkr, qkr, qkr, datt, datt, lse, lse, delta, delta)


def _mix_fwd(gl, bg, ys, ya, name):
    t, d = ys.shape
    tm = _pick(t, (512, 256, 128))

    def body(gl_ref, bg_ref, ys_ref, ya_ref, o_ref):
        g0 = _sigmoid(gl_ref[:, :d] + bg_ref[:, :d])
        g1 = _sigmoid(gl_ref[:, d:] + bg_ref[:, d:])
        o_ref[...] = (g0 * ys_ref[...] + g1 * ya_ref[...]).astype(BF16)

    row = pl.BlockSpec((tm, d), lambda i: (i, 0))
    return pl.pallas_call(
        body, name=name, grid=(t // tm,),
        in_specs=[pl.BlockSpec((tm, 2 * d), lambda i: (i, 0)), pl.BlockSpec((1, 2 * d), lambda i: (0, 0)), row, row],
        out_specs=row, out_shape=jax.ShapeDtypeStruct((t, d), BF16),
        compiler_params=_params("parallel"),
    )(gl, bg, ys, ya)


def _mix_bwd(gl, bg, ys, ya, dmixed, name):
    t, d = ys.shape
    tm = _pick(t, (512, 256, 128))

    def body(gl_ref, bg_ref, ys_ref, ya_ref, dm_ref, dys_ref, dya_ref, dgl_ref, dbg_ref):
        @pl.when(pl.program_id(0) == 0)
        def _():
            dbg_ref[...] = jnp.zeros_like(dbg_ref)

        dm = dm_ref[...]
        g0 = _sigmoid(gl_ref[:, :d] + bg_ref[:, :d])
        g1 = _sigmoid(gl_ref[:, d:] + bg_ref[:, d:])
        dys_ref[...] = (dm * g0).astype(BF16)
        dya_ref[...] = (dm * g1).astype(BF16)
        d0 = dm * ys_ref[...] * (g0 * (1.0 - g0))
        d1 = dm * ya_ref[...] * (g1 * (1.0 - g1))
        dgl_ref[:, :d] = d0.astype(BF16)
        dgl_ref[:, d:] = d1.astype(BF16)
        dbg_ref[:, :d] += jnp.sum(d0, axis=0, keepdims=True)
        dbg_ref[:, d:] += jnp.sum(d1, axis=0, keepdims=True)

    row = pl.BlockSpec((tm, d), lambda i: (i, 0))
    wide = pl.BlockSpec((tm, 2 * d), lambda i: (i, 0))
    vec = pl.BlockSpec((1, 2 * d), lambda i: (0, 0))
    return pl.pallas_call(
        body, name=name, grid=(t // tm,),
        in_specs=[wide, vec, row, row, row], out_specs=[row, row, wide, vec],
        out_shape=[jax.ShapeDtypeStruct((t, d), BF16), jax.ShapeDtypeStruct((t, d), BF16),
                   jax.ShapeDtypeStruct((t, 2 * d), BF16), jax.ShapeDtypeStruct((1, 2 * d), F32)],
        compiler_params=_params("arbitrary"),
    )(gl, bg, ys, ya, dmixed)


def _swiglu_fwd(gt, up, name):
    t, f = gt.shape
    tm = _pick(t, (512, 256, 128))

    def body(g_ref, u_ref, o_ref):
        gv = g_ref[...]
        o_ref[...] = ((gv * _sigmoid(gv)) * u_ref[...]).astype(BF16)

    row = pl.BlockSpec((tm, f), lambda i: (i, 0))
    return pl.pallas_call(
        body, name=name, grid=(t // tm,), in_specs=[row, row], out_specs=row,
        out_shape=jax.ShapeDtypeStruct((t, f), BF16), compiler_params=_params("parallel"),
    )(gt, up)


def _swiglu_bwd(gt, up, dact, name):
    t, f = gt.shape
    tm = _pick(t, (512, 256, 128))

    def body(g_ref, u_ref, d_ref, dg_ref, du_ref):
        gv = g_ref[...]
        dv = d_ref[...]
        sg = _sigmoid(gv)
        dg_ref[...] = (dv * u_ref[...] * (sg * (1.0 + gv * (1.0 - sg)))).astype(BF16)
        du_ref[...] = (dv * (gv * sg)).astype(BF16)

    row = pl.BlockSpec((tm, f), lambda i: (i, 0))
    return pl.pallas_call(
        body, name=name, grid=(t // tm,), in_specs=[row, row, row], out_specs=[row, row],
        out_shape=[jax.ShapeDtypeStruct((t, f), BF16)] * 2, compiler_params=_params("parallel"),
    )(gt, up, dact)


def _peer(k):
    x, y, c = lax.axis_index("x"), lax.axis_index("y"), lax.axis_index("c")
    px, py, pc = x ^ ((k >> 2) & 1), y ^ ((k >> 1) & 1), c ^ (k & 1)
    return (px, py, pc), 4 * px + 2 * py + pc


def _my_index():
    return 4 * lax.axis_index("x") + 2 * lax.axis_index("y") + lax.axis_index("c")


def _all_gather(parts, name):
    n_parts = len(parts)

    def body(*refs):
        ins, outs = refs[:n_parts], refs[n_parts:2 * n_parts]
        send_sems, recv_sems, local_sems = refs[2 * n_parts:]
        here, me = _peer(0)
        sibling, sib_idx = _peer(1)
        chips = [_peer(2 * q) for q in range(1, N_CHIPS)]

        def copy(i, k, block, to, src=None):
            return pltpu.make_async_remote_copy(
                src_ref=outs[i].at[block] if src is None else src, dst_ref=outs[i].at[block],
                send_sem=send_sems.at[i * (N_DEV - 1) + k], recv_sem=recv_sems.at[i * (N_DEV - 1) + k],
                device_id=to, device_id_type=MESH)

        local = [pltpu.make_async_copy(ins[i], outs[i].at[me], local_sems.at[i]) for i in range(n_parts)]
        for cp in local:
            cp.start()
        sends = []
        for i in range(n_parts):
            sends.append(copy(i, 0, me, sibling, src=ins[i]))
            sends += [copy(i, q, me, chip, src=ins[i]) for q, (chip, _) in enumerate(chips, start=1)]
        for cp in sends:
            cp.start()
        for q, (chip, chip_idx) in enumerate(chips, start=1):
            for i in range(n_parts):
                copy(i, q, chip_idx, here).wait_recv()
                fwd = copy(i, N_CHIPS - 1 + q, chip_idx, sibling)
                fwd.start()
                sends.append(fwd)
        for i in range(n_parts):
            copy(i, 0, sib_idx, here).wait_recv()
        for q, (_, chip_idx) in enumerate(chips, start=1):
            for i in range(n_parts):
                copy(i, N_CHIPS - 1 + q, chip_idx ^ 1, here).wait_recv()
        for cp in sends:
            cp.wait_send()
        for cp in local:
            cp.wait()

    hbm = pl.BlockSpec(memory_space=pl.ANY)
    return pl.pallas_call(
        body, name=name, in_specs=[hbm] * n_parts, out_specs=[hbm] * n_parts,
        out_shape=[jax.ShapeDtypeStruct((N_DEV,) + p_.shape, p_.dtype) for p_ in parts],
        scratch_shapes=[pltpu.SemaphoreType.DMA((n_parts * (N_DEV - 1),)),
                        pltpu.SemaphoreType.DMA((n_parts * (N_DEV - 1),)),
                        pltpu.SemaphoreType.DMA((n_parts,))],
        compiler_params=pltpu.CompilerParams(has_side_effects=True),
    )(*parts)


def _pair_exchange(slabs, name):
    def body(slab_ref, got_ref, send_sems, recv_sems):
        c = lax.axis_index("c")
        sibling, _ = _peer(1)
        copies = [pltpu.make_async_remote_copy(
            src_ref=slab_ref.at[2 * q + 1 - c], dst_ref=got_ref.at[q], send_sem=send_sems.at[q],
            recv_sem=recv_sems.at[q], device_id=sibling, device_id_type=MESH) for q in range(N_CHIPS)]
        for cp in copies:
            cp.start()
        for cp in copies:
            cp.wait()

    hbm = pl.BlockSpec(memory_space=pl.ANY)
    return pl.pallas_call(
        body, name=name, in_specs=[hbm], out_specs=hbm,
        out_shape=jax.ShapeDtypeStruct((N_CHIPS,) + slabs.shape[1:], slabs.dtype),
        scratch_shapes=[pltpu.SemaphoreType.DMA((N_CHIPS,)), pltpu.SemaphoreType.DMA((N_CHIPS,))],
        compiler_params=pltpu.CompilerParams(has_side_effects=True),
    )(slabs)


def _chip_sum(slabs, got, core, name):
    _, rows, lanes = slabs.shape
    tr = _pick(rows, (512, 256, 128, 64, 32, 16, 8))

    def body(core_ref, mine_ref, got_ref, o_ref):
        o_ref[...] = (mine_ref[...].astype(F32) + got_ref[...].astype(F32)).astype(BF16)

    return pl.pallas_call(
        body, name=name,
        grid_spec=pltpu.PrefetchScalarGridSpec(
            num_scalar_prefetch=1, grid=(N_CHIPS, rows // tr),
            in_specs=[pl.BlockSpec((None, tr, lanes), lambda q, i, core_ref: (2 * q + core_ref[0], i, 0)),
                      pl.BlockSpec((None, tr, lanes), lambda q, i, core_ref: (q, i, 0))],
            out_specs=pl.BlockSpec((None, tr, lanes), lambda q, i, core_ref: (q, i, 0))),
        out_shape=jax.ShapeDtypeStruct((N_CHIPS, rows, lanes), BF16),
        compiler_params=_params("parallel", "parallel"),
    )(core, slabs, got)


def _chip_exchange(chip_sums, shared, name):
    def body(sum_ref, sh_ref, got_ref, gsh_ref, send_sems, recv_sems, sh_send_sems, sh_recv_sems, local_sems):
        me = _my_index()
        my_chip = me >> 1
        local = [pltpu.make_async_copy(sum_ref.at[my_chip], got_ref.at[my_chip], local_sems.at[0]),
                 pltpu.make_async_copy(sh_ref, gsh_ref.at[me], local_sems.at[1])]
        for cp in local:
            cp.start()
        sends = []
        for q in range(1, N_CHIPS):
            peer, pidx = _peer(2 * q)
            cp = pltpu.make_async_remote_copy(
                src_ref=sum_ref.at[pidx >> 1], dst_ref=got_ref.at[my_chip], send_sem=send_sems.at[q - 1],
                recv_sem=recv_sems.at[q - 1], device_id=peer, device_id_type=MESH)
            cp.start()
            sends.append(cp)
        for k in range(1, N_DEV):
            peer, _ = _peer(k)
            cp = pltpu.make_async_remote_copy(
                src_ref=sh_ref, dst_ref=gsh_ref.at[me], send_sem=sh_send_sems.at[k - 1],
                recv_sem=sh_recv_sems.at[k - 1], device_id=peer, device_id_type=MESH)
            cp.start()
            sends.append(cp)
        for q in range(1, N_CHIPS):
            peer, pidx = _peer(2 * q)
            pltpu.make_async_remote_copy(
                src_ref=sum_ref.at[my_chip], dst_ref=got_ref.at[pidx >> 1], send_sem=send_sems.at[q - 1],
                recv_sem=recv_sems.at[q - 1], device_id=peer, device_id_type=MESH).wait_recv()
        for k in range(1, N_DEV):
            peer, pidx = _peer(k)
            pltpu.make_async_remote_copy(
                src_ref=sh_ref, dst_ref=gsh_ref.at[pidx], send_sem=sh_send_sems.at[k - 1],
                recv_sem=sh_recv_sems.at[k - 1], device_id=peer, device_id_type=MESH).wait_recv()
        for cp in sends:
            cp.wait_send()
        for cp in local:
            cp.wait()

    hbm = pl.BlockSpec(memory_space=pl.ANY)
    return pl.pallas_call(
        body, name=name, in_specs=[hbm, hbm], out_specs=[hbm, hbm],
        out_shape=[jax.ShapeDtypeStruct(chip_sums.shape, chip_sums.dtype),
                   jax.ShapeDtypeStruct((N_DEV,) + shared.shape, shared.dtype)],
        scratch_shapes=[pltpu.SemaphoreType.DMA((N_CHIPS - 1,)), pltpu.SemaphoreType.DMA((N_CHIPS - 1,)),
                        pltpu.SemaphoreType.DMA((N_DEV - 1,)), pltpu.SemaphoreType.DMA((N_DEV - 1,)),
                        pltpu.SemaphoreType.DMA((2,))],
        compiler_params=pltpu.CompilerParams(has_side_effects=True),
    )(chip_sums, shared)


def _adamw(parts, w, m, v, name):
    n_parts, rows, lanes = parts.shape
    tr = _pick(rows, (512, 256, 128, 64, 32, 16, 8))
    c1 = 1.0 - ADAM_B1 ** ADAM_STEP
    c2 = 1.0 - ADAM_B2 ** ADAM_STEP

    def body(p_ref, w_ref, m_ref, v_ref, g_ref, d_ref, nm_ref, nv_ref):
        g = p_ref[0].astype(F32)
        for j in range(1, n_parts):
            g = g + p_ref[j].astype(F32)
        nm = ADAM_B1 * m_ref[...] + (1.0 - ADAM_B1) * g
        nv = ADAM_B2 * v_ref[...] + (1.0 - ADAM_B2) * (g * g)
        g_ref[...] = g
        nm_ref[...] = nm
        nv_ref[...] = nv
        d_ref[...] = -ADAM_LR * ((nm / c1) / (jnp.sqrt(nv / c2) + ADAM_EPS) + ADAM_WD * w_ref[...])

    row = pl.BlockSpec((tr, lanes), lambda i: (i, 0))
    return pl.pallas_call(
        body, name=name, grid=(rows // tr,),
        in_specs=[pl.BlockSpec((n_parts, tr, lanes), lambda i: (0, i, 0)), row, row, row],
        out_specs=[row] * 4, out_shape=[jax.ShapeDtypeStruct((rows, lanes), F32)] * 4,
        compiler_params=_params("parallel"),
    )(parts, w, m, v)


MATRIX_SHARDS = (
    ("w_in", (D_MODEL, IN_PROJ_DIM // N_DEV), True),
    ("w_ssm_out", (SSM_D_INNER // N_DEV, D_MODEL), False),
    ("w_att_out", (ATT_OUT_DIM, D_MODEL // N_DEV), True),
    ("w_mix_out", (D_MODEL // N_DEV, D_MODEL), False),
    ("w_ffn_gate", (D_MODEL, D_FF // N_DEV), True),
    ("w_ffn_up", (D_MODEL, D_FF // N_DEV), True),
    ("w_ffn_down", (D_FF // N_DEV, D_MODEL), False),
)
CONV_SHARD = ("conv_w", (SSM_CONV, SSM_CONV_DIM // N_DEV), True)
SHARDED = MATRIX_SHARDS + (CONV_SHARD,)
REPLICATED = (("norm_mix", D_MODEL), ("b_gate", 2 * D_MODEL), ("conv_b", SSM_CONV_DIM), ("dt_bias", SSM_N_HEADS),
              ("a_log", SSM_N_HEADS), ("d_skip", SSM_N_HEADS), ("ssm_norm", SSM_D_INNER), ("norm_ffn", D_MODEL),
              ("norm_final", D_MODEL))


PACK_ROWS = 512


def _round_up(n, mult):
    return -(-n // mult) * mult


def _pack_rows(flat, row_mult):
    rows = _round_up(-(-flat.shape[0] // LANES), row_mult)
    return jnp.pad(flat, (0, rows * LANES - flat.shape[0])).reshape(rows, LANES)


def _pack_sharded(vals, specs, row_mult, dtype):
    return _pack_rows(jnp.concatenate([vals[name].reshape(-1).astype(dtype) for name, _, _ in specs]), row_mult)


def _unpack_sharded(packed, specs, lead=()):
    flat = packed.reshape(lead + (-1,))
    out, off = {}, 0
    for name, shape, _ in specs:
        size = shape[0] * shape[1]
        out[name] = flat[..., off:off + size].reshape(lead + shape)
        off += size
    return out


def _stacking(specs):
    return tuple((name, (shape[1], shape[0]) if by_cols else shape, by_cols) for name, shape, by_cols in specs)


def _to_stacking(vals, specs):
    return {name: (vals[name].T if by_cols else vals[name]) for name, _, by_cols in specs}


REPLICATED_ROWS = sum(-(-size // LANES) for _, size in REPLICATED)
LOSS_ROW = REPLICATED_ROWS


def _pack_replicated(vals):
    rows = []
    for name, size in REPLICATED:
        v = vals[name].reshape(-1).astype(F32)
        rows.append(jnp.pad(v, (0, _round_up(size, LANES) - size)))
    return _pack_rows(jnp.concatenate(rows), 8)


def _unpack_replicated(packed, shapes):
    flat = packed.reshape(-1)
    out, off = {}, 0
    for name, size in REPLICATED:
        out[name] = flat[off:off + size].reshape(shapes[name])
        off += _round_up(size, LANES)
    return out


def _lane_row(v):
    v = v.reshape(-1).astype(F32)
    return jnp.pad(v, (0, LANES - v.shape[0])).reshape(1, LANES)


IN_SPLIT = (("z", SSM_D_INNER), ("xbc", SSM_CONV_DIM), ("dt", SSM_N_HEADS), ("qkv", ATT_QKV_DIM), ("gate", 2 * D_MODEL))


def _split_w_in(w_t):
    out, off = {}, 0
    for name, size in IN_SPLIT:
        out[name] = w_t[off:off + size]
        off += size
    out["dt"] = jnp.pad(out["dt"], ((0, DT_PAD - SSM_N_HEADS), (0, 0)))
    return out


def _join_w_in(parts):
    parts = dict(parts)
    parts["dt"] = parts["dt"][:SSM_N_HEADS]
    return jnp.concatenate([parts[name] for name, _ in IN_SPLIT], axis=0)


def kernel(x, norm_mix, w_in, b_gate, conv_w, conv_b, dt_bias, a_log, d_skip, ssm_norm, w_ssm_out, w_att_out, w_mix_out, norm_ffn, w_ffn_gate, w_ffn_up, w_ffn_down, norm_final, loss_target, m_norm_mix, m_w_in, m_b_gate, m_conv_w, m_conv_b, m_dt_bias, m_a_log, m_d_skip, m_ssm_norm, m_w_ssm_out, m_w_att_out, m_w_mix_out, m_norm_ffn, m_w_ffn_gate, m_w_ffn_up, m_w_ffn_down, m_norm_final, v_norm_mix, v_w_in, v_b_gate, v_conv_w, v_conv_b, v_dt_bias, v_a_log, v_d_skip, v_ssm_norm, v_w_ssm_out, v_w_att_out, v_w_mix_out, v_norm_ffn, v_w_ffn_gate, v_w_ffn_up, v_w_ffn_down, v_norm_final):
    given = dict(locals())
    weights = {name: given[name][0] for name, _, _ in SHARDED}
    b, s, d = x.shape
    t = b * s

    mat_specs, conv_specs, all_specs = _stacking(MATRIX_SHARDS), _stacking((CONV_SHARD,)), _stacking(SHARDED)
    stacking = _to_stacking(weights, SHARDED)
    mat_local = _pack_sharded(stacking, mat_specs, 16, BF16)
    conv_local = _pack_sharded(stacking, conv_specs, 8, F32)
    mat_all, conv_all = _all_gather([mat_local, conv_local], "weights_all_gather")
    shards = _unpack_sharded(mat_all, mat_specs, (N_DEV,))
    shards.update(_unpack_sharded(conv_all, conv_specs, (N_DEV,)))
    full = {name: shards[name].reshape(N_DEV * shape[0], shape[1]) for name, shape, _ in all_specs}
    w_sec = _split_w_in(full["w_in"])
    conv_taps = full["conv_w"].T

    g_mix, g_ffn, g_fin = norm_mix.reshape(1, d), norm_ffn.reshape(1, d), norm_final.reshape(1, d)
    bg_row = b_gate.reshape(1, 2 * d)
    convb_row = conv_b.reshape(1, SSM_CONV_DIM)
    ssmn_row = ssm_norm.reshape(1, SSM_D_INNER)
    dtb_row, alog_row = _lane_row(dt_bias), _lane_row(a_log)
    cosf, sinf = _rope_tables(s)

    x2d = x.reshape(t, d)
    h1 = _rmsnorm_fwd(x2d, g_mix, "norm_mix_fwd")
    proj = {name: _mm(h1, w_sec[name], mode="nt", name="in_proj_" + name) for name, _ in IN_SPLIT}
    xbc3 = proj["xbc"].reshape(b, s, SSM_CONV_DIM)
    xc = _conv_fwd(xbc3, conv_taps, convb_row, "conv_fwd")
    dtr3 = proj["dt"].reshape(b, s, DT_PAD)
    to_channels, to_heads = _head_masks()
    dskx = jnp.repeat(d_skip.reshape(-1).astype(F32), SSM_HEAD_DIM).reshape(1, SSM_D_INNER)
    y_ssd, h_states = _ssd_fwd(xc, dtr3, dtb_row, alog_row, dskx, to_channels, "ssd_fwd")
    y_ssd2 = y_ssd.reshape(t, SSM_D_INNER)
    ynorm = _gate_norm_fwd(y_ssd2, proj["z"], ssmn_row, "ssd_gate_norm_fwd")
    y_ssm = _mm(ynorm, full["w_ssm_out"], mode="nn", name="ssm_out_proj")

    qkv3 = proj["qkv"].reshape(b, s, ATT_QKV_DIM)
    qk_parts = _rope_fwd(qkv3, cosf, sinf, "rope_fwd")
    att_parts = [_att_fwd(qk_parts[gi], "att_fwd_%d" % r) for gi, r in enumerate(ATT_DILATIONS)]
    att, *lse_parts = _att_merge([o for o, _ in att_parts], [l_ for _, l_ in att_parts], "att_merge")
    att2 = att.reshape(t, ATT_OUT_DIM)
    y_att = _mm(att2, full["w_att_out"], mode="nt", name="att_out_proj")

    mixed = _mix_fwd(proj["gate"], bg_row, y_ssm, y_att, "mix_fwd")
    x2 = _mm(mixed, full["w_mix_out"], mode="nn", name="mix_out_proj", add=x2d)
    h2 = _rmsnorm_fwd(x2, g_ffn, "norm_ffn_fwd")
    gt = _mm(h2, full["w_ffn_gate"], mode="nt", name="ffn_gate_proj")
    up = _mm(h2, full["w_ffn_up"], mode="nt", name="ffn_up_proj")
    act = _swiglu_fwd(gt, up, "swiglu_fwd")
    x3 = _mm(act, full["w_ffn_down"], mode="nn", name="ffn_down_proj", add=x2)

    loss_row, dx3, dg_fin, dx3b = _loss_head(x3, g_fin, loss_target.reshape(t, d), "loss_head")
    grads = {}
    dact = _mm(dx3b, full["w_ffn_down"], mode="nt", name="ffn_down_dx")
    grads["w_ffn_down"] = _mm(act, dx3b, mode="tn", name="ffn_down_dw", out_dtype=BF16)
    dgt, dup = _swiglu_bwd(gt, up, dact, "swiglu_bwd")
    grads["w_ffn_gate"] = _mm(dgt, h2, mode="tn", name="ffn_gate_dw", out_dtype=BF16)
    grads["w_ffn_up"] = _mm(dup, h2, mode="tn", name="ffn_up_dw", out_dtype=BF16)
    dh2 = _mm(dgt, full["w_ffn_gate"], mode="nn", name="ffn_gate_dx")
    dh2 = _mm(dup, full["w_ffn_up"], mode="nn", name="ffn_up_dx", add=dh2)
    dx2, dg_ffn, dx2b = _rmsnorm_bwd(x2, g_ffn, dh2, dx3, "norm_ffn_bwd", with_bf16=True)

    dmixed = _mm(dx2b, full["w_mix_out"], mode="nt", name="mix_out_dx")
    grads["w_mix_out"] = _mm(mixed, dx2b, mode="tn", name="mix_out_dw", out_dtype=BF16)
    dys, dya, dgl, dbg = _mix_bwd(proj["gate"], bg_row, y_ssm, y_att, dmixed, "mix_bwd")

    grads["w_ssm_out"] = _mm(ynorm, dys, mode="tn", name="ssm_out_dw", out_dtype=BF16)
    dynorm = _mm(dys, full["w_ssm_out"], mode="nt", name="ssm_out_dx")
    dy_ssd, dz, dssmn = _gate_norm_bwd(y_ssd2, proj["z"], ssmn_row, dynorm, "ssd_gate_norm_bwd")
    dxc, ddtr, dalog, ddsk, ddtb = _ssd_bwd(xc, dtr3, dy_ssd.reshape(b, s, SSM_D_INNER), h_states,
                                            dtb_row, alog_row, dskx, to_channels, to_heads, "ssd_bwd")
    dxbc, dconvw, dconvb = _conv_bwd(xbc3, dxc, conv_taps, convb_row, "conv_bwd")
    grads["conv_w"] = dconvw.T.astype(BF16)

    grads["w_att_out"] = _mm(dya, att2, mode="tn", name="att_out_dw", out_dtype=BF16)
    datt = _mm(dya, full["w_att_out"], mode="nn", name="att_out_dx").reshape(b, s, ATT_OUT_DIM)
    do_parts, dl_parts = _att_delta(att, datt, "att_delta")
    dqs, dks, dvs = [], [], []
    for gi, r in enumerate(ATT_DILATIONS):
        operands = (qk_parts[gi], do_parts[gi], lse_parts[gi], dl_parts[gi])
        dqs.append(_att_bwd_q(*operands, "att_bwd_q_%d" % r))
        dk_g, dv_g = _att_bwd_kv(*operands, "att_bwd_kv_%d" % r)
        dks.append(dk_g)
        dvs.append(dv_g)
    dqkv = _rope_bwd(dqs, dks, dvs, cosf, sinf, "rope_bwd")

    dproj = {"z": dz, "xbc": dxbc.reshape(t, SSM_CONV_DIM), "dt": ddtr.reshape(t, DT_PAD),
             "qkv": dqkv.reshape(t, ATT_QKV_DIM), "gate": dgl}
    grads["w_in"] = _join_w_in({name: _mm(dproj[name], h1, mode="tn", name="in_proj_dw_" + name, out_dtype=BF16)
                                for name, _ in IN_SPLIT})
    k_all = sum(dproj[name].shape[1] for name, _ in IN_SPLIT)
    k_pad = _round_up(k_all, 2048) - k_all
    dproj_all = jnp.concatenate([dproj[name] for name, _ in IN_SPLIT] + [jnp.zeros((t, k_pad), BF16)], axis=1)
    w_in_all = jnp.concatenate([w_sec[name] for name, _ in IN_SPLIT] + [jnp.zeros((k_pad, d), BF16)], axis=0)
    dh1 = _mm(dproj_all, w_in_all, mode="nn", name="in_proj_dx")
    grad_x, dg_mix = _rmsnorm_bwd(x2d, g_mix, dh1, dx2, "norm_mix_bwd")

    slabs = jnp.concatenate([grads[name].reshape(N_DEV, -1) for name, _, _ in all_specs], axis=1)
    slab_rows = _round_up(-(-slabs.shape[1] // LANES), PACK_ROWS)
    slabs = jnp.pad(slabs, ((0, 0), (0, slab_rows * LANES - slabs.shape[1]))).reshape(N_DEV, slab_rows, LANES)
    small = {"norm_mix": dg_mix, "b_gate": dbg, "conv_b": dconvb, "dt_bias": ddtb[:, :SSM_N_HEADS],
             "a_log": dalog[:, :SSM_N_HEADS], "d_skip": ddsk[:, :SSM_N_HEADS], "ssm_norm": dssmn,
             "norm_ffn": dg_ffn, "norm_final": dg_fin}
    core = lax.axis_index("c").astype(jnp.int32).reshape(1)
    chip_sums = _chip_sum(slabs, _pair_exchange(slabs, "grad_pair_exchange"), core, "grad_chip_sum")
    shared = _pack_replicated(small)
    shared = shared.at[LOSS_ROW, 0].set(loss_row[0, 0])
    got, got_small = _chip_exchange(chip_sums, shared, "grad_chip_exchange")

    def packed(prefix):
        vals = _to_stacking({name: given[prefix + name][0] for name, _, _ in SHARDED}, SHARDED)
        rep = {name: given[prefix + name] for name, _ in REPLICATED}
        return _pack_sharded(vals, all_specs, PACK_ROWS, F32), _pack_replicated(rep)

    (w_big, w_small), (m_big, m_small), (v_big, v_small) = packed(""), packed("m_"), packed("v_")
    big = _adamw(got, w_big, m_big, v_big, "adamw_sharded")
    sml = _adamw(got_small, w_small, m_small, v_small, "adamw_replicated")

    outs = [sml[0][LOSS_ROW, 0], grad_x.reshape(b, s, d)]
    rep_shapes = {name: given[name].shape for name, _ in REPLICATED}
    order = ["norm_mix", "w_in", "b_gate", "conv_w", "conv_b", "dt_bias", "a_log", "d_skip", "ssm_norm", "w_ssm_out",
             "w_att_out", "w_mix_out", "norm_ffn", "w_ffn_gate", "w_ffn_up", "w_ffn_down", "norm_final"]
    for big_k, sml_k in zip(big, sml):
        sharded = _to_stacking(_unpack_sharded(big_k, all_specs), SHARDED)
        rep = _unpack_replicated(sml_k, rep_shapes)
        for name in order:
            outs.append(sharded[name][None] if name in sharded else rep[name])
    return tuple(outs)
```

```python
import functools
import math

import jax
import jax.numpy as jnp
from jax import lax
from jax.experimental import pallas as pl
from jax.experimental.pallas import tpu as pltpu

F32 = jnp.float32
BF16 = jnp.bfloat16

N_DEV = 8
N_CHIPS = 4
D_MODEL = 1024
SSM_D_INNER = 2048
SSM_HEAD_DIM = 64
SSM_N_HEADS = 32
SSM_N_GROUPS = 4
SSM_HEADS_PER_GROUP = SSM_N_HEADS // SSM_N_GROUPS
SSM_D_STATE = 128
SSM_CONV = 4
SSM_CHUNK = 128
SSM_CONV_DIM = 3072
ATT_HEAD_DIM = 128
ATT_HEADS_PER_GROUP = 4
ATT_DILATIONS = (1, 4, 16)
ATT_N_HEADS = 12
ATT_QKV_DIM = 4608
ATT_OUT_DIM = 512
ATT_BLOCK = 128
ROPE_THETA = 10000.0
D_FF = 2816
IN_PROJ_DIM = 11808
EPS = 1e-6
LANES = 128
DT_PAD = LANES

DPROJ_COLS = {"qkv": 0, "z": 4608, "xbc": 6656, "dt": 9728, "gate": 10240}
DPROJ_DT_WIDTH = 512
DPROJ_WIDTH = 12288

ADAM_LR = 0.001
ADAM_B1 = 0.9
ADAM_B2 = 0.999
ADAM_EPS = 1e-08
ADAM_WD = 0.01
ADAM_STEP = 10

VMEM_LIMIT = 56 * 1024 * 1024
MESH = pl.DeviceIdType.MESH
NEG_INF = float("-inf")


def _pick(n, candidates):
    for c in candidates:
        if n % c == 0:
            return c
    return n


def _params(*sem):
    return pltpu.CompilerParams(dimension_semantics=sem, vmem_limit_bytes=VMEM_LIMIT)


def _sigmoid(x):
    return 1.0 / (1.0 + jnp.exp(-x))


def _softplus(x):
    return jnp.maximum(x, 0.0) + jnp.log(1.0 + jnp.exp(-jnp.abs(x)))


def _dot(a, b, dims):
    return lax.dot_general(a.astype(BF16), b.astype(BF16), (dims, ((), ())), preferred_element_type=F32)


def _nn(a, b):
    return _dot(a, b, ((1,), (0,)))


def _nt(a, b):
    return _dot(a, b, ((1,), (1,)))


def _tn(a, b):
    return _dot(a, b, ((0,), (0,)))


def _split3(v):
    hi = v.astype(BF16)
    r1 = v - hi.astype(F32)
    mid = r1.astype(BF16)
    lo = (r1 - mid.astype(F32)).astype(BF16)
    return hi, mid, lo


def _mask_nn(mask, v):
    mb = mask.astype(BF16)
    hi, mid, lo = _split3(v)
    return _nn(mb, hi) + (_nn(mb, mid) + _nn(mb, lo))


MM_VMEM_BUDGET = 40 * 1024 * 1024
MM_FULL_K = 2816


def _mm_tiles(m, n, k, a_bytes, b_bytes, o_bytes, has_add):
    tk = k if k <= MM_FULL_K else _pick(k, (2048, 1024, 512, 256, 128))
    tn = 1408 if (n > 1024 and n % 1408 == 0) else _pick(n, (1024, 768, 512, 384, 256, 128))
    for tm in (1408, 1024, 768, 512, 384, 256, 128):
        if m % tm:
            continue
        buffers = 2 * (tm * tk * a_bytes + tk * tn * b_bytes + tm * tn * (o_bytes + (4 if has_add else 0)))
        if tk < k:
            buffers += tm * tn * 4
        if buffers <= MM_VMEM_BUDGET:
            return tm, tn, tk
    return _pick(m, (128,)), tn, tk


def _mm(a, b, *, mode, name, out_dtype=F32, add=None):
    if mode == "nn":
        (m, k), n = a.shape, b.shape[1]
    elif mode == "nt":
        (m, k), n = a.shape, b.shape[0]
    else:
        (k, m), n = a.shape, b.shape[1]
    has_add = add is not None
    tm, tn, tk = _mm_tiles(m, n, k, a.dtype.itemsize, b.dtype.itemsize, jnp.dtype(out_dtype).itemsize, has_add)
    nk = k // tk
    dims = {"nn": ((1,), (0,)), "nt": ((1,), (1,)), "tn": ((0,), (0,))}[mode]
    a_spec = {"nn": pl.BlockSpec((tm, tk), lambda i, j, kk: (i, kk)),
              "nt": pl.BlockSpec((tm, tk), lambda i, j, kk: (i, kk)),
              "tn": pl.BlockSpec((tk, tm), lambda i, j, kk: (kk, i))}[mode]
    b_spec = {"nn": pl.BlockSpec((tk, tn), lambda i, j, kk: (kk, j)),
              "nt": pl.BlockSpec((tn, tk), lambda i, j, kk: (j, kk)),
              "tn": pl.BlockSpec((tk, tn), lambda i, j, kk: (kk, j))}[mode]
    o_spec = pl.BlockSpec((tm, tn), lambda i, j, kk: (i, j))

    def finish(r, c_ref, o_ref):
        if has_add:
            r = r + c_ref[...]
        o_ref[...] = r.astype(out_dtype)

    def body_one(*refs):
        a_ref, b_ref = refs[:2]
        finish(_dot(a_ref[...], b_ref[...], dims), refs[2] if has_add else None, refs[-1])

    def body_acc(*refs):
        a_ref, b_ref = refs[:2]
        o_ref, acc = refs[-2:]
        kk = pl.program_id(2)

        @pl.when(kk == 0)
        def _():
            acc[...] = jnp.zeros_like(acc)

        acc[...] += _dot(a_ref[...], b_ref[...], dims)

        @pl.when(kk == nk - 1)
        def _():
            finish(acc[...], refs[2] if has_add else None, o_ref)

    in_specs = [a_spec, b_spec] + ([o_spec] if has_add else [])
    args = (a, b) + ((add,) if has_add else ())
    return pl.pallas_call(
        body_one if nk == 1 else body_acc, name=name, grid=(m // tm, n // tn, nk),
        in_specs=in_specs, out_specs=o_spec,
        out_shape=jax.ShapeDtypeStruct((m, n), out_dtype),
        scratch_shapes=[] if nk == 1 else [pltpu.VMEM((tm, tn), F32)],
        compiler_params=_params("parallel", "parallel", "arbitrary"),
    )(*args)


def _rmsnorm_fwd(x, g, name):
    t, d = x.shape
    tm = _pick(t, (512, 256, 128))

    def body(x_ref, g_ref, o_ref):
        xv = x_ref[...]
        r = lax.rsqrt(jnp.mean(xv * xv, axis=-1, keepdims=True) + EPS)
        o_ref[...] = ((xv * r) * g_ref[...]).astype(BF16)

    return pl.pallas_call(
        body, name=name, grid=(t // tm,),
        in_specs=[pl.BlockSpec((tm, d), lambda i: (i, 0)), pl.BlockSpec((1, d), lambda i: (0, 0))],
        out_specs=pl.BlockSpec((tm, d), lambda i: (i, 0)),
        out_shape=jax.ShapeDtypeStruct((t, d), BF16),
        compiler_params=_params("parallel"),
    )(x, g)


def _rmsnorm_bwd(x, g, dh, dres, name, with_bf16=False):
    t, d = x.shape
    tm = _pick(t, (512, 256, 128))

    def body(x_ref, g_ref, dh_ref, dres_ref, dx_ref, dg_ref, *dxb_ref):
        @pl.when(pl.program_id(0) == 0)
        def _():
            dg_ref[...] = jnp.zeros_like(dg_ref)

        xv = x_ref[...]
        r = lax.rsqrt(jnp.mean(xv * xv, axis=-1, keepdims=True) + EPS)
        xhat = xv * r
        dhv = dh_ref[...]
        dyg = dhv * g_ref[...]
        dx = dres_ref[...] + r * (dyg - xhat * jnp.mean(dyg * xhat, axis=-1, keepdims=True))
        dx_ref[...] = dx
        if with_bf16:
            dxb_ref[0][...] = dx.astype(BF16)
        dg_ref[...] += jnp.sum(dhv * xhat, axis=0, keepdims=True)

    row = pl.BlockSpec((tm, d), lambda i: (i, 0))
    vec = pl.BlockSpec((1, d), lambda i: (0, 0))
    extra = with_bf16 * [jax.ShapeDtypeStruct((t, d), BF16)]
    return pl.pallas_call(
        body, name=name, grid=(t // tm,),
        in_specs=[row, vec, row, row], out_specs=[row, vec] + with_bf16 * [row],
        out_shape=[jax.ShapeDtypeStruct((t, d), F32), jax.ShapeDtypeStruct((1, d), F32)] + extra,
        compiler_params=_params("arbitrary"),
    )(x, g, dh, dres)


def _loss_head(x, g, target, name):
    t, d = x.shape
    tm = _pick(t, (512, 256, 128))

    def body(x_ref, g_ref, t_ref, loss_ref, dx_ref, dg_ref, dxb_ref):
        @pl.when(pl.program_id(0) == 0)
        def _():
            dg_ref[...] = jnp.zeros_like(dg_ref)
            loss_ref[...] = jnp.zeros_like(loss_ref)

        xv = x_ref[...]
        gv = g_ref[...]
        r = lax.rsqrt(jnp.mean(xv * xv, axis=-1, keepdims=True) + EPS)
        xhat = xv * r
        err = xhat * gv - t_ref[...]
        loss_ref[...] += jnp.sum(err * err) * (0.5 / d)
        dy = err * (1.0 / d)
        dyg = dy * gv
        dx = r * (dyg - xhat * jnp.mean(dyg * xhat, axis=-1, keepdims=True))
        dx_ref[...] = dx
        dxb_ref[...] = dx.astype(BF16)
        dg_ref[...] += jnp.sum(dy * xhat, axis=0, keepdims=True)

    row = pl.BlockSpec((tm, d), lambda i: (i, 0))
    vec = pl.BlockSpec((1, d), lambda i: (0, 0))
    return pl.pallas_call(
        body, name=name, grid=(t // tm,),
        in_specs=[row, vec, row],
        out_specs=[pl.BlockSpec((1, LANES), lambda i: (0, 0)), row, vec, row],
        out_shape=[jax.ShapeDtypeStruct((1, LANES), F32), jax.ShapeDtypeStruct((t, d), F32),
                   jax.ShapeDtypeStruct((1, d), F32), jax.ShapeDtypeStruct((t, d), BF16)],
        compiler_params=_params("arbitrary"),
    )(x, g, target)


CONV_HALO = 8
CONV_ROWS = 64


def _conv_taps(window, wv, bv):
    acc = bv + wv[SSM_CONV - 1:SSM_CONV, :] * window(0)
    for sh in range(1, SSM_CONV):
        kidx = SSM_CONV - 1 - sh
        acc = acc + wv[kidx:kidx + 1, :] * window(sh)
    return acc


def _conv_fwd(u, w, bias, name):
    b, s, c = u.shape
    rows = CONV_ROWS

    def body(u_ref, w_ref, b_ref, o_ref, ext):
        ext[0:CONV_HALO, :] = jnp.zeros((CONV_HALO, LANES), F32)
        ext[CONV_HALO:, :] = u_ref[...]
        wv, bv = w_ref[...], b_ref[...]
        for r0 in range(0, s, rows):
            acc = _conv_taps(lambda sh: ext[CONV_HALO + r0 - sh:CONV_HALO + r0 - sh + rows, :], wv, bv)
            o_ref[r0:r0 + rows, :] = acc * _sigmoid(acc)

    strip = pl.BlockSpec((None, s, LANES), lambda bi, j: (bi, 0, j))
    return pl.pallas_call(
        body, name=name, grid=(b, c // LANES),
        in_specs=[strip, pl.BlockSpec((SSM_CONV, LANES), lambda bi, j: (0, j)),
                  pl.BlockSpec((1, LANES), lambda bi, j: (0, j))],
        out_specs=strip, out_shape=jax.ShapeDtypeStruct((b, s, c), F32),
        scratch_shapes=[pltpu.VMEM((CONV_HALO + s, LANES), F32)],
        compiler_params=_params("parallel", "parallel"),
    )(u, w, bias)


def _conv_bwd(u, dout, w, bias, dproj, name):
    b, s, c = u.shape
    rows = CONV_ROWS

    def fold(v):
        return jnp.sum(v.reshape(rows // CONV_HALO, CONV_HALO, LANES), axis=0)

    def body(u_ref, d_ref, w_ref, b_ref, buf_ref, du_ref, dw_ref, db_ref, ext, dpre):
        @pl.when(pl.program_id(1) == 0)
        def _():
            dw_ref[...] = jnp.zeros_like(dw_ref)
            db_ref[...] = jnp.zeros_like(db_ref)

        ext[0:CONV_HALO, :] = jnp.zeros((CONV_HALO, LANES), F32)
        ext[CONV_HALO:, :] = u_ref[...]
        dpre[s:, :] = jnp.zeros((CONV_HALO, LANES), F32)
        wv, bv = w_ref[...], b_ref[...]
        sums = [jnp.zeros((CONV_HALO, LANES), F32)] * (SSM_CONV + 1)
        for r0 in range(0, s, rows):
            window = lambda sh: ext[CONV_HALO + r0 - sh:CONV_HALO + r0 - sh + rows, :]
            acc = _conv_taps(window, wv, bv)
            sg = _sigmoid(acc)
            dp = d_ref[r0:r0 + rows, :] * (sg * (1.0 + acc * (1.0 - sg)))
            dpre[r0:r0 + rows, :] = dp
            taps = [sums[SSM_CONV - 1 - sh] + fold(dp * window(sh)) for sh in range(SSM_CONV)]
            sums = taps[::-1] + [sums[SSM_CONV] + fold(dp)]
        for r0 in range(0, s, rows):
            du = wv[SSM_CONV - 1:SSM_CONV, :] * dpre[r0:r0 + rows, :]
            for sh in range(1, SSM_CONV):
                kidx = SSM_CONV - 1 - sh
                du = du + wv[kidx:kidx + 1, :] * dpre[r0 + sh:r0 + sh + rows, :]
            du_ref[r0:r0 + rows, :] = du.astype(BF16)
        for kidx in range(SSM_CONV):
            dw_ref[kidx:kidx + 1, :] += jnp.sum(sums[kidx], axis=0, keepdims=True)
        db_ref[...] += jnp.sum(sums[SSM_CONV], axis=0, keepdims=True)

    strip = pl.BlockSpec((None, s, LANES), lambda j, bi: (bi, 0, j))
    taps = pl.BlockSpec((SSM_CONV, LANES), lambda j, bi: (0, j))
    vec = pl.BlockSpec((1, LANES), lambda j, bi: (0, j))
    du_cols = pl.BlockSpec((None, s, LANES), lambda j, bi: (bi, 0, DPROJ_COLS["xbc"] // LANES + j))
    return pl.pallas_call(
        body, name=name, grid=(c // LANES, b),
        in_specs=[strip, strip, taps, vec, pl.BlockSpec(memory_space=pl.ANY)], out_specs=[du_cols, taps, vec],
        input_output_aliases={4: 0},
        out_shape=[jax.ShapeDtypeStruct(dproj.shape, dproj.dtype), jax.ShapeDtypeStruct((SSM_CONV, c), F32),
                   jax.ShapeDtypeStruct((1, c), F32)],
        scratch_shapes=[pltpu.VMEM((CONV_HALO + s, LANES), F32), pltpu.VMEM((s + CONV_HALO, LANES), F32)],
        compiler_params=_params("parallel", "arbitrary"),
    )(u, dout, w, bias, dproj)


def _ssd_chunk_terms(dtr_ref, bias_ref, alog_ref):
    q = SSM_CHUNK
    dt = _softplus(dtr_ref[...] + bias_ref[...])
    a_neg = -jnp.exp(alog_ref[...])
    row = lax.broadcasted_iota(jnp.int32, (q, q), 0)
    col = lax.broadcasted_iota(jnp.int32, (q, q), 1)
    lower = row >= col
    s = _mask_nn(lower, dt * a_neg)
    return dt, a_neg, s, s.T, lower


def _head_masks():
    heads = jnp.arange(LANES)[:, None]
    chans = jnp.arange(SSM_D_INNER)[None, :]
    to_channels = (chans // SSM_HEAD_DIM == heads).astype(BF16)
    return to_channels, to_channels.T


def _per_channel(v, to_channels):
    hi = v.astype(BF16)
    lo = (v - hi.astype(F32)).astype(BF16)
    return _nn(hi, to_channels) + _nn(lo, to_channels)


def _per_head(v, to_heads):
    hi = v.astype(BF16)
    lo = (v - hi.astype(F32)).astype(BF16)
    return _nn(hi, to_heads) + _nn(lo, to_heads)


def _decay_terms_per_channel(dt, s_col, to_channels):
    q = SSM_CHUNK
    tot = s_col[q - 1:q, :]
    stacked = jnp.concatenate([dt, jnp.exp(s_col), jnp.exp(tot - s_col)], axis=0)
    wide = _per_channel(stacked, to_channels)
    dtx, esx, decx = wide[:q], wide[q:2 * q], wide[2 * q:]
    return dtx, esx, decx, esx[0:1, :] * decx[0:1, :]


SSM_PAIRS_PER_GROUP = SSM_HEADS_PER_GROUP // 2
SSM_GROUP_CHANNELS = SSM_HEADS_PER_GROUP * SSM_HEAD_DIM


def _split_pair(v):
    first = lax.broadcasted_iota(jnp.int32, v.shape, 1) < SSM_HEAD_DIM
    return jnp.concatenate([jnp.where(first, v, 0.0), jnp.where(first, 0.0, v)], axis=0)


def _ssd_fwd(xc, dtr, dt_bias, a_log, dskx, to_channels, name):
    b, s, _ = xc.shape
    q = SSM_CHUNK
    nc = s // q
    n, gc = SSM_D_STATE, SSM_GROUP_CHANNELS

    def body(xc_ref, dtr_ref, bias_ref, alog_ref, dsk_ref, tc_ref, y_ref, hs_ref, h_scr):
        @pl.when(pl.program_id(1) == 0)
        def _():
            h_scr[...] = jnp.zeros_like(h_scr)

        dt, _, s_col, s_row, lower = _ssd_chunk_terms(dtr_ref, bias_ref, alog_ref)
        dtx, esx, decx, etotx = _decay_terms_per_channel(dt, s_col, tc_ref[...])
        x = xc_ref[:, :SSM_D_INNER]
        xdt = x * dtx
        xdec = xdt * decx
        skip = dsk_ref[...] * x
        for g in range(SSM_N_GROUPS):
            bg = xc_ref[:, SSM_D_INNER + n * g:SSM_D_INNER + n * (g + 1)].astype(BF16)
            cg = xc_ref[:, SSM_D_INNER + n * (SSM_N_GROUPS + g):SSM_D_INNER + n * (SSM_N_GROUPS + g + 1)].astype(BF16)
            gsl = slice(gc * g, gc * (g + 1))
            gm = _nt(cg, bg)
            hgt = h_scr[:, gsl]
            hs_ref[:, gsl] = hgt
            y_off = esx[:, gsl] * _nn(cg, hgt)
            h_scr[:, gsl] = etotx[:, gsl] * hgt + _tn(bg, xdec[:, gsl])
            for k in range(SSM_PAIRS_PER_GROUP):
                h0 = g * SSM_HEADS_PER_GROUP + 2 * k
                lo = gc * g + LANES * k
                ms = []
                for h in (h0, h0 + 1):
                    lm = jnp.exp(jnp.where(lower, s_col[:, h:h + 1] - s_row[h:h + 1, :], NEG_INF))
                    ms.append((gm * lm).astype(BF16))
                y_diag = _nn(jnp.concatenate(ms, axis=1), _split_pair(xdt[:, lo:lo + LANES]))
                y_ref[:, lo:lo + LANES] = y_diag + y_off[:, LANES * k:LANES * (k + 1)] + skip[:, lo:lo + LANES]

    vec = pl.BlockSpec((1, LANES), lambda bi, c: (0, 0))
    return pl.pallas_call(
        body, name=name, grid=(b, nc),
        in_specs=[pl.BlockSpec((None, q, SSM_CONV_DIM), lambda bi, c: (bi, c, 0)),
                  pl.BlockSpec((None, q, LANES), lambda bi, c: (bi, c, 0)), vec, vec,
                  pl.BlockSpec((1, SSM_D_INNER), lambda bi, c: (0, 0)),
                  pl.BlockSpec((LANES, SSM_D_INNER), lambda bi, c: (0, 0))],
        out_specs=[pl.BlockSpec((None, q, SSM_D_INNER), lambda bi, c: (bi, c, 0)),
                   pl.BlockSpec((None, None, n, SSM_D_INNER), lambda bi, c: (bi, c, 0, 0))],
        out_shape=[jax.ShapeDtypeStruct((b, s, SSM_D_INNER), F32),
                   jax.ShapeDtypeStruct((b, nc, n, SSM_D_INNER), F32)],
        scratch_shapes=[pltpu.VMEM((n, SSM_D_INNER), F32)],
        compiler_params=_params("parallel", "arbitrary"),
    )(xc, dtr, dt_bias, a_log, dskx, to_channels)


def _ssd_bwd(xc, dtr, dy, hs, dt_bias, a_log, dskx, to_channels, to_heads, dproj, name):
    b, s, _ = xc.shape
    q = SSM_CHUNK
    nc = s // q
    n, gc = SSM_D_STATE, SSM_GROUP_CHANNELS

    def colsum(v):
        return jnp.sum(v, axis=0, keepdims=True)

    def body(xc_ref, dtr_ref, dy_ref, hs_ref, bias_ref, alog_ref, dsk_ref, tc_ref, th_ref, buf_ref,
             dxc_ref, ddtr_ref, dalog_ref, ddsk_ref, dbias_ref, dh_scr, dxs_scr, dxd_scr, w_scr, dst_scr, rows_scr):
        ci = pl.program_id(1)

        @pl.when(ci == 0)
        def _():
            dh_scr[...] = jnp.zeros_like(dh_scr)

        @pl.when(jnp.logical_and(pl.program_id(0) == 0, ci == 0))
        def _():
            dalog_ref[...] = jnp.zeros_like(dalog_ref)
            ddsk_ref[...] = jnp.zeros_like(ddsk_ref)
            dbias_ref[...] = jnp.zeros_like(dbias_ref)
            dst_scr[...] = jnp.zeros_like(dst_scr)

        dt, a_neg, s_col, s_row, lower = _ssd_chunk_terms(dtr_ref, bias_ref, alog_ref)
        upper = jnp.logical_not(lower) | (lax.broadcasted_iota(jnp.int32, (q, q), 0)
                                          == lax.broadcasted_iota(jnp.int32, (q, q), 1))
        dtx, esx, decx, etotx = _decay_terms_per_channel(dt, s_col, tc_ref[...])
        x = xc_ref[:, :SSM_D_INNER]
        dyv = dy_ref[...]
        xdt = x * dtx
        xdec = xdt * decx
        dw = esx * dyv
        rows_scr[...] = jnp.zeros_like(rows_scr)
        for g in range(SSM_N_GROUPS):
            b_lo = SSM_D_INNER + n * g
            c_lo = SSM_D_INNER + n * (SSM_N_GROUPS + g)
            bg = xc_ref[:, b_lo:b_lo + n].astype(BF16)
            cg = xc_ref[:, c_lo:c_lo + n].astype(BF16)
            gsl = slice(gc * g, gc * (g + 1))
            gm = _nt(cg, bg)
            gmt = _nt(bg, cg)
            hgt = hs_ref[:, gsl]
            dhgt = dh_scr[:, gsl]
            w_scr[:, gsl] = _nn(cg, hgt)
            dcg = _nt(dw[:, gsl], hgt)
            dxs = decx[:, gsl] * _nn(bg, dhgt)
            dxs_scr[:, gsl] = dxs
            dbg = _nt(xdec[:, gsl], dhgt)
            rows_scr[2:3, gsl] = colsum(dhgt * hgt)
            dh_scr[:, gsl] = _tn(cg, dw[:, gsl]) + etotx[:, gsl] * dhgt
            dg = jnp.zeros((q, q), F32)
            dgt = jnp.zeros((q, q), F32)
            for k in range(SSM_PAIRS_PER_GROUP):
                h0 = g * SSM_HEADS_PER_GROUP + 2 * k
                lo = gc * g + LANES * k
                xp = xdt[:, lo:lo + LANES]
                dyp = dyv[:, lo:lo + LANES]
                dy2 = _split_pair(dyp)
                dm2 = _nt(dy2, xp)
                dmt2 = _nt(_split_pair(xp), dyp)
                mts = []
                for i, h in enumerate((h0, h0 + 1)):
                    lm = jnp.exp(jnp.where(lower, s_col[:, h:h + 1] - s_row[h:h + 1, :], NEG_INF))
                    lmt = jnp.exp(jnp.where(upper, s_row[h:h + 1, :] - s_col[:, h:h + 1], NEG_INF))
                    dm = dm2[q * i:q * (i + 1), :]
                    dmt = dmt2[q * i:q * (i + 1), :]
                    dg = dg + dm * lm
                    dgt = dgt + dmt * lmt
                    mt = gmt * lmt
                    dst_scr[h:h + 1, :] = colsum(dmt * mt) - colsum(dm * (gm * lm))
                    mts.append(mt.astype(BF16))
                dxd_scr[:, lo:lo + LANES] = _nn(jnp.concatenate(mts, axis=1), dy2)
            dxc_ref[:, b_lo:b_lo + n] = dbg + _nn(dgt, cg)
            dxc_ref[:, c_lo:c_lo + n] = dcg + _nn(dg, bg)
        dxs = dxs_scr[...]
        dxdt = dxd_scr[...] + dxs
        dxc_ref[:, :SSM_D_INNER] = dxdt * dtx + dsk_ref[...] * dyv
        state_part = xdt * dxs
        rows_scr[0:1, :] = colsum(dyv * x)
        rows_scr[1:2, :] = colsum(state_part)
        th = th_ref[...]
        per_head = _per_head(jnp.concatenate([dw * w_scr[...] - state_part, dxdt * x], axis=0), th)
        r_ds, r_dt = per_head[:q], per_head[q:]
        sums = _per_head(rows_scr[...], th)
        etot = jnp.exp(s_col[q - 1:q, :])
        dtot = sums[1:2, :] + etot * sums[2:3, :]
        last = lax.broadcasted_iota(jnp.int32, (q, LANES), 0) == q - 1
        ds = dst_scr[...].T + r_ds + jnp.where(last, dtot, 0.0)
        da = _mask_nn(upper, ds)
        ddt = da * a_neg + r_dt
        live = lax.broadcasted_iota(jnp.int32, (1, LANES), 1) < SSM_N_HEADS
        sg = _sigmoid(dtr_ref[...] + bias_ref[...])
        ddtr = jnp.where(live, ddt * sg, 0.0)
        ddtr_ref[:, :LANES] = ddtr.astype(BF16)
        ddtr_ref[:, LANES:] = jnp.zeros((q, DPROJ_DT_WIDTH - LANES), BF16)
        dalog_ref[...] += jnp.where(live, colsum(da * dt) * a_neg, 0.0)
        ddsk_ref[...] += jnp.where(live, sums[0:1, :], 0.0)
        dbias_ref[...] += colsum(ddtr)

    rev = lambda bi, c: (bi, nc - 1 - c, 0)
    vec = pl.BlockSpec((1, LANES), lambda bi, c: (0, 0))
    wide = pl.BlockSpec((None, q, SSM_D_INNER), rev)
    return pl.pallas_call(
        body, name=name, grid=(b, nc),
        in_specs=[pl.BlockSpec((None, q, SSM_CONV_DIM), rev), pl.BlockSpec((None, q, LANES), rev), wide,
                  pl.BlockSpec((None, None, n, SSM_D_INNER), lambda bi, c: (bi, nc - 1 - c, 0, 0)),
                  vec, vec, pl.BlockSpec((1, SSM_D_INNER), lambda bi, c: (0, 0)),
                  pl.BlockSpec((LANES, SSM_D_INNER), lambda bi, c: (0, 0)),
                  pl.BlockSpec((SSM_D_INNER, LANES), lambda bi, c: (0, 0)),
                  pl.BlockSpec(memory_space=pl.ANY)],
        out_specs=[pl.BlockSpec((None, q, SSM_CONV_DIM), rev),
                   pl.BlockSpec((None, q, DPROJ_DT_WIDTH),
                                lambda bi, c: (bi, nc - 1 - c, DPROJ_COLS["dt"] // DPROJ_DT_WIDTH)), vec, vec, vec],
        input_output_aliases={9: 1},
        out_shape=[jax.ShapeDtypeStruct((b, s, SSM_CONV_DIM), F32), jax.ShapeDtypeStruct(dproj.shape, dproj.dtype),
                   jax.ShapeDtypeStruct((1, LANES), F32), jax.ShapeDtypeStruct((1, LANES), F32),
                   jax.ShapeDtypeStruct((1, LANES), F32)],
        scratch_shapes=[pltpu.VMEM((n, SSM_D_INNER), F32)] + [pltpu.VMEM((q, SSM_D_INNER), F32)] * 3
        + [pltpu.VMEM((LANES, q), F32), pltpu.VMEM((8, SSM_D_INNER), F32)],
        compiler_params=_params("arbitrary", "arbitrary"),
    )(xc, dtr, dy, hs, dt_bias, a_log, dskx, to_channels, to_heads, dproj)


SSM_GROUP_WIDTH = SSM_D_INNER // SSM_N_GROUPS


def _gate_norm_fwd(y, z, w, name):
    t, d = y.shape
    tm = _pick(t, (256, 128))

    def body(y_ref, z_ref, w_ref, o_ref):
        for g in range(SSM_N_GROUPS):
            sl = slice(SSM_GROUP_WIDTH * g, SSM_GROUP_WIDTH * (g + 1))
            zv = z_ref[:, sl]
            u = y_ref[:, sl] * (zv * _sigmoid(zv))
            r = lax.rsqrt(jnp.mean(u * u, axis=-1, keepdims=True) + EPS)
            o_ref[:, sl] = ((u * r) * w_ref[:, sl]).astype(BF16)

    row = pl.BlockSpec((tm, d), lambda i: (i, 0))
    return pl.pallas_call(
        body, name=name, grid=(t // tm,),
        in_specs=[row, row, pl.BlockSpec((1, d), lambda i: (0, 0))], out_specs=row,
        out_shape=jax.ShapeDtypeStruct((t, d), BF16),
        compiler_params=_params("parallel"),
    )(y, z, w)


def _gate_norm_bwd(y, z, w, dout, dproj, name):
    t, d = y.shape
    gw = SSM_GROUP_WIDTH
    tm = _pick(t, (1024, 512, 256, 128))

    def body(y_ref, z_ref, w_ref, do_ref, buf_ref, dy_ref, dz_ref, dw_ref):
        @pl.when(pl.program_id(1) == 0)
        def _():
            dw_ref[...] = jnp.zeros_like(dw_ref)

        zv = z_ref[...]
        yv = y_ref[...]
        sg = _sigmoid(zv)
        silu = zv * sg
        u = yv * silu
        r = lax.rsqrt(jnp.mean(u * u, axis=-1, keepdims=True) + EPS)
        uh = u * r
        dov = do_ref[...]
        dw_ref[...] += jnp.sum(dov * uh, axis=0, keepdims=True)
        dyg = dov * w_ref[...]
        du = r * (dyg - uh * jnp.mean(dyg * uh, axis=-1, keepdims=True))
        dy_ref[...] = du * silu
        dz_ref[...] = (du * yv * (sg * (1.0 + zv * (1.0 - sg)))).astype(BF16)

    tile = pl.BlockSpec((tm, gw), lambda g, i: (i, g))
    vec = pl.BlockSpec((1, gw), lambda g, i: (0, g))
    z_cols = pl.BlockSpec((tm, gw), lambda g, i: (i, DPROJ_COLS["z"] // gw + g))
    return pl.pallas_call(
        body, name=name, grid=(SSM_N_GROUPS, t // tm),
        in_specs=[tile, tile, vec, tile, pl.BlockSpec(memory_space=pl.ANY)], out_specs=[tile, z_cols, vec],
        out_shape=[jax.ShapeDtypeStruct((t, d), F32), jax.ShapeDtypeStruct(dproj.shape, dproj.dtype),
                   jax.ShapeDtypeStruct((1, d), F32)],
        input_output_aliases={4: 1},
        compiler_params=_params("parallel", "arbitrary"),
    )(y, z, w, dout, dproj)


def _rope_tables(s):
    half = ATT_HEAD_DIM // 2
    inv = ROPE_THETA ** (-jnp.arange(half, dtype=F32) / half)
    ang = jnp.arange(s).astype(F32)[:, None] * inv[None, :]
    cos, sin = jnp.cos(ang), jnp.sin(ang)
    return jnp.concatenate([cos, cos], axis=-1), jnp.concatenate([-sin, sin], axis=-1)


ATT_TILE = 256


def _by_residue_spec(r, width):
    return pl.BlockSpec((None, r, ATT_TILE // r, width), lambda bi, i: (bi, 0, i, 0))


def _to_residues(tile, stage, r, store):
    if r == 1:
        store(0, tile)
        return
    stage[...] = tile
    for ri in range(r):
        store(ri, stage[pl.ds(ri, ATT_TILE // r, stride=r), :])


def _from_residues(load, stage, r):
    if r == 1:
        return load(0)
    for ri in range(r):
        stage[pl.ds(ri, ATT_TILE // r, stride=r), :] = load(ri)
    return stage[...]


def _rope_fwd(qkv, cosf, sinf, name):
    b, s, w = qkv.shape
    ts, d, gw = ATT_TILE, ATT_HEAD_DIM, ATT_OUT_DIM

    def body(x_ref, c_ref, s_ref, *rest):
        outs, stage = rest[:-1], rest[-1]
        cv, sv = c_ref[...], s_ref[...]
        for kind in range(3):
            for gi, r in enumerate(ATT_DILATIONS):
                for j in range(ATT_HEADS_PER_GROUP):
                    src = d * (kind * ATT_N_HEADS + gi * ATT_HEADS_PER_GROUP + j)
                    dst = slice(kind * gw + d * j, kind * gw + d * (j + 1))
                    tv = x_ref[:, src:src + d]
                    if kind < 2:
                        tv = tv * cv + pltpu.roll(tv, d // 2, 1) * sv

                    def store(ri, rows, o_ref=outs[gi], dst=dst):
                        o_ref[ri, :, dst] = rows.astype(BF16)

                    _to_residues(tv, stage, r, store)

    tab = pl.BlockSpec((ts, d), lambda bi, i: (i, 0))
    return pl.pallas_call(
        body, name=name, grid=(b, s // ts),
        in_specs=[pl.BlockSpec((None, ts, w), lambda bi, i: (bi, i, 0)), tab, tab],
        out_specs=[_by_residue_spec(r, 3 * gw) for r in ATT_DILATIONS],
        out_shape=[jax.ShapeDtypeStruct((b, r, s // r, 3 * gw), BF16) for r in ATT_DILATIONS],
        scratch_shapes=[pltpu.VMEM((ts, d), F32)],
        compiler_params=_params("parallel", "parallel"),
    )(qkv, cosf, sinf)


def _rope_bwd(dq, dk, dv, cosf, sinf, dproj, name):
    n_pat = len(ATT_DILATIONS)
    b, _, s, gw = dq[0].shape
    ts, d = ATT_TILE, ATT_HEAD_DIM

    def body(*refs):
        ins, (c_ref, s_ref, _, o_ref, stage) = refs[:3 * n_pat], refs[3 * n_pat:]
        cv, sv = c_ref[...], s_ref[...]
        for kind in range(3):
            for gi, r in enumerate(ATT_DILATIONS):
                src = ins[kind * n_pat + gi]
                for j in range(ATT_HEADS_PER_GROUP):
                    tv = _from_residues(lambda ri, src=src, j=j: src[ri, :, d * j:d * (j + 1)], stage, r)
                    if kind < 2:
                        tv = tv * cv + pltpu.roll(tv * sv, d // 2, 1)
                    lo = d * (kind * ATT_N_HEADS + gi * ATT_HEADS_PER_GROUP + j)
                    o_ref[:, lo:lo + d] = tv.astype(BF16)

    tab = pl.BlockSpec((ts, d), lambda bi, i: (i, 0))
    parts = [_by_residue_spec(r, gw) for r in ATT_DILATIONS]
    return pl.pallas_call(
        body, name=name, grid=(b, s // ts), in_specs=parts * 3 + [tab, tab, pl.BlockSpec(memory_space=pl.ANY)],
        out_specs=pl.BlockSpec((None, ts, ATT_QKV_DIM), lambda bi, i: (bi, i, DPROJ_COLS["qkv"] // ATT_QKV_DIM)),
        out_shape=jax.ShapeDtypeStruct(dproj.shape, dproj.dtype),
        input_output_aliases={3 * n_pat + 2: 0},
        scratch_shapes=[pltpu.VMEM((ts, d), F32)],
        compiler_params=_params("parallel", "parallel"),
    )(*dq, *dk, *dv, cosf, sinf, dproj)


ATT_SCALE = ATT_HEAD_DIM ** -0.5
ATT_STEP = 2 * ATT_BLOCK


def _att_spec(col):
    return pl.BlockSpec((None, None, ATT_STEP, ATT_OUT_DIM), lambda bi, ri, i: (bi, ri, i, col))


def _att_edge_spec(col, side, n_steps):
    def index(bi, ri, i):
        blk = 2 * i - 1 if side < 0 else 2 * i + 2
        return (bi, ri, jnp.clip(blk, 0, 2 * n_steps - 1), col)
    return pl.BlockSpec((None, None, ATT_BLOCK, ATT_OUT_DIM), index)


def _band_mask(shape, q_axis, has_prev):
    qi = lax.broadcasted_iota(jnp.int32, shape, q_axis)
    kj = lax.broadcasted_iota(jnp.int32, shape, 1 - q_axis)
    dist = qi + ATT_BLOCK - kj
    return (dist >= 0) & (dist <= ATT_BLOCK) & (has_prev | (kj >= ATT_BLOCK))


def _att_fwd(qkr, name):
    b, r, l, _ = qkr.shape
    nb = l // ATT_STEP
    d = ATT_HEAD_DIM

    def body(q_ref, kp_ref, k_ref, vp_ref, v_ref, o_ref, lse_ref):
        mask = _band_mask((ATT_STEP, ATT_BLOCK + ATT_STEP), 0, pl.program_id(2) > 0)
        for j in range(ATT_HEADS_PER_GROUP):
            sl = slice(d * j, d * (j + 1))
            kcat = jnp.concatenate([kp_ref[:, sl], k_ref[:, sl]], axis=0)
            vcat = jnp.concatenate([vp_ref[:, sl], v_ref[:, sl]], axis=0)
            sc = jnp.where(mask, _nt(q_ref[:, sl], kcat) * ATT_SCALE, NEG_INF)
            m = jnp.max(sc, axis=-1, keepdims=True)
            pr = jnp.exp(sc - m)
            den = jnp.sum(pr, axis=-1, keepdims=True)
            o_ref[:, sl] = _nn(pr / den, vcat)
            lse_ref[:, sl] = jnp.broadcast_to(m + jnp.log(den), (ATT_STEP, d))

    out_spec = _att_spec(0)
    return pl.pallas_call(
        body, name=name, grid=(b, r, nb),
        in_specs=[_att_spec(0), _att_edge_spec(1, -1, nb), _att_spec(1), _att_edge_spec(2, -1, nb), _att_spec(2)],
        out_specs=[out_spec, out_spec],
        out_shape=[jax.ShapeDtypeStruct((b, r, l, ATT_OUT_DIM), F32)] * 2,
        compiler_params=_params("parallel", "parallel", "parallel"),
    )(qkr, qkr, qkr, qkr, qkr)


def _att_merge(os_, lses, name):
    n_pat = len(os_)
    b, _, s, gw = os_[0].shape
    ts, d = ATT_TILE, ATT_HEAD_DIM

    def body(*refs):
        o_refs, l_refs = refs[:n_pat], refs[n_pat:2 * n_pat]
        att_ref, lse_outs, stage = refs[2 * n_pat], refs[2 * n_pat + 1:3 * n_pat + 1], refs[-1]
        for j in range(ATT_HEADS_PER_GROUP):
            sl = slice(d * j, d * (j + 1))
            ov = [_from_residues(lambda ri, g=g: o_refs[g][ri, :, sl], stage, r)
                  for g, r in enumerate(ATT_DILATIONS)]
            ls = [_from_residues(lambda ri, g=g: l_refs[g][ri, :, sl], stage, r)
                  for g, r in enumerate(ATT_DILATIONS)]
            m = functools.reduce(jnp.maximum, ls)
            es = [jnp.exp(lv - m) for lv in ls]
            tot = functools.reduce(lambda u, v: u + v, es)
            acc = (es[0] / tot) * ov[0]
            for g in range(1, n_pat):
                acc = acc + (es[g] / tot) * ov[g]
            att_ref[:, sl] = acc
            joint = m + jnp.log(tot)
            for g, r in enumerate(ATT_DILATIONS):
                def store(ri, rows, out=lse_outs[g]):
                    out[ri, :, sl] = rows
                _to_residues(joint, stage, r, store)

    parts = [_by_residue_spec(r, gw) for r in ATT_DILATIONS]
    return pl.pallas_call(
        body, name=name, grid=(b, s // ts), in_specs=parts * 2,
        out_specs=[pl.BlockSpec((None, ts, gw), lambda bi, i: (bi, i, 0))] + parts,
        out_shape=[jax.ShapeDtypeStruct((b, s, gw), F32)]
        + [jax.ShapeDtypeStruct((b, r, s // r, gw), F32) for r in ATT_DILATIONS],
        scratch_shapes=[pltpu.VMEM((ts, d), F32)],
        compiler_params=_params("parallel", "parallel"),
    )(*os_, *lses)


def _att_delta(att, datt, name):
    b, s, gw = att.shape
    ts, d = ATT_TILE, ATT_HEAD_DIM
    n_pat = len(ATT_DILATIONS)

    def body(a_ref, d_ref, *rest):
        do_outs, dl_outs, stage = rest[:n_pat], rest[n_pat:2 * n_pat], rest[-1]
        for j in range(ATT_HEADS_PER_GROUP):
            sl = slice(d * j, d * (j + 1))
            dv = d_ref[:, sl]
            delta = jnp.broadcast_to(jnp.sum(a_ref[:, sl] * dv, axis=-1, keepdims=True), (ts, d))
            for g, r in enumerate(ATT_DILATIONS):
                def store_do(ri, rows, out=do_outs[g]):
                    out[ri, :, sl] = rows.astype(BF16)

                def store_dl(ri, rows, out=dl_outs[g]):
                    out[ri, :, sl] = rows

                _to_residues(dv, stage, r, store_do)
                _to_residues(delta, stage, r, store_dl)

    row = pl.BlockSpec((None, ts, gw), lambda bi, i: (bi, i, 0))
    parts = [_by_residue_spec(r, gw) for r in ATT_DILATIONS]
    outs = pl.pallas_call(
        body, name=name, grid=(b, s // ts), in_specs=[row, row], out_specs=parts * 2,
        out_shape=[jax.ShapeDtypeStruct((b, r, s // r, gw), BF16) for r in ATT_DILATIONS]
        + [jax.ShapeDtypeStruct((b, r, s // r, gw), F32) for r in ATT_DILATIONS],
        scratch_shapes=[pltpu.VMEM((ts, d), F32)],
        compiler_params=_params("parallel", "parallel"),
    )(att, datt)
    return outs[:n_pat], outs[n_pat:]


def _att_bwd_q(qkr, datt, lse, delta, name):
    b, r, l, _ = qkr.shape
    nb = l // ATT_STEP
    d = ATT_HEAD_DIM

    def body(q_ref, kp_ref, k_ref, vp_ref, v_ref, do_ref, lse_ref, dl_ref, dq_ref):
        mask = _band_mask((ATT_STEP, ATT_BLOCK + ATT_STEP), 0, pl.program_id(2) > 0)
        for j in range(ATT_HEADS_PER_GROUP):
            sl = slice(d * j, d * (j + 1))
            kcat = jnp.concatenate([kp_ref[:, sl], k_ref[:, sl]], axis=0)
            vcat = jnp.concatenate([vp_ref[:, sl], v_ref[:, sl]], axis=0)
            sc = _nt(q_ref[:, sl], kcat) * ATT_SCALE
            pr = jnp.exp(jnp.where(mask, sc - lse_ref[:, d * j:d * j + 1], NEG_INF))
            dp = _nt(do_ref[:, sl], vcat)
            dsc = pr * (dp - dl_ref[:, d * j:d * j + 1])
            dq_ref[:, sl] = _nn(dsc, kcat) * ATT_SCALE

    tok = _att_spec(0)
    return pl.pallas_call(
        body, name=name, grid=(b, r, nb),
        in_specs=[_att_spec(0), _att_edge_spec(1, -1, nb), _att_spec(1), _att_edge_spec(2, -1, nb), _att_spec(2),
                  tok, tok, tok],
        out_specs=tok,
        out_shape=jax.ShapeDtypeStruct((b, r, l, ATT_OUT_DIM), F32),
        compiler_params=_params("parallel", "parallel", "parallel"),
    )(qkr, qkr, qkr, qkr, qkr, datt, lse, delta)


def _att_bwd_kv(qkr, datt, lse, delta, name):
    b, r, l, _ = qkr.shape
    nb = l // ATT_STEP
    d = ATT_HEAD_DIM

    def body(k_ref, v_ref, q_ref, qn_ref, do_ref, don_ref, lse_ref, lsen_ref, dl_ref, dln_ref, dk_ref, dv_ref):
        shape = (ATT_STEP, ATT_STEP + ATT_BLOCK)
        kj = lax.broadcasted_iota(jnp.int32, shape, 0)
        qi = lax.broadcasted_iota(jnp.int32, shape, 1)
        dist = qi - kj
        has_next = pl.program_id(2) < nb - 1
        mask = (dist >= 0) & (dist <= ATT_BLOCK) & (has_next | (qi < ATT_STEP))
        for j in range(ATT_HEADS_PER_GROUP):
            sl = slice(d * j, d * (j + 1))
            qcat = jnp.concatenate([q_ref[:, sl], qn_ref[:, sl]], axis=0)
            docat = jnp.concatenate([do_ref[:, sl], don_ref[:, sl]], axis=0)
            lse_t = jnp.tile(jnp.concatenate([lse_ref[:, sl], lsen_ref[:, sl]], axis=0).T, (ATT_STEP // d, 1))
            dl_t = jnp.tile(jnp.concatenate([dl_ref[:, sl], dln_ref[:, sl]], axis=0).T, (ATT_STEP // d, 1))
            sc_t = _nt(k_ref[:, sl], qcat) * ATT_SCALE
            pr_t = jnp.exp(jnp.where(mask, sc_t - lse_t, NEG_INF))
            dv_ref[:, sl] = _nn(pr_t, docat)
            dsc_t = pr_t * (_nt(v_ref[:, sl], docat) - dl_t)
            dk_ref[:, sl] = _nn(dsc_t, qcat) * ATT_SCALE

    tok, tok_n = _att_spec(0), _att_edge_spec(0, 1, nb)
    return pl.pallas_call(
        body, name=name, grid=(b, r, nb),
        in_specs=[_att_spec(1), _att_spec(2), _att_spec(0), _att_edge_spec(0, 1, nb),
                  tok, tok_n, tok, tok_n, tok, tok_n],
        out_specs=[tok, tok],
        out_shape=[jax.ShapeDtypeStruct((b, r, l, ATT_OUT_DIM), F32)] * 2,
        compiler_params=_params("parallel", "parallel", "parallel"),
    )(qkr, qkr, qkr, qkr, datt, datt, lse, lse, delta, delta)


def _mix_fwd(gl, bg, ys, ya, name):
    t, d = ys.shape
    tm = _pick(t, (512, 256, 128))

    def body(gl_ref, bg_ref, ys_ref, ya_ref, o_ref):
        g0 = _sigmoid(gl_ref[:, :d] + bg_ref[:, :d])
        g1 = _sigmoid(gl_ref[:, d:] + bg_ref[:, d:])
        o_ref[...] = (g0 * ys_ref[...] + g1 * ya_ref[...]).astype(BF16)

    row = pl.BlockSpec((tm, d), lambda i: (i, 0))
    return pl.pallas_call(
        body, name=name, grid=(t // tm,),
        in_specs=[pl.BlockSpec((tm, 2 * d), lambda i: (i, 0)), pl.BlockSpec((1, 2 * d), lambda i: (0, 0)), row, row],
        out_specs=row, out_shape=jax.ShapeDtypeStruct((t, d), BF16),
        compiler_params=_params("parallel"),
    )(gl, bg, ys, ya)


def _mix_bwd(gl, bg, ys, ya, dmixed, name):
    t, d = ys.shape
    tm = _pick(t, (512, 256, 128))

    def body(gl_ref, bg_ref, ys_ref, ya_ref, dm_ref, dys_ref, dya_ref, dgl_ref, dbg_ref):
        @pl.when(pl.program_id(0) == 0)
        def _():
            dbg_ref[...] = jnp.zeros_like(dbg_ref)

        dm = dm_ref[...]
        g0 = _sigmoid(gl_ref[:, :d] + bg_ref[:, :d])
        g1 = _sigmoid(gl_ref[:, d:] + bg_ref[:, d:])
        dys_ref[...] = (dm * g0).astype(BF16)
        dya_ref[...] = (dm * g1).astype(BF16)
        d0 = dm * ys_ref[...] * (g0 * (1.0 - g0))
        d1 = dm * ya_ref[...] * (g1 * (1.0 - g1))
        dgl_ref[:, :d] = d0.astype(BF16)
        dgl_ref[:, d:] = d1.astype(BF16)
        dbg_ref[:, :d] += jnp.sum(d0, axis=0, keepdims=True)
        dbg_ref[:, d:] += jnp.sum(d1, axis=0, keepdims=True)

    row = pl.BlockSpec((tm, d), lambda i: (i, 0))
    wide = pl.BlockSpec((tm, 2 * d), lambda i: (i, 0))
    vec = pl.BlockSpec((1, 2 * d), lambda i: (0, 0))
    gate_cols = pl.BlockSpec((tm, 2 * d), lambda i: (i, DPROJ_COLS["gate"] // (2 * d)))
    return pl.pallas_call(
        body, name=name, grid=(t // tm,),
        in_specs=[wide, vec, row, row, row], out_specs=[row, row, gate_cols, vec],
        out_shape=[jax.ShapeDtypeStruct((t, d), BF16), jax.ShapeDtypeStruct((t, d), BF16),
                   jax.ShapeDtypeStruct((t, DPROJ_WIDTH), BF16), jax.ShapeDtypeStruct((1, 2 * d), F32)],
        compiler_params=_params("arbitrary"),
    )(gl, bg, ys, ya, dmixed)


def _swiglu_fwd(gt, up, name):
    t, f = gt.shape
    tm = _pick(t, (512, 256, 128))

    def body(g_ref, u_ref, o_ref):
        gv = g_ref[...]
        o_ref[...] = ((gv * _sigmoid(gv)) * u_ref[...]).astype(BF16)

    row = pl.BlockSpec((tm, f), lambda i: (i, 0))
    return pl.pallas_call(
        body, name=name, grid=(t // tm,), in_specs=[row, row], out_specs=row,
        out_shape=jax.ShapeDtypeStruct((t, f), BF16), compiler_params=_params("parallel"),
    )(gt, up)


def _swiglu_bwd(gt, up, dact, name):
    t, f = gt.shape
    tm = _pick(t, (512, 256, 128))

    def body(g_ref, u_ref, d_ref, dg_ref, du_ref):
        gv = g_ref[...]
        dv = d_ref[...]
        sg = _sigmoid(gv)
        dg_ref[...] = (dv * u_ref[...] * (sg * (1.0 + gv * (1.0 - sg)))).astype(BF16)
        du_ref[...] = (dv * (gv * sg)).astype(BF16)

    row = pl.BlockSpec((tm, f), lambda i: (i, 0))
    return pl.pallas_call(
        body, name=name, grid=(t // tm,), in_specs=[row, row, row], out_specs=[row, row],
        out_shape=[jax.ShapeDtypeStruct((t, f), BF16)] * 2, compiler_params=_params("parallel"),
    )(gt, up, dact)


def _peer(k):
    x, y, c = lax.axis_index("x"), lax.axis_index("y"), lax.axis_index("c")
    px, py, pc = x ^ ((k >> 2) & 1), y ^ ((k >> 1) & 1), c ^ (k & 1)
    return (px, py, pc), 4 * px + 2 * py + pc


def _my_index():
    return 4 * lax.axis_index("x") + 2 * lax.axis_index("y") + lax.axis_index("c")


def _all_gather(parts, name):
    n_parts = len(parts)

    def body(*refs):
        ins, outs = refs[:n_parts], refs[n_parts:2 * n_parts]
        send_sems, recv_sems, local_sems = refs[2 * n_parts:]
        here, me = _peer(0)
        sibling, sib_idx = _peer(1)
        chips = [_peer(2 * q) for q in range(1, N_CHIPS)]

        def copy(i, k, block, to, src=None):
            return pltpu.make_async_remote_copy(
                src_ref=outs[i].at[block] if src is None else src, dst_ref=outs[i].at[block],
                send_sem=send_sems.at[i * (N_DEV - 1) + k], recv_sem=recv_sems.at[i * (N_DEV - 1) + k],
                device_id=to, device_id_type=MESH)

        local = [pltpu.make_async_copy(ins[i], outs[i].at[me], local_sems.at[i]) for i in range(n_parts)]
        for cp in local:
            cp.start()
        sends = []
        for i in range(n_parts):
            sends.append(copy(i, 0, me, sibling, src=ins[i]))
            sends += [copy(i, q, me, chip, src=ins[i]) for q, (chip, _) in enumerate(chips, start=1)]
        for cp in sends:
            cp.start()
        for q, (chip, chip_idx) in enumerate(chips, start=1):
            for i in range(n_parts):
                copy(i, q, chip_idx, here).wait_recv()
                fwd = copy(i, N_CHIPS - 1 + q, chip_idx, sibling)
                fwd.start()
                sends.append(fwd)
        for i in range(n_parts):
            copy(i, 0, sib_idx, here).wait_recv()
        for q, (_, chip_idx) in enumerate(chips, start=1):
            for i in range(n_parts):
                copy(i, N_CHIPS - 1 + q, chip_idx ^ 1, here).wait_recv()
        for cp in sends:
            cp.wait_send()
        for cp in local:
            cp.wait()

    hbm = pl.BlockSpec(memory_space=pl.ANY)
    return pl.pallas_call(
        body, name=name, in_specs=[hbm] * n_parts, out_specs=[hbm] * n_parts,
        out_shape=[jax.ShapeDtypeStruct((N_DEV,) + p_.shape, p_.dtype) for p_ in parts],
        scratch_shapes=[pltpu.SemaphoreType.DMA((n_parts * (N_DEV - 1),)),
                        pltpu.SemaphoreType.DMA((n_parts * (N_DEV - 1),)),
                        pltpu.SemaphoreType.DMA((n_parts,))],
        compiler_params=pltpu.CompilerParams(has_side_effects=True),
    )(*parts)


def _pair_exchange(slabs, name):
    def body(slab_ref, got_ref, send_sems, recv_sems):
        c = lax.axis_index("c")
        sibling, _ = _peer(1)
        copies = [pltpu.make_async_remote_copy(
            src_ref=slab_ref.at[2 * q + 1 - c], dst_ref=got_ref.at[q], send_sem=send_sems.at[q],
            recv_sem=recv_sems.at[q], device_id=sibling, device_id_type=MESH) for q in range(N_CHIPS)]
        for cp in copies:
            cp.start()
        for cp in copies:
            cp.wait()

    hbm = pl.BlockSpec(memory_space=pl.ANY)
    return pl.pallas_call(
        body, name=name, in_specs=[hbm], out_specs=hbm,
        out_shape=jax.ShapeDtypeStruct((N_CHIPS,) + slabs.shape[1:], slabs.dtype),
        scratch_shapes=[pltpu.SemaphoreType.DMA((N_CHIPS,)), pltpu.SemaphoreType.DMA((N_CHIPS,))],
        compiler_params=pltpu.CompilerParams(has_side_effects=True),
    )(slabs)


def _chip_sum(slabs, got, core, name):
    _, rows, lanes = slabs.shape
    tr = _pick(rows, (PACK_ROWS, 512, 256, 128, 64, 32, 16))

    def body(core_ref, mine_ref, got_ref, o_ref):
        o_ref[...] = (mine_ref[...].astype(F32) + got_ref[...].astype(F32)).astype(BF16)

    return pl.pallas_call(
        body, name=name,
        grid_spec=pltpu.PrefetchScalarGridSpec(
            num_scalar_prefetch=1, grid=(N_CHIPS, rows // tr),
            in_specs=[pl.BlockSpec((None, tr, lanes), lambda q, i, core_ref: (2 * q + core_ref[0], i, 0)),
                      pl.BlockSpec((None, tr, lanes), lambda q, i, core_ref: (q, i, 0))],
            out_specs=pl.BlockSpec((None, tr, lanes), lambda q, i, core_ref: (q, i, 0))),
        out_shape=jax.ShapeDtypeStruct((N_CHIPS, rows, lanes), BF16),
        compiler_params=_params("parallel", "parallel"),
    )(core, slabs, got)


def _chip_exchange(chip_sums, shared, name):
    def body(sum_ref, sh_ref, got_ref, gsh_ref, send_sems, recv_sems, sh_send_sems, sh_recv_sems, local_sems):
        me = _my_index()
        my_chip = me >> 1
        local = [pltpu.make_async_copy(sum_ref.at[my_chip], got_ref.at[my_chip], local_sems.at[0]),
                 pltpu.make_async_copy(sh_ref, gsh_ref.at[me], local_sems.at[1])]
        for cp in local:
            cp.start()
        sends = []
        for q in range(1, N_CHIPS):
            peer, pidx = _peer(2 * q)
            cp = pltpu.make_async_remote_copy(
                src_ref=sum_ref.at[pidx >> 1], dst_ref=got_ref.at[my_chip], send_sem=send_sems.at[q - 1],
                recv_sem=recv_sems.at[q - 1], device_id=peer, device_id_type=MESH)
            cp.start()
            sends.append(cp)
        for k in range(1, N_DEV):
            peer, _ = _peer(k)
            cp = pltpu.make_async_remote_copy(
                src_ref=sh_ref, dst_ref=gsh_ref.at[me], send_sem=sh_send_sems.at[k - 1],
                recv_sem=sh_recv_sems.at[k - 1], device_id=peer, device_id_type=MESH)
            cp.start()
            sends.append(cp)
        for q in range(1, N_CHIPS):
            peer, pidx = _peer(2 * q)
            pltpu.make_async_remote_copy(
                src_ref=sum_ref.at[my_chip], dst_ref=got_ref.at[pidx >> 1], send_sem=send_sems.at[q - 1],
                recv_sem=recv_sems.at[q - 1], device_id=peer, device_id_type=MESH).wait_recv()
        for k in range(1, N_DEV):
            peer, pidx = _peer(k)
            pltpu.make_async_remote_copy(
                src_ref=sh_ref, dst_ref=gsh_ref.at[pidx], send_sem=sh_send_sems.at[k - 1],
                recv_sem=sh_recv_sems.at[k - 1], device_id=peer, device_id_type=MESH).wait_recv()
        for cp in sends:
            cp.wait_send()
        for cp in local:
            cp.wait()

    hbm = pl.BlockSpec(memory_space=pl.ANY)
    return pl.pallas_call(
        body, name=name, in_specs=[hbm, hbm], out_specs=[hbm, hbm],
        out_shape=[jax.ShapeDtypeStruct(chip_sums.shape, chip_sums.dtype),
                   jax.ShapeDtypeStruct((N_DEV,) + shared.shape, shared.dtype)],
        scratch_shapes=[pltpu.SemaphoreType.DMA((N_CHIPS - 1,)), pltpu.SemaphoreType.DMA((N_CHIPS - 1,)),
                        pltpu.SemaphoreType.DMA((N_DEV - 1,)), pltpu.SemaphoreType.DMA((N_DEV - 1,)),
                        pltpu.SemaphoreType.DMA((2,))],
        compiler_params=pltpu.CompilerParams(has_side_effects=True),
    )(chip_sums, shared)


def _adamw(parts, w, m, v, name):
    n_parts, rows, lanes = parts.shape
    tr = rows if rows <= PACK_ROWS // 2 else _pick(rows, (PACK_ROWS // 2, 512, 256, 128, 64, 32, 16, 8))
    c1 = 1.0 - ADAM_B1 ** ADAM_STEP
    c2 = 1.0 - ADAM_B2 ** ADAM_STEP

    def body(p_ref, w_ref, m_ref, v_ref, g_ref, d_ref, nm_ref, nv_ref):
        g = p_ref[0].astype(F32)
        for j in range(1, n_parts):
            g = g + p_ref[j].astype(F32)
        nm = ADAM_B1 * m_ref[...] + (1.0 - ADAM_B1) * g
        nv = ADAM_B2 * v_ref[...] + (1.0 - ADAM_B2) * (g * g)
        g_ref[...] = g
        nm_ref[...] = nm
        nv_ref[...] = nv
        d_ref[...] = -ADAM_LR * ((nm / c1) / (jnp.sqrt(nv / c2) + ADAM_EPS) + ADAM_WD * w_ref[...])

    row = pl.BlockSpec((tr, lanes), lambda i: (i, 0))
    return pl.pallas_call(
        body, name=name, grid=(rows // tr,),
        in_specs=[pl.BlockSpec((n_parts, tr, lanes), lambda i: (0, i, 0)), row, row, row],
        out_specs=[row] * 4, out_shape=[jax.ShapeDtypeStruct((rows, lanes), F32)] * 4,
        compiler_params=_params("parallel"),
    )(parts, w, m, v)


MATRIX_SHARDS = (
    ("w_in", (D_MODEL, IN_PROJ_DIM // N_DEV), True),
    ("w_ssm_out", (SSM_D_INNER // N_DEV, D_MODEL), False),
    ("w_att_out", (ATT_OUT_DIM, D_MODEL // N_DEV), True),
    ("w_mix_out", (D_MODEL // N_DEV, D_MODEL), False),
    ("w_ffn_gate", (D_MODEL, D_FF // N_DEV), True),
    ("w_ffn_up", (D_MODEL, D_FF // N_DEV), True),
    ("w_ffn_down", (D_FF // N_DEV, D_MODEL), False),
)
CONV_SHARD = ("conv_w", (SSM_CONV, SSM_CONV_DIM // N_DEV), True)
SHARDED = MATRIX_SHARDS + (CONV_SHARD,)
REPLICATED = (("norm_mix", D_MODEL), ("b_gate", 2 * D_MODEL), ("conv_b", SSM_CONV_DIM), ("dt_bias", SSM_N_HEADS),
              ("a_log", SSM_N_HEADS), ("d_skip", SSM_N_HEADS), ("ssm_norm", SSM_D_INNER), ("norm_ffn", D_MODEL),
              ("norm_final", D_MODEL))


PACK_ROWS = 4096


def _round_up(n, mult):
    return -(-n // mult) * mult


def _pack_rows(flat, row_mult):
    rows = _round_up(-(-flat.shape[0] // LANES), row_mult)
    return jnp.pad(flat, (0, rows * LANES - flat.shape[0])).reshape(rows, LANES)


def _pack_sharded(vals, specs, row_mult, dtype):
    return _pack_rows(jnp.concatenate([vals[name].reshape(-1).astype(dtype) for name, _, _ in specs]), row_mult)


def _unpack_sharded(packed, specs, lead=()):
    flat = packed.reshape(lead + (-1,))
    out, off = {}, 0
    for name, shape, _ in specs:
        size = shape[0] * shape[1]
        out[name] = flat[..., off:off + size].reshape(lead + shape)
        off += size
    return out


def _stacking(specs):
    return tuple((name, (shape[1], shape[0]) if by_cols else shape, by_cols) for name, shape, by_cols in specs)


def _to_stacking(vals, specs):
    return {name: (vals[name].T if by_cols else vals[name]) for name, _, by_cols in specs}


REPLICATED_ROWS = sum(-(-size // LANES) for _, size in REPLICATED)
LOSS_ROW = REPLICATED_ROWS


def _pack_replicated(vals):
    rows = []
    for name, size in REPLICATED:
        v = vals[name].reshape(-1).astype(F32)
        rows.append(jnp.pad(v, (0, _round_up(size, LANES) - size)))
    return _pack_rows(jnp.concatenate(rows), 8)


def _unpack_replicated(packed, shapes):
    flat = packed.reshape(-1)
    out, off = {}, 0
    for name, size in REPLICATED:
        out[name] = flat[off:off + size].reshape(shapes[name])
        off += _round_up(size, LANES)
    return out


def _lane_row(v):
    v = v.reshape(-1).astype(F32)
    return jnp.pad(v, (0, LANES - v.shape[0])).reshape(1, LANES)


IN_SPLIT = (("z", SSM_D_INNER), ("xbc", SSM_CONV_DIM), ("dt", SSM_N_HEADS), ("qkv", ATT_QKV_DIM), ("gate", 2 * D_MODEL))


def _split_w_in(w_t):
    out, off = {}, 0
    for name, size in IN_SPLIT:
        out[name] = w_t[off:off + size]
        off += size
    out["dt"] = jnp.pad(out["dt"], ((0, DT_PAD - SSM_N_HEADS), (0, 0)))
    return out


def kernel(x, norm_mix, w_in, b_gate, conv_w, conv_b, dt_bias, a_log, d_skip, ssm_norm, w_ssm_out, w_att_out, w_mix_out, norm_ffn, w_ffn_gate, w_ffn_up, w_ffn_down, norm_final, loss_target, m_norm_mix, m_w_in, m_b_gate, m_conv_w, m_conv_b, m_dt_bias, m_a_log, m_d_skip, m_ssm_norm, m_w_ssm_out, m_w_att_out, m_w_mix_out, m_norm_ffn, m_w_ffn_gate, m_w_ffn_up, m_w_ffn_down, m_norm_final, v_norm_mix, v_w_in, v_b_gate, v_conv_w, v_conv_b, v_dt_bias, v_a_log, v_d_skip, v_ssm_norm, v_w_ssm_out, v_w_att_out, v_w_mix_out, v_norm_ffn, v_w_ffn_gate, v_w_ffn_up, v_w_ffn_down, v_norm_final):
    given = dict(locals())
    weights = {name: given[name][0] for name, _, _ in SHARDED}
    b, s, d = x.shape
    t = b * s

    mat_specs, conv_specs, all_specs = _stacking(MATRIX_SHARDS), _stacking((CONV_SHARD,)), _stacking(SHARDED)
    stacking = _to_stacking(weights, SHARDED)
    mat_local = _pack_sharded(stacking, mat_specs, 16, BF16)
    conv_local = _pack_sharded(stacking, conv_specs, 8, F32)
    mat_all, conv_all = _all_gather([mat_local, conv_local], "weights_all_gather")
    shards = _unpack_sharded(mat_all, mat_specs, (N_DEV,))
    shards.update(_unpack_sharded(conv_all, conv_specs, (N_DEV,)))
    full = {name: shards[name].reshape(N_DEV * shape[0], shape[1]) for name, shape, _ in all_specs}
    w_sec = _split_w_in(full["w_in"])
    conv_taps = full["conv_w"].T

    g_mix, g_ffn, g_fin = norm_mix.reshape(1, d), norm_ffn.reshape(1, d), norm_final.reshape(1, d)
    bg_row = b_gate.reshape(1, 2 * d)
    convb_row = conv_b.reshape(1, SSM_CONV_DIM)
    ssmn_row = ssm_norm.reshape(1, SSM_D_INNER)
    dtb_row, alog_row = _lane_row(dt_bias), _lane_row(a_log)
    cosf, sinf = _rope_tables(s)

    x2d = x.reshape(t, d)
    h1 = _rmsnorm_fwd(x2d, g_mix, "norm_mix_fwd")
    proj = {name: _mm(h1, w_sec[name], mode="nt", name="in_proj_" + name) for name, _ in IN_SPLIT}
    xbc3 = proj["xbc"].reshape(b, s, SSM_CONV_DIM)
    xc = _conv_fwd(xbc3, conv_taps, convb_row, "conv_fwd")
    dtr3 = proj["dt"].reshape(b, s, DT_PAD)
    to_channels, to_heads = _head_masks()
    dskx = jnp.repeat(d_skip.reshape(-1).astype(F32), SSM_HEAD_DIM).reshape(1, SSM_D_INNER)
    y_ssd, h_states = _ssd_fwd(xc, dtr3, dtb_row, alog_row, dskx, to_channels, "ssd_fwd")
    y_ssd2 = y_ssd.reshape(t, SSM_D_INNER)
    ynorm = _gate_norm_fwd(y_ssd2, proj["z"], ssmn_row, "ssd_gate_norm_fwd")
    y_ssm = _mm(ynorm, full["w_ssm_out"], mode="nn", name="ssm_out_proj")

    qkv3 = proj["qkv"].reshape(b, s, ATT_QKV_DIM)
    qk_parts = _rope_fwd(qkv3, cosf, sinf, "rope_fwd")
    att_parts = [_att_fwd(qk_parts[gi], "att_fwd_%d" % r) for gi, r in enumerate(ATT_DILATIONS)]
    att, *lse_parts = _att_merge([o for o, _ in att_parts], [l_ for _, l_ in att_parts], "att_merge")
    att2 = att.reshape(t, ATT_OUT_DIM)
    y_att = _mm(att2, full["w_att_out"], mode="nt", name="att_out_proj")

    mixed = _mix_fwd(proj["gate"], bg_row, y_ssm, y_att, "mix_fwd")
    x2 = _mm(mixed, full["w_mix_out"], mode="nn", name="mix_out_proj", add=x2d)
    h2 = _rmsnorm_fwd(x2, g_ffn, "norm_ffn_fwd")
    gt = _mm(h2, full["w_ffn_gate"], mode="nt", name="ffn_gate_proj")
    up = _mm(h2, full["w_ffn_up"], mode="nt", name="ffn_up_proj")
    act = _swiglu_fwd(gt, up, "swiglu_fwd")
    x3 = _mm(act, full["w_ffn_down"], mode="nn", name="ffn_down_proj", add=x2)

    loss_row, dx3, dg_fin, dx3b = _loss_head(x3, g_fin, loss_target.reshape(t, d), "loss_head")
    grads = {}
    dact = _mm(dx3b, full["w_ffn_down"], mode="nt", name="ffn_down_dx")
    grads["w_ffn_down"] = _mm(act, dx3b, mode="tn", name="ffn_down_dw", out_dtype=BF16)
    dgt, dup = _swiglu_bwd(gt, up, dact, "swiglu_bwd")
    grads["w_ffn_gate"] = _mm(dgt, h2, mode="tn", name="ffn_gate_dw", out_dtype=BF16)
    grads["w_ffn_up"] = _mm(dup, h2, mode="tn", name="ffn_up_dw", out_dtype=BF16)
    dh2 = _mm(dgt, full["w_ffn_gate"], mode="nn", name="ffn_gate_dx")
    dh2 = _mm(dup, full["w_ffn_up"], mode="nn", name="ffn_up_dx", add=dh2)
    dx2, dg_ffn, dx2b = _rmsnorm_bwd(x2, g_ffn, dh2, dx3, "norm_ffn_bwd", with_bf16=True)

    dmixed = _mm(dx2b, full["w_mix_out"], mode="nt", name="mix_out_dx")
    grads["w_mix_out"] = _mm(mixed, dx2b, mode="tn", name="mix_out_dw", out_dtype=BF16)
    dys, dya, dproj, dbg = _mix_bwd(proj["gate"], bg_row, y_ssm, y_att, dmixed, "mix_bwd")

    grads["w_ssm_out"] = _mm(ynorm, dys, mode="tn", name="ssm_out_dw", out_dtype=BF16)
    dynorm = _mm(dys, full["w_ssm_out"], mode="nt", name="ssm_out_dx")
    dy_ssd, dproj, dssmn = _gate_norm_bwd(y_ssd2, proj["z"], ssmn_row, dynorm, dproj, "ssd_gate_norm_bwd")
    dxc, dproj, dalog, ddsk, ddtb = _ssd_bwd(xc, dtr3, dy_ssd.reshape(b, s, SSM_D_INNER), h_states, dtb_row, alog_row,
                                             dskx, to_channels, to_heads, dproj.reshape(b, s, DPROJ_WIDTH), "ssd_bwd")
    dproj, dconvw, dconvb = _conv_bwd(xbc3, dxc, conv_taps, convb_row, dproj, "conv_bwd")
    grads["conv_w"] = dconvw.T.astype(BF16)

    grads["w_att_out"] = _mm(dya, att2, mode="tn", name="att_out_dw", out_dtype=BF16)
    datt = _mm(dya, full["w_att_out"], mode="nn", name="att_out_dx").reshape(b, s, ATT_OUT_DIM)
    do_parts, dl_parts = _att_delta(att, datt, "att_delta")
    dqs, dks, dvs = [], [], []
    for gi, r in enumerate(ATT_DILATIONS):
        operands = (qk_parts[gi], do_parts[gi], lse_parts[gi], dl_parts[gi])
        dqs.append(_att_bwd_q(*operands, "att_bwd_q_%d" % r))
        dk_g, dv_g = _att_bwd_kv(*operands, "att_bwd_kv_%d" % r)
        dks.append(dk_g)
        dvs.append(dv_g)
    dproj = _rope_bwd(dqs, dks, dvs, cosf, sinf, dproj, "rope_bwd").reshape(t, DPROJ_WIDTH)

    dw_all = _mm(dproj, h1, mode="tn", name="in_proj_dw", out_dtype=BF16)
    grads["w_in"] = jnp.concatenate([dw_all[DPROJ_COLS[name]:DPROJ_COLS[name] + size] for name, size in IN_SPLIT],
                                    axis=0)
    pieces = dict(w_sec, dt=jnp.pad(w_sec["dt"], ((0, DPROJ_DT_WIDTH - DT_PAD), (0, 0))))
    w_in_all = jnp.concatenate([pieces[name] for name in sorted(DPROJ_COLS, key=DPROJ_COLS.get)], axis=0)
    dh1 = _mm(dproj, w_in_all, mode="nn", name="in_proj_dx")
    grad_x, dg_mix = _rmsnorm_bwd(x2d, g_mix, dh1, dx2, "norm_mix_bwd")

    slabs = jnp.concatenate([grads[name].reshape(N_DEV, -1) for name, _, _ in all_specs], axis=1)
    slab_rows = _round_up(-(-slabs.shape[1] // LANES), PACK_ROWS)
    slabs = jnp.pad(slabs, ((0, 0), (0, slab_rows * LANES - slabs.shape[1]))).reshape(N_DEV, slab_rows, LANES)
    small = {"norm_mix": dg_mix, "b_gate": dbg, "conv_b": dconvb, "dt_bias": ddtb[:, :SSM_N_HEADS],
             "a_log": dalog[:, :SSM_N_HEADS], "d_skip": ddsk[:, :SSM_N_HEADS], "ssm_norm": dssmn,
             "norm_ffn": dg_ffn, "norm_final": dg_fin}
    core = lax.axis_index("c").astype(jnp.int32).reshape(1)
    chip_sums = _chip_sum(slabs, _pair_exchange(slabs, "grad_pair_exchange"), core, "grad_chip_sum")
    shared = _pack_replicated(small)
    shared = shared.at[LOSS_ROW, 0].set(loss_row[0, 0])
    got, got_small = _chip_exchange(chip_sums, shared, "grad_chip_exchange")

    def packed(prefix):
        vals = _to_stacking({name: given[prefix + name][0] for name, _, _ in SHARDED}, SHARDED)
        rep = {name: given[prefix + name] for name, _ in REPLICATED}
        return _pack_sharded(vals, all_specs, PACK_ROWS, F32), _pack_replicated(rep)

    (w_big, w_small), (m_big, m_small), (v_big, v_small) = packed(""), packed("m_"), packed("v_")
    big = _adamw(got, w_big, m_big, v_big, "adamw_sharded")
    sml = _adamw(got_small, w_small, m_small, v_small, "adamw_replicated")

    outs = [sml[0][LOSS_ROW, 0], grad_x.reshape(b, s, d)]
    rep_shapes = {name: given[name].shape for name, _ in REPLICATED}
    order = ["norm_mix", "w_in", "b_gate", "conv_w", "conv_b", "dt_bias", "a_log", "d_skip", "ssm_norm", "w_ssm_out",
             "w_att_out", "w_mix_out", "norm_ffn", "w_ffn_gate", "w_ffn_up", "w_ffn_down", "norm_final"]
    for big_k, sml_k in zip(big, sml):
        sharded = _to_stacking(_unpack_sharded(big_k, all_specs), SHARDED)
        rep = _unpack_replicated(sml_k, rep_shapes)
        for name in order:
            outs.append(sharded[name][None] if name in sharded else rep[name])
    return tuple(outs)
```

```python
import functools
import math

import jax
import jax.numpy as jnp
from jax import lax
from jax.experimental import pallas as pl
from jax.experimental.pallas import tpu as pltpu

F32 = jnp.float32
BF16 = jnp.bfloat16

N_DEV = 8
N_CHIPS = 4
D_MODEL = 1024
SSM_D_INNER = 2048
SSM_HEAD_DIM = 64
SSM_N_HEADS = 32
SSM_N_GROUPS = 4
SSM_HEADS_PER_GROUP = SSM_N_HEADS // SSM_N_GROUPS
SSM_D_STATE = 128
SSM_CONV = 4
SSM_CHUNK = 128
SSM_CONV_DIM = 3072
ATT_HEAD_DIM = 128
ATT_HEADS_PER_GROUP = 4
ATT_DILATIONS = (1, 4, 16)
ATT_N_HEADS = 12
ATT_QKV_DIM = 4608
ATT_OUT_DIM = 512
ATT_BLOCK = 128
ROPE_THETA = 10000.0
D_FF = 2816
IN_PROJ_DIM = 11808
EPS = 1e-6
LANES = 128
DT_PAD = LANES

DPROJ_COLS = {"qkv": 0, "z": 4608, "xbc": 6656, "dt": 9728, "gate": 10240}
DPROJ_DT_WIDTH = 512
DPROJ_WIDTH = 12288

ADAM_LR = 0.001
ADAM_B1 = 0.9
ADAM_B2 = 0.999
ADAM_EPS = 1e-08
ADAM_WD = 0.01
ADAM_STEP = 10

VMEM_LIMIT = 56 * 1024 * 1024
MESH = pl.DeviceIdType.MESH
NEG_INF = float("-inf")


def _pick(n, candidates):
    for c in candidates:
        if n % c == 0:
            return c
    return n


def _params(*sem):
    return pltpu.CompilerParams(dimension_semantics=sem, vmem_limit_bytes=VMEM_LIMIT)


def _sigmoid(x):
    return 1.0 / (1.0 + jnp.exp(-x))


def _softplus(x):
    return jnp.maximum(x, 0.0) + jnp.log(1.0 + jnp.exp(-jnp.abs(x)))


def _dot(a, b, dims):
    return lax.dot_general(a.astype(BF16), b.astype(BF16), (dims, ((), ())), preferred_element_type=F32)


def _nn(a, b):
    return _dot(a, b, ((1,), (0,)))


def _nt(a, b):
    return _dot(a, b, ((1,), (1,)))


def _tn(a, b):
    return _dot(a, b, ((0,), (0,)))


def _split3(v):
    hi = v.astype(BF16)
    r1 = v - hi.astype(F32)
    mid = r1.astype(BF16)
    lo = (r1 - mid.astype(F32)).astype(BF16)
    return hi, mid, lo


def _mask_nn(mask, v):
    mb = mask.astype(BF16)
    hi, mid, lo = _split3(v)
    return _nn(mb, hi) + (_nn(mb, mid) + _nn(mb, lo))


MM_VMEM_BUDGET = 40 * 1024 * 1024
MM_FULL_K = 2816


def _mm_tiles(m, n, k, a_bytes, b_bytes, o_bytes, has_add):
    tk = k if k <= MM_FULL_K else _pick(k, (2048, 1024, 512, 256, 128))
    tn = 1408 if (n > 1024 and n % 1408 == 0) else _pick(n, (1024, 768, 512, 384, 256, 128))
    for tm in (1408, 1024, 768, 512, 384, 256, 128):
        if m % tm:
            continue
        buffers = 2 * (tm * tk * a_bytes + tk * tn * b_bytes + tm * tn * (o_bytes + (4 if has_add else 0)))
        if tk < k:
            buffers += tm * tn * 4
        if buffers <= MM_VMEM_BUDGET:
            return tm, tn, tk
    return _pick(m, (128,)), tn, tk


def _mm(a, b, *, mode, name, out_dtype=F32, add=None):
    if mode == "nn":
        (m, k), n = a.shape, b.shape[1]
    elif mode == "nt":
        (m, k), n = a.shape, b.shape[0]
    else:
        (k, m), n = a.shape, b.shape[1]
    has_add = add is not None
    tm, tn, tk = _mm_tiles(m, n, k, a.dtype.itemsize, b.dtype.itemsize, jnp.dtype(out_dtype).itemsize, has_add)
    nk = k // tk
    dims = {"nn": ((1,), (0,)), "nt": ((1,), (1,)), "tn": ((0,), (0,))}[mode]
    a_spec = {"nn": pl.BlockSpec((tm, tk), lambda i, j, kk: (i, kk)),
              "nt": pl.BlockSpec((tm, tk), lambda i, j, kk: (i, kk)),
              "tn": pl.BlockSpec((tk, tm), lambda i, j, kk: (kk, i))}[mode]
    b_spec = {"nn": pl.BlockSpec((tk, tn), lambda i, j, kk: (kk, j)),
              "nt": pl.BlockSpec((tn, tk), lambda i, j, kk: (j, kk)),
              "tn": pl.BlockSpec((tk, tn), lambda i, j, kk: (kk, j))}[mode]
    o_spec = pl.BlockSpec((tm, tn), lambda i, j, kk: (i, j))

    def finish(r, c_ref, o_ref):
        if has_add:
            r = r + c_ref[...]
        o_ref[...] = r.astype(out_dtype)

    def body_one(*refs):
        a_ref, b_ref = refs[:2]
        finish(_dot(a_ref[...], b_ref[...], dims), refs[2] if has_add else None, refs[-1])

    def body_acc(*refs):
        a_ref, b_ref = refs[:2]
        o_ref, acc = refs[-2:]
        kk = pl.program_id(2)

        @pl.when(kk == 0)
        def _():
            acc[...] = jnp.zeros_like(acc)

        acc[...] += _dot(a_ref[...], b_ref[...], dims)

        @pl.when(kk == nk - 1)
        def _():
            finish(acc[...], refs[2] if has_add else None, o_ref)

    in_specs = [a_spec, b_spec] + ([o_spec] if has_add else [])
    args = (a, b) + ((add,) if has_add else ())
    return pl.pallas_call(
        body_one if nk == 1 else body_acc, name=name, grid=(m // tm, n // tn, nk),
        in_specs=in_specs, out_specs=o_spec,
        out_shape=jax.ShapeDtypeStruct((m, n), out_dtype),
        scratch_shapes=[] if nk == 1 else [pltpu.VMEM((tm, tn), F32)],
        compiler_params=_params("parallel", "parallel", "arbitrary"),
    )(*args)


def _rmsnorm_fwd(x, g, name):
    t, d = x.shape
    tm = _pick(t, (512, 256, 128))

    def body(x_ref, g_ref, o_ref):
        xv = x_ref[...]
        r = lax.rsqrt(jnp.mean(xv * xv, axis=-1, keepdims=True) + EPS)
        o_ref[...] = ((xv * r) * g_ref[...]).astype(BF16)

    return pl.pallas_call(
        body, name=name, grid=(t // tm,),
        in_specs=[pl.BlockSpec((tm, d), lambda i: (i, 0)), pl.BlockSpec((1, d), lambda i: (0, 0))],
        out_specs=pl.BlockSpec((tm, d), lambda i: (i, 0)),
        out_shape=jax.ShapeDtypeStruct((t, d), BF16),
        compiler_params=_params("parallel"),
    )(x, g)


def _rmsnorm_bwd(x, g, dh, dres, name, with_bf16=False):
    t, d = x.shape
    tm = _pick(t, (512, 256, 128))

    def body(x_ref, g_ref, dh_ref, dres_ref, dx_ref, dg_ref, *dxb_ref):
        @pl.when(pl.program_id(0) == 0)
        def _():
            dg_ref[...] = jnp.zeros_like(dg_ref)

        xv = x_ref[...]
        r = lax.rsqrt(jnp.mean(xv * xv, axis=-1, keepdims=True) + EPS)
        xhat = xv * r
        dhv = dh_ref[...]
        dyg = dhv * g_ref[...]
        dx = dres_ref[...] + r * (dyg - xhat * jnp.mean(dyg * xhat, axis=-1, keepdims=True))
        dx_ref[...] = dx
        if with_bf16:
            dxb_ref[0][...] = dx.astype(BF16)
        dg_ref[...] += jnp.sum(dhv * xhat, axis=0, keepdims=True)

    row = pl.BlockSpec((tm, d), lambda i: (i, 0))
    vec = pl.BlockSpec((1, d), lambda i: (0, 0))
    extra = with_bf16 * [jax.ShapeDtypeStruct((t, d), BF16)]
    return pl.pallas_call(
        body, name=name, grid=(t // tm,),
        in_specs=[row, vec, row, row], out_specs=[row, vec] + with_bf16 * [row],
        out_shape=[jax.ShapeDtypeStruct((t, d), F32), jax.ShapeDtypeStruct((1, d), F32)] + extra,
        compiler_params=_params("arbitrary"),
    )(x, g, dh, dres)


def _loss_head(x, g, target, name):
    t, d = x.shape
    tm = _pick(t, (512, 256, 128))

    def body(x_ref, g_ref, t_ref, loss_ref, dx_ref, dg_ref, dxb_ref):
        @pl.when(pl.program_id(0) == 0)
        def _():
            dg_ref[...] = jnp.zeros_like(dg_ref)
            loss_ref[...] = jnp.zeros_like(loss_ref)

        xv = x_ref[...]
        gv = g_ref[...]
        r = lax.rsqrt(jnp.mean(xv * xv, axis=-1, keepdims=True) + EPS)
        xhat = xv * r
        err = xhat * gv - t_ref[...]
        loss_ref[...] += jnp.sum(err * err) * (0.5 / d)
        dy = err * (1.0 / d)
        dyg = dy * gv
        dx = r * (dyg - xhat * jnp.mean(dyg * xhat, axis=-1, keepdims=True))
        dx_ref[...] = dx
        dxb_ref[...] = dx.astype(BF16)
        dg_ref[...] += jnp.sum(dy * xhat, axis=0, keepdims=True)

    row = pl.BlockSpec((tm, d), lambda i: (i, 0))
    vec = pl.BlockSpec((1, d), lambda i: (0, 0))
    return pl.pallas_call(
        body, name=name, grid=(t // tm,),
        in_specs=[row, vec, row],
        out_specs=[pl.BlockSpec((1, LANES), lambda i: (0, 0)), row, vec, row],
        out_shape=[jax.ShapeDtypeStruct((1, LANES), F32), jax.ShapeDtypeStruct((t, d), F32),
                   jax.ShapeDtypeStruct((1, d), F32), jax.ShapeDtypeStruct((t, d), BF16)],
        compiler_params=_params("arbitrary"),
    )(x, g, target)


CONV_HALO = 8
CONV_ROWS = 64


def _conv_taps(window, wv, bv):
    acc = bv + wv[SSM_CONV - 1:SSM_CONV, :] * window(0)
    for sh in range(1, SSM_CONV):
        kidx = SSM_CONV - 1 - sh
        acc = acc + wv[kidx:kidx + 1, :] * window(sh)
    return acc


def _conv_fwd(u, w, bias, name):
    b, s, c = u.shape
    rows = CONV_ROWS

    def body(u_ref, w_ref, b_ref, o_ref, ext):
        ext[0:CONV_HALO, :] = jnp.zeros((CONV_HALO, LANES), F32)
        ext[CONV_HALO:, :] = u_ref[...]
        wv, bv = w_ref[...], b_ref[...]
        for r0 in range(0, s, rows):
            acc = _conv_taps(lambda sh: ext[CONV_HALO + r0 - sh:CONV_HALO + r0 - sh + rows, :], wv, bv)
            o_ref[r0:r0 + rows, :] = acc * _sigmoid(acc)

    strip = pl.BlockSpec((None, s, LANES), lambda bi, j: (bi, 0, j))
    return pl.pallas_call(
        body, name=name, grid=(b, c // LANES),
        in_specs=[strip, pl.BlockSpec((SSM_CONV, LANES), lambda bi, j: (0, j)),
                  pl.BlockSpec((1, LANES), lambda bi, j: (0, j))],
        out_specs=strip, out_shape=jax.ShapeDtypeStruct((b, s, c), F32),
        scratch_shapes=[pltpu.VMEM((CONV_HALO + s, LANES), F32)],
        compiler_params=_params("parallel", "parallel"),
    )(u, w, bias)


def _conv_bwd(u, dout, w, bias, dproj, name):
    b, s, c = u.shape
    rows = CONV_ROWS

    def fold(v):
        return jnp.sum(v.reshape(rows // CONV_HALO, CONV_HALO, LANES), axis=0)

    def body(u_ref, d_ref, w_ref, b_ref, buf_ref, du_ref, dw_ref, db_ref, ext, dpre):
        @pl.when(pl.program_id(1) == 0)
        def _():
            dw_ref[...] = jnp.zeros_like(dw_ref)
            db_ref[...] = jnp.zeros_like(db_ref)

        ext[0:CONV_HALO, :] = jnp.zeros((CONV_HALO, LANES), F32)
        ext[CONV_HALO:, :] = u_ref[...]
        dpre[s:, :] = jnp.zeros((CONV_HALO, LANES), F32)
        wv, bv = w_ref[...], b_ref[...]
        sums = [jnp.zeros((CONV_HALO, LANES), F32)] * (SSM_CONV + 1)
        for r0 in range(0, s, rows):
            window = lambda sh: ext[CONV_HALO + r0 - sh:CONV_HALO + r0 - sh + rows, :]
            acc = _conv_taps(window, wv, bv)
            sg = _sigmoid(acc)
            dp = d_ref[r0:r0 + rows, :] * (sg * (1.0 + acc * (1.0 - sg)))
            dpre[r0:r0 + rows, :] = dp
            taps = [sums[SSM_CONV - 1 - sh] + fold(dp * window(sh)) for sh in range(SSM_CONV)]
            sums = taps[::-1] + [sums[SSM_CONV] + fold(dp)]
        for r0 in range(0, s, rows):
            du = wv[SSM_CONV - 1:SSM_CONV, :] * dpre[r0:r0 + rows, :]
            for sh in range(1, SSM_CONV):
                kidx = SSM_CONV - 1 - sh
                du = du + wv[kidx:kidx + 1, :] * dpre[r0 + sh:r0 + sh + rows, :]
            du_ref[r0:r0 + rows, :] = du.astype(BF16)
        for kidx in range(SSM_CONV):
            dw_ref[kidx:kidx + 1, :] += jnp.sum(sums[kidx], axis=0, keepdims=True)
        db_ref[...] += jnp.sum(sums[SSM_CONV], axis=0, keepdims=True)

    strip = pl.BlockSpec((None, s, LANES), lambda j, bi: (bi, 0, j))
    taps = pl.BlockSpec((SSM_CONV, LANES), lambda j, bi: (0, j))
    vec = pl.BlockSpec((1, LANES), lambda j, bi: (0, j))
    du_cols = pl.BlockSpec((None, s, LANES), lambda j, bi: (bi, 0, DPROJ_COLS["xbc"] // LANES + j))
    return pl.pallas_call(
        body, name=name, grid=(c // LANES, b),
        in_specs=[strip, strip, taps, vec, pl.BlockSpec(memory_space=pl.ANY)], out_specs=[du_cols, taps, vec],
        input_output_aliases={4: 0},
        out_shape=[jax.ShapeDtypeStruct(dproj.shape, dproj.dtype), jax.ShapeDtypeStruct((SSM_CONV, c), F32),
                   jax.ShapeDtypeStruct((1, c), F32)],
        scratch_shapes=[pltpu.VMEM((CONV_HALO + s, LANES), F32), pltpu.VMEM((s + CONV_HALO, LANES), F32)],
        compiler_params=_params("parallel", "arbitrary"),
    )(u, dout, w, bias, dproj)


def _ssd_chunk_terms(dtr_ref, bias_ref, alog_ref):
    q = SSM_CHUNK
    dt = _softplus(dtr_ref[...] + bias_ref[...])
    a_neg = -jnp.exp(alog_ref[...])
    row = lax.broadcasted_iota(jnp.int32, (q, q), 0)
    col = lax.broadcasted_iota(jnp.int32, (q, q), 1)
    lower = row >= col
    s = _mask_nn(lower, dt * a_neg)
    return dt, a_neg, s, s.T, lower


def _head_masks():
    heads = jnp.arange(LANES)[:, None]
    chans = jnp.arange(SSM_D_INNER)[None, :]
    to_channels = (chans // SSM_HEAD_DIM == heads).astype(BF16)
    return to_channels, to_channels.T


def _per_channel(v, to_channels):
    hi = v.astype(BF16)
    lo = (v - hi.astype(F32)).astype(BF16)
    return _nn(hi, to_channels) + _nn(lo, to_channels)


def _per_head(v, to_heads):
    hi = v.astype(BF16)
    lo = (v - hi.astype(F32)).astype(BF16)
    return _nn(hi, to_heads) + _nn(lo, to_heads)


def _decay_terms_per_channel(dt, s_col, to_channels):
    q = SSM_CHUNK
    tot = s_col[q - 1:q, :]
    stacked = jnp.concatenate([dt, jnp.exp(s_col), jnp.exp(tot - s_col)], axis=0)
    wide = _per_channel(stacked, to_channels)
    dtx, esx, decx = wide[:q], wide[q:2 * q], wide[2 * q:]
    return dtx, esx, decx, esx[0:1, :] * decx[0:1, :]


SSM_PAIRS_PER_GROUP = SSM_HEADS_PER_GROUP // 2
SSM_GROUP_CHANNELS = SSM_HEADS_PER_GROUP * SSM_HEAD_DIM


def _split_pair(v):
    first = lax.broadcasted_iota(jnp.int32, v.shape, 1) < SSM_HEAD_DIM
    return jnp.concatenate([jnp.where(first, v, 0.0), jnp.where(first, 0.0, v)], axis=0)


def _ssd_fwd(xc, dtr, dt_bias, a_log, dskx, to_channels, name):
    b, s, _ = xc.shape
    q = SSM_CHUNK
    nc = s // q
    n, gc = SSM_D_STATE, SSM_GROUP_CHANNELS

    def body(xc_ref, dtr_ref, bias_ref, alog_ref, dsk_ref, tc_ref, y_ref, hs_ref, h_scr):
        @pl.when(pl.program_id(1) == 0)
        def _():
            h_scr[...] = jnp.zeros_like(h_scr)

        dt, _, s_col, s_row, lower = _ssd_chunk_terms(dtr_ref, bias_ref, alog_ref)
        dtx, esx, decx, etotx = _decay_terms_per_channel(dt, s_col, tc_ref[...])
        x = xc_ref[:, :SSM_D_INNER]
        xdt = x * dtx
        xdec = xdt * decx
        skip = dsk_ref[...] * x
        for g in range(SSM_N_GROUPS):
            bg = xc_ref[:, SSM_D_INNER + n * g:SSM_D_INNER + n * (g + 1)].astype(BF16)
            cg = xc_ref[:, SSM_D_INNER + n * (SSM_N_GROUPS + g):SSM_D_INNER + n * (SSM_N_GROUPS + g + 1)].astype(BF16)
            gsl = slice(gc * g, gc * (g + 1))
            gm = _nt(cg, bg)
            hgt = h_scr[:, gsl]
            hs_ref[:, gsl] = hgt
            y_off = esx[:, gsl] * _nn(cg, hgt)
            h_scr[:, gsl] = etotx[:, gsl] * hgt + _tn(bg, xdec[:, gsl])
            for k in range(SSM_PAIRS_PER_GROUP):
                h0 = g * SSM_HEADS_PER_GROUP + 2 * k
                lo = gc * g + LANES * k
                ms = []
                for h in (h0, h0 + 1):
                    lm = jnp.exp(jnp.where(lower, s_col[:, h:h + 1] - s_row[h:h + 1, :], NEG_INF))
                    ms.append((gm * lm).astype(BF16))
                y_diag = _nn(jnp.concatenate(ms, axis=1), _split_pair(xdt[:, lo:lo + LANES]))
                y_ref[:, lo:lo + LANES] = y_diag + y_off[:, LANES * k:LANES * (k + 1)] + skip[:, lo:lo + LANES]

    vec = pl.BlockSpec((1, LANES), lambda bi, c: (0, 0))
    return pl.pallas_call(
        body, name=name, grid=(b, nc),
        in_specs=[pl.BlockSpec((None, q, SSM_CONV_DIM), lambda bi, c: (bi, c, 0)),
                  pl.BlockSpec((None, q, LANES), lambda bi, c: (bi, c, 0)), vec, vec,
                  pl.BlockSpec((1, SSM_D_INNER), lambda bi, c: (0, 0)),
                  pl.BlockSpec((LANES, SSM_D_INNER), lambda bi, c: (0, 0))],
        out_specs=[pl.BlockSpec((None, q, SSM_D_INNER), lambda bi, c: (bi, c, 0)),
                   pl.BlockSpec((None, None, n, SSM_D_INNER), lambda bi, c: (bi, c, 0, 0))],
        out_shape=[jax.ShapeDtypeStruct((b, s, SSM_D_INNER), F32),
                   jax.ShapeDtypeStruct((b, nc, n, SSM_D_INNER), F32)],
        scratch_shapes=[pltpu.VMEM((n, SSM_D_INNER), F32)],
        compiler_params=_params("parallel", "arbitrary"),
    )(xc, dtr, dt_bias, a_log, dskx, to_channels)


def _ssd_bwd(xc, dtr, dy, hs, dt_bias, a_log, dskx, to_channels, to_heads, dproj, name):
    b, s, _ = xc.shape
    q = SSM_CHUNK
    nc = s // q
    n, gc = SSM_D_STATE, SSM_GROUP_CHANNELS

    def colsum(v):
        return jnp.sum(v, axis=0, keepdims=True)

    def body(xc_ref, dtr_ref, dy_ref, hs_ref, bias_ref, alog_ref, dsk_ref, tc_ref, th_ref, buf_ref,
             dxc_ref, ddtr_ref, dalog_ref, ddsk_ref, dbias_ref, dh_scr, dxs_scr, dxd_scr, w_scr, dst_scr, rows_scr):
        ci = pl.program_id(1)

        @pl.when(ci == 0)
        def _():
            dh_scr[...] = jnp.zeros_like(dh_scr)

        @pl.when(jnp.logical_and(pl.program_id(0) == 0, ci == 0))
        def _():
            dalog_ref[...] = jnp.zeros_like(dalog_ref)
            ddsk_ref[...] = jnp.zeros_like(ddsk_ref)
            dbias_ref[...] = jnp.zeros_like(dbias_ref)
            dst_scr[...] = jnp.zeros_like(dst_scr)

        dt, a_neg, s_col, s_row, lower = _ssd_chunk_terms(dtr_ref, bias_ref, alog_ref)
        upper = jnp.logical_not(lower) | (lax.broadcasted_iota(jnp.int32, (q, q), 0)
                                          == lax.broadcasted_iota(jnp.int32, (q, q), 1))
        dtx, esx, decx, etotx = _decay_terms_per_channel(dt, s_col, tc_ref[...])
        x = xc_ref[:, :SSM_D_INNER]
        dyv = dy_ref[...]
        xdt = x * dtx
        xdec = xdt * decx
        dw = esx * dyv
        rows_scr[...] = jnp.zeros_like(rows_scr)
        for g in range(SSM_N_GROUPS):
            b_lo = SSM_D_INNER + n * g
            c_lo = SSM_D_INNER + n * (SSM_N_GROUPS + g)
            bg = xc_ref[:, b_lo:b_lo + n].astype(BF16)
            cg = xc_ref[:, c_lo:c_lo + n].astype(BF16)
            gsl = slice(gc * g, gc * (g + 1))
            gm = _nt(cg, bg)
            gmt = _nt(bg, cg)
            hgt = hs_ref[:, gsl]
            dhgt = dh_scr[:, gsl]
            w_scr[:, gsl] = _nn(cg, hgt)
            dcg = _nt(dw[:, gsl], hgt)
            dxs = decx[:, gsl] * _nn(bg, dhgt)
            dxs_scr[:, gsl] = dxs
            dbg = _nt(xdec[:, gsl], dhgt)
            rows_scr[2:3, gsl] = colsum(dhgt * hgt)
            dh_scr[:, gsl] = _tn(cg, dw[:, gsl]) + etotx[:, gsl] * dhgt
            dg = jnp.zeros((q, q), F32)
            dgt = jnp.zeros((q, q), F32)
            for k in range(SSM_PAIRS_PER_GROUP):
                h0 = g * SSM_HEADS_PER_GROUP + 2 * k
                lo = gc * g + LANES * k
                xp = xdt[:, lo:lo + LANES]
                dyp = dyv[:, lo:lo + LANES]
                dy2 = _split_pair(dyp)
                dm2 = _nt(dy2, xp)
                dmt2 = _nt(_split_pair(xp), dyp)
                mts = []
                for i, h in enumerate((h0, h0 + 1)):
                    lm = jnp.exp(jnp.where(lower, s_col[:, h:h + 1] - s_row[h:h + 1, :], NEG_INF))
                    lmt = jnp.exp(jnp.where(upper, s_row[h:h + 1, :] - s_col[:, h:h + 1], NEG_INF))
                    dm = dm2[q * i:q * (i + 1), :]
                    dmt = dmt2[q * i:q * (i + 1), :]
                    dg = dg + dm * lm
                    dgt = dgt + dmt * lmt
                    mt = gmt * lmt
                    dst_scr[h:h + 1, :] = colsum(dmt * mt) - colsum(dm * (gm * lm))
                    mts.append(mt.astype(BF16))
                dxd_scr[:, lo:lo + LANES] = _nn(jnp.concatenate(mts, axis=1), dy2)
            dxc_ref[:, b_lo:b_lo + n] = dbg + _nn(dgt, cg)
            dxc_ref[:, c_lo:c_lo + n] = dcg + _nn(dg, bg)
        dxs = dxs_scr[...]
        dxdt = dxd_scr[...] + dxs
        dxc_ref[:, :SSM_D_INNER] = dxdt * dtx + dsk_ref[...] * dyv
        state_part = xdt * dxs
        rows_scr[0:1, :] = colsum(dyv * x)
        rows_scr[1:2, :] = colsum(state_part)
        th = th_ref[...]
        per_head = _per_head(jnp.concatenate([dw * w_scr[...] - state_part, dxdt * x], axis=0), th)
        r_ds, r_dt = per_head[:q], per_head[q:]
        sums = _per_head(rows_scr[...], th)
        etot = jnp.exp(s_col[q - 1:q, :])
        dtot = sums[1:2, :] + etot * sums[2:3, :]
        last = lax.broadcasted_iota(jnp.int32, (q, LANES), 0) == q - 1
        ds = dst_scr[...].T + r_ds + jnp.where(last, dtot, 0.0)
        da = _mask_nn(upper, ds)
        ddt = da * a_neg + r_dt
        live = lax.broadcasted_iota(jnp.int32, (1, LANES), 1) < SSM_N_HEADS
        sg = _sigmoid(dtr_ref[...] + bias_ref[...])
        ddtr = jnp.where(live, ddt * sg, 0.0)
        ddtr_ref[:, :LANES] = ddtr.astype(BF16)
        ddtr_ref[:, LANES:] = jnp.zeros((q, DPROJ_DT_WIDTH - LANES), BF16)
        dalog_ref[...] += jnp.where(live, colsum(da * dt) * a_neg, 0.0)
        ddsk_ref[...] += jnp.where(live, sums[0:1, :], 0.0)
        dbias_ref[...] += colsum(ddtr)

    rev = lambda bi, c: (bi, nc - 1 - c, 0)
    vec = pl.BlockSpec((1, LANES), lambda bi, c: (0, 0))
    wide = pl.BlockSpec((None, q, SSM_D_INNER), rev)
    return pl.pallas_call(
        body, name=name, grid=(b, nc),
        in_specs=[pl.BlockSpec((None, q, SSM_CONV_DIM), rev), pl.BlockSpec((None, q, LANES), rev), wide,
                  pl.BlockSpec((None, None, n, SSM_D_INNER), lambda bi, c: (bi, nc - 1 - c, 0, 0)),
                  vec, vec, pl.BlockSpec((1, SSM_D_INNER), lambda bi, c: (0, 0)),
                  pl.BlockSpec((LANES, SSM_D_INNER), lambda bi, c: (0, 0)),
                  pl.BlockSpec((SSM_D_INNER, LANES), lambda bi, c: (0, 0)),
                  pl.BlockSpec(memory_space=pl.ANY)],
        out_specs=[pl.BlockSpec((None, q, SSM_CONV_DIM), rev),
                   pl.BlockSpec((None, q, DPROJ_DT_WIDTH),
                                lambda bi, c: (bi, nc - 1 - c, DPROJ_COLS["dt"] // DPROJ_DT_WIDTH)), vec, vec, vec],
        input_output_aliases={9: 1},
        out_shape=[jax.ShapeDtypeStruct((b, s, SSM_CONV_DIM), F32), jax.ShapeDtypeStruct(dproj.shape, dproj.dtype),
                   jax.ShapeDtypeStruct((1, LANES), F32), jax.ShapeDtypeStruct((1, LANES), F32),
                   jax.ShapeDtypeStruct((1, LANES), F32)],
        scratch_shapes=[pltpu.VMEM((n, SSM_D_INNER), F32)] + [pltpu.VMEM((q, SSM_D_INNER), F32)] * 3
        + [pltpu.VMEM((LANES, q), F32), pltpu.VMEM((8, SSM_D_INNER), F32)],
        compiler_params=_params("arbitrary", "arbitrary"),
    )(xc, dtr, dy, hs, dt_bias, a_log, dskx, to_channels, to_heads, dproj)


SSM_GROUP_WIDTH = SSM_D_INNER // SSM_N_GROUPS


def _gate_norm_fwd(y, z, w, name):
    t, d = y.shape
    tm = _pick(t, (256, 128))

    def body(y_ref, z_ref, w_ref, o_ref):
        for g in range(SSM_N_GROUPS):
            sl = slice(SSM_GROUP_WIDTH * g, SSM_GROUP_WIDTH * (g + 1))
            zv = z_ref[:, sl]
            u = y_ref[:, sl] * (zv * _sigmoid(zv))
            r = lax.rsqrt(jnp.mean(u * u, axis=-1, keepdims=True) + EPS)
            o_ref[:, sl] = ((u * r) * w_ref[:, sl]).astype(BF16)

    row = pl.BlockSpec((tm, d), lambda i: (i, 0))
    return pl.pallas_call(
        body, name=name, grid=(t // tm,),
        in_specs=[row, row, pl.BlockSpec((1, d), lambda i: (0, 0))], out_specs=row,
        out_shape=jax.ShapeDtypeStruct((t, d), BF16),
        compiler_params=_params("parallel"),
    )(y, z, w)


def _gate_norm_bwd(y, z, w, dout, dproj, name):
    t, d = y.shape
    gw = SSM_GROUP_WIDTH
    tm = _pick(t, (1024, 512, 256, 128))

    def body(y_ref, z_ref, w_ref, do_ref, buf_ref, dy_ref, dz_ref, dw_ref):
        @pl.when(pl.program_id(1) == 0)
        def _():
            dw_ref[...] = jnp.zeros_like(dw_ref)

        zv = z_ref[...]
        yv = y_ref[...]
        sg = _sigmoid(zv)
        silu = zv * sg
        u = yv * silu
        r = lax.rsqrt(jnp.mean(u * u, axis=-1, keepdims=True) + EPS)
        uh = u * r
        dov = do_ref[...]
        dw_ref[...] += jnp.sum(dov * uh, axis=0, keepdims=True)
        dyg = dov * w_ref[...]
        du = r * (dyg - uh * jnp.mean(dyg * uh, axis=-1, keepdims=True))
        dy_ref[...] = du * silu
        dz_ref[...] = (du * yv * (sg * (1.0 + zv * (1.0 - sg)))).astype(BF16)

    tile = pl.BlockSpec((tm, gw), lambda g, i: (i, g))
    vec = pl.BlockSpec((1, gw), lambda g, i: (0, g))
    z_cols = pl.BlockSpec((tm, gw), lambda g, i: (i, DPROJ_COLS["z"] // gw + g))
    return pl.pallas_call(
        body, name=name, grid=(SSM_N_GROUPS, t // tm),
        in_specs=[tile, tile, vec, tile, pl.BlockSpec(memory_space=pl.ANY)], out_specs=[tile, z_cols, vec],
        out_shape=[jax.ShapeDtypeStruct((t, d), F32), jax.ShapeDtypeStruct(dproj.shape, dproj.dtype),
                   jax.ShapeDtypeStruct((1, d), F32)],
        input_output_aliases={4: 1},
        compiler_params=_params("parallel", "arbitrary"),
    )(y, z, w, dout, dproj)


def _rope_tables(s):
    half = ATT_HEAD_DIM // 2
    inv = ROPE_THETA ** (-jnp.arange(half, dtype=F32) / half)
    ang = jnp.arange(s).astype(F32)[:, None] * inv[None, :]
    cos, sin = jnp.cos(ang), jnp.sin(ang)
    return jnp.concatenate([cos, cos], axis=-1), jnp.concatenate([-sin, sin], axis=-1)


ATT_TILE = 256


def _by_residue_spec(r, width):
    return pl.BlockSpec((None, r, ATT_TILE // r, width), lambda bi, i: (bi, 0, i, 0))


def _to_residues(tile, stage, r, store):
    if r == 1:
        store(0, tile)
        return
    stage[...] = tile
    for ri in range(r):
        store(ri, stage[pl.ds(ri, ATT_TILE // r, stride=r), :])


def _from_residues(load, stage, r):
    if r == 1:
        return load(0)
    for ri in range(r):
        stage[pl.ds(ri, ATT_TILE // r, stride=r), :] = load(ri)
    return stage[...]


def _rope_fwd(qkv, cosf, sinf, name):
    b, s, w = qkv.shape
    ts, d, gw = ATT_TILE, ATT_HEAD_DIM, ATT_OUT_DIM

    def body(x_ref, c_ref, s_ref, *rest):
        outs, stage = rest[:-1], rest[-1]
        cv, sv = c_ref[...], s_ref[...]
        for kind in range(3):
            for gi, r in enumerate(ATT_DILATIONS):
                for j in range(ATT_HEADS_PER_GROUP):
                    src = d * (kind * ATT_N_HEADS + gi * ATT_HEADS_PER_GROUP + j)
                    dst = slice(kind * gw + d * j, kind * gw + d * (j + 1))
                    tv = x_ref[:, src:src + d]
                    if kind < 2:
                        tv = tv * cv + pltpu.roll(tv, d // 2, 1) * sv

                    def store(ri, rows, o_ref=outs[gi], dst=dst):
                        o_ref[ri, :, dst] = rows.astype(BF16)

                    _to_residues(tv, stage, r, store)

    tab = pl.BlockSpec((ts, d), lambda bi, i: (i, 0))
    return pl.pallas_call(
        body, name=name, grid=(b, s // ts),
        in_specs=[pl.BlockSpec((None, ts, w), lambda bi, i: (bi, i, 0)), tab, tab],
        out_specs=[_by_residue_spec(r, 3 * gw) for r in ATT_DILATIONS],
        out_shape=[jax.ShapeDtypeStruct((b, r, s // r, 3 * gw), BF16) for r in ATT_DILATIONS],
        scratch_shapes=[pltpu.VMEM((ts, d), F32)],
        compiler_params=_params("parallel", "parallel"),
    )(qkv, cosf, sinf)


def _rope_bwd(dq, dk, dv, cosf, sinf, dproj, name):
    n_pat = len(ATT_DILATIONS)
    b, _, s, gw = dq[0].shape
    ts, d = ATT_TILE, ATT_HEAD_DIM

    def body(*refs):
        ins, (c_ref, s_ref, _, o_ref, stage) = refs[:3 * n_pat], refs[3 * n_pat:]
        cv, sv = c_ref[...], s_ref[...]
        for kind in range(3):
            for gi, r in enumerate(ATT_DILATIONS):
                src = ins[kind * n_pat + gi]
                for j in range(ATT_HEADS_PER_GROUP):
                    tv = _from_residues(lambda ri, src=src, j=j: src[ri, :, d * j:d * (j + 1)], stage, r)
                    if kind < 2:
                        tv = tv * cv + pltpu.roll(tv * sv, d // 2, 1)
                    lo = d * (kind * ATT_N_HEADS + gi * ATT_HEADS_PER_GROUP + j)
                    o_ref[:, lo:lo + d] = tv.astype(BF16)

    tab = pl.BlockSpec((ts, d), lambda bi, i: (i, 0))
    parts = [_by_residue_spec(r, gw) for r in ATT_DILATIONS]
    return pl.pallas_call(
        body, name=name, grid=(b, s // ts), in_specs=parts * 3 + [tab, tab, pl.BlockSpec(memory_space=pl.ANY)],
        out_specs=pl.BlockSpec((None, ts, ATT_QKV_DIM), lambda bi, i: (bi, i, DPROJ_COLS["qkv"] // ATT_QKV_DIM)),
        out_shape=jax.ShapeDtypeStruct(dproj.shape, dproj.dtype),
        input_output_aliases={3 * n_pat + 2: 0},
        scratch_shapes=[pltpu.VMEM((ts, d), F32)],
        compiler_params=_params("parallel", "parallel"),
    )(*dq, *dk, *dv, cosf, sinf, dproj)


ATT_SCALE = ATT_HEAD_DIM ** -0.5
ATT_STEP = 2 * ATT_BLOCK


def _att_spec(col):
    return pl.BlockSpec((None, None, ATT_STEP, ATT_OUT_DIM), lambda bi, ri, i: (bi, ri, i, col))


def _att_edge_spec(col, side, n_steps):
    def index(bi, ri, i):
        blk = 2 * i - 1 if side < 0 else 2 * i + 2
        return (bi, ri, jnp.clip(blk, 0, 2 * n_steps - 1), col)
    return pl.BlockSpec((None, None, ATT_BLOCK, ATT_OUT_DIM), index)


def _band_mask(shape, q_axis, has_prev):
    qi = lax.broadcasted_iota(jnp.int32, shape, q_axis)
    kj = lax.broadcasted_iota(jnp.int32, shape, 1 - q_axis)
    dist = qi + ATT_BLOCK - kj
    return (dist >= 0) & (dist <= ATT_BLOCK) & (has_prev | (kj >= ATT_BLOCK))


def _att_fwd(qkr, name):
    b, r, l, _ = qkr.shape
    nb = l // ATT_STEP
    d = ATT_HEAD_DIM

    def body(q_ref, kp_ref, k_ref, vp_ref, v_ref, o_ref, lse_ref):
        mask = _band_mask((ATT_STEP, ATT_BLOCK + ATT_STEP), 0, pl.program_id(2) > 0)
        for j in range(ATT_HEADS_PER_GROUP):
            sl = slice(d * j, d * (j + 1))
            kcat = jnp.concatenate([kp_ref[:, sl], k_ref[:, sl]], axis=0)
            vcat = jnp.concatenate([vp_ref[:, sl], v_ref[:, sl]], axis=0)
            sc = jnp.where(mask, _nt(q_ref[:, sl], kcat) * ATT_SCALE, NEG_INF)
            m = jnp.max(sc, axis=-1, keepdims=True)
            pr = jnp.exp(sc - m)
            den = jnp.sum(pr, axis=-1, keepdims=True)
            o_ref[:, sl] = _nn(pr / den, vcat)
            lse_ref[:, sl] = jnp.broadcast_to(m + jnp.log(den), (ATT_STEP, d))

    out_spec = _att_spec(0)
    return pl.pallas_call(
        body, name=name, grid=(b, r, nb),
        in_specs=[_att_spec(0), _att_edge_spec(1, -1, nb), _att_spec(1), _att_edge_spec(2, -1, nb), _att_spec(2)],
        out_specs=[out_spec, out_spec],
        out_shape=[jax.ShapeDtypeStruct((b, r, l, ATT_OUT_DIM), F32)] * 2,
        compiler_params=_params("parallel", "parallel", "parallel"),
    )(qkr, qkr, qkr, qkr, qkr)


def _att_merge(os_, lses, name):
    n_pat = len(os_)
    b, _, s, gw = os_[0].shape
    ts, d = ATT_TILE, ATT_HEAD_DIM

    def body(*refs):
        o_refs, l_refs = refs[:n_pat], refs[n_pat:2 * n_pat]
        att_ref, lse_outs, stage = refs[2 * n_pat], refs[2 * n_pat + 1:3 * n_pat + 1], refs[-1]
        for j in range(ATT_HEADS_PER_GROUP):
            sl = slice(d * j, d * (j + 1))
            ov = [_from_residues(lambda ri, g=g: o_refs[g][ri, :, sl], stage, r)
                  for g, r in enumerate(ATT_DILATIONS)]
            ls = [_from_residues(lambda ri, g=g: l_refs[g][ri, :, sl], stage, r)
                  for g, r in enumerate(ATT_DILATIONS)]
            m = functools.reduce(jnp.maximum, ls)
            es = [jnp.exp(lv - m) for lv in ls]
            tot = functools.reduce(lambda u, v: u + v, es)
            acc = (es[0] / tot) * ov[0]
            for g in range(1, n_pat):
                acc = acc + (es[g] / tot) * ov[g]
            att_ref[:, sl] = acc
            joint = m + jnp.log(tot)
            for g, r in enumerate(ATT_DILATIONS):
                def store(ri, rows, out=lse_outs[g]):
                    out[ri, :, sl] = rows
                _to_residues(joint, stage, r, store)

    parts = [_by_residue_spec(r, gw) for r in ATT_DILATIONS]
    return pl.pallas_call(
        body, name=name, grid=(b, s // ts), in_specs=parts * 2,
        out_specs=[pl.BlockSpec((None, ts, gw), lambda bi, i: (bi, i, 0))] + parts,
        out_shape=[jax.ShapeDtypeStruct((b, s, gw), F32)]
        + [jax.ShapeDtypeStruct((b, r, s // r, gw), F32) for r in ATT_DILATIONS],
        scratch_shapes=[pltpu.VMEM((ts, d), F32)],
        compiler_params=_params("parallel", "parallel"),
    )(*os_, *lses)


def _att_delta(att, datt, name):
    b, s, gw = att.shape
    ts, d = ATT_TILE, ATT_HEAD_DIM
    n_pat = len(ATT_DILATIONS)

    def body(a_ref, d_ref, *rest):
        do_outs, dl_outs, stage = rest[:n_pat], rest[n_pat:2 * n_pat], rest[-1]
        for j in range(ATT_HEADS_PER_GROUP):
            sl = slice(d * j, d * (j + 1))
            dv = d_ref[:, sl]
            delta = jnp.broadcast_to(jnp.sum(a_ref[:, sl] * dv, axis=-1, keepdims=True), (ts, d))
            for g, r in enumerate(ATT_DILATIONS):
                def store_do(ri, rows, out=do_outs[g]):
                    out[ri, :, sl] = rows.astype(BF16)

                def store_dl(ri, rows, out=dl_outs[g]):
                    out[ri, :, sl] = rows

                _to_residues(dv, stage, r, store_do)
                _to_residues(delta, stage, r, store_dl)

    row = pl.BlockSpec((None, ts, gw), lambda bi, i: (bi, i, 0))
    parts = [_by_residue_spec(r, gw) for r in ATT_DILATIONS]
    outs = pl.pallas_call(
        body, name=name, grid=(b, s // ts), in_specs=[row, row], out_specs=parts * 2,
        out_shape=[jax.ShapeDtypeStruct((b, r, s // r, gw), BF16) for r in ATT_DILATIONS]
        + [jax.ShapeDtypeStruct((b, r, s // r, gw), F32) for r in ATT_DILATIONS],
        scratch_shapes=[pltpu.VMEM((ts, d), F32)],
        compiler_params=_params("parallel", "parallel"),
    )(att, datt)
    return outs[:n_pat], outs[n_pat:]


def _att_bwd_q(qkr, datt, lse, delta, name):
    b, r, l, _ = qkr.shape
    nb = l // ATT_STEP
    d = ATT_HEAD_DIM

    def body(q_ref, kp_ref, k_ref, vp_ref, v_ref, do_ref, lse_ref, dl_ref, dq_ref):
        mask = _band_mask((ATT_STEP, ATT_BLOCK + ATT_STEP), 0, pl.program_id(2) > 0)
        for j in range(ATT_HEADS_PER_GROUP):
            sl = slice(d * j, d * (j + 1))
            kcat = jnp.concatenate([kp_ref[:, sl], k_ref[:, sl]], axis=0)
            vcat = jnp.concatenate([vp_ref[:, sl], v_ref[:, sl]], axis=0)
            sc = _nt(q_ref[:, sl], kcat) * ATT_SCALE
            pr = jnp.exp(jnp.where(mask, sc - lse_ref[:, d * j:d * j + 1], NEG_INF))
            dp = _nt(do_ref[:, sl], vcat)
            dsc = pr * (dp - dl_ref[:, d * j:d * j + 1])
            dq_ref[:, sl] = _nn(dsc, kcat) * ATT_SCALE

    tok = _att_spec(0)
    return pl.pallas_call(
        body, name=name, grid=(b, r, nb),
        in_specs=[_att_spec(0), _att_edge_spec(1, -1, nb), _att_spec(1), _att_edge_spec(2, -1, nb), _att_spec(2),
                  tok, tok, tok],
        out_specs=tok,
        out_shape=jax.ShapeDtypeStruct((b, r, l, ATT_OUT_DIM), F32),
        compiler_params=_params("parallel", "parallel", "parallel"),
    )(qkr, qkr, qkr, qkr, qkr, datt, lse, delta)


def _att_bwd_kv(qkr, datt, lse, delta, name):
    b, r, l, _ = qkr.shape
    nb = l // ATT_STEP
    d = ATT_HEAD_DIM

    def body(k_ref, v_ref, q_ref, qn_ref, do_ref, don_ref, lse_ref, lsen_ref, dl_ref, dln_ref, dk_ref, dv_ref):
        shape = (ATT_STEP, ATT_STEP + ATT_BLOCK)
        kj = lax.broadcasted_iota(jnp.int32, shape, 0)
        qi = lax.broadcasted_iota(jnp.int32, shape, 1)
        dist = qi - kj
        has_next = pl.program_id(2) < nb - 1
        mask = (dist >= 0) & (dist <= ATT_BLOCK) & (has_next | (qi < ATT_STEP))
        for j in range(ATT_HEADS_PER_GROUP):
            sl = slice(d * j, d * (j + 1))
            qcat = jnp.concatenate([q_ref[:, sl], qn_ref[:, sl]], axis=0)
            docat = jnp.concatenate([do_ref[:, sl], don_ref[:, sl]], axis=0)
            lse_t = jnp.tile(jnp.concatenate([lse_ref[:, sl], lsen_ref[:, sl]], axis=0).T, (ATT_STEP // d, 1))
            dl_t = jnp.tile(jnp.concatenate([dl_ref[:, sl], dln_ref[:, sl]], axis=0).T, (ATT_STEP // d, 1))
            sc_t = _nt(k_ref[:, sl], qcat) * ATT_SCALE
            pr_t = jnp.exp(jnp.where(mask, sc_t - lse_t, NEG_INF))
            dv_ref[:, sl] = _nn(pr_t, docat)
            dsc_t = pr_t * (_nt(v_ref[:, sl], docat) - dl_t)
            dk_ref[:, sl] = _nn(dsc_t, qcat) * ATT_SCALE

    tok, tok_n = _att_spec(0), _att_edge_spec(0, 1, nb)
    return pl.pallas_call(
        body, name=name, grid=(b, r, nb),
        in_specs=[_att_spec(1), _att_spec(2), _att_spec(0), _att_edge_spec(0, 1, nb),
                  tok, tok_n, tok, tok_n, tok, tok_n],
        out_specs=[tok, tok],
        out_shape=[jax.ShapeDtypeStruct((b, r, l, ATT_OUT_DIM), F32)] * 2,
        compiler_params=_params("parallel", "parallel", "parallel"),
    )(qkr, qkr, qkr, qkr, datt, datt, lse, lse, delta, delta)


def _mix_fwd(gl, bg, ys, ya, name):
    t, d = ys.shape
    tm = _pick(t, (512, 256, 128))

    def body(gl_ref, bg_ref, ys_ref, ya_ref, o_ref):
        g0 = _sigmoid(gl_ref[:, :d] + bg_ref[:, :d])
        g1 = _sigmoid(gl_ref[:, d:] + bg_ref[:, d:])
        o_ref[...] = (g0 * ys_ref[...] + g1 * ya_ref[...]).astype(BF16)

    row = pl.BlockSpec((tm, d), lambda i: (i, 0))
    return pl.pallas_call(
        body, name=name, grid=(t // tm,),
        in_specs=[pl.BlockSpec((tm, 2 * d), lambda i: (i, 0)), pl.BlockSpec((1, 2 * d), lambda i: (0, 0)), row, row],
        out_specs=row, out_shape=jax.ShapeDtypeStruct((t, d), BF16),
        compiler_params=_params("parallel"),
    )(gl, bg, ys, ya)


def _mix_bwd(gl, bg, ys, ya, dmixed, name):
    t, d = ys.shape
    tm = _pick(t, (512, 256, 128))

    def body(gl_ref, bg_ref, ys_ref, ya_ref, dm_ref, dys_ref, dya_ref, dgl_ref, dbg_ref):
        @pl.when(pl.program_id(0) == 0)
        def _():
            dbg_ref[...] = jnp.zeros_like(dbg_ref)

        dm = dm_ref[...]
        g0 = _sigmoid(gl_ref[:, :d] + bg_ref[:, :d])
        g1 = _sigmoid(gl_ref[:, d:] + bg_ref[:, d:])
        dys_ref[...] = (dm * g0).astype(BF16)
        dya_ref[...] = (dm * g1).astype(BF16)
        d0 = dm * ys_ref[...] * (g0 * (1.0 - g0))
        d1 = dm * ya_ref[...] * (g1 * (1.0 - g1))
        dgl_ref[:, :d] = d0.astype(BF16)
        dgl_ref[:, d:] = d1.astype(BF16)
        dbg_ref[:, :d] += jnp.sum(d0, axis=0, keepdims=True)
        dbg_ref[:, d:] += jnp.sum(d1, axis=0, keepdims=True)

    row = pl.BlockSpec((tm, d), lambda i: (i, 0))
    wide = pl.BlockSpec((tm, 2 * d), lambda i: (i, 0))
    vec = pl.BlockSpec((1, 2 * d), lambda i: (0, 0))
    gate_cols = pl.BlockSpec((tm, 2 * d), lambda i: (i, DPROJ_COLS["gate"] // (2 * d)))
    return pl.pallas_call(
        body, name=name, grid=(t // tm,),
        in_specs=[wide, vec, row, row, row], out_specs=[row, row, gate_cols, vec],
        out_shape=[jax.ShapeDtypeStruct((t, d), BF16), jax.ShapeDtypeStruct((t, d), BF16),
                   jax.ShapeDtypeStruct((t, DPROJ_WIDTH), BF16), jax.ShapeDtypeStruct((1, 2 * d), F32)],
        compiler_params=_params("arbitrary"),
    )(gl, bg, ys, ya, dmixed)


def _swiglu_fwd(gt, up, name):
    t, f = gt.shape
    tm = _pick(t, (512, 256, 128))

    def body(g_ref, u_ref, o_ref):
        gv = g_ref[...]
        o_ref[...] = ((gv * _sigmoid(gv)) * u_ref[...]).astype(BF16)

    row = pl.BlockSpec((tm, f), lambda i: (i, 0))
    return pl.pallas_call(
        body, name=name, grid=(t // tm,), in_specs=[row, row], out_specs=row,
        out_shape=jax.ShapeDtypeStruct((t, f), BF16), compiler_params=_params("parallel"),
    )(gt, up)


def _swiglu_bwd(gt, up, dact, name):
    t, f = gt.shape
    tm = _pick(t, (512, 256, 128))

    def body(g_ref, u_ref, d_ref, dg_ref, du_ref):
        gv = g_ref[...]
        dv = d_ref[...]
        sg = _sigmoid(gv)
        dg_ref[...] = (dv * u_ref[...] * (sg * (1.0 + gv * (1.0 - sg)))).astype(BF16)
        du_ref[...] = (dv * (gv * sg)).astype(BF16)

    row = pl.BlockSpec((tm, f), lambda i: (i, 0))
    return pl.pallas_call(
        body, name=name, grid=(t // tm,), in_specs=[row, row, row], out_specs=[row, row],
        out_shape=[jax.ShapeDtypeStruct((t, f), BF16)] * 2, compiler_params=_params("parallel"),
    )(gt, up, dact)


def _peer(k):
    x, y, c = lax.axis_index("x"), lax.axis_index("y"), lax.axis_index("c")
    px, py, pc = x ^ ((k >> 2) & 1), y ^ ((k >> 1) & 1), c ^ (k & 1)
    return (px, py, pc), 4 * px + 2 * py + pc


def _my_index():
    return 4 * lax.axis_index("x") + 2 * lax.axis_index("y") + lax.axis_index("c")


def _all_gather(parts, name):
    n_parts = len(parts)

    def body(*refs):
        ins, outs = refs[:n_parts], refs[n_parts:2 * n_parts]
        send_sems, recv_sems, local_sems = refs[2 * n_parts:]
        here, me = _peer(0)
        sibling, sib_idx = _peer(1)
        chips = [_peer(2 * q) for q in range(1, N_CHIPS)]

        def copy(i, k, block, to, src=None):
            return pltpu.make_async_remote_copy(
                src_ref=outs[i].at[block] if src is None else src, dst_ref=outs[i].at[block],
                send_sem=send_sems.at[i * (N_DEV - 1) + k], recv_sem=recv_sems.at[i * (N_DEV - 1) + k],
                device_id=to, device_id_type=MESH)

        local = [pltpu.make_async_copy(ins[i], outs[i].at[me], local_sems.at[i]) for i in range(n_parts)]
        for cp in local:
            cp.start()
        sends = []
        for i in range(n_parts):
            sends.append(copy(i, 0, me, sibling, src=ins[i]))
            sends += [copy(i, q, me, chip, src=ins[i]) for q, (chip, _) in enumerate(chips, start=1)]
        for cp in sends:
            cp.start()
        for q, (chip, chip_idx) in enumerate(chips, start=1):
            for i in range(n_parts):
                copy(i, q, chip_idx, here).wait_recv()
                fwd = copy(i, N_CHIPS - 1 + q, chip_idx, sibling)
                fwd.start()
                sends.append(fwd)
        for i in range(n_parts):
            copy(i, 0, sib_idx, here).wait_recv()
        for q, (_, chip_idx) in enumerate(chips, start=1):
            for i in range(n_parts):
                copy(i, N_CHIPS - 1 + q, chip_idx ^ 1, here).wait_recv()
        for cp in sends:
            cp.wait_send()
        for cp in local:
            cp.wait()

    hbm = pl.BlockSpec(memory_space=pl.ANY)
    return pl.pallas_call(
        body, name=name, in_specs=[hbm] * n_parts, out_specs=[hbm] * n_parts,
        out_shape=[jax.ShapeDtypeStruct((N_DEV,) + p_.shape, p_.dtype) for p_ in parts],
        scratch_shapes=[pltpu.SemaphoreType.DMA((n_parts * (N_DEV - 1),)),
                        pltpu.SemaphoreType.DMA((n_parts * (N_DEV - 1),)),
                        pltpu.SemaphoreType.DMA((n_parts,))],
        compiler_params=pltpu.CompilerParams(has_side_effects=True),
    )(*parts)


TILE_ELEMS = 1024 * 1024


def _pair_exchange(slabs, name):
    def body(slab_ref, got_ref, send_sems, recv_sems):
        c = lax.axis_index("c")
        sibling, _ = _peer(1)
        copies = [pltpu.make_async_remote_copy(
            src_ref=slab_ref.at[2 * q + 1 - c], dst_ref=got_ref.at[q], send_sem=send_sems.at[q],
            recv_sem=recv_sems.at[q], device_id=sibling, device_id_type=MESH) for q in range(N_CHIPS)]
        for cp in copies:
            cp.start()
        for cp in copies:
            cp.wait()

    hbm = pl.BlockSpec(memory_space=pl.ANY)
    return pl.pallas_call(
        body, name=name, in_specs=[hbm], out_specs=hbm,
        out_shape=jax.ShapeDtypeStruct((N_CHIPS,) + slabs.shape[1:], slabs.dtype),
        scratch_shapes=[pltpu.SemaphoreType.DMA((N_CHIPS,)), pltpu.SemaphoreType.DMA((N_CHIPS,))],
        compiler_params=pltpu.CompilerParams(has_side_effects=True),
    )(slabs)


def _chip_sum(slabs, got, core, name):
    _, rows, lanes = slabs.shape
    tr = _pick(rows, (TILE_ELEMS // lanes, 512, 256, 128, 64, 32, 16))

    def body(core_ref, mine_ref, got_ref, o_ref):
        o_ref[...] = (mine_ref[...].astype(F32) + got_ref[...].astype(F32)).astype(BF16)

    return pl.pallas_call(
        body, name=name,
        grid_spec=pltpu.PrefetchScalarGridSpec(
            num_scalar_prefetch=1, grid=(N_CHIPS, rows // tr),
            in_specs=[pl.BlockSpec((None, tr, lanes), lambda q, i, core_ref: (2 * q + core_ref[0], i, 0)),
                      pl.BlockSpec((None, tr, lanes), lambda q, i, core_ref: (q, i, 0))],
            out_specs=pl.BlockSpec((None, tr, lanes), lambda q, i, core_ref: (q, i, 0))),
        out_shape=jax.ShapeDtypeStruct((N_CHIPS, rows, lanes), BF16),
        compiler_params=_params("parallel", "parallel"),
    )(core, slabs, got)


def _chip_exchange(chip_sums, shared, name):
    def body(sum_ref, sh_ref, got_ref, gsh_ref, send_sems, recv_sems, sh_send_sems, sh_recv_sems, local_sems):
        me = _my_index()
        my_chip = me >> 1
        local = [pltpu.make_async_copy(sum_ref.at[my_chip], got_ref.at[my_chip], local_sems.at[0]),
                 pltpu.make_async_copy(sh_ref, gsh_ref.at[me], local_sems.at[1])]
        for cp in local:
            cp.start()
        sends = []
        for q in range(1, N_CHIPS):
            peer, pidx = _peer(2 * q)
            cp = pltpu.make_async_remote_copy(
                src_ref=sum_ref.at[pidx >> 1], dst_ref=got_ref.at[my_chip], send_sem=send_sems.at[q - 1],
                recv_sem=recv_sems.at[q - 1], device_id=peer, device_id_type=MESH)
            cp.start()
            sends.append(cp)
        for k in range(1, N_DEV):
            peer, _ = _peer(k)
            cp = pltpu.make_async_remote_copy(
                src_ref=sh_ref, dst_ref=gsh_ref.at[me], send_sem=sh_send_sems.at[k - 1],
                recv_sem=sh_recv_sems.at[k - 1], device_id=peer, device_id_type=MESH)
            cp.start()
            sends.append(cp)
        for q in range(1, N_CHIPS):
            peer, pidx = _peer(2 * q)
            pltpu.make_async_remote_copy(
                src_ref=sum_ref.at[my_chip], dst_ref=got_ref.at[pidx >> 1], send_sem=send_sems.at[q - 1],
                recv_sem=recv_sems.at[q - 1], device_id=peer, device_id_type=MESH).wait_recv()
        for k in range(1, N_DEV):
            peer, pidx = _peer(k)
            pltpu.make_async_remote_copy(
                src_ref=sh_ref, dst_ref=gsh_ref.at[pidx], send_sem=sh_send_sems.at[k - 1],
                recv_sem=sh_recv_sems.at[k - 1], device_id=peer, device_id_type=MESH).wait_recv()
        for cp in sends:
            cp.wait_send()
        for cp in local:
            cp.wait()

    hbm = pl.BlockSpec(memory_space=pl.ANY)
    return pl.pallas_call(
        body, name=name, in_specs=[hbm, hbm], out_specs=[hbm, hbm],
        out_shape=[jax.ShapeDtypeStruct(chip_sums.shape, chip_sums.dtype),
                   jax.ShapeDtypeStruct((N_DEV,) + shared.shape, shared.dtype)],
        scratch_shapes=[pltpu.SemaphoreType.DMA((N_CHIPS - 1,)), pltpu.SemaphoreType.DMA((N_CHIPS - 1,)),
                        pltpu.SemaphoreType.DMA((N_DEV - 1,)), pltpu.SemaphoreType.DMA((N_DEV - 1,)),
                        pltpu.SemaphoreType.DMA((2,))],
        compiler_params=pltpu.CompilerParams(has_side_effects=True),
    )(chip_sums, shared)


def _adamw(parts, w, m, v, name):
    n_parts, rows, lanes = parts.shape
    tr = rows if rows * lanes <= TILE_ELEMS // 2 else _pick(rows, (TILE_ELEMS // 4 // lanes, 128, 64, 32, 16, 8))
    c1 = 1.0 - ADAM_B1 ** ADAM_STEP
    c2 = 1.0 - ADAM_B2 ** ADAM_STEP

    def body(p_ref, w_ref, m_ref, v_ref, g_ref, d_ref, nm_ref, nv_ref):
        g = p_ref[0].astype(F32)
        for j in range(1, n_parts):
            g = g + p_ref[j].astype(F32)
        nm = ADAM_B1 * m_ref[...] + (1.0 - ADAM_B1) * g
        nv = ADAM_B2 * v_ref[...] + (1.0 - ADAM_B2) * (g * g)
        g_ref[...] = g
        nm_ref[...] = nm
        nv_ref[...] = nv
        d_ref[...] = -ADAM_LR * ((nm / c1) / (jnp.sqrt(nv / c2) + ADAM_EPS) + ADAM_WD * w_ref[...])

    row = pl.BlockSpec((tr, lanes), lambda i: (i, 0))
    return pl.pallas_call(
        body, name=name, grid=(rows // tr,),
        in_specs=[pl.BlockSpec((n_parts, tr, lanes), lambda i: (0, i, 0)), row, row, row],
        out_specs=[row] * 4, out_shape=[jax.ShapeDtypeStruct((rows, lanes), F32)] * 4,
        compiler_params=_params("parallel"),
    )(parts, w, m, v)


MATRIX_SHARDS = (
    ("w_in", (D_MODEL, IN_PROJ_DIM // N_DEV), True),
    ("w_ssm_out", (SSM_D_INNER // N_DEV, D_MODEL), False),
    ("w_att_out", (ATT_OUT_DIM, D_MODEL // N_DEV), True),
    ("w_mix_out", (D_MODEL // N_DEV, D_MODEL), False),
    ("w_ffn_gate", (D_MODEL, D_FF // N_DEV), True),
    ("w_ffn_up", (D_MODEL, D_FF // N_DEV), True),
    ("w_ffn_down", (D_FF // N_DEV, D_MODEL), False),
)
CONV_SHARD = ("conv_w", (SSM_CONV, SSM_CONV_DIM // N_DEV), True)
SHARDED = MATRIX_SHARDS + (CONV_SHARD,)
REPLICATED = (("norm_mix", D_MODEL), ("b_gate", 2 * D_MODEL), ("conv_b", SSM_CONV_DIM), ("dt_bias", SSM_N_HEADS),
              ("a_log", SSM_N_HEADS), ("d_skip", SSM_N_HEADS), ("ssm_norm", SSM_D_INNER), ("norm_ffn", D_MODEL),
              ("norm_final", D_MODEL))


def _round_up(n, mult):
    return -(-n // mult) * mult


def _pack_rows(flat, row_mult):
    rows = _round_up(-(-flat.shape[0] // LANES), row_mult)
    return jnp.pad(flat, (0, rows * LANES - flat.shape[0])).reshape(rows, LANES)


def _stacking(specs):
    return tuple((name, (shape[1], shape[0]) if by_cols else shape, by_cols) for name, shape, by_cols in specs)


def _to_stacking(vals, specs):
    return {name: (vals[name].T if by_cols else vals[name]) for name, _, by_cols in specs}


STACK_WIDTH = D_MODEL
STACK_ALIGN = 16
STACK_ORDER = ("w_ssm_out", "w_mix_out", "w_ffn_gate", "w_ffn_up", "w_ffn_down", "w_att_out", "conv_w", "w_in")


def _stack_layout():
    shapes = {name: shape for name, shape, _ in _stacking(SHARDED)}
    layout, off = {}, 0
    for name in STACK_ORDER:
        r, c = shapes[name]
        rows = r if c == STACK_WIDTH else _round_up(-(-(r * c) // STACK_WIDTH), STACK_ALIGN)
        layout[name] = (off, rows, (r, c))
        off = _round_up(off + rows, STACK_ALIGN)
    return layout, _round_up(off, 1024)


def _to_stack_rows(v, rows):
    if v.shape[-1] == STACK_WIDTH:
        return v
    lead = v.shape[:-2]
    flat = v.reshape(lead + (-1,))
    flat = jnp.pad(flat, [(0, 0)] * len(lead) + [(0, rows * STACK_WIDTH - flat.shape[-1])])
    return flat.reshape(lead + (rows, STACK_WIDTH))


def _from_stack_rows(block, shape):
    r, c = shape
    if c == STACK_WIDTH:
        return block
    lead = block.shape[:-2]
    return block.reshape(lead + (-1,))[..., :r * c].reshape(lead + (r, c))


def _stack(vals, dtype, skip=()):
    layout, total = _stack_layout()
    lead = next(iter(vals.values())).shape[:-2]
    pieces, at = [], 0
    for name in STACK_ORDER:
        off, rows, _ = layout[name]
        pieces.append(jnp.zeros(lead + (off - at, STACK_WIDTH), dtype))
        piece = jnp.zeros(lead + (rows, STACK_WIDTH), dtype) if name in skip else _to_stack_rows(vals[name], rows)
        pieces.append(piece.astype(dtype))
        at = off + rows
    pieces.append(jnp.zeros(lead + (total - at, STACK_WIDTH), dtype))
    return jnp.concatenate([p_ for p_ in pieces if p_.shape[-2]], axis=-2)


def _unstack(stacked, names):
    layout, _ = _stack_layout()
    return {name: _from_stack_rows(stacked[..., layout[name][0]:layout[name][0] + layout[name][1], :], layout[name][2])
            for name in names}


W_IN_SHARD_ROWS = IN_PROJ_DIM // N_DEV


def _w_in_row_moves():
    moves, orig = [], 0
    for name, size in IN_SPLIT:
        for j in range(N_DEV):
            lo, hi = max(orig, W_IN_SHARD_ROWS * j), min(orig + size, W_IN_SHARD_ROWS * (j + 1))
            if lo < hi:
                moves.append((j, lo - W_IN_SHARD_ROWS * j, DPROJ_COLS[name] + lo - orig, hi - lo))
        orig += size
    return moves


def _w_in_from_shards(stacked_all, name):
    layout, total = _stack_layout()
    base = layout["w_in"][0]
    pad_lo, pad_hi = DPROJ_COLS["dt"] + _round_up(SSM_N_HEADS, STACK_ALIGN), DPROJ_COLS["dt"] + DPROJ_DT_WIDTH

    def body(x_ref, o_ref):
        o_ref[pad_lo:pad_hi, :] = jnp.zeros((pad_hi - pad_lo, LANES), x_ref.dtype)
        for j, r, at, n in _w_in_row_moves():
            o_ref[at:at + n, :] = x_ref[j, base + r:base + r + n, :]

    return pl.pallas_call(
        body, name=name, grid=(STACK_WIDTH // LANES,),
        in_specs=[pl.BlockSpec((N_DEV, total, LANES), lambda c: (0, 0, c))],
        out_specs=pl.BlockSpec((DPROJ_WIDTH, LANES), lambda c: (0, c)),
        out_shape=jax.ShapeDtypeStruct((DPROJ_WIDTH, STACK_WIDTH), stacked_all.dtype),
        compiler_params=_params("parallel"),
    )(stacked_all)


def _w_in_to_shards(dw_all, name):
    def body(x_ref, o_ref):
        for j, r, at, n in _w_in_row_moves():
            o_ref[j, r:r + n, :] = x_ref[at:at + n, :]

    return pl.pallas_call(
        body, name=name, grid=(STACK_WIDTH // LANES,),
        in_specs=[pl.BlockSpec((DPROJ_WIDTH, LANES), lambda c: (0, c))],
        out_specs=pl.BlockSpec((N_DEV, W_IN_SHARD_ROWS, LANES), lambda c: (0, 0, c)),
        out_shape=jax.ShapeDtypeStruct((N_DEV, W_IN_SHARD_ROWS, STACK_WIDTH), dw_all.dtype),
        compiler_params=_params("parallel"),
    )(dw_all)


REPLICATED_ROWS = sum(-(-size // LANES) for _, size in REPLICATED)
LOSS_ROW = REPLICATED_ROWS


def _pack_replicated(vals):
    rows = []
    for name, size in REPLICATED:
        v = vals[name].reshape(-1).astype(F32)
        rows.append(jnp.pad(v, (0, _round_up(size, LANES) - size)))
    return _pack_rows(jnp.concatenate(rows), 8)


def _unpack_replicated(packed, shapes):
    flat = packed.reshape(-1)
    out, off = {}, 0
    for name, size in REPLICATED:
        out[name] = flat[off:off + size].reshape(shapes[name])
        off += _round_up(size, LANES)
    return out


def _lane_row(v):
    v = v.reshape(-1).astype(F32)
    return jnp.pad(v, (0, LANES - v.shape[0])).reshape(1, LANES)


IN_SPLIT = (("z", SSM_D_INNER), ("xbc", SSM_CONV_DIM), ("dt", SSM_N_HEADS), ("qkv", ATT_QKV_DIM), ("gate", 2 * D_MODEL))


def kernel(x, norm_mix, w_in, b_gate, conv_w, conv_b, dt_bias, a_log, d_skip, ssm_norm, w_ssm_out, w_att_out, w_mix_out, norm_ffn, w_ffn_gate, w_ffn_up, w_ffn_down, norm_final, loss_target, m_norm_mix, m_w_in, m_b_gate, m_conv_w, m_conv_b, m_dt_bias, m_a_log, m_d_skip, m_ssm_norm, m_w_ssm_out, m_w_att_out, m_w_mix_out, m_norm_ffn, m_w_ffn_gate, m_w_ffn_up, m_w_ffn_down, m_norm_final, v_norm_mix, v_w_in, v_b_gate, v_conv_w, v_conv_b, v_dt_bias, v_a_log, v_d_skip, v_ssm_norm, v_w_ssm_out, v_w_att_out, v_w_mix_out, v_norm_ffn, v_w_ffn_gate, v_w_ffn_up, v_w_ffn_down, v_norm_final):
    given = dict(locals())
    weights = {name: given[name][0] for name, _, _ in SHARDED}
    b, s, d = x.shape
    t = b * s

    stacking = _to_stacking(weights, SHARDED)
    conv_shape = dict((name, shape) for name, shape, _ in _stacking(SHARDED))["conv_w"]
    mat_local = _stack(stacking, BF16, skip=("conv_w",))
    conv_local = _pack_rows(stacking["conv_w"].reshape(-1), 8)
    mat_all, conv_all = _all_gather([mat_local, conv_local], "weights_all_gather")
    full = {name: v.reshape((-1,) + v.shape[2:]) for name, v in _unstack(mat_all, STACK_ORDER[:-2]).items()}
    w_in_all = _w_in_from_shards(mat_all, "w_in_from_shards")
    w_sec = {name: w_in_all[DPROJ_COLS[name]:DPROJ_COLS[name] + _round_up(size, LANES)] for name, size in IN_SPLIT}
    conv_size = conv_shape[0] * conv_shape[1]
    conv_taps = conv_all.reshape(N_DEV, -1)[:, :conv_size].reshape(N_DEV * conv_shape[0], conv_shape[1]).T

    g_mix, g_ffn, g_fin = norm_mix.reshape(1, d), norm_ffn.reshape(1, d), norm_final.reshape(1, d)
    bg_row = b_gate.reshape(1, 2 * d)
    convb_row = conv_b.reshape(1, SSM_CONV_DIM)
    ssmn_row = ssm_norm.reshape(1, SSM_D_INNER)
    dtb_row, alog_row = _lane_row(dt_bias), _lane_row(a_log)
    cosf, sinf = _rope_tables(s)

    x2d = x.reshape(t, d)
    h1 = _rmsnorm_fwd(x2d, g_mix, "norm_mix_fwd")
    proj = {name: _mm(h1, w_sec[name], mode="nt", name="in_proj_" + name) for name, _ in IN_SPLIT}
    xbc3 = proj["xbc"].reshape(b, s, SSM_CONV_DIM)
    xc = _conv_fwd(xbc3, conv_taps, convb_row, "conv_fwd")
    dtr3 = proj["dt"].reshape(b, s, DT_PAD)
    to_channels, to_heads = _head_masks()
    dskx = jnp.repeat(d_skip.reshape(-1).astype(F32), SSM_HEAD_DIM).reshape(1, SSM_D_INNER)
    y_ssd, h_states = _ssd_fwd(xc, dtr3, dtb_row, alog_row, dskx, to_channels, "ssd_fwd")
    y_ssd2 = y_ssd.reshape(t, SSM_D_INNER)
    ynorm = _gate_norm_fwd(y_ssd2, proj["z"], ssmn_row, "ssd_gate_norm_fwd")
    y_ssm = _mm(ynorm, full["w_ssm_out"], mode="nn", name="ssm_out_proj")

    qkv3 = proj["qkv"].reshape(b, s, ATT_QKV_DIM)
    qk_parts = _rope_fwd(qkv3, cosf, sinf, "rope_fwd")
    att_parts = [_att_fwd(qk_parts[gi], "att_fwd_%d" % r) for gi, r in enumerate(ATT_DILATIONS)]
    att, *lse_parts = _att_merge([o for o, _ in att_parts], [l_ for _, l_ in att_parts], "att_merge")
    att2 = att.reshape(t, ATT_OUT_DIM)
    y_att = _mm(att2, full["w_att_out"], mode="nt", name="att_out_proj")

    mixed = _mix_fwd(proj["gate"], bg_row, y_ssm, y_att, "mix_fwd")
    x2 = _mm(mixed, full["w_mix_out"], mode="nn", name="mix_out_proj", add=x2d)
    h2 = _rmsnorm_fwd(x2, g_ffn, "norm_ffn_fwd")
    gt = _mm(h2, full["w_ffn_gate"], mode="nt", name="ffn_gate_proj")
    up = _mm(h2, full["w_ffn_up"], mode="nt", name="ffn_up_proj")
    act = _swiglu_fwd(gt, up, "swiglu_fwd")
    x3 = _mm(act, full["w_ffn_down"], mode="nn", name="ffn_down_proj", add=x2)

    loss_row, dx3, dg_fin, dx3b = _loss_head(x3, g_fin, loss_target.reshape(t, d), "loss_head")
    grads = {}
    dact = _mm(dx3b, full["w_ffn_down"], mode="nt", name="ffn_down_dx")
    grads["w_ffn_down"] = _mm(act, dx3b, mode="tn", name="ffn_down_dw", out_dtype=BF16)
    dgt, dup = _swiglu_bwd(gt, up, dact, "swiglu_bwd")
    grads["w_ffn_gate"] = _mm(dgt, h2, mode="tn", name="ffn_gate_dw", out_dtype=BF16)
    grads["w_ffn_up"] = _mm(dup, h2, mode="tn", name="ffn_up_dw", out_dtype=BF16)
    dh2 = _mm(dgt, full["w_ffn_gate"], mode="nn", name="ffn_gate_dx")
    dh2 = _mm(dup, full["w_ffn_up"], mode="nn", name="ffn_up_dx", add=dh2)
    dx2, dg_ffn, dx2b = _rmsnorm_bwd(x2, g_ffn, dh2, dx3, "norm_ffn_bwd", with_bf16=True)

    dmixed = _mm(dx2b, full["w_mix_out"], mode="nt", name="mix_out_dx")
    grads["w_mix_out"] = _mm(mixed, dx2b, mode="tn", name="mix_out_dw", out_dtype=BF16)
    dys, dya, dproj, dbg = _mix_bwd(proj["gate"], bg_row, y_ssm, y_att, dmixed, "mix_bwd")

    grads["w_ssm_out"] = _mm(ynorm, dys, mode="tn", name="ssm_out_dw", out_dtype=BF16)
    dynorm = _mm(dys, full["w_ssm_out"], mode="nt", name="ssm_out_dx")
    dy_ssd, dproj, dssmn = _gate_norm_bwd(y_ssd2, proj["z"], ssmn_row, dynorm, dproj, "ssd_gate_norm_bwd")
    dxc, dproj, dalog, ddsk, ddtb = _ssd_bwd(xc, dtr3, dy_ssd.reshape(b, s, SSM_D_INNER), h_states, dtb_row, alog_row,
                                             dskx, to_channels, to_heads, dproj.reshape(b, s, DPROJ_WIDTH), "ssd_bwd")
    dproj, dconvw, dconvb = _conv_bwd(xbc3, dxc, conv_taps, convb_row, dproj, "conv_bwd")
    grads["conv_w"] = dconvw.T.astype(BF16)

    grads["w_att_out"] = _mm(dya, att2, mode="tn", name="att_out_dw", out_dtype=BF16)
    datt = _mm(dya, full["w_att_out"], mode="nn", name="att_out_dx").reshape(b, s, ATT_OUT_DIM)
    do_parts, dl_parts = _att_delta(att, datt, "att_delta")
    dqs, dks, dvs = [], [], []
    for gi, r in enumerate(ATT_DILATIONS):
        operands = (qk_parts[gi], do_parts[gi], lse_parts[gi], dl_parts[gi])
        dqs.append(_att_bwd_q(*operands, "att_bwd_q_%d" % r))
        dk_g, dv_g = _att_bwd_kv(*operands, "att_bwd_kv_%d" % r)
        dks.append(dk_g)
        dvs.append(dv_g)
    dproj = _rope_bwd(dqs, dks, dvs, cosf, sinf, dproj, "rope_bwd").reshape(t, DPROJ_WIDTH)

    dw_all = _mm(dproj, h1, mode="tn", name="in_proj_dw", out_dtype=BF16)
    grads["w_in"] = _w_in_to_shards(dw_all, "w_in_to_shards")
    dh1 = _mm(dproj, w_in_all, mode="nn", name="in_proj_dx")
    grad_x, dg_mix = _rmsnorm_bwd(x2d, g_mix, dh1, dx2, "norm_mix_bwd")

    slabs = _stack({name: grads[name].reshape((N_DEV, -1, grads[name].shape[-1])) for name in STACK_ORDER}, BF16)
    small = {"norm_mix": dg_mix, "b_gate": dbg, "conv_b": dconvb, "dt_bias": ddtb[:, :SSM_N_HEADS],
             "a_log": dalog[:, :SSM_N_HEADS], "d_skip": ddsk[:, :SSM_N_HEADS], "ssm_norm": dssmn,
             "norm_ffn": dg_ffn, "norm_final": dg_fin}
    core = lax.axis_index("c").astype(jnp.int32).reshape(1)
    chip_sums = _chip_sum(slabs, _pair_exchange(slabs, "grad_pair_exchange"), core, "grad_chip_sum")
    shared = _pack_replicated(small)
    shared = shared.at[LOSS_ROW, 0].set(loss_row[0, 0])
    got, got_small = _chip_exchange(chip_sums, shared, "grad_chip_exchange")

    def packed(prefix):
        vals = _to_stacking({name: given[prefix + name][0] for name, _, _ in SHARDED}, SHARDED)
        rep = {name: given[prefix + name] for name, _ in REPLICATED}
        return _stack(vals, F32), _pack_replicated(rep)

    (w_big, w_small), (m_big, m_small), (v_big, v_small) = packed(""), packed("m_"), packed("v_")
    big = _adamw(got, w_big, m_big, v_big, "adamw_sharded")
    sml = _adamw(got_small, w_small, m_small, v_small, "adamw_replicated")

    outs = [sml[0][LOSS_ROW, 0], grad_x.reshape(b, s, d)]
    rep_shapes = {name: given[name].shape for name, _ in REPLICATED}
    order = ["norm_mix", "w_in", "b_gate", "conv_w", "conv_b", "dt_bias", "a_log", "d_skip", "ssm_norm", "w_ssm_out",
             "w_att_out", "w_mix_out", "norm_ffn", "w_ffn_gate", "w_ffn_up", "w_ffn_down", "norm_final"]
    for big_k, sml_k in zip(big, sml):
        sharded = _to_stacking(_unstack(big_k, STACK_ORDER), SHARDED)
        rep = _unpack_replicated(sml_k, rep_shapes)
        for name in order:
            outs.append(sharded[name][None] if name in sharded else rep[name])
    return tuple(outs)
```

```python
import functools
import math

import jax
import jax.numpy as jnp
from jax import lax
from jax.experimental import pallas as pl
from jax.experimental.pallas import tpu as pltpu

F32 = jnp.float32
BF16 = jnp.bfloat16

N_DEV = 8
N_CHIPS = 4
D_MODEL = 1024
SSM_D_INNER = 2048
SSM_HEAD_DIM = 64
SSM_N_HEADS = 32
SSM_N_GROUPS = 4
SSM_HEADS_PER_GROUP = SSM_N_HEADS // SSM_N_GROUPS
SSM_D_STATE = 128
SSM_CONV = 4
SSM_CHUNK = 128
SSM_CONV_DIM = 3072
ATT_HEAD_DIM = 128
ATT_HEADS_PER_GROUP = 4
ATT_DILATIONS = (1, 4, 16)
ATT_N_HEADS = 12
ATT_QKV_DIM = 4608
ATT_OUT_DIM = 512
ATT_BLOCK = 128
ROPE_THETA = 10000.0
D_FF = 2816
IN_PROJ_DIM = 11808
EPS = 1e-6
LANES = 128
DT_PAD = LANES

DPROJ_COLS = {"qkv": 0, "z": 4608, "xbc": 6656, "dt": 9728, "gate": 10240}
DPROJ_DT_WIDTH = 512
DPROJ_WIDTH = 12288

ADAM_LR = 0.001
ADAM_B1 = 0.9
ADAM_B2 = 0.999
ADAM_EPS = 1e-08
ADAM_WD = 0.01
ADAM_STEP = 10

VMEM_LIMIT = 56 * 1024 * 1024
MESH = pl.DeviceIdType.MESH
NEG_INF = float("-inf")


def _pick(n, candidates):
    for c in candidates:
        if n % c == 0:
            return c
    return n


def _params(*sem):
    return pltpu.CompilerParams(dimension_semantics=sem, vmem_limit_bytes=VMEM_LIMIT)


def _sigmoid(x):
    return 1.0 / (1.0 + jnp.exp(-x))


def _softplus(x):
    return jnp.maximum(x, 0.0) + jnp.log(1.0 + jnp.exp(-jnp.abs(x)))


def _dot(a, b, dims):
    return lax.dot_general(a.astype(BF16), b.astype(BF16), (dims, ((), ())), preferred_element_type=F32)


def _nn(a, b):
    return _dot(a, b, ((1,), (0,)))


def _nt(a, b):
    return _dot(a, b, ((1,), (1,)))


def _tn(a, b):
    return _dot(a, b, ((0,), (0,)))


def _split3(v):
    hi = v.astype(BF16)
    r1 = v - hi.astype(F32)
    mid = r1.astype(BF16)
    lo = (r1 - mid.astype(F32)).astype(BF16)
    return hi, mid, lo


def _mask_nn(mask, v):
    mb = mask.astype(BF16)
    hi, mid, lo = _split3(v)
    return _nn(mb, hi) + (_nn(mb, mid) + _nn(mb, lo))


MM_VMEM_BUDGET = 40 * 1024 * 1024
MM_FULL_K = 2816


def _mm_tiles(m, n, k, a_bytes, b_bytes, o_bytes, has_add):
    tk = k if k <= MM_FULL_K else _pick(k, (2048, 1024, 512, 256, 128))
    tn = 1408 if (n > 1024 and n % 1408 == 0) else _pick(n, (1024, 768, 512, 384, 256, 128))
    for tm in (1408, 1024, 768, 512, 384, 256, 128):
        if m % tm:
            continue
        buffers = 2 * (tm * tk * a_bytes + tk * tn * b_bytes + tm * tn * (o_bytes + (4 if has_add else 0)))
        if tk < k:
            buffers += tm * tn * 4
        if buffers <= MM_VMEM_BUDGET:
            return tm, tn, tk
    return _pick(m, (128,)), tn, tk


def _mm(a, b, *, mode, name, out_dtype=F32, add=None):
    if mode == "nn":
        (m, k), n = a.shape, b.shape[1]
    elif mode == "nt":
        (m, k), n = a.shape, b.shape[0]
    else:
        (k, m), n = a.shape, b.shape[1]
    has_add = add is not None
    tm, tn, tk = _mm_tiles(m, n, k, a.dtype.itemsize, b.dtype.itemsize, jnp.dtype(out_dtype).itemsize, has_add)
    nk = k // tk
    dims = {"nn": ((1,), (0,)), "nt": ((1,), (1,)), "tn": ((0,), (0,))}[mode]
    a_spec = {"nn": pl.BlockSpec((tm, tk), lambda i, j, kk: (i, kk)),
              "nt": pl.BlockSpec((tm, tk), lambda i, j, kk: (i, kk)),
              "tn": pl.BlockSpec((tk, tm), lambda i, j, kk: (kk, i))}[mode]
    b_spec = {"nn": pl.BlockSpec((tk, tn), lambda i, j, kk: (kk, j)),
              "nt": pl.BlockSpec((tn, tk), lambda i, j, kk: (j, kk)),
              "tn": pl.BlockSpec((tk, tn), lambda i, j, kk: (kk, j))}[mode]
    o_spec = pl.BlockSpec((tm, tn), lambda i, j, kk: (i, j))

    def finish(r, c_ref, o_ref):
        if has_add:
            r = r + c_ref[...]
        o_ref[...] = r.astype(out_dtype)

    def body_one(*refs):
        a_ref, b_ref = refs[:2]
        finish(_dot(a_ref[...], b_ref[...], dims), refs[2] if has_add else None, refs[-1])

    def body_acc(*refs):
        a_ref, b_ref = refs[:2]
        o_ref, acc = refs[-2:]
        kk = pl.program_id(2)

        @pl.when(kk == 0)
        def _():
            acc[...] = jnp.zeros_like(acc)

        acc[...] += _dot(a_ref[...], b_ref[...], dims)

        @pl.when(kk == nk - 1)
        def _():
            finish(acc[...], refs[2] if has_add else None, o_ref)

    in_specs = [a_spec, b_spec] + ([o_spec] if has_add else [])
    args = (a, b) + ((add,) if has_add else ())
    return pl.pallas_call(
        body_one if nk == 1 else body_acc, name=name, grid=(m // tm, n // tn, nk),
        in_specs=in_specs, out_specs=o_spec,
        out_shape=jax.ShapeDtypeStruct((m, n), out_dtype),
        scratch_shapes=[] if nk == 1 else [pltpu.VMEM((tm, tn), F32)],
        compiler_params=_params("parallel", "parallel", "arbitrary"),
    )(*args)


def _rmsnorm_fwd(x, g, name):
    t, d = x.shape
    tm = _pick(t, (512, 256, 128))

    def body(x_ref, g_ref, o_ref):
        xv = x_ref[...]
        r = lax.rsqrt(jnp.mean(xv * xv, axis=-1, keepdims=True) + EPS)
        o_ref[...] = ((xv * r) * g_ref[...]).astype(BF16)

    return pl.pallas_call(
        body, name=name, grid=(t // tm,),
        in_specs=[pl.BlockSpec((tm, d), lambda i: (i, 0)), pl.BlockSpec((1, d), lambda i: (0, 0))],
        out_specs=pl.BlockSpec((tm, d), lambda i: (i, 0)),
        out_shape=jax.ShapeDtypeStruct((t, d), BF16),
        compiler_params=_params("parallel"),
    )(x, g)


def _rmsnorm_bwd(x, g, dh, dres, name, with_bf16=False):
    t, d = x.shape
    tm = _pick(t, (512, 256, 128))

    def body(x_ref, g_ref, dh_ref, dres_ref, dx_ref, dg_ref, *dxb_ref):
        @pl.when(pl.program_id(0) == 0)
        def _():
            dg_ref[...] = jnp.zeros_like(dg_ref)

        xv = x_ref[...]
        r = lax.rsqrt(jnp.mean(xv * xv, axis=-1, keepdims=True) + EPS)
        xhat = xv * r
        dhv = dh_ref[...]
        dyg = dhv * g_ref[...]
        dx = dres_ref[...] + r * (dyg - xhat * jnp.mean(dyg * xhat, axis=-1, keepdims=True))
        dx_ref[...] = dx
        if with_bf16:
            dxb_ref[0][...] = dx.astype(BF16)
        dg_ref[...] += jnp.sum(dhv * xhat, axis=0, keepdims=True)

    row = pl.BlockSpec((tm, d), lambda i: (i, 0))
    vec = pl.BlockSpec((1, d), lambda i: (0, 0))
    extra = with_bf16 * [jax.ShapeDtypeStruct((t, d), BF16)]
    return pl.pallas_call(
        body, name=name, grid=(t // tm,),
        in_specs=[row, vec, row, row], out_specs=[row, vec] + with_bf16 * [row],
        out_shape=[jax.ShapeDtypeStruct((t, d), F32), jax.ShapeDtypeStruct((1, d), F32)] + extra,
        compiler_params=_params("arbitrary"),
    )(x, g, dh, dres)


def _loss_head(x, g, target, name):
    t, d = x.shape
    tm = _pick(t, (512, 256, 128))

    def body(x_ref, g_ref, t_ref, loss_ref, dx_ref, dg_ref, dxb_ref):
        @pl.when(pl.program_id(0) == 0)
        def _():
            dg_ref[...] = jnp.zeros_like(dg_ref)
            loss_ref[...] = jnp.zeros_like(loss_ref)

        xv = x_ref[...]
        gv = g_ref[...]
        r = lax.rsqrt(jnp.mean(xv * xv, axis=-1, keepdims=True) + EPS)
        xhat = xv * r
        err = xhat * gv - t_ref[...]
        loss_ref[...] += jnp.sum(err * err) * (0.5 / d)
        dy = err * (1.0 / d)
        dyg = dy * gv
        dx = r * (dyg - xhat * jnp.mean(dyg * xhat, axis=-1, keepdims=True))
        dx_ref[...] = dx
        dxb_ref[...] = dx.astype(BF16)
        dg_ref[...] += jnp.sum(dy * xhat, axis=0, keepdims=True)

    row = pl.BlockSpec((tm, d), lambda i: (i, 0))
    vec = pl.BlockSpec((1, d), lambda i: (0, 0))
    return pl.pallas_call(
        body, name=name, grid=(t // tm,),
        in_specs=[row, vec, row],
        out_specs=[pl.BlockSpec((1, LANES), lambda i: (0, 0)), row, vec, row],
        out_shape=[jax.ShapeDtypeStruct((1, LANES), F32), jax.ShapeDtypeStruct((t, d), F32),
                   jax.ShapeDtypeStruct((1, d), F32), jax.ShapeDtypeStruct((t, d), BF16)],
        compiler_params=_params("arbitrary"),
    )(x, g, target)


CONV_HALO = 8
CONV_ROWS = 64


def _conv_taps(window, wv, bv):
    acc = bv + wv[SSM_CONV - 1:SSM_CONV, :] * window(0)
    for sh in range(1, SSM_CONV):
        kidx = SSM_CONV - 1 - sh
        acc = acc + wv[kidx:kidx + 1, :] * window(sh)
    return acc


def _conv_fwd(u, w, bias, name):
    b, s, c = u.shape
    rows = CONV_ROWS

    def body(u_ref, w_ref, b_ref, o_ref, ext):
        ext[0:CONV_HALO, :] = jnp.zeros((CONV_HALO, LANES), F32)
        ext[CONV_HALO:, :] = u_ref[...]
        wv, bv = w_ref[...], b_ref[...]
        for r0 in range(0, s, rows):
            acc = _conv_taps(lambda sh: ext[CONV_HALO + r0 - sh:CONV_HALO + r0 - sh + rows, :], wv, bv)
            o_ref[r0:r0 + rows, :] = acc * _sigmoid(acc)

    strip = pl.BlockSpec((None, s, LANES), lambda bi, j: (bi, 0, j))
    return pl.pallas_call(
        body, name=name, grid=(b, c // LANES),
        in_specs=[strip, pl.BlockSpec((SSM_CONV, LANES), lambda bi, j: (0, j)),
                  pl.BlockSpec((1, LANES), lambda bi, j: (0, j))],
        out_specs=strip, out_shape=jax.ShapeDtypeStruct((b, s, c), F32),
        scratch_shapes=[pltpu.VMEM((CONV_HALO + s, LANES), F32)],
        compiler_params=_params("parallel", "parallel"),
    )(u, w, bias)


def _conv_bwd(u, dout, w, bias, dproj, name):
    b, s, c = u.shape
    rows = CONV_ROWS

    def fold(v):
        return jnp.sum(v.reshape(rows // CONV_HALO, CONV_HALO, LANES), axis=0)

    def body(u_ref, d_ref, w_ref, b_ref, buf_ref, du_ref, dw_ref, db_ref, ext, dpre):
        @pl.when(pl.program_id(1) == 0)
        def _():
            dw_ref[...] = jnp.zeros_like(dw_ref)
            db_ref[...] = jnp.zeros_like(db_ref)

        ext[0:CONV_HALO, :] = jnp.zeros((CONV_HALO, LANES), F32)
        ext[CONV_HALO:, :] = u_ref[...]
        dpre[s:, :] = jnp.zeros((CONV_HALO, LANES), F32)
        wv, bv = w_ref[...], b_ref[...]
        sums = [jnp.zeros((CONV_HALO, LANES), F32)] * (SSM_CONV + 1)
        for r0 in range(0, s, rows):
            window = lambda sh: ext[CONV_HALO + r0 - sh:CONV_HALO + r0 - sh + rows, :]
            acc = _conv_taps(window, wv, bv)
            sg = _sigmoid(acc)
            dp = d_ref[r0:r0 + rows, :] * (sg * (1.0 + acc * (1.0 - sg)))
            dpre[r0:r0 + rows, :] = dp
            taps = [sums[SSM_CONV - 1 - sh] + fold(dp * window(sh)) for sh in range(SSM_CONV)]
            sums = taps[::-1] + [sums[SSM_CONV] + fold(dp)]
        for r0 in range(0, s, rows):
            du = wv[SSM_CONV - 1:SSM_CONV, :] * dpre[r0:r0 + rows, :]
            for sh in range(1, SSM_CONV):
                kidx = SSM_CONV - 1 - sh
                du = du + wv[kidx:kidx + 1, :] * dpre[r0 + sh:r0 + sh + rows, :]
            du_ref[r0:r0 + rows, :] = du.astype(BF16)
        for kidx in range(SSM_CONV):
            dw_ref[kidx:kidx + 1, :] += jnp.sum(sums[kidx], axis=0, keepdims=True)
        db_ref[...] += jnp.sum(sums[SSM_CONV], axis=0, keepdims=True)

    strip = pl.BlockSpec((None, s, LANES), lambda j, bi: (bi, 0, j))
    taps = pl.BlockSpec((SSM_CONV, LANES), lambda j, bi: (0, j))
    vec = pl.BlockSpec((1, LANES), lambda j, bi: (0, j))
    du_cols = pl.BlockSpec((None, s, LANES), lambda j, bi: (bi, 0, DPROJ_COLS["xbc"] // LANES + j))
    return pl.pallas_call(
        body, name=name, grid=(c // LANES, b),
        in_specs=[strip, strip, taps, vec, pl.BlockSpec(memory_space=pl.ANY)], out_specs=[du_cols, taps, vec],
        input_output_aliases={4: 0},
        out_shape=[jax.ShapeDtypeStruct(dproj.shape, dproj.dtype), jax.ShapeDtypeStruct((SSM_CONV, c), F32),
                   jax.ShapeDtypeStruct((1, c), F32)],
        scratch_shapes=[pltpu.VMEM((CONV_HALO + s, LANES), F32), pltpu.VMEM((s + CONV_HALO, LANES), F32)],
        compiler_params=_params("parallel", "arbitrary"),
    )(u, dout, w, bias, dproj)


def _ssd_chunk_terms(dtr_ref, bias_ref, alog_ref):
    q = SSM_CHUNK
    dt = _softplus(dtr_ref[...] + bias_ref[...])
    a_neg = -jnp.exp(alog_ref[...])
    row = lax.broadcasted_iota(jnp.int32, (q, q), 0)
    col = lax.broadcasted_iota(jnp.int32, (q, q), 1)
    lower = row >= col
    s = _mask_nn(lower, dt * a_neg)
    return dt, a_neg, s, s.T, lower


def _head_masks():
    heads = jnp.arange(LANES)[:, None]
    chans = jnp.arange(SSM_D_INNER)[None, :]
    to_channels = (chans // SSM_HEAD_DIM == heads).astype(BF16)
    return to_channels, to_channels.T


def _per_channel(v, to_channels):
    hi = v.astype(BF16)
    lo = (v - hi.astype(F32)).astype(BF16)
    return _nn(hi, to_channels) + _nn(lo, to_channels)


def _per_head(v, to_heads):
    hi = v.astype(BF16)
    lo = (v - hi.astype(F32)).astype(BF16)
    return _nn(hi, to_heads) + _nn(lo, to_heads)


def _decay_terms_per_channel(dt, s_col, to_channels):
    q = SSM_CHUNK
    tot = s_col[q - 1:q, :]
    stacked = jnp.concatenate([dt, jnp.exp(s_col), jnp.exp(tot - s_col)], axis=0)
    wide = _per_channel(stacked, to_channels)
    dtx, esx, decx = wide[:q], wide[q:2 * q], wide[2 * q:]
    return dtx, esx, decx, esx[0:1, :] * decx[0:1, :]


SSM_PAIRS_PER_GROUP = SSM_HEADS_PER_GROUP // 2
SSM_GROUP_CHANNELS = SSM_HEADS_PER_GROUP * SSM_HEAD_DIM


def _split_pair(v):
    first = lax.broadcasted_iota(jnp.int32, v.shape, 1) < SSM_HEAD_DIM
    return jnp.concatenate([jnp.where(first, v, 0.0), jnp.where(first, 0.0, v)], axis=0)


def _ssd_fwd(xc, dtr, dt_bias, a_log, dskx, to_channels, name):
    b, s, _ = xc.shape
    q = SSM_CHUNK
    nc = s // q
    n, gc = SSM_D_STATE, SSM_GROUP_CHANNELS

    def body(xc_ref, dtr_ref, bias_ref, alog_ref, dsk_ref, tc_ref, y_ref, hs_ref, h_scr):
        @pl.when(pl.program_id(1) == 0)
        def _():
            h_scr[...] = jnp.zeros_like(h_scr)

        dt, _, s_col, s_row, lower = _ssd_chunk_terms(dtr_ref, bias_ref, alog_ref)
        dtx, esx, decx, etotx = _decay_terms_per_channel(dt, s_col, tc_ref[...])
        x = xc_ref[:, :SSM_D_INNER]
        xdt = x * dtx
        xdec = xdt * decx
        skip = dsk_ref[...] * x
        for g in range(SSM_N_GROUPS):
            bg = xc_ref[:, SSM_D_INNER + n * g:SSM_D_INNER + n * (g + 1)].astype(BF16)
            cg = xc_ref[:, SSM_D_INNER + n * (SSM_N_GROUPS + g):SSM_D_INNER + n * (SSM_N_GROUPS + g + 1)].astype(BF16)
            gsl = slice(gc * g, gc * (g + 1))
            gm = _nt(cg, bg)
            hgt = h_scr[:, gsl]
            hs_ref[:, gsl] = hgt
            y_off = esx[:, gsl] * _nn(cg, hgt)
            h_scr[:, gsl] = etotx[:, gsl] * hgt + _tn(bg, xdec[:, gsl])
            for k in range(SSM_PAIRS_PER_GROUP):
                h0 = g * SSM_HEADS_PER_GROUP + 2 * k
                lo = gc * g + LANES * k
                ms = []
                for h in (h0, h0 + 1):
                    lm = jnp.exp(jnp.where(lower, s_col[:, h:h + 1] - s_row[h:h + 1, :], NEG_INF))
                    ms.append((gm * lm).astype(BF16))
                y_diag = _nn(jnp.concatenate(ms, axis=1), _split_pair(xdt[:, lo:lo + LANES]))
                y_ref[:, lo:lo + LANES] = y_diag + y_off[:, LANES * k:LANES * (k + 1)] + skip[:, lo:lo + LANES]

    vec = pl.BlockSpec((1, LANES), lambda bi, c: (0, 0))
    return pl.pallas_call(
        body, name=name, grid=(b, nc),
        in_specs=[pl.BlockSpec((None, q, SSM_CONV_DIM), lambda bi, c: (bi, c, 0)),
                  pl.BlockSpec((None, q, LANES), lambda bi, c: (bi, c, 0)), vec, vec,
                  pl.BlockSpec((1, SSM_D_INNER), lambda bi, c: (0, 0)),
                  pl.BlockSpec((LANES, SSM_D_INNER), lambda bi, c: (0, 0))],
        out_specs=[pl.BlockSpec((None, q, SSM_D_INNER), lambda bi, c: (bi, c, 0)),
                   pl.BlockSpec((None, None, n, SSM_D_INNER), lambda bi, c: (bi, c, 0, 0))],
        out_shape=[jax.ShapeDtypeStruct((b, s, SSM_D_INNER), F32),
                   jax.ShapeDtypeStruct((b, nc, n, SSM_D_INNER), F32)],
        scratch_shapes=[pltpu.VMEM((n, SSM_D_INNER), F32)],
        compiler_params=_params("parallel", "arbitrary"),
    )(xc, dtr, dt_bias, a_log, dskx, to_channels)


def _ssd_bwd(xc, dtr, dy, hs, dt_bias, a_log, dskx, to_channels, to_heads, dproj, name):
    b, s, _ = xc.shape
    q = SSM_CHUNK
    nc = s // q
    n, gc = SSM_D_STATE, SSM_GROUP_CHANNELS

    def colsum(v):
        return jnp.sum(v, axis=0, keepdims=True)

    def body(xc_ref, dtr_ref, dy_ref, hs_ref, bias_ref, alog_ref, dsk_ref, tc_ref, th_ref, buf_ref,
             dxc_ref, ddtr_ref, dalog_ref, ddsk_ref, dbias_ref, dh_scr, dxs_scr, dxd_scr, w_scr, dst_scr, rows_scr):
        ci = pl.program_id(1)

        @pl.when(ci == 0)
        def _():
            dh_scr[...] = jnp.zeros_like(dh_scr)

        @pl.when(jnp.logical_and(pl.program_id(0) == 0, ci == 0))
        def _():
            dalog_ref[...] = jnp.zeros_like(dalog_ref)
            ddsk_ref[...] = jnp.zeros_like(ddsk_ref)
            dbias_ref[...] = jnp.zeros_like(dbias_ref)
            dst_scr[...] = jnp.zeros_like(dst_scr)

        dt, a_neg, s_col, s_row, lower = _ssd_chunk_terms(dtr_ref, bias_ref, alog_ref)
        upper = jnp.logical_not(lower) | (lax.broadcasted_iota(jnp.int32, (q, q), 0)
                                          == lax.broadcasted_iota(jnp.int32, (q, q), 1))
        dtx, esx, decx, etotx = _decay_terms_per_channel(dt, s_col, tc_ref[...])
        x = xc_ref[:, :SSM_D_INNER]
        dyv = dy_ref[...]
        xdt = x * dtx
        xdec = xdt * decx
        dw = esx * dyv
        rows_scr[...] = jnp.zeros_like(rows_scr)
        for g in range(SSM_N_GROUPS):
            b_lo = SSM_D_INNER + n * g
            c_lo = SSM_D_INNER + n * (SSM_N_GROUPS + g)
            bg = xc_ref[:, b_lo:b_lo + n].astype(BF16)
            cg = xc_ref[:, c_lo:c_lo + n].astype(BF16)
            gsl = slice(gc * g, gc * (g + 1))
            gm = _nt(cg, bg)
            gmt = _nt(bg, cg)
            hgt = hs_ref[:, gsl]
            dhgt = dh_scr[:, gsl]
            w_scr[:, gsl] = _nn(cg, hgt)
            dcg = _nt(dw[:, gsl], hgt)
            dxs = decx[:, gsl] * _nn(bg, dhgt)
            dxs_scr[:, gsl] = dxs
            dbg = _nt(xdec[:, gsl], dhgt)
            rows_scr[2:3, gsl] = colsum(dhgt * hgt)
            dh_scr[:, gsl] = _tn(cg, dw[:, gsl]) + etotx[:, gsl] * dhgt
            dg = jnp.zeros((q, q), F32)
            dgt = jnp.zeros((q, q), F32)
            for k in range(SSM_PAIRS_PER_GROUP):
                h0 = g * SSM_HEADS_PER_GROUP + 2 * k
                lo = gc * g + LANES * k
                xp = xdt[:, lo:lo + LANES]
                dyp = dyv[:, lo:lo + LANES]
                dy2 = _split_pair(dyp)
                dm2 = _nt(dy2, xp)
                dmt2 = _nt(_split_pair(xp), dyp)
                mts = []
                for i, h in enumerate((h0, h0 + 1)):
                    lm = jnp.exp(jnp.where(lower, s_col[:, h:h + 1] - s_row[h:h + 1, :], NEG_INF))
                    lmt = jnp.exp(jnp.where(upper, s_row[h:h + 1, :] - s_col[:, h:h + 1], NEG_INF))
                    dm = dm2[q * i:q * (i + 1), :]
                    dmt = dmt2[q * i:q * (i + 1), :]
                    dg = dg + dm * lm
                    dgt = dgt + dmt * lmt
                    mt = gmt * lmt
                    dst_scr[h:h + 1, :] = colsum(dmt * mt) - colsum(dm * (gm * lm))
                    mts.append(mt.astype(BF16))
                dxd_scr[:, lo:lo + LANES] = _nn(jnp.concatenate(mts, axis=1), dy2)
            dxc_ref[:, b_lo:b_lo + n] = dbg + _nn(dgt, cg)
            dxc_ref[:, c_lo:c_lo + n] = dcg + _nn(dg, bg)
        dxs = dxs_scr[...]
        dxdt = dxd_scr[...] + dxs
        dxc_ref[:, :SSM_D_INNER] = dxdt * dtx + dsk_ref[...] * dyv
        state_part = xdt * dxs
        rows_scr[0:1, :] = colsum(dyv * x)
        rows_scr[1:2, :] = colsum(state_part)
        th = th_ref[...]
        per_head = _per_head(jnp.concatenate([dw * w_scr[...] - state_part, dxdt * x], axis=0), th)
        r_ds, r_dt = per_head[:q], per_head[q:]
        sums = _per_head(rows_scr[...], th)
        etot = jnp.exp(s_col[q - 1:q, :])
        dtot = sums[1:2, :] + etot * sums[2:3, :]
        last = lax.broadcasted_iota(jnp.int32, (q, LANES), 0) == q - 1
        ds = dst_scr[...].T + r_ds + jnp.where(last, dtot, 0.0)
        da = _mask_nn(upper, ds)
        ddt = da * a_neg + r_dt
        live = lax.broadcasted_iota(jnp.int32, (1, LANES), 1) < SSM_N_HEADS
        sg = _sigmoid(dtr_ref[...] + bias_ref[...])
        ddtr = jnp.where(live, ddt * sg, 0.0)
        ddtr_ref[:, :LANES] = ddtr.astype(BF16)
        ddtr_ref[:, LANES:] = jnp.zeros((q, DPROJ_DT_WIDTH - LANES), BF16)
        dalog_ref[...] += jnp.where(live, colsum(da * dt) * a_neg, 0.0)
        ddsk_ref[...] += jnp.where(live, sums[0:1, :], 0.0)
        dbias_ref[...] += colsum(ddtr)

    rev = lambda bi, c: (bi, nc - 1 - c, 0)
    vec = pl.BlockSpec((1, LANES), lambda bi, c: (0, 0))
    wide = pl.BlockSpec((None, q, SSM_D_INNER), rev)
    return pl.pallas_call(
        body, name=name, grid=(b, nc),
        in_specs=[pl.BlockSpec((None, q, SSM_CONV_DIM), rev), pl.BlockSpec((None, q, LANES), rev), wide,
                  pl.BlockSpec((None, None, n, SSM_D_INNER), lambda bi, c: (bi, nc - 1 - c, 0, 0)),
                  vec, vec, pl.BlockSpec((1, SSM_D_INNER), lambda bi, c: (0, 0)),
                  pl.BlockSpec((LANES, SSM_D_INNER), lambda bi, c: (0, 0)),
                  pl.BlockSpec((SSM_D_INNER, LANES), lambda bi, c: (0, 0)),
                  pl.BlockSpec(memory_space=pl.ANY)],
        out_specs=[pl.BlockSpec((None, q, SSM_CONV_DIM), rev),
                   pl.BlockSpec((None, q, DPROJ_DT_WIDTH),
                                lambda bi, c: (bi, nc - 1 - c, DPROJ_COLS["dt"] // DPROJ_DT_WIDTH)), vec, vec, vec],
        input_output_aliases={9: 1},
        out_shape=[jax.ShapeDtypeStruct((b, s, SSM_CONV_DIM), F32), jax.ShapeDtypeStruct(dproj.shape, dproj.dtype),
                   jax.ShapeDtypeStruct((1, LANES), F32), jax.ShapeDtypeStruct((1, LANES), F32),
                   jax.ShapeDtypeStruct((1, LANES), F32)],
        scratch_shapes=[pltpu.VMEM((n, SSM_D_INNER), F32)] + [pltpu.VMEM((q, SSM_D_INNER), F32)] * 3
        + [pltpu.VMEM((LANES, q), F32), pltpu.VMEM((8, SSM_D_INNER), F32)],
        compiler_params=_params("arbitrary", "arbitrary"),
    )(xc, dtr, dy, hs, dt_bias, a_log, dskx, to_channels, to_heads, dproj)


SSM_GROUP_WIDTH = SSM_D_INNER // SSM_N_GROUPS


def _gate_norm_fwd(y, z, w, name):
    t, d = y.shape
    tm = _pick(t, (256, 128))

    def body(y_ref, z_ref, w_ref, o_ref):
        for g in range(SSM_N_GROUPS):
            sl = slice(SSM_GROUP_WIDTH * g, SSM_GROUP_WIDTH * (g + 1))
            zv = z_ref[:, sl]
            u = y_ref[:, sl] * (zv * _sigmoid(zv))
            r = lax.rsqrt(jnp.mean(u * u, axis=-1, keepdims=True) + EPS)
            o_ref[:, sl] = ((u * r) * w_ref[:, sl]).astype(BF16)

    row = pl.BlockSpec((tm, d), lambda i: (i, 0))
    return pl.pallas_call(
        body, name=name, grid=(t // tm,),
        in_specs=[row, row, pl.BlockSpec((1, d), lambda i: (0, 0))], out_specs=row,
        out_shape=jax.ShapeDtypeStruct((t, d), BF16),
        compiler_params=_params("parallel"),
    )(y, z, w)


def _gate_norm_bwd(y, z, w, dout, dproj, name):
    t, d = y.shape
    gw = SSM_GROUP_WIDTH
    tm = _pick(t, (1024, 512, 256, 128))

    def body(y_ref, z_ref, w_ref, do_ref, buf_ref, dy_ref, dz_ref, dw_ref):
        @pl.when(pl.program_id(1) == 0)
        def _():
            dw_ref[...] = jnp.zeros_like(dw_ref)

        zv = z_ref[...]
        yv = y_ref[...]
        sg = _sigmoid(zv)
        silu = zv * sg
        u = yv * silu
        r = lax.rsqrt(jnp.mean(u * u, axis=-1, keepdims=True) + EPS)
        uh = u * r
        dov = do_ref[...]
        dw_ref[...] += jnp.sum(dov * uh, axis=0, keepdims=True)
        dyg = dov * w_ref[...]
        du = r * (dyg - uh * jnp.mean(dyg * uh, axis=-1, keepdims=True))
        dy_ref[...] = du * silu
        dz_ref[...] = (du * yv * (sg * (1.0 + zv * (1.0 - sg)))).astype(BF16)

    tile = pl.BlockSpec((tm, gw), lambda g, i: (i, g))
    vec = pl.BlockSpec((1, gw), lambda g, i: (0, g))
    z_cols = pl.BlockSpec((tm, gw), lambda g, i: (i, DPROJ_COLS["z"] // gw + g))
    return pl.pallas_call(
        body, name=name, grid=(SSM_N_GROUPS, t // tm),
        in_specs=[tile, tile, vec, tile, pl.BlockSpec(memory_space=pl.ANY)], out_specs=[tile, z_cols, vec],
        out_shape=[jax.ShapeDtypeStruct((t, d), F32), jax.ShapeDtypeStruct(dproj.shape, dproj.dtype),
                   jax.ShapeDtypeStruct((1, d), F32)],
        input_output_aliases={4: 1},
        compiler_params=_params("parallel", "arbitrary"),
    )(y, z, w, dout, dproj)


def _rope_tables(s):
    half = ATT_HEAD_DIM // 2
    inv = ROPE_THETA ** (-jnp.arange(half, dtype=F32) / half)
    ang = jnp.arange(s).astype(F32)[:, None] * inv[None, :]
    cos, sin = jnp.cos(ang), jnp.sin(ang)
    return jnp.concatenate([cos, cos], axis=-1), jnp.concatenate([-sin, sin], axis=-1)


ATT_TILE = 256


def _by_residue_spec(r, width):
    return pl.BlockSpec((None, r, ATT_TILE // r, width), lambda bi, i: (bi, 0, i, 0))


def _to_residues(tile, stage, r, store):
    if r == 1:
        store(0, tile)
        return
    stage[...] = tile
    for ri in range(r):
        store(ri, stage[pl.ds(ri, ATT_TILE // r, stride=r), :])


def _from_residues(load, stage, r):
    if r == 1:
        return load(0)
    for ri in range(r):
        stage[pl.ds(ri, ATT_TILE // r, stride=r), :] = load(ri)
    return stage[...]


def _rope_fwd(qkv, cosf, sinf, name):
    b, s, w = qkv.shape
    ts, d, gw = ATT_TILE, ATT_HEAD_DIM, ATT_OUT_DIM

    def body(x_ref, c_ref, s_ref, *rest):
        outs, stage = rest[:-1], rest[-1]
        cv, sv = c_ref[...], s_ref[...]
        for kind in range(3):
            for gi, r in enumerate(ATT_DILATIONS):
                for j in range(ATT_HEADS_PER_GROUP):
                    src = d * (kind * ATT_N_HEADS + gi * ATT_HEADS_PER_GROUP + j)
                    dst = slice(kind * gw + d * j, kind * gw + d * (j + 1))
                    tv = x_ref[:, src:src + d]
                    if kind < 2:
                        tv = tv * cv + pltpu.roll(tv, d // 2, 1) * sv

                    def store(ri, rows, o_ref=outs[gi], dst=dst):
                        o_ref[ri, :, dst] = rows.astype(BF16)

                    _to_residues(tv, stage, r, store)

    tab = pl.BlockSpec((ts, d), lambda bi, i: (i, 0))
    return pl.pallas_call(
        body, name=name, grid=(b, s // ts),
        in_specs=[pl.BlockSpec((None, ts, w), lambda bi, i: (bi, i, 0)), tab, tab],
        out_specs=[_by_residue_spec(r, 3 * gw) for r in ATT_DILATIONS],
        out_shape=[jax.ShapeDtypeStruct((b, r, s // r, 3 * gw), BF16) for r in ATT_DILATIONS],
        scratch_shapes=[pltpu.VMEM((ts, d), F32)],
        compiler_params=_params("parallel", "parallel"),
    )(qkv, cosf, sinf)


def _rope_bwd(dq, dk, dv, cosf, sinf, dproj, name):
    n_pat = len(ATT_DILATIONS)
    b, _, s, gw = dq[0].shape
    ts, d = ATT_TILE, ATT_HEAD_DIM

    def body(*refs):
        ins, (c_ref, s_ref, _, o_ref, stage) = refs[:3 * n_pat], refs[3 * n_pat:]
        cv, sv = c_ref[...], s_ref[...]
        for kind in range(3):
            for gi, r in enumerate(ATT_DILATIONS):
                src = ins[kind * n_pat + gi]
                for j in range(ATT_HEADS_PER_GROUP):
                    tv = _from_residues(lambda ri, src=src, j=j: src[ri, :, d * j:d * (j + 1)], stage, r)
                    if kind < 2:
                        tv = tv * cv + pltpu.roll(tv * sv, d // 2, 1)
                    lo = d * (kind * ATT_N_HEADS + gi * ATT_HEADS_PER_GROUP + j)
                    o_ref[:, lo:lo + d] = tv.astype(BF16)

    tab = pl.BlockSpec((ts, d), lambda bi, i: (i, 0))
    parts = [_by_residue_spec(r, gw) for r in ATT_DILATIONS]
    return pl.pallas_call(
        body, name=name, grid=(b, s // ts), in_specs=parts * 3 + [tab, tab, pl.BlockSpec(memory_space=pl.ANY)],
        out_specs=pl.BlockSpec((None, ts, ATT_QKV_DIM), lambda bi, i: (bi, i, DPROJ_COLS["qkv"] // ATT_QKV_DIM)),
        out_shape=jax.ShapeDtypeStruct(dproj.shape, dproj.dtype),
        input_output_aliases={3 * n_pat + 2: 0},
        scratch_shapes=[pltpu.VMEM((ts, d), F32)],
        compiler_params=_params("parallel", "parallel"),
    )(*dq, *dk, *dv, cosf, sinf, dproj)


ATT_SCALE = ATT_HEAD_DIM ** -0.5
ATT_STEP = 2 * ATT_BLOCK


def _att_spec(col):
    return pl.BlockSpec((None, None, ATT_STEP, ATT_OUT_DIM), lambda bi, ri, i: (bi, ri, i, col))


def _att_edge_spec(col, side, n_steps):
    def index(bi, ri, i):
        blk = 2 * i - 1 if side < 0 else 2 * i + 2
        return (bi, ri, jnp.clip(blk, 0, 2 * n_steps - 1), col)
    return pl.BlockSpec((None, None, ATT_BLOCK, ATT_OUT_DIM), index)


def _band_mask(shape, q_axis, has_prev):
    qi = lax.broadcasted_iota(jnp.int32, shape, q_axis)
    kj = lax.broadcasted_iota(jnp.int32, shape, 1 - q_axis)
    dist = qi + ATT_BLOCK - kj
    return (dist >= 0) & (dist <= ATT_BLOCK) & (has_prev | (kj >= ATT_BLOCK))


def _att_fwd(qkr, name):
    b, r, l, _ = qkr.shape
    nb = l // ATT_STEP
    d = ATT_HEAD_DIM

    def body(q_ref, kp_ref, k_ref, vp_ref, v_ref, o_ref, lse_ref):
        mask = _band_mask((ATT_STEP, ATT_BLOCK + ATT_STEP), 0, pl.program_id(2) > 0)
        for j in range(ATT_HEADS_PER_GROUP):
            sl = slice(d * j, d * (j + 1))
            kcat = jnp.concatenate([kp_ref[:, sl], k_ref[:, sl]], axis=0)
            vcat = jnp.concatenate([vp_ref[:, sl], v_ref[:, sl]], axis=0)
            sc = jnp.where(mask, _nt(q_ref[:, sl], kcat) * ATT_SCALE, NEG_INF)
            m = jnp.max(sc, axis=-1, keepdims=True)
            pr = jnp.exp(sc - m)
            den = jnp.sum(pr, axis=-1, keepdims=True)
            o_ref[:, sl] = _nn(pr / den, vcat)
            lse_ref[:, sl] = jnp.broadcast_to(m + jnp.log(den), (ATT_STEP, d))

    out_spec = _att_spec(0)
    return pl.pallas_call(
        body, name=name, grid=(b, r, nb),
        in_specs=[_att_spec(0), _att_edge_spec(1, -1, nb), _att_spec(1), _att_edge_spec(2, -1, nb), _att_spec(2)],
        out_specs=[out_spec, out_spec],
        out_shape=[jax.ShapeDtypeStruct((b, r, l, ATT_OUT_DIM), F32)] * 2,
        compiler_params=_params("parallel", "parallel", "parallel"),
    )(qkr, qkr, qkr, qkr, qkr)


def _att_merge(os_, lses, name):
    n_pat = len(os_)
    b, _, s, gw = os_[0].shape
    ts, d = ATT_TILE, ATT_HEAD_DIM

    def body(*refs):
        o_refs, l_refs = refs[:n_pat], refs[n_pat:2 * n_pat]
        att_ref, lse_outs, stage = refs[2 * n_pat], refs[2 * n_pat + 1:3 * n_pat + 1], refs[-1]
        for j in range(ATT_HEADS_PER_GROUP):
            sl = slice(d * j, d * (j + 1))
            ov = [_from_residues(lambda ri, g=g: o_refs[g][ri, :, sl], stage, r)
                  for g, r in enumerate(ATT_DILATIONS)]
            ls = [_from_residues(lambda ri, g=g: l_refs[g][ri, :, sl], stage, r)
                  for g, r in enumerate(ATT_DILATIONS)]
            m = functools.reduce(jnp.maximum, ls)
            es = [jnp.exp(lv - m) for lv in ls]
            tot = functools.reduce(lambda u, v: u + v, es)
            acc = (es[0] / tot) * ov[0]
            for g in range(1, n_pat):
                acc = acc + (es[g] / tot) * ov[g]
            att_ref[:, sl] = acc
            joint = m + jnp.log(tot)
            for g, r in enumerate(ATT_DILATIONS):
                def store(ri, rows, out=lse_outs[g]):
                    out[ri, :, sl] = rows
                _to_residues(joint, stage, r, store)

    parts = [_by_residue_spec(r, gw) for r in ATT_DILATIONS]
    return pl.pallas_call(
        body, name=name, grid=(b, s // ts), in_specs=parts * 2,
        out_specs=[pl.BlockSpec((None, ts, gw), lambda bi, i: (bi, i, 0))] + parts,
        out_shape=[jax.ShapeDtypeStruct((b, s, gw), F32)]
        + [jax.ShapeDtypeStruct((b, r, s // r, gw), F32) for r in ATT_DILATIONS],
        scratch_shapes=[pltpu.VMEM((ts, d), F32)],
        compiler_params=_params("parallel", "parallel"),
    )(*os_, *lses)


def _att_delta(att, datt, name):
    b, s, gw = att.shape
    ts, d = ATT_TILE, ATT_HEAD_DIM
    n_pat = len(ATT_DILATIONS)

    def body(a_ref, d_ref, *rest):
        do_outs, dl_outs, stage = rest[:n_pat], rest[n_pat:2 * n_pat], rest[-1]
        for j in range(ATT_HEADS_PER_GROUP):
            sl = slice(d * j, d * (j + 1))
            dv = d_ref[:, sl]
            delta = jnp.broadcast_to(jnp.sum(a_ref[:, sl] * dv, axis=-1, keepdims=True), (ts, d))
            for g, r in enumerate(ATT_DILATIONS):
                def store_do(ri, rows, out=do_outs[g]):
                    out[ri, :, sl] = rows.astype(BF16)

                def store_dl(ri, rows, out=dl_outs[g]):
                    out[ri, :, sl] = rows

                _to_residues(dv, stage, r, store_do)
                _to_residues(delta, stage, r, store_dl)

    row = pl.BlockSpec((None, ts, gw), lambda bi, i: (bi, i, 0))
    parts = [_by_residue_spec(r, gw) for r in ATT_DILATIONS]
    outs = pl.pallas_call(
        body, name=name, grid=(b, s // ts), in_specs=[row, row], out_specs=parts * 2,
        out_shape=[jax.ShapeDtypeStruct((b, r, s // r, gw), BF16) for r in ATT_DILATIONS]
        + [jax.ShapeDtypeStruct((b, r, s // r, gw), F32) for r in ATT_DILATIONS],
        scratch_shapes=[pltpu.VMEM((ts, d), F32)],
        compiler_params=_params("parallel", "parallel"),
    )(att, datt)
    return outs[:n_pat], outs[n_pat:]


def _att_bwd_q(qkr, datt, lse, delta, name):
    b, r, l, _ = qkr.shape
    nb = l // ATT_STEP
    d = ATT_HEAD_DIM

    def body(q_ref, kp_ref, k_ref, vp_ref, v_ref, do_ref, lse_ref, dl_ref, dq_ref):
        mask = _band_mask((ATT_STEP, ATT_BLOCK + ATT_STEP), 0, pl.program_id(2) > 0)
        for j in range(ATT_HEADS_PER_GROUP):
            sl = slice(d * j, d * (j + 1))
            kcat = jnp.concatenate([kp_ref[:, sl], k_ref[:, sl]], axis=0)
            vcat = jnp.concatenate([vp_ref[:, sl], v_ref[:, sl]], axis=0)
            sc = _nt(q_ref[:, sl], kcat) * ATT_SCALE
            pr = jnp.exp(jnp.where(mask, sc - lse_ref[:, d * j:d * j + 1], NEG_INF))
            dp = _nt(do_ref[:, sl], vcat)
            dsc = pr * (dp - dl_ref[:, d * j:d * j + 1])
            dq_ref[:, sl] = _nn(dsc, kcat) * ATT_SCALE

    tok = _att_spec(0)
    return pl.pallas_call(
        body, name=name, grid=(b, r, nb),
        in_specs=[_att_spec(0), _att_edge_spec(1, -1, nb), _att_spec(1), _att_edge_spec(2, -1, nb), _att_spec(2),
                  tok, tok, tok],
        out_specs=tok,
        out_shape=jax.ShapeDtypeStruct((b, r, l, ATT_OUT_DIM), F32),
        compiler_params=_params("parallel", "parallel", "parallel"),
    )(qkr, qkr, qkr, qkr, qkr, datt, lse, delta)


def _att_bwd_kv(qkr, datt, lse, delta, name):
    b, r, l, _ = qkr.shape
    nb = l // ATT_STEP
    d = ATT_HEAD_DIM

    def body(k_ref, v_ref, q_ref, qn_ref, do_ref, don_ref, lse_ref, lsen_ref, dl_ref, dln_ref, dk_ref, dv_ref):
        shape = (ATT_STEP, ATT_STEP + ATT_BLOCK)
        kj = lax.broadcasted_iota(jnp.int32, shape, 0)
        qi = lax.broadcasted_iota(jnp.int32, shape, 1)
        dist = qi - kj
        has_next = pl.program_id(2) < nb - 1
        mask = (dist >= 0) & (dist <= ATT_BLOCK) & (has_next | (qi < ATT_STEP))
        for j in range(ATT_HEADS_PER_GROUP):
            sl = slice(d * j, d * (j + 1))
            qcat = jnp.concatenate([q_ref[:, sl], qn_ref[:, sl]], axis=0)
            docat = jnp.concatenate([do_ref[:, sl], don_ref[:, sl]], axis=0)
            lse_t = jnp.tile(jnp.concatenate([lse_ref[:, sl], lsen_ref[:, sl]], axis=0).T, (ATT_STEP // d, 1))
            dl_t = jnp.tile(jnp.concatenate([dl_ref[:, sl], dln_ref[:, sl]], axis=0).T, (ATT_STEP // d, 1))
            sc_t = _nt(k_ref[:, sl], qcat) * ATT_SCALE
            pr_t = jnp.exp(jnp.where(mask, sc_t - lse_t, NEG_INF))
            dv_ref[:, sl] = _nn(pr_t, docat)
            dsc_t = pr_t * (_nt(v_ref[:, sl], docat) - dl_t)
            dk_ref[:, sl] = _nn(dsc_t, qcat) * ATT_SCALE

    tok, tok_n = _att_spec(0), _att_edge_spec(0, 1, nb)
    return pl.pallas_call(
        body, name=name, grid=(b, r, nb),
        in_specs=[_att_spec(1), _att_spec(2), _att_spec(0), _att_edge_spec(0, 1, nb),
                  tok, tok_n, tok, tok_n, tok, tok_n],
        out_specs=[tok, tok],
        out_shape=[jax.ShapeDtypeStruct((b, r, l, ATT_OUT_DIM), F32)] * 2,
        compiler_params=_params("parallel", "parallel", "parallel"),
    )(qkr, qkr, qkr, qkr, datt, datt, lse, lse, delta, delta)


def _mix_fwd(gl, bg, ys, ya, name):
    t, d = ys.shape
    tm = _pick(t, (512, 256, 128))

    def body(gl_ref, bg_ref, ys_ref, ya_ref, o_ref):
        g0 = _sigmoid(gl_ref[:, :d] + bg_ref[:, :d])
        g1 = _sigmoid(gl_ref[:, d:] + bg_ref[:, d:])
        o_ref[...] = (g0 * ys_ref[...] + g1 * ya_ref[...]).astype(BF16)

    row = pl.BlockSpec((tm, d), lambda i: (i, 0))
    return pl.pallas_call(
        body, name=name, grid=(t // tm,),
        in_specs=[pl.BlockSpec((tm, 2 * d), lambda i: (i, 0)), pl.BlockSpec((1, 2 * d), lambda i: (0, 0)), row, row],
        out_specs=row, out_shape=jax.ShapeDtypeStruct((t, d), BF16),
        compiler_params=_params("parallel"),
    )(gl, bg, ys, ya)


def _mix_bwd(gl, bg, ys, ya, dmixed, name):
    t, d = ys.shape
    tm = _pick(t, (512, 256, 128))

    def body(gl_ref, bg_ref, ys_ref, ya_ref, dm_ref, dys_ref, dya_ref, dgl_ref, dbg_ref):
        @pl.when(pl.program_id(0) == 0)
        def _():
            dbg_ref[...] = jnp.zeros_like(dbg_ref)

        dm = dm_ref[...]
        g0 = _sigmoid(gl_ref[:, :d] + bg_ref[:, :d])
        g1 = _sigmoid(gl_ref[:, d:] + bg_ref[:, d:])
        dys_ref[...] = (dm * g0).astype(BF16)
        dya_ref[...] = (dm * g1).astype(BF16)
        d0 = dm * ys_ref[...] * (g0 * (1.0 - g0))
        d1 = dm * ya_ref[...] * (g1 * (1.0 - g1))
        dgl_ref[:, :d] = d0.astype(BF16)
        dgl_ref[:, d:] = d1.astype(BF16)
        dbg_ref[:, :d] += jnp.sum(d0, axis=0, keepdims=True)
        dbg_ref[:, d:] += jnp.sum(d1, axis=0, keepdims=True)

    row = pl.BlockSpec((tm, d), lambda i: (i, 0))
    wide = pl.BlockSpec((tm, 2 * d), lambda i: (i, 0))
    vec = pl.BlockSpec((1, 2 * d), lambda i: (0, 0))
    gate_cols = pl.BlockSpec((tm, 2 * d), lambda i: (i, DPROJ_COLS["gate"] // (2 * d)))
    return pl.pallas_call(
        body, name=name, grid=(t // tm,),
        in_specs=[wide, vec, row, row, row], out_specs=[row, row, gate_cols, vec],
        out_shape=[jax.ShapeDtypeStruct((t, d), BF16), jax.ShapeDtypeStruct((t, d), BF16),
                   jax.ShapeDtypeStruct((t, DPROJ_WIDTH), BF16), jax.ShapeDtypeStruct((1, 2 * d), F32)],
        compiler_params=_params("arbitrary"),
    )(gl, bg, ys, ya, dmixed)


def _swiglu_fwd(gt, up, name):
    t, f = gt.shape
    tm = _pick(t, (512, 256, 128))

    def body(g_ref, u_ref, o_ref):
        gv = g_ref[...]
        o_ref[...] = ((gv * _sigmoid(gv)) * u_ref[...]).astype(BF16)

    row = pl.BlockSpec((tm, f), lambda i: (i, 0))
    return pl.pallas_call(
        body, name=name, grid=(t // tm,), in_specs=[row, row], out_specs=row,
        out_shape=jax.ShapeDtypeStruct((t, f), BF16), compiler_params=_params("parallel"),
    )(gt, up)


def _swiglu_bwd(gt, up, dact, name):
    t, f = gt.shape
    tm = _pick(t, (512, 256, 128))

    def body(g_ref, u_ref, d_ref, dg_ref, du_ref):
        gv = g_ref[...]
        dv = d_ref[...]
        sg = _sigmoid(gv)
        dg_ref[...] = (dv * u_ref[...] * (sg * (1.0 + gv * (1.0 - sg)))).astype(BF16)
        du_ref[...] = (dv * (gv * sg)).astype(BF16)

    row = pl.BlockSpec((tm, f), lambda i: (i, 0))
    return pl.pallas_call(
        body, name=name, grid=(t // tm,), in_specs=[row, row, row], out_specs=[row, row],
        out_shape=[jax.ShapeDtypeStruct((t, f), BF16)] * 2, compiler_params=_params("parallel"),
    )(gt, up, dact)


def _peer(k):
    x, y, c = lax.axis_index("x"), lax.axis_index("y"), lax.axis_index("c")
    px, py, pc = x ^ ((k >> 2) & 1), y ^ ((k >> 1) & 1), c ^ (k & 1)
    return (px, py, pc), 4 * px + 2 * py + pc


def _my_index():
    return 4 * lax.axis_index("x") + 2 * lax.axis_index("y") + lax.axis_index("c")


def _all_gather(parts, name):
    n_parts = len(parts)

    def body(*refs):
        ins, outs = refs[:n_parts], refs[n_parts:2 * n_parts]
        send_sems, recv_sems, local_sems = refs[2 * n_parts:]
        here, me = _peer(0)
        sibling, sib_idx = _peer(1)
        chips = [_peer(2 * q) for q in range(1, N_CHIPS)]

        def copy(i, k, block, to, src=None):
            return pltpu.make_async_remote_copy(
                src_ref=outs[i].at[block] if src is None else src, dst_ref=outs[i].at[block],
                send_sem=send_sems.at[i * (N_DEV - 1) + k], recv_sem=recv_sems.at[i * (N_DEV - 1) + k],
                device_id=to, device_id_type=MESH)

        local = [pltpu.make_async_copy(ins[i], outs[i].at[me], local_sems.at[i]) for i in range(n_parts)]
        for cp in local:
            cp.start()
        sends = []
        for i in range(n_parts):
            sends.append(copy(i, 0, me, sibling, src=ins[i]))
            sends += [copy(i, q, me, chip, src=ins[i]) for q, (chip, _) in enumerate(chips, start=1)]
        for cp in sends:
            cp.start()
        for q, (chip, chip_idx) in enumerate(chips, start=1):
            for i in range(n_parts):
                copy(i, q, chip_idx, here).wait_recv()
                fwd = copy(i, N_CHIPS - 1 + q, chip_idx, sibling)
                fwd.start()
                sends.append(fwd)
        for i in range(n_parts):
            copy(i, 0, sib_idx, here).wait_recv()
        for q, (_, chip_idx) in enumerate(chips, start=1):
            for i in range(n_parts):
                copy(i, N_CHIPS - 1 + q, chip_idx ^ 1, here).wait_recv()
        for cp in sends:
            cp.wait_send()
        for cp in local:
            cp.wait()

    hbm = pl.BlockSpec(memory_space=pl.ANY)
    return pl.pallas_call(
        body, name=name, in_specs=[hbm] * n_parts, out_specs=[hbm] * n_parts,
        out_shape=[jax.ShapeDtypeStruct((N_DEV,) + p_.shape, p_.dtype) for p_ in parts],
        scratch_shapes=[pltpu.SemaphoreType.DMA((n_parts * (N_DEV - 1),)),
                        pltpu.SemaphoreType.DMA((n_parts * (N_DEV - 1),)),
                        pltpu.SemaphoreType.DMA((n_parts,))],
        compiler_params=pltpu.CompilerParams(has_side_effects=True),
    )(*parts)


TILE_ELEMS = 1024 * 1024


def _pair_exchange(slabs, name):
    def body(slab_ref, got_ref, send_sems, recv_sems):
        c = lax.axis_index("c")
        sibling, _ = _peer(1)
        copies = [pltpu.make_async_remote_copy(
            src_ref=slab_ref.at[2 * q + 1 - c], dst_ref=got_ref.at[q], send_sem=send_sems.at[q],
            recv_sem=recv_sems.at[q], device_id=sibling, device_id_type=MESH) for q in range(N_CHIPS)]
        for cp in copies:
            cp.start()
        for cp in copies:
            cp.wait()

    hbm = pl.BlockSpec(memory_space=pl.ANY)
    return pl.pallas_call(
        body, name=name, in_specs=[hbm], out_specs=hbm,
        out_shape=jax.ShapeDtypeStruct((N_CHIPS,) + slabs.shape[1:], slabs.dtype),
        scratch_shapes=[pltpu.SemaphoreType.DMA((N_CHIPS,)), pltpu.SemaphoreType.DMA((N_CHIPS,))],
        compiler_params=pltpu.CompilerParams(has_side_effects=True),
    )(slabs)


def _chip_sum(slabs, got, core, name):
    _, rows, lanes = slabs.shape
    tr = _pick(rows, (TILE_ELEMS // lanes, 512, 256, 128, 64, 32, 16))

    def body(core_ref, mine_ref, got_ref, o_ref):
        o_ref[...] = (mine_ref[...].astype(F32) + got_ref[...].astype(F32)).astype(BF16)

    return pl.pallas_call(
        body, name=name,
        grid_spec=pltpu.PrefetchScalarGridSpec(
            num_scalar_prefetch=1, grid=(N_CHIPS, rows // tr),
            in_specs=[pl.BlockSpec((None, tr, lanes), lambda q, i, core_ref: (2 * q + core_ref[0], i, 0)),
                      pl.BlockSpec((None, tr, lanes), lambda q, i, core_ref: (q, i, 0))],
            out_specs=pl.BlockSpec((None, tr, lanes), lambda q, i, core_ref: (q, i, 0))),
        out_shape=jax.ShapeDtypeStruct((N_CHIPS, rows, lanes), BF16),
        compiler_params=_params("parallel", "parallel"),
    )(core, slabs, got)


def _chip_exchange(chip_sums, shared, name):
    def body(sum_ref, sh_ref, got_ref, gsh_ref, send_sems, recv_sems, sh_send_sems, sh_recv_sems, local_sems):
        me = _my_index()
        my_chip = me >> 1
        local = [pltpu.make_async_copy(sum_ref.at[my_chip], got_ref.at[my_chip], local_sems.at[0]),
                 pltpu.make_async_copy(sh_ref, gsh_ref.at[me], local_sems.at[1])]
        for cp in local:
            cp.start()
        sends = []
        for q in range(1, N_CHIPS):
            peer, pidx = _peer(2 * q)
            cp = pltpu.make_async_remote_copy(
                src_ref=sum_ref.at[pidx >> 1], dst_ref=got_ref.at[my_chip], send_sem=send_sems.at[q - 1],
                recv_sem=recv_sems.at[q - 1], device_id=peer, device_id_type=MESH)
            cp.start()
            sends.append(cp)
        for k in range(1, N_DEV):
            peer, _ = _peer(k)
            cp = pltpu.make_async_remote_copy(
                src_ref=sh_ref, dst_ref=gsh_ref.at[me], send_sem=sh_send_sems.at[k - 1],
                recv_sem=sh_recv_sems.at[k - 1], device_id=peer, device_id_type=MESH)
            cp.start()
            sends.append(cp)
        for q in range(1, N_CHIPS):
            peer, pidx = _peer(2 * q)
            pltpu.make_async_remote_copy(
                src_ref=sum_ref.at[my_chip], dst_ref=got_ref.at[pidx >> 1], send_sem=send_sems.at[q - 1],
                recv_sem=recv_sems.at[q - 1], device_id=peer, device_id_type=MESH).wait_recv()
        for k in range(1, N_DEV):
            peer, pidx = _peer(k)
            pltpu.make_async_remote_copy(
                src_ref=sh_ref, dst_ref=gsh_ref.at[pidx], send_sem=sh_send_sems.at[k - 1],
                recv_sem=sh_recv_sems.at[k - 1], device_id=peer, device_id_type=MESH).wait_recv()
        for cp in sends:
            cp.wait_send()
        for cp in local:
            cp.wait()

    hbm = pl.BlockSpec(memory_space=pl.ANY)
    return pl.pallas_call(
        body, name=name, in_specs=[hbm, hbm], out_specs=[hbm, hbm],
        out_shape=[jax.ShapeDtypeStruct(chip_sums.shape, chip_sums.dtype),
                   jax.ShapeDtypeStruct((N_DEV,) + shared.shape, shared.dtype)],
        scratch_shapes=[pltpu.SemaphoreType.DMA((N_CHIPS - 1,)), pltpu.SemaphoreType.DMA((N_CHIPS - 1,)),
                        pltpu.SemaphoreType.DMA((N_DEV - 1,)), pltpu.SemaphoreType.DMA((N_DEV - 1,)),
                        pltpu.SemaphoreType.DMA((2,))],
        compiler_params=pltpu.CompilerParams(has_side_effects=True),
    )(chip_sums, shared)


def _adamw(parts, w, m, v, name):
    n_parts, rows, lanes = parts.shape
    tr = rows if rows * lanes <= TILE_ELEMS // 2 else _pick(rows, (TILE_ELEMS // 4 // lanes, 128, 64, 32, 16, 8))
    c1 = 1.0 - ADAM_B1 ** ADAM_STEP
    c2 = 1.0 - ADAM_B2 ** ADAM_STEP

    def body(p_ref, w_ref, m_ref, v_ref, g_ref, d_ref, nm_ref, nv_ref):
        g = p_ref[0].astype(F32)
        for j in range(1, n_parts):
            g = g + p_ref[j].astype(F32)
        nm = ADAM_B1 * m_ref[...] + (1.0 - ADAM_B1) * g
        nv = ADAM_B2 * v_ref[...] + (1.0 - ADAM_B2) * (g * g)
        g_ref[...] = g
        nm_ref[...] = nm
        nv_ref[...] = nv
        d_ref[...] = -ADAM_LR * ((nm / c1) / (jnp.sqrt(nv / c2) + ADAM_EPS) + ADAM_WD * w_ref[...])

    row = pl.BlockSpec((tr, lanes), lambda i: (i, 0))
    return pl.pallas_call(
        body, name=name, grid=(rows // tr,),
        in_specs=[pl.BlockSpec((n_parts, tr, lanes), lambda i: (0, i, 0)), row, row, row],
        out_specs=[row] * 4, out_shape=[jax.ShapeDtypeStruct((rows, lanes), F32)] * 4,
        compiler_params=_params("parallel"),
    )(parts, w, m, v)


MATRIX_SHARDS = (
    ("w_in", (D_MODEL, IN_PROJ_DIM // N_DEV), True),
    ("w_ssm_out", (SSM_D_INNER // N_DEV, D_MODEL), False),
    ("w_att_out", (ATT_OUT_DIM, D_MODEL // N_DEV), True),
    ("w_mix_out", (D_MODEL // N_DEV, D_MODEL), False),
    ("w_ffn_gate", (D_MODEL, D_FF // N_DEV), True),
    ("w_ffn_up", (D_MODEL, D_FF // N_DEV), True),
    ("w_ffn_down", (D_FF // N_DEV, D_MODEL), False),
)
CONV_SHARD = ("conv_w", (SSM_CONV, SSM_CONV_DIM // N_DEV), True)
SHARDED = MATRIX_SHARDS + (CONV_SHARD,)
REPLICATED = (("norm_mix", D_MODEL), ("b_gate", 2 * D_MODEL), ("conv_b", SSM_CONV_DIM), ("dt_bias", SSM_N_HEADS),
              ("a_log", SSM_N_HEADS), ("d_skip", SSM_N_HEADS), ("ssm_norm", SSM_D_INNER), ("norm_ffn", D_MODEL),
              ("norm_final", D_MODEL))


def _round_up(n, mult):
    return -(-n // mult) * mult


def _pack_rows(flat, row_mult):
    rows = _round_up(-(-flat.shape[0] // LANES), row_mult)
    return jnp.pad(flat, (0, rows * LANES - flat.shape[0])).reshape(rows, LANES)


def _stacking(specs):
    return tuple((name, (shape[1], shape[0]) if by_cols else shape, by_cols) for name, shape, by_cols in specs)


def _to_stacking(vals, specs):
    return {name: (vals[name].T if by_cols else vals[name]) for name, _, by_cols in specs}


STACK_WIDTH = D_MODEL
STACK_ALIGN = 16
STACK_ORDER = ("w_ssm_out", "w_mix_out", "w_ffn_gate", "w_ffn_up", "w_ffn_down", "w_att_out", "conv_w", "w_in")


def _stack_layout():
    shapes = {name: shape for name, shape, _ in _stacking(SHARDED)}
    layout, off = {}, 0
    for name in STACK_ORDER:
        r, c = shapes[name]
        rows = r if c == STACK_WIDTH else _round_up(-(-(r * c) // STACK_WIDTH), STACK_ALIGN)
        layout[name] = (off, rows, (r, c))
        off = _round_up(off + rows, STACK_ALIGN)
    return layout, _round_up(off, 1024)


def _to_stack_rows(v, rows):
    if v.shape[-1] == STACK_WIDTH:
        return v
    lead = v.shape[:-2]
    flat = v.reshape(lead + (-1,))
    flat = jnp.pad(flat, [(0, 0)] * len(lead) + [(0, rows * STACK_WIDTH - flat.shape[-1])])
    return flat.reshape(lead + (rows, STACK_WIDTH))


def _from_stack_rows(block, shape):
    r, c = shape
    if c == STACK_WIDTH:
        return block
    lead = block.shape[:-2]
    return block.reshape(lead + (-1,))[..., :r * c].reshape(lead + (r, c))


def _stack(vals, dtype, skip=(), head_only=False):
    layout, total = _stack_layout()
    order = STACK_ORDER
    if head_only:
        order, total = STACK_ORDER[:-1], layout["w_in"][0]
    lead = next(iter(vals.values())).shape[:-2]
    pieces = []
    for i, name in enumerate(order):
        off, rows, _ = layout[name]
        until = layout[order[i + 1]][0] if i + 1 < len(order) else total
        piece = jnp.zeros(lead + (rows, STACK_WIDTH), dtype) if name in skip else _to_stack_rows(vals[name], rows)
        pieces.append(jnp.pad(piece.astype(dtype), [(0, 0)] * len(lead) + [(0, until - off - rows), (0, 0)]))
    return jnp.concatenate(pieces, axis=-2)


def _unstack(stacked, names):
    layout, _ = _stack_layout()
    return {name: _from_stack_rows(stacked[..., layout[name][0]:layout[name][0] + layout[name][1], :], layout[name][2])
            for name in names}


W_IN_SHARD_ROWS = IN_PROJ_DIM // N_DEV


def _w_in_row_moves():
    moves, orig = [], 0
    for name, size in IN_SPLIT:
        for j in range(N_DEV):
            lo, hi = max(orig, W_IN_SHARD_ROWS * j), min(orig + size, W_IN_SHARD_ROWS * (j + 1))
            if lo < hi:
                moves.append((j, lo - W_IN_SHARD_ROWS * j, DPROJ_COLS[name] + lo - orig, hi - lo))
        orig += size
    return moves


def _w_in_from_shards(stacked_all, name):
    layout, total = _stack_layout()
    base = layout["w_in"][0]
    pad_lo, pad_hi = DPROJ_COLS["dt"] + _round_up(SSM_N_HEADS, STACK_ALIGN), DPROJ_COLS["dt"] + DPROJ_DT_WIDTH

    def body(x_ref, o_ref):
        o_ref[pad_lo:pad_hi, :] = jnp.zeros((pad_hi - pad_lo, LANES), x_ref.dtype)
        for j, r, at, n in _w_in_row_moves():
            o_ref[at:at + n, :] = x_ref[j, base + r:base + r + n, :]

    return pl.pallas_call(
        body, name=name, grid=(STACK_WIDTH // LANES,),
        in_specs=[pl.BlockSpec((N_DEV, total, LANES), lambda c: (0, 0, c))],
        out_specs=pl.BlockSpec((DPROJ_WIDTH, LANES), lambda c: (0, c)),
        out_shape=jax.ShapeDtypeStruct((DPROJ_WIDTH, STACK_WIDTH), stacked_all.dtype),
        compiler_params=_params("parallel"),
    )(stacked_all)


def _w_in_to_shards(dw_all, head, name):
    layout, total = _stack_layout()
    base = layout["w_in"][0]
    end = base + W_IN_SHARD_ROWS

    def body(x_ref, h_ref, o_ref):
        o_ref[:, 0:base, :] = h_ref[...]
        for j, r, at, n in _w_in_row_moves():
            o_ref[j, base + r:base + r + n, :] = x_ref[at:at + n, :]
        o_ref[:, end:total, :] = jnp.zeros((N_DEV, total - end, LANES), o_ref.dtype)

    return pl.pallas_call(
        body, name=name, grid=(STACK_WIDTH // LANES,),
        in_specs=[pl.BlockSpec((DPROJ_WIDTH, LANES), lambda c: (0, c)),
                  pl.BlockSpec((N_DEV, base, LANES), lambda c: (0, 0, c))],
        out_specs=pl.BlockSpec((N_DEV, total, LANES), lambda c: (0, 0, c)),
        out_shape=jax.ShapeDtypeStruct((N_DEV, total, STACK_WIDTH), dw_all.dtype),
        compiler_params=_params("parallel"),
    )(dw_all, head)


REPLICATED_ROWS = sum(-(-size // LANES) for _, size in REPLICATED)
LOSS_ROW = REPLICATED_ROWS


def _pack_replicated(vals):
    rows = []
    for name, size in REPLICATED:
        v = vals[name].reshape(-1).astype(F32)
        rows.append(jnp.pad(v, (0, _round_up(size, LANES) - size)))
    return _pack_rows(jnp.concatenate(rows), 8)


def _unpack_replicated(packed, shapes):
    flat = packed.reshape(-1)
    out, off = {}, 0
    for name, size in REPLICATED:
        out[name] = flat[off:off + size].reshape(shapes[name])
        off += _round_up(size, LANES)
    return out


def _lane_row(v):
    v = v.reshape(-1).astype(F32)
    return jnp.pad(v, (0, LANES - v.shape[0])).reshape(1, LANES)


IN_SPLIT = (("z", SSM_D_INNER), ("xbc", SSM_CONV_DIM), ("dt", SSM_N_HEADS), ("qkv", ATT_QKV_DIM), ("gate", 2 * D_MODEL))


def kernel(x, norm_mix, w_in, b_gate, conv_w, conv_b, dt_bias, a_log, d_skip, ssm_norm, w_ssm_out, w_att_out, w_mix_out, norm_ffn, w_ffn_gate, w_ffn_up, w_ffn_down, norm_final, loss_target, m_norm_mix, m_w_in, m_b_gate, m_conv_w, m_conv_b, m_dt_bias, m_a_log, m_d_skip, m_ssm_norm, m_w_ssm_out, m_w_att_out, m_w_mix_out, m_norm_ffn, m_w_ffn_gate, m_w_ffn_up, m_w_ffn_down, m_norm_final, v_norm_mix, v_w_in, v_b_gate, v_conv_w, v_conv_b, v_dt_bias, v_a_log, v_d_skip, v_ssm_norm, v_w_ssm_out, v_w_att_out, v_w_mix_out, v_norm_ffn, v_w_ffn_gate, v_w_ffn_up, v_w_ffn_down, v_norm_final):
    given = dict(locals())
    weights = {name: given[name][0] for name, _, _ in SHARDED}
    b, s, d = x.shape
    t = b * s

    stacking = _to_stacking(weights, SHARDED)
    conv_shape = dict((name, shape) for name, shape, _ in _stacking(SHARDED))["conv_w"]
    mat_local = _stack(stacking, BF16, skip=("conv_w",))
    conv_local = _pack_rows(stacking["conv_w"].reshape(-1), 8)
    mat_all, conv_all = _all_gather([mat_local, conv_local], "weights_all_gather")
    full = {name: v.reshape((-1,) + v.shape[2:]) for name, v in _unstack(mat_all, STACK_ORDER[:-2]).items()}
    w_in_all = _w_in_from_shards(mat_all, "w_in_from_shards")
    w_sec = {name: w_in_all[DPROJ_COLS[name]:DPROJ_COLS[name] + _round_up(size, LANES)] for name, size in IN_SPLIT}
    conv_size = conv_shape[0] * conv_shape[1]
    conv_taps = conv_all.reshape(N_DEV, -1)[:, :conv_size].reshape(N_DEV * conv_shape[0], conv_shape[1]).T

    g_mix, g_ffn, g_fin = norm_mix.reshape(1, d), norm_ffn.reshape(1, d), norm_final.reshape(1, d)
    bg_row = b_gate.reshape(1, 2 * d)
    convb_row = conv_b.reshape(1, SSM_CONV_DIM)
    ssmn_row = ssm_norm.reshape(1, SSM_D_INNER)
    dtb_row, alog_row = _lane_row(dt_bias), _lane_row(a_log)
    cosf, sinf = _rope_tables(s)

    x2d = x.reshape(t, d)
    h1 = _rmsnorm_fwd(x2d, g_mix, "norm_mix_fwd")
    proj = {name: _mm(h1, w_sec[name], mode="nt", name="in_proj_" + name) for name, _ in IN_SPLIT}
    xbc3 = proj["xbc"].reshape(b, s, SSM_CONV_DIM)
    xc = _conv_fwd(xbc3, conv_taps, convb_row, "conv_fwd")
    dtr3 = proj["dt"].reshape(b, s, DT_PAD)
    to_channels, to_heads = _head_masks()
    dskx = jnp.repeat(d_skip.reshape(-1).astype(F32), SSM_HEAD_DIM).reshape(1, SSM_D_INNER)
    y_ssd, h_states = _ssd_fwd(xc, dtr3, dtb_row, alog_row, dskx, to_channels, "ssd_fwd")
    y_ssd2 = y_ssd.reshape(t, SSM_D_INNER)
    ynorm = _gate_norm_fwd(y_ssd2, proj["z"], ssmn_row, "ssd_gate_norm_fwd")
    y_ssm = _mm(ynorm, full["w_ssm_out"], mode="nn", name="ssm_out_proj")

    qkv3 = proj["qkv"].reshape(b, s, ATT_QKV_DIM)
    qk_parts = _rope_fwd(qkv3, cosf, sinf, "rope_fwd")
    att_parts = [_att_fwd(qk_parts[gi], "att_fwd_%d" % r) for gi, r in enumerate(ATT_DILATIONS)]
    att, *lse_parts = _att_merge([o for o, _ in att_parts], [l_ for _, l_ in att_parts], "att_merge")
    att2 = att.reshape(t, ATT_OUT_DIM)
    y_att = _mm(att2, full["w_att_out"], mode="nt", name="att_out_proj")

    mixed = _mix_fwd(proj["gate"], bg_row, y_ssm, y_att, "mix_fwd")
    x2 = _mm(mixed, full["w_mix_out"], mode="nn", name="mix_out_proj", add=x2d)
    h2 = _rmsnorm_fwd(x2, g_ffn, "norm_ffn_fwd")
    gt = _mm(h2, full["w_ffn_gate"], mode="nt", name="ffn_gate_proj")
    up = _mm(h2, full["w_ffn_up"], mode="nt", name="ffn_up_proj")
    act = _swiglu_fwd(gt, up, "swiglu_fwd")
    x3 = _mm(act, full["w_ffn_down"], mode="nn", name="ffn_down_proj", add=x2)

    loss_row, dx3, dg_fin, dx3b = _loss_head(x3, g_fin, loss_target.reshape(t, d), "loss_head")
    grads = {}
    dact = _mm(dx3b, full["w_ffn_down"], mode="nt", name="ffn_down_dx")
    grads["w_ffn_down"] = _mm(act, dx3b, mode="tn", name="ffn_down_dw", out_dtype=BF16)
    dgt, dup = _swiglu_bwd(gt, up, dact, "swiglu_bwd")
    grads["w_ffn_gate"] = _mm(dgt, h2, mode="tn", name="ffn_gate_dw", out_dtype=BF16)
    grads["w_ffn_up"] = _mm(dup, h2, mode="tn", name="ffn_up_dw", out_dtype=BF16)
    dh2 = _mm(dgt, full["w_ffn_gate"], mode="nn", name="ffn_gate_dx")
    dh2 = _mm(dup, full["w_ffn_up"], mode="nn", name="ffn_up_dx", add=dh2)
    dx2, dg_ffn, dx2b = _rmsnorm_bwd(x2, g_ffn, dh2, dx3, "norm_ffn_bwd", with_bf16=True)

    dmixed = _mm(dx2b, full["w_mix_out"], mode="nt", name="mix_out_dx")
    grads["w_mix_out"] = _mm(mixed, dx2b, mode="tn", name="mix_out_dw", out_dtype=BF16)
    dys, dya, dproj, dbg = _mix_bwd(proj["gate"], bg_row, y_ssm, y_att, dmixed, "mix_bwd")

    grads["w_ssm_out"] = _mm(ynorm, dys, mode="tn", name="ssm_out_dw", out_dtype=BF16)
    dynorm = _mm(dys, full["w_ssm_out"], mode="nt", name="ssm_out_dx")
    dy_ssd, dproj, dssmn = _gate_norm_bwd(y_ssd2, proj["z"], ssmn_row, dynorm, dproj, "ssd_gate_norm_bwd")
    dxc, dproj, dalog, ddsk, ddtb = _ssd_bwd(xc, dtr3, dy_ssd.reshape(b, s, SSM_D_INNER), h_states, dtb_row, alog_row,
                                             dskx, to_channels, to_heads, dproj.reshape(b, s, DPROJ_WIDTH), "ssd_bwd")
    dproj, dconvw, dconvb = _conv_bwd(xbc3, dxc, conv_taps, convb_row, dproj, "conv_bwd")
    grads["conv_w"] = dconvw.T.astype(BF16)

    grads["w_att_out"] = _mm(dya, att2, mode="tn", name="att_out_dw", out_dtype=BF16)
    datt = _mm(dya, full["w_att_out"], mode="nn", name="att_out_dx").reshape(b, s, ATT_OUT_DIM)
    do_parts, dl_parts = _att_delta(att, datt, "att_delta")
    dqs, dks, dvs = [], [], []
    for gi, r in enumerate(ATT_DILATIONS):
        operands = (qk_parts[gi], do_parts[gi], lse_parts[gi], dl_parts[gi])
        dqs.append(_att_bwd_q(*operands, "att_bwd_q_%d" % r))
        dk_g, dv_g = _att_bwd_kv(*operands, "att_bwd_kv_%d" % r)
        dks.append(dk_g)
        dvs.append(dv_g)
    dproj = _rope_bwd(dqs, dks, dvs, cosf, sinf, dproj, "rope_bwd").reshape(t, DPROJ_WIDTH)

    dw_all = _mm(dproj, h1, mode="tn", name="in_proj_dw", out_dtype=BF16)
    dh1 = _mm(dproj, w_in_all, mode="nn", name="in_proj_dx")
    grad_x, dg_mix = _rmsnorm_bwd(x2d, g_mix, dh1, dx2, "norm_mix_bwd")

    head = _stack({name: v.reshape((N_DEV, -1, v.shape[-1])) for name, v in grads.items()}, BF16, head_only=True)
    slabs = _w_in_to_shards(dw_all, head, "grad_stacks")
    small = {"norm_mix": dg_mix, "b_gate": dbg, "conv_b": dconvb, "dt_bias": ddtb[:, :SSM_N_HEADS],
             "a_log": dalog[:, :SSM_N_HEADS], "d_skip": ddsk[:, :SSM_N_HEADS], "ssm_norm": dssmn,
             "norm_ffn": dg_ffn, "norm_final": dg_fin}
    core = lax.axis_index("c").astype(jnp.int32).reshape(1)
    chip_sums = _chip_sum(slabs, _pair_exchange(slabs, "grad_pair_exchange"), core, "grad_chip_sum")
    shared = _pack_replicated(small)
    shared = shared.at[LOSS_ROW, 0].set(loss_row[0, 0])
    got, got_small = _chip_exchange(chip_sums, shared, "grad_chip_exchange")

    def packed(prefix):
        vals = _to_stacking({name: given[prefix + name][0] for name, _, _ in SHARDED}, SHARDED)
        rep = {name: given[prefix + name] for name, _ in REPLICATED}
        return _stack(vals, F32), _pack_replicated(rep)

    (w_big, w_small), (m_big, m_small), (v_big, v_small) = packed(""), packed("m_"), packed("v_")
    big = _adamw(got, w_big, m_big, v_big, "adamw_sharded")
    sml = _adamw(got_small, w_small, m_small, v_small, "adamw_replicated")

    outs = [sml[0][LOSS_ROW, 0], grad_x.reshape(b, s, d)]
    rep_shapes = {name: given[name].shape for name, _ in REPLICATED}
    order = ["norm_mix", "w_in", "b_gate", "conv_w", "conv_b", "dt_bias", "a_log", "d_skip", "ssm_norm", "w_ssm_out",
             "w_att_out", "w_mix_out", "norm_ffn", "w_ffn_gate", "w_ffn_up", "w_ffn_down", "norm_final"]
    for big_k, sml_k in zip(big, sml):
        sharded = _to_stacking(_unstack(big_k, STACK_ORDER), SHARDED)
        rep = _unpack_replicated(sml_k, rep_shapes)
        for name in order:
            outs.append(sharded[name][None] if name in sharded else rep[name])
    return tuple(outs)
```

```python
import functools
import math

import jax
import jax.numpy as jnp
from jax import lax
from jax.experimental import pallas as pl
from jax.experimental.pallas import tpu as pltpu

F32 = jnp.float32
BF16 = jnp.bfloat16

N_DEV = 8
N_CHIPS = 4
D_MODEL = 1024
SSM_D_INNER = 2048
SSM_HEAD_DIM = 64
SSM_N_HEADS = 32
SSM_N_GROUPS = 4
SSM_HEADS_PER_GROUP = SSM_N_HEADS // SSM_N_GROUPS
SSM_D_STATE = 128
SSM_CONV = 4
SSM_CHUNK = 128
SSM_CONV_DIM = 3072
ATT_HEAD_DIM = 128
ATT_HEADS_PER_GROUP = 4
ATT_DILATIONS = (1, 4, 16)
ATT_N_HEADS = 12
ATT_QKV_DIM = 4608
ATT_OUT_DIM = 512
ATT_BLOCK = 128
ROPE_THETA = 10000.0
D_FF = 2816
IN_PROJ_DIM = 11808
EPS = 1e-6
LANES = 128
DT_PAD = LANES

DPROJ_COLS = {"qkv": 0, "z": 4608, "xbc": 6656, "dt": 9728, "gate": 10240}
DPROJ_DT_WIDTH = 512
DPROJ_WIDTH = 12288

ADAM_LR = 0.001
ADAM_B1 = 0.9
ADAM_B2 = 0.999
ADAM_EPS = 1e-08
ADAM_WD = 0.01
ADAM_STEP = 10

VMEM_LIMIT = 56 * 1024 * 1024
MESH = pl.DeviceIdType.MESH
NEG_INF = float("-inf")


def _pick(n, candidates):
    for c in candidates:
        if n % c == 0:
            return c
    return n


def _params(*sem):
    return pltpu.CompilerParams(dimension_semantics=sem, vmem_limit_bytes=VMEM_LIMIT)


def _sigmoid(x):
    return 1.0 / (1.0 + jnp.exp(-x))


def _softplus(x):
    return jnp.maximum(x, 0.0) + jnp.log(1.0 + jnp.exp(-jnp.abs(x)))


def _dot(a, b, dims):
    return lax.dot_general(a.astype(BF16), b.astype(BF16), (dims, ((), ())), preferred_element_type=F32)


def _nn(a, b):
    return _dot(a, b, ((1,), (0,)))


def _nt(a, b):
    return _dot(a, b, ((1,), (1,)))


def _tn(a, b):
    return _dot(a, b, ((0,), (0,)))


def _split3(v):
    hi = v.astype(BF16)
    r1 = v - hi.astype(F32)
    mid = r1.astype(BF16)
    lo = (r1 - mid.astype(F32)).astype(BF16)
    return hi, mid, lo


def _mask_nn(mask, v):
    mb = mask.astype(BF16)
    hi, mid, lo = _split3(v)
    return _nn(mb, hi) + (_nn(mb, mid) + _nn(mb, lo))


MM_VMEM_BUDGET = 40 * 1024 * 1024
MM_FULL_K = 2816


def _mm_tiles(m, n, k, a_bytes, b_bytes, o_bytes, has_add):
    tk = k if k <= MM_FULL_K else _pick(k, (2048, 1024, 512, 256, 128))
    tn = 1408 if (n > 1024 and n % 1408 == 0) else _pick(n, (1024, 768, 512, 384, 256, 128))
    for tm in (1408, 1024, 768, 512, 384, 256, 128):
        if m % tm:
            continue
        buffers = 2 * (tm * tk * a_bytes + tk * tn * b_bytes + tm * tn * (o_bytes + (4 if has_add else 0)))
        if tk < k:
            buffers += tm * tn * 4
        if buffers <= MM_VMEM_BUDGET:
            return tm, tn, tk
    return _pick(m, (128,)), tn, tk


def _mm(a, b, *, mode, name, out_dtype=F32, add=None):
    if mode == "nn":
        (m, k), n = a.shape, b.shape[1]
    elif mode == "nt":
        (m, k), n = a.shape, b.shape[0]
    else:
        (k, m), n = a.shape, b.shape[1]
    has_add = add is not None
    tm, tn, tk = _mm_tiles(m, n, k, a.dtype.itemsize, b.dtype.itemsize, jnp.dtype(out_dtype).itemsize, has_add)
    nk = k // tk
    dims = {"nn": ((1,), (0,)), "nt": ((1,), (1,)), "tn": ((0,), (0,))}[mode]
    a_spec = {"nn": pl.BlockSpec((tm, tk), lambda i, j, kk: (i, kk)),
              "nt": pl.BlockSpec((tm, tk), lambda i, j, kk: (i, kk)),
              "tn": pl.BlockSpec((tk, tm), lambda i, j, kk: (kk, i))}[mode]
    b_spec = {"nn": pl.BlockSpec((tk, tn), lambda i, j, kk: (kk, j)),
              "nt": pl.BlockSpec((tn, tk), lambda i, j, kk: (j, kk)),
              "tn": pl.BlockSpec((tk, tn), lambda i, j, kk: (kk, j))}[mode]
    o_spec = pl.BlockSpec((tm, tn), lambda i, j, kk: (i, j))

    def finish(r, c_ref, o_ref):
        if has_add:
            r = r + c_ref[...]
        o_ref[...] = r.astype(out_dtype)

    def body_one(*refs):
        a_ref, b_ref = refs[:2]
        finish(_dot(a_ref[...], b_ref[...], dims), refs[2] if has_add else None, refs[-1])

    def body_acc(*refs):
        a_ref, b_ref = refs[:2]
        o_ref, acc = refs[-2:]
        kk = pl.program_id(2)

        @pl.when(kk == 0)
        def _():
            acc[...] = jnp.zeros_like(acc)

        acc[...] += _dot(a_ref[...], b_ref[...], dims)

        @pl.when(kk == nk - 1)
        def _():
            finish(acc[...], refs[2] if has_add else None, o_ref)

    in_specs = [a_spec, b_spec] + ([o_spec] if has_add else [])
    args = (a, b) + ((add,) if has_add else ())
    return pl.pallas_call(
        body_one if nk == 1 else body_acc, name=name, grid=(m // tm, n // tn, nk),
        in_specs=in_specs, out_specs=o_spec,
        out_shape=jax.ShapeDtypeStruct((m, n), out_dtype),
        scratch_shapes=[] if nk == 1 else [pltpu.VMEM((tm, tn), F32)],
        compiler_params=_params("parallel", "parallel", "arbitrary"),
    )(*args)


def _rmsnorm_fwd(x, g, name):
    t, d = x.shape
    tm = _pick(t, (512, 256, 128))

    def body(x_ref, g_ref, o_ref):
        xv = x_ref[...]
        r = lax.rsqrt(jnp.mean(xv * xv, axis=-1, keepdims=True) + EPS)
        o_ref[...] = ((xv * r) * g_ref[...]).astype(BF16)

    return pl.pallas_call(
        body, name=name, grid=(t // tm,),
        in_specs=[pl.BlockSpec((tm, d), lambda i: (i, 0)), pl.BlockSpec((1, d), lambda i: (0, 0))],
        out_specs=pl.BlockSpec((tm, d), lambda i: (i, 0)),
        out_shape=jax.ShapeDtypeStruct((t, d), BF16),
        compiler_params=_params("parallel"),
    )(x, g)


def _rmsnorm_bwd(x, g, dh, dres, name, with_bf16=False):
    t, d = x.shape
    tm = _pick(t, (512, 256, 128))

    def body(x_ref, g_ref, dh_ref, dres_ref, dx_ref, dg_ref, *dxb_ref):
        @pl.when(pl.program_id(0) == 0)
        def _():
            dg_ref[...] = jnp.zeros_like(dg_ref)

        xv = x_ref[...]
        r = lax.rsqrt(jnp.mean(xv * xv, axis=-1, keepdims=True) + EPS)
        xhat = xv * r
        dhv = dh_ref[...]
        dyg = dhv * g_ref[...]
        dx = dres_ref[...] + r * (dyg - xhat * jnp.mean(dyg * xhat, axis=-1, keepdims=True))
        dx_ref[...] = dx
        if with_bf16:
            dxb_ref[0][...] = dx.astype(BF16)
        dg_ref[...] += jnp.sum(dhv * xhat, axis=0, keepdims=True)

    row = pl.BlockSpec((tm, d), lambda i: (i, 0))
    vec = pl.BlockSpec((1, d), lambda i: (0, 0))
    extra = with_bf16 * [jax.ShapeDtypeStruct((t, d), BF16)]
    return pl.pallas_call(
        body, name=name, grid=(t // tm,),
        in_specs=[row, vec, row, row], out_specs=[row, vec] + with_bf16 * [row],
        out_shape=[jax.ShapeDtypeStruct((t, d), F32), jax.ShapeDtypeStruct((1, d), F32)] + extra,
        compiler_params=_params("arbitrary"),
    )(x, g, dh, dres)


def _loss_head(x, g, target, name):
    t, d = x.shape
    tm = _pick(t, (512, 256, 128))

    def body(x_ref, g_ref, t_ref, loss_ref, dx_ref, dg_ref, dxb_ref):
        @pl.when(pl.program_id(0) == 0)
        def _():
            dg_ref[...] = jnp.zeros_like(dg_ref)
            loss_ref[...] = jnp.zeros_like(loss_ref)

        xv = x_ref[...]
        gv = g_ref[...]
        r = lax.rsqrt(jnp.mean(xv * xv, axis=-1, keepdims=True) + EPS)
        xhat = xv * r
        err = xhat * gv - t_ref[...]
        loss_ref[...] += jnp.sum(err * err) * (0.5 / d)
        dy = err * (1.0 / d)
        dyg = dy * gv
        dx = r * (dyg - xhat * jnp.mean(dyg * xhat, axis=-1, keepdims=True))
        dx_ref[...] = dx
        dxb_ref[...] = dx.astype(BF16)
        dg_ref[...] += jnp.sum(dy * xhat, axis=0, keepdims=True)

    row = pl.BlockSpec((tm, d), lambda i: (i, 0))
    vec = pl.BlockSpec((1, d), lambda i: (0, 0))
    return pl.pallas_call(
        body, name=name, grid=(t // tm,),
        in_specs=[row, vec, row],
        out_specs=[pl.BlockSpec((1, LANES), lambda i: (0, 0)), row, vec, row],
        out_shape=[jax.ShapeDtypeStruct((1, LANES), F32), jax.ShapeDtypeStruct((t, d), F32),
                   jax.ShapeDtypeStruct((1, d), F32), jax.ShapeDtypeStruct((t, d), BF16)],
        compiler_params=_params("arbitrary"),
    )(x, g, target)


CONV_HALO = 8
CONV_ROWS = 64


def _conv_taps(window, wv, bv):
    acc = bv + wv[SSM_CONV - 1:SSM_CONV, :] * window(0)
    for sh in range(1, SSM_CONV):
        kidx = SSM_CONV - 1 - sh
        acc = acc + wv[kidx:kidx + 1, :] * window(sh)
    return acc


def _conv_fwd(u, w, bias, name):
    b, s, c = u.shape
    rows = CONV_ROWS

    def body(u_ref, w_ref, b_ref, o_ref, ext):
        ext[0:CONV_HALO, :] = jnp.zeros((CONV_HALO, LANES), F32)
        ext[CONV_HALO:, :] = u_ref[...]
        wv, bv = w_ref[...], b_ref[...]
        for r0 in range(0, s, rows):
            acc = _conv_taps(lambda sh: ext[CONV_HALO + r0 - sh:CONV_HALO + r0 - sh + rows, :], wv, bv)
            o_ref[r0:r0 + rows, :] = acc * _sigmoid(acc)

    strip = pl.BlockSpec((None, s, LANES), lambda bi, j: (bi, 0, j))
    return pl.pallas_call(
        body, name=name, grid=(b, c // LANES),
        in_specs=[strip, pl.BlockSpec((SSM_CONV, LANES), lambda bi, j: (0, j)),
                  pl.BlockSpec((1, LANES), lambda bi, j: (0, j))],
        out_specs=strip, out_shape=jax.ShapeDtypeStruct((b, s, c), F32),
        scratch_shapes=[pltpu.VMEM((CONV_HALO + s, LANES), F32)],
        compiler_params=_params("parallel", "parallel"),
    )(u, w, bias)


def _conv_bwd(u, dout, w, bias, dproj, name):
    b, s, c = u.shape
    rows = CONV_ROWS

    def fold(v):
        return jnp.sum(v.reshape(rows // CONV_HALO, CONV_HALO, LANES), axis=0)

    def body(u_ref, d_ref, w_ref, b_ref, buf_ref, du_ref, dw_ref, db_ref, ext, dpre):
        @pl.when(pl.program_id(1) == 0)
        def _():
            dw_ref[...] = jnp.zeros_like(dw_ref)
            db_ref[...] = jnp.zeros_like(db_ref)

        ext[0:CONV_HALO, :] = jnp.zeros((CONV_HALO, LANES), F32)
        ext[CONV_HALO:, :] = u_ref[...]
        dpre[s:, :] = jnp.zeros((CONV_HALO, LANES), F32)
        wv, bv = w_ref[...], b_ref[...]
        sums = [jnp.zeros((CONV_HALO, LANES), F32)] * (SSM_CONV + 1)
        for r0 in range(0, s, rows):
            window = lambda sh: ext[CONV_HALO + r0 - sh:CONV_HALO + r0 - sh + rows, :]
            acc = _conv_taps(window, wv, bv)
            sg = _sigmoid(acc)
            dp = d_ref[r0:r0 + rows, :] * (sg * (1.0 + acc * (1.0 - sg)))
            dpre[r0:r0 + rows, :] = dp
            taps = [sums[SSM_CONV - 1 - sh] + fold(dp * window(sh)) for sh in range(SSM_CONV)]
            sums = taps[::-1] + [sums[SSM_CONV] + fold(dp)]
        for r0 in range(0, s, rows):
            du = wv[SSM_CONV - 1:SSM_CONV, :] * dpre[r0:r0 + rows, :]
            for sh in range(1, SSM_CONV):
                kidx = SSM_CONV - 1 - sh
                du = du + wv[kidx:kidx + 1, :] * dpre[r0 + sh:r0 + sh + rows, :]
            du_ref[r0:r0 + rows, :] = du.astype(BF16)
        for kidx in range(SSM_CONV):
            dw_ref[kidx:kidx + 1, :] += jnp.sum(sums[kidx], axis=0, keepdims=True)
        db_ref[...] += jnp.sum(sums[SSM_CONV], axis=0, keepdims=True)

    strip = pl.BlockSpec((None, s, LANES), lambda j, bi: (bi, 0, j))
    taps = pl.BlockSpec((SSM_CONV, LANES), lambda j, bi: (0, j))
    vec = pl.BlockSpec((1, LANES), lambda j, bi: (0, j))
    du_cols = pl.BlockSpec((None, s, LANES), lambda j, bi: (bi, 0, DPROJ_COLS["xbc"] // LANES + j))
    return pl.pallas_call(
        body, name=name, grid=(c // LANES, b),
        in_specs=[strip, strip, taps, vec, pl.BlockSpec(memory_space=pl.ANY)], out_specs=[du_cols, taps, vec],
        input_output_aliases={4: 0},
        out_shape=[jax.ShapeDtypeStruct(dproj.shape, dproj.dtype), jax.ShapeDtypeStruct((SSM_CONV, c), F32),
                   jax.ShapeDtypeStruct((1, c), F32)],
        scratch_shapes=[pltpu.VMEM((CONV_HALO + s, LANES), F32), pltpu.VMEM((s + CONV_HALO, LANES), F32)],
        compiler_params=_params("parallel", "arbitrary"),
    )(u, dout, w, bias, dproj)


def _ssd_chunk_terms(dtr_ref, bias_ref, alog_ref):
    q = SSM_CHUNK
    dt = _softplus(dtr_ref[...] + bias_ref[...])
    a_neg = -jnp.exp(alog_ref[...])
    row = lax.broadcasted_iota(jnp.int32, (q, q), 0)
    col = lax.broadcasted_iota(jnp.int32, (q, q), 1)
    lower = row >= col
    s = _mask_nn(lower, dt * a_neg)
    return dt, a_neg, s, s.T, lower


def _head_masks():
    heads = jnp.arange(LANES)[:, None]
    chans = jnp.arange(SSM_D_INNER)[None, :]
    to_channels = (chans // SSM_HEAD_DIM == heads).astype(BF16)
    return to_channels, to_channels.T


def _per_channel(v, to_channels):
    hi = v.astype(BF16)
    lo = (v - hi.astype(F32)).astype(BF16)
    return _nn(hi, to_channels) + _nn(lo, to_channels)


def _per_head(v, to_heads):
    hi = v.astype(BF16)
    lo = (v - hi.astype(F32)).astype(BF16)
    return _nn(hi, to_heads) + _nn(lo, to_heads)


def _decay_terms_per_channel(dt, s_col, to_channels):
    q = SSM_CHUNK
    tot = s_col[q - 1:q, :]
    stacked = jnp.concatenate([dt, jnp.exp(s_col), jnp.exp(tot - s_col)], axis=0)
    wide = _per_channel(stacked, to_channels)
    dtx, esx, decx = wide[:q], wide[q:2 * q], wide[2 * q:]
    return dtx, esx, decx, esx[0:1, :] * decx[0:1, :]


SSM_PAIRS_PER_GROUP = SSM_HEADS_PER_GROUP // 2
SSM_GROUP_CHANNELS = SSM_HEADS_PER_GROUP * SSM_HEAD_DIM


def _split_pair(v):
    first = lax.broadcasted_iota(jnp.int32, v.shape, 1) < SSM_HEAD_DIM
    return jnp.concatenate([jnp.where(first, v, 0.0), jnp.where(first, 0.0, v)], axis=0)


def _ssd_fwd(xc, dtr, dt_bias, a_log, dskx, to_channels, name):
    b, s, _ = xc.shape
    q = SSM_CHUNK
    nc = s // q
    n, gc = SSM_D_STATE, SSM_GROUP_CHANNELS

    def body(xc_ref, dtr_ref, bias_ref, alog_ref, dsk_ref, tc_ref, y_ref, hs_ref, h_scr):
        @pl.when(pl.program_id(1) == 0)
        def _():
            h_scr[...] = jnp.zeros_like(h_scr)

        dt, _, s_col, s_row, lower = _ssd_chunk_terms(dtr_ref, bias_ref, alog_ref)
        dtx, esx, decx, etotx = _decay_terms_per_channel(dt, s_col, tc_ref[...])
        x = xc_ref[:, :SSM_D_INNER]
        xdt = x * dtx
        xdec = xdt * decx
        skip = dsk_ref[...] * x
        for g in range(SSM_N_GROUPS):
            bg = xc_ref[:, SSM_D_INNER + n * g:SSM_D_INNER + n * (g + 1)].astype(BF16)
            cg = xc_ref[:, SSM_D_INNER + n * (SSM_N_GROUPS + g):SSM_D_INNER + n * (SSM_N_GROUPS + g + 1)].astype(BF16)
            gsl = slice(gc * g, gc * (g + 1))
            gm = _nt(cg, bg)
            hgt = h_scr[:, gsl]
            hs_ref[:, gsl] = hgt
            y_off = esx[:, gsl] * _nn(cg, hgt)
            h_scr[:, gsl] = etotx[:, gsl] * hgt + _tn(bg, xdec[:, gsl])
            for k in range(SSM_PAIRS_PER_GROUP):
                h0 = g * SSM_HEADS_PER_GROUP + 2 * k
                lo = gc * g + LANES * k
                ms = []
                for h in (h0, h0 + 1):
                    lm = jnp.exp(jnp.where(lower, s_col[:, h:h + 1] - s_row[h:h + 1, :], NEG_INF))
                    ms.append((gm * lm).astype(BF16))
                y_diag = _nn(jnp.concatenate(ms, axis=1), _split_pair(xdt[:, lo:lo + LANES]))
                y_ref[:, lo:lo + LANES] = y_diag + y_off[:, LANES * k:LANES * (k + 1)] + skip[:, lo:lo + LANES]

    vec = pl.BlockSpec((1, LANES), lambda bi, c: (0, 0))
    return pl.pallas_call(
        body, name=name, grid=(b, nc),
        in_specs=[pl.BlockSpec((None, q, SSM_CONV_DIM), lambda bi, c: (bi, c, 0)),
                  pl.BlockSpec((None, q, LANES), lambda bi, c: (bi, c, 0)), vec, vec,
                  pl.BlockSpec((1, SSM_D_INNER), lambda bi, c: (0, 0)),
                  pl.BlockSpec((LANES, SSM_D_INNER), lambda bi, c: (0, 0))],
        out_specs=[pl.BlockSpec((None, q, SSM_D_INNER), lambda bi, c: (bi, c, 0)),
                   pl.BlockSpec((None, None, n, SSM_D_INNER), lambda bi, c: (bi, c, 0, 0))],
        out_shape=[jax.ShapeDtypeStruct((b, s, SSM_D_INNER), F32),
                   jax.ShapeDtypeStruct((b, nc, n, SSM_D_INNER), F32)],
        scratch_shapes=[pltpu.VMEM((n, SSM_D_INNER), F32)],
        compiler_params=_params("parallel", "arbitrary"),
    )(xc, dtr, dt_bias, a_log, dskx, to_channels)


def _ssd_bwd(xc, dtr, dy, hs, dt_bias, a_log, dskx, to_channels, to_heads, dproj, name):
    b, s, _ = xc.shape
    q = SSM_CHUNK
    nc = s // q
    n, gc = SSM_D_STATE, SSM_GROUP_CHANNELS

    def colsum(v):
        return jnp.sum(v, axis=0, keepdims=True)

    def body(xc_ref, dtr_ref, dy_ref, hs_ref, bias_ref, alog_ref, dsk_ref, tc_ref, th_ref, buf_ref,
             dxc_ref, ddtr_ref, dalog_ref, ddsk_ref, dbias_ref, dh_scr, dxs_scr, dxd_scr, w_scr, dst_scr, rows_scr):
        ci = pl.program_id(1)

        @pl.when(ci == 0)
        def _():
            dh_scr[...] = jnp.zeros_like(dh_scr)

        @pl.when(jnp.logical_and(pl.program_id(0) == 0, ci == 0))
        def _():
            dalog_ref[...] = jnp.zeros_like(dalog_ref)
            ddsk_ref[...] = jnp.zeros_like(ddsk_ref)
            dbias_ref[...] = jnp.zeros_like(dbias_ref)
            dst_scr[...] = jnp.zeros_like(dst_scr)

        dt, a_neg, s_col, s_row, lower = _ssd_chunk_terms(dtr_ref, bias_ref, alog_ref)
        upper = jnp.logical_not(lower) | (lax.broadcasted_iota(jnp.int32, (q, q), 0)
                                          == lax.broadcasted_iota(jnp.int32, (q, q), 1))
        dtx, esx, decx, etotx = _decay_terms_per_channel(dt, s_col, tc_ref[...])
        x = xc_ref[:, :SSM_D_INNER]
        dyv = dy_ref[...]
        xdt = x * dtx
        xdec = xdt * decx
        dw = esx * dyv
        rows_scr[...] = jnp.zeros_like(rows_scr)
        for g in range(SSM_N_GROUPS):
            b_lo = SSM_D_INNER + n * g
            c_lo = SSM_D_INNER + n * (SSM_N_GROUPS + g)
            bg = xc_ref[:, b_lo:b_lo + n].astype(BF16)
            cg = xc_ref[:, c_lo:c_lo + n].astype(BF16)
            gsl = slice(gc * g, gc * (g + 1))
            gm = _nt(cg, bg)
            gmt = _nt(bg, cg)
            hgt = hs_ref[:, gsl]
            dhgt = dh_scr[:, gsl]
            w_scr[:, gsl] = _nn(cg, hgt)
            dcg = _nt(dw[:, gsl], hgt)
            dxs = decx[:, gsl] * _nn(bg, dhgt)
            dxs_scr[:, gsl] = dxs
            dbg = _nt(xdec[:, gsl], dhgt)
            rows_scr[2:3, gsl] = colsum(dhgt * hgt)
            dh_scr[:, gsl] = _tn(cg, dw[:, gsl]) + etotx[:, gsl] * dhgt
            dg = jnp.zeros((q, q), F32)
            dgt = jnp.zeros((q, q), F32)
            for k in range(SSM_PAIRS_PER_GROUP):
                h0 = g * SSM_HEADS_PER_GROUP + 2 * k
                lo = gc * g + LANES * k
                xp = xdt[:, lo:lo + LANES]
                dyp = dyv[:, lo:lo + LANES]
                dy2 = _split_pair(dyp)
                dm2 = _nt(dy2, xp)
                dmt2 = _nt(_split_pair(xp), dyp)
                mts = []
                for i, h in enumerate((h0, h0 + 1)):
                    lm = jnp.exp(jnp.where(lower, s_col[:, h:h + 1] - s_row[h:h + 1, :], NEG_INF))
                    lmt = jnp.exp(jnp.where(upper, s_row[h:h + 1, :] - s_col[:, h:h + 1], NEG_INF))
                    dm = dm2[q * i:q * (i + 1), :]
                    dmt = dmt2[q * i:q * (i + 1), :]
                    dg = dg + dm * lm
                    dgt = dgt + dmt * lmt
                    mt = gmt * lmt
                    dst_scr[h:h + 1, :] = colsum(dmt * mt) - colsum(dm * (gm * lm))
                    mts.append(mt.astype(BF16))
                dxd_scr[:, lo:lo + LANES] = _nn(jnp.concatenate(mts, axis=1), dy2)
            dxc_ref[:, b_lo:b_lo + n] = dbg + _nn(dgt, cg)
            dxc_ref[:, c_lo:c_lo + n] = dcg + _nn(dg, bg)
        dxs = dxs_scr[...]
        dxdt = dxd_scr[...] + dxs
        dxc_ref[:, :SSM_D_INNER] = dxdt * dtx + dsk_ref[...] * dyv
        state_part = xdt * dxs
        rows_scr[0:1, :] = colsum(dyv * x)
        rows_scr[1:2, :] = colsum(state_part)
        th = th_ref[...]
        per_head = _per_head(jnp.concatenate([dw * w_scr[...] - state_part, dxdt * x], axis=0), th)
        r_ds, r_dt = per_head[:q], per_head[q:]
        sums = _per_head(rows_scr[...], th)
        etot = jnp.exp(s_col[q - 1:q, :])
        dtot = sums[1:2, :] + etot * sums[2:3, :]
        last = lax.broadcasted_iota(jnp.int32, (q, LANES), 0) == q - 1
        ds = dst_scr[...].T + r_ds + jnp.where(last, dtot, 0.0)
        da = _mask_nn(upper, ds)
        ddt = da * a_neg + r_dt
        live = lax.broadcasted_iota(jnp.int32, (1, LANES), 1) < SSM_N_HEADS
        sg = _sigmoid(dtr_ref[...] + bias_ref[...])
        ddtr = jnp.where(live, ddt * sg, 0.0)
        ddtr_ref[:, :LANES] = ddtr.astype(BF16)
        ddtr_ref[:, LANES:] = jnp.zeros((q, DPROJ_DT_WIDTH - LANES), BF16)
        dalog_ref[...] += jnp.where(live, colsum(da * dt) * a_neg, 0.0)
        ddsk_ref[...] += jnp.where(live, sums[0:1, :], 0.0)
        dbias_ref[...] += colsum(ddtr)

    rev = lambda bi, c: (bi, nc - 1 - c, 0)
    vec = pl.BlockSpec((1, LANES), lambda bi, c: (0, 0))
    wide = pl.BlockSpec((None, q, SSM_D_INNER), rev)
    return pl.pallas_call(
        body, name=name, grid=(b, nc),
        in_specs=[pl.BlockSpec((None, q, SSM_CONV_DIM), rev), pl.BlockSpec((None, q, LANES), rev), wide,
                  pl.BlockSpec((None, None, n, SSM_D_INNER), lambda bi, c: (bi, nc - 1 - c, 0, 0)),
                  vec, vec, pl.BlockSpec((1, SSM_D_INNER), lambda bi, c: (0, 0)),
                  pl.BlockSpec((LANES, SSM_D_INNER), lambda bi, c: (0, 0)),
                  pl.BlockSpec((SSM_D_INNER, LANES), lambda bi, c: (0, 0)),
                  pl.BlockSpec(memory_space=pl.ANY)],
        out_specs=[pl.BlockSpec((None, q, SSM_CONV_DIM), rev),
                   pl.BlockSpec((None, q, DPROJ_DT_WIDTH),
                                lambda bi, c: (bi, nc - 1 - c, DPROJ_COLS["dt"] // DPROJ_DT_WIDTH)), vec, vec, vec],
        input_output_aliases={9: 1},
        out_shape=[jax.ShapeDtypeStruct((b, s, SSM_CONV_DIM), F32), jax.ShapeDtypeStruct(dproj.shape, dproj.dtype),
                   jax.ShapeDtypeStruct((1, LANES), F32), jax.ShapeDtypeStruct((1, LANES), F32),
                   jax.ShapeDtypeStruct((1, LANES), F32)],
        scratch_shapes=[pltpu.VMEM((n, SSM_D_INNER), F32)] + [pltpu.VMEM((q, SSM_D_INNER), F32)] * 3
        + [pltpu.VMEM((LANES, q), F32), pltpu.VMEM((8, SSM_D_INNER), F32)],
        compiler_params=_params("arbitrary", "arbitrary"),
    )(xc, dtr, dy, hs, dt_bias, a_log, dskx, to_channels, to_heads, dproj)


SSM_GROUP_WIDTH = SSM_D_INNER // SSM_N_GROUPS


def _gate_norm_fwd(y, z, w, name):
    t, d = y.shape
    tm = _pick(t, (256, 128))

    def body(y_ref, z_ref, w_ref, o_ref):
        for g in range(SSM_N_GROUPS):
            sl = slice(SSM_GROUP_WIDTH * g, SSM_GROUP_WIDTH * (g + 1))
            zv = z_ref[:, sl]
            u = y_ref[:, sl] * (zv * _sigmoid(zv))
            r = lax.rsqrt(jnp.mean(u * u, axis=-1, keepdims=True) + EPS)
            o_ref[:, sl] = ((u * r) * w_ref[:, sl]).astype(BF16)

    row = pl.BlockSpec((tm, d), lambda i: (i, 0))
    return pl.pallas_call(
        body, name=name, grid=(t // tm,),
        in_specs=[row, row, pl.BlockSpec((1, d), lambda i: (0, 0))], out_specs=row,
        out_shape=jax.ShapeDtypeStruct((t, d), BF16),
        compiler_params=_params("parallel"),
    )(y, z, w)


def _gate_norm_bwd(y, z, w, dout, dproj, name):
    t, d = y.shape
    gw = SSM_GROUP_WIDTH
    tm = _pick(t, (1024, 512, 256, 128))

    def body(y_ref, z_ref, w_ref, do_ref, buf_ref, dy_ref, dz_ref, dw_ref):
        @pl.when(pl.program_id(1) == 0)
        def _():
            dw_ref[...] = jnp.zeros_like(dw_ref)

        zv = z_ref[...]
        yv = y_ref[...]
        sg = _sigmoid(zv)
        silu = zv * sg
        u = yv * silu
        r = lax.rsqrt(jnp.mean(u * u, axis=-1, keepdims=True) + EPS)
        uh = u * r
        dov = do_ref[...]
        dw_ref[...] += jnp.sum(dov * uh, axis=0, keepdims=True)
        dyg = dov * w_ref[...]
        du = r * (dyg - uh * jnp.mean(dyg * uh, axis=-1, keepdims=True))
        dy_ref[...] = du * silu
        dz_ref[...] = (du * yv * (sg * (1.0 + zv * (1.0 - sg)))).astype(BF16)

    tile = pl.BlockSpec((tm, gw), lambda g, i: (i, g))
    vec = pl.BlockSpec((1, gw), lambda g, i: (0, g))
    z_cols = pl.BlockSpec((tm, gw), lambda g, i: (i, DPROJ_COLS["z"] // gw + g))
    return pl.pallas_call(
        body, name=name, grid=(SSM_N_GROUPS, t // tm),
        in_specs=[tile, tile, vec, tile, pl.BlockSpec(memory_space=pl.ANY)], out_specs=[tile, z_cols, vec],
        out_shape=[jax.ShapeDtypeStruct((t, d), F32), jax.ShapeDtypeStruct(dproj.shape, dproj.dtype),
                   jax.ShapeDtypeStruct((1, d), F32)],
        input_output_aliases={4: 1},
        compiler_params=_params("parallel", "arbitrary"),
    )(y, z, w, dout, dproj)


def _rope_tables(s):
    half = ATT_HEAD_DIM // 2
    inv = ROPE_THETA ** (-jnp.arange(half, dtype=F32) / half)
    ang = jnp.arange(s).astype(F32)[:, None] * inv[None, :]
    cos, sin = jnp.cos(ang), jnp.sin(ang)
    return jnp.concatenate([cos, cos], axis=-1), jnp.concatenate([-sin, sin], axis=-1)


ATT_TILE = 256


def _by_residue_spec(r, width):
    return pl.BlockSpec((None, r, ATT_TILE // r, width), lambda bi, i: (bi, 0, i, 0))


def _to_residues(tile, stage, r, store):
    if r == 1:
        store(0, tile)
        return
    stage[...] = tile
    for ri in range(r):
        store(ri, stage[pl.ds(ri, ATT_TILE // r, stride=r), :])


def _from_residues(load, stage, r):
    if r == 1:
        return load(0)
    for ri in range(r):
        stage[pl.ds(ri, ATT_TILE // r, stride=r), :] = load(ri)
    return stage[...]


def _rope_fwd(qkv, cosf, sinf, name):
    b, s, w = qkv.shape
    ts, d, gw = ATT_TILE, ATT_HEAD_DIM, ATT_OUT_DIM

    def body(x_ref, c_ref, s_ref, *rest):
        outs, stage = rest[:-1], rest[-1]
        cv, sv = c_ref[...], s_ref[...]
        for kind in range(3):
            for gi, r in enumerate(ATT_DILATIONS):
                for j in range(ATT_HEADS_PER_GROUP):
                    src = d * (kind * ATT_N_HEADS + gi * ATT_HEADS_PER_GROUP + j)
                    dst = slice(kind * gw + d * j, kind * gw + d * (j + 1))
                    tv = x_ref[:, src:src + d]
                    if kind < 2:
                        tv = tv * cv + pltpu.roll(tv, d // 2, 1) * sv

                    def store(ri, rows, o_ref=outs[gi], dst=dst):
                        o_ref[ri, :, dst] = rows.astype(BF16)

                    _to_residues(tv, stage, r, store)

    tab = pl.BlockSpec((ts, d), lambda bi, i: (i, 0))
    return pl.pallas_call(
        body, name=name, grid=(b, s // ts),
        in_specs=[pl.BlockSpec((None, ts, w), lambda bi, i: (bi, i, 0)), tab, tab],
        out_specs=[_by_residue_spec(r, 3 * gw) for r in ATT_DILATIONS],
        out_shape=[jax.ShapeDtypeStruct((b, r, s // r, 3 * gw), BF16) for r in ATT_DILATIONS],
        scratch_shapes=[pltpu.VMEM((ts, d), F32)],
        compiler_params=_params("parallel", "parallel"),
    )(qkv, cosf, sinf)


def _rope_bwd(dq, dk, dv, cosf, sinf, dproj, name):
    n_pat = len(ATT_DILATIONS)
    b, _, s, gw = dq[0].shape
    ts, d = ATT_TILE, ATT_HEAD_DIM

    def body(*refs):
        ins, (c_ref, s_ref, _, o_ref, stage) = refs[:3 * n_pat], refs[3 * n_pat:]
        cv, sv = c_ref[...], s_ref[...]
        for kind in range(3):
            for gi, r in enumerate(ATT_DILATIONS):
                src = ins[kind * n_pat + gi]
                for j in range(ATT_HEADS_PER_GROUP):
                    tv = _from_residues(lambda ri, src=src, j=j: src[ri, :, d * j:d * (j + 1)], stage, r)
                    if kind < 2:
                        tv = tv * cv + pltpu.roll(tv * sv, d // 2, 1)
                    lo = d * (kind * ATT_N_HEADS + gi * ATT_HEADS_PER_GROUP + j)
                    o_ref[:, lo:lo + d] = tv.astype(BF16)

    tab = pl.BlockSpec((ts, d), lambda bi, i: (i, 0))
    parts = [_by_residue_spec(r, gw) for r in ATT_DILATIONS]
    return pl.pallas_call(
        body, name=name, grid=(b, s // ts), in_specs=parts * 3 + [tab, tab, pl.BlockSpec(memory_space=pl.ANY)],
        out_specs=pl.BlockSpec((None, ts, ATT_QKV_DIM), lambda bi, i: (bi, i, DPROJ_COLS["qkv"] // ATT_QKV_DIM)),
        out_shape=jax.ShapeDtypeStruct(dproj.shape, dproj.dtype),
        input_output_aliases={3 * n_pat + 2: 0},
        scratch_shapes=[pltpu.VMEM((ts, d), F32)],
        compiler_params=_params("parallel", "parallel"),
    )(*dq, *dk, *dv, cosf, sinf, dproj)


ATT_SCALE = ATT_HEAD_DIM ** -0.5
ATT_STEP = 2 * ATT_BLOCK


def _att_spec(col):
    return pl.BlockSpec((None, None, ATT_STEP, ATT_OUT_DIM), lambda bi, ri, i: (bi, ri, i, col))


def _att_edge_spec(col, side, n_steps):
    def index(bi, ri, i):
        blk = 2 * i - 1 if side < 0 else 2 * i + 2
        return (bi, ri, jnp.clip(blk, 0, 2 * n_steps - 1), col)
    return pl.BlockSpec((None, None, ATT_BLOCK, ATT_OUT_DIM), index)


def _band_mask(shape, q_axis, has_prev):
    qi = lax.broadcasted_iota(jnp.int32, shape, q_axis)
    kj = lax.broadcasted_iota(jnp.int32, shape, 1 - q_axis)
    dist = qi + ATT_BLOCK - kj
    return (dist >= 0) & (dist <= ATT_BLOCK) & (has_prev | (kj >= ATT_BLOCK))


def _att_fwd(qkr, name):
    b, r, l, _ = qkr.shape
    nb = l // ATT_STEP
    d = ATT_HEAD_DIM

    def body(q_ref, kp_ref, k_ref, vp_ref, v_ref, o_ref, lse_ref):
        mask = _band_mask((ATT_STEP, ATT_BLOCK + ATT_STEP), 0, pl.program_id(2) > 0)
        for j in range(ATT_HEADS_PER_GROUP):
            sl = slice(d * j, d * (j + 1))
            kcat = jnp.concatenate([kp_ref[:, sl], k_ref[:, sl]], axis=0)
            vcat = jnp.concatenate([vp_ref[:, sl], v_ref[:, sl]], axis=0)
            sc = jnp.where(mask, _nt(q_ref[:, sl], kcat) * ATT_SCALE, NEG_INF)
            m = jnp.max(sc, axis=-1, keepdims=True)
            pr = jnp.exp(sc - m)
            den = jnp.sum(pr, axis=-1, keepdims=True)
            o_ref[:, sl] = _nn(pr / den, vcat)
            lse_ref[:, sl] = jnp.broadcast_to(m + jnp.log(den), (ATT_STEP, d))

    out_spec = _att_spec(0)
    return pl.pallas_call(
        body, name=name, grid=(b, r, nb),
        in_specs=[_att_spec(0), _att_edge_spec(1, -1, nb), _att_spec(1), _att_edge_spec(2, -1, nb), _att_spec(2)],
        out_specs=[out_spec, out_spec],
        out_shape=[jax.ShapeDtypeStruct((b, r, l, ATT_OUT_DIM), F32)] * 2,
        compiler_params=_params("parallel", "parallel", "parallel"),
    )(qkr, qkr, qkr, qkr, qkr)


def _att_merge(os_, lses, name):
    n_pat = len(os_)
    b, _, s, gw = os_[0].shape
    ts, d = ATT_TILE, ATT_HEAD_DIM

    def body(*refs):
        o_refs, l_refs = refs[:n_pat], refs[n_pat:2 * n_pat]
        att_ref, lse_outs, stage = refs[2 * n_pat], refs[2 * n_pat + 1:3 * n_pat + 1], refs[-1]
        for j in range(ATT_HEADS_PER_GROUP):
            sl = slice(d * j, d * (j + 1))
            ov = [_from_residues(lambda ri, g=g: o_refs[g][ri, :, sl], stage, r)
                  for g, r in enumerate(ATT_DILATIONS)]
            ls = [_from_residues(lambda ri, g=g: l_refs[g][ri, :, sl], stage, r)
                  for g, r in enumerate(ATT_DILATIONS)]
            m = functools.reduce(jnp.maximum, ls)
            es = [jnp.exp(lv - m) for lv in ls]
            tot = functools.reduce(lambda u, v: u + v, es)
            acc = (es[0] / tot) * ov[0]
            for g in range(1, n_pat):
                acc = acc + (es[g] / tot) * ov[g]
            att_ref[:, sl] = acc
            joint = m + jnp.log(tot)
            for g, r in enumerate(ATT_DILATIONS):
                def store(ri, rows, out=lse_outs[g]):
                    out[ri, :, sl] = rows
                _to_residues(joint, stage, r, store)

    parts = [_by_residue_spec(r, gw) for r in ATT_DILATIONS]
    return pl.pallas_call(
        body, name=name, grid=(b, s // ts), in_specs=parts * 2,
        out_specs=[pl.BlockSpec((None, ts, gw), lambda bi, i: (bi, i, 0))] + parts,
        out_shape=[jax.ShapeDtypeStruct((b, s, gw), F32)]
        + [jax.ShapeDtypeStruct((b, r, s // r, gw), F32) for r in ATT_DILATIONS],
        scratch_shapes=[pltpu.VMEM((ts, d), F32)],
        compiler_params=_params("parallel", "parallel"),
    )(*os_, *lses)


def _att_delta(att, datt, name):
    b, s, gw = att.shape
    ts, d = ATT_TILE, ATT_HEAD_DIM
    n_pat = len(ATT_DILATIONS)

    def body(a_ref, d_ref, *rest):
        do_outs, dl_outs, stage = rest[:n_pat], rest[n_pat:2 * n_pat], rest[-1]
        for j in range(ATT_HEADS_PER_GROUP):
            sl = slice(d * j, d * (j + 1))
            dv = d_ref[:, sl]
            delta = jnp.broadcast_to(jnp.sum(a_ref[:, sl] * dv, axis=-1, keepdims=True), (ts, d))
            for g, r in enumerate(ATT_DILATIONS):
                def store_do(ri, rows, out=do_outs[g]):
                    out[ri, :, sl] = rows.astype(BF16)

                def store_dl(ri, rows, out=dl_outs[g]):
                    out[ri, :, sl] = rows

                _to_residues(dv, stage, r, store_do)
                _to_residues(delta, stage, r, store_dl)

    row = pl.BlockSpec((None, ts, gw), lambda bi, i: (bi, i, 0))
    parts = [_by_residue_spec(r, gw) for r in ATT_DILATIONS]
    outs = pl.pallas_call(
        body, name=name, grid=(b, s // ts), in_specs=[row, row], out_specs=parts * 2,
        out_shape=[jax.ShapeDtypeStruct((b, r, s // r, gw), BF16) for r in ATT_DILATIONS]
        + [jax.ShapeDtypeStruct((b, r, s // r, gw), F32) for r in ATT_DILATIONS],
        scratch_shapes=[pltpu.VMEM((ts, d), F32)],
        compiler_params=_params("parallel", "parallel"),
    )(att, datt)
    return outs[:n_pat], outs[n_pat:]


def _att_bwd_q(qkr, datt, lse, delta, name):
    b, r, l, _ = qkr.shape
    nb = l // ATT_STEP
    d = ATT_HEAD_DIM

    def body(q_ref, kp_ref, k_ref, vp_ref, v_ref, do_ref, lse_ref, dl_ref, dq_ref):
        mask = _band_mask((ATT_STEP, ATT_BLOCK + ATT_STEP), 0, pl.program_id(2) > 0)
        for j in range(ATT_HEADS_PER_GROUP):
            sl = slice(d * j, d * (j + 1))
            kcat = jnp.concatenate([kp_ref[:, sl], k_ref[:, sl]], axis=0)
            vcat = jnp.concatenate([vp_ref[:, sl], v_ref[:, sl]], axis=0)
            sc = _nt(q_ref[:, sl], kcat) * ATT_SCALE
            pr = jnp.exp(jnp.where(mask, sc - lse_ref[:, d * j:d * j + 1], NEG_INF))
            dp = _nt(do_ref[:, sl], vcat)
            dsc = pr * (dp - dl_ref[:, d * j:d * j + 1])
            dq_ref[:, sl] = _nn(dsc, kcat) * ATT_SCALE

    tok = _att_spec(0)
    return pl.pallas_call(
        body, name=name, grid=(b, r, nb),
        in_specs=[_att_spec(0), _att_edge_spec(1, -1, nb), _att_spec(1), _att_edge_spec(2, -1, nb), _att_spec(2),
                  tok, tok, tok],
        out_specs=tok,
        out_shape=jax.ShapeDtypeStruct((b, r, l, ATT_OUT_DIM), F32),
        compiler_params=_params("parallel", "parallel", "parallel"),
    )(qkr, qkr, qkr, qkr, qkr, datt, lse, delta)


def _att_bwd_kv(qkr, datt, lse, delta, name):
    b, r, l, _ = qkr.shape
    nb = l // ATT_STEP
    d = ATT_HEAD_DIM

    def body(k_ref, v_ref, q_ref, qn_ref, do_ref, don_ref, lse_ref, lsen_ref, dl_ref, dln_ref, dk_ref, dv_ref):
        shape = (ATT_STEP, ATT_STEP + ATT_BLOCK)
        kj = lax.broadcasted_iota(jnp.int32, shape, 0)
        qi = lax.broadcasted_iota(jnp.int32, shape, 1)
        dist = qi - kj
        has_next = pl.program_id(2) < nb - 1
        mask = (dist >= 0) & (dist <= ATT_BLOCK) & (has_next | (qi < ATT_STEP))
        for j in range(ATT_HEADS_PER_GROUP):
            sl = slice(d * j, d * (j + 1))
            qcat = jnp.concatenate([q_ref[:, sl], qn_ref[:, sl]], axis=0)
            docat = jnp.concatenate([do_ref[:, sl], don_ref[:, sl]], axis=0)
            lse_t = jnp.tile(jnp.concatenate([lse_ref[:, sl], lsen_ref[:, sl]], axis=0).T, (ATT_STEP // d, 1))
            dl_t = jnp.tile(jnp.concatenate([dl_ref[:, sl], dln_ref[:, sl]], axis=0).T, (ATT_STEP // d, 1))
            sc_t = _nt(k_ref[:, sl], qcat) * ATT_SCALE
            pr_t = jnp.exp(jnp.where(mask, sc_t - lse_t, NEG_INF))
            dv_ref[:, sl] = _nn(pr_t, docat)
            dsc_t = pr_t * (_nt(v_ref[:, sl], docat) - dl_t)
            dk_ref[:, sl] = _nn(dsc_t, qcat) * ATT_SCALE

    tok, tok_n = _att_spec(0), _att_edge_spec(0, 1, nb)
    return pl.pallas_call(
        body, name=name, grid=(b, r, nb),
        in_specs=[_att_spec(1), _att_spec(2), _att_spec(0), _att_edge_spec(0, 1, nb),
                  tok, tok_n, tok, tok_n, tok, tok_n],
        out_specs=[tok, tok],
        out_shape=[jax.ShapeDtypeStruct((b, r, l, ATT_OUT_DIM), F32)] * 2,
        compiler_params=_params("parallel", "parallel", "parallel"),
    )(qkr, qkr, qkr, qkr, datt, datt, lse, lse, delta, delta)


def _mix_fwd(gl, bg, ys, ya, name):
    t, d = ys.shape
    tm = _pick(t, (512, 256, 128))

    def body(gl_ref, bg_ref, ys_ref, ya_ref, o_ref):
        g0 = _sigmoid(gl_ref[:, :d] + bg_ref[:, :d])
        g1 = _sigmoid(gl_ref[:, d:] + bg_ref[:, d:])
        o_ref[...] = (g0 * ys_ref[...] + g1 * ya_ref[...]).astype(BF16)

    row = pl.BlockSpec((tm, d), lambda i: (i, 0))
    return pl.pallas_call(
        body, name=name, grid=(t // tm,),
        in_specs=[pl.BlockSpec((tm, 2 * d), lambda i: (i, 0)), pl.BlockSpec((1, 2 * d), lambda i: (0, 0)), row, row],
        out_specs=row, out_shape=jax.ShapeDtypeStruct((t, d), BF16),
        compiler_params=_params("parallel"),
    )(gl, bg, ys, ya)


def _mix_bwd(gl, bg, ys, ya, dmixed, name):
    t, d = ys.shape
    tm = _pick(t, (512, 256, 128))

    def body(gl_ref, bg_ref, ys_ref, ya_ref, dm_ref, dys_ref, dya_ref, dgl_ref, dbg_ref):
        @pl.when(pl.program_id(0) == 0)
        def _():
            dbg_ref[...] = jnp.zeros_like(dbg_ref)

        dm = dm_ref[...]
        g0 = _sigmoid(gl_ref[:, :d] + bg_ref[:, :d])
        g1 = _sigmoid(gl_ref[:, d:] + bg_ref[:, d:])
        dys_ref[...] = (dm * g0).astype(BF16)
        dya_ref[...] = (dm * g1).astype(BF16)
        d0 = dm * ys_ref[...] * (g0 * (1.0 - g0))
        d1 = dm * ya_ref[...] * (g1 * (1.0 - g1))
        dgl_ref[:, :d] = d0.astype(BF16)
        dgl_ref[:, d:] = d1.astype(BF16)
        dbg_ref[:, :d] += jnp.sum(d0, axis=0, keepdims=True)
        dbg_ref[:, d:] += jnp.sum(d1, axis=0, keepdims=True)

    row = pl.BlockSpec((tm, d), lambda i: (i, 0))
    wide = pl.BlockSpec((tm, 2 * d), lambda i: (i, 0))
    vec = pl.BlockSpec((1, 2 * d), lambda i: (0, 0))
    gate_cols = pl.BlockSpec((tm, 2 * d), lambda i: (i, DPROJ_COLS["gate"] // (2 * d)))
    return pl.pallas_call(
        body, name=name, grid=(t // tm,),
        in_specs=[wide, vec, row, row, row], out_specs=[row, row, gate_cols, vec],
        out_shape=[jax.ShapeDtypeStruct((t, d), BF16), jax.ShapeDtypeStruct((t, d), BF16),
                   jax.ShapeDtypeStruct((t, DPROJ_WIDTH), BF16), jax.ShapeDtypeStruct((1, 2 * d), F32)],
        compiler_params=_params("arbitrary"),
    )(gl, bg, ys, ya, dmixed)


def _swiglu_fwd(gt, up, name):
    t, f = gt.shape
    tm = _pick(t, (512, 256, 128))

    def body(g_ref, u_ref, o_ref):
        gv = g_ref[...]
        o_ref[...] = ((gv * _sigmoid(gv)) * u_ref[...]).astype(BF16)

    row = pl.BlockSpec((tm, f), lambda i: (i, 0))
    return pl.pallas_call(
        body, name=name, grid=(t // tm,), in_specs=[row, row], out_specs=row,
        out_shape=jax.ShapeDtypeStruct((t, f), BF16), compiler_params=_params("parallel"),
    )(gt, up)


def _swiglu_bwd(gt, up, dact, name):
    t, f = gt.shape
    tm = _pick(t, (512, 256, 128))

    def body(g_ref, u_ref, d_ref, dg_ref, du_ref):
        gv = g_ref[...]
        dv = d_ref[...]
        sg = _sigmoid(gv)
        dg_ref[...] = (dv * u_ref[...] * (sg * (1.0 + gv * (1.0 - sg)))).astype(BF16)
        du_ref[...] = (dv * (gv * sg)).astype(BF16)

    row = pl.BlockSpec((tm, f), lambda i: (i, 0))
    return pl.pallas_call(
        body, name=name, grid=(t // tm,), in_specs=[row, row, row], out_specs=[row, row],
        out_shape=[jax.ShapeDtypeStruct((t, f), BF16)] * 2, compiler_params=_params("parallel"),
    )(gt, up, dact)


def _peer(k):
    x, y, c = lax.axis_index("x"), lax.axis_index("y"), lax.axis_index("c")
    px, py, pc = x ^ ((k >> 2) & 1), y ^ ((k >> 1) & 1), c ^ (k & 1)
    return (px, py, pc), 4 * px + 2 * py + pc


def _my_index():
    return 4 * lax.axis_index("x") + 2 * lax.axis_index("y") + lax.axis_index("c")


def _all_gather(parts, name):
    n_parts = len(parts)

    def body(*refs):
        ins, outs = refs[:n_parts], refs[n_parts:2 * n_parts]
        send_sems, recv_sems, local_sems = refs[2 * n_parts:]
        here, me = _peer(0)
        sibling, sib_idx = _peer(1)
        chips = [_peer(2 * q) for q in range(1, N_CHIPS)]

        def copy(i, k, block, to, src=None):
            return pltpu.make_async_remote_copy(
                src_ref=outs[i].at[block] if src is None else src, dst_ref=outs[i].at[block],
                send_sem=send_sems.at[i * (N_DEV - 1) + k], recv_sem=recv_sems.at[i * (N_DEV - 1) + k],
                device_id=to, device_id_type=MESH)

        local = [pltpu.make_async_copy(ins[i], outs[i].at[me], local_sems.at[i]) for i in range(n_parts)]
        for cp in local:
            cp.start()
        sends = []
        for i in range(n_parts):
            sends.append(copy(i, 0, me, sibling, src=ins[i]))
            sends += [copy(i, q, me, chip, src=ins[i]) for q, (chip, _) in enumerate(chips, start=1)]
        for cp in sends:
            cp.start()
        for q, (chip, chip_idx) in enumerate(chips, start=1):
            for i in range(n_parts):
                copy(i, q, chip_idx, here).wait_recv()
                fwd = copy(i, N_CHIPS - 1 + q, chip_idx, sibling)
                fwd.start()
                sends.append(fwd)
        for i in range(n_parts):
            copy(i, 0, sib_idx, here).wait_recv()
        for q, (_, chip_idx) in enumerate(chips, start=1):
            for i in range(n_parts):
                copy(i, N_CHIPS - 1 + q, chip_idx ^ 1, here).wait_recv()
        for cp in sends:
            cp.wait_send()
        for cp in local:
            cp.wait()

    hbm = pl.BlockSpec(memory_space=pl.ANY)
    return pl.pallas_call(
        body, name=name, in_specs=[hbm] * n_parts, out_specs=[hbm] * n_parts,
        out_shape=[jax.ShapeDtypeStruct((N_DEV,) + p_.shape, p_.dtype) for p_ in parts],
        scratch_shapes=[pltpu.SemaphoreType.DMA((n_parts * (N_DEV - 1),)),
                        pltpu.SemaphoreType.DMA((n_parts * (N_DEV - 1),)),
                        pltpu.SemaphoreType.DMA((n_parts,))],
        compiler_params=pltpu.CompilerParams(has_side_effects=True),
    )(*parts)


HBM_SPEC = pl.BlockSpec(memory_space=pltpu.HBM)
SEM_SPEC = pl.BlockSpec(memory_space=pltpu.SEMAPHORE)
DATAFLOW = pltpu.SideEffectType.DATAFLOW_SIDE_EFFECTING


def _gather_start(block, after, name):
    def body(v_ref, land_ref, after_ref, send_sems, recv_sems, v_thru, land_thru, token):
        me = _my_index()
        for k in range(1, N_DEV):
            peer, _ = _peer(k)
            pltpu.make_async_remote_copy(
                src_ref=v_ref, dst_ref=land_ref.at[me], send_sem=send_sems.at[k - 1], recv_sem=recv_sems.at[k - 1],
                device_id=peer, device_id_type=MESH).start()
        token[...] = jnp.zeros_like(token)

    land_shape = (N_DEV,) + block.shape
    return pl.pallas_call(
        body, name=name,
        out_shape=(pltpu.SemaphoreType.DMA((N_DEV - 1,)), pltpu.SemaphoreType.DMA((N_DEV - 1,)),
                   pltpu.HBM(block.shape, block.dtype), pltpu.HBM(land_shape, block.dtype),
                   jax.ShapeDtypeStruct((8, LANES), F32)),
        in_specs=(HBM_SPEC, HBM_SPEC, pl.BlockSpec(memory_space=pl.ANY)),
        out_specs=(SEM_SPEC, SEM_SPEC, HBM_SPEC, HBM_SPEC, pl.BlockSpec(memory_space=pltpu.VMEM)),
        input_output_aliases={0: 2, 1: 3},
        compiler_params=pltpu.CompilerParams(has_side_effects=DATAFLOW),
    )(pltpu.with_memory_space_constraint(block, pltpu.HBM),
      pltpu.with_memory_space_constraint(lax.empty(land_shape, block.dtype), pltpu.HBM), after)


def _gather_wait(send_sems, recv_sems, block, landing, after, name):
    def body(v_ref, land_ref, send_sems, recv_sems, after_ref, v_dead, got_ref):
        for k in range(1, N_DEV):
            peer, pidx = _peer(k)
            copy = pltpu.make_async_remote_copy(
                src_ref=v_ref, dst_ref=land_ref.at[pidx], send_sem=send_sems.at[k - 1], recv_sem=recv_sems.at[k - 1],
                device_id=peer, device_id_type=MESH)
            copy.wait_send()
            copy.wait_recv()

    return pl.pallas_call(
        body, name=name,
        out_shape=(pltpu.HBM(block.shape, block.dtype), pltpu.HBM(landing.shape, landing.dtype)),
        in_specs=(HBM_SPEC, HBM_SPEC, SEM_SPEC, SEM_SPEC, pl.BlockSpec(memory_space=pl.ANY)),
        out_specs=(HBM_SPEC, HBM_SPEC), input_output_aliases={0: 0, 1: 1},
        compiler_params=pltpu.CompilerParams(has_side_effects=DATAFLOW),
    )(block, landing, send_sems, recv_sems, after)[1]


TILE_ELEMS = 1024 * 1024


def _pair_exchange(slabs, name):
    def body(slab_ref, got_ref, send_sems, recv_sems):
        c = lax.axis_index("c")
        sibling, _ = _peer(1)
        copies = [pltpu.make_async_remote_copy(
            src_ref=slab_ref.at[2 * q + 1 - c], dst_ref=got_ref.at[q], send_sem=send_sems.at[q],
            recv_sem=recv_sems.at[q], device_id=sibling, device_id_type=MESH) for q in range(N_CHIPS)]
        for cp in copies:
            cp.start()
        for cp in copies:
            cp.wait()

    hbm = pl.BlockSpec(memory_space=pl.ANY)
    return pl.pallas_call(
        body, name=name, in_specs=[hbm], out_specs=hbm,
        out_shape=jax.ShapeDtypeStruct((N_CHIPS,) + slabs.shape[1:], slabs.dtype),
        scratch_shapes=[pltpu.SemaphoreType.DMA((N_CHIPS,)), pltpu.SemaphoreType.DMA((N_CHIPS,))],
        compiler_params=pltpu.CompilerParams(has_side_effects=True),
    )(slabs)


def _chip_sum(slabs, got, core, name):
    _, rows, lanes = slabs.shape
    tr = _pick(rows, (TILE_ELEMS // lanes, 512, 256, 128, 64, 32, 16))

    def body(core_ref, mine_ref, got_ref, o_ref):
        o_ref[...] = (mine_ref[...].astype(F32) + got_ref[...].astype(F32)).astype(BF16)

    return pl.pallas_call(
        body, name=name,
        grid_spec=pltpu.PrefetchScalarGridSpec(
            num_scalar_prefetch=1, grid=(N_CHIPS, rows // tr),
            in_specs=[pl.BlockSpec((None, tr, lanes), lambda q, i, core_ref: (2 * q + core_ref[0], i, 0)),
                      pl.BlockSpec((None, tr, lanes), lambda q, i, core_ref: (q, i, 0))],
            out_specs=pl.BlockSpec((None, tr, lanes), lambda q, i, core_ref: (q, i, 0))),
        out_shape=jax.ShapeDtypeStruct((N_CHIPS, rows, lanes), BF16),
        compiler_params=_params("parallel", "parallel"),
    )(core, slabs, got)


def _chip_exchange(chip_sums, shared, name):
    def body(sum_ref, sh_ref, got_ref, gsh_ref, send_sems, recv_sems, sh_send_sems, sh_recv_sems, local_sems):
        me = _my_index()
        my_chip = me >> 1
        local = [pltpu.make_async_copy(sum_ref.at[my_chip], got_ref.at[my_chip], local_sems.at[0]),
                 pltpu.make_async_copy(sh_ref, gsh_ref.at[me], local_sems.at[1])]
        for cp in local:
            cp.start()
        sends = []
        for q in range(1, N_CHIPS):
            peer, pidx = _peer(2 * q)
            cp = pltpu.make_async_remote_copy(
                src_ref=sum_ref.at[pidx >> 1], dst_ref=got_ref.at[my_chip], send_sem=send_sems.at[q - 1],
                recv_sem=recv_sems.at[q - 1], device_id=peer, device_id_type=MESH)
            cp.start()
            sends.append(cp)
        for k in range(1, N_DEV):
            peer, _ = _peer(k)
            cp = pltpu.make_async_remote_copy(
                src_ref=sh_ref, dst_ref=gsh_ref.at[me], send_sem=sh_send_sems.at[k - 1],
                recv_sem=sh_recv_sems.at[k - 1], device_id=peer, device_id_type=MESH)
            cp.start()
            sends.append(cp)
        for q in range(1, N_CHIPS):
            peer, pidx = _peer(2 * q)
            pltpu.make_async_remote_copy(
                src_ref=sum_ref.at[my_chip], dst_ref=got_ref.at[pidx >> 1], send_sem=send_sems.at[q - 1],
                recv_sem=recv_sems.at[q - 1], device_id=peer, device_id_type=MESH).wait_recv()
        for k in range(1, N_DEV):
            peer, pidx = _peer(k)
            pltpu.make_async_remote_copy(
                src_ref=sh_ref, dst_ref=gsh_ref.at[pidx], send_sem=sh_send_sems.at[k - 1],
                recv_sem=sh_recv_sems.at[k - 1], device_id=peer, device_id_type=MESH).wait_recv()
        for cp in sends:
            cp.wait_send()
        for cp in local:
            cp.wait()

    hbm = pl.BlockSpec(memory_space=pl.ANY)
    return pl.pallas_call(
        body, name=name, in_specs=[hbm, hbm], out_specs=[hbm, hbm],
        out_shape=[jax.ShapeDtypeStruct(chip_sums.shape, chip_sums.dtype),
                   jax.ShapeDtypeStruct((N_DEV,) + shared.shape, shared.dtype)],
        scratch_shapes=[pltpu.SemaphoreType.DMA((N_CHIPS - 1,)), pltpu.SemaphoreType.DMA((N_CHIPS - 1,)),
                        pltpu.SemaphoreType.DMA((N_DEV - 1,)), pltpu.SemaphoreType.DMA((N_DEV - 1,)),
                        pltpu.SemaphoreType.DMA((2,))],
        compiler_params=pltpu.CompilerParams(has_side_effects=True),
    )(chip_sums, shared)


def _adamw(parts, w, m, v, name):
    n_parts, rows, lanes = parts.shape
    tr = rows if rows * lanes <= TILE_ELEMS // 2 else _pick(rows, (TILE_ELEMS // 4 // lanes, 128, 64, 32, 16, 8))
    c1 = 1.0 - ADAM_B1 ** ADAM_STEP
    c2 = 1.0 - ADAM_B2 ** ADAM_STEP

    def body(p_ref, w_ref, m_ref, v_ref, g_ref, d_ref, nm_ref, nv_ref):
        g = p_ref[0].astype(F32)
        for j in range(1, n_parts):
            g = g + p_ref[j].astype(F32)
        nm = ADAM_B1 * m_ref[...] + (1.0 - ADAM_B1) * g
        nv = ADAM_B2 * v_ref[...] + (1.0 - ADAM_B2) * (g * g)
        g_ref[...] = g
        nm_ref[...] = nm
        nv_ref[...] = nv
        d_ref[...] = -ADAM_LR * ((nm / c1) / (jnp.sqrt(nv / c2) + ADAM_EPS) + ADAM_WD * w_ref[...])

    row = pl.BlockSpec((tr, lanes), lambda i: (i, 0))
    return pl.pallas_call(
        body, name=name, grid=(rows // tr,),
        in_specs=[pl.BlockSpec((n_parts, tr, lanes), lambda i: (0, i, 0)), row, row, row],
        out_specs=[row] * 4, out_shape=[jax.ShapeDtypeStruct((rows, lanes), F32)] * 4,
        compiler_params=_params("parallel"),
    )(parts, w, m, v)


MATRIX_SHARDS = (
    ("w_in", (D_MODEL, IN_PROJ_DIM // N_DEV), True),
    ("w_ssm_out", (SSM_D_INNER // N_DEV, D_MODEL), False),
    ("w_att_out", (ATT_OUT_DIM, D_MODEL // N_DEV), True),
    ("w_mix_out", (D_MODEL // N_DEV, D_MODEL), False),
    ("w_ffn_gate", (D_MODEL, D_FF // N_DEV), True),
    ("w_ffn_up", (D_MODEL, D_FF // N_DEV), True),
    ("w_ffn_down", (D_FF // N_DEV, D_MODEL), False),
)
CONV_SHARD = ("conv_w", (SSM_CONV, SSM_CONV_DIM // N_DEV), True)
SHARDED = MATRIX_SHARDS + (CONV_SHARD,)
REPLICATED = (("norm_mix", D_MODEL), ("b_gate", 2 * D_MODEL), ("conv_b", SSM_CONV_DIM), ("dt_bias", SSM_N_HEADS),
              ("a_log", SSM_N_HEADS), ("d_skip", SSM_N_HEADS), ("ssm_norm", SSM_D_INNER), ("norm_ffn", D_MODEL),
              ("norm_final", D_MODEL))


def _round_up(n, mult):
    return -(-n // mult) * mult


def _pack_rows(flat, row_mult):
    rows = _round_up(-(-flat.shape[0] // LANES), row_mult)
    return jnp.pad(flat, (0, rows * LANES - flat.shape[0])).reshape(rows, LANES)


def _stacking(specs):
    return tuple((name, (shape[1], shape[0]) if by_cols else shape, by_cols) for name, shape, by_cols in specs)


def _to_stacking(vals, specs):
    return {name: (vals[name].T if by_cols else vals[name]) for name, _, by_cols in specs}


STACK_WIDTH = D_MODEL
STACK_ALIGN = 16
STACK_ORDER = ("w_ssm_out", "w_mix_out", "w_ffn_gate", "w_ffn_up", "w_ffn_down", "w_att_out", "conv_w", "w_in")


def _stack_layout():
    shapes = {name: shape for name, shape, _ in _stacking(SHARDED)}
    layout, off = {}, 0
    for name in STACK_ORDER:
        r, c = shapes[name]
        rows = r if c == STACK_WIDTH else _round_up(-(-(r * c) // STACK_WIDTH), STACK_ALIGN)
        layout[name] = (off, rows, (r, c))
        off = _round_up(off + rows, STACK_ALIGN)
    return layout, _round_up(off, 1024)


def _to_stack_rows(v, rows):
    if v.shape[-1] == STACK_WIDTH:
        return v
    lead = v.shape[:-2]
    flat = v.reshape(lead + (-1,))
    flat = jnp.pad(flat, [(0, 0)] * len(lead) + [(0, rows * STACK_WIDTH - flat.shape[-1])])
    return flat.reshape(lead + (rows, STACK_WIDTH))


def _from_stack_rows(block, shape):
    r, c = shape
    if c == STACK_WIDTH:
        return block
    lead = block.shape[:-2]
    return block.reshape(lead + (-1,))[..., :r * c].reshape(lead + (r, c))


def _stack(vals, dtype, skip=(), head_only=False):
    layout, total = _stack_layout()
    order = STACK_ORDER
    if head_only:
        order, total = STACK_ORDER[:-1], layout["w_in"][0]
    lead = next(iter(vals.values())).shape[:-2]
    pieces = []
    for i, name in enumerate(order):
        off, rows, _ = layout[name]
        until = layout[order[i + 1]][0] if i + 1 < len(order) else total
        piece = jnp.zeros(lead + (rows, STACK_WIDTH), dtype) if name in skip else _to_stack_rows(vals[name], rows)
        pieces.append(jnp.pad(piece.astype(dtype), [(0, 0)] * len(lead) + [(0, until - off - rows), (0, 0)]))
    return jnp.concatenate(pieces, axis=-2)


def _unstack(stacked, names):
    layout, _ = _stack_layout()
    return {name: _from_stack_rows(stacked[..., layout[name][0]:layout[name][0] + layout[name][1], :], layout[name][2])
            for name in names}


W_IN_SHARD_ROWS = IN_PROJ_DIM // N_DEV


def _w_in_row_moves():
    moves, orig = [], 0
    for name, size in IN_SPLIT:
        for j in range(N_DEV):
            lo, hi = max(orig, W_IN_SHARD_ROWS * j), min(orig + size, W_IN_SHARD_ROWS * (j + 1))
            if lo < hi:
                moves.append((j, lo - W_IN_SHARD_ROWS * j, DPROJ_COLS[name] + lo - orig, hi - lo))
        orig += size
    return moves


def _w_in_from_shards(shards, name):
    total, base = shards.shape[1], 0
    pad_lo, pad_hi = DPROJ_COLS["dt"] + _round_up(SSM_N_HEADS, STACK_ALIGN), DPROJ_COLS["dt"] + DPROJ_DT_WIDTH

    def body(x_ref, o_ref):
        o_ref[pad_lo:pad_hi, :] = jnp.zeros((pad_hi - pad_lo, LANES), x_ref.dtype)
        for j, r, at, n in _w_in_row_moves():
            o_ref[at:at + n, :] = x_ref[j, base + r:base + r + n, :]

    return pl.pallas_call(
        body, name=name, grid=(STACK_WIDTH // LANES,),
        in_specs=[pl.BlockSpec((N_DEV, total, LANES), lambda c: (0, 0, c))],
        out_specs=pl.BlockSpec((DPROJ_WIDTH, LANES), lambda c: (0, c)),
        out_shape=jax.ShapeDtypeStruct((DPROJ_WIDTH, STACK_WIDTH), shards.dtype),
        compiler_params=_params("parallel"),
    )(shards)


def _w_in_to_shards(dw_all, head, name):
    layout, total = _stack_layout()
    base = layout["w_in"][0]
    end = base + W_IN_SHARD_ROWS

    def body(x_ref, h_ref, o_ref):
        o_ref[:, 0:base, :] = h_ref[...]
        for j, r, at, n in _w_in_row_moves():
            o_ref[j, base + r:base + r + n, :] = x_ref[at:at + n, :]
        o_ref[:, end:total, :] = jnp.zeros((N_DEV, total - end, LANES), o_ref.dtype)

    return pl.pallas_call(
        body, name=name, grid=(STACK_WIDTH // LANES,),
        in_specs=[pl.BlockSpec((DPROJ_WIDTH, LANES), lambda c: (0, c)),
                  pl.BlockSpec((N_DEV, base, LANES), lambda c: (0, 0, c))],
        out_specs=pl.BlockSpec((N_DEV, total, LANES), lambda c: (0, 0, c)),
        out_shape=jax.ShapeDtypeStruct((N_DEV, total, STACK_WIDTH), dw_all.dtype),
        compiler_params=_params("parallel"),
    )(dw_all, head)


REPLICATED_ROWS = sum(-(-size // LANES) for _, size in REPLICATED)
LOSS_ROW = REPLICATED_ROWS


def _pack_replicated(vals):
    rows = []
    for name, size in REPLICATED:
        v = vals[name].reshape(-1).astype(F32)
        rows.append(jnp.pad(v, (0, _round_up(size, LANES) - size)))
    return _pack_rows(jnp.concatenate(rows), 8)


def _unpack_replicated(packed, shapes):
    flat = packed.reshape(-1)
    out, off = {}, 0
    for name, size in REPLICATED:
        out[name] = flat[off:off + size].reshape(shapes[name])
        off += _round_up(size, LANES)
    return out


def _lane_row(v):
    v = v.reshape(-1).astype(F32)
    return jnp.pad(v, (0, LANES - v.shape[0])).reshape(1, LANES)


IN_SPLIT = (("z", SSM_D_INNER), ("xbc", SSM_CONV_DIM), ("dt", SSM_N_HEADS), ("qkv", ATT_QKV_DIM), ("gate", 2 * D_MODEL))


def kernel(x, norm_mix, w_in, b_gate, conv_w, conv_b, dt_bias, a_log, d_skip, ssm_norm, w_ssm_out, w_att_out, w_mix_out, norm_ffn, w_ffn_gate, w_ffn_up, w_ffn_down, norm_final, loss_target, m_norm_mix, m_w_in, m_b_gate, m_conv_w, m_conv_b, m_dt_bias, m_a_log, m_d_skip, m_ssm_norm, m_w_ssm_out, m_w_att_out, m_w_mix_out, m_norm_ffn, m_w_ffn_gate, m_w_ffn_up, m_w_ffn_down, m_norm_final, v_norm_mix, v_w_in, v_b_gate, v_conv_w, v_conv_b, v_dt_bias, v_a_log, v_d_skip, v_ssm_norm, v_w_ssm_out, v_w_att_out, v_w_mix_out, v_norm_ffn, v_w_ffn_gate, v_w_ffn_up, v_w_ffn_down, v_norm_final):
    given = dict(locals())
    weights = {name: given[name][0] for name, _, _ in SHARDED}
    b, s, d = x.shape
    t = b * s

    stacking = _to_stacking(weights, SHARDED)
    conv_shape = dict((name, shape) for name, shape, _ in _stacking(SHARDED))["conv_w"]
    w_in_local = jnp.pad(stacking["w_in"].astype(BF16), ((0, -W_IN_SHARD_ROWS % STACK_ALIGN), (0, 0)))
    conv_local = _pack_rows(stacking["conv_w"].reshape(-1), 8)
    w_in_shards, conv_all = _all_gather([w_in_local, conv_local], "w_in_all_gather")
    head_local = _stack(stacking, BF16, skip=("conv_w",), head_only=True)
    in_flight = _gather_start(head_local, conv_all, "weights_gather_start")
    w_in_all = _w_in_from_shards(w_in_shards, "w_in_from_shards")
    w_sec = {name: w_in_all[DPROJ_COLS[name]:DPROJ_COLS[name] + _round_up(size, LANES)] for name, size in IN_SPLIT}
    conv_size = conv_shape[0] * conv_shape[1]
    conv_taps = conv_all.reshape(N_DEV, -1)[:, :conv_size].reshape(N_DEV * conv_shape[0], conv_shape[1]).T

    g_mix, g_ffn, g_fin = norm_mix.reshape(1, d), norm_ffn.reshape(1, d), norm_final.reshape(1, d)
    g_mix = g_mix + in_flight[4][:1, :1]
    bg_row = b_gate.reshape(1, 2 * d)
    convb_row = conv_b.reshape(1, SSM_CONV_DIM)
    ssmn_row = ssm_norm.reshape(1, SSM_D_INNER)
    dtb_row, alog_row = _lane_row(dt_bias), _lane_row(a_log)
    cosf, sinf = _rope_tables(s)

    x2d = x.reshape(t, d)
    h1 = _rmsnorm_fwd(x2d, g_mix, "norm_mix_fwd")
    proj = {name: _mm(h1, w_sec[name], mode="nt", name="in_proj_" + name) for name, _ in IN_SPLIT}
    xbc3 = proj["xbc"].reshape(b, s, SSM_CONV_DIM)
    xc = _conv_fwd(xbc3, conv_taps, convb_row, "conv_fwd")
    dtr3 = proj["dt"].reshape(b, s, DT_PAD)
    to_channels, to_heads = _head_masks()
    dskx = jnp.repeat(d_skip.reshape(-1).astype(F32), SSM_HEAD_DIM).reshape(1, SSM_D_INNER)
    y_ssd, h_states = _ssd_fwd(xc, dtr3, dtb_row, alog_row, dskx, to_channels, "ssd_fwd")
    y_ssd2 = y_ssd.reshape(t, SSM_D_INNER)
    ynorm = _gate_norm_fwd(y_ssd2, proj["z"], ssmn_row, "ssd_gate_norm_fwd")
    landed = _gather_wait(*in_flight[:4], ynorm, "weights_gather_wait")
    head_all = lax.dynamic_update_slice(landed, head_local[None], (_my_index(), 0, 0))
    full = {name: v.reshape((-1,) + v.shape[2:]) for name, v in _unstack(head_all, STACK_ORDER[:-2]).items()}
    y_ssm = _mm(ynorm, full["w_ssm_out"], mode="nn", name="ssm_out_proj")

    qkv3 = proj["qkv"].reshape(b, s, ATT_QKV_DIM)
    qk_parts = _rope_fwd(qkv3, cosf, sinf, "rope_fwd")
    att_parts = [_att_fwd(qk_parts[gi], "att_fwd_%d" % r) for gi, r in enumerate(ATT_DILATIONS)]
    att, *lse_parts = _att_merge([o for o, _ in att_parts], [l_ for _, l_ in att_parts], "att_merge")
    att2 = att.reshape(t, ATT_OUT_DIM)
    y_att = _mm(att2, full["w_att_out"], mode="nt", name="att_out_proj")

    mixed = _mix_fwd(proj["gate"], bg_row, y_ssm, y_att, "mix_fwd")
    x2 = _mm(mixed, full["w_mix_out"], mode="nn", name="mix_out_proj", add=x2d)
    h2 = _rmsnorm_fwd(x2, g_ffn, "norm_ffn_fwd")
    gt = _mm(h2, full["w_ffn_gate"], mode="nt", name="ffn_gate_proj")
    up = _mm(h2, full["w_ffn_up"], mode="nt", name="ffn_up_proj")
    act = _swiglu_fwd(gt, up, "swiglu_fwd")
    x3 = _mm(act, full["w_ffn_down"], mode="nn", name="ffn_down_proj", add=x2)

    loss_row, dx3, dg_fin, dx3b = _loss_head(x3, g_fin, loss_target.reshape(t, d), "loss_head")
    grads = {}
    dact = _mm(dx3b, full["w_ffn_down"], mode="nt", name="ffn_down_dx")
    grads["w_ffn_down"] = _mm(act, dx3b, mode="tn", name="ffn_down_dw", out_dtype=BF16)
    dgt, dup = _swiglu_bwd(gt, up, dact, "swiglu_bwd")
    grads["w_ffn_gate"] = _mm(dgt, h2, mode="tn", name="ffn_gate_dw", out_dtype=BF16)
    grads["w_ffn_up"] = _mm(dup, h2, mode="tn", name="ffn_up_dw", out_dtype=BF16)
    dh2 = _mm(dgt, full["w_ffn_gate"], mode="nn", name="ffn_gate_dx")
    dh2 = _mm(dup, full["w_ffn_up"], mode="nn", name="ffn_up_dx", add=dh2)
    dx2, dg_ffn, dx2b = _rmsnorm_bwd(x2, g_ffn, dh2, dx3, "norm_ffn_bwd", with_bf16=True)

    dmixed = _mm(dx2b, full["w_mix_out"], mode="nt", name="mix_out_dx")
    grads["w_mix_out"] = _mm(mixed, dx2b, mode="tn", name="mix_out_dw", out_dtype=BF16)
    dys, dya, dproj, dbg = _mix_bwd(proj["gate"], bg_row, y_ssm, y_att, dmixed, "mix_bwd")

    grads["w_ssm_out"] = _mm(ynorm, dys, mode="tn", name="ssm_out_dw", out_dtype=BF16)
    dynorm = _mm(dys, full["w_ssm_out"], mode="nt", name="ssm_out_dx")
    dy_ssd, dproj, dssmn = _gate_norm_bwd(y_ssd2, proj["z"], ssmn_row, dynorm, dproj, "ssd_gate_norm_bwd")
    dxc, dproj, dalog, ddsk, ddtb = _ssd_bwd(xc, dtr3, dy_ssd.reshape(b, s, SSM_D_INNER), h_states, dtb_row, alog_row,
                                             dskx, to_channels, to_heads, dproj.reshape(b, s, DPROJ_WIDTH), "ssd_bwd")
    dproj, dconvw, dconvb = _conv_bwd(xbc3, dxc, conv_taps, convb_row, dproj, "conv_bwd")
    grads["conv_w"] = dconvw.T.astype(BF16)

    grads["w_att_out"] = _mm(dya, att2, mode="tn", name="att_out_dw", out_dtype=BF16)
    datt = _mm(dya, full["w_att_out"], mode="nn", name="att_out_dx").reshape(b, s, ATT_OUT_DIM)
    do_parts, dl_parts = _att_delta(att, datt, "att_delta")
    dqs, dks, dvs = [], [], []
    for gi, r in enumerate(ATT_DILATIONS):
        operands = (qk_parts[gi], do_parts[gi], lse_parts[gi], dl_parts[gi])
        dqs.append(_att_bwd_q(*operands, "att_bwd_q_%d" % r))
        dk_g, dv_g = _att_bwd_kv(*operands, "att_bwd_kv_%d" % r)
        dks.append(dk_g)
        dvs.append(dv_g)
    dproj = _rope_bwd(dqs, dks, dvs, cosf, sinf, dproj, "rope_bwd").reshape(t, DPROJ_WIDTH)

    dw_all = _mm(dproj, h1, mode="tn", name="in_proj_dw", out_dtype=BF16)
    dh1 = _mm(dproj, w_in_all, mode="nn", name="in_proj_dx")
    grad_x, dg_mix = _rmsnorm_bwd(x2d, g_mix, dh1, dx2, "norm_mix_bwd")

    head = _stack({name: v.reshape((N_DEV, -1, v.shape[-1])) for name, v in grads.items()}, BF16, head_only=True)
    slabs = _w_in_to_shards(dw_all, head, "grad_stacks")
    small = {"norm_mix": dg_mix, "b_gate": dbg, "conv_b": dconvb, "dt_bias": ddtb[:, :SSM_N_HEADS],
             "a_log": dalog[:, :SSM_N_HEADS], "d_skip": ddsk[:, :SSM_N_HEADS], "ssm_norm": dssmn,
             "norm_ffn": dg_ffn, "norm_final": dg_fin}
    core = lax.axis_index("c").astype(jnp.int32).reshape(1)
    chip_sums = _chip_sum(slabs, _pair_exchange(slabs, "grad_pair_exchange"), core, "grad_chip_sum")
    shared = _pack_replicated(small)
    shared = shared.at[LOSS_ROW, 0].set(loss_row[0, 0])
    got, got_small = _chip_exchange(chip_sums, shared, "grad_chip_exchange")

    def packed(prefix):
        vals = _to_stacking({name: given[prefix + name][0] for name, _, _ in SHARDED}, SHARDED)
        rep = {name: given[prefix + name] for name, _ in REPLICATED}
        return _stack(vals, F32), _pack_replicated(rep)

    (w_big, w_small), (m_big, m_small), (v_big, v_small) = packed(""), packed("m_"), packed("v_")
    big = _adamw(got, w_big, m_big, v_big, "adamw_sharded")
    sml = _adamw(got_small, w_small, m_small, v_small, "adamw_replicated")

    outs = [sml[0][LOSS_ROW, 0], grad_x.reshape(b, s, d)]
    rep_shapes = {name: given[name].shape for name, _ in REPLICATED}
    order = ["norm_mix", "w_in", "b_gate", "conv_w", "conv_b", "dt_bias", "a_log", "d_skip", "ssm_norm", "w_ssm_out",
             "w_att_out", "w_mix_out", "norm_ffn", "w_ffn_gate", "w_ffn_up", "w_ffn_down", "norm_final"]
    for big_k, sml_k in zip(big, sml):
        sharded = _to_stacking(_unstack(big_k, STACK_ORDER), SHARDED)
        rep = _unpack_replicated(sml_k, rep_shapes)
        for name in order:
            outs.append(sharded[name][None] if name in sharded else rep[name])
    return tuple(outs)
```

```python
import functools
import math

import jax
import jax.numpy as jnp
from jax import lax
from jax.experimental import pallas as pl
from jax.experimental.pallas import tpu as pltpu

F32 = jnp.float32
BF16 = jnp.bfloat16

N_DEV = 8
N_CHIPS = 4
D_MODEL = 1024
SSM_D_INNER = 2048
SSM_HEAD_DIM = 64
SSM_N_HEADS = 32
SSM_N_GROUPS = 4
SSM_HEADS_PER_GROUP = SSM_N_HEADS // SSM_N_GROUPS
SSM_D_STATE = 128
SSM_CONV = 4
SSM_CHUNK = 128
SSM_CONV_DIM = 3072
ATT_HEAD_DIM = 128
ATT_HEADS_PER_GROUP = 4
ATT_DILATIONS = (1, 4, 16)
ATT_N_HEADS = 12
ATT_QKV_DIM = 4608
ATT_OUT_DIM = 512
ATT_BLOCK = 128
ROPE_THETA = 10000.0
D_FF = 2816
IN_PROJ_DIM = 11808
EPS = 1e-6
LANES = 128
DT_PAD = LANES

DPROJ_COLS = {"qkv": 0, "z": 4608, "xbc": 6656, "dt": 9728, "gate": 10240}
DPROJ_DT_WIDTH = 512
DPROJ_WIDTH = 12288

ADAM_LR = 0.001
ADAM_B1 = 0.9
ADAM_B2 = 0.999
ADAM_EPS = 1e-08
ADAM_WD = 0.01
ADAM_STEP = 10

VMEM_LIMIT = 56 * 1024 * 1024
MESH = pl.DeviceIdType.MESH
NEG_INF = float("-inf")


def _tile_rows(n, cap, mult):
    return max(t for t in range(mult, min(n, cap) + 1, mult) if n % t == 0)


def _pick(n, candidates):
    for c in candidates:
        if n % c == 0:
            return c
    return n


def _params(*sem):
    return pltpu.CompilerParams(dimension_semantics=sem, vmem_limit_bytes=VMEM_LIMIT)


def _sigmoid(x):
    return 1.0 / (1.0 + jnp.exp(-x))


def _softplus(x):
    return jnp.maximum(x, 0.0) + jnp.log(1.0 + jnp.exp(-jnp.abs(x)))


def _dot(a, b, dims):
    return lax.dot_general(a.astype(BF16), b.astype(BF16), (dims, ((), ())), preferred_element_type=F32)


def _nn(a, b):
    return _dot(a, b, ((1,), (0,)))


def _nt(a, b):
    return _dot(a, b, ((1,), (1,)))


def _tn(a, b):
    return _dot(a, b, ((0,), (0,)))


def _split3(v):
    hi = v.astype(BF16)
    r1 = v - hi.astype(F32)
    mid = r1.astype(BF16)
    lo = (r1 - mid.astype(F32)).astype(BF16)
    return hi, mid, lo


def _mask_nn(mask, v):
    mb = mask.astype(BF16)
    hi, mid, lo = _split3(v)
    return _nn(mb, hi) + (_nn(mb, mid) + _nn(mb, lo))


MM_VMEM_BUDGET = 40 * 1024 * 1024
MM_FULL_K = 2816


def _mm_tiles(m, n, k, a_bytes, b_bytes, o_bytes, has_add):
    tk = k if k <= MM_FULL_K else _pick(k, (2048, 1024, 512, 256, 128))
    tn = 1408 if (n > 1024 and n % 1408 == 0) else _pick(n, (1024, 768, 512, 384, 256, 128))
    for tm in (1408, 1024, 768, 512, 384, 256, 128):
        if m % tm:
            continue
        buffers = 2 * (tm * tk * a_bytes + tk * tn * b_bytes + tm * tn * (o_bytes + (4 if has_add else 0)))
        if tk < k:
            buffers += tm * tn * 4
        if buffers <= MM_VMEM_BUDGET:
            return tm, tn, tk
    return _pick(m, (128,)), tn, tk


def _mm(a, b, *, mode, name, out_dtype=F32, add=None):
    if mode == "nn":
        (m, k), n = a.shape, b.shape[1]
    elif mode == "nt":
        (m, k), n = a.shape, b.shape[0]
    else:
        (k, m), n = a.shape, b.shape[1]
    has_add = add is not None
    tm, tn, tk = _mm_tiles(m, n, k, a.dtype.itemsize, b.dtype.itemsize, jnp.dtype(out_dtype).itemsize, has_add)
    nk = k // tk
    dims = {"nn": ((1,), (0,)), "nt": ((1,), (1,)), "tn": ((0,), (0,))}[mode]
    a_spec = {"nn": pl.BlockSpec((tm, tk), lambda i, j, kk: (i, kk)),
              "nt": pl.BlockSpec((tm, tk), lambda i, j, kk: (i, kk)),
              "tn": pl.BlockSpec((tk, tm), lambda i, j, kk: (kk, i))}[mode]
    b_spec = {"nn": pl.BlockSpec((tk, tn), lambda i, j, kk: (kk, j)),
              "nt": pl.BlockSpec((tn, tk), lambda i, j, kk: (j, kk)),
              "tn": pl.BlockSpec((tk, tn), lambda i, j, kk: (kk, j))}[mode]
    o_spec = pl.BlockSpec((tm, tn), lambda i, j, kk: (i, j))

    def finish(r, c_ref, o_ref):
        if has_add:
            r = r + c_ref[...]
        o_ref[...] = r.astype(out_dtype)

    def body_one(*refs):
        a_ref, b_ref = refs[:2]
        finish(_dot(a_ref[...], b_ref[...], dims), refs[2] if has_add else None, refs[-1])

    def body_acc(*refs):
        a_ref, b_ref = refs[:2]
        o_ref, acc = refs[-2:]
        kk = pl.program_id(2)

        @pl.when(kk == 0)
        def _():
            acc[...] = jnp.zeros_like(acc)

        acc[...] += _dot(a_ref[...], b_ref[...], dims)

        @pl.when(kk == nk - 1)
        def _():
            finish(acc[...], refs[2] if has_add else None, o_ref)

    in_specs = [a_spec, b_spec] + ([o_spec] if has_add else [])
    args = (a, b) + ((add,) if has_add else ())
    return pl.pallas_call(
        body_one if nk == 1 else body_acc, name=name, grid=(m // tm, n // tn, nk),
        in_specs=in_specs, out_specs=o_spec,
        out_shape=jax.ShapeDtypeStruct((m, n), out_dtype),
        scratch_shapes=[] if nk == 1 else [pltpu.VMEM((tm, tn), F32)],
        compiler_params=_params("parallel", "parallel", "arbitrary"),
    )(*args)


def _rmsnorm_fwd(x, g, name):
    t, d = x.shape
    tm = _pick(t, (512, 256, 128))

    def body(x_ref, g_ref, o_ref):
        xv = x_ref[...]
        r = lax.rsqrt(jnp.mean(xv * xv, axis=-1, keepdims=True) + EPS)
        o_ref[...] = ((xv * r) * g_ref[...]).astype(BF16)

    return pl.pallas_call(
        body, name=name, grid=(t // tm,),
        in_specs=[pl.BlockSpec((tm, d), lambda i: (i, 0)), pl.BlockSpec((1, d), lambda i: (0, 0))],
        out_specs=pl.BlockSpec((tm, d), lambda i: (i, 0)),
        out_shape=jax.ShapeDtypeStruct((t, d), BF16),
        compiler_params=_params("parallel"),
    )(x, g)


def _rmsnorm_bwd(x, g, dh, dres, name, with_bf16=False):
    t, d = x.shape
    tm = _pick(t, (512, 256, 128))

    def body(x_ref, g_ref, dh_ref, dres_ref, dx_ref, dg_ref, *dxb_ref):
        @pl.when(pl.program_id(0) == 0)
        def _():
            dg_ref[...] = jnp.zeros_like(dg_ref)

        xv = x_ref[...]
        r = lax.rsqrt(jnp.mean(xv * xv, axis=-1, keepdims=True) + EPS)
        xhat = xv * r
        dhv = dh_ref[...]
        dyg = dhv * g_ref[...]
        dx = dres_ref[...] + r * (dyg - xhat * jnp.mean(dyg * xhat, axis=-1, keepdims=True))
        dx_ref[...] = dx
        if with_bf16:
            dxb_ref[0][...] = dx.astype(BF16)
        dg_ref[...] += jnp.sum(dhv * xhat, axis=0, keepdims=True)

    row = pl.BlockSpec((tm, d), lambda i: (i, 0))
    vec = pl.BlockSpec((1, d), lambda i: (0, 0))
    extra = with_bf16 * [jax.ShapeDtypeStruct((t, d), BF16)]
    return pl.pallas_call(
        body, name=name, grid=(t // tm,),
        in_specs=[row, vec, row, row], out_specs=[row, vec] + with_bf16 * [row],
        out_shape=[jax.ShapeDtypeStruct((t, d), F32), jax.ShapeDtypeStruct((1, d), F32)] + extra,
        compiler_params=_params("arbitrary"),
    )(x, g, dh, dres)


def _loss_head(x, g, target, name):
    t, d = x.shape
    tm = _pick(t, (512, 256, 128))

    def body(x_ref, g_ref, t_ref, loss_ref, dx_ref, dg_ref, dxb_ref):
        @pl.when(pl.program_id(0) == 0)
        def _():
            dg_ref[...] = jnp.zeros_like(dg_ref)
            loss_ref[...] = jnp.zeros_like(loss_ref)

        xv = x_ref[...]
        gv = g_ref[...]
        r = lax.rsqrt(jnp.mean(xv * xv, axis=-1, keepdims=True) + EPS)
        xhat = xv * r
        err = xhat * gv - t_ref[...]
        loss_ref[...] += jnp.sum(err * err) * (0.5 / d)
        dy = err * (1.0 / d)
        dyg = dy * gv
        dx = r * (dyg - xhat * jnp.mean(dyg * xhat, axis=-1, keepdims=True))
        dx_ref[...] = dx
        dxb_ref[...] = dx.astype(BF16)
        dg_ref[...] += jnp.sum(dy * xhat, axis=0, keepdims=True)

    row = pl.BlockSpec((tm, d), lambda i: (i, 0))
    vec = pl.BlockSpec((1, d), lambda i: (0, 0))
    return pl.pallas_call(
        body, name=name, grid=(t // tm,),
        in_specs=[row, vec, row],
        out_specs=[pl.BlockSpec((1, LANES), lambda i: (0, 0)), row, vec, row],
        out_shape=[jax.ShapeDtypeStruct((1, LANES), F32), jax.ShapeDtypeStruct((t, d), F32),
                   jax.ShapeDtypeStruct((1, d), F32), jax.ShapeDtypeStruct((t, d), BF16)],
        compiler_params=_params("arbitrary"),
    )(x, g, target)


CONV_HALO = 8
CONV_ROWS = 64


def _conv_taps(window, wv, bv):
    acc = bv + wv[SSM_CONV - 1:SSM_CONV, :] * window(0)
    for sh in range(1, SSM_CONV):
        kidx = SSM_CONV - 1 - sh
        acc = acc + wv[kidx:kidx + 1, :] * window(sh)
    return acc


def _conv_fwd(u, w, bias, name):
    b, s, c = u.shape
    rows = CONV_ROWS

    def body(u_ref, w_ref, b_ref, o_ref, ext):
        ext[0:CONV_HALO, :] = jnp.zeros((CONV_HALO, LANES), F32)
        ext[CONV_HALO:, :] = u_ref[...]
        wv, bv = w_ref[...], b_ref[...]
        for r0 in range(0, s, rows):
            acc = _conv_taps(lambda sh: ext[CONV_HALO + r0 - sh:CONV_HALO + r0 - sh + rows, :], wv, bv)
            o_ref[r0:r0 + rows, :] = acc * _sigmoid(acc)

    strip = pl.BlockSpec((None, s, LANES), lambda bi, j: (bi, 0, j))
    return pl.pallas_call(
        body, name=name, grid=(b, c // LANES),
        in_specs=[strip, pl.BlockSpec((SSM_CONV, LANES), lambda bi, j: (0, j)),
                  pl.BlockSpec((1, LANES), lambda bi, j: (0, j))],
        out_specs=strip, out_shape=jax.ShapeDtypeStruct((b, s, c), F32),
        scratch_shapes=[pltpu.VMEM((CONV_HALO + s, LANES), F32)],
        compiler_params=_params("parallel", "parallel"),
    )(u, w, bias)


def _conv_bwd(u, dout, w, bias, dproj, name):
    b, s, c = u.shape
    rows = CONV_ROWS

    def fold(v):
        return jnp.sum(v.reshape(rows // CONV_HALO, CONV_HALO, LANES), axis=0)

    def body(u_ref, d_ref, w_ref, b_ref, buf_ref, du_ref, dw_ref, db_ref, ext, dpre):
        @pl.when(pl.program_id(1) == 0)
        def _():
            dw_ref[...] = jnp.zeros_like(dw_ref)
            db_ref[...] = jnp.zeros_like(db_ref)

        ext[0:CONV_HALO, :] = jnp.zeros((CONV_HALO, LANES), F32)
        ext[CONV_HALO:, :] = u_ref[...]
        dpre[s:, :] = jnp.zeros((CONV_HALO, LANES), F32)
        wv, bv = w_ref[...], b_ref[...]
        sums = [jnp.zeros((CONV_HALO, LANES), F32)] * (SSM_CONV + 1)
        for r0 in range(0, s, rows):
            window = lambda sh: ext[CONV_HALO + r0 - sh:CONV_HALO + r0 - sh + rows, :]
            acc = _conv_taps(window, wv, bv)
            sg = _sigmoid(acc)
            dp = d_ref[r0:r0 + rows, :] * (sg * (1.0 + acc * (1.0 - sg)))
            dpre[r0:r0 + rows, :] = dp
            taps = [sums[SSM_CONV - 1 - sh] + fold(dp * window(sh)) for sh in range(SSM_CONV)]
            sums = taps[::-1] + [sums[SSM_CONV] + fold(dp)]
        for r0 in range(0, s, rows):
            du = wv[SSM_CONV - 1:SSM_CONV, :] * dpre[r0:r0 + rows, :]
            for sh in range(1, SSM_CONV):
                kidx = SSM_CONV - 1 - sh
                du = du + wv[kidx:kidx + 1, :] * dpre[r0 + sh:r0 + sh + rows, :]
            du_ref[r0:r0 + rows, :] = du.astype(BF16)
        for kidx in range(SSM_CONV):
            dw_ref[kidx:kidx + 1, :] += jnp.sum(sums[kidx], axis=0, keepdims=True)
        db_ref[...] += jnp.sum(sums[SSM_CONV], axis=0, keepdims=True)

    strip = pl.BlockSpec((None, s, LANES), lambda j, bi: (bi, 0, j))
    taps = pl.BlockSpec((SSM_CONV, LANES), lambda j, bi: (0, j))
    vec = pl.BlockSpec((1, LANES), lambda j, bi: (0, j))
    du_cols = pl.BlockSpec((None, s, LANES), lambda j, bi: (bi, 0, DPROJ_COLS["xbc"] // LANES + j))
    return pl.pallas_call(
        body, name=name, grid=(c // LANES, b),
        in_specs=[strip, strip, taps, vec, pl.BlockSpec(memory_space=pl.ANY)], out_specs=[du_cols, taps, vec],
        input_output_aliases={4: 0},
        out_shape=[jax.ShapeDtypeStruct(dproj.shape, dproj.dtype), jax.ShapeDtypeStruct((SSM_CONV, c), F32),
                   jax.ShapeDtypeStruct((1, c), F32)],
        scratch_shapes=[pltpu.VMEM((CONV_HALO + s, LANES), F32), pltpu.VMEM((s + CONV_HALO, LANES), F32)],
        compiler_params=_params("parallel", "arbitrary"),
    )(u, dout, w, bias, dproj)


def _ssd_chunk_terms(dtr_ref, bias_ref, alog_ref):
    q = SSM_CHUNK
    dt = _softplus(dtr_ref[...] + bias_ref[...])
    a_neg = -jnp.exp(alog_ref[...])
    row = lax.broadcasted_iota(jnp.int32, (q, q), 0)
    col = lax.broadcasted_iota(jnp.int32, (q, q), 1)
    lower = row >= col
    s = _mask_nn(lower, dt * a_neg)
    return dt, a_neg, s, s.T, lower


def _head_masks():
    heads = jnp.arange(LANES)[:, None]
    chans = jnp.arange(SSM_D_INNER)[None, :]
    to_channels = (chans // SSM_HEAD_DIM == heads).astype(BF16)
    return to_channels, to_channels.T


def _per_channel(v, to_channels):
    hi = v.astype(BF16)
    lo = (v - hi.astype(F32)).astype(BF16)
    return _nn(hi, to_channels) + _nn(lo, to_channels)


def _per_head(v, to_heads):
    hi = v.astype(BF16)
    lo = (v - hi.astype(F32)).astype(BF16)
    return _nn(hi, to_heads) + _nn(lo, to_heads)


def _decay_terms_per_channel(dt, s_col, to_channels):
    q = SSM_CHUNK
    tot = s_col[q - 1:q, :]
    stacked = jnp.concatenate([dt, jnp.exp(s_col), jnp.exp(tot - s_col)], axis=0)
    wide = _per_channel(stacked, to_channels)
    dtx, esx, decx = wide[:q], wide[q:2 * q], wide[2 * q:]
    return dtx, esx, decx, esx[0:1, :] * decx[0:1, :]


SSM_PAIRS_PER_GROUP = SSM_HEADS_PER_GROUP // 2
SSM_GROUP_CHANNELS = SSM_HEADS_PER_GROUP * SSM_HEAD_DIM


def _split_pair(v):
    first = lax.broadcasted_iota(jnp.int32, v.shape, 1) < SSM_HEAD_DIM
    return jnp.concatenate([jnp.where(first, v, 0.0), jnp.where(first, 0.0, v)], axis=0)


def _ssd_fwd(xc, dtr, dt_bias, a_log, dskx, to_channels, name):
    b, s, _ = xc.shape
    q = SSM_CHUNK
    nc = s // q
    n, gc = SSM_D_STATE, SSM_GROUP_CHANNELS

    def body(xc_ref, dtr_ref, bias_ref, alog_ref, dsk_ref, tc_ref, y_ref, hs_ref, h_scr):
        @pl.when(pl.program_id(1) == 0)
        def _():
            h_scr[...] = jnp.zeros_like(h_scr)

        dt, _, s_col, s_row, lower = _ssd_chunk_terms(dtr_ref, bias_ref, alog_ref)
        dtx, esx, decx, etotx = _decay_terms_per_channel(dt, s_col, tc_ref[...])
        x = xc_ref[:, :SSM_D_INNER]
        xdt = x * dtx
        xdec = xdt * decx
        skip = dsk_ref[...] * x
        for g in range(SSM_N_GROUPS):
            bg = xc_ref[:, SSM_D_INNER + n * g:SSM_D_INNER + n * (g + 1)].astype(BF16)
            cg = xc_ref[:, SSM_D_INNER + n * (SSM_N_GROUPS + g):SSM_D_INNER + n * (SSM_N_GROUPS + g + 1)].astype(BF16)
            gsl = slice(gc * g, gc * (g + 1))
            gm = _nt(cg, bg)
            hgt = h_scr[:, gsl]
            hs_ref[:, gsl] = hgt
            y_off = esx[:, gsl] * _nn(cg, hgt)
            h_scr[:, gsl] = etotx[:, gsl] * hgt + _tn(bg, xdec[:, gsl])
            for k in range(SSM_PAIRS_PER_GROUP):
                h0 = g * SSM_HEADS_PER_GROUP + 2 * k
                lo = gc * g + LANES * k
                ms = []
                for h in (h0, h0 + 1):
                    lm = jnp.exp(jnp.where(lower, s_col[:, h:h + 1] - s_row[h:h + 1, :], NEG_INF))
                    ms.append((gm * lm).astype(BF16))
                y_diag = _nn(jnp.concatenate(ms, axis=1), _split_pair(xdt[:, lo:lo + LANES]))
                y_ref[:, lo:lo + LANES] = y_diag + y_off[:, LANES * k:LANES * (k + 1)] + skip[:, lo:lo + LANES]

    vec = pl.BlockSpec((1, LANES), lambda bi, c: (0, 0))
    return pl.pallas_call(
        body, name=name, grid=(b, nc),
        in_specs=[pl.BlockSpec((None, q, SSM_CONV_DIM), lambda bi, c: (bi, c, 0)),
                  pl.BlockSpec((None, q, LANES), lambda bi, c: (bi, c, 0)), vec, vec,
                  pl.BlockSpec((1, SSM_D_INNER), lambda bi, c: (0, 0)),
                  pl.BlockSpec((LANES, SSM_D_INNER), lambda bi, c: (0, 0))],
        out_specs=[pl.BlockSpec((None, q, SSM_D_INNER), lambda bi, c: (bi, c, 0)),
                   pl.BlockSpec((None, None, n, SSM_D_INNER), lambda bi, c: (bi, c, 0, 0))],
        out_shape=[jax.ShapeDtypeStruct((b, s, SSM_D_INNER), F32),
                   jax.ShapeDtypeStruct((b, nc, n, SSM_D_INNER), F32)],
        scratch_shapes=[pltpu.VMEM((n, SSM_D_INNER), F32)],
        compiler_params=_params("parallel", "arbitrary"),
    )(xc, dtr, dt_bias, a_log, dskx, to_channels)


def _ssd_bwd(xc, dtr, dy, hs, dt_bias, a_log, dskx, to_channels, to_heads, dproj, name):
    b, s, _ = xc.shape
    q = SSM_CHUNK
    nc = s // q
    n, gc = SSM_D_STATE, SSM_GROUP_CHANNELS

    def colsum(v):
        return jnp.sum(v, axis=0, keepdims=True)

    def body(xc_ref, dtr_ref, dy_ref, hs_ref, bias_ref, alog_ref, dsk_ref, tc_ref, th_ref, buf_ref,
             dxc_ref, ddtr_ref, dalog_ref, ddsk_ref, dbias_ref, dh_scr, dxs_scr, dxd_scr, w_scr, dst_scr, rows_scr):
        ci = pl.program_id(1)

        @pl.when(ci == 0)
        def _():
            dh_scr[...] = jnp.zeros_like(dh_scr)

        @pl.when(jnp.logical_and(pl.program_id(0) == 0, ci == 0))
        def _():
            dalog_ref[...] = jnp.zeros_like(dalog_ref)
            ddsk_ref[...] = jnp.zeros_like(ddsk_ref)
            dbias_ref[...] = jnp.zeros_like(dbias_ref)
            dst_scr[...] = jnp.zeros_like(dst_scr)

        dt, a_neg, s_col, s_row, lower = _ssd_chunk_terms(dtr_ref, bias_ref, alog_ref)
        upper = jnp.logical_not(lower) | (lax.broadcasted_iota(jnp.int32, (q, q), 0)
                                          == lax.broadcasted_iota(jnp.int32, (q, q), 1))
        dtx, esx, decx, etotx = _decay_terms_per_channel(dt, s_col, tc_ref[...])
        x = xc_ref[:, :SSM_D_INNER]
        dyv = dy_ref[...]
        xdt = x * dtx
        xdec = xdt * decx
        dw = esx * dyv
        rows_scr[...] = jnp.zeros_like(rows_scr)
        for g in range(SSM_N_GROUPS):
            b_lo = SSM_D_INNER + n * g
            c_lo = SSM_D_INNER + n * (SSM_N_GROUPS + g)
            bg = xc_ref[:, b_lo:b_lo + n].astype(BF16)
            cg = xc_ref[:, c_lo:c_lo + n].astype(BF16)
            gsl = slice(gc * g, gc * (g + 1))
            gm = _nt(cg, bg)
            gmt = _nt(bg, cg)
            hgt = hs_ref[:, gsl]
            dhgt = dh_scr[:, gsl]
            w_scr[:, gsl] = _nn(cg, hgt)
            dcg = _nt(dw[:, gsl], hgt)
            dxs = decx[:, gsl] * _nn(bg, dhgt)
            dxs_scr[:, gsl] = dxs
            dbg = _nt(xdec[:, gsl], dhgt)
            rows_scr[2:3, gsl] = colsum(dhgt * hgt)
            dh_scr[:, gsl] = _tn(cg, dw[:, gsl]) + etotx[:, gsl] * dhgt
            dg = jnp.zeros((q, q), F32)
            dgt = jnp.zeros((q, q), F32)
            for k in range(SSM_PAIRS_PER_GROUP):
                h0 = g * SSM_HEADS_PER_GROUP + 2 * k
                lo = gc * g + LANES * k
                xp = xdt[:, lo:lo + LANES]
                dyp = dyv[:, lo:lo + LANES]
                dy2 = _split_pair(dyp)
                dm2 = _nt(dy2, xp)
                dmt2 = _nt(_split_pair(xp), dyp)
                mts = []
                for i, h in enumerate((h0, h0 + 1)):
                    lm = jnp.exp(jnp.where(lower, s_col[:, h:h + 1] - s_row[h:h + 1, :], NEG_INF))
                    lmt = jnp.exp(jnp.where(upper, s_row[h:h + 1, :] - s_col[:, h:h + 1], NEG_INF))
                    dm = dm2[q * i:q * (i + 1), :]
                    dmt = dmt2[q * i:q * (i + 1), :]
                    dg = dg + dm * lm
                    dgt = dgt + dmt * lmt
                    mt = gmt * lmt
                    dst_scr[h:h + 1, :] = colsum(dmt * mt) - colsum(dm * (gm * lm))
                    mts.append(mt.astype(BF16))
                dxd_scr[:, lo:lo + LANES] = _nn(jnp.concatenate(mts, axis=1), dy2)
            dxc_ref[:, b_lo:b_lo + n] = dbg + _nn(dgt, cg)
            dxc_ref[:, c_lo:c_lo + n] = dcg + _nn(dg, bg)
        dxs = dxs_scr[...]
        dxdt = dxd_scr[...] + dxs
        dxc_ref[:, :SSM_D_INNER] = dxdt * dtx + dsk_ref[...] * dyv
        state_part = xdt * dxs
        rows_scr[0:1, :] = colsum(dyv * x)
        rows_scr[1:2, :] = colsum(state_part)
        th = th_ref[...]
        per_head = _per_head(jnp.concatenate([dw * w_scr[...] - state_part, dxdt * x], axis=0), th)
        r_ds, r_dt = per_head[:q], per_head[q:]
        sums = _per_head(rows_scr[...], th)
        etot = jnp.exp(s_col[q - 1:q, :])
        dtot = sums[1:2, :] + etot * sums[2:3, :]
        last = lax.broadcasted_iota(jnp.int32, (q, LANES), 0) == q - 1
        ds = dst_scr[...].T + r_ds + jnp.where(last, dtot, 0.0)
        da = _mask_nn(upper, ds)
        ddt = da * a_neg + r_dt
        live = lax.broadcasted_iota(jnp.int32, (1, LANES), 1) < SSM_N_HEADS
        sg = _sigmoid(dtr_ref[...] + bias_ref[...])
        ddtr = jnp.where(live, ddt * sg, 0.0)
        ddtr_ref[:, :LANES] = ddtr.astype(BF16)
        ddtr_ref[:, LANES:] = jnp.zeros((q, DPROJ_DT_WIDTH - LANES), BF16)
        dalog_ref[...] += jnp.where(live, colsum(da * dt) * a_neg, 0.0)
        ddsk_ref[...] += jnp.where(live, sums[0:1, :], 0.0)
        dbias_ref[...] += colsum(ddtr)

    rev = lambda bi, c: (bi, nc - 1 - c, 0)
    vec = pl.BlockSpec((1, LANES), lambda bi, c: (0, 0))
    wide = pl.BlockSpec((None, q, SSM_D_INNER), rev)
    return pl.pallas_call(
        body, name=name, grid=(b, nc),
        in_specs=[pl.BlockSpec((None, q, SSM_CONV_DIM), rev), pl.BlockSpec((None, q, LANES), rev), wide,
                  pl.BlockSpec((None, None, n, SSM_D_INNER), lambda bi, c: (bi, nc - 1 - c, 0, 0)),
                  vec, vec, pl.BlockSpec((1, SSM_D_INNER), lambda bi, c: (0, 0)),
                  pl.BlockSpec((LANES, SSM_D_INNER), lambda bi, c: (0, 0)),
                  pl.BlockSpec((SSM_D_INNER, LANES), lambda bi, c: (0, 0)),
                  pl.BlockSpec(memory_space=pl.ANY)],
        out_specs=[pl.BlockSpec((None, q, SSM_CONV_DIM), rev),
                   pl.BlockSpec((None, q, DPROJ_DT_WIDTH),
                                lambda bi, c: (bi, nc - 1 - c, DPROJ_COLS["dt"] // DPROJ_DT_WIDTH)), vec, vec, vec],
        input_output_aliases={9: 1},
        out_shape=[jax.ShapeDtypeStruct((b, s, SSM_CONV_DIM), F32), jax.ShapeDtypeStruct(dproj.shape, dproj.dtype),
                   jax.ShapeDtypeStruct((1, LANES), F32), jax.ShapeDtypeStruct((1, LANES), F32),
                   jax.ShapeDtypeStruct((1, LANES), F32)],
        scratch_shapes=[pltpu.VMEM((n, SSM_D_INNER), F32)] + [pltpu.VMEM((q, SSM_D_INNER), F32)] * 3
        + [pltpu.VMEM((LANES, q), F32), pltpu.VMEM((8, SSM_D_INNER), F32)],
        compiler_params=_params("arbitrary", "arbitrary"),
    )(xc, dtr, dy, hs, dt_bias, a_log, dskx, to_channels, to_heads, dproj)


SSM_GROUP_WIDTH = SSM_D_INNER // SSM_N_GROUPS


def _gate_norm_fwd(y, z, w, name):
    t, d = y.shape
    tm = _pick(t, (256, 128))

    def body(y_ref, z_ref, w_ref, o_ref):
        for g in range(SSM_N_GROUPS):
            sl = slice(SSM_GROUP_WIDTH * g, SSM_GROUP_WIDTH * (g + 1))
            zv = z_ref[:, sl]
            u = y_ref[:, sl] * (zv * _sigmoid(zv))
            r = lax.rsqrt(jnp.mean(u * u, axis=-1, keepdims=True) + EPS)
            o_ref[:, sl] = ((u * r) * w_ref[:, sl]).astype(BF16)

    row = pl.BlockSpec((tm, d), lambda i: (i, 0))
    return pl.pallas_call(
        body, name=name, grid=(t // tm,),
        in_specs=[row, row, pl.BlockSpec((1, d), lambda i: (0, 0))], out_specs=row,
        out_shape=jax.ShapeDtypeStruct((t, d), BF16),
        compiler_params=_params("parallel"),
    )(y, z, w)


def _gate_norm_bwd(y, z, w, dout, dproj, name):
    t, d = y.shape
    gw = SSM_GROUP_WIDTH
    tm = _pick(t, (1024, 512, 256, 128))

    def body(y_ref, z_ref, w_ref, do_ref, buf_ref, dy_ref, dz_ref, dw_ref):
        @pl.when(pl.program_id(1) == 0)
        def _():
            dw_ref[...] = jnp.zeros_like(dw_ref)

        zv = z_ref[...]
        yv = y_ref[...]
        sg = _sigmoid(zv)
        silu = zv * sg
        u = yv * silu
        r = lax.rsqrt(jnp.mean(u * u, axis=-1, keepdims=True) + EPS)
        uh = u * r
        dov = do_ref[...]
        dw_ref[...] += jnp.sum(dov * uh, axis=0, keepdims=True)
        dyg = dov * w_ref[...]
        du = r * (dyg - uh * jnp.mean(dyg * uh, axis=-1, keepdims=True))
        dy_ref[...] = du * silu
        dz_ref[...] = (du * yv * (sg * (1.0 + zv * (1.0 - sg)))).astype(BF16)

    tile = pl.BlockSpec((tm, gw), lambda g, i: (i, g))
    vec = pl.BlockSpec((1, gw), lambda g, i: (0, g))
    z_cols = pl.BlockSpec((tm, gw), lambda g, i: (i, DPROJ_COLS["z"] // gw + g))
    return pl.pallas_call(
        body, name=name, grid=(SSM_N_GROUPS, t // tm),
        in_specs=[tile, tile, vec, tile, pl.BlockSpec(memory_space=pl.ANY)], out_specs=[tile, z_cols, vec],
        out_shape=[jax.ShapeDtypeStruct((t, d), F32), jax.ShapeDtypeStruct(dproj.shape, dproj.dtype),
                   jax.ShapeDtypeStruct((1, d), F32)],
        input_output_aliases={4: 1},
        compiler_params=_params("parallel", "arbitrary"),
    )(y, z, w, dout, dproj)


def _rope_tables(s):
    half = ATT_HEAD_DIM // 2
    inv = ROPE_THETA ** (-jnp.arange(half, dtype=F32) / half)
    ang = jnp.arange(s).astype(F32)[:, None] * inv[None, :]
    cos, sin = jnp.cos(ang), jnp.sin(ang)
    return jnp.concatenate([cos, cos], axis=-1), jnp.concatenate([-sin, sin], axis=-1)


ATT_TILE = 256


def _by_residue_spec(r, width):
    return pl.BlockSpec((None, r, ATT_TILE // r, width), lambda bi, i: (bi, 0, i, 0))


def _to_residues(tile, stage, r, store):
    if r == 1:
        store(0, tile)
        return
    stage[...] = tile
    for ri in range(r):
        store(ri, stage[pl.ds(ri, ATT_TILE // r, stride=r), :])


def _from_residues(load, stage, r):
    if r == 1:
        return load(0)
    for ri in range(r):
        stage[pl.ds(ri, ATT_TILE // r, stride=r), :] = load(ri)
    return stage[...]


def _rope_fwd(qkv, cosf, sinf, name):
    b, s, w = qkv.shape
    ts, d, gw = ATT_TILE, ATT_HEAD_DIM, ATT_OUT_DIM

    def body(x_ref, c_ref, s_ref, *rest):
        outs, stage = rest[:-1], rest[-1]
        cv, sv = c_ref[...], s_ref[...]
        for kind in range(3):
            for gi, r in enumerate(ATT_DILATIONS):
                for j in range(ATT_HEADS_PER_GROUP):
                    src = d * (kind * ATT_N_HEADS + gi * ATT_HEADS_PER_GROUP + j)
                    dst = slice(kind * gw + d * j, kind * gw + d * (j + 1))
                    tv = x_ref[:, src:src + d]
                    if kind < 2:
                        tv = tv * cv + pltpu.roll(tv, d // 2, 1) * sv

                    def store(ri, rows, o_ref=outs[gi], dst=dst):
                        o_ref[ri, :, dst] = rows.astype(BF16)

                    _to_residues(tv, stage, r, store)

    tab = pl.BlockSpec((ts, d), lambda bi, i: (i, 0))
    return pl.pallas_call(
        body, name=name, grid=(b, s // ts),
        in_specs=[pl.BlockSpec((None, ts, w), lambda bi, i: (bi, i, 0)), tab, tab],
        out_specs=[_by_residue_spec(r, 3 * gw) for r in ATT_DILATIONS],
        out_shape=[jax.ShapeDtypeStruct((b, r, s // r, 3 * gw), BF16) for r in ATT_DILATIONS],
        scratch_shapes=[pltpu.VMEM((ts, d), F32)],
        compiler_params=_params("parallel", "parallel"),
    )(qkv, cosf, sinf)


def _rope_bwd(dq, dk, dv, cosf, sinf, dproj, name):
    n_pat = len(ATT_DILATIONS)
    b, _, s, gw = dq[0].shape
    ts, d = ATT_TILE, ATT_HEAD_DIM

    def body(*refs):
        ins, (c_ref, s_ref, _, o_ref, stage) = refs[:3 * n_pat], refs[3 * n_pat:]
        cv, sv = c_ref[...], s_ref[...]
        for kind in range(3):
            for gi, r in enumerate(ATT_DILATIONS):
                src = ins[kind * n_pat + gi]
                for j in range(ATT_HEADS_PER_GROUP):
                    tv = _from_residues(lambda ri, src=src, j=j: src[ri, :, d * j:d * (j + 1)], stage, r)
                    if kind < 2:
                        tv = tv * cv + pltpu.roll(tv * sv, d // 2, 1)
                    lo = d * (kind * ATT_N_HEADS + gi * ATT_HEADS_PER_GROUP + j)
                    o_ref[:, lo:lo + d] = tv.astype(BF16)

    tab = pl.BlockSpec((ts, d), lambda bi, i: (i, 0))
    parts = [_by_residue_spec(r, gw) for r in ATT_DILATIONS]
    return pl.pallas_call(
        body, name=name, grid=(b, s // ts), in_specs=parts * 3 + [tab, tab, pl.BlockSpec(memory_space=pl.ANY)],
        out_specs=pl.BlockSpec((None, ts, ATT_QKV_DIM), lambda bi, i: (bi, i, DPROJ_COLS["qkv"] // ATT_QKV_DIM)),
        out_shape=jax.ShapeDtypeStruct(dproj.shape, dproj.dtype),
        input_output_aliases={3 * n_pat + 2: 0},
        scratch_shapes=[pltpu.VMEM((ts, d), F32)],
        compiler_params=_params("parallel", "parallel"),
    )(*dq, *dk, *dv, cosf, sinf, dproj)


ATT_SCALE = ATT_HEAD_DIM ** -0.5
ATT_STEP = 2 * ATT_BLOCK


def _att_spec(col):
    return pl.BlockSpec((None, None, ATT_STEP, ATT_OUT_DIM), lambda bi, ri, i: (bi, ri, i, col))


def _att_edge_spec(col, side, n_steps):
    def index(bi, ri, i):
        blk = 2 * i - 1 if side < 0 else 2 * i + 2
        return (bi, ri, jnp.clip(blk, 0, 2 * n_steps - 1), col)
    return pl.BlockSpec((None, None, ATT_BLOCK, ATT_OUT_DIM), index)


def _band_mask(shape, q_axis, has_prev):
    qi = lax.broadcasted_iota(jnp.int32, shape, q_axis)
    kj = lax.broadcasted_iota(jnp.int32, shape, 1 - q_axis)
    dist = qi + ATT_BLOCK - kj
    return (dist >= 0) & (dist <= ATT_BLOCK) & (has_prev | (kj >= ATT_BLOCK))


def _att_fwd(qkr, name):
    b, r, l, _ = qkr.shape
    nb = l // ATT_STEP
    d = ATT_HEAD_DIM

    def body(q_ref, kp_ref, k_ref, vp_ref, v_ref, o_ref, lse_ref):
        mask = _band_mask((ATT_STEP, ATT_BLOCK + ATT_STEP), 0, pl.program_id(2) > 0)
        for j in range(ATT_HEADS_PER_GROUP):
            sl = slice(d * j, d * (j + 1))
            kcat = jnp.concatenate([kp_ref[:, sl], k_ref[:, sl]], axis=0)
            vcat = jnp.concatenate([vp_ref[:, sl], v_ref[:, sl]], axis=0)
            sc = jnp.where(mask, _nt(q_ref[:, sl], kcat) * ATT_SCALE, NEG_INF)
            m = jnp.max(sc, axis=-1, keepdims=True)
            pr = jnp.exp(sc - m)
            den = jnp.sum(pr, axis=-1, keepdims=True)
            o_ref[:, sl] = _nn(pr / den, vcat)
            lse_ref[:, sl] = jnp.broadcast_to(m + jnp.log(den), (ATT_STEP, d))

    out_spec = _att_spec(0)
    return pl.pallas_call(
        body, name=name, grid=(b, r, nb),
        in_specs=[_att_spec(0), _att_edge_spec(1, -1, nb), _att_spec(1), _att_edge_spec(2, -1, nb), _att_spec(2)],
        out_specs=[out_spec, out_spec],
        out_shape=[jax.ShapeDtypeStruct((b, r, l, ATT_OUT_DIM), F32)] * 2,
        compiler_params=_params("parallel", "parallel", "parallel"),
    )(qkr, qkr, qkr, qkr, qkr)


def _att_merge(os_, lses, name):
    n_pat = len(os_)
    b, _, s, gw = os_[0].shape
    ts, d = ATT_TILE, ATT_HEAD_DIM

    def body(*refs):
        o_refs, l_refs = refs[:n_pat], refs[n_pat:2 * n_pat]
        att_ref, lse_outs, stage = refs[2 * n_pat], refs[2 * n_pat + 1:3 * n_pat + 1], refs[-1]
        for j in range(ATT_HEADS_PER_GROUP):
            sl = slice(d * j, d * (j + 1))
            ov = [_from_residues(lambda ri, g=g: o_refs[g][ri, :, sl], stage, r)
                  for g, r in enumerate(ATT_DILATIONS)]
            ls = [_from_residues(lambda ri, g=g: l_refs[g][ri, :, sl], stage, r)
                  for g, r in enumerate(ATT_DILATIONS)]
            m = functools.reduce(jnp.maximum, ls)
            es = [jnp.exp(lv - m) for lv in ls]
            tot = functools.reduce(lambda u, v: u + v, es)
            acc = (es[0] / tot) * ov[0]
            for g in range(1, n_pat):
                acc = acc + (es[g] / tot) * ov[g]
            att_ref[:, sl] = acc
            joint = m + jnp.log(tot)
            for g, r in enumerate(ATT_DILATIONS):
                def store(ri, rows, out=lse_outs[g]):
                    out[ri, :, sl] = rows
                _to_residues(joint, stage, r, store)

    parts = [_by_residue_spec(r, gw) for r in ATT_DILATIONS]
    return pl.pallas_call(
        body, name=name, grid=(b, s // ts), in_specs=parts * 2,
        out_specs=[pl.BlockSpec((None, ts, gw), lambda bi, i: (bi, i, 0))] + parts,
        out_shape=[jax.ShapeDtypeStruct((b, s, gw), F32)]
        + [jax.ShapeDtypeStruct((b, r, s // r, gw), F32) for r in ATT_DILATIONS],
        scratch_shapes=[pltpu.VMEM((ts, d), F32)],
        compiler_params=_params("parallel", "parallel"),
    )(*os_, *lses)


def _att_delta(att, datt, name):
    b, s, gw = att.shape
    ts, d = ATT_TILE, ATT_HEAD_DIM
    n_pat = len(ATT_DILATIONS)

    def body(a_ref, d_ref, *rest):
        do_outs, dl_outs, stage = rest[:n_pat], rest[n_pat:2 * n_pat], rest[-1]
        for j in range(ATT_HEADS_PER_GROUP):
            sl = slice(d * j, d * (j + 1))
            dv = d_ref[:, sl]
            delta = jnp.broadcast_to(jnp.sum(a_ref[:, sl] * dv, axis=-1, keepdims=True), (ts, d))
            for g, r in enumerate(ATT_DILATIONS):
                def store_do(ri, rows, out=do_outs[g]):
                    out[ri, :, sl] = rows.astype(BF16)

                def store_dl(ri, rows, out=dl_outs[g]):
                    out[ri, :, sl] = rows

                _to_residues(dv, stage, r, store_do)
                _to_residues(delta, stage, r, store_dl)

    row = pl.BlockSpec((None, ts, gw), lambda bi, i: (bi, i, 0))
    parts = [_by_residue_spec(r, gw) for r in ATT_DILATIONS]
    outs = pl.pallas_call(
        body, name=name, grid=(b, s // ts), in_specs=[row, row], out_specs=parts * 2,
        out_shape=[jax.ShapeDtypeStruct((b, r, s // r, gw), BF16) for r in ATT_DILATIONS]
        + [jax.ShapeDtypeStruct((b, r, s // r, gw), F32) for r in ATT_DILATIONS],
        scratch_shapes=[pltpu.VMEM((ts, d), F32)],
        compiler_params=_params("parallel", "parallel"),
    )(att, datt)
    return outs[:n_pat], outs[n_pat:]


def _att_bwd_q(qkr, datt, lse, delta, name):
    b, r, l, _ = qkr.shape
    nb = l // ATT_STEP
    d = ATT_HEAD_DIM

    def body(q_ref, kp_ref, k_ref, vp_ref, v_ref, do_ref, lse_ref, dl_ref, dq_ref):
        mask = _band_mask((ATT_STEP, ATT_BLOCK + ATT_STEP), 0, pl.program_id(2) > 0)
        for j in range(ATT_HEADS_PER_GROUP):
            sl = slice(d * j, d * (j + 1))
            kcat = jnp.concatenate([kp_ref[:, sl], k_ref[:, sl]], axis=0)
            vcat = jnp.concatenate([vp_ref[:, sl], v_ref[:, sl]], axis=0)
            sc = _nt(q_ref[:, sl], kcat) * ATT_SCALE
            pr = jnp.exp(jnp.where(mask, sc - lse_ref[:, d * j:d * j + 1], NEG_INF))
            dp = _nt(do_ref[:, sl], vcat)
            dsc = pr * (dp - dl_ref[:, d * j:d * j + 1])
            dq_ref[:, sl] = _nn(dsc, kcat) * ATT_SCALE

    tok = _att_spec(0)
    return pl.pallas_call(
        body, name=name, grid=(b, r, nb),
        in_specs=[_att_spec(0), _att_edge_spec(1, -1, nb), _att_spec(1), _att_edge_spec(2, -1, nb), _att_spec(2),
                  tok, tok, tok],
        out_specs=tok,
        out_shape=jax.ShapeDtypeStruct((b, r, l, ATT_OUT_DIM), F32),
        compiler_params=_params("parallel", "parallel", "parallel"),
    )(qkr, qkr, qkr, qkr, qkr, datt, lse, delta)


def _att_bwd_kv(qkr, datt, lse, delta, name):
    b, r, l, _ = qkr.shape
    nb = l // ATT_STEP
    d = ATT_HEAD_DIM

    def body(k_ref, v_ref, q_ref, qn_ref, do_ref, don_ref, lse_ref, lsen_ref, dl_ref, dln_ref, dk_ref, dv_ref):
        shape = (ATT_STEP, ATT_STEP + ATT_BLOCK)
        kj = lax.broadcasted_iota(jnp.int32, shape, 0)
        qi = lax.broadcasted_iota(jnp.int32, shape, 1)
        dist = qi - kj
        has_next = pl.program_id(2) < nb - 1
        mask = (dist >= 0) & (dist <= ATT_BLOCK) & (has_next | (qi < ATT_STEP))
        for j in range(ATT_HEADS_PER_GROUP):
            sl = slice(d * j, d * (j + 1))
            qcat = jnp.concatenate([q_ref[:, sl], qn_ref[:, sl]], axis=0)
            docat = jnp.concatenate([do_ref[:, sl], don_ref[:, sl]], axis=0)
            lse_t = jnp.tile(jnp.concatenate([lse_ref[:, sl], lsen_ref[:, sl]], axis=0).T, (ATT_STEP // d, 1))
            dl_t = jnp.tile(jnp.concatenate([dl_ref[:, sl], dln_ref[:, sl]], axis=0).T, (ATT_STEP // d, 1))
            sc_t = _nt(k_ref[:, sl], qcat) * ATT_SCALE
            pr_t = jnp.exp(jnp.where(mask, sc_t - lse_t, NEG_INF))
            dv_ref[:, sl] = _nn(pr_t, docat)
            dsc_t = pr_t * (_nt(v_ref[:, sl], docat) - dl_t)
            dk_ref[:, sl] = _nn(dsc_t, qcat) * ATT_SCALE

    tok, tok_n = _att_spec(0), _att_edge_spec(0, 1, nb)
    return pl.pallas_call(
        body, name=name, grid=(b, r, nb),
        in_specs=[_att_spec(1), _att_spec(2), _att_spec(0), _att_edge_spec(0, 1, nb),
                  tok, tok_n, tok, tok_n, tok, tok_n],
        out_specs=[tok, tok],
        out_shape=[jax.ShapeDtypeStruct((b, r, l, ATT_OUT_DIM), F32)] * 2,
        compiler_params=_params("parallel", "parallel", "parallel"),
    )(qkr, qkr, qkr, qkr, datt, datt, lse, lse, delta, delta)


def _mix_fwd(gl, bg, ys, ya, name):
    t, d = ys.shape
    tm = _pick(t, (512, 256, 128))

    def body(gl_ref, bg_ref, ys_ref, ya_ref, o_ref):
        g0 = _sigmoid(gl_ref[:, :d] + bg_ref[:, :d])
        g1 = _sigmoid(gl_ref[:, d:] + bg_ref[:, d:])
        o_ref[...] = (g0 * ys_ref[...] + g1 * ya_ref[...]).astype(BF16)

    row = pl.BlockSpec((tm, d), lambda i: (i, 0))
    return pl.pallas_call(
        body, name=name, grid=(t // tm,),
        in_specs=[pl.BlockSpec((tm, 2 * d), lambda i: (i, 0)), pl.BlockSpec((1, 2 * d), lambda i: (0, 0)), row, row],
        out_specs=row, out_shape=jax.ShapeDtypeStruct((t, d), BF16),
        compiler_params=_params("parallel"),
    )(gl, bg, ys, ya)


def _mix_bwd(gl, bg, ys, ya, dmixed, name):
    t, d = ys.shape
    tm = _pick(t, (512, 256, 128))

    def body(gl_ref, bg_ref, ys_ref, ya_ref, dm_ref, dys_ref, dya_ref, dgl_ref, dbg_ref):
        @pl.when(pl.program_id(0) == 0)
        def _():
            dbg_ref[...] = jnp.zeros_like(dbg_ref)

        dm = dm_ref[...]
        g0 = _sigmoid(gl_ref[:, :d] + bg_ref[:, :d])
        g1 = _sigmoid(gl_ref[:, d:] + bg_ref[:, d:])
        dys_ref[...] = (dm * g0).astype(BF16)
        dya_ref[...] = (dm * g1).astype(BF16)
        d0 = dm * ys_ref[...] * (g0 * (1.0 - g0))
        d1 = dm * ya_ref[...] * (g1 * (1.0 - g1))
        dgl_ref[:, :d] = d0.astype(BF16)
        dgl_ref[:, d:] = d1.astype(BF16)
        dbg_ref[:, :d] += jnp.sum(d0, axis=0, keepdims=True)
        dbg_ref[:, d:] += jnp.sum(d1, axis=0, keepdims=True)

    row = pl.BlockSpec((tm, d), lambda i: (i, 0))
    wide = pl.BlockSpec((tm, 2 * d), lambda i: (i, 0))
    vec = pl.BlockSpec((1, 2 * d), lambda i: (0, 0))
    gate_cols = pl.BlockSpec((tm, 2 * d), lambda i: (i, DPROJ_COLS["gate"] // (2 * d)))
    return pl.pallas_call(
        body, name=name, grid=(t // tm,),
        in_specs=[wide, vec, row, row, row], out_specs=[row, row, gate_cols, vec],
        out_shape=[jax.ShapeDtypeStruct((t, d), BF16), jax.ShapeDtypeStruct((t, d), BF16),
                   jax.ShapeDtypeStruct((t, DPROJ_WIDTH), BF16), jax.ShapeDtypeStruct((1, 2 * d), F32)],
        compiler_params=_params("arbitrary"),
    )(gl, bg, ys, ya, dmixed)


def _swiglu_fwd(gt, up, name):
    t, f = gt.shape
    tm = _pick(t, (512, 256, 128))

    def body(g_ref, u_ref, o_ref):
        gv = g_ref[...]
        o_ref[...] = ((gv * _sigmoid(gv)) * u_ref[...]).astype(BF16)

    row = pl.BlockSpec((tm, f), lambda i: (i, 0))
    return pl.pallas_call(
        body, name=name, grid=(t // tm,), in_specs=[row, row], out_specs=row,
        out_shape=jax.ShapeDtypeStruct((t, f), BF16), compiler_params=_params("parallel"),
    )(gt, up)


def _swiglu_bwd(gt, up, dact, name):
    t, f = gt.shape
    tm = _pick(t, (512, 256, 128))

    def body(g_ref, u_ref, d_ref, dg_ref, du_ref):
        gv = g_ref[...]
        dv = d_ref[...]
        sg = _sigmoid(gv)
        dg_ref[...] = (dv * u_ref[...] * (sg * (1.0 + gv * (1.0 - sg)))).astype(BF16)
        du_ref[...] = (dv * (gv * sg)).astype(BF16)

    row = pl.BlockSpec((tm, f), lambda i: (i, 0))
    return pl.pallas_call(
        body, name=name, grid=(t // tm,), in_specs=[row, row, row], out_specs=[row, row],
        out_shape=[jax.ShapeDtypeStruct((t, f), BF16)] * 2, compiler_params=_params("parallel"),
    )(gt, up, dact)


def _peer(k):
    x, y, c = lax.axis_index("x"), lax.axis_index("y"), lax.axis_index("c")
    px, py, pc = x ^ ((k >> 2) & 1), y ^ ((k >> 1) & 1), c ^ (k & 1)
    return (px, py, pc), 4 * px + 2 * py + pc


def _my_index():
    return 4 * lax.axis_index("x") + 2 * lax.axis_index("y") + lax.axis_index("c")


def _all_gather(parts, name):
    n_parts = len(parts)

    def body(*refs):
        ins, outs = refs[:n_parts], refs[n_parts:2 * n_parts]
        send_sems, recv_sems, local_sems = refs[2 * n_parts:]
        here, me = _peer(0)
        sibling, sib_idx = _peer(1)
        chips = [_peer(2 * q) for q in range(1, N_CHIPS)]

        def copy(i, k, block, to, src=None):
            return pltpu.make_async_remote_copy(
                src_ref=outs[i].at[block] if src is None else src, dst_ref=outs[i].at[block],
                send_sem=send_sems.at[i * (N_DEV - 1) + k], recv_sem=recv_sems.at[i * (N_DEV - 1) + k],
                device_id=to, device_id_type=MESH)

        local = [pltpu.make_async_copy(ins[i], outs[i].at[me], local_sems.at[i]) for i in range(n_parts)]
        for cp in local:
            cp.start()
        sends = []
        for i in range(n_parts):
            sends.append(copy(i, 0, me, sibling, src=ins[i]))
            sends += [copy(i, q, me, chip, src=ins[i]) for q, (chip, _) in enumerate(chips, start=1)]
        for cp in sends:
            cp.start()
        for q, (chip, chip_idx) in enumerate(chips, start=1):
            for i in range(n_parts):
                copy(i, q, chip_idx, here).wait_recv()
                fwd = copy(i, N_CHIPS - 1 + q, chip_idx, sibling)
                fwd.start()
                sends.append(fwd)
        for i in range(n_parts):
            copy(i, 0, sib_idx, here).wait_recv()
        for q, (_, chip_idx) in enumerate(chips, start=1):
            for i in range(n_parts):
                copy(i, N_CHIPS - 1 + q, chip_idx ^ 1, here).wait_recv()
        for cp in sends:
            cp.wait_send()
        for cp in local:
            cp.wait()

    hbm = pl.BlockSpec(memory_space=pl.ANY)
    return pl.pallas_call(
        body, name=name, in_specs=[hbm] * n_parts, out_specs=[hbm] * n_parts,
        out_shape=[jax.ShapeDtypeStruct((N_DEV,) + p_.shape, p_.dtype) for p_ in parts],
        scratch_shapes=[pltpu.SemaphoreType.DMA((n_parts * (N_DEV - 1),)),
                        pltpu.SemaphoreType.DMA((n_parts * (N_DEV - 1),)),
                        pltpu.SemaphoreType.DMA((n_parts,))],
        compiler_params=pltpu.CompilerParams(has_side_effects=True),
    )(*parts)


HBM_SPEC = pl.BlockSpec(memory_space=pltpu.HBM)
SEM_SPEC = pl.BlockSpec(memory_space=pltpu.SEMAPHORE)
DATAFLOW = pltpu.SideEffectType.DATAFLOW_SIDE_EFFECTING


def _gather_start(block, after, name):
    per_peer = block.ndim == 3

    def body(v_ref, land_ref, after_ref, send_sems, recv_sems, v_thru, land_thru, token):
        me = _my_index()
        for k in range(1, N_DEV):
            peer, pidx = _peer(k)
            pltpu.make_async_remote_copy(
                src_ref=v_ref.at[pidx] if per_peer else v_ref, dst_ref=land_ref.at[me],
                send_sem=send_sems.at[k - 1], recv_sem=recv_sems.at[k - 1],
                device_id=peer, device_id_type=MESH).start()
        token[...] = jnp.zeros_like(token)

    land_shape = (N_DEV,) + block.shape[-2:]
    return pl.pallas_call(
        body, name=name,
        out_shape=(pltpu.SemaphoreType.DMA((N_DEV - 1,)), pltpu.SemaphoreType.DMA((N_DEV - 1,)),
                   pltpu.HBM(block.shape, block.dtype), pltpu.HBM(land_shape, block.dtype),
                   jax.ShapeDtypeStruct((8, LANES), F32)),
        in_specs=(HBM_SPEC, HBM_SPEC, pl.BlockSpec(memory_space=pl.ANY)),
        out_specs=(SEM_SPEC, SEM_SPEC, HBM_SPEC, HBM_SPEC, pl.BlockSpec(memory_space=pltpu.VMEM)),
        input_output_aliases={0: 2, 1: 3},
        compiler_params=pltpu.CompilerParams(has_side_effects=DATAFLOW),
    )(pltpu.with_memory_space_constraint(block, pltpu.HBM),
      pltpu.with_memory_space_constraint(lax.empty(land_shape, block.dtype), pltpu.HBM), after)


def _gather_wait(send_sems, recv_sems, block, landing, after, name):
    per_peer = block.ndim == 3

    def body(v_ref, land_ref, send_sems, recv_sems, after_ref, v_dead, got_ref):
        for k in range(1, N_DEV):
            peer, pidx = _peer(k)
            copy = pltpu.make_async_remote_copy(
                src_ref=v_ref.at[pidx] if per_peer else v_ref, dst_ref=land_ref.at[pidx],
                send_sem=send_sems.at[k - 1], recv_sem=recv_sems.at[k - 1],
                device_id=peer, device_id_type=MESH)
            copy.wait_send()
            copy.wait_recv()

    return pl.pallas_call(
        body, name=name,
        out_shape=(pltpu.HBM(block.shape, block.dtype), pltpu.HBM(landing.shape, landing.dtype)),
        in_specs=(HBM_SPEC, HBM_SPEC, SEM_SPEC, SEM_SPEC, pl.BlockSpec(memory_space=pl.ANY)),
        out_specs=(HBM_SPEC, HBM_SPEC), input_output_aliases={0: 0, 1: 1},
        compiler_params=pltpu.CompilerParams(has_side_effects=DATAFLOW),
    )(block, landing, send_sems, recv_sems, after)[1]


TILE_ELEMS = 1024 * 1024


def _pair_exchange(slabs, name):
    def body(slab_ref, got_ref, send_sems, recv_sems):
        c = lax.axis_index("c")
        sibling, _ = _peer(1)
        copies = [pltpu.make_async_remote_copy(
            src_ref=slab_ref.at[2 * q + 1 - c], dst_ref=got_ref.at[q], send_sem=send_sems.at[q],
            recv_sem=recv_sems.at[q], device_id=sibling, device_id_type=MESH) for q in range(N_CHIPS)]
        for cp in copies:
            cp.start()
        for cp in copies:
            cp.wait()

    hbm = pl.BlockSpec(memory_space=pl.ANY)
    return pl.pallas_call(
        body, name=name, in_specs=[hbm], out_specs=hbm,
        out_shape=jax.ShapeDtypeStruct((N_CHIPS,) + slabs.shape[1:], slabs.dtype),
        scratch_shapes=[pltpu.SemaphoreType.DMA((N_CHIPS,)), pltpu.SemaphoreType.DMA((N_CHIPS,))],
        compiler_params=pltpu.CompilerParams(has_side_effects=True),
    )(slabs)


def _chip_sum(slabs, got, core, name):
    _, rows, lanes = slabs.shape
    tr = _tile_rows(rows, TILE_ELEMS // lanes, 16)

    def body(core_ref, mine_ref, got_ref, o_ref):
        o_ref[...] = (mine_ref[...].astype(F32) + got_ref[...].astype(F32)).astype(BF16)

    return pl.pallas_call(
        body, name=name,
        grid_spec=pltpu.PrefetchScalarGridSpec(
            num_scalar_prefetch=1, grid=(N_CHIPS, rows // tr),
            in_specs=[pl.BlockSpec((None, tr, lanes), lambda q, i, core_ref: (2 * q + core_ref[0], i, 0)),
                      pl.BlockSpec((None, tr, lanes), lambda q, i, core_ref: (q, i, 0))],
            out_specs=pl.BlockSpec((None, tr, lanes), lambda q, i, core_ref: (q, i, 0))),
        out_shape=jax.ShapeDtypeStruct((N_CHIPS, rows, lanes), BF16),
        compiler_params=_params("parallel", "parallel"),
    )(core, slabs, got)


def _chip_exchange(chip_sums, shared, name):
    def body(sum_ref, sh_ref, got_ref, gsh_ref, send_sems, recv_sems, sh_send_sems, sh_recv_sems, local_sems):
        me = _my_index()
        my_chip = me >> 1
        local = [pltpu.make_async_copy(sum_ref.at[my_chip], got_ref.at[my_chip], local_sems.at[0]),
                 pltpu.make_async_copy(sh_ref, gsh_ref.at[me], local_sems.at[1])]
        for cp in local:
            cp.start()
        sends = []
        for q in range(1, N_CHIPS):
            peer, pidx = _peer(2 * q)
            cp = pltpu.make_async_remote_copy(
                src_ref=sum_ref.at[pidx >> 1], dst_ref=got_ref.at[my_chip], send_sem=send_sems.at[q - 1],
                recv_sem=recv_sems.at[q - 1], device_id=peer, device_id_type=MESH)
            cp.start()
            sends.append(cp)
        for k in range(1, N_DEV):
            peer, _ = _peer(k)
            cp = pltpu.make_async_remote_copy(
                src_ref=sh_ref, dst_ref=gsh_ref.at[me], send_sem=sh_send_sems.at[k - 1],
                recv_sem=sh_recv_sems.at[k - 1], device_id=peer, device_id_type=MESH)
            cp.start()
            sends.append(cp)
        for q in range(1, N_CHIPS):
            peer, pidx = _peer(2 * q)
            pltpu.make_async_remote_copy(
                src_ref=sum_ref.at[my_chip], dst_ref=got_ref.at[pidx >> 1], send_sem=send_sems.at[q - 1],
                recv_sem=recv_sems.at[q - 1], device_id=peer, device_id_type=MESH).wait_recv()
        for k in range(1, N_DEV):
            peer, pidx = _peer(k)
            pltpu.make_async_remote_copy(
                src_ref=sh_ref, dst_ref=gsh_ref.at[pidx], send_sem=sh_send_sems.at[k - 1],
                recv_sem=sh_recv_sems.at[k - 1], device_id=peer, device_id_type=MESH).wait_recv()
        for cp in sends:
            cp.wait_send()
        for cp in local:
            cp.wait()

    hbm = pl.BlockSpec(memory_space=pl.ANY)
    return pl.pallas_call(
        body, name=name, in_specs=[hbm, hbm], out_specs=[hbm, hbm],
        out_shape=[jax.ShapeDtypeStruct(chip_sums.shape, chip_sums.dtype),
                   jax.ShapeDtypeStruct((N_DEV,) + shared.shape, shared.dtype)],
        scratch_shapes=[pltpu.SemaphoreType.DMA((N_CHIPS - 1,)), pltpu.SemaphoreType.DMA((N_CHIPS - 1,)),
                        pltpu.SemaphoreType.DMA((N_DEV - 1,)), pltpu.SemaphoreType.DMA((N_DEV - 1,)),
                        pltpu.SemaphoreType.DMA((2,))],
        compiler_params=pltpu.CompilerParams(has_side_effects=True),
    )(chip_sums, shared)


def _adamw(parts, w, m, v, name, row0=0):
    n_parts, rows, lanes = parts.shape
    tr = rows if rows * lanes <= TILE_ELEMS // 2 else _tile_rows(math.gcd(rows, row0), TILE_ELEMS // 4 // lanes, 8)
    c1 = 1.0 - ADAM_B1 ** ADAM_STEP
    c2 = 1.0 - ADAM_B2 ** ADAM_STEP

    def body(p_ref, w_ref, m_ref, v_ref, g_ref, d_ref, nm_ref, nv_ref):
        g = p_ref[0].astype(F32)
        for j in range(1, n_parts):
            g = g + p_ref[j].astype(F32)
        nm = ADAM_B1 * m_ref[...] + (1.0 - ADAM_B1) * g
        nv = ADAM_B2 * v_ref[...] + (1.0 - ADAM_B2) * (g * g)
        g_ref[...] = g
        nm_ref[...] = nm
        nv_ref[...] = nv
        d_ref[...] = -ADAM_LR * ((nm / c1) / (jnp.sqrt(nv / c2) + ADAM_EPS) + ADAM_WD * w_ref[...])

    row = pl.BlockSpec((tr, lanes), lambda i: (i, 0))
    state = pl.BlockSpec((tr, lanes), lambda i: (row0 // tr + i, 0))
    return pl.pallas_call(
        body, name=name, grid=(rows // tr,),
        in_specs=[pl.BlockSpec((n_parts, tr, lanes), lambda i: (0, i, 0)), state, state, state],
        out_specs=[row] * 4, out_shape=[jax.ShapeDtypeStruct((rows, lanes), F32)] * 4,
        compiler_params=_params("parallel"),
    )(parts, w, m, v)


MATRIX_SHARDS = (
    ("w_in", (D_MODEL, IN_PROJ_DIM // N_DEV), True),
    ("w_ssm_out", (SSM_D_INNER // N_DEV, D_MODEL), False),
    ("w_att_out", (ATT_OUT_DIM, D_MODEL // N_DEV), True),
    ("w_mix_out", (D_MODEL // N_DEV, D_MODEL), False),
    ("w_ffn_gate", (D_MODEL, D_FF // N_DEV), True),
    ("w_ffn_up", (D_MODEL, D_FF // N_DEV), True),
    ("w_ffn_down", (D_FF // N_DEV, D_MODEL), False),
)
CONV_SHARD = ("conv_w", (SSM_CONV, SSM_CONV_DIM // N_DEV), True)
SHARDED = MATRIX_SHARDS + (CONV_SHARD,)
REPLICATED = (("norm_mix", D_MODEL), ("b_gate", 2 * D_MODEL), ("conv_b", SSM_CONV_DIM), ("dt_bias", SSM_N_HEADS),
              ("a_log", SSM_N_HEADS), ("d_skip", SSM_N_HEADS), ("ssm_norm", SSM_D_INNER), ("norm_ffn", D_MODEL),
              ("norm_final", D_MODEL))


def _round_up(n, mult):
    return -(-n // mult) * mult


def _pack_rows(flat, row_mult):
    rows = _round_up(-(-flat.shape[0] // LANES), row_mult)
    return jnp.pad(flat, (0, rows * LANES - flat.shape[0])).reshape(rows, LANES)


def _stacking(specs):
    return tuple((name, (shape[1], shape[0]) if by_cols else shape, by_cols) for name, shape, by_cols in specs)


def _to_stacking(vals, specs):
    return {name: (vals[name].T if by_cols else vals[name]) for name, _, by_cols in specs}


STACK_WIDTH = D_MODEL
STACK_ALIGN = 16
STACK_ORDER = ("w_ssm_out", "w_mix_out", "w_ffn_gate", "w_ffn_up", "w_ffn_down", "w_att_out", "conv_w", "w_in")
GATHER_LATER = STACK_ORDER[:-1]
REDUCE_EARLY = STACK_ORDER[:5]
REDUCE_LATE = STACK_ORDER[5:]


def _stack_layout():
    shapes = {name: shape for name, shape, _ in _stacking(SHARDED)}
    layout, off = {}, 0
    for name in STACK_ORDER:
        r, c = shapes[name]
        rows = r if c == STACK_WIDTH else _round_up(-(-(r * c) // STACK_WIDTH), STACK_ALIGN)
        layout[name] = (off, rows, (r, c))
        off = _round_up(off + rows, STACK_ALIGN)
    return layout, _round_up(off, 1024)


def _to_stack_rows(v, rows):
    if v.shape[-1] == STACK_WIDTH:
        return v
    lead = v.shape[:-2]
    flat = v.reshape(lead + (-1,))
    flat = jnp.pad(flat, [(0, 0)] * len(lead) + [(0, rows * STACK_WIDTH - flat.shape[-1])])
    return flat.reshape(lead + (rows, STACK_WIDTH))


def _from_stack_rows(block, shape):
    r, c = shape
    if c == STACK_WIDTH:
        return block
    lead = block.shape[:-2]
    return block.reshape(lead + (-1,))[..., :r * c].reshape(lead + (r, c))


def _stack(vals, dtype, skip=(), names=STACK_ORDER):
    layout, total = _stack_layout()
    order = names
    after = STACK_ORDER.index(order[-1]) + 1
    if after < len(STACK_ORDER):
        total = layout[STACK_ORDER[after]][0]
    lead = next(iter(vals.values())).shape[:-2]
    pieces = []
    for i, name in enumerate(order):
        off, rows, _ = layout[name]
        until = layout[order[i + 1]][0] if i + 1 < len(order) else total
        piece = jnp.zeros(lead + (rows, STACK_WIDTH), dtype) if name in skip else _to_stack_rows(vals[name], rows)
        pieces.append(jnp.pad(piece.astype(dtype), [(0, 0)] * len(lead) + [(0, until - off - rows), (0, 0)]))
    return jnp.concatenate(pieces, axis=-2)


def _unstack(stacked, names):
    layout, _ = _stack_layout()
    row0 = layout[names[0]][0]
    return {name: _from_stack_rows(stacked[..., layout[name][0] - row0:layout[name][0] - row0 + layout[name][1], :],
                                   layout[name][2]) for name in names}


W_IN_SHARD_ROWS = IN_PROJ_DIM // N_DEV


def _w_in_row_moves():
    moves, orig = [], 0
    for name, size in IN_SPLIT:
        for j in range(N_DEV):
            lo, hi = max(orig, W_IN_SHARD_ROWS * j), min(orig + size, W_IN_SHARD_ROWS * (j + 1))
            if lo < hi:
                moves.append((j, lo - W_IN_SHARD_ROWS * j, DPROJ_COLS[name] + lo - orig, hi - lo))
        orig += size
    return moves


def _w_in_from_shards(shards, name):
    total, base = shards.shape[1], 0
    pad_lo, pad_hi = DPROJ_COLS["dt"] + _round_up(SSM_N_HEADS, STACK_ALIGN), DPROJ_COLS["dt"] + DPROJ_DT_WIDTH

    def body(x_ref, o_ref):
        o_ref[pad_lo:pad_hi, :] = jnp.zeros((pad_hi - pad_lo, LANES), x_ref.dtype)
        for j, r, at, n in _w_in_row_moves():
            o_ref[at:at + n, :] = x_ref[j, base + r:base + r + n, :]

    return pl.pallas_call(
        body, name=name, grid=(STACK_WIDTH // LANES,),
        in_specs=[pl.BlockSpec((N_DEV, total, LANES), lambda c: (0, 0, c))],
        out_specs=pl.BlockSpec((DPROJ_WIDTH, LANES), lambda c: (0, c)),
        out_shape=jax.ShapeDtypeStruct((DPROJ_WIDTH, STACK_WIDTH), shards.dtype),
        compiler_params=_params("parallel"),
    )(shards)


def _w_in_to_shards(dw_all, head, name):
    layout, total = _stack_layout()
    total -= layout[REDUCE_LATE[0]][0]
    base = head.shape[1]
    end = base + W_IN_SHARD_ROWS

    def body(x_ref, h_ref, o_ref):
        o_ref[:, 0:base, :] = h_ref[...]
        for j, r, at, n in _w_in_row_moves():
            o_ref[j, base + r:base + r + n, :] = x_ref[at:at + n, :]
        o_ref[:, end:total, :] = jnp.zeros((N_DEV, total - end, LANES), o_ref.dtype)

    return pl.pallas_call(
        body, name=name, grid=(STACK_WIDTH // LANES,),
        in_specs=[pl.BlockSpec((DPROJ_WIDTH, LANES), lambda c: (0, c)),
                  pl.BlockSpec((N_DEV, base, LANES), lambda c: (0, 0, c))],
        out_specs=pl.BlockSpec((N_DEV, total, LANES), lambda c: (0, 0, c)),
        out_shape=jax.ShapeDtypeStruct((N_DEV, total, STACK_WIDTH), dw_all.dtype),
        compiler_params=_params("parallel"),
    )(dw_all, head)


REPLICATED_ROWS = sum(-(-size // LANES) for _, size in REPLICATED)
LOSS_ROW = REPLICATED_ROWS


def _pack_replicated(vals):
    rows = []
    for name, size in REPLICATED:
        v = vals[name].reshape(-1).astype(F32)
        rows.append(jnp.pad(v, (0, _round_up(size, LANES) - size)))
    return _pack_rows(jnp.concatenate(rows), 8)


def _unpack_replicated(packed, shapes):
    flat = packed.reshape(-1)
    out, off = {}, 0
    for name, size in REPLICATED:
        out[name] = flat[off:off + size].reshape(shapes[name])
        off += _round_up(size, LANES)
    return out


def _lane_row(v):
    v = v.reshape(-1).astype(F32)
    return jnp.pad(v, (0, LANES - v.shape[0])).reshape(1, LANES)


IN_SPLIT = (("z", SSM_D_INNER), ("xbc", SSM_CONV_DIM), ("dt", SSM_N_HEADS), ("qkv", ATT_QKV_DIM), ("gate", 2 * D_MODEL))


def kernel(x, norm_mix, w_in, b_gate, conv_w, conv_b, dt_bias, a_log, d_skip, ssm_norm, w_ssm_out, w_att_out, w_mix_out, norm_ffn, w_ffn_gate, w_ffn_up, w_ffn_down, norm_final, loss_target, m_norm_mix, m_w_in, m_b_gate, m_conv_w, m_conv_b, m_dt_bias, m_a_log, m_d_skip, m_ssm_norm, m_w_ssm_out, m_w_att_out, m_w_mix_out, m_norm_ffn, m_w_ffn_gate, m_w_ffn_up, m_w_ffn_down, m_norm_final, v_norm_mix, v_w_in, v_b_gate, v_conv_w, v_conv_b, v_dt_bias, v_a_log, v_d_skip, v_ssm_norm, v_w_ssm_out, v_w_att_out, v_w_mix_out, v_norm_ffn, v_w_ffn_gate, v_w_ffn_up, v_w_ffn_down, v_norm_final):
    given = dict(locals())
    weights = {name: given[name][0] for name, _, _ in SHARDED}
    b, s, d = x.shape
    t = b * s

    stacking = _to_stacking(weights, SHARDED)
    conv_shape = dict((name, shape) for name, shape, _ in _stacking(SHARDED))["conv_w"]
    w_in_local = jnp.pad(stacking["w_in"].astype(BF16), ((0, -W_IN_SHARD_ROWS % STACK_ALIGN), (0, 0)))
    conv_local = _pack_rows(stacking["conv_w"].reshape(-1), 8)
    w_in_shards, conv_all = _all_gather([w_in_local, conv_local], "w_in_all_gather")
    head_local = _stack(stacking, BF16, skip=("conv_w",), names=GATHER_LATER)
    in_flight = _gather_start(head_local, conv_all, "weights_gather_start")
    w_in_all = _w_in_from_shards(w_in_shards, "w_in_from_shards")
    w_sec = {name: w_in_all[DPROJ_COLS[name]:DPROJ_COLS[name] + _round_up(size, LANES)] for name, size in IN_SPLIT}
    conv_size = conv_shape[0] * conv_shape[1]
    conv_taps = conv_all.reshape(N_DEV, -1)[:, :conv_size].reshape(N_DEV * conv_shape[0], conv_shape[1]).T

    g_mix, g_ffn, g_fin = norm_mix.reshape(1, d), norm_ffn.reshape(1, d), norm_final.reshape(1, d)
    g_mix = g_mix + in_flight[4][:1, :1]
    bg_row = b_gate.reshape(1, 2 * d)
    convb_row = conv_b.reshape(1, SSM_CONV_DIM)
    ssmn_row = ssm_norm.reshape(1, SSM_D_INNER)
    dtb_row, alog_row = _lane_row(dt_bias), _lane_row(a_log)
    cosf, sinf = _rope_tables(s)

    x2d = x.reshape(t, d)
    h1 = _rmsnorm_fwd(x2d, g_mix, "norm_mix_fwd")
    proj = {name: _mm(h1, w_sec[name], mode="nt", name="in_proj_" + name) for name, _ in IN_SPLIT}
    xbc3 = proj["xbc"].reshape(b, s, SSM_CONV_DIM)
    xc = _conv_fwd(xbc3, conv_taps, convb_row, "conv_fwd")
    dtr3 = proj["dt"].reshape(b, s, DT_PAD)
    to_channels, to_heads = _head_masks()
    dskx = jnp.repeat(d_skip.reshape(-1).astype(F32), SSM_HEAD_DIM).reshape(1, SSM_D_INNER)
    y_ssd, h_states = _ssd_fwd(xc, dtr3, dtb_row, alog_row, dskx, to_channels, "ssd_fwd")
    y_ssd2 = y_ssd.reshape(t, SSM_D_INNER)
    ynorm = _gate_norm_fwd(y_ssd2, proj["z"], ssmn_row, "ssd_gate_norm_fwd")
    landed = _gather_wait(*in_flight[:4], ynorm, "weights_gather_wait")
    head_all = lax.dynamic_update_slice(landed, head_local[None], (_my_index(), 0, 0))
    full = {name: v.reshape((-1,) + v.shape[2:]) for name, v in _unstack(head_all, STACK_ORDER[:-2]).items()}
    y_ssm = _mm(ynorm, full["w_ssm_out"], mode="nn", name="ssm_out_proj")

    qkv3 = proj["qkv"].reshape(b, s, ATT_QKV_DIM)
    qk_parts = _rope_fwd(qkv3, cosf, sinf, "rope_fwd")
    att_parts = [_att_fwd(qk_parts[gi], "att_fwd_%d" % r) for gi, r in enumerate(ATT_DILATIONS)]
    att, *lse_parts = _att_merge([o for o, _ in att_parts], [l_ for _, l_ in att_parts], "att_merge")
    att2 = att.reshape(t, ATT_OUT_DIM)
    y_att = _mm(att2, full["w_att_out"], mode="nt", name="att_out_proj")

    mixed = _mix_fwd(proj["gate"], bg_row, y_ssm, y_att, "mix_fwd")
    x2 = _mm(mixed, full["w_mix_out"], mode="nn", name="mix_out_proj", add=x2d)
    h2 = _rmsnorm_fwd(x2, g_ffn, "norm_ffn_fwd")
    gt = _mm(h2, full["w_ffn_gate"], mode="nt", name="ffn_gate_proj")
    up = _mm(h2, full["w_ffn_up"], mode="nt", name="ffn_up_proj")
    act = _swiglu_fwd(gt, up, "swiglu_fwd")
    x3 = _mm(act, full["w_ffn_down"], mode="nn", name="ffn_down_proj", add=x2)

    loss_row, dx3, dg_fin, dx3b = _loss_head(x3, g_fin, loss_target.reshape(t, d), "loss_head")
    grads = {}
    dact = _mm(dx3b, full["w_ffn_down"], mode="nt", name="ffn_down_dx")
    grads["w_ffn_down"] = _mm(act, dx3b, mode="tn", name="ffn_down_dw", out_dtype=BF16)
    dgt, dup = _swiglu_bwd(gt, up, dact, "swiglu_bwd")
    grads["w_ffn_gate"] = _mm(dgt, h2, mode="tn", name="ffn_gate_dw", out_dtype=BF16)
    grads["w_ffn_up"] = _mm(dup, h2, mode="tn", name="ffn_up_dw", out_dtype=BF16)
    dh2 = _mm(dgt, full["w_ffn_gate"], mode="nn", name="ffn_gate_dx")
    dh2 = _mm(dup, full["w_ffn_up"], mode="nn", name="ffn_up_dx", add=dh2)
    dx2, dg_ffn, dx2b = _rmsnorm_bwd(x2, g_ffn, dh2, dx3, "norm_ffn_bwd", with_bf16=True)

    dmixed = _mm(dx2b, full["w_mix_out"], mode="nt", name="mix_out_dx")
    grads["w_mix_out"] = _mm(mixed, dx2b, mode="tn", name="mix_out_dw", out_dtype=BF16)
    dys, dya, dproj, dbg = _mix_bwd(proj["gate"], bg_row, y_ssm, y_att, dmixed, "mix_bwd")

    grads["w_ssm_out"] = _mm(ynorm, dys, mode="tn", name="ssm_out_dw", out_dtype=BF16)
    early = _stack({name: grads[name].reshape((N_DEV, -1, STACK_WIDTH)) for name in REDUCE_EARLY}, BF16,
                   names=REDUCE_EARLY)
    early_flight = _gather_start(early, dys, "grads_scatter_start")
    ssmn_row = ssmn_row + early_flight[4][:1, :1]
    dynorm = _mm(dys, full["w_ssm_out"], mode="nt", name="ssm_out_dx")
    dy_ssd, dproj, dssmn = _gate_norm_bwd(y_ssd2, proj["z"], ssmn_row, dynorm, dproj, "ssd_gate_norm_bwd")
    dxc, dproj, dalog, ddsk, ddtb = _ssd_bwd(xc, dtr3, dy_ssd.reshape(b, s, SSM_D_INNER), h_states, dtb_row, alog_row,
                                             dskx, to_channels, to_heads, dproj.reshape(b, s, DPROJ_WIDTH), "ssd_bwd")
    dproj, dconvw, dconvb = _conv_bwd(xbc3, dxc, conv_taps, convb_row, dproj, "conv_bwd")
    grads["conv_w"] = dconvw.T.astype(BF16)

    grads["w_att_out"] = _mm(dya, att2, mode="tn", name="att_out_dw", out_dtype=BF16)
    datt = _mm(dya, full["w_att_out"], mode="nn", name="att_out_dx").reshape(b, s, ATT_OUT_DIM)
    do_parts, dl_parts = _att_delta(att, datt, "att_delta")
    dqs, dks, dvs = [], [], []
    for gi, r in enumerate(ATT_DILATIONS):
        operands = (qk_parts[gi], do_parts[gi], lse_parts[gi], dl_parts[gi])
        dqs.append(_att_bwd_q(*operands, "att_bwd_q_%d" % r))
        dk_g, dv_g = _att_bwd_kv(*operands, "att_bwd_kv_%d" % r)
        dks.append(dk_g)
        dvs.append(dv_g)
    dproj = _rope_bwd(dqs, dks, dvs, cosf, sinf, dproj, "rope_bwd").reshape(t, DPROJ_WIDTH)

    dw_all = _mm(dproj, h1, mode="tn", name="in_proj_dw", out_dtype=BF16)
    dh1 = _mm(dproj, w_in_all, mode="nn", name="in_proj_dx")
    grad_x, dg_mix = _rmsnorm_bwd(x2d, g_mix, dh1, dx2, "norm_mix_bwd")

    head = _stack({name: grads[name].reshape((N_DEV, -1, grads[name].shape[-1])) for name in REDUCE_LATE[:-1]}, BF16,
                  names=REDUCE_LATE[:-1])
    slabs = _w_in_to_shards(dw_all, head, "grad_stacks")
    small = {"norm_mix": dg_mix, "b_gate": dbg, "conv_b": dconvb, "dt_bias": ddtb[:, :SSM_N_HEADS],
             "a_log": dalog[:, :SSM_N_HEADS], "d_skip": ddsk[:, :SSM_N_HEADS], "ssm_norm": dssmn,
             "norm_ffn": dg_ffn, "norm_final": dg_fin}
    core = lax.axis_index("c").astype(jnp.int32).reshape(1)
    chip_sums = _chip_sum(slabs, _pair_exchange(slabs, "grad_pair_exchange"), core, "grad_chip_sum")
    shared = _pack_replicated(small)
    shared = shared.at[LOSS_ROW, 0].set(loss_row[0, 0])
    got, got_small = _chip_exchange(chip_sums, shared, "grad_chip_exchange")

    def packed(prefix):
        vals = _to_stacking({name: given[prefix + name][0] for name, _, _ in SHARDED}, SHARDED)
        rep = {name: given[prefix + name] for name, _ in REPLICATED}
        return _stack(vals, F32), _pack_replicated(rep)

    (w_big, w_small), (m_big, m_small), (v_big, v_small) = packed(""), packed("m_"), packed("v_")
    landed = _gather_wait(*early_flight[:4], grad_x, "grads_scatter_wait")
    me = _my_index()
    mine = lax.dynamic_slice(early, (me, 0, 0), (1,) + early.shape[1:])
    got_early = lax.dynamic_update_slice(landed, mine, (me, 0, 0))
    big_early = _adamw(got_early, w_big, m_big, v_big, "adamw_early")
    big_late = _adamw(got, w_big, m_big, v_big, "adamw_late", row0=early.shape[1])
    sml = _adamw(got_small, w_small, m_small, v_small, "adamw_replicated")

    outs = [sml[0][LOSS_ROW, 0], grad_x.reshape(b, s, d)]
    rep_shapes = {name: given[name].shape for name, _ in REPLICATED}
    order = ["norm_mix", "w_in", "b_gate", "conv_w", "conv_b", "dt_bias", "a_log", "d_skip", "ssm_norm", "w_ssm_out",
             "w_att_out", "w_mix_out", "norm_ffn", "w_ffn_gate", "w_ffn_up", "w_ffn_down", "norm_final"]
    for early_k, late_k, sml_k in zip(big_early, big_late, sml):
        stacks = dict(_unstack(early_k, REDUCE_EARLY), **_unstack(late_k, REDUCE_LATE))
        sharded = _to_stacking(stacks, SHARDED)
        rep = _unpack_replicated(sml_k, rep_shapes)
        for name in order:
            outs.append(sharded[name][None] if name in sharded else rep[name])
    return tuple(outs)
```

```python
import functools
import math

import jax
import jax.numpy as jnp
from jax import lax
from jax.experimental import pallas as pl
from jax.experimental.pallas import tpu as pltpu

F32 = jnp.float32
BF16 = jnp.bfloat16

N_DEV = 8
N_CHIPS = 4
D_MODEL = 1024
SSM_D_INNER = 2048
SSM_HEAD_DIM = 64
SSM_N_HEADS = 32
SSM_N_GROUPS = 4
SSM_HEADS_PER_GROUP = SSM_N_HEADS // SSM_N_GROUPS
SSM_D_STATE = 128
SSM_CONV = 4
SSM_CHUNK = 128
SSM_CONV_DIM = 3072
ATT_HEAD_DIM = 128
ATT_HEADS_PER_GROUP = 4
ATT_DILATIONS = (1, 4, 16)
ATT_N_HEADS = 12
ATT_QKV_DIM = 4608
ATT_OUT_DIM = 512
ATT_BLOCK = 128
ROPE_THETA = 10000.0
D_FF = 2816
IN_PROJ_DIM = 11808
EPS = 1e-6
LANES = 128
DT_PAD = LANES

DPROJ_COLS = {"qkv": 0, "z": 4608, "xbc": 6656, "dt": 9728, "gate": 10240}
DPROJ_DT_WIDTH = 512
DPROJ_WIDTH = 12288

ADAM_LR = 0.001
ADAM_B1 = 0.9
ADAM_B2 = 0.999
ADAM_EPS = 1e-08
ADAM_WD = 0.01
ADAM_STEP = 10

VMEM_LIMIT = 56 * 1024 * 1024
MESH = pl.DeviceIdType.MESH
NEG_INF = float("-inf")


def _tile_rows(n, cap, mult):
    return max(t for t in range(mult, min(n, cap) + 1, mult) if n % t == 0)


def _pick(n, candidates):
    for c in candidates:
        if n % c == 0:
            return c
    return n


def _params(*sem):
    return pltpu.CompilerParams(dimension_semantics=sem, vmem_limit_bytes=VMEM_LIMIT)


def _sigmoid(x):
    return 1.0 / (1.0 + jnp.exp(-x))


def _softplus(x):
    return jnp.maximum(x, 0.0) + jnp.log(1.0 + jnp.exp(-jnp.abs(x)))


def _dot(a, b, dims):
    return lax.dot_general(a.astype(BF16), b.astype(BF16), (dims, ((), ())), preferred_element_type=F32)


def _nn(a, b):
    return _dot(a, b, ((1,), (0,)))


def _nt(a, b):
    return _dot(a, b, ((1,), (1,)))


def _tn(a, b):
    return _dot(a, b, ((0,), (0,)))


def _split3(v):
    hi = v.astype(BF16)
    r1 = v - hi.astype(F32)
    mid = r1.astype(BF16)
    lo = (r1 - mid.astype(F32)).astype(BF16)
    return hi, mid, lo


def _mask_nn(mask, v):
    mb = mask.astype(BF16)
    hi, mid, lo = _split3(v)
    return _nn(mb, hi) + (_nn(mb, mid) + _nn(mb, lo))


MM_VMEM_BUDGET = 40 * 1024 * 1024
MM_FULL_K = 2816


def _mm_tiles(m, n, k, a_bytes, b_bytes, o_bytes, has_add):
    tk = k if k <= MM_FULL_K else _pick(k, (2048, 1024, 512, 256, 128))
    tn = 1408 if (n > 1024 and n % 1408 == 0) else _pick(n, (1024, 768, 512, 384, 256, 128))
    for tm in (1408, 1024, 768, 512, 384, 256, 128):
        if m % tm:
            continue
        buffers = 2 * (tm * tk * a_bytes + tk * tn * b_bytes + tm * tn * (o_bytes + (4 if has_add else 0)))
        if tk < k:
            buffers += tm * tn * 4
        if buffers <= MM_VMEM_BUDGET:
            return tm, tn, tk
    return _pick(m, (128,)), tn, tk


def _mm(a, b, *, mode, name, out_dtype=F32, add=None, after=None):
    if mode == "nn":
        (m, k), n = a.shape, b.shape[1]
    elif mode == "nt":
        (m, k), n = a.shape, b.shape[0]
    else:
        (k, m), n = a.shape, b.shape[1]
    has_add = add is not None
    tm, tn, tk = _mm_tiles(m, n, k, a.dtype.itemsize, b.dtype.itemsize, jnp.dtype(out_dtype).itemsize, has_add)
    nk = k // tk
    dims = {"nn": ((1,), (0,)), "nt": ((1,), (1,)), "tn": ((0,), (0,))}[mode]
    a_spec = {"nn": pl.BlockSpec((tm, tk), lambda i, j, kk: (i, kk)),
              "nt": pl.BlockSpec((tm, tk), lambda i, j, kk: (i, kk)),
              "tn": pl.BlockSpec((tk, tm), lambda i, j, kk: (kk, i))}[mode]
    b_spec = {"nn": pl.BlockSpec((tk, tn), lambda i, j, kk: (kk, j)),
              "nt": pl.BlockSpec((tn, tk), lambda i, j, kk: (j, kk)),
              "tn": pl.BlockSpec((tk, tn), lambda i, j, kk: (kk, j))}[mode]
    o_spec = pl.BlockSpec((tm, tn), lambda i, j, kk: (i, j))

    def finish(r, c_ref, o_ref):
        if has_add:
            r = r + c_ref[...]
        o_ref[...] = r.astype(out_dtype)

    def body_one(*refs):
        a_ref, b_ref = refs[:2]
        finish(_dot(a_ref[...], b_ref[...], dims), refs[2] if has_add else None, refs[-1])

    def body_acc(*refs):
        a_ref, b_ref = refs[:2]
        o_ref, acc = refs[-2:]
        kk = pl.program_id(2)

        @pl.when(kk == 0)
        def _():
            acc[...] = jnp.zeros_like(acc)

        acc[...] += _dot(a_ref[...], b_ref[...], dims)

        @pl.when(kk == nk - 1)
        def _():
            finish(acc[...], refs[2] if has_add else None, o_ref)

    in_specs = [a_spec, b_spec] + ([o_spec] if has_add else [])
    args = (a, b) + ((add,) if has_add else ())
    if after is not None:
        in_specs, args = in_specs + [pl.BlockSpec(memory_space=pl.ANY)], args + (after,)
    return pl.pallas_call(
        body_one if nk == 1 else body_acc, name=name, grid=(m // tm, n // tn, nk),
        in_specs=in_specs, out_specs=o_spec,
        out_shape=jax.ShapeDtypeStruct((m, n), out_dtype),
        scratch_shapes=[] if nk == 1 else [pltpu.VMEM((tm, tn), F32)],
        compiler_params=_params("parallel", "parallel", "arbitrary"),
    )(*args)


def _rmsnorm_fwd(x, g, name):
    t, d = x.shape
    tm = _pick(t, (512, 256, 128))

    def body(x_ref, g_ref, o_ref):
        xv = x_ref[...]
        r = lax.rsqrt(jnp.mean(xv * xv, axis=-1, keepdims=True) + EPS)
        o_ref[...] = ((xv * r) * g_ref[...]).astype(BF16)

    return pl.pallas_call(
        body, name=name, grid=(t // tm,),
        in_specs=[pl.BlockSpec((tm, d), lambda i: (i, 0)), pl.BlockSpec((1, d), lambda i: (0, 0))],
        out_specs=pl.BlockSpec((tm, d), lambda i: (i, 0)),
        out_shape=jax.ShapeDtypeStruct((t, d), BF16),
        compiler_params=_params("parallel"),
    )(x, g)


def _rmsnorm_bwd(x, g, dh, dres, name, with_bf16=False):
    t, d = x.shape
    tm = _pick(t, (512, 256, 128))

    def body(x_ref, g_ref, dh_ref, dres_ref, dx_ref, dg_ref, *dxb_ref):
        @pl.when(pl.program_id(0) == 0)
        def _():
            dg_ref[...] = jnp.zeros_like(dg_ref)

        xv = x_ref[...]
        r = lax.rsqrt(jnp.mean(xv * xv, axis=-1, keepdims=True) + EPS)
        xhat = xv * r
        dhv = dh_ref[...]
        dyg = dhv * g_ref[...]
        dx = dres_ref[...] + r * (dyg - xhat * jnp.mean(dyg * xhat, axis=-1, keepdims=True))
        dx_ref[...] = dx
        if with_bf16:
            dxb_ref[0][...] = dx.astype(BF16)
        dg_ref[...] += jnp.sum(dhv * xhat, axis=0, keepdims=True)

    row = pl.BlockSpec((tm, d), lambda i: (i, 0))
    vec = pl.BlockSpec((1, d), lambda i: (0, 0))
    extra = with_bf16 * [jax.ShapeDtypeStruct((t, d), BF16)]
    return pl.pallas_call(
        body, name=name, grid=(t // tm,),
        in_specs=[row, vec, row, row], out_specs=[row, vec] + with_bf16 * [row],
        out_shape=[jax.ShapeDtypeStruct((t, d), F32), jax.ShapeDtypeStruct((1, d), F32)] + extra,
        compiler_params=_params("arbitrary"),
    )(x, g, dh, dres)


def _loss_head(x, g, target, name):
    t, d = x.shape
    tm = _pick(t, (512, 256, 128))

    def body(x_ref, g_ref, t_ref, loss_ref, dx_ref, dg_ref, dxb_ref):
        @pl.when(pl.program_id(0) == 0)
        def _():
            dg_ref[...] = jnp.zeros_like(dg_ref)
            loss_ref[...] = jnp.zeros_like(loss_ref)

        xv = x_ref[...]
        gv = g_ref[...]
        r = lax.rsqrt(jnp.mean(xv * xv, axis=-1, keepdims=True) + EPS)
        xhat = xv * r
        err = xhat * gv - t_ref[...]
        loss_ref[...] += jnp.sum(err * err) * (0.5 / d)
        dy = err * (1.0 / d)
        dyg = dy * gv
        dx = r * (dyg - xhat * jnp.mean(dyg * xhat, axis=-1, keepdims=True))
        dx_ref[...] = dx
        dxb_ref[...] = dx.astype(BF16)
        dg_ref[...] += jnp.sum(dy * xhat, axis=0, keepdims=True)

    row = pl.BlockSpec((tm, d), lambda i: (i, 0))
    vec = pl.BlockSpec((1, d), lambda i: (0, 0))
    return pl.pallas_call(
        body, name=name, grid=(t // tm,),
        in_specs=[row, vec, row],
        out_specs=[pl.BlockSpec((1, LANES), lambda i: (0, 0)), row, vec, row],
        out_shape=[jax.ShapeDtypeStruct((1, LANES), F32), jax.ShapeDtypeStruct((t, d), F32),
                   jax.ShapeDtypeStruct((1, d), F32), jax.ShapeDtypeStruct((t, d), BF16)],
        compiler_params=_params("arbitrary"),
    )(x, g, target)


CONV_HALO = 8
CONV_ROWS = 64


def _conv_taps(window, wv, bv):
    acc = bv + wv[SSM_CONV - 1:SSM_CONV, :] * window(0)
    for sh in range(1, SSM_CONV):
        kidx = SSM_CONV - 1 - sh
        acc = acc + wv[kidx:kidx + 1, :] * window(sh)
    return acc


def _conv_fwd(u, w, bias, name):
    b, s, c = u.shape
    rows = CONV_ROWS

    def body(u_ref, w_ref, b_ref, o_ref, ext):
        ext[0:CONV_HALO, :] = jnp.zeros((CONV_HALO, LANES), F32)
        ext[CONV_HALO:, :] = u_ref[...]
        wv, bv = w_ref[...], b_ref[...]
        for r0 in range(0, s, rows):
            acc = _conv_taps(lambda sh: ext[CONV_HALO + r0 - sh:CONV_HALO + r0 - sh + rows, :], wv, bv)
            o_ref[r0:r0 + rows, :] = acc * _sigmoid(acc)

    strip = pl.BlockSpec((None, s, LANES), lambda bi, j: (bi, 0, j))
    return pl.pallas_call(
        body, name=name, grid=(b, c // LANES),
        in_specs=[strip, pl.BlockSpec((SSM_CONV, LANES), lambda bi, j: (0, j)),
                  pl.BlockSpec((1, LANES), lambda bi, j: (0, j))],
        out_specs=strip, out_shape=jax.ShapeDtypeStruct((b, s, c), F32),
        scratch_shapes=[pltpu.VMEM((CONV_HALO + s, LANES), F32)],
        compiler_params=_params("parallel", "parallel"),
    )(u, w, bias)


def _conv_bwd(u, dout, w, bias, dproj, name):
    b, s, c = u.shape
    rows = CONV_ROWS

    def fold(v):
        return jnp.sum(v.reshape(rows // CONV_HALO, CONV_HALO, LANES), axis=0)

    def body(u_ref, d_ref, w_ref, b_ref, buf_ref, du_ref, dw_ref, db_ref, ext, dpre):
        @pl.when(pl.program_id(1) == 0)
        def _():
            dw_ref[...] = jnp.zeros_like(dw_ref)
            db_ref[...] = jnp.zeros_like(db_ref)

        ext[0:CONV_HALO, :] = jnp.zeros((CONV_HALO, LANES), F32)
        ext[CONV_HALO:, :] = u_ref[...]
        dpre[s:, :] = jnp.zeros((CONV_HALO, LANES), F32)
        wv, bv = w_ref[...], b_ref[...]
        sums = [jnp.zeros((CONV_HALO, LANES), F32)] * (SSM_CONV + 1)
        for r0 in range(0, s, rows):
            window = lambda sh: ext[CONV_HALO + r0 - sh:CONV_HALO + r0 - sh + rows, :]
            acc = _conv_taps(window, wv, bv)
            sg = _sigmoid(acc)
            dp = d_ref[r0:r0 + rows, :] * (sg * (1.0 + acc * (1.0 - sg)))
            dpre[r0:r0 + rows, :] = dp
            taps = [sums[SSM_CONV - 1 - sh] + fold(dp * window(sh)) for sh in range(SSM_CONV)]
            sums = taps[::-1] + [sums[SSM_CONV] + fold(dp)]
        for r0 in range(0, s, rows):
            du = wv[SSM_CONV - 1:SSM_CONV, :] * dpre[r0:r0 + rows, :]
            for sh in range(1, SSM_CONV):
                kidx = SSM_CONV - 1 - sh
                du = du + wv[kidx:kidx + 1, :] * dpre[r0 + sh:r0 + sh + rows, :]
            du_ref[r0:r0 + rows, :] = du.astype(BF16)
        for kidx in range(SSM_CONV):
            dw_ref[kidx:kidx + 1, :] += jnp.sum(sums[kidx], axis=0, keepdims=True)
        db_ref[...] += jnp.sum(sums[SSM_CONV], axis=0, keepdims=True)

    strip = pl.BlockSpec((None, s, LANES), lambda j, bi: (bi, 0, j))
    taps = pl.BlockSpec((SSM_CONV, LANES), lambda j, bi: (0, j))
    vec = pl.BlockSpec((1, LANES), lambda j, bi: (0, j))
    du_cols = pl.BlockSpec((None, s, LANES), lambda j, bi: (bi, 0, DPROJ_COLS["xbc"] // LANES + j))
    return pl.pallas_call(
        body, name=name, grid=(c // LANES, b),
        in_specs=[strip, strip, taps, vec, pl.BlockSpec(memory_space=pl.ANY)], out_specs=[du_cols, taps, vec],
        input_output_aliases={4: 0},
        out_shape=[jax.ShapeDtypeStruct(dproj.shape, dproj.dtype), jax.ShapeDtypeStruct((SSM_CONV, c), F32),
                   jax.ShapeDtypeStruct((1, c), F32)],
        scratch_shapes=[pltpu.VMEM((CONV_HALO + s, LANES), F32), pltpu.VMEM((s + CONV_HALO, LANES), F32)],
        compiler_params=_params("parallel", "arbitrary"),
    )(u, dout, w, bias, dproj)


def _ssd_chunk_terms(dtr_ref, bias_ref, alog_ref):
    q = SSM_CHUNK
    dt = _softplus(dtr_ref[...] + bias_ref[...])
    a_neg = -jnp.exp(alog_ref[...])
    row = lax.broadcasted_iota(jnp.int32, (q, q), 0)
    col = lax.broadcasted_iota(jnp.int32, (q, q), 1)
    lower = row >= col
    s = _mask_nn(lower, dt * a_neg)
    return dt, a_neg, s, s.T, lower


def _head_masks():
    heads = jnp.arange(LANES)[:, None]
    chans = jnp.arange(SSM_D_INNER)[None, :]
    to_channels = (chans // SSM_HEAD_DIM == heads).astype(BF16)
    return to_channels, to_channels.T


def _per_channel(v, to_channels):
    hi = v.astype(BF16)
    lo = (v - hi.astype(F32)).astype(BF16)
    return _nn(hi, to_channels) + _nn(lo, to_channels)


def _per_head(v, to_heads):
    hi = v.astype(BF16)
    lo = (v - hi.astype(F32)).astype(BF16)
    return _nn(hi, to_heads) + _nn(lo, to_heads)


def _decay_terms_per_channel(dt, s_col, to_channels):
    q = SSM_CHUNK
    tot = s_col[q - 1:q, :]
    stacked = jnp.concatenate([dt, jnp.exp(s_col), jnp.exp(tot - s_col)], axis=0)
    wide = _per_channel(stacked, to_channels)
    dtx, esx, decx = wide[:q], wide[q:2 * q], wide[2 * q:]
    return dtx, esx, decx, esx[0:1, :] * decx[0:1, :]


SSM_PAIRS_PER_GROUP = SSM_HEADS_PER_GROUP // 2
SSM_GROUP_CHANNELS = SSM_HEADS_PER_GROUP * SSM_HEAD_DIM


def _split_pair(v):
    first = lax.broadcasted_iota(jnp.int32, v.shape, 1) < SSM_HEAD_DIM
    return jnp.concatenate([jnp.where(first, v, 0.0), jnp.where(first, 0.0, v)], axis=0)


def _ssd_fwd(xc, dtr, dt_bias, a_log, dskx, to_channels, name):
    b, s, _ = xc.shape
    q = SSM_CHUNK
    nc = s // q
    n, gc = SSM_D_STATE, SSM_GROUP_CHANNELS

    def body(xc_ref, dtr_ref, bias_ref, alog_ref, dsk_ref, tc_ref, y_ref, hs_ref, h_scr):
        @pl.when(pl.program_id(1) == 0)
        def _():
            h_scr[...] = jnp.zeros_like(h_scr)

        dt, _, s_col, s_row, lower = _ssd_chunk_terms(dtr_ref, bias_ref, alog_ref)
        dtx, esx, decx, etotx = _decay_terms_per_channel(dt, s_col, tc_ref[...])
        x = xc_ref[:, :SSM_D_INNER]
        xdt = x * dtx
        xdec = xdt * decx
        skip = dsk_ref[...] * x
        for g in range(SSM_N_GROUPS):
            bg = xc_ref[:, SSM_D_INNER + n * g:SSM_D_INNER + n * (g + 1)].astype(BF16)
            cg = xc_ref[:, SSM_D_INNER + n * (SSM_N_GROUPS + g):SSM_D_INNER + n * (SSM_N_GROUPS + g + 1)].astype(BF16)
            gsl = slice(gc * g, gc * (g + 1))
            gm = _nt(cg, bg)
            hgt = h_scr[:, gsl]
            hs_ref[:, gsl] = hgt
            y_off = esx[:, gsl] * _nn(cg, hgt)
            h_scr[:, gsl] = etotx[:, gsl] * hgt + _tn(bg, xdec[:, gsl])
            for k in range(SSM_PAIRS_PER_GROUP):
                h0 = g * SSM_HEADS_PER_GROUP + 2 * k
                lo = gc * g + LANES * k
                ms = []
                for h in (h0, h0 + 1):
                    lm = jnp.exp(jnp.where(lower, s_col[:, h:h + 1] - s_row[h:h + 1, :], NEG_INF))
                    ms.append((gm * lm).astype(BF16))
                y_diag = _nn(jnp.concatenate(ms, axis=1), _split_pair(xdt[:, lo:lo + LANES]))
                y_ref[:, lo:lo + LANES] = y_diag + y_off[:, LANES * k:LANES * (k + 1)] + skip[:, lo:lo + LANES]

    vec = pl.BlockSpec((1, LANES), lambda bi, c: (0, 0))
    return pl.pallas_call(
        body, name=name, grid=(b, nc),
        in_specs=[pl.BlockSpec((None, q, SSM_CONV_DIM), lambda bi, c: (bi, c, 0)),
                  pl.BlockSpec((None, q, LANES), lambda bi, c: (bi, c, 0)), vec, vec,
                  pl.BlockSpec((1, SSM_D_INNER), lambda bi, c: (0, 0)),
                  pl.BlockSpec((LANES, SSM_D_INNER), lambda bi, c: (0, 0))],
        out_specs=[pl.BlockSpec((None, q, SSM_D_INNER), lambda bi, c: (bi, c, 0)),
                   pl.BlockSpec((None, None, n, SSM_D_INNER), lambda bi, c: (bi, c, 0, 0))],
        out_shape=[jax.ShapeDtypeStruct((b, s, SSM_D_INNER), F32),
                   jax.ShapeDtypeStruct((b, nc, n, SSM_D_INNER), F32)],
        scratch_shapes=[pltpu.VMEM((n, SSM_D_INNER), F32)],
        compiler_params=_params("parallel", "arbitrary"),
    )(xc, dtr, dt_bias, a_log, dskx, to_channels)


def _ssd_bwd(xc, dtr, dy, hs, dt_bias, a_log, dskx, to_channels, to_heads, dproj, name):
    b, s, _ = xc.shape
    q = SSM_CHUNK
    nc = s // q
    n, gc = SSM_D_STATE, SSM_GROUP_CHANNELS

    def colsum(v):
        return jnp.sum(v, axis=0, keepdims=True)

    def body(xc_ref, dtr_ref, dy_ref, hs_ref, bias_ref, alog_ref, dsk_ref, tc_ref, th_ref, buf_ref,
             dxc_ref, ddtr_ref, dalog_ref, ddsk_ref, dbias_ref, dh_scr, dxs_scr, dxd_scr, w_scr, dst_scr, rows_scr):
        ci = pl.program_id(1)

        @pl.when(ci == 0)
        def _():
            dh_scr[...] = jnp.zeros_like(dh_scr)

        @pl.when(jnp.logical_and(pl.program_id(0) == 0, ci == 0))
        def _():
            dalog_ref[...] = jnp.zeros_like(dalog_ref)
            ddsk_ref[...] = jnp.zeros_like(ddsk_ref)
            dbias_ref[...] = jnp.zeros_like(dbias_ref)
            dst_scr[...] = jnp.zeros_like(dst_scr)

        dt, a_neg, s_col, s_row, lower = _ssd_chunk_terms(dtr_ref, bias_ref, alog_ref)
        upper = jnp.logical_not(lower) | (lax.broadcasted_iota(jnp.int32, (q, q), 0)
                                          == lax.broadcasted_iota(jnp.int32, (q, q), 1))
        dtx, esx, decx, etotx = _decay_terms_per_channel(dt, s_col, tc_ref[...])
        x = xc_ref[:, :SSM_D_INNER]
        dyv = dy_ref[...]
        xdt = x * dtx
        xdec = xdt * decx
        dw = esx * dyv
        rows_scr[...] = jnp.zeros_like(rows_scr)
        for g in range(SSM_N_GROUPS):
            b_lo = SSM_D_INNER + n * g
            c_lo = SSM_D_INNER + n * (SSM_N_GROUPS + g)
            bg = xc_ref[:, b_lo:b_lo + n].astype(BF16)
            cg = xc_ref[:, c_lo:c_lo + n].astype(BF16)
            gsl = slice(gc * g, gc * (g + 1))
            gm = _nt(cg, bg)
            gmt = _nt(bg, cg)
            hgt = hs_ref[:, gsl]
            dhgt = dh_scr[:, gsl]
            w_scr[:, gsl] = _nn(cg, hgt)
            dcg = _nt(dw[:, gsl], hgt)
            dxs = decx[:, gsl] * _nn(bg, dhgt)
            dxs_scr[:, gsl] = dxs
            dbg = _nt(xdec[:, gsl], dhgt)
            rows_scr[2:3, gsl] = colsum(dhgt * hgt)
            dh_scr[:, gsl] = _tn(cg, dw[:, gsl]) + etotx[:, gsl] * dhgt
            dg = jnp.zeros((q, q), F32)
            dgt = jnp.zeros((q, q), F32)
            for k in range(SSM_PAIRS_PER_GROUP):
                h0 = g * SSM_HEADS_PER_GROUP + 2 * k
                lo = gc * g + LANES * k
                xp = xdt[:, lo:lo + LANES]
                dyp = dyv[:, lo:lo + LANES]
                dy2 = _split_pair(dyp)
                dm2 = _nt(dy2, xp)
                dmt2 = _nt(_split_pair(xp), dyp)
                mts = []
                for i, h in enumerate((h0, h0 + 1)):
                    lm = jnp.exp(jnp.where(lower, s_col[:, h:h + 1] - s_row[h:h + 1, :], NEG_INF))
                    lmt = jnp.exp(jnp.where(upper, s_row[h:h + 1, :] - s_col[:, h:h + 1], NEG_INF))
                    dm = dm2[q * i:q * (i + 1), :]
                    dmt = dmt2[q * i:q * (i + 1), :]
                    dg = dg + dm * lm
                    dgt = dgt + dmt * lmt
                    mt = gmt * lmt
                    dst_scr[h:h + 1, :] = colsum(dmt * mt) - colsum(dm * (gm * lm))
                    mts.append(mt.astype(BF16))
                dxd_scr[:, lo:lo + LANES] = _nn(jnp.concatenate(mts, axis=1), dy2)
            dxc_ref[:, b_lo:b_lo + n] = dbg + _nn(dgt, cg)
            dxc_ref[:, c_lo:c_lo + n] = dcg + _nn(dg, bg)
        dxs = dxs_scr[...]
        dxdt = dxd_scr[...] + dxs
        dxc_ref[:, :SSM_D_INNER] = dxdt * dtx + dsk_ref[...] * dyv
        state_part = xdt * dxs
        rows_scr[0:1, :] = colsum(dyv * x)
        rows_scr[1:2, :] = colsum(state_part)
        th = th_ref[...]
        per_head = _per_head(jnp.concatenate([dw * w_scr[...] - state_part, dxdt * x], axis=0), th)
        r_ds, r_dt = per_head[:q], per_head[q:]
        sums = _per_head(rows_scr[...], th)
        etot = jnp.exp(s_col[q - 1:q, :])
        dtot = sums[1:2, :] + etot * sums[2:3, :]
        last = lax.broadcasted_iota(jnp.int32, (q, LANES), 0) == q - 1
        ds = dst_scr[...].T + r_ds + jnp.where(last, dtot, 0.0)
        da = _mask_nn(upper, ds)
        ddt = da * a_neg + r_dt
        live = lax.broadcasted_iota(jnp.int32, (1, LANES), 1) < SSM_N_HEADS
        sg = _sigmoid(dtr_ref[...] + bias_ref[...])
        ddtr = jnp.where(live, ddt * sg, 0.0)
        ddtr_ref[:, :LANES] = ddtr.astype(BF16)
        ddtr_ref[:, LANES:] = jnp.zeros((q, DPROJ_DT_WIDTH - LANES), BF16)
        dalog_ref[...] += jnp.where(live, colsum(da * dt) * a_neg, 0.0)
        ddsk_ref[...] += jnp.where(live, sums[0:1, :], 0.0)
        dbias_ref[...] += colsum(ddtr)

    rev = lambda bi, c: (bi, nc - 1 - c, 0)
    vec = pl.BlockSpec((1, LANES), lambda bi, c: (0, 0))
    wide = pl.BlockSpec((None, q, SSM_D_INNER), rev)
    return pl.pallas_call(
        body, name=name, grid=(b, nc),
        in_specs=[pl.BlockSpec((None, q, SSM_CONV_DIM), rev), pl.BlockSpec((None, q, LANES), rev), wide,
                  pl.BlockSpec((None, None, n, SSM_D_INNER), lambda bi, c: (bi, nc - 1 - c, 0, 0)),
                  vec, vec, pl.BlockSpec((1, SSM_D_INNER), lambda bi, c: (0, 0)),
                  pl.BlockSpec((LANES, SSM_D_INNER), lambda bi, c: (0, 0)),
                  pl.BlockSpec((SSM_D_INNER, LANES), lambda bi, c: (0, 0)),
                  pl.BlockSpec(memory_space=pl.ANY)],
        out_specs=[pl.BlockSpec((None, q, SSM_CONV_DIM), rev),
                   pl.BlockSpec((None, q, DPROJ_DT_WIDTH),
                                lambda bi, c: (bi, nc - 1 - c, DPROJ_COLS["dt"] // DPROJ_DT_WIDTH)), vec, vec, vec],
        input_output_aliases={9: 1},
        out_shape=[jax.ShapeDtypeStruct((b, s, SSM_CONV_DIM), F32), jax.ShapeDtypeStruct(dproj.shape, dproj.dtype),
                   jax.ShapeDtypeStruct((1, LANES), F32), jax.ShapeDtypeStruct((1, LANES), F32),
                   jax.ShapeDtypeStruct((1, LANES), F32)],
        scratch_shapes=[pltpu.VMEM((n, SSM_D_INNER), F32)] + [pltpu.VMEM((q, SSM_D_INNER), F32)] * 3
        + [pltpu.VMEM((LANES, q), F32), pltpu.VMEM((8, SSM_D_INNER), F32)],
        compiler_params=_params("arbitrary", "arbitrary"),
    )(xc, dtr, dy, hs, dt_bias, a_log, dskx, to_channels, to_heads, dproj)


SSM_GROUP_WIDTH = SSM_D_INNER // SSM_N_GROUPS


def _gate_norm_fwd(y, z, w, name):
    t, d = y.shape
    tm = _pick(t, (256, 128))

    def body(y_ref, z_ref, w_ref, o_ref):
        for g in range(SSM_N_GROUPS):
            sl = slice(SSM_GROUP_WIDTH * g, SSM_GROUP_WIDTH * (g + 1))
            zv = z_ref[:, sl]
            u = y_ref[:, sl] * (zv * _sigmoid(zv))
            r = lax.rsqrt(jnp.mean(u * u, axis=-1, keepdims=True) + EPS)
            o_ref[:, sl] = ((u * r) * w_ref[:, sl]).astype(BF16)

    row = pl.BlockSpec((tm, d), lambda i: (i, 0))
    return pl.pallas_call(
        body, name=name, grid=(t // tm,),
        in_specs=[row, row, pl.BlockSpec((1, d), lambda i: (0, 0))], out_specs=row,
        out_shape=jax.ShapeDtypeStruct((t, d), BF16),
        compiler_params=_params("parallel"),
    )(y, z, w)


def _gate_norm_bwd(y, z, w, dout, dproj, name):
    t, d = y.shape
    gw = SSM_GROUP_WIDTH
    tm = _pick(t, (1024, 512, 256, 128))

    def body(y_ref, z_ref, w_ref, do_ref, buf_ref, dy_ref, dz_ref, dw_ref):
        @pl.when(pl.program_id(1) == 0)
        def _():
            dw_ref[...] = jnp.zeros_like(dw_ref)

        zv = z_ref[...]
        yv = y_ref[...]
        sg = _sigmoid(zv)
        silu = zv * sg
        u = yv * silu
        r = lax.rsqrt(jnp.mean(u * u, axis=-1, keepdims=True) + EPS)
        uh = u * r
        dov = do_ref[...]
        dw_ref[...] += jnp.sum(dov * uh, axis=0, keepdims=True)
        dyg = dov * w_ref[...]
        du = r * (dyg - uh * jnp.mean(dyg * uh, axis=-1, keepdims=True))
        dy_ref[...] = du * silu
        dz_ref[...] = (du * yv * (sg * (1.0 + zv * (1.0 - sg)))).astype(BF16)

    tile = pl.BlockSpec((tm, gw), lambda g, i: (i, g))
    vec = pl.BlockSpec((1, gw), lambda g, i: (0, g))
    z_cols = pl.BlockSpec((tm, gw), lambda g, i: (i, DPROJ_COLS["z"] // gw + g))
    return pl.pallas_call(
        body, name=name, grid=(SSM_N_GROUPS, t // tm),
        in_specs=[tile, tile, vec, tile, pl.BlockSpec(memory_space=pl.ANY)], out_specs=[tile, z_cols, vec],
        out_shape=[jax.ShapeDtypeStruct((t, d), F32), jax.ShapeDtypeStruct(dproj.shape, dproj.dtype),
                   jax.ShapeDtypeStruct((1, d), F32)],
        input_output_aliases={4: 1},
        compiler_params=_params("parallel", "arbitrary"),
    )(y, z, w, dout, dproj)


def _rope_tables(s):
    half = ATT_HEAD_DIM // 2
    inv = ROPE_THETA ** (-jnp.arange(half, dtype=F32) / half)
    ang = jnp.arange(s).astype(F32)[:, None] * inv[None, :]
    cos, sin = jnp.cos(ang), jnp.sin(ang)
    return jnp.concatenate([cos, cos], axis=-1), jnp.concatenate([-sin, sin], axis=-1)


ATT_TILE = 256


def _by_residue_spec(r, width):
    return pl.BlockSpec((None, r, ATT_TILE // r, width), lambda bi, i: (bi, 0, i, 0))


def _to_residues(tile, stage, r, store):
    if r == 1:
        store(0, tile)
        return
    stage[...] = tile
    for ri in range(r):
        store(ri, stage[pl.ds(ri, ATT_TILE // r, stride=r), :])


def _from_residues(load, stage, r):
    if r == 1:
        return load(0)
    for ri in range(r):
        stage[pl.ds(ri, ATT_TILE // r, stride=r), :] = load(ri)
    return stage[...]


def _rope_fwd(qkv, cosf, sinf, name):
    b, s, w = qkv.shape
    ts, d, gw = ATT_TILE, ATT_HEAD_DIM, ATT_OUT_DIM

    def body(x_ref, c_ref, s_ref, *rest):
        outs, stage = rest[:-1], rest[-1]
        cv, sv = c_ref[...], s_ref[...]
        for kind in range(3):
            for gi, r in enumerate(ATT_DILATIONS):
                for j in range(ATT_HEADS_PER_GROUP):
                    src = d * (kind * ATT_N_HEADS + gi * ATT_HEADS_PER_GROUP + j)
                    dst = slice(kind * gw + d * j, kind * gw + d * (j + 1))
                    tv = x_ref[:, src:src + d]
                    if kind < 2:
                        tv = tv * cv + pltpu.roll(tv, d // 2, 1) * sv

                    def store(ri, rows, o_ref=outs[gi], dst=dst):
                        o_ref[ri, :, dst] = rows.astype(BF16)

                    _to_residues(tv, stage, r, store)

    tab = pl.BlockSpec((ts, d), lambda bi, i: (i, 0))
    return pl.pallas_call(
        body, name=name, grid=(b, s // ts),
        in_specs=[pl.BlockSpec((None, ts, w), lambda bi, i: (bi, i, 0)), tab, tab],
        out_specs=[_by_residue_spec(r, 3 * gw) for r in ATT_DILATIONS],
        out_shape=[jax.ShapeDtypeStruct((b, r, s // r, 3 * gw), BF16) for r in ATT_DILATIONS],
        scratch_shapes=[pltpu.VMEM((ts, d), F32)],
        compiler_params=_params("parallel", "parallel"),
    )(qkv, cosf, sinf)


def _rope_bwd(dq, dk, dv, cosf, sinf, dproj, name):
    n_pat = len(ATT_DILATIONS)
    b, _, s, gw = dq[0].shape
    ts, d = ATT_TILE, ATT_HEAD_DIM

    def body(*refs):
        ins, (c_ref, s_ref, _, o_ref, stage) = refs[:3 * n_pat], refs[3 * n_pat:]
        cv, sv = c_ref[...], s_ref[...]
        for kind in range(3):
            for gi, r in enumerate(ATT_DILATIONS):
                src = ins[kind * n_pat + gi]
                for j in range(ATT_HEADS_PER_GROUP):
                    tv = _from_residues(lambda ri, src=src, j=j: src[ri, :, d * j:d * (j + 1)], stage, r)
                    if kind < 2:
                        tv = tv * cv + pltpu.roll(tv * sv, d // 2, 1)
                    lo = d * (kind * ATT_N_HEADS + gi * ATT_HEADS_PER_GROUP + j)
                    o_ref[:, lo:lo + d] = tv.astype(BF16)

    tab = pl.BlockSpec((ts, d), lambda bi, i: (i, 0))
    parts = [_by_residue_spec(r, gw) for r in ATT_DILATIONS]
    return pl.pallas_call(
        body, name=name, grid=(b, s // ts), in_specs=parts * 3 + [tab, tab, pl.BlockSpec(memory_space=pl.ANY)],
        out_specs=pl.BlockSpec((None, ts, ATT_QKV_DIM), lambda bi, i: (bi, i, DPROJ_COLS["qkv"] // ATT_QKV_DIM)),
        out_shape=jax.ShapeDtypeStruct(dproj.shape, dproj.dtype),
        input_output_aliases={3 * n_pat + 2: 0},
        scratch_shapes=[pltpu.VMEM((ts, d), F32)],
        compiler_params=_params("parallel", "parallel"),
    )(*dq, *dk, *dv, cosf, sinf, dproj)


ATT_SCALE = ATT_HEAD_DIM ** -0.5
ATT_STEP = 2 * ATT_BLOCK


def _att_spec(col):
    return pl.BlockSpec((None, None, ATT_STEP, ATT_OUT_DIM), lambda bi, ri, i: (bi, ri, i, col))


def _att_edge_spec(col, side, n_steps):
    def index(bi, ri, i):
        blk = 2 * i - 1 if side < 0 else 2 * i + 2
        return (bi, ri, jnp.clip(blk, 0, 2 * n_steps - 1), col)
    return pl.BlockSpec((None, None, ATT_BLOCK, ATT_OUT_DIM), index)


def _band_mask(shape, q_axis, has_prev):
    qi = lax.broadcasted_iota(jnp.int32, shape, q_axis)
    kj = lax.broadcasted_iota(jnp.int32, shape, 1 - q_axis)
    dist = qi + ATT_BLOCK - kj
    return (dist >= 0) & (dist <= ATT_BLOCK) & (has_prev | (kj >= ATT_BLOCK))


def _att_fwd(qkr, name):
    b, r, l, _ = qkr.shape
    nb = l // ATT_STEP
    d = ATT_HEAD_DIM

    def body(q_ref, kp_ref, k_ref, vp_ref, v_ref, o_ref, lse_ref):
        mask = _band_mask((ATT_STEP, ATT_BLOCK + ATT_STEP), 0, pl.program_id(2) > 0)
        for j in range(ATT_HEADS_PER_GROUP):
            sl = slice(d * j, d * (j + 1))
            kcat = jnp.concatenate([kp_ref[:, sl], k_ref[:, sl]], axis=0)
            vcat = jnp.concatenate([vp_ref[:, sl], v_ref[:, sl]], axis=0)
            sc = jnp.where(mask, _nt(q_ref[:, sl], kcat) * ATT_SCALE, NEG_INF)
            m = jnp.max(sc, axis=-1, keepdims=True)
            pr = jnp.exp(sc - m)
            den = jnp.sum(pr, axis=-1, keepdims=True)
            o_ref[:, sl] = _nn(pr / den, vcat)
            lse_ref[:, sl] = jnp.broadcast_to(m + jnp.log(den), (ATT_STEP, d))

    out_spec = _att_spec(0)
    return pl.pallas_call(
        body, name=name, grid=(b, r, nb),
        in_specs=[_att_spec(0), _att_edge_spec(1, -1, nb), _att_spec(1), _att_edge_spec(2, -1, nb), _att_spec(2)],
        out_specs=[out_spec, out_spec],
        out_shape=[jax.ShapeDtypeStruct((b, r, l, ATT_OUT_DIM), F32)] * 2,
        compiler_params=_params("parallel", "parallel", "parallel"),
    )(qkr, qkr, qkr, qkr, qkr)


def _att_merge(os_, lses, name):
    n_pat = len(os_)
    b, _, s, gw = os_[0].shape
    ts, d = ATT_TILE, ATT_HEAD_DIM

    def body(*refs):
        o_refs, l_refs = refs[:n_pat], refs[n_pat:2 * n_pat]
        att_ref, lse_outs, stage = refs[2 * n_pat], refs[2 * n_pat + 1:3 * n_pat + 1], refs[-1]
        for j in range(ATT_HEADS_PER_GROUP):
            sl = slice(d * j, d * (j + 1))
            ov = [_from_residues(lambda ri, g=g: o_refs[g][ri, :, sl], stage, r)
                  for g, r in enumerate(ATT_DILATIONS)]
            ls = [_from_residues(lambda ri, g=g: l_refs[g][ri, :, sl], stage, r)
                  for g, r in enumerate(ATT_DILATIONS)]
            m = functools.reduce(jnp.maximum, ls)
            es = [jnp.exp(lv - m) for lv in ls]
            tot = functools.reduce(lambda u, v: u + v, es)
            acc = (es[0] / tot) * ov[0]
            for g in range(1, n_pat):
                acc = acc + (es[g] / tot) * ov[g]
            att_ref[:, sl] = acc
            joint = m + jnp.log(tot)
            for g, r in enumerate(ATT_DILATIONS):
                def store(ri, rows, out=lse_outs[g]):
                    out[ri, :, sl] = rows
                _to_residues(joint, stage, r, store)

    parts = [_by_residue_spec(r, gw) for r in ATT_DILATIONS]
    return pl.pallas_call(
        body, name=name, grid=(b, s // ts), in_specs=parts * 2,
        out_specs=[pl.BlockSpec((None, ts, gw), lambda bi, i: (bi, i, 0))] + parts,
        out_shape=[jax.ShapeDtypeStruct((b, s, gw), F32)]
        + [jax.ShapeDtypeStruct((b, r, s // r, gw), F32) for r in ATT_DILATIONS],
        scratch_shapes=[pltpu.VMEM((ts, d), F32)],
        compiler_params=_params("parallel", "parallel"),
    )(*os_, *lses)


def _att_delta(att, datt, name):
    b, s, gw = att.shape
    ts, d = ATT_TILE, ATT_HEAD_DIM
    n_pat = len(ATT_DILATIONS)

    def body(a_ref, d_ref, *rest):
        do_outs, dl_outs, stage = rest[:n_pat], rest[n_pat:2 * n_pat], rest[-1]
        for j in range(ATT_HEADS_PER_GROUP):
            sl = slice(d * j, d * (j + 1))
            dv = d_ref[:, sl]
            delta = jnp.broadcast_to(jnp.sum(a_ref[:, sl] * dv, axis=-1, keepdims=True), (ts, d))
            for g, r in enumerate(ATT_DILATIONS):
                def store_do(ri, rows, out=do_outs[g]):
                    out[ri, :, sl] = rows.astype(BF16)

                def store_dl(ri, rows, out=dl_outs[g]):
                    out[ri, :, sl] = rows

                _to_residues(dv, stage, r, store_do)
                _to_residues(delta, stage, r, store_dl)

    row = pl.BlockSpec((None, ts, gw), lambda bi, i: (bi, i, 0))
    parts = [_by_residue_spec(r, gw) for r in ATT_DILATIONS]
    outs = pl.pallas_call(
        body, name=name, grid=(b, s // ts), in_specs=[row, row], out_specs=parts * 2,
        out_shape=[jax.ShapeDtypeStruct((b, r, s // r, gw), BF16) for r in ATT_DILATIONS]
        + [jax.ShapeDtypeStruct((b, r, s // r, gw), F32) for r in ATT_DILATIONS],
        scratch_shapes=[pltpu.VMEM((ts, d), F32)],
        compiler_params=_params("parallel", "parallel"),
    )(att, datt)
    return outs[:n_pat], outs[n_pat:]


def _att_bwd_q(qkr, datt, lse, delta, name):
    b, r, l, _ = qkr.shape
    nb = l // ATT_STEP
    d = ATT_HEAD_DIM

    def body(q_ref, kp_ref, k_ref, vp_ref, v_ref, do_ref, lse_ref, dl_ref, dq_ref):
        mask = _band_mask((ATT_STEP, ATT_BLOCK + ATT_STEP), 0, pl.program_id(2) > 0)
        for j in range(ATT_HEADS_PER_GROUP):
            sl = slice(d * j, d * (j + 1))
            kcat = jnp.concatenate([kp_ref[:, sl], k_ref[:, sl]], axis=0)
            vcat = jnp.concatenate([vp_ref[:, sl], v_ref[:, sl]], axis=0)
            sc = _nt(q_ref[:, sl], kcat) * ATT_SCALE
            pr = jnp.exp(jnp.where(mask, sc - lse_ref[:, d * j:d * j + 1], NEG_INF))
            dp = _nt(do_ref[:, sl], vcat)
            dsc = pr * (dp - dl_ref[:, d * j:d * j + 1])
            dq_ref[:, sl] = _nn(dsc, kcat) * ATT_SCALE

    tok = _att_spec(0)
    return pl.pallas_call(
        body, name=name, grid=(b, r, nb),
        in_specs=[_att_spec(0), _att_edge_spec(1, -1, nb), _att_spec(1), _att_edge_spec(2, -1, nb), _att_spec(2),
                  tok, tok, tok],
        out_specs=tok,
        out_shape=jax.ShapeDtypeStruct((b, r, l, ATT_OUT_DIM), F32),
        compiler_params=_params("parallel", "parallel", "parallel"),
    )(qkr, qkr, qkr, qkr, qkr, datt, lse, delta)


def _att_bwd_kv(qkr, datt, lse, delta, name):
    b, r, l, _ = qkr.shape
    nb = l // ATT_STEP
    d = ATT_HEAD_DIM

    def body(k_ref, v_ref, q_ref, qn_ref, do_ref, don_ref, lse_ref, lsen_ref, dl_ref, dln_ref, dk_ref, dv_ref):
        shape = (ATT_STEP, ATT_STEP + ATT_BLOCK)
        kj = lax.broadcasted_iota(jnp.int32, shape, 0)
        qi = lax.broadcasted_iota(jnp.int32, shape, 1)
        dist = qi - kj
        has_next = pl.program_id(2) < nb - 1
        mask = (dist >= 0) & (dist <= ATT_BLOCK) & (has_next | (qi < ATT_STEP))
        for j in range(ATT_HEADS_PER_GROUP):
            sl = slice(d * j, d * (j + 1))
            qcat = jnp.concatenate([q_ref[:, sl], qn_ref[:, sl]], axis=0)
            docat = jnp.concatenate([do_ref[:, sl], don_ref[:, sl]], axis=0)
            lse_t = jnp.tile(jnp.concatenate([lse_ref[:, sl], lsen_ref[:, sl]], axis=0).T, (ATT_STEP // d, 1))
            dl_t = jnp.tile(jnp.concatenate([dl_ref[:, sl], dln_ref[:, sl]], axis=0).T, (ATT_STEP // d, 1))
            sc_t = _nt(k_ref[:, sl], qcat) * ATT_SCALE
            pr_t = jnp.exp(jnp.where(mask, sc_t - lse_t, NEG_INF))
            dv_ref[:, sl] = _nn(pr_t, docat)
            dsc_t = pr_t * (_nt(v_ref[:, sl], docat) - dl_t)
            dk_ref[:, sl] = _nn(dsc_t, qcat) * ATT_SCALE

    tok, tok_n = _att_spec(0), _att_edge_spec(0, 1, nb)
    return pl.pallas_call(
        body, name=name, grid=(b, r, nb),
        in_specs=[_att_spec(1), _att_spec(2), _att_spec(0), _att_edge_spec(0, 1, nb),
                  tok, tok_n, tok, tok_n, tok, tok_n],
        out_specs=[tok, tok],
        out_shape=[jax.ShapeDtypeStruct((b, r, l, ATT_OUT_DIM), F32)] * 2,
        compiler_params=_params("parallel", "parallel", "parallel"),
    )(qkr, qkr, qkr, qkr, datt, datt, lse, lse, delta, delta)


def _mix_fwd(gl, bg, ys, ya, name):
    t, d = ys.shape
    tm = _pick(t, (512, 256, 128))

    def body(gl_ref, bg_ref, ys_ref, ya_ref, o_ref):
        g0 = _sigmoid(gl_ref[:, :d] + bg_ref[:, :d])
        g1 = _sigmoid(gl_ref[:, d:] + bg_ref[:, d:])
        o_ref[...] = (g0 * ys_ref[...] + g1 * ya_ref[...]).astype(BF16)

    row = pl.BlockSpec((tm, d), lambda i: (i, 0))
    return pl.pallas_call(
        body, name=name, grid=(t // tm,),
        in_specs=[pl.BlockSpec((tm, 2 * d), lambda i: (i, 0)), pl.BlockSpec((1, 2 * d), lambda i: (0, 0)), row, row],
        out_specs=row, out_shape=jax.ShapeDtypeStruct((t, d), BF16),
        compiler_params=_params("parallel"),
    )(gl, bg, ys, ya)


def _mix_bwd(gl, bg, ys, ya, dmixed, name):
    t, d = ys.shape
    tm = _pick(t, (512, 256, 128))

    def body(gl_ref, bg_ref, ys_ref, ya_ref, dm_ref, dys_ref, dya_ref, dgl_ref, dbg_ref):
        @pl.when(pl.program_id(0) == 0)
        def _():
            dbg_ref[...] = jnp.zeros_like(dbg_ref)

        dm = dm_ref[...]
        g0 = _sigmoid(gl_ref[:, :d] + bg_ref[:, :d])
        g1 = _sigmoid(gl_ref[:, d:] + bg_ref[:, d:])
        dys_ref[...] = (dm * g0).astype(BF16)
        dya_ref[...] = (dm * g1).astype(BF16)
        d0 = dm * ys_ref[...] * (g0 * (1.0 - g0))
        d1 = dm * ya_ref[...] * (g1 * (1.0 - g1))
        dgl_ref[:, :d] = d0.astype(BF16)
        dgl_ref[:, d:] = d1.astype(BF16)
        dbg_ref[:, :d] += jnp.sum(d0, axis=0, keepdims=True)
        dbg_ref[:, d:] += jnp.sum(d1, axis=0, keepdims=True)

    row = pl.BlockSpec((tm, d), lambda i: (i, 0))
    wide = pl.BlockSpec((tm, 2 * d), lambda i: (i, 0))
    vec = pl.BlockSpec((1, 2 * d), lambda i: (0, 0))
    gate_cols = pl.BlockSpec((tm, 2 * d), lambda i: (i, DPROJ_COLS["gate"] // (2 * d)))
    return pl.pallas_call(
        body, name=name, grid=(t // tm,),
        in_specs=[wide, vec, row, row, row], out_specs=[row, row, gate_cols, vec],
        out_shape=[jax.ShapeDtypeStruct((t, d), BF16), jax.ShapeDtypeStruct((t, d), BF16),
                   jax.ShapeDtypeStruct((t, DPROJ_WIDTH), BF16), jax.ShapeDtypeStruct((1, 2 * d), F32)],
        compiler_params=_params("arbitrary"),
    )(gl, bg, ys, ya, dmixed)


def _swiglu_fwd(gt, up, name):
    t, f = gt.shape
    tm = _pick(t, (512, 256, 128))

    def body(g_ref, u_ref, o_ref):
        gv = g_ref[...]
        o_ref[...] = ((gv * _sigmoid(gv)) * u_ref[...]).astype(BF16)

    row = pl.BlockSpec((tm, f), lambda i: (i, 0))
    return pl.pallas_call(
        body, name=name, grid=(t // tm,), in_specs=[row, row], out_specs=row,
        out_shape=jax.ShapeDtypeStruct((t, f), BF16), compiler_params=_params("parallel"),
    )(gt, up)


def _swiglu_bwd(gt, up, dact, name):
    t, f = gt.shape
    tm = _pick(t, (512, 256, 128))

    def body(g_ref, u_ref, d_ref, dg_ref, du_ref):
        gv = g_ref[...]
        dv = d_ref[...]
        sg = _sigmoid(gv)
        dg_ref[...] = (dv * u_ref[...] * (sg * (1.0 + gv * (1.0 - sg)))).astype(BF16)
        du_ref[...] = (dv * (gv * sg)).astype(BF16)

    row = pl.BlockSpec((tm, f), lambda i: (i, 0))
    return pl.pallas_call(
        body, name=name, grid=(t // tm,), in_specs=[row, row, row], out_specs=[row, row],
        out_shape=[jax.ShapeDtypeStruct((t, f), BF16)] * 2, compiler_params=_params("parallel"),
    )(gt, up, dact)


def _peer(k):
    x, y, c = lax.axis_index("x"), lax.axis_index("y"), lax.axis_index("c")
    px, py, pc = x ^ ((k >> 2) & 1), y ^ ((k >> 1) & 1), c ^ (k & 1)
    return (px, py, pc), 4 * px + 2 * py + pc


def _my_index():
    return 4 * lax.axis_index("x") + 2 * lax.axis_index("y") + lax.axis_index("c")


def _all_gather(parts, name):
    n_parts = len(parts)

    def body(*refs):
        ins, outs = refs[:n_parts], refs[n_parts:2 * n_parts]
        send_sems, recv_sems, local_sems = refs[2 * n_parts:]
        here, me = _peer(0)
        sibling, sib_idx = _peer(1)
        chips = [_peer(2 * q) for q in range(1, N_CHIPS)]

        def copy(i, k, block, to, src=None):
            return pltpu.make_async_remote_copy(
                src_ref=outs[i].at[block] if src is None else src, dst_ref=outs[i].at[block],
                send_sem=send_sems.at[i * (N_DEV - 1) + k], recv_sem=recv_sems.at[i * (N_DEV - 1) + k],
                device_id=to, device_id_type=MESH)

        local = [pltpu.make_async_copy(ins[i], outs[i].at[me], local_sems.at[i]) for i in range(n_parts)]
        for cp in local:
            cp.start()
        sends = []
        for i in range(n_parts):
            sends.append(copy(i, 0, me, sibling, src=ins[i]))
            sends += [copy(i, q, me, chip, src=ins[i]) for q, (chip, _) in enumerate(chips, start=1)]
        for cp in sends:
            cp.start()
        for q, (chip, chip_idx) in enumerate(chips, start=1):
            for i in range(n_parts):
                copy(i, q, chip_idx, here).wait_recv()
                fwd = copy(i, N_CHIPS - 1 + q, chip_idx, sibling)
                fwd.start()
                sends.append(fwd)
        for i in range(n_parts):
            copy(i, 0, sib_idx, here).wait_recv()
        for q, (_, chip_idx) in enumerate(chips, start=1):
            for i in range(n_parts):
                copy(i, N_CHIPS - 1 + q, chip_idx ^ 1, here).wait_recv()
        for cp in sends:
            cp.wait_send()
        for cp in local:
            cp.wait()

    hbm = pl.BlockSpec(memory_space=pl.ANY)
    return pl.pallas_call(
        body, name=name, in_specs=[hbm] * n_parts, out_specs=[hbm] * n_parts,
        out_shape=[jax.ShapeDtypeStruct((N_DEV,) + p_.shape, p_.dtype) for p_ in parts],
        scratch_shapes=[pltpu.SemaphoreType.DMA((n_parts * (N_DEV - 1),)),
                        pltpu.SemaphoreType.DMA((n_parts * (N_DEV - 1),)),
                        pltpu.SemaphoreType.DMA((n_parts,))],
        compiler_params=pltpu.CompilerParams(has_side_effects=True),
    )(*parts)


HBM_SPEC = pl.BlockSpec(memory_space=pltpu.HBM)
SEM_SPEC = pl.BlockSpec(memory_space=pltpu.SEMAPHORE)
DATAFLOW = pltpu.SideEffectType.DATAFLOW_SIDE_EFFECTING


def _gather_start(block, after, name):
    per_peer = block.ndim == 3

    def body(v_ref, land_ref, after_ref, send_sems, recv_sems, v_thru, land_thru, token):
        me = _my_index()
        for k in range(1, N_DEV):
            peer, pidx = _peer(k)
            pltpu.make_async_remote_copy(
                src_ref=v_ref.at[pidx] if per_peer else v_ref, dst_ref=land_ref.at[me],
                send_sem=send_sems.at[k - 1], recv_sem=recv_sems.at[k - 1],
                device_id=peer, device_id_type=MESH).start()
        token[...] = jnp.zeros_like(token)

    land_shape = (N_DEV,) + block.shape[-2:]
    return pl.pallas_call(
        body, name=name,
        out_shape=(pltpu.SemaphoreType.DMA((N_DEV - 1,)), pltpu.SemaphoreType.DMA((N_DEV - 1,)),
                   pltpu.HBM(block.shape, block.dtype), pltpu.HBM(land_shape, block.dtype),
                   jax.ShapeDtypeStruct((8, LANES), F32)),
        in_specs=(HBM_SPEC, HBM_SPEC, pl.BlockSpec(memory_space=pl.ANY)),
        out_specs=(SEM_SPEC, SEM_SPEC, HBM_SPEC, HBM_SPEC, pl.BlockSpec(memory_space=pltpu.VMEM)),
        input_output_aliases={0: 2, 1: 3},
        compiler_params=pltpu.CompilerParams(has_side_effects=DATAFLOW),
    )(pltpu.with_memory_space_constraint(block, pltpu.HBM),
      pltpu.with_memory_space_constraint(lax.empty(land_shape, block.dtype), pltpu.HBM), after)


def _gather_wait(send_sems, recv_sems, block, landing, after, name):
    per_peer = block.ndim == 3

    def body(v_ref, land_ref, send_sems, recv_sems, after_ref, v_dead, got_ref):
        for k in range(1, N_DEV):
            peer, pidx = _peer(k)
            copy = pltpu.make_async_remote_copy(
                src_ref=v_ref.at[pidx] if per_peer else v_ref, dst_ref=land_ref.at[pidx],
                send_sem=send_sems.at[k - 1], recv_sem=recv_sems.at[k - 1],
                device_id=peer, device_id_type=MESH)
            copy.wait_send()
            copy.wait_recv()

    return pl.pallas_call(
        body, name=name,
        out_shape=(pltpu.HBM(block.shape, block.dtype), pltpu.HBM(landing.shape, landing.dtype)),
        in_specs=(HBM_SPEC, HBM_SPEC, SEM_SPEC, SEM_SPEC, pl.BlockSpec(memory_space=pl.ANY)),
        out_specs=(HBM_SPEC, HBM_SPEC), input_output_aliases={0: 0, 1: 1},
        compiler_params=pltpu.CompilerParams(has_side_effects=DATAFLOW),
    )(block, landing, send_sems, recv_sems, after)[1]


TILE_ELEMS = 1024 * 1024


def _shared_exchange(shared, name):
    def body(sh_ref, gsh_ref, send_sems, recv_sems, local_sem):
        me = _my_index()
        local = pltpu.make_async_copy(sh_ref, gsh_ref.at[me], local_sem)
        local.start()
        sends = []
        for k in range(1, N_DEV):
            peer, _ = _peer(k)
            cp = pltpu.make_async_remote_copy(
                src_ref=sh_ref, dst_ref=gsh_ref.at[me], send_sem=send_sems.at[k - 1],
                recv_sem=recv_sems.at[k - 1], device_id=peer, device_id_type=MESH)
            cp.start()
            sends.append(cp)
        for k in range(1, N_DEV):
            peer, pidx = _peer(k)
            pltpu.make_async_remote_copy(
                src_ref=sh_ref, dst_ref=gsh_ref.at[pidx], send_sem=send_sems.at[k - 1],
                recv_sem=recv_sems.at[k - 1], device_id=peer, device_id_type=MESH).wait_recv()
        for cp in sends:
            cp.wait_send()
        local.wait()

    hbm = pl.BlockSpec(memory_space=pl.ANY)
    return pl.pallas_call(
        body, name=name, in_specs=[hbm], out_specs=hbm,
        out_shape=jax.ShapeDtypeStruct((N_DEV,) + shared.shape, shared.dtype),
        scratch_shapes=[pltpu.SemaphoreType.DMA((N_DEV - 1,)), pltpu.SemaphoreType.DMA((N_DEV - 1,)),
                        pltpu.SemaphoreType.DMA],
        compiler_params=pltpu.CompilerParams(has_side_effects=True),
    )(shared)


def _adamw(parts, w, m, v, name, row0=0):
    n_parts, rows, lanes = parts.shape
    tr = rows if rows * lanes <= TILE_ELEMS // 2 else _tile_rows(math.gcd(rows, row0), TILE_ELEMS // 4 // lanes, 8)
    c1 = 1.0 - ADAM_B1 ** ADAM_STEP
    c2 = 1.0 - ADAM_B2 ** ADAM_STEP

    def body(p_ref, w_ref, m_ref, v_ref, g_ref, d_ref, nm_ref, nv_ref):
        g = p_ref[0].astype(F32)
        for j in range(1, n_parts):
            g = g + p_ref[j].astype(F32)
        nm = ADAM_B1 * m_ref[...] + (1.0 - ADAM_B1) * g
        nv = ADAM_B2 * v_ref[...] + (1.0 - ADAM_B2) * (g * g)
        g_ref[...] = g
        nm_ref[...] = nm
        nv_ref[...] = nv
        d_ref[...] = -ADAM_LR * ((nm / c1) / (jnp.sqrt(nv / c2) + ADAM_EPS) + ADAM_WD * w_ref[...])

    row = pl.BlockSpec((tr, lanes), lambda i: (i, 0))
    state = pl.BlockSpec((tr, lanes), lambda i: (row0 // tr + i, 0))
    return pl.pallas_call(
        body, name=name, grid=(rows // tr,),
        in_specs=[pl.BlockSpec((n_parts, tr, lanes), lambda i: (0, i, 0)), state, state, state],
        out_specs=[row] * 4, out_shape=[jax.ShapeDtypeStruct((rows, lanes), F32)] * 4,
        compiler_params=_params("parallel"),
    )(parts, w, m, v)


MATRIX_SHARDS = (
    ("w_in", (D_MODEL, IN_PROJ_DIM // N_DEV), True),
    ("w_ssm_out", (SSM_D_INNER // N_DEV, D_MODEL), False),
    ("w_att_out", (ATT_OUT_DIM, D_MODEL // N_DEV), True),
    ("w_mix_out", (D_MODEL // N_DEV, D_MODEL), False),
    ("w_ffn_gate", (D_MODEL, D_FF // N_DEV), True),
    ("w_ffn_up", (D_MODEL, D_FF // N_DEV), True),
    ("w_ffn_down", (D_FF // N_DEV, D_MODEL), False),
)
CONV_SHARD = ("conv_w", (SSM_CONV, SSM_CONV_DIM // N_DEV), True)
SHARDED = MATRIX_SHARDS + (CONV_SHARD,)
REPLICATED = (("norm_mix", D_MODEL), ("b_gate", 2 * D_MODEL), ("conv_b", SSM_CONV_DIM), ("dt_bias", SSM_N_HEADS),
              ("a_log", SSM_N_HEADS), ("d_skip", SSM_N_HEADS), ("ssm_norm", SSM_D_INNER), ("norm_ffn", D_MODEL),
              ("norm_final", D_MODEL))


def _round_up(n, mult):
    return -(-n // mult) * mult


def _pack_rows(flat, row_mult):
    rows = _round_up(-(-flat.shape[0] // LANES), row_mult)
    return jnp.pad(flat, (0, rows * LANES - flat.shape[0])).reshape(rows, LANES)


def _stacking(specs):
    return tuple((name, (shape[1], shape[0]) if by_cols else shape, by_cols) for name, shape, by_cols in specs)


def _to_stacking(vals, specs):
    return {name: (vals[name].T if by_cols else vals[name]) for name, _, by_cols in specs}


STACK_WIDTH = D_MODEL
STACK_ALIGN = 16
STACK_ORDER = ("w_ssm_out", "w_mix_out", "w_ffn_gate", "w_ffn_up", "w_ffn_down", "w_att_out", "conv_w", "w_in")
GATHER_LATER = STACK_ORDER[:-1]
REDUCE_EARLY = STACK_ORDER[:5]
REDUCE_LATE = STACK_ORDER[5:]


def _stack_layout():
    shapes = {name: shape for name, shape, _ in _stacking(SHARDED)}
    layout, off = {}, 0
    for name in STACK_ORDER:
        r, c = shapes[name]
        rows = r if c == STACK_WIDTH else _round_up(-(-(r * c) // STACK_WIDTH), STACK_ALIGN)
        layout[name] = (off, rows, (r, c))
        off = _round_up(off + rows, STACK_ALIGN)
    return layout, _round_up(off, 1024)


def _to_stack_rows(v, rows):
    if v.shape[-1] == STACK_WIDTH:
        return v
    lead = v.shape[:-2]
    flat = v.reshape(lead + (-1,))
    flat = jnp.pad(flat, [(0, 0)] * len(lead) + [(0, rows * STACK_WIDTH - flat.shape[-1])])
    return flat.reshape(lead + (rows, STACK_WIDTH))


def _from_stack_rows(block, shape):
    r, c = shape
    if c == STACK_WIDTH:
        return block
    lead = block.shape[:-2]
    return block.reshape(lead + (-1,))[..., :r * c].reshape(lead + (r, c))


def _stack(vals, dtype, skip=(), names=STACK_ORDER):
    layout, total = _stack_layout()
    order = names
    after = STACK_ORDER.index(order[-1]) + 1
    if after < len(STACK_ORDER):
        total = layout[STACK_ORDER[after]][0]
    lead = next(iter(vals.values())).shape[:-2]
    pieces = []
    for i, name in enumerate(order):
        off, rows, _ = layout[name]
        until = layout[order[i + 1]][0] if i + 1 < len(order) else total
        piece = jnp.zeros(lead + (rows, STACK_WIDTH), dtype) if name in skip else _to_stack_rows(vals[name], rows)
        pieces.append(jnp.pad(piece.astype(dtype), [(0, 0)] * len(lead) + [(0, until - off - rows), (0, 0)]))
    return jnp.concatenate(pieces, axis=-2)


def _unstack(stacked, names):
    layout, _ = _stack_layout()
    row0 = layout[names[0]][0]
    return {name: _from_stack_rows(stacked[..., layout[name][0] - row0:layout[name][0] - row0 + layout[name][1], :],
                                   layout[name][2]) for name in names}


W_IN_SHARD_ROWS = IN_PROJ_DIM // N_DEV


def _w_in_row_moves():
    moves, orig = [], 0
    for name, size in IN_SPLIT:
        for j in range(N_DEV):
            lo, hi = max(orig, W_IN_SHARD_ROWS * j), min(orig + size, W_IN_SHARD_ROWS * (j + 1))
            if lo < hi:
                moves.append((j, lo - W_IN_SHARD_ROWS * j, DPROJ_COLS[name] + lo - orig, hi - lo))
        orig += size
    return moves


def _w_in_from_shards(shards, name):
    total, base = shards.shape[1], 0
    pad_lo, pad_hi = DPROJ_COLS["dt"] + _round_up(SSM_N_HEADS, STACK_ALIGN), DPROJ_COLS["dt"] + DPROJ_DT_WIDTH

    def body(x_ref, o_ref):
        o_ref[pad_lo:pad_hi, :] = jnp.zeros((pad_hi - pad_lo, LANES), x_ref.dtype)
        for j, r, at, n in _w_in_row_moves():
            o_ref[at:at + n, :] = x_ref[j, base + r:base + r + n, :]

    return pl.pallas_call(
        body, name=name, grid=(STACK_WIDTH // LANES,),
        in_specs=[pl.BlockSpec((N_DEV, total, LANES), lambda c: (0, 0, c))],
        out_specs=pl.BlockSpec((DPROJ_WIDTH, LANES), lambda c: (0, c)),
        out_shape=jax.ShapeDtypeStruct((DPROJ_WIDTH, STACK_WIDTH), shards.dtype),
        compiler_params=_params("parallel"),
    )(shards)


def _w_in_to_shards(dw_all, head, name):
    layout, total = _stack_layout()
    total -= layout[REDUCE_LATE[0]][0]
    base = head.shape[1]
    end = base + W_IN_SHARD_ROWS

    def body(x_ref, h_ref, o_ref):
        o_ref[:, 0:base, :] = h_ref[...]
        for j, r, at, n in _w_in_row_moves():
            o_ref[j, base + r:base + r + n, :] = x_ref[at:at + n, :]
        o_ref[:, end:total, :] = jnp.zeros((N_DEV, total - end, LANES), o_ref.dtype)

    return pl.pallas_call(
        body, name=name, grid=(STACK_WIDTH // LANES,),
        in_specs=[pl.BlockSpec((DPROJ_WIDTH, LANES), lambda c: (0, c)),
                  pl.BlockSpec((N_DEV, base, LANES), lambda c: (0, 0, c))],
        out_specs=pl.BlockSpec((N_DEV, total, LANES), lambda c: (0, 0, c)),
        out_shape=jax.ShapeDtypeStruct((N_DEV, total, STACK_WIDTH), dw_all.dtype),
        compiler_params=_params("parallel"),
    )(dw_all, head)


REPLICATED_ROWS = sum(-(-size // LANES) for _, size in REPLICATED)
LOSS_ROW = REPLICATED_ROWS


def _pack_replicated(vals):
    rows = []
    for name, size in REPLICATED:
        v = vals[name].reshape(-1).astype(F32)
        rows.append(jnp.pad(v, (0, _round_up(size, LANES) - size)))
    return _pack_rows(jnp.concatenate(rows), 8)


def _unpack_replicated(packed, shapes):
    flat = packed.reshape(-1)
    out, off = {}, 0
    for name, size in REPLICATED:
        out[name] = flat[off:off + size].reshape(shapes[name])
        off += _round_up(size, LANES)
    return out


def _lane_row(v):
    v = v.reshape(-1).astype(F32)
    return jnp.pad(v, (0, LANES - v.shape[0])).reshape(1, LANES)


IN_SPLIT = (("z", SSM_D_INNER), ("xbc", SSM_CONV_DIM), ("dt", SSM_N_HEADS), ("qkv", ATT_QKV_DIM), ("gate", 2 * D_MODEL))


def kernel(x, norm_mix, w_in, b_gate, conv_w, conv_b, dt_bias, a_log, d_skip, ssm_norm, w_ssm_out, w_att_out, w_mix_out, norm_ffn, w_ffn_gate, w_ffn_up, w_ffn_down, norm_final, loss_target, m_norm_mix, m_w_in, m_b_gate, m_conv_w, m_conv_b, m_dt_bias, m_a_log, m_d_skip, m_ssm_norm, m_w_ssm_out, m_w_att_out, m_w_mix_out, m_norm_ffn, m_w_ffn_gate, m_w_ffn_up, m_w_ffn_down, m_norm_final, v_norm_mix, v_w_in, v_b_gate, v_conv_w, v_conv_b, v_dt_bias, v_a_log, v_d_skip, v_ssm_norm, v_w_ssm_out, v_w_att_out, v_w_mix_out, v_norm_ffn, v_w_ffn_gate, v_w_ffn_up, v_w_ffn_down, v_norm_final):
    given = dict(locals())
    weights = {name: given[name][0] for name, _, _ in SHARDED}
    b, s, d = x.shape
    t = b * s

    stacking = _to_stacking(weights, SHARDED)
    conv_shape = dict((name, shape) for name, shape, _ in _stacking(SHARDED))["conv_w"]
    w_in_local = jnp.pad(stacking["w_in"].astype(BF16), ((0, -W_IN_SHARD_ROWS % STACK_ALIGN), (0, 0)))
    conv_local = _pack_rows(stacking["conv_w"].reshape(-1), 8)
    w_in_shards, conv_all = _all_gather([w_in_local, conv_local], "w_in_all_gather")
    head_local = _stack(stacking, BF16, skip=("conv_w",), names=GATHER_LATER)
    in_flight = _gather_start(head_local, conv_all, "weights_gather_start")
    w_in_all = _w_in_from_shards(w_in_shards, "w_in_from_shards")
    w_sec = {name: w_in_all[DPROJ_COLS[name]:DPROJ_COLS[name] + _round_up(size, LANES)] for name, size in IN_SPLIT}
    conv_size = conv_shape[0] * conv_shape[1]
    conv_taps = conv_all.reshape(N_DEV, -1)[:, :conv_size].reshape(N_DEV * conv_shape[0], conv_shape[1]).T

    g_mix, g_ffn, g_fin = norm_mix.reshape(1, d), norm_ffn.reshape(1, d), norm_final.reshape(1, d)
    g_mix = g_mix + in_flight[4][:1, :1]
    bg_row = b_gate.reshape(1, 2 * d)
    convb_row = conv_b.reshape(1, SSM_CONV_DIM)
    ssmn_row = ssm_norm.reshape(1, SSM_D_INNER)
    dtb_row, alog_row = _lane_row(dt_bias), _lane_row(a_log)
    cosf, sinf = _rope_tables(s)

    x2d = x.reshape(t, d)
    h1 = _rmsnorm_fwd(x2d, g_mix, "norm_mix_fwd")
    proj = {name: _mm(h1, w_sec[name], mode="nt", name="in_proj_" + name) for name, _ in IN_SPLIT}
    xbc3 = proj["xbc"].reshape(b, s, SSM_CONV_DIM)
    xc = _conv_fwd(xbc3, conv_taps, convb_row, "conv_fwd")
    dtr3 = proj["dt"].reshape(b, s, DT_PAD)
    to_channels, to_heads = _head_masks()
    dskx = jnp.repeat(d_skip.reshape(-1).astype(F32), SSM_HEAD_DIM).reshape(1, SSM_D_INNER)
    y_ssd, h_states = _ssd_fwd(xc, dtr3, dtb_row, alog_row, dskx, to_channels, "ssd_fwd")
    y_ssd2 = y_ssd.reshape(t, SSM_D_INNER)
    ynorm = _gate_norm_fwd(y_ssd2, proj["z"], ssmn_row, "ssd_gate_norm_fwd")
    landed = _gather_wait(*in_flight[:4], ynorm, "weights_gather_wait")
    head_all = lax.dynamic_update_slice(landed, head_local[None], (_my_index(), 0, 0))
    full = {name: v.reshape((-1,) + v.shape[2:]) for name, v in _unstack(head_all, STACK_ORDER[:-2]).items()}
    y_ssm = _mm(ynorm, full["w_ssm_out"], mode="nn", name="ssm_out_proj")

    qkv3 = proj["qkv"].reshape(b, s, ATT_QKV_DIM)
    qk_parts = _rope_fwd(qkv3, cosf, sinf, "rope_fwd")
    att_parts = [_att_fwd(qk_parts[gi], "att_fwd_%d" % r) for gi, r in enumerate(ATT_DILATIONS)]
    att, *lse_parts = _att_merge([o for o, _ in att_parts], [l_ for _, l_ in att_parts], "att_merge")
    att2 = att.reshape(t, ATT_OUT_DIM)
    y_att = _mm(att2, full["w_att_out"], mode="nt", name="att_out_proj")

    mixed = _mix_fwd(proj["gate"], bg_row, y_ssm, y_att, "mix_fwd")
    x2 = _mm(mixed, full["w_mix_out"], mode="nn", name="mix_out_proj", add=x2d)
    h2 = _rmsnorm_fwd(x2, g_ffn, "norm_ffn_fwd")
    gt = _mm(h2, full["w_ffn_gate"], mode="nt", name="ffn_gate_proj")
    up = _mm(h2, full["w_ffn_up"], mode="nt", name="ffn_up_proj")
    act = _swiglu_fwd(gt, up, "swiglu_fwd")
    x3 = _mm(act, full["w_ffn_down"], mode="nn", name="ffn_down_proj", add=x2)

    loss_row, dx3, dg_fin, dx3b = _loss_head(x3, g_fin, loss_target.reshape(t, d), "loss_head")
    grads = {}
    dact = _mm(dx3b, full["w_ffn_down"], mode="nt", name="ffn_down_dx")
    grads["w_ffn_down"] = _mm(act, dx3b, mode="tn", name="ffn_down_dw", out_dtype=BF16)
    dgt, dup = _swiglu_bwd(gt, up, dact, "swiglu_bwd")
    grads["w_ffn_gate"] = _mm(dgt, h2, mode="tn", name="ffn_gate_dw", out_dtype=BF16)
    grads["w_ffn_up"] = _mm(dup, h2, mode="tn", name="ffn_up_dw", out_dtype=BF16)
    dh2 = _mm(dgt, full["w_ffn_gate"], mode="nn", name="ffn_gate_dx")
    dh2 = _mm(dup, full["w_ffn_up"], mode="nn", name="ffn_up_dx", add=dh2)
    dx2, dg_ffn, dx2b = _rmsnorm_bwd(x2, g_ffn, dh2, dx3, "norm_ffn_bwd", with_bf16=True)

    dmixed = _mm(dx2b, full["w_mix_out"], mode="nt", name="mix_out_dx")
    grads["w_mix_out"] = _mm(mixed, dx2b, mode="tn", name="mix_out_dw", out_dtype=BF16)
    dys, dya, dproj, dbg = _mix_bwd(proj["gate"], bg_row, y_ssm, y_att, dmixed, "mix_bwd")

    grads["w_ssm_out"] = _mm(ynorm, dys, mode="tn", name="ssm_out_dw", out_dtype=BF16)
    early = _stack({name: grads[name].reshape((N_DEV, -1, STACK_WIDTH)) for name in REDUCE_EARLY}, BF16,
                   names=REDUCE_EARLY)
    early_flight = _gather_start(early, dys, "grads_scatter_start")
    ssmn_row = ssmn_row + early_flight[4][:1, :1]
    dynorm = _mm(dys, full["w_ssm_out"], mode="nt", name="ssm_out_dx")
    dy_ssd, dproj, dssmn = _gate_norm_bwd(y_ssd2, proj["z"], ssmn_row, dynorm, dproj, "ssd_gate_norm_bwd")
    dxc, dproj, dalog, ddsk, ddtb = _ssd_bwd(xc, dtr3, dy_ssd.reshape(b, s, SSM_D_INNER), h_states, dtb_row, alog_row,
                                             dskx, to_channels, to_heads, dproj.reshape(b, s, DPROJ_WIDTH), "ssd_bwd")
    dproj, dconvw, dconvb = _conv_bwd(xbc3, dxc, conv_taps, convb_row, dproj, "conv_bwd")
    grads["conv_w"] = dconvw.T.astype(BF16)

    grads["w_att_out"] = _mm(dya, att2, mode="tn", name="att_out_dw", out_dtype=BF16)
    datt = _mm(dya, full["w_att_out"], mode="nn", name="att_out_dx").reshape(b, s, ATT_OUT_DIM)
    do_parts, dl_parts = _att_delta(att, datt, "att_delta")
    dqs, dks, dvs = [], [], []
    for gi, r in enumerate(ATT_DILATIONS):
        operands = (qk_parts[gi], do_parts[gi], lse_parts[gi], dl_parts[gi])
        dqs.append(_att_bwd_q(*operands, "att_bwd_q_%d" % r))
        dk_g, dv_g = _att_bwd_kv(*operands, "att_bwd_kv_%d" % r)
        dks.append(dk_g)
        dvs.append(dv_g)
    dproj = _rope_bwd(dqs, dks, dvs, cosf, sinf, dproj, "rope_bwd").reshape(t, DPROJ_WIDTH)

    dw_all = _mm(dproj, h1, mode="tn", name="in_proj_dw", out_dtype=BF16)
    head = _stack({name: grads[name].reshape((N_DEV, -1, grads[name].shape[-1])) for name in REDUCE_LATE[:-1]}, BF16,
                  names=REDUCE_LATE[:-1])
    late = _w_in_to_shards(dw_all, head, "grad_stacks")
    late_flight = _gather_start(late, dw_all, "grads_late_scatter_start")
    dh1 = _mm(dproj, w_in_all, mode="nn", name="in_proj_dx", after=late_flight[4])
    grad_x, dg_mix = _rmsnorm_bwd(x2d, g_mix, dh1, dx2, "norm_mix_bwd")

    small = {"norm_mix": dg_mix, "b_gate": dbg, "conv_b": dconvb, "dt_bias": ddtb[:, :SSM_N_HEADS],
             "a_log": dalog[:, :SSM_N_HEADS], "d_skip": ddsk[:, :SSM_N_HEADS], "ssm_norm": dssmn,
             "norm_ffn": dg_ffn, "norm_final": dg_fin}
    shared = _pack_replicated(small)
    shared = shared.at[LOSS_ROW, 0].set(loss_row[0, 0])
    got_small = _shared_exchange(shared, "shared_grads_exchange")

    def packed(prefix):
        vals = _to_stacking({name: given[prefix + name][0] for name, _, _ in SHARDED}, SHARDED)
        rep = {name: given[prefix + name] for name, _ in REPLICATED}
        return _stack(vals, F32), _pack_replicated(rep)

    (w_big, w_small), (m_big, m_small), (v_big, v_small) = packed(""), packed("m_"), packed("v_")
    me = _my_index()

    def arrived(flight, slabs, name):
        landed = _gather_wait(*flight[:4], got_small, name)
        mine = lax.dynamic_slice(slabs, (me, 0, 0), (1,) + slabs.shape[1:])
        return lax.dynamic_update_slice(landed, mine, (me, 0, 0))

    big_early = _adamw(arrived(early_flight, early, "grads_scatter_wait"), w_big, m_big, v_big, "adamw_early")
    big_late = _adamw(arrived(late_flight, late, "grads_late_scatter_wait"), w_big, m_big, v_big, "adamw_late",
                      row0=early.shape[1])
    sml = _adamw(got_small, w_small, m_small, v_small, "adamw_replicated")

    outs = [sml[0][LOSS_ROW, 0], grad_x.reshape(b, s, d)]
    rep_shapes = {name: given[name].shape for name, _ in REPLICATED}
    order = ["norm_mix", "w_in", "b_gate", "conv_w", "conv_b", "dt_bias", "a_log", "d_skip", "ssm_norm", "w_ssm_out",
             "w_att_out", "w_mix_out", "norm_ffn", "w_ffn_gate", "w_ffn_up", "w_ffn_down", "norm_final"]
    for early_k, late_k, sml_k in zip(big_early, big_late, sml):
        stacks = dict(_unstack(early_k, REDUCE_EARLY), **_unstack(late_k, REDUCE_LATE))
        sharded = _to_stacking(stacks, SHARDED)
        rep = _unpack_replicated(sml_k, rep_shapes)
        for name in order:
            outs.append(sharded[name][None] if name in sharded else rep[name])
    return tuple(outs)
```

```python
import functools
import math

import jax
import jax.numpy as jnp
from jax import lax
from jax.experimental import pallas as pl
from jax.experimental.pallas import tpu as pltpu

F32 = jnp.float32
BF16 = jnp.bfloat16

N_DEV = 8
N_CHIPS = 4
D_MODEL = 1024
SSM_D_INNER = 2048
SSM_HEAD_DIM = 64
SSM_N_HEADS = 32
SSM_N_GROUPS = 4
SSM_HEADS_PER_GROUP = SSM_N_HEADS // SSM_N_GROUPS
SSM_D_STATE = 128
SSM_CONV = 4
SSM_CHUNK = 128
SSM_CONV_DIM = 3072
ATT_HEAD_DIM = 128
ATT_HEADS_PER_GROUP = 4
ATT_DILATIONS = (1, 4, 16)
ATT_N_HEADS = 12
ATT_QKV_DIM = 4608
ATT_OUT_DIM = 512
ATT_BLOCK = 128
ROPE_THETA = 10000.0
D_FF = 2816
IN_PROJ_DIM = 11808
EPS = 1e-6
LANES = 128
DT_PAD = LANES

DPROJ_COLS = {"qkv": 0, "z": 4608, "xbc": 6656, "dt": 9728, "gate": 10240}
DPROJ_DT_WIDTH = 512
DPROJ_WIDTH = 12288

ADAM_LR = 0.001
ADAM_B1 = 0.9
ADAM_B2 = 0.999
ADAM_EPS = 1e-08
ADAM_WD = 0.01
ADAM_STEP = 10

VMEM_LIMIT = 56 * 1024 * 1024
MESH = pl.DeviceIdType.MESH
NEG_INF = float("-inf")


def _tile_rows(n, cap, mult):
    return max(t for t in range(mult, min(n, cap) + 1, mult) if n % t == 0)


def _pick(n, candidates):
    for c in candidates:
        if n % c == 0:
            return c
    return n


def _params(*sem):
    return pltpu.CompilerParams(dimension_semantics=sem, vmem_limit_bytes=VMEM_LIMIT)


def _sigmoid(x):
    return 1.0 / (1.0 + jnp.exp(-x))


def _softplus(x):
    return jnp.maximum(x, 0.0) + jnp.log(1.0 + jnp.exp(-jnp.abs(x)))


def _dot(a, b, dims):
    return lax.dot_general(a.astype(BF16), b.astype(BF16), (dims, ((), ())), preferred_element_type=F32)


def _nn(a, b):
    return _dot(a, b, ((1,), (0,)))


def _nt(a, b):
    return _dot(a, b, ((1,), (1,)))


def _tn(a, b):
    return _dot(a, b, ((0,), (0,)))


def _split3(v):
    hi = v.astype(BF16)
    r1 = v - hi.astype(F32)
    mid = r1.astype(BF16)
    lo = (r1 - mid.astype(F32)).astype(BF16)
    return hi, mid, lo


def _mask_nn(mask, v):
    mb = mask.astype(BF16)
    hi, mid, lo = _split3(v)
    return _nn(mb, hi) + (_nn(mb, mid) + _nn(mb, lo))


MM_VMEM_BUDGET = 40 * 1024 * 1024
MM_FULL_K = 2816


def _mm_tiles(m, n, k, a_bytes, b_bytes, o_bytes, has_add):
    tk = k if k <= MM_FULL_K else _pick(k, (2048, 1024, 512, 256, 128))
    tn = 1408 if (n > 1024 and n % 1408 == 0) else _pick(n, (1024, 768, 512, 384, 256, 128))
    for tm in (1408, 1024, 768, 512, 384, 256, 128):
        if m % tm:
            continue
        buffers = 2 * (tm * tk * a_bytes + tk * tn * b_bytes + tm * tn * (o_bytes + (4 if has_add else 0)))
        if tk < k:
            buffers += tm * tn * 4
        if buffers <= MM_VMEM_BUDGET:
            return tm, tn, tk
    return _pick(m, (128,)), tn, tk


def _mm(a, b, *, mode, name, out_dtype=F32, add=None, after=None):
    if mode == "nn":
        (m, k), n = a.shape, b.shape[1]
    elif mode == "nt":
        (m, k), n = a.shape, b.shape[0]
    else:
        (k, m), n = a.shape, b.shape[1]
    has_add = add is not None
    tm, tn, tk = _mm_tiles(m, n, k, a.dtype.itemsize, b.dtype.itemsize, jnp.dtype(out_dtype).itemsize, has_add)
    nk = k // tk
    dims = {"nn": ((1,), (0,)), "nt": ((1,), (1,)), "tn": ((0,), (0,))}[mode]
    a_spec = {"nn": pl.BlockSpec((tm, tk), lambda i, j, kk: (i, kk)),
              "nt": pl.BlockSpec((tm, tk), lambda i, j, kk: (i, kk)),
              "tn": pl.BlockSpec((tk, tm), lambda i, j, kk: (kk, i))}[mode]
    b_spec = {"nn": pl.BlockSpec((tk, tn), lambda i, j, kk: (kk, j)),
              "nt": pl.BlockSpec((tn, tk), lambda i, j, kk: (j, kk)),
              "tn": pl.BlockSpec((tk, tn), lambda i, j, kk: (kk, j))}[mode]
    o_spec = pl.BlockSpec((tm, tn), lambda i, j, kk: (i, j))

    def finish(r, c_ref, o_ref):
        if has_add:
            r = r + c_ref[...]
        o_ref[...] = r.astype(out_dtype)

    def body_one(*refs):
        a_ref, b_ref = refs[:2]
        finish(_dot(a_ref[...], b_ref[...], dims), refs[2] if has_add else None, refs[-1])

    def body_acc(*refs):
        a_ref, b_ref = refs[:2]
        o_ref, acc = refs[-2:]
        kk = pl.program_id(2)

        @pl.when(kk == 0)
        def _():
            acc[...] = jnp.zeros_like(acc)

        acc[...] += _dot(a_ref[...], b_ref[...], dims)

        @pl.when(kk == nk - 1)
        def _():
            finish(acc[...], refs[2] if has_add else None, o_ref)

    in_specs = [a_spec, b_spec] + ([o_spec] if has_add else [])
    args = (a, b) + ((add,) if has_add else ())
    if after is not None:
        in_specs, args = in_specs + [pl.BlockSpec(memory_space=pl.ANY)], args + (after,)
    return pl.pallas_call(
        body_one if nk == 1 else body_acc, name=name, grid=(m // tm, n // tn, nk),
        in_specs=in_specs, out_specs=o_spec,
        out_shape=jax.ShapeDtypeStruct((m, n), out_dtype),
        scratch_shapes=[] if nk == 1 else [pltpu.VMEM((tm, tn), F32)],
        compiler_params=_params("parallel", "parallel", "arbitrary"),
    )(*args)


def _rmsnorm_fwd(x, g, name):
    t, d = x.shape
    tm = _pick(t, (512, 256, 128))

    def body(x_ref, g_ref, o_ref):
        xv = x_ref[...]
        r = lax.rsqrt(jnp.mean(xv * xv, axis=-1, keepdims=True) + EPS)
        o_ref[...] = ((xv * r) * g_ref[...]).astype(BF16)

    return pl.pallas_call(
        body, name=name, grid=(t // tm,),
        in_specs=[pl.BlockSpec((tm, d), lambda i: (i, 0)), pl.BlockSpec((1, d), lambda i: (0, 0))],
        out_specs=pl.BlockSpec((tm, d), lambda i: (i, 0)),
        out_shape=jax.ShapeDtypeStruct((t, d), BF16),
        compiler_params=_params("parallel"),
    )(x, g)


def _rmsnorm_bwd(x, g, dh, dres, name, with_bf16=False):
    t, d = x.shape
    tm = _pick(t, (512, 256, 128))

    def body(x_ref, g_ref, dh_ref, dres_ref, dx_ref, dg_ref, *dxb_ref):
        @pl.when(pl.program_id(0) == 0)
        def _():
            dg_ref[...] = jnp.zeros_like(dg_ref)

        xv = x_ref[...]
        r = lax.rsqrt(jnp.mean(xv * xv, axis=-1, keepdims=True) + EPS)
        xhat = xv * r
        dhv = dh_ref[...]
        dyg = dhv * g_ref[...]
        dx = dres_ref[...] + r * (dyg - xhat * jnp.mean(dyg * xhat, axis=-1, keepdims=True))
        dx_ref[...] = dx
        if with_bf16:
            dxb_ref[0][...] = dx.astype(BF16)
        dg_ref[...] += jnp.sum(dhv * xhat, axis=0, keepdims=True)

    row = pl.BlockSpec((tm, d), lambda i: (i, 0))
    vec = pl.BlockSpec((1, d), lambda i: (0, 0))
    extra = with_bf16 * [jax.ShapeDtypeStruct((t, d), BF16)]
    return pl.pallas_call(
        body, name=name, grid=(t // tm,),
        in_specs=[row, vec, row, row], out_specs=[row, vec] + with_bf16 * [row],
        out_shape=[jax.ShapeDtypeStruct((t, d), F32), jax.ShapeDtypeStruct((1, d), F32)] + extra,
        compiler_params=_params("arbitrary"),
    )(x, g, dh, dres)


def _loss_head(x, g, target, name):
    t, d = x.shape
    tm = _pick(t, (512, 256, 128))

    def body(x_ref, g_ref, t_ref, loss_ref, dx_ref, dg_ref, dxb_ref):
        @pl.when(pl.program_id(0) == 0)
        def _():
            dg_ref[...] = jnp.zeros_like(dg_ref)
            loss_ref[...] = jnp.zeros_like(loss_ref)

        xv = x_ref[...]
        gv = g_ref[...]
        r = lax.rsqrt(jnp.mean(xv * xv, axis=-1, keepdims=True) + EPS)
        xhat = xv * r
        err = xhat * gv - t_ref[...]
        loss_ref[...] += jnp.sum(err * err) * (0.5 / d)
        dy = err * (1.0 / d)
        dyg = dy * gv
        dx = r * (dyg - xhat * jnp.mean(dyg * xhat, axis=-1, keepdims=True))
        dx_ref[...] = dx
        dxb_ref[...] = dx.astype(BF16)
        dg_ref[...] += jnp.sum(dy * xhat, axis=0, keepdims=True)

    row = pl.BlockSpec((tm, d), lambda i: (i, 0))
    vec = pl.BlockSpec((1, d), lambda i: (0, 0))
    return pl.pallas_call(
        body, name=name, grid=(t // tm,),
        in_specs=[row, vec, row],
        out_specs=[pl.BlockSpec((1, LANES), lambda i: (0, 0)), row, vec, row],
        out_shape=[jax.ShapeDtypeStruct((1, LANES), F32), jax.ShapeDtypeStruct((t, d), F32),
                   jax.ShapeDtypeStruct((1, d), F32), jax.ShapeDtypeStruct((t, d), BF16)],
        compiler_params=_params("arbitrary"),
    )(x, g, target)


CONV_HALO = 8
CONV_ROWS = 64


def _conv_taps(window, wv, bv):
    acc = bv + wv[SSM_CONV - 1:SSM_CONV, :] * window(0)
    for sh in range(1, SSM_CONV):
        kidx = SSM_CONV - 1 - sh
        acc = acc + wv[kidx:kidx + 1, :] * window(sh)
    return acc


def _conv_fwd(u, w, bias, name):
    b, s, c = u.shape
    rows = CONV_ROWS

    def body(u_ref, w_ref, b_ref, o_ref, ext):
        ext[0:CONV_HALO, :] = jnp.zeros((CONV_HALO, LANES), F32)
        ext[CONV_HALO:, :] = u_ref[...]
        wv, bv = w_ref[...], b_ref[...]
        for r0 in range(0, s, rows):
            acc = _conv_taps(lambda sh: ext[CONV_HALO + r0 - sh:CONV_HALO + r0 - sh + rows, :], wv, bv)
            o_ref[r0:r0 + rows, :] = acc * _sigmoid(acc)

    strip = pl.BlockSpec((None, s, LANES), lambda bi, j: (bi, 0, j))
    return pl.pallas_call(
        body, name=name, grid=(b, c // LANES),
        in_specs=[strip, pl.BlockSpec((SSM_CONV, LANES), lambda bi, j: (0, j)),
                  pl.BlockSpec((1, LANES), lambda bi, j: (0, j))],
        out_specs=strip, out_shape=jax.ShapeDtypeStruct((b, s, c), F32),
        scratch_shapes=[pltpu.VMEM((CONV_HALO + s, LANES), F32)],
        compiler_params=_params("parallel", "parallel"),
    )(u, w, bias)


def _conv_bwd(u, dout, w, bias, dproj, name):
    b, s, c = u.shape
    rows = CONV_ROWS

    def fold(v):
        return jnp.sum(v.reshape(rows // CONV_HALO, CONV_HALO, LANES), axis=0)

    def body(u_ref, d_ref, w_ref, b_ref, buf_ref, du_ref, dw_ref, db_ref, ext, dpre):
        @pl.when(pl.program_id(1) == 0)
        def _():
            dw_ref[...] = jnp.zeros_like(dw_ref)
            db_ref[...] = jnp.zeros_like(db_ref)

        ext[0:CONV_HALO, :] = jnp.zeros((CONV_HALO, LANES), F32)
        ext[CONV_HALO:, :] = u_ref[...]
        dpre[s:, :] = jnp.zeros((CONV_HALO, LANES), F32)
        wv, bv = w_ref[...], b_ref[...]
        sums = [jnp.zeros((CONV_HALO, LANES), F32)] * (SSM_CONV + 1)
        for r0 in range(0, s, rows):
            window = lambda sh: ext[CONV_HALO + r0 - sh:CONV_HALO + r0 - sh + rows, :]
            acc = _conv_taps(window, wv, bv)
            sg = _sigmoid(acc)
            dp = d_ref[r0:r0 + rows, :] * (sg * (1.0 + acc * (1.0 - sg)))
            dpre[r0:r0 + rows, :] = dp
            taps = [sums[SSM_CONV - 1 - sh] + fold(dp * window(sh)) for sh in range(SSM_CONV)]
            sums = taps[::-1] + [sums[SSM_CONV] + fold(dp)]
        for r0 in range(0, s, rows):
            du = wv[SSM_CONV - 1:SSM_CONV, :] * dpre[r0:r0 + rows, :]
            for sh in range(1, SSM_CONV):
                kidx = SSM_CONV - 1 - sh
                du = du + wv[kidx:kidx + 1, :] * dpre[r0 + sh:r0 + sh + rows, :]
            du_ref[r0:r0 + rows, :] = du.astype(BF16)
        for kidx in range(SSM_CONV):
            dw_ref[kidx:kidx + 1, :] += jnp.sum(sums[kidx], axis=0, keepdims=True)
        db_ref[...] += jnp.sum(sums[SSM_CONV], axis=0, keepdims=True)

    strip = pl.BlockSpec((None, s, LANES), lambda j, bi: (bi, 0, j))
    taps = pl.BlockSpec((SSM_CONV, LANES), lambda j, bi: (0, j))
    vec = pl.BlockSpec((1, LANES), lambda j, bi: (0, j))
    du_cols = pl.BlockSpec((None, s, LANES), lambda j, bi: (bi, 0, DPROJ_COLS["xbc"] // LANES + j))
    return pl.pallas_call(
        body, name=name, grid=(c // LANES, b),
        in_specs=[strip, strip, taps, vec, pl.BlockSpec(memory_space=pl.ANY)], out_specs=[du_cols, taps, vec],
        input_output_aliases={4: 0},
        out_shape=[jax.ShapeDtypeStruct(dproj.shape, dproj.dtype), jax.ShapeDtypeStruct((SSM_CONV, c), F32),
                   jax.ShapeDtypeStruct((1, c), F32)],
        scratch_shapes=[pltpu.VMEM((CONV_HALO + s, LANES), F32), pltpu.VMEM((s + CONV_HALO, LANES), F32)],
        compiler_params=_params("parallel", "arbitrary"),
    )(u, dout, w, bias, dproj)


def _ssd_chunk_terms(dtr_ref, bias_ref, alog_ref):
    q = SSM_CHUNK
    dt = _softplus(dtr_ref[...] + bias_ref[...])
    a_neg = -jnp.exp(alog_ref[...])
    row = lax.broadcasted_iota(jnp.int32, (q, q), 0)
    col = lax.broadcasted_iota(jnp.int32, (q, q), 1)
    lower = row >= col
    s = _mask_nn(lower, dt * a_neg)
    return dt, a_neg, s, s.T, lower


def _head_masks():
    heads = jnp.arange(LANES)[:, None]
    chans = jnp.arange(SSM_D_INNER)[None, :]
    to_channels = (chans // SSM_HEAD_DIM == heads).astype(BF16)
    return to_channels, to_channels.T


def _per_channel(v, to_channels):
    hi = v.astype(BF16)
    lo = (v - hi.astype(F32)).astype(BF16)
    return _nn(hi, to_channels) + _nn(lo, to_channels)


def _per_head(v, to_heads):
    hi = v.astype(BF16)
    lo = (v - hi.astype(F32)).astype(BF16)
    return _nn(hi, to_heads) + _nn(lo, to_heads)


def _decay_terms_per_channel(dt, s_col, to_channels):
    q = SSM_CHUNK
    tot = s_col[q - 1:q, :]
    stacked = jnp.concatenate([dt, jnp.exp(s_col), jnp.exp(tot - s_col)], axis=0)
    wide = _per_channel(stacked, to_channels)
    dtx, esx, decx = wide[:q], wide[q:2 * q], wide[2 * q:]
    return dtx, esx, decx, esx[0:1, :] * decx[0:1, :]


SSM_PAIRS_PER_GROUP = SSM_HEADS_PER_GROUP // 2
SSM_GROUP_CHANNELS = SSM_HEADS_PER_GROUP * SSM_HEAD_DIM


def _split_pair(v):
    first = lax.broadcasted_iota(jnp.int32, v.shape, 1) < SSM_HEAD_DIM
    return jnp.concatenate([jnp.where(first, v, 0.0), jnp.where(first, 0.0, v)], axis=0)


def _ssd_fwd(xc, dtr, dt_bias, a_log, dskx, to_channels, name):
    b, s, _ = xc.shape
    q = SSM_CHUNK
    nc = s // q
    n, gc = SSM_D_STATE, SSM_GROUP_CHANNELS

    def body(xc_ref, dtr_ref, bias_ref, alog_ref, dsk_ref, tc_ref, y_ref, hs_ref, h_scr):
        @pl.when(pl.program_id(1) == 0)
        def _():
            h_scr[...] = jnp.zeros_like(h_scr)

        dt, _, s_col, s_row, lower = _ssd_chunk_terms(dtr_ref, bias_ref, alog_ref)
        dtx, esx, decx, etotx = _decay_terms_per_channel(dt, s_col, tc_ref[...])
        x = xc_ref[:, :SSM_D_INNER]
        xdt = x * dtx
        xdec = xdt * decx
        skip = dsk_ref[...] * x
        for g in range(SSM_N_GROUPS):
            bg = xc_ref[:, SSM_D_INNER + n * g:SSM_D_INNER + n * (g + 1)].astype(BF16)
            cg = xc_ref[:, SSM_D_INNER + n * (SSM_N_GROUPS + g):SSM_D_INNER + n * (SSM_N_GROUPS + g + 1)].astype(BF16)
            gsl = slice(gc * g, gc * (g + 1))
            gm = _nt(cg, bg)
            hgt = h_scr[:, gsl]
            hs_ref[:, gsl] = hgt
            y_off = esx[:, gsl] * _nn(cg, hgt)
            h_scr[:, gsl] = etotx[:, gsl] * hgt + _tn(bg, xdec[:, gsl])
            for k in range(SSM_PAIRS_PER_GROUP):
                h0 = g * SSM_HEADS_PER_GROUP + 2 * k
                lo = gc * g + LANES * k
                ms = []
                for h in (h0, h0 + 1):
                    lm = jnp.exp(jnp.where(lower, s_col[:, h:h + 1] - s_row[h:h + 1, :], NEG_INF))
                    ms.append((gm * lm).astype(BF16))
                y_diag = _nn(jnp.concatenate(ms, axis=1), _split_pair(xdt[:, lo:lo + LANES]))
                y_ref[:, lo:lo + LANES] = y_diag + y_off[:, LANES * k:LANES * (k + 1)] + skip[:, lo:lo + LANES]

    vec = pl.BlockSpec((1, LANES), lambda bi, c: (0, 0))
    return pl.pallas_call(
        body, name=name, grid=(b, nc),
        in_specs=[pl.BlockSpec((None, q, SSM_CONV_DIM), lambda bi, c: (bi, c, 0)),
                  pl.BlockSpec((None, q, LANES), lambda bi, c: (bi, c, 0)), vec, vec,
                  pl.BlockSpec((1, SSM_D_INNER), lambda bi, c: (0, 0)),
                  pl.BlockSpec((LANES, SSM_D_INNER), lambda bi, c: (0, 0))],
        out_specs=[pl.BlockSpec((None, q, SSM_D_INNER), lambda bi, c: (bi, c, 0)),
                   pl.BlockSpec((None, None, n, SSM_D_INNER), lambda bi, c: (bi, c, 0, 0))],
        out_shape=[jax.ShapeDtypeStruct((b, s, SSM_D_INNER), F32),
                   jax.ShapeDtypeStruct((b, nc, n, SSM_D_INNER), F32)],
        scratch_shapes=[pltpu.VMEM((n, SSM_D_INNER), F32)],
        compiler_params=_params("parallel", "arbitrary"),
    )(xc, dtr, dt_bias, a_log, dskx, to_channels)


def _ssd_bwd(xc, dtr, dy, hs, dt_bias, a_log, dskx, to_channels, to_heads, dproj, name):
    b, s, _ = xc.shape
    q = SSM_CHUNK
    nc = s // q
    n, gc = SSM_D_STATE, SSM_GROUP_CHANNELS

    def colsum(v):
        return jnp.sum(v, axis=0, keepdims=True)

    def body(xc_ref, dtr_ref, dy_ref, hs_ref, bias_ref, alog_ref, dsk_ref, tc_ref, th_ref, buf_ref,
             dxc_ref, ddtr_ref, dalog_ref, ddsk_ref, dbias_ref, dh_scr, dxs_scr, dxd_scr, w_scr, dst_scr, rows_scr):
        ci = pl.program_id(1)

        @pl.when(ci == 0)
        def _():
            dh_scr[...] = jnp.zeros_like(dh_scr)

        @pl.when(jnp.logical_and(pl.program_id(0) == 0, ci == 0))
        def _():
            dalog_ref[...] = jnp.zeros_like(dalog_ref)
            ddsk_ref[...] = jnp.zeros_like(ddsk_ref)
            dbias_ref[...] = jnp.zeros_like(dbias_ref)
            dst_scr[...] = jnp.zeros_like(dst_scr)

        dt, a_neg, s_col, s_row, lower = _ssd_chunk_terms(dtr_ref, bias_ref, alog_ref)
        upper = jnp.logical_not(lower) | (lax.broadcasted_iota(jnp.int32, (q, q), 0)
                                          == lax.broadcasted_iota(jnp.int32, (q, q), 1))
        dtx, esx, decx, etotx = _decay_terms_per_channel(dt, s_col, tc_ref[...])
        x = xc_ref[:, :SSM_D_INNER]
        dyv = dy_ref[...]
        xdt = x * dtx
        xdec = xdt * decx
        dw = esx * dyv
        rows_scr[...] = jnp.zeros_like(rows_scr)
        for g in range(SSM_N_GROUPS):
            b_lo = SSM_D_INNER + n * g
            c_lo = SSM_D_INNER + n * (SSM_N_GROUPS + g)
            bg = xc_ref[:, b_lo:b_lo + n].astype(BF16)
            cg = xc_ref[:, c_lo:c_lo + n].astype(BF16)
            gsl = slice(gc * g, gc * (g + 1))
            gm = _nt(cg, bg)
            gmt = _nt(bg, cg)
            hgt = hs_ref[:, gsl]
            dhgt = dh_scr[:, gsl]
            w_scr[:, gsl] = _nn(cg, hgt)
            dcg = _nt(dw[:, gsl], hgt)
            dxs = decx[:, gsl] * _nn(bg, dhgt)
            dxs_scr[:, gsl] = dxs
            dbg = _nt(xdec[:, gsl], dhgt)
            rows_scr[2:3, gsl] = colsum(dhgt * hgt)
            dh_scr[:, gsl] = _tn(cg, dw[:, gsl]) + etotx[:, gsl] * dhgt
            dg = jnp.zeros((q, q), F32)
            dgt = jnp.zeros((q, q), F32)
            for k in range(SSM_PAIRS_PER_GROUP):
                h0 = g * SSM_HEADS_PER_GROUP + 2 * k
                lo = gc * g + LANES * k
                xp = xdt[:, lo:lo + LANES]
                dyp = dyv[:, lo:lo + LANES]
                dy2 = _split_pair(dyp)
                dm2 = _nt(dy2, xp)
                dmt2 = _nt(_split_pair(xp), dyp)
                mts = []
                for i, h in enumerate((h0, h0 + 1)):
                    lm = jnp.exp(jnp.where(lower, s_col[:, h:h + 1] - s_row[h:h + 1, :], NEG_INF))
                    lmt = jnp.exp(jnp.where(upper, s_row[h:h + 1, :] - s_col[:, h:h + 1], NEG_INF))
                    dm = dm2[q * i:q * (i + 1), :]
                    dmt = dmt2[q * i:q * (i + 1), :]
                    dg = dg + dm * lm
                    dgt = dgt + dmt * lmt
                    mt = gmt * lmt
                    dst_scr[h:h + 1, :] = colsum(dmt * mt) - colsum(dm * (gm * lm))
                    mts.append(mt.astype(BF16))
                dxd_scr[:, lo:lo + LANES] = _nn(jnp.concatenate(mts, axis=1), dy2)
            dxc_ref[:, b_lo:b_lo + n] = dbg + _nn(dgt, cg)
            dxc_ref[:, c_lo:c_lo + n] = dcg + _nn(dg, bg)
        dxs = dxs_scr[...]
        dxdt = dxd_scr[...] + dxs
        dxc_ref[:, :SSM_D_INNER] = dxdt * dtx + dsk_ref[...] * dyv
        state_part = xdt * dxs
        rows_scr[0:1, :] = colsum(dyv * x)
        rows_scr[1:2, :] = colsum(state_part)
        th = th_ref[...]
        per_head = _per_head(jnp.concatenate([dw * w_scr[...] - state_part, dxdt * x], axis=0), th)
        r_ds, r_dt = per_head[:q], per_head[q:]
        sums = _per_head(rows_scr[...], th)
        etot = jnp.exp(s_col[q - 1:q, :])
        dtot = sums[1:2, :] + etot * sums[2:3, :]
        last = lax.broadcasted_iota(jnp.int32, (q, LANES), 0) == q - 1
        ds = dst_scr[...].T + r_ds + jnp.where(last, dtot, 0.0)
        da = _mask_nn(upper, ds)
        ddt = da * a_neg + r_dt
        live = lax.broadcasted_iota(jnp.int32, (1, LANES), 1) < SSM_N_HEADS
        sg = _sigmoid(dtr_ref[...] + bias_ref[...])
        ddtr = jnp.where(live, ddt * sg, 0.0)
        ddtr_ref[:, :LANES] = ddtr.astype(BF16)
        ddtr_ref[:, LANES:] = jnp.zeros((q, DPROJ_DT_WIDTH - LANES), BF16)
        dalog_ref[...] += jnp.where(live, colsum(da * dt) * a_neg, 0.0)
        ddsk_ref[...] += jnp.where(live, sums[0:1, :], 0.0)
        dbias_ref[...] += colsum(ddtr)

    rev = lambda bi, c: (bi, nc - 1 - c, 0)
    vec = pl.BlockSpec((1, LANES), lambda bi, c: (0, 0))
    wide = pl.BlockSpec((None, q, SSM_D_INNER), rev)
    return pl.pallas_call(
        body, name=name, grid=(b, nc),
        in_specs=[pl.BlockSpec((None, q, SSM_CONV_DIM), rev), pl.BlockSpec((None, q, LANES), rev), wide,
                  pl.BlockSpec((None, None, n, SSM_D_INNER), lambda bi, c: (bi, nc - 1 - c, 0, 0)),
                  vec, vec, pl.BlockSpec((1, SSM_D_INNER), lambda bi, c: (0, 0)),
                  pl.BlockSpec((LANES, SSM_D_INNER), lambda bi, c: (0, 0)),
                  pl.BlockSpec((SSM_D_INNER, LANES), lambda bi, c: (0, 0)),
                  pl.BlockSpec(memory_space=pl.ANY)],
        out_specs=[pl.BlockSpec((None, q, SSM_CONV_DIM), rev),
                   pl.BlockSpec((None, q, DPROJ_DT_WIDTH),
                                lambda bi, c: (bi, nc - 1 - c, DPROJ_COLS["dt"] // DPROJ_DT_WIDTH)), vec, vec, vec],
        input_output_aliases={9: 1},
        out_shape=[jax.ShapeDtypeStruct((b, s, SSM_CONV_DIM), F32), jax.ShapeDtypeStruct(dproj.shape, dproj.dtype),
                   jax.ShapeDtypeStruct((1, LANES), F32), jax.ShapeDtypeStruct((1, LANES), F32),
                   jax.ShapeDtypeStruct((1, LANES), F32)],
        scratch_shapes=[pltpu.VMEM((n, SSM_D_INNER), F32)] + [pltpu.VMEM((q, SSM_D_INNER), F32)] * 3
        + [pltpu.VMEM((LANES, q), F32), pltpu.VMEM((8, SSM_D_INNER), F32)],
        compiler_params=_params("arbitrary", "arbitrary"),
    )(xc, dtr, dy, hs, dt_bias, a_log, dskx, to_channels, to_heads, dproj)


SSM_GROUP_WIDTH = SSM_D_INNER // SSM_N_GROUPS


def _gate_norm_fwd(y, z, w, name):
    t, d = y.shape
    tm = _pick(t, (256, 128))

    def body(y_ref, z_ref, w_ref, o_ref):
        for g in range(SSM_N_GROUPS):
            sl = slice(SSM_GROUP_WIDTH * g, SSM_GROUP_WIDTH * (g + 1))
            zv = z_ref[:, sl]
            u = y_ref[:, sl] * (zv * _sigmoid(zv))
            r = lax.rsqrt(jnp.mean(u * u, axis=-1, keepdims=True) + EPS)
            o_ref[:, sl] = ((u * r) * w_ref[:, sl]).astype(BF16)

    row = pl.BlockSpec((tm, d), lambda i: (i, 0))
    return pl.pallas_call(
        body, name=name, grid=(t // tm,),
        in_specs=[row, row, pl.BlockSpec((1, d), lambda i: (0, 0))], out_specs=row,
        out_shape=jax.ShapeDtypeStruct((t, d), BF16),
        compiler_params=_params("parallel"),
    )(y, z, w)


def _gate_norm_bwd(y, z, w, dout, dproj, name):
    t, d = y.shape
    gw = SSM_GROUP_WIDTH
    tm = _pick(t, (1024, 512, 256, 128))

    def body(y_ref, z_ref, w_ref, do_ref, buf_ref, dy_ref, dz_ref, dw_ref):
        @pl.when(pl.program_id(1) == 0)
        def _():
            dw_ref[...] = jnp.zeros_like(dw_ref)

        zv = z_ref[...]
        yv = y_ref[...]
        sg = _sigmoid(zv)
        silu = zv * sg
        u = yv * silu
        r = lax.rsqrt(jnp.mean(u * u, axis=-1, keepdims=True) + EPS)
        uh = u * r
        dov = do_ref[...]
        dw_ref[...] += jnp.sum(dov * uh, axis=0, keepdims=True)
        dyg = dov * w_ref[...]
        du = r * (dyg - uh * jnp.mean(dyg * uh, axis=-1, keepdims=True))
        dy_ref[...] = du * silu
        dz_ref[...] = (du * yv * (sg * (1.0 + zv * (1.0 - sg)))).astype(BF16)

    tile = pl.BlockSpec((tm, gw), lambda g, i: (i, g))
    vec = pl.BlockSpec((1, gw), lambda g, i: (0, g))
    z_cols = pl.BlockSpec((tm, gw), lambda g, i: (i, DPROJ_COLS["z"] // gw + g))
    return pl.pallas_call(
        body, name=name, grid=(SSM_N_GROUPS, t // tm),
        in_specs=[tile, tile, vec, tile, pl.BlockSpec(memory_space=pl.ANY)], out_specs=[tile, z_cols, vec],
        out_shape=[jax.ShapeDtypeStruct((t, d), F32), jax.ShapeDtypeStruct(dproj.shape, dproj.dtype),
                   jax.ShapeDtypeStruct((1, d), F32)],
        input_output_aliases={4: 1},
        compiler_params=_params("parallel", "arbitrary"),
    )(y, z, w, dout, dproj)


def _rope_tables(s):
    half = ATT_HEAD_DIM // 2
    inv = ROPE_THETA ** (-jnp.arange(half, dtype=F32) / half)
    ang = jnp.arange(s).astype(F32)[:, None] * inv[None, :]
    cos, sin = jnp.cos(ang), jnp.sin(ang)
    return jnp.concatenate([cos, cos], axis=-1), jnp.concatenate([-sin, sin], axis=-1)


ATT_TILE = 256


def _by_residue_spec(r, width):
    return pl.BlockSpec((None, r, ATT_TILE // r, width), lambda bi, i: (bi, 0, i, 0))


def _to_residues(tile, stage, r, store):
    if r == 1:
        store(0, tile)
        return
    stage[...] = tile
    for ri in range(r):
        store(ri, stage[pl.ds(ri, ATT_TILE // r, stride=r), :])


def _from_residues(load, stage, r):
    if r == 1:
        return load(0)
    for ri in range(r):
        stage[pl.ds(ri, ATT_TILE // r, stride=r), :] = load(ri)
    return stage[...]


def _rope_fwd(qkv, cosf, sinf, name):
    b, s, w = qkv.shape
    ts, d, gw = ATT_TILE, ATT_HEAD_DIM, ATT_OUT_DIM

    def body(x_ref, c_ref, s_ref, *rest):
        outs, stage = rest[:-1], rest[-1]
        cv, sv = c_ref[...], s_ref[...]
        for kind in range(3):
            for gi, r in enumerate(ATT_DILATIONS):
                for j in range(ATT_HEADS_PER_GROUP):
                    src = d * (kind * ATT_N_HEADS + gi * ATT_HEADS_PER_GROUP + j)
                    dst = slice(kind * gw + d * j, kind * gw + d * (j + 1))
                    tv = x_ref[:, src:src + d]
                    if kind < 2:
                        tv = tv * cv + pltpu.roll(tv, d // 2, 1) * sv

                    def store(ri, rows, o_ref=outs[gi], dst=dst):
                        o_ref[ri, :, dst] = rows.astype(BF16)

                    _to_residues(tv, stage, r, store)

    tab = pl.BlockSpec((ts, d), lambda bi, i: (i, 0))
    return pl.pallas_call(
        body, name=name, grid=(b, s // ts),
        in_specs=[pl.BlockSpec((None, ts, w), lambda bi, i: (bi, i, 0)), tab, tab],
        out_specs=[_by_residue_spec(r, 3 * gw) for r in ATT_DILATIONS],
        out_shape=[jax.ShapeDtypeStruct((b, r, s // r, 3 * gw), BF16) for r in ATT_DILATIONS],
        scratch_shapes=[pltpu.VMEM((ts, d), F32)],
        compiler_params=_params("parallel", "parallel"),
    )(qkv, cosf, sinf)


def _rope_bwd(dq, dk, dv, cosf, sinf, dproj, name):
    n_pat = len(ATT_DILATIONS)
    b, _, s, gw = dq[0].shape
    ts, d = ATT_TILE, ATT_HEAD_DIM

    def body(*refs):
        ins, (c_ref, s_ref, _, o_ref, stage) = refs[:3 * n_pat], refs[3 * n_pat:]
        cv, sv = c_ref[...], s_ref[...]
        for kind in range(3):
            for gi, r in enumerate(ATT_DILATIONS):
                src = ins[kind * n_pat + gi]
                for j in range(ATT_HEADS_PER_GROUP):
                    tv = _from_residues(lambda ri, src=src, j=j: src[ri, :, d * j:d * (j + 1)], stage, r)
                    if kind < 2:
                        tv = tv * cv + pltpu.roll(tv * sv, d // 2, 1)
                    lo = d * (kind * ATT_N_HEADS + gi * ATT_HEADS_PER_GROUP + j)
                    o_ref[:, lo:lo + d] = tv.astype(BF16)

    tab = pl.BlockSpec((ts, d), lambda bi, i: (i, 0))
    parts = [_by_residue_spec(r, gw) for r in ATT_DILATIONS]
    return pl.pallas_call(
        body, name=name, grid=(b, s // ts), in_specs=parts * 3 + [tab, tab, pl.BlockSpec(memory_space=pl.ANY)],
        out_specs=pl.BlockSpec((None, ts, ATT_QKV_DIM), lambda bi, i: (bi, i, DPROJ_COLS["qkv"] // ATT_QKV_DIM)),
        out_shape=jax.ShapeDtypeStruct(dproj.shape, dproj.dtype),
        input_output_aliases={3 * n_pat + 2: 0},
        scratch_shapes=[pltpu.VMEM((ts, d), F32)],
        compiler_params=_params("parallel", "parallel"),
    )(*dq, *dk, *dv, cosf, sinf, dproj)


ATT_SCALE = ATT_HEAD_DIM ** -0.5
ATT_STEP = 2 * ATT_BLOCK


def _att_spec(col):
    return pl.BlockSpec((None, None, ATT_STEP, ATT_OUT_DIM), lambda bi, ri, i: (bi, ri, i, col))


def _att_edge_spec(col, side, n_steps):
    def index(bi, ri, i):
        blk = 2 * i - 1 if side < 0 else 2 * i + 2
        return (bi, ri, jnp.clip(blk, 0, 2 * n_steps - 1), col)
    return pl.BlockSpec((None, None, ATT_BLOCK, ATT_OUT_DIM), index)


def _band_mask(shape, q_axis, has_prev):
    qi = lax.broadcasted_iota(jnp.int32, shape, q_axis)
    kj = lax.broadcasted_iota(jnp.int32, shape, 1 - q_axis)
    dist = qi + ATT_BLOCK - kj
    return (dist >= 0) & (dist <= ATT_BLOCK) & (has_prev | (kj >= ATT_BLOCK))


def _att_fwd(qkr, name):
    b, r, l, _ = qkr.shape
    nb = l // ATT_STEP
    d = ATT_HEAD_DIM

    def body(q_ref, kp_ref, k_ref, vp_ref, v_ref, o_ref, lse_ref):
        mask = _band_mask((ATT_STEP, ATT_BLOCK + ATT_STEP), 0, pl.program_id(2) > 0)
        heads = [slice(d * j, d * (j + 1)) for j in range(ATT_HEADS_PER_GROUP)]
        scores = [_nt(q_ref[:, sl], jnp.concatenate([kp_ref[:, sl], k_ref[:, sl]], axis=0)) for sl in heads]
        scores = [jnp.where(mask, sc * ATT_SCALE, NEG_INF) for sc in scores]
        tops = [jnp.max(sc, axis=-1, keepdims=True) for sc in scores]
        probs = [jnp.exp(sc - m) for sc, m in zip(scores, tops)]
        dens = [jnp.sum(pr, axis=-1, keepdims=True) for pr in probs]
        for sl, m, pr, den in zip(heads, tops, probs, dens):
            o_ref[:, sl] = _nn(pr / den, jnp.concatenate([vp_ref[:, sl], v_ref[:, sl]], axis=0))
            lse_ref[:, sl] = jnp.broadcast_to(m + jnp.log(den), (ATT_STEP, d))

    out_spec = _att_spec(0)
    return pl.pallas_call(
        body, name=name, grid=(b, r, nb),
        in_specs=[_att_spec(0), _att_edge_spec(1, -1, nb), _att_spec(1), _att_edge_spec(2, -1, nb), _att_spec(2)],
        out_specs=[out_spec, out_spec],
        out_shape=[jax.ShapeDtypeStruct((b, r, l, ATT_OUT_DIM), F32)] * 2,
        compiler_params=_params("parallel", "parallel", "parallel"),
    )(qkr, qkr, qkr, qkr, qkr)


def _att_merge(os_, lses, name):
    n_pat = len(os_)
    b, _, s, gw = os_[0].shape
    ts, d = ATT_TILE, ATT_HEAD_DIM

    def body(*refs):
        o_refs, l_refs = refs[:n_pat], refs[n_pat:2 * n_pat]
        att_ref, lse_outs, stage = refs[2 * n_pat], refs[2 * n_pat + 1:3 * n_pat + 1], refs[-1]
        for j in range(ATT_HEADS_PER_GROUP):
            sl = slice(d * j, d * (j + 1))
            ov = [_from_residues(lambda ri, g=g: o_refs[g][ri, :, sl], stage, r)
                  for g, r in enumerate(ATT_DILATIONS)]
            ls = [_from_residues(lambda ri, g=g: l_refs[g][ri, :, sl], stage, r)
                  for g, r in enumerate(ATT_DILATIONS)]
            m = functools.reduce(jnp.maximum, ls)
            es = [jnp.exp(lv - m) for lv in ls]
            tot = functools.reduce(lambda u, v: u + v, es)
            acc = (es[0] / tot) * ov[0]
            for g in range(1, n_pat):
                acc = acc + (es[g] / tot) * ov[g]
            att_ref[:, sl] = acc
            joint = m + jnp.log(tot)
            for g, r in enumerate(ATT_DILATIONS):
                def store(ri, rows, out=lse_outs[g]):
                    out[ri, :, sl] = rows
                _to_residues(joint, stage, r, store)

    parts = [_by_residue_spec(r, gw) for r in ATT_DILATIONS]
    return pl.pallas_call(
        body, name=name, grid=(b, s // ts), in_specs=parts * 2,
        out_specs=[pl.BlockSpec((None, ts, gw), lambda bi, i: (bi, i, 0))] + parts,
        out_shape=[jax.ShapeDtypeStruct((b, s, gw), F32)]
        + [jax.ShapeDtypeStruct((b, r, s // r, gw), F32) for r in ATT_DILATIONS],
        scratch_shapes=[pltpu.VMEM((ts, d), F32)],
        compiler_params=_params("parallel", "parallel"),
    )(*os_, *lses)


def _att_delta(att, datt, name):
    b, s, gw = att.shape
    ts, d = ATT_TILE, ATT_HEAD_DIM
    n_pat = len(ATT_DILATIONS)

    def body(a_ref, d_ref, *rest):
        do_outs, dl_outs, stage = rest[:n_pat], rest[n_pat:2 * n_pat], rest[-1]
        for j in range(ATT_HEADS_PER_GROUP):
            sl = slice(d * j, d * (j + 1))
            dv = d_ref[:, sl]
            delta = jnp.broadcast_to(jnp.sum(a_ref[:, sl] * dv, axis=-1, keepdims=True), (ts, d))
            for g, r in enumerate(ATT_DILATIONS):
                def store_do(ri, rows, out=do_outs[g]):
                    out[ri, :, sl] = rows.astype(BF16)

                def store_dl(ri, rows, out=dl_outs[g]):
                    out[ri, :, sl] = rows

                _to_residues(dv, stage, r, store_do)
                _to_residues(delta, stage, r, store_dl)

    row = pl.BlockSpec((None, ts, gw), lambda bi, i: (bi, i, 0))
    parts = [_by_residue_spec(r, gw) for r in ATT_DILATIONS]
    outs = pl.pallas_call(
        body, name=name, grid=(b, s // ts), in_specs=[row, row], out_specs=parts * 2,
        out_shape=[jax.ShapeDtypeStruct((b, r, s // r, gw), BF16) for r in ATT_DILATIONS]
        + [jax.ShapeDtypeStruct((b, r, s // r, gw), F32) for r in ATT_DILATIONS],
        scratch_shapes=[pltpu.VMEM((ts, d), F32)],
        compiler_params=_params("parallel", "parallel"),
    )(att, datt)
    return outs[:n_pat], outs[n_pat:]


def _att_bwd_q(qkr, datt, lse, delta, name):
    b, r, l, _ = qkr.shape
    nb = l // ATT_STEP
    d = ATT_HEAD_DIM

    def body(q_ref, kp_ref, k_ref, vp_ref, v_ref, do_ref, lse_ref, dl_ref, dq_ref):
        mask = _band_mask((ATT_STEP, ATT_BLOCK + ATT_STEP), 0, pl.program_id(2) > 0)
        heads = [slice(d * j, d * (j + 1)) for j in range(ATT_HEADS_PER_GROUP)]
        kcats = [jnp.concatenate([kp_ref[:, sl], k_ref[:, sl]], axis=0) for sl in heads]
        scores = [_nt(q_ref[:, sl], kcat) for sl, kcat in zip(heads, kcats)]
        dps = [_nt(do_ref[:, sl], jnp.concatenate([vp_ref[:, sl], v_ref[:, sl]], axis=0)) for sl in heads]
        probs = [jnp.exp(jnp.where(mask, sc * ATT_SCALE - lse_ref[:, sl.start:sl.start + 1], NEG_INF))
                 for sl, sc in zip(heads, scores)]
        dscs = [pr * (dp - dl_ref[:, sl.start:sl.start + 1]) for sl, pr, dp in zip(heads, probs, dps)]
        for sl, dsc, kcat in zip(heads, dscs, kcats):
            dq_ref[:, sl] = _nn(dsc, kcat) * ATT_SCALE

    tok = _att_spec(0)
    return pl.pallas_call(
        body, name=name, grid=(b, r, nb),
        in_specs=[_att_spec(0), _att_edge_spec(1, -1, nb), _att_spec(1), _att_edge_spec(2, -1, nb), _att_spec(2),
                  tok, tok, tok],
        out_specs=tok,
        out_shape=jax.ShapeDtypeStruct((b, r, l, ATT_OUT_DIM), F32),
        compiler_params=_params("parallel", "parallel", "parallel"),
    )(qkr, qkr, qkr, qkr, qkr, datt, lse, delta)


def _att_bwd_kv(qkr, datt, lse, delta, name):
    b, r, l, _ = qkr.shape
    nb = l // ATT_STEP
    d = ATT_HEAD_DIM

    def body(k_ref, v_ref, q_ref, qn_ref, do_ref, don_ref, lse_ref, lsen_ref, dl_ref, dln_ref, dk_ref, dv_ref):
        shape = (ATT_STEP, ATT_STEP + ATT_BLOCK)
        kj = lax.broadcasted_iota(jnp.int32, shape, 0)
        qi = lax.broadcasted_iota(jnp.int32, shape, 1)
        dist = qi - kj
        has_next = pl.program_id(2) < nb - 1
        mask = (dist >= 0) & (dist <= ATT_BLOCK) & (has_next | (qi < ATT_STEP))
        def per_query(own_ref, next_ref, sl):
            return jnp.tile(jnp.concatenate([own_ref[:, sl], next_ref[:, sl]], axis=0).T, (ATT_STEP // d, 1))

        heads = [slice(d * j, d * (j + 1)) for j in range(ATT_HEADS_PER_GROUP)]
        qcats = [jnp.concatenate([q_ref[:, sl], qn_ref[:, sl]], axis=0) for sl in heads]
        docats = [jnp.concatenate([do_ref[:, sl], don_ref[:, sl]], axis=0) for sl in heads]
        scores = [_nt(k_ref[:, sl], qcat) for sl, qcat in zip(heads, qcats)]
        dps = [_nt(v_ref[:, sl], docat) for sl, docat in zip(heads, docats)]
        probs = [jnp.exp(jnp.where(mask, sc * ATT_SCALE - per_query(lse_ref, lsen_ref, sl), NEG_INF))
                 for sl, sc in zip(heads, scores)]
        for sl, pr, docat in zip(heads, probs, docats):
            dv_ref[:, sl] = _nn(pr, docat)
        dscs = [pr * (dp - per_query(dl_ref, dln_ref, sl)) for sl, pr, dp in zip(heads, probs, dps)]
        for sl, dsc, qcat in zip(heads, dscs, qcats):
            dk_ref[:, sl] = _nn(dsc, qcat) * ATT_SCALE

    tok, tok_n = _att_spec(0), _att_edge_spec(0, 1, nb)
    return pl.pallas_call(
        body, name=name, grid=(b, r, nb),
        in_specs=[_att_spec(1), _att_spec(2), _att_spec(0), _att_edge_spec(0, 1, nb),
                  tok, tok_n, tok, tok_n, tok, tok_n],
        out_specs=[tok, tok],
        out_shape=[jax.ShapeDtypeStruct((b, r, l, ATT_OUT_DIM), F32)] * 2,
        compiler_params=_params("parallel", "parallel", "parallel"),
    )(qkr, qkr, qkr, qkr, datt, datt, lse, lse, delta, delta)


def _mix_fwd(gl, bg, ys, ya, name):
    t, d = ys.shape
    tm = _pick(t, (512, 256, 128))

    def body(gl_ref, bg_ref, ys_ref, ya_ref, o_ref):
        g0 = _sigmoid(gl_ref[:, :d] + bg_ref[:, :d])
        g1 = _sigmoid(gl_ref[:, d:] + bg_ref[:, d:])
        o_ref[...] = (g0 * ys_ref[...] + g1 * ya_ref[...]).astype(BF16)

    row = pl.BlockSpec((tm, d), lambda i: (i, 0))
    return pl.pallas_call(
        body, name=name, grid=(t // tm,),
        in_specs=[pl.BlockSpec((tm, 2 * d), lambda i: (i, 0)), pl.BlockSpec((1, 2 * d), lambda i: (0, 0)), row, row],
        out_specs=row, out_shape=jax.ShapeDtypeStruct((t, d), BF16),
        compiler_params=_params("parallel"),
    )(gl, bg, ys, ya)


def _mix_bwd(gl, bg, ys, ya, dmixed, name):
    t, d = ys.shape
    tm = _pick(t, (512, 256, 128))

    def body(gl_ref, bg_ref, ys_ref, ya_ref, dm_ref, dys_ref, dya_ref, dgl_ref, dbg_ref):
        @pl.when(pl.program_id(0) == 0)
        def _():
            dbg_ref[...] = jnp.zeros_like(dbg_ref)

        dm = dm_ref[...]
        g0 = _sigmoid(gl_ref[:, :d] + bg_ref[:, :d])
        g1 = _sigmoid(gl_ref[:, d:] + bg_ref[:, d:])
        dys_ref[...] = (dm * g0).astype(BF16)
        dya_ref[...] = (dm * g1).astype(BF16)
        d0 = dm * ys_ref[...] * (g0 * (1.0 - g0))
        d1 = dm * ya_ref[...] * (g1 * (1.0 - g1))
        dgl_ref[:, :d] = d0.astype(BF16)
        dgl_ref[:, d:] = d1.astype(BF16)
        dbg_ref[:, :d] += jnp.sum(d0, axis=0, keepdims=True)
        dbg_ref[:, d:] += jnp.sum(d1, axis=0, keepdims=True)

    row = pl.BlockSpec((tm, d), lambda i: (i, 0))
    wide = pl.BlockSpec((tm, 2 * d), lambda i: (i, 0))
    vec = pl.BlockSpec((1, 2 * d), lambda i: (0, 0))
    gate_cols = pl.BlockSpec((tm, 2 * d), lambda i: (i, DPROJ_COLS["gate"] // (2 * d)))
    return pl.pallas_call(
        body, name=name, grid=(t // tm,),
        in_specs=[wide, vec, row, row, row], out_specs=[row, row, gate_cols, vec],
        out_shape=[jax.ShapeDtypeStruct((t, d), BF16), jax.ShapeDtypeStruct((t, d), BF16),
                   jax.ShapeDtypeStruct((t, DPROJ_WIDTH), BF16), jax.ShapeDtypeStruct((1, 2 * d), F32)],
        compiler_params=_params("arbitrary"),
    )(gl, bg, ys, ya, dmixed)


def _swiglu_fwd(gt, up, name):
    t, f = gt.shape
    tm = _pick(t, (512, 256, 128))

    def body(g_ref, u_ref, o_ref):
        gv = g_ref[...]
        o_ref[...] = ((gv * _sigmoid(gv)) * u_ref[...]).astype(BF16)

    row = pl.BlockSpec((tm, f), lambda i: (i, 0))
    return pl.pallas_call(
        body, name=name, grid=(t // tm,), in_specs=[row, row], out_specs=row,
        out_shape=jax.ShapeDtypeStruct((t, f), BF16), compiler_params=_params("parallel"),
    )(gt, up)


def _swiglu_bwd(gt, up, dact, name):
    t, f = gt.shape
    tm = _pick(t, (512, 256, 128))

    def body(g_ref, u_ref, d_ref, dg_ref, du_ref):
        gv = g_ref[...]
        dv = d_ref[...]
        sg = _sigmoid(gv)
        dg_ref[...] = (dv * u_ref[...] * (sg * (1.0 + gv * (1.0 - sg)))).astype(BF16)
        du_ref[...] = (dv * (gv * sg)).astype(BF16)

    row = pl.BlockSpec((tm, f), lambda i: (i, 0))
    return pl.pallas_call(
        body, name=name, grid=(t // tm,), in_specs=[row, row, row], out_specs=[row, row],
        out_shape=[jax.ShapeDtypeStruct((t, f), BF16)] * 2, compiler_params=_params("parallel"),
    )(gt, up, dact)


def _peer(k):
    x, y, c = lax.axis_index("x"), lax.axis_index("y"), lax.axis_index("c")
    px, py, pc = x ^ ((k >> 2) & 1), y ^ ((k >> 1) & 1), c ^ (k & 1)
    return (px, py, pc), 4 * px + 2 * py + pc


def _my_index():
    return 4 * lax.axis_index("x") + 2 * lax.axis_index("y") + lax.axis_index("c")


def _all_gather(parts, name):
    n_parts = len(parts)

    def body(*refs):
        ins, outs = refs[:n_parts], refs[n_parts:2 * n_parts]
        send_sems, recv_sems, local_sems = refs[2 * n_parts:]
        here, me = _peer(0)
        sibling, sib_idx = _peer(1)
        chips = [_peer(2 * q) for q in range(1, N_CHIPS)]

        def copy(i, k, block, to, src=None):
            return pltpu.make_async_remote_copy(
                src_ref=outs[i].at[block] if src is None else src, dst_ref=outs[i].at[block],
                send_sem=send_sems.at[i * (N_DEV - 1) + k], recv_sem=recv_sems.at[i * (N_DEV - 1) + k],
                device_id=to, device_id_type=MESH)

        local = [pltpu.make_async_copy(ins[i], outs[i].at[me], local_sems.at[i]) for i in range(n_parts)]
        for cp in local:
            cp.start()
        sends = []
        for i in range(n_parts):
            sends.append(copy(i, 0, me, sibling, src=ins[i]))
            sends += [copy(i, q, me, chip, src=ins[i]) for q, (chip, _) in enumerate(chips, start=1)]
        for cp in sends:
            cp.start()
        for q, (chip, chip_idx) in enumerate(chips, start=1):
            for i in range(n_parts):
                copy(i, q, chip_idx, here).wait_recv()
                fwd = copy(i, N_CHIPS - 1 + q, chip_idx, sibling)
                fwd.start()
                sends.append(fwd)
        for i in range(n_parts):
            copy(i, 0, sib_idx, here).wait_recv()
        for q, (_, chip_idx) in enumerate(chips, start=1):
            for i in range(n_parts):
                copy(i, N_CHIPS - 1 + q, chip_idx ^ 1, here).wait_recv()
        for cp in sends:
            cp.wait_send()
        for cp in local:
            cp.wait()

    hbm = pl.BlockSpec(memory_space=pl.ANY)
    return pl.pallas_call(
        body, name=name, in_specs=[hbm] * n_parts, out_specs=[hbm] * n_parts,
        out_shape=[jax.ShapeDtypeStruct((N_DEV,) + p_.shape, p_.dtype) for p_ in parts],
        scratch_shapes=[pltpu.SemaphoreType.DMA((n_parts * (N_DEV - 1),)),
                        pltpu.SemaphoreType.DMA((n_parts * (N_DEV - 1),)),
                        pltpu.SemaphoreType.DMA((n_parts,))],
        compiler_params=pltpu.CompilerParams(has_side_effects=True),
    )(*parts)


HBM_SPEC = pl.BlockSpec(memory_space=pltpu.HBM)
SEM_SPEC = pl.BlockSpec(memory_space=pltpu.SEMAPHORE)
DATAFLOW = pltpu.SideEffectType.DATAFLOW_SIDE_EFFECTING


def _gather_start(block, after, name):
    per_peer = block.ndim == 3

    def body(v_ref, land_ref, after_ref, send_sems, recv_sems, v_thru, land_thru, token):
        me = _my_index()
        for k in range(1, N_DEV):
            peer, pidx = _peer(k)
            pltpu.make_async_remote_copy(
                src_ref=v_ref.at[pidx] if per_peer else v_ref, dst_ref=land_ref.at[me],
                send_sem=send_sems.at[k - 1], recv_sem=recv_sems.at[k - 1],
                device_id=peer, device_id_type=MESH).start()
        token[...] = jnp.zeros_like(token)

    land_shape = (N_DEV,) + block.shape[-2:]
    return pl.pallas_call(
        body, name=name,
        out_shape=(pltpu.SemaphoreType.DMA((N_DEV - 1,)), pltpu.SemaphoreType.DMA((N_DEV - 1,)),
                   pltpu.HBM(block.shape, block.dtype), pltpu.HBM(land_shape, block.dtype),
                   jax.ShapeDtypeStruct((8, LANES), F32)),
        in_specs=(HBM_SPEC, HBM_SPEC, pl.BlockSpec(memory_space=pl.ANY)),
        out_specs=(SEM_SPEC, SEM_SPEC, HBM_SPEC, HBM_SPEC, pl.BlockSpec(memory_space=pltpu.VMEM)),
        input_output_aliases={0: 2, 1: 3},
        compiler_params=pltpu.CompilerParams(has_side_effects=DATAFLOW),
    )(pltpu.with_memory_space_constraint(block, pltpu.HBM),
      pltpu.with_memory_space_constraint(lax.empty(land_shape, block.dtype), pltpu.HBM), after)


def _gather_wait(send_sems, recv_sems, block, landing, after, name):
    per_peer = block.ndim == 3

    def body(v_ref, land_ref, send_sems, recv_sems, after_ref, v_dead, got_ref):
        for k in range(1, N_DEV):
            peer, pidx = _peer(k)
            copy = pltpu.make_async_remote_copy(
                src_ref=v_ref.at[pidx] if per_peer else v_ref, dst_ref=land_ref.at[pidx],
                send_sem=send_sems.at[k - 1], recv_sem=recv_sems.at[k - 1],
                device_id=peer, device_id_type=MESH)
            copy.wait_send()
            copy.wait_recv()

    return pl.pallas_call(
        body, name=name,
        out_shape=(pltpu.HBM(block.shape, block.dtype), pltpu.HBM(landing.shape, landing.dtype)),
        in_specs=(HBM_SPEC, HBM_SPEC, SEM_SPEC, SEM_SPEC, pl.BlockSpec(memory_space=pl.ANY)),
        out_specs=(HBM_SPEC, HBM_SPEC), input_output_aliases={0: 0, 1: 1},
        compiler_params=pltpu.CompilerParams(has_side_effects=DATAFLOW),
    )(block, landing, send_sems, recv_sems, after)[1]


TILE_ELEMS = 1024 * 1024


def _shared_exchange(shared, name):
    def body(sh_ref, gsh_ref, send_sems, recv_sems, local_sem):
        me = _my_index()
        local = pltpu.make_async_copy(sh_ref, gsh_ref.at[me], local_sem)
        local.start()
        sends = []
        for k in range(1, N_DEV):
            peer, _ = _peer(k)
            cp = pltpu.make_async_remote_copy(
                src_ref=sh_ref, dst_ref=gsh_ref.at[me], send_sem=send_sems.at[k - 1],
                recv_sem=recv_sems.at[k - 1], device_id=peer, device_id_type=MESH)
            cp.start()
            sends.append(cp)
        for k in range(1, N_DEV):
            peer, pidx = _peer(k)
            pltpu.make_async_remote_copy(
                src_ref=sh_ref, dst_ref=gsh_ref.at[pidx], send_sem=send_sems.at[k - 1],
                recv_sem=recv_sems.at[k - 1], device_id=peer, device_id_type=MESH).wait_recv()
        for cp in sends:
            cp.wait_send()
        local.wait()

    hbm = pl.BlockSpec(memory_space=pl.ANY)
    return pl.pallas_call(
        body, name=name, in_specs=[hbm], out_specs=hbm,
        out_shape=jax.ShapeDtypeStruct((N_DEV,) + shared.shape, shared.dtype),
        scratch_shapes=[pltpu.SemaphoreType.DMA((N_DEV - 1,)), pltpu.SemaphoreType.DMA((N_DEV - 1,)),
                        pltpu.SemaphoreType.DMA],
        compiler_params=pltpu.CompilerParams(has_side_effects=True),
    )(shared)


def _adamw(parts, w, m, v, name, row0=0):
    n_parts, rows, lanes = parts.shape
    tr = rows if rows * lanes <= TILE_ELEMS // 2 else _tile_rows(math.gcd(rows, row0), TILE_ELEMS // 4 // lanes, 8)
    c1 = 1.0 - ADAM_B1 ** ADAM_STEP
    c2 = 1.0 - ADAM_B2 ** ADAM_STEP

    def body(p_ref, w_ref, m_ref, v_ref, g_ref, d_ref, nm_ref, nv_ref):
        g = p_ref[0].astype(F32)
        for j in range(1, n_parts):
            g = g + p_ref[j].astype(F32)
        nm = ADAM_B1 * m_ref[...] + (1.0 - ADAM_B1) * g
        nv = ADAM_B2 * v_ref[...] + (1.0 - ADAM_B2) * (g * g)
        g_ref[...] = g
        nm_ref[...] = nm
        nv_ref[...] = nv
        d_ref[...] = -ADAM_LR * ((nm / c1) / (jnp.sqrt(nv / c2) + ADAM_EPS) + ADAM_WD * w_ref[...])

    row = pl.BlockSpec((tr, lanes), lambda i: (i, 0))
    state = pl.BlockSpec((tr, lanes), lambda i: (row0 // tr + i, 0))
    return pl.pallas_call(
        body, name=name, grid=(rows // tr,),
        in_specs=[pl.BlockSpec((n_parts, tr, lanes), lambda i: (0, i, 0)), state, state, state],
        out_specs=[row] * 4, out_shape=[jax.ShapeDtypeStruct((rows, lanes), F32)] * 4,
        compiler_params=_params("parallel"),
    )(parts, w, m, v)


MATRIX_SHARDS = (
    ("w_in", (D_MODEL, IN_PROJ_DIM // N_DEV), True),
    ("w_ssm_out", (SSM_D_INNER // N_DEV, D_MODEL), False),
    ("w_att_out", (ATT_OUT_DIM, D_MODEL // N_DEV), True),
    ("w_mix_out", (D_MODEL // N_DEV, D_MODEL), False),
    ("w_ffn_gate", (D_MODEL, D_FF // N_DEV), True),
    ("w_ffn_up", (D_MODEL, D_FF // N_DEV), True),
    ("w_ffn_down", (D_FF // N_DEV, D_MODEL), False),
)
CONV_SHARD = ("conv_w", (SSM_CONV, SSM_CONV_DIM // N_DEV), True)
SHARDED = MATRIX_SHARDS + (CONV_SHARD,)
REPLICATED = (("norm_mix", D_MODEL), ("b_gate", 2 * D_MODEL), ("conv_b", SSM_CONV_DIM), ("dt_bias", SSM_N_HEADS),
              ("a_log", SSM_N_HEADS), ("d_skip", SSM_N_HEADS), ("ssm_norm", SSM_D_INNER), ("norm_ffn", D_MODEL),
              ("norm_final", D_MODEL))


def _round_up(n, mult):
    return -(-n // mult) * mult


def _pack_rows(flat, row_mult):
    rows = _round_up(-(-flat.shape[0] // LANES), row_mult)
    return jnp.pad(flat, (0, rows * LANES - flat.shape[0])).reshape(rows, LANES)


def _stacking(specs):
    return tuple((name, (shape[1], shape[0]) if by_cols else shape, by_cols) for name, shape, by_cols in specs)


def _to_stacking(vals, specs):
    return {name: (vals[name].T if by_cols else vals[name]) for name, _, by_cols in specs}


STACK_WIDTH = D_MODEL
STACK_ALIGN = 16
STACK_ORDER = ("w_ssm_out", "w_mix_out", "w_ffn_gate", "w_ffn_up", "w_ffn_down", "w_att_out", "conv_w", "w_in")
GATHER_LATER = STACK_ORDER[:-1]
REDUCE_EARLY = STACK_ORDER[:5]
REDUCE_LATE = STACK_ORDER[5:]


def _stack_layout():
    shapes = {name: shape for name, shape, _ in _stacking(SHARDED)}
    layout, off = {}, 0
    for name in STACK_ORDER:
        r, c = shapes[name]
        rows = r if c == STACK_WIDTH else _round_up(-(-(r * c) // STACK_WIDTH), STACK_ALIGN)
        layout[name] = (off, rows, (r, c))
        off = _round_up(off + rows, STACK_ALIGN)
    return layout, _round_up(off, 1024)


def _to_stack_rows(v, rows):
    if v.shape[-1] == STACK_WIDTH:
        return v
    lead = v.shape[:-2]
    flat = v.reshape(lead + (-1,))
    flat = jnp.pad(flat, [(0, 0)] * len(lead) + [(0, rows * STACK_WIDTH - flat.shape[-1])])
    return flat.reshape(lead + (rows, STACK_WIDTH))


def _from_stack_rows(block, shape):
    r, c = shape
    if c == STACK_WIDTH:
        return block
    lead = block.shape[:-2]
    return block.reshape(lead + (-1,))[..., :r * c].reshape(lead + (r, c))


def _stack(vals, dtype, skip=(), names=STACK_ORDER):
    layout, total = _stack_layout()
    order = names
    after = STACK_ORDER.index(order[-1]) + 1
    if after < len(STACK_ORDER):
        total = layout[STACK_ORDER[after]][0]
    lead = next(iter(vals.values())).shape[:-2]
    pieces = []
    for i, name in enumerate(order):
        off, rows, _ = layout[name]
        until = layout[order[i + 1]][0] if i + 1 < len(order) else total
        piece = jnp.zeros(lead + (rows, STACK_WIDTH), dtype) if name in skip else _to_stack_rows(vals[name], rows)
        pieces.append(jnp.pad(piece.astype(dtype), [(0, 0)] * len(lead) + [(0, until - off - rows), (0, 0)]))
    return jnp.concatenate(pieces, axis=-2)


def _unstack(stacked, names):
    layout, _ = _stack_layout()
    row0 = layout[names[0]][0]
    return {name: _from_stack_rows(stacked[..., layout[name][0] - row0:layout[name][0] - row0 + layout[name][1], :],
                                   layout[name][2]) for name in names}


W_IN_SHARD_ROWS = IN_PROJ_DIM // N_DEV


def _w_in_row_moves():
    moves, orig = [], 0
    for name, size in IN_SPLIT:
        for j in range(N_DEV):
            lo, hi = max(orig, W_IN_SHARD_ROWS * j), min(orig + size, W_IN_SHARD_ROWS * (j + 1))
            if lo < hi:
                moves.append((j, lo - W_IN_SHARD_ROWS * j, DPROJ_COLS[name] + lo - orig, hi - lo))
        orig += size
    return moves


def _w_in_from_shards(shards, name):
    total, base = shards.shape[1], 0
    pad_lo, pad_hi = DPROJ_COLS["dt"] + _round_up(SSM_N_HEADS, STACK_ALIGN), DPROJ_COLS["dt"] + DPROJ_DT_WIDTH

    def body(x_ref, o_ref):
        o_ref[pad_lo:pad_hi, :] = jnp.zeros((pad_hi - pad_lo, LANES), x_ref.dtype)
        for j, r, at, n in _w_in_row_moves():
            o_ref[at:at + n, :] = x_ref[j, base + r:base + r + n, :]

    return pl.pallas_call(
        body, name=name, grid=(STACK_WIDTH // LANES,),
        in_specs=[pl.BlockSpec((N_DEV, total, LANES), lambda c: (0, 0, c))],
        out_specs=pl.BlockSpec((DPROJ_WIDTH, LANES), lambda c: (0, c)),
        out_shape=jax.ShapeDtypeStruct((DPROJ_WIDTH, STACK_WIDTH), shards.dtype),
        compiler_params=_params("parallel"),
    )(shards)


def _w_in_to_shards(dw_all, head, name):
    layout, total = _stack_layout()
    total -= layout[REDUCE_LATE[0]][0]
    base = head.shape[1]
    end = base + W_IN_SHARD_ROWS

    def body(x_ref, h_ref, o_ref):
        o_ref[:, 0:base, :] = h_ref[...]
        for j, r, at, n in _w_in_row_moves():
            o_ref[j, base + r:base + r + n, :] = x_ref[at:at + n, :]
        o_ref[:, end:total, :] = jnp.zeros((N_DEV, total - end, LANES), o_ref.dtype)

    return pl.pallas_call(
        body, name=name, grid=(STACK_WIDTH // LANES,),
        in_specs=[pl.BlockSpec((DPROJ_WIDTH, LANES), lambda c: (0, c)),
                  pl.BlockSpec((N_DEV, base, LANES), lambda c: (0, 0, c))],
        out_specs=pl.BlockSpec((N_DEV, total, LANES), lambda c: (0, 0, c)),
        out_shape=jax.ShapeDtypeStruct((N_DEV, total, STACK_WIDTH), dw_all.dtype),
        compiler_params=_params("parallel"),
    )(dw_all, head)


REPLICATED_ROWS = sum(-(-size // LANES) for _, size in REPLICATED)
LOSS_ROW = REPLICATED_ROWS


def _pack_replicated(vals):
    rows = []
    for name, size in REPLICATED:
        v = vals[name].reshape(-1).astype(F32)
        rows.append(jnp.pad(v, (0, _round_up(size, LANES) - size)))
    return _pack_rows(jnp.concatenate(rows), 8)


def _unpack_replicated(packed, shapes):
    flat = packed.reshape(-1)
    out, off = {}, 0
    for name, size in REPLICATED:
        out[name] = flat[off:off + size].reshape(shapes[name])
        off += _round_up(size, LANES)
    return out


def _lane_row(v):
    v = v.reshape(-1).astype(F32)
    return jnp.pad(v, (0, LANES - v.shape[0])).reshape(1, LANES)


IN_SPLIT = (("z", SSM_D_INNER), ("xbc", SSM_CONV_DIM), ("dt", SSM_N_HEADS), ("qkv", ATT_QKV_DIM), ("gate", 2 * D_MODEL))


def kernel(x, norm_mix, w_in, b_gate, conv_w, conv_b, dt_bias, a_log, d_skip, ssm_norm, w_ssm_out, w_att_out, w_mix_out, norm_ffn, w_ffn_gate, w_ffn_up, w_ffn_down, norm_final, loss_target, m_norm_mix, m_w_in, m_b_gate, m_conv_w, m_conv_b, m_dt_bias, m_a_log, m_d_skip, m_ssm_norm, m_w_ssm_out, m_w_att_out, m_w_mix_out, m_norm_ffn, m_w_ffn_gate, m_w_ffn_up, m_w_ffn_down, m_norm_final, v_norm_mix, v_w_in, v_b_gate, v_conv_w, v_conv_b, v_dt_bias, v_a_log, v_d_skip, v_ssm_norm, v_w_ssm_out, v_w_att_out, v_w_mix_out, v_norm_ffn, v_w_ffn_gate, v_w_ffn_up, v_w_ffn_down, v_norm_final):
    given = dict(locals())
    weights = {name: given[name][0] for name, _, _ in SHARDED}
    b, s, d = x.shape
    t = b * s

    stacking = _to_stacking(weights, SHARDED)
    conv_shape = dict((name, shape) for name, shape, _ in _stacking(SHARDED))["conv_w"]
    w_in_local = jnp.pad(stacking["w_in"].astype(BF16), ((0, -W_IN_SHARD_ROWS % STACK_ALIGN), (0, 0)))
    conv_local = _pack_rows(stacking["conv_w"].reshape(-1), 8)
    w_in_shards, conv_all = _all_gather([w_in_local, conv_local], "w_in_all_gather")
    head_local = _stack(stacking, BF16, skip=("conv_w",), names=GATHER_LATER)
    in_flight = _gather_start(head_local, conv_all, "weights_gather_start")
    w_in_all = _w_in_from_shards(w_in_shards, "w_in_from_shards")
    w_sec = {name: w_in_all[DPROJ_COLS[name]:DPROJ_COLS[name] + _round_up(size, LANES)] for name, size in IN_SPLIT}
    conv_size = conv_shape[0] * conv_shape[1]
    conv_taps = conv_all.reshape(N_DEV, -1)[:, :conv_size].reshape(N_DEV * conv_shape[0], conv_shape[1]).T

    g_mix, g_ffn, g_fin = norm_mix.reshape(1, d), norm_ffn.reshape(1, d), norm_final.reshape(1, d)
    g_mix = g_mix + in_flight[4][:1, :1]
    bg_row = b_gate.reshape(1, 2 * d)
    convb_row = conv_b.reshape(1, SSM_CONV_DIM)
    ssmn_row = ssm_norm.reshape(1, SSM_D_INNER)
    dtb_row, alog_row = _lane_row(dt_bias), _lane_row(a_log)
    cosf, sinf = _rope_tables(s)

    x2d = x.reshape(t, d)
    h1 = _rmsnorm_fwd(x2d, g_mix, "norm_mix_fwd")
    proj = {name: _mm(h1, w_sec[name], mode="nt", name="in_proj_" + name) for name, _ in IN_SPLIT}
    xbc3 = proj["xbc"].reshape(b, s, SSM_CONV_DIM)
    xc = _conv_fwd(xbc3, conv_taps, convb_row, "conv_fwd")
    dtr3 = proj["dt"].reshape(b, s, DT_PAD)
    to_channels, to_heads = _head_masks()
    dskx = jnp.repeat(d_skip.reshape(-1).astype(F32), SSM_HEAD_DIM).reshape(1, SSM_D_INNER)
    y_ssd, h_states = _ssd_fwd(xc, dtr3, dtb_row, alog_row, dskx, to_channels, "ssd_fwd")
    y_ssd2 = y_ssd.reshape(t, SSM_D_INNER)
    ynorm = _gate_norm_fwd(y_ssd2, proj["z"], ssmn_row, "ssd_gate_norm_fwd")
    landed = _gather_wait(*in_flight[:4], ynorm, "weights_gather_wait")
    head_all = lax.dynamic_update_slice(landed, head_local[None], (_my_index(), 0, 0))
    full = {name: v.reshape((-1,) + v.shape[2:]) for name, v in _unstack(head_all, STACK_ORDER[:-2]).items()}
    y_ssm = _mm(ynorm, full["w_ssm_out"], mode="nn", name="ssm_out_proj")

    qkv3 = proj["qkv"].reshape(b, s, ATT_QKV_DIM)
    qk_parts = _rope_fwd(qkv3, cosf, sinf, "rope_fwd")
    att_parts = [_att_fwd(qk_parts[gi], "att_fwd_%d" % r) for gi, r in enumerate(ATT_DILATIONS)]
    att, *lse_parts = _att_merge([o for o, _ in att_parts], [l_ for _, l_ in att_parts], "att_merge")
    att2 = att.reshape(t, ATT_OUT_DIM)
    y_att = _mm(att2, full["w_att_out"], mode="nt", name="att_out_proj")

    mixed = _mix_fwd(proj["gate"], bg_row, y_ssm, y_att, "mix_fwd")
    x2 = _mm(mixed, full["w_mix_out"], mode="nn", name="mix_out_proj", add=x2d)
    h2 = _rmsnorm_fwd(x2, g_ffn, "norm_ffn_fwd")
    gt = _mm(h2, full["w_ffn_gate"], mode="nt", name="ffn_gate_proj")
    up = _mm(h2, full["w_ffn_up"], mode="nt", name="ffn_up_proj")
    act = _swiglu_fwd(gt, up, "swiglu_fwd")
    x3 = _mm(act, full["w_ffn_down"], mode="nn", name="ffn_down_proj", add=x2)

    loss_row, dx3, dg_fin, dx3b = _loss_head(x3, g_fin, loss_target.reshape(t, d), "loss_head")
    grads = {}
    dact = _mm(dx3b, full["w_ffn_down"], mode="nt", name="ffn_down_dx")
    grads["w_ffn_down"] = _mm(act, dx3b, mode="tn", name="ffn_down_dw", out_dtype=BF16)
    dgt, dup = _swiglu_bwd(gt, up, dact, "swiglu_bwd")
    grads["w_ffn_gate"] = _mm(dgt, h2, mode="tn", name="ffn_gate_dw", out_dtype=BF16)
    grads["w_ffn_up"] = _mm(dup, h2, mode="tn", name="ffn_up_dw", out_dtype=BF16)
    dh2 = _mm(dgt, full["w_ffn_gate"], mode="nn", name="ffn_gate_dx")
    dh2 = _mm(dup, full["w_ffn_up"], mode="nn", name="ffn_up_dx", add=dh2)
    dx2, dg_ffn, dx2b = _rmsnorm_bwd(x2, g_ffn, dh2, dx3, "norm_ffn_bwd", with_bf16=True)

    dmixed = _mm(dx2b, full["w_mix_out"], mode="nt", name="mix_out_dx")
    grads["w_mix_out"] = _mm(mixed, dx2b, mode="tn", name="mix_out_dw", out_dtype=BF16)
    dys, dya, dproj, dbg = _mix_bwd(proj["gate"], bg_row, y_ssm, y_att, dmixed, "mix_bwd")

    grads["w_ssm_out"] = _mm(ynorm, dys, mode="tn", name="ssm_out_dw", out_dtype=BF16)
    early = _stack({name: grads[name].reshape((N_DEV, -1, STACK_WIDTH)) for name in REDUCE_EARLY}, BF16,
                   names=REDUCE_EARLY)
    early_flight = _gather_start(early, dys, "grads_scatter_start")
    ssmn_row = ssmn_row + early_flight[4][:1, :1]
    dynorm = _mm(dys, full["w_ssm_out"], mode="nt", name="ssm_out_dx")
    dy_ssd, dproj, dssmn = _gate_norm_bwd(y_ssd2, proj["z"], ssmn_row, dynorm, dproj, "ssd_gate_norm_bwd")
    dxc, dproj, dalog, ddsk, ddtb = _ssd_bwd(xc, dtr3, dy_ssd.reshape(b, s, SSM_D_INNER), h_states, dtb_row, alog_row,
                                             dskx, to_channels, to_heads, dproj.reshape(b, s, DPROJ_WIDTH), "ssd_bwd")
    dproj, dconvw, dconvb = _conv_bwd(xbc3, dxc, conv_taps, convb_row, dproj, "conv_bwd")
    grads["conv_w"] = dconvw.T.astype(BF16)

    grads["w_att_out"] = _mm(dya, att2, mode="tn", name="att_out_dw", out_dtype=BF16)
    datt = _mm(dya, full["w_att_out"], mode="nn", name="att_out_dx").reshape(b, s, ATT_OUT_DIM)
    do_parts, dl_parts = _att_delta(att, datt, "att_delta")
    dqs, dks, dvs = [], [], []
    for gi, r in enumerate(ATT_DILATIONS):
        operands = (qk_parts[gi], do_parts[gi], lse_parts[gi], dl_parts[gi])
        dqs.append(_att_bwd_q(*operands, "att_bwd_q_%d" % r))
        dk_g, dv_g = _att_bwd_kv(*operands, "att_bwd_kv_%d" % r)
        dks.append(dk_g)
        dvs.append(dv_g)
    dproj = _rope_bwd(dqs, dks, dvs, cosf, sinf, dproj, "rope_bwd").reshape(t, DPROJ_WIDTH)

    dw_all = _mm(dproj, h1, mode="tn", name="in_proj_dw", out_dtype=BF16)
    head = _stack({name: grads[name].reshape((N_DEV, -1, grads[name].shape[-1])) for name in REDUCE_LATE[:-1]}, BF16,
                  names=REDUCE_LATE[:-1])
    late = _w_in_to_shards(dw_all, head, "grad_stacks")
    late_flight = _gather_start(late, dw_all, "grads_late_scatter_start")
    dh1 = _mm(dproj, w_in_all, mode="nn", name="in_proj_dx", after=late_flight[4])
    grad_x, dg_mix = _rmsnorm_bwd(x2d, g_mix, dh1, dx2, "norm_mix_bwd")

    small = {"norm_mix": dg_mix, "b_gate": dbg, "conv_b": dconvb, "dt_bias": ddtb[:, :SSM_N_HEADS],
             "a_log": dalog[:, :SSM_N_HEADS], "d_skip": ddsk[:, :SSM_N_HEADS], "ssm_norm": dssmn,
             "norm_ffn": dg_ffn, "norm_final": dg_fin}
    shared = _pack_replicated(small)
    shared = shared.at[LOSS_ROW, 0].set(loss_row[0, 0])
    got_small = _shared_exchange(shared, "shared_grads_exchange")

    def packed(prefix):
        vals = _to_stacking({name: given[prefix + name][0] for name, _, _ in SHARDED}, SHARDED)
        rep = {name: given[prefix + name] for name, _ in REPLICATED}
        return _stack(vals, F32), _pack_replicated(rep)

    (w_big, w_small), (m_big, m_small), (v_big, v_small) = packed(""), packed("m_"), packed("v_")
    me = _my_index()

    def arrived(flight, slabs, name):
        landed = _gather_wait(*flight[:4], got_small, name)
        mine = lax.dynamic_slice(slabs, (me, 0, 0), (1,) + slabs.shape[1:])
        return lax.dynamic_update_slice(landed, mine, (me, 0, 0))

    big_early = _adamw(arrived(early_flight, early, "grads_scatter_wait"), w_big, m_big, v_big, "adamw_early")
    big_late = _adamw(arrived(late_flight, late, "grads_late_scatter_wait"), w_big, m_big, v_big, "adamw_late",
                      row0=early.shape[1])
    sml = _adamw(got_small, w_small, m_small, v_small, "adamw_replicated")

    outs = [sml[0][LOSS_ROW, 0], grad_x.reshape(b, s, d)]
    rep_shapes = {name: given[name].shape for name, _ in REPLICATED}
    order = ["norm_mix", "w_in", "b_gate", "conv_w", "conv_b", "dt_bias", "a_log", "d_skip", "ssm_norm", "w_ssm_out",
             "w_att_out", "w_mix_out", "norm_ffn", "w_ffn_gate", "w_ffn_up", "w_ffn_down", "norm_final"]
    for early_k, late_k, sml_k in zip(big_early, big_late, sml):
        stacks = dict(_unstack(early_k, REDUCE_EARLY), **_unstack(late_k, REDUCE_LATE))
        sharded = _to_stacking(stacks, SHARDED)
        rep = _unpack_replicated(sml_k, rep_shapes)
        for name in order:
            outs.append(sharded[name][None] if name in sharded else rep[name])
    return tuple(outs)
```

```python
import functools
import math

import jax
import jax.numpy as jnp
from jax import lax
from jax.experimental import pallas as pl
from jax.experimental.pallas import tpu as pltpu

F32 = jnp.float32
BF16 = jnp.bfloat16

N_DEV = 8
N_CHIPS = 4
D_MODEL = 1024
SSM_D_INNER = 2048
SSM_HEAD_DIM = 64
SSM_N_HEADS = 32
SSM_N_GROUPS = 4
SSM_HEADS_PER_GROUP = SSM_N_HEADS // SSM_N_GROUPS
SSM_D_STATE = 128
SSM_CONV = 4
SSM_CHUNK = 128
SSM_CONV_DIM = 3072
ATT_HEAD_DIM = 128
ATT_HEADS_PER_GROUP = 4
ATT_DILATIONS = (1, 4, 16)
ATT_N_HEADS = 12
ATT_QKV_DIM = 4608
ATT_OUT_DIM = 512
ATT_BLOCK = 128
ROPE_THETA = 10000.0
D_FF = 2816
IN_PROJ_DIM = 11808
EPS = 1e-6
LANES = 128
DT_PAD = LANES

DPROJ_COLS = {"qkv": 0, "z": 4608, "xbc": 6656, "dt": 9728, "gate": 10240}
DPROJ_DT_WIDTH = 512
DPROJ_WIDTH = 12288

ADAM_LR = 0.001
ADAM_B1 = 0.9
ADAM_B2 = 0.999
ADAM_EPS = 1e-08
ADAM_WD = 0.01
ADAM_STEP = 10

VMEM_LIMIT = 56 * 1024 * 1024
MESH = pl.DeviceIdType.MESH
NEG_INF = float("-inf")


def _tile_rows(n, cap, mult):
    return max(t for t in range(mult, min(n, cap) + 1, mult) if n % t == 0)


def _pick(n, candidates):
    for c in candidates:
        if n % c == 0:
            return c
    return n


def _params(*sem):
    return pltpu.CompilerParams(dimension_semantics=sem, vmem_limit_bytes=VMEM_LIMIT)


def _sigmoid(x):
    return 0.5 * jnp.tanh(0.5 * x) + 0.5


def _softplus(x):
    return jnp.maximum(x, 0.0) + jnp.log(1.0 + jnp.exp(-jnp.abs(x)))


def _dot(a, b, dims):
    return lax.dot_general(a.astype(BF16), b.astype(BF16), (dims, ((), ())), preferred_element_type=F32)


def _nn(a, b):
    return _dot(a, b, ((1,), (0,)))


def _nt(a, b):
    return _dot(a, b, ((1,), (1,)))


def _tn(a, b):
    return _dot(a, b, ((0,), (0,)))


def _split3(v):
    hi = v.astype(BF16)
    r1 = v - hi.astype(F32)
    mid = r1.astype(BF16)
    lo = (r1 - mid.astype(F32)).astype(BF16)
    return hi, mid, lo


def _mask_nn(mask, v):
    mb = mask.astype(BF16)
    hi, mid, lo = _split3(v)
    return _nn(mb, hi) + (_nn(mb, mid) + _nn(mb, lo))


MM_VMEM_BUDGET = 40 * 1024 * 1024
MM_FULL_K = 2816


def _mm_tiles(m, n, k, a_bytes, b_bytes, o_bytes, has_add):
    tk = k if k <= MM_FULL_K else _pick(k, (2048, 1024, 512, 256, 128))
    tn = 1408 if (n > 1024 and n % 1408 == 0) else _pick(n, (1024, 768, 512, 384, 256, 128))
    for tm in (1408, 1024, 768, 512, 384, 256, 128):
        if m % tm:
            continue
        buffers = 2 * (tm * tk * a_bytes + tk * tn * b_bytes + tm * tn * (o_bytes + (4 if has_add else 0)))
        if tk < k:
            buffers += tm * tn * 4
        if buffers <= MM_VMEM_BUDGET:
            return tm, tn, tk
    return _pick(m, (128,)), tn, tk


def _mm(a, b, *, mode, name, out_dtype=F32, add=None, after=None):
    if mode == "nn":
        (m, k), n = a.shape, b.shape[1]
    elif mode == "nt":
        (m, k), n = a.shape, b.shape[0]
    else:
        (k, m), n = a.shape, b.shape[1]
    has_add = add is not None
    tm, tn, tk = _mm_tiles(m, n, k, a.dtype.itemsize, b.dtype.itemsize, jnp.dtype(out_dtype).itemsize, has_add)
    nk = k // tk
    dims = {"nn": ((1,), (0,)), "nt": ((1,), (1,)), "tn": ((0,), (0,))}[mode]
    a_spec = {"nn": pl.BlockSpec((tm, tk), lambda i, j, kk: (i, kk)),
              "nt": pl.BlockSpec((tm, tk), lambda i, j, kk: (i, kk)),
              "tn": pl.BlockSpec((tk, tm), lambda i, j, kk: (kk, i))}[mode]
    b_spec = {"nn": pl.BlockSpec((tk, tn), lambda i, j, kk: (kk, j)),
              "nt": pl.BlockSpec((tn, tk), lambda i, j, kk: (j, kk)),
              "tn": pl.BlockSpec((tk, tn), lambda i, j, kk: (kk, j))}[mode]
    o_spec = pl.BlockSpec((tm, tn), lambda i, j, kk: (i, j))

    def finish(r, c_ref, o_ref):
        if has_add:
            r = r + c_ref[...]
        o_ref[...] = r.astype(out_dtype)

    def body_one(*refs):
        a_ref, b_ref = refs[:2]
        finish(_dot(a_ref[...], b_ref[...], dims), refs[2] if has_add else None, refs[-1])

    def body_acc(*refs):
        a_ref, b_ref = refs[:2]
        o_ref, acc = refs[-2:]
        kk = pl.program_id(2)

        @pl.when(kk == 0)
        def _():
            acc[...] = jnp.zeros_like(acc)

        acc[...] += _dot(a_ref[...], b_ref[...], dims)

        @pl.when(kk == nk - 1)
        def _():
            finish(acc[...], refs[2] if has_add else None, o_ref)

    in_specs = [a_spec, b_spec] + ([o_spec] if has_add else [])
    args = (a, b) + ((add,) if has_add else ())
    if after is not None:
        in_specs, args = in_specs + [pl.BlockSpec(memory_space=pl.ANY)], args + (after,)
    return pl.pallas_call(
        body_one if nk == 1 else body_acc, name=name, grid=(m // tm, n // tn, nk),
        in_specs=in_specs, out_specs=o_spec,
        out_shape=jax.ShapeDtypeStruct((m, n), out_dtype),
        scratch_shapes=[] if nk == 1 else [pltpu.VMEM((tm, tn), F32)],
        compiler_params=_params("parallel", "parallel", "arbitrary"),
    )(*args)


def _rmsnorm_fwd(x, g, name):
    t, d = x.shape
    tm = _pick(t, (512, 256, 128))

    def body(x_ref, g_ref, o_ref):
        xv = x_ref[...]
        r = lax.rsqrt(jnp.mean(xv * xv, axis=-1, keepdims=True) + EPS)
        o_ref[...] = ((xv * r) * g_ref[...]).astype(BF16)

    return pl.pallas_call(
        body, name=name, grid=(t // tm,),
        in_specs=[pl.BlockSpec((tm, d), lambda i: (i, 0)), pl.BlockSpec((1, d), lambda i: (0, 0))],
        out_specs=pl.BlockSpec((tm, d), lambda i: (i, 0)),
        out_shape=jax.ShapeDtypeStruct((t, d), BF16),
        compiler_params=_params("parallel"),
    )(x, g)


def _rmsnorm_bwd(x, g, dh, dres, name, with_bf16=False):
    t, d = x.shape
    tm = _pick(t, (512, 256, 128))

    def body(x_ref, g_ref, dh_ref, dres_ref, dx_ref, dg_ref, *dxb_ref):
        @pl.when(pl.program_id(0) == 0)
        def _():
            dg_ref[...] = jnp.zeros_like(dg_ref)

        xv = x_ref[...]
        r = lax.rsqrt(jnp.mean(xv * xv, axis=-1, keepdims=True) + EPS)
        xhat = xv * r
        dhv = dh_ref[...]
        dyg = dhv * g_ref[...]
        dx = dres_ref[...] + r * (dyg - xhat * jnp.mean(dyg * xhat, axis=-1, keepdims=True))
        dx_ref[...] = dx
        if with_bf16:
            dxb_ref[0][...] = dx.astype(BF16)
        dg_ref[...] += jnp.sum(dhv * xhat, axis=0, keepdims=True)

    row = pl.BlockSpec((tm, d), lambda i: (i, 0))
    vec = pl.BlockSpec((1, d), lambda i: (0, 0))
    extra = with_bf16 * [jax.ShapeDtypeStruct((t, d), BF16)]
    return pl.pallas_call(
        body, name=name, grid=(t // tm,),
        in_specs=[row, vec, row, row], out_specs=[row, vec] + with_bf16 * [row],
        out_shape=[jax.ShapeDtypeStruct((t, d), F32), jax.ShapeDtypeStruct((1, d), F32)] + extra,
        compiler_params=_params("arbitrary"),
    )(x, g, dh, dres)


def _loss_head(x, g, target, name):
    t, d = x.shape
    tm = _pick(t, (512, 256, 128))

    def body(x_ref, g_ref, t_ref, loss_ref, dx_ref, dg_ref, dxb_ref):
        @pl.when(pl.program_id(0) == 0)
        def _():
            dg_ref[...] = jnp.zeros_like(dg_ref)
            loss_ref[...] = jnp.zeros_like(loss_ref)

        xv = x_ref[...]
        gv = g_ref[...]
        r = lax.rsqrt(jnp.mean(xv * xv, axis=-1, keepdims=True) + EPS)
        xhat = xv * r
        err = xhat * gv - t_ref[...]
        loss_ref[...] += jnp.sum(err * err) * (0.5 / d)
        dy = err * (1.0 / d)
        dyg = dy * gv
        dx = r * (dyg - xhat * jnp.mean(dyg * xhat, axis=-1, keepdims=True))
        dx_ref[...] = dx
        dxb_ref[...] = dx.astype(BF16)
        dg_ref[...] += jnp.sum(dy * xhat, axis=0, keepdims=True)

    row = pl.BlockSpec((tm, d), lambda i: (i, 0))
    vec = pl.BlockSpec((1, d), lambda i: (0, 0))
    return pl.pallas_call(
        body, name=name, grid=(t // tm,),
        in_specs=[row, vec, row],
        out_specs=[pl.BlockSpec((1, LANES), lambda i: (0, 0)), row, vec, row],
        out_shape=[jax.ShapeDtypeStruct((1, LANES), F32), jax.ShapeDtypeStruct((t, d), F32),
                   jax.ShapeDtypeStruct((1, d), F32), jax.ShapeDtypeStruct((t, d), BF16)],
        compiler_params=_params("arbitrary"),
    )(x, g, target)


CONV_HALO = 8
CONV_ROWS = 64


def _conv_taps(window, wv, bv):
    acc = bv + wv[SSM_CONV - 1:SSM_CONV, :] * window(0)
    for sh in range(1, SSM_CONV):
        kidx = SSM_CONV - 1 - sh
        acc = acc + wv[kidx:kidx + 1, :] * window(sh)
    return acc


def _conv_fwd(u, w, bias, name):
    b, s, c = u.shape
    rows = CONV_ROWS

    def body(u_ref, w_ref, b_ref, o_ref, ext):
        ext[0:CONV_HALO, :] = jnp.zeros((CONV_HALO, LANES), F32)
        ext[CONV_HALO:, :] = u_ref[...]
        wv, bv = w_ref[...], b_ref[...]
        for r0 in range(0, s, rows):
            acc = _conv_taps(lambda sh: ext[CONV_HALO + r0 - sh:CONV_HALO + r0 - sh + rows, :], wv, bv)
            o_ref[r0:r0 + rows, :] = acc * _sigmoid(acc)

    strip = pl.BlockSpec((None, s, LANES), lambda bi, j: (bi, 0, j))
    return pl.pallas_call(
        body, name=name, grid=(b, c // LANES),
        in_specs=[strip, pl.BlockSpec((SSM_CONV, LANES), lambda bi, j: (0, j)),
                  pl.BlockSpec((1, LANES), lambda bi, j: (0, j))],
        out_specs=strip, out_shape=jax.ShapeDtypeStruct((b, s, c), F32),
        scratch_shapes=[pltpu.VMEM((CONV_HALO + s, LANES), F32)],
        compiler_params=_params("parallel", "parallel"),
    )(u, w, bias)


def _conv_bwd(u, dout, w, bias, dproj, name):
    b, s, c = u.shape
    rows = CONV_ROWS

    def fold(v):
        return jnp.sum(v.reshape(rows // CONV_HALO, CONV_HALO, LANES), axis=0)

    def body(u_ref, d_ref, w_ref, b_ref, buf_ref, du_ref, dw_ref, db_ref, ext, dpre):
        @pl.when(pl.program_id(1) == 0)
        def _():
            dw_ref[...] = jnp.zeros_like(dw_ref)
            db_ref[...] = jnp.zeros_like(db_ref)

        ext[0:CONV_HALO, :] = jnp.zeros((CONV_HALO, LANES), F32)
        ext[CONV_HALO:, :] = u_ref[...]
        dpre[s:, :] = jnp.zeros((CONV_HALO, LANES), F32)
        wv, bv = w_ref[...], b_ref[...]
        sums = [jnp.zeros((CONV_HALO, LANES), F32)] * (SSM_CONV + 1)
        for r0 in range(0, s, rows):
            window = lambda sh: ext[CONV_HALO + r0 - sh:CONV_HALO + r0 - sh + rows, :]
            acc = _conv_taps(window, wv, bv)
            sg = _sigmoid(acc)
            dp = d_ref[r0:r0 + rows, :] * (sg * (1.0 + acc * (1.0 - sg)))
            dpre[r0:r0 + rows, :] = dp
            taps = [sums[SSM_CONV - 1 - sh] + fold(dp * window(sh)) for sh in range(SSM_CONV)]
            sums = taps[::-1] + [sums[SSM_CONV] + fold(dp)]
        for r0 in range(0, s, rows):
            du = wv[SSM_CONV - 1:SSM_CONV, :] * dpre[r0:r0 + rows, :]
            for sh in range(1, SSM_CONV):
                kidx = SSM_CONV - 1 - sh
                du = du + wv[kidx:kidx + 1, :] * dpre[r0 + sh:r0 + sh + rows, :]
            du_ref[r0:r0 + rows, :] = du.astype(BF16)
        for kidx in range(SSM_CONV):
            dw_ref[kidx:kidx + 1, :] += jnp.sum(sums[kidx], axis=0, keepdims=True)
        db_ref[...] += jnp.sum(sums[SSM_CONV], axis=0, keepdims=True)

    strip = pl.BlockSpec((None, s, LANES), lambda j, bi: (bi, 0, j))
    taps = pl.BlockSpec((SSM_CONV, LANES), lambda j, bi: (0, j))
    vec = pl.BlockSpec((1, LANES), lambda j, bi: (0, j))
    du_cols = pl.BlockSpec((None, s, LANES), lambda j, bi: (bi, 0, DPROJ_COLS["xbc"] // LANES + j))
    return pl.pallas_call(
        body, name=name, grid=(c // LANES, b),
        in_specs=[strip, strip, taps, vec, pl.BlockSpec(memory_space=pl.ANY)], out_specs=[du_cols, taps, vec],
        input_output_aliases={4: 0},
        out_shape=[jax.ShapeDtypeStruct(dproj.shape, dproj.dtype), jax.ShapeDtypeStruct((SSM_CONV, c), F32),
                   jax.ShapeDtypeStruct((1, c), F32)],
        scratch_shapes=[pltpu.VMEM((CONV_HALO + s, LANES), F32), pltpu.VMEM((s + CONV_HALO, LANES), F32)],
        compiler_params=_params("parallel", "arbitrary"),
    )(u, dout, w, bias, dproj)


def _ssd_chunk_terms(dtr_ref, bias_ref, alog_ref):
    q = SSM_CHUNK
    dt = _softplus(dtr_ref[...] + bias_ref[...])
    a_neg = -jnp.exp(alog_ref[...])
    row = lax.broadcasted_iota(jnp.int32, (q, q), 0)
    col = lax.broadcasted_iota(jnp.int32, (q, q), 1)
    lower = row >= col
    s = _mask_nn(lower, dt * a_neg)
    return dt, a_neg, s, s.T, lower


def _head_masks():
    heads = jnp.arange(LANES)[:, None]
    chans = jnp.arange(SSM_D_INNER)[None, :]
    to_channels = (chans // SSM_HEAD_DIM == heads).astype(BF16)
    return to_channels, to_channels.T


def _per_channel(v, to_channels):
    hi = v.astype(BF16)
    lo = (v - hi.astype(F32)).astype(BF16)
    return _nn(hi, to_channels) + _nn(lo, to_channels)


def _per_head(v, to_heads):
    hi = v.astype(BF16)
    lo = (v - hi.astype(F32)).astype(BF16)
    return _nn(hi, to_heads) + _nn(lo, to_heads)


def _decay_terms_per_channel(dt, s_col, to_channels):
    q = SSM_CHUNK
    tot = s_col[q - 1:q, :]
    stacked = jnp.concatenate([dt, jnp.exp(s_col), jnp.exp(tot - s_col)], axis=0)
    wide = _per_channel(stacked, to_channels)
    dtx, esx, decx = wide[:q], wide[q:2 * q], wide[2 * q:]
    return dtx, esx, decx, esx[0:1, :] * decx[0:1, :]


SSM_PAIRS_PER_GROUP = SSM_HEADS_PER_GROUP // 2
SSM_GROUP_CHANNELS = SSM_HEADS_PER_GROUP * SSM_HEAD_DIM


def _split_pair(v):
    first = lax.broadcasted_iota(jnp.int32, v.shape, 1) < SSM_HEAD_DIM
    return jnp.concatenate([jnp.where(first, v, 0.0), jnp.where(first, 0.0, v)], axis=0)


def _ssd_fwd(xc, dtr, dt_bias, a_log, dskx, to_channels, name):
    b, s, _ = xc.shape
    q = SSM_CHUNK
    nc = s // q
    n, gc = SSM_D_STATE, SSM_GROUP_CHANNELS

    def body(xc_ref, dtr_ref, bias_ref, alog_ref, dsk_ref, tc_ref, y_ref, hs_ref, h_scr):
        @pl.when(pl.program_id(1) == 0)
        def _():
            h_scr[...] = jnp.zeros_like(h_scr)

        dt, _, s_col, s_row, lower = _ssd_chunk_terms(dtr_ref, bias_ref, alog_ref)
        dtx, esx, decx, etotx = _decay_terms_per_channel(dt, s_col, tc_ref[...])
        x = xc_ref[:, :SSM_D_INNER]
        xdt = x * dtx
        xdec = xdt * decx
        skip = dsk_ref[...] * x
        for g in range(SSM_N_GROUPS):
            bg = xc_ref[:, SSM_D_INNER + n * g:SSM_D_INNER + n * (g + 1)].astype(BF16)
            cg = xc_ref[:, SSM_D_INNER + n * (SSM_N_GROUPS + g):SSM_D_INNER + n * (SSM_N_GROUPS + g + 1)].astype(BF16)
            gsl = slice(gc * g, gc * (g + 1))
            gm = _nt(cg, bg)
            hgt = h_scr[:, gsl]
            hs_ref[:, gsl] = hgt
            y_off = esx[:, gsl] * _nn(cg, hgt)
            h_scr[:, gsl] = etotx[:, gsl] * hgt + _tn(bg, xdec[:, gsl])
            for k in range(SSM_PAIRS_PER_GROUP):
                h0 = g * SSM_HEADS_PER_GROUP + 2 * k
                lo = gc * g + LANES * k
                ms = []
                for h in (h0, h0 + 1):
                    lm = jnp.exp(jnp.where(lower, s_col[:, h:h + 1] - s_row[h:h + 1, :], NEG_INF))
                    ms.append((gm * lm).astype(BF16))
                y_diag = _nn(jnp.concatenate(ms, axis=1), _split_pair(xdt[:, lo:lo + LANES]))
                y_ref[:, lo:lo + LANES] = y_diag + y_off[:, LANES * k:LANES * (k + 1)] + skip[:, lo:lo + LANES]

    vec = pl.BlockSpec((1, LANES), lambda bi, c: (0, 0))
    return pl.pallas_call(
        body, name=name, grid=(b, nc),
        in_specs=[pl.BlockSpec((None, q, SSM_CONV_DIM), lambda bi, c: (bi, c, 0)),
                  pl.BlockSpec((None, q, LANES), lambda bi, c: (bi, c, 0)), vec, vec,
                  pl.BlockSpec((1, SSM_D_INNER), lambda bi, c: (0, 0)),
                  pl.BlockSpec((LANES, SSM_D_INNER), lambda bi, c: (0, 0))],
        out_specs=[pl.BlockSpec((None, q, SSM_D_INNER), lambda bi, c: (bi, c, 0)),
                   pl.BlockSpec((None, None, n, SSM_D_INNER), lambda bi, c: (bi, c, 0, 0))],
        out_shape=[jax.ShapeDtypeStruct((b, s, SSM_D_INNER), F32),
                   jax.ShapeDtypeStruct((b, nc, n, SSM_D_INNER), F32)],
        scratch_shapes=[pltpu.VMEM((n, SSM_D_INNER), F32)],
        compiler_params=_params("parallel", "arbitrary"),
    )(xc, dtr, dt_bias, a_log, dskx, to_channels)


def _ssd_bwd(xc, dtr, dy, hs, dt_bias, a_log, dskx, to_channels, to_heads, dproj, name):
    b, s, _ = xc.shape
    q = SSM_CHUNK
    nc = s // q
    n, gc = SSM_D_STATE, SSM_GROUP_CHANNELS

    def colsum(v):
        return jnp.sum(v, axis=0, keepdims=True)

    def body(xc_ref, dtr_ref, dy_ref, hs_ref, bias_ref, alog_ref, dsk_ref, tc_ref, th_ref, buf_ref,
             dxc_ref, ddtr_ref, dalog_ref, ddsk_ref, dbias_ref, dh_scr, dxs_scr, dxd_scr, w_scr, dst_scr, rows_scr):
        ci = pl.program_id(1)

        @pl.when(ci == 0)
        def _():
            dh_scr[...] = jnp.zeros_like(dh_scr)

        @pl.when(jnp.logical_and(pl.program_id(0) == 0, ci == 0))
        def _():
            dalog_ref[...] = jnp.zeros_like(dalog_ref)
            ddsk_ref[...] = jnp.zeros_like(ddsk_ref)
            dbias_ref[...] = jnp.zeros_like(dbias_ref)
            dst_scr[...] = jnp.zeros_like(dst_scr)

        dt, a_neg, s_col, s_row, lower = _ssd_chunk_terms(dtr_ref, bias_ref, alog_ref)
        upper = jnp.logical_not(lower) | (lax.broadcasted_iota(jnp.int32, (q, q), 0)
                                          == lax.broadcasted_iota(jnp.int32, (q, q), 1))
        dtx, esx, decx, etotx = _decay_terms_per_channel(dt, s_col, tc_ref[...])
        x = xc_ref[:, :SSM_D_INNER]
        dyv = dy_ref[...]
        xdt = x * dtx
        xdec = xdt * decx
        dw = esx * dyv
        rows_scr[...] = jnp.zeros_like(rows_scr)
        for g in range(SSM_N_GROUPS):
            b_lo = SSM_D_INNER + n * g
            c_lo = SSM_D_INNER + n * (SSM_N_GROUPS + g)
            bg = xc_ref[:, b_lo:b_lo + n].astype(BF16)
            cg = xc_ref[:, c_lo:c_lo + n].astype(BF16)
            gsl = slice(gc * g, gc * (g + 1))
            gm = _nt(cg, bg)
            gmt = _nt(bg, cg)
            hgt = hs_ref[:, gsl]
            dhgt = dh_scr[:, gsl]
            w_scr[:, gsl] = _nn(cg, hgt)
            dcg = _nt(dw[:, gsl], hgt)
            dxs = decx[:, gsl] * _nn(bg, dhgt)
            dxs_scr[:, gsl] = dxs
            dbg = _nt(xdec[:, gsl], dhgt)
            rows_scr[2:3, gsl] = colsum(dhgt * hgt)
            dh_scr[:, gsl] = _tn(cg, dw[:, gsl]) + etotx[:, gsl] * dhgt
            dg = jnp.zeros((q, q), F32)
            dgt = jnp.zeros((q, q), F32)
            for k in range(SSM_PAIRS_PER_GROUP):
                h0 = g * SSM_HEADS_PER_GROUP + 2 * k
                lo = gc * g + LANES * k
                xp = xdt[:, lo:lo + LANES]
                dyp = dyv[:, lo:lo + LANES]
                dy2 = _split_pair(dyp)
                dm2 = _nt(dy2, xp)
                dmt2 = _nt(_split_pair(xp), dyp)
                mts = []
                for i, h in enumerate((h0, h0 + 1)):
                    lm = jnp.exp(jnp.where(lower, s_col[:, h:h + 1] - s_row[h:h + 1, :], NEG_INF))
                    lmt = jnp.exp(jnp.where(upper, s_row[h:h + 1, :] - s_col[:, h:h + 1], NEG_INF))
                    dm = dm2[q * i:q * (i + 1), :]
                    dmt = dmt2[q * i:q * (i + 1), :]
                    dg = dg + dm * lm
                    dgt = dgt + dmt * lmt
                    mt = gmt * lmt
                    dst_scr[h:h + 1, :] = colsum(dmt * mt) - colsum(dm * (gm * lm))
                    mts.append(mt.astype(BF16))
                dxd_scr[:, lo:lo + LANES] = _nn(jnp.concatenate(mts, axis=1), dy2)
            dxc_ref[:, b_lo:b_lo + n] = dbg + _nn(dgt, cg)
            dxc_ref[:, c_lo:c_lo + n] = dcg + _nn(dg, bg)
        dxs = dxs_scr[...]
        dxdt = dxd_scr[...] + dxs
        dxc_ref[:, :SSM_D_INNER] = dxdt * dtx + dsk_ref[...] * dyv
        state_part = xdt * dxs
        rows_scr[0:1, :] = colsum(dyv * x)
        rows_scr[1:2, :] = colsum(state_part)
        th = th_ref[...]
        per_head = _per_head(jnp.concatenate([dw * w_scr[...] - state_part, dxdt * x], axis=0), th)
        r_ds, r_dt = per_head[:q], per_head[q:]
        sums = _per_head(rows_scr[...], th)
        etot = jnp.exp(s_col[q - 1:q, :])
        dtot = sums[1:2, :] + etot * sums[2:3, :]
        last = lax.broadcasted_iota(jnp.int32, (q, LANES), 0) == q - 1
        ds = dst_scr[...].T + r_ds + jnp.where(last, dtot, 0.0)
        da = _mask_nn(upper, ds)
        ddt = da * a_neg + r_dt
        live = lax.broadcasted_iota(jnp.int32, (1, LANES), 1) < SSM_N_HEADS
        sg = _sigmoid(dtr_ref[...] + bias_ref[...])
        ddtr = jnp.where(live, ddt * sg, 0.0)
        ddtr_ref[:, :LANES] = ddtr.astype(BF16)
        ddtr_ref[:, LANES:] = jnp.zeros((q, DPROJ_DT_WIDTH - LANES), BF16)
        dalog_ref[...] += jnp.where(live, colsum(da * dt) * a_neg, 0.0)
        ddsk_ref[...] += jnp.where(live, sums[0:1, :], 0.0)
        dbias_ref[...] += colsum(ddtr)

    rev = lambda bi, c: (bi, nc - 1 - c, 0)
    vec = pl.BlockSpec((1, LANES), lambda bi, c: (0, 0))
    wide = pl.BlockSpec((None, q, SSM_D_INNER), rev)
    return pl.pallas_call(
        body, name=name, grid=(b, nc),
        in_specs=[pl.BlockSpec((None, q, SSM_CONV_DIM), rev), pl.BlockSpec((None, q, LANES), rev), wide,
                  pl.BlockSpec((None, None, n, SSM_D_INNER), lambda bi, c: (bi, nc - 1 - c, 0, 0)),
                  vec, vec, pl.BlockSpec((1, SSM_D_INNER), lambda bi, c: (0, 0)),
                  pl.BlockSpec((LANES, SSM_D_INNER), lambda bi, c: (0, 0)),
                  pl.BlockSpec((SSM_D_INNER, LANES), lambda bi, c: (0, 0)),
                  pl.BlockSpec(memory_space=pl.ANY)],
        out_specs=[pl.BlockSpec((None, q, SSM_CONV_DIM), rev),
                   pl.BlockSpec((None, q, DPROJ_DT_WIDTH),
                                lambda bi, c: (bi, nc - 1 - c, DPROJ_COLS["dt"] // DPROJ_DT_WIDTH)), vec, vec, vec],
        input_output_aliases={9: 1},
        out_shape=[jax.ShapeDtypeStruct((b, s, SSM_CONV_DIM), F32), jax.ShapeDtypeStruct(dproj.shape, dproj.dtype),
                   jax.ShapeDtypeStruct((1, LANES), F32), jax.ShapeDtypeStruct((1, LANES), F32),
                   jax.ShapeDtypeStruct((1, LANES), F32)],
        scratch_shapes=[pltpu.VMEM((n, SSM_D_INNER), F32)] + [pltpu.VMEM((q, SSM_D_INNER), F32)] * 3
        + [pltpu.VMEM((LANES, q), F32), pltpu.VMEM((8, SSM_D_INNER), F32)],
        compiler_params=_params("arbitrary", "arbitrary"),
    )(xc, dtr, dy, hs, dt_bias, a_log, dskx, to_channels, to_heads, dproj)


SSM_GROUP_WIDTH = SSM_D_INNER // SSM_N_GROUPS


def _gate_norm_fwd(y, z, w, name):
    t, d = y.shape
    tm = _pick(t, (256, 128))

    def body(y_ref, z_ref, w_ref, o_ref):
        for g in range(SSM_N_GROUPS):
            sl = slice(SSM_GROUP_WIDTH * g, SSM_GROUP_WIDTH * (g + 1))
            zv = z_ref[:, sl]
            u = y_ref[:, sl] * (zv * _sigmoid(zv))
            r = lax.rsqrt(jnp.mean(u * u, axis=-1, keepdims=True) + EPS)
            o_ref[:, sl] = ((u * r) * w_ref[:, sl]).astype(BF16)

    row = pl.BlockSpec((tm, d), lambda i: (i, 0))
    return pl.pallas_call(
        body, name=name, grid=(t // tm,),
        in_specs=[row, row, pl.BlockSpec((1, d), lambda i: (0, 0))], out_specs=row,
        out_shape=jax.ShapeDtypeStruct((t, d), BF16),
        compiler_params=_params("parallel"),
    )(y, z, w)


def _gate_norm_bwd(y, z, w, dout, dproj, name):
    t, d = y.shape
    gw = SSM_GROUP_WIDTH
    tm = _pick(t, (1024, 512, 256, 128))

    def body(y_ref, z_ref, w_ref, do_ref, buf_ref, dy_ref, dz_ref, dw_ref):
        @pl.when(pl.program_id(1) == 0)
        def _():
            dw_ref[...] = jnp.zeros_like(dw_ref)

        zv = z_ref[...]
        yv = y_ref[...]
        sg = _sigmoid(zv)
        silu = zv * sg
        u = yv * silu
        r = lax.rsqrt(jnp.mean(u * u, axis=-1, keepdims=True) + EPS)
        uh = u * r
        dov = do_ref[...]
        dw_ref[...] += jnp.sum(dov * uh, axis=0, keepdims=True)
        dyg = dov * w_ref[...]
        du = r * (dyg - uh * jnp.mean(dyg * uh, axis=-1, keepdims=True))
        dy_ref[...] = du * silu
        dz_ref[...] = (du * yv * (sg * (1.0 + zv * (1.0 - sg)))).astype(BF16)

    tile = pl.BlockSpec((tm, gw), lambda g, i: (i, g))
    vec = pl.BlockSpec((1, gw), lambda g, i: (0, g))
    z_cols = pl.BlockSpec((tm, gw), lambda g, i: (i, DPROJ_COLS["z"] // gw + g))
    return pl.pallas_call(
        body, name=name, grid=(SSM_N_GROUPS, t // tm),
        in_specs=[tile, tile, vec, tile, pl.BlockSpec(memory_space=pl.ANY)], out_specs=[tile, z_cols, vec],
        out_shape=[jax.ShapeDtypeStruct((t, d), F32), jax.ShapeDtypeStruct(dproj.shape, dproj.dtype),
                   jax.ShapeDtypeStruct((1, d), F32)],
        input_output_aliases={4: 1},
        compiler_params=_params("parallel", "arbitrary"),
    )(y, z, w, dout, dproj)


def _rope_tables(s):
    half = ATT_HEAD_DIM // 2
    inv = ROPE_THETA ** (-jnp.arange(half, dtype=F32) / half)
    ang = jnp.arange(s).astype(F32)[:, None] * inv[None, :]
    cos, sin = jnp.cos(ang), jnp.sin(ang)
    return jnp.concatenate([cos, cos], axis=-1), jnp.concatenate([-sin, sin], axis=-1)


ATT_TILE = 256


def _by_residue_spec(r, width):
    return pl.BlockSpec((None, r, ATT_TILE // r, width), lambda bi, i: (bi, 0, i, 0))


def _to_residues(tile, stage, r, store):
    if r == 1:
        store(0, tile)
        return
    stage[...] = tile
    for ri in range(r):
        store(ri, stage[pl.ds(ri, ATT_TILE // r, stride=r), :])


def _from_residues(load, stage, r):
    if r == 1:
        return load(0)
    for ri in range(r):
        stage[pl.ds(ri, ATT_TILE // r, stride=r), :] = load(ri)
    return stage[...]


def _rope_fwd(qkv, cosf, sinf, name):
    b, s, w = qkv.shape
    ts, d, gw = ATT_TILE, ATT_HEAD_DIM, ATT_OUT_DIM

    def body(x_ref, c_ref, s_ref, *rest):
        outs, stage = rest[:-1], rest[-1]
        cv, sv = c_ref[...], s_ref[...]
        for kind in range(3):
            for gi, r in enumerate(ATT_DILATIONS):
                for j in range(ATT_HEADS_PER_GROUP):
                    src = d * (kind * ATT_N_HEADS + gi * ATT_HEADS_PER_GROUP + j)
                    dst = slice(kind * gw + d * j, kind * gw + d * (j + 1))
                    tv = x_ref[:, src:src + d]
                    if kind < 2:
                        tv = tv * cv + pltpu.roll(tv, d // 2, 1) * sv

                    def store(ri, rows, o_ref=outs[gi], dst=dst):
                        o_ref[ri, :, dst] = rows.astype(BF16)

                    _to_residues(tv, stage, r, store)

    tab = pl.BlockSpec((ts, d), lambda bi, i: (i, 0))
    return pl.pallas_call(
        body, name=name, grid=(b, s // ts),
        in_specs=[pl.BlockSpec((None, ts, w), lambda bi, i: (bi, i, 0)), tab, tab],
        out_specs=[_by_residue_spec(r, 3 * gw) for r in ATT_DILATIONS],
        out_shape=[jax.ShapeDtypeStruct((b, r, s // r, 3 * gw), BF16) for r in ATT_DILATIONS],
        scratch_shapes=[pltpu.VMEM((ts, d), F32)],
        compiler_params=_params("parallel", "parallel"),
    )(qkv, cosf, sinf)


def _rope_bwd(dq, dk, dv, cosf, sinf, dproj, name):
    n_pat = len(ATT_DILATIONS)
    b, _, s, gw = dq[0].shape
    ts, d = ATT_TILE, ATT_HEAD_DIM

    def body(*refs):
        ins, (c_ref, s_ref, _, o_ref, stage) = refs[:3 * n_pat], refs[3 * n_pat:]
        cv, sv = c_ref[...], s_ref[...]
        for kind in range(3):
            for gi, r in enumerate(ATT_DILATIONS):
                src = ins[kind * n_pat + gi]
                for j in range(ATT_HEADS_PER_GROUP):
                    tv = _from_residues(lambda ri, src=src, j=j: src[ri, :, d * j:d * (j + 1)], stage, r)
                    if kind < 2:
                        tv = tv * cv + pltpu.roll(tv * sv, d // 2, 1)
                    lo = d * (kind * ATT_N_HEADS + gi * ATT_HEADS_PER_GROUP + j)
                    o_ref[:, lo:lo + d] = tv.astype(BF16)

    tab = pl.BlockSpec((ts, d), lambda bi, i: (i, 0))
    parts = [_by_residue_spec(r, gw) for r in ATT_DILATIONS]
    return pl.pallas_call(
        body, name=name, grid=(b, s // ts), in_specs=parts * 3 + [tab, tab, pl.BlockSpec(memory_space=pl.ANY)],
        out_specs=pl.BlockSpec((None, ts, ATT_QKV_DIM), lambda bi, i: (bi, i, DPROJ_COLS["qkv"] // ATT_QKV_DIM)),
        out_shape=jax.ShapeDtypeStruct(dproj.shape, dproj.dtype),
        input_output_aliases={3 * n_pat + 2: 0},
        scratch_shapes=[pltpu.VMEM((ts, d), F32)],
        compiler_params=_params("parallel", "parallel"),
    )(*dq, *dk, *dv, cosf, sinf, dproj)


ATT_SCALE = ATT_HEAD_DIM ** -0.5
ATT_STEP = 2 * ATT_BLOCK


def _att_spec(col):
    return pl.BlockSpec((None, None, ATT_STEP, ATT_OUT_DIM), lambda bi, ri, i: (bi, ri, i, col))


def _att_edge_spec(col, side, n_steps):
    def index(bi, ri, i):
        blk = 2 * i - 1 if side < 0 else 2 * i + 2
        return (bi, ri, jnp.clip(blk, 0, 2 * n_steps - 1), col)
    return pl.BlockSpec((None, None, ATT_BLOCK, ATT_OUT_DIM), index)


def _band_mask(shape, q_axis, has_prev):
    qi = lax.broadcasted_iota(jnp.int32, shape, q_axis)
    kj = lax.broadcasted_iota(jnp.int32, shape, 1 - q_axis)
    dist = qi + ATT_BLOCK - kj
    return (dist >= 0) & (dist <= ATT_BLOCK) & (has_prev | (kj >= ATT_BLOCK))


def _att_fwd(qkr, name):
    b, r, l, _ = qkr.shape
    nb = l // ATT_STEP
    d = ATT_HEAD_DIM

    def body(q_ref, kp_ref, k_ref, vp_ref, v_ref, o_ref, lse_ref):
        mask = _band_mask((ATT_STEP, ATT_BLOCK + ATT_STEP), 0, pl.program_id(2) > 0)
        heads = [slice(d * j, d * (j + 1)) for j in range(ATT_HEADS_PER_GROUP)]
        scores = [_nt(q_ref[:, sl], jnp.concatenate([kp_ref[:, sl], k_ref[:, sl]], axis=0)) for sl in heads]
        scores = [jnp.where(mask, sc * ATT_SCALE, NEG_INF) for sc in scores]
        tops = [jnp.max(sc, axis=-1, keepdims=True) for sc in scores]
        probs = [jnp.exp(sc - m) for sc, m in zip(scores, tops)]
        dens = [jnp.sum(pr, axis=-1, keepdims=True) for pr in probs]
        for sl, m, pr, den in zip(heads, tops, probs, dens):
            o_ref[:, sl] = _nn(pr / den, jnp.concatenate([vp_ref[:, sl], v_ref[:, sl]], axis=0))
            lse_ref[:, sl] = jnp.broadcast_to(m + jnp.log(den), (ATT_STEP, d))

    out_spec = _att_spec(0)
    return pl.pallas_call(
        body, name=name, grid=(b, r, nb),
        in_specs=[_att_spec(0), _att_edge_spec(1, -1, nb), _att_spec(1), _att_edge_spec(2, -1, nb), _att_spec(2)],
        out_specs=[out_spec, out_spec],
        out_shape=[jax.ShapeDtypeStruct((b, r, l, ATT_OUT_DIM), F32)] * 2,
        compiler_params=_params("parallel", "parallel", "parallel"),
    )(qkr, qkr, qkr, qkr, qkr)


def _att_merge(os_, lses, name):
    n_pat = len(os_)
    b, _, s, gw = os_[0].shape
    ts, d = ATT_TILE, ATT_HEAD_DIM

    def body(*refs):
        o_refs, l_refs = refs[:n_pat], refs[n_pat:2 * n_pat]
        att_ref, lse_outs, stage = refs[2 * n_pat], refs[2 * n_pat + 1:3 * n_pat + 1], refs[-1]
        for j in range(ATT_HEADS_PER_GROUP):
            sl = slice(d * j, d * (j + 1))
            ov = [_from_residues(lambda ri, g=g: o_refs[g][ri, :, sl], stage, r)
                  for g, r in enumerate(ATT_DILATIONS)]
            ls = [_from_residues(lambda ri, g=g: l_refs[g][ri, :, sl], stage, r)
                  for g, r in enumerate(ATT_DILATIONS)]
            m = functools.reduce(jnp.maximum, ls)
            es = [jnp.exp(lv - m) for lv in ls]
            tot = functools.reduce(lambda u, v: u + v, es)
            acc = (es[0] / tot) * ov[0]
            for g in range(1, n_pat):
                acc = acc + (es[g] / tot) * ov[g]
            att_ref[:, sl] = acc
            joint = m + jnp.log(tot)
            for g, r in enumerate(ATT_DILATIONS):
                def store(ri, rows, out=lse_outs[g]):
                    out[ri, :, sl] = rows
                _to_residues(joint, stage, r, store)

    parts = [_by_residue_spec(r, gw) for r in ATT_DILATIONS]
    return pl.pallas_call(
        body, name=name, grid=(b, s // ts), in_specs=parts * 2,
        out_specs=[pl.BlockSpec((None, ts, gw), lambda bi, i: (bi, i, 0))] + parts,
        out_shape=[jax.ShapeDtypeStruct((b, s, gw), F32)]
        + [jax.ShapeDtypeStruct((b, r, s // r, gw), F32) for r in ATT_DILATIONS],
        scratch_shapes=[pltpu.VMEM((ts, d), F32)],
        compiler_params=_params("parallel", "parallel"),
    )(*os_, *lses)


def _att_delta(att, datt, name):
    b, s, gw = att.shape
    ts, d = ATT_TILE, ATT_HEAD_DIM
    n_pat = len(ATT_DILATIONS)

    def body(a_ref, d_ref, *rest):
        do_outs, dl_outs, stage = rest[:n_pat], rest[n_pat:2 * n_pat], rest[-1]
        for j in range(ATT_HEADS_PER_GROUP):
            sl = slice(d * j, d * (j + 1))
            dv = d_ref[:, sl]
            delta = jnp.broadcast_to(jnp.sum(a_ref[:, sl] * dv, axis=-1, keepdims=True), (ts, d))
            for g, r in enumerate(ATT_DILATIONS):
                def store_do(ri, rows, out=do_outs[g]):
                    out[ri, :, sl] = rows.astype(BF16)

                def store_dl(ri, rows, out=dl_outs[g]):
                    out[ri, :, sl] = rows

                _to_residues(dv, stage, r, store_do)
                _to_residues(delta, stage, r, store_dl)

    row = pl.BlockSpec((None, ts, gw), lambda bi, i: (bi, i, 0))
    parts = [_by_residue_spec(r, gw) for r in ATT_DILATIONS]
    outs = pl.pallas_call(
        body, name=name, grid=(b, s // ts), in_specs=[row, row], out_specs=parts * 2,
        out_shape=[jax.ShapeDtypeStruct((b, r, s // r, gw), BF16) for r in ATT_DILATIONS]
        + [jax.ShapeDtypeStruct((b, r, s // r, gw), F32) for r in ATT_DILATIONS],
        scratch_shapes=[pltpu.VMEM((ts, d), F32)],
        compiler_params=_params("parallel", "parallel"),
    )(att, datt)
    return outs[:n_pat], outs[n_pat:]


def _att_bwd_q(qkr, datt, lse, delta, name):
    b, r, l, _ = qkr.shape
    nb = l // ATT_STEP
    d = ATT_HEAD_DIM

    def body(q_ref, kp_ref, k_ref, vp_ref, v_ref, do_ref, lse_ref, dl_ref, dq_ref):
        mask = _band_mask((ATT_STEP, ATT_BLOCK + ATT_STEP), 0, pl.program_id(2) > 0)
        heads = [slice(d * j, d * (j + 1)) for j in range(ATT_HEADS_PER_GROUP)]
        kcats = [jnp.concatenate([kp_ref[:, sl], k_ref[:, sl]], axis=0) for sl in heads]
        scores = [_nt(q_ref[:, sl], kcat) for sl, kcat in zip(heads, kcats)]
        dps = [_nt(do_ref[:, sl], jnp.concatenate([vp_ref[:, sl], v_ref[:, sl]], axis=0)) for sl in heads]
        probs = [jnp.exp(jnp.where(mask, sc * ATT_SCALE - lse_ref[:, sl.start:sl.start + 1], NEG_INF))
                 for sl, sc in zip(heads, scores)]
        dscs = [pr * (dp - dl_ref[:, sl.start:sl.start + 1]) for sl, pr, dp in zip(heads, probs, dps)]
        for sl, dsc, kcat in zip(heads, dscs, kcats):
            dq_ref[:, sl] = _nn(dsc, kcat) * ATT_SCALE

    tok = _att_spec(0)
    return pl.pallas_call(
        body, name=name, grid=(b, r, nb),
        in_specs=[_att_spec(0), _att_edge_spec(1, -1, nb), _att_spec(1), _att_edge_spec(2, -1, nb), _att_spec(2),
                  tok, tok, tok],
        out_specs=tok,
        out_shape=jax.ShapeDtypeStruct((b, r, l, ATT_OUT_DIM), F32),
        compiler_params=_params("parallel", "parallel", "parallel"),
    )(qkr, qkr, qkr, qkr, qkr, datt, lse, delta)


def _att_bwd_kv(qkr, datt, lse, delta, name):
    b, r, l, _ = qkr.shape
    nb = l // ATT_STEP
    d = ATT_HEAD_DIM

    def body(k_ref, v_ref, q_ref, qn_ref, do_ref, don_ref, lse_ref, lsen_ref, dl_ref, dln_ref, dk_ref, dv_ref):
        shape = (ATT_STEP, ATT_STEP + ATT_BLOCK)
        kj = lax.broadcasted_iota(jnp.int32, shape, 0)
        qi = lax.broadcasted_iota(jnp.int32, shape, 1)
        dist = qi - kj
        has_next = pl.program_id(2) < nb - 1
        mask = (dist >= 0) & (dist <= ATT_BLOCK) & (has_next | (qi < ATT_STEP))
        def per_query(own_ref, next_ref, sl):
            return jnp.tile(jnp.concatenate([own_ref[:, sl], next_ref[:, sl]], axis=0).T, (ATT_STEP // d, 1))

        heads = [slice(d * j, d * (j + 1)) for j in range(ATT_HEADS_PER_GROUP)]
        qcats = [jnp.concatenate([q_ref[:, sl], qn_ref[:, sl]], axis=0) for sl in heads]
        docats = [jnp.concatenate([do_ref[:, sl], don_ref[:, sl]], axis=0) for sl in heads]
        scores = [_nt(k_ref[:, sl], qcat) for sl, qcat in zip(heads, qcats)]
        dps = [_nt(v_ref[:, sl], docat) for sl, docat in zip(heads, docats)]
        probs = [jnp.exp(jnp.where(mask, sc * ATT_SCALE - per_query(lse_ref, lsen_ref, sl), NEG_INF))
                 for sl, sc in zip(heads, scores)]
        for sl, pr, docat in zip(heads, probs, docats):
            dv_ref[:, sl] = _nn(pr, docat)
        dscs = [pr * (dp - per_query(dl_ref, dln_ref, sl)) for sl, pr, dp in zip(heads, probs, dps)]
        for sl, dsc, qcat in zip(heads, dscs, qcats):
            dk_ref[:, sl] = _nn(dsc, qcat) * ATT_SCALE

    tok, tok_n = _att_spec(0), _att_edge_spec(0, 1, nb)
    return pl.pallas_call(
        body, name=name, grid=(b, r, nb),
        in_specs=[_att_spec(1), _att_spec(2), _att_spec(0), _att_edge_spec(0, 1, nb),
                  tok, tok_n, tok, tok_n, tok, tok_n],
        out_specs=[tok, tok],
        out_shape=[jax.ShapeDtypeStruct((b, r, l, ATT_OUT_DIM), F32)] * 2,
        compiler_params=_params("parallel", "parallel", "parallel"),
    )(qkr, qkr, qkr, qkr, datt, datt, lse, lse, delta, delta)


def _mix_fwd(gl, bg, ys, ya, name):
    t, d = ys.shape
    tm = _pick(t, (512, 256, 128))

    def body(gl_ref, bg_ref, ys_ref, ya_ref, o_ref):
        g0 = _sigmoid(gl_ref[:, :d] + bg_ref[:, :d])
        g1 = _sigmoid(gl_ref[:, d:] + bg_ref[:, d:])
        o_ref[...] = (g0 * ys_ref[...] + g1 * ya_ref[...]).astype(BF16)

    row = pl.BlockSpec((tm, d), lambda i: (i, 0))
    return pl.pallas_call(
        body, name=name, grid=(t // tm,),
        in_specs=[pl.BlockSpec((tm, 2 * d), lambda i: (i, 0)), pl.BlockSpec((1, 2 * d), lambda i: (0, 0)), row, row],
        out_specs=row, out_shape=jax.ShapeDtypeStruct((t, d), BF16),
        compiler_params=_params("parallel"),
    )(gl, bg, ys, ya)


def _mix_bwd(gl, bg, ys, ya, dmixed, name):
    t, d = ys.shape
    tm = _pick(t, (512, 256, 128))

    def body(gl_ref, bg_ref, ys_ref, ya_ref, dm_ref, dys_ref, dya_ref, dgl_ref, dbg_ref):
        @pl.when(pl.program_id(0) == 0)
        def _():
            dbg_ref[...] = jnp.zeros_like(dbg_ref)

        dm = dm_ref[...]
        g0 = _sigmoid(gl_ref[:, :d] + bg_ref[:, :d])
        g1 = _sigmoid(gl_ref[:, d:] + bg_ref[:, d:])
        dys_ref[...] = (dm * g0).astype(BF16)
        dya_ref[...] = (dm * g1).astype(BF16)
        d0 = dm * ys_ref[...] * (g0 * (1.0 - g0))
        d1 = dm * ya_ref[...] * (g1 * (1.0 - g1))
        dgl_ref[:, :d] = d0.astype(BF16)
        dgl_ref[:, d:] = d1.astype(BF16)
        dbg_ref[:, :d] += jnp.sum(d0, axis=0, keepdims=True)
        dbg_ref[:, d:] += jnp.sum(d1, axis=0, keepdims=True)

    row = pl.BlockSpec((tm, d), lambda i: (i, 0))
    wide = pl.BlockSpec((tm, 2 * d), lambda i: (i, 0))
    vec = pl.BlockSpec((1, 2 * d), lambda i: (0, 0))
    gate_cols = pl.BlockSpec((tm, 2 * d), lambda i: (i, DPROJ_COLS["gate"] // (2 * d)))
    return pl.pallas_call(
        body, name=name, grid=(t // tm,),
        in_specs=[wide, vec, row, row, row], out_specs=[row, row, gate_cols, vec],
        out_shape=[jax.ShapeDtypeStruct((t, d), BF16), jax.ShapeDtypeStruct((t, d), BF16),
                   jax.ShapeDtypeStruct((t, DPROJ_WIDTH), BF16), jax.ShapeDtypeStruct((1, 2 * d), F32)],
        compiler_params=_params("arbitrary"),
    )(gl, bg, ys, ya, dmixed)


def _up_proj_swiglu(h, w_up_t, gt, name):
    t, k = h.shape
    f = w_up_t.shape[0]
    tm, tn, _ = _mm_tiles(t, f, k, h.dtype.itemsize, w_up_t.dtype.itemsize, 4 + 2, True)

    def body(h_ref, w_ref, g_ref, up_ref, act_ref):
        up = _nt(h_ref[...], w_ref[...])
        up_ref[...] = up
        gv = g_ref[...]
        act_ref[...] = ((gv * _sigmoid(gv)) * up).astype(BF16)

    tile = pl.BlockSpec((tm, tn), lambda i, j: (i, j))
    return pl.pallas_call(
        body, name=name, grid=(t // tm, f // tn),
        in_specs=[pl.BlockSpec((tm, k), lambda i, j: (i, 0)), pl.BlockSpec((tn, k), lambda i, j: (j, 0)), tile],
        out_specs=[tile, tile],
        out_shape=[jax.ShapeDtypeStruct((t, f), F32), jax.ShapeDtypeStruct((t, f), BF16)],
        compiler_params=_params("parallel", "parallel"),
    )(h, w_up_t, gt)


def _swiglu_bwd(gt, up, dact, name):
    t, f = gt.shape
    tm = _pick(t, (512, 256, 128))

    def body(g_ref, u_ref, d_ref, dg_ref, du_ref):
        gv = g_ref[...]
        dv = d_ref[...]
        sg = _sigmoid(gv)
        dg_ref[...] = (dv * u_ref[...] * (sg * (1.0 + gv * (1.0 - sg)))).astype(BF16)
        du_ref[...] = (dv * (gv * sg)).astype(BF16)

    row = pl.BlockSpec((tm, f), lambda i: (i, 0))
    return pl.pallas_call(
        body, name=name, grid=(t // tm,), in_specs=[row, row, row], out_specs=[row, row],
        out_shape=[jax.ShapeDtypeStruct((t, f), BF16)] * 2, compiler_params=_params("parallel"),
    )(gt, up, dact)


def _peer(k):
    x, y, c = lax.axis_index("x"), lax.axis_index("y"), lax.axis_index("c")
    px, py, pc = x ^ ((k >> 2) & 1), y ^ ((k >> 1) & 1), c ^ (k & 1)
    return (px, py, pc), 4 * px + 2 * py + pc


def _my_index():
    return 4 * lax.axis_index("x") + 2 * lax.axis_index("y") + lax.axis_index("c")


def _all_gather(parts, name):
    n_parts = len(parts)

    def body(*refs):
        ins, outs = refs[:n_parts], refs[n_parts:2 * n_parts]
        send_sems, recv_sems, local_sems = refs[2 * n_parts:]
        here, me = _peer(0)
        sibling, sib_idx = _peer(1)
        chips = [_peer(2 * q) for q in range(1, N_CHIPS)]

        def copy(i, k, block, to, src=None):
            return pltpu.make_async_remote_copy(
                src_ref=outs[i].at[block] if src is None else src, dst_ref=outs[i].at[block],
                send_sem=send_sems.at[i * (N_DEV - 1) + k], recv_sem=recv_sems.at[i * (N_DEV - 1) + k],
                device_id=to, device_id_type=MESH)

        local = [pltpu.make_async_copy(ins[i], outs[i].at[me], local_sems.at[i]) for i in range(n_parts)]
        for cp in local:
            cp.start()
        sends = []
        for i in range(n_parts):
            sends.append(copy(i, 0, me, sibling, src=ins[i]))
            sends += [copy(i, q, me, chip, src=ins[i]) for q, (chip, _) in enumerate(chips, start=1)]
        for cp in sends:
            cp.start()
        for q, (chip, chip_idx) in enumerate(chips, start=1):
            for i in range(n_parts):
                copy(i, q, chip_idx, here).wait_recv()
                fwd = copy(i, N_CHIPS - 1 + q, chip_idx, sibling)
                fwd.start()
                sends.append(fwd)
        for i in range(n_parts):
            copy(i, 0, sib_idx, here).wait_recv()
        for q, (_, chip_idx) in enumerate(chips, start=1):
            for i in range(n_parts):
                copy(i, N_CHIPS - 1 + q, chip_idx ^ 1, here).wait_recv()
        for cp in sends:
            cp.wait_send()
        for cp in local:
            cp.wait()

    hbm = pl.BlockSpec(memory_space=pl.ANY)
    return pl.pallas_call(
        body, name=name, in_specs=[hbm] * n_parts, out_specs=[hbm] * n_parts,
        out_shape=[jax.ShapeDtypeStruct((N_DEV,) + p_.shape, p_.dtype) for p_ in parts],
        scratch_shapes=[pltpu.SemaphoreType.DMA((n_parts * (N_DEV - 1),)),
                        pltpu.SemaphoreType.DMA((n_parts * (N_DEV - 1),)),
                        pltpu.SemaphoreType.DMA((n_parts,))],
        compiler_params=pltpu.CompilerParams(has_side_effects=True),
    )(*parts)


HBM_SPEC = pl.BlockSpec(memory_space=pltpu.HBM)
SEM_SPEC = pl.BlockSpec(memory_space=pltpu.SEMAPHORE)
DATAFLOW = pltpu.SideEffectType.DATAFLOW_SIDE_EFFECTING


def _gather_start(block, after, name):
    per_peer = block.ndim == 3

    def body(v_ref, land_ref, after_ref, send_sems, recv_sems, v_thru, land_thru, token):
        me = _my_index()
        for k in range(1, N_DEV):
            peer, pidx = _peer(k)
            pltpu.make_async_remote_copy(
                src_ref=v_ref.at[pidx] if per_peer else v_ref, dst_ref=land_ref.at[me],
                send_sem=send_sems.at[k - 1], recv_sem=recv_sems.at[k - 1],
                device_id=peer, device_id_type=MESH).start()
        token[...] = jnp.zeros_like(token)

    land_shape = (N_DEV,) + block.shape[-2:]
    return pl.pallas_call(
        body, name=name,
        out_shape=(pltpu.SemaphoreType.DMA((N_DEV - 1,)), pltpu.SemaphoreType.DMA((N_DEV - 1,)),
                   pltpu.HBM(block.shape, block.dtype), pltpu.HBM(land_shape, block.dtype),
                   jax.ShapeDtypeStruct((8, LANES), F32)),
        in_specs=(HBM_SPEC, HBM_SPEC, pl.BlockSpec(memory_space=pl.ANY)),
        out_specs=(SEM_SPEC, SEM_SPEC, HBM_SPEC, HBM_SPEC, pl.BlockSpec(memory_space=pltpu.VMEM)),
        input_output_aliases={0: 2, 1: 3},
        compiler_params=pltpu.CompilerParams(has_side_effects=DATAFLOW),
    )(pltpu.with_memory_space_constraint(block, pltpu.HBM),
      pltpu.with_memory_space_constraint(lax.empty(land_shape, block.dtype), pltpu.HBM), after)


def _gather_wait(send_sems, recv_sems, block, landing, after, name):
    per_peer = block.ndim == 3

    def body(v_ref, land_ref, send_sems, recv_sems, after_ref, v_dead, got_ref):
        for k in range(1, N_DEV):
            peer, pidx = _peer(k)
            copy = pltpu.make_async_remote_copy(
                src_ref=v_ref.at[pidx] if per_peer else v_ref, dst_ref=land_ref.at[pidx],
                send_sem=send_sems.at[k - 1], recv_sem=recv_sems.at[k - 1],
                device_id=peer, device_id_type=MESH)
            copy.wait_send()
            copy.wait_recv()

    return pl.pallas_call(
        body, name=name,
        out_shape=(pltpu.HBM(block.shape, block.dtype), pltpu.HBM(landing.shape, landing.dtype)),
        in_specs=(HBM_SPEC, HBM_SPEC, SEM_SPEC, SEM_SPEC, pl.BlockSpec(memory_space=pl.ANY)),
        out_specs=(HBM_SPEC, HBM_SPEC), input_output_aliases={0: 0, 1: 1},
        compiler_params=pltpu.CompilerParams(has_side_effects=DATAFLOW),
    )(block, landing, send_sems, recv_sems, after)[1]


TILE_ELEMS = 1024 * 1024


def _shared_exchange(shared, name):
    def body(sh_ref, gsh_ref, send_sems, recv_sems, local_sem):
        me = _my_index()
        local = pltpu.make_async_copy(sh_ref, gsh_ref.at[me], local_sem)
        local.start()
        sends = []
        for k in range(1, N_DEV):
            peer, _ = _peer(k)
            cp = pltpu.make_async_remote_copy(
                src_ref=sh_ref, dst_ref=gsh_ref.at[me], send_sem=send_sems.at[k - 1],
                recv_sem=recv_sems.at[k - 1], device_id=peer, device_id_type=MESH)
            cp.start()
            sends.append(cp)
        for k in range(1, N_DEV):
            peer, pidx = _peer(k)
            pltpu.make_async_remote_copy(
                src_ref=sh_ref, dst_ref=gsh_ref.at[pidx], send_sem=send_sems.at[k - 1],
                recv_sem=recv_sems.at[k - 1], device_id=peer, device_id_type=MESH).wait_recv()
        for cp in sends:
            cp.wait_send()
        local.wait()

    hbm = pl.BlockSpec(memory_space=pl.ANY)
    return pl.pallas_call(
        body, name=name, in_specs=[hbm], out_specs=hbm,
        out_shape=jax.ShapeDtypeStruct((N_DEV,) + shared.shape, shared.dtype),
        scratch_shapes=[pltpu.SemaphoreType.DMA((N_DEV - 1,)), pltpu.SemaphoreType.DMA((N_DEV - 1,)),
                        pltpu.SemaphoreType.DMA],
        compiler_params=pltpu.CompilerParams(has_side_effects=True),
    )(shared)


def _adamw(parts, w, m, v, name, row0=0, own=None):
    n_parts, rows, lanes = parts.shape
    tr = rows if rows * lanes <= TILE_ELEMS // 2 else _tile_rows(math.gcd(rows, row0), TILE_ELEMS // 4 // lanes, 8)
    c1 = 1.0 - ADAM_B1 ** ADAM_STEP
    c2 = 1.0 - ADAM_B2 ** ADAM_STEP

    def body(*refs):
        if own is None:
            p_ref, w_ref, m_ref, v_ref, g_ref, d_ref, nm_ref, nv_ref = refs
            terms = [p_ref[j].astype(F32) for j in range(n_parts)]
        else:
            me_ref, p_ref, own_ref, w_ref, m_ref, v_ref, g_ref, d_ref, nm_ref, nv_ref = refs
            terms = [jnp.where(me_ref[0] == j, own_ref[...], p_ref[j]).astype(F32) for j in range(n_parts)]
        g = terms[0]
        for term in terms[1:]:
            g = g + term
        nm = ADAM_B1 * m_ref[...] + (1.0 - ADAM_B1) * g
        nv = ADAM_B2 * v_ref[...] + (1.0 - ADAM_B2) * (g * g)
        g_ref[...] = g
        nm_ref[...] = nm
        nv_ref[...] = nv
        d_ref[...] = -ADAM_LR * ((nm / c1) / (jnp.sqrt(nv / c2) + ADAM_EPS) + ADAM_WD * w_ref[...])

    row = pl.BlockSpec((tr, lanes), lambda i, *_: (i, 0))
    state = pl.BlockSpec((tr, lanes), lambda i, *_: (row0 // tr + i, 0))
    in_specs = [pl.BlockSpec((n_parts, tr, lanes), lambda i, *_: (0, i, 0)), state, state, state]
    args, n_prefetch = (parts, w, m, v), 0
    if own is not None:
        slabs, me = own
        in_specs.insert(1, pl.BlockSpec((None, tr, lanes), lambda i, me_ref: (me_ref[0], i, 0)))
        args, n_prefetch = (me, parts, slabs, w, m, v), 1
    return pl.pallas_call(
        body, name=name,
        grid_spec=pltpu.PrefetchScalarGridSpec(num_scalar_prefetch=n_prefetch, grid=(rows // tr,),
                                               in_specs=in_specs, out_specs=[row] * 4),
        out_shape=[jax.ShapeDtypeStruct((rows, lanes), F32)] * 4,
        compiler_params=_params("parallel"),
    )(*args)


MATRIX_SHARDS = (
    ("w_in", (D_MODEL, IN_PROJ_DIM // N_DEV), True),
    ("w_ssm_out", (SSM_D_INNER // N_DEV, D_MODEL), False),
    ("w_att_out", (ATT_OUT_DIM, D_MODEL // N_DEV), True),
    ("w_mix_out", (D_MODEL // N_DEV, D_MODEL), False),
    ("w_ffn_gate", (D_MODEL, D_FF // N_DEV), True),
    ("w_ffn_up", (D_MODEL, D_FF // N_DEV), True),
    ("w_ffn_down", (D_FF // N_DEV, D_MODEL), False),
)
CONV_SHARD = ("conv_w", (SSM_CONV, SSM_CONV_DIM // N_DEV), True)
SHARDED = MATRIX_SHARDS + (CONV_SHARD,)
REPLICATED = (("norm_mix", D_MODEL), ("b_gate", 2 * D_MODEL), ("conv_b", SSM_CONV_DIM), ("dt_bias", SSM_N_HEADS),
              ("a_log", SSM_N_HEADS), ("d_skip", SSM_N_HEADS), ("ssm_norm", SSM_D_INNER), ("norm_ffn", D_MODEL),
              ("norm_final", D_MODEL))


def _round_up(n, mult):
    return -(-n // mult) * mult


def _pack_rows(flat, row_mult):
    rows = _round_up(-(-flat.shape[0] // LANES), row_mult)
    return jnp.pad(flat, (0, rows * LANES - flat.shape[0])).reshape(rows, LANES)


def _stacking(specs):
    return tuple((name, (shape[1], shape[0]) if by_cols else shape, by_cols) for name, shape, by_cols in specs)


def _to_stacking(vals, specs):
    return {name: (vals[name].T if by_cols else vals[name]) for name, _, by_cols in specs}


STACK_WIDTH = D_MODEL
STACK_ALIGN = 16
STACK_ORDER = ("w_ssm_out", "w_mix_out", "w_ffn_gate", "w_ffn_up", "w_ffn_down", "w_att_out", "conv_w", "w_in")
GATHER_LATER = STACK_ORDER[:-1]
REDUCE_EARLY = STACK_ORDER[:5]
REDUCE_LATE = STACK_ORDER[5:]


def _stack_layout():
    shapes = {name: shape for name, shape, _ in _stacking(SHARDED)}
    layout, off = {}, 0
    for name in STACK_ORDER:
        r, c = shapes[name]
        rows = r if c == STACK_WIDTH else _round_up(-(-(r * c) // STACK_WIDTH), STACK_ALIGN)
        layout[name] = (off, rows, (r, c))
        off = _round_up(off + rows, STACK_ALIGN)
    return layout, _round_up(off, 1024)


def _to_stack_rows(v, rows):
    if v.shape[-1] == STACK_WIDTH:
        return v
    lead = v.shape[:-2]
    flat = v.reshape(lead + (-1,))
    flat = jnp.pad(flat, [(0, 0)] * len(lead) + [(0, rows * STACK_WIDTH - flat.shape[-1])])
    return flat.reshape(lead + (rows, STACK_WIDTH))


def _from_stack_rows(block, shape):
    r, c = shape
    if c == STACK_WIDTH:
        return block
    lead = block.shape[:-2]
    return block.reshape(lead + (-1,))[..., :r * c].reshape(lead + (r, c))


def _stack(vals, dtype, skip=(), names=STACK_ORDER):
    layout, total = _stack_layout()
    order = names
    after = STACK_ORDER.index(order[-1]) + 1
    if after < len(STACK_ORDER):
        total = layout[STACK_ORDER[after]][0]
    lead = next(iter(vals.values())).shape[:-2]
    pieces = []
    for i, name in enumerate(order):
        off, rows, _ = layout[name]
        until = layout[order[i + 1]][0] if i + 1 < len(order) else total
        piece = jnp.zeros(lead + (rows, STACK_WIDTH), dtype) if name in skip else _to_stack_rows(vals[name], rows)
        pieces.append(jnp.pad(piece.astype(dtype), [(0, 0)] * len(lead) + [(0, until - off - rows), (0, 0)]))
    return jnp.concatenate(pieces, axis=-2)


def _unstack(stacked, names):
    layout, _ = _stack_layout()
    row0 = layout[names[0]][0]
    return {name: _from_stack_rows(stacked[..., layout[name][0] - row0:layout[name][0] - row0 + layout[name][1], :],
                                   layout[name][2]) for name in names}


W_IN_SHARD_ROWS = IN_PROJ_DIM // N_DEV


def _w_in_row_moves():
    moves, orig = [], 0
    for name, size in IN_SPLIT:
        for j in range(N_DEV):
            lo, hi = max(orig, W_IN_SHARD_ROWS * j), min(orig + size, W_IN_SHARD_ROWS * (j + 1))
            if lo < hi:
                moves.append((j, lo - W_IN_SHARD_ROWS * j, DPROJ_COLS[name] + lo - orig, hi - lo))
        orig += size
    return moves


def _w_in_from_shards(shards, name):
    total, base = shards.shape[1], 0
    pad_lo, pad_hi = DPROJ_COLS["dt"] + _round_up(SSM_N_HEADS, STACK_ALIGN), DPROJ_COLS["dt"] + DPROJ_DT_WIDTH

    def body(x_ref, o_ref):
        o_ref[pad_lo:pad_hi, :] = jnp.zeros((pad_hi - pad_lo, LANES), x_ref.dtype)
        for j, r, at, n in _w_in_row_moves():
            o_ref[at:at + n, :] = x_ref[j, base + r:base + r + n, :]

    return pl.pallas_call(
        body, name=name, grid=(STACK_WIDTH // LANES,),
        in_specs=[pl.BlockSpec((N_DEV, total, LANES), lambda c: (0, 0, c))],
        out_specs=pl.BlockSpec((DPROJ_WIDTH, LANES), lambda c: (0, c)),
        out_shape=jax.ShapeDtypeStruct((DPROJ_WIDTH, STACK_WIDTH), shards.dtype),
        compiler_params=_params("parallel"),
    )(shards)


def _w_in_to_shards(dw_all, head, name):
    layout, total = _stack_layout()
    total -= layout[REDUCE_LATE[0]][0]
    base = head.shape[1]
    end = base + W_IN_SHARD_ROWS

    def body(x_ref, h_ref, o_ref):
        o_ref[:, 0:base, :] = h_ref[...]
        for j, r, at, n in _w_in_row_moves():
            o_ref[j, base + r:base + r + n, :] = x_ref[at:at + n, :]
        o_ref[:, end:total, :] = jnp.zeros((N_DEV, total - end, LANES), o_ref.dtype)

    return pl.pallas_call(
        body, name=name, grid=(STACK_WIDTH // LANES,),
        in_specs=[pl.BlockSpec((DPROJ_WIDTH, LANES), lambda c: (0, c)),
                  pl.BlockSpec((N_DEV, base, LANES), lambda c: (0, 0, c))],
        out_specs=pl.BlockSpec((N_DEV, total, LANES), lambda c: (0, 0, c)),
        out_shape=jax.ShapeDtypeStruct((N_DEV, total, STACK_WIDTH), dw_all.dtype),
        compiler_params=_params("parallel"),
    )(dw_all, head)


REPLICATED_ROWS = sum(-(-size // LANES) for _, size in REPLICATED)
LOSS_ROW = REPLICATED_ROWS


def _pack_replicated(vals):
    rows = []
    for name, size in REPLICATED:
        v = vals[name].reshape(-1).astype(F32)
        rows.append(jnp.pad(v, (0, _round_up(size, LANES) - size)))
    return _pack_rows(jnp.concatenate(rows), 8)


def _unpack_replicated(packed, shapes):
    flat = packed.reshape(-1)
    out, off = {}, 0
    for name, size in REPLICATED:
        out[name] = flat[off:off + size].reshape(shapes[name])
        off += _round_up(size, LANES)
    return out


def _lane_row(v):
    v = v.reshape(-1).astype(F32)
    return jnp.pad(v, (0, LANES - v.shape[0])).reshape(1, LANES)


IN_SPLIT = (("z", SSM_D_INNER), ("xbc", SSM_CONV_DIM), ("dt", SSM_N_HEADS), ("qkv", ATT_QKV_DIM), ("gate", 2 * D_MODEL))


def kernel(x, norm_mix, w_in, b_gate, conv_w, conv_b, dt_bias, a_log, d_skip, ssm_norm, w_ssm_out, w_att_out, w_mix_out, norm_ffn, w_ffn_gate, w_ffn_up, w_ffn_down, norm_final, loss_target, m_norm_mix, m_w_in, m_b_gate, m_conv_w, m_conv_b, m_dt_bias, m_a_log, m_d_skip, m_ssm_norm, m_w_ssm_out, m_w_att_out, m_w_mix_out, m_norm_ffn, m_w_ffn_gate, m_w_ffn_up, m_w_ffn_down, m_norm_final, v_norm_mix, v_w_in, v_b_gate, v_conv_w, v_conv_b, v_dt_bias, v_a_log, v_d_skip, v_ssm_norm, v_w_ssm_out, v_w_att_out, v_w_mix_out, v_norm_ffn, v_w_ffn_gate, v_w_ffn_up, v_w_ffn_down, v_norm_final):
    given = dict(locals())
    weights = {name: given[name][0] for name, _, _ in SHARDED}
    b, s, d = x.shape
    t = b * s

    stacking = _to_stacking(weights, SHARDED)
    conv_shape = dict((name, shape) for name, shape, _ in _stacking(SHARDED))["conv_w"]
    w_in_local = jnp.pad(stacking["w_in"].astype(BF16), ((0, -W_IN_SHARD_ROWS % STACK_ALIGN), (0, 0)))
    conv_local = _pack_rows(stacking["conv_w"].reshape(-1), 8)
    w_in_shards, conv_all = _all_gather([w_in_local, conv_local], "w_in_all_gather")
    head_local = _stack(stacking, BF16, skip=("conv_w",), names=GATHER_LATER)
    in_flight = _gather_start(head_local, conv_all, "weights_gather_start")
    w_in_all = _w_in_from_shards(w_in_shards, "w_in_from_shards")
    w_sec = {name: w_in_all[DPROJ_COLS[name]:DPROJ_COLS[name] + _round_up(size, LANES)] for name, size in IN_SPLIT}
    conv_size = conv_shape[0] * conv_shape[1]
    conv_taps = conv_all.reshape(N_DEV, -1)[:, :conv_size].reshape(N_DEV * conv_shape[0], conv_shape[1]).T

    g_mix, g_ffn, g_fin = norm_mix.reshape(1, d), norm_ffn.reshape(1, d), norm_final.reshape(1, d)
    g_mix = g_mix + in_flight[4][:1, :1]
    bg_row = b_gate.reshape(1, 2 * d)
    convb_row = conv_b.reshape(1, SSM_CONV_DIM)
    ssmn_row = ssm_norm.reshape(1, SSM_D_INNER)
    dtb_row, alog_row = _lane_row(dt_bias), _lane_row(a_log)
    cosf, sinf = _rope_tables(s)

    x2d = x.reshape(t, d)
    h1 = _rmsnorm_fwd(x2d, g_mix, "norm_mix_fwd")
    proj = {name: _mm(h1, w_sec[name], mode="nt", name="in_proj_" + name) for name, _ in IN_SPLIT}
    xbc3 = proj["xbc"].reshape(b, s, SSM_CONV_DIM)
    xc = _conv_fwd(xbc3, conv_taps, convb_row, "conv_fwd")
    dtr3 = proj["dt"].reshape(b, s, DT_PAD)
    to_channels, to_heads = _head_masks()
    dskx = jnp.repeat(d_skip.reshape(-1).astype(F32), SSM_HEAD_DIM).reshape(1, SSM_D_INNER)
    y_ssd, h_states = _ssd_fwd(xc, dtr3, dtb_row, alog_row, dskx, to_channels, "ssd_fwd")
    y_ssd2 = y_ssd.reshape(t, SSM_D_INNER)
    ynorm = _gate_norm_fwd(y_ssd2, proj["z"], ssmn_row, "ssd_gate_norm_fwd")
    landed = _gather_wait(*in_flight[:4], ynorm, "weights_gather_wait")
    head_all = lax.dynamic_update_slice(landed, head_local[None], (_my_index(), 0, 0))
    full = {name: v.reshape((-1,) + v.shape[2:]) for name, v in _unstack(head_all, STACK_ORDER[:-2]).items()}
    y_ssm = _mm(ynorm, full["w_ssm_out"], mode="nn", name="ssm_out_proj")

    qkv3 = proj["qkv"].reshape(b, s, ATT_QKV_DIM)
    qk_parts = _rope_fwd(qkv3, cosf, sinf, "rope_fwd")
    att_parts = [_att_fwd(qk_parts[gi], "att_fwd_%d" % r) for gi, r in enumerate(ATT_DILATIONS)]
    att, *lse_parts = _att_merge([o for o, _ in att_parts], [l_ for _, l_ in att_parts], "att_merge")
    att2 = att.reshape(t, ATT_OUT_DIM)
    y_att = _mm(att2, full["w_att_out"], mode="nt", name="att_out_proj")

    mixed = _mix_fwd(proj["gate"], bg_row, y_ssm, y_att, "mix_fwd")
    x2 = _mm(mixed, full["w_mix_out"], mode="nn", name="mix_out_proj", add=x2d)
    h2 = _rmsnorm_fwd(x2, g_ffn, "norm_ffn_fwd")
    gt = _mm(h2, full["w_ffn_gate"], mode="nt", name="ffn_gate_proj")
    up, act = _up_proj_swiglu(h2, full["w_ffn_up"], gt, "ffn_up_proj_swiglu")
    x3 = _mm(act, full["w_ffn_down"], mode="nn", name="ffn_down_proj", add=x2)

    loss_row, dx3, dg_fin, dx3b = _loss_head(x3, g_fin, loss_target.reshape(t, d), "loss_head")
    grads = {}
    dact = _mm(dx3b, full["w_ffn_down"], mode="nt", name="ffn_down_dx")
    grads["w_ffn_down"] = _mm(act, dx3b, mode="tn", name="ffn_down_dw", out_dtype=BF16)
    dgt, dup = _swiglu_bwd(gt, up, dact, "swiglu_bwd")
    grads["w_ffn_gate"] = _mm(dgt, h2, mode="tn", name="ffn_gate_dw", out_dtype=BF16)
    grads["w_ffn_up"] = _mm(dup, h2, mode="tn", name="ffn_up_dw", out_dtype=BF16)
    dh2 = _mm(dgt, full["w_ffn_gate"], mode="nn", name="ffn_gate_dx")
    dh2 = _mm(dup, full["w_ffn_up"], mode="nn", name="ffn_up_dx", add=dh2)
    dx2, dg_ffn, dx2b = _rmsnorm_bwd(x2, g_ffn, dh2, dx3, "norm_ffn_bwd", with_bf16=True)

    dmixed = _mm(dx2b, full["w_mix_out"], mode="nt", name="mix_out_dx")
    grads["w_mix_out"] = _mm(mixed, dx2b, mode="tn", name="mix_out_dw", out_dtype=BF16)
    dys, dya, dproj, dbg = _mix_bwd(proj["gate"], bg_row, y_ssm, y_att, dmixed, "mix_bwd")

    grads["w_ssm_out"] = _mm(ynorm, dys, mode="tn", name="ssm_out_dw", out_dtype=BF16)
    early = _stack({name: grads[name].reshape((N_DEV, -1, STACK_WIDTH)) for name in REDUCE_EARLY}, BF16,
                   names=REDUCE_EARLY)
    early_flight = _gather_start(early, dys, "grads_scatter_start")
    ssmn_row = ssmn_row + early_flight[4][:1, :1]
    dynorm = _mm(dys, full["w_ssm_out"], mode="nt", name="ssm_out_dx")
    dy_ssd, dproj, dssmn = _gate_norm_bwd(y_ssd2, proj["z"], ssmn_row, dynorm, dproj, "ssd_gate_norm_bwd")
    dxc, dproj, dalog, ddsk, ddtb = _ssd_bwd(xc, dtr3, dy_ssd.reshape(b, s, SSM_D_INNER), h_states, dtb_row, alog_row,
                                             dskx, to_channels, to_heads, dproj.reshape(b, s, DPROJ_WIDTH), "ssd_bwd")
    dproj, dconvw, dconvb = _conv_bwd(xbc3, dxc, conv_taps, convb_row, dproj, "conv_bwd")
    grads["conv_w"] = dconvw.T.astype(BF16)

    grads["w_att_out"] = _mm(dya, att2, mode="tn", name="att_out_dw", out_dtype=BF16)
    datt = _mm(dya, full["w_att_out"], mode="nn", name="att_out_dx").reshape(b, s, ATT_OUT_DIM)
    do_parts, dl_parts = _att_delta(att, datt, "att_delta")
    dqs, dks, dvs = [], [], []
    for gi, r in enumerate(ATT_DILATIONS):
        operands = (qk_parts[gi], do_parts[gi], lse_parts[gi], dl_parts[gi])
        dqs.append(_att_bwd_q(*operands, "att_bwd_q_%d" % r))
        dk_g, dv_g = _att_bwd_kv(*operands, "att_bwd_kv_%d" % r)
        dks.append(dk_g)
        dvs.append(dv_g)
    dproj = _rope_bwd(dqs, dks, dvs, cosf, sinf, dproj, "rope_bwd").reshape(t, DPROJ_WIDTH)

    dw_all = _mm(dproj, h1, mode="tn", name="in_proj_dw", out_dtype=BF16)
    head = _stack({name: grads[name].reshape((N_DEV, -1, grads[name].shape[-1])) for name in REDUCE_LATE[:-1]}, BF16,
                  names=REDUCE_LATE[:-1])
    late = _w_in_to_shards(dw_all, head, "grad_stacks")
    late_flight = _gather_start(late, dw_all, "grads_late_scatter_start")
    dh1 = _mm(dproj, w_in_all, mode="nn", name="in_proj_dx", after=late_flight[4])
    grad_x, dg_mix = _rmsnorm_bwd(x2d, g_mix, dh1, dx2, "norm_mix_bwd")

    small = {"norm_mix": dg_mix, "b_gate": dbg, "conv_b": dconvb, "dt_bias": ddtb[:, :SSM_N_HEADS],
             "a_log": dalog[:, :SSM_N_HEADS], "d_skip": ddsk[:, :SSM_N_HEADS], "ssm_norm": dssmn,
             "norm_ffn": dg_ffn, "norm_final": dg_fin}
    shared = _pack_replicated(small)
    shared = shared.at[LOSS_ROW, 0].set(loss_row[0, 0])
    got_small = _shared_exchange(shared, "shared_grads_exchange")

    def packed(prefix):
        vals = _to_stacking({name: given[prefix + name][0] for name, _, _ in SHARDED}, SHARDED)
        rep = {name: given[prefix + name] for name, _ in REPLICATED}
        return _stack(vals, F32), _pack_replicated(rep)

    (w_big, w_small), (m_big, m_small), (v_big, v_small) = packed(""), packed("m_"), packed("v_")
    me = _my_index().astype(jnp.int32).reshape(1)
    big_early = _adamw(_gather_wait(*early_flight[:4], got_small, "grads_scatter_wait"), w_big, m_big, v_big,
                       "adamw_early", own=(early, me))
    big_late = _adamw(_gather_wait(*late_flight[:4], got_small, "grads_late_scatter_wait"), w_big, m_big, v_big,
                      "adamw_late", row0=early.shape[1], own=(late, me))
    sml = _adamw(got_small, w_small, m_small, v_small, "adamw_replicated")

    outs = [sml[0][LOSS_ROW, 0], grad_x.reshape(b, s, d)]
    rep_shapes = {name: given[name].shape for name, _ in REPLICATED}
    order = ["norm_mix", "w_in", "b_gate", "conv_w", "conv_b", "dt_bias", "a_log", "d_skip", "ssm_norm", "w_ssm_out",
             "w_att_out", "w_mix_out", "norm_ffn", "w_ffn_gate", "w_ffn_up", "w_ffn_down", "norm_final"]
    for early_k, late_k, sml_k in zip(big_early, big_late, sml):
        stacks = dict(_unstack(early_k, REDUCE_EARLY), **_unstack(late_k, REDUCE_LATE))
        sharded = _to_stacking(stacks, SHARDED)
        rep = _unpack_replicated(sml_k, rep_shapes)
        for name in order:
            outs.append(sharded[name][None] if name in sharded else rep[name])
    return tuple(outs)
```

```python
import functools
import math

import jax
import jax.numpy as jnp
from jax import lax
from jax.experimental import pallas as pl
from jax.experimental.pallas import tpu as pltpu

F32 = jnp.float32
BF16 = jnp.bfloat16

N_DEV = 8
N_CHIPS = 4
D_MODEL = 1024
SSM_D_INNER = 2048
SSM_HEAD_DIM = 64
SSM_N_HEADS = 32
SSM_N_GROUPS = 4
SSM_HEADS_PER_GROUP = SSM_N_HEADS // SSM_N_GROUPS
SSM_D_STATE = 128
SSM_CONV = 4
SSM_CHUNK = 128
SSM_CONV_DIM = 3072
ATT_HEAD_DIM = 128
ATT_HEADS_PER_GROUP = 4
ATT_DILATIONS = (1, 4, 16)
ATT_N_HEADS = 12
ATT_QKV_DIM = 4608
ATT_OUT_DIM = 512
ATT_BLOCK = 128
ROPE_THETA = 10000.0
D_FF = 2816
IN_PROJ_DIM = 11808
EPS = 1e-6
LANES = 128
DT_PAD = LANES

DPROJ_COLS = {"qkv": 0, "z": 4608, "xbc": 6656, "dt": 9728, "gate": 10240}
DPROJ_DT_WIDTH = 512
DPROJ_WIDTH = 12288

ADAM_LR = 0.001
ADAM_B1 = 0.9
ADAM_B2 = 0.999
ADAM_EPS = 1e-08
ADAM_WD = 0.01
ADAM_STEP = 10

VMEM_LIMIT = 56 * 1024 * 1024
MESH = pl.DeviceIdType.MESH
NEG_INF = float("-inf")


def _tile_rows(n, cap, mult):
    return max(t for t in range(mult, min(n, cap) + 1, mult) if n % t == 0)


def _pick(n, candidates):
    for c in candidates:
        if n % c == 0:
            return c
    return n


def _params(*sem):
    return pltpu.CompilerParams(dimension_semantics=sem, vmem_limit_bytes=VMEM_LIMIT)


def _sigmoid(x):
    return 0.5 * jnp.tanh(0.5 * x) + 0.5


def _softplus(x):
    return jnp.maximum(x, 0.0) + jnp.log(1.0 + jnp.exp(-jnp.abs(x)))


def _dot(a, b, dims):
    return lax.dot_general(a.astype(BF16), b.astype(BF16), (dims, ((), ())), preferred_element_type=F32)


def _nn(a, b):
    return _dot(a, b, ((1,), (0,)))


def _nt(a, b):
    return _dot(a, b, ((1,), (1,)))


def _tn(a, b):
    return _dot(a, b, ((0,), (0,)))


def _split3(v):
    hi = v.astype(BF16)
    r1 = v - hi.astype(F32)
    mid = r1.astype(BF16)
    lo = (r1 - mid.astype(F32)).astype(BF16)
    return hi, mid, lo


def _mask_nn(mask, v):
    mb = mask.astype(BF16)
    hi, mid, lo = _split3(v)
    return _nn(mb, hi) + (_nn(mb, mid) + _nn(mb, lo))


MM_VMEM_BUDGET = 40 * 1024 * 1024
MM_FULL_K = 2816


def _mm_tiles(m, n, k, a_bytes, b_bytes, o_bytes, has_add):
    tk = k if k <= MM_FULL_K else _pick(k, (2048, 1024, 512, 256, 128))
    tn = 1408 if (n > 1024 and n % 1408 == 0) else _pick(n, (1024, 768, 512, 384, 256, 128))
    for tm in (1408, 1024, 768, 512, 384, 256, 128):
        if m % tm:
            continue
        buffers = 2 * (tm * tk * a_bytes + tk * tn * b_bytes + tm * tn * (o_bytes + (4 if has_add else 0)))
        if tk < k:
            buffers += tm * tn * 4
        if buffers <= MM_VMEM_BUDGET:
            return tm, tn, tk
    return _pick(m, (128,)), tn, tk


def _mm(a, b, *, mode, name, out_dtype=F32, add=None, after=None):
    if mode == "nn":
        (m, k), n = a.shape, b.shape[1]
    elif mode == "nt":
        (m, k), n = a.shape, b.shape[0]
    else:
        (k, m), n = a.shape, b.shape[1]
    has_add = add is not None
    tm, tn, tk = _mm_tiles(m, n, k, a.dtype.itemsize, b.dtype.itemsize, jnp.dtype(out_dtype).itemsize, has_add)
    nk = k // tk
    dims = {"nn": ((1,), (0,)), "nt": ((1,), (1,)), "tn": ((0,), (0,))}[mode]
    a_spec = {"nn": pl.BlockSpec((tm, tk), lambda i, j, kk: (i, kk)),
              "nt": pl.BlockSpec((tm, tk), lambda i, j, kk: (i, kk)),
              "tn": pl.BlockSpec((tk, tm), lambda i, j, kk: (kk, i))}[mode]
    b_spec = {"nn": pl.BlockSpec((tk, tn), lambda i, j, kk: (kk, j)),
              "nt": pl.BlockSpec((tn, tk), lambda i, j, kk: (j, kk)),
              "tn": pl.BlockSpec((tk, tn), lambda i, j, kk: (kk, j))}[mode]
    o_spec = pl.BlockSpec((tm, tn), lambda i, j, kk: (i, j))

    def finish(r, c_ref, o_ref):
        if has_add:
            r = r + c_ref[...]
        o_ref[...] = r.astype(out_dtype)

    def body_one(*refs):
        a_ref, b_ref = refs[:2]
        finish(_dot(a_ref[...], b_ref[...], dims), refs[2] if has_add else None, refs[-1])

    def body_acc(*refs):
        a_ref, b_ref = refs[:2]
        o_ref, acc = refs[-2:]
        kk = pl.program_id(2)

        @pl.when(kk == 0)
        def _():
            acc[...] = jnp.zeros_like(acc)

        acc[...] += _dot(a_ref[...], b_ref[...], dims)

        @pl.when(kk == nk - 1)
        def _():
            finish(acc[...], refs[2] if has_add else None, o_ref)

    in_specs = [a_spec, b_spec] + ([o_spec] if has_add else [])
    args = (a, b) + ((add,) if has_add else ())
    if after is not None:
        in_specs, args = in_specs + [pl.BlockSpec(memory_space=pl.ANY)], args + (after,)
    return pl.pallas_call(
        body_one if nk == 1 else body_acc, name=name, grid=(m // tm, n // tn, nk),
        in_specs=in_specs, out_specs=o_spec,
        out_shape=jax.ShapeDtypeStruct((m, n), out_dtype),
        scratch_shapes=[] if nk == 1 else [pltpu.VMEM((tm, tn), F32)],
        compiler_params=_params("parallel", "parallel", "arbitrary"),
    )(*args)


def _rmsnorm_fwd(x, g, name):
    t, d = x.shape
    tm = _pick(t, (512, 256, 128))

    def body(x_ref, g_ref, o_ref):
        xv = x_ref[...]
        r = lax.rsqrt(jnp.mean(xv * xv, axis=-1, keepdims=True) + EPS)
        o_ref[...] = ((xv * r) * g_ref[...]).astype(BF16)

    return pl.pallas_call(
        body, name=name, grid=(t // tm,),
        in_specs=[pl.BlockSpec((tm, d), lambda i: (i, 0)), pl.BlockSpec((1, d), lambda i: (0, 0))],
        out_specs=pl.BlockSpec((tm, d), lambda i: (i, 0)),
        out_shape=jax.ShapeDtypeStruct((t, d), BF16),
        compiler_params=_params("parallel"),
    )(x, g)


def _rmsnorm_bwd(x, g, dh, dres, name, with_bf16=False):
    t, d = x.shape
    tm = _pick(t, (512, 256, 128))

    def body(x_ref, g_ref, dh_ref, dres_ref, dx_ref, dg_ref, *dxb_ref):
        @pl.when(pl.program_id(0) == 0)
        def _():
            dg_ref[...] = jnp.zeros_like(dg_ref)

        xv = x_ref[...]
        r = lax.rsqrt(jnp.mean(xv * xv, axis=-1, keepdims=True) + EPS)
        xhat = xv * r
        dhv = dh_ref[...]
        dyg = dhv * g_ref[...]
        dx = dres_ref[...] + r * (dyg - xhat * jnp.mean(dyg * xhat, axis=-1, keepdims=True))
        dx_ref[...] = dx
        if with_bf16:
            dxb_ref[0][...] = dx.astype(BF16)
        dg_ref[...] += jnp.sum(dhv * xhat, axis=0, keepdims=True)

    row = pl.BlockSpec((tm, d), lambda i: (i, 0))
    vec = pl.BlockSpec((1, d), lambda i: (0, 0))
    extra = with_bf16 * [jax.ShapeDtypeStruct((t, d), BF16)]
    return pl.pallas_call(
        body, name=name, grid=(t // tm,),
        in_specs=[row, vec, row, row], out_specs=[row, vec] + with_bf16 * [row],
        out_shape=[jax.ShapeDtypeStruct((t, d), F32), jax.ShapeDtypeStruct((1, d), F32)] + extra,
        compiler_params=_params("arbitrary"),
    )(x, g, dh, dres)


def _loss_head(x, g, target, name):
    t, d = x.shape
    tm = _pick(t, (512, 256, 128))

    def body(x_ref, g_ref, t_ref, loss_ref, dx_ref, dg_ref, dxb_ref):
        @pl.when(pl.program_id(0) == 0)
        def _():
            dg_ref[...] = jnp.zeros_like(dg_ref)
            loss_ref[...] = jnp.zeros_like(loss_ref)

        xv = x_ref[...]
        gv = g_ref[...]
        r = lax.rsqrt(jnp.mean(xv * xv, axis=-1, keepdims=True) + EPS)
        xhat = xv * r
        err = xhat * gv - t_ref[...]
        loss_ref[...] += jnp.sum(err * err) * (0.5 / d)
        dy = err * (1.0 / d)
        dyg = dy * gv
        dx = r * (dyg - xhat * jnp.mean(dyg * xhat, axis=-1, keepdims=True))
        dx_ref[...] = dx
        dxb_ref[...] = dx.astype(BF16)
        dg_ref[...] += jnp.sum(dy * xhat, axis=0, keepdims=True)

    row = pl.BlockSpec((tm, d), lambda i: (i, 0))
    vec = pl.BlockSpec((1, d), lambda i: (0, 0))
    return pl.pallas_call(
        body, name=name, grid=(t // tm,),
        in_specs=[row, vec, row],
        out_specs=[pl.BlockSpec((1, LANES), lambda i: (0, 0)), row, vec, row],
        out_shape=[jax.ShapeDtypeStruct((1, LANES), F32), jax.ShapeDtypeStruct((t, d), F32),
                   jax.ShapeDtypeStruct((1, d), F32), jax.ShapeDtypeStruct((t, d), BF16)],
        compiler_params=_params("arbitrary"),
    )(x, g, target)


CONV_HALO = 8
CONV_ROWS = 64


def _conv_taps(window, wv, bv):
    acc = bv + wv[SSM_CONV - 1:SSM_CONV, :] * window(0)
    for sh in range(1, SSM_CONV):
        kidx = SSM_CONV - 1 - sh
        acc = acc + wv[kidx:kidx + 1, :] * window(sh)
    return acc


def _conv_fwd(u, w, bias, name):
    b, s, c = u.shape
    rows = CONV_ROWS

    def body(u_ref, w_ref, b_ref, o_ref, ext):
        ext[0:CONV_HALO, :] = jnp.zeros((CONV_HALO, LANES), F32)
        ext[CONV_HALO:, :] = u_ref[...]
        wv, bv = w_ref[...], b_ref[...]
        for r0 in range(0, s, rows):
            acc = _conv_taps(lambda sh: ext[CONV_HALO + r0 - sh:CONV_HALO + r0 - sh + rows, :], wv, bv)
            o_ref[r0:r0 + rows, :] = acc * _sigmoid(acc)

    strip = pl.BlockSpec((None, s, LANES), lambda bi, j: (bi, 0, j))
    return pl.pallas_call(
        body, name=name, grid=(b, c // LANES),
        in_specs=[strip, pl.BlockSpec((SSM_CONV, LANES), lambda bi, j: (0, j)),
                  pl.BlockSpec((1, LANES), lambda bi, j: (0, j))],
        out_specs=strip, out_shape=jax.ShapeDtypeStruct((b, s, c), F32),
        scratch_shapes=[pltpu.VMEM((CONV_HALO + s, LANES), F32)],
        compiler_params=_params("parallel", "parallel"),
    )(u, w, bias)


def _conv_bwd(u, dout, w, bias, dproj, name):
    b, s, c = u.shape
    rows = CONV_ROWS

    def fold(v):
        return jnp.sum(v.reshape(rows // CONV_HALO, CONV_HALO, LANES), axis=0)

    def body(u_ref, d_ref, w_ref, b_ref, buf_ref, du_ref, dw_ref, db_ref, ext, dpre):
        @pl.when(pl.program_id(1) == 0)
        def _():
            dw_ref[...] = jnp.zeros_like(dw_ref)
            db_ref[...] = jnp.zeros_like(db_ref)

        ext[0:CONV_HALO, :] = jnp.zeros((CONV_HALO, LANES), F32)
        ext[CONV_HALO:, :] = u_ref[...]
        dpre[s:, :] = jnp.zeros((CONV_HALO, LANES), F32)
        wv, bv = w_ref[...], b_ref[...]
        sums = [jnp.zeros((CONV_HALO, LANES), F32)] * (SSM_CONV + 1)
        for r0 in range(0, s, rows):
            window = lambda sh: ext[CONV_HALO + r0 - sh:CONV_HALO + r0 - sh + rows, :]
            acc = _conv_taps(window, wv, bv)
            sg = _sigmoid(acc)
            dp = d_ref[r0:r0 + rows, :] * (sg * (1.0 + acc * (1.0 - sg)))
            dpre[r0:r0 + rows, :] = dp
            taps = [sums[SSM_CONV - 1 - sh] + fold(dp * window(sh)) for sh in range(SSM_CONV)]
            sums = taps[::-1] + [sums[SSM_CONV] + fold(dp)]
        for r0 in range(0, s, rows):
            du = wv[SSM_CONV - 1:SSM_CONV, :] * dpre[r0:r0 + rows, :]
            for sh in range(1, SSM_CONV):
                kidx = SSM_CONV - 1 - sh
                du = du + wv[kidx:kidx + 1, :] * dpre[r0 + sh:r0 + sh + rows, :]
            du_ref[r0:r0 + rows, :] = du.astype(BF16)
        for kidx in range(SSM_CONV):
            dw_ref[kidx:kidx + 1, :] += jnp.sum(sums[kidx], axis=0, keepdims=True)
        db_ref[...] += jnp.sum(sums[SSM_CONV], axis=0, keepdims=True)

    strip = pl.BlockSpec((None, s, LANES), lambda j, bi: (bi, 0, j))
    taps = pl.BlockSpec((SSM_CONV, LANES), lambda j, bi: (0, j))
    vec = pl.BlockSpec((1, LANES), lambda j, bi: (0, j))
    du_cols = pl.BlockSpec((None, s, LANES), lambda j, bi: (bi, 0, DPROJ_COLS["xbc"] // LANES + j))
    return pl.pallas_call(
        body, name=name, grid=(c // LANES, b),
        in_specs=[strip, strip, taps, vec, pl.BlockSpec(memory_space=pl.ANY)], out_specs=[du_cols, taps, vec],
        input_output_aliases={4: 0},
        out_shape=[jax.ShapeDtypeStruct(dproj.shape, dproj.dtype), jax.ShapeDtypeStruct((SSM_CONV, c), F32),
                   jax.ShapeDtypeStruct((1, c), F32)],
        scratch_shapes=[pltpu.VMEM((CONV_HALO + s, LANES), F32), pltpu.VMEM((s + CONV_HALO, LANES), F32)],
        compiler_params=_params("parallel", "arbitrary"),
    )(u, dout, w, bias, dproj)


def _ssd_chunk_terms(dtr_ref, bias_ref, alog_ref):
    q = SSM_CHUNK
    dt = _softplus(dtr_ref[...] + bias_ref[...])
    a_neg = -jnp.exp(alog_ref[...])
    row = lax.broadcasted_iota(jnp.int32, (q, q), 0)
    col = lax.broadcasted_iota(jnp.int32, (q, q), 1)
    lower = row >= col
    s = _mask_nn(lower, dt * a_neg)
    return dt, a_neg, s, s.T, lower


def _head_masks():
    heads = jnp.arange(LANES)[:, None]
    chans = jnp.arange(SSM_D_INNER)[None, :]
    to_channels = (chans // SSM_HEAD_DIM == heads).astype(BF16)
    return to_channels, to_channels.T


def _per_channel(v, to_channels):
    hi = v.astype(BF16)
    lo = (v - hi.astype(F32)).astype(BF16)
    return _nn(hi, to_channels) + _nn(lo, to_channels)


def _per_head(v, to_heads):
    hi = v.astype(BF16)
    lo = (v - hi.astype(F32)).astype(BF16)
    return _nn(hi, to_heads) + _nn(lo, to_heads)


def _decay_terms_per_channel(dt, s_col, to_channels):
    q = SSM_CHUNK
    tot = s_col[q - 1:q, :]
    stacked = jnp.concatenate([dt, jnp.exp(s_col), jnp.exp(tot - s_col)], axis=0)
    wide = _per_channel(stacked, to_channels)
    dtx, esx, decx = wide[:q], wide[q:2 * q], wide[2 * q:]
    return dtx, esx, decx, esx[0:1, :] * decx[0:1, :]


SSM_PAIRS_PER_GROUP = SSM_HEADS_PER_GROUP // 2
SSM_GROUP_CHANNELS = SSM_HEADS_PER_GROUP * SSM_HEAD_DIM


def _split_pair(v):
    first = lax.broadcasted_iota(jnp.int32, v.shape, 1) < SSM_HEAD_DIM
    return jnp.concatenate([jnp.where(first, v, 0.0), jnp.where(first, 0.0, v)], axis=0)


def _ssd_fwd(xc, dtr, dt_bias, a_log, dskx, to_channels, name):
    b, s, _ = xc.shape
    q = SSM_CHUNK
    nc = s // q
    n, gc = SSM_D_STATE, SSM_GROUP_CHANNELS

    def body(xc_ref, dtr_ref, bias_ref, alog_ref, dsk_ref, tc_ref, y_ref, hs_ref, h_scr):
        @pl.when(pl.program_id(1) == 0)
        def _():
            h_scr[...] = jnp.zeros_like(h_scr)

        dt, _, s_col, s_row, lower = _ssd_chunk_terms(dtr_ref, bias_ref, alog_ref)
        dtx, esx, decx, etotx = _decay_terms_per_channel(dt, s_col, tc_ref[...])
        x = xc_ref[:, :SSM_D_INNER]
        xdt = x * dtx
        xdec = xdt * decx
        skip = dsk_ref[...] * x
        for g in range(SSM_N_GROUPS):
            bg = xc_ref[:, SSM_D_INNER + n * g:SSM_D_INNER + n * (g + 1)].astype(BF16)
            cg = xc_ref[:, SSM_D_INNER + n * (SSM_N_GROUPS + g):SSM_D_INNER + n * (SSM_N_GROUPS + g + 1)].astype(BF16)
            gsl = slice(gc * g, gc * (g + 1))
            gm = _nt(cg, bg)
            hgt = h_scr[:, gsl]
            hs_ref[:, gsl] = hgt
            y_off = esx[:, gsl] * _nn(cg, hgt)
            h_scr[:, gsl] = etotx[:, gsl] * hgt + _tn(bg, xdec[:, gsl])
            for k in range(SSM_PAIRS_PER_GROUP):
                h0 = g * SSM_HEADS_PER_GROUP + 2 * k
                lo = gc * g + LANES * k
                ms = []
                for h in (h0, h0 + 1):
                    lm = jnp.exp(jnp.where(lower, s_col[:, h:h + 1] - s_row[h:h + 1, :], NEG_INF))
                    ms.append((gm * lm).astype(BF16))
                y_diag = _nn(jnp.concatenate(ms, axis=1), _split_pair(xdt[:, lo:lo + LANES]))
                y_ref[:, lo:lo + LANES] = y_diag + y_off[:, LANES * k:LANES * (k + 1)] + skip[:, lo:lo + LANES]

    vec = pl.BlockSpec((1, LANES), lambda bi, c: (0, 0))
    return pl.pallas_call(
        body, name=name, grid=(b, nc),
        in_specs=[pl.BlockSpec((None, q, SSM_CONV_DIM), lambda bi, c: (bi, c, 0)),
                  pl.BlockSpec((None, q, LANES), lambda bi, c: (bi, c, 0)), vec, vec,
                  pl.BlockSpec((1, SSM_D_INNER), lambda bi, c: (0, 0)),
                  pl.BlockSpec((LANES, SSM_D_INNER), lambda bi, c: (0, 0))],
        out_specs=[pl.BlockSpec((None, q, SSM_D_INNER), lambda bi, c: (bi, c, 0)),
                   pl.BlockSpec((None, None, n, SSM_D_INNER), lambda bi, c: (bi, c, 0, 0))],
        out_shape=[jax.ShapeDtypeStruct((b, s, SSM_D_INNER), F32),
                   jax.ShapeDtypeStruct((b, nc, n, SSM_D_INNER), F32)],
        scratch_shapes=[pltpu.VMEM((n, SSM_D_INNER), F32)],
        compiler_params=_params("parallel", "arbitrary"),
    )(xc, dtr, dt_bias, a_log, dskx, to_channels)


def _ssd_bwd(xc, dtr, dy, hs, dt_bias, a_log, dskx, to_channels, to_heads, dproj, name):
    b, s, _ = xc.shape
    q = SSM_CHUNK
    nc = s // q
    n, gc = SSM_D_STATE, SSM_GROUP_CHANNELS

    def colsum(v):
        return jnp.sum(v, axis=0, keepdims=True)

    def body(xc_ref, dtr_ref, dy_ref, hs_ref, bias_ref, alog_ref, dsk_ref, tc_ref, th_ref, buf_ref,
             dxc_ref, ddtr_ref, dalog_ref, ddsk_ref, dbias_ref, dh_scr, dxs_scr, dxd_scr, w_scr, dst_scr, rows_scr):
        ci = pl.program_id(1)

        @pl.when(ci == 0)
        def _():
            dh_scr[...] = jnp.zeros_like(dh_scr)

        @pl.when(jnp.logical_and(pl.program_id(0) == 0, ci == 0))
        def _():
            dalog_ref[...] = jnp.zeros_like(dalog_ref)
            ddsk_ref[...] = jnp.zeros_like(ddsk_ref)
            dbias_ref[...] = jnp.zeros_like(dbias_ref)
            dst_scr[...] = jnp.zeros_like(dst_scr)

        dt, a_neg, s_col, s_row, lower = _ssd_chunk_terms(dtr_ref, bias_ref, alog_ref)
        upper = jnp.logical_not(lower) | (lax.broadcasted_iota(jnp.int32, (q, q), 0)
                                          == lax.broadcasted_iota(jnp.int32, (q, q), 1))
        dtx, esx, decx, etotx = _decay_terms_per_channel(dt, s_col, tc_ref[...])
        x = xc_ref[:, :SSM_D_INNER]
        dyv = dy_ref[...]
        xdt = x * dtx
        xdec = xdt * decx
        dw = esx * dyv
        rows_scr[...] = jnp.zeros_like(rows_scr)
        for g in range(SSM_N_GROUPS):
            b_lo = SSM_D_INNER + n * g
            c_lo = SSM_D_INNER + n * (SSM_N_GROUPS + g)
            bg = xc_ref[:, b_lo:b_lo + n].astype(BF16)
            cg = xc_ref[:, c_lo:c_lo + n].astype(BF16)
            gsl = slice(gc * g, gc * (g + 1))
            gm = _nt(cg, bg)
            gmt = _nt(bg, cg)
            hgt = hs_ref[:, gsl]
            dhgt = dh_scr[:, gsl]
            w_scr[:, gsl] = _nn(cg, hgt)
            dcg = _nt(dw[:, gsl], hgt)
            dxs = decx[:, gsl] * _nn(bg, dhgt)
            dxs_scr[:, gsl] = dxs
            dbg = _nt(xdec[:, gsl], dhgt)
            rows_scr[2:3, gsl] = colsum(dhgt * hgt)
            dh_scr[:, gsl] = _tn(cg, dw[:, gsl]) + etotx[:, gsl] * dhgt
            dg = jnp.zeros((q, q), F32)
            dgt = jnp.zeros((q, q), F32)
            for k in range(SSM_PAIRS_PER_GROUP):
                h0 = g * SSM_HEADS_PER_GROUP + 2 * k
                lo = gc * g + LANES * k
                xp = xdt[:, lo:lo + LANES]
                dyp = dyv[:, lo:lo + LANES]
                dy2 = _split_pair(dyp)
                dm2 = _nt(dy2, xp)
                dmt2 = _nt(_split_pair(xp), dyp)
                mts = []
                for i, h in enumerate((h0, h0 + 1)):
                    lm = jnp.exp(jnp.where(lower, s_col[:, h:h + 1] - s_row[h:h + 1, :], NEG_INF))
                    lmt = jnp.exp(jnp.where(upper, s_row[h:h + 1, :] - s_col[:, h:h + 1], NEG_INF))
                    dm = dm2[q * i:q * (i + 1), :]
                    dmt = dmt2[q * i:q * (i + 1), :]
                    dg = dg + dm * lm
                    dgt = dgt + dmt * lmt
                    mt = gmt * lmt
                    dst_scr[h:h + 1, :] = colsum(dmt * mt) - colsum(dm * (gm * lm))
                    mts.append(mt.astype(BF16))
                dxd_scr[:, lo:lo + LANES] = _nn(jnp.concatenate(mts, axis=1), dy2)
            dxc_ref[:, b_lo:b_lo + n] = dbg + _nn(dgt, cg)
            dxc_ref[:, c_lo:c_lo + n] = dcg + _nn(dg, bg)
        dxs = dxs_scr[...]
        dxdt = dxd_scr[...] + dxs
        dxc_ref[:, :SSM_D_INNER] = dxdt * dtx + dsk_ref[...] * dyv
        state_part = xdt * dxs
        rows_scr[0:1, :] = colsum(dyv * x)
        rows_scr[1:2, :] = colsum(state_part)
        th = th_ref[...]
        per_head = _per_head(jnp.concatenate([dw * w_scr[...] - state_part, dxdt * x], axis=0), th)
        r_ds, r_dt = per_head[:q], per_head[q:]
        sums = _per_head(rows_scr[...], th)
        etot = jnp.exp(s_col[q - 1:q, :])
        dtot = sums[1:2, :] + etot * sums[2:3, :]
        last = lax.broadcasted_iota(jnp.int32, (q, LANES), 0) == q - 1
        ds = dst_scr[...].T + r_ds + jnp.where(last, dtot, 0.0)
        da = _mask_nn(upper, ds)
        ddt = da * a_neg + r_dt
        live = lax.broadcasted_iota(jnp.int32, (1, LANES), 1) < SSM_N_HEADS
        sg = _sigmoid(dtr_ref[...] + bias_ref[...])
        ddtr = jnp.where(live, ddt * sg, 0.0)
        ddtr_ref[:, :LANES] = ddtr.astype(BF16)
        ddtr_ref[:, LANES:] = jnp.zeros((q, DPROJ_DT_WIDTH - LANES), BF16)
        dalog_ref[...] += jnp.where(live, colsum(da * dt) * a_neg, 0.0)
        ddsk_ref[...] += jnp.where(live, sums[0:1, :], 0.0)
        dbias_ref[...] += colsum(ddtr)

    rev = lambda bi, c: (bi, nc - 1 - c, 0)
    vec = pl.BlockSpec((1, LANES), lambda bi, c: (0, 0))
    wide = pl.BlockSpec((None, q, SSM_D_INNER), rev)
    return pl.pallas_call(
        body, name=name, grid=(b, nc),
        in_specs=[pl.BlockSpec((None, q, SSM_CONV_DIM), rev), pl.BlockSpec((None, q, LANES), rev), wide,
                  pl.BlockSpec((None, None, n, SSM_D_INNER), lambda bi, c: (bi, nc - 1 - c, 0, 0)),
                  vec, vec, pl.BlockSpec((1, SSM_D_INNER), lambda bi, c: (0, 0)),
                  pl.BlockSpec((LANES, SSM_D_INNER), lambda bi, c: (0, 0)),
                  pl.BlockSpec((SSM_D_INNER, LANES), lambda bi, c: (0, 0)),
                  pl.BlockSpec(memory_space=pl.ANY)],
        out_specs=[pl.BlockSpec((None, q, SSM_CONV_DIM), rev),
                   pl.BlockSpec((None, q, DPROJ_DT_WIDTH),
                                lambda bi, c: (bi, nc - 1 - c, DPROJ_COLS["dt"] // DPROJ_DT_WIDTH)), vec, vec, vec],
        input_output_aliases={9: 1},
        out_shape=[jax.ShapeDtypeStruct((b, s, SSM_CONV_DIM), F32), jax.ShapeDtypeStruct(dproj.shape, dproj.dtype),
                   jax.ShapeDtypeStruct((1, LANES), F32), jax.ShapeDtypeStruct((1, LANES), F32),
                   jax.ShapeDtypeStruct((1, LANES), F32)],
        scratch_shapes=[pltpu.VMEM((n, SSM_D_INNER), F32)] + [pltpu.VMEM((q, SSM_D_INNER), F32)] * 3
        + [pltpu.VMEM((LANES, q), F32), pltpu.VMEM((8, SSM_D_INNER), F32)],
        compiler_params=_params("arbitrary", "arbitrary"),
    )(xc, dtr, dy, hs, dt_bias, a_log, dskx, to_channels, to_heads, dproj)


SSM_GROUP_WIDTH = SSM_D_INNER // SSM_N_GROUPS


def _gate_norm_fwd(y, z, w, name):
    t, d = y.shape
    tm = _pick(t, (256, 128))

    def body(y_ref, z_ref, w_ref, o_ref):
        for g in range(SSM_N_GROUPS):
            sl = slice(SSM_GROUP_WIDTH * g, SSM_GROUP_WIDTH * (g + 1))
            zv = z_ref[:, sl]
            u = y_ref[:, sl] * (zv * _sigmoid(zv))
            r = lax.rsqrt(jnp.mean(u * u, axis=-1, keepdims=True) + EPS)
            o_ref[:, sl] = ((u * r) * w_ref[:, sl]).astype(BF16)

    row = pl.BlockSpec((tm, d), lambda i: (i, 0))
    return pl.pallas_call(
        body, name=name, grid=(t // tm,),
        in_specs=[row, row, pl.BlockSpec((1, d), lambda i: (0, 0))], out_specs=row,
        out_shape=jax.ShapeDtypeStruct((t, d), BF16),
        compiler_params=_params("parallel"),
    )(y, z, w)


def _gate_norm_bwd(y, z, w, dout, dproj, name):
    t, d = y.shape
    gw = SSM_GROUP_WIDTH
    tm = _pick(t, (1024, 512, 256, 128))

    def body(y_ref, z_ref, w_ref, do_ref, buf_ref, dy_ref, dz_ref, dw_ref):
        @pl.when(pl.program_id(1) == 0)
        def _():
            dw_ref[...] = jnp.zeros_like(dw_ref)

        zv = z_ref[...]
        yv = y_ref[...]
        sg = _sigmoid(zv)
        silu = zv * sg
        u = yv * silu
        r = lax.rsqrt(jnp.mean(u * u, axis=-1, keepdims=True) + EPS)
        uh = u * r
        dov = do_ref[...]
        dw_ref[...] += jnp.sum(dov * uh, axis=0, keepdims=True)
        dyg = dov * w_ref[...]
        du = r * (dyg - uh * jnp.mean(dyg * uh, axis=-1, keepdims=True))
        dy_ref[...] = du * silu
        dz_ref[...] = (du * yv * (sg * (1.0 + zv * (1.0 - sg)))).astype(BF16)

    tile = pl.BlockSpec((tm, gw), lambda g, i: (i, g))
    vec = pl.BlockSpec((1, gw), lambda g, i: (0, g))
    z_cols = pl.BlockSpec((tm, gw), lambda g, i: (i, DPROJ_COLS["z"] // gw + g))
    return pl.pallas_call(
        body, name=name, grid=(SSM_N_GROUPS, t // tm),
        in_specs=[tile, tile, vec, tile, pl.BlockSpec(memory_space=pl.ANY)], out_specs=[tile, z_cols, vec],
        out_shape=[jax.ShapeDtypeStruct((t, d), F32), jax.ShapeDtypeStruct(dproj.shape, dproj.dtype),
                   jax.ShapeDtypeStruct((1, d), F32)],
        input_output_aliases={4: 1},
        compiler_params=_params("parallel", "arbitrary"),
    )(y, z, w, dout, dproj)


def _rope_tables(s):
    half = ATT_HEAD_DIM // 2
    inv = ROPE_THETA ** (-jnp.arange(half, dtype=F32) / half)
    ang = jnp.arange(s).astype(F32)[:, None] * inv[None, :]
    cos, sin = jnp.cos(ang), jnp.sin(ang)
    return jnp.concatenate([cos, cos], axis=-1), jnp.concatenate([-sin, sin], axis=-1)


ATT_TILE = 256


def _by_residue_spec(r, width):
    return pl.BlockSpec((None, r, ATT_TILE // r, width), lambda bi, i: (bi, 0, i, 0))


def _to_residues(tile, stage, r, store):
    if r == 1:
        store(0, tile)
        return
    stage[...] = tile
    for ri in range(r):
        store(ri, stage[pl.ds(ri, tile.shape[0] // r, stride=r), :])


def _from_residues(load, stage, r):
    if r == 1:
        return load(0)
    for ri in range(r):
        stage[pl.ds(ri, ATT_TILE // r, stride=r), :] = load(ri)
    return stage[...]


QKV_ROWS = 1024
QKV_COLS = 768


def _qkv_proj_rope(h, w_qkv_t, cosf, sinf, b, s, name):
    t, k = h.shape
    tm, d, gw = QKV_ROWS, ATT_HEAD_DIM, ATT_OUT_DIM
    per_seq = s // tm

    def body(h_ref, w_ref, c_ref, s_ref, *rest):
        outs, stage = rest[:-1], rest[-1]
        cv, sv = c_ref[...], s_ref[...]
        hv = h_ref[...]
        for lo in range(0, ATT_QKV_DIM, QKV_COLS):
            acc = _nt(hv, w_ref[lo:lo + QKV_COLS, :])
            for hh in range(QKV_COLS // d):
                kind, head = divmod(lo // d + hh, ATT_N_HEADS)
                gi, j = divmod(head, ATT_HEADS_PER_GROUP)
                dst = slice(kind * gw + d * j, kind * gw + d * (j + 1))
                tv = acc[:, d * hh:d * (hh + 1)]
                if kind < 2:
                    tv = tv * cv + pltpu.roll(tv, d // 2, 1) * sv

                def store(ri, rows, o_ref=outs[gi], dst=dst):
                    o_ref[ri, :, dst] = rows.astype(BF16)

                _to_residues(tv, stage, ATT_DILATIONS[gi], store)

    tab = pl.BlockSpec((tm, d), lambda i: (i % per_seq, 0))
    return pl.pallas_call(
        body, name=name, grid=(t // tm,),
        in_specs=[pl.BlockSpec((tm, k), lambda i: (i, 0)), pl.BlockSpec((ATT_QKV_DIM, k), lambda i: (0, 0)), tab, tab],
        out_specs=[pl.BlockSpec((None, r, tm // r, 3 * gw), lambda i: (i // per_seq, 0, i % per_seq, 0))
                   for r in ATT_DILATIONS],
        out_shape=[jax.ShapeDtypeStruct((b, r, s // r, 3 * gw), BF16) for r in ATT_DILATIONS],
        scratch_shapes=[pltpu.VMEM((tm, d), F32)],
        compiler_params=_params("parallel"),
    )(h, w_qkv_t, cosf, sinf)


def _rope_bwd(dq, dk, dv, cosf, sinf, dproj, name):
    n_pat = len(ATT_DILATIONS)
    b, _, s, gw = dq[0].shape
    ts, d = ATT_TILE, ATT_HEAD_DIM

    def body(*refs):
        ins, (c_ref, s_ref, _, o_ref, stage) = refs[:3 * n_pat], refs[3 * n_pat:]
        cv, sv = c_ref[...], s_ref[...]
        for kind in range(3):
            for gi, r in enumerate(ATT_DILATIONS):
                src = ins[kind * n_pat + gi]
                for j in range(ATT_HEADS_PER_GROUP):
                    tv = _from_residues(lambda ri, src=src, j=j: src[ri, :, d * j:d * (j + 1)], stage, r)
                    if kind < 2:
                        tv = tv * cv + pltpu.roll(tv * sv, d // 2, 1)
                    lo = d * (kind * ATT_N_HEADS + gi * ATT_HEADS_PER_GROUP + j)
                    o_ref[:, lo:lo + d] = tv.astype(BF16)

    tab = pl.BlockSpec((ts, d), lambda bi, i: (i, 0))
    parts = [_by_residue_spec(r, gw) for r in ATT_DILATIONS]
    return pl.pallas_call(
        body, name=name, grid=(b, s // ts), in_specs=parts * 3 + [tab, tab, pl.BlockSpec(memory_space=pl.ANY)],
        out_specs=pl.BlockSpec((None, ts, ATT_QKV_DIM), lambda bi, i: (bi, i, DPROJ_COLS["qkv"] // ATT_QKV_DIM)),
        out_shape=jax.ShapeDtypeStruct(dproj.shape, dproj.dtype),
        input_output_aliases={3 * n_pat + 2: 0},
        scratch_shapes=[pltpu.VMEM((ts, d), F32)],
        compiler_params=_params("parallel", "parallel"),
    )(*dq, *dk, *dv, cosf, sinf, dproj)


ATT_SCALE = ATT_HEAD_DIM ** -0.5
ATT_STEP = 2 * ATT_BLOCK


def _att_spec(col):
    return pl.BlockSpec((None, None, ATT_STEP, ATT_OUT_DIM), lambda bi, ri, i: (bi, ri, i, col))


def _att_edge_spec(col, side, n_steps):
    def index(bi, ri, i):
        blk = 2 * i - 1 if side < 0 else 2 * i + 2
        return (bi, ri, jnp.clip(blk, 0, 2 * n_steps - 1), col)
    return pl.BlockSpec((None, None, ATT_BLOCK, ATT_OUT_DIM), index)


def _band_mask(shape, q_axis, has_prev):
    qi = lax.broadcasted_iota(jnp.int32, shape, q_axis)
    kj = lax.broadcasted_iota(jnp.int32, shape, 1 - q_axis)
    dist = qi + ATT_BLOCK - kj
    return (dist >= 0) & (dist <= ATT_BLOCK) & (has_prev | (kj >= ATT_BLOCK))


def _att_fwd(qkr, name):
    b, r, l, _ = qkr.shape
    nb = l // ATT_STEP
    d = ATT_HEAD_DIM

    def body(q_ref, kp_ref, k_ref, vp_ref, v_ref, o_ref, lse_ref):
        mask = _band_mask((ATT_STEP, ATT_BLOCK + ATT_STEP), 0, pl.program_id(2) > 0)
        heads = [slice(d * j, d * (j + 1)) for j in range(ATT_HEADS_PER_GROUP)]
        scores = [_nt(q_ref[:, sl], jnp.concatenate([kp_ref[:, sl], k_ref[:, sl]], axis=0)) for sl in heads]
        scores = [jnp.where(mask, sc * ATT_SCALE, NEG_INF) for sc in scores]
        tops = [jnp.max(sc, axis=-1, keepdims=True) for sc in scores]
        probs = [jnp.exp(sc - m) for sc, m in zip(scores, tops)]
        dens = [jnp.sum(pr, axis=-1, keepdims=True) for pr in probs]
        for sl, m, pr, den in zip(heads, tops, probs, dens):
            o_ref[:, sl] = _nn(pr / den, jnp.concatenate([vp_ref[:, sl], v_ref[:, sl]], axis=0))
            lse_ref[:, sl] = jnp.broadcast_to(m + jnp.log(den), (ATT_STEP, d))

    out_spec = _att_spec(0)
    return pl.pallas_call(
        body, name=name, grid=(b, r, nb),
        in_specs=[_att_spec(0), _att_edge_spec(1, -1, nb), _att_spec(1), _att_edge_spec(2, -1, nb), _att_spec(2)],
        out_specs=[out_spec, out_spec],
        out_shape=[jax.ShapeDtypeStruct((b, r, l, ATT_OUT_DIM), F32)] * 2,
        compiler_params=_params("parallel", "parallel", "parallel"),
    )(qkr, qkr, qkr, qkr, qkr)


def _att_merge(os_, lses, name):
    n_pat = len(os_)
    b, _, s, gw = os_[0].shape
    ts, d = ATT_TILE, ATT_HEAD_DIM

    def body(*refs):
        o_refs, l_refs = refs[:n_pat], refs[n_pat:2 * n_pat]
        att_ref, lse_outs, stage = refs[2 * n_pat], refs[2 * n_pat + 1:3 * n_pat + 1], refs[-1]
        for j in range(ATT_HEADS_PER_GROUP):
            sl = slice(d * j, d * (j + 1))
            ov = [_from_residues(lambda ri, g=g: o_refs[g][ri, :, sl], stage, r)
                  for g, r in enumerate(ATT_DILATIONS)]
            ls = [_from_residues(lambda ri, g=g: l_refs[g][ri, :, sl], stage, r)
                  for g, r in enumerate(ATT_DILATIONS)]
            m = functools.reduce(jnp.maximum, ls)
            es = [jnp.exp(lv - m) for lv in ls]
            tot = functools.reduce(lambda u, v: u + v, es)
            acc = (es[0] / tot) * ov[0]
            for g in range(1, n_pat):
                acc = acc + (es[g] / tot) * ov[g]
            att_ref[:, sl] = acc
            joint = m + jnp.log(tot)
            for g, r in enumerate(ATT_DILATIONS):
                def store(ri, rows, out=lse_outs[g]):
                    out[ri, :, sl] = rows
                _to_residues(joint, stage, r, store)

    parts = [_by_residue_spec(r, gw) for r in ATT_DILATIONS]
    return pl.pallas_call(
        body, name=name, grid=(b, s // ts), in_specs=parts * 2,
        out_specs=[pl.BlockSpec((None, ts, gw), lambda bi, i: (bi, i, 0))] + parts,
        out_shape=[jax.ShapeDtypeStruct((b, s, gw), F32)]
        + [jax.ShapeDtypeStruct((b, r, s // r, gw), F32) for r in ATT_DILATIONS],
        scratch_shapes=[pltpu.VMEM((ts, d), F32)],
        compiler_params=_params("parallel", "parallel"),
    )(*os_, *lses)


def _att_delta(att, datt, name):
    b, s, gw = att.shape
    ts, d = ATT_TILE, ATT_HEAD_DIM
    n_pat = len(ATT_DILATIONS)

    def body(a_ref, d_ref, *rest):
        do_outs, dl_outs, stage = rest[:n_pat], rest[n_pat:2 * n_pat], rest[-1]
        for j in range(ATT_HEADS_PER_GROUP):
            sl = slice(d * j, d * (j + 1))
            dv = d_ref[:, sl]
            delta = jnp.broadcast_to(jnp.sum(a_ref[:, sl] * dv, axis=-1, keepdims=True), (ts, d))
            for g, r in enumerate(ATT_DILATIONS):
                def store_do(ri, rows, out=do_outs[g]):
                    out[ri, :, sl] = rows.astype(BF16)

                def store_dl(ri, rows, out=dl_outs[g]):
                    out[ri, :, sl] = rows

                _to_residues(dv, stage, r, store_do)
                _to_residues(delta, stage, r, store_dl)

    row = pl.BlockSpec((None, ts, gw), lambda bi, i: (bi, i, 0))
    parts = [_by_residue_spec(r, gw) for r in ATT_DILATIONS]
    outs = pl.pallas_call(
        body, name=name, grid=(b, s // ts), in_specs=[row, row], out_specs=parts * 2,
        out_shape=[jax.ShapeDtypeStruct((b, r, s // r, gw), BF16) for r in ATT_DILATIONS]
        + [jax.ShapeDtypeStruct((b, r, s // r, gw), F32) for r in ATT_DILATIONS],
        scratch_shapes=[pltpu.VMEM((ts, d), F32)],
        compiler_params=_params("parallel", "parallel"),
    )(att, datt)
    return outs[:n_pat], outs[n_pat:]


def _att_bwd_q(qkr, datt, lse, delta, name):
    b, r, l, _ = qkr.shape
    nb = l // ATT_STEP
    d = ATT_HEAD_DIM

    def body(q_ref, kp_ref, k_ref, vp_ref, v_ref, do_ref, lse_ref, dl_ref, dq_ref):
        mask = _band_mask((ATT_STEP, ATT_BLOCK + ATT_STEP), 0, pl.program_id(2) > 0)
        heads = [slice(d * j, d * (j + 1)) for j in range(ATT_HEADS_PER_GROUP)]
        kcats = [jnp.concatenate([kp_ref[:, sl], k_ref[:, sl]], axis=0) for sl in heads]
        scores = [_nt(q_ref[:, sl], kcat) for sl, kcat in zip(heads, kcats)]
        dps = [_nt(do_ref[:, sl], jnp.concatenate([vp_ref[:, sl], v_ref[:, sl]], axis=0)) for sl in heads]
        probs = [jnp.exp(jnp.where(mask, sc * ATT_SCALE - lse_ref[:, sl.start:sl.start + 1], NEG_INF))
                 for sl, sc in zip(heads, scores)]
        dscs = [pr * (dp - dl_ref[:, sl.start:sl.start + 1]) for sl, pr, dp in zip(heads, probs, dps)]
        for sl, dsc, kcat in zip(heads, dscs, kcats):
            dq_ref[:, sl] = _nn(dsc, kcat) * ATT_SCALE

    tok = _att_spec(0)
    return pl.pallas_call(
        body, name=name, grid=(b, r, nb),
        in_specs=[_att_spec(0), _att_edge_spec(1, -1, nb), _att_spec(1), _att_edge_spec(2, -1, nb), _att_spec(2),
                  tok, tok, tok],
        out_specs=tok,
        out_shape=jax.ShapeDtypeStruct((b, r, l, ATT_OUT_DIM), F32),
        compiler_params=_params("parallel", "parallel", "parallel"),
    )(qkr, qkr, qkr, qkr, qkr, datt, lse, delta)


def _att_bwd_kv(qkr, datt, lse, delta, name):
    b, r, l, _ = qkr.shape
    nb = l // ATT_STEP
    d = ATT_HEAD_DIM

    def body(k_ref, v_ref, q_ref, qn_ref, do_ref, don_ref, lse_ref, lsen_ref, dl_ref, dln_ref, dk_ref, dv_ref):
        shape = (ATT_STEP, ATT_STEP + ATT_BLOCK)
        kj = lax.broadcasted_iota(jnp.int32, shape, 0)
        qi = lax.broadcasted_iota(jnp.int32, shape, 1)
        dist = qi - kj
        has_next = pl.program_id(2) < nb - 1
        mask = (dist >= 0) & (dist <= ATT_BLOCK) & (has_next | (qi < ATT_STEP))
        def per_query(own_ref, next_ref, sl):
            return jnp.tile(jnp.concatenate([own_ref[:, sl], next_ref[:, sl]], axis=0).T, (ATT_STEP // d, 1))

        heads = [slice(d * j, d * (j + 1)) for j in range(ATT_HEADS_PER_GROUP)]
        qcats = [jnp.concatenate([q_ref[:, sl], qn_ref[:, sl]], axis=0) for sl in heads]
        docats = [jnp.concatenate([do_ref[:, sl], don_ref[:, sl]], axis=0) for sl in heads]
        scores = [_nt(k_ref[:, sl], qcat) for sl, qcat in zip(heads, qcats)]
        dps = [_nt(v_ref[:, sl], docat) for sl, docat in zip(heads, docats)]
        probs = [jnp.exp(jnp.where(mask, sc * ATT_SCALE - per_query(lse_ref, lsen_ref, sl), NEG_INF))
                 for sl, sc in zip(heads, scores)]
        for sl, pr, docat in zip(heads, probs, docats):
            dv_ref[:, sl] = _nn(pr, docat)
        dscs = [pr * (dp - per_query(dl_ref, dln_ref, sl)) for sl, pr, dp in zip(heads, probs, dps)]
        for sl, dsc, qcat in zip(heads, dscs, qcats):
            dk_ref[:, sl] = _nn(dsc, qcat) * ATT_SCALE

    tok, tok_n = _att_spec(0), _att_edge_spec(0, 1, nb)
    return pl.pallas_call(
        body, name=name, grid=(b, r, nb),
        in_specs=[_att_spec(1), _att_spec(2), _att_spec(0), _att_edge_spec(0, 1, nb),
                  tok, tok_n, tok, tok_n, tok, tok_n],
        out_specs=[tok, tok],
        out_shape=[jax.ShapeDtypeStruct((b, r, l, ATT_OUT_DIM), F32)] * 2,
        compiler_params=_params("parallel", "parallel", "parallel"),
    )(qkr, qkr, qkr, qkr, datt, datt, lse, lse, delta, delta)


def _mix_fwd(gl, bg, ys, ya, name):
    t, d = ys.shape
    tm = _pick(t, (512, 256, 128))

    def body(gl_ref, bg_ref, ys_ref, ya_ref, o_ref):
        g0 = _sigmoid(gl_ref[:, :d] + bg_ref[:, :d])
        g1 = _sigmoid(gl_ref[:, d:] + bg_ref[:, d:])
        o_ref[...] = (g0 * ys_ref[...] + g1 * ya_ref[...]).astype(BF16)

    row = pl.BlockSpec((tm, d), lambda i: (i, 0))
    return pl.pallas_call(
        body, name=name, grid=(t // tm,),
        in_specs=[pl.BlockSpec((tm, 2 * d), lambda i: (i, 0)), pl.BlockSpec((1, 2 * d), lambda i: (0, 0)), row, row],
        out_specs=row, out_shape=jax.ShapeDtypeStruct((t, d), BF16),
        compiler_params=_params("parallel"),
    )(gl, bg, ys, ya)


def _mix_bwd(gl, bg, ys, ya, dmixed, name):
    t, d = ys.shape
    tm = _pick(t, (512, 256, 128))

    def body(gl_ref, bg_ref, ys_ref, ya_ref, dm_ref, dys_ref, dya_ref, dgl_ref, dbg_ref):
        @pl.when(pl.program_id(0) == 0)
        def _():
            dbg_ref[...] = jnp.zeros_like(dbg_ref)

        dm = dm_ref[...]
        g0 = _sigmoid(gl_ref[:, :d] + bg_ref[:, :d])
        g1 = _sigmoid(gl_ref[:, d:] + bg_ref[:, d:])
        dys_ref[...] = (dm * g0).astype(BF16)
        dya_ref[...] = (dm * g1).astype(BF16)
        d0 = dm * ys_ref[...] * (g0 * (1.0 - g0))
        d1 = dm * ya_ref[...] * (g1 * (1.0 - g1))
        dgl_ref[:, :d] = d0.astype(BF16)
        dgl_ref[:, d:] = d1.astype(BF16)
        dbg_ref[:, :d] += jnp.sum(d0, axis=0, keepdims=True)
        dbg_ref[:, d:] += jnp.sum(d1, axis=0, keepdims=True)

    row = pl.BlockSpec((tm, d), lambda i: (i, 0))
    wide = pl.BlockSpec((tm, 2 * d), lambda i: (i, 0))
    vec = pl.BlockSpec((1, 2 * d), lambda i: (0, 0))
    gate_cols = pl.BlockSpec((tm, 2 * d), lambda i: (i, DPROJ_COLS["gate"] // (2 * d)))
    return pl.pallas_call(
        body, name=name, grid=(t // tm,),
        in_specs=[wide, vec, row, row, row], out_specs=[row, row, gate_cols, vec],
        out_shape=[jax.ShapeDtypeStruct((t, d), BF16), jax.ShapeDtypeStruct((t, d), BF16),
                   jax.ShapeDtypeStruct((t, DPROJ_WIDTH), BF16), jax.ShapeDtypeStruct((1, 2 * d), F32)],
        compiler_params=_params("arbitrary"),
    )(gl, bg, ys, ya, dmixed)


def _up_proj_swiglu(h, w_up_t, gt, name):
    t, k = h.shape
    f = w_up_t.shape[0]
    tm, tn, _ = _mm_tiles(t, f, k, h.dtype.itemsize, w_up_t.dtype.itemsize, 4 + 2, True)

    def body(h_ref, w_ref, g_ref, up_ref, act_ref):
        up = _nt(h_ref[...], w_ref[...])
        up_ref[...] = up
        gv = g_ref[...]
        act_ref[...] = ((gv * _sigmoid(gv)) * up).astype(BF16)

    tile = pl.BlockSpec((tm, tn), lambda i, j: (i, j))
    return pl.pallas_call(
        body, name=name, grid=(t // tm, f // tn),
        in_specs=[pl.BlockSpec((tm, k), lambda i, j: (i, 0)), pl.BlockSpec((tn, k), lambda i, j: (j, 0)), tile],
        out_specs=[tile, tile],
        out_shape=[jax.ShapeDtypeStruct((t, f), F32), jax.ShapeDtypeStruct((t, f), BF16)],
        compiler_params=_params("parallel", "parallel"),
    )(h, w_up_t, gt)


def _swiglu_bwd(gt, up, dact, name):
    t, f = gt.shape
    tm = _pick(t, (512, 256, 128))

    def body(g_ref, u_ref, d_ref, dg_ref, du_ref):
        gv = g_ref[...]
        dv = d_ref[...]
        sg = _sigmoid(gv)
        dg_ref[...] = (dv * u_ref[...] * (sg * (1.0 + gv * (1.0 - sg)))).astype(BF16)
        du_ref[...] = (dv * (gv * sg)).astype(BF16)

    row = pl.BlockSpec((tm, f), lambda i: (i, 0))
    return pl.pallas_call(
        body, name=name, grid=(t // tm,), in_specs=[row, row, row], out_specs=[row, row],
        out_shape=[jax.ShapeDtypeStruct((t, f), BF16)] * 2, compiler_params=_params("parallel"),
    )(gt, up, dact)


def _peer(k):
    x, y, c = lax.axis_index("x"), lax.axis_index("y"), lax.axis_index("c")
    px, py, pc = x ^ ((k >> 2) & 1), y ^ ((k >> 1) & 1), c ^ (k & 1)
    return (px, py, pc), 4 * px + 2 * py + pc


def _my_index():
    return 4 * lax.axis_index("x") + 2 * lax.axis_index("y") + lax.axis_index("c")


def _all_gather(parts, name):
    n_parts = len(parts)

    def body(*refs):
        ins, outs = refs[:n_parts], refs[n_parts:2 * n_parts]
        send_sems, recv_sems, local_sems = refs[2 * n_parts:]
        here, me = _peer(0)
        sibling, sib_idx = _peer(1)
        chips = [_peer(2 * q) for q in range(1, N_CHIPS)]

        def copy(i, k, block, to, src=None):
            return pltpu.make_async_remote_copy(
                src_ref=outs[i].at[block] if src is None else src, dst_ref=outs[i].at[block],
                send_sem=send_sems.at[i * (N_DEV - 1) + k], recv_sem=recv_sems.at[i * (N_DEV - 1) + k],
                device_id=to, device_id_type=MESH)

        local = [pltpu.make_async_copy(ins[i], outs[i].at[me], local_sems.at[i]) for i in range(n_parts)]
        for cp in local:
            cp.start()
        sends = []
        for i in range(n_parts):
            sends.append(copy(i, 0, me, sibling, src=ins[i]))
            sends += [copy(i, q, me, chip, src=ins[i]) for q, (chip, _) in enumerate(chips, start=1)]
        for cp in sends:
            cp.start()
        for q, (chip, chip_idx) in enumerate(chips, start=1):
            for i in range(n_parts):
                copy(i, q, chip_idx, here).wait_recv()
                fwd = copy(i, N_CHIPS - 1 + q, chip_idx, sibling)
                fwd.start()
                sends.append(fwd)
        for i in range(n_parts):
            copy(i, 0, sib_idx, here).wait_recv()
        for q, (_, chip_idx) in enumerate(chips, start=1):
            for i in range(n_parts):
                copy(i, N_CHIPS - 1 + q, chip_idx ^ 1, here).wait_recv()
        for cp in sends:
            cp.wait_send()
        for cp in local:
            cp.wait()

    hbm = pl.BlockSpec(memory_space=pl.ANY)
    return pl.pallas_call(
        body, name=name, in_specs=[hbm] * n_parts, out_specs=[hbm] * n_parts,
        out_shape=[jax.ShapeDtypeStruct((N_DEV,) + p_.shape, p_.dtype) for p_ in parts],
        scratch_shapes=[pltpu.SemaphoreType.DMA((n_parts * (N_DEV - 1),)),
                        pltpu.SemaphoreType.DMA((n_parts * (N_DEV - 1),)),
                        pltpu.SemaphoreType.DMA((n_parts,))],
        compiler_params=pltpu.CompilerParams(has_side_effects=True),
    )(*parts)


HBM_SPEC = pl.BlockSpec(memory_space=pltpu.HBM)
SEM_SPEC = pl.BlockSpec(memory_space=pltpu.SEMAPHORE)
DATAFLOW = pltpu.SideEffectType.DATAFLOW_SIDE_EFFECTING


def _gather_start(block, after, name):
    per_peer = block.ndim == 3

    def body(v_ref, land_ref, after_ref, send_sems, recv_sems, v_thru, land_thru, token):
        me = _my_index()
        for k in range(1, N_DEV):
            peer, pidx = _peer(k)
            pltpu.make_async_remote_copy(
                src_ref=v_ref.at[pidx] if per_peer else v_ref, dst_ref=land_ref.at[me],
                send_sem=send_sems.at[k - 1], recv_sem=recv_sems.at[k - 1],
                device_id=peer, device_id_type=MESH).start()
        token[...] = jnp.zeros_like(token)

    land_shape = (N_DEV,) + block.shape[-2:]
    return pl.pallas_call(
        body, name=name,
        out_shape=(pltpu.SemaphoreType.DMA((N_DEV - 1,)), pltpu.SemaphoreType.DMA((N_DEV - 1,)),
                   pltpu.HBM(block.shape, block.dtype), pltpu.HBM(land_shape, block.dtype),
                   jax.ShapeDtypeStruct((8, LANES), F32)),
        in_specs=(HBM_SPEC, HBM_SPEC, pl.BlockSpec(memory_space=pl.ANY)),
        out_specs=(SEM_SPEC, SEM_SPEC, HBM_SPEC, HBM_SPEC, pl.BlockSpec(memory_space=pltpu.VMEM)),
        input_output_aliases={0: 2, 1: 3},
        compiler_params=pltpu.CompilerParams(has_side_effects=DATAFLOW),
    )(pltpu.with_memory_space_constraint(block, pltpu.HBM),
      pltpu.with_memory_space_constraint(lax.empty(land_shape, block.dtype), pltpu.HBM), after)


def _gather_wait(send_sems, recv_sems, block, landing, after, name):
    per_peer = block.ndim == 3

    def body(v_ref, land_ref, send_sems, recv_sems, after_ref, v_dead, got_ref):
        for k in range(1, N_DEV):
            peer, pidx = _peer(k)
            copy = pltpu.make_async_remote_copy(
                src_ref=v_ref.at[pidx] if per_peer else v_ref, dst_ref=land_ref.at[pidx],
                send_sem=send_sems.at[k - 1], recv_sem=recv_sems.at[k - 1],
                device_id=peer, device_id_type=MESH)
            copy.wait_send()
            copy.wait_recv()

    return pl.pallas_call(
        body, name=name,
        out_shape=(pltpu.HBM(block.shape, block.dtype), pltpu.HBM(landing.shape, landing.dtype)),
        in_specs=(HBM_SPEC, HBM_SPEC, SEM_SPEC, SEM_SPEC, pl.BlockSpec(memory_space=pl.ANY)),
        out_specs=(HBM_SPEC, HBM_SPEC), input_output_aliases={0: 0, 1: 1},
        compiler_params=pltpu.CompilerParams(has_side_effects=DATAFLOW),
    )(block, landing, send_sems, recv_sems, after)[1]


TILE_ELEMS = 1024 * 1024


def _shared_exchange(shared, name):
    def body(sh_ref, gsh_ref, send_sems, recv_sems, local_sem):
        me = _my_index()
        local = pltpu.make_async_copy(sh_ref, gsh_ref.at[me], local_sem)
        local.start()
        sends = []
        for k in range(1, N_DEV):
            peer, _ = _peer(k)
            cp = pltpu.make_async_remote_copy(
                src_ref=sh_ref, dst_ref=gsh_ref.at[me], send_sem=send_sems.at[k - 1],
                recv_sem=recv_sems.at[k - 1], device_id=peer, device_id_type=MESH)
            cp.start()
            sends.append(cp)
        for k in range(1, N_DEV):
            peer, pidx = _peer(k)
            pltpu.make_async_remote_copy(
                src_ref=sh_ref, dst_ref=gsh_ref.at[pidx], send_sem=send_sems.at[k - 1],
                recv_sem=recv_sems.at[k - 1], device_id=peer, device_id_type=MESH).wait_recv()
        for cp in sends:
            cp.wait_send()
        local.wait()

    hbm = pl.BlockSpec(memory_space=pl.ANY)
    return pl.pallas_call(
        body, name=name, in_specs=[hbm], out_specs=hbm,
        out_shape=jax.ShapeDtypeStruct((N_DEV,) + shared.shape, shared.dtype),
        scratch_shapes=[pltpu.SemaphoreType.DMA((N_DEV - 1,)), pltpu.SemaphoreType.DMA((N_DEV - 1,)),
                        pltpu.SemaphoreType.DMA],
        compiler_params=pltpu.CompilerParams(has_side_effects=True),
    )(shared)


def _adamw(parts, w, m, v, name, row0=0, own=None):
    n_parts, rows, lanes = parts.shape
    tr = rows if rows * lanes <= TILE_ELEMS // 2 else _tile_rows(math.gcd(rows, row0), TILE_ELEMS // 4 // lanes, 8)
    c1 = 1.0 - ADAM_B1 ** ADAM_STEP
    c2 = 1.0 - ADAM_B2 ** ADAM_STEP

    def body(*refs):
        if own is None:
            p_ref, w_ref, m_ref, v_ref, g_ref, d_ref, nm_ref, nv_ref = refs
            terms = [p_ref[j].astype(F32) for j in range(n_parts)]
        else:
            me_ref, p_ref, own_ref, w_ref, m_ref, v_ref, g_ref, d_ref, nm_ref, nv_ref = refs
            terms = [jnp.where(me_ref[0] == j, own_ref[...], p_ref[j]).astype(F32) for j in range(n_parts)]
        g = terms[0]
        for term in terms[1:]:
            g = g + term
        nm = ADAM_B1 * m_ref[...] + (1.0 - ADAM_B1) * g
        nv = ADAM_B2 * v_ref[...] + (1.0 - ADAM_B2) * (g * g)
        g_ref[...] = g
        nm_ref[...] = nm
        nv_ref[...] = nv
        d_ref[...] = -ADAM_LR * ((nm / c1) / (jnp.sqrt(nv / c2) + ADAM_EPS) + ADAM_WD * w_ref[...])

    row = pl.BlockSpec((tr, lanes), lambda i, *_: (i, 0))
    state = pl.BlockSpec((tr, lanes), lambda i, *_: (row0 // tr + i, 0))
    in_specs = [pl.BlockSpec((n_parts, tr, lanes), lambda i, *_: (0, i, 0)), state, state, state]
    args, n_prefetch = (parts, w, m, v), 0
    if own is not None:
        slabs, me = own
        in_specs.insert(1, pl.BlockSpec((None, tr, lanes), lambda i, me_ref: (me_ref[0], i, 0)))
        args, n_prefetch = (me, parts, slabs, w, m, v), 1
    return pl.pallas_call(
        body, name=name,
        grid_spec=pltpu.PrefetchScalarGridSpec(num_scalar_prefetch=n_prefetch, grid=(rows // tr,),
                                               in_specs=in_specs, out_specs=[row] * 4),
        out_shape=[jax.ShapeDtypeStruct((rows, lanes), F32)] * 4,
        compiler_params=_params("parallel"),
    )(*args)


MATRIX_SHARDS = (
    ("w_in", (D_MODEL, IN_PROJ_DIM // N_DEV), True),
    ("w_ssm_out", (SSM_D_INNER // N_DEV, D_MODEL), False),
    ("w_att_out", (ATT_OUT_DIM, D_MODEL // N_DEV), True),
    ("w_mix_out", (D_MODEL // N_DEV, D_MODEL), False),
    ("w_ffn_gate", (D_MODEL, D_FF // N_DEV), True),
    ("w_ffn_up", (D_MODEL, D_FF // N_DEV), True),
    ("w_ffn_down", (D_FF // N_DEV, D_MODEL), False),
)
CONV_SHARD = ("conv_w", (SSM_CONV, SSM_CONV_DIM // N_DEV), True)
SHARDED = MATRIX_SHARDS + (CONV_SHARD,)
REPLICATED = (("norm_mix", D_MODEL), ("b_gate", 2 * D_MODEL), ("conv_b", SSM_CONV_DIM), ("dt_bias", SSM_N_HEADS),
              ("a_log", SSM_N_HEADS), ("d_skip", SSM_N_HEADS), ("ssm_norm", SSM_D_INNER), ("norm_ffn", D_MODEL),
              ("norm_final", D_MODEL))


def _round_up(n, mult):
    return -(-n // mult) * mult


def _pack_rows(flat, row_mult):
    rows = _round_up(-(-flat.shape[0] // LANES), row_mult)
    return jnp.pad(flat, (0, rows * LANES - flat.shape[0])).reshape(rows, LANES)


def _stacking(specs):
    return tuple((name, (shape[1], shape[0]) if by_cols else shape, by_cols) for name, shape, by_cols in specs)


def _to_stacking(vals, specs):
    return {name: (vals[name].T if by_cols else vals[name]) for name, _, by_cols in specs}


STACK_WIDTH = D_MODEL
STACK_ALIGN = 16
STACK_ORDER = ("w_ssm_out", "w_mix_out", "w_ffn_gate", "w_ffn_up", "w_ffn_down", "w_att_out", "conv_w", "w_in")
GATHER_LATER = STACK_ORDER[:-1]
REDUCE_EARLY = STACK_ORDER[:5]
REDUCE_LATE = STACK_ORDER[5:]


def _stack_layout():
    shapes = {name: shape for name, shape, _ in _stacking(SHARDED)}
    layout, off = {}, 0
    for name in STACK_ORDER:
        r, c = shapes[name]
        rows = r if c == STACK_WIDTH else _round_up(-(-(r * c) // STACK_WIDTH), STACK_ALIGN)
        layout[name] = (off, rows, (r, c))
        off = _round_up(off + rows, STACK_ALIGN)
    return layout, _round_up(off, 1024)


def _to_stack_rows(v, rows):
    if v.shape[-1] == STACK_WIDTH:
        return v
    lead = v.shape[:-2]
    flat = v.reshape(lead + (-1,))
    flat = jnp.pad(flat, [(0, 0)] * len(lead) + [(0, rows * STACK_WIDTH - flat.shape[-1])])
    return flat.reshape(lead + (rows, STACK_WIDTH))


def _from_stack_rows(block, shape):
    r, c = shape
    if c == STACK_WIDTH:
        return block
    lead = block.shape[:-2]
    return block.reshape(lead + (-1,))[..., :r * c].reshape(lead + (r, c))


def _stack(vals, dtype, skip=(), names=STACK_ORDER):
    layout, total = _stack_layout()
    order = names
    after = STACK_ORDER.index(order[-1]) + 1
    if after < len(STACK_ORDER):
        total = layout[STACK_ORDER[after]][0]
    lead = next(iter(vals.values())).shape[:-2]
    pieces = []
    for i, name in enumerate(order):
        off, rows, _ = layout[name]
        until = layout[order[i + 1]][0] if i + 1 < len(order) else total
        piece = jnp.zeros(lead + (rows, STACK_WIDTH), dtype) if name in skip else _to_stack_rows(vals[name], rows)
        pieces.append(jnp.pad(piece.astype(dtype), [(0, 0)] * len(lead) + [(0, until - off - rows), (0, 0)]))
    return jnp.concatenate(pieces, axis=-2)


def _unstack(stacked, names):
    layout, _ = _stack_layout()
    row0 = layout[names[0]][0]
    return {name: _from_stack_rows(stacked[..., layout[name][0] - row0:layout[name][0] - row0 + layout[name][1], :],
                                   layout[name][2]) for name in names}


W_IN_SHARD_ROWS = IN_PROJ_DIM // N_DEV


def _w_in_row_moves():
    moves, orig = [], 0
    for name, size in IN_SPLIT:
        for j in range(N_DEV):
            lo, hi = max(orig, W_IN_SHARD_ROWS * j), min(orig + size, W_IN_SHARD_ROWS * (j + 1))
            if lo < hi:
                moves.append((j, lo - W_IN_SHARD_ROWS * j, DPROJ_COLS[name] + lo - orig, hi - lo))
        orig += size
    return moves


def _w_in_from_shards(shards, name):
    total, base = shards.shape[1], 0
    pad_lo, pad_hi = DPROJ_COLS["dt"] + _round_up(SSM_N_HEADS, STACK_ALIGN), DPROJ_COLS["dt"] + DPROJ_DT_WIDTH

    def body(x_ref, o_ref):
        o_ref[pad_lo:pad_hi, :] = jnp.zeros((pad_hi - pad_lo, LANES), x_ref.dtype)
        for j, r, at, n in _w_in_row_moves():
            o_ref[at:at + n, :] = x_ref[j, base + r:base + r + n, :]

    return pl.pallas_call(
        body, name=name, grid=(STACK_WIDTH // LANES,),
        in_specs=[pl.BlockSpec((N_DEV, total, LANES), lambda c: (0, 0, c))],
        out_specs=pl.BlockSpec((DPROJ_WIDTH, LANES), lambda c: (0, c)),
        out_shape=jax.ShapeDtypeStruct((DPROJ_WIDTH, STACK_WIDTH), shards.dtype),
        compiler_params=_params("parallel"),
    )(shards)


def _w_in_to_shards(dw_all, head, name):
    layout, total = _stack_layout()
    total -= layout[REDUCE_LATE[0]][0]
    base = head.shape[1]
    end = base + W_IN_SHARD_ROWS

    def body(x_ref, h_ref, o_ref):
        o_ref[:, 0:base, :] = h_ref[...]
        for j, r, at, n in _w_in_row_moves():
            o_ref[j, base + r:base + r + n, :] = x_ref[at:at + n, :]
        o_ref[:, end:total, :] = jnp.zeros((N_DEV, total - end, LANES), o_ref.dtype)

    return pl.pallas_call(
        body, name=name, grid=(STACK_WIDTH // LANES,),
        in_specs=[pl.BlockSpec((DPROJ_WIDTH, LANES), lambda c: (0, c)),
                  pl.BlockSpec((N_DEV, base, LANES), lambda c: (0, 0, c))],
        out_specs=pl.BlockSpec((N_DEV, total, LANES), lambda c: (0, 0, c)),
        out_shape=jax.ShapeDtypeStruct((N_DEV, total, STACK_WIDTH), dw_all.dtype),
        compiler_params=_params("parallel"),
    )(dw_all, head)


REPLICATED_ROWS = sum(-(-size // LANES) for _, size in REPLICATED)
LOSS_ROW = REPLICATED_ROWS


def _pack_replicated(vals):
    rows = []
    for name, size in REPLICATED:
        v = vals[name].reshape(-1).astype(F32)
        rows.append(jnp.pad(v, (0, _round_up(size, LANES) - size)))
    return _pack_rows(jnp.concatenate(rows), 8)


def _unpack_replicated(packed, shapes):
    flat = packed.reshape(-1)
    out, off = {}, 0
    for name, size in REPLICATED:
        out[name] = flat[off:off + size].reshape(shapes[name])
        off += _round_up(size, LANES)
    return out


def _lane_row(v):
    v = v.reshape(-1).astype(F32)
    return jnp.pad(v, (0, LANES - v.shape[0])).reshape(1, LANES)


IN_SPLIT = (("z", SSM_D_INNER), ("xbc", SSM_CONV_DIM), ("dt", SSM_N_HEADS), ("qkv", ATT_QKV_DIM), ("gate", 2 * D_MODEL))


def kernel(x, norm_mix, w_in, b_gate, conv_w, conv_b, dt_bias, a_log, d_skip, ssm_norm, w_ssm_out, w_att_out, w_mix_out, norm_ffn, w_ffn_gate, w_ffn_up, w_ffn_down, norm_final, loss_target, m_norm_mix, m_w_in, m_b_gate, m_conv_w, m_conv_b, m_dt_bias, m_a_log, m_d_skip, m_ssm_norm, m_w_ssm_out, m_w_att_out, m_w_mix_out, m_norm_ffn, m_w_ffn_gate, m_w_ffn_up, m_w_ffn_down, m_norm_final, v_norm_mix, v_w_in, v_b_gate, v_conv_w, v_conv_b, v_dt_bias, v_a_log, v_d_skip, v_ssm_norm, v_w_ssm_out, v_w_att_out, v_w_mix_out, v_norm_ffn, v_w_ffn_gate, v_w_ffn_up, v_w_ffn_down, v_norm_final):
    given = dict(locals())
    weights = {name: given[name][0] for name, _, _ in SHARDED}
    b, s, d = x.shape
    t = b * s

    stacking = _to_stacking(weights, SHARDED)
    conv_shape = dict((name, shape) for name, shape, _ in _stacking(SHARDED))["conv_w"]
    w_in_local = jnp.pad(stacking["w_in"].astype(BF16), ((0, -W_IN_SHARD_ROWS % STACK_ALIGN), (0, 0)))
    conv_local = _pack_rows(stacking["conv_w"].reshape(-1), 8)
    w_in_shards, conv_all = _all_gather([w_in_local, conv_local], "w_in_all_gather")
    head_local = _stack(stacking, BF16, skip=("conv_w",), names=GATHER_LATER)
    in_flight = _gather_start(head_local, conv_all, "weights_gather_start")
    w_in_all = _w_in_from_shards(w_in_shards, "w_in_from_shards")
    w_sec = {name: w_in_all[DPROJ_COLS[name]:DPROJ_COLS[name] + _round_up(size, LANES)] for name, size in IN_SPLIT}
    conv_size = conv_shape[0] * conv_shape[1]
    conv_taps = conv_all.reshape(N_DEV, -1)[:, :conv_size].reshape(N_DEV * conv_shape[0], conv_shape[1]).T

    g_mix, g_ffn, g_fin = norm_mix.reshape(1, d), norm_ffn.reshape(1, d), norm_final.reshape(1, d)
    g_mix = g_mix + in_flight[4][:1, :1]
    bg_row = b_gate.reshape(1, 2 * d)
    convb_row = conv_b.reshape(1, SSM_CONV_DIM)
    ssmn_row = ssm_norm.reshape(1, SSM_D_INNER)
    dtb_row, alog_row = _lane_row(dt_bias), _lane_row(a_log)
    cosf, sinf = _rope_tables(s)

    x2d = x.reshape(t, d)
    h1 = _rmsnorm_fwd(x2d, g_mix, "norm_mix_fwd")
    proj = {name: _mm(h1, w_sec[name], mode="nt", name="in_proj_" + name) for name, _ in IN_SPLIT if name != "qkv"}
    xbc3 = proj["xbc"].reshape(b, s, SSM_CONV_DIM)
    xc = _conv_fwd(xbc3, conv_taps, convb_row, "conv_fwd")
    dtr3 = proj["dt"].reshape(b, s, DT_PAD)
    to_channels, to_heads = _head_masks()
    dskx = jnp.repeat(d_skip.reshape(-1).astype(F32), SSM_HEAD_DIM).reshape(1, SSM_D_INNER)
    y_ssd, h_states = _ssd_fwd(xc, dtr3, dtb_row, alog_row, dskx, to_channels, "ssd_fwd")
    y_ssd2 = y_ssd.reshape(t, SSM_D_INNER)
    ynorm = _gate_norm_fwd(y_ssd2, proj["z"], ssmn_row, "ssd_gate_norm_fwd")
    landed = _gather_wait(*in_flight[:4], ynorm, "weights_gather_wait")
    head_all = lax.dynamic_update_slice(landed, head_local[None], (_my_index(), 0, 0))
    full = {name: v.reshape((-1,) + v.shape[2:]) for name, v in _unstack(head_all, STACK_ORDER[:-2]).items()}
    y_ssm = _mm(ynorm, full["w_ssm_out"], mode="nn", name="ssm_out_proj")

    qk_parts = _qkv_proj_rope(h1, w_sec["qkv"], cosf, sinf, b, s, "in_proj_qkv_rope")
    att_parts = [_att_fwd(qk_parts[gi], "att_fwd_%d" % r) for gi, r in enumerate(ATT_DILATIONS)]
    att, *lse_parts = _att_merge([o for o, _ in att_parts], [l_ for _, l_ in att_parts], "att_merge")
    att2 = att.reshape(t, ATT_OUT_DIM)
    y_att = _mm(att2, full["w_att_out"], mode="nt", name="att_out_proj")

    mixed = _mix_fwd(proj["gate"], bg_row, y_ssm, y_att, "mix_fwd")
    x2 = _mm(mixed, full["w_mix_out"], mode="nn", name="mix_out_proj", add=x2d)
    h2 = _rmsnorm_fwd(x2, g_ffn, "norm_ffn_fwd")
    gt = _mm(h2, full["w_ffn_gate"], mode="nt", name="ffn_gate_proj")
    up, act = _up_proj_swiglu(h2, full["w_ffn_up"], gt, "ffn_up_proj_swiglu")
    x3 = _mm(act, full["w_ffn_down"], mode="nn", name="ffn_down_proj", add=x2)

    loss_row, dx3, dg_fin, dx3b = _loss_head(x3, g_fin, loss_target.reshape(t, d), "loss_head")
    grads = {}
    dact = _mm(dx3b, full["w_ffn_down"], mode="nt", name="ffn_down_dx")
    grads["w_ffn_down"] = _mm(act, dx3b, mode="tn", name="ffn_down_dw", out_dtype=BF16)
    dgt, dup = _swiglu_bwd(gt, up, dact, "swiglu_bwd")
    grads["w_ffn_gate"] = _mm(dgt, h2, mode="tn", name="ffn_gate_dw", out_dtype=BF16)
    grads["w_ffn_up"] = _mm(dup, h2, mode="tn", name="ffn_up_dw", out_dtype=BF16)
    dh2 = _mm(dgt, full["w_ffn_gate"], mode="nn", name="ffn_gate_dx")
    dh2 = _mm(dup, full["w_ffn_up"], mode="nn", name="ffn_up_dx", add=dh2)
    dx2, dg_ffn, dx2b = _rmsnorm_bwd(x2, g_ffn, dh2, dx3, "norm_ffn_bwd", with_bf16=True)

    dmixed = _mm(dx2b, full["w_mix_out"], mode="nt", name="mix_out_dx")
    grads["w_mix_out"] = _mm(mixed, dx2b, mode="tn", name="mix_out_dw", out_dtype=BF16)
    dys, dya, dproj, dbg = _mix_bwd(proj["gate"], bg_row, y_ssm, y_att, dmixed, "mix_bwd")

    grads["w_ssm_out"] = _mm(ynorm, dys, mode="tn", name="ssm_out_dw", out_dtype=BF16)
    early = _stack({name: grads[name].reshape((N_DEV, -1, STACK_WIDTH)) for name in REDUCE_EARLY}, BF16,
                   names=REDUCE_EARLY)
    early_flight = _gather_start(early, dys, "grads_scatter_start")
    ssmn_row = ssmn_row + early_flight[4][:1, :1]
    dynorm = _mm(dys, full["w_ssm_out"], mode="nt", name="ssm_out_dx")
    dy_ssd, dproj, dssmn = _gate_norm_bwd(y_ssd2, proj["z"], ssmn_row, dynorm, dproj, "ssd_gate_norm_bwd")
    dxc, dproj, dalog, ddsk, ddtb = _ssd_bwd(xc, dtr3, dy_ssd.reshape(b, s, SSM_D_INNER), h_states, dtb_row, alog_row,
                                             dskx, to_channels, to_heads, dproj.reshape(b, s, DPROJ_WIDTH), "ssd_bwd")
    dproj, dconvw, dconvb = _conv_bwd(xbc3, dxc, conv_taps, convb_row, dproj, "conv_bwd")
    grads["conv_w"] = dconvw.T.astype(BF16)

    grads["w_att_out"] = _mm(dya, att2, mode="tn", name="att_out_dw", out_dtype=BF16)
    datt = _mm(dya, full["w_att_out"], mode="nn", name="att_out_dx").reshape(b, s, ATT_OUT_DIM)
    do_parts, dl_parts = _att_delta(att, datt, "att_delta")
    dqs, dks, dvs = [], [], []
    for gi, r in enumerate(ATT_DILATIONS):
        operands = (qk_parts[gi], do_parts[gi], lse_parts[gi], dl_parts[gi])
        dqs.append(_att_bwd_q(*operands, "att_bwd_q_%d" % r))
        dk_g, dv_g = _att_bwd_kv(*operands, "att_bwd_kv_%d" % r)
        dks.append(dk_g)
        dvs.append(dv_g)
    dproj = _rope_bwd(dqs, dks, dvs, cosf, sinf, dproj, "rope_bwd").reshape(t, DPROJ_WIDTH)

    dw_all = _mm(dproj, h1, mode="tn", name="in_proj_dw", out_dtype=BF16)
    head = _stack({name: grads[name].reshape((N_DEV, -1, grads[name].shape[-1])) for name in REDUCE_LATE[:-1]}, BF16,
                  names=REDUCE_LATE[:-1])
    late = _w_in_to_shards(dw_all, head, "grad_stacks")
    late_flight = _gather_start(late, dw_all, "grads_late_scatter_start")
    dh1 = _mm(dproj, w_in_all, mode="nn", name="in_proj_dx", after=late_flight[4])
    grad_x, dg_mix = _rmsnorm_bwd(x2d, g_mix, dh1, dx2, "norm_mix_bwd")

    small = {"norm_mix": dg_mix, "b_gate": dbg, "conv_b": dconvb, "dt_bias": ddtb[:, :SSM_N_HEADS],
             "a_log": dalog[:, :SSM_N_HEADS], "d_skip": ddsk[:, :SSM_N_HEADS], "ssm_norm": dssmn,
             "norm_ffn": dg_ffn, "norm_final": dg_fin}
    shared = _pack_replicated(small)
    shared = shared.at[LOSS_ROW, 0].set(loss_row[0, 0])
    got_small = _shared_exchange(shared, "shared_grads_exchange")

    def packed(prefix):
        vals = _to_stacking({name: given[prefix + name][0] for name, _, _ in SHARDED}, SHARDED)
        rep = {name: given[prefix + name] for name, _ in REPLICATED}
        return _stack(vals, F32), _pack_replicated(rep)

    (w_big, w_small), (m_big, m_small), (v_big, v_small) = packed(""), packed("m_"), packed("v_")
    me = _my_index().astype(jnp.int32).reshape(1)
    big_early = _adamw(_gather_wait(*early_flight[:4], got_small, "grads_scatter_wait"), w_big, m_big, v_big,
                       "adamw_early", own=(early, me))
    big_late = _adamw(_gather_wait(*late_flight[:4], got_small, "grads_late_scatter_wait"), w_big, m_big, v_big,
                      "adamw_late", row0=early.shape[1], own=(late, me))
    sml = _adamw(got_small, w_small, m_small, v_small, "adamw_replicated")

    outs = [sml[0][LOSS_ROW, 0], grad_x.reshape(b, s, d)]
    rep_shapes = {name: given[name].shape for name, _ in REPLICATED}
    order = ["norm_mix", "w_in", "b_gate", "conv_w", "conv_b", "dt_bias", "a_log", "d_skip", "ssm_norm", "w_ssm_out",
             "w_att_out", "w_mix_out", "norm_ffn", "w_ffn_gate", "w_ffn_up", "w_ffn_down", "norm_final"]
    for early_k, late_k, sml_k in zip(big_early, big_late, sml):
        stacks = dict(_unstack(early_k, REDUCE_EARLY), **_unstack(late_k, REDUCE_LATE))
        sharded = _to_stacking(stacks, SHARDED)
        rep = _unpack_replicated(sml_k, rep_shapes)
        for name in order:
            outs.append(sharded[name][None] if name in sharded else rep[name])
    return tuple(outs)
```

```python
import functools
import math

import jax
import jax.numpy as jnp
from jax import lax
from jax.experimental import pallas as pl
from jax.experimental.pallas import tpu as pltpu

F32 = jnp.float32
BF16 = jnp.bfloat16

N_DEV = 8
N_CHIPS = 4
D_MODEL = 1024
SSM_D_INNER = 2048
SSM_HEAD_DIM = 64
SSM_N_HEADS = 32
SSM_N_GROUPS = 4
SSM_HEADS_PER_GROUP = SSM_N_HEADS // SSM_N_GROUPS
SSM_D_STATE = 128
SSM_CONV = 4
SSM_CHUNK = 128
SSM_CONV_DIM = 3072
ATT_HEAD_DIM = 128
ATT_HEADS_PER_GROUP = 4
ATT_DILATIONS = (1, 4, 16)
ATT_N_HEADS = 12
ATT_QKV_DIM = 4608
ATT_OUT_DIM = 512
ATT_BLOCK = 128
ROPE_THETA = 10000.0
D_FF = 2816
IN_PROJ_DIM = 11808
EPS = 1e-6
LANES = 128
DT_PAD = LANES

DPROJ_COLS = {"qkv": 0, "z": 4608, "xbc": 6656, "dt": 9728, "gate": 10240}
DPROJ_DT_WIDTH = 512
DPROJ_WIDTH = 12288

ADAM_LR = 0.001
ADAM_B1 = 0.9
ADAM_B2 = 0.999
ADAM_EPS = 1e-08
ADAM_WD = 0.01
ADAM_STEP = 10

VMEM_LIMIT = 56 * 1024 * 1024
MESH = pl.DeviceIdType.MESH
NEG_INF = float("-inf")


def _tile_rows(n, cap, mult):
    return max(t for t in range(mult, min(n, cap) + 1, mult) if n % t == 0)


def _pick(n, candidates):
    for c in candidates:
        if n % c == 0:
            return c
    return n


def _params(*sem):
    return pltpu.CompilerParams(dimension_semantics=sem, vmem_limit_bytes=VMEM_LIMIT)


def _sigmoid(x):
    return 0.5 * jnp.tanh(0.5 * x) + 0.5


def _softplus(x):
    return jnp.maximum(x, 0.0) + jnp.log(1.0 + jnp.exp(-jnp.abs(x)))


def _dot(a, b, dims):
    return lax.dot_general(a.astype(BF16), b.astype(BF16), (dims, ((), ())), preferred_element_type=F32)


def _nn(a, b):
    return _dot(a, b, ((1,), (0,)))


def _nt(a, b):
    return _dot(a, b, ((1,), (1,)))


def _tn(a, b):
    return _dot(a, b, ((0,), (0,)))


def _split3(v):
    hi = v.astype(BF16)
    r1 = v - hi.astype(F32)
    mid = r1.astype(BF16)
    lo = (r1 - mid.astype(F32)).astype(BF16)
    return hi, mid, lo


def _mask_nn(mask, v):
    mb = mask.astype(BF16)
    hi, mid, lo = _split3(v)
    return _nn(mb, hi) + (_nn(mb, mid) + _nn(mb, lo))


MM_VMEM_BUDGET = 40 * 1024 * 1024
MM_FULL_K = 2816


def _mm_tiles(m, n, k, a_bytes, b_bytes, o_bytes, has_add):
    tk = k if k <= MM_FULL_K else _pick(k, (2048, 1024, 512, 256, 128))
    tn = 1408 if (n > 1024 and n % 1408 == 0) else _pick(n, (1024, 768, 512, 384, 256, 128))
    for tm in (1408, 1024, 768, 512, 384, 256, 128):
        if m % tm:
            continue
        buffers = 2 * (tm * tk * a_bytes + tk * tn * b_bytes + tm * tn * (o_bytes + (4 if has_add else 0)))
        if tk < k:
            buffers += tm * tn * 4
        if buffers <= MM_VMEM_BUDGET:
            return tm, tn, tk
    return _pick(m, (128,)), tn, tk


def _mm(a, b, *, mode, name, out_dtype=F32, add=None, after=None):
    if mode == "nn":
        (m, k), n = a.shape, b.shape[1]
    elif mode == "nt":
        (m, k), n = a.shape, b.shape[0]
    else:
        (k, m), n = a.shape, b.shape[1]
    has_add = add is not None
    tm, tn, tk = _mm_tiles(m, n, k, a.dtype.itemsize, b.dtype.itemsize, jnp.dtype(out_dtype).itemsize, has_add)
    nk = k // tk
    dims = {"nn": ((1,), (0,)), "nt": ((1,), (1,)), "tn": ((0,), (0,))}[mode]
    a_spec = {"nn": pl.BlockSpec((tm, tk), lambda i, j, kk: (i, kk)),
              "nt": pl.BlockSpec((tm, tk), lambda i, j, kk: (i, kk)),
              "tn": pl.BlockSpec((tk, tm), lambda i, j, kk: (kk, i))}[mode]
    b_spec = {"nn": pl.BlockSpec((tk, tn), lambda i, j, kk: (kk, j)),
              "nt": pl.BlockSpec((tn, tk), lambda i, j, kk: (j, kk)),
              "tn": pl.BlockSpec((tk, tn), lambda i, j, kk: (kk, j))}[mode]
    o_spec = pl.BlockSpec((tm, tn), lambda i, j, kk: (i, j))

    def finish(r, c_ref, o_ref):
        if has_add:
            r = r + c_ref[...]
        o_ref[...] = r.astype(out_dtype)

    def body_one(*refs):
        a_ref, b_ref = refs[:2]
        finish(_dot(a_ref[...], b_ref[...], dims), refs[2] if has_add else None, refs[-1])

    def body_acc(*refs):
        a_ref, b_ref = refs[:2]
        o_ref, acc = refs[-2:]
        kk = pl.program_id(2)

        @pl.when(kk == 0)
        def _():
            acc[...] = jnp.zeros_like(acc)

        acc[...] += _dot(a_ref[...], b_ref[...], dims)

        @pl.when(kk == nk - 1)
        def _():
            finish(acc[...], refs[2] if has_add else None, o_ref)

    in_specs = [a_spec, b_spec] + ([o_spec] if has_add else [])
    args = (a, b) + ((add,) if has_add else ())
    if after is not None:
        in_specs, args = in_specs + [pl.BlockSpec(memory_space=pl.ANY)], args + (after,)
    return pl.pallas_call(
        body_one if nk == 1 else body_acc, name=name, grid=(m // tm, n // tn, nk),
        in_specs=in_specs, out_specs=o_spec,
        out_shape=jax.ShapeDtypeStruct((m, n), out_dtype),
        scratch_shapes=[] if nk == 1 else [pltpu.VMEM((tm, tn), F32)],
        compiler_params=_params("parallel", "parallel", "arbitrary"),
    )(*args)


def _rmsnorm_fwd(x, g, name):
    t, d = x.shape
    tm = _pick(t, (512, 256, 128))

    def body(x_ref, g_ref, o_ref):
        xv = x_ref[...]
        r = lax.rsqrt(jnp.mean(xv * xv, axis=-1, keepdims=True) + EPS)
        o_ref[...] = ((xv * r) * g_ref[...]).astype(BF16)

    return pl.pallas_call(
        body, name=name, grid=(t // tm,),
        in_specs=[pl.BlockSpec((tm, d), lambda i: (i, 0)), pl.BlockSpec((1, d), lambda i: (0, 0))],
        out_specs=pl.BlockSpec((tm, d), lambda i: (i, 0)),
        out_shape=jax.ShapeDtypeStruct((t, d), BF16),
        compiler_params=_params("parallel"),
    )(x, g)


def _proj_residual_norm(a, w, res, g, name):
    t, k = a.shape
    d = w.shape[1]
    tm, _, _ = _mm_tiles(t, d, k, a.dtype.itemsize, w.dtype.itemsize, 4 + 2, True)

    def body(a_ref, w_ref, r_ref, g_ref, x_ref, h_ref):
        xv = r_ref[...] + _nn(a_ref[...], w_ref[...])
        x_ref[...] = xv
        r = lax.rsqrt(jnp.mean(xv * xv, axis=-1, keepdims=True) + EPS)
        h_ref[...] = ((xv * r) * g_ref[...]).astype(BF16)

    row = pl.BlockSpec((tm, d), lambda i: (i, 0))
    return pl.pallas_call(
        body, name=name, grid=(t // tm,),
        in_specs=[pl.BlockSpec((tm, k), lambda i: (i, 0)), pl.BlockSpec((k, d), lambda i: (0, 0)), row,
                  pl.BlockSpec((1, d), lambda i: (0, 0))],
        out_specs=[row, row],
        out_shape=[jax.ShapeDtypeStruct((t, d), F32), jax.ShapeDtypeStruct((t, d), BF16)],
        compiler_params=_params("parallel"),
    )(a, w, res, g)


def _rmsnorm_bwd(x, g, dh, dres, name, with_bf16=False):
    t, d = x.shape
    tm = _pick(t, (512, 256, 128))

    def body(x_ref, g_ref, dh_ref, dres_ref, dx_ref, dg_ref, *dxb_ref):
        @pl.when(pl.program_id(0) == 0)
        def _():
            dg_ref[...] = jnp.zeros_like(dg_ref)

        xv = x_ref[...]
        r = lax.rsqrt(jnp.mean(xv * xv, axis=-1, keepdims=True) + EPS)
        xhat = xv * r
        dhv = dh_ref[...]
        dyg = dhv * g_ref[...]
        dx = dres_ref[...] + r * (dyg - xhat * jnp.mean(dyg * xhat, axis=-1, keepdims=True))
        dx_ref[...] = dx
        if with_bf16:
            dxb_ref[0][...] = dx.astype(BF16)
        dg_ref[...] += jnp.sum(dhv * xhat, axis=0, keepdims=True)

    row = pl.BlockSpec((tm, d), lambda i: (i, 0))
    vec = pl.BlockSpec((1, d), lambda i: (0, 0))
    extra = with_bf16 * [jax.ShapeDtypeStruct((t, d), BF16)]
    return pl.pallas_call(
        body, name=name, grid=(t // tm,),
        in_specs=[row, vec, row, row], out_specs=[row, vec] + with_bf16 * [row],
        out_shape=[jax.ShapeDtypeStruct((t, d), F32), jax.ShapeDtypeStruct((1, d), F32)] + extra,
        compiler_params=_params("arbitrary"),
    )(x, g, dh, dres)


def _down_proj_loss_head(act, w_down, res, g, target, name):
    t, k = act.shape
    d = w_down.shape[1]
    tm, _, _ = _mm_tiles(t, d, k, act.dtype.itemsize, w_down.dtype.itemsize, 4 + 2, True)
    tm = min(tm, 512)

    def body(a_ref, w_ref, r_ref, g_ref, t_ref, loss_ref, dx_ref, dg_ref, dxb_ref):
        @pl.when(pl.program_id(0) == 0)
        def _():
            dg_ref[...] = jnp.zeros_like(dg_ref)
            loss_ref[...] = jnp.zeros_like(loss_ref)

        xv = r_ref[...] + _nn(a_ref[...], w_ref[...])
        gv = g_ref[...]
        r = lax.rsqrt(jnp.mean(xv * xv, axis=-1, keepdims=True) + EPS)
        xhat = xv * r
        err = xhat * gv - t_ref[...]
        loss_ref[...] += jnp.sum(err * err) * (0.5 / d)
        dy = err * (1.0 / d)
        dyg = dy * gv
        dx = r * (dyg - xhat * jnp.mean(dyg * xhat, axis=-1, keepdims=True))
        dx_ref[...] = dx
        dxb_ref[...] = dx.astype(BF16)
        dg_ref[...] += jnp.sum(dy * xhat, axis=0, keepdims=True)

    row = pl.BlockSpec((tm, d), lambda i: (i, 0))
    vec = pl.BlockSpec((1, d), lambda i: (0, 0))
    return pl.pallas_call(
        body, name=name, grid=(t // tm,),
        in_specs=[pl.BlockSpec((tm, k), lambda i: (i, 0)), pl.BlockSpec((k, d), lambda i: (0, 0)), row, vec, row],
        out_specs=[pl.BlockSpec((1, LANES), lambda i: (0, 0)), row, vec, row],
        out_shape=[jax.ShapeDtypeStruct((1, LANES), F32), jax.ShapeDtypeStruct((t, d), F32),
                   jax.ShapeDtypeStruct((1, d), F32), jax.ShapeDtypeStruct((t, d), BF16)],
        compiler_params=_params("arbitrary"),
    )(act, w_down, res, g, target)


CONV_HALO = 8
CONV_ROWS = 64


def _conv_taps(window, wv, bv):
    acc = bv + wv[SSM_CONV - 1:SSM_CONV, :] * window(0)
    for sh in range(1, SSM_CONV):
        kidx = SSM_CONV - 1 - sh
        acc = acc + wv[kidx:kidx + 1, :] * window(sh)
    return acc


def _conv_fwd(u, w, bias, name):
    b, s, c = u.shape
    rows = CONV_ROWS

    def body(u_ref, w_ref, b_ref, o_ref, ext):
        ext[0:CONV_HALO, :] = jnp.zeros((CONV_HALO, LANES), F32)
        ext[CONV_HALO:, :] = u_ref[...]
        wv, bv = w_ref[...], b_ref[...]
        for r0 in range(0, s, rows):
            acc = _conv_taps(lambda sh: ext[CONV_HALO + r0 - sh:CONV_HALO + r0 - sh + rows, :], wv, bv)
            o_ref[r0:r0 + rows, :] = acc * _sigmoid(acc)

    strip = pl.BlockSpec((None, s, LANES), lambda bi, j: (bi, 0, j))
    return pl.pallas_call(
        body, name=name, grid=(b, c // LANES),
        in_specs=[strip, pl.BlockSpec((SSM_CONV, LANES), lambda bi, j: (0, j)),
                  pl.BlockSpec((1, LANES), lambda bi, j: (0, j))],
        out_specs=strip, out_shape=jax.ShapeDtypeStruct((b, s, c), F32),
        scratch_shapes=[pltpu.VMEM((CONV_HALO + s, LANES), F32)],
        compiler_params=_params("parallel", "parallel"),
    )(u, w, bias)


def _conv_bwd(u, dout, w, bias, dproj, name):
    b, s, c = u.shape
    rows = CONV_ROWS

    def fold(v):
        return jnp.sum(v.reshape(rows // CONV_HALO, CONV_HALO, LANES), axis=0)

    def body(u_ref, d_ref, w_ref, b_ref, buf_ref, du_ref, dw_ref, db_ref, ext, dpre):
        @pl.when(pl.program_id(1) == 0)
        def _():
            dw_ref[...] = jnp.zeros_like(dw_ref)
            db_ref[...] = jnp.zeros_like(db_ref)

        ext[0:CONV_HALO, :] = jnp.zeros((CONV_HALO, LANES), F32)
        ext[CONV_HALO:, :] = u_ref[...]
        dpre[s:, :] = jnp.zeros((CONV_HALO, LANES), F32)
        wv, bv = w_ref[...], b_ref[...]
        sums = [jnp.zeros((CONV_HALO, LANES), F32)] * (SSM_CONV + 1)
        for r0 in range(0, s, rows):
            window = lambda sh: ext[CONV_HALO + r0 - sh:CONV_HALO + r0 - sh + rows, :]
            acc = _conv_taps(window, wv, bv)
            sg = _sigmoid(acc)
            dp = d_ref[r0:r0 + rows, :] * (sg * (1.0 + acc * (1.0 - sg)))
            dpre[r0:r0 + rows, :] = dp
            taps = [sums[SSM_CONV - 1 - sh] + fold(dp * window(sh)) for sh in range(SSM_CONV)]
            sums = taps[::-1] + [sums[SSM_CONV] + fold(dp)]
        for r0 in range(0, s, rows):
            du = wv[SSM_CONV - 1:SSM_CONV, :] * dpre[r0:r0 + rows, :]
            for sh in range(1, SSM_CONV):
                kidx = SSM_CONV - 1 - sh
                du = du + wv[kidx:kidx + 1, :] * dpre[r0 + sh:r0 + sh + rows, :]
            du_ref[r0:r0 + rows, :] = du.astype(BF16)
        for kidx in range(SSM_CONV):
            dw_ref[kidx:kidx + 1, :] += jnp.sum(sums[kidx], axis=0, keepdims=True)
        db_ref[...] += jnp.sum(sums[SSM_CONV], axis=0, keepdims=True)

    strip = pl.BlockSpec((None, s, LANES), lambda j, bi: (bi, 0, j))
    taps = pl.BlockSpec((SSM_CONV, LANES), lambda j, bi: (0, j))
    vec = pl.BlockSpec((1, LANES), lambda j, bi: (0, j))
    du_cols = pl.BlockSpec((None, s, LANES), lambda j, bi: (bi, 0, DPROJ_COLS["xbc"] // LANES + j))
    return pl.pallas_call(
        body, name=name, grid=(c // LANES, b),
        in_specs=[strip, strip, taps, vec, pl.BlockSpec(memory_space=pl.ANY)], out_specs=[du_cols, taps, vec],
        input_output_aliases={4: 0},
        out_shape=[jax.ShapeDtypeStruct(dproj.shape, dproj.dtype), jax.ShapeDtypeStruct((SSM_CONV, c), F32),
                   jax.ShapeDtypeStruct((1, c), F32)],
        scratch_shapes=[pltpu.VMEM((CONV_HALO + s, LANES), F32), pltpu.VMEM((s + CONV_HALO, LANES), F32)],
        compiler_params=_params("parallel", "arbitrary"),
    )(u, dout, w, bias, dproj)


def _ssd_chunk_terms(dtr_ref, bias_ref, alog_ref):
    q = SSM_CHUNK
    dt = _softplus(dtr_ref[...] + bias_ref[...])
    a_neg = -jnp.exp(alog_ref[...])
    row = lax.broadcasted_iota(jnp.int32, (q, q), 0)
    col = lax.broadcasted_iota(jnp.int32, (q, q), 1)
    lower = row >= col
    s = _mask_nn(lower, dt * a_neg)
    return dt, a_neg, s, s.T, lower


def _head_masks():
    heads = jnp.arange(LANES)[:, None]
    chans = jnp.arange(SSM_D_INNER)[None, :]
    to_channels = (chans // SSM_HEAD_DIM == heads).astype(BF16)
    return to_channels, to_channels.T


def _per_channel(v, to_channels):
    hi = v.astype(BF16)
    lo = (v - hi.astype(F32)).astype(BF16)
    return _nn(hi, to_channels) + _nn(lo, to_channels)


def _per_head(v, to_heads):
    hi = v.astype(BF16)
    lo = (v - hi.astype(F32)).astype(BF16)
    return _nn(hi, to_heads) + _nn(lo, to_heads)


def _decay_terms_per_channel(dt, s_col, to_channels):
    q = SSM_CHUNK
    tot = s_col[q - 1:q, :]
    stacked = jnp.concatenate([dt, jnp.exp(s_col), jnp.exp(tot - s_col)], axis=0)
    wide = _per_channel(stacked, to_channels)
    dtx, esx, decx = wide[:q], wide[q:2 * q], wide[2 * q:]
    return dtx, esx, decx, esx[0:1, :] * decx[0:1, :]


SSM_PAIRS_PER_GROUP = SSM_HEADS_PER_GROUP // 2
SSM_GROUP_CHANNELS = SSM_HEADS_PER_GROUP * SSM_HEAD_DIM


def _split_pair(v):
    first = lax.broadcasted_iota(jnp.int32, v.shape, 1) < SSM_HEAD_DIM
    return jnp.concatenate([jnp.where(first, v, 0.0), jnp.where(first, 0.0, v)], axis=0)


def _ssd_fwd(xc, dtr, dt_bias, a_log, dskx, to_channels, name):
    b, s, _ = xc.shape
    q = SSM_CHUNK
    nc = s // q
    n, gc = SSM_D_STATE, SSM_GROUP_CHANNELS

    def body(xc_ref, dtr_ref, bias_ref, alog_ref, dsk_ref, tc_ref, y_ref, hs_ref, h_scr):
        @pl.when(pl.program_id(1) == 0)
        def _():
            h_scr[...] = jnp.zeros_like(h_scr)

        dt, _, s_col, s_row, lower = _ssd_chunk_terms(dtr_ref, bias_ref, alog_ref)
        dtx, esx, decx, etotx = _decay_terms_per_channel(dt, s_col, tc_ref[...])
        x = xc_ref[:, :SSM_D_INNER]
        xdt = x * dtx
        xdec = xdt * decx
        skip = dsk_ref[...] * x
        for g in range(SSM_N_GROUPS):
            bg = xc_ref[:, SSM_D_INNER + n * g:SSM_D_INNER + n * (g + 1)].astype(BF16)
            cg = xc_ref[:, SSM_D_INNER + n * (SSM_N_GROUPS + g):SSM_D_INNER + n * (SSM_N_GROUPS + g + 1)].astype(BF16)
            gsl = slice(gc * g, gc * (g + 1))
            gm = _nt(cg, bg)
            hgt = h_scr[:, gsl]
            hs_ref[:, gsl] = hgt
            y_off = esx[:, gsl] * _nn(cg, hgt)
            h_scr[:, gsl] = etotx[:, gsl] * hgt + _tn(bg, xdec[:, gsl])
            for k in range(SSM_PAIRS_PER_GROUP):
                h0 = g * SSM_HEADS_PER_GROUP + 2 * k
                lo = gc * g + LANES * k
                ms = []
                for h in (h0, h0 + 1):
                    lm = jnp.exp(jnp.where(lower, s_col[:, h:h + 1] - s_row[h:h + 1, :], NEG_INF))
                    ms.append((gm * lm).astype(BF16))
                y_diag = _nn(jnp.concatenate(ms, axis=1), _split_pair(xdt[:, lo:lo + LANES]))
                y_ref[:, lo:lo + LANES] = y_diag + y_off[:, LANES * k:LANES * (k + 1)] + skip[:, lo:lo + LANES]

    vec = pl.BlockSpec((1, LANES), lambda bi, c: (0, 0))
    return pl.pallas_call(
        body, name=name, grid=(b, nc),
        in_specs=[pl.BlockSpec((None, q, SSM_CONV_DIM), lambda bi, c: (bi, c, 0)),
                  pl.BlockSpec((None, q, LANES), lambda bi, c: (bi, c, 0)), vec, vec,
                  pl.BlockSpec((1, SSM_D_INNER), lambda bi, c: (0, 0)),
                  pl.BlockSpec((LANES, SSM_D_INNER), lambda bi, c: (0, 0))],
        out_specs=[pl.BlockSpec((None, q, SSM_D_INNER), lambda bi, c: (bi, c, 0)),
                   pl.BlockSpec((None, None, n, SSM_D_INNER), lambda bi, c: (bi, c, 0, 0))],
        out_shape=[jax.ShapeDtypeStruct((b, s, SSM_D_INNER), F32),
                   jax.ShapeDtypeStruct((b, nc, n, SSM_D_INNER), F32)],
        scratch_shapes=[pltpu.VMEM((n, SSM_D_INNER), F32)],
        compiler_params=_params("parallel", "arbitrary"),
    )(xc, dtr, dt_bias, a_log, dskx, to_channels)


def _ssd_bwd(xc, dtr, dy, hs, dt_bias, a_log, dskx, to_channels, to_heads, dproj, name):
    b, s, _ = xc.shape
    q = SSM_CHUNK
    nc = s // q
    n, gc = SSM_D_STATE, SSM_GROUP_CHANNELS

    def colsum(v):
        return jnp.sum(v, axis=0, keepdims=True)

    def body(xc_ref, dtr_ref, dy_ref, hs_ref, bias_ref, alog_ref, dsk_ref, tc_ref, th_ref, buf_ref,
             dxc_ref, ddtr_ref, dalog_ref, ddsk_ref, dbias_ref, dh_scr, dxs_scr, dxd_scr, w_scr, dst_scr, rows_scr):
        ci = pl.program_id(1)

        @pl.when(ci == 0)
        def _():
            dh_scr[...] = jnp.zeros_like(dh_scr)

        @pl.when(jnp.logical_and(pl.program_id(0) == 0, ci == 0))
        def _():
            dalog_ref[...] = jnp.zeros_like(dalog_ref)
            ddsk_ref[...] = jnp.zeros_like(ddsk_ref)
            dbias_ref[...] = jnp.zeros_like(dbias_ref)
            dst_scr[...] = jnp.zeros_like(dst_scr)

        dt, a_neg, s_col, s_row, lower = _ssd_chunk_terms(dtr_ref, bias_ref, alog_ref)
        upper = jnp.logical_not(lower) | (lax.broadcasted_iota(jnp.int32, (q, q), 0)
                                          == lax.broadcasted_iota(jnp.int32, (q, q), 1))
        dtx, esx, decx, etotx = _decay_terms_per_channel(dt, s_col, tc_ref[...])
        x = xc_ref[:, :SSM_D_INNER]
        dyv = dy_ref[...]
        xdt = x * dtx
        xdec = xdt * decx
        dw = esx * dyv
        rows_scr[...] = jnp.zeros_like(rows_scr)
        for g in range(SSM_N_GROUPS):
            b_lo = SSM_D_INNER + n * g
            c_lo = SSM_D_INNER + n * (SSM_N_GROUPS + g)
            bg = xc_ref[:, b_lo:b_lo + n].astype(BF16)
            cg = xc_ref[:, c_lo:c_lo + n].astype(BF16)
            gsl = slice(gc * g, gc * (g + 1))
            gm = _nt(cg, bg)
            gmt = _nt(bg, cg)
            hgt = hs_ref[:, gsl]
            dhgt = dh_scr[:, gsl]
            w_scr[:, gsl] = _nn(cg, hgt)
            dcg = _nt(dw[:, gsl], hgt)
            dxs = decx[:, gsl] * _nn(bg, dhgt)
            dxs_scr[:, gsl] = dxs
            dbg = _nt(xdec[:, gsl], dhgt)
            rows_scr[2:3, gsl] = colsum(dhgt * hgt)
            dh_scr[:, gsl] = _tn(cg, dw[:, gsl]) + etotx[:, gsl] * dhgt
            dg = jnp.zeros((q, q), F32)
            dgt = jnp.zeros((q, q), F32)
            for k in range(SSM_PAIRS_PER_GROUP):
                h0 = g * SSM_HEADS_PER_GROUP + 2 * k
                lo = gc * g + LANES * k
                xp = xdt[:, lo:lo + LANES]
                dyp = dyv[:, lo:lo + LANES]
                dy2 = _split_pair(dyp)
                dm2 = _nt(dy2, xp)
                dmt2 = _nt(_split_pair(xp), dyp)
                mts = []
                for i, h in enumerate((h0, h0 + 1)):
                    lm = jnp.exp(jnp.where(lower, s_col[:, h:h + 1] - s_row[h:h + 1, :], NEG_INF))
                    lmt = jnp.exp(jnp.where(upper, s_row[h:h + 1, :] - s_col[:, h:h + 1], NEG_INF))
                    dm = dm2[q * i:q * (i + 1), :]
                    dmt = dmt2[q * i:q * (i + 1), :]
                    dg = dg + dm * lm
                    dgt = dgt + dmt * lmt
                    mt = gmt * lmt
                    dst_scr[h:h + 1, :] = colsum(dmt * mt) - colsum(dm * (gm * lm))
                    mts.append(mt.astype(BF16))
                dxd_scr[:, lo:lo + LANES] = _nn(jnp.concatenate(mts, axis=1), dy2)
            dxc_ref[:, b_lo:b_lo + n] = dbg + _nn(dgt, cg)
            dxc_ref[:, c_lo:c_lo + n] = dcg + _nn(dg, bg)
        dxs = dxs_scr[...]
        dxdt = dxd_scr[...] + dxs
        dxc_ref[:, :SSM_D_INNER] = dxdt * dtx + dsk_ref[...] * dyv
        state_part = xdt * dxs
        rows_scr[0:1, :] = colsum(dyv * x)
        rows_scr[1:2, :] = colsum(state_part)
        th = th_ref[...]
        per_head = _per_head(jnp.concatenate([dw * w_scr[...] - state_part, dxdt * x], axis=0), th)
        r_ds, r_dt = per_head[:q], per_head[q:]
        sums = _per_head(rows_scr[...], th)
        etot = jnp.exp(s_col[q - 1:q, :])
        dtot = sums[1:2, :] + etot * sums[2:3, :]
        last = lax.broadcasted_iota(jnp.int32, (q, LANES), 0) == q - 1
        ds = dst_scr[...].T + r_ds + jnp.where(last, dtot, 0.0)
        da = _mask_nn(upper, ds)
        ddt = da * a_neg + r_dt
        live = lax.broadcasted_iota(jnp.int32, (1, LANES), 1) < SSM_N_HEADS
        sg = _sigmoid(dtr_ref[...] + bias_ref[...])
        ddtr = jnp.where(live, ddt * sg, 0.0)
        ddtr_ref[:, :LANES] = ddtr.astype(BF16)
        ddtr_ref[:, LANES:] = jnp.zeros((q, DPROJ_DT_WIDTH - LANES), BF16)
        dalog_ref[...] += jnp.where(live, colsum(da * dt) * a_neg, 0.0)
        ddsk_ref[...] += jnp.where(live, sums[0:1, :], 0.0)
        dbias_ref[...] += colsum(ddtr)

    rev = lambda bi, c: (bi, nc - 1 - c, 0)
    vec = pl.BlockSpec((1, LANES), lambda bi, c: (0, 0))
    wide = pl.BlockSpec((None, q, SSM_D_INNER), rev)
    return pl.pallas_call(
        body, name=name, grid=(b, nc),
        in_specs=[pl.BlockSpec((None, q, SSM_CONV_DIM), rev), pl.BlockSpec((None, q, LANES), rev), wide,
                  pl.BlockSpec((None, None, n, SSM_D_INNER), lambda bi, c: (bi, nc - 1 - c, 0, 0)),
                  vec, vec, pl.BlockSpec((1, SSM_D_INNER), lambda bi, c: (0, 0)),
                  pl.BlockSpec((LANES, SSM_D_INNER), lambda bi, c: (0, 0)),
                  pl.BlockSpec((SSM_D_INNER, LANES), lambda bi, c: (0, 0)),
                  pl.BlockSpec(memory_space=pl.ANY)],
        out_specs=[pl.BlockSpec((None, q, SSM_CONV_DIM), rev),
                   pl.BlockSpec((None, q, DPROJ_DT_WIDTH),
                                lambda bi, c: (bi, nc - 1 - c, DPROJ_COLS["dt"] // DPROJ_DT_WIDTH)), vec, vec, vec],
        input_output_aliases={9: 1},
        out_shape=[jax.ShapeDtypeStruct((b, s, SSM_CONV_DIM), F32), jax.ShapeDtypeStruct(dproj.shape, dproj.dtype),
                   jax.ShapeDtypeStruct((1, LANES), F32), jax.ShapeDtypeStruct((1, LANES), F32),
                   jax.ShapeDtypeStruct((1, LANES), F32)],
        scratch_shapes=[pltpu.VMEM((n, SSM_D_INNER), F32)] + [pltpu.VMEM((q, SSM_D_INNER), F32)] * 3
        + [pltpu.VMEM((LANES, q), F32), pltpu.VMEM((8, SSM_D_INNER), F32)],
        compiler_params=_params("arbitrary", "arbitrary"),
    )(xc, dtr, dy, hs, dt_bias, a_log, dskx, to_channels, to_heads, dproj)


SSM_GROUP_WIDTH = SSM_D_INNER // SSM_N_GROUPS


def _gate_norm_fwd(y, z, w, name):
    t, d = y.shape
    tm = _pick(t, (256, 128))

    def body(y_ref, z_ref, w_ref, o_ref):
        for g in range(SSM_N_GROUPS):
            sl = slice(SSM_GROUP_WIDTH * g, SSM_GROUP_WIDTH * (g + 1))
            zv = z_ref[:, sl]
            u = y_ref[:, sl] * (zv * _sigmoid(zv))
            r = lax.rsqrt(jnp.mean(u * u, axis=-1, keepdims=True) + EPS)
            o_ref[:, sl] = ((u * r) * w_ref[:, sl]).astype(BF16)

    row = pl.BlockSpec((tm, d), lambda i: (i, 0))
    return pl.pallas_call(
        body, name=name, grid=(t // tm,),
        in_specs=[row, row, pl.BlockSpec((1, d), lambda i: (0, 0))], out_specs=row,
        out_shape=jax.ShapeDtypeStruct((t, d), BF16),
        compiler_params=_params("parallel"),
    )(y, z, w)


def _gate_norm_bwd(y, z, w, dout, dproj, name):
    t, d = y.shape
    gw = SSM_GROUP_WIDTH
    tm = _pick(t, (1024, 512, 256, 128))

    def body(y_ref, z_ref, w_ref, do_ref, buf_ref, dy_ref, dz_ref, dw_ref):
        @pl.when(pl.program_id(1) == 0)
        def _():
            dw_ref[...] = jnp.zeros_like(dw_ref)

        zv = z_ref[...]
        yv = y_ref[...]
        sg = _sigmoid(zv)
        silu = zv * sg
        u = yv * silu
        r = lax.rsqrt(jnp.mean(u * u, axis=-1, keepdims=True) + EPS)
        uh = u * r
        dov = do_ref[...]
        dw_ref[...] += jnp.sum(dov * uh, axis=0, keepdims=True)
        dyg = dov * w_ref[...]
        du = r * (dyg - uh * jnp.mean(dyg * uh, axis=-1, keepdims=True))
        dy_ref[...] = du * silu
        dz_ref[...] = (du * yv * (sg * (1.0 + zv * (1.0 - sg)))).astype(BF16)

    tile = pl.BlockSpec((tm, gw), lambda g, i: (i, g))
    vec = pl.BlockSpec((1, gw), lambda g, i: (0, g))
    z_cols = pl.BlockSpec((tm, gw), lambda g, i: (i, DPROJ_COLS["z"] // gw + g))
    return pl.pallas_call(
        body, name=name, grid=(SSM_N_GROUPS, t // tm),
        in_specs=[tile, tile, vec, tile, pl.BlockSpec(memory_space=pl.ANY)], out_specs=[tile, z_cols, vec],
        out_shape=[jax.ShapeDtypeStruct((t, d), F32), jax.ShapeDtypeStruct(dproj.shape, dproj.dtype),
                   jax.ShapeDtypeStruct((1, d), F32)],
        input_output_aliases={4: 1},
        compiler_params=_params("parallel", "arbitrary"),
    )(y, z, w, dout, dproj)


def _rope_tables(s):
    half = ATT_HEAD_DIM // 2
    inv = ROPE_THETA ** (-jnp.arange(half, dtype=F32) / half)
    ang = jnp.arange(s).astype(F32)[:, None] * inv[None, :]
    cos, sin = jnp.cos(ang), jnp.sin(ang)
    return jnp.concatenate([cos, cos], axis=-1), jnp.concatenate([-sin, sin], axis=-1)


ATT_TILE = 256


def _by_residue_spec(r, width):
    return pl.BlockSpec((None, r, ATT_TILE // r, width), lambda bi, i: (bi, 0, i, 0))


def _to_residues(tile, stage, r, store):
    if r == 1:
        store(0, tile)
        return
    stage[...] = tile
    for ri in range(r):
        store(ri, stage[pl.ds(ri, tile.shape[0] // r, stride=r), :])


def _from_residues(load, stage, r):
    if r == 1:
        return load(0)
    for ri in range(r):
        stage[pl.ds(ri, ATT_TILE // r, stride=r), :] = load(ri)
    return stage[...]


QKV_ROWS = 1024
QKV_COLS = 768


def _qkv_proj_rope(h, w_qkv_t, cosf, sinf, b, s, name):
    t, k = h.shape
    tm, d, gw = QKV_ROWS, ATT_HEAD_DIM, ATT_OUT_DIM
    per_seq = s // tm

    def body(h_ref, w_ref, c_ref, s_ref, *rest):
        outs, stage = rest[:-1], rest[-1]
        cv, sv = c_ref[...], s_ref[...]
        hv = h_ref[...]
        for lo in range(0, ATT_QKV_DIM, QKV_COLS):
            acc = _nt(hv, w_ref[lo:lo + QKV_COLS, :])
            for hh in range(QKV_COLS // d):
                kind, head = divmod(lo // d + hh, ATT_N_HEADS)
                gi, j = divmod(head, ATT_HEADS_PER_GROUP)
                dst = slice(kind * gw + d * j, kind * gw + d * (j + 1))
                tv = acc[:, d * hh:d * (hh + 1)]
                if kind < 2:
                    tv = tv * cv + pltpu.roll(tv, d // 2, 1) * sv

                def store(ri, rows, o_ref=outs[gi], dst=dst):
                    o_ref[ri, :, dst] = rows.astype(BF16)

                _to_residues(tv, stage, ATT_DILATIONS[gi], store)

    tab = pl.BlockSpec((tm, d), lambda i: (i % per_seq, 0))
    return pl.pallas_call(
        body, name=name, grid=(t // tm,),
        in_specs=[pl.BlockSpec((tm, k), lambda i: (i, 0)), pl.BlockSpec((ATT_QKV_DIM, k), lambda i: (0, 0)), tab, tab],
        out_specs=[pl.BlockSpec((None, r, tm // r, 3 * gw), lambda i: (i // per_seq, 0, i % per_seq, 0))
                   for r in ATT_DILATIONS],
        out_shape=[jax.ShapeDtypeStruct((b, r, s // r, 3 * gw), BF16) for r in ATT_DILATIONS],
        scratch_shapes=[pltpu.VMEM((tm, d), F32)],
        compiler_params=_params("parallel"),
    )(h, w_qkv_t, cosf, sinf)


def _rope_bwd(dq, dk, dv, cosf, sinf, dproj, name):
    n_pat = len(ATT_DILATIONS)
    b, _, s, gw = dq[0].shape
    ts, d = ATT_TILE, ATT_HEAD_DIM

    def body(*refs):
        ins, (c_ref, s_ref, _, o_ref, stage) = refs[:3 * n_pat], refs[3 * n_pat:]
        cv, sv = c_ref[...], s_ref[...]
        for kind in range(3):
            for gi, r in enumerate(ATT_DILATIONS):
                src = ins[kind * n_pat + gi]
                for j in range(ATT_HEADS_PER_GROUP):
                    tv = _from_residues(lambda ri, src=src, j=j: src[ri, :, d * j:d * (j + 1)], stage, r)
                    if kind < 2:
                        tv = tv * cv + pltpu.roll(tv * sv, d // 2, 1)
                    lo = d * (kind * ATT_N_HEADS + gi * ATT_HEADS_PER_GROUP + j)
                    o_ref[:, lo:lo + d] = tv.astype(BF16)

    tab = pl.BlockSpec((ts, d), lambda bi, i: (i, 0))
    parts = [_by_residue_spec(r, gw) for r in ATT_DILATIONS]
    return pl.pallas_call(
        body, name=name, grid=(b, s // ts), in_specs=parts * 3 + [tab, tab, pl.BlockSpec(memory_space=pl.ANY)],
        out_specs=pl.BlockSpec((None, ts, ATT_QKV_DIM), lambda bi, i: (bi, i, DPROJ_COLS["qkv"] // ATT_QKV_DIM)),
        out_shape=jax.ShapeDtypeStruct(dproj.shape, dproj.dtype),
        input_output_aliases={3 * n_pat + 2: 0},
        scratch_shapes=[pltpu.VMEM((ts, d), F32)],
        compiler_params=_params("parallel", "parallel"),
    )(*dq, *dk, *dv, cosf, sinf, dproj)


ATT_SCALE = ATT_HEAD_DIM ** -0.5
ATT_STEP = 2 * ATT_BLOCK


def _att_spec(col):
    return pl.BlockSpec((None, None, ATT_STEP, ATT_OUT_DIM), lambda bi, ri, i: (bi, ri, i, col))


def _att_edge_spec(col, side, n_steps):
    def index(bi, ri, i):
        blk = 2 * i - 1 if side < 0 else 2 * i + 2
        return (bi, ri, jnp.clip(blk, 0, 2 * n_steps - 1), col)
    return pl.BlockSpec((None, None, ATT_BLOCK, ATT_OUT_DIM), index)


def _band_mask(shape, q_axis, has_prev):
    qi = lax.broadcasted_iota(jnp.int32, shape, q_axis)
    kj = lax.broadcasted_iota(jnp.int32, shape, 1 - q_axis)
    dist = qi + ATT_BLOCK - kj
    return (dist >= 0) & (dist <= ATT_BLOCK) & (has_prev | (kj >= ATT_BLOCK))


def _att_fwd(qkr, name):
    b, r, l, _ = qkr.shape
    nb = l // ATT_STEP
    d = ATT_HEAD_DIM

    def body(q_ref, kp_ref, k_ref, vp_ref, v_ref, o_ref, lse_ref):
        mask = _band_mask((ATT_STEP, ATT_BLOCK + ATT_STEP), 0, pl.program_id(2) > 0)
        heads = [slice(d * j, d * (j + 1)) for j in range(ATT_HEADS_PER_GROUP)]
        scores = [_nt(q_ref[:, sl], jnp.concatenate([kp_ref[:, sl], k_ref[:, sl]], axis=0)) for sl in heads]
        scores = [jnp.where(mask, sc * ATT_SCALE, NEG_INF) for sc in scores]
        tops = [jnp.max(sc, axis=-1, keepdims=True) for sc in scores]
        probs = [jnp.exp(sc - m) for sc, m in zip(scores, tops)]
        dens = [jnp.sum(pr, axis=-1, keepdims=True) for pr in probs]
        for sl, m, pr, den in zip(heads, tops, probs, dens):
            o_ref[:, sl] = _nn(pr / den, jnp.concatenate([vp_ref[:, sl], v_ref[:, sl]], axis=0))
            lse_ref[:, sl] = jnp.broadcast_to(m + jnp.log(den), (ATT_STEP, d))

    out_spec = _att_spec(0)
    return pl.pallas_call(
        body, name=name, grid=(b, r, nb),
        in_specs=[_att_spec(0), _att_edge_spec(1, -1, nb), _att_spec(1), _att_edge_spec(2, -1, nb), _att_spec(2)],
        out_specs=[out_spec, out_spec],
        out_shape=[jax.ShapeDtypeStruct((b, r, l, ATT_OUT_DIM), F32)] * 2,
        compiler_params=_params("parallel", "parallel", "parallel"),
    )(qkr, qkr, qkr, qkr, qkr)


def _att_merge(os_, lses, name):
    n_pat = len(os_)
    b, _, s, gw = os_[0].shape
    ts, d = ATT_TILE, ATT_HEAD_DIM

    def body(*refs):
        o_refs, l_refs = refs[:n_pat], refs[n_pat:2 * n_pat]
        att_ref, lse_outs, stage = refs[2 * n_pat], refs[2 * n_pat + 1:3 * n_pat + 1], refs[-1]
        for j in range(ATT_HEADS_PER_GROUP):
            sl = slice(d * j, d * (j + 1))
            ov = [_from_residues(lambda ri, g=g: o_refs[g][ri, :, sl], stage, r)
                  for g, r in enumerate(ATT_DILATIONS)]
            ls = [_from_residues(lambda ri, g=g: l_refs[g][ri, :, sl], stage, r)
                  for g, r in enumerate(ATT_DILATIONS)]
            m = functools.reduce(jnp.maximum, ls)
            es = [jnp.exp(lv - m) for lv in ls]
            tot = functools.reduce(lambda u, v: u + v, es)
            acc = (es[0] / tot) * ov[0]
            for g in range(1, n_pat):
                acc = acc + (es[g] / tot) * ov[g]
            att_ref[:, sl] = acc
            joint = m + jnp.log(tot)
            for g, r in enumerate(ATT_DILATIONS):
                def store(ri, rows, out=lse_outs[g]):
                    out[ri, :, sl] = rows
                _to_residues(joint, stage, r, store)

    parts = [_by_residue_spec(r, gw) for r in ATT_DILATIONS]
    return pl.pallas_call(
        body, name=name, grid=(b, s // ts), in_specs=parts * 2,
        out_specs=[pl.BlockSpec((None, ts, gw), lambda bi, i: (bi, i, 0))] + parts,
        out_shape=[jax.ShapeDtypeStruct((b, s, gw), F32)]
        + [jax.ShapeDtypeStruct((b, r, s // r, gw), F32) for r in ATT_DILATIONS],
        scratch_shapes=[pltpu.VMEM((ts, d), F32)],
        compiler_params=_params("parallel", "parallel"),
    )(*os_, *lses)


def _att_delta(att, datt, name):
    b, s, gw = att.shape
    ts, d = ATT_TILE, ATT_HEAD_DIM
    n_pat = len(ATT_DILATIONS)

    def body(a_ref, d_ref, *rest):
        do_outs, dl_outs, stage = rest[:n_pat], rest[n_pat:2 * n_pat], rest[-1]
        for j in range(ATT_HEADS_PER_GROUP):
            sl = slice(d * j, d * (j + 1))
            dv = d_ref[:, sl]
            delta = jnp.broadcast_to(jnp.sum(a_ref[:, sl] * dv, axis=-1, keepdims=True), (ts, d))
            for g, r in enumerate(ATT_DILATIONS):
                def store_do(ri, rows, out=do_outs[g]):
                    out[ri, :, sl] = rows.astype(BF16)

                def store_dl(ri, rows, out=dl_outs[g]):
                    out[ri, :, sl] = rows

                _to_residues(dv, stage, r, store_do)
                _to_residues(delta, stage, r, store_dl)

    row = pl.BlockSpec((None, ts, gw), lambda bi, i: (bi, i, 0))
    parts = [_by_residue_spec(r, gw) for r in ATT_DILATIONS]
    outs = pl.pallas_call(
        body, name=name, grid=(b, s // ts), in_specs=[row, row], out_specs=parts * 2,
        out_shape=[jax.ShapeDtypeStruct((b, r, s // r, gw), BF16) for r in ATT_DILATIONS]
        + [jax.ShapeDtypeStruct((b, r, s // r, gw), F32) for r in ATT_DILATIONS],
        scratch_shapes=[pltpu.VMEM((ts, d), F32)],
        compiler_params=_params("parallel", "parallel"),
    )(att, datt)
    return outs[:n_pat], outs[n_pat:]


def _att_bwd_q(qkr, datt, lse, delta, name):
    b, r, l, _ = qkr.shape
    nb = l // ATT_STEP
    d = ATT_HEAD_DIM

    def body(q_ref, kp_ref, k_ref, vp_ref, v_ref, do_ref, lse_ref, dl_ref, dq_ref):
        mask = _band_mask((ATT_STEP, ATT_BLOCK + ATT_STEP), 0, pl.program_id(2) > 0)
        heads = [slice(d * j, d * (j + 1)) for j in range(ATT_HEADS_PER_GROUP)]
        kcats = [jnp.concatenate([kp_ref[:, sl], k_ref[:, sl]], axis=0) for sl in heads]
        scores = [_nt(q_ref[:, sl], kcat) for sl, kcat in zip(heads, kcats)]
        dps = [_nt(do_ref[:, sl], jnp.concatenate([vp_ref[:, sl], v_ref[:, sl]], axis=0)) for sl in heads]
        probs = [jnp.exp(jnp.where(mask, sc * ATT_SCALE - lse_ref[:, sl.start:sl.start + 1], NEG_INF))
                 for sl, sc in zip(heads, scores)]
        dscs = [pr * (dp - dl_ref[:, sl.start:sl.start + 1]) for sl, pr, dp in zip(heads, probs, dps)]
        for sl, dsc, kcat in zip(heads, dscs, kcats):
            dq_ref[:, sl] = _nn(dsc, kcat) * ATT_SCALE

    tok = _att_spec(0)
    return pl.pallas_call(
        body, name=name, grid=(b, r, nb),
        in_specs=[_att_spec(0), _att_edge_spec(1, -1, nb), _att_spec(1), _att_edge_spec(2, -1, nb), _att_spec(2),
                  tok, tok, tok],
        out_specs=tok,
        out_shape=jax.ShapeDtypeStruct((b, r, l, ATT_OUT_DIM), F32),
        compiler_params=_params("parallel", "parallel", "parallel"),
    )(qkr, qkr, qkr, qkr, qkr, datt, lse, delta)


def _att_bwd_kv(qkr, datt, lse, delta, name):
    b, r, l, _ = qkr.shape
    nb = l // ATT_STEP
    d = ATT_HEAD_DIM

    def body(k_ref, v_ref, q_ref, qn_ref, do_ref, don_ref, lse_ref, lsen_ref, dl_ref, dln_ref, dk_ref, dv_ref):
        shape = (ATT_STEP, ATT_STEP + ATT_BLOCK)
        kj = lax.broadcasted_iota(jnp.int32, shape, 0)
        qi = lax.broadcasted_iota(jnp.int32, shape, 1)
        dist = qi - kj
        has_next = pl.program_id(2) < nb - 1
        mask = (dist >= 0) & (dist <= ATT_BLOCK) & (has_next | (qi < ATT_STEP))
        def per_query(own_ref, next_ref, sl):
            return jnp.tile(jnp.concatenate([own_ref[:, sl], next_ref[:, sl]], axis=0).T, (ATT_STEP // d, 1))

        heads = [slice(d * j, d * (j + 1)) for j in range(ATT_HEADS_PER_GROUP)]
        qcats = [jnp.concatenate([q_ref[:, sl], qn_ref[:, sl]], axis=0) for sl in heads]
        docats = [jnp.concatenate([do_ref[:, sl], don_ref[:, sl]], axis=0) for sl in heads]
        scores = [_nt(k_ref[:, sl], qcat) for sl, qcat in zip(heads, qcats)]
        dps = [_nt(v_ref[:, sl], docat) for sl, docat in zip(heads, docats)]
        probs = [jnp.exp(jnp.where(mask, sc * ATT_SCALE - per_query(lse_ref, lsen_ref, sl), NEG_INF))
                 for sl, sc in zip(heads, scores)]
        for sl, pr, docat in zip(heads, probs, docats):
            dv_ref[:, sl] = _nn(pr, docat)
        dscs = [pr * (dp - per_query(dl_ref, dln_ref, sl)) for sl, pr, dp in zip(heads, probs, dps)]
        for sl, dsc, qcat in zip(heads, dscs, qcats):
            dk_ref[:, sl] = _nn(dsc, qcat) * ATT_SCALE

    tok, tok_n = _att_spec(0), _att_edge_spec(0, 1, nb)
    return pl.pallas_call(
        body, name=name, grid=(b, r, nb),
        in_specs=[_att_spec(1), _att_spec(2), _att_spec(0), _att_edge_spec(0, 1, nb),
                  tok, tok_n, tok, tok_n, tok, tok_n],
        out_specs=[tok, tok],
        out_shape=[jax.ShapeDtypeStruct((b, r, l, ATT_OUT_DIM), F32)] * 2,
        compiler_params=_params("parallel", "parallel", "parallel"),
    )(qkr, qkr, qkr, qkr, datt, datt, lse, lse, delta, delta)


def _mix_fwd(gl, bg, ys, ya, name):
    t, d = ys.shape
    tm = _pick(t, (512, 256, 128))

    def body(gl_ref, bg_ref, ys_ref, ya_ref, o_ref):
        g0 = _sigmoid(gl_ref[:, :d] + bg_ref[:, :d])
        g1 = _sigmoid(gl_ref[:, d:] + bg_ref[:, d:])
        o_ref[...] = (g0 * ys_ref[...] + g1 * ya_ref[...]).astype(BF16)

    row = pl.BlockSpec((tm, d), lambda i: (i, 0))
    return pl.pallas_call(
        body, name=name, grid=(t // tm,),
        in_specs=[pl.BlockSpec((tm, 2 * d), lambda i: (i, 0)), pl.BlockSpec((1, 2 * d), lambda i: (0, 0)), row, row],
        out_specs=row, out_shape=jax.ShapeDtypeStruct((t, d), BF16),
        compiler_params=_params("parallel"),
    )(gl, bg, ys, ya)


def _mix_bwd(gl, bg, ys, ya, dmixed, name):
    t, d = ys.shape
    tm = _pick(t, (512, 256, 128))

    def body(gl_ref, bg_ref, ys_ref, ya_ref, dm_ref, dys_ref, dya_ref, dgl_ref, dbg_ref):
        @pl.when(pl.program_id(0) == 0)
        def _():
            dbg_ref[...] = jnp.zeros_like(dbg_ref)

        dm = dm_ref[...]
        g0 = _sigmoid(gl_ref[:, :d] + bg_ref[:, :d])
        g1 = _sigmoid(gl_ref[:, d:] + bg_ref[:, d:])
        dys_ref[...] = (dm * g0).astype(BF16)
        dya_ref[...] = (dm * g1).astype(BF16)
        d0 = dm * ys_ref[...] * (g0 * (1.0 - g0))
        d1 = dm * ya_ref[...] * (g1 * (1.0 - g1))
        dgl_ref[:, :d] = d0.astype(BF16)
        dgl_ref[:, d:] = d1.astype(BF16)
        dbg_ref[:, :d] += jnp.sum(d0, axis=0, keepdims=True)
        dbg_ref[:, d:] += jnp.sum(d1, axis=0, keepdims=True)

    row = pl.BlockSpec((tm, d), lambda i: (i, 0))
    wide = pl.BlockSpec((tm, 2 * d), lambda i: (i, 0))
    vec = pl.BlockSpec((1, 2 * d), lambda i: (0, 0))
    gate_cols = pl.BlockSpec((tm, 2 * d), lambda i: (i, DPROJ_COLS["gate"] // (2 * d)))
    return pl.pallas_call(
        body, name=name, grid=(t // tm,),
        in_specs=[wide, vec, row, row, row], out_specs=[row, row, gate_cols, vec],
        out_shape=[jax.ShapeDtypeStruct((t, d), BF16), jax.ShapeDtypeStruct((t, d), BF16),
                   jax.ShapeDtypeStruct((t, DPROJ_WIDTH), BF16), jax.ShapeDtypeStruct((1, 2 * d), F32)],
        compiler_params=_params("arbitrary"),
    )(gl, bg, ys, ya, dmixed)


def _up_proj_swiglu(h, w_up_t, gt, name):
    t, k = h.shape
    f = w_up_t.shape[0]
    tm, tn, _ = _mm_tiles(t, f, k, h.dtype.itemsize, w_up_t.dtype.itemsize, 4 + 2, True)

    def body(h_ref, w_ref, g_ref, up_ref, act_ref):
        up = _nt(h_ref[...], w_ref[...])
        up_ref[...] = up
        gv = g_ref[...]
        act_ref[...] = ((gv * _sigmoid(gv)) * up).astype(BF16)

    tile = pl.BlockSpec((tm, tn), lambda i, j: (i, j))
    return pl.pallas_call(
        body, name=name, grid=(t // tm, f // tn),
        in_specs=[pl.BlockSpec((tm, k), lambda i, j: (i, 0)), pl.BlockSpec((tn, k), lambda i, j: (j, 0)), tile],
        out_specs=[tile, tile],
        out_shape=[jax.ShapeDtypeStruct((t, f), F32), jax.ShapeDtypeStruct((t, f), BF16)],
        compiler_params=_params("parallel", "parallel"),
    )(h, w_up_t, gt)


def _down_dx_swiglu_bwd(dx, w_down, gt, up, name):
    t, k = dx.shape
    f = w_down.shape[0]
    tm, tn, _ = _mm_tiles(t, f, k, dx.dtype.itemsize, w_down.dtype.itemsize, 2 + 2, True)
    tm = min(tm, 512)

    def body(d_ref, w_ref, g_ref, u_ref, dg_ref, du_ref):
        dact = _nt(d_ref[...], w_ref[...])
        gv = g_ref[...]
        sg = _sigmoid(gv)
        dg_ref[...] = (dact * u_ref[...] * (sg * (1.0 + gv * (1.0 - sg)))).astype(BF16)
        du_ref[...] = (dact * (gv * sg)).astype(BF16)

    tile = pl.BlockSpec((tm, tn), lambda i, j: (i, j))
    return pl.pallas_call(
        body, name=name, grid=(t // tm, f // tn),
        in_specs=[pl.BlockSpec((tm, k), lambda i, j: (i, 0)), pl.BlockSpec((tn, k), lambda i, j: (j, 0)), tile, tile],
        out_specs=[tile, tile], out_shape=[jax.ShapeDtypeStruct((t, f), BF16)] * 2,
        compiler_params=_params("parallel", "parallel"),
    )(dx, w_down, gt, up)


def _peer(k):
    x, y, c = lax.axis_index("x"), lax.axis_index("y"), lax.axis_index("c")
    px, py, pc = x ^ ((k >> 2) & 1), y ^ ((k >> 1) & 1), c ^ (k & 1)
    return (px, py, pc), 4 * px + 2 * py + pc


def _my_index():
    return 4 * lax.axis_index("x") + 2 * lax.axis_index("y") + lax.axis_index("c")


def _all_gather(parts, name):
    n_parts = len(parts)

    def body(*refs):
        ins, outs = refs[:n_parts], refs[n_parts:2 * n_parts]
        send_sems, recv_sems, local_sems = refs[2 * n_parts:]
        here, me = _peer(0)
        sibling, sib_idx = _peer(1)
        chips = [_peer(2 * q) for q in range(1, N_CHIPS)]

        def copy(i, k, block, to, src=None):
            return pltpu.make_async_remote_copy(
                src_ref=outs[i].at[block] if src is None else src, dst_ref=outs[i].at[block],
                send_sem=send_sems.at[i * (N_DEV - 1) + k], recv_sem=recv_sems.at[i * (N_DEV - 1) + k],
                device_id=to, device_id_type=MESH)

        local = [pltpu.make_async_copy(ins[i], outs[i].at[me], local_sems.at[i]) for i in range(n_parts)]
        for cp in local:
            cp.start()
        sends = []
        for i in range(n_parts):
            sends.append(copy(i, 0, me, sibling, src=ins[i]))
            sends += [copy(i, q, me, chip, src=ins[i]) for q, (chip, _) in enumerate(chips, start=1)]
        for cp in sends:
            cp.start()
        for q, (chip, chip_idx) in enumerate(chips, start=1):
            for i in range(n_parts):
                copy(i, q, chip_idx, here).wait_recv()
                fwd = copy(i, N_CHIPS - 1 + q, chip_idx, sibling)
                fwd.start()
                sends.append(fwd)
        for i in range(n_parts):
            copy(i, 0, sib_idx, here).wait_recv()
        for q, (_, chip_idx) in enumerate(chips, start=1):
            for i in range(n_parts):
                copy(i, N_CHIPS - 1 + q, chip_idx ^ 1, here).wait_recv()
        for cp in sends:
            cp.wait_send()
        for cp in local:
            cp.wait()

    hbm = pl.BlockSpec(memory_space=pl.ANY)
    return pl.pallas_call(
        body, name=name, in_specs=[hbm] * n_parts, out_specs=[hbm] * n_parts,
        out_shape=[jax.ShapeDtypeStruct((N_DEV,) + p_.shape, p_.dtype) for p_ in parts],
        scratch_shapes=[pltpu.SemaphoreType.DMA((n_parts * (N_DEV - 1),)),
                        pltpu.SemaphoreType.DMA((n_parts * (N_DEV - 1),)),
                        pltpu.SemaphoreType.DMA((n_parts,))],
        compiler_params=pltpu.CompilerParams(has_side_effects=True),
    )(*parts)


HBM_SPEC = pl.BlockSpec(memory_space=pltpu.HBM)
SEM_SPEC = pl.BlockSpec(memory_space=pltpu.SEMAPHORE)
DATAFLOW = pltpu.SideEffectType.DATAFLOW_SIDE_EFFECTING


def _gather_start(block, after, name):
    per_peer = block.ndim == 3

    def body(v_ref, land_ref, after_ref, send_sems, recv_sems, v_thru, land_thru, token):
        me = _my_index()
        for k in range(1, N_DEV):
            peer, pidx = _peer(k)
            pltpu.make_async_remote_copy(
                src_ref=v_ref.at[pidx] if per_peer else v_ref, dst_ref=land_ref.at[me],
                send_sem=send_sems.at[k - 1], recv_sem=recv_sems.at[k - 1],
                device_id=peer, device_id_type=MESH).start()
        token[...] = jnp.zeros_like(token)

    land_shape = (N_DEV,) + block.shape[-2:]
    return pl.pallas_call(
        body, name=name,
        out_shape=(pltpu.SemaphoreType.DMA((N_DEV - 1,)), pltpu.SemaphoreType.DMA((N_DEV - 1,)),
                   pltpu.HBM(block.shape, block.dtype), pltpu.HBM(land_shape, block.dtype),
                   jax.ShapeDtypeStruct((8, LANES), F32)),
        in_specs=(HBM_SPEC, HBM_SPEC, pl.BlockSpec(memory_space=pl.ANY)),
        out_specs=(SEM_SPEC, SEM_SPEC, HBM_SPEC, HBM_SPEC, pl.BlockSpec(memory_space=pltpu.VMEM)),
        input_output_aliases={0: 2, 1: 3},
        compiler_params=pltpu.CompilerParams(has_side_effects=DATAFLOW),
    )(pltpu.with_memory_space_constraint(block, pltpu.HBM),
      pltpu.with_memory_space_constraint(lax.empty(land_shape, block.dtype), pltpu.HBM), after)


def _gather_wait(send_sems, recv_sems, block, landing, after, name):
    per_peer = block.ndim == 3

    def body(v_ref, land_ref, send_sems, recv_sems, after_ref, v_dead, got_ref):
        for k in range(1, N_DEV):
            peer, pidx = _peer(k)
            copy = pltpu.make_async_remote_copy(
                src_ref=v_ref.at[pidx] if per_peer else v_ref, dst_ref=land_ref.at[pidx],
                send_sem=send_sems.at[k - 1], recv_sem=recv_sems.at[k - 1],
                device_id=peer, device_id_type=MESH)
            copy.wait_send()
            copy.wait_recv()

    return pl.pallas_call(
        body, name=name,
        out_shape=(pltpu.HBM(block.shape, block.dtype), pltpu.HBM(landing.shape, landing.dtype)),
        in_specs=(HBM_SPEC, HBM_SPEC, SEM_SPEC, SEM_SPEC, pl.BlockSpec(memory_space=pl.ANY)),
        out_specs=(HBM_SPEC, HBM_SPEC), input_output_aliases={0: 0, 1: 1},
        compiler_params=pltpu.CompilerParams(has_side_effects=DATAFLOW),
    )(block, landing, send_sems, recv_sems, after)[1]


TILE_ELEMS = 1024 * 1024


def _shared_exchange(shared, name):
    def body(sh_ref, gsh_ref, send_sems, recv_sems, local_sem):
        me = _my_index()
        local = pltpu.make_async_copy(sh_ref, gsh_ref.at[me], local_sem)
        local.start()
        sends = []
        for k in range(1, N_DEV):
            peer, _ = _peer(k)
            cp = pltpu.make_async_remote_copy(
                src_ref=sh_ref, dst_ref=gsh_ref.at[me], send_sem=send_sems.at[k - 1],
                recv_sem=recv_sems.at[k - 1], device_id=peer, device_id_type=MESH)
            cp.start()
            sends.append(cp)
        for k in range(1, N_DEV):
            peer, pidx = _peer(k)
            pltpu.make_async_remote_copy(
                src_ref=sh_ref, dst_ref=gsh_ref.at[pidx], send_sem=send_sems.at[k - 1],
                recv_sem=recv_sems.at[k - 1], device_id=peer, device_id_type=MESH).wait_recv()
        for cp in sends:
            cp.wait_send()
        local.wait()

    hbm = pl.BlockSpec(memory_space=pl.ANY)
    return pl.pallas_call(
        body, name=name, in_specs=[hbm], out_specs=hbm,
        out_shape=jax.ShapeDtypeStruct((N_DEV,) + shared.shape, shared.dtype),
        scratch_shapes=[pltpu.SemaphoreType.DMA((N_DEV - 1,)), pltpu.SemaphoreType.DMA((N_DEV - 1,)),
                        pltpu.SemaphoreType.DMA],
        compiler_params=pltpu.CompilerParams(has_side_effects=True),
    )(shared)


def _adamw(parts, w, m, v, name, row0=0, own=None):
    n_parts, rows, lanes = parts.shape
    tr = rows if rows * lanes <= TILE_ELEMS // 2 else _tile_rows(math.gcd(rows, row0), TILE_ELEMS // 4 // lanes, 8)
    c1 = 1.0 - ADAM_B1 ** ADAM_STEP
    c2 = 1.0 - ADAM_B2 ** ADAM_STEP

    def body(*refs):
        if own is None:
            p_ref, w_ref, m_ref, v_ref, g_ref, d_ref, nm_ref, nv_ref = refs
            terms = [p_ref[j].astype(F32) for j in range(n_parts)]
        else:
            me_ref, p_ref, own_ref, w_ref, m_ref, v_ref, g_ref, d_ref, nm_ref, nv_ref = refs
            terms = [jnp.where(me_ref[0] == j, own_ref[...], p_ref[j]).astype(F32) for j in range(n_parts)]
        g = terms[0]
        for term in terms[1:]:
            g = g + term
        nm = ADAM_B1 * m_ref[...] + (1.0 - ADAM_B1) * g
        nv = ADAM_B2 * v_ref[...] + (1.0 - ADAM_B2) * (g * g)
        g_ref[...] = g
        nm_ref[...] = nm
        nv_ref[...] = nv
        d_ref[...] = -ADAM_LR * ((nm / c1) / (jnp.sqrt(nv / c2) + ADAM_EPS) + ADAM_WD * w_ref[...])

    row = pl.BlockSpec((tr, lanes), lambda i, *_: (i, 0))
    state = pl.BlockSpec((tr, lanes), lambda i, *_: (row0 // tr + i, 0))
    in_specs = [pl.BlockSpec((n_parts, tr, lanes), lambda i, *_: (0, i, 0)), state, state, state]
    args, n_prefetch = (parts, w, m, v), 0
    if own is not None:
        slabs, me = own
        in_specs.insert(1, pl.BlockSpec((None, tr, lanes), lambda i, me_ref: (me_ref[0], i, 0)))
        args, n_prefetch = (me, parts, slabs, w, m, v), 1
    return pl.pallas_call(
        body, name=name,
        grid_spec=pltpu.PrefetchScalarGridSpec(num_scalar_prefetch=n_prefetch, grid=(rows // tr,),
                                               in_specs=in_specs, out_specs=[row] * 4),
        out_shape=[jax.ShapeDtypeStruct((rows, lanes), F32)] * 4,
        compiler_params=_params("parallel"),
    )(*args)


MATRIX_SHARDS = (
    ("w_in", (D_MODEL, IN_PROJ_DIM // N_DEV), True),
    ("w_ssm_out", (SSM_D_INNER // N_DEV, D_MODEL), False),
    ("w_att_out", (ATT_OUT_DIM, D_MODEL // N_DEV), True),
    ("w_mix_out", (D_MODEL // N_DEV, D_MODEL), False),
    ("w_ffn_gate", (D_MODEL, D_FF // N_DEV), True),
    ("w_ffn_up", (D_MODEL, D_FF // N_DEV), True),
    ("w_ffn_down", (D_FF // N_DEV, D_MODEL), False),
)
CONV_SHARD = ("conv_w", (SSM_CONV, SSM_CONV_DIM // N_DEV), True)
SHARDED = MATRIX_SHARDS + (CONV_SHARD,)
REPLICATED = (("norm_mix", D_MODEL), ("b_gate", 2 * D_MODEL), ("conv_b", SSM_CONV_DIM), ("dt_bias", SSM_N_HEADS),
              ("a_log", SSM_N_HEADS), ("d_skip", SSM_N_HEADS), ("ssm_norm", SSM_D_INNER), ("norm_ffn", D_MODEL),
              ("norm_final", D_MODEL))


def _round_up(n, mult):
    return -(-n // mult) * mult


def _pack_rows(flat, row_mult):
    rows = _round_up(-(-flat.shape[0] // LANES), row_mult)
    return jnp.pad(flat, (0, rows * LANES - flat.shape[0])).reshape(rows, LANES)


def _stacking(specs):
    return tuple((name, (shape[1], shape[0]) if by_cols else shape, by_cols) for name, shape, by_cols in specs)


def _to_stacking(vals, specs):
    return {name: (vals[name].T if by_cols else vals[name]) for name, _, by_cols in specs}


STACK_WIDTH = D_MODEL
STACK_ALIGN = 16
STACK_ORDER = ("w_ssm_out", "w_mix_out", "w_ffn_gate", "w_ffn_up", "w_ffn_down", "w_att_out", "conv_w", "w_in")
GATHER_LATER = STACK_ORDER[:-1]
REDUCE_EARLY = STACK_ORDER[:5]
REDUCE_LATE = STACK_ORDER[5:]


def _stack_layout():
    shapes = {name: shape for name, shape, _ in _stacking(SHARDED)}
    layout, off = {}, 0
    for name in STACK_ORDER:
        r, c = shapes[name]
        rows = r if c == STACK_WIDTH else _round_up(-(-(r * c) // STACK_WIDTH), STACK_ALIGN)
        layout[name] = (off, rows, (r, c))
        off = _round_up(off + rows, STACK_ALIGN)
    return layout, _round_up(off, 1024)


def _to_stack_rows(v, rows):
    if v.shape[-1] == STACK_WIDTH:
        return v
    lead = v.shape[:-2]
    flat = v.reshape(lead + (-1,))
    flat = jnp.pad(flat, [(0, 0)] * len(lead) + [(0, rows * STACK_WIDTH - flat.shape[-1])])
    return flat.reshape(lead + (rows, STACK_WIDTH))


def _from_stack_rows(block, shape):
    r, c = shape
    if c == STACK_WIDTH:
        return block
    lead = block.shape[:-2]
    return block.reshape(lead + (-1,))[..., :r * c].reshape(lead + (r, c))


def _stack(vals, dtype, skip=(), names=STACK_ORDER):
    layout, total = _stack_layout()
    order = names
    after = STACK_ORDER.index(order[-1]) + 1
    if after < len(STACK_ORDER):
        total = layout[STACK_ORDER[after]][0]
    lead = next(iter(vals.values())).shape[:-2]
    pieces = []
    for i, name in enumerate(order):
        off, rows, _ = layout[name]
        until = layout[order[i + 1]][0] if i + 1 < len(order) else total
        piece = jnp.zeros(lead + (rows, STACK_WIDTH), dtype) if name in skip else _to_stack_rows(vals[name], rows)
        pieces.append(jnp.pad(piece.astype(dtype), [(0, 0)] * len(lead) + [(0, until - off - rows), (0, 0)]))
    return jnp.concatenate(pieces, axis=-2)


def _unstack(stacked, names):
    layout, _ = _stack_layout()
    row0 = layout[names[0]][0]
    return {name: _from_stack_rows(stacked[..., layout[name][0] - row0:layout[name][0] - row0 + layout[name][1], :],
                                   layout[name][2]) for name in names}


W_IN_SHARD_ROWS = IN_PROJ_DIM // N_DEV


def _w_in_row_moves():
    moves, orig = [], 0
    for name, size in IN_SPLIT:
        for j in range(N_DEV):
            lo, hi = max(orig, W_IN_SHARD_ROWS * j), min(orig + size, W_IN_SHARD_ROWS * (j + 1))
            if lo < hi:
                moves.append((j, lo - W_IN_SHARD_ROWS * j, DPROJ_COLS[name] + lo - orig, hi - lo))
        orig += size
    return moves


def _w_in_from_shards(shards, name):
    total, base = shards.shape[1], 0
    pad_lo, pad_hi = DPROJ_COLS["dt"] + _round_up(SSM_N_HEADS, STACK_ALIGN), DPROJ_COLS["dt"] + DPROJ_DT_WIDTH

    def body(x_ref, o_ref):
        o_ref[pad_lo:pad_hi, :] = jnp.zeros((pad_hi - pad_lo, LANES), x_ref.dtype)
        for j, r, at, n in _w_in_row_moves():
            o_ref[at:at + n, :] = x_ref[j, base + r:base + r + n, :]

    return pl.pallas_call(
        body, name=name, grid=(STACK_WIDTH // LANES,),
        in_specs=[pl.BlockSpec((N_DEV, total, LANES), lambda c: (0, 0, c))],
        out_specs=pl.BlockSpec((DPROJ_WIDTH, LANES), lambda c: (0, c)),
        out_shape=jax.ShapeDtypeStruct((DPROJ_WIDTH, STACK_WIDTH), shards.dtype),
        compiler_params=_params("parallel"),
    )(shards)


def _w_in_to_shards(dw_all, head, name):
    layout, total = _stack_layout()
    total -= layout[REDUCE_LATE[0]][0]
    base = head.shape[1]
    end = base + W_IN_SHARD_ROWS

    def body(x_ref, h_ref, o_ref):
        o_ref[:, 0:base, :] = h_ref[...]
        for j, r, at, n in _w_in_row_moves():
            o_ref[j, base + r:base + r + n, :] = x_ref[at:at + n, :]
        o_ref[:, end:total, :] = jnp.zeros((N_DEV, total - end, LANES), o_ref.dtype)

    return pl.pallas_call(
        body, name=name, grid=(STACK_WIDTH // LANES,),
        in_specs=[pl.BlockSpec((DPROJ_WIDTH, LANES), lambda c: (0, c)),
                  pl.BlockSpec((N_DEV, base, LANES), lambda c: (0, 0, c))],
        out_specs=pl.BlockSpec((N_DEV, total, LANES), lambda c: (0, 0, c)),
        out_shape=jax.ShapeDtypeStruct((N_DEV, total, STACK_WIDTH), dw_all.dtype),
        compiler_params=_params("parallel"),
    )(dw_all, head)


REPLICATED_ROWS = sum(-(-size // LANES) for _, size in REPLICATED)
LOSS_ROW = REPLICATED_ROWS


def _pack_replicated(vals):
    rows = []
    for name, size in REPLICATED:
        v = vals[name].reshape(-1).astype(F32)
        rows.append(jnp.pad(v, (0, _round_up(size, LANES) - size)))
    return _pack_rows(jnp.concatenate(rows), 8)


def _unpack_replicated(packed, shapes):
    flat = packed.reshape(-1)
    out, off = {}, 0
    for name, size in REPLICATED:
        out[name] = flat[off:off + size].reshape(shapes[name])
        off += _round_up(size, LANES)
    return out


def _lane_row(v):
    v = v.reshape(-1).astype(F32)
    return jnp.pad(v, (0, LANES - v.shape[0])).reshape(1, LANES)


IN_SPLIT = (("z", SSM_D_INNER), ("xbc", SSM_CONV_DIM), ("dt", SSM_N_HEADS), ("qkv", ATT_QKV_DIM), ("gate", 2 * D_MODEL))


def kernel(x, norm_mix, w_in, b_gate, conv_w, conv_b, dt_bias, a_log, d_skip, ssm_norm, w_ssm_out, w_att_out, w_mix_out, norm_ffn, w_ffn_gate, w_ffn_up, w_ffn_down, norm_final, loss_target, m_norm_mix, m_w_in, m_b_gate, m_conv_w, m_conv_b, m_dt_bias, m_a_log, m_d_skip, m_ssm_norm, m_w_ssm_out, m_w_att_out, m_w_mix_out, m_norm_ffn, m_w_ffn_gate, m_w_ffn_up, m_w_ffn_down, m_norm_final, v_norm_mix, v_w_in, v_b_gate, v_conv_w, v_conv_b, v_dt_bias, v_a_log, v_d_skip, v_ssm_norm, v_w_ssm_out, v_w_att_out, v_w_mix_out, v_norm_ffn, v_w_ffn_gate, v_w_ffn_up, v_w_ffn_down, v_norm_final):
    given = dict(locals())
    weights = {name: given[name][0] for name, _, _ in SHARDED}
    b, s, d = x.shape
    t = b * s

    stacking = _to_stacking(weights, SHARDED)
    conv_shape = dict((name, shape) for name, shape, _ in _stacking(SHARDED))["conv_w"]
    w_in_local = jnp.pad(stacking["w_in"].astype(BF16), ((0, -W_IN_SHARD_ROWS % STACK_ALIGN), (0, 0)))
    conv_local = _pack_rows(stacking["conv_w"].reshape(-1), 8)
    w_in_shards, conv_all = _all_gather([w_in_local, conv_local], "w_in_all_gather")
    head_local = _stack(stacking, BF16, skip=("conv_w",), names=GATHER_LATER)
    in_flight = _gather_start(head_local, conv_all, "weights_gather_start")
    w_in_all = _w_in_from_shards(w_in_shards, "w_in_from_shards")
    w_sec = {name: w_in_all[DPROJ_COLS[name]:DPROJ_COLS[name] + _round_up(size, LANES)] for name, size in IN_SPLIT}
    conv_size = conv_shape[0] * conv_shape[1]
    conv_taps = conv_all.reshape(N_DEV, -1)[:, :conv_size].reshape(N_DEV * conv_shape[0], conv_shape[1]).T

    g_mix, g_ffn, g_fin = norm_mix.reshape(1, d), norm_ffn.reshape(1, d), norm_final.reshape(1, d)
    g_mix = g_mix + in_flight[4][:1, :1]
    bg_row = b_gate.reshape(1, 2 * d)
    convb_row = conv_b.reshape(1, SSM_CONV_DIM)
    ssmn_row = ssm_norm.reshape(1, SSM_D_INNER)
    dtb_row, alog_row = _lane_row(dt_bias), _lane_row(a_log)
    cosf, sinf = _rope_tables(s)

    x2d = x.reshape(t, d)
    h1 = _rmsnorm_fwd(x2d, g_mix, "norm_mix_fwd")
    proj = {name: _mm(h1, w_sec[name], mode="nt", name="in_proj_" + name) for name, _ in IN_SPLIT if name != "qkv"}
    xbc3 = proj["xbc"].reshape(b, s, SSM_CONV_DIM)
    xc = _conv_fwd(xbc3, conv_taps, convb_row, "conv_fwd")
    dtr3 = proj["dt"].reshape(b, s, DT_PAD)
    to_channels, to_heads = _head_masks()
    dskx = jnp.repeat(d_skip.reshape(-1).astype(F32), SSM_HEAD_DIM).reshape(1, SSM_D_INNER)
    y_ssd, h_states = _ssd_fwd(xc, dtr3, dtb_row, alog_row, dskx, to_channels, "ssd_fwd")
    y_ssd2 = y_ssd.reshape(t, SSM_D_INNER)
    ynorm = _gate_norm_fwd(y_ssd2, proj["z"], ssmn_row, "ssd_gate_norm_fwd")
    landed = _gather_wait(*in_flight[:4], ynorm, "weights_gather_wait")
    head_all = lax.dynamic_update_slice(landed, head_local[None], (_my_index(), 0, 0))
    full = {name: v.reshape((-1,) + v.shape[2:]) for name, v in _unstack(head_all, STACK_ORDER[:-2]).items()}
    y_ssm = _mm(ynorm, full["w_ssm_out"], mode="nn", name="ssm_out_proj")

    qk_parts = _qkv_proj_rope(h1, w_sec["qkv"], cosf, sinf, b, s, "in_proj_qkv_rope")
    att_parts = [_att_fwd(qk_parts[gi], "att_fwd_%d" % r) for gi, r in enumerate(ATT_DILATIONS)]
    att, *lse_parts = _att_merge([o for o, _ in att_parts], [l_ for _, l_ in att_parts], "att_merge")
    att2 = att.reshape(t, ATT_OUT_DIM)
    y_att = _mm(att2, full["w_att_out"], mode="nt", name="att_out_proj")

    mixed = _mix_fwd(proj["gate"], bg_row, y_ssm, y_att, "mix_fwd")
    x2, h2 = _proj_residual_norm(mixed, full["w_mix_out"], x2d, g_ffn, "mix_out_proj_norm")
    gt = _mm(h2, full["w_ffn_gate"], mode="nt", name="ffn_gate_proj")
    up, act = _up_proj_swiglu(h2, full["w_ffn_up"], gt, "ffn_up_proj_swiglu")

    loss_row, dx3, dg_fin, dx3b = _down_proj_loss_head(act, full["w_ffn_down"], x2, g_fin, loss_target.reshape(t, d),
                                                       "ffn_down_proj_loss_head")
    grads = {}
    grads["w_ffn_down"] = _mm(act, dx3b, mode="tn", name="ffn_down_dw", out_dtype=BF16)
    dgt, dup = _down_dx_swiglu_bwd(dx3b, full["w_ffn_down"], gt, up, "ffn_down_dx_swiglu_bwd")
    grads["w_ffn_gate"] = _mm(dgt, h2, mode="tn", name="ffn_gate_dw", out_dtype=BF16)
    grads["w_ffn_up"] = _mm(dup, h2, mode="tn", name="ffn_up_dw", out_dtype=BF16)
    dh2 = _mm(dgt, full["w_ffn_gate"], mode="nn", name="ffn_gate_dx")
    dh2 = _mm(dup, full["w_ffn_up"], mode="nn", name="ffn_up_dx", add=dh2)
    dx2, dg_ffn, dx2b = _rmsnorm_bwd(x2, g_ffn, dh2, dx3, "norm_ffn_bwd", with_bf16=True)

    dmixed = _mm(dx2b, full["w_mix_out"], mode="nt", name="mix_out_dx")
    grads["w_mix_out"] = _mm(mixed, dx2b, mode="tn", name="mix_out_dw", out_dtype=BF16)
    dys, dya, dproj, dbg = _mix_bwd(proj["gate"], bg_row, y_ssm, y_att, dmixed, "mix_bwd")

    grads["w_ssm_out"] = _mm(ynorm, dys, mode="tn", name="ssm_out_dw", out_dtype=BF16)
    early = _stack({name: grads[name].reshape((N_DEV, -1, STACK_WIDTH)) for name in REDUCE_EARLY}, BF16,
                   names=REDUCE_EARLY)
    early_flight = _gather_start(early, dys, "grads_scatter_start")
    ssmn_row = ssmn_row + early_flight[4][:1, :1]
    dynorm = _mm(dys, full["w_ssm_out"], mode="nt", name="ssm_out_dx")
    dy_ssd, dproj, dssmn = _gate_norm_bwd(y_ssd2, proj["z"], ssmn_row, dynorm, dproj, "ssd_gate_norm_bwd")
    dxc, dproj, dalog, ddsk, ddtb = _ssd_bwd(xc, dtr3, dy_ssd.reshape(b, s, SSM_D_INNER), h_states, dtb_row, alog_row,
                                             dskx, to_channels, to_heads, dproj.reshape(b, s, DPROJ_WIDTH), "ssd_bwd")
    dproj, dconvw, dconvb = _conv_bwd(xbc3, dxc, conv_taps, convb_row, dproj, "conv_bwd")
    grads["conv_w"] = dconvw.T.astype(BF16)

    grads["w_att_out"] = _mm(dya, att2, mode="tn", name="att_out_dw", out_dtype=BF16)
    datt = _mm(dya, full["w_att_out"], mode="nn", name="att_out_dx").reshape(b, s, ATT_OUT_DIM)
    do_parts, dl_parts = _att_delta(att, datt, "att_delta")
    dqs, dks, dvs = [], [], []
    for gi, r in enumerate(ATT_DILATIONS):
        operands = (qk_parts[gi], do_parts[gi], lse_parts[gi], dl_parts[gi])
        dqs.append(_att_bwd_q(*operands, "att_bwd_q_%d" % r))
        dk_g, dv_g = _att_bwd_kv(*operands, "att_bwd_kv_%d" % r)
        dks.append(dk_g)
        dvs.append(dv_g)
    dproj = _rope_bwd(dqs, dks, dvs, cosf, sinf, dproj, "rope_bwd").reshape(t, DPROJ_WIDTH)

    dw_all = _mm(dproj, h1, mode="tn", name="in_proj_dw", out_dtype=BF16)
    head = _stack({name: grads[name].reshape((N_DEV, -1, grads[name].shape[-1])) for name in REDUCE_LATE[:-1]}, BF16,
                  names=REDUCE_LATE[:-1])
    late = _w_in_to_shards(dw_all, head, "grad_stacks")
    late_flight = _gather_start(late, dw_all, "grads_late_scatter_start")
    dh1 = _mm(dproj, w_in_all, mode="nn", name="in_proj_dx", after=late_flight[4])
    grad_x, dg_mix = _rmsnorm_bwd(x2d, g_mix, dh1, dx2, "norm_mix_bwd")

    small = {"norm_mix": dg_mix, "b_gate": dbg, "conv_b": dconvb, "dt_bias": ddtb[:, :SSM_N_HEADS],
             "a_log": dalog[:, :SSM_N_HEADS], "d_skip": ddsk[:, :SSM_N_HEADS], "ssm_norm": dssmn,
             "norm_ffn": dg_ffn, "norm_final": dg_fin}
    shared = _pack_replicated(small)
    shared = shared.at[LOSS_ROW, 0].set(loss_row[0, 0])
    got_small = _shared_exchange(shared, "shared_grads_exchange")

    def packed(prefix):
        vals = _to_stacking({name: given[prefix + name][0] for name, _, _ in SHARDED}, SHARDED)
        rep = {name: given[prefix + name] for name, _ in REPLICATED}
        return _stack(vals, F32), _pack_replicated(rep)

    (w_big, w_small), (m_big, m_small), (v_big, v_small) = packed(""), packed("m_"), packed("v_")
    me = _my_index().astype(jnp.int32).reshape(1)
    big_early = _adamw(_gather_wait(*early_flight[:4], got_small, "grads_scatter_wait"), w_big, m_big, v_big,
                       "adamw_early", own=(early, me))
    big_late = _adamw(_gather_wait(*late_flight[:4], got_small, "grads_late_scatter_wait"), w_big, m_big, v_big,
                      "adamw_late", row0=early.shape[1], own=(late, me))
    sml = _adamw(got_small, w_small, m_small, v_small, "adamw_replicated")

    outs = [sml[0][LOSS_ROW, 0], grad_x.reshape(b, s, d)]
    rep_shapes = {name: given[name].shape for name, _ in REPLICATED}
    order = ["norm_mix", "w_in", "b_gate", "conv_w", "conv_b", "dt_bias", "a_log", "d_skip", "ssm_norm", "w_ssm_out",
             "w_att_out", "w_mix_out", "norm_ffn", "w_ffn_gate", "w_ffn_up", "w_ffn_down", "norm_final"]
    for early_k, late_k, sml_k in zip(big_early, big_late, sml):
        stacks = dict(_unstack(early_k, REDUCE_EARLY), **_unstack(late_k, REDUCE_LATE))
        sharded = _to_stacking(stacks, SHARDED)
        rep = _unpack_replicated(sml_k, rep_shapes)
        for name in order:
            outs.append(sharded[name][None] if name in sharded else rep[name])
    return tuple(outs)
```

```python
import functools
import math

import jax
import jax.numpy as jnp
from jax import lax
from jax.experimental import pallas as pl
from jax.experimental.pallas import tpu as pltpu

F32 = jnp.float32
BF16 = jnp.bfloat16

N_DEV = 8
N_CHIPS = 4
D_MODEL = 1024
SSM_D_INNER = 2048
SSM_HEAD_DIM = 64
SSM_N_HEADS = 32
SSM_N_GROUPS = 4
SSM_HEADS_PER_GROUP = SSM_N_HEADS // SSM_N_GROUPS
SSM_D_STATE = 128
SSM_CONV = 4
SSM_CHUNK = 128
SSM_CONV_DIM = 3072
ATT_HEAD_DIM = 128
ATT_HEADS_PER_GROUP = 4
ATT_DILATIONS = (1, 4, 16)
ATT_N_HEADS = 12
ATT_QKV_DIM = 4608
ATT_OUT_DIM = 512
ATT_BLOCK = 128
ROPE_THETA = 10000.0
D_FF = 2816
IN_PROJ_DIM = 11808
EPS = 1e-6
LANES = 128
DT_PAD = LANES

DPROJ_COLS = {"qkv": 0, "xbc": 4608, "dt": 7680, "z": 8192, "gate": 10240}
DPROJ_DT_WIDTH = 512
DPROJ_WIDTH = 12288

ADAM_LR = 0.001
ADAM_B1 = 0.9
ADAM_B2 = 0.999
ADAM_EPS = 1e-08
ADAM_WD = 0.01
ADAM_STEP = 10

VMEM_LIMIT = 56 * 1024 * 1024
MESH = pl.DeviceIdType.MESH
NEG_INF = float("-inf")


def _tile_rows(n, cap, mult):
    return max(t for t in range(mult, min(n, cap) + 1, mult) if n % t == 0)


def _pick(n, candidates):
    for c in candidates:
        if n % c == 0:
            return c
    return n


def _params(*sem):
    return pltpu.CompilerParams(dimension_semantics=sem, vmem_limit_bytes=VMEM_LIMIT)


def _sigmoid(x):
    return 0.5 * jnp.tanh(0.5 * x) + 0.5


def _softplus(x):
    return jnp.maximum(x, 0.0) + jnp.log(1.0 + jnp.exp(-jnp.abs(x)))


def _dot(a, b, dims):
    return lax.dot_general(a.astype(BF16), b.astype(BF16), (dims, ((), ())), preferred_element_type=F32)


def _nn(a, b):
    return _dot(a, b, ((1,), (0,)))


def _nt(a, b):
    return _dot(a, b, ((1,), (1,)))


def _tn(a, b):
    return _dot(a, b, ((0,), (0,)))


def _split3(v):
    hi = v.astype(BF16)
    r1 = v - hi.astype(F32)
    mid = r1.astype(BF16)
    lo = (r1 - mid.astype(F32)).astype(BF16)
    return hi, mid, lo


def _mask_nn(mask, v):
    mb = mask.astype(BF16)
    hi, mid, lo = _split3(v)
    return _nn(mb, hi) + (_nn(mb, mid) + _nn(mb, lo))


MM_VMEM_BUDGET = 40 * 1024 * 1024
MM_FULL_K = 2816


def _mm_tiles(m, n, k, a_bytes, b_bytes, o_bytes, has_add):
    tk = k if k <= MM_FULL_K else _pick(k, (2048, 1024, 512, 256, 128))
    tn = 1408 if (n > 1024 and n % 1408 == 0) else _pick(n, (1024, 768, 512, 384, 256, 128))
    for tm in (1408, 1024, 768, 512, 384, 256, 128):
        if m % tm:
            continue
        buffers = 2 * (tm * tk * a_bytes + tk * tn * b_bytes + tm * tn * (o_bytes + (4 if has_add else 0)))
        if tk < k:
            buffers += tm * tn * 4
        if buffers <= MM_VMEM_BUDGET:
            return tm, tn, tk
    return _pick(m, (128,)), tn, tk


def _mm(a, b, *, mode, name, out_dtype=F32, add=None, after=None):
    if mode == "nn":
        (m, k), n = a.shape, b.shape[1]
    elif mode == "nt":
        (m, k), n = a.shape, b.shape[0]
    else:
        (k, m), n = a.shape, b.shape[1]
    has_add = add is not None
    tm, tn, tk = _mm_tiles(m, n, k, a.dtype.itemsize, b.dtype.itemsize, jnp.dtype(out_dtype).itemsize, has_add)
    nk = k // tk
    dims = {"nn": ((1,), (0,)), "nt": ((1,), (1,)), "tn": ((0,), (0,))}[mode]
    a_spec = {"nn": pl.BlockSpec((tm, tk), lambda i, j, kk: (i, kk)),
              "nt": pl.BlockSpec((tm, tk), lambda i, j, kk: (i, kk)),
              "tn": pl.BlockSpec((tk, tm), lambda i, j, kk: (kk, i))}[mode]
    b_spec = {"nn": pl.BlockSpec((tk, tn), lambda i, j, kk: (kk, j)),
              "nt": pl.BlockSpec((tn, tk), lambda i, j, kk: (j, kk)),
              "tn": pl.BlockSpec((tk, tn), lambda i, j, kk: (kk, j))}[mode]
    o_spec = pl.BlockSpec((tm, tn), lambda i, j, kk: (i, j))

    def finish(r, c_ref, o_ref):
        if has_add:
            r = r + c_ref[...]
        o_ref[...] = r.astype(out_dtype)

    def body_one(*refs):
        a_ref, b_ref = refs[:2]
        finish(_dot(a_ref[...], b_ref[...], dims), refs[2] if has_add else None, refs[-1])

    def body_acc(*refs):
        a_ref, b_ref = refs[:2]
        o_ref, acc = refs[-2:]
        kk = pl.program_id(2)

        @pl.when(kk == 0)
        def _():
            acc[...] = jnp.zeros_like(acc)

        acc[...] += _dot(a_ref[...], b_ref[...], dims)

        @pl.when(kk == nk - 1)
        def _():
            finish(acc[...], refs[2] if has_add else None, o_ref)

    in_specs = [a_spec, b_spec] + ([o_spec] if has_add else [])
    args = (a, b) + ((add,) if has_add else ())
    if after is not None:
        in_specs, args = in_specs + [pl.BlockSpec(memory_space=pl.ANY)], args + (after,)
    return pl.pallas_call(
        body_one if nk == 1 else body_acc, name=name, grid=(m // tm, n // tn, nk),
        in_specs=in_specs, out_specs=o_spec,
        out_shape=jax.ShapeDtypeStruct((m, n), out_dtype),
        scratch_shapes=[] if nk == 1 else [pltpu.VMEM((tm, tn), F32)],
        compiler_params=_params("parallel", "parallel", "arbitrary"),
    )(*args)


def _rmsnorm_fwd(x, g, name):
    t, d = x.shape
    tm = _pick(t, (512, 256, 128))

    def body(x_ref, g_ref, o_ref):
        xv = x_ref[...]
        r = lax.rsqrt(jnp.mean(xv * xv, axis=-1, keepdims=True) + EPS)
        o_ref[...] = ((xv * r) * g_ref[...]).astype(BF16)

    return pl.pallas_call(
        body, name=name, grid=(t // tm,),
        in_specs=[pl.BlockSpec((tm, d), lambda i: (i, 0)), pl.BlockSpec((1, d), lambda i: (0, 0))],
        out_specs=pl.BlockSpec((tm, d), lambda i: (i, 0)),
        out_shape=jax.ShapeDtypeStruct((t, d), BF16),
        compiler_params=_params("parallel"),
    )(x, g)


def _proj_residual_norm(a, w, res, g, name):
    t, k = a.shape
    d = w.shape[1]
    tm, _, _ = _mm_tiles(t, d, k, a.dtype.itemsize, w.dtype.itemsize, 4 + 2, True)

    def body(a_ref, w_ref, r_ref, g_ref, x_ref, h_ref):
        xv = r_ref[...] + _nn(a_ref[...], w_ref[...])
        x_ref[...] = xv
        r = lax.rsqrt(jnp.mean(xv * xv, axis=-1, keepdims=True) + EPS)
        h_ref[...] = ((xv * r) * g_ref[...]).astype(BF16)

    row = pl.BlockSpec((tm, d), lambda i: (i, 0))
    return pl.pallas_call(
        body, name=name, grid=(t // tm,),
        in_specs=[pl.BlockSpec((tm, k), lambda i: (i, 0)), pl.BlockSpec((k, d), lambda i: (0, 0)), row,
                  pl.BlockSpec((1, d), lambda i: (0, 0))],
        out_specs=[row, row],
        out_shape=[jax.ShapeDtypeStruct((t, d), F32), jax.ShapeDtypeStruct((t, d), BF16)],
        compiler_params=_params("parallel"),
    )(a, w, res, g)


def _proj_norm_bwd(a, w, x, g, dres, name, add=None, with_bf16=False, after=None):
    t, k = a.shape
    d = w.shape[1]
    has_add = add is not None
    tm, _, tk = _mm_tiles(t, d, k, a.dtype.itemsize, w.dtype.itemsize, 4 + 4 + 4 + (2 if with_bf16 else 0), has_add)
    tm = min(tm, 512)
    nk = k // tk

    def body(*refs):
        a_ref, w_ref, x_ref, g_ref, dres_ref = refs[:5]
        rest = refs[5 + has_add + (after is not None):]
        dx_ref, dg_ref = rest[:2]
        i, kk = pl.program_id(0), pl.program_id(1)

        @pl.when(jnp.logical_and(i == 0, kk == 0))
        def _():
            dg_ref[...] = jnp.zeros_like(dg_ref)

        part = _nn(a_ref[...], w_ref[...])
        if nk > 1:
            acc = rest[-1]

            @pl.when(kk == 0)
            def _():
                acc[...] = jnp.zeros_like(acc)

            acc[...] += part

        @pl.when(kk == nk - 1)
        def _():
            dhv = part if nk == 1 else acc[...]
            if has_add:
                dhv = dhv + refs[5][...]
            xv = x_ref[...]
            r = lax.rsqrt(jnp.mean(xv * xv, axis=-1, keepdims=True) + EPS)
            xhat = xv * r
            dyg = dhv * g_ref[...]
            dx = dres_ref[...] + r * (dyg - xhat * jnp.mean(dyg * xhat, axis=-1, keepdims=True))
            dx_ref[...] = dx
            if with_bf16:
                rest[2][...] = dx.astype(BF16)
            dg_ref[...] += jnp.sum(dhv * xhat, axis=0, keepdims=True)

    row = pl.BlockSpec((tm, d), lambda i, kk: (i, 0))
    vec = pl.BlockSpec((1, d), lambda i, kk: (0, 0))
    in_specs = [pl.BlockSpec((tm, tk), lambda i, kk: (i, kk)), pl.BlockSpec((tk, d), lambda i, kk: (kk, 0)),
                row, vec, row] + has_add * [row]
    args = (a, w, x, g, dres) + has_add * (add,)
    if after is not None:
        in_specs, args = in_specs + [pl.BlockSpec(memory_space=pl.ANY)], args + (after,)
    return pl.pallas_call(
        body, name=name, grid=(t // tm, nk), in_specs=in_specs, out_specs=[row, vec] + with_bf16 * [row],
        out_shape=[jax.ShapeDtypeStruct((t, d), F32), jax.ShapeDtypeStruct((1, d), F32)]
        + with_bf16 * [jax.ShapeDtypeStruct((t, d), BF16)],
        scratch_shapes=[] if nk == 1 else [pltpu.VMEM((tm, d), F32)],
        compiler_params=_params("arbitrary", "arbitrary"),
    )(*args)


def _down_proj_loss_head(act, w_down, res, g, target, name):
    t, k = act.shape
    d = w_down.shape[1]
    tm, _, _ = _mm_tiles(t, d, k, act.dtype.itemsize, w_down.dtype.itemsize, 4 + 2, True)
    tm = min(tm, 512)

    def body(a_ref, w_ref, r_ref, g_ref, t_ref, loss_ref, dx_ref, dg_ref, dxb_ref):
        @pl.when(pl.program_id(0) == 0)
        def _():
            dg_ref[...] = jnp.zeros_like(dg_ref)
            loss_ref[...] = jnp.zeros_like(loss_ref)

        xv = r_ref[...] + _nn(a_ref[...], w_ref[...])
        gv = g_ref[...]
        r = lax.rsqrt(jnp.mean(xv * xv, axis=-1, keepdims=True) + EPS)
        xhat = xv * r
        err = xhat * gv - t_ref[...]
        loss_ref[...] += jnp.sum(err * err) * (0.5 / d)
        dy = err * (1.0 / d)
        dyg = dy * gv
        dx = r * (dyg - xhat * jnp.mean(dyg * xhat, axis=-1, keepdims=True))
        dx_ref[...] = dx
        dxb_ref[...] = dx.astype(BF16)
        dg_ref[...] += jnp.sum(dy * xhat, axis=0, keepdims=True)

    row = pl.BlockSpec((tm, d), lambda i: (i, 0))
    vec = pl.BlockSpec((1, d), lambda i: (0, 0))
    return pl.pallas_call(
        body, name=name, grid=(t // tm,),
        in_specs=[pl.BlockSpec((tm, k), lambda i: (i, 0)), pl.BlockSpec((k, d), lambda i: (0, 0)), row, vec, row],
        out_specs=[pl.BlockSpec((1, LANES), lambda i: (0, 0)), row, vec, row],
        out_shape=[jax.ShapeDtypeStruct((1, LANES), F32), jax.ShapeDtypeStruct((t, d), F32),
                   jax.ShapeDtypeStruct((1, d), F32), jax.ShapeDtypeStruct((t, d), BF16)],
        compiler_params=_params("arbitrary"),
    )(act, w_down, res, g, target)


CONV_HALO = 8
CONV_ROWS = 64


def _conv_taps(window, wv, bv):
    acc = bv + wv[SSM_CONV - 1:SSM_CONV, :] * window(0)
    for sh in range(1, SSM_CONV):
        kidx = SSM_CONV - 1 - sh
        acc = acc + wv[kidx:kidx + 1, :] * window(sh)
    return acc


def _conv_fwd(u, w, bias, name):
    b, s, c = u.shape
    rows = CONV_ROWS

    def body(u_ref, w_ref, b_ref, o_ref, ext):
        ext[0:CONV_HALO, :] = jnp.zeros((CONV_HALO, LANES), F32)
        ext[CONV_HALO:, :] = u_ref[...]
        wv, bv = w_ref[...], b_ref[...]
        for r0 in range(0, s, rows):
            acc = _conv_taps(lambda sh: ext[CONV_HALO + r0 - sh:CONV_HALO + r0 - sh + rows, :], wv, bv)
            o_ref[r0:r0 + rows, :] = acc * _sigmoid(acc)

    strip = pl.BlockSpec((None, s, LANES), lambda bi, j: (bi, 0, j))
    return pl.pallas_call(
        body, name=name, grid=(b, c // LANES),
        in_specs=[strip, pl.BlockSpec((SSM_CONV, LANES), lambda bi, j: (0, j)),
                  pl.BlockSpec((1, LANES), lambda bi, j: (0, j))],
        out_specs=strip, out_shape=jax.ShapeDtypeStruct((b, s, c), F32),
        scratch_shapes=[pltpu.VMEM((CONV_HALO + s, LANES), F32)],
        compiler_params=_params("parallel", "parallel"),
    )(u, w, bias)


def _conv_bwd(u, dout, w, bias, dproj, name):
    b, s, c = u.shape
    rows = CONV_ROWS

    def fold(v):
        return jnp.sum(v.reshape(rows // CONV_HALO, CONV_HALO, LANES), axis=0)

    def body(u_ref, d_ref, w_ref, b_ref, buf_ref, du_ref, dw_ref, db_ref, ext, dpre):
        @pl.when(pl.program_id(1) == 0)
        def _():
            dw_ref[...] = jnp.zeros_like(dw_ref)
            db_ref[...] = jnp.zeros_like(db_ref)

        ext[0:CONV_HALO, :] = jnp.zeros((CONV_HALO, LANES), F32)
        ext[CONV_HALO:, :] = u_ref[...]
        dpre[s:, :] = jnp.zeros((CONV_HALO, LANES), F32)
        wv, bv = w_ref[...], b_ref[...]
        sums = [jnp.zeros((CONV_HALO, LANES), F32)] * (SSM_CONV + 1)
        for r0 in range(0, s, rows):
            window = lambda sh: ext[CONV_HALO + r0 - sh:CONV_HALO + r0 - sh + rows, :]
            acc = _conv_taps(window, wv, bv)
            sg = _sigmoid(acc)
            dp = d_ref[r0:r0 + rows, :] * (sg * (1.0 + acc * (1.0 - sg)))
            dpre[r0:r0 + rows, :] = dp
            taps = [sums[SSM_CONV - 1 - sh] + fold(dp * window(sh)) for sh in range(SSM_CONV)]
            sums = taps[::-1] + [sums[SSM_CONV] + fold(dp)]
        for r0 in range(0, s, rows):
            du = wv[SSM_CONV - 1:SSM_CONV, :] * dpre[r0:r0 + rows, :]
            for sh in range(1, SSM_CONV):
                kidx = SSM_CONV - 1 - sh
                du = du + wv[kidx:kidx + 1, :] * dpre[r0 + sh:r0 + sh + rows, :]
            du_ref[r0:r0 + rows, :] = du.astype(BF16)
        for kidx in range(SSM_CONV):
            dw_ref[kidx:kidx + 1, :] += jnp.sum(sums[kidx], axis=0, keepdims=True)
        db_ref[...] += jnp.sum(sums[SSM_CONV], axis=0, keepdims=True)

    strip = pl.BlockSpec((None, s, LANES), lambda j, bi: (bi, 0, j))
    taps = pl.BlockSpec((SSM_CONV, LANES), lambda j, bi: (0, j))
    vec = pl.BlockSpec((1, LANES), lambda j, bi: (0, j))
    du_cols = pl.BlockSpec((None, s, LANES), lambda j, bi: (bi, 0, DPROJ_COLS["xbc"] // LANES + j))
    return pl.pallas_call(
        body, name=name, grid=(c // LANES, b),
        in_specs=[strip, strip, taps, vec, pl.BlockSpec(memory_space=pl.ANY)], out_specs=[du_cols, taps, vec],
        input_output_aliases={4: 0},
        out_shape=[jax.ShapeDtypeStruct(dproj.shape, dproj.dtype), jax.ShapeDtypeStruct((SSM_CONV, c), F32),
                   jax.ShapeDtypeStruct((1, c), F32)],
        scratch_shapes=[pltpu.VMEM((CONV_HALO + s, LANES), F32), pltpu.VMEM((s + CONV_HALO, LANES), F32)],
        compiler_params=_params("parallel", "arbitrary"),
    )(u, dout, w, bias, dproj)


def _ssd_chunk_terms(dtr_ref, bias_ref, alog_ref):
    q = SSM_CHUNK
    dt = _softplus(dtr_ref[...] + bias_ref[...])
    a_neg = -jnp.exp(alog_ref[...])
    row = lax.broadcasted_iota(jnp.int32, (q, q), 0)
    col = lax.broadcasted_iota(jnp.int32, (q, q), 1)
    lower = row >= col
    s = _mask_nn(lower, dt * a_neg)
    return dt, a_neg, s, s.T, lower


def _head_masks():
    heads = jnp.arange(LANES)[:, None]
    chans = jnp.arange(SSM_D_INNER)[None, :]
    to_channels = (chans // SSM_HEAD_DIM == heads).astype(BF16)
    return to_channels, to_channels.T


def _per_channel(v, to_channels):
    hi = v.astype(BF16)
    lo = (v - hi.astype(F32)).astype(BF16)
    return _nn(hi, to_channels) + _nn(lo, to_channels)


def _per_head(v, to_heads):
    hi = v.astype(BF16)
    lo = (v - hi.astype(F32)).astype(BF16)
    return _nn(hi, to_heads) + _nn(lo, to_heads)


def _decay_terms_per_channel(dt, s_col, to_channels):
    q = SSM_CHUNK
    tot = s_col[q - 1:q, :]
    stacked = jnp.concatenate([dt, jnp.exp(s_col), jnp.exp(tot - s_col)], axis=0)
    wide = _per_channel(stacked, to_channels)
    dtx, esx, decx = wide[:q], wide[q:2 * q], wide[2 * q:]
    return dtx, esx, decx, esx[0:1, :] * decx[0:1, :]


SSM_PAIRS_PER_GROUP = SSM_HEADS_PER_GROUP // 2
SSM_GROUP_CHANNELS = SSM_HEADS_PER_GROUP * SSM_HEAD_DIM


def _split_pair(v):
    first = lax.broadcasted_iota(jnp.int32, v.shape, 1) < SSM_HEAD_DIM
    return jnp.concatenate([jnp.where(first, v, 0.0), jnp.where(first, 0.0, v)], axis=0)


def _ssd_fwd(xc, dtr, dt_bias, a_log, dskx, to_channels, name):
    b, s, _ = xc.shape
    q = SSM_CHUNK
    nc = s // q
    n, gc = SSM_D_STATE, SSM_GROUP_CHANNELS

    def body(xc_ref, dtr_ref, bias_ref, alog_ref, dsk_ref, tc_ref, y_ref, hs_ref, h_scr):
        @pl.when(pl.program_id(1) == 0)
        def _():
            h_scr[...] = jnp.zeros_like(h_scr)

        dt, _, s_col, s_row, lower = _ssd_chunk_terms(dtr_ref, bias_ref, alog_ref)
        dtx, esx, decx, etotx = _decay_terms_per_channel(dt, s_col, tc_ref[...])
        x = xc_ref[:, :SSM_D_INNER]
        xdt = x * dtx
        xdec = xdt * decx
        skip = dsk_ref[...] * x
        for g in range(SSM_N_GROUPS):
            bg = xc_ref[:, SSM_D_INNER + n * g:SSM_D_INNER + n * (g + 1)].astype(BF16)
            cg = xc_ref[:, SSM_D_INNER + n * (SSM_N_GROUPS + g):SSM_D_INNER + n * (SSM_N_GROUPS + g + 1)].astype(BF16)
            gsl = slice(gc * g, gc * (g + 1))
            gm = _nt(cg, bg)
            hgt = h_scr[:, gsl]
            hs_ref[:, gsl] = hgt
            y_off = esx[:, gsl] * _nn(cg, hgt)
            h_scr[:, gsl] = etotx[:, gsl] * hgt + _tn(bg, xdec[:, gsl])
            for k in range(SSM_PAIRS_PER_GROUP):
                h0 = g * SSM_HEADS_PER_GROUP + 2 * k
                lo = gc * g + LANES * k
                ms = []
                for h in (h0, h0 + 1):
                    lm = jnp.exp(jnp.where(lower, s_col[:, h:h + 1] - s_row[h:h + 1, :], NEG_INF))
                    ms.append((gm * lm).astype(BF16))
                y_diag = _nn(jnp.concatenate(ms, axis=1), _split_pair(xdt[:, lo:lo + LANES]))
                y_ref[:, lo:lo + LANES] = y_diag + y_off[:, LANES * k:LANES * (k + 1)] + skip[:, lo:lo + LANES]

    vec = pl.BlockSpec((1, LANES), lambda bi, c: (0, 0))
    return pl.pallas_call(
        body, name=name, grid=(b, nc),
        in_specs=[pl.BlockSpec((None, q, SSM_CONV_DIM), lambda bi, c: (bi, c, 0)),
                  pl.BlockSpec((None, q, LANES), lambda bi, c: (bi, c, 0)), vec, vec,
                  pl.BlockSpec((1, SSM_D_INNER), lambda bi, c: (0, 0)),
                  pl.BlockSpec((LANES, SSM_D_INNER), lambda bi, c: (0, 0))],
        out_specs=[pl.BlockSpec((None, q, SSM_D_INNER), lambda bi, c: (bi, c, 0)),
                   pl.BlockSpec((None, None, n, SSM_D_INNER), lambda bi, c: (bi, c, 0, 0))],
        out_shape=[jax.ShapeDtypeStruct((b, s, SSM_D_INNER), F32),
                   jax.ShapeDtypeStruct((b, nc, n, SSM_D_INNER), F32)],
        scratch_shapes=[pltpu.VMEM((n, SSM_D_INNER), F32)],
        compiler_params=_params("parallel", "arbitrary"),
    )(xc, dtr, dt_bias, a_log, dskx, to_channels)


def _ssd_bwd(xc, dtr, dy, hs, dt_bias, a_log, dskx, to_channels, to_heads, dproj, name):
    b, s, _ = xc.shape
    q = SSM_CHUNK
    nc = s // q
    n, gc = SSM_D_STATE, SSM_GROUP_CHANNELS

    def colsum(v):
        return jnp.sum(v, axis=0, keepdims=True)

    def body(xc_ref, dtr_ref, dy_ref, hs_ref, bias_ref, alog_ref, dsk_ref, tc_ref, th_ref, buf_ref,
             dxc_ref, ddtr_ref, dalog_ref, ddsk_ref, dbias_ref, dh_scr, dxs_scr, dxd_scr, w_scr, dst_scr, rows_scr):
        ci = pl.program_id(1)

        @pl.when(ci == 0)
        def _():
            dh_scr[...] = jnp.zeros_like(dh_scr)

        @pl.when(jnp.logical_and(pl.program_id(0) == 0, ci == 0))
        def _():
            dalog_ref[...] = jnp.zeros_like(dalog_ref)
            ddsk_ref[...] = jnp.zeros_like(ddsk_ref)
            dbias_ref[...] = jnp.zeros_like(dbias_ref)
            dst_scr[...] = jnp.zeros_like(dst_scr)

        dt, a_neg, s_col, s_row, lower = _ssd_chunk_terms(dtr_ref, bias_ref, alog_ref)
        upper = jnp.logical_not(lower) | (lax.broadcasted_iota(jnp.int32, (q, q), 0)
                                          == lax.broadcasted_iota(jnp.int32, (q, q), 1))
        dtx, esx, decx, etotx = _decay_terms_per_channel(dt, s_col, tc_ref[...])
        x = xc_ref[:, :SSM_D_INNER]
        dyv = dy_ref[...]
        xdt = x * dtx
        xdec = xdt * decx
        dw = esx * dyv
        rows_scr[...] = jnp.zeros_like(rows_scr)
        for g in range(SSM_N_GROUPS):
            b_lo = SSM_D_INNER + n * g
            c_lo = SSM_D_INNER + n * (SSM_N_GROUPS + g)
            bg = xc_ref[:, b_lo:b_lo + n].astype(BF16)
            cg = xc_ref[:, c_lo:c_lo + n].astype(BF16)
            gsl = slice(gc * g, gc * (g + 1))
            gm = _nt(cg, bg)
            gmt = _nt(bg, cg)
            hgt = hs_ref[:, gsl]
            dhgt = dh_scr[:, gsl]
            w_scr[:, gsl] = _nn(cg, hgt)
            dcg = _nt(dw[:, gsl], hgt)
            dxs = decx[:, gsl] * _nn(bg, dhgt)
            dxs_scr[:, gsl] = dxs
            dbg = _nt(xdec[:, gsl], dhgt)
            rows_scr[2:3, gsl] = colsum(dhgt * hgt)
            dh_scr[:, gsl] = _tn(cg, dw[:, gsl]) + etotx[:, gsl] * dhgt
            dg = jnp.zeros((q, q), F32)
            dgt = jnp.zeros((q, q), F32)
            for k in range(SSM_PAIRS_PER_GROUP):
                h0 = g * SSM_HEADS_PER_GROUP + 2 * k
                lo = gc * g + LANES * k
                xp = xdt[:, lo:lo + LANES]
                dyp = dyv[:, lo:lo + LANES]
                dy2 = _split_pair(dyp)
                dm2 = _nt(dy2, xp)
                dmt2 = _nt(_split_pair(xp), dyp)
                mts = []
                for i, h in enumerate((h0, h0 + 1)):
                    lm = jnp.exp(jnp.where(lower, s_col[:, h:h + 1] - s_row[h:h + 1, :], NEG_INF))
                    lmt = jnp.exp(jnp.where(upper, s_row[h:h + 1, :] - s_col[:, h:h + 1], NEG_INF))
                    dm = dm2[q * i:q * (i + 1), :]
                    dmt = dmt2[q * i:q * (i + 1), :]
                    dg = dg + dm * lm
                    dgt = dgt + dmt * lmt
                    mt = gmt * lmt
                    dst_scr[h:h + 1, :] = colsum(dmt * mt) - colsum(dm * (gm * lm))
                    mts.append(mt.astype(BF16))
                dxd_scr[:, lo:lo + LANES] = _nn(jnp.concatenate(mts, axis=1), dy2)
            dxc_ref[:, b_lo:b_lo + n] = dbg + _nn(dgt, cg)
            dxc_ref[:, c_lo:c_lo + n] = dcg + _nn(dg, bg)
        dxs = dxs_scr[...]
        dxdt = dxd_scr[...] + dxs
        dxc_ref[:, :SSM_D_INNER] = dxdt * dtx + dsk_ref[...] * dyv
        state_part = xdt * dxs
        rows_scr[0:1, :] = colsum(dyv * x)
        rows_scr[1:2, :] = colsum(state_part)
        th = th_ref[...]
        per_head = _per_head(jnp.concatenate([dw * w_scr[...] - state_part, dxdt * x], axis=0), th)
        r_ds, r_dt = per_head[:q], per_head[q:]
        sums = _per_head(rows_scr[...], th)
        etot = jnp.exp(s_col[q - 1:q, :])
        dtot = sums[1:2, :] + etot * sums[2:3, :]
        last = lax.broadcasted_iota(jnp.int32, (q, LANES), 0) == q - 1
        ds = dst_scr[...].T + r_ds + jnp.where(last, dtot, 0.0)
        da = _mask_nn(upper, ds)
        ddt = da * a_neg + r_dt
        live = lax.broadcasted_iota(jnp.int32, (1, LANES), 1) < SSM_N_HEADS
        sg = _sigmoid(dtr_ref[...] + bias_ref[...])
        ddtr = jnp.where(live, ddt * sg, 0.0)
        ddtr_ref[:, :LANES] = ddtr.astype(BF16)
        ddtr_ref[:, LANES:] = jnp.zeros((q, DPROJ_DT_WIDTH - LANES), BF16)
        dalog_ref[...] += jnp.where(live, colsum(da * dt) * a_neg, 0.0)
        ddsk_ref[...] += jnp.where(live, sums[0:1, :], 0.0)
        dbias_ref[...] += colsum(ddtr)

    rev = lambda bi, c: (bi, nc - 1 - c, 0)
    vec = pl.BlockSpec((1, LANES), lambda bi, c: (0, 0))
    wide = pl.BlockSpec((None, q, SSM_D_INNER), rev)
    return pl.pallas_call(
        body, name=name, grid=(b, nc),
        in_specs=[pl.BlockSpec((None, q, SSM_CONV_DIM), rev), pl.BlockSpec((None, q, LANES), rev), wide,
                  pl.BlockSpec((None, None, n, SSM_D_INNER), lambda bi, c: (bi, nc - 1 - c, 0, 0)),
                  vec, vec, pl.BlockSpec((1, SSM_D_INNER), lambda bi, c: (0, 0)),
                  pl.BlockSpec((LANES, SSM_D_INNER), lambda bi, c: (0, 0)),
                  pl.BlockSpec((SSM_D_INNER, LANES), lambda bi, c: (0, 0)),
                  pl.BlockSpec(memory_space=pl.ANY)],
        out_specs=[pl.BlockSpec((None, q, SSM_CONV_DIM), rev),
                   pl.BlockSpec((None, q, DPROJ_DT_WIDTH),
                                lambda bi, c: (bi, nc - 1 - c, DPROJ_COLS["dt"] // DPROJ_DT_WIDTH)), vec, vec, vec],
        input_output_aliases={9: 1},
        out_shape=[jax.ShapeDtypeStruct((b, s, SSM_CONV_DIM), F32), jax.ShapeDtypeStruct(dproj.shape, dproj.dtype),
                   jax.ShapeDtypeStruct((1, LANES), F32), jax.ShapeDtypeStruct((1, LANES), F32),
                   jax.ShapeDtypeStruct((1, LANES), F32)],
        scratch_shapes=[pltpu.VMEM((n, SSM_D_INNER), F32)] + [pltpu.VMEM((q, SSM_D_INNER), F32)] * 3
        + [pltpu.VMEM((LANES, q), F32), pltpu.VMEM((8, SSM_D_INNER), F32)],
        compiler_params=_params("arbitrary", "arbitrary"),
    )(xc, dtr, dy, hs, dt_bias, a_log, dskx, to_channels, to_heads, dproj)


SSM_GROUP_WIDTH = SSM_D_INNER // SSM_N_GROUPS


def _gate_norm_fwd(y, z, w, name):
    t, d = y.shape
    tm = _pick(t, (256, 128))

    def body(y_ref, z_ref, w_ref, o_ref):
        for g in range(SSM_N_GROUPS):
            sl = slice(SSM_GROUP_WIDTH * g, SSM_GROUP_WIDTH * (g + 1))
            zv = z_ref[:, sl]
            u = y_ref[:, sl] * (zv * _sigmoid(zv))
            r = lax.rsqrt(jnp.mean(u * u, axis=-1, keepdims=True) + EPS)
            o_ref[:, sl] = ((u * r) * w_ref[:, sl]).astype(BF16)

    row = pl.BlockSpec((tm, d), lambda i: (i, 0))
    return pl.pallas_call(
        body, name=name, grid=(t // tm,),
        in_specs=[row, row, pl.BlockSpec((1, d), lambda i: (0, 0))], out_specs=row,
        out_shape=jax.ShapeDtypeStruct((t, d), BF16),
        compiler_params=_params("parallel"),
    )(y, z, w)


def _ssm_out_dx_gate_norm_bwd(dys, w_ssm_out, y, z, w, dproj, name):
    t, d = y.shape
    k = dys.shape[1]
    gw = SSM_GROUP_WIDTH
    tm = _pick(t, (512, 256, 128))

    def body(dys_ref, ws_ref, y_ref, z_ref, w_ref, buf_ref, dy_ref, dz_ref, dw_ref):
        @pl.when(pl.program_id(0) == 0)
        def _():
            dw_ref[...] = jnp.zeros_like(dw_ref)

        dout = _nt(dys_ref[...], ws_ref[...])
        for g in range(SSM_N_GROUPS):
            sl = slice(gw * g, gw * (g + 1))
            zv = z_ref[:, sl]
            yv = y_ref[:, sl]
            sg = _sigmoid(zv)
            silu = zv * sg
            u = yv * silu
            r = lax.rsqrt(jnp.mean(u * u, axis=-1, keepdims=True) + EPS)
            uh = u * r
            dov = dout[:, sl]
            dw_ref[:, sl] += jnp.sum(dov * uh, axis=0, keepdims=True)
            dyg = dov * w_ref[:, sl]
            du = r * (dyg - uh * jnp.mean(dyg * uh, axis=-1, keepdims=True))
            dy_ref[:, sl] = du * silu
            dz_ref[:, sl] = (du * yv * (sg * (1.0 + zv * (1.0 - sg)))).astype(BF16)

    row = pl.BlockSpec((tm, d), lambda i: (i, 0))
    vec = pl.BlockSpec((1, d), lambda i: (0, 0))
    z_cols = pl.BlockSpec((tm, d), lambda i: (i, DPROJ_COLS["z"] // d))
    return pl.pallas_call(
        body, name=name, grid=(t // tm,),
        in_specs=[pl.BlockSpec((tm, k), lambda i: (i, 0)), pl.BlockSpec((d, k), lambda i: (0, 0)), row, row, vec,
                  pl.BlockSpec(memory_space=pl.ANY)],
        out_specs=[row, z_cols, vec],
        out_shape=[jax.ShapeDtypeStruct((t, d), F32), jax.ShapeDtypeStruct(dproj.shape, dproj.dtype),
                   jax.ShapeDtypeStruct((1, d), F32)],
        input_output_aliases={5: 1},
        compiler_params=_params("arbitrary"),
    )(dys, w_ssm_out, y, z, w, dproj)


def _rope_tables(s):
    half = ATT_HEAD_DIM // 2
    inv = ROPE_THETA ** (-jnp.arange(half, dtype=F32) / half)
    ang = jnp.arange(s).astype(F32)[:, None] * inv[None, :]
    cos, sin = jnp.cos(ang), jnp.sin(ang)
    return jnp.concatenate([cos, cos], axis=-1), jnp.concatenate([-sin, sin], axis=-1)


ATT_TILE = 256


def _by_residue_spec(r, width):
    return pl.BlockSpec((None, r, ATT_TILE // r, width), lambda bi, i: (bi, 0, i, 0))


def _to_residues(tile, stage, r, store):
    if r == 1:
        store(0, tile)
        return
    stage[...] = tile
    for ri in range(r):
        store(ri, stage[pl.ds(ri, tile.shape[0] // r, stride=r), :])


def _from_residues(load, stage, r):
    if r == 1:
        return load(0)
    for ri in range(r):
        stage[pl.ds(ri, ATT_TILE // r, stride=r), :] = load(ri)
    return stage[...]


QKV_ROWS = 1024
QKV_COLS = 768


def _qkv_proj_rope(h, w_qkv_t, cosf, sinf, b, s, name):
    t, k = h.shape
    tm, d, gw = QKV_ROWS, ATT_HEAD_DIM, ATT_OUT_DIM
    per_seq = s // tm

    def body(h_ref, w_ref, c_ref, s_ref, *rest):
        outs, stage = rest[:-1], rest[-1]
        cv, sv = c_ref[...], s_ref[...]
        hv = h_ref[...]
        for lo in range(0, ATT_QKV_DIM, QKV_COLS):
            acc = _nt(hv, w_ref[lo:lo + QKV_COLS, :])
            for hh in range(QKV_COLS // d):
                kind, head = divmod(lo // d + hh, ATT_N_HEADS)
                gi, j = divmod(head, ATT_HEADS_PER_GROUP)
                dst = slice(kind * gw + d * j, kind * gw + d * (j + 1))
                tv = acc[:, d * hh:d * (hh + 1)]
                if kind < 2:
                    tv = tv * cv + pltpu.roll(tv, d // 2, 1) * sv

                def store(ri, rows, o_ref=outs[gi], dst=dst):
                    o_ref[ri, :, dst] = rows.astype(BF16)

                _to_residues(tv, stage, ATT_DILATIONS[gi], store)

    tab = pl.BlockSpec((tm, d), lambda i: (i % per_seq, 0))
    return pl.pallas_call(
        body, name=name, grid=(t // tm,),
        in_specs=[pl.BlockSpec((tm, k), lambda i: (i, 0)), pl.BlockSpec((ATT_QKV_DIM, k), lambda i: (0, 0)), tab, tab],
        out_specs=[pl.BlockSpec((None, r, tm // r, 3 * gw), lambda i: (i // per_seq, 0, i % per_seq, 0))
                   for r in ATT_DILATIONS],
        out_shape=[jax.ShapeDtypeStruct((b, r, s // r, 3 * gw), BF16) for r in ATT_DILATIONS],
        scratch_shapes=[pltpu.VMEM((tm, d), F32)],
        compiler_params=_params("parallel"),
    )(h, w_qkv_t, cosf, sinf)


def _rope_bwd(dq, dk, dv, cosf, sinf, dproj, name):
    n_pat = len(ATT_DILATIONS)
    b, _, s, gw = dq[0].shape
    ts, d = ATT_TILE, ATT_HEAD_DIM

    def body(*refs):
        ins, (c_ref, s_ref, _, o_ref, stage) = refs[:3 * n_pat], refs[3 * n_pat:]
        cv, sv = c_ref[...], s_ref[...]
        for kind in range(3):
            for gi, r in enumerate(ATT_DILATIONS):
                src = ins[kind * n_pat + gi]
                for j in range(ATT_HEADS_PER_GROUP):
                    tv = _from_residues(lambda ri, src=src, j=j: src[ri, :, d * j:d * (j + 1)], stage, r)
                    if kind < 2:
                        tv = tv * cv + pltpu.roll(tv * sv, d // 2, 1)
                    lo = d * (kind * ATT_N_HEADS + gi * ATT_HEADS_PER_GROUP + j)
                    o_ref[:, lo:lo + d] = tv.astype(BF16)

    tab = pl.BlockSpec((ts, d), lambda bi, i: (i, 0))
    parts = [_by_residue_spec(r, gw) for r in ATT_DILATIONS]
    return pl.pallas_call(
        body, name=name, grid=(b, s // ts), in_specs=parts * 3 + [tab, tab, pl.BlockSpec(memory_space=pl.ANY)],
        out_specs=pl.BlockSpec((None, ts, ATT_QKV_DIM), lambda bi, i: (bi, i, DPROJ_COLS["qkv"] // ATT_QKV_DIM)),
        out_shape=jax.ShapeDtypeStruct(dproj.shape, dproj.dtype),
        input_output_aliases={3 * n_pat + 2: 0},
        scratch_shapes=[pltpu.VMEM((ts, d), F32)],
        compiler_params=_params("parallel", "parallel"),
    )(*dq, *dk, *dv, cosf, sinf, dproj)


ATT_SCALE = ATT_HEAD_DIM ** -0.5
ATT_STEP = 2 * ATT_BLOCK


def _att_spec(col):
    return pl.BlockSpec((None, None, ATT_STEP, ATT_OUT_DIM), lambda bi, ri, i: (bi, ri, i, col))


def _att_edge_spec(col, side, n_steps):
    def index(bi, ri, i):
        blk = 2 * i - 1 if side < 0 else 2 * i + 2
        return (bi, ri, jnp.clip(blk, 0, 2 * n_steps - 1), col)
    return pl.BlockSpec((None, None, ATT_BLOCK, ATT_OUT_DIM), index)


def _band_mask(shape, q_axis, has_prev):
    qi = lax.broadcasted_iota(jnp.int32, shape, q_axis)
    kj = lax.broadcasted_iota(jnp.int32, shape, 1 - q_axis)
    dist = qi + ATT_BLOCK - kj
    return (dist >= 0) & (dist <= ATT_BLOCK) & (has_prev | (kj >= ATT_BLOCK))


def _att_fwd(qkr, name):
    b, r, l, _ = qkr.shape
    nb = l // ATT_STEP
    d = ATT_HEAD_DIM

    def body(q_ref, kp_ref, k_ref, vp_ref, v_ref, o_ref, lse_ref):
        mask = _band_mask((ATT_STEP, ATT_BLOCK + ATT_STEP), 0, pl.program_id(2) > 0)
        heads = [slice(d * j, d * (j + 1)) for j in range(ATT_HEADS_PER_GROUP)]
        scores = [_nt(q_ref[:, sl], jnp.concatenate([kp_ref[:, sl], k_ref[:, sl]], axis=0)) for sl in heads]
        scores = [jnp.where(mask, sc * ATT_SCALE, NEG_INF) for sc in scores]
        tops = [jnp.max(sc, axis=-1, keepdims=True) for sc in scores]
        probs = [jnp.exp(sc - m) for sc, m in zip(scores, tops)]
        dens = [jnp.sum(pr, axis=-1, keepdims=True) for pr in probs]
        for sl, m, pr, den in zip(heads, tops, probs, dens):
            o_ref[:, sl] = _nn(pr / den, jnp.concatenate([vp_ref[:, sl], v_ref[:, sl]], axis=0))
            lse_ref[:, sl] = jnp.broadcast_to(m + jnp.log(den), (ATT_STEP, d))

    out_spec = _att_spec(0)
    return pl.pallas_call(
        body, name=name, grid=(b, r, nb),
        in_specs=[_att_spec(0), _att_edge_spec(1, -1, nb), _att_spec(1), _att_edge_spec(2, -1, nb), _att_spec(2)],
        out_specs=[out_spec, out_spec],
        out_shape=[jax.ShapeDtypeStruct((b, r, l, ATT_OUT_DIM), F32)] * 2,
        compiler_params=_params("parallel", "parallel", "parallel"),
    )(qkr, qkr, qkr, qkr, qkr)


def _att_merge(os_, lses, name):
    n_pat = len(os_)
    b, _, s, gw = os_[0].shape
    ts, d = ATT_TILE, ATT_HEAD_DIM

    def body(*refs):
        o_refs, l_refs = refs[:n_pat], refs[n_pat:2 * n_pat]
        att_ref, lse_outs, stage = refs[2 * n_pat], refs[2 * n_pat + 1:3 * n_pat + 1], refs[-1]
        for j in range(ATT_HEADS_PER_GROUP):
            sl = slice(d * j, d * (j + 1))
            ov = [_from_residues(lambda ri, g=g: o_refs[g][ri, :, sl], stage, r)
                  for g, r in enumerate(ATT_DILATIONS)]
            ls = [_from_residues(lambda ri, g=g: l_refs[g][ri, :, sl], stage, r)
                  for g, r in enumerate(ATT_DILATIONS)]
            m = functools.reduce(jnp.maximum, ls)
            es = [jnp.exp(lv - m) for lv in ls]
            tot = functools.reduce(lambda u, v: u + v, es)
            acc = (es[0] / tot) * ov[0]
            for g in range(1, n_pat):
                acc = acc + (es[g] / tot) * ov[g]
            att_ref[:, sl] = acc
            joint = m + jnp.log(tot)
            for g, r in enumerate(ATT_DILATIONS):
                def store(ri, rows, out=lse_outs[g]):
                    out[ri, :, sl] = rows
                _to_residues(joint, stage, r, store)

    parts = [_by_residue_spec(r, gw) for r in ATT_DILATIONS]
    return pl.pallas_call(
        body, name=name, grid=(b, s // ts), in_specs=parts * 2,
        out_specs=[pl.BlockSpec((None, ts, gw), lambda bi, i: (bi, i, 0))] + parts,
        out_shape=[jax.ShapeDtypeStruct((b, s, gw), F32)]
        + [jax.ShapeDtypeStruct((b, r, s // r, gw), F32) for r in ATT_DILATIONS],
        scratch_shapes=[pltpu.VMEM((ts, d), F32)],
        compiler_params=_params("parallel", "parallel"),
    )(*os_, *lses)


def _att_delta(att, datt, name):
    b, s, gw = att.shape
    ts, d = ATT_TILE, ATT_HEAD_DIM
    n_pat = len(ATT_DILATIONS)

    def body(a_ref, d_ref, *rest):
        do_outs, dl_outs, stage = rest[:n_pat], rest[n_pat:2 * n_pat], rest[-1]
        for j in range(ATT_HEADS_PER_GROUP):
            sl = slice(d * j, d * (j + 1))
            dv = d_ref[:, sl]
            delta = jnp.broadcast_to(jnp.sum(a_ref[:, sl] * dv, axis=-1, keepdims=True), (ts, d))
            for g, r in enumerate(ATT_DILATIONS):
                def store_do(ri, rows, out=do_outs[g]):
                    out[ri, :, sl] = rows.astype(BF16)

                def store_dl(ri, rows, out=dl_outs[g]):
                    out[ri, :, sl] = rows

                _to_residues(dv, stage, r, store_do)
                _to_residues(delta, stage, r, store_dl)

    row = pl.BlockSpec((None, ts, gw), lambda bi, i: (bi, i, 0))
    parts = [_by_residue_spec(r, gw) for r in ATT_DILATIONS]
    outs = pl.pallas_call(
        body, name=name, grid=(b, s // ts), in_specs=[row, row], out_specs=parts * 2,
        out_shape=[jax.ShapeDtypeStruct((b, r, s // r, gw), BF16) for r in ATT_DILATIONS]
        + [jax.ShapeDtypeStruct((b, r, s // r, gw), F32) for r in ATT_DILATIONS],
        scratch_shapes=[pltpu.VMEM((ts, d), F32)],
        compiler_params=_params("parallel", "parallel"),
    )(att, datt)
    return outs[:n_pat], outs[n_pat:]


def _att_bwd_q(qkr, datt, lse, delta, name):
    b, r, l, _ = qkr.shape
    nb = l // ATT_STEP
    d = ATT_HEAD_DIM

    def body(q_ref, kp_ref, k_ref, vp_ref, v_ref, do_ref, lse_ref, dl_ref, dq_ref):
        mask = _band_mask((ATT_STEP, ATT_BLOCK + ATT_STEP), 0, pl.program_id(2) > 0)
        heads = [slice(d * j, d * (j + 1)) for j in range(ATT_HEADS_PER_GROUP)]
        kcats = [jnp.concatenate([kp_ref[:, sl], k_ref[:, sl]], axis=0) for sl in heads]
        scores = [_nt(q_ref[:, sl], kcat) for sl, kcat in zip(heads, kcats)]
        dps = [_nt(do_ref[:, sl], jnp.concatenate([vp_ref[:, sl], v_ref[:, sl]], axis=0)) for sl in heads]
        probs = [jnp.exp(jnp.where(mask, sc * ATT_SCALE - lse_ref[:, sl.start:sl.start + 1], NEG_INF))
                 for sl, sc in zip(heads, scores)]
        dscs = [pr * (dp - dl_ref[:, sl.start:sl.start + 1]) for sl, pr, dp in zip(heads, probs, dps)]
        for sl, dsc, kcat in zip(heads, dscs, kcats):
            dq_ref[:, sl] = _nn(dsc, kcat) * ATT_SCALE

    tok = _att_spec(0)
    return pl.pallas_call(
        body, name=name, grid=(b, r, nb),
        in_specs=[_att_spec(0), _att_edge_spec(1, -1, nb), _att_spec(1), _att_edge_spec(2, -1, nb), _att_spec(2),
                  tok, tok, tok],
        out_specs=tok,
        out_shape=jax.ShapeDtypeStruct((b, r, l, ATT_OUT_DIM), F32),
        compiler_params=_params("parallel", "parallel", "parallel"),
    )(qkr, qkr, qkr, qkr, qkr, datt, lse, delta)


def _att_bwd_kv(qkr, datt, lse, delta, name):
    b, r, l, _ = qkr.shape
    nb = l // ATT_STEP
    d = ATT_HEAD_DIM

    def body(k_ref, v_ref, q_ref, qn_ref, do_ref, don_ref, lse_ref, lsen_ref, dl_ref, dln_ref, dk_ref, dv_ref):
        shape = (ATT_STEP, ATT_STEP + ATT_BLOCK)
        kj = lax.broadcasted_iota(jnp.int32, shape, 0)
        qi = lax.broadcasted_iota(jnp.int32, shape, 1)
        dist = qi - kj
        has_next = pl.program_id(2) < nb - 1
        mask = (dist >= 0) & (dist <= ATT_BLOCK) & (has_next | (qi < ATT_STEP))
        def per_query(own_ref, next_ref, sl):
            return jnp.tile(jnp.concatenate([own_ref[:, sl], next_ref[:, sl]], axis=0).T, (ATT_STEP // d, 1))

        heads = [slice(d * j, d * (j + 1)) for j in range(ATT_HEADS_PER_GROUP)]
        qcats = [jnp.concatenate([q_ref[:, sl], qn_ref[:, sl]], axis=0) for sl in heads]
        docats = [jnp.concatenate([do_ref[:, sl], don_ref[:, sl]], axis=0) for sl in heads]
        scores = [_nt(k_ref[:, sl], qcat) for sl, qcat in zip(heads, qcats)]
        dps = [_nt(v_ref[:, sl], docat) for sl, docat in zip(heads, docats)]
        probs = [jnp.exp(jnp.where(mask, sc * ATT_SCALE - per_query(lse_ref, lsen_ref, sl), NEG_INF))
                 for sl, sc in zip(heads, scores)]
        for sl, pr, docat in zip(heads, probs, docats):
            dv_ref[:, sl] = _nn(pr, docat)
        dscs = [pr * (dp - per_query(dl_ref, dln_ref, sl)) for sl, pr, dp in zip(heads, probs, dps)]
        for sl, dsc, qcat in zip(heads, dscs, qcats):
            dk_ref[:, sl] = _nn(dsc, qcat) * ATT_SCALE

    tok, tok_n = _att_spec(0), _att_edge_spec(0, 1, nb)
    return pl.pallas_call(
        body, name=name, grid=(b, r, nb),
        in_specs=[_att_spec(1), _att_spec(2), _att_spec(0), _att_edge_spec(0, 1, nb),
                  tok, tok_n, tok, tok_n, tok, tok_n],
        out_specs=[tok, tok],
        out_shape=[jax.ShapeDtypeStruct((b, r, l, ATT_OUT_DIM), F32)] * 2,
        compiler_params=_params("parallel", "parallel", "parallel"),
    )(qkr, qkr, qkr, qkr, datt, datt, lse, lse, delta, delta)


def _mix_fwd(gl, bg, ys, ya, name):
    t, d = ys.shape
    tm = _pick(t, (512, 256, 128))

    def body(gl_ref, bg_ref, ys_ref, ya_ref, o_ref):
        g0 = _sigmoid(gl_ref[:, :d] + bg_ref[:, :d])
        g1 = _sigmoid(gl_ref[:, d:] + bg_ref[:, d:])
        o_ref[...] = (g0 * ys_ref[...] + g1 * ya_ref[...]).astype(BF16)

    row = pl.BlockSpec((tm, d), lambda i: (i, 0))
    return pl.pallas_call(
        body, name=name, grid=(t // tm,),
        in_specs=[pl.BlockSpec((tm, 2 * d), lambda i: (i, 0)), pl.BlockSpec((1, 2 * d), lambda i: (0, 0)), row, row],
        out_specs=row, out_shape=jax.ShapeDtypeStruct((t, d), BF16),
        compiler_params=_params("parallel"),
    )(gl, bg, ys, ya)


def _mix_bwd(gl, bg, ys, ya, dmixed, name):
    t, d = ys.shape
    tm = _pick(t, (512, 256, 128))

    def body(gl_ref, bg_ref, ys_ref, ya_ref, dm_ref, dys_ref, dya_ref, dgl_ref, dbg_ref):
        @pl.when(pl.program_id(0) == 0)
        def _():
            dbg_ref[...] = jnp.zeros_like(dbg_ref)

        dm = dm_ref[...]
        g0 = _sigmoid(gl_ref[:, :d] + bg_ref[:, :d])
        g1 = _sigmoid(gl_ref[:, d:] + bg_ref[:, d:])
        dys_ref[...] = (dm * g0).astype(BF16)
        dya_ref[...] = (dm * g1).astype(BF16)
        d0 = dm * ys_ref[...] * (g0 * (1.0 - g0))
        d1 = dm * ya_ref[...] * (g1 * (1.0 - g1))
        dgl_ref[:, :d] = d0.astype(BF16)
        dgl_ref[:, d:] = d1.astype(BF16)
        dbg_ref[:, :d] += jnp.sum(d0, axis=0, keepdims=True)
        dbg_ref[:, d:] += jnp.sum(d1, axis=0, keepdims=True)

    row = pl.BlockSpec((tm, d), lambda i: (i, 0))
    wide = pl.BlockSpec((tm, 2 * d), lambda i: (i, 0))
    vec = pl.BlockSpec((1, 2 * d), lambda i: (0, 0))
    gate_cols = pl.BlockSpec((tm, 2 * d), lambda i: (i, DPROJ_COLS["gate"] // (2 * d)))
    return pl.pallas_call(
        body, name=name, grid=(t // tm,),
        in_specs=[wide, vec, row, row, row], out_specs=[row, row, gate_cols, vec],
        out_shape=[jax.ShapeDtypeStruct((t, d), BF16), jax.ShapeDtypeStruct((t, d), BF16),
                   jax.ShapeDtypeStruct((t, DPROJ_WIDTH), BF16), jax.ShapeDtypeStruct((1, 2 * d), F32)],
        compiler_params=_params("arbitrary"),
    )(gl, bg, ys, ya, dmixed)


def _up_proj_swiglu(h, w_up_t, gt, name):
    t, k = h.shape
    f = w_up_t.shape[0]
    tm, tn, _ = _mm_tiles(t, f, k, h.dtype.itemsize, w_up_t.dtype.itemsize, 4 + 2, True)

    def body(h_ref, w_ref, g_ref, up_ref, act_ref):
        up = _nt(h_ref[...], w_ref[...])
        up_ref[...] = up
        gv = g_ref[...]
        act_ref[...] = ((gv * _sigmoid(gv)) * up).astype(BF16)

    tile = pl.BlockSpec((tm, tn), lambda i, j: (i, j))
    return pl.pallas_call(
        body, name=name, grid=(t // tm, f // tn),
        in_specs=[pl.BlockSpec((tm, k), lambda i, j: (i, 0)), pl.BlockSpec((tn, k), lambda i, j: (j, 0)), tile],
        out_specs=[tile, tile],
        out_shape=[jax.ShapeDtypeStruct((t, f), F32), jax.ShapeDtypeStruct((t, f), BF16)],
        compiler_params=_params("parallel", "parallel"),
    )(h, w_up_t, gt)


def _down_dx_swiglu_bwd(dx, w_down, gt, up, name):
    t, k = dx.shape
    f = w_down.shape[0]
    tm, tn, _ = _mm_tiles(t, f, k, dx.dtype.itemsize, w_down.dtype.itemsize, 2 + 2, True)
    tm = min(tm, 512)

    def body(d_ref, w_ref, g_ref, u_ref, dg_ref, du_ref):
        dact = _nt(d_ref[...], w_ref[...])
        gv = g_ref[...]
        sg = _sigmoid(gv)
        dg_ref[...] = (dact * u_ref[...] * (sg * (1.0 + gv * (1.0 - sg)))).astype(BF16)
        du_ref[...] = (dact * (gv * sg)).astype(BF16)

    tile = pl.BlockSpec((tm, tn), lambda i, j: (i, j))
    return pl.pallas_call(
        body, name=name, grid=(t // tm, f // tn),
        in_specs=[pl.BlockSpec((tm, k), lambda i, j: (i, 0)), pl.BlockSpec((tn, k), lambda i, j: (j, 0)), tile, tile],
        out_specs=[tile, tile], out_shape=[jax.ShapeDtypeStruct((t, f), BF16)] * 2,
        compiler_params=_params("parallel", "parallel"),
    )(dx, w_down, gt, up)


def _peer(k):
    x, y, c = lax.axis_index("x"), lax.axis_index("y"), lax.axis_index("c")
    px, py, pc = x ^ ((k >> 2) & 1), y ^ ((k >> 1) & 1), c ^ (k & 1)
    return (px, py, pc), 4 * px + 2 * py + pc


def _my_index():
    return 4 * lax.axis_index("x") + 2 * lax.axis_index("y") + lax.axis_index("c")


def _all_gather(parts, name):
    n_parts = len(parts)

    def body(*refs):
        ins, outs = refs[:n_parts], refs[n_parts:2 * n_parts]
        send_sems, recv_sems, local_sems = refs[2 * n_parts:]
        here, me = _peer(0)
        sibling, sib_idx = _peer(1)
        chips = [_peer(2 * q) for q in range(1, N_CHIPS)]

        def copy(i, k, block, to, src=None):
            return pltpu.make_async_remote_copy(
                src_ref=outs[i].at[block] if src is None else src, dst_ref=outs[i].at[block],
                send_sem=send_sems.at[i * (N_DEV - 1) + k], recv_sem=recv_sems.at[i * (N_DEV - 1) + k],
                device_id=to, device_id_type=MESH)

        local = [pltpu.make_async_copy(ins[i], outs[i].at[me], local_sems.at[i]) for i in range(n_parts)]
        for cp in local:
            cp.start()
        sends = []
        for i in range(n_parts):
            sends.append(copy(i, 0, me, sibling, src=ins[i]))
            sends += [copy(i, q, me, chip, src=ins[i]) for q, (chip, _) in enumerate(chips, start=1)]
        for cp in sends:
            cp.start()
        for q, (chip, chip_idx) in enumerate(chips, start=1):
            for i in range(n_parts):
                copy(i, q, chip_idx, here).wait_recv()
                fwd = copy(i, N_CHIPS - 1 + q, chip_idx, sibling)
                fwd.start()
                sends.append(fwd)
        for i in range(n_parts):
            copy(i, 0, sib_idx, here).wait_recv()
        for q, (_, chip_idx) in enumerate(chips, start=1):
            for i in range(n_parts):
                copy(i, N_CHIPS - 1 + q, chip_idx ^ 1, here).wait_recv()
        for cp in sends:
            cp.wait_send()
        for cp in local:
            cp.wait()

    hbm = pl.BlockSpec(memory_space=pl.ANY)
    return pl.pallas_call(
        body, name=name, in_specs=[hbm] * n_parts, out_specs=[hbm] * n_parts,
        out_shape=[jax.ShapeDtypeStruct((N_DEV,) + p_.shape, p_.dtype) for p_ in parts],
        scratch_shapes=[pltpu.SemaphoreType.DMA((n_parts * (N_DEV - 1),)),
                        pltpu.SemaphoreType.DMA((n_parts * (N_DEV - 1),)),
                        pltpu.SemaphoreType.DMA((n_parts,))],
        compiler_params=pltpu.CompilerParams(has_side_effects=True),
    )(*parts)


HBM_SPEC = pl.BlockSpec(memory_space=pltpu.HBM)
SEM_SPEC = pl.BlockSpec(memory_space=pltpu.SEMAPHORE)
DATAFLOW = pltpu.SideEffectType.DATAFLOW_SIDE_EFFECTING


def _gather_start(block, after, name):
    per_peer = block.ndim == 3

    def body(v_ref, land_ref, after_ref, send_sems, recv_sems, v_thru, land_thru, token):
        me = _my_index()
        for k in range(1, N_DEV):
            peer, pidx = _peer(k)
            pltpu.make_async_remote_copy(
                src_ref=v_ref.at[pidx] if per_peer else v_ref, dst_ref=land_ref.at[me],
                send_sem=send_sems.at[k - 1], recv_sem=recv_sems.at[k - 1],
                device_id=peer, device_id_type=MESH).start()
        token[...] = jnp.zeros_like(token)

    land_shape = (N_DEV,) + block.shape[-2:]
    return pl.pallas_call(
        body, name=name,
        out_shape=(pltpu.SemaphoreType.DMA((N_DEV - 1,)), pltpu.SemaphoreType.DMA((N_DEV - 1,)),
                   pltpu.HBM(block.shape, block.dtype), pltpu.HBM(land_shape, block.dtype),
                   jax.ShapeDtypeStruct((8, LANES), F32)),
        in_specs=(HBM_SPEC, HBM_SPEC, pl.BlockSpec(memory_space=pl.ANY)),
        out_specs=(SEM_SPEC, SEM_SPEC, HBM_SPEC, HBM_SPEC, pl.BlockSpec(memory_space=pltpu.VMEM)),
        input_output_aliases={0: 2, 1: 3},
        compiler_params=pltpu.CompilerParams(has_side_effects=DATAFLOW),
    )(pltpu.with_memory_space_constraint(block, pltpu.HBM),
      pltpu.with_memory_space_constraint(lax.empty(land_shape, block.dtype), pltpu.HBM), after)


def _gather_wait(send_sems, recv_sems, block, landing, after, name):
    per_peer = block.ndim == 3

    def body(v_ref, land_ref, send_sems, recv_sems, after_ref, v_dead, got_ref):
        for k in range(1, N_DEV):
            peer, pidx = _peer(k)
            copy = pltpu.make_async_remote_copy(
                src_ref=v_ref.at[pidx] if per_peer else v_ref, dst_ref=land_ref.at[pidx],
                send_sem=send_sems.at[k - 1], recv_sem=recv_sems.at[k - 1],
                device_id=peer, device_id_type=MESH)
            copy.wait_send()
            copy.wait_recv()

    return pl.pallas_call(
        body, name=name,
        out_shape=(pltpu.HBM(block.shape, block.dtype), pltpu.HBM(landing.shape, landing.dtype)),
        in_specs=(HBM_SPEC, HBM_SPEC, SEM_SPEC, SEM_SPEC, pl.BlockSpec(memory_space=pl.ANY)),
        out_specs=(HBM_SPEC, HBM_SPEC), input_output_aliases={0: 0, 1: 1},
        compiler_params=pltpu.CompilerParams(has_side_effects=DATAFLOW),
    )(block, landing, send_sems, recv_sems, after)[1]


TILE_ELEMS = 1024 * 1024


def _shared_exchange(shared, name):
    def body(sh_ref, gsh_ref, send_sems, recv_sems, local_sem):
        me = _my_index()
        local = pltpu.make_async_copy(sh_ref, gsh_ref.at[me], local_sem)
        local.start()
        sends = []
        for k in range(1, N_DEV):
            peer, _ = _peer(k)
            cp = pltpu.make_async_remote_copy(
                src_ref=sh_ref, dst_ref=gsh_ref.at[me], send_sem=send_sems.at[k - 1],
                recv_sem=recv_sems.at[k - 1], device_id=peer, device_id_type=MESH)
            cp.start()
            sends.append(cp)
        for k in range(1, N_DEV):
            peer, pidx = _peer(k)
            pltpu.make_async_remote_copy(
                src_ref=sh_ref, dst_ref=gsh_ref.at[pidx], send_sem=send_sems.at[k - 1],
                recv_sem=recv_sems.at[k - 1], device_id=peer, device_id_type=MESH).wait_recv()
        for cp in sends:
            cp.wait_send()
        local.wait()

    hbm = pl.BlockSpec(memory_space=pl.ANY)
    return pl.pallas_call(
        body, name=name, in_specs=[hbm], out_specs=hbm,
        out_shape=jax.ShapeDtypeStruct((N_DEV,) + shared.shape, shared.dtype),
        scratch_shapes=[pltpu.SemaphoreType.DMA((N_DEV - 1,)), pltpu.SemaphoreType.DMA((N_DEV - 1,)),
                        pltpu.SemaphoreType.DMA],
        compiler_params=pltpu.CompilerParams(has_side_effects=True),
    )(shared)


def _adamw(parts, w, m, v, name, row0=0, own=None):
    n_parts, rows, lanes = parts.shape
    tr = rows if rows * lanes <= TILE_ELEMS // 2 else _tile_rows(math.gcd(rows, row0), TILE_ELEMS // 4 // lanes, 8)
    c1 = 1.0 - ADAM_B1 ** ADAM_STEP
    c2 = 1.0 - ADAM_B2 ** ADAM_STEP

    def body(*refs):
        if own is None:
            p_ref, w_ref, m_ref, v_ref, g_ref, d_ref, nm_ref, nv_ref = refs
            terms = [p_ref[j].astype(F32) for j in range(n_parts)]
        else:
            me_ref, p_ref, own_ref, w_ref, m_ref, v_ref, g_ref, d_ref, nm_ref, nv_ref = refs
            terms = [jnp.where(me_ref[0] == j, own_ref[...], p_ref[j]).astype(F32) for j in range(n_parts)]
        g = terms[0]
        for term in terms[1:]:
            g = g + term
        nm = ADAM_B1 * m_ref[...] + (1.0 - ADAM_B1) * g
        nv = ADAM_B2 * v_ref[...] + (1.0 - ADAM_B2) * (g * g)
        g_ref[...] = g
        nm_ref[...] = nm
        nv_ref[...] = nv
        d_ref[...] = -ADAM_LR * ((nm / c1) / (jnp.sqrt(nv / c2) + ADAM_EPS) + ADAM_WD * w_ref[...])

    row = pl.BlockSpec((tr, lanes), lambda i, *_: (i, 0))
    state = pl.BlockSpec((tr, lanes), lambda i, *_: (row0 // tr + i, 0))
    in_specs = [pl.BlockSpec((n_parts, tr, lanes), lambda i, *_: (0, i, 0)), state, state, state]
    args, n_prefetch = (parts, w, m, v), 0
    if own is not None:
        slabs, me = own
        in_specs.insert(1, pl.BlockSpec((None, tr, lanes), lambda i, me_ref: (me_ref[0], i, 0)))
        args, n_prefetch = (me, parts, slabs, w, m, v), 1
    return pl.pallas_call(
        body, name=name,
        grid_spec=pltpu.PrefetchScalarGridSpec(num_scalar_prefetch=n_prefetch, grid=(rows // tr,),
                                               in_specs=in_specs, out_specs=[row] * 4),
        out_shape=[jax.ShapeDtypeStruct((rows, lanes), F32)] * 4,
        compiler_params=_params("parallel"),
    )(*args)


MATRIX_SHARDS = (
    ("w_in", (D_MODEL, IN_PROJ_DIM // N_DEV), True),
    ("w_ssm_out", (SSM_D_INNER // N_DEV, D_MODEL), False),
    ("w_att_out", (ATT_OUT_DIM, D_MODEL // N_DEV), True),
    ("w_mix_out", (D_MODEL // N_DEV, D_MODEL), False),
    ("w_ffn_gate", (D_MODEL, D_FF // N_DEV), True),
    ("w_ffn_up", (D_MODEL, D_FF // N_DEV), True),
    ("w_ffn_down", (D_FF // N_DEV, D_MODEL), False),
)
CONV_SHARD = ("conv_w", (SSM_CONV, SSM_CONV_DIM // N_DEV), True)
SHARDED = MATRIX_SHARDS + (CONV_SHARD,)
REPLICATED = (("norm_mix", D_MODEL), ("b_gate", 2 * D_MODEL), ("conv_b", SSM_CONV_DIM), ("dt_bias", SSM_N_HEADS),
              ("a_log", SSM_N_HEADS), ("d_skip", SSM_N_HEADS), ("ssm_norm", SSM_D_INNER), ("norm_ffn", D_MODEL),
              ("norm_final", D_MODEL))


def _round_up(n, mult):
    return -(-n // mult) * mult


def _pack_rows(flat, row_mult):
    rows = _round_up(-(-flat.shape[0] // LANES), row_mult)
    return jnp.pad(flat, (0, rows * LANES - flat.shape[0])).reshape(rows, LANES)


def _stacking(specs):
    return tuple((name, (shape[1], shape[0]) if by_cols else shape, by_cols) for name, shape, by_cols in specs)


def _to_stacking(vals, specs):
    return {name: (vals[name].T if by_cols else vals[name]) for name, _, by_cols in specs}


STACK_WIDTH = D_MODEL
STACK_ALIGN = 16
STACK_ORDER = ("w_ssm_out", "w_mix_out", "w_ffn_gate", "w_ffn_up", "w_ffn_down", "w_att_out", "conv_w", "w_in")
GATHER_LATER = STACK_ORDER[:-1]
REDUCE_EARLY = STACK_ORDER[:5]
REDUCE_LATE = STACK_ORDER[5:]


def _stack_layout():
    shapes = {name: shape for name, shape, _ in _stacking(SHARDED)}
    layout, off = {}, 0
    for name in STACK_ORDER:
        r, c = shapes[name]
        rows = r if c == STACK_WIDTH else _round_up(-(-(r * c) // STACK_WIDTH), STACK_ALIGN)
        layout[name] = (off, rows, (r, c))
        off = _round_up(off + rows, STACK_ALIGN)
    return layout, _round_up(off, 1024)


def _to_stack_rows(v, rows):
    if v.shape[-1] == STACK_WIDTH:
        return v
    lead = v.shape[:-2]
    flat = v.reshape(lead + (-1,))
    flat = jnp.pad(flat, [(0, 0)] * len(lead) + [(0, rows * STACK_WIDTH - flat.shape[-1])])
    return flat.reshape(lead + (rows, STACK_WIDTH))


def _from_stack_rows(block, shape):
    r, c = shape
    if c == STACK_WIDTH:
        return block
    lead = block.shape[:-2]
    return block.reshape(lead + (-1,))[..., :r * c].reshape(lead + (r, c))


def _stack(vals, dtype, skip=(), names=STACK_ORDER):
    layout, total = _stack_layout()
    order = names
    after = STACK_ORDER.index(order[-1]) + 1
    if after < len(STACK_ORDER):
        total = layout[STACK_ORDER[after]][0]
    lead = next(iter(vals.values())).shape[:-2]
    pieces = []
    for i, name in enumerate(order):
        off, rows, _ = layout[name]
        until = layout[order[i + 1]][0] if i + 1 < len(order) else total
        piece = jnp.zeros(lead + (rows, STACK_WIDTH), dtype) if name in skip else _to_stack_rows(vals[name], rows)
        pieces.append(jnp.pad(piece.astype(dtype), [(0, 0)] * len(lead) + [(0, until - off - rows), (0, 0)]))
    return jnp.concatenate(pieces, axis=-2)


def _unstack(stacked, names):
    layout, _ = _stack_layout()
    row0 = layout[names[0]][0]
    return {name: _from_stack_rows(stacked[..., layout[name][0] - row0:layout[name][0] - row0 + layout[name][1], :],
                                   layout[name][2]) for name in names}


W_IN_SHARD_ROWS = IN_PROJ_DIM // N_DEV


def _w_in_row_moves():
    moves, orig = [], 0
    for name, size in IN_SPLIT:
        for j in range(N_DEV):
            lo, hi = max(orig, W_IN_SHARD_ROWS * j), min(orig + size, W_IN_SHARD_ROWS * (j + 1))
            if lo < hi:
                moves.append((j, lo - W_IN_SHARD_ROWS * j, DPROJ_COLS[name] + lo - orig, hi - lo))
        orig += size
    return moves


def _w_in_from_shards(shards, name):
    total, base = shards.shape[1], 0
    pad_lo, pad_hi = DPROJ_COLS["dt"] + _round_up(SSM_N_HEADS, STACK_ALIGN), DPROJ_COLS["dt"] + DPROJ_DT_WIDTH

    def body(x_ref, o_ref):
        o_ref[pad_lo:pad_hi, :] = jnp.zeros((pad_hi - pad_lo, LANES), x_ref.dtype)
        for j, r, at, n in _w_in_row_moves():
            o_ref[at:at + n, :] = x_ref[j, base + r:base + r + n, :]

    return pl.pallas_call(
        body, name=name, grid=(STACK_WIDTH // LANES,),
        in_specs=[pl.BlockSpec((N_DEV, total, LANES), lambda c: (0, 0, c))],
        out_specs=pl.BlockSpec((DPROJ_WIDTH, LANES), lambda c: (0, c)),
        out_shape=jax.ShapeDtypeStruct((DPROJ_WIDTH, STACK_WIDTH), shards.dtype),
        compiler_params=_params("parallel"),
    )(shards)


def _w_in_to_shards(dw_all, head, name):
    layout, total = _stack_layout()
    total -= layout[REDUCE_LATE[0]][0]
    base = head.shape[1]
    end = base + W_IN_SHARD_ROWS

    def body(x_ref, h_ref, o_ref):
        o_ref[:, 0:base, :] = h_ref[...]
        for j, r, at, n in _w_in_row_moves():
            o_ref[j, base + r:base + r + n, :] = x_ref[at:at + n, :]
        o_ref[:, end:total, :] = jnp.zeros((N_DEV, total - end, LANES), o_ref.dtype)

    return pl.pallas_call(
        body, name=name, grid=(STACK_WIDTH // LANES,),
        in_specs=[pl.BlockSpec((DPROJ_WIDTH, LANES), lambda c: (0, c)),
                  pl.BlockSpec((N_DEV, base, LANES), lambda c: (0, 0, c))],
        out_specs=pl.BlockSpec((N_DEV, total, LANES), lambda c: (0, 0, c)),
        out_shape=jax.ShapeDtypeStruct((N_DEV, total, STACK_WIDTH), dw_all.dtype),
        compiler_params=_params("parallel"),
    )(dw_all, head)


REPLICATED_ROWS = sum(-(-size // LANES) for _, size in REPLICATED)
LOSS_ROW = REPLICATED_ROWS


def _pack_replicated(vals):
    rows = []
    for name, size in REPLICATED:
        v = vals[name].reshape(-1).astype(F32)
        rows.append(jnp.pad(v, (0, _round_up(size, LANES) - size)))
    return _pack_rows(jnp.concatenate(rows), 8)


def _unpack_replicated(packed, shapes):
    flat = packed.reshape(-1)
    out, off = {}, 0
    for name, size in REPLICATED:
        out[name] = flat[off:off + size].reshape(shapes[name])
        off += _round_up(size, LANES)
    return out


def _lane_row(v):
    v = v.reshape(-1).astype(F32)
    return jnp.pad(v, (0, LANES - v.shape[0])).reshape(1, LANES)


IN_SPLIT = (("z", SSM_D_INNER), ("xbc", SSM_CONV_DIM), ("dt", SSM_N_HEADS), ("qkv", ATT_QKV_DIM), ("gate", 2 * D_MODEL))


def kernel(x, norm_mix, w_in, b_gate, conv_w, conv_b, dt_bias, a_log, d_skip, ssm_norm, w_ssm_out, w_att_out, w_mix_out, norm_ffn, w_ffn_gate, w_ffn_up, w_ffn_down, norm_final, loss_target, m_norm_mix, m_w_in, m_b_gate, m_conv_w, m_conv_b, m_dt_bias, m_a_log, m_d_skip, m_ssm_norm, m_w_ssm_out, m_w_att_out, m_w_mix_out, m_norm_ffn, m_w_ffn_gate, m_w_ffn_up, m_w_ffn_down, m_norm_final, v_norm_mix, v_w_in, v_b_gate, v_conv_w, v_conv_b, v_dt_bias, v_a_log, v_d_skip, v_ssm_norm, v_w_ssm_out, v_w_att_out, v_w_mix_out, v_norm_ffn, v_w_ffn_gate, v_w_ffn_up, v_w_ffn_down, v_norm_final):
    given = dict(locals())
    weights = {name: given[name][0] for name, _, _ in SHARDED}
    b, s, d = x.shape
    t = b * s

    stacking = _to_stacking(weights, SHARDED)
    conv_shape = dict((name, shape) for name, shape, _ in _stacking(SHARDED))["conv_w"]
    w_in_local = jnp.pad(stacking["w_in"].astype(BF16), ((0, -W_IN_SHARD_ROWS % STACK_ALIGN), (0, 0)))
    conv_local = _pack_rows(stacking["conv_w"].reshape(-1), 8)
    w_in_shards, conv_all = _all_gather([w_in_local, conv_local], "w_in_all_gather")
    head_local = _stack(stacking, BF16, skip=("conv_w",), names=GATHER_LATER)
    in_flight = _gather_start(head_local, conv_all, "weights_gather_start")
    w_in_all = _w_in_from_shards(w_in_shards, "w_in_from_shards")
    w_sec = {name: w_in_all[DPROJ_COLS[name]:DPROJ_COLS[name] + _round_up(size, LANES)] for name, size in IN_SPLIT}
    conv_size = conv_shape[0] * conv_shape[1]
    conv_taps = conv_all.reshape(N_DEV, -1)[:, :conv_size].reshape(N_DEV * conv_shape[0], conv_shape[1]).T

    g_mix, g_ffn, g_fin = norm_mix.reshape(1, d), norm_ffn.reshape(1, d), norm_final.reshape(1, d)
    g_mix = g_mix + in_flight[4][:1, :1]
    bg_row = b_gate.reshape(1, 2 * d)
    convb_row = conv_b.reshape(1, SSM_CONV_DIM)
    ssmn_row = ssm_norm.reshape(1, SSM_D_INNER)
    dtb_row, alog_row = _lane_row(dt_bias), _lane_row(a_log)
    cosf, sinf = _rope_tables(s)

    x2d = x.reshape(t, d)
    h1 = _rmsnorm_fwd(x2d, g_mix, "norm_mix_fwd")
    proj = {name: _mm(h1, w_sec[name], mode="nt", name="in_proj_" + name) for name, _ in IN_SPLIT if name != "qkv"}
    xbc3 = proj["xbc"].reshape(b, s, SSM_CONV_DIM)
    xc = _conv_fwd(xbc3, conv_taps, convb_row, "conv_fwd")
    dtr3 = proj["dt"].reshape(b, s, DT_PAD)
    to_channels, to_heads = _head_masks()
    dskx = jnp.repeat(d_skip.reshape(-1).astype(F32), SSM_HEAD_DIM).reshape(1, SSM_D_INNER)
    y_ssd, h_states = _ssd_fwd(xc, dtr3, dtb_row, alog_row, dskx, to_channels, "ssd_fwd")
    y_ssd2 = y_ssd.reshape(t, SSM_D_INNER)
    ynorm = _gate_norm_fwd(y_ssd2, proj["z"], ssmn_row, "ssd_gate_norm_fwd")
    landed = _gather_wait(*in_flight[:4], ynorm, "weights_gather_wait")
    head_all = lax.dynamic_update_slice(landed, head_local[None], (_my_index(), 0, 0))
    full = {name: v.reshape((-1,) + v.shape[2:]) for name, v in _unstack(head_all, STACK_ORDER[:-2]).items()}
    y_ssm = _mm(ynorm, full["w_ssm_out"], mode="nn", name="ssm_out_proj")

    qk_parts = _qkv_proj_rope(h1, w_sec["qkv"], cosf, sinf, b, s, "in_proj_qkv_rope")
    att_parts = [_att_fwd(qk_parts[gi], "att_fwd_%d" % r) for gi, r in enumerate(ATT_DILATIONS)]
    att, *lse_parts = _att_merge([o for o, _ in att_parts], [l_ for _, l_ in att_parts], "att_merge")
    att2 = att.reshape(t, ATT_OUT_DIM)
    y_att = _mm(att2, full["w_att_out"], mode="nt", name="att_out_proj")

    mixed = _mix_fwd(proj["gate"], bg_row, y_ssm, y_att, "mix_fwd")
    x2, h2 = _proj_residual_norm(mixed, full["w_mix_out"], x2d, g_ffn, "mix_out_proj_norm")
    gt = _mm(h2, full["w_ffn_gate"], mode="nt", name="ffn_gate_proj")
    up, act = _up_proj_swiglu(h2, full["w_ffn_up"], gt, "ffn_up_proj_swiglu")

    loss_row, dx3, dg_fin, dx3b = _down_proj_loss_head(act, full["w_ffn_down"], x2, g_fin, loss_target.reshape(t, d),
                                                       "ffn_down_proj_loss_head")
    grads = {}
    grads["w_ffn_down"] = _mm(act, dx3b, mode="tn", name="ffn_down_dw", out_dtype=BF16)
    dgt, dup = _down_dx_swiglu_bwd(dx3b, full["w_ffn_down"], gt, up, "ffn_down_dx_swiglu_bwd")
    grads["w_ffn_gate"] = _mm(dgt, h2, mode="tn", name="ffn_gate_dw", out_dtype=BF16)
    grads["w_ffn_up"] = _mm(dup, h2, mode="tn", name="ffn_up_dw", out_dtype=BF16)
    dh2 = _mm(dgt, full["w_ffn_gate"], mode="nn", name="ffn_gate_dx")
    dx2, dg_ffn, dx2b = _proj_norm_bwd(dup, full["w_ffn_up"], x2, g_ffn, dx3, "ffn_up_dx_norm_bwd", add=dh2,
                                       with_bf16=True)

    dmixed = _mm(dx2b, full["w_mix_out"], mode="nt", name="mix_out_dx")
    grads["w_mix_out"] = _mm(mixed, dx2b, mode="tn", name="mix_out_dw", out_dtype=BF16)
    dys, dya, dproj, dbg = _mix_bwd(proj["gate"], bg_row, y_ssm, y_att, dmixed, "mix_bwd")

    grads["w_ssm_out"] = _mm(ynorm, dys, mode="tn", name="ssm_out_dw", out_dtype=BF16)
    early = _stack({name: grads[name].reshape((N_DEV, -1, STACK_WIDTH)) for name in REDUCE_EARLY}, BF16,
                   names=REDUCE_EARLY)
    early_flight = _gather_start(early, dys, "grads_scatter_start")
    ssmn_row = ssmn_row + early_flight[4][:1, :1]
    dy_ssd, dproj, dssmn = _ssm_out_dx_gate_norm_bwd(dys, full["w_ssm_out"], y_ssd2, proj["z"], ssmn_row, dproj,
                                                     "ssm_out_dx_gate_norm_bwd")
    dxc, dproj, dalog, ddsk, ddtb = _ssd_bwd(xc, dtr3, dy_ssd.reshape(b, s, SSM_D_INNER), h_states, dtb_row, alog_row,
                                             dskx, to_channels, to_heads, dproj.reshape(b, s, DPROJ_WIDTH), "ssd_bwd")
    dproj, dconvw, dconvb = _conv_bwd(xbc3, dxc, conv_taps, convb_row, dproj, "conv_bwd")
    grads["conv_w"] = dconvw.T.astype(BF16)

    grads["w_att_out"] = _mm(dya, att2, mode="tn", name="att_out_dw", out_dtype=BF16)
    datt = _mm(dya, full["w_att_out"], mode="nn", name="att_out_dx").reshape(b, s, ATT_OUT_DIM)
    do_parts, dl_parts = _att_delta(att, datt, "att_delta")
    dqs, dks, dvs = [], [], []
    for gi, r in enumerate(ATT_DILATIONS):
        operands = (qk_parts[gi], do_parts[gi], lse_parts[gi], dl_parts[gi])
        dqs.append(_att_bwd_q(*operands, "att_bwd_q_%d" % r))
        dk_g, dv_g = _att_bwd_kv(*operands, "att_bwd_kv_%d" % r)
        dks.append(dk_g)
        dvs.append(dv_g)
    dproj = _rope_bwd(dqs, dks, dvs, cosf, sinf, dproj, "rope_bwd").reshape(t, DPROJ_WIDTH)

    dw_all = _mm(dproj, h1, mode="tn", name="in_proj_dw", out_dtype=BF16)
    head = _stack({name: grads[name].reshape((N_DEV, -1, grads[name].shape[-1])) for name in REDUCE_LATE[:-1]}, BF16,
                  names=REDUCE_LATE[:-1])
    late = _w_in_to_shards(dw_all, head, "grad_stacks")
    late_flight = _gather_start(late, dw_all, "grads_late_scatter_start")
    grad_x, dg_mix = _proj_norm_bwd(dproj, w_in_all, x2d, g_mix, dx2, "in_proj_dx_norm_bwd", after=late_flight[4])

    small = {"norm_mix": dg_mix, "b_gate": dbg, "conv_b": dconvb, "dt_bias": ddtb[:, :SSM_N_HEADS],
             "a_log": dalog[:, :SSM_N_HEADS], "d_skip": ddsk[:, :SSM_N_HEADS], "ssm_norm": dssmn,
             "norm_ffn": dg_ffn, "norm_final": dg_fin}
    shared = _pack_replicated(small)
    shared = shared.at[LOSS_ROW, 0].set(loss_row[0, 0])
    got_small = _shared_exchange(shared, "shared_grads_exchange")

    def packed(prefix):
        vals = _to_stacking({name: given[prefix + name][0] for name, _, _ in SHARDED}, SHARDED)
        rep = {name: given[prefix + name] for name, _ in REPLICATED}
        return _stack(vals, F32), _pack_replicated(rep)

    (w_big, w_small), (m_big, m_small), (v_big, v_small) = packed(""), packed("m_"), packed("v_")
    me = _my_index().astype(jnp.int32).reshape(1)
    big_early = _adamw(_gather_wait(*early_flight[:4], got_small, "grads_scatter_wait"), w_big, m_big, v_big,
                       "adamw_early", own=(early, me))
    big_late = _adamw(_gather_wait(*late_flight[:4], got_small, "grads_late_scatter_wait"), w_big, m_big, v_big,
                      "adamw_late", row0=early.shape[1], own=(late, me))
    sml = _adamw(got_small, w_small, m_small, v_small, "adamw_replicated")

    outs = [sml[0][LOSS_ROW, 0], grad_x.reshape(b, s, d)]
    rep_shapes = {name: given[name].shape for name, _ in REPLICATED}
    order = ["norm_mix", "w_in", "b_gate", "conv_w", "conv_b", "dt_bias", "a_log", "d_skip", "ssm_norm", "w_ssm_out",
             "w_att_out", "w_mix_out", "norm_ffn", "w_ffn_gate", "w_ffn_up", "w_ffn_down", "norm_final"]
    for early_k, late_k, sml_k in zip(big_early, big_late, sml):
        stacks = dict(_unstack(early_k, REDUCE_EARLY), **_unstack(late_k, REDUCE_LATE))
        sharded = _to_stacking(stacks, SHARDED)
        rep = _unpack_replicated(sml_k, rep_shapes)
        for name in order:
            outs.append(sharded[name][None] if name in sharded else rep[name])
    return tuple(outs)
```

```python
import functools
import math

import jax
import jax.numpy as jnp
from jax import lax
from jax.experimental import pallas as pl
from jax.experimental.pallas import tpu as pltpu

F32 = jnp.float32
BF16 = jnp.bfloat16

N_DEV = 8
N_CHIPS = 4
D_MODEL = 1024
SSM_D_INNER = 2048
SSM_HEAD_DIM = 64
SSM_N_HEADS = 32
SSM_N_GROUPS = 4
SSM_HEADS_PER_GROUP = SSM_N_HEADS // SSM_N_GROUPS
SSM_D_STATE = 128
SSM_CONV = 4
SSM_CHUNK = 128
SSM_CONV_DIM = 3072
ATT_HEAD_DIM = 128
ATT_HEADS_PER_GROUP = 4
ATT_DILATIONS = (1, 4, 16)
ATT_N_HEADS = 12
ATT_QKV_DIM = 4608
ATT_OUT_DIM = 512
ATT_BLOCK = 128
ROPE_THETA = 10000.0
D_FF = 2816
IN_PROJ_DIM = 11808
EPS = 1e-6
LANES = 128
DT_PAD = LANES

DPROJ_COLS = {"qkv": 0, "xbc": 4608, "dt": 7680, "z": 8192, "gate": 10240}
DPROJ_DT_WIDTH = 512
DPROJ_WIDTH = 12288

ADAM_LR = 0.001
ADAM_B1 = 0.9
ADAM_B2 = 0.999
ADAM_EPS = 1e-08
ADAM_WD = 0.01
ADAM_STEP = 10

VMEM_LIMIT = 56 * 1024 * 1024
MESH = pl.DeviceIdType.MESH
NEG_INF = float("-inf")


def _tile_rows(n, cap, mult):
    return max(t for t in range(mult, min(n, cap) + 1, mult) if n % t == 0)


def _pick(n, candidates):
    for c in candidates:
        if n % c == 0:
            return c
    return n


def _params(*sem):
    return pltpu.CompilerParams(dimension_semantics=sem, vmem_limit_bytes=VMEM_LIMIT)


def _sigmoid(x):
    return 0.5 * jnp.tanh(0.5 * x) + 0.5


def _softplus(x):
    return jnp.maximum(x, 0.0) + jnp.log(1.0 + jnp.exp(-jnp.abs(x)))


def _dot(a, b, dims):
    return lax.dot_general(a.astype(BF16), b.astype(BF16), (dims, ((), ())), preferred_element_type=F32)


def _nn(a, b):
    return _dot(a, b, ((1,), (0,)))


def _nt(a, b):
    return _dot(a, b, ((1,), (1,)))


def _tn(a, b):
    return _dot(a, b, ((0,), (0,)))


def _split3(v):
    hi = v.astype(BF16)
    r1 = v - hi.astype(F32)
    mid = r1.astype(BF16)
    lo = (r1 - mid.astype(F32)).astype(BF16)
    return hi, mid, lo


def _mask_nn(mask, v):
    mb = mask.astype(BF16)
    hi, mid, lo = _split3(v)
    return _nn(mb, hi) + (_nn(mb, mid) + _nn(mb, lo))


MM_VMEM_BUDGET = 40 * 1024 * 1024
MM_FULL_K = 2816


def _mm_tiles(m, n, k, a_bytes, b_bytes, o_bytes, has_add):
    tk = k if k <= MM_FULL_K else _pick(k, (2048, 1024, 512, 256, 128))
    tn = 1408 if (n > 1024 and n % 1408 == 0) else _pick(n, (1024, 768, 512, 384, 256, 128))
    for tm in (1408, 1024, 768, 512, 384, 256, 128):
        if m % tm:
            continue
        buffers = 2 * (tm * tk * a_bytes + tk * tn * b_bytes + tm * tn * (o_bytes + (4 if has_add else 0)))
        if tk < k:
            buffers += tm * tn * 4
        if buffers <= MM_VMEM_BUDGET:
            return tm, tn, tk
    return _pick(m, (128,)), tn, tk


def _mm(a, b, *, mode, name, out_dtype=F32, add=None, after=None):
    if mode == "nn":
        (m, k), n = a.shape, b.shape[1]
    elif mode == "nt":
        (m, k), n = a.shape, b.shape[0]
    else:
        (k, m), n = a.shape, b.shape[1]
    has_add = add is not None
    tm, tn, tk = _mm_tiles(m, n, k, a.dtype.itemsize, b.dtype.itemsize, jnp.dtype(out_dtype).itemsize, has_add)
    nk = k // tk
    dims = {"nn": ((1,), (0,)), "nt": ((1,), (1,)), "tn": ((0,), (0,))}[mode]
    a_spec = {"nn": pl.BlockSpec((tm, tk), lambda i, j, kk: (i, kk)),
              "nt": pl.BlockSpec((tm, tk), lambda i, j, kk: (i, kk)),
              "tn": pl.BlockSpec((tk, tm), lambda i, j, kk: (kk, i))}[mode]
    b_spec = {"nn": pl.BlockSpec((tk, tn), lambda i, j, kk: (kk, j)),
              "nt": pl.BlockSpec((tn, tk), lambda i, j, kk: (j, kk)),
              "tn": pl.BlockSpec((tk, tn), lambda i, j, kk: (kk, j))}[mode]
    o_spec = pl.BlockSpec((tm, tn), lambda i, j, kk: (i, j))

    def finish(r, c_ref, o_ref):
        if has_add:
            r = r + c_ref[...]
        o_ref[...] = r.astype(out_dtype)

    def body_one(*refs):
        a_ref, b_ref = refs[:2]
        finish(_dot(a_ref[...], b_ref[...], dims), refs[2] if has_add else None, refs[-1])

    def body_acc(*refs):
        a_ref, b_ref = refs[:2]
        o_ref, acc = refs[-2:]
        kk = pl.program_id(2)

        @pl.when(kk == 0)
        def _():
            acc[...] = jnp.zeros_like(acc)

        acc[...] += _dot(a_ref[...], b_ref[...], dims)

        @pl.when(kk == nk - 1)
        def _():
            finish(acc[...], refs[2] if has_add else None, o_ref)

    in_specs = [a_spec, b_spec] + ([o_spec] if has_add else [])
    args = (a, b) + ((add,) if has_add else ())
    if after is not None:
        in_specs, args = in_specs + [pl.BlockSpec(memory_space=pl.ANY)], args + (after,)
    return pl.pallas_call(
        body_one if nk == 1 else body_acc, name=name, grid=(m // tm, n // tn, nk),
        in_specs=in_specs, out_specs=o_spec,
        out_shape=jax.ShapeDtypeStruct((m, n), out_dtype),
        scratch_shapes=[] if nk == 1 else [pltpu.VMEM((tm, tn), F32)],
        compiler_params=_params("parallel", "parallel", "arbitrary"),
    )(*args)


def _rmsnorm_fwd(x, g, name):
    t, d = x.shape
    tm = _pick(t, (512, 256, 128))

    def body(x_ref, g_ref, o_ref):
        xv = x_ref[...]
        r = lax.rsqrt(jnp.mean(xv * xv, axis=-1, keepdims=True) + EPS)
        o_ref[...] = ((xv * r) * g_ref[...]).astype(BF16)

    return pl.pallas_call(
        body, name=name, grid=(t // tm,),
        in_specs=[pl.BlockSpec((tm, d), lambda i: (i, 0)), pl.BlockSpec((1, d), lambda i: (0, 0))],
        out_specs=pl.BlockSpec((tm, d), lambda i: (i, 0)),
        out_shape=jax.ShapeDtypeStruct((t, d), BF16),
        compiler_params=_params("parallel"),
    )(x, g)


def _proj_residual_norm(a, w, res, g, name):
    t, k = a.shape
    d = w.shape[1]
    tm, _, _ = _mm_tiles(t, d, k, a.dtype.itemsize, w.dtype.itemsize, 4 + 2, True)

    def body(a_ref, w_ref, r_ref, g_ref, x_ref, h_ref):
        xv = r_ref[...] + _nn(a_ref[...], w_ref[...])
        x_ref[...] = xv
        r = lax.rsqrt(jnp.mean(xv * xv, axis=-1, keepdims=True) + EPS)
        h_ref[...] = ((xv * r) * g_ref[...]).astype(BF16)

    row = pl.BlockSpec((tm, d), lambda i: (i, 0))
    return pl.pallas_call(
        body, name=name, grid=(t // tm,),
        in_specs=[pl.BlockSpec((tm, k), lambda i: (i, 0)), pl.BlockSpec((k, d), lambda i: (0, 0)), row,
                  pl.BlockSpec((1, d), lambda i: (0, 0))],
        out_specs=[row, row],
        out_shape=[jax.ShapeDtypeStruct((t, d), F32), jax.ShapeDtypeStruct((t, d), BF16)],
        compiler_params=_params("parallel"),
    )(a, w, res, g)


def _proj_norm_bwd(a, w, x, g, dres, name, add=None, with_bf16=False, after=None):
    t, k = a.shape
    d = w.shape[1]
    has_add = add is not None
    tm, _, tk = _mm_tiles(t, d, k, a.dtype.itemsize, w.dtype.itemsize, 4 + 4 + 4 + (2 if with_bf16 else 0), has_add)
    if tk == k:
        tm = min(tm, 512)
    else:
        tm, tk = _pick(t, (1024, 512, 256, 128)), min(tk, 1024)
    nk = k // tk

    def body(*refs):
        a_ref, w_ref, x_ref, g_ref, dres_ref = refs[:5]
        rest = refs[5 + has_add + (after is not None):]
        dx_ref, dg_ref = rest[:2]
        i, kk = pl.program_id(0), pl.program_id(1)

        @pl.when(jnp.logical_and(i == 0, kk == 0))
        def _():
            dg_ref[...] = jnp.zeros_like(dg_ref)

        part = _nn(a_ref[...], w_ref[...])
        if nk > 1:
            acc = rest[-1]

            @pl.when(kk == 0)
            def _():
                acc[...] = jnp.zeros_like(acc)

            acc[...] += part

        @pl.when(kk == nk - 1)
        def _():
            dhv = part if nk == 1 else acc[...]
            if has_add:
                dhv = dhv + refs[5][...]
            xv = x_ref[...]
            r = lax.rsqrt(jnp.mean(xv * xv, axis=-1, keepdims=True) + EPS)
            xhat = xv * r
            dyg = dhv * g_ref[...]
            dx = dres_ref[...] + r * (dyg - xhat * jnp.mean(dyg * xhat, axis=-1, keepdims=True))
            dx_ref[...] = dx
            if with_bf16:
                rest[2][...] = dx.astype(BF16)
            dg_ref[...] += jnp.sum(dhv * xhat, axis=0, keepdims=True)

    row = pl.BlockSpec((tm, d), lambda i, kk: (i, 0))
    vec = pl.BlockSpec((1, d), lambda i, kk: (0, 0))
    in_specs = [pl.BlockSpec((tm, tk), lambda i, kk: (i, kk)), pl.BlockSpec((tk, d), lambda i, kk: (kk, 0)),
                row, vec, row] + has_add * [row]
    args = (a, w, x, g, dres) + has_add * (add,)
    if after is not None:
        in_specs, args = in_specs + [pl.BlockSpec(memory_space=pl.ANY)], args + (after,)
    return pl.pallas_call(
        body, name=name, grid=(t // tm, nk), in_specs=in_specs, out_specs=[row, vec] + with_bf16 * [row],
        out_shape=[jax.ShapeDtypeStruct((t, d), F32), jax.ShapeDtypeStruct((1, d), F32)]
        + with_bf16 * [jax.ShapeDtypeStruct((t, d), BF16)],
        scratch_shapes=[] if nk == 1 else [pltpu.VMEM((tm, d), F32)],
        compiler_params=_params("arbitrary", "arbitrary"),
    )(*args)


def _down_proj_loss_head(act, w_down, res, g, target, name):
    t, k = act.shape
    d = w_down.shape[1]
    tm, _, _ = _mm_tiles(t, d, k, act.dtype.itemsize, w_down.dtype.itemsize, 4 + 2, True)
    tm = min(tm, 512)

    def body(a_ref, w_ref, r_ref, g_ref, t_ref, loss_ref, dx_ref, dg_ref, dxb_ref):
        @pl.when(pl.program_id(0) == 0)
        def _():
            dg_ref[...] = jnp.zeros_like(dg_ref)
            loss_ref[...] = jnp.zeros_like(loss_ref)

        xv = r_ref[...] + _nn(a_ref[...], w_ref[...])
        gv = g_ref[...]
        r = lax.rsqrt(jnp.mean(xv * xv, axis=-1, keepdims=True) + EPS)
        xhat = xv * r
        err = xhat * gv - t_ref[...]
        loss_ref[...] += jnp.sum(err * err) * (0.5 / d)
        dy = err * (1.0 / d)
        dyg = dy * gv
        dx = r * (dyg - xhat * jnp.mean(dyg * xhat, axis=-1, keepdims=True))
        dx_ref[...] = dx
        dxb_ref[...] = dx.astype(BF16)
        dg_ref[...] += jnp.sum(dy * xhat, axis=0, keepdims=True)

    row = pl.BlockSpec((tm, d), lambda i: (i, 0))
    vec = pl.BlockSpec((1, d), lambda i: (0, 0))
    return pl.pallas_call(
        body, name=name, grid=(t // tm,),
        in_specs=[pl.BlockSpec((tm, k), lambda i: (i, 0)), pl.BlockSpec((k, d), lambda i: (0, 0)), row, vec, row],
        out_specs=[pl.BlockSpec((1, LANES), lambda i: (0, 0)), row, vec, row],
        out_shape=[jax.ShapeDtypeStruct((1, LANES), F32), jax.ShapeDtypeStruct((t, d), F32),
                   jax.ShapeDtypeStruct((1, d), F32), jax.ShapeDtypeStruct((t, d), BF16)],
        compiler_params=_params("arbitrary"),
    )(act, w_down, res, g, target)


CONV_HALO = 8
CONV_ROWS = 64


def _conv_taps(window, wv, bv):
    acc = bv + wv[SSM_CONV - 1:SSM_CONV, :] * window(0)
    for sh in range(1, SSM_CONV):
        kidx = SSM_CONV - 1 - sh
        acc = acc + wv[kidx:kidx + 1, :] * window(sh)
    return acc


def _conv_fwd(u, w, bias, name):
    b, s, c = u.shape
    rows = CONV_ROWS

    def body(u_ref, w_ref, b_ref, o_ref, ext):
        ext[0:CONV_HALO, :] = jnp.zeros((CONV_HALO, LANES), F32)
        ext[CONV_HALO:, :] = u_ref[...]
        wv, bv = w_ref[...], b_ref[...]
        for r0 in range(0, s, rows):
            acc = _conv_taps(lambda sh: ext[CONV_HALO + r0 - sh:CONV_HALO + r0 - sh + rows, :], wv, bv)
            o_ref[r0:r0 + rows, :] = acc * _sigmoid(acc)

    strip = pl.BlockSpec((None, s, LANES), lambda bi, j: (bi, 0, j))
    return pl.pallas_call(
        body, name=name, grid=(b, c // LANES),
        in_specs=[strip, pl.BlockSpec((SSM_CONV, LANES), lambda bi, j: (0, j)),
                  pl.BlockSpec((1, LANES), lambda bi, j: (0, j))],
        out_specs=strip, out_shape=jax.ShapeDtypeStruct((b, s, c), F32),
        scratch_shapes=[pltpu.VMEM((CONV_HALO + s, LANES), F32)],
        compiler_params=_params("parallel", "parallel"),
    )(u, w, bias)


def _conv_bwd(u, dout, w, bias, dproj, name):
    b, s, c = u.shape
    rows = CONV_ROWS

    def fold(v):
        return jnp.sum(v.reshape(rows // CONV_HALO, CONV_HALO, LANES), axis=0)

    def body(u_ref, d_ref, w_ref, b_ref, buf_ref, du_ref, dw_ref, db_ref, ext, dpre):
        @pl.when(pl.program_id(1) == 0)
        def _():
            dw_ref[...] = jnp.zeros_like(dw_ref)
            db_ref[...] = jnp.zeros_like(db_ref)

        ext[0:CONV_HALO, :] = jnp.zeros((CONV_HALO, LANES), F32)
        ext[CONV_HALO:, :] = u_ref[...]
        dpre[s:, :] = jnp.zeros((CONV_HALO, LANES), F32)
        wv, bv = w_ref[...], b_ref[...]
        sums = [jnp.zeros((CONV_HALO, LANES), F32)] * (SSM_CONV + 1)
        for r0 in range(0, s, rows):
            window = lambda sh: ext[CONV_HALO + r0 - sh:CONV_HALO + r0 - sh + rows, :]
            acc = _conv_taps(window, wv, bv)
            sg = _sigmoid(acc)
            dp = d_ref[r0:r0 + rows, :] * (sg * (1.0 + acc * (1.0 - sg)))
            dpre[r0:r0 + rows, :] = dp
            taps = [sums[SSM_CONV - 1 - sh] + fold(dp * window(sh)) for sh in range(SSM_CONV)]
            sums = taps[::-1] + [sums[SSM_CONV] + fold(dp)]
        for r0 in range(0, s, rows):
            du = wv[SSM_CONV - 1:SSM_CONV, :] * dpre[r0:r0 + rows, :]
            for sh in range(1, SSM_CONV):
                kidx = SSM_CONV - 1 - sh
                du = du + wv[kidx:kidx + 1, :] * dpre[r0 + sh:r0 + sh + rows, :]
            du_ref[r0:r0 + rows, :] = du.astype(BF16)
        for kidx in range(SSM_CONV):
            dw_ref[kidx:kidx + 1, :] += jnp.sum(sums[kidx], axis=0, keepdims=True)
        db_ref[...] += jnp.sum(sums[SSM_CONV], axis=0, keepdims=True)

    strip = pl.BlockSpec((None, s, LANES), lambda j, bi: (bi, 0, j))
    taps = pl.BlockSpec((SSM_CONV, LANES), lambda j, bi: (0, j))
    vec = pl.BlockSpec((1, LANES), lambda j, bi: (0, j))
    du_cols = pl.BlockSpec((None, s, LANES), lambda j, bi: (bi, 0, DPROJ_COLS["xbc"] // LANES + j))
    return pl.pallas_call(
        body, name=name, grid=(c // LANES, b),
        in_specs=[strip, strip, taps, vec, pl.BlockSpec(memory_space=pl.ANY)], out_specs=[du_cols, taps, vec],
        input_output_aliases={4: 0},
        out_shape=[jax.ShapeDtypeStruct(dproj.shape, dproj.dtype), jax.ShapeDtypeStruct((SSM_CONV, c), F32),
                   jax.ShapeDtypeStruct((1, c), F32)],
        scratch_shapes=[pltpu.VMEM((CONV_HALO + s, LANES), F32), pltpu.VMEM((s + CONV_HALO, LANES), F32)],
        compiler_params=_params("parallel", "arbitrary"),
    )(u, dout, w, bias, dproj)


def _ssd_chunk_terms(dtr_ref, bias_ref, alog_ref):
    q = SSM_CHUNK
    dt = _softplus(dtr_ref[...] + bias_ref[...])
    a_neg = -jnp.exp(alog_ref[...])
    row = lax.broadcasted_iota(jnp.int32, (q, q), 0)
    col = lax.broadcasted_iota(jnp.int32, (q, q), 1)
    lower = row >= col
    s = _mask_nn(lower, dt * a_neg)
    return dt, a_neg, s, s.T, lower


def _head_masks():
    heads = jnp.arange(LANES)[:, None]
    chans = jnp.arange(SSM_D_INNER)[None, :]
    to_channels = (chans // SSM_HEAD_DIM == heads).astype(BF16)
    return to_channels, to_channels.T


def _per_channel(v, to_channels):
    hi = v.astype(BF16)
    lo = (v - hi.astype(F32)).astype(BF16)
    return _nn(hi, to_channels) + _nn(lo, to_channels)


def _per_head(v, to_heads):
    hi = v.astype(BF16)
    lo = (v - hi.astype(F32)).astype(BF16)
    return _nn(hi, to_heads) + _nn(lo, to_heads)


def _decay_terms_per_channel(dt, s_col, to_channels):
    q = SSM_CHUNK
    tot = s_col[q - 1:q, :]
    stacked = jnp.concatenate([dt, jnp.exp(s_col), jnp.exp(tot - s_col)], axis=0)
    wide = _per_channel(stacked, to_channels)
    dtx, esx, decx = wide[:q], wide[q:2 * q], wide[2 * q:]
    return dtx, esx, decx, esx[0:1, :] * decx[0:1, :]


SSM_PAIRS_PER_GROUP = SSM_HEADS_PER_GROUP // 2
SSM_GROUP_CHANNELS = SSM_HEADS_PER_GROUP * SSM_HEAD_DIM


def _split_pair(v):
    first = lax.broadcasted_iota(jnp.int32, v.shape, 1) < SSM_HEAD_DIM
    return jnp.concatenate([jnp.where(first, v, 0.0), jnp.where(first, 0.0, v)], axis=0)


def _ssd_fwd(xc, dtr, dt_bias, a_log, dskx, to_channels, name):
    b, s, _ = xc.shape
    q = SSM_CHUNK
    nc = s // q
    n, gc = SSM_D_STATE, SSM_GROUP_CHANNELS

    def body(xc_ref, dtr_ref, bias_ref, alog_ref, dsk_ref, tc_ref, y_ref, hs_ref, h_scr):
        @pl.when(pl.program_id(1) == 0)
        def _():
            h_scr[...] = jnp.zeros_like(h_scr)

        dt, _, s_col, s_row, lower = _ssd_chunk_terms(dtr_ref, bias_ref, alog_ref)
        dtx, esx, decx, etotx = _decay_terms_per_channel(dt, s_col, tc_ref[...])
        x = xc_ref[:, :SSM_D_INNER]
        xdt = x * dtx
        xdec = xdt * decx
        skip = dsk_ref[...] * x
        for g in range(SSM_N_GROUPS):
            bg = xc_ref[:, SSM_D_INNER + n * g:SSM_D_INNER + n * (g + 1)].astype(BF16)
            cg = xc_ref[:, SSM_D_INNER + n * (SSM_N_GROUPS + g):SSM_D_INNER + n * (SSM_N_GROUPS + g + 1)].astype(BF16)
            gsl = slice(gc * g, gc * (g + 1))
            gm = _nt(cg, bg)
            hgt = h_scr[:, gsl]
            hs_ref[:, gsl] = hgt
            y_off = esx[:, gsl] * _nn(cg, hgt)
            h_scr[:, gsl] = etotx[:, gsl] * hgt + _tn(bg, xdec[:, gsl])
            for k in range(SSM_PAIRS_PER_GROUP):
                h0 = g * SSM_HEADS_PER_GROUP + 2 * k
                lo = gc * g + LANES * k
                ms = []
                for h in (h0, h0 + 1):
                    lm = jnp.exp(jnp.where(lower, s_col[:, h:h + 1] - s_row[h:h + 1, :], NEG_INF))
                    ms.append((gm * lm).astype(BF16))
                y_diag = _nn(jnp.concatenate(ms, axis=1), _split_pair(xdt[:, lo:lo + LANES]))
                y_ref[:, lo:lo + LANES] = y_diag + y_off[:, LANES * k:LANES * (k + 1)] + skip[:, lo:lo + LANES]

    vec = pl.BlockSpec((1, LANES), lambda bi, c: (0, 0))
    return pl.pallas_call(
        body, name=name, grid=(b, nc),
        in_specs=[pl.BlockSpec((None, q, SSM_CONV_DIM), lambda bi, c: (bi, c, 0)),
                  pl.BlockSpec((None, q, LANES), lambda bi, c: (bi, c, 0)), vec, vec,
                  pl.BlockSpec((1, SSM_D_INNER), lambda bi, c: (0, 0)),
                  pl.BlockSpec((LANES, SSM_D_INNER), lambda bi, c: (0, 0))],
        out_specs=[pl.BlockSpec((None, q, SSM_D_INNER), lambda bi, c: (bi, c, 0)),
                   pl.BlockSpec((None, None, n, SSM_D_INNER), lambda bi, c: (bi, c, 0, 0))],
        out_shape=[jax.ShapeDtypeStruct((b, s, SSM_D_INNER), F32),
                   jax.ShapeDtypeStruct((b, nc, n, SSM_D_INNER), F32)],
        scratch_shapes=[pltpu.VMEM((n, SSM_D_INNER), F32)],
        compiler_params=_params("parallel", "arbitrary"),
    )(xc, dtr, dt_bias, a_log, dskx, to_channels)


def _ssd_bwd(xc, dtr, dy, hs, dt_bias, a_log, dskx, to_channels, to_heads, dproj, name):
    b, s, _ = xc.shape
    q = SSM_CHUNK
    nc = s // q
    n, gc = SSM_D_STATE, SSM_GROUP_CHANNELS

    def colsum(v):
        return jnp.sum(v, axis=0, keepdims=True)

    def body(xc_ref, dtr_ref, dy_ref, hs_ref, bias_ref, alog_ref, dsk_ref, tc_ref, th_ref, buf_ref,
             dxc_ref, ddtr_ref, dalog_ref, ddsk_ref, dbias_ref, dh_scr, dxs_scr, dxd_scr, w_scr, dst_scr, rows_scr):
        ci = pl.program_id(1)

        @pl.when(ci == 0)
        def _():
            dh_scr[...] = jnp.zeros_like(dh_scr)

        @pl.when(jnp.logical_and(pl.program_id(0) == 0, ci == 0))
        def _():
            dalog_ref[...] = jnp.zeros_like(dalog_ref)
            ddsk_ref[...] = jnp.zeros_like(ddsk_ref)
            dbias_ref[...] = jnp.zeros_like(dbias_ref)
            dst_scr[...] = jnp.zeros_like(dst_scr)

        dt, a_neg, s_col, s_row, lower = _ssd_chunk_terms(dtr_ref, bias_ref, alog_ref)
        upper = jnp.logical_not(lower) | (lax.broadcasted_iota(jnp.int32, (q, q), 0)
                                          == lax.broadcasted_iota(jnp.int32, (q, q), 1))
        dtx, esx, decx, etotx = _decay_terms_per_channel(dt, s_col, tc_ref[...])
        x = xc_ref[:, :SSM_D_INNER]
        dyv = dy_ref[...]
        xdt = x * dtx
        xdec = xdt * decx
        dw = esx * dyv
        rows_scr[...] = jnp.zeros_like(rows_scr)
        for g in range(SSM_N_GROUPS):
            b_lo = SSM_D_INNER + n * g
            c_lo = SSM_D_INNER + n * (SSM_N_GROUPS + g)
            bg = xc_ref[:, b_lo:b_lo + n].astype(BF16)
            cg = xc_ref[:, c_lo:c_lo + n].astype(BF16)
            gsl = slice(gc * g, gc * (g + 1))
            gm = _nt(cg, bg)
            gmt = _nt(bg, cg)
            hgt = hs_ref[:, gsl]
            dhgt = dh_scr[:, gsl]
            w_scr[:, gsl] = _nn(cg, hgt)
            dcg = _nt(dw[:, gsl], hgt)
            dxs = decx[:, gsl] * _nn(bg, dhgt)
            dxs_scr[:, gsl] = dxs
            dbg = _nt(xdec[:, gsl], dhgt)
            rows_scr[2:3, gsl] = colsum(dhgt * hgt)
            dh_scr[:, gsl] = _tn(cg, dw[:, gsl]) + etotx[:, gsl] * dhgt
            dg = jnp.zeros((q, q), F32)
            dgt = jnp.zeros((q, q), F32)
            for k in range(SSM_PAIRS_PER_GROUP):
                h0 = g * SSM_HEADS_PER_GROUP + 2 * k
                lo = gc * g + LANES * k
                xp = xdt[:, lo:lo + LANES]
                dyp = dyv[:, lo:lo + LANES]
                dy2 = _split_pair(dyp)
                dm2 = _nt(dy2, xp)
                dmt2 = _nt(_split_pair(xp), dyp)
                mts = []
                for i, h in enumerate((h0, h0 + 1)):
                    lm = jnp.exp(jnp.where(lower, s_col[:, h:h + 1] - s_row[h:h + 1, :], NEG_INF))
                    lmt = jnp.exp(jnp.where(upper, s_row[h:h + 1, :] - s_col[:, h:h + 1], NEG_INF))
                    dm = dm2[q * i:q * (i + 1), :]
                    dmt = dmt2[q * i:q * (i + 1), :]
                    dg = dg + dm * lm
                    dgt = dgt + dmt * lmt
                    mt = gmt * lmt
                    dst_scr[h:h + 1, :] = colsum(dmt * mt) - colsum(dm * (gm * lm))
                    mts.append(mt.astype(BF16))
                dxd_scr[:, lo:lo + LANES] = _nn(jnp.concatenate(mts, axis=1), dy2)
            dxc_ref[:, b_lo:b_lo + n] = dbg + _nn(dgt, cg)
            dxc_ref[:, c_lo:c_lo + n] = dcg + _nn(dg, bg)
        dxs = dxs_scr[...]
        dxdt = dxd_scr[...] + dxs
        dxc_ref[:, :SSM_D_INNER] = dxdt * dtx + dsk_ref[...] * dyv
        state_part = xdt * dxs
        rows_scr[0:1, :] = colsum(dyv * x)
        rows_scr[1:2, :] = colsum(state_part)
        th = th_ref[...]
        per_head = _per_head(jnp.concatenate([dw * w_scr[...] - state_part, dxdt * x], axis=0), th)
        r_ds, r_dt = per_head[:q], per_head[q:]
        sums = _per_head(rows_scr[...], th)
        etot = jnp.exp(s_col[q - 1:q, :])
        dtot = sums[1:2, :] + etot * sums[2:3, :]
        last = lax.broadcasted_iota(jnp.int32, (q, LANES), 0) == q - 1
        ds = dst_scr[...].T + r_ds + jnp.where(last, dtot, 0.0)
        da = _mask_nn(upper, ds)
        ddt = da * a_neg + r_dt
        live = lax.broadcasted_iota(jnp.int32, (1, LANES), 1) < SSM_N_HEADS
        sg = _sigmoid(dtr_ref[...] + bias_ref[...])
        ddtr = jnp.where(live, ddt * sg, 0.0)
        ddtr_ref[:, :LANES] = ddtr.astype(BF16)
        ddtr_ref[:, LANES:] = jnp.zeros((q, DPROJ_DT_WIDTH - LANES), BF16)
        dalog_ref[...] += jnp.where(live, colsum(da * dt) * a_neg, 0.0)
        ddsk_ref[...] += jnp.where(live, sums[0:1, :], 0.0)
        dbias_ref[...] += colsum(ddtr)

    rev = lambda bi, c: (bi, nc - 1 - c, 0)
    vec = pl.BlockSpec((1, LANES), lambda bi, c: (0, 0))
    wide = pl.BlockSpec((None, q, SSM_D_INNER), rev)
    return pl.pallas_call(
        body, name=name, grid=(b, nc),
        in_specs=[pl.BlockSpec((None, q, SSM_CONV_DIM), rev), pl.BlockSpec((None, q, LANES), rev), wide,
                  pl.BlockSpec((None, None, n, SSM_D_INNER), lambda bi, c: (bi, nc - 1 - c, 0, 0)),
                  vec, vec, pl.BlockSpec((1, SSM_D_INNER), lambda bi, c: (0, 0)),
                  pl.BlockSpec((LANES, SSM_D_INNER), lambda bi, c: (0, 0)),
                  pl.BlockSpec((SSM_D_INNER, LANES), lambda bi, c: (0, 0)),
                  pl.BlockSpec(memory_space=pl.ANY)],
        out_specs=[pl.BlockSpec((None, q, SSM_CONV_DIM), rev),
                   pl.BlockSpec((None, q, DPROJ_DT_WIDTH),
                                lambda bi, c: (bi, nc - 1 - c, DPROJ_COLS["dt"] // DPROJ_DT_WIDTH)), vec, vec, vec],
        input_output_aliases={9: 1},
        out_shape=[jax.ShapeDtypeStruct((b, s, SSM_CONV_DIM), F32), jax.ShapeDtypeStruct(dproj.shape, dproj.dtype),
                   jax.ShapeDtypeStruct((1, LANES), F32), jax.ShapeDtypeStruct((1, LANES), F32),
                   jax.ShapeDtypeStruct((1, LANES), F32)],
        scratch_shapes=[pltpu.VMEM((n, SSM_D_INNER), F32)] + [pltpu.VMEM((q, SSM_D_INNER), F32)] * 3
        + [pltpu.VMEM((LANES, q), F32), pltpu.VMEM((8, SSM_D_INNER), F32)],
        compiler_params=_params("arbitrary", "arbitrary"),
    )(xc, dtr, dy, hs, dt_bias, a_log, dskx, to_channels, to_heads, dproj)


SSM_GROUP_WIDTH = SSM_D_INNER // SSM_N_GROUPS


def _gate_norm_fwd(y, z, w, name):
    t, d = y.shape
    tm = _pick(t, (256, 128))

    def body(y_ref, z_ref, w_ref, o_ref):
        for g in range(SSM_N_GROUPS):
            sl = slice(SSM_GROUP_WIDTH * g, SSM_GROUP_WIDTH * (g + 1))
            zv = z_ref[:, sl]
            u = y_ref[:, sl] * (zv * _sigmoid(zv))
            r = lax.rsqrt(jnp.mean(u * u, axis=-1, keepdims=True) + EPS)
            o_ref[:, sl] = ((u * r) * w_ref[:, sl]).astype(BF16)

    row = pl.BlockSpec((tm, d), lambda i: (i, 0))
    return pl.pallas_call(
        body, name=name, grid=(t // tm,),
        in_specs=[row, row, pl.BlockSpec((1, d), lambda i: (0, 0))], out_specs=row,
        out_shape=jax.ShapeDtypeStruct((t, d), BF16),
        compiler_params=_params("parallel"),
    )(y, z, w)


def _ssm_out_dx_gate_norm_bwd(dys, w_ssm_out, y, z, w, dproj, name):
    t, d = y.shape
    k = dys.shape[1]
    gw = SSM_GROUP_WIDTH
    tm = _pick(t, (512, 256, 128))

    def body(dys_ref, ws_ref, y_ref, z_ref, w_ref, buf_ref, dy_ref, dz_ref, dw_ref):
        @pl.when(pl.program_id(0) == 0)
        def _():
            dw_ref[...] = jnp.zeros_like(dw_ref)

        dout = _nt(dys_ref[...], ws_ref[...])
        for g in range(SSM_N_GROUPS):
            sl = slice(gw * g, gw * (g + 1))
            zv = z_ref[:, sl]
            yv = y_ref[:, sl]
            sg = _sigmoid(zv)
            silu = zv * sg
            u = yv * silu
            r = lax.rsqrt(jnp.mean(u * u, axis=-1, keepdims=True) + EPS)
            uh = u * r
            dov = dout[:, sl]
            dw_ref[:, sl] += jnp.sum(dov * uh, axis=0, keepdims=True)
            dyg = dov * w_ref[:, sl]
            du = r * (dyg - uh * jnp.mean(dyg * uh, axis=-1, keepdims=True))
            dy_ref[:, sl] = du * silu
            dz_ref[:, sl] = (du * yv * (sg * (1.0 + zv * (1.0 - sg)))).astype(BF16)

    row = pl.BlockSpec((tm, d), lambda i: (i, 0))
    vec = pl.BlockSpec((1, d), lambda i: (0, 0))
    z_cols = pl.BlockSpec((tm, d), lambda i: (i, DPROJ_COLS["z"] // d))
    return pl.pallas_call(
        body, name=name, grid=(t // tm,),
        in_specs=[pl.BlockSpec((tm, k), lambda i: (i, 0)), pl.BlockSpec((d, k), lambda i: (0, 0)), row, row, vec,
                  pl.BlockSpec(memory_space=pl.ANY)],
        out_specs=[row, z_cols, vec],
        out_shape=[jax.ShapeDtypeStruct((t, d), F32), jax.ShapeDtypeStruct(dproj.shape, dproj.dtype),
                   jax.ShapeDtypeStruct((1, d), F32)],
        input_output_aliases={5: 1},
        compiler_params=_params("arbitrary"),
    )(dys, w_ssm_out, y, z, w, dproj)


def _rope_tables(s):
    half = ATT_HEAD_DIM // 2
    inv = ROPE_THETA ** (-jnp.arange(half, dtype=F32) / half)
    ang = jnp.arange(s).astype(F32)[:, None] * inv[None, :]
    cos, sin = jnp.cos(ang), jnp.sin(ang)
    return jnp.concatenate([cos, cos], axis=-1), jnp.concatenate([-sin, sin], axis=-1)


ATT_TILE = 256


def _by_residue_spec(r, width):
    return pl.BlockSpec((None, r, ATT_TILE // r, width), lambda bi, i: (bi, 0, i, 0))


def _to_residues(tile, stage, r, store):
    if r == 1:
        store(0, tile)
        return
    stage[...] = tile
    for ri in range(r):
        store(ri, stage[pl.ds(ri, tile.shape[0] // r, stride=r), :])


def _from_residues(load, stage, r):
    if r == 1:
        return load(0)
    for ri in range(r):
        stage[pl.ds(ri, ATT_TILE // r, stride=r), :] = load(ri)
    return stage[...]


QKV_ROWS = 1024
QKV_COLS = 768


def _qkv_proj_rope(h, w_qkv_t, cosf, sinf, b, s, name):
    t, k = h.shape
    tm, d, gw = QKV_ROWS, ATT_HEAD_DIM, ATT_OUT_DIM
    per_seq = s // tm

    def body(h_ref, w_ref, c_ref, s_ref, *rest):
        outs, stage = rest[:-1], rest[-1]
        cv, sv = c_ref[...], s_ref[...]
        hv = h_ref[...]
        for lo in range(0, ATT_QKV_DIM, QKV_COLS):
            acc = _nt(hv, w_ref[lo:lo + QKV_COLS, :])
            for hh in range(QKV_COLS // d):
                kind, head = divmod(lo // d + hh, ATT_N_HEADS)
                gi, j = divmod(head, ATT_HEADS_PER_GROUP)
                dst = slice(kind * gw + d * j, kind * gw + d * (j + 1))
                tv = acc[:, d * hh:d * (hh + 1)]
                if kind < 2:
                    tv = tv * cv + pltpu.roll(tv, d // 2, 1) * sv

                def store(ri, rows, o_ref=outs[gi], dst=dst):
                    o_ref[ri, :, dst] = rows.astype(BF16)

                _to_residues(tv, stage, ATT_DILATIONS[gi], store)

    tab = pl.BlockSpec((tm, d), lambda i: (i % per_seq, 0))
    return pl.pallas_call(
        body, name=name, grid=(t // tm,),
        in_specs=[pl.BlockSpec((tm, k), lambda i: (i, 0)), pl.BlockSpec((ATT_QKV_DIM, k), lambda i: (0, 0)), tab, tab],
        out_specs=[pl.BlockSpec((None, r, tm // r, 3 * gw), lambda i: (i // per_seq, 0, i % per_seq, 0))
                   for r in ATT_DILATIONS],
        out_shape=[jax.ShapeDtypeStruct((b, r, s // r, 3 * gw), BF16) for r in ATT_DILATIONS],
        scratch_shapes=[pltpu.VMEM((tm, d), F32)],
        compiler_params=_params("parallel"),
    )(h, w_qkv_t, cosf, sinf)


def _rope_bwd(dq, dk, dv, cosf, sinf, dproj, name):
    n_pat = len(ATT_DILATIONS)
    b, _, s, gw = dq[0].shape
    ts, d = ATT_TILE, ATT_HEAD_DIM

    def body(*refs):
        ins, (c_ref, s_ref, _, o_ref, stage) = refs[:3 * n_pat], refs[3 * n_pat:]
        cv, sv = c_ref[...], s_ref[...]
        for kind in range(3):
            for gi, r in enumerate(ATT_DILATIONS):
                src = ins[kind * n_pat + gi]
                for j in range(ATT_HEADS_PER_GROUP):
                    tv = _from_residues(lambda ri, src=src, j=j: src[ri, :, d * j:d * (j + 1)], stage, r)
                    if kind < 2:
                        tv = tv * cv + pltpu.roll(tv * sv, d // 2, 1)
                    lo = d * (kind * ATT_N_HEADS + gi * ATT_HEADS_PER_GROUP + j)
                    o_ref[:, lo:lo + d] = tv.astype(BF16)

    tab = pl.BlockSpec((ts, d), lambda bi, i: (i, 0))
    parts = [_by_residue_spec(r, gw) for r in ATT_DILATIONS]
    return pl.pallas_call(
        body, name=name, grid=(b, s // ts), in_specs=parts * 3 + [tab, tab, pl.BlockSpec(memory_space=pl.ANY)],
        out_specs=pl.BlockSpec((None, ts, ATT_QKV_DIM), lambda bi, i: (bi, i, DPROJ_COLS["qkv"] // ATT_QKV_DIM)),
        out_shape=jax.ShapeDtypeStruct(dproj.shape, dproj.dtype),
        input_output_aliases={3 * n_pat + 2: 0},
        scratch_shapes=[pltpu.VMEM((ts, d), F32)],
        compiler_params=_params("parallel", "parallel"),
    )(*dq, *dk, *dv, cosf, sinf, dproj)


ATT_SCALE = ATT_HEAD_DIM ** -0.5
ATT_STEP = 2 * ATT_BLOCK


def _att_spec(col):
    return pl.BlockSpec((None, None, ATT_STEP, ATT_OUT_DIM), lambda bi, ri, i: (bi, ri, i, col))


def _att_edge_spec(col, side, n_steps):
    def index(bi, ri, i):
        blk = 2 * i - 1 if side < 0 else 2 * i + 2
        return (bi, ri, jnp.clip(blk, 0, 2 * n_steps - 1), col)
    return pl.BlockSpec((None, None, ATT_BLOCK, ATT_OUT_DIM), index)


def _band_mask(shape, q_axis, has_prev):
    qi = lax.broadcasted_iota(jnp.int32, shape, q_axis)
    kj = lax.broadcasted_iota(jnp.int32, shape, 1 - q_axis)
    dist = qi + ATT_BLOCK - kj
    return (dist >= 0) & (dist <= ATT_BLOCK) & (has_prev | (kj >= ATT_BLOCK))


def _att_fwd(qkr, name):
    b, r, l, _ = qkr.shape
    nb = l // ATT_STEP
    d = ATT_HEAD_DIM

    def body(q_ref, kp_ref, k_ref, vp_ref, v_ref, o_ref, lse_ref):
        mask = _band_mask((ATT_STEP, ATT_BLOCK + ATT_STEP), 0, pl.program_id(2) > 0)
        heads = [slice(d * j, d * (j + 1)) for j in range(ATT_HEADS_PER_GROUP)]
        scores = [_nt(q_ref[:, sl], jnp.concatenate([kp_ref[:, sl], k_ref[:, sl]], axis=0)) for sl in heads]
        scores = [jnp.where(mask, sc * ATT_SCALE, NEG_INF) for sc in scores]
        tops = [jnp.max(sc, axis=-1, keepdims=True) for sc in scores]
        probs = [jnp.exp(sc - m) for sc, m in zip(scores, tops)]
        dens = [jnp.sum(pr, axis=-1, keepdims=True) for pr in probs]
        for sl, m, pr, den in zip(heads, tops, probs, dens):
            o_ref[:, sl] = _nn(pr / den, jnp.concatenate([vp_ref[:, sl], v_ref[:, sl]], axis=0))
            lse_ref[:, sl] = jnp.broadcast_to(m + jnp.log(den), (ATT_STEP, d))

    out_spec = _att_spec(0)
    return pl.pallas_call(
        body, name=name, grid=(b, r, nb),
        in_specs=[_att_spec(0), _att_edge_spec(1, -1, nb), _att_spec(1), _att_edge_spec(2, -1, nb), _att_spec(2)],
        out_specs=[out_spec, out_spec],
        out_shape=[jax.ShapeDtypeStruct((b, r, l, ATT_OUT_DIM), F32)] * 2,
        compiler_params=_params("parallel", "parallel", "parallel"),
    )(qkr, qkr, qkr, qkr, qkr)


def _att_merge(os_, lses, name):
    n_pat = len(os_)
    b, _, s, gw = os_[0].shape
    ts, d = ATT_TILE, ATT_HEAD_DIM

    def body(*refs):
        o_refs, l_refs = refs[:n_pat], refs[n_pat:2 * n_pat]
        att_ref, lse_outs, stage = refs[2 * n_pat], refs[2 * n_pat + 1:3 * n_pat + 1], refs[-1]
        for j in range(ATT_HEADS_PER_GROUP):
            sl = slice(d * j, d * (j + 1))
            ov = [_from_residues(lambda ri, g=g: o_refs[g][ri, :, sl], stage, r)
                  for g, r in enumerate(ATT_DILATIONS)]
            ls = [_from_residues(lambda ri, g=g: l_refs[g][ri, :, sl], stage, r)
                  for g, r in enumerate(ATT_DILATIONS)]
            m = functools.reduce(jnp.maximum, ls)
            es = [jnp.exp(lv - m) for lv in ls]
            tot = functools.reduce(lambda u, v: u + v, es)
            acc = (es[0] / tot) * ov[0]
            for g in range(1, n_pat):
                acc = acc + (es[g] / tot) * ov[g]
            att_ref[:, sl] = acc
            joint = m + jnp.log(tot)
            for g, r in enumerate(ATT_DILATIONS):
                def store(ri, rows, out=lse_outs[g]):
                    out[ri, :, sl] = rows
                _to_residues(joint, stage, r, store)

    parts = [_by_residue_spec(r, gw) for r in ATT_DILATIONS]
    return pl.pallas_call(
        body, name=name, grid=(b, s // ts), in_specs=parts * 2,
        out_specs=[pl.BlockSpec((None, ts, gw), lambda bi, i: (bi, i, 0))] + parts,
        out_shape=[jax.ShapeDtypeStruct((b, s, gw), F32)]
        + [jax.ShapeDtypeStruct((b, r, s // r, gw), F32) for r in ATT_DILATIONS],
        scratch_shapes=[pltpu.VMEM((ts, d), F32)],
        compiler_params=_params("parallel", "parallel"),
    )(*os_, *lses)


def _att_delta(att, datt, name):
    b, s, gw = att.shape
    ts, d = ATT_TILE, ATT_HEAD_DIM
    n_pat = len(ATT_DILATIONS)

    def body(a_ref, d_ref, *rest):
        do_outs, dl_outs, stage = rest[:n_pat], rest[n_pat:2 * n_pat], rest[-1]
        for j in range(ATT_HEADS_PER_GROUP):
            sl = slice(d * j, d * (j + 1))
            dv = d_ref[:, sl]
            delta = jnp.broadcast_to(jnp.sum(a_ref[:, sl] * dv, axis=-1, keepdims=True), (ts, d))
            for g, r in enumerate(ATT_DILATIONS):
                def store_do(ri, rows, out=do_outs[g]):
                    out[ri, :, sl] = rows.astype(BF16)

                def store_dl(ri, rows, out=dl_outs[g]):
                    out[ri, :, sl] = rows

                _to_residues(dv, stage, r, store_do)
                _to_residues(delta, stage, r, store_dl)

    row = pl.BlockSpec((None, ts, gw), lambda bi, i: (bi, i, 0))
    parts = [_by_residue_spec(r, gw) for r in ATT_DILATIONS]
    outs = pl.pallas_call(
        body, name=name, grid=(b, s // ts), in_specs=[row, row], out_specs=parts * 2,
        out_shape=[jax.ShapeDtypeStruct((b, r, s // r, gw), BF16) for r in ATT_DILATIONS]
        + [jax.ShapeDtypeStruct((b, r, s // r, gw), F32) for r in ATT_DILATIONS],
        scratch_shapes=[pltpu.VMEM((ts, d), F32)],
        compiler_params=_params("parallel", "parallel"),
    )(att, datt)
    return outs[:n_pat], outs[n_pat:]


def _att_bwd_q(qkr, datt, lse, delta, name):
    b, r, l, _ = qkr.shape
    nb = l // ATT_STEP
    d = ATT_HEAD_DIM

    def body(q_ref, kp_ref, k_ref, vp_ref, v_ref, do_ref, lse_ref, dl_ref, dq_ref):
        mask = _band_mask((ATT_STEP, ATT_BLOCK + ATT_STEP), 0, pl.program_id(2) > 0)
        heads = [slice(d * j, d * (j + 1)) for j in range(ATT_HEADS_PER_GROUP)]
        kcats = [jnp.concatenate([kp_ref[:, sl], k_ref[:, sl]], axis=0) for sl in heads]
        scores = [_nt(q_ref[:, sl], kcat) for sl, kcat in zip(heads, kcats)]
        dps = [_nt(do_ref[:, sl], jnp.concatenate([vp_ref[:, sl], v_ref[:, sl]], axis=0)) for sl in heads]
        probs = [jnp.exp(jnp.where(mask, sc * ATT_SCALE - lse_ref[:, sl.start:sl.start + 1], NEG_INF))
                 for sl, sc in zip(heads, scores)]
        dscs = [pr * (dp - dl_ref[:, sl.start:sl.start + 1]) for sl, pr, dp in zip(heads, probs, dps)]
        for sl, dsc, kcat in zip(heads, dscs, kcats):
            dq_ref[:, sl] = _nn(dsc, kcat) * ATT_SCALE

    tok = _att_spec(0)
    return pl.pallas_call(
        body, name=name, grid=(b, r, nb),
        in_specs=[_att_spec(0), _att_edge_spec(1, -1, nb), _att_spec(1), _att_edge_spec(2, -1, nb), _att_spec(2),
                  tok, tok, tok],
        out_specs=tok,
        out_shape=jax.ShapeDtypeStruct((b, r, l, ATT_OUT_DIM), F32),
        compiler_params=_params("parallel", "parallel", "parallel"),
    )(qkr, qkr, qkr, qkr, qkr, datt, lse, delta)


def _att_bwd_kv(qkr, datt, lse, delta, name):
    b, r, l, _ = qkr.shape
    nb = l // ATT_STEP
    d = ATT_HEAD_DIM

    def body(k_ref, v_ref, q_ref, qn_ref, do_ref, don_ref, lse_ref, lsen_ref, dl_ref, dln_ref, dk_ref, dv_ref):
        shape = (ATT_STEP, ATT_STEP + ATT_BLOCK)
        kj = lax.broadcasted_iota(jnp.int32, shape, 0)
        qi = lax.broadcasted_iota(jnp.int32, shape, 1)
        dist = qi - kj
        has_next = pl.program_id(2) < nb - 1
        mask = (dist >= 0) & (dist <= ATT_BLOCK) & (has_next | (qi < ATT_STEP))
        def per_query(own_ref, next_ref, sl):
            return jnp.tile(jnp.concatenate([own_ref[:, sl], next_ref[:, sl]], axis=0).T, (ATT_STEP // d, 1))

        heads = [slice(d * j, d * (j + 1)) for j in range(ATT_HEADS_PER_GROUP)]
        qcats = [jnp.concatenate([q_ref[:, sl], qn_ref[:, sl]], axis=0) for sl in heads]
        docats = [jnp.concatenate([do_ref[:, sl], don_ref[:, sl]], axis=0) for sl in heads]
        scores = [_nt(k_ref[:, sl], qcat) for sl, qcat in zip(heads, qcats)]
        dps = [_nt(v_ref[:, sl], docat) for sl, docat in zip(heads, docats)]
        probs = [jnp.exp(jnp.where(mask, sc * ATT_SCALE - per_query(lse_ref, lsen_ref, sl), NEG_INF))
                 for sl, sc in zip(heads, scores)]
        for sl, pr, docat in zip(heads, probs, docats):
            dv_ref[:, sl] = _nn(pr, docat)
        dscs = [pr * (dp - per_query(dl_ref, dln_ref, sl)) for sl, pr, dp in zip(heads, probs, dps)]
        for sl, dsc, qcat in zip(heads, dscs, qcats):
            dk_ref[:, sl] = _nn(dsc, qcat) * ATT_SCALE

    tok, tok_n = _att_spec(0), _att_edge_spec(0, 1, nb)
    return pl.pallas_call(
        body, name=name, grid=(b, r, nb),
        in_specs=[_att_spec(1), _att_spec(2), _att_spec(0), _att_edge_spec(0, 1, nb),
                  tok, tok_n, tok, tok_n, tok, tok_n],
        out_specs=[tok, tok],
        out_shape=[jax.ShapeDtypeStruct((b, r, l, ATT_OUT_DIM), F32)] * 2,
        compiler_params=_params("parallel", "parallel", "parallel"),
    )(qkr, qkr, qkr, qkr, datt, datt, lse, lse, delta, delta)


def _att_out_proj_mix(att, w_att_t, gl, bg, ys, name):
    t, k = att.shape
    d = w_att_t.shape[0]
    tm = _pick(t, (512, 256, 128))

    def body(a_ref, w_ref, gl_ref, bg_ref, ys_ref, ya_ref, o_ref):
        ya = _nt(a_ref[...], w_ref[...])
        ya_ref[...] = ya
        g0 = _sigmoid(gl_ref[:, :d] + bg_ref[:, :d])
        g1 = _sigmoid(gl_ref[:, d:] + bg_ref[:, d:])
        o_ref[...] = (g0 * ys_ref[...] + g1 * ya).astype(BF16)

    row = pl.BlockSpec((tm, d), lambda i: (i, 0))
    return pl.pallas_call(
        body, name=name, grid=(t // tm,),
        in_specs=[pl.BlockSpec((tm, k), lambda i: (i, 0)), pl.BlockSpec((d, k), lambda i: (0, 0)),
                  pl.BlockSpec((tm, 2 * d), lambda i: (i, 0)), pl.BlockSpec((1, 2 * d), lambda i: (0, 0)), row],
        out_specs=[row, row],
        out_shape=[jax.ShapeDtypeStruct((t, d), F32), jax.ShapeDtypeStruct((t, d), BF16)],
        compiler_params=_params("parallel"),
    )(att, w_att_t, gl, bg, ys)


def _mix_out_dx_mix_bwd(dx, w_mix, gl, bg, ys, ya, name):
    t, d = ys.shape
    tm = _pick(t, (512, 256, 128))

    def body(dx_ref, w_ref, gl_ref, bg_ref, ys_ref, ya_ref, dys_ref, dya_ref, dgl_ref, dbg_ref):
        @pl.when(pl.program_id(0) == 0)
        def _():
            dbg_ref[...] = jnp.zeros_like(dbg_ref)

        dm = _nt(dx_ref[...], w_ref[...])
        g0 = _sigmoid(gl_ref[:, :d] + bg_ref[:, :d])
        g1 = _sigmoid(gl_ref[:, d:] + bg_ref[:, d:])
        dys_ref[...] = (dm * g0).astype(BF16)
        dya_ref[...] = (dm * g1).astype(BF16)
        d0 = dm * ys_ref[...] * (g0 * (1.0 - g0))
        d1 = dm * ya_ref[...] * (g1 * (1.0 - g1))
        dgl_ref[:, :d] = d0.astype(BF16)
        dgl_ref[:, d:] = d1.astype(BF16)
        dbg_ref[:, :d] += jnp.sum(d0, axis=0, keepdims=True)
        dbg_ref[:, d:] += jnp.sum(d1, axis=0, keepdims=True)

    row = pl.BlockSpec((tm, d), lambda i: (i, 0))
    wide = pl.BlockSpec((tm, 2 * d), lambda i: (i, 0))
    vec = pl.BlockSpec((1, 2 * d), lambda i: (0, 0))
    gate_cols = pl.BlockSpec((tm, 2 * d), lambda i: (i, DPROJ_COLS["gate"] // (2 * d)))
    return pl.pallas_call(
        body, name=name, grid=(t // tm,),
        in_specs=[row, pl.BlockSpec((d, d), lambda i: (0, 0)), wide, vec, row, row],
        out_specs=[row, row, gate_cols, vec],
        out_shape=[jax.ShapeDtypeStruct((t, d), BF16), jax.ShapeDtypeStruct((t, d), BF16),
                   jax.ShapeDtypeStruct((t, DPROJ_WIDTH), BF16), jax.ShapeDtypeStruct((1, 2 * d), F32)],
        compiler_params=_params("arbitrary"),
    )(dx, w_mix, gl, bg, ys, ya)


def _up_proj_swiglu(h, w_up_t, gt, name):
    t, k = h.shape
    f = w_up_t.shape[0]
    tm, tn, _ = _mm_tiles(t, f, k, h.dtype.itemsize, w_up_t.dtype.itemsize, 4 + 2, True)

    def body(h_ref, w_ref, g_ref, up_ref, act_ref):
        up = _nt(h_ref[...], w_ref[...])
        up_ref[...] = up
        gv = g_ref[...]
        act_ref[...] = ((gv * _sigmoid(gv)) * up).astype(BF16)

    tile = pl.BlockSpec((tm, tn), lambda i, j: (i, j))
    return pl.pallas_call(
        body, name=name, grid=(t // tm, f // tn),
        in_specs=[pl.BlockSpec((tm, k), lambda i, j: (i, 0)), pl.BlockSpec((tn, k), lambda i, j: (j, 0)), tile],
        out_specs=[tile, tile],
        out_shape=[jax.ShapeDtypeStruct((t, f), F32), jax.ShapeDtypeStruct((t, f), BF16)],
        compiler_params=_params("parallel", "parallel"),
    )(h, w_up_t, gt)


def _down_dx_swiglu_bwd(dx, w_down, gt, up, name):
    t, k = dx.shape
    f = w_down.shape[0]
    tm, tn, _ = _mm_tiles(t, f, k, dx.dtype.itemsize, w_down.dtype.itemsize, 2 + 2, True)
    tm = min(tm, 512)

    def body(d_ref, w_ref, g_ref, u_ref, dg_ref, du_ref):
        dact = _nt(d_ref[...], w_ref[...])
        gv = g_ref[...]
        sg = _sigmoid(gv)
        dg_ref[...] = (dact * u_ref[...] * (sg * (1.0 + gv * (1.0 - sg)))).astype(BF16)
        du_ref[...] = (dact * (gv * sg)).astype(BF16)

    tile = pl.BlockSpec((tm, tn), lambda i, j: (i, j))
    return pl.pallas_call(
        body, name=name, grid=(t // tm, f // tn),
        in_specs=[pl.BlockSpec((tm, k), lambda i, j: (i, 0)), pl.BlockSpec((tn, k), lambda i, j: (j, 0)), tile, tile],
        out_specs=[tile, tile], out_shape=[jax.ShapeDtypeStruct((t, f), BF16)] * 2,
        compiler_params=_params("parallel", "parallel"),
    )(dx, w_down, gt, up)


def _peer(k):
    x, y, c = lax.axis_index("x"), lax.axis_index("y"), lax.axis_index("c")
    px, py, pc = x ^ ((k >> 2) & 1), y ^ ((k >> 1) & 1), c ^ (k & 1)
    return (px, py, pc), 4 * px + 2 * py + pc


def _my_index():
    return 4 * lax.axis_index("x") + 2 * lax.axis_index("y") + lax.axis_index("c")


def _all_gather(parts, name):
    n_parts = len(parts)

    def body(*refs):
        ins, outs = refs[:n_parts], refs[n_parts:2 * n_parts]
        send_sems, recv_sems, local_sems = refs[2 * n_parts:]
        here, me = _peer(0)
        sibling, sib_idx = _peer(1)
        chips = [_peer(2 * q) for q in range(1, N_CHIPS)]

        def copy(i, k, block, to, src=None):
            return pltpu.make_async_remote_copy(
                src_ref=outs[i].at[block] if src is None else src, dst_ref=outs[i].at[block],
                send_sem=send_sems.at[i * (N_DEV - 1) + k], recv_sem=recv_sems.at[i * (N_DEV - 1) + k],
                device_id=to, device_id_type=MESH)

        local = [pltpu.make_async_copy(ins[i], outs[i].at[me], local_sems.at[i]) for i in range(n_parts)]
        for cp in local:
            cp.start()
        sends = []
        for i in range(n_parts):
            sends.append(copy(i, 0, me, sibling, src=ins[i]))
            sends += [copy(i, q, me, chip, src=ins[i]) for q, (chip, _) in enumerate(chips, start=1)]
        for cp in sends:
            cp.start()
        for q, (chip, chip_idx) in enumerate(chips, start=1):
            for i in range(n_parts):
                copy(i, q, chip_idx, here).wait_recv()
                fwd = copy(i, N_CHIPS - 1 + q, chip_idx, sibling)
                fwd.start()
                sends.append(fwd)
        for i in range(n_parts):
            copy(i, 0, sib_idx, here).wait_recv()
        for q, (_, chip_idx) in enumerate(chips, start=1):
            for i in range(n_parts):
                copy(i, N_CHIPS - 1 + q, chip_idx ^ 1, here).wait_recv()
        for cp in sends:
            cp.wait_send()
        for cp in local:
            cp.wait()

    hbm = pl.BlockSpec(memory_space=pl.ANY)
    return pl.pallas_call(
        body, name=name, in_specs=[hbm] * n_parts, out_specs=[hbm] * n_parts,
        out_shape=[jax.ShapeDtypeStruct((N_DEV,) + p_.shape, p_.dtype) for p_ in parts],
        scratch_shapes=[pltpu.SemaphoreType.DMA((n_parts * (N_DEV - 1),)),
                        pltpu.SemaphoreType.DMA((n_parts * (N_DEV - 1),)),
                        pltpu.SemaphoreType.DMA((n_parts,))],
        compiler_params=pltpu.CompilerParams(has_side_effects=True),
    )(*parts)


HBM_SPEC = pl.BlockSpec(memory_space=pltpu.HBM)
SEM_SPEC = pl.BlockSpec(memory_space=pltpu.SEMAPHORE)
DATAFLOW = pltpu.SideEffectType.DATAFLOW_SIDE_EFFECTING


def _gather_start(block, after, name):
    per_peer = block.ndim == 3

    def body(v_ref, land_ref, after_ref, send_sems, recv_sems, v_thru, land_thru, token):
        me = _my_index()
        for k in range(1, N_DEV):
            peer, pidx = _peer(k)
            pltpu.make_async_remote_copy(
                src_ref=v_ref.at[pidx] if per_peer else v_ref, dst_ref=land_ref.at[me],
                send_sem=send_sems.at[k - 1], recv_sem=recv_sems.at[k - 1],
                device_id=peer, device_id_type=MESH).start()
        token[...] = jnp.zeros_like(token)

    land_shape = (N_DEV,) + block.shape[-2:]
    return pl.pallas_call(
        body, name=name,
        out_shape=(pltpu.SemaphoreType.DMA((N_DEV - 1,)), pltpu.SemaphoreType.DMA((N_DEV - 1,)),
                   pltpu.HBM(block.shape, block.dtype), pltpu.HBM(land_shape, block.dtype),
                   jax.ShapeDtypeStruct((8, LANES), F32)),
        in_specs=(HBM_SPEC, HBM_SPEC, pl.BlockSpec(memory_space=pl.ANY)),
        out_specs=(SEM_SPEC, SEM_SPEC, HBM_SPEC, HBM_SPEC, pl.BlockSpec(memory_space=pltpu.VMEM)),
        input_output_aliases={0: 2, 1: 3},
        compiler_params=pltpu.CompilerParams(has_side_effects=DATAFLOW),
    )(pltpu.with_memory_space_constraint(block, pltpu.HBM),
      pltpu.with_memory_space_constraint(lax.empty(land_shape, block.dtype), pltpu.HBM), after)


def _gather_wait(send_sems, recv_sems, block, landing, after, name):
    per_peer = block.ndim == 3

    def body(v_ref, land_ref, send_sems, recv_sems, after_ref, v_dead, got_ref):
        for k in range(1, N_DEV):
            peer, pidx = _peer(k)
            copy = pltpu.make_async_remote_copy(
                src_ref=v_ref.at[pidx] if per_peer else v_ref, dst_ref=land_ref.at[pidx],
                send_sem=send_sems.at[k - 1], recv_sem=recv_sems.at[k - 1],
                device_id=peer, device_id_type=MESH)
            copy.wait_send()
            copy.wait_recv()

    return pl.pallas_call(
        body, name=name,
        out_shape=(pltpu.HBM(block.shape, block.dtype), pltpu.HBM(landing.shape, landing.dtype)),
        in_specs=(HBM_SPEC, HBM_SPEC, SEM_SPEC, SEM_SPEC, pl.BlockSpec(memory_space=pl.ANY)),
        out_specs=(HBM_SPEC, HBM_SPEC), input_output_aliases={0: 0, 1: 1},
        compiler_params=pltpu.CompilerParams(has_side_effects=DATAFLOW),
    )(block, landing, send_sems, recv_sems, after)[1]


TILE_ELEMS = 1024 * 1024


def _shared_exchange(shared, name):
    def body(sh_ref, gsh_ref, send_sems, recv_sems, local_sem):
        me = _my_index()
        local = pltpu.make_async_copy(sh_ref, gsh_ref.at[me], local_sem)
        local.start()
        sends = []
        for k in range(1, N_DEV):
            peer, _ = _peer(k)
            cp = pltpu.make_async_remote_copy(
                src_ref=sh_ref, dst_ref=gsh_ref.at[me], send_sem=send_sems.at[k - 1],
                recv_sem=recv_sems.at[k - 1], device_id=peer, device_id_type=MESH)
            cp.start()
            sends.append(cp)
        for k in range(1, N_DEV):
            peer, pidx = _peer(k)
            pltpu.make_async_remote_copy(
                src_ref=sh_ref, dst_ref=gsh_ref.at[pidx], send_sem=send_sems.at[k - 1],
                recv_sem=recv_sems.at[k - 1], device_id=peer, device_id_type=MESH).wait_recv()
        for cp in sends:
            cp.wait_send()
        local.wait()

    hbm = pl.BlockSpec(memory_space=pl.ANY)
    return pl.pallas_call(
        body, name=name, in_specs=[hbm], out_specs=hbm,
        out_shape=jax.ShapeDtypeStruct((N_DEV,) + shared.shape, shared.dtype),
        scratch_shapes=[pltpu.SemaphoreType.DMA((N_DEV - 1,)), pltpu.SemaphoreType.DMA((N_DEV - 1,)),
                        pltpu.SemaphoreType.DMA],
        compiler_params=pltpu.CompilerParams(has_side_effects=True),
    )(shared)


def _adamw(parts, w, m, v, name, row0=0, own=None):
    n_parts, rows, lanes = parts.shape
    tr = rows if rows * lanes <= TILE_ELEMS // 2 else _tile_rows(math.gcd(rows, row0), TILE_ELEMS // 4 // lanes, 8)
    c1 = 1.0 - ADAM_B1 ** ADAM_STEP
    c2 = 1.0 - ADAM_B2 ** ADAM_STEP

    def body(*refs):
        if own is None:
            p_ref, w_ref, m_ref, v_ref, g_ref, d_ref, nm_ref, nv_ref = refs
            terms = [p_ref[j].astype(F32) for j in range(n_parts)]
        else:
            me_ref, p_ref, own_ref, w_ref, m_ref, v_ref, g_ref, d_ref, nm_ref, nv_ref = refs
            terms = [jnp.where(me_ref[0] == j, own_ref[...], p_ref[j]).astype(F32) for j in range(n_parts)]
        g = terms[0]
        for term in terms[1:]:
            g = g + term
        nm = ADAM_B1 * m_ref[...] + (1.0 - ADAM_B1) * g
        nv = ADAM_B2 * v_ref[...] + (1.0 - ADAM_B2) * (g * g)
        g_ref[...] = g
        nm_ref[...] = nm
        nv_ref[...] = nv
        d_ref[...] = -ADAM_LR * ((nm / c1) / (jnp.sqrt(nv / c2) + ADAM_EPS) + ADAM_WD * w_ref[...])

    row = pl.BlockSpec((tr, lanes), lambda i, *_: (i, 0))
    state = pl.BlockSpec((tr, lanes), lambda i, *_: (row0 // tr + i, 0))
    in_specs = [pl.BlockSpec((n_parts, tr, lanes), lambda i, *_: (0, i, 0)), state, state, state]
    args, n_prefetch = (parts, w, m, v), 0
    if own is not None:
        slabs, me = own
        in_specs.insert(1, pl.BlockSpec((None, tr, lanes), lambda i, me_ref: (me_ref[0], i, 0)))
        args, n_prefetch = (me, parts, slabs, w, m, v), 1
    return pl.pallas_call(
        body, name=name,
        grid_spec=pltpu.PrefetchScalarGridSpec(num_scalar_prefetch=n_prefetch, grid=(rows // tr,),
                                               in_specs=in_specs, out_specs=[row] * 4),
        out_shape=[jax.ShapeDtypeStruct((rows, lanes), F32)] * 4,
        compiler_params=_params("parallel"),
    )(*args)


MATRIX_SHARDS = (
    ("w_in", (D_MODEL, IN_PROJ_DIM // N_DEV), True),
    ("w_ssm_out", (SSM_D_INNER // N_DEV, D_MODEL), False),
    ("w_att_out", (ATT_OUT_DIM, D_MODEL // N_DEV), True),
    ("w_mix_out", (D_MODEL // N_DEV, D_MODEL), False),
    ("w_ffn_gate", (D_MODEL, D_FF // N_DEV), True),
    ("w_ffn_up", (D_MODEL, D_FF // N_DEV), True),
    ("w_ffn_down", (D_FF // N_DEV, D_MODEL), False),
)
CONV_SHARD = ("conv_w", (SSM_CONV, SSM_CONV_DIM // N_DEV), True)
SHARDED = MATRIX_SHARDS + (CONV_SHARD,)
REPLICATED = (("norm_mix", D_MODEL), ("b_gate", 2 * D_MODEL), ("conv_b", SSM_CONV_DIM), ("dt_bias", SSM_N_HEADS),
              ("a_log", SSM_N_HEADS), ("d_skip", SSM_N_HEADS), ("ssm_norm", SSM_D_INNER), ("norm_ffn", D_MODEL),
              ("norm_final", D_MODEL))


def _round_up(n, mult):
    return -(-n // mult) * mult


def _pack_rows(flat, row_mult):
    rows = _round_up(-(-flat.shape[0] // LANES), row_mult)
    return jnp.pad(flat, (0, rows * LANES - flat.shape[0])).reshape(rows, LANES)


def _stacking(specs):
    return tuple((name, (shape[1], shape[0]) if by_cols else shape, by_cols) for name, shape, by_cols in specs)


def _to_stacking(vals, specs):
    return {name: (vals[name].T if by_cols else vals[name]) for name, _, by_cols in specs}


STACK_WIDTH = D_MODEL
STACK_ALIGN = 16
STACK_ORDER = ("w_ssm_out", "w_mix_out", "w_ffn_gate", "w_ffn_up", "w_ffn_down", "w_att_out", "conv_w", "w_in")
GATHER_LATER = STACK_ORDER[:-1]
REDUCE_EARLY = STACK_ORDER[:5]
REDUCE_LATE = STACK_ORDER[5:]


def _stack_layout():
    shapes = {name: shape for name, shape, _ in _stacking(SHARDED)}
    layout, off = {}, 0
    for name in STACK_ORDER:
        r, c = shapes[name]
        rows = r if c == STACK_WIDTH else _round_up(-(-(r * c) // STACK_WIDTH), STACK_ALIGN)
        layout[name] = (off, rows, (r, c))
        off = _round_up(off + rows, STACK_ALIGN)
    return layout, _round_up(off, 1024)


def _to_stack_rows(v, rows):
    if v.shape[-1] == STACK_WIDTH:
        return v
    lead = v.shape[:-2]
    flat = v.reshape(lead + (-1,))
    flat = jnp.pad(flat, [(0, 0)] * len(lead) + [(0, rows * STACK_WIDTH - flat.shape[-1])])
    return flat.reshape(lead + (rows, STACK_WIDTH))


def _from_stack_rows(block, shape):
    r, c = shape
    if c == STACK_WIDTH:
        return block
    lead = block.shape[:-2]
    return block.reshape(lead + (-1,))[..., :r * c].reshape(lead + (r, c))


def _stack(vals, dtype, skip=(), names=STACK_ORDER):
    layout, total = _stack_layout()
    order = names
    after = STACK_ORDER.index(order[-1]) + 1
    if after < len(STACK_ORDER):
        total = layout[STACK_ORDER[after]][0]
    lead = next(iter(vals.values())).shape[:-2]
    pieces = []
    for i, name in enumerate(order):
        off, rows, _ = layout[name]
        until = layout[order[i + 1]][0] if i + 1 < len(order) else total
        piece = jnp.zeros(lead + (rows, STACK_WIDTH), dtype) if name in skip else _to_stack_rows(vals[name], rows)
        pieces.append(jnp.pad(piece.astype(dtype), [(0, 0)] * len(lead) + [(0, until - off - rows), (0, 0)]))
    return jnp.concatenate(pieces, axis=-2)


def _unstack(stacked, names):
    layout, _ = _stack_layout()
    row0 = layout[names[0]][0]
    return {name: _from_stack_rows(stacked[..., layout[name][0] - row0:layout[name][0] - row0 + layout[name][1], :],
                                   layout[name][2]) for name in names}


W_IN_SHARD_ROWS = IN_PROJ_DIM // N_DEV


def _w_in_row_moves():
    moves, orig = [], 0
    for name, size in IN_SPLIT:
        for j in range(N_DEV):
            lo, hi = max(orig, W_IN_SHARD_ROWS * j), min(orig + size, W_IN_SHARD_ROWS * (j + 1))
            if lo < hi:
                moves.append((j, lo - W_IN_SHARD_ROWS * j, DPROJ_COLS[name] + lo - orig, hi - lo))
        orig += size
    return moves


def _w_in_from_shards(shards, name):
    total, base = shards.shape[1], 0
    pad_lo, pad_hi = DPROJ_COLS["dt"] + _round_up(SSM_N_HEADS, STACK_ALIGN), DPROJ_COLS["dt"] + DPROJ_DT_WIDTH

    def body(x_ref, o_ref):
        o_ref[pad_lo:pad_hi, :] = jnp.zeros((pad_hi - pad_lo, LANES), x_ref.dtype)
        for j, r, at, n in _w_in_row_moves():
            o_ref[at:at + n, :] = x_ref[j, base + r:base + r + n, :]

    return pl.pallas_call(
        body, name=name, grid=(STACK_WIDTH // LANES,),
        in_specs=[pl.BlockSpec((N_DEV, total, LANES), lambda c: (0, 0, c))],
        out_specs=pl.BlockSpec((DPROJ_WIDTH, LANES), lambda c: (0, c)),
        out_shape=jax.ShapeDtypeStruct((DPROJ_WIDTH, STACK_WIDTH), shards.dtype),
        compiler_params=_params("parallel"),
    )(shards)


def _w_in_to_shards(dw_all, head, name):
    layout, total = _stack_layout()
    total -= layout[REDUCE_LATE[0]][0]
    base = head.shape[1]
    end = base + W_IN_SHARD_ROWS

    def body(x_ref, h_ref, o_ref):
        o_ref[:, 0:base, :] = h_ref[...]
        for j, r, at, n in _w_in_row_moves():
            o_ref[j, base + r:base + r + n, :] = x_ref[at:at + n, :]
        o_ref[:, end:total, :] = jnp.zeros((N_DEV, total - end, LANES), o_ref.dtype)

    return pl.pallas_call(
        body, name=name, grid=(STACK_WIDTH // LANES,),
        in_specs=[pl.BlockSpec((DPROJ_WIDTH, LANES), lambda c: (0, c)),
                  pl.BlockSpec((N_DEV, base, LANES), lambda c: (0, 0, c))],
        out_specs=pl.BlockSpec((N_DEV, total, LANES), lambda c: (0, 0, c)),
        out_shape=jax.ShapeDtypeStruct((N_DEV, total, STACK_WIDTH), dw_all.dtype),
        compiler_params=_params("parallel"),
    )(dw_all, head)


REPLICATED_ROWS = sum(-(-size // LANES) for _, size in REPLICATED)
LOSS_ROW = REPLICATED_ROWS


def _pack_replicated(vals):
    rows = []
    for name, size in REPLICATED:
        v = vals[name].reshape(-1).astype(F32)
        rows.append(jnp.pad(v, (0, _round_up(size, LANES) - size)))
    return _pack_rows(jnp.concatenate(rows), 8)


def _unpack_replicated(packed, shapes):
    flat = packed.reshape(-1)
    out, off = {}, 0
    for name, size in REPLICATED:
        out[name] = flat[off:off + size].reshape(shapes[name])
        off += _round_up(size, LANES)
    return out


def _lane_row(v):
    v = v.reshape(-1).astype(F32)
    return jnp.pad(v, (0, LANES - v.shape[0])).reshape(1, LANES)


IN_SPLIT = (("z", SSM_D_INNER), ("xbc", SSM_CONV_DIM), ("dt", SSM_N_HEADS), ("qkv", ATT_QKV_DIM), ("gate", 2 * D_MODEL))


def kernel(x, norm_mix, w_in, b_gate, conv_w, conv_b, dt_bias, a_log, d_skip, ssm_norm, w_ssm_out, w_att_out, w_mix_out, norm_ffn, w_ffn_gate, w_ffn_up, w_ffn_down, norm_final, loss_target, m_norm_mix, m_w_in, m_b_gate, m_conv_w, m_conv_b, m_dt_bias, m_a_log, m_d_skip, m_ssm_norm, m_w_ssm_out, m_w_att_out, m_w_mix_out, m_norm_ffn, m_w_ffn_gate, m_w_ffn_up, m_w_ffn_down, m_norm_final, v_norm_mix, v_w_in, v_b_gate, v_conv_w, v_conv_b, v_dt_bias, v_a_log, v_d_skip, v_ssm_norm, v_w_ssm_out, v_w_att_out, v_w_mix_out, v_norm_ffn, v_w_ffn_gate, v_w_ffn_up, v_w_ffn_down, v_norm_final):
    given = dict(locals())
    weights = {name: given[name][0] for name, _, _ in SHARDED}
    b, s, d = x.shape
    t = b * s

    stacking = _to_stacking(weights, SHARDED)
    conv_shape = dict((name, shape) for name, shape, _ in _stacking(SHARDED))["conv_w"]
    w_in_local = jnp.pad(stacking["w_in"].astype(BF16), ((0, -W_IN_SHARD_ROWS % STACK_ALIGN), (0, 0)))
    conv_local = _pack_rows(stacking["conv_w"].reshape(-1), 8)
    w_in_shards, conv_all = _all_gather([w_in_local, conv_local], "w_in_all_gather")
    head_local = _stack(stacking, BF16, skip=("conv_w",), names=GATHER_LATER)
    in_flight = _gather_start(head_local, conv_all, "weights_gather_start")
    w_in_all = _w_in_from_shards(w_in_shards, "w_in_from_shards")
    w_sec = {name: w_in_all[DPROJ_COLS[name]:DPROJ_COLS[name] + _round_up(size, LANES)] for name, size in IN_SPLIT}
    conv_size = conv_shape[0] * conv_shape[1]
    conv_taps = conv_all.reshape(N_DEV, -1)[:, :conv_size].reshape(N_DEV * conv_shape[0], conv_shape[1]).T

    g_mix, g_ffn, g_fin = norm_mix.reshape(1, d), norm_ffn.reshape(1, d), norm_final.reshape(1, d)
    g_mix = g_mix + in_flight[4][:1, :1]
    bg_row = b_gate.reshape(1, 2 * d)
    convb_row = conv_b.reshape(1, SSM_CONV_DIM)
    ssmn_row = ssm_norm.reshape(1, SSM_D_INNER)
    dtb_row, alog_row = _lane_row(dt_bias), _lane_row(a_log)
    cosf, sinf = _rope_tables(s)

    x2d = x.reshape(t, d)
    h1 = _rmsnorm_fwd(x2d, g_mix, "norm_mix_fwd")
    proj = {name: _mm(h1, w_sec[name], mode="nt", name="in_proj_" + name) for name, _ in IN_SPLIT if name != "qkv"}
    xbc3 = proj["xbc"].reshape(b, s, SSM_CONV_DIM)
    xc = _conv_fwd(xbc3, conv_taps, convb_row, "conv_fwd")
    dtr3 = proj["dt"].reshape(b, s, DT_PAD)
    to_channels, to_heads = _head_masks()
    dskx = jnp.repeat(d_skip.reshape(-1).astype(F32), SSM_HEAD_DIM).reshape(1, SSM_D_INNER)
    y_ssd, h_states = _ssd_fwd(xc, dtr3, dtb_row, alog_row, dskx, to_channels, "ssd_fwd")
    y_ssd2 = y_ssd.reshape(t, SSM_D_INNER)
    ynorm = _gate_norm_fwd(y_ssd2, proj["z"], ssmn_row, "ssd_gate_norm_fwd")
    landed = _gather_wait(*in_flight[:4], ynorm, "weights_gather_wait")
    head_all = lax.dynamic_update_slice(landed, head_local[None], (_my_index(), 0, 0))
    full = {name: v.reshape((-1,) + v.shape[2:]) for name, v in _unstack(head_all, STACK_ORDER[:-2]).items()}
    y_ssm = _mm(ynorm, full["w_ssm_out"], mode="nn", name="ssm_out_proj")

    qk_parts = _qkv_proj_rope(h1, w_sec["qkv"], cosf, sinf, b, s, "in_proj_qkv_rope")
    att_parts = [_att_fwd(qk_parts[gi], "att_fwd_%d" % r) for gi, r in enumerate(ATT_DILATIONS)]
    att, *lse_parts = _att_merge([o for o, _ in att_parts], [l_ for _, l_ in att_parts], "att_merge")
    att2 = att.reshape(t, ATT_OUT_DIM)
    y_att, mixed = _att_out_proj_mix(att2, full["w_att_out"], proj["gate"], bg_row, y_ssm, "att_out_proj_mix")
    x2, h2 = _proj_residual_norm(mixed, full["w_mix_out"], x2d, g_ffn, "mix_out_proj_norm")
    gt = _mm(h2, full["w_ffn_gate"], mode="nt", name="ffn_gate_proj")
    up, act = _up_proj_swiglu(h2, full["w_ffn_up"], gt, "ffn_up_proj_swiglu")

    loss_row, dx3, dg_fin, dx3b = _down_proj_loss_head(act, full["w_ffn_down"], x2, g_fin, loss_target.reshape(t, d),
                                                       "ffn_down_proj_loss_head")
    grads = {}
    grads["w_ffn_down"] = _mm(act, dx3b, mode="tn", name="ffn_down_dw", out_dtype=BF16)
    dgt, dup = _down_dx_swiglu_bwd(dx3b, full["w_ffn_down"], gt, up, "ffn_down_dx_swiglu_bwd")
    grads["w_ffn_gate"] = _mm(dgt, h2, mode="tn", name="ffn_gate_dw", out_dtype=BF16)
    grads["w_ffn_up"] = _mm(dup, h2, mode="tn", name="ffn_up_dw", out_dtype=BF16)
    dh2 = _mm(dgt, full["w_ffn_gate"], mode="nn", name="ffn_gate_dx")
    dx2, dg_ffn, dx2b = _proj_norm_bwd(dup, full["w_ffn_up"], x2, g_ffn, dx3, "ffn_up_dx_norm_bwd", add=dh2,
                                       with_bf16=True)

    grads["w_mix_out"] = _mm(mixed, dx2b, mode="tn", name="mix_out_dw", out_dtype=BF16)
    dys, dya, dproj, dbg = _mix_out_dx_mix_bwd(dx2b, full["w_mix_out"], proj["gate"], bg_row, y_ssm, y_att,
                                               "mix_out_dx_mix_bwd")

    grads["w_ssm_out"] = _mm(ynorm, dys, mode="tn", name="ssm_out_dw", out_dtype=BF16)
    early = _stack({name: grads[name].reshape((N_DEV, -1, STACK_WIDTH)) for name in REDUCE_EARLY}, BF16,
                   names=REDUCE_EARLY)
    early_flight = _gather_start(early, dys, "grads_scatter_start")
    ssmn_row = ssmn_row + early_flight[4][:1, :1]
    dy_ssd, dproj, dssmn = _ssm_out_dx_gate_norm_bwd(dys, full["w_ssm_out"], y_ssd2, proj["z"], ssmn_row, dproj,
                                                     "ssm_out_dx_gate_norm_bwd")
    dxc, dproj, dalog, ddsk, ddtb = _ssd_bwd(xc, dtr3, dy_ssd.reshape(b, s, SSM_D_INNER), h_states, dtb_row, alog_row,
                                             dskx, to_channels, to_heads, dproj.reshape(b, s, DPROJ_WIDTH), "ssd_bwd")
    dproj, dconvw, dconvb = _conv_bwd(xbc3, dxc, conv_taps, convb_row, dproj, "conv_bwd")
    grads["conv_w"] = dconvw.T.astype(BF16)

    grads["w_att_out"] = _mm(dya, att2, mode="tn", name="att_out_dw", out_dtype=BF16)
    datt = _mm(dya, full["w_att_out"], mode="nn", name="att_out_dx").reshape(b, s, ATT_OUT_DIM)
    do_parts, dl_parts = _att_delta(att, datt, "att_delta")
    dqs, dks, dvs = [], [], []
    for gi, r in enumerate(ATT_DILATIONS):
        operands = (qk_parts[gi], do_parts[gi], lse_parts[gi], dl_parts[gi])
        dqs.append(_att_bwd_q(*operands, "att_bwd_q_%d" % r))
        dk_g, dv_g = _att_bwd_kv(*operands, "att_bwd_kv_%d" % r)
        dks.append(dk_g)
        dvs.append(dv_g)
    dproj = _rope_bwd(dqs, dks, dvs, cosf, sinf, dproj, "rope_bwd").reshape(t, DPROJ_WIDTH)

    dw_all = _mm(dproj, h1, mode="tn", name="in_proj_dw", out_dtype=BF16)
    head = _stack({name: grads[name].reshape((N_DEV, -1, grads[name].shape[-1])) for name in REDUCE_LATE[:-1]}, BF16,
                  names=REDUCE_LATE[:-1])
    late = _w_in_to_shards(dw_all, head, "grad_stacks")
    late_flight = _gather_start(late, dw_all, "grads_late_scatter_start")
    grad_x, dg_mix = _proj_norm_bwd(dproj, w_in_all, x2d, g_mix, dx2, "in_proj_dx_norm_bwd", after=late_flight[4])

    small = {"norm_mix": dg_mix, "b_gate": dbg, "conv_b": dconvb, "dt_bias": ddtb[:, :SSM_N_HEADS],
             "a_log": dalog[:, :SSM_N_HEADS], "d_skip": ddsk[:, :SSM_N_HEADS], "ssm_norm": dssmn,
             "norm_ffn": dg_ffn, "norm_final": dg_fin}
    shared = _pack_replicated(small)
    shared = shared.at[LOSS_ROW, 0].set(loss_row[0, 0])
    got_small = _shared_exchange(shared, "shared_grads_exchange")

    def packed(prefix):
        vals = _to_stacking({name: given[prefix + name][0] for name, _, _ in SHARDED}, SHARDED)
        rep = {name: given[prefix + name] for name, _ in REPLICATED}
        return _stack(vals, F32), _pack_replicated(rep)

    (w_big, w_small), (m_big, m_small), (v_big, v_small) = packed(""), packed("m_"), packed("v_")
    me = _my_index().astype(jnp.int32).reshape(1)
    big_early = _adamw(_gather_wait(*early_flight[:4], got_small, "grads_scatter_wait"), w_big, m_big, v_big,
                       "adamw_early", own=(early, me))
    big_late = _adamw(_gather_wait(*late_flight[:4], got_small, "grads_late_scatter_wait"), w_big, m_big, v_big,
                      "adamw_late", row0=early.shape[1], own=(late, me))
    sml = _adamw(got_small, w_small, m_small, v_small, "adamw_replicated")

    outs = [sml[0][LOSS_ROW, 0], grad_x.reshape(b, s, d)]
    rep_shapes = {name: given[name].shape for name, _ in REPLICATED}
    order = ["norm_mix", "w_in", "b_gate", "conv_w", "conv_b", "dt_bias", "a_log", "d_skip", "ssm_norm", "w_ssm_out",
             "w_att_out", "w_mix_out", "norm_ffn", "w_ffn_gate", "w_ffn_up", "w_ffn_down", "norm_final"]
    for early_k, late_k, sml_k in zip(big_early, big_late, sml):
        stacks = dict(_unstack(early_k, REDUCE_EARLY), **_unstack(late_k, REDUCE_LATE))
        sharded = _to_stacking(stacks, SHARDED)
        rep = _unpack_replicated(sml_k, rep_shapes)
        for name in order:
            outs.append(sharded[name][None] if name in sharded else rep[name])
    return tuple(outs)
```

```python
import functools
import math

import jax
import jax.numpy as jnp
from jax import lax
from jax.experimental import pallas as pl
from jax.experimental.pallas import tpu as pltpu

F32 = jnp.float32
BF16 = jnp.bfloat16

N_DEV = 8
N_CHIPS = 4
D_MODEL = 1024
SSM_D_INNER = 2048
SSM_HEAD_DIM = 64
SSM_N_HEADS = 32
SSM_N_GROUPS = 4
SSM_HEADS_PER_GROUP = SSM_N_HEADS // SSM_N_GROUPS
SSM_D_STATE = 128
SSM_CONV = 4
SSM_CHUNK = 128
SSM_CONV_DIM = 3072
ATT_HEAD_DIM = 128
ATT_HEADS_PER_GROUP = 4
ATT_DILATIONS = (1, 4, 16)
ATT_N_HEADS = 12
ATT_QKV_DIM = 4608
ATT_OUT_DIM = 512
ATT_BLOCK = 128
ROPE_THETA = 10000.0
D_FF = 2816
IN_PROJ_DIM = 11808
EPS = 1e-6
LANES = 128
DT_PAD = LANES

DPROJ_COLS = {"qkv": 0, "xbc": 4608, "dt": 7680, "z": 8192, "gate": 10240}
DPROJ_DT_WIDTH = 512
DPROJ_WIDTH = 12288

ADAM_LR = 0.001
ADAM_B1 = 0.9
ADAM_B2 = 0.999
ADAM_EPS = 1e-08
ADAM_WD = 0.01
ADAM_STEP = 10

VMEM_LIMIT = 56 * 1024 * 1024
MESH = pl.DeviceIdType.MESH
NEG_INF = float("-inf")


def _tile_rows(n, cap, mult):
    return max(t for t in range(mult, min(n, cap) + 1, mult) if n % t == 0)


def _pick(n, candidates):
    for c in candidates:
        if n % c == 0:
            return c
    return n


def _params(*sem):
    return pltpu.CompilerParams(dimension_semantics=sem, vmem_limit_bytes=VMEM_LIMIT)


def _sigmoid(x):
    return 0.5 * jnp.tanh(0.5 * x) + 0.5


def _softplus(x):
    return jnp.maximum(x, 0.0) + jnp.log(1.0 + jnp.exp(-jnp.abs(x)))


def _dot(a, b, dims):
    return lax.dot_general(a.astype(BF16), b.astype(BF16), (dims, ((), ())), preferred_element_type=F32)


def _nn(a, b):
    return _dot(a, b, ((1,), (0,)))


def _nt(a, b):
    return _dot(a, b, ((1,), (1,)))


def _tn(a, b):
    return _dot(a, b, ((0,), (0,)))


def _split3(v):
    hi = v.astype(BF16)
    r1 = v - hi.astype(F32)
    mid = r1.astype(BF16)
    lo = (r1 - mid.astype(F32)).astype(BF16)
    return hi, mid, lo


def _mask_nn(mask, v):
    mb = mask.astype(BF16)
    hi, mid, lo = _split3(v)
    return _nn(mb, hi) + (_nn(mb, mid) + _nn(mb, lo))


MM_VMEM_BUDGET = 40 * 1024 * 1024
MM_FULL_K = 2816


def _mm_tiles(m, n, k, a_bytes, b_bytes, o_bytes, has_add):
    tk = k if k <= MM_FULL_K else _pick(k, (2048, 1024, 512, 256, 128))
    tn = 1408 if (n > 1024 and n % 1408 == 0) else _pick(n, (1024, 768, 512, 384, 256, 128))
    for tm in (1408, 1024, 768, 512, 384, 256, 128):
        if m % tm:
            continue
        buffers = 2 * (tm * tk * a_bytes + tk * tn * b_bytes + tm * tn * (o_bytes + (4 if has_add else 0)))
        if tk < k:
            buffers += tm * tn * 4
        if buffers <= MM_VMEM_BUDGET:
            return tm, tn, tk
    return _pick(m, (128,)), tn, tk


def _mm(a, b, *, mode, name, out_dtype=F32, add=None, after=None):
    if mode == "nn":
        (m, k), n = a.shape, b.shape[1]
    elif mode == "nt":
        (m, k), n = a.shape, b.shape[0]
    else:
        (k, m), n = a.shape, b.shape[1]
    has_add = add is not None
    tm, tn, tk = _mm_tiles(m, n, k, a.dtype.itemsize, b.dtype.itemsize, jnp.dtype(out_dtype).itemsize, has_add)
    nk = k // tk
    dims = {"nn": ((1,), (0,)), "nt": ((1,), (1,)), "tn": ((0,), (0,))}[mode]
    a_spec = {"nn": pl.BlockSpec((tm, tk), lambda i, j, kk: (i, kk)),
              "nt": pl.BlockSpec((tm, tk), lambda i, j, kk: (i, kk)),
              "tn": pl.BlockSpec((tk, tm), lambda i, j, kk: (kk, i))}[mode]
    b_spec = {"nn": pl.BlockSpec((tk, tn), lambda i, j, kk: (kk, j)),
              "nt": pl.BlockSpec((tn, tk), lambda i, j, kk: (j, kk)),
              "tn": pl.BlockSpec((tk, tn), lambda i, j, kk: (kk, j))}[mode]
    o_spec = pl.BlockSpec((tm, tn), lambda i, j, kk: (i, j))

    def finish(r, c_ref, o_ref):
        if has_add:
            r = r + c_ref[...]
        o_ref[...] = r.astype(out_dtype)

    def body_one(*refs):
        a_ref, b_ref = refs[:2]
        finish(_dot(a_ref[...], b_ref[...], dims), refs[2] if has_add else None, refs[-1])

    def body_acc(*refs):
        a_ref, b_ref = refs[:2]
        o_ref, acc = refs[-2:]
        kk = pl.program_id(2)

        @pl.when(kk == 0)
        def _():
            acc[...] = jnp.zeros_like(acc)

        acc[...] += _dot(a_ref[...], b_ref[...], dims)

        @pl.when(kk == nk - 1)
        def _():
            finish(acc[...], refs[2] if has_add else None, o_ref)

    in_specs = [a_spec, b_spec] + ([o_spec] if has_add else [])
    args = (a, b) + ((add,) if has_add else ())
    if after is not None:
        in_specs, args = in_specs + [pl.BlockSpec(memory_space=pl.ANY)], args + (after,)
    return pl.pallas_call(
        body_one if nk == 1 else body_acc, name=name, grid=(m // tm, n // tn, nk),
        in_specs=in_specs, out_specs=o_spec,
        out_shape=jax.ShapeDtypeStruct((m, n), out_dtype),
        scratch_shapes=[] if nk == 1 else [pltpu.VMEM((tm, tn), F32)],
        compiler_params=_params("parallel", "parallel", "arbitrary"),
    )(*args)


def _rmsnorm_fwd(x, g, name):
    t, d = x.shape
    tm = _pick(t, (512, 256, 128))

    def body(x_ref, g_ref, o_ref):
        xv = x_ref[...]
        r = lax.rsqrt(jnp.mean(xv * xv, axis=-1, keepdims=True) + EPS)
        o_ref[...] = ((xv * r) * g_ref[...]).astype(BF16)

    return pl.pallas_call(
        body, name=name, grid=(t // tm,),
        in_specs=[pl.BlockSpec((tm, d), lambda i: (i, 0)), pl.BlockSpec((1, d), lambda i: (0, 0))],
        out_specs=pl.BlockSpec((tm, d), lambda i: (i, 0)),
        out_shape=jax.ShapeDtypeStruct((t, d), BF16),
        compiler_params=_params("parallel"),
    )(x, g)


def _proj_residual_norm(a, w, res, g, name):
    t, k = a.shape
    d = w.shape[1]
    tm, _, _ = _mm_tiles(t, d, k, a.dtype.itemsize, w.dtype.itemsize, 4 + 2, True)

    def body(a_ref, w_ref, r_ref, g_ref, x_ref, h_ref):
        xv = r_ref[...] + _nn(a_ref[...], w_ref[...])
        x_ref[...] = xv
        r = lax.rsqrt(jnp.mean(xv * xv, axis=-1, keepdims=True) + EPS)
        h_ref[...] = ((xv * r) * g_ref[...]).astype(BF16)

    row = pl.BlockSpec((tm, d), lambda i: (i, 0))
    return pl.pallas_call(
        body, name=name, grid=(t // tm,),
        in_specs=[pl.BlockSpec((tm, k), lambda i: (i, 0)), pl.BlockSpec((k, d), lambda i: (0, 0)), row,
                  pl.BlockSpec((1, d), lambda i: (0, 0))],
        out_specs=[row, row],
        out_shape=[jax.ShapeDtypeStruct((t, d), F32), jax.ShapeDtypeStruct((t, d), BF16)],
        compiler_params=_params("parallel"),
    )(a, w, res, g)


def _proj_norm_bwd(a, w, x, g, dres, name, add=None, with_bf16=False, after=None):
    t, k = a.shape
    d = w.shape[1]
    has_add = add is not None
    tm, _, tk = _mm_tiles(t, d, k, a.dtype.itemsize, w.dtype.itemsize, 4 + 4 + 4 + (2 if with_bf16 else 0), has_add)
    if tk == k:
        tm = min(tm, 512)
    else:
        tm, tk = _pick(t, (1024, 512, 256, 128)), min(tk, 1024)
    nk = k // tk

    def body(*refs):
        a_ref, w_ref, x_ref, g_ref, dres_ref = refs[:5]
        rest = refs[5 + has_add + (after is not None):]
        dx_ref, dg_ref = rest[:2]
        i, kk = pl.program_id(0), pl.program_id(1)

        @pl.when(jnp.logical_and(i == 0, kk == 0))
        def _():
            dg_ref[...] = jnp.zeros_like(dg_ref)

        part = _nn(a_ref[...], w_ref[...])
        if nk > 1:
            acc = rest[-1]

            @pl.when(kk == 0)
            def _():
                acc[...] = jnp.zeros_like(acc)

            acc[...] += part

        @pl.when(kk == nk - 1)
        def _():
            dhv = part if nk == 1 else acc[...]
            if has_add:
                dhv = dhv + refs[5][...]
            xv = x_ref[...]
            r = lax.rsqrt(jnp.mean(xv * xv, axis=-1, keepdims=True) + EPS)
            xhat = xv * r
            dyg = dhv * g_ref[...]
            dx = dres_ref[...] + r * (dyg - xhat * jnp.mean(dyg * xhat, axis=-1, keepdims=True))
            dx_ref[...] = dx
            if with_bf16:
                rest[2][...] = dx.astype(BF16)
            dg_ref[...] += jnp.sum(dhv * xhat, axis=0, keepdims=True)

    row = pl.BlockSpec((tm, d), lambda i, kk: (i, 0))
    vec = pl.BlockSpec((1, d), lambda i, kk: (0, 0))
    in_specs = [pl.BlockSpec((tm, tk), lambda i, kk: (i, kk)), pl.BlockSpec((tk, d), lambda i, kk: (kk, 0)),
                row, vec, row] + has_add * [row]
    args = (a, w, x, g, dres) + has_add * (add,)
    if after is not None:
        in_specs, args = in_specs + [pl.BlockSpec(memory_space=pl.ANY)], args + (after,)
    return pl.pallas_call(
        body, name=name, grid=(t // tm, nk), in_specs=in_specs, out_specs=[row, vec] + with_bf16 * [row],
        out_shape=[jax.ShapeDtypeStruct((t, d), F32), jax.ShapeDtypeStruct((1, d), F32)]
        + with_bf16 * [jax.ShapeDtypeStruct((t, d), BF16)],
        scratch_shapes=[] if nk == 1 else [pltpu.VMEM((tm, d), F32)],
        compiler_params=_params("arbitrary", "arbitrary"),
    )(*args)


def _down_proj_loss_head(act, w_down, res, g, target, name):
    t, k = act.shape
    d = w_down.shape[1]
    tm, _, _ = _mm_tiles(t, d, k, act.dtype.itemsize, w_down.dtype.itemsize, 4 + 2, True)
    tm = min(tm, 512)

    def body(a_ref, w_ref, r_ref, g_ref, t_ref, loss_ref, dx_ref, dg_ref, dxb_ref):
        @pl.when(pl.program_id(0) == 0)
        def _():
            dg_ref[...] = jnp.zeros_like(dg_ref)
            loss_ref[...] = jnp.zeros_like(loss_ref)

        xv = r_ref[...] + _nn(a_ref[...], w_ref[...])
        gv = g_ref[...]
        r = lax.rsqrt(jnp.mean(xv * xv, axis=-1, keepdims=True) + EPS)
        xhat = xv * r
        err = xhat * gv - t_ref[...]
        loss_ref[...] += jnp.sum(err * err) * (0.5 / d)
        dy = err * (1.0 / d)
        dyg = dy * gv
        dx = r * (dyg - xhat * jnp.mean(dyg * xhat, axis=-1, keepdims=True))
        dx_ref[...] = dx
        dxb_ref[...] = dx.astype(BF16)
        dg_ref[...] += jnp.sum(dy * xhat, axis=0, keepdims=True)

    row = pl.BlockSpec((tm, d), lambda i: (i, 0))
    vec = pl.BlockSpec((1, d), lambda i: (0, 0))
    return pl.pallas_call(
        body, name=name, grid=(t // tm,),
        in_specs=[pl.BlockSpec((tm, k), lambda i: (i, 0)), pl.BlockSpec((k, d), lambda i: (0, 0)), row, vec, row],
        out_specs=[pl.BlockSpec((1, LANES), lambda i: (0, 0)), row, vec, row],
        out_shape=[jax.ShapeDtypeStruct((1, LANES), F32), jax.ShapeDtypeStruct((t, d), F32),
                   jax.ShapeDtypeStruct((1, d), F32), jax.ShapeDtypeStruct((t, d), BF16)],
        compiler_params=_params("arbitrary"),
    )(act, w_down, res, g, target)


CONV_HALO = 8
CONV_ROWS = 64


def _conv_taps(window, wv, bv):
    acc = bv + wv[SSM_CONV - 1:SSM_CONV, :] * window(0)
    for sh in range(1, SSM_CONV):
        kidx = SSM_CONV - 1 - sh
        acc = acc + wv[kidx:kidx + 1, :] * window(sh)
    return acc


def _conv_fwd(u, w, bias, name):
    b, s, c = u.shape
    rows = CONV_ROWS

    def body(u_ref, w_ref, b_ref, o_ref, ext):
        ext[0:CONV_HALO, :] = jnp.zeros((CONV_HALO, LANES), F32)
        ext[CONV_HALO:, :] = u_ref[...]
        wv, bv = w_ref[...], b_ref[...]
        for r0 in range(0, s, rows):
            acc = _conv_taps(lambda sh: ext[CONV_HALO + r0 - sh:CONV_HALO + r0 - sh + rows, :], wv, bv)
            o_ref[r0:r0 + rows, :] = acc * _sigmoid(acc)

    strip = pl.BlockSpec((None, s, LANES), lambda bi, j: (bi, 0, j))
    return pl.pallas_call(
        body, name=name, grid=(b, c // LANES),
        in_specs=[strip, pl.BlockSpec((SSM_CONV, LANES), lambda bi, j: (0, j)),
                  pl.BlockSpec((1, LANES), lambda bi, j: (0, j))],
        out_specs=strip, out_shape=jax.ShapeDtypeStruct((b, s, c), F32),
        scratch_shapes=[pltpu.VMEM((CONV_HALO + s, LANES), F32)],
        compiler_params=_params("parallel", "parallel"),
    )(u, w, bias)


def _conv_bwd(u, dout, w, bias, dproj, name):
    b, s, c = u.shape
    rows = CONV_ROWS

    def fold(v):
        return jnp.sum(v.reshape(rows // CONV_HALO, CONV_HALO, LANES), axis=0)

    def body(u_ref, d_ref, w_ref, b_ref, buf_ref, du_ref, dw_ref, db_ref, ext, dpre):
        @pl.when(pl.program_id(1) == 0)
        def _():
            dw_ref[...] = jnp.zeros_like(dw_ref)
            db_ref[...] = jnp.zeros_like(db_ref)

        ext[0:CONV_HALO, :] = jnp.zeros((CONV_HALO, LANES), F32)
        ext[CONV_HALO:, :] = u_ref[...]
        dpre[s:, :] = jnp.zeros((CONV_HALO, LANES), F32)
        wv, bv = w_ref[...], b_ref[...]
        sums = [jnp.zeros((CONV_HALO, LANES), F32)] * (SSM_CONV + 1)
        for r0 in range(0, s, rows):
            window = lambda sh: ext[CONV_HALO + r0 - sh:CONV_HALO + r0 - sh + rows, :]
            acc = _conv_taps(window, wv, bv)
            sg = _sigmoid(acc)
            dp = d_ref[r0:r0 + rows, :] * (sg * (1.0 + acc * (1.0 - sg)))
            dpre[r0:r0 + rows, :] = dp
            taps = [sums[SSM_CONV - 1 - sh] + fold(dp * window(sh)) for sh in range(SSM_CONV)]
            sums = taps[::-1] + [sums[SSM_CONV] + fold(dp)]
        for r0 in range(0, s, rows):
            du = wv[SSM_CONV - 1:SSM_CONV, :] * dpre[r0:r0 + rows, :]
            for sh in range(1, SSM_CONV):
                kidx = SSM_CONV - 1 - sh
                du = du + wv[kidx:kidx + 1, :] * dpre[r0 + sh:r0 + sh + rows, :]
            du_ref[r0:r0 + rows, :] = du.astype(BF16)
        for kidx in range(SSM_CONV):
            dw_ref[kidx:kidx + 1, :] += jnp.sum(sums[kidx], axis=0, keepdims=True)
        db_ref[...] += jnp.sum(sums[SSM_CONV], axis=0, keepdims=True)

    strip = pl.BlockSpec((None, s, LANES), lambda j, bi: (bi, 0, j))
    taps = pl.BlockSpec((SSM_CONV, LANES), lambda j, bi: (0, j))
    vec = pl.BlockSpec((1, LANES), lambda j, bi: (0, j))
    du_cols = pl.BlockSpec((None, s, LANES), lambda j, bi: (bi, 0, DPROJ_COLS["xbc"] // LANES + j))
    return pl.pallas_call(
        body, name=name, grid=(c // LANES, b),
        in_specs=[strip, strip, taps, vec, pl.BlockSpec(memory_space=pl.ANY)], out_specs=[du_cols, taps, vec],
        input_output_aliases={4: 0},
        out_shape=[jax.ShapeDtypeStruct(dproj.shape, dproj.dtype), jax.ShapeDtypeStruct((SSM_CONV, c), F32),
                   jax.ShapeDtypeStruct((1, c), F32)],
        scratch_shapes=[pltpu.VMEM((CONV_HALO + s, LANES), F32), pltpu.VMEM((s + CONV_HALO, LANES), F32)],
        compiler_params=_params("parallel", "arbitrary"),
    )(u, dout, w, bias, dproj)


def _ssd_chunk_terms(dtr_ref, bias_ref, alog_ref):
    q = SSM_CHUNK
    dt = _softplus(dtr_ref[...] + bias_ref[...])
    a_neg = -jnp.exp(alog_ref[...])
    row = lax.broadcasted_iota(jnp.int32, (q, q), 0)
    col = lax.broadcasted_iota(jnp.int32, (q, q), 1)
    lower = row >= col
    s = _mask_nn(lower, dt * a_neg)
    return dt, a_neg, s, s.T, lower


def _head_masks():
    heads = jnp.arange(LANES)[:, None]
    chans = jnp.arange(SSM_D_INNER)[None, :]
    to_channels = (chans // SSM_HEAD_DIM == heads).astype(BF16)
    return to_channels, to_channels.T


def _per_channel(v, to_channels):
    hi = v.astype(BF16)
    lo = (v - hi.astype(F32)).astype(BF16)
    return _nn(hi, to_channels) + _nn(lo, to_channels)


def _per_head(v, to_heads):
    hi = v.astype(BF16)
    lo = (v - hi.astype(F32)).astype(BF16)
    return _nn(hi, to_heads) + _nn(lo, to_heads)


def _decay_terms_per_channel(dt, s_col, to_channels):
    q = SSM_CHUNK
    tot = s_col[q - 1:q, :]
    stacked = jnp.concatenate([dt, jnp.exp(s_col), jnp.exp(tot - s_col)], axis=0)
    wide = _per_channel(stacked, to_channels)
    dtx, esx, decx = wide[:q], wide[q:2 * q], wide[2 * q:]
    return dtx, esx, decx, esx[0:1, :] * decx[0:1, :]


SSM_PAIRS_PER_GROUP = SSM_HEADS_PER_GROUP // 2
SSM_GROUP_CHANNELS = SSM_HEADS_PER_GROUP * SSM_HEAD_DIM


def _split_pair(v):
    first = lax.broadcasted_iota(jnp.int32, v.shape, 1) < SSM_HEAD_DIM
    return jnp.concatenate([jnp.where(first, v, 0.0), jnp.where(first, 0.0, v)], axis=0)


def _ssd_fwd(xc, dtr, dt_bias, a_log, dskx, to_channels, name):
    b, s, _ = xc.shape
    q = SSM_CHUNK
    nc = s // q
    n, gc = SSM_D_STATE, SSM_GROUP_CHANNELS

    def body(xc_ref, dtr_ref, bias_ref, alog_ref, dsk_ref, tc_ref, y_ref, hs_ref, h_scr):
        @pl.when(pl.program_id(1) == 0)
        def _():
            h_scr[...] = jnp.zeros_like(h_scr)

        dt, _, s_col, s_row, lower = _ssd_chunk_terms(dtr_ref, bias_ref, alog_ref)
        dtx, esx, decx, etotx = _decay_terms_per_channel(dt, s_col, tc_ref[...])
        x = xc_ref[:, :SSM_D_INNER]
        xdt = x * dtx
        xdec = xdt * decx
        skip = dsk_ref[...] * x
        for g in range(SSM_N_GROUPS):
            bg = xc_ref[:, SSM_D_INNER + n * g:SSM_D_INNER + n * (g + 1)].astype(BF16)
            cg = xc_ref[:, SSM_D_INNER + n * (SSM_N_GROUPS + g):SSM_D_INNER + n * (SSM_N_GROUPS + g + 1)].astype(BF16)
            gsl = slice(gc * g, gc * (g + 1))
            gm = _nt(cg, bg)
            hgt = h_scr[:, gsl]
            hs_ref[:, gsl] = hgt
            y_off = esx[:, gsl] * _nn(cg, hgt)
            h_scr[:, gsl] = etotx[:, gsl] * hgt + _tn(bg, xdec[:, gsl])
            for k in range(SSM_PAIRS_PER_GROUP):
                h0 = g * SSM_HEADS_PER_GROUP + 2 * k
                lo = gc * g + LANES * k
                ms = []
                for h in (h0, h0 + 1):
                    lm = jnp.exp(jnp.where(lower, s_col[:, h:h + 1] - s_row[h:h + 1, :], NEG_INF))
                    ms.append((gm * lm).astype(BF16))
                y_diag = _nn(jnp.concatenate(ms, axis=1), _split_pair(xdt[:, lo:lo + LANES]))
                y_ref[:, lo:lo + LANES] = y_diag + y_off[:, LANES * k:LANES * (k + 1)] + skip[:, lo:lo + LANES]

    vec = pl.BlockSpec((1, LANES), lambda bi, c: (0, 0))
    return pl.pallas_call(
        body, name=name, grid=(b, nc),
        in_specs=[pl.BlockSpec((None, q, SSM_CONV_DIM), lambda bi, c: (bi, c, 0)),
                  pl.BlockSpec((None, q, LANES), lambda bi, c: (bi, c, 0)), vec, vec,
                  pl.BlockSpec((1, SSM_D_INNER), lambda bi, c: (0, 0)),
                  pl.BlockSpec((LANES, SSM_D_INNER), lambda bi, c: (0, 0))],
        out_specs=[pl.BlockSpec((None, q, SSM_D_INNER), lambda bi, c: (bi, c, 0)),
                   pl.BlockSpec((None, None, n, SSM_D_INNER), lambda bi, c: (bi, c, 0, 0))],
        out_shape=[jax.ShapeDtypeStruct((b, s, SSM_D_INNER), F32),
                   jax.ShapeDtypeStruct((b, nc, n, SSM_D_INNER), F32)],
        scratch_shapes=[pltpu.VMEM((n, SSM_D_INNER), F32)],
        compiler_params=_params("parallel", "arbitrary"),
    )(xc, dtr, dt_bias, a_log, dskx, to_channels)


def _ssd_bwd(xc, dtr, dy, hs, dt_bias, a_log, dskx, to_channels, to_heads, dproj, name):
    b, s, _ = xc.shape
    q = SSM_CHUNK
    nc = s // q
    n, gc = SSM_D_STATE, SSM_GROUP_CHANNELS

    def colsum(v):
        return jnp.sum(v, axis=0, keepdims=True)

    def body(xc_ref, dtr_ref, dy_ref, hs_ref, bias_ref, alog_ref, dsk_ref, tc_ref, th_ref, buf_ref,
             dxc_ref, ddtr_ref, dalog_ref, ddsk_ref, dbias_ref, dh_scr, dxs_scr, dxd_scr, w_scr, dst_scr, rows_scr):
        ci = pl.program_id(1)

        @pl.when(ci == 0)
        def _():
            dh_scr[...] = jnp.zeros_like(dh_scr)

        @pl.when(jnp.logical_and(pl.program_id(0) == 0, ci == 0))
        def _():
            dalog_ref[...] = jnp.zeros_like(dalog_ref)
            ddsk_ref[...] = jnp.zeros_like(ddsk_ref)
            dbias_ref[...] = jnp.zeros_like(dbias_ref)
            dst_scr[...] = jnp.zeros_like(dst_scr)

        dt, a_neg, s_col, s_row, lower = _ssd_chunk_terms(dtr_ref, bias_ref, alog_ref)
        upper = jnp.logical_not(lower) | (lax.broadcasted_iota(jnp.int32, (q, q), 0)
                                          == lax.broadcasted_iota(jnp.int32, (q, q), 1))
        dtx, esx, decx, etotx = _decay_terms_per_channel(dt, s_col, tc_ref[...])
        x = xc_ref[:, :SSM_D_INNER]
        dyv = dy_ref[...]
        xdt = x * dtx
        xdec = xdt * decx
        dw = esx * dyv
        rows_scr[...] = jnp.zeros_like(rows_scr)
        for g in range(SSM_N_GROUPS):
            b_lo = SSM_D_INNER + n * g
            c_lo = SSM_D_INNER + n * (SSM_N_GROUPS + g)
            bg = xc_ref[:, b_lo:b_lo + n].astype(BF16)
            cg = xc_ref[:, c_lo:c_lo + n].astype(BF16)
            gsl = slice(gc * g, gc * (g + 1))
            gm = _nt(cg, bg)
            gmt = _nt(bg, cg)
            hgt = hs_ref[:, gsl]
            dhgt = dh_scr[:, gsl]
            w_scr[:, gsl] = _nn(cg, hgt)
            dcg = _nt(dw[:, gsl], hgt)
            dxs = decx[:, gsl] * _nn(bg, dhgt)
            dxs_scr[:, gsl] = dxs
            dbg = _nt(xdec[:, gsl], dhgt)
            rows_scr[2:3, gsl] = colsum(dhgt * hgt)
            dh_scr[:, gsl] = _tn(cg, dw[:, gsl]) + etotx[:, gsl] * dhgt
            dg = jnp.zeros((q, q), F32)
            dgt = jnp.zeros((q, q), F32)
            for k in range(SSM_PAIRS_PER_GROUP):
                h0 = g * SSM_HEADS_PER_GROUP + 2 * k
                lo = gc * g + LANES * k
                xp = xdt[:, lo:lo + LANES]
                dyp = dyv[:, lo:lo + LANES]
                dy2 = _split_pair(dyp)
                dm2 = _nt(dy2, xp)
                dmt2 = _nt(_split_pair(xp), dyp)
                mts = []
                for i, h in enumerate((h0, h0 + 1)):
                    lm = jnp.exp(jnp.where(lower, s_col[:, h:h + 1] - s_row[h:h + 1, :], NEG_INF))
                    lmt = jnp.exp(jnp.where(upper, s_row[h:h + 1, :] - s_col[:, h:h + 1], NEG_INF))
                    dm = dm2[q * i:q * (i + 1), :]
                    dmt = dmt2[q * i:q * (i + 1), :]
                    dg = dg + dm * lm
                    dgt = dgt + dmt * lmt
                    mt = gmt * lmt
                    dst_scr[h:h + 1, :] = colsum(dmt * mt) - colsum(dm * (gm * lm))
                    mts.append(mt.astype(BF16))
                dxd_scr[:, lo:lo + LANES] = _nn(jnp.concatenate(mts, axis=1), dy2)
            dxc_ref[:, b_lo:b_lo + n] = dbg + _nn(dgt, cg)
            dxc_ref[:, c_lo:c_lo + n] = dcg + _nn(dg, bg)
        dxs = dxs_scr[...]
        dxdt = dxd_scr[...] + dxs
        dxc_ref[:, :SSM_D_INNER] = dxdt * dtx + dsk_ref[...] * dyv
        state_part = xdt * dxs
        rows_scr[0:1, :] = colsum(dyv * x)
        rows_scr[1:2, :] = colsum(state_part)
        th = th_ref[...]
        per_head = _per_head(jnp.concatenate([dw * w_scr[...] - state_part, dxdt * x], axis=0), th)
        r_ds, r_dt = per_head[:q], per_head[q:]
        sums = _per_head(rows_scr[...], th)
        etot = jnp.exp(s_col[q - 1:q, :])
        dtot = sums[1:2, :] + etot * sums[2:3, :]
        last = lax.broadcasted_iota(jnp.int32, (q, LANES), 0) == q - 1
        ds = dst_scr[...].T + r_ds + jnp.where(last, dtot, 0.0)
        da = _mask_nn(upper, ds)
        ddt = da * a_neg + r_dt
        live = lax.broadcasted_iota(jnp.int32, (1, LANES), 1) < SSM_N_HEADS
        sg = _sigmoid(dtr_ref[...] + bias_ref[...])
        ddtr = jnp.where(live, ddt * sg, 0.0)
        ddtr_ref[:, :LANES] = ddtr.astype(BF16)
        ddtr_ref[:, LANES:] = jnp.zeros((q, DPROJ_DT_WIDTH - LANES), BF16)
        dalog_ref[...] += jnp.where(live, colsum(da * dt) * a_neg, 0.0)
        ddsk_ref[...] += jnp.where(live, sums[0:1, :], 0.0)
        dbias_ref[...] += colsum(ddtr)

    rev = lambda bi, c: (bi, nc - 1 - c, 0)
    vec = pl.BlockSpec((1, LANES), lambda bi, c: (0, 0))
    wide = pl.BlockSpec((None, q, SSM_D_INNER), rev)
    return pl.pallas_call(
        body, name=name, grid=(b, nc),
        in_specs=[pl.BlockSpec((None, q, SSM_CONV_DIM), rev), pl.BlockSpec((None, q, LANES), rev), wide,
                  pl.BlockSpec((None, None, n, SSM_D_INNER), lambda bi, c: (bi, nc - 1 - c, 0, 0)),
                  vec, vec, pl.BlockSpec((1, SSM_D_INNER), lambda bi, c: (0, 0)),
                  pl.BlockSpec((LANES, SSM_D_INNER), lambda bi, c: (0, 0)),
                  pl.BlockSpec((SSM_D_INNER, LANES), lambda bi, c: (0, 0)),
                  pl.BlockSpec(memory_space=pl.ANY)],
        out_specs=[pl.BlockSpec((None, q, SSM_CONV_DIM), rev),
                   pl.BlockSpec((None, q, DPROJ_DT_WIDTH),
                                lambda bi, c: (bi, nc - 1 - c, DPROJ_COLS["dt"] // DPROJ_DT_WIDTH)), vec, vec, vec],
        input_output_aliases={9: 1},
        out_shape=[jax.ShapeDtypeStruct((b, s, SSM_CONV_DIM), F32), jax.ShapeDtypeStruct(dproj.shape, dproj.dtype),
                   jax.ShapeDtypeStruct((1, LANES), F32), jax.ShapeDtypeStruct((1, LANES), F32),
                   jax.ShapeDtypeStruct((1, LANES), F32)],
        scratch_shapes=[pltpu.VMEM((n, SSM_D_INNER), F32)] + [pltpu.VMEM((q, SSM_D_INNER), F32)] * 3
        + [pltpu.VMEM((LANES, q), F32), pltpu.VMEM((8, SSM_D_INNER), F32)],
        compiler_params=_params("arbitrary", "arbitrary"),
    )(xc, dtr, dy, hs, dt_bias, a_log, dskx, to_channels, to_heads, dproj)


SSM_GROUP_WIDTH = SSM_D_INNER // SSM_N_GROUPS


def _gate_norm_fwd(y, z, w, name):
    t, d = y.shape
    tm = _pick(t, (256, 128))

    def body(y_ref, z_ref, w_ref, o_ref):
        for g in range(SSM_N_GROUPS):
            sl = slice(SSM_GROUP_WIDTH * g, SSM_GROUP_WIDTH * (g + 1))
            zv = z_ref[:, sl]
            u = y_ref[:, sl] * (zv * _sigmoid(zv))
            r = lax.rsqrt(jnp.mean(u * u, axis=-1, keepdims=True) + EPS)
            o_ref[:, sl] = ((u * r) * w_ref[:, sl]).astype(BF16)

    row = pl.BlockSpec((tm, d), lambda i: (i, 0))
    return pl.pallas_call(
        body, name=name, grid=(t // tm,),
        in_specs=[row, row, pl.BlockSpec((1, d), lambda i: (0, 0))], out_specs=row,
        out_shape=jax.ShapeDtypeStruct((t, d), BF16),
        compiler_params=_params("parallel"),
    )(y, z, w)


def _ssm_out_dx_gate_norm_bwd(dys, w_ssm_out, y, z, w, dproj, name):
    t, d = y.shape
    k = dys.shape[1]
    gw = SSM_GROUP_WIDTH
    tm = _pick(t, (512, 256, 128))

    def body(dys_ref, ws_ref, y_ref, z_ref, w_ref, buf_ref, dy_ref, dz_ref, dw_ref):
        @pl.when(pl.program_id(0) == 0)
        def _():
            dw_ref[...] = jnp.zeros_like(dw_ref)

        dout = _nt(dys_ref[...], ws_ref[...])
        for g in range(SSM_N_GROUPS):
            sl = slice(gw * g, gw * (g + 1))
            zv = z_ref[:, sl]
            yv = y_ref[:, sl]
            sg = _sigmoid(zv)
            silu = zv * sg
            u = yv * silu
            r = lax.rsqrt(jnp.mean(u * u, axis=-1, keepdims=True) + EPS)
            uh = u * r
            dov = dout[:, sl]
            dw_ref[:, sl] += jnp.sum(dov * uh, axis=0, keepdims=True)
            dyg = dov * w_ref[:, sl]
            du = r * (dyg - uh * jnp.mean(dyg * uh, axis=-1, keepdims=True))
            dy_ref[:, sl] = du * silu
            dz_ref[:, sl] = (du * yv * (sg * (1.0 + zv * (1.0 - sg)))).astype(BF16)

    row = pl.BlockSpec((tm, d), lambda i: (i, 0))
    vec = pl.BlockSpec((1, d), lambda i: (0, 0))
    z_cols = pl.BlockSpec((tm, d), lambda i: (i, DPROJ_COLS["z"] // d))
    return pl.pallas_call(
        body, name=name, grid=(t // tm,),
        in_specs=[pl.BlockSpec((tm, k), lambda i: (i, 0)), pl.BlockSpec((d, k), lambda i: (0, 0)), row, row, vec,
                  pl.BlockSpec(memory_space=pl.ANY)],
        out_specs=[row, z_cols, vec],
        out_shape=[jax.ShapeDtypeStruct((t, d), F32), jax.ShapeDtypeStruct(dproj.shape, dproj.dtype),
                   jax.ShapeDtypeStruct((1, d), F32)],
        input_output_aliases={5: 1},
        compiler_params=_params("arbitrary"),
    )(dys, w_ssm_out, y, z, w, dproj)


def _rope_tables(s):
    half = ATT_HEAD_DIM // 2
    inv = ROPE_THETA ** (-jnp.arange(half, dtype=F32) / half)
    ang = jnp.arange(s).astype(F32)[:, None] * inv[None, :]
    cos, sin = jnp.cos(ang), jnp.sin(ang)
    return jnp.concatenate([cos, cos], axis=-1), jnp.concatenate([-sin, sin], axis=-1)


ATT_TILE = 256


def _by_residue_spec(r, width):
    return pl.BlockSpec((None, r, ATT_TILE // r, width), lambda bi, i: (bi, 0, i, 0))


def _to_residues(tile, stage, r, store):
    if r == 1:
        store(0, tile)
        return
    stage[...] = tile
    for ri in range(r):
        store(ri, stage[pl.ds(ri, tile.shape[0] // r, stride=r), :])


def _from_residues(load, stage, r):
    if r == 1:
        return load(0)
    for ri in range(r):
        stage[pl.ds(ri, ATT_TILE // r, stride=r), :] = load(ri)
    return stage[...]


QKV_ROWS = 1024
QKV_COLS = 768


def _qkv_proj_rope(h, w_qkv_t, cosf, sinf, b, s, name):
    t, k = h.shape
    tm, d, gw = QKV_ROWS, ATT_HEAD_DIM, ATT_OUT_DIM
    per_seq = s // tm

    def body(h_ref, w_ref, c_ref, s_ref, *rest):
        outs, stage = rest[:-1], rest[-1]
        cv, sv = c_ref[...], s_ref[...]
        hv = h_ref[...]
        for lo in range(0, ATT_QKV_DIM, QKV_COLS):
            acc = _nt(hv, w_ref[lo:lo + QKV_COLS, :])
            for hh in range(QKV_COLS // d):
                kind, head = divmod(lo // d + hh, ATT_N_HEADS)
                gi, j = divmod(head, ATT_HEADS_PER_GROUP)
                dst = slice(kind * gw + d * j, kind * gw + d * (j + 1))
                tv = acc[:, d * hh:d * (hh + 1)]
                if kind < 2:
                    tv = tv * cv + pltpu.roll(tv, d // 2, 1) * sv

                def store(ri, rows, o_ref=outs[gi], dst=dst):
                    o_ref[ri, :, dst] = rows.astype(BF16)

                _to_residues(tv, stage, ATT_DILATIONS[gi], store)

    tab = pl.BlockSpec((tm, d), lambda i: (i % per_seq, 0))
    return pl.pallas_call(
        body, name=name, grid=(t // tm,),
        in_specs=[pl.BlockSpec((tm, k), lambda i: (i, 0)), pl.BlockSpec((ATT_QKV_DIM, k), lambda i: (0, 0)), tab, tab],
        out_specs=[pl.BlockSpec((None, r, tm // r, 3 * gw), lambda i: (i // per_seq, 0, i % per_seq, 0))
                   for r in ATT_DILATIONS],
        out_shape=[jax.ShapeDtypeStruct((b, r, s // r, 3 * gw), BF16) for r in ATT_DILATIONS],
        scratch_shapes=[pltpu.VMEM((tm, d), F32)],
        compiler_params=_params("parallel"),
    )(h, w_qkv_t, cosf, sinf)


def _rope_bwd(dq, dk, dv, cosf, sinf, dproj, name):
    n_pat = len(ATT_DILATIONS)
    b, _, s, gw = dq[0].shape
    ts, d = ATT_TILE, ATT_HEAD_DIM

    def body(*refs):
        ins, (c_ref, s_ref, _, o_ref, stage) = refs[:3 * n_pat], refs[3 * n_pat:]
        cv, sv = c_ref[...], s_ref[...]
        for kind in range(3):
            for gi, r in enumerate(ATT_DILATIONS):
                src = ins[kind * n_pat + gi]
                for j in range(ATT_HEADS_PER_GROUP):
                    tv = _from_residues(lambda ri, src=src, j=j: src[ri, :, d * j:d * (j + 1)], stage, r)
                    if kind < 2:
                        tv = tv * cv + pltpu.roll(tv * sv, d // 2, 1)
                    lo = d * (kind * ATT_N_HEADS + gi * ATT_HEADS_PER_GROUP + j)
                    o_ref[:, lo:lo + d] = tv.astype(BF16)

    tab = pl.BlockSpec((ts, d), lambda bi, i: (i, 0))
    parts = [_by_residue_spec(r, gw) for r in ATT_DILATIONS]
    return pl.pallas_call(
        body, name=name, grid=(b, s // ts), in_specs=parts * 3 + [tab, tab, pl.BlockSpec(memory_space=pl.ANY)],
        out_specs=pl.BlockSpec((None, ts, ATT_QKV_DIM), lambda bi, i: (bi, i, DPROJ_COLS["qkv"] // ATT_QKV_DIM)),
        out_shape=jax.ShapeDtypeStruct(dproj.shape, dproj.dtype),
        input_output_aliases={3 * n_pat + 2: 0},
        scratch_shapes=[pltpu.VMEM((ts, d), F32)],
        compiler_params=_params("parallel", "parallel"),
    )(*dq, *dk, *dv, cosf, sinf, dproj)


ATT_SCALE = ATT_HEAD_DIM ** -0.5
ATT_STEP = 2 * ATT_BLOCK


def _att_spec(col):
    return pl.BlockSpec((None, None, ATT_STEP, ATT_OUT_DIM), lambda bi, ri, i: (bi, ri, i, col))


def _att_edge_spec(col, side, n_steps):
    def index(bi, ri, i):
        blk = 2 * i - 1 if side < 0 else 2 * i + 2
        return (bi, ri, jnp.clip(blk, 0, 2 * n_steps - 1), col)
    return pl.BlockSpec((None, None, ATT_BLOCK, ATT_OUT_DIM), index)


def _band_mask(shape, q_axis, has_prev):
    qi = lax.broadcasted_iota(jnp.int32, shape, q_axis)
    kj = lax.broadcasted_iota(jnp.int32, shape, 1 - q_axis)
    dist = qi + ATT_BLOCK - kj
    return (dist >= 0) & (dist <= ATT_BLOCK) & (has_prev | (kj >= ATT_BLOCK))


def _att_fwd(qkr, name):
    b, r, l, _ = qkr.shape
    nb = l // ATT_STEP
    d = ATT_HEAD_DIM

    def body(q_ref, kp_ref, k_ref, vp_ref, v_ref, o_ref, lse_ref):
        mask = _band_mask((ATT_STEP, ATT_BLOCK + ATT_STEP), 0, pl.program_id(2) > 0)
        heads = [slice(d * j, d * (j + 1)) for j in range(ATT_HEADS_PER_GROUP)]
        scores = [_nt(q_ref[:, sl], jnp.concatenate([kp_ref[:, sl], k_ref[:, sl]], axis=0)) for sl in heads]
        scores = [jnp.where(mask, sc * ATT_SCALE, NEG_INF) for sc in scores]
        tops = [jnp.max(sc, axis=-1, keepdims=True) for sc in scores]
        probs = [jnp.exp(sc - m) for sc, m in zip(scores, tops)]
        dens = [jnp.sum(pr, axis=-1, keepdims=True) for pr in probs]
        for sl, m, pr, den in zip(heads, tops, probs, dens):
            o_ref[:, sl] = _nn(pr / den, jnp.concatenate([vp_ref[:, sl], v_ref[:, sl]], axis=0))
            lse_ref[:, sl] = jnp.broadcast_to(m + jnp.log(den), (ATT_STEP, d))

    out_spec = _att_spec(0)
    return pl.pallas_call(
        body, name=name, grid=(b, r, nb),
        in_specs=[_att_spec(0), _att_edge_spec(1, -1, nb), _att_spec(1), _att_edge_spec(2, -1, nb), _att_spec(2)],
        out_specs=[out_spec, out_spec],
        out_shape=[jax.ShapeDtypeStruct((b, r, l, ATT_OUT_DIM), F32)] * 2,
        compiler_params=_params("parallel", "parallel", "parallel"),
    )(qkr, qkr, qkr, qkr, qkr)


def _att_merge(os_, lses, name):
    n_pat = len(os_)
    b, _, s, gw = os_[0].shape
    ts, d = ATT_TILE, ATT_HEAD_DIM

    def body(*refs):
        o_refs, l_refs = refs[:n_pat], refs[n_pat:2 * n_pat]
        att_ref, lse_outs, stage = refs[2 * n_pat], refs[2 * n_pat + 1:3 * n_pat + 1], refs[-1]
        for j in range(ATT_HEADS_PER_GROUP):
            sl = slice(d * j, d * (j + 1))
            ov = [_from_residues(lambda ri, g=g: o_refs[g][ri, :, sl], stage, r)
                  for g, r in enumerate(ATT_DILATIONS)]
            ls = [_from_residues(lambda ri, g=g: l_refs[g][ri, :, sl], stage, r)
                  for g, r in enumerate(ATT_DILATIONS)]
            m = functools.reduce(jnp.maximum, ls)
            es = [jnp.exp(lv - m) for lv in ls]
            tot = functools.reduce(lambda u, v: u + v, es)
            acc = (es[0] / tot) * ov[0]
            for g in range(1, n_pat):
                acc = acc + (es[g] / tot) * ov[g]
            att_ref[:, sl] = acc
            joint = m + jnp.log(tot)
            for g, r in enumerate(ATT_DILATIONS):
                def store(ri, rows, out=lse_outs[g]):
                    out[ri, :, sl] = rows
                _to_residues(joint, stage, r, store)

    parts = [_by_residue_spec(r, gw) for r in ATT_DILATIONS]
    return pl.pallas_call(
        body, name=name, grid=(b, s // ts), in_specs=parts * 2,
        out_specs=[pl.BlockSpec((None, ts, gw), lambda bi, i: (bi, i, 0))] + parts,
        out_shape=[jax.ShapeDtypeStruct((b, s, gw), F32)]
        + [jax.ShapeDtypeStruct((b, r, s // r, gw), F32) for r in ATT_DILATIONS],
        scratch_shapes=[pltpu.VMEM((ts, d), F32)],
        compiler_params=_params("parallel", "parallel"),
    )(*os_, *lses)


def _att_delta(att, datt, name):
    b, s, gw = att.shape
    ts, d = ATT_TILE, ATT_HEAD_DIM
    n_pat = len(ATT_DILATIONS)

    def body(a_ref, d_ref, *rest):
        do_outs, dl_outs, stage = rest[:n_pat], rest[n_pat:2 * n_pat], rest[-1]
        for j in range(ATT_HEADS_PER_GROUP):
            sl = slice(d * j, d * (j + 1))
            dv = d_ref[:, sl]
            delta = jnp.broadcast_to(jnp.sum(a_ref[:, sl] * dv, axis=-1, keepdims=True), (ts, d))
            for g, r in enumerate(ATT_DILATIONS):
                def store_do(ri, rows, out=do_outs[g]):
                    out[ri, :, sl] = rows.astype(BF16)

                def store_dl(ri, rows, out=dl_outs[g]):
                    out[ri, :, sl] = rows

                _to_residues(dv, stage, r, store_do)
                _to_residues(delta, stage, r, store_dl)

    row = pl.BlockSpec((None, ts, gw), lambda bi, i: (bi, i, 0))
    parts = [_by_residue_spec(r, gw) for r in ATT_DILATIONS]
    outs = pl.pallas_call(
        body, name=name, grid=(b, s // ts), in_specs=[row, row], out_specs=parts * 2,
        out_shape=[jax.ShapeDtypeStruct((b, r, s // r, gw), BF16) for r in ATT_DILATIONS]
        + [jax.ShapeDtypeStruct((b, r, s // r, gw), F32) for r in ATT_DILATIONS],
        scratch_shapes=[pltpu.VMEM((ts, d), F32)],
        compiler_params=_params("parallel", "parallel"),
    )(att, datt)
    return outs[:n_pat], outs[n_pat:]


def _att_bwd_q(qkr, datt, lse, delta, name):
    b, r, l, _ = qkr.shape
    nb = l // ATT_STEP
    d = ATT_HEAD_DIM

    def body(q_ref, kp_ref, k_ref, vp_ref, v_ref, do_ref, lse_ref, dl_ref, dq_ref):
        mask = _band_mask((ATT_STEP, ATT_BLOCK + ATT_STEP), 0, pl.program_id(2) > 0)
        heads = [slice(d * j, d * (j + 1)) for j in range(ATT_HEADS_PER_GROUP)]
        kcats = [jnp.concatenate([kp_ref[:, sl], k_ref[:, sl]], axis=0) for sl in heads]
        scores = [_nt(q_ref[:, sl], kcat) for sl, kcat in zip(heads, kcats)]
        dps = [_nt(do_ref[:, sl], jnp.concatenate([vp_ref[:, sl], v_ref[:, sl]], axis=0)) for sl in heads]
        probs = [jnp.exp(jnp.where(mask, sc * ATT_SCALE - lse_ref[:, sl.start:sl.start + 1], NEG_INF))
                 for sl, sc in zip(heads, scores)]
        dscs = [pr * (dp - dl_ref[:, sl.start:sl.start + 1]) for sl, pr, dp in zip(heads, probs, dps)]
        for sl, dsc, kcat in zip(heads, dscs, kcats):
            dq_ref[:, sl] = _nn(dsc, kcat) * ATT_SCALE

    tok = _att_spec(0)
    return pl.pallas_call(
        body, name=name, grid=(b, r, nb),
        in_specs=[_att_spec(0), _att_edge_spec(1, -1, nb), _att_spec(1), _att_edge_spec(2, -1, nb), _att_spec(2),
                  tok, tok, tok],
        out_specs=tok,
        out_shape=jax.ShapeDtypeStruct((b, r, l, ATT_OUT_DIM), F32),
        compiler_params=_params("parallel", "parallel", "parallel"),
    )(qkr, qkr, qkr, qkr, qkr, datt, lse, delta)


def _att_bwd_kv(qkr, datt, lse, delta, name):
    b, r, l, _ = qkr.shape
    nb = l // ATT_STEP
    d = ATT_HEAD_DIM

    def body(k_ref, v_ref, q_ref, qn_ref, do_ref, don_ref, lse_ref, lsen_ref, dl_ref, dln_ref, dk_ref, dv_ref):
        shape = (ATT_STEP, ATT_STEP + ATT_BLOCK)
        kj = lax.broadcasted_iota(jnp.int32, shape, 0)
        qi = lax.broadcasted_iota(jnp.int32, shape, 1)
        dist = qi - kj
        has_next = pl.program_id(2) < nb - 1
        mask = (dist >= 0) & (dist <= ATT_BLOCK) & (has_next | (qi < ATT_STEP))
        def per_query(own_ref, next_ref, sl):
            return jnp.tile(jnp.concatenate([own_ref[:, sl], next_ref[:, sl]], axis=0).T, (ATT_STEP // d, 1))

        heads = [slice(d * j, d * (j + 1)) for j in range(ATT_HEADS_PER_GROUP)]
        qcats = [jnp.concatenate([q_ref[:, sl], qn_ref[:, sl]], axis=0) for sl in heads]
        docats = [jnp.concatenate([do_ref[:, sl], don_ref[:, sl]], axis=0) for sl in heads]
        scores = [_nt(k_ref[:, sl], qcat) for sl, qcat in zip(heads, qcats)]
        dps = [_nt(v_ref[:, sl], docat) for sl, docat in zip(heads, docats)]
        probs = [jnp.exp(jnp.where(mask, sc * ATT_SCALE - per_query(lse_ref, lsen_ref, sl), NEG_INF))
                 for sl, sc in zip(heads, scores)]
        for sl, pr, docat in zip(heads, probs, docats):
            dv_ref[:, sl] = _nn(pr, docat)
        dscs = [pr * (dp - per_query(dl_ref, dln_ref, sl)) for sl, pr, dp in zip(heads, probs, dps)]
        for sl, dsc, qcat in zip(heads, dscs, qcats):
            dk_ref[:, sl] = _nn(dsc, qcat) * ATT_SCALE

    tok, tok_n = _att_spec(0), _att_edge_spec(0, 1, nb)
    return pl.pallas_call(
        body, name=name, grid=(b, r, nb),
        in_specs=[_att_spec(1), _att_spec(2), _att_spec(0), _att_edge_spec(0, 1, nb),
                  tok, tok_n, tok, tok_n, tok, tok_n],
        out_specs=[tok, tok],
        out_shape=[jax.ShapeDtypeStruct((b, r, l, ATT_OUT_DIM), F32)] * 2,
        compiler_params=_params("parallel", "parallel", "parallel"),
    )(qkr, qkr, qkr, qkr, datt, datt, lse, lse, delta, delta)


def _att_out_proj_mix(att, w_att_t, gl, bg, ys, name):
    t, k = att.shape
    d = w_att_t.shape[0]
    tm = _pick(t, (512, 256, 128))

    def body(a_ref, w_ref, gl_ref, bg_ref, ys_ref, ya_ref, o_ref):
        ya = _nt(a_ref[...], w_ref[...])
        ya_ref[...] = ya
        g0 = _sigmoid(gl_ref[:, :d] + bg_ref[:, :d])
        g1 = _sigmoid(gl_ref[:, d:] + bg_ref[:, d:])
        o_ref[...] = (g0 * ys_ref[...] + g1 * ya).astype(BF16)

    row = pl.BlockSpec((tm, d), lambda i: (i, 0))
    return pl.pallas_call(
        body, name=name, grid=(t // tm,),
        in_specs=[pl.BlockSpec((tm, k), lambda i: (i, 0)), pl.BlockSpec((d, k), lambda i: (0, 0)),
                  pl.BlockSpec((tm, 2 * d), lambda i: (i, 0)), pl.BlockSpec((1, 2 * d), lambda i: (0, 0)), row],
        out_specs=[row, row],
        out_shape=[jax.ShapeDtypeStruct((t, d), F32), jax.ShapeDtypeStruct((t, d), BF16)],
        compiler_params=_params("parallel"),
    )(att, w_att_t, gl, bg, ys)


def _mix_out_dx_mix_bwd(dx, w_mix, gl, bg, ys, ya, name):
    t, d = ys.shape
    tm = _pick(t, (512, 256, 128))

    def body(dx_ref, w_ref, gl_ref, bg_ref, ys_ref, ya_ref, dys_ref, dya_ref, dgl_ref, dbg_ref):
        @pl.when(pl.program_id(0) == 0)
        def _():
            dbg_ref[...] = jnp.zeros_like(dbg_ref)

        dm = _nt(dx_ref[...], w_ref[...])
        g0 = _sigmoid(gl_ref[:, :d] + bg_ref[:, :d])
        g1 = _sigmoid(gl_ref[:, d:] + bg_ref[:, d:])
        dys_ref[...] = (dm * g0).astype(BF16)
        dya_ref[...] = (dm * g1).astype(BF16)
        d0 = dm * ys_ref[...] * (g0 * (1.0 - g0))
        d1 = dm * ya_ref[...] * (g1 * (1.0 - g1))
        dgl_ref[:, :d] = d0.astype(BF16)
        dgl_ref[:, d:] = d1.astype(BF16)
        dbg_ref[:, :d] += jnp.sum(d0, axis=0, keepdims=True)
        dbg_ref[:, d:] += jnp.sum(d1, axis=0, keepdims=True)

    row = pl.BlockSpec((tm, d), lambda i: (i, 0))
    wide = pl.BlockSpec((tm, 2 * d), lambda i: (i, 0))
    vec = pl.BlockSpec((1, 2 * d), lambda i: (0, 0))
    gate_cols = pl.BlockSpec((tm, 2 * d), lambda i: (i, DPROJ_COLS["gate"] // (2 * d)))
    return pl.pallas_call(
        body, name=name, grid=(t // tm,),
        in_specs=[row, pl.BlockSpec((d, d), lambda i: (0, 0)), wide, vec, row, row],
        out_specs=[row, row, gate_cols, vec],
        out_shape=[jax.ShapeDtypeStruct((t, d), BF16), jax.ShapeDtypeStruct((t, d), BF16),
                   jax.ShapeDtypeStruct((t, DPROJ_WIDTH), BF16), jax.ShapeDtypeStruct((1, 2 * d), F32)],
        compiler_params=_params("arbitrary"),
    )(dx, w_mix, gl, bg, ys, ya)


def _up_proj_swiglu(h, w_up_t, gt, name):
    t, k = h.shape
    f = w_up_t.shape[0]
    tm, tn, _ = _mm_tiles(t, f, k, h.dtype.itemsize, w_up_t.dtype.itemsize, 4 + 2, True)

    def body(h_ref, w_ref, g_ref, up_ref, act_ref):
        up = _nt(h_ref[...], w_ref[...])
        up_ref[...] = up
        gv = g_ref[...]
        act_ref[...] = ((gv * _sigmoid(gv)) * up).astype(BF16)

    tile = pl.BlockSpec((tm, tn), lambda i, j: (i, j))
    return pl.pallas_call(
        body, name=name, grid=(t // tm, f // tn),
        in_specs=[pl.BlockSpec((tm, k), lambda i, j: (i, 0)), pl.BlockSpec((tn, k), lambda i, j: (j, 0)), tile],
        out_specs=[tile, tile],
        out_shape=[jax.ShapeDtypeStruct((t, f), F32), jax.ShapeDtypeStruct((t, f), BF16)],
        compiler_params=_params("parallel", "parallel"),
    )(h, w_up_t, gt)


def _down_dx_swiglu_bwd(dx, w_down, gt, up, name):
    t, k = dx.shape
    f = w_down.shape[0]
    tm, tn, _ = _mm_tiles(t, f, k, dx.dtype.itemsize, w_down.dtype.itemsize, 2 + 2, True)
    tm = min(tm, 512)

    def body(d_ref, w_ref, g_ref, u_ref, dg_ref, du_ref):
        dact = _nt(d_ref[...], w_ref[...])
        gv = g_ref[...]
        sg = _sigmoid(gv)
        dg_ref[...] = (dact * u_ref[...] * (sg * (1.0 + gv * (1.0 - sg)))).astype(BF16)
        du_ref[...] = (dact * (gv * sg)).astype(BF16)

    tile = pl.BlockSpec((tm, tn), lambda i, j: (i, j))
    return pl.pallas_call(
        body, name=name, grid=(t // tm, f // tn),
        in_specs=[pl.BlockSpec((tm, k), lambda i, j: (i, 0)), pl.BlockSpec((tn, k), lambda i, j: (j, 0)), tile, tile],
        out_specs=[tile, tile], out_shape=[jax.ShapeDtypeStruct((t, f), BF16)] * 2,
        compiler_params=_params("parallel", "parallel"),
    )(dx, w_down, gt, up)


def _peer(k):
    x, y, c = lax.axis_index("x"), lax.axis_index("y"), lax.axis_index("c")
    px, py, pc = x ^ ((k >> 2) & 1), y ^ ((k >> 1) & 1), c ^ (k & 1)
    return (px, py, pc), 4 * px + 2 * py + pc


def _my_index():
    return 4 * lax.axis_index("x") + 2 * lax.axis_index("y") + lax.axis_index("c")


def _all_gather(parts, name):
    n_parts = len(parts)

    def body(*refs):
        ins, outs = refs[:n_parts], refs[n_parts:2 * n_parts]
        send_sems, recv_sems, local_sems = refs[2 * n_parts:]
        here, me = _peer(0)
        sibling, sib_idx = _peer(1)
        chips = [_peer(2 * q) for q in range(1, N_CHIPS)]

        def copy(i, k, block, to, src=None):
            return pltpu.make_async_remote_copy(
                src_ref=outs[i].at[block] if src is None else src, dst_ref=outs[i].at[block],
                send_sem=send_sems.at[i * (N_DEV - 1) + k], recv_sem=recv_sems.at[i * (N_DEV - 1) + k],
                device_id=to, device_id_type=MESH)

        local = [pltpu.make_async_copy(ins[i], outs[i].at[me], local_sems.at[i]) for i in range(n_parts)]
        for cp in local:
            cp.start()
        sends = []
        for i in range(n_parts):
            sends.append(copy(i, 0, me, sibling, src=ins[i]))
            sends += [copy(i, q, me, chip, src=ins[i]) for q, (chip, _) in enumerate(chips, start=1)]
        for cp in sends:
            cp.start()
        for q, (chip, chip_idx) in enumerate(chips, start=1):
            for i in range(n_parts):
                copy(i, q, chip_idx, here).wait_recv()
                fwd = copy(i, N_CHIPS - 1 + q, chip_idx, sibling)
                fwd.start()
                sends.append(fwd)
        for i in range(n_parts):
            copy(i, 0, sib_idx, here).wait_recv()
        for q, (_, chip_idx) in enumerate(chips, start=1):
            for i in range(n_parts):
                copy(i, N_CHIPS - 1 + q, chip_idx ^ 1, here).wait_recv()
        for cp in sends:
            cp.wait_send()
        for cp in local:
            cp.wait()

    hbm = pl.BlockSpec(memory_space=pl.ANY)
    return pl.pallas_call(
        body, name=name, in_specs=[hbm] * n_parts, out_specs=[hbm] * n_parts,
        out_shape=[jax.ShapeDtypeStruct((N_DEV,) + p_.shape, p_.dtype) for p_ in parts],
        scratch_shapes=[pltpu.SemaphoreType.DMA((n_parts * (N_DEV - 1),)),
                        pltpu.SemaphoreType.DMA((n_parts * (N_DEV - 1),)),
                        pltpu.SemaphoreType.DMA((n_parts,))],
        compiler_params=pltpu.CompilerParams(has_side_effects=True),
    )(*parts)


HBM_SPEC = pl.BlockSpec(memory_space=pltpu.HBM)
SEM_SPEC = pl.BlockSpec(memory_space=pltpu.SEMAPHORE)
DATAFLOW = pltpu.SideEffectType.DATAFLOW_SIDE_EFFECTING


def _gather_start(block, after, name):
    per_peer = block.ndim == 3

    def body(v_ref, land_ref, after_ref, send_sems, recv_sems, v_thru, land_thru, token):
        me = _my_index()
        for k in range(1, N_DEV):
            peer, pidx = _peer(k)
            pltpu.make_async_remote_copy(
                src_ref=v_ref.at[pidx] if per_peer else v_ref, dst_ref=land_ref.at[me],
                send_sem=send_sems.at[k - 1], recv_sem=recv_sems.at[k - 1],
                device_id=peer, device_id_type=MESH).start()
        token[...] = jnp.zeros_like(token)

    land_shape = (N_DEV,) + block.shape[-2:]
    return pl.pallas_call(
        body, name=name,
        out_shape=(pltpu.SemaphoreType.DMA((N_DEV - 1,)), pltpu.SemaphoreType.DMA((N_DEV - 1,)),
                   pltpu.HBM(block.shape, block.dtype), pltpu.HBM(land_shape, block.dtype),
                   jax.ShapeDtypeStruct((8, LANES), F32)),
        in_specs=(HBM_SPEC, HBM_SPEC, pl.BlockSpec(memory_space=pl.ANY)),
        out_specs=(SEM_SPEC, SEM_SPEC, HBM_SPEC, HBM_SPEC, pl.BlockSpec(memory_space=pltpu.VMEM)),
        input_output_aliases={0: 2, 1: 3},
        compiler_params=pltpu.CompilerParams(has_side_effects=DATAFLOW),
    )(pltpu.with_memory_space_constraint(block, pltpu.HBM),
      pltpu.with_memory_space_constraint(lax.empty(land_shape, block.dtype), pltpu.HBM), after)


def _gather_wait(send_sems, recv_sems, block, landing, after, name):
    per_peer = block.ndim == 3

    def body(v_ref, land_ref, send_sems, recv_sems, after_ref, v_dead, got_ref):
        for k in range(1, N_DEV):
            peer, pidx = _peer(k)
            copy = pltpu.make_async_remote_copy(
                src_ref=v_ref.at[pidx] if per_peer else v_ref, dst_ref=land_ref.at[pidx],
                send_sem=send_sems.at[k - 1], recv_sem=recv_sems.at[k - 1],
                device_id=peer, device_id_type=MESH)
            copy.wait_send()
            copy.wait_recv()

    return pl.pallas_call(
        body, name=name,
        out_shape=(pltpu.HBM(block.shape, block.dtype), pltpu.HBM(landing.shape, landing.dtype)),
        in_specs=(HBM_SPEC, HBM_SPEC, SEM_SPEC, SEM_SPEC, pl.BlockSpec(memory_space=pl.ANY)),
        out_specs=(HBM_SPEC, HBM_SPEC), input_output_aliases={0: 0, 1: 1},
        compiler_params=pltpu.CompilerParams(has_side_effects=DATAFLOW),
    )(block, landing, send_sems, recv_sems, after)


TILE_ELEMS = 1024 * 1024


def _shared_exchange(shared, name):
    def body(sh_ref, gsh_ref, send_sems, recv_sems, local_sem):
        me = _my_index()
        local = pltpu.make_async_copy(sh_ref, gsh_ref.at[me], local_sem)
        local.start()
        sends = []
        for k in range(1, N_DEV):
            peer, _ = _peer(k)
            cp = pltpu.make_async_remote_copy(
                src_ref=sh_ref, dst_ref=gsh_ref.at[me], send_sem=send_sems.at[k - 1],
                recv_sem=recv_sems.at[k - 1], device_id=peer, device_id_type=MESH)
            cp.start()
            sends.append(cp)
        for k in range(1, N_DEV):
            peer, pidx = _peer(k)
            pltpu.make_async_remote_copy(
                src_ref=sh_ref, dst_ref=gsh_ref.at[pidx], send_sem=send_sems.at[k - 1],
                recv_sem=recv_sems.at[k - 1], device_id=peer, device_id_type=MESH).wait_recv()
        for cp in sends:
            cp.wait_send()
        local.wait()

    hbm = pl.BlockSpec(memory_space=pl.ANY)
    return pl.pallas_call(
        body, name=name, in_specs=[hbm], out_specs=hbm,
        out_shape=jax.ShapeDtypeStruct((N_DEV,) + shared.shape, shared.dtype),
        scratch_shapes=[pltpu.SemaphoreType.DMA((N_DEV - 1,)), pltpu.SemaphoreType.DMA((N_DEV - 1,)),
                        pltpu.SemaphoreType.DMA],
        compiler_params=pltpu.CompilerParams(has_side_effects=True),
    )(shared)


def _adamw(parts, w, m, v, name, row0=0, own=None):
    n_parts, rows, lanes = parts.shape
    tr = rows if rows * lanes <= TILE_ELEMS // 2 else _tile_rows(math.gcd(rows, row0), TILE_ELEMS // 4 // lanes, 8)
    c1 = 1.0 - ADAM_B1 ** ADAM_STEP
    c2 = 1.0 - ADAM_B2 ** ADAM_STEP

    def body(*refs):
        if own is None:
            p_ref, w_ref, m_ref, v_ref, g_ref, d_ref, nm_ref, nv_ref = refs
            terms = [p_ref[j].astype(F32) for j in range(n_parts)]
        else:
            me_ref, p_ref, own_ref, w_ref, m_ref, v_ref, g_ref, d_ref, nm_ref, nv_ref = refs
            terms = [jnp.where(me_ref[0] == j, own_ref[...], p_ref[j]).astype(F32) for j in range(n_parts)]
        g = terms[0]
        for term in terms[1:]:
            g = g + term
        nm = ADAM_B1 * m_ref[...] + (1.0 - ADAM_B1) * g
        nv = ADAM_B2 * v_ref[...] + (1.0 - ADAM_B2) * (g * g)
        g_ref[...] = g
        nm_ref[...] = nm
        nv_ref[...] = nv
        d_ref[...] = -ADAM_LR * ((nm / c1) / (jnp.sqrt(nv / c2) + ADAM_EPS) + ADAM_WD * w_ref[...])

    row = pl.BlockSpec((tr, lanes), lambda i, *_: (i, 0))
    state = pl.BlockSpec((tr, lanes), lambda i, *_: (row0 // tr + i, 0))
    in_specs = [pl.BlockSpec((n_parts, tr, lanes), lambda i, *_: (0, i, 0)), state, state, state]
    args, n_prefetch = (parts, w, m, v), 0
    if own is not None:
        slabs, me = own
        in_specs.insert(1, pl.BlockSpec((None, tr, lanes), lambda i, me_ref: (me_ref[0], i, 0)))
        args, n_prefetch = (me, parts, slabs, w, m, v), 1
    return pl.pallas_call(
        body, name=name,
        grid_spec=pltpu.PrefetchScalarGridSpec(num_scalar_prefetch=n_prefetch, grid=(rows // tr,),
                                               in_specs=in_specs, out_specs=[row] * 4),
        out_shape=[jax.ShapeDtypeStruct((rows, lanes), F32)] * 4,
        compiler_params=_params("parallel"),
    )(*args)


MATRIX_SHARDS = (
    ("w_in", (D_MODEL, IN_PROJ_DIM // N_DEV), True),
    ("w_ssm_out", (SSM_D_INNER // N_DEV, D_MODEL), False),
    ("w_att_out", (ATT_OUT_DIM, D_MODEL // N_DEV), True),
    ("w_mix_out", (D_MODEL // N_DEV, D_MODEL), False),
    ("w_ffn_gate", (D_MODEL, D_FF // N_DEV), True),
    ("w_ffn_up", (D_MODEL, D_FF // N_DEV), True),
    ("w_ffn_down", (D_FF // N_DEV, D_MODEL), False),
)
CONV_SHARD = ("conv_w", (SSM_CONV, SSM_CONV_DIM // N_DEV), True)
SHARDED = MATRIX_SHARDS + (CONV_SHARD,)
REPLICATED = (("norm_mix", D_MODEL), ("b_gate", 2 * D_MODEL), ("conv_b", SSM_CONV_DIM), ("dt_bias", SSM_N_HEADS),
              ("a_log", SSM_N_HEADS), ("d_skip", SSM_N_HEADS), ("ssm_norm", SSM_D_INNER), ("norm_ffn", D_MODEL),
              ("norm_final", D_MODEL))


def _round_up(n, mult):
    return -(-n // mult) * mult


def _pack_rows(flat, row_mult):
    rows = _round_up(-(-flat.shape[0] // LANES), row_mult)
    return jnp.pad(flat, (0, rows * LANES - flat.shape[0])).reshape(rows, LANES)


def _stacking(specs):
    return tuple((name, (shape[1], shape[0]) if by_cols else shape, by_cols) for name, shape, by_cols in specs)


def _to_stacking(vals, specs):
    return {name: (vals[name].T if by_cols else vals[name]) for name, _, by_cols in specs}


STACK_WIDTH = D_MODEL
STACK_ALIGN = 16
STACK_ORDER = ("w_ssm_out", "w_mix_out", "w_ffn_gate", "w_ffn_up", "w_ffn_down", "w_att_out", "conv_w", "w_in")
GATHER_LATER = STACK_ORDER[:-1]
REDUCE_EARLY = STACK_ORDER[:5]
REDUCE_LATE = STACK_ORDER[5:]


def _stack_layout():
    shapes = {name: shape for name, shape, _ in _stacking(SHARDED)}
    layout, off = {}, 0
    for name in STACK_ORDER:
        r, c = shapes[name]
        rows = r if c == STACK_WIDTH else _round_up(-(-(r * c) // STACK_WIDTH), STACK_ALIGN)
        layout[name] = (off, rows, (r, c))
        off = _round_up(off + rows, STACK_ALIGN)
    return layout, _round_up(off, 1024)


def _to_stack_rows(v, rows):
    if v.shape[-1] == STACK_WIDTH:
        return v
    lead = v.shape[:-2]
    flat = v.reshape(lead + (-1,))
    flat = jnp.pad(flat, [(0, 0)] * len(lead) + [(0, rows * STACK_WIDTH - flat.shape[-1])])
    return flat.reshape(lead + (rows, STACK_WIDTH))


def _from_stack_rows(block, shape):
    r, c = shape
    if c == STACK_WIDTH:
        return block
    lead = block.shape[:-2]
    return block.reshape(lead + (-1,))[..., :r * c].reshape(lead + (r, c))


def _stack(vals, dtype, skip=(), names=STACK_ORDER):
    layout, total = _stack_layout()
    order = names
    after = STACK_ORDER.index(order[-1]) + 1
    if after < len(STACK_ORDER):
        total = layout[STACK_ORDER[after]][0]
    lead = next(iter(vals.values())).shape[:-2]
    pieces = []
    for i, name in enumerate(order):
        off, rows, _ = layout[name]
        until = layout[order[i + 1]][0] if i + 1 < len(order) else total
        piece = jnp.zeros(lead + (rows, STACK_WIDTH), dtype) if name in skip else _to_stack_rows(vals[name], rows)
        pieces.append(jnp.pad(piece.astype(dtype), [(0, 0)] * len(lead) + [(0, until - off - rows), (0, 0)]))
    return jnp.concatenate(pieces, axis=-2)


def _unstack(stacked, names):
    layout, _ = _stack_layout()
    row0 = layout[names[0]][0]
    return {name: _from_stack_rows(stacked[..., layout[name][0] - row0:layout[name][0] - row0 + layout[name][1], :],
                                   layout[name][2]) for name in names}


W_IN_SHARD_ROWS = IN_PROJ_DIM // N_DEV


def _w_in_row_moves():
    moves, orig = [], 0
    for name, size in IN_SPLIT:
        for j in range(N_DEV):
            lo, hi = max(orig, W_IN_SHARD_ROWS * j), min(orig + size, W_IN_SHARD_ROWS * (j + 1))
            if lo < hi:
                moves.append((j, lo - W_IN_SHARD_ROWS * j, DPROJ_COLS[name] + lo - orig, hi - lo))
        orig += size
    return moves


def _w_in_from_shards(shards, name):
    total, base = shards.shape[1], 0
    pad_lo, pad_hi = DPROJ_COLS["dt"] + _round_up(SSM_N_HEADS, STACK_ALIGN), DPROJ_COLS["dt"] + DPROJ_DT_WIDTH

    def body(x_ref, o_ref):
        o_ref[pad_lo:pad_hi, :] = jnp.zeros((pad_hi - pad_lo, LANES), x_ref.dtype)
        for j, r, at, n in _w_in_row_moves():
            o_ref[at:at + n, :] = x_ref[j, base + r:base + r + n, :]

    return pl.pallas_call(
        body, name=name, grid=(STACK_WIDTH // LANES,),
        in_specs=[pl.BlockSpec((N_DEV, total, LANES), lambda c: (0, 0, c))],
        out_specs=pl.BlockSpec((DPROJ_WIDTH, LANES), lambda c: (0, c)),
        out_shape=jax.ShapeDtypeStruct((DPROJ_WIDTH, STACK_WIDTH), shards.dtype),
        compiler_params=_params("parallel"),
    )(shards)


def _w_in_to_shards(dw_all, head, name):
    layout, total = _stack_layout()
    total -= layout[REDUCE_LATE[0]][0]
    base = head.shape[1]
    end = base + W_IN_SHARD_ROWS

    def body(x_ref, h_ref, o_ref):
        o_ref[:, 0:base, :] = h_ref[...]
        for j, r, at, n in _w_in_row_moves():
            o_ref[j, base + r:base + r + n, :] = x_ref[at:at + n, :]
        o_ref[:, end:total, :] = jnp.zeros((N_DEV, total - end, LANES), o_ref.dtype)

    return pl.pallas_call(
        body, name=name, grid=(STACK_WIDTH // LANES,),
        in_specs=[pl.BlockSpec((DPROJ_WIDTH, LANES), lambda c: (0, c)),
                  pl.BlockSpec((N_DEV, base, LANES), lambda c: (0, 0, c))],
        out_specs=pl.BlockSpec((N_DEV, total, LANES), lambda c: (0, 0, c)),
        out_shape=jax.ShapeDtypeStruct((N_DEV, total, STACK_WIDTH), dw_all.dtype),
        compiler_params=_params("parallel"),
    )(dw_all, head)


REPLICATED_ROWS = sum(-(-size // LANES) for _, size in REPLICATED)
LOSS_ROW = REPLICATED_ROWS


def _pack_replicated(vals):
    rows = []
    for name, size in REPLICATED:
        v = vals[name].reshape(-1).astype(F32)
        rows.append(jnp.pad(v, (0, _round_up(size, LANES) - size)))
    return _pack_rows(jnp.concatenate(rows), 8)


def _unpack_replicated(packed, shapes):
    flat = packed.reshape(-1)
    out, off = {}, 0
    for name, size in REPLICATED:
        out[name] = flat[off:off + size].reshape(shapes[name])
        off += _round_up(size, LANES)
    return out


def _lane_row(v):
    v = v.reshape(-1).astype(F32)
    return jnp.pad(v, (0, LANES - v.shape[0])).reshape(1, LANES)


IN_SPLIT = (("z", SSM_D_INNER), ("xbc", SSM_CONV_DIM), ("dt", SSM_N_HEADS), ("qkv", ATT_QKV_DIM), ("gate", 2 * D_MODEL))


def kernel(x, norm_mix, w_in, b_gate, conv_w, conv_b, dt_bias, a_log, d_skip, ssm_norm, w_ssm_out, w_att_out, w_mix_out, norm_ffn, w_ffn_gate, w_ffn_up, w_ffn_down, norm_final, loss_target, m_norm_mix, m_w_in, m_b_gate, m_conv_w, m_conv_b, m_dt_bias, m_a_log, m_d_skip, m_ssm_norm, m_w_ssm_out, m_w_att_out, m_w_mix_out, m_norm_ffn, m_w_ffn_gate, m_w_ffn_up, m_w_ffn_down, m_norm_final, v_norm_mix, v_w_in, v_b_gate, v_conv_w, v_conv_b, v_dt_bias, v_a_log, v_d_skip, v_ssm_norm, v_w_ssm_out, v_w_att_out, v_w_mix_out, v_norm_ffn, v_w_ffn_gate, v_w_ffn_up, v_w_ffn_down, v_norm_final):
    given = dict(locals())
    weights = {name: given[name][0] for name, _, _ in SHARDED}
    b, s, d = x.shape
    t = b * s

    stacking = _to_stacking(weights, SHARDED)
    conv_shape = dict((name, shape) for name, shape, _ in _stacking(SHARDED))["conv_w"]
    w_in_local = jnp.pad(stacking["w_in"].astype(BF16), ((0, -W_IN_SHARD_ROWS % STACK_ALIGN), (0, 0)))
    conv_local = _pack_rows(stacking["conv_w"].reshape(-1), 8)
    w_in_shards, conv_all = _all_gather([w_in_local, conv_local], "w_in_all_gather")
    head_local = _stack(stacking, BF16, skip=("conv_w",), names=GATHER_LATER)
    in_flight = _gather_start(head_local, conv_all, "weights_gather_start")
    w_in_all = _w_in_from_shards(w_in_shards, "w_in_from_shards")
    w_sec = {name: w_in_all[DPROJ_COLS[name]:DPROJ_COLS[name] + _round_up(size, LANES)] for name, size in IN_SPLIT}
    conv_size = conv_shape[0] * conv_shape[1]
    conv_taps = conv_all.reshape(N_DEV, -1)[:, :conv_size].reshape(N_DEV * conv_shape[0], conv_shape[1]).T

    g_mix, g_ffn, g_fin = norm_mix.reshape(1, d), norm_ffn.reshape(1, d), norm_final.reshape(1, d)
    g_mix = g_mix + in_flight[4][:1, :1]
    bg_row = b_gate.reshape(1, 2 * d)
    convb_row = conv_b.reshape(1, SSM_CONV_DIM)
    ssmn_row = ssm_norm.reshape(1, SSM_D_INNER)
    dtb_row, alog_row = _lane_row(dt_bias), _lane_row(a_log)
    cosf, sinf = _rope_tables(s)

    x2d = x.reshape(t, d)
    h1 = _rmsnorm_fwd(x2d, g_mix, "norm_mix_fwd")
    proj = {name: _mm(h1, w_sec[name], mode="nt", name="in_proj_" + name) for name, _ in IN_SPLIT if name != "qkv"}
    xbc3 = proj["xbc"].reshape(b, s, SSM_CONV_DIM)
    xc = _conv_fwd(xbc3, conv_taps, convb_row, "conv_fwd")
    dtr3 = proj["dt"].reshape(b, s, DT_PAD)
    to_channels, to_heads = _head_masks()
    dskx = jnp.repeat(d_skip.reshape(-1).astype(F32), SSM_HEAD_DIM).reshape(1, SSM_D_INNER)
    y_ssd, h_states = _ssd_fwd(xc, dtr3, dtb_row, alog_row, dskx, to_channels, "ssd_fwd")
    y_ssd2 = y_ssd.reshape(t, SSM_D_INNER)
    ynorm = _gate_norm_fwd(y_ssd2, proj["z"], ssmn_row, "ssd_gate_norm_fwd")
    head_local, landed = _gather_wait(*in_flight[:4], ynorm, "weights_gather_wait")
    head_all = lax.dynamic_update_slice(landed, head_local[None], (_my_index(), 0, 0))
    full = {name: v.reshape((-1,) + v.shape[2:]) for name, v in _unstack(head_all, STACK_ORDER[:-2]).items()}
    y_ssm = _mm(ynorm, full["w_ssm_out"], mode="nn", name="ssm_out_proj")

    qk_parts = _qkv_proj_rope(h1, w_sec["qkv"], cosf, sinf, b, s, "in_proj_qkv_rope")
    att_parts = [_att_fwd(qk_parts[gi], "att_fwd_%d" % r) for gi, r in enumerate(ATT_DILATIONS)]
    att, *lse_parts = _att_merge([o for o, _ in att_parts], [l_ for _, l_ in att_parts], "att_merge")
    att2 = att.reshape(t, ATT_OUT_DIM)
    y_att, mixed = _att_out_proj_mix(att2, full["w_att_out"], proj["gate"], bg_row, y_ssm, "att_out_proj_mix")
    x2, h2 = _proj_residual_norm(mixed, full["w_mix_out"], x2d, g_ffn, "mix_out_proj_norm")
    gt = _mm(h2, full["w_ffn_gate"], mode="nt", name="ffn_gate_proj")
    up, act = _up_proj_swiglu(h2, full["w_ffn_up"], gt, "ffn_up_proj_swiglu")

    loss_row, dx3, dg_fin, dx3b = _down_proj_loss_head(act, full["w_ffn_down"], x2, g_fin, loss_target.reshape(t, d),
                                                       "ffn_down_proj_loss_head")
    grads = {}
    grads["w_ffn_down"] = _mm(act, dx3b, mode="tn", name="ffn_down_dw", out_dtype=BF16)
    dgt, dup = _down_dx_swiglu_bwd(dx3b, full["w_ffn_down"], gt, up, "ffn_down_dx_swiglu_bwd")
    grads["w_ffn_gate"] = _mm(dgt, h2, mode="tn", name="ffn_gate_dw", out_dtype=BF16)
    grads["w_ffn_up"] = _mm(dup, h2, mode="tn", name="ffn_up_dw", out_dtype=BF16)
    dh2 = _mm(dgt, full["w_ffn_gate"], mode="nn", name="ffn_gate_dx")
    dx2, dg_ffn, dx2b = _proj_norm_bwd(dup, full["w_ffn_up"], x2, g_ffn, dx3, "ffn_up_dx_norm_bwd", add=dh2,
                                       with_bf16=True)

    grads["w_mix_out"] = _mm(mixed, dx2b, mode="tn", name="mix_out_dw", out_dtype=BF16)
    dys, dya, dproj, dbg = _mix_out_dx_mix_bwd(dx2b, full["w_mix_out"], proj["gate"], bg_row, y_ssm, y_att,
                                               "mix_out_dx_mix_bwd")

    grads["w_ssm_out"] = _mm(ynorm, dys, mode="tn", name="ssm_out_dw", out_dtype=BF16)
    early = _stack({name: grads[name].reshape((N_DEV, -1, STACK_WIDTH)) for name in REDUCE_EARLY}, BF16,
                   names=REDUCE_EARLY)
    early_flight = _gather_start(early, dys, "grads_scatter_start")
    ssmn_row = ssmn_row + early_flight[4][:1, :1]
    dy_ssd, dproj, dssmn = _ssm_out_dx_gate_norm_bwd(dys, full["w_ssm_out"], y_ssd2, proj["z"], ssmn_row, dproj,
                                                     "ssm_out_dx_gate_norm_bwd")
    dxc, dproj, dalog, ddsk, ddtb = _ssd_bwd(xc, dtr3, dy_ssd.reshape(b, s, SSM_D_INNER), h_states, dtb_row, alog_row,
                                             dskx, to_channels, to_heads, dproj.reshape(b, s, DPROJ_WIDTH), "ssd_bwd")
    dproj, dconvw, dconvb = _conv_bwd(xbc3, dxc, conv_taps, convb_row, dproj, "conv_bwd")
    grads["conv_w"] = dconvw.T.astype(BF16)

    grads["w_att_out"] = _mm(dya, att2, mode="tn", name="att_out_dw", out_dtype=BF16)
    datt = _mm(dya, full["w_att_out"], mode="nn", name="att_out_dx").reshape(b, s, ATT_OUT_DIM)
    do_parts, dl_parts = _att_delta(att, datt, "att_delta")
    dqs, dks, dvs = [], [], []
    for gi, r in enumerate(ATT_DILATIONS):
        operands = (qk_parts[gi], do_parts[gi], lse_parts[gi], dl_parts[gi])
        dqs.append(_att_bwd_q(*operands, "att_bwd_q_%d" % r))
        dk_g, dv_g = _att_bwd_kv(*operands, "att_bwd_kv_%d" % r)
        dks.append(dk_g)
        dvs.append(dv_g)
    dproj = _rope_bwd(dqs, dks, dvs, cosf, sinf, dproj, "rope_bwd").reshape(t, DPROJ_WIDTH)

    dw_all = _mm(dproj, h1, mode="tn", name="in_proj_dw", out_dtype=BF16)
    head = _stack({name: grads[name].reshape((N_DEV, -1, grads[name].shape[-1])) for name in REDUCE_LATE[:-1]}, BF16,
                  names=REDUCE_LATE[:-1])
    late = _w_in_to_shards(dw_all, head, "grad_stacks")
    late_flight = _gather_start(late, dw_all, "grads_late_scatter_start")
    grad_x, dg_mix = _proj_norm_bwd(dproj, w_in_all, x2d, g_mix, dx2, "in_proj_dx_norm_bwd", after=late_flight[4])

    small = {"norm_mix": dg_mix, "b_gate": dbg, "conv_b": dconvb, "dt_bias": ddtb[:, :SSM_N_HEADS],
             "a_log": dalog[:, :SSM_N_HEADS], "d_skip": ddsk[:, :SSM_N_HEADS], "ssm_norm": dssmn,
             "norm_ffn": dg_ffn, "norm_final": dg_fin}
    shared = _pack_replicated(small)
    shared = shared.at[LOSS_ROW, 0].set(loss_row[0, 0])
    got_small = _shared_exchange(shared, "shared_grads_exchange")

    def packed(prefix):
        vals = _to_stacking({name: given[prefix + name][0] for name, _, _ in SHARDED}, SHARDED)
        rep = {name: given[prefix + name] for name, _ in REPLICATED}
        return _stack(vals, F32), _pack_replicated(rep)

    (w_big, w_small), (m_big, m_small), (v_big, v_small) = packed(""), packed("m_"), packed("v_")
    me = _my_index().astype(jnp.int32).reshape(1)
    early, early_landed = _gather_wait(*early_flight[:4], got_small, "grads_scatter_wait")
    late, late_landed = _gather_wait(*late_flight[:4], got_small, "grads_late_scatter_wait")
    big_early = _adamw(early_landed, w_big, m_big, v_big, "adamw_early", own=(early, me))
    big_late = _adamw(late_landed, w_big, m_big, v_big, "adamw_late", row0=early.shape[1], own=(late, me))
    sml = _adamw(got_small, w_small, m_small, v_small, "adamw_replicated")

    outs = [sml[0][LOSS_ROW, 0], grad_x.reshape(b, s, d)]
    rep_shapes = {name: given[name].shape for name, _ in REPLICATED}
    order = ["norm_mix", "w_in", "b_gate", "conv_w", "conv_b", "dt_bias", "a_log", "d_skip", "ssm_norm", "w_ssm_out",
             "w_att_out", "w_mix_out", "norm_ffn", "w_ffn_gate", "w_ffn_up", "w_ffn_down", "norm_final"]
    for early_k, late_k, sml_k in zip(big_early, big_late, sml):
        stacks = dict(_unstack(early_k, REDUCE_EARLY), **_unstack(late_k, REDUCE_LATE))
        sharded = _to_stacking(stacks, SHARDED)
        rep = _unpack_replicated(sml_k, rep_shapes)
        for name in order:
            outs.append(sharded[name][None] if name in sharded else rep[name])
    return tuple(outs)
```

```python
import functools
import math

import jax
import jax.numpy as jnp
from jax import lax
from jax.experimental import pallas as pl
from jax.experimental.pallas import tpu as pltpu

F32 = jnp.float32
BF16 = jnp.bfloat16

N_DEV = 8
N_CHIPS = 4
D_MODEL = 1024
SSM_D_INNER = 2048
SSM_HEAD_DIM = 64
SSM_N_HEADS = 32
SSM_N_GROUPS = 4
SSM_HEADS_PER_GROUP = SSM_N_HEADS // SSM_N_GROUPS
SSM_D_STATE = 128
SSM_CONV = 4
SSM_CHUNK = 128
SSM_CONV_DIM = 3072
ATT_HEAD_DIM = 128
ATT_HEADS_PER_GROUP = 4
ATT_DILATIONS = (1, 4, 16)
ATT_N_HEADS = 12
ATT_QKV_DIM = 4608
ATT_OUT_DIM = 512
ATT_BLOCK = 128
ROPE_THETA = 10000.0
D_FF = 2816
IN_PROJ_DIM = 11808
EPS = 1e-6
LANES = 128
DT_PAD = LANES

DPROJ_COLS = {"qkv": 0, "xbc": 4608, "dt": 7680, "z": 8192, "gate": 10240}
DPROJ_DT_WIDTH = 512
DPROJ_WIDTH = 12288

ADAM_LR = 0.001
ADAM_B1 = 0.9
ADAM_B2 = 0.999
ADAM_EPS = 1e-08
ADAM_WD = 0.01
ADAM_STEP = 10

VMEM_LIMIT = 56 * 1024 * 1024
MESH = pl.DeviceIdType.MESH
NEG_INF = float("-inf")


def _tile_rows(n, cap, mult):
    return max(t for t in range(mult, min(n, cap) + 1, mult) if n % t == 0)


def _pick(n, candidates):
    for c in candidates:
        if n % c == 0:
            return c
    return n


def _params(*sem):
    return pltpu.CompilerParams(dimension_semantics=sem, vmem_limit_bytes=VMEM_LIMIT)


def _sigmoid(x):
    return 0.5 * jnp.tanh(0.5 * x) + 0.5


def _softplus(x):
    return jnp.maximum(x, 0.0) + jnp.log(1.0 + jnp.exp(-jnp.abs(x)))


def _dot(a, b, dims):
    return lax.dot_general(a.astype(BF16), b.astype(BF16), (dims, ((), ())), preferred_element_type=F32)


def _nn(a, b):
    return _dot(a, b, ((1,), (0,)))


def _nt(a, b):
    return _dot(a, b, ((1,), (1,)))


def _tn(a, b):
    return _dot(a, b, ((0,), (0,)))


def _split3(v):
    hi = v.astype(BF16)
    r1 = v - hi.astype(F32)
    mid = r1.astype(BF16)
    lo = (r1 - mid.astype(F32)).astype(BF16)
    return hi, mid, lo


def _mask_nn(mask, v):
    mb = mask.astype(BF16)
    hi, mid, lo = _split3(v)
    return _nn(mb, hi) + (_nn(mb, mid) + _nn(mb, lo))


MM_VMEM_BUDGET = 40 * 1024 * 1024
MM_FULL_K = 2816


def _mm_tiles(m, n, k, a_bytes, b_bytes, o_bytes, has_add):
    tk = k if k <= MM_FULL_K else _pick(k, (2048, 1024, 512, 256, 128))
    tn = 1408 if (n > 1024 and n % 1408 == 0) else _pick(n, (1024, 768, 512, 384, 256, 128))
    for tm in (1408, 1024, 768, 512, 384, 256, 128):
        if m % tm:
            continue
        buffers = 2 * (tm * tk * a_bytes + tk * tn * b_bytes + tm * tn * (o_bytes + (4 if has_add else 0)))
        if tk < k:
            buffers += tm * tn * 4
        if buffers <= MM_VMEM_BUDGET:
            return tm, tn, tk
    return _pick(m, (128,)), tn, tk


def _mm(a, b, *, mode, name, out_dtype=F32, add=None, after=None):
    if mode == "nn":
        (m, k), n = a.shape, b.shape[1]
    elif mode == "nt":
        (m, k), n = a.shape, b.shape[0]
    else:
        (k, m), n = a.shape, b.shape[1]
    has_add = add is not None
    tm, tn, tk = _mm_tiles(m, n, k, a.dtype.itemsize, b.dtype.itemsize, jnp.dtype(out_dtype).itemsize, has_add)
    nk = k // tk
    dims = {"nn": ((1,), (0,)), "nt": ((1,), (1,)), "tn": ((0,), (0,))}[mode]
    a_spec = {"nn": pl.BlockSpec((tm, tk), lambda i, j, kk: (i, kk)),
              "nt": pl.BlockSpec((tm, tk), lambda i, j, kk: (i, kk)),
              "tn": pl.BlockSpec((tk, tm), lambda i, j, kk: (kk, i))}[mode]
    b_spec = {"nn": pl.BlockSpec((tk, tn), lambda i, j, kk: (kk, j)),
              "nt": pl.BlockSpec((tn, tk), lambda i, j, kk: (j, kk)),
              "tn": pl.BlockSpec((tk, tn), lambda i, j, kk: (kk, j))}[mode]
    o_spec = pl.BlockSpec((tm, tn), lambda i, j, kk: (i, j))

    def finish(r, c_ref, o_ref):
        if has_add:
            r = r + c_ref[...]
        o_ref[...] = r.astype(out_dtype)

    def body_one(*refs):
        a_ref, b_ref = refs[:2]
        finish(_dot(a_ref[...], b_ref[...], dims), refs[2] if has_add else None, refs[-1])

    def body_acc(*refs):
        a_ref, b_ref = refs[:2]
        o_ref, acc = refs[-2:]
        kk = pl.program_id(2)

        @pl.when(kk == 0)
        def _():
            acc[...] = jnp.zeros_like(acc)

        acc[...] += _dot(a_ref[...], b_ref[...], dims)

        @pl.when(kk == nk - 1)
        def _():
            finish(acc[...], refs[2] if has_add else None, o_ref)

    in_specs = [a_spec, b_spec] + ([o_spec] if has_add else [])
    args = (a, b) + ((add,) if has_add else ())
    if after is not None:
        in_specs, args = in_specs + [pl.BlockSpec(memory_space=pl.ANY)], args + (after,)
    return pl.pallas_call(
        body_one if nk == 1 else body_acc, name=name, grid=(m // tm, n // tn, nk),
        in_specs=in_specs, out_specs=o_spec,
        out_shape=jax.ShapeDtypeStruct((m, n), out_dtype),
        scratch_shapes=[] if nk == 1 else [pltpu.VMEM((tm, tn), F32)],
        compiler_params=_params("parallel", "parallel", "arbitrary"),
    )(*args)


def _rmsnorm_fwd(x, g, name):
    t, d = x.shape
    tm = _pick(t, (512, 256, 128))

    def body(x_ref, g_ref, o_ref):
        xv = x_ref[...]
        r = lax.rsqrt(jnp.mean(xv * xv, axis=-1, keepdims=True) + EPS)
        o_ref[...] = ((xv * r) * g_ref[...]).astype(BF16)

    return pl.pallas_call(
        body, name=name, grid=(t // tm,),
        in_specs=[pl.BlockSpec((tm, d), lambda i: (i, 0)), pl.BlockSpec((1, d), lambda i: (0, 0))],
        out_specs=pl.BlockSpec((tm, d), lambda i: (i, 0)),
        out_shape=jax.ShapeDtypeStruct((t, d), BF16),
        compiler_params=_params("parallel"),
    )(x, g)


def _proj_residual_norm(a, w, res, g, name):
    t, k = a.shape
    d = w.shape[1]
    tm, _, _ = _mm_tiles(t, d, k, a.dtype.itemsize, w.dtype.itemsize, 4 + 2, True)

    def body(a_ref, w_ref, r_ref, g_ref, x_ref, h_ref):
        xv = r_ref[...] + _nn(a_ref[...], w_ref[...])
        x_ref[...] = xv
        r = lax.rsqrt(jnp.mean(xv * xv, axis=-1, keepdims=True) + EPS)
        h_ref[...] = ((xv * r) * g_ref[...]).astype(BF16)

    row = pl.BlockSpec((tm, d), lambda i: (i, 0))
    return pl.pallas_call(
        body, name=name, grid=(t // tm,),
        in_specs=[pl.BlockSpec((tm, k), lambda i: (i, 0)), pl.BlockSpec((k, d), lambda i: (0, 0)), row,
                  pl.BlockSpec((1, d), lambda i: (0, 0))],
        out_specs=[row, row],
        out_shape=[jax.ShapeDtypeStruct((t, d), F32), jax.ShapeDtypeStruct((t, d), BF16)],
        compiler_params=_params("parallel"),
    )(a, w, res, g)


def _proj_norm_bwd(a, w, x, g, dres, name, add=None, with_bf16=False, after=None):
    t, k = a.shape
    d = w.shape[1]
    has_add = add is not None
    tm, _, tk = _mm_tiles(t, d, k, a.dtype.itemsize, w.dtype.itemsize, 4 + 4 + 4 + (2 if with_bf16 else 0), has_add)
    if tk == k:
        tm = min(tm, 512)
    nk = k // tk

    def body(*refs):
        a_ref, w_ref, x_ref, g_ref, dres_ref = refs[:5]
        rest = refs[5 + has_add + (after is not None):]
        dx_ref, dg_ref = rest[:2]
        i, kk = pl.program_id(0), pl.program_id(1)

        @pl.when(jnp.logical_and(i == 0, kk == 0))
        def _():
            dg_ref[...] = jnp.zeros_like(dg_ref)

        part = _nn(a_ref[...], w_ref[...])
        if nk > 1:
            acc = rest[-1]

            @pl.when(kk == 0)
            def _():
                acc[...] = jnp.zeros_like(acc)

            acc[...] += part

        @pl.when(kk == nk - 1)
        def _():
            dhv = part if nk == 1 else acc[...]
            if has_add:
                dhv = dhv + refs[5][...]
            xv = x_ref[...]
            r = lax.rsqrt(jnp.mean(xv * xv, axis=-1, keepdims=True) + EPS)
            xhat = xv * r
            dyg = dhv * g_ref[...]
            dx = dres_ref[...] + r * (dyg - xhat * jnp.mean(dyg * xhat, axis=-1, keepdims=True))
            dx_ref[...] = dx
            if with_bf16:
                rest[2][...] = dx.astype(BF16)
            dg_ref[...] += jnp.sum(dhv * xhat, axis=0, keepdims=True)

    row = pl.BlockSpec((tm, d), lambda i, kk: (i, 0))
    vec = pl.BlockSpec((1, d), lambda i, kk: (0, 0))
    in_specs = [pl.BlockSpec((tm, tk), lambda i, kk: (i, kk)), pl.BlockSpec((tk, d), lambda i, kk: (kk, 0)),
                row, vec, row] + has_add * [row]
    args = (a, w, x, g, dres) + has_add * (add,)
    if after is not None:
        in_specs, args = in_specs + [pl.BlockSpec(memory_space=pl.ANY)], args + (after,)
    return pl.pallas_call(
        body, name=name, grid=(t // tm, nk), in_specs=in_specs, out_specs=[row, vec] + with_bf16 * [row],
        out_shape=[jax.ShapeDtypeStruct((t, d), F32), jax.ShapeDtypeStruct((1, d), F32)]
        + with_bf16 * [jax.ShapeDtypeStruct((t, d), BF16)],
        scratch_shapes=[] if nk == 1 else [pltpu.VMEM((tm, d), F32)],
        compiler_params=_params("arbitrary", "arbitrary"),
    )(*args)


def _down_proj_loss_head(act, w_down, res, g, target, name):
    t, k = act.shape
    d = w_down.shape[1]
    tm, _, _ = _mm_tiles(t, d, k, act.dtype.itemsize, w_down.dtype.itemsize, 4 + 2, True)
    tm = min(tm, 512)

    def body(a_ref, w_ref, r_ref, g_ref, t_ref, loss_ref, dx_ref, dg_ref, dxb_ref):
        @pl.when(pl.program_id(0) == 0)
        def _():
            dg_ref[...] = jnp.zeros_like(dg_ref)
            loss_ref[...] = jnp.zeros_like(loss_ref)

        xv = r_ref[...] + _nn(a_ref[...], w_ref[...])
        gv = g_ref[...]
        r = lax.rsqrt(jnp.mean(xv * xv, axis=-1, keepdims=True) + EPS)
        xhat = xv * r
        err = xhat * gv - t_ref[...]
        loss_ref[...] += jnp.sum(err * err) * (0.5 / d)
        dy = err * (1.0 / d)
        dyg = dy * gv
        dx = r * (dyg - xhat * jnp.mean(dyg * xhat, axis=-1, keepdims=True))
        dx_ref[...] = dx
        dxb_ref[...] = dx.astype(BF16)
        dg_ref[...] += jnp.sum(dy * xhat, axis=0, keepdims=True)

    row = pl.BlockSpec((tm, d), lambda i: (i, 0))
    vec = pl.BlockSpec((1, d), lambda i: (0, 0))
    return pl.pallas_call(
        body, name=name, grid=(t // tm,),
        in_specs=[pl.BlockSpec((tm, k), lambda i: (i, 0)), pl.BlockSpec((k, d), lambda i: (0, 0)), row, vec, row],
        out_specs=[pl.BlockSpec((1, LANES), lambda i: (0, 0)), row, vec, row],
        out_shape=[jax.ShapeDtypeStruct((1, LANES), F32), jax.ShapeDtypeStruct((t, d), F32),
                   jax.ShapeDtypeStruct((1, d), F32), jax.ShapeDtypeStruct((t, d), BF16)],
        compiler_params=_params("arbitrary"),
    )(act, w_down, res, g, target)


CONV_HALO = 8
CONV_ROWS = 64


def _conv_taps(window, wv, bv):
    acc = bv + wv[SSM_CONV - 1:SSM_CONV, :] * window(0)
    for sh in range(1, SSM_CONV):
        kidx = SSM_CONV - 1 - sh
        acc = acc + wv[kidx:kidx + 1, :] * window(sh)
    return acc


def _conv_fwd(u, w, bias, name):
    b, s, c = u.shape
    rows = CONV_ROWS

    def body(u_ref, w_ref, b_ref, o_ref, ext):
        ext[0:CONV_HALO, :] = jnp.zeros((CONV_HALO, LANES), F32)
        ext[CONV_HALO:, :] = u_ref[...]
        wv, bv = w_ref[...], b_ref[...]
        for r0 in range(0, s, rows):
            acc = _conv_taps(lambda sh: ext[CONV_HALO + r0 - sh:CONV_HALO + r0 - sh + rows, :], wv, bv)
            o_ref[r0:r0 + rows, :] = acc * _sigmoid(acc)

    strip = pl.BlockSpec((None, s, LANES), lambda bi, j: (bi, 0, j))
    return pl.pallas_call(
        body, name=name, grid=(b, c // LANES),
        in_specs=[strip, pl.BlockSpec((SSM_CONV, LANES), lambda bi, j: (0, j)),
                  pl.BlockSpec((1, LANES), lambda bi, j: (0, j))],
        out_specs=strip, out_shape=jax.ShapeDtypeStruct((b, s, c), F32),
        scratch_shapes=[pltpu.VMEM((CONV_HALO + s, LANES), F32)],
        compiler_params=_params("parallel", "parallel"),
    )(u, w, bias)


def _conv_bwd(u, dout, w, bias, dproj, name):
    b, s, c = u.shape
    rows = CONV_ROWS

    def fold(v):
        return jnp.sum(v.reshape(rows // CONV_HALO, CONV_HALO, LANES), axis=0)

    def body(u_ref, d_ref, w_ref, b_ref, buf_ref, du_ref, dw_ref, db_ref, ext, dpre):
        @pl.when(pl.program_id(1) == 0)
        def _():
            dw_ref[...] = jnp.zeros_like(dw_ref)
            db_ref[...] = jnp.zeros_like(db_ref)

        ext[0:CONV_HALO, :] = jnp.zeros((CONV_HALO, LANES), F32)
        ext[CONV_HALO:, :] = u_ref[...]
        dpre[s:, :] = jnp.zeros((CONV_HALO, LANES), F32)
        wv, bv = w_ref[...], b_ref[...]
        sums = [jnp.zeros((CONV_HALO, LANES), F32)] * (SSM_CONV + 1)
        for r0 in range(0, s, rows):
            window = lambda sh: ext[CONV_HALO + r0 - sh:CONV_HALO + r0 - sh + rows, :]
            acc = _conv_taps(window, wv, bv)
            sg = _sigmoid(acc)
            dp = d_ref[r0:r0 + rows, :] * (sg * (1.0 + acc * (1.0 - sg)))
            dpre[r0:r0 + rows, :] = dp
            taps = [sums[SSM_CONV - 1 - sh] + fold(dp * window(sh)) for sh in range(SSM_CONV)]
            sums = taps[::-1] + [sums[SSM_CONV] + fold(dp)]
        for r0 in range(0, s, rows):
            du = wv[SSM_CONV - 1:SSM_CONV, :] * dpre[r0:r0 + rows, :]
            for sh in range(1, SSM_CONV):
                kidx = SSM_CONV - 1 - sh
                du = du + wv[kidx:kidx + 1, :] * dpre[r0 + sh:r0 + sh + rows, :]
            du_ref[r0:r0 + rows, :] = du.astype(BF16)
        for kidx in range(SSM_CONV):
            dw_ref[kidx:kidx + 1, :] += jnp.sum(sums[kidx], axis=0, keepdims=True)
        db_ref[...] += jnp.sum(sums[SSM_CONV], axis=0, keepdims=True)

    strip = pl.BlockSpec((None, s, LANES), lambda j, bi: (bi, 0, j))
    taps = pl.BlockSpec((SSM_CONV, LANES), lambda j, bi: (0, j))
    vec = pl.BlockSpec((1, LANES), lambda j, bi: (0, j))
    du_cols = pl.BlockSpec((None, s, LANES), lambda j, bi: (bi, 0, DPROJ_COLS["xbc"] // LANES + j))
    return pl.pallas_call(
        body, name=name, grid=(c // LANES, b),
        in_specs=[strip, strip, taps, vec, pl.BlockSpec(memory_space=pl.ANY)], out_specs=[du_cols, taps, vec],
        input_output_aliases={4: 0},
        out_shape=[jax.ShapeDtypeStruct(dproj.shape, dproj.dtype), jax.ShapeDtypeStruct((SSM_CONV, c), F32),
                   jax.ShapeDtypeStruct((1, c), F32)],
        scratch_shapes=[pltpu.VMEM((CONV_HALO + s, LANES), F32), pltpu.VMEM((s + CONV_HALO, LANES), F32)],
        compiler_params=_params("parallel", "arbitrary"),
    )(u, dout, w, bias, dproj)


def _ssd_chunk_terms(dtr_ref, bias_ref, alog_ref):
    q = SSM_CHUNK
    dt = _softplus(dtr_ref[...] + bias_ref[...])
    a_neg = -jnp.exp(alog_ref[...])
    row = lax.broadcasted_iota(jnp.int32, (q, q), 0)
    col = lax.broadcasted_iota(jnp.int32, (q, q), 1)
    lower = row >= col
    s = _mask_nn(lower, dt * a_neg)
    return dt, a_neg, s, s.T, lower


def _head_masks():
    heads = jnp.arange(LANES)[:, None]
    chans = jnp.arange(SSM_D_INNER)[None, :]
    to_channels = (chans // SSM_HEAD_DIM == heads).astype(BF16)
    return to_channels, to_channels.T


def _per_channel(v, to_channels):
    hi = v.astype(BF16)
    lo = (v - hi.astype(F32)).astype(BF16)
    return _nn(hi, to_channels) + _nn(lo, to_channels)


def _per_head(v, to_heads):
    hi = v.astype(BF16)
    lo = (v - hi.astype(F32)).astype(BF16)
    return _nn(hi, to_heads) + _nn(lo, to_heads)


def _decay_terms_per_channel(dt, s_col, to_channels):
    q = SSM_CHUNK
    tot = s_col[q - 1:q, :]
    stacked = jnp.concatenate([dt, jnp.exp(s_col), jnp.exp(tot - s_col)], axis=0)
    wide = _per_channel(stacked, to_channels)
    dtx, esx, decx = wide[:q], wide[q:2 * q], wide[2 * q:]
    return dtx, esx, decx, esx[0:1, :] * decx[0:1, :]


SSM_PAIRS_PER_GROUP = SSM_HEADS_PER_GROUP // 2
SSM_GROUP_CHANNELS = SSM_HEADS_PER_GROUP * SSM_HEAD_DIM


def _split_pair(v):
    first = lax.broadcasted_iota(jnp.int32, v.shape, 1) < SSM_HEAD_DIM
    return jnp.concatenate([jnp.where(first, v, 0.0), jnp.where(first, 0.0, v)], axis=0)


def _ssd_fwd(xc, dtr, dt_bias, a_log, dskx, to_channels, name):
    b, s, _ = xc.shape
    q = SSM_CHUNK
    nc = s // q
    n, gc = SSM_D_STATE, SSM_GROUP_CHANNELS

    def body(xc_ref, dtr_ref, bias_ref, alog_ref, dsk_ref, tc_ref, y_ref, hs_ref, h_scr):
        @pl.when(pl.program_id(1) == 0)
        def _():
            h_scr[...] = jnp.zeros_like(h_scr)

        dt, _, s_col, s_row, lower = _ssd_chunk_terms(dtr_ref, bias_ref, alog_ref)
        dtx, esx, decx, etotx = _decay_terms_per_channel(dt, s_col, tc_ref[...])
        x = xc_ref[:, :SSM_D_INNER]
        xdt = x * dtx
        xdec = xdt * decx
        skip = dsk_ref[...] * x
        for g in range(SSM_N_GROUPS):
            bg = xc_ref[:, SSM_D_INNER + n * g:SSM_D_INNER + n * (g + 1)].astype(BF16)
            cg = xc_ref[:, SSM_D_INNER + n * (SSM_N_GROUPS + g):SSM_D_INNER + n * (SSM_N_GROUPS + g + 1)].astype(BF16)
            gsl = slice(gc * g, gc * (g + 1))
            gm = _nt(cg, bg)
            hgt = h_scr[:, gsl]
            hs_ref[:, gsl] = hgt
            y_off = esx[:, gsl] * _nn(cg, hgt)
            h_scr[:, gsl] = etotx[:, gsl] * hgt + _tn(bg, xdec[:, gsl])
            for k in range(SSM_PAIRS_PER_GROUP):
                h0 = g * SSM_HEADS_PER_GROUP + 2 * k
                lo = gc * g + LANES * k
                ms = []
                for h in (h0, h0 + 1):
                    lm = jnp.exp(jnp.where(lower, s_col[:, h:h + 1] - s_row[h:h + 1, :], NEG_INF))
                    ms.append((gm * lm).astype(BF16))
                y_diag = _nn(jnp.concatenate(ms, axis=1), _split_pair(xdt[:, lo:lo + LANES]))
                y_ref[:, lo:lo + LANES] = y_diag + y_off[:, LANES * k:LANES * (k + 1)] + skip[:, lo:lo + LANES]

    vec = pl.BlockSpec((1, LANES), lambda bi, c: (0, 0))
    return pl.pallas_call(
        body, name=name, grid=(b, nc),
        in_specs=[pl.BlockSpec((None, q, SSM_CONV_DIM), lambda bi, c: (bi, c, 0)),
                  pl.BlockSpec((None, q, LANES), lambda bi, c: (bi, c, 0)), vec, vec,
                  pl.BlockSpec((1, SSM_D_INNER), lambda bi, c: (0, 0)),
                  pl.BlockSpec((LANES, SSM_D_INNER), lambda bi, c: (0, 0))],
        out_specs=[pl.BlockSpec((None, q, SSM_D_INNER), lambda bi, c: (bi, c, 0)),
                   pl.BlockSpec((None, None, n, SSM_D_INNER), lambda bi, c: (bi, c, 0, 0))],
        out_shape=[jax.ShapeDtypeStruct((b, s, SSM_D_INNER), F32),
                   jax.ShapeDtypeStruct((b, nc, n, SSM_D_INNER), F32)],
        scratch_shapes=[pltpu.VMEM((n, SSM_D_INNER), F32)],
        compiler_params=_params("parallel", "arbitrary"),
    )(xc, dtr, dt_bias, a_log, dskx, to_channels)


def _ssd_bwd(xc, dtr, dy, hs, dt_bias, a_log, dskx, to_channels, to_heads, dproj, name):
    b, s, _ = xc.shape
    q = SSM_CHUNK
    nc = s // q
    n, gc = SSM_D_STATE, SSM_GROUP_CHANNELS

    def colsum(v):
        return jnp.sum(v, axis=0, keepdims=True)

    def body(xc_ref, dtr_ref, dy_ref, hs_ref, bias_ref, alog_ref, dsk_ref, tc_ref, th_ref, buf_ref,
             dxc_ref, ddtr_ref, dalog_ref, ddsk_ref, dbias_ref, dh_scr, dxs_scr, dxd_scr, w_scr, dst_scr, rows_scr):
        ci = pl.program_id(1)

        @pl.when(ci == 0)
        def _():
            dh_scr[...] = jnp.zeros_like(dh_scr)

        @pl.when(jnp.logical_and(pl.program_id(0) == 0, ci == 0))
        def _():
            dalog_ref[...] = jnp.zeros_like(dalog_ref)
            ddsk_ref[...] = jnp.zeros_like(ddsk_ref)
            dbias_ref[...] = jnp.zeros_like(dbias_ref)
            dst_scr[...] = jnp.zeros_like(dst_scr)

        dt, a_neg, s_col, s_row, lower = _ssd_chunk_terms(dtr_ref, bias_ref, alog_ref)
        upper = jnp.logical_not(lower) | (lax.broadcasted_iota(jnp.int32, (q, q), 0)
                                          == lax.broadcasted_iota(jnp.int32, (q, q), 1))
        dtx, esx, decx, etotx = _decay_terms_per_channel(dt, s_col, tc_ref[...])
        x = xc_ref[:, :SSM_D_INNER]
        dyv = dy_ref[...]
        xdt = x * dtx
        xdec = xdt * decx
        dw = esx * dyv
        rows_scr[...] = jnp.zeros_like(rows_scr)
        for g in range(SSM_N_GROUPS):
            b_lo = SSM_D_INNER + n * g
            c_lo = SSM_D_INNER + n * (SSM_N_GROUPS + g)
            bg = xc_ref[:, b_lo:b_lo + n].astype(BF16)
            cg = xc_ref[:, c_lo:c_lo + n].astype(BF16)
            gsl = slice(gc * g, gc * (g + 1))
            gm = _nt(cg, bg)
            gmt = _nt(bg, cg)
            hgt = hs_ref[:, gsl]
            dhgt = dh_scr[:, gsl]
            w_scr[:, gsl] = _nn(cg, hgt)
            dcg = _nt(dw[:, gsl], hgt)
            dxs = decx[:, gsl] * _nn(bg, dhgt)
            dxs_scr[:, gsl] = dxs
            dbg = _nt(xdec[:, gsl], dhgt)
            rows_scr[2:3, gsl] = colsum(dhgt * hgt)
            dh_scr[:, gsl] = _tn(cg, dw[:, gsl]) + etotx[:, gsl] * dhgt
            dg = jnp.zeros((q, q), F32)
            dgt = jnp.zeros((q, q), F32)
            for k in range(SSM_PAIRS_PER_GROUP):
                h0 = g * SSM_HEADS_PER_GROUP + 2 * k
                lo = gc * g + LANES * k
                xp = xdt[:, lo:lo + LANES]
                dyp = dyv[:, lo:lo + LANES]
                dy2 = _split_pair(dyp)
                dm2 = _nt(dy2, xp)
                dmt2 = _nt(_split_pair(xp), dyp)
                mts = []
                for i, h in enumerate((h0, h0 + 1)):
                    lm = jnp.exp(jnp.where(lower, s_col[:, h:h + 1] - s_row[h:h + 1, :], NEG_INF))
                    lmt = jnp.exp(jnp.where(upper, s_row[h:h + 1, :] - s_col[:, h:h + 1], NEG_INF))
                    dm = dm2[q * i:q * (i + 1), :]
                    dmt = dmt2[q * i:q * (i + 1), :]
                    dg = dg + dm * lm
                    dgt = dgt + dmt * lmt
                    mt = gmt * lmt
                    dst_scr[h:h + 1, :] = colsum(dmt * mt) - colsum(dm * (gm * lm))
                    mts.append(mt.astype(BF16))
                dxd_scr[:, lo:lo + LANES] = _nn(jnp.concatenate(mts, axis=1), dy2)
            dxc_ref[:, b_lo:b_lo + n] = dbg + _nn(dgt, cg)
            dxc_ref[:, c_lo:c_lo + n] = dcg + _nn(dg, bg)
        dxs = dxs_scr[...]
        dxdt = dxd_scr[...] + dxs
        dxc_ref[:, :SSM_D_INNER] = dxdt * dtx + dsk_ref[...] * dyv
        state_part = xdt * dxs
        rows_scr[0:1, :] = colsum(dyv * x)
        rows_scr[1:2, :] = colsum(state_part)
        th = th_ref[...]
        per_head = _per_head(jnp.concatenate([dw * w_scr[...] - state_part, dxdt * x], axis=0), th)
        r_ds, r_dt = per_head[:q], per_head[q:]
        sums = _per_head(rows_scr[...], th)
        etot = jnp.exp(s_col[q - 1:q, :])
        dtot = sums[1:2, :] + etot * sums[2:3, :]
        last = lax.broadcasted_iota(jnp.int32, (q, LANES), 0) == q - 1
        ds = dst_scr[...].T + r_ds + jnp.where(last, dtot, 0.0)
        da = _mask_nn(upper, ds)
        ddt = da * a_neg + r_dt
        live = lax.broadcasted_iota(jnp.int32, (1, LANES), 1) < SSM_N_HEADS
        sg = _sigmoid(dtr_ref[...] + bias_ref[...])
        ddtr = jnp.where(live, ddt * sg, 0.0)
        ddtr_ref[:, :LANES] = ddtr.astype(BF16)
        ddtr_ref[:, LANES:] = jnp.zeros((q, DPROJ_DT_WIDTH - LANES), BF16)
        dalog_ref[...] += jnp.where(live, colsum(da * dt) * a_neg, 0.0)
        ddsk_ref[...] += jnp.where(live, sums[0:1, :], 0.0)
        dbias_ref[...] += colsum(ddtr)

    rev = lambda bi, c: (bi, nc - 1 - c, 0)
    vec = pl.BlockSpec((1, LANES), lambda bi, c: (0, 0))
    wide = pl.BlockSpec((None, q, SSM_D_INNER), rev)
    return pl.pallas_call(
        body, name=name, grid=(b, nc),
        in_specs=[pl.BlockSpec((None, q, SSM_CONV_DIM), rev), pl.BlockSpec((None, q, LANES), rev), wide,
                  pl.BlockSpec((None, None, n, SSM_D_INNER), lambda bi, c: (bi, nc - 1 - c, 0, 0)),
                  vec, vec, pl.BlockSpec((1, SSM_D_INNER), lambda bi, c: (0, 0)),
                  pl.BlockSpec((LANES, SSM_D_INNER), lambda bi, c: (0, 0)),
                  pl.BlockSpec((SSM_D_INNER, LANES), lambda bi, c: (0, 0)),
                  pl.BlockSpec(memory_space=pl.ANY)],
        out_specs=[pl.BlockSpec((None, q, SSM_CONV_DIM), rev),
                   pl.BlockSpec((None, q, DPROJ_DT_WIDTH),
                                lambda bi, c: (bi, nc - 1 - c, DPROJ_COLS["dt"] // DPROJ_DT_WIDTH)), vec, vec, vec],
        input_output_aliases={9: 1},
        out_shape=[jax.ShapeDtypeStruct((b, s, SSM_CONV_DIM), F32), jax.ShapeDtypeStruct(dproj.shape, dproj.dtype),
                   jax.ShapeDtypeStruct((1, LANES), F32), jax.ShapeDtypeStruct((1, LANES), F32),
                   jax.ShapeDtypeStruct((1, LANES), F32)],
        scratch_shapes=[pltpu.VMEM((n, SSM_D_INNER), F32)] + [pltpu.VMEM((q, SSM_D_INNER), F32)] * 3
        + [pltpu.VMEM((LANES, q), F32), pltpu.VMEM((8, SSM_D_INNER), F32)],
        compiler_params=_params("arbitrary", "arbitrary"),
    )(xc, dtr, dy, hs, dt_bias, a_log, dskx, to_channels, to_heads, dproj)


SSM_GROUP_WIDTH = SSM_D_INNER // SSM_N_GROUPS


def _gate_norm_fwd(y, z, w, name):
    t, d = y.shape
    tm = _pick(t, (256, 128))

    def body(y_ref, z_ref, w_ref, o_ref):
        for g in range(SSM_N_GROUPS):
            sl = slice(SSM_GROUP_WIDTH * g, SSM_GROUP_WIDTH * (g + 1))
            zv = z_ref[:, sl]
            u = y_ref[:, sl] * (zv * _sigmoid(zv))
            r = lax.rsqrt(jnp.mean(u * u, axis=-1, keepdims=True) + EPS)
            o_ref[:, sl] = ((u * r) * w_ref[:, sl]).astype(BF16)

    row = pl.BlockSpec((tm, d), lambda i: (i, 0))
    return pl.pallas_call(
        body, name=name, grid=(t // tm,),
        in_specs=[row, row, pl.BlockSpec((1, d), lambda i: (0, 0))], out_specs=row,
        out_shape=jax.ShapeDtypeStruct((t, d), BF16),
        compiler_params=_params("parallel"),
    )(y, z, w)


def _ssm_out_dx_gate_norm_bwd(dys, w_ssm_out, y, z, w, dproj, name):
    t, d = y.shape
    k = dys.shape[1]
    gw = SSM_GROUP_WIDTH
    tm = _pick(t, (512, 256, 128))

    def body(dys_ref, ws_ref, y_ref, z_ref, w_ref, buf_ref, dy_ref, dz_ref, dw_ref):
        @pl.when(pl.program_id(0) == 0)
        def _():
            dw_ref[...] = jnp.zeros_like(dw_ref)

        dout = _nt(dys_ref[...], ws_ref[...])
        for g in range(SSM_N_GROUPS):
            sl = slice(gw * g, gw * (g + 1))
            zv = z_ref[:, sl]
            yv = y_ref[:, sl]
            sg = _sigmoid(zv)
            silu = zv * sg
            u = yv * silu
            r = lax.rsqrt(jnp.mean(u * u, axis=-1, keepdims=True) + EPS)
            uh = u * r
            dov = dout[:, sl]
            dw_ref[:, sl] += jnp.sum(dov * uh, axis=0, keepdims=True)
            dyg = dov * w_ref[:, sl]
            du = r * (dyg - uh * jnp.mean(dyg * uh, axis=-1, keepdims=True))
            dy_ref[:, sl] = du * silu
            dz_ref[:, sl] = (du * yv * (sg * (1.0 + zv * (1.0 - sg)))).astype(BF16)

    row = pl.BlockSpec((tm, d), lambda i: (i, 0))
    vec = pl.BlockSpec((1, d), lambda i: (0, 0))
    z_cols = pl.BlockSpec((tm, d), lambda i: (i, DPROJ_COLS["z"] // d))
    return pl.pallas_call(
        body, name=name, grid=(t // tm,),
        in_specs=[pl.BlockSpec((tm, k), lambda i: (i, 0)), pl.BlockSpec((d, k), lambda i: (0, 0)), row, row, vec,
                  pl.BlockSpec(memory_space=pl.ANY)],
        out_specs=[row, z_cols, vec],
        out_shape=[jax.ShapeDtypeStruct((t, d), F32), jax.ShapeDtypeStruct(dproj.shape, dproj.dtype),
                   jax.ShapeDtypeStruct((1, d), F32)],
        input_output_aliases={5: 1},
        compiler_params=_params("arbitrary"),
    )(dys, w_ssm_out, y, z, w, dproj)


def _rope_tables(s):
    half = ATT_HEAD_DIM // 2
    inv = ROPE_THETA ** (-jnp.arange(half, dtype=F32) / half)
    ang = jnp.arange(s).astype(F32)[:, None] * inv[None, :]
    cos, sin = jnp.cos(ang), jnp.sin(ang)
    return jnp.concatenate([cos, cos], axis=-1), jnp.concatenate([-sin, sin], axis=-1)


ATT_TILE = 256


def _by_residue_spec(r, width):
    return pl.BlockSpec((None, r, ATT_TILE // r, width), lambda bi, i: (bi, 0, i, 0))


def _to_residues(tile, stage, r, store):
    if r == 1:
        store(0, tile)
        return
    stage[...] = tile
    for ri in range(r):
        store(ri, stage[pl.ds(ri, tile.shape[0] // r, stride=r), :])


def _from_residues(load, stage, r):
    if r == 1:
        return load(0)
    for ri in range(r):
        stage[pl.ds(ri, ATT_TILE // r, stride=r), :] = load(ri)
    return stage[...]


QKV_ROWS = 1024
QKV_COLS = 768


def _qkv_proj_rope(h, w_qkv_t, cosf, sinf, b, s, name):
    t, k = h.shape
    tm, d, gw = QKV_ROWS, ATT_HEAD_DIM, ATT_OUT_DIM
    per_seq = s // tm

    def body(h_ref, w_ref, c_ref, s_ref, *rest):
        outs, stage = rest[:-1], rest[-1]
        cv, sv = c_ref[...], s_ref[...]
        hv = h_ref[...]
        for lo in range(0, ATT_QKV_DIM, QKV_COLS):
            acc = _nt(hv, w_ref[lo:lo + QKV_COLS, :])
            for hh in range(QKV_COLS // d):
                kind, head = divmod(lo // d + hh, ATT_N_HEADS)
                gi, j = divmod(head, ATT_HEADS_PER_GROUP)
                dst = slice(kind * gw + d * j, kind * gw + d * (j + 1))
                tv = acc[:, d * hh:d * (hh + 1)]
                if kind < 2:
                    tv = tv * cv + pltpu.roll(tv, d // 2, 1) * sv

                def store(ri, rows, o_ref=outs[gi], dst=dst):
                    o_ref[ri, :, dst] = rows.astype(BF16)

                _to_residues(tv, stage, ATT_DILATIONS[gi], store)

    tab = pl.BlockSpec((tm, d), lambda i: (i % per_seq, 0))
    return pl.pallas_call(
        body, name=name, grid=(t // tm,),
        in_specs=[pl.BlockSpec((tm, k), lambda i: (i, 0)), pl.BlockSpec((ATT_QKV_DIM, k), lambda i: (0, 0)), tab, tab],
        out_specs=[pl.BlockSpec((None, r, tm // r, 3 * gw), lambda i: (i // per_seq, 0, i % per_seq, 0))
                   for r in ATT_DILATIONS],
        out_shape=[jax.ShapeDtypeStruct((b, r, s // r, 3 * gw), BF16) for r in ATT_DILATIONS],
        scratch_shapes=[pltpu.VMEM((tm, d), F32)],
        compiler_params=_params("parallel"),
    )(h, w_qkv_t, cosf, sinf)


def _rope_bwd(dq, dk, dv, cosf, sinf, dproj, name):
    n_pat = len(ATT_DILATIONS)
    b, _, s, gw = dq[0].shape
    ts, d = ATT_TILE, ATT_HEAD_DIM

    def body(*refs):
        ins, (c_ref, s_ref, _, o_ref, stage) = refs[:3 * n_pat], refs[3 * n_pat:]
        cv, sv = c_ref[...], s_ref[...]
        for kind in range(3):
            for gi, r in enumerate(ATT_DILATIONS):
                src = ins[kind * n_pat + gi]
                for j in range(ATT_HEADS_PER_GROUP):
                    tv = _from_residues(lambda ri, src=src, j=j: src[ri, :, d * j:d * (j + 1)], stage, r)
                    if kind < 2:
                        tv = tv * cv + pltpu.roll(tv * sv, d // 2, 1)
                    lo = d * (kind * ATT_N_HEADS + gi * ATT_HEADS_PER_GROUP + j)
                    o_ref[:, lo:lo + d] = tv.astype(BF16)

    tab = pl.BlockSpec((ts, d), lambda bi, i: (i, 0))
    parts = [_by_residue_spec(r, gw) for r in ATT_DILATIONS]
    return pl.pallas_call(
        body, name=name, grid=(b, s // ts), in_specs=parts * 3 + [tab, tab, pl.BlockSpec(memory_space=pl.ANY)],
        out_specs=pl.BlockSpec((None, ts, ATT_QKV_DIM), lambda bi, i: (bi, i, DPROJ_COLS["qkv"] // ATT_QKV_DIM)),
        out_shape=jax.ShapeDtypeStruct(dproj.shape, dproj.dtype),
        input_output_aliases={3 * n_pat + 2: 0},
        scratch_shapes=[pltpu.VMEM((ts, d), F32)],
        compiler_params=_params("parallel", "parallel"),
    )(*dq, *dk, *dv, cosf, sinf, dproj)


ATT_SCALE = ATT_HEAD_DIM ** -0.5
ATT_STEP = 2 * ATT_BLOCK


def _att_spec(col):
    return pl.BlockSpec((None, None, ATT_STEP, ATT_OUT_DIM), lambda bi, ri, i: (bi, ri, i, col))


def _att_edge_spec(col, side, n_steps):
    def index(bi, ri, i):
        blk = 2 * i - 1 if side < 0 else 2 * i + 2
        return (bi, ri, jnp.clip(blk, 0, 2 * n_steps - 1), col)
    return pl.BlockSpec((None, None, ATT_BLOCK, ATT_OUT_DIM), index)


def _band_mask(shape, q_axis, has_prev):
    qi = lax.broadcasted_iota(jnp.int32, shape, q_axis)
    kj = lax.broadcasted_iota(jnp.int32, shape, 1 - q_axis)
    dist = qi + ATT_BLOCK - kj
    return (dist >= 0) & (dist <= ATT_BLOCK) & (has_prev | (kj >= ATT_BLOCK))


def _att_fwd(qkr, name):
    b, r, l, _ = qkr.shape
    nb = l // ATT_STEP
    d = ATT_HEAD_DIM

    def body(q_ref, kp_ref, k_ref, vp_ref, v_ref, o_ref, lse_ref):
        mask = _band_mask((ATT_STEP, ATT_BLOCK + ATT_STEP), 0, pl.program_id(2) > 0)
        heads = [slice(d * j, d * (j + 1)) for j in range(ATT_HEADS_PER_GROUP)]
        scores = [_nt(q_ref[:, sl], jnp.concatenate([kp_ref[:, sl], k_ref[:, sl]], axis=0)) for sl in heads]
        scores = [jnp.where(mask, sc * ATT_SCALE, NEG_INF) for sc in scores]
        tops = [jnp.max(sc, axis=-1, keepdims=True) for sc in scores]
        probs = [jnp.exp(sc - m) for sc, m in zip(scores, tops)]
        dens = [jnp.sum(pr, axis=-1, keepdims=True) for pr in probs]
        for sl, m, pr, den in zip(heads, tops, probs, dens):
            o_ref[:, sl] = _nn(pr / den, jnp.concatenate([vp_ref[:, sl], v_ref[:, sl]], axis=0))
            lse_ref[:, sl] = jnp.broadcast_to(m + jnp.log(den), (ATT_STEP, d))

    out_spec = _att_spec(0)
    return pl.pallas_call(
        body, name=name, grid=(b, r, nb),
        in_specs=[_att_spec(0), _att_edge_spec(1, -1, nb), _att_spec(1), _att_edge_spec(2, -1, nb), _att_spec(2)],
        out_specs=[out_spec, out_spec],
        out_shape=[jax.ShapeDtypeStruct((b, r, l, ATT_OUT_DIM), F32)] * 2,
        compiler_params=_params("parallel", "parallel", "parallel"),
    )(qkr, qkr, qkr, qkr, qkr)


def _att_merge(os_, lses, name):
    n_pat = len(os_)
    b, _, s, gw = os_[0].shape
    ts, d = ATT_TILE, ATT_HEAD_DIM

    def body(*refs):
        o_refs, l_refs = refs[:n_pat], refs[n_pat:2 * n_pat]
        att_ref, lse_outs, stage = refs[2 * n_pat], refs[2 * n_pat + 1:3 * n_pat + 1], refs[-1]
        for j in range(ATT_HEADS_PER_GROUP):
            sl = slice(d * j, d * (j + 1))
            ov = [_from_residues(lambda ri, g=g: o_refs[g][ri, :, sl], stage, r)
                  for g, r in enumerate(ATT_DILATIONS)]
            ls = [_from_residues(lambda ri, g=g: l_refs[g][ri, :, sl], stage, r)
                  for g, r in enumerate(ATT_DILATIONS)]
            m = functools.reduce(jnp.maximum, ls)
            es = [jnp.exp(lv - m) for lv in ls]
            tot = functools.reduce(lambda u, v: u + v, es)
            acc = (es[0] / tot) * ov[0]
            for g in range(1, n_pat):
                acc = acc + (es[g] / tot) * ov[g]
            att_ref[:, sl] = acc
            joint = m + jnp.log(tot)
            for g, r in enumerate(ATT_DILATIONS):
                def store(ri, rows, out=lse_outs[g]):
                    out[ri, :, sl] = rows
                _to_residues(joint, stage, r, store)

    parts = [_by_residue_spec(r, gw) for r in ATT_DILATIONS]
    return pl.pallas_call(
        body, name=name, grid=(b, s // ts), in_specs=parts * 2,
        out_specs=[pl.BlockSpec((None, ts, gw), lambda bi, i: (bi, i, 0))] + parts,
        out_shape=[jax.ShapeDtypeStruct((b, s, gw), F32)]
        + [jax.ShapeDtypeStruct((b, r, s // r, gw), F32) for r in ATT_DILATIONS],
        scratch_shapes=[pltpu.VMEM((ts, d), F32)],
        compiler_params=_params("parallel", "parallel"),
    )(*os_, *lses)


def _att_delta(att, datt, name):
    b, s, gw = att.shape
    ts, d = ATT_TILE, ATT_HEAD_DIM
    n_pat = len(ATT_DILATIONS)

    def body(a_ref, d_ref, *rest):
        do_outs, dl_outs, stage = rest[:n_pat], rest[n_pat:2 * n_pat], rest[-1]
        for j in range(ATT_HEADS_PER_GROUP):
            sl = slice(d * j, d * (j + 1))
            dv = d_ref[:, sl]
            delta = jnp.broadcast_to(jnp.sum(a_ref[:, sl] * dv, axis=-1, keepdims=True), (ts, d))
            for g, r in enumerate(ATT_DILATIONS):
                def store_do(ri, rows, out=do_outs[g]):
                    out[ri, :, sl] = rows.astype(BF16)

                def store_dl(ri, rows, out=dl_outs[g]):
                    out[ri, :, sl] = rows

                _to_residues(dv, stage, r, store_do)
                _to_residues(delta, stage, r, store_dl)

    row = pl.BlockSpec((None, ts, gw), lambda bi, i: (bi, i, 0))
    parts = [_by_residue_spec(r, gw) for r in ATT_DILATIONS]
    outs = pl.pallas_call(
        body, name=name, grid=(b, s // ts), in_specs=[row, row], out_specs=parts * 2,
        out_shape=[jax.ShapeDtypeStruct((b, r, s // r, gw), BF16) for r in ATT_DILATIONS]
        + [jax.ShapeDtypeStruct((b, r, s // r, gw), F32) for r in ATT_DILATIONS],
        scratch_shapes=[pltpu.VMEM((ts, d), F32)],
        compiler_params=_params("parallel", "parallel"),
    )(att, datt)
    return outs[:n_pat], outs[n_pat:]


def _att_bwd_q(qkr, datt, lse, delta, name):
    b, r, l, _ = qkr.shape
    nb = l // ATT_STEP
    d = ATT_HEAD_DIM

    def body(q_ref, kp_ref, k_ref, vp_ref, v_ref, do_ref, lse_ref, dl_ref, dq_ref):
        mask = _band_mask((ATT_STEP, ATT_BLOCK + ATT_STEP), 0, pl.program_id(2) > 0)
        heads = [slice(d * j, d * (j + 1)) for j in range(ATT_HEADS_PER_GROUP)]
        kcats = [jnp.concatenate([kp_ref[:, sl], k_ref[:, sl]], axis=0) for sl in heads]
        scores = [_nt(q_ref[:, sl], kcat) for sl, kcat in zip(heads, kcats)]
        dps = [_nt(do_ref[:, sl], jnp.concatenate([vp_ref[:, sl], v_ref[:, sl]], axis=0)) for sl in heads]
        probs = [jnp.exp(jnp.where(mask, sc * ATT_SCALE - lse_ref[:, sl.start:sl.start + 1], NEG_INF))
                 for sl, sc in zip(heads, scores)]
        dscs = [pr * (dp - dl_ref[:, sl.start:sl.start + 1]) for sl, pr, dp in zip(heads, probs, dps)]
        for sl, dsc, kcat in zip(heads, dscs, kcats):
            dq_ref[:, sl] = _nn(dsc, kcat) * ATT_SCALE

    tok = _att_spec(0)
    return pl.pallas_call(
        body, name=name, grid=(b, r, nb),
        in_specs=[_att_spec(0), _att_edge_spec(1, -1, nb), _att_spec(1), _att_edge_spec(2, -1, nb), _att_spec(2),
                  tok, tok, tok],
        out_specs=tok,
        out_shape=jax.ShapeDtypeStruct((b, r, l, ATT_OUT_DIM), F32),
        compiler_params=_params("parallel", "parallel", "parallel"),
    )(qkr, qkr, qkr, qkr, qkr, datt, lse, delta)


def _att_bwd_kv(qkr, datt, lse, delta, name):
    b, r, l, _ = qkr.shape
    nb = l // ATT_STEP
    d = ATT_HEAD_DIM

    def body(k_ref, v_ref, q_ref, qn_ref, do_ref, don_ref, lse_ref, lsen_ref, dl_ref, dln_ref, dk_ref, dv_ref):
        shape = (ATT_STEP, ATT_STEP + ATT_BLOCK)
        kj = lax.broadcasted_iota(jnp.int32, shape, 0)
        qi = lax.broadcasted_iota(jnp.int32, shape, 1)
        dist = qi - kj
        has_next = pl.program_id(2) < nb - 1
        mask = (dist >= 0) & (dist <= ATT_BLOCK) & (has_next | (qi < ATT_STEP))
        def per_query(own_ref, next_ref, sl):
            return jnp.tile(jnp.concatenate([own_ref[:, sl], next_ref[:, sl]], axis=0).T, (ATT_STEP // d, 1))

        heads = [slice(d * j, d * (j + 1)) for j in range(ATT_HEADS_PER_GROUP)]
        qcats = [jnp.concatenate([q_ref[:, sl], qn_ref[:, sl]], axis=0) for sl in heads]
        docats = [jnp.concatenate([do_ref[:, sl], don_ref[:, sl]], axis=0) for sl in heads]
        scores = [_nt(k_ref[:, sl], qcat) for sl, qcat in zip(heads, qcats)]
        dps = [_nt(v_ref[:, sl], docat) for sl, docat in zip(heads, docats)]
        probs = [jnp.exp(jnp.where(mask, sc * ATT_SCALE - per_query(lse_ref, lsen_ref, sl), NEG_INF))
                 for sl, sc in zip(heads, scores)]
        for sl, pr, docat in zip(heads, probs, docats):
            dv_ref[:, sl] = _nn(pr, docat)
        dscs = [pr * (dp - per_query(dl_ref, dln_ref, sl)) for sl, pr, dp in zip(heads, probs, dps)]
        for sl, dsc, qcat in zip(heads, dscs, qcats):
            dk_ref[:, sl] = _nn(dsc, qcat) * ATT_SCALE

    tok, tok_n = _att_spec(0), _att_edge_spec(0, 1, nb)
    return pl.pallas_call(
        body, name=name, grid=(b, r, nb),
        in_specs=[_att_spec(1), _att_spec(2), _att_spec(0), _att_edge_spec(0, 1, nb),
                  tok, tok_n, tok, tok_n, tok, tok_n],
        out_specs=[tok, tok],
        out_shape=[jax.ShapeDtypeStruct((b, r, l, ATT_OUT_DIM), F32)] * 2,
        compiler_params=_params("parallel", "parallel", "parallel"),
    )(qkr, qkr, qkr, qkr, datt, datt, lse, lse, delta, delta)


def _att_out_proj_mix(att, w_att_t, gl, bg, ys, name):
    t, k = att.shape
    d = w_att_t.shape[0]
    tm = _pick(t, (512, 256, 128))

    def body(a_ref, w_ref, gl_ref, bg_ref, ys_ref, ya_ref, o_ref):
        ya = _nt(a_ref[...], w_ref[...])
        ya_ref[...] = ya
        g0 = _sigmoid(gl_ref[:, :d] + bg_ref[:, :d])
        g1 = _sigmoid(gl_ref[:, d:] + bg_ref[:, d:])
        o_ref[...] = (g0 * ys_ref[...] + g1 * ya).astype(BF16)

    row = pl.BlockSpec((tm, d), lambda i: (i, 0))
    return pl.pallas_call(
        body, name=name, grid=(t // tm,),
        in_specs=[pl.BlockSpec((tm, k), lambda i: (i, 0)), pl.BlockSpec((d, k), lambda i: (0, 0)),
                  pl.BlockSpec((tm, 2 * d), lambda i: (i, 0)), pl.BlockSpec((1, 2 * d), lambda i: (0, 0)), row],
        out_specs=[row, row],
        out_shape=[jax.ShapeDtypeStruct((t, d), F32), jax.ShapeDtypeStruct((t, d), BF16)],
        compiler_params=_params("parallel"),
    )(att, w_att_t, gl, bg, ys)


def _mix_out_dx_mix_bwd(dx, w_mix, gl, bg, ys, ya, name):
    t, d = ys.shape
    tm = _pick(t, (512, 256, 128))

    def body(dx_ref, w_ref, gl_ref, bg_ref, ys_ref, ya_ref, dys_ref, dya_ref, dgl_ref, dbg_ref):
        @pl.when(pl.program_id(0) == 0)
        def _():
            dbg_ref[...] = jnp.zeros_like(dbg_ref)

        dm = _nt(dx_ref[...], w_ref[...])
        g0 = _sigmoid(gl_ref[:, :d] + bg_ref[:, :d])
        g1 = _sigmoid(gl_ref[:, d:] + bg_ref[:, d:])
        dys_ref[...] = (dm * g0).astype(BF16)
        dya_ref[...] = (dm * g1).astype(BF16)
        d0 = dm * ys_ref[...] * (g0 * (1.0 - g0))
        d1 = dm * ya_ref[...] * (g1 * (1.0 - g1))
        dgl_ref[:, :d] = d0.astype(BF16)
        dgl_ref[:, d:] = d1.astype(BF16)
        dbg_ref[:, :d] += jnp.sum(d0, axis=0, keepdims=True)
        dbg_ref[:, d:] += jnp.sum(d1, axis=0, keepdims=True)

    row = pl.BlockSpec((tm, d), lambda i: (i, 0))
    wide = pl.BlockSpec((tm, 2 * d), lambda i: (i, 0))
    vec = pl.BlockSpec((1, 2 * d), lambda i: (0, 0))
    gate_cols = pl.BlockSpec((tm, 2 * d), lambda i: (i, DPROJ_COLS["gate"] // (2 * d)))
    return pl.pallas_call(
        body, name=name, grid=(t // tm,),
        in_specs=[row, pl.BlockSpec((d, d), lambda i: (0, 0)), wide, vec, row, row],
        out_specs=[row, row, gate_cols, vec],
        out_shape=[jax.ShapeDtypeStruct((t, d), BF16), jax.ShapeDtypeStruct((t, d), BF16),
                   jax.ShapeDtypeStruct((t, DPROJ_WIDTH), BF16), jax.ShapeDtypeStruct((1, 2 * d), F32)],
        compiler_params=_params("arbitrary"),
    )(dx, w_mix, gl, bg, ys, ya)


def _up_proj_swiglu(h, w_up_t, gt, name):
    t, k = h.shape
    f = w_up_t.shape[0]
    tm, tn, _ = _mm_tiles(t, f, k, h.dtype.itemsize, w_up_t.dtype.itemsize, 4 + 2, True)

    def body(h_ref, w_ref, g_ref, up_ref, act_ref):
        up = _nt(h_ref[...], w_ref[...])
        up_ref[...] = up
        gv = g_ref[...]
        act_ref[...] = ((gv * _sigmoid(gv)) * up).astype(BF16)

    tile = pl.BlockSpec((tm, tn), lambda i, j: (i, j))
    return pl.pallas_call(
        body, name=name, grid=(t // tm, f // tn),
        in_specs=[pl.BlockSpec((tm, k), lambda i, j: (i, 0)), pl.BlockSpec((tn, k), lambda i, j: (j, 0)), tile],
        out_specs=[tile, tile],
        out_shape=[jax.ShapeDtypeStruct((t, f), F32), jax.ShapeDtypeStruct((t, f), BF16)],
        compiler_params=_params("parallel", "parallel"),
    )(h, w_up_t, gt)


def _down_dx_swiglu_bwd(dx, w_down, gt, up, name):
    t, k = dx.shape
    f = w_down.shape[0]
    tm, tn, _ = _mm_tiles(t, f, k, dx.dtype.itemsize, w_down.dtype.itemsize, 2 + 2, True)
    tm = min(tm, 512)

    def body(d_ref, w_ref, g_ref, u_ref, dg_ref, du_ref):
        dact = _nt(d_ref[...], w_ref[...])
        gv = g_ref[...]
        sg = _sigmoid(gv)
        dg_ref[...] = (dact * u_ref[...] * (sg * (1.0 + gv * (1.0 - sg)))).astype(BF16)
        du_ref[...] = (dact * (gv * sg)).astype(BF16)

    tile = pl.BlockSpec((tm, tn), lambda i, j: (i, j))
    return pl.pallas_call(
        body, name=name, grid=(t // tm, f // tn),
        in_specs=[pl.BlockSpec((tm, k), lambda i, j: (i, 0)), pl.BlockSpec((tn, k), lambda i, j: (j, 0)), tile, tile],
        out_specs=[tile, tile], out_shape=[jax.ShapeDtypeStruct((t, f), BF16)] * 2,
        compiler_params=_params("parallel", "parallel"),
    )(dx, w_down, gt, up)


def _peer(k):
    x, y, c = lax.axis_index("x"), lax.axis_index("y"), lax.axis_index("c")
    px, py, pc = x ^ ((k >> 2) & 1), y ^ ((k >> 1) & 1), c ^ (k & 1)
    return (px, py, pc), 4 * px + 2 * py + pc


def _my_index():
    return 4 * lax.axis_index("x") + 2 * lax.axis_index("y") + lax.axis_index("c")


def _all_gather(parts, name):
    n_parts = len(parts)

    def body(*refs):
        ins, outs = refs[:n_parts], refs[n_parts:2 * n_parts]
        send_sems, recv_sems, local_sems = refs[2 * n_parts:]
        here, me = _peer(0)
        sibling, sib_idx = _peer(1)
        chips = [_peer(2 * q) for q in range(1, N_CHIPS)]

        def copy(i, k, block, to, src=None):
            return pltpu.make_async_remote_copy(
                src_ref=outs[i].at[block] if src is None else src, dst_ref=outs[i].at[block],
                send_sem=send_sems.at[i * (N_DEV - 1) + k], recv_sem=recv_sems.at[i * (N_DEV - 1) + k],
                device_id=to, device_id_type=MESH)

        local = [pltpu.make_async_copy(ins[i], outs[i].at[me], local_sems.at[i]) for i in range(n_parts)]
        for cp in local:
            cp.start()
        sends = []
        for i in range(n_parts):
            sends.append(copy(i, 0, me, sibling, src=ins[i]))
            sends += [copy(i, q, me, chip, src=ins[i]) for q, (chip, _) in enumerate(chips, start=1)]
        for cp in sends:
            cp.start()
        for q, (chip, chip_idx) in enumerate(chips, start=1):
            for i in range(n_parts):
                copy(i, q, chip_idx, here).wait_recv()
                fwd = copy(i, N_CHIPS - 1 + q, chip_idx, sibling)
                fwd.start()
                sends.append(fwd)
        for i in range(n_parts):
            copy(i, 0, sib_idx, here).wait_recv()
        for q, (_, chip_idx) in enumerate(chips, start=1):
            for i in range(n_parts):
                copy(i, N_CHIPS - 1 + q, chip_idx ^ 1, here).wait_recv()
        for cp in sends:
            cp.wait_send()
        for cp in local:
            cp.wait()

    hbm = pl.BlockSpec(memory_space=pl.ANY)
    return pl.pallas_call(
        body, name=name, in_specs=[hbm] * n_parts, out_specs=[hbm] * n_parts,
        out_shape=[jax.ShapeDtypeStruct((N_DEV,) + p_.shape, p_.dtype) for p_ in parts],
        scratch_shapes=[pltpu.SemaphoreType.DMA((n_parts * (N_DEV - 1),)),
                        pltpu.SemaphoreType.DMA((n_parts * (N_DEV - 1),)),
                        pltpu.SemaphoreType.DMA((n_parts,))],
        compiler_params=pltpu.CompilerParams(has_side_effects=True),
    )(*parts)


HBM_SPEC = pl.BlockSpec(memory_space=pltpu.HBM)
SEM_SPEC = pl.BlockSpec(memory_space=pltpu.SEMAPHORE)
DATAFLOW = pltpu.SideEffectType.DATAFLOW_SIDE_EFFECTING


def _gather_start(block, after, name):
    per_peer = block.ndim == 3

    def body(v_ref, land_ref, after_ref, send_sems, recv_sems, v_thru, land_thru, token):
        me = _my_index()
        for k in range(1, N_DEV):
            peer, pidx = _peer(k)
            pltpu.make_async_remote_copy(
                src_ref=v_ref.at[pidx] if per_peer else v_ref, dst_ref=land_ref.at[me],
                send_sem=send_sems.at[k - 1], recv_sem=recv_sems.at[k - 1],
                device_id=peer, device_id_type=MESH).start()
        token[...] = jnp.zeros_like(token)

    land_shape = (N_DEV,) + block.shape[-2:]
    return pl.pallas_call(
        body, name=name,
        out_shape=(pltpu.SemaphoreType.DMA((N_DEV - 1,)), pltpu.SemaphoreType.DMA((N_DEV - 1,)),
                   pltpu.HBM(block.shape, block.dtype), pltpu.HBM(land_shape, block.dtype),
                   jax.ShapeDtypeStruct((8, LANES), F32)),
        in_specs=(HBM_SPEC, HBM_SPEC, pl.BlockSpec(memory_space=pl.ANY)),
        out_specs=(SEM_SPEC, SEM_SPEC, HBM_SPEC, HBM_SPEC, pl.BlockSpec(memory_space=pltpu.VMEM)),
        input_output_aliases={0: 2, 1: 3},
        compiler_params=pltpu.CompilerParams(has_side_effects=DATAFLOW),
    )(pltpu.with_memory_space_constraint(block, pltpu.HBM),
      pltpu.with_memory_space_constraint(lax.empty(land_shape, block.dtype), pltpu.HBM), after)


def _gather_wait(send_sems, recv_sems, block, landing, after, name):
    per_peer = block.ndim == 3

    def body(v_ref, land_ref, send_sems, recv_sems, after_ref, v_dead, got_ref):
        for k in range(1, N_DEV):
            peer, pidx = _peer(k)
            copy = pltpu.make_async_remote_copy(
                src_ref=v_ref.at[pidx] if per_peer else v_ref, dst_ref=land_ref.at[pidx],
                send_sem=send_sems.at[k - 1], recv_sem=recv_sems.at[k - 1],
                device_id=peer, device_id_type=MESH)
            copy.wait_send()
            copy.wait_recv()

    return pl.pallas_call(
        body, name=name,
        out_shape=(pltpu.HBM(block.shape, block.dtype), pltpu.HBM(landing.shape, landing.dtype)),
        in_specs=(HBM_SPEC, HBM_SPEC, SEM_SPEC, SEM_SPEC, pl.BlockSpec(memory_space=pl.ANY)),
        out_specs=(HBM_SPEC, HBM_SPEC), input_output_aliases={0: 0, 1: 1},
        compiler_params=pltpu.CompilerParams(has_side_effects=DATAFLOW),
    )(block, landing, send_sems, recv_sems, after)


TILE_ELEMS = 1024 * 1024


def _shared_exchange(shared, name):
    def body(sh_ref, gsh_ref, send_sems, recv_sems, local_sem):
        me = _my_index()
        local = pltpu.make_async_copy(sh_ref, gsh_ref.at[me], local_sem)
        local.start()
        sends = []
        for k in range(1, N_DEV):
            peer, _ = _peer(k)
            cp = pltpu.make_async_remote_copy(
                src_ref=sh_ref, dst_ref=gsh_ref.at[me], send_sem=send_sems.at[k - 1],
                recv_sem=recv_sems.at[k - 1], device_id=peer, device_id_type=MESH)
            cp.start()
            sends.append(cp)
        for k in range(1, N_DEV):
            peer, pidx = _peer(k)
            pltpu.make_async_remote_copy(
                src_ref=sh_ref, dst_ref=gsh_ref.at[pidx], send_sem=send_sems.at[k - 1],
                recv_sem=recv_sems.at[k - 1], device_id=peer, device_id_type=MESH).wait_recv()
        for cp in sends:
            cp.wait_send()
        local.wait()

    hbm = pl.BlockSpec(memory_space=pl.ANY)
    return pl.pallas_call(
        body, name=name, in_specs=[hbm], out_specs=hbm,
        out_shape=jax.ShapeDtypeStruct((N_DEV,) + shared.shape, shared.dtype),
        scratch_shapes=[pltpu.SemaphoreType.DMA((N_DEV - 1,)), pltpu.SemaphoreType.DMA((N_DEV - 1,)),
                        pltpu.SemaphoreType.DMA],
        compiler_params=pltpu.CompilerParams(has_side_effects=True),
    )(shared)


def _adamw(parts, w, m, v, name, row0=0, own=None):
    n_parts, rows, lanes = parts.shape
    tr = rows if rows * lanes <= TILE_ELEMS // 2 else _tile_rows(math.gcd(rows, row0), TILE_ELEMS // 4 // lanes, 8)
    c1 = 1.0 - ADAM_B1 ** ADAM_STEP
    c2 = 1.0 - ADAM_B2 ** ADAM_STEP

    def body(*refs):
        if own is None:
            p_ref, w_ref, m_ref, v_ref, g_ref, d_ref, nm_ref, nv_ref = refs
            terms = [p_ref[j].astype(F32) for j in range(n_parts)]
        else:
            me_ref, p_ref, own_ref, w_ref, m_ref, v_ref, g_ref, d_ref, nm_ref, nv_ref = refs
            terms = [jnp.where(me_ref[0] == j, own_ref[...], p_ref[j]).astype(F32) for j in range(n_parts)]
        g = terms[0]
        for term in terms[1:]:
            g = g + term
        nm = ADAM_B1 * m_ref[...] + (1.0 - ADAM_B1) * g
        nv = ADAM_B2 * v_ref[...] + (1.0 - ADAM_B2) * (g * g)
        g_ref[...] = g
        nm_ref[...] = nm
        nv_ref[...] = nv
        d_ref[...] = -ADAM_LR * ((nm / c1) / (jnp.sqrt(nv / c2) + ADAM_EPS) + ADAM_WD * w_ref[...])

    row = pl.BlockSpec((tr, lanes), lambda i, *_: (i, 0))
    state = pl.BlockSpec((tr, lanes), lambda i, *_: (row0 // tr + i, 0))
    in_specs = [pl.BlockSpec((n_parts, tr, lanes), lambda i, *_: (0, i, 0)), state, state, state]
    args, n_prefetch = (parts, w, m, v), 0
    if own is not None:
        slabs, me = own
        in_specs.insert(1, pl.BlockSpec((None, tr, lanes), lambda i, me_ref: (me_ref[0], i, 0)))
        args, n_prefetch = (me, parts, slabs, w, m, v), 1
    return pl.pallas_call(
        body, name=name,
        grid_spec=pltpu.PrefetchScalarGridSpec(num_scalar_prefetch=n_prefetch, grid=(rows // tr,),
                                               in_specs=in_specs, out_specs=[row] * 4),
        out_shape=[jax.ShapeDtypeStruct((rows, lanes), F32)] * 4,
        compiler_params=_params("parallel"),
    )(*args)


MATRIX_SHARDS = (
    ("w_in", (D_MODEL, IN_PROJ_DIM // N_DEV), True),
    ("w_ssm_out", (SSM_D_INNER // N_DEV, D_MODEL), False),
    ("w_att_out", (ATT_OUT_DIM, D_MODEL // N_DEV), True),
    ("w_mix_out", (D_MODEL // N_DEV, D_MODEL), False),
    ("w_ffn_gate", (D_MODEL, D_FF // N_DEV), True),
    ("w_ffn_up", (D_MODEL, D_FF // N_DEV), True),
    ("w_ffn_down", (D_FF // N_DEV, D_MODEL), False),
)
CONV_SHARD = ("conv_w", (SSM_CONV, SSM_CONV_DIM // N_DEV), True)
SHARDED = MATRIX_SHARDS + (CONV_SHARD,)
REPLICATED = (("norm_mix", D_MODEL), ("b_gate", 2 * D_MODEL), ("conv_b", SSM_CONV_DIM), ("dt_bias", SSM_N_HEADS),
              ("a_log", SSM_N_HEADS), ("d_skip", SSM_N_HEADS), ("ssm_norm", SSM_D_INNER), ("norm_ffn", D_MODEL),
              ("norm_final", D_MODEL))


def _round_up(n, mult):
    return -(-n // mult) * mult


def _pack_rows(flat, row_mult):
    rows = _round_up(-(-flat.shape[0] // LANES), row_mult)
    return jnp.pad(flat, (0, rows * LANES - flat.shape[0])).reshape(rows, LANES)


def _stacking(specs):
    return tuple((name, (shape[1], shape[0]) if by_cols else shape, by_cols) for name, shape, by_cols in specs)


def _to_stacking(vals, specs):
    return {name: (vals[name].T if by_cols else vals[name]) for name, _, by_cols in specs}


STACK_WIDTH = D_MODEL
STACK_ALIGN = 16
STACK_ORDER = ("w_ssm_out", "w_mix_out", "w_ffn_gate", "w_ffn_up", "w_ffn_down", "w_att_out", "conv_w", "w_in")
GATHER_LATER = STACK_ORDER[:-1]
REDUCE_EARLY = STACK_ORDER[:5]
REDUCE_LATE = STACK_ORDER[5:]


def _stack_layout():
    shapes = {name: shape for name, shape, _ in _stacking(SHARDED)}
    layout, off = {}, 0
    for name in STACK_ORDER:
        r, c = shapes[name]
        rows = r if c == STACK_WIDTH else _round_up(-(-(r * c) // STACK_WIDTH), STACK_ALIGN)
        layout[name] = (off, rows, (r, c))
        off = _round_up(off + rows, STACK_ALIGN)
    return layout, _round_up(off, 1024)


def _to_stack_rows(v, rows):
    if v.shape[-1] == STACK_WIDTH:
        return v
    lead = v.shape[:-2]
    flat = v.reshape(lead + (-1,))
    flat = jnp.pad(flat, [(0, 0)] * len(lead) + [(0, rows * STACK_WIDTH - flat.shape[-1])])
    return flat.reshape(lead + (rows, STACK_WIDTH))


def _from_stack_rows(block, shape):
    r, c = shape
    if c == STACK_WIDTH:
        return block
    lead = block.shape[:-2]
    return block.reshape(lead + (-1,))[..., :r * c].reshape(lead + (r, c))


def _stack(vals, dtype, skip=(), names=STACK_ORDER):
    layout, total = _stack_layout()
    order = names
    after = STACK_ORDER.index(order[-1]) + 1
    if after < len(STACK_ORDER):
        total = layout[STACK_ORDER[after]][0]
    lead = next(iter(vals.values())).shape[:-2]
    pieces = []
    for i, name in enumerate(order):
        off, rows, _ = layout[name]
        until = layout[order[i + 1]][0] if i + 1 < len(order) else total
        piece = jnp.zeros(lead + (rows, STACK_WIDTH), dtype) if name in skip else _to_stack_rows(vals[name], rows)
        pieces.append(jnp.pad(piece.astype(dtype), [(0, 0)] * len(lead) + [(0, until - off - rows), (0, 0)]))
    return jnp.concatenate(pieces, axis=-2)


def _unstack(stacked, names):
    layout, _ = _stack_layout()
    row0 = layout[names[0]][0]
    return {name: _from_stack_rows(stacked[..., layout[name][0] - row0:layout[name][0] - row0 + layout[name][1], :],
                                   layout[name][2]) for name in names}


W_IN_SHARD_ROWS = IN_PROJ_DIM // N_DEV


def _w_in_row_moves():
    moves, orig = [], 0
    for name, size in IN_SPLIT:
        for j in range(N_DEV):
            lo, hi = max(orig, W_IN_SHARD_ROWS * j), min(orig + size, W_IN_SHARD_ROWS * (j + 1))
            if lo < hi:
                moves.append((j, lo - W_IN_SHARD_ROWS * j, DPROJ_COLS[name] + lo - orig, hi - lo))
        orig += size
    return moves


def _w_in_from_shards(shards, name):
    total, base = shards.shape[1], 0
    pad_lo, pad_hi = DPROJ_COLS["dt"] + _round_up(SSM_N_HEADS, STACK_ALIGN), DPROJ_COLS["dt"] + DPROJ_DT_WIDTH

    def body(x_ref, o_ref):
        o_ref[pad_lo:pad_hi, :] = jnp.zeros((pad_hi - pad_lo, LANES), x_ref.dtype)
        for j, r, at, n in _w_in_row_moves():
            o_ref[at:at + n, :] = x_ref[j, base + r:base + r + n, :]

    return pl.pallas_call(
        body, name=name, grid=(STACK_WIDTH // LANES,),
        in_specs=[pl.BlockSpec((N_DEV, total, LANES), lambda c: (0, 0, c))],
        out_specs=pl.BlockSpec((DPROJ_WIDTH, LANES), lambda c: (0, c)),
        out_shape=jax.ShapeDtypeStruct((DPROJ_WIDTH, STACK_WIDTH), shards.dtype),
        compiler_params=_params("parallel"),
    )(shards)


def _w_in_to_shards(dw_all, head, name):
    layout, total = _stack_layout()
    total -= layout[REDUCE_LATE[0]][0]
    base = head.shape[1]
    end = base + W_IN_SHARD_ROWS

    def body(x_ref, h_ref, o_ref):
        o_ref[:, 0:base, :] = h_ref[...]
        for j, r, at, n in _w_in_row_moves():
            o_ref[j, base + r:base + r + n, :] = x_ref[at:at + n, :]
        o_ref[:, end:total, :] = jnp.zeros((N_DEV, total - end, LANES), o_ref.dtype)

    return pl.pallas_call(
        body, name=name, grid=(STACK_WIDTH // LANES,),
        in_specs=[pl.BlockSpec((DPROJ_WIDTH, LANES), lambda c: (0, c)),
                  pl.BlockSpec((N_DEV, base, LANES), lambda c: (0, 0, c))],
        out_specs=pl.BlockSpec((N_DEV, total, LANES), lambda c: (0, 0, c)),
        out_shape=jax.ShapeDtypeStruct((N_DEV, total, STACK_WIDTH), dw_all.dtype),
        compiler_params=_params("parallel"),
    )(dw_all, head)


REPLICATED_ROWS = sum(-(-size // LANES) for _, size in REPLICATED)
LOSS_ROW = REPLICATED_ROWS


def _pack_replicated(vals):
    rows = []
    for name, size in REPLICATED:
        v = vals[name].reshape(-1).astype(F32)
        rows.append(jnp.pad(v, (0, _round_up(size, LANES) - size)))
    return _pack_rows(jnp.concatenate(rows), 8)


def _unpack_replicated(packed, shapes):
    flat = packed.reshape(-1)
    out, off = {}, 0
    for name, size in REPLICATED:
        out[name] = flat[off:off + size].reshape(shapes[name])
        off += _round_up(size, LANES)
    return out


def _lane_row(v):
    v = v.reshape(-1).astype(F32)
    return jnp.pad(v, (0, LANES - v.shape[0])).reshape(1, LANES)


IN_SPLIT = (("z", SSM_D_INNER), ("xbc", SSM_CONV_DIM), ("dt", SSM_N_HEADS), ("qkv", ATT_QKV_DIM), ("gate", 2 * D_MODEL))


def kernel(x, norm_mix, w_in, b_gate, conv_w, conv_b, dt_bias, a_log, d_skip, ssm_norm, w_ssm_out, w_att_out, w_mix_out, norm_ffn, w_ffn_gate, w_ffn_up, w_ffn_down, norm_final, loss_target, m_norm_mix, m_w_in, m_b_gate, m_conv_w, m_conv_b, m_dt_bias, m_a_log, m_d_skip, m_ssm_norm, m_w_ssm_out, m_w_att_out, m_w_mix_out, m_norm_ffn, m_w_ffn_gate, m_w_ffn_up, m_w_ffn_down, m_norm_final, v_norm_mix, v_w_in, v_b_gate, v_conv_w, v_conv_b, v_dt_bias, v_a_log, v_d_skip, v_ssm_norm, v_w_ssm_out, v_w_att_out, v_w_mix_out, v_norm_ffn, v_w_ffn_gate, v_w_ffn_up, v_w_ffn_down, v_norm_final):
    given = dict(locals())
    weights = {name: given[name][0] for name, _, _ in SHARDED}
    b, s, d = x.shape
    t = b * s

    stacking = _to_stacking(weights, SHARDED)
    conv_shape = dict((name, shape) for name, shape, _ in _stacking(SHARDED))["conv_w"]
    w_in_local = jnp.pad(stacking["w_in"].astype(BF16), ((0, -W_IN_SHARD_ROWS % STACK_ALIGN), (0, 0)))
    conv_local = _pack_rows(stacking["conv_w"].reshape(-1), 8)
    w_in_shards, conv_all = _all_gather([w_in_local, conv_local], "w_in_all_gather")
    head_local = _stack(stacking, BF16, skip=("conv_w",), names=GATHER_LATER)
    in_flight = _gather_start(head_local, conv_all, "weights_gather_start")
    w_in_all = _w_in_from_shards(w_in_shards, "w_in_from_shards")
    w_sec = {name: w_in_all[DPROJ_COLS[name]:DPROJ_COLS[name] + _round_up(size, LANES)] for name, size in IN_SPLIT}
    conv_size = conv_shape[0] * conv_shape[1]
    conv_taps = conv_all.reshape(N_DEV, -1)[:, :conv_size].reshape(N_DEV * conv_shape[0], conv_shape[1]).T

    g_mix, g_ffn, g_fin = norm_mix.reshape(1, d), norm_ffn.reshape(1, d), norm_final.reshape(1, d)
    g_mix = g_mix + in_flight[4][:1, :1]
    bg_row = b_gate.reshape(1, 2 * d)
    convb_row = conv_b.reshape(1, SSM_CONV_DIM)
    ssmn_row = ssm_norm.reshape(1, SSM_D_INNER)
    dtb_row, alog_row = _lane_row(dt_bias), _lane_row(a_log)
    cosf, sinf = _rope_tables(s)

    x2d = x.reshape(t, d)
    h1 = _rmsnorm_fwd(x2d, g_mix, "norm_mix_fwd")
    proj = {name: _mm(h1, w_sec[name], mode="nt", name="in_proj_" + name) for name, _ in IN_SPLIT if name != "qkv"}
    xbc3 = proj["xbc"].reshape(b, s, SSM_CONV_DIM)
    xc = _conv_fwd(xbc3, conv_taps, convb_row, "conv_fwd")
    dtr3 = proj["dt"].reshape(b, s, DT_PAD)
    to_channels, to_heads = _head_masks()
    dskx = jnp.repeat(d_skip.reshape(-1).astype(F32), SSM_HEAD_DIM).reshape(1, SSM_D_INNER)
    y_ssd, h_states = _ssd_fwd(xc, dtr3, dtb_row, alog_row, dskx, to_channels, "ssd_fwd")
    y_ssd2 = y_ssd.reshape(t, SSM_D_INNER)
    ynorm = _gate_norm_fwd(y_ssd2, proj["z"], ssmn_row, "ssd_gate_norm_fwd")
    head_local, landed = _gather_wait(*in_flight[:4], ynorm, "weights_gather_wait")
    head_all = lax.dynamic_update_slice(landed, head_local[None], (_my_index(), 0, 0))
    full = {name: v.reshape((-1,) + v.shape[2:]) for name, v in _unstack(head_all, STACK_ORDER[:-2]).items()}
    y_ssm = _mm(ynorm, full["w_ssm_out"], mode="nn", name="ssm_out_proj")

    qk_parts = _qkv_proj_rope(h1, w_sec["qkv"], cosf, sinf, b, s, "in_proj_qkv_rope")
    att_parts = [_att_fwd(qk_parts[gi], "att_fwd_%d" % r) for gi, r in enumerate(ATT_DILATIONS)]
    att, *lse_parts = _att_merge([o for o, _ in att_parts], [l_ for _, l_ in att_parts], "att_merge")
    att2 = att.reshape(t, ATT_OUT_DIM)
    y_att, mixed = _att_out_proj_mix(att2, full["w_att_out"], proj["gate"], bg_row, y_ssm, "att_out_proj_mix")
    x2, h2 = _proj_residual_norm(mixed, full["w_mix_out"], x2d, g_ffn, "mix_out_proj_norm")
    gt = _mm(h2, full["w_ffn_gate"], mode="nt", name="ffn_gate_proj")
    up, act = _up_proj_swiglu(h2, full["w_ffn_up"], gt, "ffn_up_proj_swiglu")

    loss_row, dx3, dg_fin, dx3b = _down_proj_loss_head(act, full["w_ffn_down"], x2, g_fin, loss_target.reshape(t, d),
                                                       "ffn_down_proj_loss_head")
    grads = {}
    grads["w_ffn_down"] = _mm(act, dx3b, mode="tn", name="ffn_down_dw", out_dtype=BF16)
    dgt, dup = _down_dx_swiglu_bwd(dx3b, full["w_ffn_down"], gt, up, "ffn_down_dx_swiglu_bwd")
    grads["w_ffn_gate"] = _mm(dgt, h2, mode="tn", name="ffn_gate_dw", out_dtype=BF16)
    grads["w_ffn_up"] = _mm(dup, h2, mode="tn", name="ffn_up_dw", out_dtype=BF16)
    dh2 = _mm(dgt, full["w_ffn_gate"], mode="nn", name="ffn_gate_dx")
    dx2, dg_ffn, dx2b = _proj_norm_bwd(dup, full["w_ffn_up"], x2, g_ffn, dx3, "ffn_up_dx_norm_bwd", add=dh2,
                                       with_bf16=True)

    grads["w_mix_out"] = _mm(mixed, dx2b, mode="tn", name="mix_out_dw", out_dtype=BF16)
    dys, dya, dproj, dbg = _mix_out_dx_mix_bwd(dx2b, full["w_mix_out"], proj["gate"], bg_row, y_ssm, y_att,
                                               "mix_out_dx_mix_bwd")

    grads["w_ssm_out"] = _mm(ynorm, dys, mode="tn", name="ssm_out_dw", out_dtype=BF16)
    early = _stack({name: grads[name].reshape((N_DEV, -1, STACK_WIDTH)) for name in REDUCE_EARLY}, BF16,
                   names=REDUCE_EARLY)
    early_flight = _gather_start(early, dys, "grads_scatter_start")
    ssmn_row = ssmn_row + early_flight[4][:1, :1]
    dy_ssd, dproj, dssmn = _ssm_out_dx_gate_norm_bwd(dys, full["w_ssm_out"], y_ssd2, proj["z"], ssmn_row, dproj,
                                                     "ssm_out_dx_gate_norm_bwd")
    dxc, dproj, dalog, ddsk, ddtb = _ssd_bwd(xc, dtr3, dy_ssd.reshape(b, s, SSM_D_INNER), h_states, dtb_row, alog_row,
                                             dskx, to_channels, to_heads, dproj.reshape(b, s, DPROJ_WIDTH), "ssd_bwd")
    dproj, dconvw, dconvb = _conv_bwd(xbc3, dxc, conv_taps, convb_row, dproj, "conv_bwd")
    grads["conv_w"] = dconvw.T.astype(BF16)

    grads["w_att_out"] = _mm(dya, att2, mode="tn", name="att_out_dw", out_dtype=BF16)
    datt = _mm(dya, full["w_att_out"], mode="nn", name="att_out_dx").reshape(b, s, ATT_OUT_DIM)
    do_parts, dl_parts = _att_delta(att, datt, "att_delta")
    dqs, dks, dvs = [], [], []
    for gi, r in enumerate(ATT_DILATIONS):
        operands = (qk_parts[gi], do_parts[gi], lse_parts[gi], dl_parts[gi])
        dqs.append(_att_bwd_q(*operands, "att_bwd_q_%d" % r))
        dk_g, dv_g = _att_bwd_kv(*operands, "att_bwd_kv_%d" % r)
        dks.append(dk_g)
        dvs.append(dv_g)
    dproj = _rope_bwd(dqs, dks, dvs, cosf, sinf, dproj, "rope_bwd").reshape(t, DPROJ_WIDTH)

    dw_all = _mm(dproj, h1, mode="tn", name="in_proj_dw", out_dtype=BF16)
    head = _stack({name: grads[name].reshape((N_DEV, -1, grads[name].shape[-1])) for name in REDUCE_LATE[:-1]}, BF16,
                  names=REDUCE_LATE[:-1])
    late = _w_in_to_shards(dw_all, head, "grad_stacks")
    late_flight = _gather_start(late, dw_all, "grads_late_scatter_start")
    grad_x, dg_mix = _proj_norm_bwd(dproj, w_in_all, x2d, g_mix, dx2, "in_proj_dx_norm_bwd", after=late_flight[4])

    small = {"norm_mix": dg_mix, "b_gate": dbg, "conv_b": dconvb, "dt_bias": ddtb[:, :SSM_N_HEADS],
             "a_log": dalog[:, :SSM_N_HEADS], "d_skip": ddsk[:, :SSM_N_HEADS], "ssm_norm": dssmn,
             "norm_ffn": dg_ffn, "norm_final": dg_fin}
    shared = _pack_replicated(small)
    shared = shared.at[LOSS_ROW, 0].set(loss_row[0, 0])
    got_small = _shared_exchange(shared, "shared_grads_exchange")

    def packed(prefix):
        vals = _to_stacking({name: given[prefix + name][0] for name, _, _ in SHARDED}, SHARDED)
        rep = {name: given[prefix + name] for name, _ in REPLICATED}
        return _stack(vals, F32), _pack_replicated(rep)

    (w_big, w_small), (m_big, m_small), (v_big, v_small) = packed(""), packed("m_"), packed("v_")
    me = _my_index().astype(jnp.int32).reshape(1)
    early, early_landed = _gather_wait(*early_flight[:4], got_small, "grads_scatter_wait")
    late, late_landed = _gather_wait(*late_flight[:4], got_small, "grads_late_scatter_wait")
    big_early = _adamw(early_landed, w_big, m_big, v_big, "adamw_early", own=(early, me))
    big_late = _adamw(late_landed, w_big, m_big, v_big, "adamw_late", row0=early.shape[1], own=(late, me))
    sml = _adamw(got_small, w_small, m_small, v_small, "adamw_replicated")

    outs = [sml[0][LOSS_ROW, 0], grad_x.reshape(b, s, d)]
    rep_shapes = {name: given[name].shape for name, _ in REPLICATED}
    order = ["norm_mix", "w_in", "b_gate", "conv_w", "conv_b", "dt_bias", "a_log", "d_skip", "ssm_norm", "w_ssm_out",
             "w_att_out", "w_mix_out", "norm_ffn", "w_ffn_gate", "w_ffn_up", "w_ffn_down", "norm_final"]
    for early_k, late_k, sml_k in zip(big_early, big_late, sml):
        stacks = dict(_unstack(early_k, REDUCE_EARLY), **_unstack(late_k, REDUCE_LATE))
        sharded = _to_stacking(stacks, SHARDED)
        rep = _unpack_replicated(sml_k, rep_shapes)
        for name in order:
            outs.append(sharded[name][None] if name in sharded else rep[name])
    return tuple(outs)
```

```python
import functools
import math

import jax
import jax.numpy as jnp
from jax import lax
from jax.experimental import pallas as pl
from jax.experimental.pallas import tpu as pltpu

F32 = jnp.float32
BF16 = jnp.bfloat16

N_DEV = 8
N_CHIPS = 4
D_MODEL = 1024
SSM_D_INNER = 2048
SSM_HEAD_DIM = 64
SSM_N_HEADS = 32
SSM_N_GROUPS = 4
SSM_HEADS_PER_GROUP = SSM_N_HEADS // SSM_N_GROUPS
SSM_D_STATE = 128
SSM_CONV = 4
SSM_CHUNK = 128
SSM_CONV_DIM = 3072
ATT_HEAD_DIM = 128
ATT_HEADS_PER_GROUP = 4
ATT_DILATIONS = (1, 4, 16)
ATT_N_HEADS = 12
ATT_QKV_DIM = 4608
ATT_OUT_DIM = 512
ATT_BLOCK = 128
ROPE_THETA = 10000.0
D_FF = 2816
IN_PROJ_DIM = 11808
EPS = 1e-6
LANES = 128
DT_PAD = LANES

DPROJ_COLS = {"qkv": 0, "xbc": 4608, "dt": 7680, "z": 8192, "gate": 10240}
DPROJ_DT_WIDTH = 512
DPROJ_WIDTH = 12288

ADAM_LR = 0.001
ADAM_B1 = 0.9
ADAM_B2 = 0.999
ADAM_EPS = 1e-08
ADAM_WD = 0.01
ADAM_STEP = 10

VMEM_LIMIT = 56 * 1024 * 1024
MESH = pl.DeviceIdType.MESH
NEG_INF = float("-inf")


def _tile_rows(n, cap, mult):
    return max(t for t in range(mult, min(n, cap) + 1, mult) if n % t == 0)


def _pick(n, candidates):
    for c in candidates:
        if n % c == 0:
            return c
    return n


def _params(*sem):
    return pltpu.CompilerParams(dimension_semantics=sem, vmem_limit_bytes=VMEM_LIMIT)


def _sigmoid(x):
    return 0.5 * jnp.tanh(0.5 * x) + 0.5


def _softplus(x):
    return jnp.maximum(x, 0.0) + jnp.log(1.0 + jnp.exp(-jnp.abs(x)))


def _dot(a, b, dims):
    return lax.dot_general(a.astype(BF16), b.astype(BF16), (dims, ((), ())), preferred_element_type=F32)


def _nn(a, b):
    return _dot(a, b, ((1,), (0,)))


def _nt(a, b):
    return _dot(a, b, ((1,), (1,)))


def _tn(a, b):
    return _dot(a, b, ((0,), (0,)))


def _split3(v):
    hi = v.astype(BF16)
    r1 = v - hi.astype(F32)
    mid = r1.astype(BF16)
    lo = (r1 - mid.astype(F32)).astype(BF16)
    return hi, mid, lo


def _mask_nn(mask, v):
    mb = mask.astype(BF16)
    hi, mid, lo = _split3(v)
    return _nn(mb, hi) + (_nn(mb, mid) + _nn(mb, lo))


MM_VMEM_BUDGET = 40 * 1024 * 1024
MM_FULL_K = 2816


def _mm_tiles(m, n, k, a_bytes, b_bytes, o_bytes, has_add):
    tk = k if k <= MM_FULL_K else _pick(k, (2048, 1024, 512, 256, 128))
    tn = 1408 if (n > 1024 and n % 1408 == 0) else _pick(n, (1024, 768, 512, 384, 256, 128))
    for tm in (1408, 1024, 768, 512, 384, 256, 128):
        if m % tm:
            continue
        buffers = 2 * (tm * tk * a_bytes + tk * tn * b_bytes + tm * tn * (o_bytes + (4 if has_add else 0)))
        if tk < k:
            buffers += tm * tn * 4
        if buffers <= MM_VMEM_BUDGET:
            return tm, tn, tk
    return _pick(m, (128,)), tn, tk


def _mm(a, b, *, mode, name, out_dtype=F32, add=None, after=None):
    if mode == "nn":
        (m, k), n = a.shape, b.shape[1]
    elif mode == "nt":
        (m, k), n = a.shape, b.shape[0]
    else:
        (k, m), n = a.shape, b.shape[1]
    has_add = add is not None
    tm, tn, tk = _mm_tiles(m, n, k, a.dtype.itemsize, b.dtype.itemsize, jnp.dtype(out_dtype).itemsize, has_add)
    nk = k // tk
    dims = {"nn": ((1,), (0,)), "nt": ((1,), (1,)), "tn": ((0,), (0,))}[mode]
    a_spec = {"nn": pl.BlockSpec((tm, tk), lambda i, j, kk: (i, kk)),
              "nt": pl.BlockSpec((tm, tk), lambda i, j, kk: (i, kk)),
              "tn": pl.BlockSpec((tk, tm), lambda i, j, kk: (kk, i))}[mode]
    b_spec = {"nn": pl.BlockSpec((tk, tn), lambda i, j, kk: (kk, j)),
              "nt": pl.BlockSpec((tn, tk), lambda i, j, kk: (j, kk)),
              "tn": pl.BlockSpec((tk, tn), lambda i, j, kk: (kk, j))}[mode]
    o_spec = pl.BlockSpec((tm, tn), lambda i, j, kk: (i, j))

    def finish(r, c_ref, o_ref):
        if has_add:
            r = r + c_ref[...]
        o_ref[...] = r.astype(out_dtype)

    def body_one(*refs):
        a_ref, b_ref = refs[:2]
        finish(_dot(a_ref[...], b_ref[...], dims), refs[2] if has_add else None, refs[-1])

    def body_acc(*refs):
        a_ref, b_ref = refs[:2]
        o_ref, acc = refs[-2:]
        kk = pl.program_id(2)

        @pl.when(kk == 0)
        def _():
            acc[...] = jnp.zeros_like(acc)

        acc[...] += _dot(a_ref[...], b_ref[...], dims)

        @pl.when(kk == nk - 1)
        def _():
            finish(acc[...], refs[2] if has_add else None, o_ref)

    in_specs = [a_spec, b_spec] + ([o_spec] if has_add else [])
    args = (a, b) + ((add,) if has_add else ())
    if after is not None:
        in_specs, args = in_specs + [pl.BlockSpec(memory_space=pl.ANY)], args + (after,)
    return pl.pallas_call(
        body_one if nk == 1 else body_acc, name=name, grid=(m // tm, n // tn, nk),
        in_specs=in_specs, out_specs=o_spec,
        out_shape=jax.ShapeDtypeStruct((m, n), out_dtype),
        scratch_shapes=[] if nk == 1 else [pltpu.VMEM((tm, tn), F32)],
        compiler_params=_params("parallel", "parallel", "arbitrary"),
    )(*args)


def _rmsnorm_fwd(x, g, name):
    t, d = x.shape
    tm = _pick(t, (512, 256, 128))

    def body(x_ref, g_ref, o_ref):
        xv = x_ref[...]
        r = lax.rsqrt(jnp.mean(xv * xv, axis=-1, keepdims=True) + EPS)
        o_ref[...] = ((xv * r) * g_ref[...]).astype(BF16)

    return pl.pallas_call(
        body, name=name, grid=(t // tm,),
        in_specs=[pl.BlockSpec((tm, d), lambda i: (i, 0)), pl.BlockSpec((1, d), lambda i: (0, 0))],
        out_specs=pl.BlockSpec((tm, d), lambda i: (i, 0)),
        out_shape=jax.ShapeDtypeStruct((t, d), BF16),
        compiler_params=_params("parallel"),
    )(x, g)


def _proj_residual_norm(a, w, res, g, name):
    t, k = a.shape
    d = w.shape[1]
    tm, _, _ = _mm_tiles(t, d, k, a.dtype.itemsize, w.dtype.itemsize, 4 + 2, True)

    def body(a_ref, w_ref, r_ref, g_ref, x_ref, h_ref):
        xv = r_ref[...] + _nn(a_ref[...], w_ref[...])
        x_ref[...] = xv
        r = lax.rsqrt(jnp.mean(xv * xv, axis=-1, keepdims=True) + EPS)
        h_ref[...] = ((xv * r) * g_ref[...]).astype(BF16)

    row = pl.BlockSpec((tm, d), lambda i: (i, 0))
    return pl.pallas_call(
        body, name=name, grid=(t // tm,),
        in_specs=[pl.BlockSpec((tm, k), lambda i: (i, 0)), pl.BlockSpec((k, d), lambda i: (0, 0)), row,
                  pl.BlockSpec((1, d), lambda i: (0, 0))],
        out_specs=[row, row],
        out_shape=[jax.ShapeDtypeStruct((t, d), F32), jax.ShapeDtypeStruct((t, d), BF16)],
        compiler_params=_params("parallel"),
    )(a, w, res, g)


def _proj_norm_bwd(a, w, x, g, dres, name, add=None, with_bf16=False, after=None):
    t, k = a.shape
    d = w.shape[1]
    has_add = add is not None
    tm, _, tk = _mm_tiles(t, d, k, a.dtype.itemsize, w.dtype.itemsize, 4 + 4 + 4 + (2 if with_bf16 else 0), has_add)
    if tk == k:
        tm = min(tm, 512)
    else:
        tm, tk = _pick(t, (1024, 512, 256, 128)), min(tk, 1024)
    nk = k // tk

    def body(*refs):
        a_ref, w_ref, x_ref, g_ref, dres_ref = refs[:5]
        rest = refs[5 + has_add + (after is not None):]
        dx_ref, dg_ref = rest[:2]
        i, kk = pl.program_id(0), pl.program_id(1)

        @pl.when(jnp.logical_and(i == 0, kk == 0))
        def _():
            dg_ref[...] = jnp.zeros_like(dg_ref)

        part = _nn(a_ref[...], w_ref[...])
        if nk > 1:
            acc = rest[-1]

            @pl.when(kk == 0)
            def _():
                acc[...] = jnp.zeros_like(acc)

            acc[...] += part

        @pl.when(kk == nk - 1)
        def _():
            dhv = part if nk == 1 else acc[...]
            if has_add:
                dhv = dhv + refs[5][...]
            xv = x_ref[...]
            r = lax.rsqrt(jnp.mean(xv * xv, axis=-1, keepdims=True) + EPS)
            xhat = xv * r
            dyg = dhv * g_ref[...]
            dx = dres_ref[...] + r * (dyg - xhat * jnp.mean(dyg * xhat, axis=-1, keepdims=True))
            dx_ref[...] = dx
            if with_bf16:
                rest[2][...] = dx.astype(BF16)
            dg_ref[...] += jnp.sum(dhv * xhat, axis=0, keepdims=True)

    row = pl.BlockSpec((tm, d), lambda i, kk: (i, 0))
    vec = pl.BlockSpec((1, d), lambda i, kk: (0, 0))
    in_specs = [pl.BlockSpec((tm, tk), lambda i, kk: (i, kk)), pl.BlockSpec((tk, d), lambda i, kk: (kk, 0)),
                row, vec, row] + has_add * [row]
    args = (a, w, x, g, dres) + has_add * (add,)
    if after is not None:
        in_specs, args = in_specs + [pl.BlockSpec(memory_space=pl.ANY)], args + (after,)
    return pl.pallas_call(
        body, name=name, grid=(t // tm, nk), in_specs=in_specs, out_specs=[row, vec] + with_bf16 * [row],
        out_shape=[jax.ShapeDtypeStruct((t, d), F32), jax.ShapeDtypeStruct((1, d), F32)]
        + with_bf16 * [jax.ShapeDtypeStruct((t, d), BF16)],
        scratch_shapes=[] if nk == 1 else [pltpu.VMEM((tm, d), F32)],
        compiler_params=_params("arbitrary", "arbitrary"),
    )(*args)


def _rmsnorm_bwd(x, g, dh, dres, name):
    t, d = x.shape
    tm = _pick(t, (512, 256, 128))

    def body(x_ref, g_ref, dh_ref, dres_ref, dx_ref, dg_ref):
        @pl.when(pl.program_id(0) == 0)
        def _():
            dg_ref[...] = jnp.zeros_like(dg_ref)

        xv = x_ref[...]
        r = lax.rsqrt(jnp.mean(xv * xv, axis=-1, keepdims=True) + EPS)
        xhat = xv * r
        dhv = dh_ref[...]
        dyg = dhv * g_ref[...]
        dx_ref[...] = dres_ref[...] + r * (dyg - xhat * jnp.mean(dyg * xhat, axis=-1, keepdims=True))
        dg_ref[...] += jnp.sum(dhv * xhat, axis=0, keepdims=True)

    row = pl.BlockSpec((tm, d), lambda i: (i, 0))
    vec = pl.BlockSpec((1, d), lambda i: (0, 0))
    return pl.pallas_call(
        body, name=name, grid=(t // tm,),
        in_specs=[row, vec, row, row], out_specs=[row, vec],
        out_shape=[jax.ShapeDtypeStruct((t, d), F32), jax.ShapeDtypeStruct((1, d), F32)],
        compiler_params=_params("arbitrary"),
    )(x, g, dh, dres)


def _down_proj_loss_head(act, w_down, res, g, target, name):
    t, k = act.shape
    d = w_down.shape[1]
    tm, _, _ = _mm_tiles(t, d, k, act.dtype.itemsize, w_down.dtype.itemsize, 4 + 2, True)
    tm = min(tm, 512)

    def body(a_ref, w_ref, r_ref, g_ref, t_ref, loss_ref, dx_ref, dg_ref, dxb_ref):
        @pl.when(pl.program_id(0) == 0)
        def _():
            dg_ref[...] = jnp.zeros_like(dg_ref)
            loss_ref[...] = jnp.zeros_like(loss_ref)

        xv = r_ref[...] + _nn(a_ref[...], w_ref[...])
        gv = g_ref[...]
        r = lax.rsqrt(jnp.mean(xv * xv, axis=-1, keepdims=True) + EPS)
        xhat = xv * r
        err = xhat * gv - t_ref[...]
        loss_ref[...] += jnp.sum(err * err) * (0.5 / d)
        dy = err * (1.0 / d)
        dyg = dy * gv
        dx = r * (dyg - xhat * jnp.mean(dyg * xhat, axis=-1, keepdims=True))
        dx_ref[...] = dx
        dxb_ref[...] = dx.astype(BF16)
        dg_ref[...] += jnp.sum(dy * xhat, axis=0, keepdims=True)

    row = pl.BlockSpec((tm, d), lambda i: (i, 0))
    vec = pl.BlockSpec((1, d), lambda i: (0, 0))
    return pl.pallas_call(
        body, name=name, grid=(t // tm,),
        in_specs=[pl.BlockSpec((tm, k), lambda i: (i, 0)), pl.BlockSpec((k, d), lambda i: (0, 0)), row, vec, row],
        out_specs=[pl.BlockSpec((1, LANES), lambda i: (0, 0)), row, vec, row],
        out_shape=[jax.ShapeDtypeStruct((1, LANES), F32), jax.ShapeDtypeStruct((t, d), F32),
                   jax.ShapeDtypeStruct((1, d), F32), jax.ShapeDtypeStruct((t, d), BF16)],
        compiler_params=_params("arbitrary"),
    )(act, w_down, res, g, target)


CONV_HALO = 8
CONV_ROWS = 64


def _conv_taps(window, wv, bv):
    acc = bv + wv[SSM_CONV - 1:SSM_CONV, :] * window(0)
    for sh in range(1, SSM_CONV):
        kidx = SSM_CONV - 1 - sh
        acc = acc + wv[kidx:kidx + 1, :] * window(sh)
    return acc


def _conv_fwd(u, w, bias, name):
    b, s, c = u.shape
    rows = CONV_ROWS

    def body(u_ref, w_ref, b_ref, o_ref, ext):
        ext[0:CONV_HALO, :] = jnp.zeros((CONV_HALO, LANES), F32)
        ext[CONV_HALO:, :] = u_ref[...]
        wv, bv = w_ref[...], b_ref[...]
        for r0 in range(0, s, rows):
            acc = _conv_taps(lambda sh: ext[CONV_HALO + r0 - sh:CONV_HALO + r0 - sh + rows, :], wv, bv)
            o_ref[r0:r0 + rows, :] = acc * _sigmoid(acc)

    strip = pl.BlockSpec((None, s, LANES), lambda bi, j: (bi, 0, j))
    return pl.pallas_call(
        body, name=name, grid=(b, c // LANES),
        in_specs=[strip, pl.BlockSpec((SSM_CONV, LANES), lambda bi, j: (0, j)),
                  pl.BlockSpec((1, LANES), lambda bi, j: (0, j))],
        out_specs=strip, out_shape=jax.ShapeDtypeStruct((b, s, c), F32),
        scratch_shapes=[pltpu.VMEM((CONV_HALO + s, LANES), F32)],
        compiler_params=_params("parallel", "parallel"),
    )(u, w, bias)


def _conv_bwd(u, dout, w, bias, dproj, name):
    b, s, c = u.shape
    rows = CONV_ROWS

    def fold(v):
        return jnp.sum(v.reshape(rows // CONV_HALO, CONV_HALO, LANES), axis=0)

    def body(u_ref, d_ref, w_ref, b_ref, buf_ref, du_ref, dw_ref, db_ref, ext, dpre):
        @pl.when(pl.program_id(1) == 0)
        def _():
            dw_ref[...] = jnp.zeros_like(dw_ref)
            db_ref[...] = jnp.zeros_like(db_ref)

        ext[0:CONV_HALO, :] = jnp.zeros((CONV_HALO, LANES), F32)
        ext[CONV_HALO:, :] = u_ref[...]
        dpre[s:, :] = jnp.zeros((CONV_HALO, LANES), F32)
        wv, bv = w_ref[...], b_ref[...]
        sums = [jnp.zeros((CONV_HALO, LANES), F32)] * (SSM_CONV + 1)
        for r0 in range(0, s, rows):
            window = lambda sh: ext[CONV_HALO + r0 - sh:CONV_HALO + r0 - sh + rows, :]
            acc = _conv_taps(window, wv, bv)
            sg = _sigmoid(acc)
            dp = d_ref[r0:r0 + rows, :] * (sg * (1.0 + acc * (1.0 - sg)))
            dpre[r0:r0 + rows, :] = dp
            taps = [sums[SSM_CONV - 1 - sh] + fold(dp * window(sh)) for sh in range(SSM_CONV)]
            sums = taps[::-1] + [sums[SSM_CONV] + fold(dp)]
        for r0 in range(0, s, rows):
            du = wv[SSM_CONV - 1:SSM_CONV, :] * dpre[r0:r0 + rows, :]
            for sh in range(1, SSM_CONV):
                kidx = SSM_CONV - 1 - sh
                du = du + wv[kidx:kidx + 1, :] * dpre[r0 + sh:r0 + sh + rows, :]
            du_ref[r0:r0 + rows, :] = du.astype(BF16)
        for kidx in range(SSM_CONV):
            dw_ref[kidx:kidx + 1, :] += jnp.sum(sums[kidx], axis=0, keepdims=True)
        db_ref[...] += jnp.sum(sums[SSM_CONV], axis=0, keepdims=True)

    strip = pl.BlockSpec((None, s, LANES), lambda j, bi: (bi, 0, j))
    taps = pl.BlockSpec((SSM_CONV, LANES), lambda j, bi: (0, j))
    vec = pl.BlockSpec((1, LANES), lambda j, bi: (0, j))
    du_cols = pl.BlockSpec((None, s, LANES), lambda j, bi: (bi, 0, DPROJ_COLS["xbc"] // LANES + j))
    return pl.pallas_call(
        body, name=name, grid=(c // LANES, b),
        in_specs=[strip, strip, taps, vec, pl.BlockSpec(memory_space=pl.ANY)], out_specs=[du_cols, taps, vec],
        input_output_aliases={4: 0},
        out_shape=[jax.ShapeDtypeStruct(dproj.shape, dproj.dtype), jax.ShapeDtypeStruct((SSM_CONV, c), F32),
                   jax.ShapeDtypeStruct((1, c), F32)],
        scratch_shapes=[pltpu.VMEM((CONV_HALO + s, LANES), F32), pltpu.VMEM((s + CONV_HALO, LANES), F32)],
        compiler_params=_params("parallel", "arbitrary"),
    )(u, dout, w, bias, dproj)


def _ssd_chunk_terms(dtr_ref, bias_ref, alog_ref):
    q = SSM_CHUNK
    dt = _softplus(dtr_ref[...] + bias_ref[...])
    a_neg = -jnp.exp(alog_ref[...])
    row = lax.broadcasted_iota(jnp.int32, (q, q), 0)
    col = lax.broadcasted_iota(jnp.int32, (q, q), 1)
    lower = row >= col
    s = _mask_nn(lower, dt * a_neg)
    return dt, a_neg, s, s.T, lower


def _head_masks():
    heads = jnp.arange(LANES)[:, None]
    chans = jnp.arange(SSM_D_INNER)[None, :]
    to_channels = (chans // SSM_HEAD_DIM == heads).astype(BF16)
    return to_channels, to_channels.T


def _per_channel(v, to_channels):
    hi = v.astype(BF16)
    lo = (v - hi.astype(F32)).astype(BF16)
    return _nn(hi, to_channels) + _nn(lo, to_channels)


def _per_head(v, to_heads):
    hi = v.astype(BF16)
    lo = (v - hi.astype(F32)).astype(BF16)
    return _nn(hi, to_heads) + _nn(lo, to_heads)


def _decay_terms_per_channel(dt, s_col, to_channels):
    q = SSM_CHUNK
    tot = s_col[q - 1:q, :]
    stacked = jnp.concatenate([dt, jnp.exp(s_col), jnp.exp(tot - s_col)], axis=0)
    wide = _per_channel(stacked, to_channels)
    dtx, esx, decx = wide[:q], wide[q:2 * q], wide[2 * q:]
    return dtx, esx, decx, esx[0:1, :] * decx[0:1, :]


SSM_PAIRS_PER_GROUP = SSM_HEADS_PER_GROUP // 2
SSM_GROUP_CHANNELS = SSM_HEADS_PER_GROUP * SSM_HEAD_DIM


def _split_pair(v):
    first = lax.broadcasted_iota(jnp.int32, v.shape, 1) < SSM_HEAD_DIM
    return jnp.concatenate([jnp.where(first, v, 0.0), jnp.where(first, 0.0, v)], axis=0)


def _ssd_fwd(xc, dtr, dt_bias, a_log, dskx, to_channels, name):
    b, s, _ = xc.shape
    q = SSM_CHUNK
    nc = s // q
    n, gc = SSM_D_STATE, SSM_GROUP_CHANNELS

    def body(xc_ref, dtr_ref, bias_ref, alog_ref, dsk_ref, tc_ref, y_ref, hs_ref, h_scr):
        @pl.when(pl.program_id(1) == 0)
        def _():
            h_scr[...] = jnp.zeros_like(h_scr)

        dt, _, s_col, s_row, lower = _ssd_chunk_terms(dtr_ref, bias_ref, alog_ref)
        dtx, esx, decx, etotx = _decay_terms_per_channel(dt, s_col, tc_ref[...])
        x = xc_ref[:, :SSM_D_INNER]
        xdt = x * dtx
        xdec = xdt * decx
        skip = dsk_ref[...] * x
        for g in range(SSM_N_GROUPS):
            bg = xc_ref[:, SSM_D_INNER + n * g:SSM_D_INNER + n * (g + 1)].astype(BF16)
            cg = xc_ref[:, SSM_D_INNER + n * (SSM_N_GROUPS + g):SSM_D_INNER + n * (SSM_N_GROUPS + g + 1)].astype(BF16)
            gsl = slice(gc * g, gc * (g + 1))
            gm = _nt(cg, bg)
            hgt = h_scr[:, gsl]
            hs_ref[:, gsl] = hgt
            y_off = esx[:, gsl] * _nn(cg, hgt)
            h_scr[:, gsl] = etotx[:, gsl] * hgt + _tn(bg, xdec[:, gsl])
            for k in range(SSM_PAIRS_PER_GROUP):
                h0 = g * SSM_HEADS_PER_GROUP + 2 * k
                lo = gc * g + LANES * k
                ms = []
                for h in (h0, h0 + 1):
                    lm = jnp.exp(jnp.where(lower, s_col[:, h:h + 1] - s_row[h:h + 1, :], NEG_INF))
                    ms.append((gm * lm).astype(BF16))
                y_diag = _nn(jnp.concatenate(ms, axis=1), _split_pair(xdt[:, lo:lo + LANES]))
                y_ref[:, lo:lo + LANES] = y_diag + y_off[:, LANES * k:LANES * (k + 1)] + skip[:, lo:lo + LANES]

    vec = pl.BlockSpec((1, LANES), lambda bi, c: (0, 0))
    return pl.pallas_call(
        body, name=name, grid=(b, nc),
        in_specs=[pl.BlockSpec((None, q, SSM_CONV_DIM), lambda bi, c: (bi, c, 0)),
                  pl.BlockSpec((None, q, LANES), lambda bi, c: (bi, c, 0)), vec, vec,
                  pl.BlockSpec((1, SSM_D_INNER), lambda bi, c: (0, 0)),
                  pl.BlockSpec((LANES, SSM_D_INNER), lambda bi, c: (0, 0))],
        out_specs=[pl.BlockSpec((None, q, SSM_D_INNER), lambda bi, c: (bi, c, 0)),
                   pl.BlockSpec((None, None, n, SSM_D_INNER), lambda bi, c: (bi, c, 0, 0))],
        out_shape=[jax.ShapeDtypeStruct((b, s, SSM_D_INNER), F32),
                   jax.ShapeDtypeStruct((b, nc, n, SSM_D_INNER), F32)],
        scratch_shapes=[pltpu.VMEM((n, SSM_D_INNER), F32)],
        compiler_params=_params("parallel", "arbitrary"),
    )(xc, dtr, dt_bias, a_log, dskx, to_channels)


def _ssd_bwd(xc, dtr, dy, hs, dt_bias, a_log, dskx, to_channels, to_heads, dproj, name):
    b, s, _ = xc.shape
    q = SSM_CHUNK
    nc = s // q
    n, gc = SSM_D_STATE, SSM_GROUP_CHANNELS

    def colsum(v):
        return jnp.sum(v, axis=0, keepdims=True)

    def body(xc_ref, dtr_ref, dy_ref, hs_ref, bias_ref, alog_ref, dsk_ref, tc_ref, th_ref, buf_ref,
             dxc_ref, ddtr_ref, dalog_ref, ddsk_ref, dbias_ref, dh_scr, dxs_scr, dxd_scr, w_scr, dst_scr, rows_scr):
        ci = pl.program_id(1)

        @pl.when(ci == 0)
        def _():
            dh_scr[...] = jnp.zeros_like(dh_scr)

        @pl.when(jnp.logical_and(pl.program_id(0) == 0, ci == 0))
        def _():
            dalog_ref[...] = jnp.zeros_like(dalog_ref)
            ddsk_ref[...] = jnp.zeros_like(ddsk_ref)
            dbias_ref[...] = jnp.zeros_like(dbias_ref)
            dst_scr[...] = jnp.zeros_like(dst_scr)

        dt, a_neg, s_col, s_row, lower = _ssd_chunk_terms(dtr_ref, bias_ref, alog_ref)
        upper = jnp.logical_not(lower) | (lax.broadcasted_iota(jnp.int32, (q, q), 0)
                                          == lax.broadcasted_iota(jnp.int32, (q, q), 1))
        dtx, esx, decx, etotx = _decay_terms_per_channel(dt, s_col, tc_ref[...])
        x = xc_ref[:, :SSM_D_INNER]
        dyv = dy_ref[...]
        xdt = x * dtx
        xdec = xdt * decx
        dw = esx * dyv
        rows_scr[...] = jnp.zeros_like(rows_scr)
        for g in range(SSM_N_GROUPS):
            b_lo = SSM_D_INNER + n * g
            c_lo = SSM_D_INNER + n * (SSM_N_GROUPS + g)
            bg = xc_ref[:, b_lo:b_lo + n].astype(BF16)
            cg = xc_ref[:, c_lo:c_lo + n].astype(BF16)
            gsl = slice(gc * g, gc * (g + 1))
            gm = _nt(cg, bg)
            gmt = _nt(bg, cg)
            hgt = hs_ref[:, gsl]
            dhgt = dh_scr[:, gsl]
            w_scr[:, gsl] = _nn(cg, hgt)
            dcg = _nt(dw[:, gsl], hgt)
            dxs = decx[:, gsl] * _nn(bg, dhgt)
            dxs_scr[:, gsl] = dxs
            dbg = _nt(xdec[:, gsl], dhgt)
            rows_scr[2:3, gsl] = colsum(dhgt * hgt)
            dh_scr[:, gsl] = _tn(cg, dw[:, gsl]) + etotx[:, gsl] * dhgt
            dg = jnp.zeros((q, q), F32)
            dgt = jnp.zeros((q, q), F32)
            for k in range(SSM_PAIRS_PER_GROUP):
                h0 = g * SSM_HEADS_PER_GROUP + 2 * k
                lo = gc * g + LANES * k
                xp = xdt[:, lo:lo + LANES]
                dyp = dyv[:, lo:lo + LANES]
                dy2 = _split_pair(dyp)
                dm2 = _nt(dy2, xp)
                dmt2 = _nt(_split_pair(xp), dyp)
                mts = []
                for i, h in enumerate((h0, h0 + 1)):
                    lm = jnp.exp(jnp.where(lower, s_col[:, h:h + 1] - s_row[h:h + 1, :], NEG_INF))
                    lmt = jnp.exp(jnp.where(upper, s_row[h:h + 1, :] - s_col[:, h:h + 1], NEG_INF))
                    dm = dm2[q * i:q * (i + 1), :]
                    dmt = dmt2[q * i:q * (i + 1), :]
                    dg = dg + dm * lm
                    dgt = dgt + dmt * lmt
                    mt = gmt * lmt
                    dst_scr[h:h + 1, :] = colsum(dmt * mt) - colsum(dm * (gm * lm))
                    mts.append(mt.astype(BF16))
                dxd_scr[:, lo:lo + LANES] = _nn(jnp.concatenate(mts, axis=1), dy2)
            dxc_ref[:, b_lo:b_lo + n] = dbg + _nn(dgt, cg)
            dxc_ref[:, c_lo:c_lo + n] = dcg + _nn(dg, bg)
        dxs = dxs_scr[...]
        dxdt = dxd_scr[...] + dxs
        dxc_ref[:, :SSM_D_INNER] = dxdt * dtx + dsk_ref[...] * dyv
        state_part = xdt * dxs
        rows_scr[0:1, :] = colsum(dyv * x)
        rows_scr[1:2, :] = colsum(state_part)
        th = th_ref[...]
        per_head = _per_head(jnp.concatenate([dw * w_scr[...] - state_part, dxdt * x], axis=0), th)
        r_ds, r_dt = per_head[:q], per_head[q:]
        sums = _per_head(rows_scr[...], th)
        etot = jnp.exp(s_col[q - 1:q, :])
        dtot = sums[1:2, :] + etot * sums[2:3, :]
        last = lax.broadcasted_iota(jnp.int32, (q, LANES), 0) == q - 1
        ds = dst_scr[...].T + r_ds + jnp.where(last, dtot, 0.0)
        da = _mask_nn(upper, ds)
        ddt = da * a_neg + r_dt
        live = lax.broadcasted_iota(jnp.int32, (1, LANES), 1) < SSM_N_HEADS
        sg = _sigmoid(dtr_ref[...] + bias_ref[...])
        ddtr = jnp.where(live, ddt * sg, 0.0)
        ddtr_ref[:, :LANES] = ddtr.astype(BF16)
        ddtr_ref[:, LANES:] = jnp.zeros((q, DPROJ_DT_WIDTH - LANES), BF16)
        dalog_ref[...] += jnp.where(live, colsum(da * dt) * a_neg, 0.0)
        ddsk_ref[...] += jnp.where(live, sums[0:1, :], 0.0)
        dbias_ref[...] += colsum(ddtr)

    rev = lambda bi, c: (bi, nc - 1 - c, 0)
    vec = pl.BlockSpec((1, LANES), lambda bi, c: (0, 0))
    wide = pl.BlockSpec((None, q, SSM_D_INNER), rev)
    return pl.pallas_call(
        body, name=name, grid=(b, nc),
        in_specs=[pl.BlockSpec((None, q, SSM_CONV_DIM), rev), pl.BlockSpec((None, q, LANES), rev), wide,
                  pl.BlockSpec((None, None, n, SSM_D_INNER), lambda bi, c: (bi, nc - 1 - c, 0, 0)),
                  vec, vec, pl.BlockSpec((1, SSM_D_INNER), lambda bi, c: (0, 0)),
                  pl.BlockSpec((LANES, SSM_D_INNER), lambda bi, c: (0, 0)),
                  pl.BlockSpec((SSM_D_INNER, LANES), lambda bi, c: (0, 0)),
                  pl.BlockSpec(memory_space=pl.ANY)],
        out_specs=[pl.BlockSpec((None, q, SSM_CONV_DIM), rev),
                   pl.BlockSpec((None, q, DPROJ_DT_WIDTH),
                                lambda bi, c: (bi, nc - 1 - c, DPROJ_COLS["dt"] // DPROJ_DT_WIDTH)), vec, vec, vec],
        input_output_aliases={9: 1},
        out_shape=[jax.ShapeDtypeStruct((b, s, SSM_CONV_DIM), F32), jax.ShapeDtypeStruct(dproj.shape, dproj.dtype),
                   jax.ShapeDtypeStruct((1, LANES), F32), jax.ShapeDtypeStruct((1, LANES), F32),
                   jax.ShapeDtypeStruct((1, LANES), F32)],
        scratch_shapes=[pltpu.VMEM((n, SSM_D_INNER), F32)] + [pltpu.VMEM((q, SSM_D_INNER), F32)] * 3
        + [pltpu.VMEM((LANES, q), F32), pltpu.VMEM((8, SSM_D_INNER), F32)],
        compiler_params=_params("arbitrary", "arbitrary"),
    )(xc, dtr, dy, hs, dt_bias, a_log, dskx, to_channels, to_heads, dproj)


SSM_GROUP_WIDTH = SSM_D_INNER // SSM_N_GROUPS


def _gate_norm_fwd(y, z, w, name):
    t, d = y.shape
    tm = _pick(t, (256, 128))

    def body(y_ref, z_ref, w_ref, o_ref):
        for g in range(SSM_N_GROUPS):
            sl = slice(SSM_GROUP_WIDTH * g, SSM_GROUP_WIDTH * (g + 1))
            zv = z_ref[:, sl]
            u = y_ref[:, sl] * (zv * _sigmoid(zv))
            r = lax.rsqrt(jnp.mean(u * u, axis=-1, keepdims=True) + EPS)
            o_ref[:, sl] = ((u * r) * w_ref[:, sl]).astype(BF16)

    row = pl.BlockSpec((tm, d), lambda i: (i, 0))
    return pl.pallas_call(
        body, name=name, grid=(t // tm,),
        in_specs=[row, row, pl.BlockSpec((1, d), lambda i: (0, 0))], out_specs=row,
        out_shape=jax.ShapeDtypeStruct((t, d), BF16),
        compiler_params=_params("parallel"),
    )(y, z, w)


def _ssm_out_dx_gate_norm_bwd(dys, w_ssm_out, y, z, w, dproj, name):
    t, d = y.shape
    k = dys.shape[1]
    gw = SSM_GROUP_WIDTH
    tm = _pick(t, (512, 256, 128))

    def body(dys_ref, ws_ref, y_ref, z_ref, w_ref, buf_ref, dy_ref, dz_ref, dw_ref):
        @pl.when(pl.program_id(0) == 0)
        def _():
            dw_ref[...] = jnp.zeros_like(dw_ref)

        dout = _nt(dys_ref[...], ws_ref[...])
        for g in range(SSM_N_GROUPS):
            sl = slice(gw * g, gw * (g + 1))
            zv = z_ref[:, sl]
            yv = y_ref[:, sl]
            sg = _sigmoid(zv)
            silu = zv * sg
            u = yv * silu
            r = lax.rsqrt(jnp.mean(u * u, axis=-1, keepdims=True) + EPS)
            uh = u * r
            dov = dout[:, sl]
            dw_ref[:, sl] += jnp.sum(dov * uh, axis=0, keepdims=True)
            dyg = dov * w_ref[:, sl]
            du = r * (dyg - uh * jnp.mean(dyg * uh, axis=-1, keepdims=True))
            dy_ref[:, sl] = du * silu
            dz_ref[:, sl] = (du * yv * (sg * (1.0 + zv * (1.0 - sg)))).astype(BF16)

    row = pl.BlockSpec((tm, d), lambda i: (i, 0))
    vec = pl.BlockSpec((1, d), lambda i: (0, 0))
    z_cols = pl.BlockSpec((tm, d), lambda i: (i, DPROJ_COLS["z"] // d))
    return pl.pallas_call(
        body, name=name, grid=(t // tm,),
        in_specs=[pl.BlockSpec((tm, k), lambda i: (i, 0)), pl.BlockSpec((d, k), lambda i: (0, 0)), row, row, vec,
                  pl.BlockSpec(memory_space=pl.ANY)],
        out_specs=[row, z_cols, vec],
        out_shape=[jax.ShapeDtypeStruct((t, d), F32), jax.ShapeDtypeStruct(dproj.shape, dproj.dtype),
                   jax.ShapeDtypeStruct((1, d), F32)],
        input_output_aliases={5: 1},
        compiler_params=_params("arbitrary"),
    )(dys, w_ssm_out, y, z, w, dproj)


def _rope_tables(s):
    half = ATT_HEAD_DIM // 2
    inv = ROPE_THETA ** (-jnp.arange(half, dtype=F32) / half)
    ang = jnp.arange(s).astype(F32)[:, None] * inv[None, :]
    cos, sin = jnp.cos(ang), jnp.sin(ang)
    return jnp.concatenate([cos, cos], axis=-1), jnp.concatenate([-sin, sin], axis=-1)


ATT_TILE = 256


def _by_residue_spec(r, width):
    return pl.BlockSpec((None, r, ATT_TILE // r, width), lambda bi, i: (bi, 0, i, 0))


def _to_residues(tile, stage, r, store):
    if r == 1:
        store(0, tile)
        return
    stage[...] = tile
    for ri in range(r):
        store(ri, stage[pl.ds(ri, tile.shape[0] // r, stride=r), :])


def _from_residues(load, stage, r):
    if r == 1:
        return load(0)
    for ri in range(r):
        stage[pl.ds(ri, ATT_TILE // r, stride=r), :] = load(ri)
    return stage[...]


QKV_ROWS = 1024
QKV_COLS = 768


def _qkv_proj_rope(h, w_qkv_t, cosf, sinf, b, s, name):
    t, k = h.shape
    tm, d, gw = QKV_ROWS, ATT_HEAD_DIM, ATT_OUT_DIM
    per_seq = s // tm

    def body(h_ref, w_ref, c_ref, s_ref, *rest):
        outs, stage = rest[:-1], rest[-1]
        cv, sv = c_ref[...], s_ref[...]
        hv = h_ref[...]
        for lo in range(0, ATT_QKV_DIM, QKV_COLS):
            acc = _nt(hv, w_ref[lo:lo + QKV_COLS, :])
            for hh in range(QKV_COLS // d):
                kind, head = divmod(lo // d + hh, ATT_N_HEADS)
                gi, j = divmod(head, ATT_HEADS_PER_GROUP)
                dst = slice(kind * gw + d * j, kind * gw + d * (j + 1))
                tv = acc[:, d * hh:d * (hh + 1)]
                if kind < 2:
                    tv = tv * cv + pltpu.roll(tv, d // 2, 1) * sv

                def store(ri, rows, o_ref=outs[gi], dst=dst):
                    o_ref[ri, :, dst] = rows.astype(BF16)

                _to_residues(tv, stage, ATT_DILATIONS[gi], store)

    tab = pl.BlockSpec((tm, d), lambda i: (i % per_seq, 0))
    return pl.pallas_call(
        body, name=name, grid=(t // tm,),
        in_specs=[pl.BlockSpec((tm, k), lambda i: (i, 0)), pl.BlockSpec((ATT_QKV_DIM, k), lambda i: (0, 0)), tab, tab],
        out_specs=[pl.BlockSpec((None, r, tm // r, 3 * gw), lambda i: (i // per_seq, 0, i % per_seq, 0))
                   for r in ATT_DILATIONS],
        out_shape=[jax.ShapeDtypeStruct((b, r, s // r, 3 * gw), BF16) for r in ATT_DILATIONS],
        scratch_shapes=[pltpu.VMEM((tm, d), F32)],
        compiler_params=_params("parallel"),
    )(h, w_qkv_t, cosf, sinf)


def _rope_bwd(dq, dk, dv, cosf, sinf, dproj, name):
    n_pat = len(ATT_DILATIONS)
    b, _, s, gw = dq[0].shape
    ts, d = ATT_TILE, ATT_HEAD_DIM

    def body(*refs):
        ins, (c_ref, s_ref, _, o_ref, stage) = refs[:3 * n_pat], refs[3 * n_pat:]
        cv, sv = c_ref[...], s_ref[...]
        for kind in range(3):
            for gi, r in enumerate(ATT_DILATIONS):
                src = ins[kind * n_pat + gi]
                for j in range(ATT_HEADS_PER_GROUP):
                    tv = _from_residues(lambda ri, src=src, j=j: src[ri, :, d * j:d * (j + 1)], stage, r)
                    if kind < 2:
                        tv = tv * cv + pltpu.roll(tv * sv, d // 2, 1)
                    lo = d * (kind * ATT_N_HEADS + gi * ATT_HEADS_PER_GROUP + j)
                    o_ref[:, lo:lo + d] = tv.astype(BF16)

    tab = pl.BlockSpec((ts, d), lambda bi, i: (i, 0))
    parts = [_by_residue_spec(r, gw) for r in ATT_DILATIONS]
    return pl.pallas_call(
        body, name=name, grid=(b, s // ts), in_specs=parts * 3 + [tab, tab, pl.BlockSpec(memory_space=pl.ANY)],
        out_specs=pl.BlockSpec((None, ts, ATT_QKV_DIM), lambda bi, i: (bi, i, DPROJ_COLS["qkv"] // ATT_QKV_DIM)),
        out_shape=jax.ShapeDtypeStruct(dproj.shape, dproj.dtype),
        input_output_aliases={3 * n_pat + 2: 0},
        scratch_shapes=[pltpu.VMEM((ts, d), F32)],
        compiler_params=_params("parallel", "parallel"),
    )(*dq, *dk, *dv, cosf, sinf, dproj)


ATT_SCALE = ATT_HEAD_DIM ** -0.5
ATT_STEP = 2 * ATT_BLOCK


def _att_spec(col):
    return pl.BlockSpec((None, None, ATT_STEP, ATT_OUT_DIM), lambda bi, ri, i: (bi, ri, i, col))


def _att_edge_spec(col, side, n_steps):
    def index(bi, ri, i):
        blk = 2 * i - 1 if side < 0 else 2 * i + 2
        return (bi, ri, jnp.clip(blk, 0, 2 * n_steps - 1), col)
    return pl.BlockSpec((None, None, ATT_BLOCK, ATT_OUT_DIM), index)


def _band_mask(shape, q_axis, has_prev):
    qi = lax.broadcasted_iota(jnp.int32, shape, q_axis)
    kj = lax.broadcasted_iota(jnp.int32, shape, 1 - q_axis)
    dist = qi + ATT_BLOCK - kj
    return (dist >= 0) & (dist <= ATT_BLOCK) & (has_prev | (kj >= ATT_BLOCK))


def _att_fwd(qkr, name):
    b, r, l, _ = qkr.shape
    nb = l // ATT_STEP
    d = ATT_HEAD_DIM

    def body(q_ref, kp_ref, k_ref, vp_ref, v_ref, o_ref, lse_ref):
        mask = _band_mask((ATT_STEP, ATT_BLOCK + ATT_STEP), 0, pl.program_id(2) > 0)
        heads = [slice(d * j, d * (j + 1)) for j in range(ATT_HEADS_PER_GROUP)]
        scores = [_nt(q_ref[:, sl], jnp.concatenate([kp_ref[:, sl], k_ref[:, sl]], axis=0)) for sl in heads]
        scores = [jnp.where(mask, sc * ATT_SCALE, NEG_INF) for sc in scores]
        tops = [jnp.max(sc, axis=-1, keepdims=True) for sc in scores]
        probs = [jnp.exp(sc - m) for sc, m in zip(scores, tops)]
        dens = [jnp.sum(pr, axis=-1, keepdims=True) for pr in probs]
        for sl, m, pr, den in zip(heads, tops, probs, dens):
            o_ref[:, sl] = _nn(pr / den, jnp.concatenate([vp_ref[:, sl], v_ref[:, sl]], axis=0))
            lse_ref[:, sl] = jnp.broadcast_to(m + jnp.log(den), (ATT_STEP, d))

    out_spec = _att_spec(0)
    return pl.pallas_call(
        body, name=name, grid=(b, r, nb),
        in_specs=[_att_spec(0), _att_edge_spec(1, -1, nb), _att_spec(1), _att_edge_spec(2, -1, nb), _att_spec(2)],
        out_specs=[out_spec, out_spec],
        out_shape=[jax.ShapeDtypeStruct((b, r, l, ATT_OUT_DIM), F32)] * 2,
        compiler_params=_params("parallel", "parallel", "parallel"),
    )(qkr, qkr, qkr, qkr, qkr)


def _att_merge(os_, lses, name):
    n_pat = len(os_)
    b, _, s, gw = os_[0].shape
    ts, d = ATT_TILE, ATT_HEAD_DIM

    def body(*refs):
        o_refs, l_refs = refs[:n_pat], refs[n_pat:2 * n_pat]
        att_ref, lse_outs, stage = refs[2 * n_pat], refs[2 * n_pat + 1:3 * n_pat + 1], refs[-1]
        for j in range(ATT_HEADS_PER_GROUP):
            sl = slice(d * j, d * (j + 1))
            ov = [_from_residues(lambda ri, g=g: o_refs[g][ri, :, sl], stage, r)
                  for g, r in enumerate(ATT_DILATIONS)]
            ls = [_from_residues(lambda ri, g=g: l_refs[g][ri, :, sl], stage, r)
                  for g, r in enumerate(ATT_DILATIONS)]
            m = functools.reduce(jnp.maximum, ls)
            es = [jnp.exp(lv - m) for lv in ls]
            tot = functools.reduce(lambda u, v: u + v, es)
            acc = (es[0] / tot) * ov[0]
            for g in range(1, n_pat):
                acc = acc + (es[g] / tot) * ov[g]
            att_ref[:, sl] = acc
            joint = m + jnp.log(tot)
            for g, r in enumerate(ATT_DILATIONS):
                def store(ri, rows, out=lse_outs[g]):
                    out[ri, :, sl] = rows
                _to_residues(joint, stage, r, store)

    parts = [_by_residue_spec(r, gw) for r in ATT_DILATIONS]
    return pl.pallas_call(
        body, name=name, grid=(b, s // ts), in_specs=parts * 2,
        out_specs=[pl.BlockSpec((None, ts, gw), lambda bi, i: (bi, i, 0))] + parts,
        out_shape=[jax.ShapeDtypeStruct((b, s, gw), F32)]
        + [jax.ShapeDtypeStruct((b, r, s // r, gw), F32) for r in ATT_DILATIONS],
        scratch_shapes=[pltpu.VMEM((ts, d), F32)],
        compiler_params=_params("parallel", "parallel"),
    )(*os_, *lses)


def _att_delta(att, datt, name):
    b, s, gw = att.shape
    ts, d = ATT_TILE, ATT_HEAD_DIM
    n_pat = len(ATT_DILATIONS)

    def body(a_ref, d_ref, *rest):
        do_outs, dl_outs, stage = rest[:n_pat], rest[n_pat:2 * n_pat], rest[-1]
        for j in range(ATT_HEADS_PER_GROUP):
            sl = slice(d * j, d * (j + 1))
            dv = d_ref[:, sl]
            delta = jnp.broadcast_to(jnp.sum(a_ref[:, sl] * dv, axis=-1, keepdims=True), (ts, d))
            for g, r in enumerate(ATT_DILATIONS):
                def store_do(ri, rows, out=do_outs[g]):
                    out[ri, :, sl] = rows.astype(BF16)

                def store_dl(ri, rows, out=dl_outs[g]):
                    out[ri, :, sl] = rows

                _to_residues(dv, stage, r, store_do)
                _to_residues(delta, stage, r, store_dl)

    row = pl.BlockSpec((None, ts, gw), lambda bi, i: (bi, i, 0))
    parts = [_by_residue_spec(r, gw) for r in ATT_DILATIONS]
    outs = pl.pallas_call(
        body, name=name, grid=(b, s // ts), in_specs=[row, row], out_specs=parts * 2,
        out_shape=[jax.ShapeDtypeStruct((b, r, s // r, gw), BF16) for r in ATT_DILATIONS]
        + [jax.ShapeDtypeStruct((b, r, s // r, gw), F32) for r in ATT_DILATIONS],
        scratch_shapes=[pltpu.VMEM((ts, d), F32)],
        compiler_params=_params("parallel", "parallel"),
    )(att, datt)
    return outs[:n_pat], outs[n_pat:]


def _att_bwd_q(qkr, datt, lse, delta, name):
    b, r, l, _ = qkr.shape
    nb = l // ATT_STEP
    d = ATT_HEAD_DIM

    def body(q_ref, kp_ref, k_ref, vp_ref, v_ref, do_ref, lse_ref, dl_ref, dq_ref):
        mask = _band_mask((ATT_STEP, ATT_BLOCK + ATT_STEP), 0, pl.program_id(2) > 0)
        heads = [slice(d * j, d * (j + 1)) for j in range(ATT_HEADS_PER_GROUP)]
        kcats = [jnp.concatenate([kp_ref[:, sl], k_ref[:, sl]], axis=0) for sl in heads]
        scores = [_nt(q_ref[:, sl], kcat) for sl, kcat in zip(heads, kcats)]
        dps = [_nt(do_ref[:, sl], jnp.concatenate([vp_ref[:, sl], v_ref[:, sl]], axis=0)) for sl in heads]
        probs = [jnp.exp(jnp.where(mask, sc * ATT_SCALE - lse_ref[:, sl.start:sl.start + 1], NEG_INF))
                 for sl, sc in zip(heads, scores)]
        dscs = [pr * (dp - dl_ref[:, sl.start:sl.start + 1]) for sl, pr, dp in zip(heads, probs, dps)]
        for sl, dsc, kcat in zip(heads, dscs, kcats):
            dq_ref[:, sl] = _nn(dsc, kcat) * ATT_SCALE

    tok = _att_spec(0)
    return pl.pallas_call(
        body, name=name, grid=(b, r, nb),
        in_specs=[_att_spec(0), _att_edge_spec(1, -1, nb), _att_spec(1), _att_edge_spec(2, -1, nb), _att_spec(2),
                  tok, tok, tok],
        out_specs=tok,
        out_shape=jax.ShapeDtypeStruct((b, r, l, ATT_OUT_DIM), F32),
        compiler_params=_params("parallel", "parallel", "parallel"),
    )(qkr, qkr, qkr, qkr, qkr, datt, lse, delta)


def _att_bwd_kv(qkr, datt, lse, delta, name):
    b, r, l, _ = qkr.shape
    nb = l // ATT_STEP
    d = ATT_HEAD_DIM

    def body(k_ref, v_ref, q_ref, qn_ref, do_ref, don_ref, lse_ref, lsen_ref, dl_ref, dln_ref, dk_ref, dv_ref):
        shape = (ATT_STEP, ATT_STEP + ATT_BLOCK)
        kj = lax.broadcasted_iota(jnp.int32, shape, 0)
        qi = lax.broadcasted_iota(jnp.int32, shape, 1)
        dist = qi - kj
        has_next = pl.program_id(2) < nb - 1
        mask = (dist >= 0) & (dist <= ATT_BLOCK) & (has_next | (qi < ATT_STEP))
        def per_query(own_ref, next_ref, sl):
            return jnp.tile(jnp.concatenate([own_ref[:, sl], next_ref[:, sl]], axis=0).T, (ATT_STEP // d, 1))

        heads = [slice(d * j, d * (j + 1)) for j in range(ATT_HEADS_PER_GROUP)]
        qcats = [jnp.concatenate([q_ref[:, sl], qn_ref[:, sl]], axis=0) for sl in heads]
        docats = [jnp.concatenate([do_ref[:, sl], don_ref[:, sl]], axis=0) for sl in heads]
        scores = [_nt(k_ref[:, sl], qcat) for sl, qcat in zip(heads, qcats)]
        dps = [_nt(v_ref[:, sl], docat) for sl, docat in zip(heads, docats)]
        probs = [jnp.exp(jnp.where(mask, sc * ATT_SCALE - per_query(lse_ref, lsen_ref, sl), NEG_INF))
                 for sl, sc in zip(heads, scores)]
        for sl, pr, docat in zip(heads, probs, docats):
            dv_ref[:, sl] = _nn(pr, docat)
        dscs = [pr * (dp - per_query(dl_ref, dln_ref, sl)) for sl, pr, dp in zip(heads, probs, dps)]
        for sl, dsc, qcat in zip(heads, dscs, qcats):
            dk_ref[:, sl] = _nn(dsc, qcat) * ATT_SCALE

    tok, tok_n = _att_spec(0), _att_edge_spec(0, 1, nb)
    return pl.pallas_call(
        body, name=name, grid=(b, r, nb),
        in_specs=[_att_spec(1), _att_spec(2), _att_spec(0), _att_edge_spec(0, 1, nb),
                  tok, tok_n, tok, tok_n, tok, tok_n],
        out_specs=[tok, tok],
        out_shape=[jax.ShapeDtypeStruct((b, r, l, ATT_OUT_DIM), F32)] * 2,
        compiler_params=_params("parallel", "parallel", "parallel"),
    )(qkr, qkr, qkr, qkr, datt, datt, lse, lse, delta, delta)


def _att_out_proj_mix(att, w_att_t, gl, bg, ys, name):
    t, k = att.shape
    d = w_att_t.shape[0]
    tm = _pick(t, (512, 256, 128))

    def body(a_ref, w_ref, gl_ref, bg_ref, ys_ref, ya_ref, o_ref):
        ya = _nt(a_ref[...], w_ref[...])
        ya_ref[...] = ya
        g0 = _sigmoid(gl_ref[:, :d] + bg_ref[:, :d])
        g1 = _sigmoid(gl_ref[:, d:] + bg_ref[:, d:])
        o_ref[...] = (g0 * ys_ref[...] + g1 * ya).astype(BF16)

    row = pl.BlockSpec((tm, d), lambda i: (i, 0))
    return pl.pallas_call(
        body, name=name, grid=(t // tm,),
        in_specs=[pl.BlockSpec((tm, k), lambda i: (i, 0)), pl.BlockSpec((d, k), lambda i: (0, 0)),
                  pl.BlockSpec((tm, 2 * d), lambda i: (i, 0)), pl.BlockSpec((1, 2 * d), lambda i: (0, 0)), row],
        out_specs=[row, row],
        out_shape=[jax.ShapeDtypeStruct((t, d), F32), jax.ShapeDtypeStruct((t, d), BF16)],
        compiler_params=_params("parallel"),
    )(att, w_att_t, gl, bg, ys)


def _mix_out_dx_mix_bwd(dx, w_mix, gl, bg, ys, ya, name):
    t, d = ys.shape
    tm = _pick(t, (512, 256, 128))

    def body(dx_ref, w_ref, gl_ref, bg_ref, ys_ref, ya_ref, dys_ref, dya_ref, dgl_ref, dbg_ref):
        @pl.when(pl.program_id(0) == 0)
        def _():
            dbg_ref[...] = jnp.zeros_like(dbg_ref)

        dm = _nt(dx_ref[...], w_ref[...])
        g0 = _sigmoid(gl_ref[:, :d] + bg_ref[:, :d])
        g1 = _sigmoid(gl_ref[:, d:] + bg_ref[:, d:])
        dys_ref[...] = (dm * g0).astype(BF16)
        dya_ref[...] = (dm * g1).astype(BF16)
        d0 = dm * ys_ref[...] * (g0 * (1.0 - g0))
        d1 = dm * ya_ref[...] * (g1 * (1.0 - g1))
        dgl_ref[:, :d] = d0.astype(BF16)
        dgl_ref[:, d:] = d1.astype(BF16)
        dbg_ref[:, :d] += jnp.sum(d0, axis=0, keepdims=True)
        dbg_ref[:, d:] += jnp.sum(d1, axis=0, keepdims=True)

    row = pl.BlockSpec((tm, d), lambda i: (i, 0))
    wide = pl.BlockSpec((tm, 2 * d), lambda i: (i, 0))
    vec = pl.BlockSpec((1, 2 * d), lambda i: (0, 0))
    gate_cols = pl.BlockSpec((tm, 2 * d), lambda i: (i, DPROJ_COLS["gate"] // (2 * d)))
    return pl.pallas_call(
        body, name=name, grid=(t // tm,),
        in_specs=[row, pl.BlockSpec((d, d), lambda i: (0, 0)), wide, vec, row, row],
        out_specs=[row, row, gate_cols, vec],
        out_shape=[jax.ShapeDtypeStruct((t, d), BF16), jax.ShapeDtypeStruct((t, d), BF16),
                   jax.ShapeDtypeStruct((t, DPROJ_WIDTH), BF16), jax.ShapeDtypeStruct((1, 2 * d), F32)],
        compiler_params=_params("arbitrary"),
    )(dx, w_mix, gl, bg, ys, ya)


def _up_proj_swiglu(h, w_up_t, gt, name):
    t, k = h.shape
    f = w_up_t.shape[0]
    tm, tn, _ = _mm_tiles(t, f, k, h.dtype.itemsize, w_up_t.dtype.itemsize, 4 + 2, True)

    def body(h_ref, w_ref, g_ref, up_ref, act_ref):
        up = _nt(h_ref[...], w_ref[...])
        up_ref[...] = up
        gv = g_ref[...]
        act_ref[...] = ((gv * _sigmoid(gv)) * up).astype(BF16)

    tile = pl.BlockSpec((tm, tn), lambda i, j: (i, j))
    return pl.pallas_call(
        body, name=name, grid=(t // tm, f // tn),
        in_specs=[pl.BlockSpec((tm, k), lambda i, j: (i, 0)), pl.BlockSpec((tn, k), lambda i, j: (j, 0)), tile],
        out_specs=[tile, tile],
        out_shape=[jax.ShapeDtypeStruct((t, f), F32), jax.ShapeDtypeStruct((t, f), BF16)],
        compiler_params=_params("parallel", "parallel"),
    )(h, w_up_t, gt)


def _down_dx_swiglu_bwd(dx, w_down, gt, up, name):
    t, k = dx.shape
    f = w_down.shape[0]
    tm, tn, _ = _mm_tiles(t, f, k, dx.dtype.itemsize, w_down.dtype.itemsize, 2 + 2, True)
    tm = min(tm, 512)

    def body(d_ref, w_ref, g_ref, u_ref, dg_ref, du_ref):
        dact = _nt(d_ref[...], w_ref[...])
        gv = g_ref[...]
        sg = _sigmoid(gv)
        dg_ref[...] = (dact * u_ref[...] * (sg * (1.0 + gv * (1.0 - sg)))).astype(BF16)
        du_ref[...] = (dact * (gv * sg)).astype(BF16)

    tile = pl.BlockSpec((tm, tn), lambda i, j: (i, j))
    return pl.pallas_call(
        body, name=name, grid=(t // tm, f // tn),
        in_specs=[pl.BlockSpec((tm, k), lambda i, j: (i, 0)), pl.BlockSpec((tn, k), lambda i, j: (j, 0)), tile, tile],
        out_specs=[tile, tile], out_shape=[jax.ShapeDtypeStruct((t, f), BF16)] * 2,
        compiler_params=_params("parallel", "parallel"),
    )(dx, w_down, gt, up)


def _peer(k):
    x, y, c = lax.axis_index("x"), lax.axis_index("y"), lax.axis_index("c")
    px, py, pc = x ^ ((k >> 2) & 1), y ^ ((k >> 1) & 1), c ^ (k & 1)
    return (px, py, pc), 4 * px + 2 * py + pc


def _my_index():
    return 4 * lax.axis_index("x") + 2 * lax.axis_index("y") + lax.axis_index("c")


def _all_gather(parts, name):
    n_parts = len(parts)

    def body(*refs):
        ins, outs = refs[:n_parts], refs[n_parts:2 * n_parts]
        send_sems, recv_sems, local_sems = refs[2 * n_parts:]
        here, me = _peer(0)
        sibling, sib_idx = _peer(1)
        chips = [_peer(2 * q) for q in range(1, N_CHIPS)]

        def copy(i, k, block, to, src=None):
            return pltpu.make_async_remote_copy(
                src_ref=outs[i].at[block] if src is None else src, dst_ref=outs[i].at[block],
                send_sem=send_sems.at[i * (N_DEV - 1) + k], recv_sem=recv_sems.at[i * (N_DEV - 1) + k],
                device_id=to, device_id_type=MESH)

        local = [pltpu.make_async_copy(ins[i], outs[i].at[me], local_sems.at[i]) for i in range(n_parts)]
        for cp in local:
            cp.start()
        sends = []
        for i in range(n_parts):
            sends.append(copy(i, 0, me, sibling, src=ins[i]))
            sends += [copy(i, q, me, chip, src=ins[i]) for q, (chip, _) in enumerate(chips, start=1)]
        for cp in sends:
            cp.start()
        for q, (chip, chip_idx) in enumerate(chips, start=1):
            for i in range(n_parts):
                copy(i, q, chip_idx, here).wait_recv()
                fwd = copy(i, N_CHIPS - 1 + q, chip_idx, sibling)
                fwd.start()
                sends.append(fwd)
        for i in range(n_parts):
            copy(i, 0, sib_idx, here).wait_recv()
        for q, (_, chip_idx) in enumerate(chips, start=1):
            for i in range(n_parts):
                copy(i, N_CHIPS - 1 + q, chip_idx ^ 1, here).wait_recv()
        for cp in sends:
            cp.wait_send()
        for cp in local:
            cp.wait()

    hbm = pl.BlockSpec(memory_space=pl.ANY)
    return pl.pallas_call(
        body, name=name, in_specs=[hbm] * n_parts, out_specs=[hbm] * n_parts,
        out_shape=[jax.ShapeDtypeStruct((N_DEV,) + p_.shape, p_.dtype) for p_ in parts],
        scratch_shapes=[pltpu.SemaphoreType.DMA((n_parts * (N_DEV - 1),)),
                        pltpu.SemaphoreType.DMA((n_parts * (N_DEV - 1),)),
                        pltpu.SemaphoreType.DMA((n_parts,))],
        compiler_params=pltpu.CompilerParams(has_side_effects=True),
    )(*parts)


HBM_SPEC = pl.BlockSpec(memory_space=pltpu.HBM)
SEM_SPEC = pl.BlockSpec(memory_space=pltpu.SEMAPHORE)
DATAFLOW = pltpu.SideEffectType.DATAFLOW_SIDE_EFFECTING


def _gather_start(block, after, name):
    per_peer = block.ndim == 3

    def body(v_ref, land_ref, after_ref, send_sems, recv_sems, v_thru, land_thru, token):
        me = _my_index()
        for k in range(1, N_DEV):
            peer, pidx = _peer(k)
            pltpu.make_async_remote_copy(
                src_ref=v_ref.at[pidx] if per_peer else v_ref, dst_ref=land_ref.at[me],
                send_sem=send_sems.at[k - 1], recv_sem=recv_sems.at[k - 1],
                device_id=peer, device_id_type=MESH).start()
        token[...] = jnp.zeros_like(token)

    land_shape = (N_DEV,) + block.shape[-2:]
    return pl.pallas_call(
        body, name=name,
        out_shape=(pltpu.SemaphoreType.DMA((N_DEV - 1,)), pltpu.SemaphoreType.DMA((N_DEV - 1,)),
                   pltpu.HBM(block.shape, block.dtype), pltpu.HBM(land_shape, block.dtype),
                   jax.ShapeDtypeStruct((8, LANES), F32)),
        in_specs=(HBM_SPEC, HBM_SPEC, pl.BlockSpec(memory_space=pl.ANY)),
        out_specs=(SEM_SPEC, SEM_SPEC, HBM_SPEC, HBM_SPEC, pl.BlockSpec(memory_space=pltpu.VMEM)),
        input_output_aliases={0: 2, 1: 3},
        compiler_params=pltpu.CompilerParams(has_side_effects=DATAFLOW),
    )(pltpu.with_memory_space_constraint(block, pltpu.HBM),
      pltpu.with_memory_space_constraint(lax.empty(land_shape, block.dtype), pltpu.HBM), after)


def _gather_wait(send_sems, recv_sems, block, landing, after, name):
    per_peer = block.ndim == 3

    def body(v_ref, land_ref, send_sems, recv_sems, after_ref, v_dead, got_ref):
        for k in range(1, N_DEV):
            peer, pidx = _peer(k)
            copy = pltpu.make_async_remote_copy(
                src_ref=v_ref.at[pidx] if per_peer else v_ref, dst_ref=land_ref.at[pidx],
                send_sem=send_sems.at[k - 1], recv_sem=recv_sems.at[k - 1],
                device_id=peer, device_id_type=MESH)
            copy.wait_send()
            copy.wait_recv()

    return pl.pallas_call(
        body, name=name,
        out_shape=(pltpu.HBM(block.shape, block.dtype), pltpu.HBM(landing.shape, landing.dtype)),
        in_specs=(HBM_SPEC, HBM_SPEC, SEM_SPEC, SEM_SPEC, pl.BlockSpec(memory_space=pl.ANY)),
        out_specs=(HBM_SPEC, HBM_SPEC), input_output_aliases={0: 0, 1: 1},
        compiler_params=pltpu.CompilerParams(has_side_effects=DATAFLOW),
    )(block, landing, send_sems, recv_sems, after)


TILE_ELEMS = 1024 * 1024


def _shared_exchange(shared, name):
    def body(sh_ref, gsh_ref, send_sems, recv_sems, local_sem):
        me = _my_index()
        local = pltpu.make_async_copy(sh_ref, gsh_ref.at[me], local_sem)
        local.start()
        sends = []
        for k in range(1, N_DEV):
            peer, _ = _peer(k)
            cp = pltpu.make_async_remote_copy(
                src_ref=sh_ref, dst_ref=gsh_ref.at[me], send_sem=send_sems.at[k - 1],
                recv_sem=recv_sems.at[k - 1], device_id=peer, device_id_type=MESH)
            cp.start()
            sends.append(cp)
        for k in range(1, N_DEV):
            peer, pidx = _peer(k)
            pltpu.make_async_remote_copy(
                src_ref=sh_ref, dst_ref=gsh_ref.at[pidx], send_sem=send_sems.at[k - 1],
                recv_sem=recv_sems.at[k - 1], device_id=peer, device_id_type=MESH).wait_recv()
        for cp in sends:
            cp.wait_send()
        local.wait()

    hbm = pl.BlockSpec(memory_space=pl.ANY)
    return pl.pallas_call(
        body, name=name, in_specs=[hbm], out_specs=hbm,
        out_shape=jax.ShapeDtypeStruct((N_DEV,) + shared.shape, shared.dtype),
        scratch_shapes=[pltpu.SemaphoreType.DMA((N_DEV - 1,)), pltpu.SemaphoreType.DMA((N_DEV - 1,)),
                        pltpu.SemaphoreType.DMA],
        compiler_params=pltpu.CompilerParams(has_side_effects=True),
    )(shared)


def _adamw(parts, w, m, v, name, row0=0, own=None):
    n_parts, rows, lanes = parts.shape
    tr = rows if rows * lanes <= TILE_ELEMS // 2 else _tile_rows(math.gcd(rows, row0), TILE_ELEMS // 4 // lanes, 8)
    c1 = 1.0 - ADAM_B1 ** ADAM_STEP
    c2 = 1.0 - ADAM_B2 ** ADAM_STEP

    def body(*refs):
        if own is None:
            p_ref, w_ref, m_ref, v_ref, g_ref, d_ref, nm_ref, nv_ref = refs
            terms = [p_ref[j].astype(F32) for j in range(n_parts)]
        else:
            me_ref, p_ref, own_ref, w_ref, m_ref, v_ref, g_ref, d_ref, nm_ref, nv_ref = refs
            terms = [jnp.where(me_ref[0] == j, own_ref[...], p_ref[j]).astype(F32) for j in range(n_parts)]
        g = terms[0]
        for term in terms[1:]:
            g = g + term
        nm = ADAM_B1 * m_ref[...] + (1.0 - ADAM_B1) * g
        nv = ADAM_B2 * v_ref[...] + (1.0 - ADAM_B2) * (g * g)
        g_ref[...] = g
        nm_ref[...] = nm
        nv_ref[...] = nv
        d_ref[...] = -ADAM_LR * ((nm / c1) / (jnp.sqrt(nv / c2) + ADAM_EPS) + ADAM_WD * w_ref[...])

    row = pl.BlockSpec((tr, lanes), lambda i, *_: (i, 0))
    state = pl.BlockSpec((tr, lanes), lambda i, *_: (row0 // tr + i, 0))
    in_specs = [pl.BlockSpec((n_parts, tr, lanes), lambda i, *_: (0, i, 0)), state, state, state]
    args, n_prefetch = (parts, w, m, v), 0
    if own is not None:
        slabs, me = own
        in_specs.insert(1, pl.BlockSpec((None, tr, lanes), lambda i, me_ref: (me_ref[0], i, 0)))
        args, n_prefetch = (me, parts, slabs, w, m, v), 1
    return pl.pallas_call(
        body, name=name,
        grid_spec=pltpu.PrefetchScalarGridSpec(num_scalar_prefetch=n_prefetch, grid=(rows // tr,),
                                               in_specs=in_specs, out_specs=[row] * 4),
        out_shape=[jax.ShapeDtypeStruct((rows, lanes), F32)] * 4,
        compiler_params=_params("parallel"),
    )(*args)


MATRIX_SHARDS = (
    ("w_in", (D_MODEL, IN_PROJ_DIM // N_DEV), True),
    ("w_ssm_out", (SSM_D_INNER // N_DEV, D_MODEL), False),
    ("w_att_out", (ATT_OUT_DIM, D_MODEL // N_DEV), True),
    ("w_mix_out", (D_MODEL // N_DEV, D_MODEL), False),
    ("w_ffn_gate", (D_MODEL, D_FF // N_DEV), True),
    ("w_ffn_up", (D_MODEL, D_FF // N_DEV), True),
    ("w_ffn_down", (D_FF // N_DEV, D_MODEL), False),
)
CONV_SHARD = ("conv_w", (SSM_CONV, SSM_CONV_DIM // N_DEV), True)
SHARDED = MATRIX_SHARDS + (CONV_SHARD,)
REPLICATED = (("norm_mix", D_MODEL), ("b_gate", 2 * D_MODEL), ("conv_b", SSM_CONV_DIM), ("dt_bias", SSM_N_HEADS),
              ("a_log", SSM_N_HEADS), ("d_skip", SSM_N_HEADS), ("ssm_norm", SSM_D_INNER), ("norm_ffn", D_MODEL),
              ("norm_final", D_MODEL))


def _round_up(n, mult):
    return -(-n // mult) * mult


def _pack_rows(flat, row_mult):
    rows = _round_up(-(-flat.shape[0] // LANES), row_mult)
    return jnp.pad(flat, (0, rows * LANES - flat.shape[0])).reshape(rows, LANES)


def _stacking(specs):
    return tuple((name, (shape[1], shape[0]) if by_cols else shape, by_cols) for name, shape, by_cols in specs)


def _to_stacking(vals, specs):
    return {name: (vals[name].T if by_cols else vals[name]) for name, _, by_cols in specs}


STACK_WIDTH = D_MODEL
STACK_ALIGN = 16
STACK_ORDER = ("w_ssm_out", "w_mix_out", "w_ffn_gate", "w_ffn_up", "w_ffn_down", "w_att_out", "conv_w", "w_in")
GATHER_LATER = STACK_ORDER[:-1]
REDUCE_EARLY = STACK_ORDER[:5]
REDUCE_LATE = STACK_ORDER[5:]


def _stack_layout():
    shapes = {name: shape for name, shape, _ in _stacking(SHARDED)}
    layout, off = {}, 0
    for name in STACK_ORDER:
        r, c = shapes[name]
        rows = r if c == STACK_WIDTH else _round_up(-(-(r * c) // STACK_WIDTH), STACK_ALIGN)
        layout[name] = (off, rows, (r, c))
        off = _round_up(off + rows, STACK_ALIGN)
    return layout, _round_up(off, 1024)


def _to_stack_rows(v, rows):
    if v.shape[-1] == STACK_WIDTH:
        return v
    lead = v.shape[:-2]
    flat = v.reshape(lead + (-1,))
    flat = jnp.pad(flat, [(0, 0)] * len(lead) + [(0, rows * STACK_WIDTH - flat.shape[-1])])
    return flat.reshape(lead + (rows, STACK_WIDTH))


def _from_stack_rows(block, shape):
    r, c = shape
    if c == STACK_WIDTH:
        return block
    lead = block.shape[:-2]
    return block.reshape(lead + (-1,))[..., :r * c].reshape(lead + (r, c))


def _stack(vals, dtype, skip=(), names=STACK_ORDER):
    layout, total = _stack_layout()
    order = names
    after = STACK_ORDER.index(order[-1]) + 1
    if after < len(STACK_ORDER):
        total = layout[STACK_ORDER[after]][0]
    lead = next(iter(vals.values())).shape[:-2]
    pieces = []
    for i, name in enumerate(order):
        off, rows, _ = layout[name]
        until = layout[order[i + 1]][0] if i + 1 < len(order) else total
        piece = jnp.zeros(lead + (rows, STACK_WIDTH), dtype) if name in skip else _to_stack_rows(vals[name], rows)
        pieces.append(jnp.pad(piece.astype(dtype), [(0, 0)] * len(lead) + [(0, until - off - rows), (0, 0)]))
    return jnp.concatenate(pieces, axis=-2)


def _unstack(stacked, names):
    layout, _ = _stack_layout()
    row0 = layout[names[0]][0]
    return {name: _from_stack_rows(stacked[..., layout[name][0] - row0:layout[name][0] - row0 + layout[name][1], :],
                                   layout[name][2]) for name in names}


W_IN_SHARD_ROWS = IN_PROJ_DIM // N_DEV


def _w_in_row_moves():
    moves, orig = [], 0
    for name, size in IN_SPLIT:
        for j in range(N_DEV):
            lo, hi = max(orig, W_IN_SHARD_ROWS * j), min(orig + size, W_IN_SHARD_ROWS * (j + 1))
            if lo < hi:
                moves.append((j, lo - W_IN_SHARD_ROWS * j, DPROJ_COLS[name] + lo - orig, hi - lo))
        orig += size
    return moves


def _w_in_from_shards(shards, name):
    total, base = shards.shape[1], 0
    pad_lo, pad_hi = DPROJ_COLS["dt"] + _round_up(SSM_N_HEADS, STACK_ALIGN), DPROJ_COLS["dt"] + DPROJ_DT_WIDTH

    def body(x_ref, o_ref):
        o_ref[pad_lo:pad_hi, :] = jnp.zeros((pad_hi - pad_lo, LANES), x_ref.dtype)
        for j, r, at, n in _w_in_row_moves():
            o_ref[at:at + n, :] = x_ref[j, base + r:base + r + n, :]

    return pl.pallas_call(
        body, name=name, grid=(STACK_WIDTH // LANES,),
        in_specs=[pl.BlockSpec((N_DEV, total, LANES), lambda c: (0, 0, c))],
        out_specs=pl.BlockSpec((DPROJ_WIDTH, LANES), lambda c: (0, c)),
        out_shape=jax.ShapeDtypeStruct((DPROJ_WIDTH, STACK_WIDTH), shards.dtype),
        compiler_params=_params("parallel"),
    )(shards)


def _w_in_to_shards(dw_all, head, name):
    layout, total = _stack_layout()
    total -= layout[REDUCE_LATE[0]][0]
    base = head.shape[1]
    end = base + W_IN_SHARD_ROWS

    def body(x_ref, h_ref, o_ref):
        o_ref[:, 0:base, :] = h_ref[...]
        for j, r, at, n in _w_in_row_moves():
            o_ref[j, base + r:base + r + n, :] = x_ref[at:at + n, :]
        o_ref[:, end:total, :] = jnp.zeros((N_DEV, total - end, LANES), o_ref.dtype)

    return pl.pallas_call(
        body, name=name, grid=(STACK_WIDTH // LANES,),
        in_specs=[pl.BlockSpec((DPROJ_WIDTH, LANES), lambda c: (0, c)),
                  pl.BlockSpec((N_DEV, base, LANES), lambda c: (0, 0, c))],
        out_specs=pl.BlockSpec((N_DEV, total, LANES), lambda c: (0, 0, c)),
        out_shape=jax.ShapeDtypeStruct((N_DEV, total, STACK_WIDTH), dw_all.dtype),
        compiler_params=_params("parallel"),
    )(dw_all, head)


REPLICATED_ROWS = sum(-(-size // LANES) for _, size in REPLICATED)
LOSS_ROW = REPLICATED_ROWS


def _pack_replicated(vals):
    rows = []
    for name, size in REPLICATED:
        v = vals[name].reshape(-1).astype(F32)
        rows.append(jnp.pad(v, (0, _round_up(size, LANES) - size)))
    return _pack_rows(jnp.concatenate(rows), 8)


def _unpack_replicated(packed, shapes):
    flat = packed.reshape(-1)
    out, off = {}, 0
    for name, size in REPLICATED:
        out[name] = flat[off:off + size].reshape(shapes[name])
        off += _round_up(size, LANES)
    return out


def _lane_row(v):
    v = v.reshape(-1).astype(F32)
    return jnp.pad(v, (0, LANES - v.shape[0])).reshape(1, LANES)


IN_SPLIT = (("z", SSM_D_INNER), ("xbc", SSM_CONV_DIM), ("dt", SSM_N_HEADS), ("qkv", ATT_QKV_DIM), ("gate", 2 * D_MODEL))


def kernel(x, norm_mix, w_in, b_gate, conv_w, conv_b, dt_bias, a_log, d_skip, ssm_norm, w_ssm_out, w_att_out, w_mix_out, norm_ffn, w_ffn_gate, w_ffn_up, w_ffn_down, norm_final, loss_target, m_norm_mix, m_w_in, m_b_gate, m_conv_w, m_conv_b, m_dt_bias, m_a_log, m_d_skip, m_ssm_norm, m_w_ssm_out, m_w_att_out, m_w_mix_out, m_norm_ffn, m_w_ffn_gate, m_w_ffn_up, m_w_ffn_down, m_norm_final, v_norm_mix, v_w_in, v_b_gate, v_conv_w, v_conv_b, v_dt_bias, v_a_log, v_d_skip, v_ssm_norm, v_w_ssm_out, v_w_att_out, v_w_mix_out, v_norm_ffn, v_w_ffn_gate, v_w_ffn_up, v_w_ffn_down, v_norm_final):
    given = dict(locals())
    weights = {name: given[name][0] for name, _, _ in SHARDED}
    b, s, d = x.shape
    t = b * s

    stacking = _to_stacking(weights, SHARDED)
    conv_shape = dict((name, shape) for name, shape, _ in _stacking(SHARDED))["conv_w"]
    w_in_local = jnp.pad(stacking["w_in"].astype(BF16), ((0, -W_IN_SHARD_ROWS % STACK_ALIGN), (0, 0)))
    conv_local = _pack_rows(stacking["conv_w"].reshape(-1), 8)
    w_in_shards, conv_all = _all_gather([w_in_local, conv_local], "w_in_all_gather")
    head_local = _stack(stacking, BF16, skip=("conv_w",), names=GATHER_LATER)
    in_flight = _gather_start(head_local, conv_all, "weights_gather_start")
    w_in_all = _w_in_from_shards(w_in_shards, "w_in_from_shards")
    w_sec = {name: w_in_all[DPROJ_COLS[name]:DPROJ_COLS[name] + _round_up(size, LANES)] for name, size in IN_SPLIT}
    conv_size = conv_shape[0] * conv_shape[1]
    conv_taps = conv_all.reshape(N_DEV, -1)[:, :conv_size].reshape(N_DEV * conv_shape[0], conv_shape[1]).T

    g_mix, g_ffn, g_fin = norm_mix.reshape(1, d), norm_ffn.reshape(1, d), norm_final.reshape(1, d)
    g_mix = g_mix + in_flight[4][:1, :1]
    bg_row = b_gate.reshape(1, 2 * d)
    convb_row = conv_b.reshape(1, SSM_CONV_DIM)
    ssmn_row = ssm_norm.reshape(1, SSM_D_INNER)
    dtb_row, alog_row = _lane_row(dt_bias), _lane_row(a_log)
    cosf, sinf = _rope_tables(s)

    x2d = x.reshape(t, d)
    h1 = _rmsnorm_fwd(x2d, g_mix, "norm_mix_fwd")
    proj = {name: _mm(h1, w_sec[name], mode="nt", name="in_proj_" + name) for name, _ in IN_SPLIT if name != "qkv"}
    xbc3 = proj["xbc"].reshape(b, s, SSM_CONV_DIM)
    xc = _conv_fwd(xbc3, conv_taps, convb_row, "conv_fwd")
    dtr3 = proj["dt"].reshape(b, s, DT_PAD)
    to_channels, to_heads = _head_masks()
    dskx = jnp.repeat(d_skip.reshape(-1).astype(F32), SSM_HEAD_DIM).reshape(1, SSM_D_INNER)
    y_ssd, h_states = _ssd_fwd(xc, dtr3, dtb_row, alog_row, dskx, to_channels, "ssd_fwd")
    y_ssd2 = y_ssd.reshape(t, SSM_D_INNER)
    ynorm = _gate_norm_fwd(y_ssd2, proj["z"], ssmn_row, "ssd_gate_norm_fwd")
    head_local, landed = _gather_wait(*in_flight[:4], ynorm, "weights_gather_wait")
    head_all = lax.dynamic_update_slice(landed, head_local[None], (_my_index(), 0, 0))
    full = {name: v.reshape((-1,) + v.shape[2:]) for name, v in _unstack(head_all, STACK_ORDER[:-2]).items()}
    y_ssm = _mm(ynorm, full["w_ssm_out"], mode="nn", name="ssm_out_proj")

    qk_parts = _qkv_proj_rope(h1, w_sec["qkv"], cosf, sinf, b, s, "in_proj_qkv_rope")
    att_parts = [_att_fwd(qk_parts[gi], "att_fwd_%d" % r) for gi, r in enumerate(ATT_DILATIONS)]
    att, *lse_parts = _att_merge([o for o, _ in att_parts], [l_ for _, l_ in att_parts], "att_merge")
    att2 = att.reshape(t, ATT_OUT_DIM)
    y_att, mixed = _att_out_proj_mix(att2, full["w_att_out"], proj["gate"], bg_row, y_ssm, "att_out_proj_mix")
    x2, h2 = _proj_residual_norm(mixed, full["w_mix_out"], x2d, g_ffn, "mix_out_proj_norm")
    gt = _mm(h2, full["w_ffn_gate"], mode="nt", name="ffn_gate_proj")
    up, act = _up_proj_swiglu(h2, full["w_ffn_up"], gt, "ffn_up_proj_swiglu")

    loss_row, dx3, dg_fin, dx3b = _down_proj_loss_head(act, full["w_ffn_down"], x2, g_fin, loss_target.reshape(t, d),
                                                       "ffn_down_proj_loss_head")
    grads = {}
    grads["w_ffn_down"] = _mm(act, dx3b, mode="tn", name="ffn_down_dw", out_dtype=BF16)
    dgt, dup = _down_dx_swiglu_bwd(dx3b, full["w_ffn_down"], gt, up, "ffn_down_dx_swiglu_bwd")
    grads["w_ffn_gate"] = _mm(dgt, h2, mode="tn", name="ffn_gate_dw", out_dtype=BF16)
    grads["w_ffn_up"] = _mm(dup, h2, mode="tn", name="ffn_up_dw", out_dtype=BF16)
    dh2 = _mm(dgt, full["w_ffn_gate"], mode="nn", name="ffn_gate_dx")
    dx2, dg_ffn, dx2b = _proj_norm_bwd(dup, full["w_ffn_up"], x2, g_ffn, dx3, "ffn_up_dx_norm_bwd", add=dh2,
                                       with_bf16=True)

    grads["w_mix_out"] = _mm(mixed, dx2b, mode="tn", name="mix_out_dw", out_dtype=BF16)
    dys, dya, dproj, dbg = _mix_out_dx_mix_bwd(dx2b, full["w_mix_out"], proj["gate"], bg_row, y_ssm, y_att,
                                               "mix_out_dx_mix_bwd")

    grads["w_ssm_out"] = _mm(ynorm, dys, mode="tn", name="ssm_out_dw", out_dtype=BF16)
    early = _stack({name: grads[name].reshape((N_DEV, -1, STACK_WIDTH)) for name in REDUCE_EARLY}, BF16,
                   names=REDUCE_EARLY)
    early_flight = _gather_start(early, dys, "grads_scatter_start")
    ssmn_row = ssmn_row + early_flight[4][:1, :1]
    dy_ssd, dproj, dssmn = _ssm_out_dx_gate_norm_bwd(dys, full["w_ssm_out"], y_ssd2, proj["z"], ssmn_row, dproj,
                                                     "ssm_out_dx_gate_norm_bwd")
    dxc, dproj, dalog, ddsk, ddtb = _ssd_bwd(xc, dtr3, dy_ssd.reshape(b, s, SSM_D_INNER), h_states, dtb_row, alog_row,
                                             dskx, to_channels, to_heads, dproj.reshape(b, s, DPROJ_WIDTH), "ssd_bwd")
    dproj, dconvw, dconvb = _conv_bwd(xbc3, dxc, conv_taps, convb_row, dproj, "conv_bwd")
    grads["conv_w"] = dconvw.T.astype(BF16)

    grads["w_att_out"] = _mm(dya, att2, mode="tn", name="att_out_dw", out_dtype=BF16)
    datt = _mm(dya, full["w_att_out"], mode="nn", name="att_out_dx").reshape(b, s, ATT_OUT_DIM)
    do_parts, dl_parts = _att_delta(att, datt, "att_delta")
    dqs, dks, dvs = [], [], []
    for gi, r in enumerate(ATT_DILATIONS):
        operands = (qk_parts[gi], do_parts[gi], lse_parts[gi], dl_parts[gi])
        dqs.append(_att_bwd_q(*operands, "att_bwd_q_%d" % r))
        dk_g, dv_g = _att_bwd_kv(*operands, "att_bwd_kv_%d" % r)
        dks.append(dk_g)
        dvs.append(dv_g)
    dproj = _rope_bwd(dqs, dks, dvs, cosf, sinf, dproj, "rope_bwd").reshape(t, DPROJ_WIDTH)

    dw_all = _mm(dproj, h1, mode="tn", name="in_proj_dw", out_dtype=BF16)
    head = _stack({name: grads[name].reshape((N_DEV, -1, grads[name].shape[-1])) for name in REDUCE_LATE[:-1]}, BF16,
                  names=REDUCE_LATE[:-1])
    late = _w_in_to_shards(dw_all, head, "grad_stacks")
    late_flight = _gather_start(late, dw_all, "grads_late_scatter_start")
    dh1 = _mm(dproj, w_in_all, mode="nn", name="in_proj_dx", after=late_flight[4])
    grad_x, dg_mix = _rmsnorm_bwd(x2d, g_mix, dh1, dx2, "norm_mix_bwd")

    small = {"norm_mix": dg_mix, "b_gate": dbg, "conv_b": dconvb, "dt_bias": ddtb[:, :SSM_N_HEADS],
             "a_log": dalog[:, :SSM_N_HEADS], "d_skip": ddsk[:, :SSM_N_HEADS], "ssm_norm": dssmn,
             "norm_ffn": dg_ffn, "norm_final": dg_fin}
    shared = _pack_replicated(small)
    shared = shared.at[LOSS_ROW, 0].set(loss_row[0, 0])
    got_small = _shared_exchange(shared, "shared_grads_exchange")

    def packed(prefix):
        vals = _to_stacking({name: given[prefix + name][0] for name, _, _ in SHARDED}, SHARDED)
        rep = {name: given[prefix + name] for name, _ in REPLICATED}
        return _stack(vals, F32), _pack_replicated(rep)

    (w_big, w_small), (m_big, m_small), (v_big, v_small) = packed(""), packed("m_"), packed("v_")
    me = _my_index().astype(jnp.int32).reshape(1)
    early, early_landed = _gather_wait(*early_flight[:4], got_small, "grads_scatter_wait")
    late, late_landed = _gather_wait(*late_flight[:4], got_small, "grads_late_scatter_wait")
    big_early = _adamw(early_landed, w_big, m_big, v_big, "adamw_early", own=(early, me))
    big_late = _adamw(late_landed, w_big, m_big, v_big, "adamw_late", row0=early.shape[1], own=(late, me))
    sml = _adamw(got_small, w_small, m_small, v_small, "adamw_replicated")

    outs = [sml[0][LOSS_ROW, 0], grad_x.reshape(b, s, d)]
    rep_shapes = {name: given[name].shape for name, _ in REPLICATED}
    order = ["norm_mix", "w_in", "b_gate", "conv_w", "conv_b", "dt_bias", "a_log", "d_skip", "ssm_norm", "w_ssm_out",
             "w_att_out", "w_mix_out", "norm_ffn", "w_ffn_gate", "w_ffn_up", "w_ffn_down", "norm_final"]
    for early_k, late_k, sml_k in zip(big_early, big_late, sml):
        stacks = dict(_unstack(early_k, REDUCE_EARLY), **_unstack(late_k, REDUCE_LATE))
        sharded = _to_stacking(stacks, SHARDED)
        rep = _unpack_replicated(sml_k, rep_shapes)
        for name in order:
            outs.append(sharded[name][None] if name in sharded else rep[name])
    return tuple(outs)
```

```python
import functools
import math

import jax
import jax.numpy as jnp
from jax import lax
from jax.experimental import pallas as pl
from jax.experimental.pallas import tpu as pltpu

F32 = jnp.float32
BF16 = jnp.bfloat16

N_DEV = 8
N_CHIPS = 4
D_MODEL = 1024
SSM_D_INNER = 2048
SSM_HEAD_DIM = 64
SSM_N_HEADS = 32
SSM_N_GROUPS = 4
SSM_HEADS_PER_GROUP = SSM_N_HEADS // SSM_N_GROUPS
SSM_D_STATE = 128
SSM_CONV = 4
SSM_CHUNK = 128
SSM_CONV_DIM = 3072
ATT_HEAD_DIM = 128
ATT_HEADS_PER_GROUP = 4
ATT_DILATIONS = (1, 4, 16)
ATT_N_HEADS = 12
ATT_QKV_DIM = 4608
ATT_OUT_DIM = 512
ATT_BLOCK = 128
ROPE_THETA = 10000.0
D_FF = 2816
IN_PROJ_DIM = 11808
EPS = 1e-6
LANES = 128
DT_PAD = LANES

DPROJ_COLS = {"qkv": 0, "xbc": 4608, "dt": 7680, "z": 8192, "gate": 10240}
DPROJ_DT_WIDTH = 512
DPROJ_WIDTH = 12288

ADAM_LR = 0.001
ADAM_B1 = 0.9
ADAM_B2 = 0.999
ADAM_EPS = 1e-08
ADAM_WD = 0.01
ADAM_STEP = 10

VMEM_LIMIT = 56 * 1024 * 1024
MESH = pl.DeviceIdType.MESH
NEG_INF = float("-inf")


def _tile_rows(n, cap, mult):
    return max(t for t in range(mult, min(n, cap) + 1, mult) if n % t == 0)


def _pick(n, candidates):
    for c in candidates:
        if n % c == 0:
            return c
    return n


def _params(*sem):
    return pltpu.CompilerParams(dimension_semantics=sem, vmem_limit_bytes=VMEM_LIMIT)


def _sigmoid(x):
    return 0.5 * jnp.tanh(0.5 * x) + 0.5


def _softplus(x):
    return jnp.maximum(x, 0.0) + jnp.log(1.0 + jnp.exp(-jnp.abs(x)))


def _dot(a, b, dims):
    return lax.dot_general(a.astype(BF16), b.astype(BF16), (dims, ((), ())), preferred_element_type=F32)


def _nn(a, b):
    return _dot(a, b, ((1,), (0,)))


def _nt(a, b):
    return _dot(a, b, ((1,), (1,)))


def _tn(a, b):
    return _dot(a, b, ((0,), (0,)))


def _split3(v):
    hi = v.astype(BF16)
    r1 = v - hi.astype(F32)
    mid = r1.astype(BF16)
    lo = (r1 - mid.astype(F32)).astype(BF16)
    return hi, mid, lo


def _mask_nn(mask, v):
    mb = mask.astype(BF16)
    hi, mid, lo = _split3(v)
    return _nn(mb, hi) + (_nn(mb, mid) + _nn(mb, lo))


MM_VMEM_BUDGET = 40 * 1024 * 1024
MM_FULL_K = 2816


def _mm_tiles(m, n, k, a_bytes, b_bytes, o_bytes, has_add):
    tk = k if k <= MM_FULL_K else _pick(k, (2048, 1024, 512, 256, 128))
    tn = 1408 if (n > 1024 and n % 1408 == 0) else _pick(n, (1024, 768, 512, 384, 256, 128))
    for tm in (1408, 1024, 768, 512, 384, 256, 128):
        if m % tm:
            continue
        buffers = 2 * (tm * tk * a_bytes + tk * tn * b_bytes + tm * tn * (o_bytes + (4 if has_add else 0)))
        if tk < k:
            buffers += tm * tn * 4
        if buffers <= MM_VMEM_BUDGET:
            return tm, tn, tk
    return _pick(m, (128,)), tn, tk


def _mm(a, b, *, mode, name, out_dtype=F32, add=None, after=None):
    if mode == "nn":
        (m, k), n = a.shape, b.shape[1]
    elif mode == "nt":
        (m, k), n = a.shape, b.shape[0]
    else:
        (k, m), n = a.shape, b.shape[1]
    has_add = add is not None
    tm, tn, tk = _mm_tiles(m, n, k, a.dtype.itemsize, b.dtype.itemsize, jnp.dtype(out_dtype).itemsize, has_add)
    nk = k // tk
    dims = {"nn": ((1,), (0,)), "nt": ((1,), (1,)), "tn": ((0,), (0,))}[mode]
    a_spec = {"nn": pl.BlockSpec((tm, tk), lambda i, j, kk: (i, kk)),
              "nt": pl.BlockSpec((tm, tk), lambda i, j, kk: (i, kk)),
              "tn": pl.BlockSpec((tk, tm), lambda i, j, kk: (kk, i))}[mode]
    b_spec = {"nn": pl.BlockSpec((tk, tn), lambda i, j, kk: (kk, j)),
              "nt": pl.BlockSpec((tn, tk), lambda i, j, kk: (j, kk)),
              "tn": pl.BlockSpec((tk, tn), lambda i, j, kk: (kk, j))}[mode]
    o_spec = pl.BlockSpec((tm, tn), lambda i, j, kk: (i, j))

    def finish(r, c_ref, o_ref):
        if has_add:
            r = r + c_ref[...]
        o_ref[...] = r.astype(out_dtype)

    def body_one(*refs):
        a_ref, b_ref = refs[:2]
        finish(_dot(a_ref[...], b_ref[...], dims), refs[2] if has_add else None, refs[-1])

    def body_acc(*refs):
        a_ref, b_ref = refs[:2]
        o_ref, acc = refs[-2:]
        kk = pl.program_id(2)

        @pl.when(kk == 0)
        def _():
            acc[...] = jnp.zeros_like(acc)

        acc[...] += _dot(a_ref[...], b_ref[...], dims)

        @pl.when(kk == nk - 1)
        def _():
            finish(acc[...], refs[2] if has_add else None, o_ref)

    in_specs = [a_spec, b_spec] + ([o_spec] if has_add else [])
    args = (a, b) + ((add,) if has_add else ())
    if after is not None:
        in_specs, args = in_specs + [pl.BlockSpec(memory_space=pl.ANY)], args + (after,)
    return pl.pallas_call(
        body_one if nk == 1 else body_acc, name=name, grid=(m // tm, n // tn, nk),
        in_specs=in_specs, out_specs=o_spec,
        out_shape=jax.ShapeDtypeStruct((m, n), out_dtype),
        scratch_shapes=[] if nk == 1 else [pltpu.VMEM((tm, tn), F32)],
        compiler_params=_params("parallel", "parallel", "arbitrary"),
    )(*args)


def _rmsnorm_fwd(x, g, name):
    t, d = x.shape
    tm = _pick(t, (512, 256, 128))

    def body(x_ref, g_ref, o_ref):
        xv = x_ref[...]
        r = lax.rsqrt(jnp.mean(xv * xv, axis=-1, keepdims=True) + EPS)
        o_ref[...] = ((xv * r) * g_ref[...]).astype(BF16)

    return pl.pallas_call(
        body, name=name, grid=(t // tm,),
        in_specs=[pl.BlockSpec((tm, d), lambda i: (i, 0)), pl.BlockSpec((1, d), lambda i: (0, 0))],
        out_specs=pl.BlockSpec((tm, d), lambda i: (i, 0)),
        out_shape=jax.ShapeDtypeStruct((t, d), BF16),
        compiler_params=_params("parallel"),
    )(x, g)


def _proj_residual_norm(a, w, res, g, name):
    t, k = a.shape
    d = w.shape[1]
    tm, _, _ = _mm_tiles(t, d, k, a.dtype.itemsize, w.dtype.itemsize, 4 + 2, True)

    def body(a_ref, w_ref, r_ref, g_ref, x_ref, h_ref):
        xv = r_ref[...] + _nn(a_ref[...], w_ref[...])
        x_ref[...] = xv
        r = lax.rsqrt(jnp.mean(xv * xv, axis=-1, keepdims=True) + EPS)
        h_ref[...] = ((xv * r) * g_ref[...]).astype(BF16)

    row = pl.BlockSpec((tm, d), lambda i: (i, 0))
    return pl.pallas_call(
        body, name=name, grid=(t // tm,),
        in_specs=[pl.BlockSpec((tm, k), lambda i: (i, 0)), pl.BlockSpec((k, d), lambda i: (0, 0)), row,
                  pl.BlockSpec((1, d), lambda i: (0, 0))],
        out_specs=[row, row],
        out_shape=[jax.ShapeDtypeStruct((t, d), F32), jax.ShapeDtypeStruct((t, d), BF16)],
        compiler_params=_params("parallel"),
    )(a, w, res, g)


def _proj_norm_bwd(a, w, x, g, dres, name, add=None, with_bf16=False, after=None):
    t, k = a.shape
    d = w.shape[1]
    has_add = add is not None
    tm, _, tk = _mm_tiles(t, d, k, a.dtype.itemsize, w.dtype.itemsize, 4 + 4 + 4 + (2 if with_bf16 else 0), has_add)
    if tk == k:
        tm = min(tm, 512)
    else:
        tm, tk = _pick(t, (1024, 512, 256, 128)), min(tk, 1024)
    nk = k // tk

    def body(*refs):
        a_ref, w_ref, x_ref, g_ref, dres_ref = refs[:5]
        rest = refs[5 + has_add + (after is not None):]
        dx_ref, dg_ref = rest[:2]
        i, kk = pl.program_id(0), pl.program_id(1)

        @pl.when(jnp.logical_and(i == 0, kk == 0))
        def _():
            dg_ref[...] = jnp.zeros_like(dg_ref)

        part = _nn(a_ref[...], w_ref[...])
        if nk > 1:
            acc = rest[-1]

            @pl.when(kk == 0)
            def _():
                acc[...] = jnp.zeros_like(acc)

            acc[...] += part

        @pl.when(kk == nk - 1)
        def _():
            dhv = part if nk == 1 else acc[...]
            if has_add:
                dhv = dhv + refs[5][...]
            xv = x_ref[...]
            r = lax.rsqrt(jnp.mean(xv * xv, axis=-1, keepdims=True) + EPS)
            xhat = xv * r
            dyg = dhv * g_ref[...]
            dx = dres_ref[...] + r * (dyg - xhat * jnp.mean(dyg * xhat, axis=-1, keepdims=True))
            dx_ref[...] = dx
            if with_bf16:
                rest[2][...] = dx.astype(BF16)
            dg_ref[...] += jnp.sum(dhv * xhat, axis=0, keepdims=True)

    row = pl.BlockSpec((tm, d), lambda i, kk: (i, 0))
    vec = pl.BlockSpec((1, d), lambda i, kk: (0, 0))
    in_specs = [pl.BlockSpec((tm, tk), lambda i, kk: (i, kk)), pl.BlockSpec((tk, d), lambda i, kk: (kk, 0)),
                row, vec, row] + has_add * [row]
    args = (a, w, x, g, dres) + has_add * (add,)
    if after is not None:
        in_specs, args = in_specs + [pl.BlockSpec(memory_space=pl.ANY)], args + (after,)
    return pl.pallas_call(
        body, name=name, grid=(t // tm, nk), in_specs=in_specs, out_specs=[row, vec] + with_bf16 * [row],
        out_shape=[jax.ShapeDtypeStruct((t, d), F32), jax.ShapeDtypeStruct((1, d), F32)]
        + with_bf16 * [jax.ShapeDtypeStruct((t, d), BF16)],
        scratch_shapes=[] if nk == 1 else [pltpu.VMEM((tm, d), F32)],
        compiler_params=_params("arbitrary", "arbitrary"),
    )(*args)


def _rmsnorm_bwd(x, g, dh, dres, name):
    t, d = x.shape
    tm = _pick(t, (512, 256, 128))

    def body(x_ref, g_ref, dh_ref, dres_ref, dx_ref, dg_ref):
        @pl.when(pl.program_id(0) == 0)
        def _():
            dg_ref[...] = jnp.zeros_like(dg_ref)

        xv = x_ref[...]
        r = lax.rsqrt(jnp.mean(xv * xv, axis=-1, keepdims=True) + EPS)
        xhat = xv * r
        dhv = dh_ref[...]
        dyg = dhv * g_ref[...]
        dx_ref[...] = dres_ref[...] + r * (dyg - xhat * jnp.mean(dyg * xhat, axis=-1, keepdims=True))
        dg_ref[...] += jnp.sum(dhv * xhat, axis=0, keepdims=True)

    row = pl.BlockSpec((tm, d), lambda i: (i, 0))
    vec = pl.BlockSpec((1, d), lambda i: (0, 0))
    return pl.pallas_call(
        body, name=name, grid=(t // tm,),
        in_specs=[row, vec, row, row], out_specs=[row, vec],
        out_shape=[jax.ShapeDtypeStruct((t, d), F32), jax.ShapeDtypeStruct((1, d), F32)],
        compiler_params=_params("arbitrary"),
    )(x, g, dh, dres)


def _down_proj_loss_head(act, w_down, res, g, target, name):
    t, k = act.shape
    d = w_down.shape[1]
    tm, _, _ = _mm_tiles(t, d, k, act.dtype.itemsize, w_down.dtype.itemsize, 4 + 2, True)
    tm = min(tm, 512)

    def body(a_ref, w_ref, r_ref, g_ref, t_ref, loss_ref, dx_ref, dg_ref, dxb_ref):
        @pl.when(pl.program_id(0) == 0)
        def _():
            dg_ref[...] = jnp.zeros_like(dg_ref)
            loss_ref[...] = jnp.zeros_like(loss_ref)

        xv = r_ref[...] + _nn(a_ref[...], w_ref[...])
        gv = g_ref[...]
        r = lax.rsqrt(jnp.mean(xv * xv, axis=-1, keepdims=True) + EPS)
        xhat = xv * r
        err = xhat * gv - t_ref[...]
        loss_ref[...] += jnp.sum(err * err) * (0.5 / d)
        dy = err * (1.0 / d)
        dyg = dy * gv
        dx = r * (dyg - xhat * jnp.mean(dyg * xhat, axis=-1, keepdims=True))
        dx_ref[...] = dx
        dxb_ref[...] = dx.astype(BF16)
        dg_ref[...] += jnp.sum(dy * xhat, axis=0, keepdims=True)

    row = pl.BlockSpec((tm, d), lambda i: (i, 0))
    vec = pl.BlockSpec((1, d), lambda i: (0, 0))
    return pl.pallas_call(
        body, name=name, grid=(t // tm,),
        in_specs=[pl.BlockSpec((tm, k), lambda i: (i, 0)), pl.BlockSpec((k, d), lambda i: (0, 0)), row, vec, row],
        out_specs=[pl.BlockSpec((1, LANES), lambda i: (0, 0)), row, vec, row],
        out_shape=[jax.ShapeDtypeStruct((1, LANES), F32), jax.ShapeDtypeStruct((t, d), F32),
                   jax.ShapeDtypeStruct((1, d), F32), jax.ShapeDtypeStruct((t, d), BF16)],
        compiler_params=_params("arbitrary"),
    )(act, w_down, res, g, target)


CONV_HALO = 8
CONV_ROWS = 64


def _conv_taps(window, wv, bv):
    acc = bv + wv[SSM_CONV - 1:SSM_CONV, :] * window(0)
    for sh in range(1, SSM_CONV):
        kidx = SSM_CONV - 1 - sh
        acc = acc + wv[kidx:kidx + 1, :] * window(sh)
    return acc


def _conv_fwd(u, w, bias, name):
    b, s, c = u.shape
    rows = CONV_ROWS

    def body(u_ref, w_ref, b_ref, o_ref, ext):
        ext[0:CONV_HALO, :] = jnp.zeros((CONV_HALO, LANES), F32)
        ext[CONV_HALO:, :] = u_ref[...]
        wv, bv = w_ref[...], b_ref[...]
        for r0 in range(0, s, rows):
            acc = _conv_taps(lambda sh: ext[CONV_HALO + r0 - sh:CONV_HALO + r0 - sh + rows, :], wv, bv)
            o_ref[r0:r0 + rows, :] = acc * _sigmoid(acc)

    strip = pl.BlockSpec((None, s, LANES), lambda bi, j: (bi, 0, j))
    return pl.pallas_call(
        body, name=name, grid=(b, c // LANES),
        in_specs=[strip, pl.BlockSpec((SSM_CONV, LANES), lambda bi, j: (0, j)),
                  pl.BlockSpec((1, LANES), lambda bi, j: (0, j))],
        out_specs=strip, out_shape=jax.ShapeDtypeStruct((b, s, c), F32),
        scratch_shapes=[pltpu.VMEM((CONV_HALO + s, LANES), F32)],
        compiler_params=_params("parallel", "parallel"),
    )(u, w, bias)


def _conv_bwd(u, dout, w, bias, dproj, name):
    b, s, c = u.shape
    rows = CONV_ROWS

    def fold(v):
        return jnp.sum(v.reshape(rows // CONV_HALO, CONV_HALO, LANES), axis=0)

    def body(u_ref, d_ref, w_ref, b_ref, buf_ref, du_ref, dw_ref, db_ref, ext, dpre):
        @pl.when(pl.program_id(1) == 0)
        def _():
            dw_ref[...] = jnp.zeros_like(dw_ref)
            db_ref[...] = jnp.zeros_like(db_ref)

        ext[0:CONV_HALO, :] = jnp.zeros((CONV_HALO, LANES), F32)
        ext[CONV_HALO:, :] = u_ref[...]
        dpre[s:, :] = jnp.zeros((CONV_HALO, LANES), F32)
        wv, bv = w_ref[...], b_ref[...]
        sums = [jnp.zeros((CONV_HALO, LANES), F32)] * (SSM_CONV + 1)
        for r0 in range(0, s, rows):
            window = lambda sh: ext[CONV_HALO + r0 - sh:CONV_HALO + r0 - sh + rows, :]
            acc = _conv_taps(window, wv, bv)
            sg = _sigmoid(acc)
            dp = d_ref[r0:r0 + rows, :] * (sg * (1.0 + acc * (1.0 - sg)))
            dpre[r0:r0 + rows, :] = dp
            taps = [sums[SSM_CONV - 1 - sh] + fold(dp * window(sh)) for sh in range(SSM_CONV)]
            sums = taps[::-1] + [sums[SSM_CONV] + fold(dp)]
        for r0 in range(0, s, rows):
            du = wv[SSM_CONV - 1:SSM_CONV, :] * dpre[r0:r0 + rows, :]
            for sh in range(1, SSM_CONV):
                kidx = SSM_CONV - 1 - sh
                du = du + wv[kidx:kidx + 1, :] * dpre[r0 + sh:r0 + sh + rows, :]
            du_ref[r0:r0 + rows, :] = du.astype(BF16)
        for kidx in range(SSM_CONV):
            dw_ref[kidx:kidx + 1, :] += jnp.sum(sums[kidx], axis=0, keepdims=True)
        db_ref[...] += jnp.sum(sums[SSM_CONV], axis=0, keepdims=True)

    strip = pl.BlockSpec((None, s, LANES), lambda j, bi: (bi, 0, j))
    taps = pl.BlockSpec((SSM_CONV, LANES), lambda j, bi: (0, j))
    vec = pl.BlockSpec((1, LANES), lambda j, bi: (0, j))
    du_cols = pl.BlockSpec((None, s, LANES), lambda j, bi: (bi, 0, DPROJ_COLS["xbc"] // LANES + j))
    return pl.pallas_call(
        body, name=name, grid=(c // LANES, b),
        in_specs=[strip, strip, taps, vec, pl.BlockSpec(memory_space=pl.ANY)], out_specs=[du_cols, taps, vec],
        input_output_aliases={4: 0},
        out_shape=[jax.ShapeDtypeStruct(dproj.shape, dproj.dtype), jax.ShapeDtypeStruct((SSM_CONV, c), F32),
                   jax.ShapeDtypeStruct((1, c), F32)],
        scratch_shapes=[pltpu.VMEM((CONV_HALO + s, LANES), F32), pltpu.VMEM((s + CONV_HALO, LANES), F32)],
        compiler_params=_params("parallel", "arbitrary"),
    )(u, dout, w, bias, dproj)


def _ssd_chunk_terms(dtr_ref, bias_ref, alog_ref):
    q = SSM_CHUNK
    dt = _softplus(dtr_ref[...] + bias_ref[...])
    a_neg = -jnp.exp(alog_ref[...])
    row = lax.broadcasted_iota(jnp.int32, (q, q), 0)
    col = lax.broadcasted_iota(jnp.int32, (q, q), 1)
    lower = row >= col
    s = _mask_nn(lower, dt * a_neg)
    return dt, a_neg, s, s.T, lower


def _head_masks():
    heads = jnp.arange(LANES)[:, None]
    chans = jnp.arange(SSM_D_INNER)[None, :]
    to_channels = (chans // SSM_HEAD_DIM == heads).astype(BF16)
    return to_channels, to_channels.T


def _per_channel(v, to_channels):
    hi = v.astype(BF16)
    lo = (v - hi.astype(F32)).astype(BF16)
    return _nn(hi, to_channels) + _nn(lo, to_channels)


def _per_head(v, to_heads):
    hi = v.astype(BF16)
    lo = (v - hi.astype(F32)).astype(BF16)
    return _nn(hi, to_heads) + _nn(lo, to_heads)


def _decay_terms_per_channel(dt, s_col, to_channels):
    q = SSM_CHUNK
    tot = s_col[q - 1:q, :]
    stacked = jnp.concatenate([dt, jnp.exp(s_col), jnp.exp(tot - s_col)], axis=0)
    wide = _per_channel(stacked, to_channels)
    dtx, esx, decx = wide[:q], wide[q:2 * q], wide[2 * q:]
    return dtx, esx, decx, esx[0:1, :] * decx[0:1, :]


SSM_PAIRS_PER_GROUP = SSM_HEADS_PER_GROUP // 2
SSM_GROUP_CHANNELS = SSM_HEADS_PER_GROUP * SSM_HEAD_DIM


def _split_pair(v):
    first = lax.broadcasted_iota(jnp.int32, v.shape, 1) < SSM_HEAD_DIM
    return jnp.concatenate([jnp.where(first, v, 0.0), jnp.where(first, 0.0, v)], axis=0)


def _ssd_fwd(xc, dtr, dt_bias, a_log, dskx, to_channels, name):
    b, s, _ = xc.shape
    q = SSM_CHUNK
    nc = s // q
    n, gc = SSM_D_STATE, SSM_GROUP_CHANNELS

    def body(xc_ref, dtr_ref, bias_ref, alog_ref, dsk_ref, tc_ref, y_ref, hs_ref, h_scr):
        @pl.when(pl.program_id(1) == 0)
        def _():
            h_scr[...] = jnp.zeros_like(h_scr)

        dt, _, s_col, s_row, lower = _ssd_chunk_terms(dtr_ref, bias_ref, alog_ref)
        dtx, esx, decx, etotx = _decay_terms_per_channel(dt, s_col, tc_ref[...])
        x = xc_ref[:, :SSM_D_INNER]
        xdt = x * dtx
        xdec = xdt * decx
        skip = dsk_ref[...] * x
        for g in range(SSM_N_GROUPS):
            bg = xc_ref[:, SSM_D_INNER + n * g:SSM_D_INNER + n * (g + 1)].astype(BF16)
            cg = xc_ref[:, SSM_D_INNER + n * (SSM_N_GROUPS + g):SSM_D_INNER + n * (SSM_N_GROUPS + g + 1)].astype(BF16)
            gsl = slice(gc * g, gc * (g + 1))
            gm = _nt(cg, bg)
            hgt = h_scr[:, gsl]
            hs_ref[:, gsl] = hgt
            y_off = esx[:, gsl] * _nn(cg, hgt)
            h_scr[:, gsl] = etotx[:, gsl] * hgt + _tn(bg, xdec[:, gsl])
            for k in range(SSM_PAIRS_PER_GROUP):
                h0 = g * SSM_HEADS_PER_GROUP + 2 * k
                lo = gc * g + LANES * k
                ms = []
                for h in (h0, h0 + 1):
                    lm = jnp.exp(jnp.where(lower, s_col[:, h:h + 1] - s_row[h:h + 1, :], NEG_INF))
                    ms.append((gm * lm).astype(BF16))
                y_diag = _nn(jnp.concatenate(ms, axis=1), _split_pair(xdt[:, lo:lo + LANES]))
                y_ref[:, lo:lo + LANES] = y_diag + y_off[:, LANES * k:LANES * (k + 1)] + skip[:, lo:lo + LANES]

    vec = pl.BlockSpec((1, LANES), lambda bi, c: (0, 0))
    return pl.pallas_call(
        body, name=name, grid=(b, nc),
        in_specs=[pl.BlockSpec((None, q, SSM_CONV_DIM), lambda bi, c: (bi, c, 0)),
                  pl.BlockSpec((None, q, LANES), lambda bi, c: (bi, c, 0)), vec, vec,
                  pl.BlockSpec((1, SSM_D_INNER), lambda bi, c: (0, 0)),
                  pl.BlockSpec((LANES, SSM_D_INNER), lambda bi, c: (0, 0))],
        out_specs=[pl.BlockSpec((None, q, SSM_D_INNER), lambda bi, c: (bi, c, 0)),
                   pl.BlockSpec((None, None, n, SSM_D_INNER), lambda bi, c: (bi, c, 0, 0))],
        out_shape=[jax.ShapeDtypeStruct((b, s, SSM_D_INNER), F32),
                   jax.ShapeDtypeStruct((b, nc, n, SSM_D_INNER), F32)],
        scratch_shapes=[pltpu.VMEM((n, SSM_D_INNER), F32)],
        compiler_params=_params("parallel", "arbitrary"),
    )(xc, dtr, dt_bias, a_log, dskx, to_channels)


def _ssd_bwd(xc, dtr, dy, hs, dt_bias, a_log, dskx, to_channels, to_heads, dproj, name):
    b, s, _ = xc.shape
    q = SSM_CHUNK
    nc = s // q
    n, gc = SSM_D_STATE, SSM_GROUP_CHANNELS

    def colsum(v):
        return jnp.sum(v, axis=0, keepdims=True)

    def body(xc_ref, dtr_ref, dy_ref, hs_ref, bias_ref, alog_ref, dsk_ref, tc_ref, th_ref, buf_ref,
             dxc_ref, ddtr_ref, dalog_ref, ddsk_ref, dbias_ref, dh_scr, dxs_scr, dxd_scr, w_scr, dst_scr, rows_scr):
        ci = pl.program_id(1)

        @pl.when(ci == 0)
        def _():
            dh_scr[...] = jnp.zeros_like(dh_scr)

        @pl.when(jnp.logical_and(pl.program_id(0) == 0, ci == 0))
        def _():
            dalog_ref[...] = jnp.zeros_like(dalog_ref)
            ddsk_ref[...] = jnp.zeros_like(ddsk_ref)
            dbias_ref[...] = jnp.zeros_like(dbias_ref)
            dst_scr[...] = jnp.zeros_like(dst_scr)

        dt, a_neg, s_col, s_row, lower = _ssd_chunk_terms(dtr_ref, bias_ref, alog_ref)
        upper = jnp.logical_not(lower) | (lax.broadcasted_iota(jnp.int32, (q, q), 0)
                                          == lax.broadcasted_iota(jnp.int32, (q, q), 1))
        dtx, esx, decx, etotx = _decay_terms_per_channel(dt, s_col, tc_ref[...])
        x = xc_ref[:, :SSM_D_INNER]
        dyv = dy_ref[...]
        xdt = x * dtx
        xdec = xdt * decx
        dw = esx * dyv
        rows_scr[...] = jnp.zeros_like(rows_scr)
        for g in range(SSM_N_GROUPS):
            b_lo = SSM_D_INNER + n * g
            c_lo = SSM_D_INNER + n * (SSM_N_GROUPS + g)
            bg = xc_ref[:, b_lo:b_lo + n].astype(BF16)
            cg = xc_ref[:, c_lo:c_lo + n].astype(BF16)
            gsl = slice(gc * g, gc * (g + 1))
            gm = _nt(cg, bg)
            gmt = _nt(bg, cg)
            hgt = hs_ref[:, gsl]
            dhgt = dh_scr[:, gsl]
            w_scr[:, gsl] = _nn(cg, hgt)
            dcg = _nt(dw[:, gsl], hgt)
            dxs = decx[:, gsl] * _nn(bg, dhgt)
            dxs_scr[:, gsl] = dxs
            dbg = _nt(xdec[:, gsl], dhgt)
            rows_scr[2:3, gsl] = colsum(dhgt * hgt)
            dh_scr[:, gsl] = _tn(cg, dw[:, gsl]) + etotx[:, gsl] * dhgt
            dg = jnp.zeros((q, q), F32)
            dgt = jnp.zeros((q, q), F32)
            for k in range(SSM_PAIRS_PER_GROUP):
                h0 = g * SSM_HEADS_PER_GROUP + 2 * k
                lo = gc * g + LANES * k
                xp = xdt[:, lo:lo + LANES]
                dyp = dyv[:, lo:lo + LANES]
                dy2 = _split_pair(dyp)
                dm2 = _nt(dy2, xp)
                dmt2 = _nt(_split_pair(xp), dyp)
                mts = []
                for i, h in enumerate((h0, h0 + 1)):
                    lm = jnp.exp(jnp.where(lower, s_col[:, h:h + 1] - s_row[h:h + 1, :], NEG_INF))
                    lmt = jnp.exp(jnp.where(upper, s_row[h:h + 1, :] - s_col[:, h:h + 1], NEG_INF))
                    dm = dm2[q * i:q * (i + 1), :]
                    dmt = dmt2[q * i:q * (i + 1), :]
                    dg = dg + dm * lm
                    dgt = dgt + dmt * lmt
                    mt = gmt * lmt
                    dst_scr[h:h + 1, :] = colsum(dmt * mt) - colsum(dm * (gm * lm))
                    mts.append(mt.astype(BF16))
                dxd_scr[:, lo:lo + LANES] = _nn(jnp.concatenate(mts, axis=1), dy2)
            dxc_ref[:, b_lo:b_lo + n] = dbg + _nn(dgt, cg)
            dxc_ref[:, c_lo:c_lo + n] = dcg + _nn(dg, bg)
        dxs = dxs_scr[...]
        dxdt = dxd_scr[...] + dxs
        dxc_ref[:, :SSM_D_INNER] = dxdt * dtx + dsk_ref[...] * dyv
        state_part = xdt * dxs
        rows_scr[0:1, :] = colsum(dyv * x)
        rows_scr[1:2, :] = colsum(state_part)
        th = th_ref[...]
        per_head = _per_head(jnp.concatenate([dw * w_scr[...] - state_part, dxdt * x], axis=0), th)
        r_ds, r_dt = per_head[:q], per_head[q:]
        sums = _per_head(rows_scr[...], th)
        etot = jnp.exp(s_col[q - 1:q, :])
        dtot = sums[1:2, :] + etot * sums[2:3, :]
        last = lax.broadcasted_iota(jnp.int32, (q, LANES), 0) == q - 1
        ds = dst_scr[...].T + r_ds + jnp.where(last, dtot, 0.0)
        da = _mask_nn(upper, ds)
        ddt = da * a_neg + r_dt
        live = lax.broadcasted_iota(jnp.int32, (1, LANES), 1) < SSM_N_HEADS
        sg = _sigmoid(dtr_ref[...] + bias_ref[...])
        ddtr = jnp.where(live, ddt * sg, 0.0)
        ddtr_ref[:, :LANES] = ddtr.astype(BF16)
        ddtr_ref[:, LANES:] = jnp.zeros((q, DPROJ_DT_WIDTH - LANES), BF16)
        dalog_ref[...] += jnp.where(live, colsum(da * dt) * a_neg, 0.0)
        ddsk_ref[...] += jnp.where(live, sums[0:1, :], 0.0)
        dbias_ref[...] += colsum(ddtr)

    rev = lambda bi, c: (bi, nc - 1 - c, 0)
    vec = pl.BlockSpec((1, LANES), lambda bi, c: (0, 0))
    wide = pl.BlockSpec((None, q, SSM_D_INNER), rev)
    return pl.pallas_call(
        body, name=name, grid=(b, nc),
        in_specs=[pl.BlockSpec((None, q, SSM_CONV_DIM), rev), pl.BlockSpec((None, q, LANES), rev), wide,
                  pl.BlockSpec((None, None, n, SSM_D_INNER), lambda bi, c: (bi, nc - 1 - c, 0, 0)),
                  vec, vec, pl.BlockSpec((1, SSM_D_INNER), lambda bi, c: (0, 0)),
                  pl.BlockSpec((LANES, SSM_D_INNER), lambda bi, c: (0, 0)),
                  pl.BlockSpec((SSM_D_INNER, LANES), lambda bi, c: (0, 0)),
                  pl.BlockSpec(memory_space=pl.ANY)],
        out_specs=[pl.BlockSpec((None, q, SSM_CONV_DIM), rev),
                   pl.BlockSpec((None, q, DPROJ_DT_WIDTH),
                                lambda bi, c: (bi, nc - 1 - c, DPROJ_COLS["dt"] // DPROJ_DT_WIDTH)), vec, vec, vec],
        input_output_aliases={9: 1},
        out_shape=[jax.ShapeDtypeStruct((b, s, SSM_CONV_DIM), F32), jax.ShapeDtypeStruct(dproj.shape, dproj.dtype),
                   jax.ShapeDtypeStruct((1, LANES), F32), jax.ShapeDtypeStruct((1, LANES), F32),
                   jax.ShapeDtypeStruct((1, LANES), F32)],
        scratch_shapes=[pltpu.VMEM((n, SSM_D_INNER), F32)] + [pltpu.VMEM((q, SSM_D_INNER), F32)] * 3
        + [pltpu.VMEM((LANES, q), F32), pltpu.VMEM((8, SSM_D_INNER), F32)],
        compiler_params=_params("arbitrary", "arbitrary"),
    )(xc, dtr, dy, hs, dt_bias, a_log, dskx, to_channels, to_heads, dproj)


SSM_GROUP_WIDTH = SSM_D_INNER // SSM_N_GROUPS


def _gate_norm_fwd(y, z, w, name):
    t, d = y.shape
    tm = _pick(t, (256, 128))

    def body(y_ref, z_ref, w_ref, o_ref):
        for g in range(SSM_N_GROUPS):
            sl = slice(SSM_GROUP_WIDTH * g, SSM_GROUP_WIDTH * (g + 1))
            zv = z_ref[:, sl]
            u = y_ref[:, sl] * (zv * _sigmoid(zv))
            r = lax.rsqrt(jnp.mean(u * u, axis=-1, keepdims=True) + EPS)
            o_ref[:, sl] = ((u * r) * w_ref[:, sl]).astype(BF16)

    row = pl.BlockSpec((tm, d), lambda i: (i, 0))
    return pl.pallas_call(
        body, name=name, grid=(t // tm,),
        in_specs=[row, row, pl.BlockSpec((1, d), lambda i: (0, 0))], out_specs=row,
        out_shape=jax.ShapeDtypeStruct((t, d), BF16),
        compiler_params=_params("parallel"),
    )(y, z, w)


def _ssm_out_dx_gate_norm_bwd(dys, w_ssm_out, y, z, w, dproj, name):
    t, d = y.shape
    k = dys.shape[1]
    gw = SSM_GROUP_WIDTH
    tm = _pick(t, (512, 256, 128))

    def body(dys_ref, ws_ref, y_ref, z_ref, w_ref, buf_ref, dy_ref, dz_ref, dw_ref):
        @pl.when(pl.program_id(0) == 0)
        def _():
            dw_ref[...] = jnp.zeros_like(dw_ref)

        dout = _nt(dys_ref[...], ws_ref[...])
        for g in range(SSM_N_GROUPS):
            sl = slice(gw * g, gw * (g + 1))
            zv = z_ref[:, sl]
            yv = y_ref[:, sl]
            sg = _sigmoid(zv)
            silu = zv * sg
            u = yv * silu
            r = lax.rsqrt(jnp.mean(u * u, axis=-1, keepdims=True) + EPS)
            uh = u * r
            dov = dout[:, sl]
            dw_ref[:, sl] += jnp.sum(dov * uh, axis=0, keepdims=True)
            dyg = dov * w_ref[:, sl]
            du = r * (dyg - uh * jnp.mean(dyg * uh, axis=-1, keepdims=True))
            dy_ref[:, sl] = du * silu
            dz_ref[:, sl] = (du * yv * (sg * (1.0 + zv * (1.0 - sg)))).astype(BF16)

    row = pl.BlockSpec((tm, d), lambda i: (i, 0))
    vec = pl.BlockSpec((1, d), lambda i: (0, 0))
    z_cols = pl.BlockSpec((tm, d), lambda i: (i, DPROJ_COLS["z"] // d))
    return pl.pallas_call(
        body, name=name, grid=(t // tm,),
        in_specs=[pl.BlockSpec((tm, k), lambda i: (i, 0)), pl.BlockSpec((d, k), lambda i: (0, 0)), row, row, vec,
                  pl.BlockSpec(memory_space=pl.ANY)],
        out_specs=[row, z_cols, vec],
        out_shape=[jax.ShapeDtypeStruct((t, d), F32), jax.ShapeDtypeStruct(dproj.shape, dproj.dtype),
                   jax.ShapeDtypeStruct((1, d), F32)],
        input_output_aliases={5: 1},
        compiler_params=_params("arbitrary"),
    )(dys, w_ssm_out, y, z, w, dproj)


def _rope_tables(s):
    half = ATT_HEAD_DIM // 2
    inv = ROPE_THETA ** (-jnp.arange(half, dtype=F32) / half)
    ang = jnp.arange(s).astype(F32)[:, None] * inv[None, :]
    cos, sin = jnp.cos(ang), jnp.sin(ang)
    return jnp.concatenate([cos, cos], axis=-1), jnp.concatenate([-sin, sin], axis=-1)


ATT_TILE = 256


def _by_residue_spec(r, width):
    return pl.BlockSpec((None, r, ATT_TILE // r, width), lambda bi, i: (bi, 0, i, 0))


def _to_residues(tile, stage, r, store):
    if r == 1:
        store(0, tile)
        return
    stage[...] = tile
    for ri in range(r):
        store(ri, stage[pl.ds(ri, tile.shape[0] // r, stride=r), :])


def _from_residues(load, stage, r):
    if r == 1:
        return load(0)
    for ri in range(r):
        stage[pl.ds(ri, ATT_TILE // r, stride=r), :] = load(ri)
    return stage[...]


QKV_ROWS = 1024
QKV_COLS = 768


def _qkv_proj_rope(h, w_qkv_t, cosf, sinf, b, s, name):
    t, k = h.shape
    tm, d, gw = QKV_ROWS, ATT_HEAD_DIM, ATT_OUT_DIM
    per_seq = s // tm

    def body(h_ref, w_ref, c_ref, s_ref, *rest):
        outs, stage = rest[:-1], rest[-1]
        cv, sv = c_ref[...], s_ref[...]
        hv = h_ref[...]
        for lo in range(0, ATT_QKV_DIM, QKV_COLS):
            acc = _nt(hv, w_ref[lo:lo + QKV_COLS, :])
            for hh in range(QKV_COLS // d):
                kind, head = divmod(lo // d + hh, ATT_N_HEADS)
                gi, j = divmod(head, ATT_HEADS_PER_GROUP)
                dst = slice(kind * gw + d * j, kind * gw + d * (j + 1))
                tv = acc[:, d * hh:d * (hh + 1)]
                if kind < 2:
                    tv = tv * cv + pltpu.roll(tv, d // 2, 1) * sv

                def store(ri, rows, o_ref=outs[gi], dst=dst):
                    o_ref[ri, :, dst] = rows.astype(BF16)

                _to_residues(tv, stage, ATT_DILATIONS[gi], store)

    tab = pl.BlockSpec((tm, d), lambda i: (i % per_seq, 0))
    return pl.pallas_call(
        body, name=name, grid=(t // tm,),
        in_specs=[pl.BlockSpec((tm, k), lambda i: (i, 0)), pl.BlockSpec((ATT_QKV_DIM, k), lambda i: (0, 0)), tab, tab],
        out_specs=[pl.BlockSpec((None, r, tm // r, 3 * gw), lambda i: (i // per_seq, 0, i % per_seq, 0))
                   for r in ATT_DILATIONS],
        out_shape=[jax.ShapeDtypeStruct((b, r, s // r, 3 * gw), BF16) for r in ATT_DILATIONS],
        scratch_shapes=[pltpu.VMEM((tm, d), F32)],
        compiler_params=_params("parallel"),
    )(h, w_qkv_t, cosf, sinf)


def _rope_bwd(dq, dk, dv, cosf, sinf, dproj, name):
    n_pat = len(ATT_DILATIONS)
    b, _, s, gw = dq[0].shape
    ts, d = ATT_TILE, ATT_HEAD_DIM

    def body(*refs):
        ins, (c_ref, s_ref, _, o_ref, stage) = refs[:3 * n_pat], refs[3 * n_pat:]
        cv, sv = c_ref[...], s_ref[...]
        for kind in range(3):
            for gi, r in enumerate(ATT_DILATIONS):
                src = ins[kind * n_pat + gi]
                for j in range(ATT_HEADS_PER_GROUP):
                    tv = _from_residues(lambda ri, src=src, j=j: src[ri, :, d * j:d * (j + 1)], stage, r)
                    if kind < 2:
                        tv = tv * cv + pltpu.roll(tv * sv, d // 2, 1)
                    lo = d * (kind * ATT_N_HEADS + gi * ATT_HEADS_PER_GROUP + j)
                    o_ref[:, lo:lo + d] = tv.astype(BF16)

    tab = pl.BlockSpec((ts, d), lambda bi, i: (i, 0))
    parts = [_by_residue_spec(r, gw) for r in ATT_DILATIONS]
    return pl.pallas_call(
        body, name=name, grid=(b, s // ts), in_specs=parts * 3 + [tab, tab, pl.BlockSpec(memory_space=pl.ANY)],
        out_specs=pl.BlockSpec((None, ts, ATT_QKV_DIM), lambda bi, i: (bi, i, DPROJ_COLS["qkv"] // ATT_QKV_DIM)),
        out_shape=jax.ShapeDtypeStruct(dproj.shape, dproj.dtype),
        input_output_aliases={3 * n_pat + 2: 0},
        scratch_shapes=[pltpu.VMEM((ts, d), F32)],
        compiler_params=_params("parallel", "parallel"),
    )(*dq, *dk, *dv, cosf, sinf, dproj)


ATT_SCALE = ATT_HEAD_DIM ** -0.5
ATT_STEP = 2 * ATT_BLOCK


def _att_spec(col):
    return pl.BlockSpec((None, None, ATT_STEP, ATT_OUT_DIM), lambda bi, ri, i: (bi, ri, i, col))


def _att_edge_spec(col, side, n_steps):
    def index(bi, ri, i):
        blk = 2 * i - 1 if side < 0 else 2 * i + 2
        return (bi, ri, jnp.clip(blk, 0, 2 * n_steps - 1), col)
    return pl.BlockSpec((None, None, ATT_BLOCK, ATT_OUT_DIM), index)


def _band_mask(shape, q_axis, has_prev):
    qi = lax.broadcasted_iota(jnp.int32, shape, q_axis)
    kj = lax.broadcasted_iota(jnp.int32, shape, 1 - q_axis)
    dist = qi + ATT_BLOCK - kj
    return (dist >= 0) & (dist <= ATT_BLOCK) & (has_prev | (kj >= ATT_BLOCK))


def _att_fwd(qkr, name):
    b, r, l, _ = qkr.shape
    nb = l // ATT_STEP
    d = ATT_HEAD_DIM

    def body(q_ref, kp_ref, k_ref, vp_ref, v_ref, o_ref, lse_ref):
        mask = _band_mask((ATT_STEP, ATT_BLOCK + ATT_STEP), 0, pl.program_id(2) > 0)
        heads = [slice(d * j, d * (j + 1)) for j in range(ATT_HEADS_PER_GROUP)]
        scores = [_nt(q_ref[:, sl], jnp.concatenate([kp_ref[:, sl], k_ref[:, sl]], axis=0)) for sl in heads]
        scores = [jnp.where(mask, sc * ATT_SCALE, NEG_INF) for sc in scores]
        tops = [jnp.max(sc, axis=-1, keepdims=True) for sc in scores]
        probs = [jnp.exp(sc - m) for sc, m in zip(scores, tops)]
        dens = [jnp.sum(pr, axis=-1, keepdims=True) for pr in probs]
        for sl, m, pr, den in zip(heads, tops, probs, dens):
            o_ref[:, sl] = _nn(pr / den, jnp.concatenate([vp_ref[:, sl], v_ref[:, sl]], axis=0))
            lse_ref[:, sl] = jnp.broadcast_to(m + jnp.log(den), (ATT_STEP, d))

    out_spec = _att_spec(0)
    return pl.pallas_call(
        body, name=name, grid=(b, r, nb),
        in_specs=[_att_spec(0), _att_edge_spec(1, -1, nb), _att_spec(1), _att_edge_spec(2, -1, nb), _att_spec(2)],
        out_specs=[out_spec, out_spec],
        out_shape=[jax.ShapeDtypeStruct((b, r, l, ATT_OUT_DIM), F32)] * 2,
        compiler_params=_params("parallel", "parallel", "parallel"),
    )(qkr, qkr, qkr, qkr, qkr)


def _att_merge(os_, lses, name):
    n_pat = len(os_)
    b, _, s, gw = os_[0].shape
    ts, d = ATT_TILE, ATT_HEAD_DIM

    def body(*refs):
        o_refs, l_refs = refs[:n_pat], refs[n_pat:2 * n_pat]
        att_ref, lse_outs, stage = refs[2 * n_pat], refs[2 * n_pat + 1:3 * n_pat + 1], refs[-1]
        for j in range(ATT_HEADS_PER_GROUP):
            sl = slice(d * j, d * (j + 1))
            ov = [_from_residues(lambda ri, g=g: o_refs[g][ri, :, sl], stage, r)
                  for g, r in enumerate(ATT_DILATIONS)]
            ls = [_from_residues(lambda ri, g=g: l_refs[g][ri, :, sl], stage, r)
                  for g, r in enumerate(ATT_DILATIONS)]
            m = functools.reduce(jnp.maximum, ls)
            es = [jnp.exp(lv - m) for lv in ls]
            tot = functools.reduce(lambda u, v: u + v, es)
            acc = (es[0] / tot) * ov[0]
            for g in range(1, n_pat):
                acc = acc + (es[g] / tot) * ov[g]
            att_ref[:, sl] = acc
            joint = m + jnp.log(tot)
            for g, r in enumerate(ATT_DILATIONS):
                def store(ri, rows, out=lse_outs[g]):
                    out[ri, :, sl] = rows
                _to_residues(joint, stage, r, store)

    parts = [_by_residue_spec(r, gw) for r in ATT_DILATIONS]
    return pl.pallas_call(
        body, name=name, grid=(b, s // ts), in_specs=parts * 2,
        out_specs=[pl.BlockSpec((None, ts, gw), lambda bi, i: (bi, i, 0))] + parts,
        out_shape=[jax.ShapeDtypeStruct((b, s, gw), F32)]
        + [jax.ShapeDtypeStruct((b, r, s // r, gw), F32) for r in ATT_DILATIONS],
        scratch_shapes=[pltpu.VMEM((ts, d), F32)],
        compiler_params=_params("parallel", "parallel"),
    )(*os_, *lses)


def _att_delta(att, datt, name):
    b, s, gw = att.shape
    ts, d = ATT_TILE, ATT_HEAD_DIM
    n_pat = len(ATT_DILATIONS)

    def body(a_ref, d_ref, *rest):
        do_outs, dl_outs, stage = rest[:n_pat], rest[n_pat:2 * n_pat], rest[-1]
        for j in range(ATT_HEADS_PER_GROUP):
            sl = slice(d * j, d * (j + 1))
            dv = d_ref[:, sl]
            delta = jnp.broadcast_to(jnp.sum(a_ref[:, sl] * dv, axis=-1, keepdims=True), (ts, d))
            for g, r in enumerate(ATT_DILATIONS):
                def store_do(ri, rows, out=do_outs[g]):
                    out[ri, :, sl] = rows.astype(BF16)

                def store_dl(ri, rows, out=dl_outs[g]):
                    out[ri, :, sl] = rows

                _to_residues(dv, stage, r, store_do)
                _to_residues(delta, stage, r, store_dl)

    row = pl.BlockSpec((None, ts, gw), lambda bi, i: (bi, i, 0))
    parts = [_by_residue_spec(r, gw) for r in ATT_DILATIONS]
    outs = pl.pallas_call(
        body, name=name, grid=(b, s // ts), in_specs=[row, row], out_specs=parts * 2,
        out_shape=[jax.ShapeDtypeStruct((b, r, s // r, gw), BF16) for r in ATT_DILATIONS]
        + [jax.ShapeDtypeStruct((b, r, s // r, gw), F32) for r in ATT_DILATIONS],
        scratch_shapes=[pltpu.VMEM((ts, d), F32)],
        compiler_params=_params("parallel", "parallel"),
    )(att, datt)
    return outs[:n_pat], outs[n_pat:]


def _att_bwd_q(qkr, datt, lse, delta, name):
    b, r, l, _ = qkr.shape
    nb = l // ATT_STEP
    d = ATT_HEAD_DIM

    def body(q_ref, kp_ref, k_ref, vp_ref, v_ref, do_ref, lse_ref, dl_ref, dq_ref):
        mask = _band_mask((ATT_STEP, ATT_BLOCK + ATT_STEP), 0, pl.program_id(2) > 0)
        heads = [slice(d * j, d * (j + 1)) for j in range(ATT_HEADS_PER_GROUP)]
        kcats = [jnp.concatenate([kp_ref[:, sl], k_ref[:, sl]], axis=0) for sl in heads]
        scores = [_nt(q_ref[:, sl], kcat) for sl, kcat in zip(heads, kcats)]
        dps = [_nt(do_ref[:, sl], jnp.concatenate([vp_ref[:, sl], v_ref[:, sl]], axis=0)) for sl in heads]
        probs = [jnp.exp(jnp.where(mask, sc * ATT_SCALE - lse_ref[:, sl.start:sl.start + 1], NEG_INF))
                 for sl, sc in zip(heads, scores)]
        dscs = [pr * (dp - dl_ref[:, sl.start:sl.start + 1]) for sl, pr, dp in zip(heads, probs, dps)]
        for sl, dsc, kcat in zip(heads, dscs, kcats):
            dq_ref[:, sl] = _nn(dsc, kcat) * ATT_SCALE

    tok = _att_spec(0)
    return pl.pallas_call(
        body, name=name, grid=(b, r, nb),
        in_specs=[_att_spec(0), _att_edge_spec(1, -1, nb), _att_spec(1), _att_edge_spec(2, -1, nb), _att_spec(2),
                  tok, tok, tok],
        out_specs=tok,
        out_shape=jax.ShapeDtypeStruct((b, r, l, ATT_OUT_DIM), F32),
        compiler_params=_params("parallel", "parallel", "parallel"),
    )(qkr, qkr, qkr, qkr, qkr, datt, lse, delta)


def _att_bwd_kv(qkr, datt, lse, delta, name):
    b, r, l, _ = qkr.shape
    nb = l // ATT_STEP
    d = ATT_HEAD_DIM

    def body(k_ref, v_ref, q_ref, qn_ref, do_ref, don_ref, lse_ref, lsen_ref, dl_ref, dln_ref, dk_ref, dv_ref):
        shape = (ATT_STEP, ATT_STEP + ATT_BLOCK)
        kj = lax.broadcasted_iota(jnp.int32, shape, 0)
        qi = lax.broadcasted_iota(jnp.int32, shape, 1)
        dist = qi - kj
        has_next = pl.program_id(2) < nb - 1
        mask = (dist >= 0) & (dist <= ATT_BLOCK) & (has_next | (qi < ATT_STEP))
        def per_query(own_ref, next_ref, sl):
            return jnp.tile(jnp.concatenate([own_ref[:, sl], next_ref[:, sl]], axis=0).T, (ATT_STEP // d, 1))

        heads = [slice(d * j, d * (j + 1)) for j in range(ATT_HEADS_PER_GROUP)]
        qcats = [jnp.concatenate([q_ref[:, sl], qn_ref[:, sl]], axis=0) for sl in heads]
        docats = [jnp.concatenate([do_ref[:, sl], don_ref[:, sl]], axis=0) for sl in heads]
        scores = [_nt(k_ref[:, sl], qcat) for sl, qcat in zip(heads, qcats)]
        dps = [_nt(v_ref[:, sl], docat) for sl, docat in zip(heads, docats)]
        probs = [jnp.exp(jnp.where(mask, sc * ATT_SCALE - per_query(lse_ref, lsen_ref, sl), NEG_INF))
                 for sl, sc in zip(heads, scores)]
        for sl, pr, docat in zip(heads, probs, docats):
            dv_ref[:, sl] = _nn(pr, docat)
        dscs = [pr * (dp - per_query(dl_ref, dln_ref, sl)) for sl, pr, dp in zip(heads, probs, dps)]
        for sl, dsc, qcat in zip(heads, dscs, qcats):
            dk_ref[:, sl] = _nn(dsc, qcat) * ATT_SCALE

    tok, tok_n = _att_spec(0), _att_edge_spec(0, 1, nb)
    return pl.pallas_call(
        body, name=name, grid=(b, r, nb),
        in_specs=[_att_spec(1), _att_spec(2), _att_spec(0), _att_edge_spec(0, 1, nb),
                  tok, tok_n, tok, tok_n, tok, tok_n],
        out_specs=[tok, tok],
        out_shape=[jax.ShapeDtypeStruct((b, r, l, ATT_OUT_DIM), F32)] * 2,
        compiler_params=_params("parallel", "parallel", "parallel"),
    )(qkr, qkr, qkr, qkr, datt, datt, lse, lse, delta, delta)


def _att_out_proj_mix(att, w_att_t, gl, bg, ys, name):
    t, k = att.shape
    d = w_att_t.shape[0]
    tm = _pick(t, (512, 256, 128))

    def body(a_ref, w_ref, gl_ref, bg_ref, ys_ref, ya_ref, o_ref):
        ya = _nt(a_ref[...], w_ref[...])
        ya_ref[...] = ya
        g0 = _sigmoid(gl_ref[:, :d] + bg_ref[:, :d])
        g1 = _sigmoid(gl_ref[:, d:] + bg_ref[:, d:])
        o_ref[...] = (g0 * ys_ref[...] + g1 * ya).astype(BF16)

    row = pl.BlockSpec((tm, d), lambda i: (i, 0))
    return pl.pallas_call(
        body, name=name, grid=(t // tm,),
        in_specs=[pl.BlockSpec((tm, k), lambda i: (i, 0)), pl.BlockSpec((d, k), lambda i: (0, 0)),
                  pl.BlockSpec((tm, 2 * d), lambda i: (i, 0)), pl.BlockSpec((1, 2 * d), lambda i: (0, 0)), row],
        out_specs=[row, row],
        out_shape=[jax.ShapeDtypeStruct((t, d), F32), jax.ShapeDtypeStruct((t, d), BF16)],
        compiler_params=_params("parallel"),
    )(att, w_att_t, gl, bg, ys)


def _mix_out_dx_mix_bwd(dx, w_mix, gl, bg, ys, ya, name):
    t, d = ys.shape
    tm = _pick(t, (512, 256, 128))

    def body(dx_ref, w_ref, gl_ref, bg_ref, ys_ref, ya_ref, dys_ref, dya_ref, dgl_ref, dbg_ref):
        @pl.when(pl.program_id(0) == 0)
        def _():
            dbg_ref[...] = jnp.zeros_like(dbg_ref)

        dm = _nt(dx_ref[...], w_ref[...])
        g0 = _sigmoid(gl_ref[:, :d] + bg_ref[:, :d])
        g1 = _sigmoid(gl_ref[:, d:] + bg_ref[:, d:])
        dys_ref[...] = (dm * g0).astype(BF16)
        dya_ref[...] = (dm * g1).astype(BF16)
        d0 = dm * ys_ref[...] * (g0 * (1.0 - g0))
        d1 = dm * ya_ref[...] * (g1 * (1.0 - g1))
        dgl_ref[:, :d] = d0.astype(BF16)
        dgl_ref[:, d:] = d1.astype(BF16)
        dbg_ref[:, :d] += jnp.sum(d0, axis=0, keepdims=True)
        dbg_ref[:, d:] += jnp.sum(d1, axis=0, keepdims=True)

    row = pl.BlockSpec((tm, d), lambda i: (i, 0))
    wide = pl.BlockSpec((tm, 2 * d), lambda i: (i, 0))
    vec = pl.BlockSpec((1, 2 * d), lambda i: (0, 0))
    gate_cols = pl.BlockSpec((tm, 2 * d), lambda i: (i, DPROJ_COLS["gate"] // (2 * d)))
    return pl.pallas_call(
        body, name=name, grid=(t // tm,),
        in_specs=[row, pl.BlockSpec((d, d), lambda i: (0, 0)), wide, vec, row, row],
        out_specs=[row, row, gate_cols, vec],
        out_shape=[jax.ShapeDtypeStruct((t, d), BF16), jax.ShapeDtypeStruct((t, d), BF16),
                   jax.ShapeDtypeStruct((t, DPROJ_WIDTH), BF16), jax.ShapeDtypeStruct((1, 2 * d), F32)],
        compiler_params=_params("arbitrary"),
    )(dx, w_mix, gl, bg, ys, ya)


def _gate_up_proj_swiglu(h, w_gate_t, w_up_t, name):
    t, k = h.shape
    f = w_up_t.shape[0]
    tm, tn, _ = _mm_tiles(t, f, k, h.dtype.itemsize, w_gate_t.dtype.itemsize + w_up_t.dtype.itemsize, 4 + 4 + 2, False)

    def body(h_ref, wg_ref, wu_ref, gt_ref, up_ref, act_ref):
        hv = h_ref[...]
        gv = _nt(hv, wg_ref[...])
        gt_ref[...] = gv
        up = _nt(hv, wu_ref[...])
        up_ref[...] = up
        act_ref[...] = ((gv * _sigmoid(gv)) * up).astype(BF16)

    tile = pl.BlockSpec((tm, tn), lambda i, j: (i, j))
    w_tile = pl.BlockSpec((tn, k), lambda i, j: (j, 0))
    return pl.pallas_call(
        body, name=name, grid=(t // tm, f // tn),
        in_specs=[pl.BlockSpec((tm, k), lambda i, j: (i, 0)), w_tile, w_tile],
        out_specs=[tile, tile, tile],
        out_shape=[jax.ShapeDtypeStruct((t, f), F32), jax.ShapeDtypeStruct((t, f), F32),
                   jax.ShapeDtypeStruct((t, f), BF16)],
        compiler_params=_params("parallel", "parallel"),
    )(h, w_gate_t, w_up_t)


def _down_dx_swiglu_bwd(dx, w_down, gt, up, name):
    t, k = dx.shape
    f = w_down.shape[0]
    tm, tn, _ = _mm_tiles(t, f, k, dx.dtype.itemsize, w_down.dtype.itemsize, 2 + 2, True)
    tm = min(tm, 512)

    def body(d_ref, w_ref, g_ref, u_ref, dg_ref, du_ref):
        dact = _nt(d_ref[...], w_ref[...])
        gv = g_ref[...]
        sg = _sigmoid(gv)
        dg_ref[...] = (dact * u_ref[...] * (sg * (1.0 + gv * (1.0 - sg)))).astype(BF16)
        du_ref[...] = (dact * (gv * sg)).astype(BF16)

    tile = pl.BlockSpec((tm, tn), lambda i, j: (i, j))
    return pl.pallas_call(
        body, name=name, grid=(t // tm, f // tn),
        in_specs=[pl.BlockSpec((tm, k), lambda i, j: (i, 0)), pl.BlockSpec((tn, k), lambda i, j: (j, 0)), tile, tile],
        out_specs=[tile, tile], out_shape=[jax.ShapeDtypeStruct((t, f), BF16)] * 2,
        compiler_params=_params("parallel", "parallel"),
    )(dx, w_down, gt, up)


def _peer(k):
    x, y, c = lax.axis_index("x"), lax.axis_index("y"), lax.axis_index("c")
    px, py, pc = x ^ ((k >> 2) & 1), y ^ ((k >> 1) & 1), c ^ (k & 1)
    return (px, py, pc), 4 * px + 2 * py + pc


def _my_index():
    return 4 * lax.axis_index("x") + 2 * lax.axis_index("y") + lax.axis_index("c")


def _all_gather(parts, name):
    n_parts = len(parts)

    def body(*refs):
        ins, outs = refs[:n_parts], refs[n_parts:2 * n_parts]
        send_sems, recv_sems, local_sems = refs[2 * n_parts:]
        here, me = _peer(0)
        sibling, sib_idx = _peer(1)
        chips = [_peer(2 * q) for q in range(1, N_CHIPS)]

        def copy(i, k, block, to, src=None):
            return pltpu.make_async_remote_copy(
                src_ref=outs[i].at[block] if src is None else src, dst_ref=outs[i].at[block],
                send_sem=send_sems.at[i * (N_DEV - 1) + k], recv_sem=recv_sems.at[i * (N_DEV - 1) + k],
                device_id=to, device_id_type=MESH)

        local = [pltpu.make_async_copy(ins[i], outs[i].at[me], local_sems.at[i]) for i in range(n_parts)]
        for cp in local:
            cp.start()
        sends = []
        for i in range(n_parts):
            sends.append(copy(i, 0, me, sibling, src=ins[i]))
            sends += [copy(i, q, me, chip, src=ins[i]) for q, (chip, _) in enumerate(chips, start=1)]
        for cp in sends:
            cp.start()
        for q, (chip, chip_idx) in enumerate(chips, start=1):
            for i in range(n_parts):
                copy(i, q, chip_idx, here).wait_recv()
                fwd = copy(i, N_CHIPS - 1 + q, chip_idx, sibling)
                fwd.start()
                sends.append(fwd)
        for i in range(n_parts):
            copy(i, 0, sib_idx, here).wait_recv()
        for q, (_, chip_idx) in enumerate(chips, start=1):
            for i in range(n_parts):
                copy(i, N_CHIPS - 1 + q, chip_idx ^ 1, here).wait_recv()
        for cp in sends:
            cp.wait_send()
        for cp in local:
            cp.wait()

    hbm = pl.BlockSpec(memory_space=pl.ANY)
    return pl.pallas_call(
        body, name=name, in_specs=[hbm] * n_parts, out_specs=[hbm] * n_parts,
        out_shape=[jax.ShapeDtypeStruct((N_DEV,) + p_.shape, p_.dtype) for p_ in parts],
        scratch_shapes=[pltpu.SemaphoreType.DMA((n_parts * (N_DEV - 1),)),
                        pltpu.SemaphoreType.DMA((n_parts * (N_DEV - 1),)),
                        pltpu.SemaphoreType.DMA((n_parts,))],
        compiler_params=pltpu.CompilerParams(has_side_effects=True),
    )(*parts)


HBM_SPEC = pl.BlockSpec(memory_space=pltpu.HBM)
SEM_SPEC = pl.BlockSpec(memory_space=pltpu.SEMAPHORE)
DATAFLOW = pltpu.SideEffectType.DATAFLOW_SIDE_EFFECTING


def _gather_start(block, after, name):
    per_peer = block.ndim == 3

    def body(v_ref, land_ref, after_ref, send_sems, recv_sems, v_thru, land_thru, token):
        me = _my_index()
        for k in range(1, N_DEV):
            peer, pidx = _peer(k)
            pltpu.make_async_remote_copy(
                src_ref=v_ref.at[pidx] if per_peer else v_ref, dst_ref=land_ref.at[me],
                send_sem=send_sems.at[k - 1], recv_sem=recv_sems.at[k - 1],
                device_id=peer, device_id_type=MESH).start()
        token[...] = jnp.zeros_like(token)

    land_shape = (N_DEV,) + block.shape[-2:]
    return pl.pallas_call(
        body, name=name,
        out_shape=(pltpu.SemaphoreType.DMA((N_DEV - 1,)), pltpu.SemaphoreType.DMA((N_DEV - 1,)),
                   pltpu.HBM(block.shape, block.dtype), pltpu.HBM(land_shape, block.dtype),
                   jax.ShapeDtypeStruct((8, LANES), F32)),
        in_specs=(HBM_SPEC, HBM_SPEC, pl.BlockSpec(memory_space=pl.ANY)),
        out_specs=(SEM_SPEC, SEM_SPEC, HBM_SPEC, HBM_SPEC, pl.BlockSpec(memory_space=pltpu.VMEM)),
        input_output_aliases={0: 2, 1: 3},
        compiler_params=pltpu.CompilerParams(has_side_effects=DATAFLOW),
    )(pltpu.with_memory_space_constraint(block, pltpu.HBM),
      pltpu.with_memory_space_constraint(lax.empty(land_shape, block.dtype), pltpu.HBM), after)


def _gather_wait(send_sems, recv_sems, block, landing, after, name):
    per_peer = block.ndim == 3

    def body(v_ref, land_ref, send_sems, recv_sems, after_ref, v_dead, got_ref):
        for k in range(1, N_DEV):
            peer, pidx = _peer(k)
            copy = pltpu.make_async_remote_copy(
                src_ref=v_ref.at[pidx] if per_peer else v_ref, dst_ref=land_ref.at[pidx],
                send_sem=send_sems.at[k - 1], recv_sem=recv_sems.at[k - 1],
                device_id=peer, device_id_type=MESH)
            copy.wait_send()
            copy.wait_recv()

    return pl.pallas_call(
        body, name=name,
        out_shape=(pltpu.HBM(block.shape, block.dtype), pltpu.HBM(landing.shape, landing.dtype)),
        in_specs=(HBM_SPEC, HBM_SPEC, SEM_SPEC, SEM_SPEC, pl.BlockSpec(memory_space=pl.ANY)),
        out_specs=(HBM_SPEC, HBM_SPEC), input_output_aliases={0: 0, 1: 1},
        compiler_params=pltpu.CompilerParams(has_side_effects=DATAFLOW),
    )(block, landing, send_sems, recv_sems, after)


TILE_ELEMS = 1024 * 1024


def _shared_exchange(shared, name):
    def body(sh_ref, gsh_ref, send_sems, recv_sems, local_sem):
        me = _my_index()
        local = pltpu.make_async_copy(sh_ref, gsh_ref.at[me], local_sem)
        local.start()
        sends = []
        for k in range(1, N_DEV):
            peer, _ = _peer(k)
            cp = pltpu.make_async_remote_copy(
                src_ref=sh_ref, dst_ref=gsh_ref.at[me], send_sem=send_sems.at[k - 1],
                recv_sem=recv_sems.at[k - 1], device_id=peer, device_id_type=MESH)
            cp.start()
            sends.append(cp)
        for k in range(1, N_DEV):
            peer, pidx = _peer(k)
            pltpu.make_async_remote_copy(
                src_ref=sh_ref, dst_ref=gsh_ref.at[pidx], send_sem=send_sems.at[k - 1],
                recv_sem=recv_sems.at[k - 1], device_id=peer, device_id_type=MESH).wait_recv()
        for cp in sends:
            cp.wait_send()
        local.wait()

    hbm = pl.BlockSpec(memory_space=pl.ANY)
    return pl.pallas_call(
        body, name=name, in_specs=[hbm], out_specs=hbm,
        out_shape=jax.ShapeDtypeStruct((N_DEV,) + shared.shape, shared.dtype),
        scratch_shapes=[pltpu.SemaphoreType.DMA((N_DEV - 1,)), pltpu.SemaphoreType.DMA((N_DEV - 1,)),
                        pltpu.SemaphoreType.DMA],
        compiler_params=pltpu.CompilerParams(has_side_effects=True),
    )(shared)


def _adamw(parts, w, m, v, name, row0=0, own=None):
    n_parts, rows, lanes = parts.shape
    tr = rows if rows * lanes <= TILE_ELEMS // 2 else _tile_rows(math.gcd(rows, row0), TILE_ELEMS // 4 // lanes, 8)
    c1 = 1.0 - ADAM_B1 ** ADAM_STEP
    c2 = 1.0 - ADAM_B2 ** ADAM_STEP

    def body(*refs):
        if own is None:
            p_ref, w_ref, m_ref, v_ref, g_ref, d_ref, nm_ref, nv_ref = refs
            terms = [p_ref[j].astype(F32) for j in range(n_parts)]
        else:
            me_ref, p_ref, own_ref, w_ref, m_ref, v_ref, g_ref, d_ref, nm_ref, nv_ref = refs
            terms = [jnp.where(me_ref[0] == j, own_ref[...], p_ref[j]).astype(F32) for j in range(n_parts)]
        g = terms[0]
        for term in terms[1:]:
            g = g + term
        nm = ADAM_B1 * m_ref[...] + (1.0 - ADAM_B1) * g
        nv = ADAM_B2 * v_ref[...] + (1.0 - ADAM_B2) * (g * g)
        g_ref[...] = g
        nm_ref[...] = nm
        nv_ref[...] = nv
        d_ref[...] = -ADAM_LR * ((nm / c1) / (jnp.sqrt(nv / c2) + ADAM_EPS) + ADAM_WD * w_ref[...])

    row = pl.BlockSpec((tr, lanes), lambda i, *_: (i, 0))
    state = pl.BlockSpec((tr, lanes), lambda i, *_: (row0 // tr + i, 0))
    in_specs = [pl.BlockSpec((n_parts, tr, lanes), lambda i, *_: (0, i, 0)), state, state, state]
    args, n_prefetch = (parts, w, m, v), 0
    if own is not None:
        slabs, me = own
        in_specs.insert(1, pl.BlockSpec((None, tr, lanes), lambda i, me_ref: (me_ref[0], i, 0)))
        args, n_prefetch = (me, parts, slabs, w, m, v), 1
    return pl.pallas_call(
        body, name=name,
        grid_spec=pltpu.PrefetchScalarGridSpec(num_scalar_prefetch=n_prefetch, grid=(rows // tr,),
                                               in_specs=in_specs, out_specs=[row] * 4),
        out_shape=[jax.ShapeDtypeStruct((rows, lanes), F32)] * 4,
        compiler_params=_params("parallel"),
    )(*args)


MATRIX_SHARDS = (
    ("w_in", (D_MODEL, IN_PROJ_DIM // N_DEV), True),
    ("w_ssm_out", (SSM_D_INNER // N_DEV, D_MODEL), False),
    ("w_att_out", (ATT_OUT_DIM, D_MODEL // N_DEV), True),
    ("w_mix_out", (D_MODEL // N_DEV, D_MODEL), False),
    ("w_ffn_gate", (D_MODEL, D_FF // N_DEV), True),
    ("w_ffn_up", (D_MODEL, D_FF // N_DEV), True),
    ("w_ffn_down", (D_FF // N_DEV, D_MODEL), False),
)
CONV_SHARD = ("conv_w", (SSM_CONV, SSM_CONV_DIM // N_DEV), True)
SHARDED = MATRIX_SHARDS + (CONV_SHARD,)
REPLICATED = (("norm_mix", D_MODEL), ("b_gate", 2 * D_MODEL), ("conv_b", SSM_CONV_DIM), ("dt_bias", SSM_N_HEADS),
              ("a_log", SSM_N_HEADS), ("d_skip", SSM_N_HEADS), ("ssm_norm", SSM_D_INNER), ("norm_ffn", D_MODEL),
              ("norm_final", D_MODEL))


def _round_up(n, mult):
    return -(-n // mult) * mult


def _pack_rows(flat, row_mult):
    rows = _round_up(-(-flat.shape[0] // LANES), row_mult)
    return jnp.pad(flat, (0, rows * LANES - flat.shape[0])).reshape(rows, LANES)


def _stacking(specs):
    return tuple((name, (shape[1], shape[0]) if by_cols else shape, by_cols) for name, shape, by_cols in specs)


def _to_stacking(vals, specs):
    return {name: (vals[name].T if by_cols else vals[name]) for name, _, by_cols in specs}


STACK_WIDTH = D_MODEL
STACK_ALIGN = 16
STACK_ORDER = ("w_ssm_out", "w_mix_out", "w_ffn_gate", "w_ffn_up", "w_ffn_down", "w_att_out", "conv_w", "w_in")
GATHER_LATER = STACK_ORDER[:-1]
REDUCE_EARLY = STACK_ORDER[:5]
REDUCE_LATE = STACK_ORDER[5:]


def _stack_layout():
    shapes = {name: shape for name, shape, _ in _stacking(SHARDED)}
    layout, off = {}, 0
    for name in STACK_ORDER:
        r, c = shapes[name]
        rows = r if c == STACK_WIDTH else _round_up(-(-(r * c) // STACK_WIDTH), STACK_ALIGN)
        layout[name] = (off, rows, (r, c))
        off = _round_up(off + rows, STACK_ALIGN)
    return layout, _round_up(off, 1024)


def _to_stack_rows(v, rows):
    if v.shape[-1] == STACK_WIDTH:
        return v
    lead = v.shape[:-2]
    flat = v.reshape(lead + (-1,))
    flat = jnp.pad(flat, [(0, 0)] * len(lead) + [(0, rows * STACK_WIDTH - flat.shape[-1])])
    return flat.reshape(lead + (rows, STACK_WIDTH))


def _from_stack_rows(block, shape):
    r, c = shape
    if c == STACK_WIDTH:
        return block
    lead = block.shape[:-2]
    return block.reshape(lead + (-1,))[..., :r * c].reshape(lead + (r, c))


def _stack(vals, dtype, skip=(), names=STACK_ORDER):
    layout, total = _stack_layout()
    order = names
    after = STACK_ORDER.index(order[-1]) + 1
    if after < len(STACK_ORDER):
        total = layout[STACK_ORDER[after]][0]
    lead = next(iter(vals.values())).shape[:-2]
    pieces = []
    for i, name in enumerate(order):
        off, rows, _ = layout[name]
        until = layout[order[i + 1]][0] if i + 1 < len(order) else total
        piece = jnp.zeros(lead + (rows, STACK_WIDTH), dtype) if name in skip else _to_stack_rows(vals[name], rows)
        pieces.append(jnp.pad(piece.astype(dtype), [(0, 0)] * len(lead) + [(0, until - off - rows), (0, 0)]))
    return jnp.concatenate(pieces, axis=-2)


def _unstack(stacked, names):
    layout, _ = _stack_layout()
    row0 = layout[names[0]][0]
    return {name: _from_stack_rows(stacked[..., layout[name][0] - row0:layout[name][0] - row0 + layout[name][1], :],
                                   layout[name][2]) for name in names}


W_IN_SHARD_ROWS = IN_PROJ_DIM // N_DEV


def _w_in_row_moves():
    moves, orig = [], 0
    for name, size in IN_SPLIT:
        for j in range(N_DEV):
            lo, hi = max(orig, W_IN_SHARD_ROWS * j), min(orig + size, W_IN_SHARD_ROWS * (j + 1))
            if lo < hi:
                moves.append((j, lo - W_IN_SHARD_ROWS * j, DPROJ_COLS[name] + lo - orig, hi - lo))
        orig += size
    return moves


def _w_in_from_shards(shards, name):
    total, base = shards.shape[1], 0
    pad_lo, pad_hi = DPROJ_COLS["dt"] + _round_up(SSM_N_HEADS, STACK_ALIGN), DPROJ_COLS["dt"] + DPROJ_DT_WIDTH

    def body(x_ref, o_ref):
        o_ref[pad_lo:pad_hi, :] = jnp.zeros((pad_hi - pad_lo, LANES), x_ref.dtype)
        for j, r, at, n in _w_in_row_moves():
            o_ref[at:at + n, :] = x_ref[j, base + r:base + r + n, :]

    return pl.pallas_call(
        body, name=name, grid=(STACK_WIDTH // LANES,),
        in_specs=[pl.BlockSpec((N_DEV, total, LANES), lambda c: (0, 0, c))],
        out_specs=pl.BlockSpec((DPROJ_WIDTH, LANES), lambda c: (0, c)),
        out_shape=jax.ShapeDtypeStruct((DPROJ_WIDTH, STACK_WIDTH), shards.dtype),
        compiler_params=_params("parallel"),
    )(shards)


def _w_in_to_shards(dw_all, head, name):
    layout, total = _stack_layout()
    total -= layout[REDUCE_LATE[0]][0]
    base = head.shape[1]
    end = base + W_IN_SHARD_ROWS

    def body(x_ref, h_ref, o_ref):
        o_ref[:, 0:base, :] = h_ref[...]
        for j, r, at, n in _w_in_row_moves():
            o_ref[j, base + r:base + r + n, :] = x_ref[at:at + n, :]
        o_ref[:, end:total, :] = jnp.zeros((N_DEV, total - end, LANES), o_ref.dtype)

    return pl.pallas_call(
        body, name=name, grid=(STACK_WIDTH // LANES,),
        in_specs=[pl.BlockSpec((DPROJ_WIDTH, LANES), lambda c: (0, c)),
                  pl.BlockSpec((N_DEV, base, LANES), lambda c: (0, 0, c))],
        out_specs=pl.BlockSpec((N_DEV, total, LANES), lambda c: (0, 0, c)),
        out_shape=jax.ShapeDtypeStruct((N_DEV, total, STACK_WIDTH), dw_all.dtype),
        compiler_params=_params("parallel"),
    )(dw_all, head)


REPLICATED_ROWS = sum(-(-size // LANES) for _, size in REPLICATED)
LOSS_ROW = REPLICATED_ROWS


def _pack_replicated(vals):
    rows = []
    for name, size in REPLICATED:
        v = vals[name].reshape(-1).astype(F32)
        rows.append(jnp.pad(v, (0, _round_up(size, LANES) - size)))
    return _pack_rows(jnp.concatenate(rows), 8)


def _unpack_replicated(packed, shapes):
    flat = packed.reshape(-1)
    out, off = {}, 0
    for name, size in REPLICATED:
        out[name] = flat[off:off + size].reshape(shapes[name])
        off += _round_up(size, LANES)
    return out


def _lane_row(v):
    v = v.reshape(-1).astype(F32)
    return jnp.pad(v, (0, LANES - v.shape[0])).reshape(1, LANES)


IN_SPLIT = (("z", SSM_D_INNER), ("xbc", SSM_CONV_DIM), ("dt", SSM_N_HEADS), ("qkv", ATT_QKV_DIM), ("gate", 2 * D_MODEL))


def kernel(x, norm_mix, w_in, b_gate, conv_w, conv_b, dt_bias, a_log, d_skip, ssm_norm, w_ssm_out, w_att_out, w_mix_out, norm_ffn, w_ffn_gate, w_ffn_up, w_ffn_down, norm_final, loss_target, m_norm_mix, m_w_in, m_b_gate, m_conv_w, m_conv_b, m_dt_bias, m_a_log, m_d_skip, m_ssm_norm, m_w_ssm_out, m_w_att_out, m_w_mix_out, m_norm_ffn, m_w_ffn_gate, m_w_ffn_up, m_w_ffn_down, m_norm_final, v_norm_mix, v_w_in, v_b_gate, v_conv_w, v_conv_b, v_dt_bias, v_a_log, v_d_skip, v_ssm_norm, v_w_ssm_out, v_w_att_out, v_w_mix_out, v_norm_ffn, v_w_ffn_gate, v_w_ffn_up, v_w_ffn_down, v_norm_final):
    given = dict(locals())
    weights = {name: given[name][0] for name, _, _ in SHARDED}
    b, s, d = x.shape
    t = b * s

    stacking = _to_stacking(weights, SHARDED)
    conv_shape = dict((name, shape) for name, shape, _ in _stacking(SHARDED))["conv_w"]
    w_in_local = jnp.pad(stacking["w_in"].astype(BF16), ((0, -W_IN_SHARD_ROWS % STACK_ALIGN), (0, 0)))
    conv_local = _pack_rows(stacking["conv_w"].reshape(-1), 8)
    w_in_shards, conv_all = _all_gather([w_in_local, conv_local], "w_in_all_gather")
    head_local = _stack(stacking, BF16, skip=("conv_w",), names=GATHER_LATER)
    in_flight = _gather_start(head_local, conv_all, "weights_gather_start")
    w_in_all = _w_in_from_shards(w_in_shards, "w_in_from_shards")
    w_sec = {name: w_in_all[DPROJ_COLS[name]:DPROJ_COLS[name] + _round_up(size, LANES)] for name, size in IN_SPLIT}
    conv_size = conv_shape[0] * conv_shape[1]
    conv_taps = conv_all.reshape(N_DEV, -1)[:, :conv_size].reshape(N_DEV * conv_shape[0], conv_shape[1]).T

    g_mix, g_ffn, g_fin = norm_mix.reshape(1, d), norm_ffn.reshape(1, d), norm_final.reshape(1, d)
    g_mix = g_mix + in_flight[4][:1, :1]
    bg_row = b_gate.reshape(1, 2 * d)
    convb_row = conv_b.reshape(1, SSM_CONV_DIM)
    ssmn_row = ssm_norm.reshape(1, SSM_D_INNER)
    dtb_row, alog_row = _lane_row(dt_bias), _lane_row(a_log)
    cosf, sinf = _rope_tables(s)

    x2d = x.reshape(t, d)
    h1 = _rmsnorm_fwd(x2d, g_mix, "norm_mix_fwd")
    proj = {name: _mm(h1, w_sec[name], mode="nt", name="in_proj_" + name) for name, _ in IN_SPLIT if name != "qkv"}
    xbc3 = proj["xbc"].reshape(b, s, SSM_CONV_DIM)
    xc = _conv_fwd(xbc3, conv_taps, convb_row, "conv_fwd")
    dtr3 = proj["dt"].reshape(b, s, DT_PAD)
    to_channels, to_heads = _head_masks()
    dskx = jnp.repeat(d_skip.reshape(-1).astype(F32), SSM_HEAD_DIM).reshape(1, SSM_D_INNER)
    y_ssd, h_states = _ssd_fwd(xc, dtr3, dtb_row, alog_row, dskx, to_channels, "ssd_fwd")
    y_ssd2 = y_ssd.reshape(t, SSM_D_INNER)
    ynorm = _gate_norm_fwd(y_ssd2, proj["z"], ssmn_row, "ssd_gate_norm_fwd")
    head_local, landed = _gather_wait(*in_flight[:4], ynorm, "weights_gather_wait")
    head_all = lax.dynamic_update_slice(landed, head_local[None], (_my_index(), 0, 0))
    full = {name: v.reshape((-1,) + v.shape[2:]) for name, v in _unstack(head_all, STACK_ORDER[:-2]).items()}
    y_ssm = _mm(ynorm, full["w_ssm_out"], mode="nn", name="ssm_out_proj")

    qk_parts = _qkv_proj_rope(h1, w_sec["qkv"], cosf, sinf, b, s, "in_proj_qkv_rope")
    att_parts = [_att_fwd(qk_parts[gi], "att_fwd_%d" % r) for gi, r in enumerate(ATT_DILATIONS)]
    att, *lse_parts = _att_merge([o for o, _ in att_parts], [l_ for _, l_ in att_parts], "att_merge")
    att2 = att.reshape(t, ATT_OUT_DIM)
    y_att, mixed = _att_out_proj_mix(att2, full["w_att_out"], proj["gate"], bg_row, y_ssm, "att_out_proj_mix")
    x2, h2 = _proj_residual_norm(mixed, full["w_mix_out"], x2d, g_ffn, "mix_out_proj_norm")
    gt, up, act = _gate_up_proj_swiglu(h2, full["w_ffn_gate"], full["w_ffn_up"], "ffn_gate_up_proj_swiglu")

    loss_row, dx3, dg_fin, dx3b = _down_proj_loss_head(act, full["w_ffn_down"], x2, g_fin, loss_target.reshape(t, d),
                                                       "ffn_down_proj_loss_head")
    grads = {}
    grads["w_ffn_down"] = _mm(act, dx3b, mode="tn", name="ffn_down_dw", out_dtype=BF16)
    dgt, dup = _down_dx_swiglu_bwd(dx3b, full["w_ffn_down"], gt, up, "ffn_down_dx_swiglu_bwd")
    grads["w_ffn_gate"] = _mm(dgt, h2, mode="tn", name="ffn_gate_dw", out_dtype=BF16)
    grads["w_ffn_up"] = _mm(dup, h2, mode="tn", name="ffn_up_dw", out_dtype=BF16)
    dh2 = _mm(dgt, full["w_ffn_gate"], mode="nn", name="ffn_gate_dx")
    dx2, dg_ffn, dx2b = _proj_norm_bwd(dup, full["w_ffn_up"], x2, g_ffn, dx3, "ffn_up_dx_norm_bwd", add=dh2,
                                       with_bf16=True)

    grads["w_mix_out"] = _mm(mixed, dx2b, mode="tn", name="mix_out_dw", out_dtype=BF16)
    dys, dya, dproj, dbg = _mix_out_dx_mix_bwd(dx2b, full["w_mix_out"], proj["gate"], bg_row, y_ssm, y_att,
                                               "mix_out_dx_mix_bwd")

    grads["w_ssm_out"] = _mm(ynorm, dys, mode="tn", name="ssm_out_dw", out_dtype=BF16)
    early = _stack({name: grads[name].reshape((N_DEV, -1, STACK_WIDTH)) for name in REDUCE_EARLY}, BF16,
                   names=REDUCE_EARLY)
    early_flight = _gather_start(early, dys, "grads_scatter_start")
    ssmn_row = ssmn_row + early_flight[4][:1, :1]
    dy_ssd, dproj, dssmn = _ssm_out_dx_gate_norm_bwd(dys, full["w_ssm_out"], y_ssd2, proj["z"], ssmn_row, dproj,
                                                     "ssm_out_dx_gate_norm_bwd")
    dxc, dproj, dalog, ddsk, ddtb = _ssd_bwd(xc, dtr3, dy_ssd.reshape(b, s, SSM_D_INNER), h_states, dtb_row, alog_row,
                                             dskx, to_channels, to_heads, dproj.reshape(b, s, DPROJ_WIDTH), "ssd_bwd")
    dproj, dconvw, dconvb = _conv_bwd(xbc3, dxc, conv_taps, convb_row, dproj, "conv_bwd")
    grads["conv_w"] = dconvw.T.astype(BF16)

    grads["w_att_out"] = _mm(dya, att2, mode="tn", name="att_out_dw", out_dtype=BF16)
    datt = _mm(dya, full["w_att_out"], mode="nn", name="att_out_dx").reshape(b, s, ATT_OUT_DIM)
    do_parts, dl_parts = _att_delta(att, datt, "att_delta")
    dqs, dks, dvs = [], [], []
    for gi, r in enumerate(ATT_DILATIONS):
        operands = (qk_parts[gi], do_parts[gi], lse_parts[gi], dl_parts[gi])
        dqs.append(_att_bwd_q(*operands, "att_bwd_q_%d" % r))
        dk_g, dv_g = _att_bwd_kv(*operands, "att_bwd_kv_%d" % r)
        dks.append(dk_g)
        dvs.append(dv_g)
    dproj = _rope_bwd(dqs, dks, dvs, cosf, sinf, dproj, "rope_bwd").reshape(t, DPROJ_WIDTH)

    dw_all = _mm(dproj, h1, mode="tn", name="in_proj_dw", out_dtype=BF16)
    head = _stack({name: grads[name].reshape((N_DEV, -1, grads[name].shape[-1])) for name in REDUCE_LATE[:-1]}, BF16,
                  names=REDUCE_LATE[:-1])
    late = _w_in_to_shards(dw_all, head, "grad_stacks")
    late_flight = _gather_start(late, dw_all, "grads_late_scatter_start")
    dh1 = _mm(dproj, w_in_all, mode="nn", name="in_proj_dx", after=late_flight[4])
    grad_x, dg_mix = _rmsnorm_bwd(x2d, g_mix, dh1, dx2, "norm_mix_bwd")

    small = {"norm_mix": dg_mix, "b_gate": dbg, "conv_b": dconvb, "dt_bias": ddtb[:, :SSM_N_HEADS],
             "a_log": dalog[:, :SSM_N_HEADS], "d_skip": ddsk[:, :SSM_N_HEADS], "ssm_norm": dssmn,
             "norm_ffn": dg_ffn, "norm_final": dg_fin}
    shared = _pack_replicated(small)
    shared = shared.at[LOSS_ROW, 0].set(loss_row[0, 0])
    got_small = _shared_exchange(shared, "shared_grads_exchange")

    def packed(prefix):
        vals = _to_stacking({name: given[prefix + name][0] for name, _, _ in SHARDED}, SHARDED)
        rep = {name: given[prefix + name] for name, _ in REPLICATED}
        return _stack(vals, F32), _pack_replicated(rep)

    (w_big, w_small), (m_big, m_small), (v_big, v_small) = packed(""), packed("m_"), packed("v_")
    me = _my_index().astype(jnp.int32).reshape(1)
    early, early_landed = _gather_wait(*early_flight[:4], got_small, "grads_scatter_wait")
    late, late_landed = _gather_wait(*late_flight[:4], got_small, "grads_late_scatter_wait")
    big_early = _adamw(early_landed, w_big, m_big, v_big, "adamw_early", own=(early, me))
    big_late = _adamw(late_landed, w_big, m_big, v_big, "adamw_late", row0=early.shape[1], own=(late, me))
    sml = _adamw(got_small, w_small, m_small, v_small, "adamw_replicated")

    outs = [sml[0][LOSS_ROW, 0], grad_x.reshape(b, s, d)]
    rep_shapes = {name: given[name].shape for name, _ in REPLICATED}
    order = ["norm_mix", "w_in", "b_gate", "conv_w", "conv_b", "dt_bias", "a_log", "d_skip", "ssm_norm", "w_ssm_out",
             "w_att_out", "w_mix_out", "norm_ffn", "w_ffn_gate", "w_ffn_up", "w_ffn_down", "norm_final"]
    for early_k, late_k, sml_k in zip(big_early, big_late, sml):
        stacks = dict(_unstack(early_k, REDUCE_EARLY), **_unstack(late_k, REDUCE_LATE))
        sharded = _to_stacking(stacks, SHARDED)
        rep = _unpack_replicated(sml_k, rep_shapes)
        for name in order:
            outs.append(sharded[name][None] if name in sharded else rep[name])
    return tuple(outs)
```

```python
import functools
import math

import jax
import jax.numpy as jnp
from jax import lax
from jax.experimental import pallas as pl
from jax.experimental.pallas import tpu as pltpu

F32 = jnp.float32
BF16 = jnp.bfloat16

N_DEV = 8
N_CHIPS = 4
D_MODEL = 1024
SSM_D_INNER = 2048
SSM_HEAD_DIM = 64
SSM_N_HEADS = 32
SSM_N_GROUPS = 4
SSM_HEADS_PER_GROUP = SSM_N_HEADS // SSM_N_GROUPS
SSM_D_STATE = 128
SSM_CONV = 4
SSM_CHUNK = 128
SSM_CONV_DIM = 3072
ATT_HEAD_DIM = 128
ATT_HEADS_PER_GROUP = 4
ATT_DILATIONS = (1, 4, 16)
ATT_N_HEADS = 12
ATT_QKV_DIM = 4608
ATT_OUT_DIM = 512
ATT_BLOCK = 128
ROPE_THETA = 10000.0
D_FF = 2816
IN_PROJ_DIM = 11808
EPS = 1e-6
LANES = 128
DT_PAD = LANES

DPROJ_COLS = {"qkv": 0, "xbc": 4608, "dt": 7680, "z": 8192, "gate": 10240}
DPROJ_DT_WIDTH = 512
DPROJ_WIDTH = 12288

ADAM_LR = 0.001
ADAM_B1 = 0.9
ADAM_B2 = 0.999
ADAM_EPS = 1e-08
ADAM_WD = 0.01
ADAM_STEP = 10

VMEM_LIMIT = 56 * 1024 * 1024
MESH = pl.DeviceIdType.MESH
NEG_INF = float("-inf")


def _tile_rows(n, cap, mult):
    return max(t for t in range(mult, min(n, cap) + 1, mult) if n % t == 0)


def _pick(n, candidates):
    for c in candidates:
        if n % c == 0:
            return c
    return n


def _params(*sem):
    return pltpu.CompilerParams(dimension_semantics=sem, vmem_limit_bytes=VMEM_LIMIT)


def _sigmoid(x):
    return 0.5 * jnp.tanh(0.5 * x) + 0.5


def _softplus(x):
    return jnp.maximum(x, 0.0) + jnp.log(1.0 + jnp.exp(-jnp.abs(x)))


def _dot(a, b, dims):
    return lax.dot_general(a.astype(BF16), b.astype(BF16), (dims, ((), ())), preferred_element_type=F32)


def _nn(a, b):
    return _dot(a, b, ((1,), (0,)))


def _nt(a, b):
    return _dot(a, b, ((1,), (1,)))


def _tn(a, b):
    return _dot(a, b, ((0,), (0,)))


def _split3(v):
    hi = v.astype(BF16)
    r1 = v - hi.astype(F32)
    mid = r1.astype(BF16)
    lo = (r1 - mid.astype(F32)).astype(BF16)
    return hi, mid, lo


def _mask_nn(mask, v):
    mb = mask.astype(BF16)
    hi, mid, lo = _split3(v)
    return _nn(mb, hi) + (_nn(mb, mid) + _nn(mb, lo))


MM_VMEM_BUDGET = 40 * 1024 * 1024
MM_FULL_K = 2816


def _mm_tiles(m, n, k, a_bytes, b_bytes, o_bytes, has_add):
    tk = k if k <= MM_FULL_K else _pick(k, (2048, 1024, 512, 256, 128))
    tn = 1408 if (n > 1024 and n % 1408 == 0) else _pick(n, (1024, 768, 512, 384, 256, 128))
    for tm in (1408, 1024, 768, 512, 384, 256, 128):
        if m % tm:
            continue
        buffers = 2 * (tm * tk * a_bytes + tk * tn * b_bytes + tm * tn * (o_bytes + (4 if has_add else 0)))
        if tk < k:
            buffers += tm * tn * 4
        if buffers <= MM_VMEM_BUDGET:
            return tm, tn, tk
    return _pick(m, (128,)), tn, tk


def _mm(a, b, *, mode, name, out_dtype=F32, add=None, after=None):
    if mode == "nn":
        (m, k), n = a.shape, b.shape[1]
    elif mode == "nt":
        (m, k), n = a.shape, b.shape[0]
    else:
        (k, m), n = a.shape, b.shape[1]
    has_add = add is not None
    tm, tn, tk = _mm_tiles(m, n, k, a.dtype.itemsize, b.dtype.itemsize, jnp.dtype(out_dtype).itemsize, has_add)
    nk = k // tk
    dims = {"nn": ((1,), (0,)), "nt": ((1,), (1,)), "tn": ((0,), (0,))}[mode]
    a_spec = {"nn": pl.BlockSpec((tm, tk), lambda i, j, kk: (i, kk)),
              "nt": pl.BlockSpec((tm, tk), lambda i, j, kk: (i, kk)),
              "tn": pl.BlockSpec((tk, tm), lambda i, j, kk: (kk, i))}[mode]
    b_spec = {"nn": pl.BlockSpec((tk, tn), lambda i, j, kk: (kk, j)),
              "nt": pl.BlockSpec((tn, tk), lambda i, j, kk: (j, kk)),
              "tn": pl.BlockSpec((tk, tn), lambda i, j, kk: (kk, j))}[mode]
    o_spec = pl.BlockSpec((tm, tn), lambda i, j, kk: (i, j))

    def finish(r, c_ref, o_ref):
        if has_add:
            r = r + c_ref[...]
        o_ref[...] = r.astype(out_dtype)

    def body_one(*refs):
        a_ref, b_ref = refs[:2]
        finish(_dot(a_ref[...], b_ref[...], dims), refs[2] if has_add else None, refs[-1])

    def body_acc(*refs):
        a_ref, b_ref = refs[:2]
        o_ref, acc = refs[-2:]
        kk = pl.program_id(2)

        @pl.when(kk == 0)
        def _():
            acc[...] = jnp.zeros_like(acc)

        acc[...] += _dot(a_ref[...], b_ref[...], dims)

        @pl.when(kk == nk - 1)
        def _():
            finish(acc[...], refs[2] if has_add else None, o_ref)

    in_specs = [a_spec, b_spec] + ([o_spec] if has_add else [])
    args = (a, b) + ((add,) if has_add else ())
    if after is not None:
        in_specs, args = in_specs + [pl.BlockSpec(memory_space=pl.ANY)], args + (after,)
    return pl.pallas_call(
        body_one if nk == 1 else body_acc, name=name, grid=(m // tm, n // tn, nk),
        in_specs=in_specs, out_specs=o_spec,
        out_shape=jax.ShapeDtypeStruct((m, n), out_dtype),
        scratch_shapes=[] if nk == 1 else [pltpu.VMEM((tm, tn), F32)],
        compiler_params=_params("parallel", "parallel", "arbitrary"),
    )(*args)


def _rmsnorm_fwd(x, g, name):
    t, d = x.shape
    tm = _pick(t, (512, 256, 128))

    def body(x_ref, g_ref, o_ref):
        xv = x_ref[...]
        r = lax.rsqrt(jnp.mean(xv * xv, axis=-1, keepdims=True) + EPS)
        o_ref[...] = ((xv * r) * g_ref[...]).astype(BF16)

    return pl.pallas_call(
        body, name=name, grid=(t // tm,),
        in_specs=[pl.BlockSpec((tm, d), lambda i: (i, 0)), pl.BlockSpec((1, d), lambda i: (0, 0))],
        out_specs=pl.BlockSpec((tm, d), lambda i: (i, 0)),
        out_shape=jax.ShapeDtypeStruct((t, d), BF16),
        compiler_params=_params("parallel"),
    )(x, g)


def _proj_residual_norm(a, w, res, g, name):
    t, k = a.shape
    d = w.shape[1]
    tm, _, _ = _mm_tiles(t, d, k, a.dtype.itemsize, w.dtype.itemsize, 4 + 2, True)

    def body(a_ref, w_ref, r_ref, g_ref, x_ref, h_ref):
        xv = r_ref[...] + _nn(a_ref[...], w_ref[...])
        x_ref[...] = xv
        r = lax.rsqrt(jnp.mean(xv * xv, axis=-1, keepdims=True) + EPS)
        h_ref[...] = ((xv * r) * g_ref[...]).astype(BF16)

    row = pl.BlockSpec((tm, d), lambda i: (i, 0))
    return pl.pallas_call(
        body, name=name, grid=(t // tm,),
        in_specs=[pl.BlockSpec((tm, k), lambda i: (i, 0)), pl.BlockSpec((k, d), lambda i: (0, 0)), row,
                  pl.BlockSpec((1, d), lambda i: (0, 0))],
        out_specs=[row, row],
        out_shape=[jax.ShapeDtypeStruct((t, d), F32), jax.ShapeDtypeStruct((t, d), BF16)],
        compiler_params=_params("parallel"),
    )(a, w, res, g)


def _proj_norm_bwd(a, w, x, g, dres, name, second=None, with_bf16=False, after=None):
    t, k = a.shape
    d = w.shape[1]
    has_second = second is not None
    pairs = 1 + has_second
    tm, _, tk = _mm_tiles(t, d, k, pairs * a.dtype.itemsize, pairs * w.dtype.itemsize,
                          4 + 4 + 4 + (2 if with_bf16 else 0), False)
    assert not has_second or tk == k
    if tk == k:
        tm = min(tm, 512)
    else:
        tm, tk = _pick(t, (1024, 512, 256, 128)), min(tk, 1024)
    nk = k // tk

    def body(*refs):
        a_ref, w_ref, x_ref, g_ref, dres_ref = refs[:5]
        rest = refs[5 + 2 * has_second + (after is not None):]
        dx_ref, dg_ref = rest[:2]
        i, kk = pl.program_id(0), pl.program_id(1)

        @pl.when(jnp.logical_and(i == 0, kk == 0))
        def _():
            dg_ref[...] = jnp.zeros_like(dg_ref)

        part = _nn(a_ref[...], w_ref[...])
        if has_second:
            part = part + _nn(refs[5][...], refs[6][...])
        if nk > 1:
            acc = rest[-1]

            @pl.when(kk == 0)
            def _():
                acc[...] = jnp.zeros_like(acc)

            acc[...] += part

        @pl.when(kk == nk - 1)
        def _():
            dhv = part if nk == 1 else acc[...]
            xv = x_ref[...]
            r = lax.rsqrt(jnp.mean(xv * xv, axis=-1, keepdims=True) + EPS)
            xhat = xv * r
            dyg = dhv * g_ref[...]
            dx = dres_ref[...] + r * (dyg - xhat * jnp.mean(dyg * xhat, axis=-1, keepdims=True))
            dx_ref[...] = dx
            if with_bf16:
                rest[2][...] = dx.astype(BF16)
            dg_ref[...] += jnp.sum(dhv * xhat, axis=0, keepdims=True)

    row = pl.BlockSpec((tm, d), lambda i, kk: (i, 0))
    vec = pl.BlockSpec((1, d), lambda i, kk: (0, 0))
    pair = [pl.BlockSpec((tm, tk), lambda i, kk: (i, kk)), pl.BlockSpec((tk, d), lambda i, kk: (kk, 0))]
    in_specs = pair + [row, vec, row] + has_second * pair
    args = (a, w, x, g, dres) + (tuple(second) if has_second else ())
    if after is not None:
        in_specs, args = in_specs + [pl.BlockSpec(memory_space=pl.ANY)], args + (after,)
    return pl.pallas_call(
        body, name=name, grid=(t // tm, nk), in_specs=in_specs, out_specs=[row, vec] + with_bf16 * [row],
        out_shape=[jax.ShapeDtypeStruct((t, d), F32), jax.ShapeDtypeStruct((1, d), F32)]
        + with_bf16 * [jax.ShapeDtypeStruct((t, d), BF16)],
        scratch_shapes=[] if nk == 1 else [pltpu.VMEM((tm, d), F32)],
        compiler_params=_params("arbitrary", "arbitrary"),
    )(*args)


def _rmsnorm_bwd(x, g, dh, dres, name):
    t, d = x.shape
    tm = _pick(t, (512, 256, 128))

    def body(x_ref, g_ref, dh_ref, dres_ref, dx_ref, dg_ref):
        @pl.when(pl.program_id(0) == 0)
        def _():
            dg_ref[...] = jnp.zeros_like(dg_ref)

        xv = x_ref[...]
        r = lax.rsqrt(jnp.mean(xv * xv, axis=-1, keepdims=True) + EPS)
        xhat = xv * r
        dhv = dh_ref[...]
        dyg = dhv * g_ref[...]
        dx_ref[...] = dres_ref[...] + r * (dyg - xhat * jnp.mean(dyg * xhat, axis=-1, keepdims=True))
        dg_ref[...] += jnp.sum(dhv * xhat, axis=0, keepdims=True)

    row = pl.BlockSpec((tm, d), lambda i: (i, 0))
    vec = pl.BlockSpec((1, d), lambda i: (0, 0))
    return pl.pallas_call(
        body, name=name, grid=(t // tm,),
        in_specs=[row, vec, row, row], out_specs=[row, vec],
        out_shape=[jax.ShapeDtypeStruct((t, d), F32), jax.ShapeDtypeStruct((1, d), F32)],
        compiler_params=_params("arbitrary"),
    )(x, g, dh, dres)


def _down_proj_loss_head(act, w_down, res, g, target, name):
    t, k = act.shape
    d = w_down.shape[1]
    tm, _, _ = _mm_tiles(t, d, k, act.dtype.itemsize, w_down.dtype.itemsize, 4 + 2, True)
    tm = min(tm, 512)

    def body(a_ref, w_ref, r_ref, g_ref, t_ref, loss_ref, dx_ref, dg_ref, dxb_ref):
        @pl.when(pl.program_id(0) == 0)
        def _():
            dg_ref[...] = jnp.zeros_like(dg_ref)
            loss_ref[...] = jnp.zeros_like(loss_ref)

        xv = r_ref[...] + _nn(a_ref[...], w_ref[...])
        gv = g_ref[...]
        r = lax.rsqrt(jnp.mean(xv * xv, axis=-1, keepdims=True) + EPS)
        xhat = xv * r
        err = xhat * gv - t_ref[...]
        loss_ref[...] += jnp.sum(err * err) * (0.5 / d)
        dy = err * (1.0 / d)
        dyg = dy * gv
        dx = r * (dyg - xhat * jnp.mean(dyg * xhat, axis=-1, keepdims=True))
        dx_ref[...] = dx
        dxb_ref[...] = dx.astype(BF16)
        dg_ref[...] += jnp.sum(dy * xhat, axis=0, keepdims=True)

    row = pl.BlockSpec((tm, d), lambda i: (i, 0))
    vec = pl.BlockSpec((1, d), lambda i: (0, 0))
    return pl.pallas_call(
        body, name=name, grid=(t // tm,),
        in_specs=[pl.BlockSpec((tm, k), lambda i: (i, 0)), pl.BlockSpec((k, d), lambda i: (0, 0)), row, vec, row],
        out_specs=[pl.BlockSpec((1, LANES), lambda i: (0, 0)), row, vec, row],
        out_shape=[jax.ShapeDtypeStruct((1, LANES), F32), jax.ShapeDtypeStruct((t, d), F32),
                   jax.ShapeDtypeStruct((1, d), F32), jax.ShapeDtypeStruct((t, d), BF16)],
        compiler_params=_params("arbitrary"),
    )(act, w_down, res, g, target)


CONV_HALO = 8
CONV_ROWS = 64


def _conv_taps(window, wv, bv):
    acc = bv + wv[SSM_CONV - 1:SSM_CONV, :] * window(0)
    for sh in range(1, SSM_CONV):
        kidx = SSM_CONV - 1 - sh
        acc = acc + wv[kidx:kidx + 1, :] * window(sh)
    return acc


def _conv_fwd(u, w, bias, name):
    b, s, c = u.shape
    rows = CONV_ROWS

    def body(u_ref, w_ref, b_ref, o_ref, ext):
        ext[0:CONV_HALO, :] = jnp.zeros((CONV_HALO, LANES), F32)
        ext[CONV_HALO:, :] = u_ref[...]
        wv, bv = w_ref[...], b_ref[...]
        for r0 in range(0, s, rows):
            acc = _conv_taps(lambda sh: ext[CONV_HALO + r0 - sh:CONV_HALO + r0 - sh + rows, :], wv, bv)
            o_ref[r0:r0 + rows, :] = acc * _sigmoid(acc)

    strip = pl.BlockSpec((None, s, LANES), lambda bi, j: (bi, 0, j))
    return pl.pallas_call(
        body, name=name, grid=(b, c // LANES),
        in_specs=[strip, pl.BlockSpec((SSM_CONV, LANES), lambda bi, j: (0, j)),
                  pl.BlockSpec((1, LANES), lambda bi, j: (0, j))],
        out_specs=strip, out_shape=jax.ShapeDtypeStruct((b, s, c), F32),
        scratch_shapes=[pltpu.VMEM((CONV_HALO + s, LANES), F32)],
        compiler_params=_params("parallel", "parallel"),
    )(u, w, bias)


def _conv_bwd(u, dout, w, bias, dproj, name):
    b, s, c = u.shape
    rows = CONV_ROWS

    def fold(v):
        return jnp.sum(v.reshape(rows // CONV_HALO, CONV_HALO, LANES), axis=0)

    def body(u_ref, d_ref, w_ref, b_ref, buf_ref, du_ref, dw_ref, db_ref, ext, dpre):
        @pl.when(pl.program_id(1) == 0)
        def _():
            dw_ref[...] = jnp.zeros_like(dw_ref)
            db_ref[...] = jnp.zeros_like(db_ref)

        ext[0:CONV_HALO, :] = jnp.zeros((CONV_HALO, LANES), F32)
        ext[CONV_HALO:, :] = u_ref[...]
        dpre[s:, :] = jnp.zeros((CONV_HALO, LANES), F32)
        wv, bv = w_ref[...], b_ref[...]
        sums = [jnp.zeros((CONV_HALO, LANES), F32)] * (SSM_CONV + 1)
        for r0 in range(0, s, rows):
            window = lambda sh: ext[CONV_HALO + r0 - sh:CONV_HALO + r0 - sh + rows, :]
            acc = _conv_taps(window, wv, bv)
            sg = _sigmoid(acc)
            dp = d_ref[r0:r0 + rows, :] * (sg * (1.0 + acc * (1.0 - sg)))
            dpre[r0:r0 + rows, :] = dp
            taps = [sums[SSM_CONV - 1 - sh] + fold(dp * window(sh)) for sh in range(SSM_CONV)]
            sums = taps[::-1] + [sums[SSM_CONV] + fold(dp)]
        for r0 in range(0, s, rows):
            du = wv[SSM_CONV - 1:SSM_CONV, :] * dpre[r0:r0 + rows, :]
            for sh in range(1, SSM_CONV):
                kidx = SSM_CONV - 1 - sh
                du = du + wv[kidx:kidx + 1, :] * dpre[r0 + sh:r0 + sh + rows, :]
            du_ref[r0:r0 + rows, :] = du.astype(BF16)
        for kidx in range(SSM_CONV):
            dw_ref[kidx:kidx + 1, :] += jnp.sum(sums[kidx], axis=0, keepdims=True)
        db_ref[...] += jnp.sum(sums[SSM_CONV], axis=0, keepdims=True)

    strip = pl.BlockSpec((None, s, LANES), lambda j, bi: (bi, 0, j))
    taps = pl.BlockSpec((SSM_CONV, LANES), lambda j, bi: (0, j))
    vec = pl.BlockSpec((1, LANES), lambda j, bi: (0, j))
    du_cols = pl.BlockSpec((None, s, LANES), lambda j, bi: (bi, 0, DPROJ_COLS["xbc"] // LANES + j))
    return pl.pallas_call(
        body, name=name, grid=(c // LANES, b),
        in_specs=[strip, strip, taps, vec, pl.BlockSpec(memory_space=pl.ANY)], out_specs=[du_cols, taps, vec],
        input_output_aliases={4: 0},
        out_shape=[jax.ShapeDtypeStruct(dproj.shape, dproj.dtype), jax.ShapeDtypeStruct((SSM_CONV, c), F32),
                   jax.ShapeDtypeStruct((1, c), F32)],
        scratch_shapes=[pltpu.VMEM((CONV_HALO + s, LANES), F32), pltpu.VMEM((s + CONV_HALO, LANES), F32)],
        compiler_params=_params("parallel", "arbitrary"),
    )(u, dout, w, bias, dproj)


def _ssd_chunk_terms(dtr_ref, bias_ref, alog_ref):
    q = SSM_CHUNK
    dt = _softplus(dtr_ref[...] + bias_ref[...])
    a_neg = -jnp.exp(alog_ref[...])
    row = lax.broadcasted_iota(jnp.int32, (q, q), 0)
    col = lax.broadcasted_iota(jnp.int32, (q, q), 1)
    lower = row >= col
    s = _mask_nn(lower, dt * a_neg)
    return dt, a_neg, s, s.T, lower


def _head_masks():
    heads = jnp.arange(LANES)[:, None]
    chans = jnp.arange(SSM_D_INNER)[None, :]
    to_channels = (chans // SSM_HEAD_DIM == heads).astype(BF16)
    return to_channels, to_channels.T


def _per_channel(v, to_channels):
    hi = v.astype(BF16)
    lo = (v - hi.astype(F32)).astype(BF16)
    return _nn(hi, to_channels) + _nn(lo, to_channels)


def _per_head(v, to_heads):
    hi = v.astype(BF16)
    lo = (v - hi.astype(F32)).astype(BF16)
    return _nn(hi, to_heads) + _nn(lo, to_heads)


def _decay_terms_per_channel(dt, s_col, to_channels):
    q = SSM_CHUNK
    tot = s_col[q - 1:q, :]
    stacked = jnp.concatenate([dt, jnp.exp(s_col), jnp.exp(tot - s_col)], axis=0)
    wide = _per_channel(stacked, to_channels)
    dtx, esx, decx = wide[:q], wide[q:2 * q], wide[2 * q:]
    return dtx, esx, decx, esx[0:1, :] * decx[0:1, :]


SSM_PAIRS_PER_GROUP = SSM_HEADS_PER_GROUP // 2
SSM_GROUP_CHANNELS = SSM_HEADS_PER_GROUP * SSM_HEAD_DIM


def _split_pair(v):
    first = lax.broadcasted_iota(jnp.int32, v.shape, 1) < SSM_HEAD_DIM
    return jnp.concatenate([jnp.where(first, v, 0.0), jnp.where(first, 0.0, v)], axis=0)


def _ssd_fwd(xc, dtr, dt_bias, a_log, dskx, to_channels, name):
    b, s, _ = xc.shape
    q = SSM_CHUNK
    nc = s // q
    n, gc = SSM_D_STATE, SSM_GROUP_CHANNELS

    def body(xc_ref, dtr_ref, bias_ref, alog_ref, dsk_ref, tc_ref, y_ref, hs_ref, h_scr):
        @pl.when(pl.program_id(1) == 0)
        def _():
            h_scr[...] = jnp.zeros_like(h_scr)

        dt, _, s_col, s_row, lower = _ssd_chunk_terms(dtr_ref, bias_ref, alog_ref)
        dtx, esx, decx, etotx = _decay_terms_per_channel(dt, s_col, tc_ref[...])
        x = xc_ref[:, :SSM_D_INNER]
        xdt = x * dtx
        xdec = xdt * decx
        skip = dsk_ref[...] * x
        for g in range(SSM_N_GROUPS):
            bg = xc_ref[:, SSM_D_INNER + n * g:SSM_D_INNER + n * (g + 1)].astype(BF16)
            cg = xc_ref[:, SSM_D_INNER + n * (SSM_N_GROUPS + g):SSM_D_INNER + n * (SSM_N_GROUPS + g + 1)].astype(BF16)
            gsl = slice(gc * g, gc * (g + 1))
            gm = _nt(cg, bg)
            hgt = h_scr[:, gsl]
            hs_ref[:, gsl] = hgt
            y_off = esx[:, gsl] * _nn(cg, hgt)
            h_scr[:, gsl] = etotx[:, gsl] * hgt + _tn(bg, xdec[:, gsl])
            for k in range(SSM_PAIRS_PER_GROUP):
                h0 = g * SSM_HEADS_PER_GROUP + 2 * k
                lo = gc * g + LANES * k
                ms = []
                for h in (h0, h0 + 1):
                    lm = jnp.exp(jnp.where(lower, s_col[:, h:h + 1] - s_row[h:h + 1, :], NEG_INF))
                    ms.append((gm * lm).astype(BF16))
                y_diag = _nn(jnp.concatenate(ms, axis=1), _split_pair(xdt[:, lo:lo + LANES]))
                y_ref[:, lo:lo + LANES] = y_diag + y_off[:, LANES * k:LANES * (k + 1)] + skip[:, lo:lo + LANES]

    vec = pl.BlockSpec((1, LANES), lambda bi, c: (0, 0))
    return pl.pallas_call(
        body, name=name, grid=(b, nc),
        in_specs=[pl.BlockSpec((None, q, SSM_CONV_DIM), lambda bi, c: (bi, c, 0)),
                  pl.BlockSpec((None, q, LANES), lambda bi, c: (bi, c, 0)), vec, vec,
                  pl.BlockSpec((1, SSM_D_INNER), lambda bi, c: (0, 0)),
                  pl.BlockSpec((LANES, SSM_D_INNER), lambda bi, c: (0, 0))],
        out_specs=[pl.BlockSpec((None, q, SSM_D_INNER), lambda bi, c: (bi, c, 0)),
                   pl.BlockSpec((None, None, n, SSM_D_INNER), lambda bi, c: (bi, c, 0, 0))],
        out_shape=[jax.ShapeDtypeStruct((b, s, SSM_D_INNER), F32),
                   jax.ShapeDtypeStruct((b, nc, n, SSM_D_INNER), F32)],
        scratch_shapes=[pltpu.VMEM((n, SSM_D_INNER), F32)],
        compiler_params=_params("parallel", "arbitrary"),
    )(xc, dtr, dt_bias, a_log, dskx, to_channels)


def _ssd_bwd(xc, dtr, dy, hs, dt_bias, a_log, dskx, to_channels, to_heads, dproj, name):
    b, s, _ = xc.shape
    q = SSM_CHUNK
    nc = s // q
    n, gc = SSM_D_STATE, SSM_GROUP_CHANNELS

    def colsum(v):
        return jnp.sum(v, axis=0, keepdims=True)

    def body(xc_ref, dtr_ref, dy_ref, hs_ref, bias_ref, alog_ref, dsk_ref, tc_ref, th_ref, buf_ref,
             dxc_ref, ddtr_ref, dalog_ref, ddsk_ref, dbias_ref, dh_scr, dxs_scr, dxd_scr, w_scr, dst_scr, rows_scr):
        ci = pl.program_id(1)

        @pl.when(ci == 0)
        def _():
            dh_scr[...] = jnp.zeros_like(dh_scr)

        @pl.when(jnp.logical_and(pl.program_id(0) == 0, ci == 0))
        def _():
            dalog_ref[...] = jnp.zeros_like(dalog_ref)
            ddsk_ref[...] = jnp.zeros_like(ddsk_ref)
            dbias_ref[...] = jnp.zeros_like(dbias_ref)
            dst_scr[...] = jnp.zeros_like(dst_scr)

        dt, a_neg, s_col, s_row, lower = _ssd_chunk_terms(dtr_ref, bias_ref, alog_ref)
        upper = jnp.logical_not(lower) | (lax.broadcasted_iota(jnp.int32, (q, q), 0)
                                          == lax.broadcasted_iota(jnp.int32, (q, q), 1))
        dtx, esx, decx, etotx = _decay_terms_per_channel(dt, s_col, tc_ref[...])
        x = xc_ref[:, :SSM_D_INNER]
        dyv = dy_ref[...]
        xdt = x * dtx
        xdec = xdt * decx
        dw = esx * dyv
        rows_scr[...] = jnp.zeros_like(rows_scr)
        for g in range(SSM_N_GROUPS):
            b_lo = SSM_D_INNER + n * g
            c_lo = SSM_D_INNER + n * (SSM_N_GROUPS + g)
            bg = xc_ref[:, b_lo:b_lo + n].astype(BF16)
            cg = xc_ref[:, c_lo:c_lo + n].astype(BF16)
            gsl = slice(gc * g, gc * (g + 1))
            gm = _nt(cg, bg)
            gmt = _nt(bg, cg)
            hgt = hs_ref[:, gsl]
            dhgt = dh_scr[:, gsl]
            w_scr[:, gsl] = _nn(cg, hgt)
            dcg = _nt(dw[:, gsl], hgt)
            dxs = decx[:, gsl] * _nn(bg, dhgt)
            dxs_scr[:, gsl] = dxs
            dbg = _nt(xdec[:, gsl], dhgt)
            rows_scr[2:3, gsl] = colsum(dhgt * hgt)
            dh_scr[:, gsl] = _tn(cg, dw[:, gsl]) + etotx[:, gsl] * dhgt
            dg = jnp.zeros((q, q), F32)
            dgt = jnp.zeros((q, q), F32)
            for k in range(SSM_PAIRS_PER_GROUP):
                h0 = g * SSM_HEADS_PER_GROUP + 2 * k
                lo = gc * g + LANES * k
                xp = xdt[:, lo:lo + LANES]
                dyp = dyv[:, lo:lo + LANES]
                dy2 = _split_pair(dyp)
                dm2 = _nt(dy2, xp)
                dmt2 = _nt(_split_pair(xp), dyp)
                mts = []
                for i, h in enumerate((h0, h0 + 1)):
                    lm = jnp.exp(jnp.where(lower, s_col[:, h:h + 1] - s_row[h:h + 1, :], NEG_INF))
                    lmt = jnp.exp(jnp.where(upper, s_row[h:h + 1, :] - s_col[:, h:h + 1], NEG_INF))
                    dm = dm2[q * i:q * (i + 1), :]
                    dmt = dmt2[q * i:q * (i + 1), :]
                    dg = dg + dm * lm
                    dgt = dgt + dmt * lmt
                    mt = gmt * lmt
                    dst_scr[h:h + 1, :] = colsum(dmt * mt) - colsum(dm * (gm * lm))
                    mts.append(mt.astype(BF16))
                dxd_scr[:, lo:lo + LANES] = _nn(jnp.concatenate(mts, axis=1), dy2)
            dxc_ref[:, b_lo:b_lo + n] = dbg + _nn(dgt, cg)
            dxc_ref[:, c_lo:c_lo + n] = dcg + _nn(dg, bg)
        dxs = dxs_scr[...]
        dxdt = dxd_scr[...] + dxs
        dxc_ref[:, :SSM_D_INNER] = dxdt * dtx + dsk_ref[...] * dyv
        state_part = xdt * dxs
        rows_scr[0:1, :] = colsum(dyv * x)
        rows_scr[1:2, :] = colsum(state_part)
        th = th_ref[...]
        per_head = _per_head(jnp.concatenate([dw * w_scr[...] - state_part, dxdt * x], axis=0), th)
        r_ds, r_dt = per_head[:q], per_head[q:]
        sums = _per_head(rows_scr[...], th)
        etot = jnp.exp(s_col[q - 1:q, :])
        dtot = sums[1:2, :] + etot * sums[2:3, :]
        last = lax.broadcasted_iota(jnp.int32, (q, LANES), 0) == q - 1
        ds = dst_scr[...].T + r_ds + jnp.where(last, dtot, 0.0)
        da = _mask_nn(upper, ds)
        ddt = da * a_neg + r_dt
        live = lax.broadcasted_iota(jnp.int32, (1, LANES), 1) < SSM_N_HEADS
        sg = _sigmoid(dtr_ref[...] + bias_ref[...])
        ddtr = jnp.where(live, ddt * sg, 0.0)
        ddtr_ref[:, :LANES] = ddtr.astype(BF16)
        ddtr_ref[:, LANES:] = jnp.zeros((q, DPROJ_DT_WIDTH - LANES), BF16)
        dalog_ref[...] += jnp.where(live, colsum(da * dt) * a_neg, 0.0)
        ddsk_ref[...] += jnp.where(live, sums[0:1, :], 0.0)
        dbias_ref[...] += colsum(ddtr)

    rev = lambda bi, c: (bi, nc - 1 - c, 0)
    vec = pl.BlockSpec((1, LANES), lambda bi, c: (0, 0))
    wide = pl.BlockSpec((None, q, SSM_D_INNER), rev)
    return pl.pallas_call(
        body, name=name, grid=(b, nc),
        in_specs=[pl.BlockSpec((None, q, SSM_CONV_DIM), rev), pl.BlockSpec((None, q, LANES), rev), wide,
                  pl.BlockSpec((None, None, n, SSM_D_INNER), lambda bi, c: (bi, nc - 1 - c, 0, 0)),
                  vec, vec, pl.BlockSpec((1, SSM_D_INNER), lambda bi, c: (0, 0)),
                  pl.BlockSpec((LANES, SSM_D_INNER), lambda bi, c: (0, 0)),
                  pl.BlockSpec((SSM_D_INNER, LANES), lambda bi, c: (0, 0)),
                  pl.BlockSpec(memory_space=pl.ANY)],
        out_specs=[pl.BlockSpec((None, q, SSM_CONV_DIM), rev),
                   pl.BlockSpec((None, q, DPROJ_DT_WIDTH),
                                lambda bi, c: (bi, nc - 1 - c, DPROJ_COLS["dt"] // DPROJ_DT_WIDTH)), vec, vec, vec],
        input_output_aliases={9: 1},
        out_shape=[jax.ShapeDtypeStruct((b, s, SSM_CONV_DIM), F32), jax.ShapeDtypeStruct(dproj.shape, dproj.dtype),
                   jax.ShapeDtypeStruct((1, LANES), F32), jax.ShapeDtypeStruct((1, LANES), F32),
                   jax.ShapeDtypeStruct((1, LANES), F32)],
        scratch_shapes=[pltpu.VMEM((n, SSM_D_INNER), F32)] + [pltpu.VMEM((q, SSM_D_INNER), F32)] * 3
        + [pltpu.VMEM((LANES, q), F32), pltpu.VMEM((8, SSM_D_INNER), F32)],
        compiler_params=_params("arbitrary", "arbitrary"),
    )(xc, dtr, dy, hs, dt_bias, a_log, dskx, to_channels, to_heads, dproj)


SSM_GROUP_WIDTH = SSM_D_INNER // SSM_N_GROUPS


def _gate_norm_fwd(y, z, w, name):
    t, d = y.shape
    tm = _pick(t, (256, 128))

    def body(y_ref, z_ref, w_ref, o_ref):
        for g in range(SSM_N_GROUPS):
            sl = slice(SSM_GROUP_WIDTH * g, SSM_GROUP_WIDTH * (g + 1))
            zv = z_ref[:, sl]
            u = y_ref[:, sl] * (zv * _sigmoid(zv))
            r = lax.rsqrt(jnp.mean(u * u, axis=-1, keepdims=True) + EPS)
            o_ref[:, sl] = ((u * r) * w_ref[:, sl]).astype(BF16)

    row = pl.BlockSpec((tm, d), lambda i: (i, 0))
    return pl.pallas_call(
        body, name=name, grid=(t // tm,),
        in_specs=[row, row, pl.BlockSpec((1, d), lambda i: (0, 0))], out_specs=row,
        out_shape=jax.ShapeDtypeStruct((t, d), BF16),
        compiler_params=_params("parallel"),
    )(y, z, w)


def _ssm_out_dx_gate_norm_bwd(dys, w_ssm_out, y, z, w, dproj, name):
    t, d = y.shape
    k = dys.shape[1]
    gw = SSM_GROUP_WIDTH
    tm = _pick(t, (512, 256, 128))

    def body(dys_ref, ws_ref, y_ref, z_ref, w_ref, buf_ref, dy_ref, dz_ref, dw_ref):
        @pl.when(pl.program_id(0) == 0)
        def _():
            dw_ref[...] = jnp.zeros_like(dw_ref)

        dout = _nt(dys_ref[...], ws_ref[...])
        for g in range(SSM_N_GROUPS):
            sl = slice(gw * g, gw * (g + 1))
            zv = z_ref[:, sl]
            yv = y_ref[:, sl]
            sg = _sigmoid(zv)
            silu = zv * sg
            u = yv * silu
            r = lax.rsqrt(jnp.mean(u * u, axis=-1, keepdims=True) + EPS)
            uh = u * r
            dov = dout[:, sl]
            dw_ref[:, sl] += jnp.sum(dov * uh, axis=0, keepdims=True)
            dyg = dov * w_ref[:, sl]
            du = r * (dyg - uh * jnp.mean(dyg * uh, axis=-1, keepdims=True))
            dy_ref[:, sl] = du * silu
            dz_ref[:, sl] = (du * yv * (sg * (1.0 + zv * (1.0 - sg)))).astype(BF16)

    row = pl.BlockSpec((tm, d), lambda i: (i, 0))
    vec = pl.BlockSpec((1, d), lambda i: (0, 0))
    z_cols = pl.BlockSpec((tm, d), lambda i: (i, DPROJ_COLS["z"] // d))
    return pl.pallas_call(
        body, name=name, grid=(t // tm,),
        in_specs=[pl.BlockSpec((tm, k), lambda i: (i, 0)), pl.BlockSpec((d, k), lambda i: (0, 0)), row, row, vec,
                  pl.BlockSpec(memory_space=pl.ANY)],
        out_specs=[row, z_cols, vec],
        out_shape=[jax.ShapeDtypeStruct((t, d), F32), jax.ShapeDtypeStruct(dproj.shape, dproj.dtype),
                   jax.ShapeDtypeStruct((1, d), F32)],
        input_output_aliases={5: 1},
        compiler_params=_params("arbitrary"),
    )(dys, w_ssm_out, y, z, w, dproj)


def _rope_tables(s):
    half = ATT_HEAD_DIM // 2
    inv = ROPE_THETA ** (-jnp.arange(half, dtype=F32) / half)
    ang = jnp.arange(s).astype(F32)[:, None] * inv[None, :]
    cos, sin = jnp.cos(ang), jnp.sin(ang)
    return jnp.concatenate([cos, cos], axis=-1), jnp.concatenate([-sin, sin], axis=-1)


ATT_TILE = 256


def _by_residue_spec(r, width):
    return pl.BlockSpec((None, r, ATT_TILE // r, width), lambda bi, i: (bi, 0, i, 0))


def _to_residues(tile, stage, r, store):
    if r == 1:
        store(0, tile)
        return
    stage[...] = tile
    for ri in range(r):
        store(ri, stage[pl.ds(ri, tile.shape[0] // r, stride=r), :])


def _from_residues(load, stage, r):
    if r == 1:
        return load(0)
    for ri in range(r):
        stage[pl.ds(ri, ATT_TILE // r, stride=r), :] = load(ri)
    return stage[...]


QKV_ROWS = 1024
QKV_COLS = 768


def _qkv_proj_rope(h, w_qkv_t, cosf, sinf, b, s, name):
    t, k = h.shape
    tm, d, gw = QKV_ROWS, ATT_HEAD_DIM, ATT_OUT_DIM
    per_seq = s // tm

    def body(h_ref, w_ref, c_ref, s_ref, *rest):
        outs, stage = rest[:-1], rest[-1]
        cv, sv = c_ref[...], s_ref[...]
        hv = h_ref[...]
        for lo in range(0, ATT_QKV_DIM, QKV_COLS):
            acc = _nt(hv, w_ref[lo:lo + QKV_COLS, :])
            for hh in range(QKV_COLS // d):
                kind, head = divmod(lo // d + hh, ATT_N_HEADS)
                gi, j = divmod(head, ATT_HEADS_PER_GROUP)
                dst = slice(kind * gw + d * j, kind * gw + d * (j + 1))
                tv = acc[:, d * hh:d * (hh + 1)]
                if kind < 2:
                    tv = tv * cv + pltpu.roll(tv, d // 2, 1) * sv

                def store(ri, rows, o_ref=outs[gi], dst=dst):
                    o_ref[ri, :, dst] = rows.astype(BF16)

                _to_residues(tv, stage, ATT_DILATIONS[gi], store)

    tab = pl.BlockSpec((tm, d), lambda i: (i % per_seq, 0))
    return pl.pallas_call(
        body, name=name, grid=(t // tm,),
        in_specs=[pl.BlockSpec((tm, k), lambda i: (i, 0)), pl.BlockSpec((ATT_QKV_DIM, k), lambda i: (0, 0)), tab, tab],
        out_specs=[pl.BlockSpec((None, r, tm // r, 3 * gw), lambda i: (i // per_seq, 0, i % per_seq, 0))
                   for r in ATT_DILATIONS],
        out_shape=[jax.ShapeDtypeStruct((b, r, s // r, 3 * gw), BF16) for r in ATT_DILATIONS],
        scratch_shapes=[pltpu.VMEM((tm, d), F32)],
        compiler_params=_params("parallel"),
    )(h, w_qkv_t, cosf, sinf)


def _rope_bwd(dq, dk, dv, cosf, sinf, dproj, name):
    n_pat = len(ATT_DILATIONS)
    b, _, s, gw = dq[0].shape
    ts, d = ATT_TILE, ATT_HEAD_DIM

    def body(*refs):
        ins, (c_ref, s_ref, _, o_ref, stage) = refs[:3 * n_pat], refs[3 * n_pat:]
        cv, sv = c_ref[...], s_ref[...]
        for kind in range(3):
            for gi, r in enumerate(ATT_DILATIONS):
                src = ins[kind * n_pat + gi]
                for j in range(ATT_HEADS_PER_GROUP):
                    tv = _from_residues(lambda ri, src=src, j=j: src[ri, :, d * j:d * (j + 1)], stage, r)
                    if kind < 2:
                        tv = tv * cv + pltpu.roll(tv * sv, d // 2, 1)
                    lo = d * (kind * ATT_N_HEADS + gi * ATT_HEADS_PER_GROUP + j)
                    o_ref[:, lo:lo + d] = tv.astype(BF16)

    tab = pl.BlockSpec((ts, d), lambda bi, i: (i, 0))
    parts = [_by_residue_spec(r, gw) for r in ATT_DILATIONS]
    return pl.pallas_call(
        body, name=name, grid=(b, s // ts), in_specs=parts * 3 + [tab, tab, pl.BlockSpec(memory_space=pl.ANY)],
        out_specs=pl.BlockSpec((None, ts, ATT_QKV_DIM), lambda bi, i: (bi, i, DPROJ_COLS["qkv"] // ATT_QKV_DIM)),
        out_shape=jax.ShapeDtypeStruct(dproj.shape, dproj.dtype),
        input_output_aliases={3 * n_pat + 2: 0},
        scratch_shapes=[pltpu.VMEM((ts, d), F32)],
        compiler_params=_params("parallel", "parallel"),
    )(*dq, *dk, *dv, cosf, sinf, dproj)


ATT_SCALE = ATT_HEAD_DIM ** -0.5
ATT_STEP = 2 * ATT_BLOCK


def _att_spec(col):
    return pl.BlockSpec((None, None, ATT_STEP, ATT_OUT_DIM), lambda bi, ri, i: (bi, ri, i, col))


def _att_edge_spec(col, side, n_steps):
    def index(bi, ri, i):
        blk = 2 * i - 1 if side < 0 else 2 * i + 2
        return (bi, ri, jnp.clip(blk, 0, 2 * n_steps - 1), col)
    return pl.BlockSpec((None, None, ATT_BLOCK, ATT_OUT_DIM), index)


def _band_mask(shape, q_axis, has_prev):
    qi = lax.broadcasted_iota(jnp.int32, shape, q_axis)
    kj = lax.broadcasted_iota(jnp.int32, shape, 1 - q_axis)
    dist = qi + ATT_BLOCK - kj
    return (dist >= 0) & (dist <= ATT_BLOCK) & (has_prev | (kj >= ATT_BLOCK))


def _att_fwd(qkr, name):
    b, r, l, _ = qkr.shape
    nb = l // ATT_STEP
    d = ATT_HEAD_DIM

    def body(q_ref, kp_ref, k_ref, vp_ref, v_ref, o_ref, lse_ref):
        mask = _band_mask((ATT_STEP, ATT_BLOCK + ATT_STEP), 0, pl.program_id(2) > 0)
        heads = [slice(d * j, d * (j + 1)) for j in range(ATT_HEADS_PER_GROUP)]
        scores = [_nt(q_ref[:, sl], jnp.concatenate([kp_ref[:, sl], k_ref[:, sl]], axis=0)) for sl in heads]
        scores = [jnp.where(mask, sc * ATT_SCALE, NEG_INF) for sc in scores]
        tops = [jnp.max(sc, axis=-1, keepdims=True) for sc in scores]
        probs = [jnp.exp(sc - m) for sc, m in zip(scores, tops)]
        dens = [jnp.sum(pr, axis=-1, keepdims=True) for pr in probs]
        for sl, m, pr, den in zip(heads, tops, probs, dens):
            o_ref[:, sl] = _nn(pr / den, jnp.concatenate([vp_ref[:, sl], v_ref[:, sl]], axis=0))
            lse_ref[:, sl] = jnp.broadcast_to(m + jnp.log(den), (ATT_STEP, d))

    out_spec = _att_spec(0)
    return pl.pallas_call(
        body, name=name, grid=(b, r, nb),
        in_specs=[_att_spec(0), _att_edge_spec(1, -1, nb), _att_spec(1), _att_edge_spec(2, -1, nb), _att_spec(2)],
        out_specs=[out_spec, out_spec],
        out_shape=[jax.ShapeDtypeStruct((b, r, l, ATT_OUT_DIM), F32)] * 2,
        compiler_params=_params("parallel", "parallel", "parallel"),
    )(qkr, qkr, qkr, qkr, qkr)


def _att_merge(os_, lses, name):
    n_pat = len(os_)
    b, _, s, gw = os_[0].shape
    ts, d = ATT_TILE, ATT_HEAD_DIM

    def body(*refs):
        o_refs, l_refs = refs[:n_pat], refs[n_pat:2 * n_pat]
        att_ref, lse_outs, stage = refs[2 * n_pat], refs[2 * n_pat + 1:3 * n_pat + 1], refs[-1]
        for j in range(ATT_HEADS_PER_GROUP):
            sl = slice(d * j, d * (j + 1))
            ov = [_from_residues(lambda ri, g=g: o_refs[g][ri, :, sl], stage, r)
                  for g, r in enumerate(ATT_DILATIONS)]
            ls = [_from_residues(lambda ri, g=g: l_refs[g][ri, :, sl], stage, r)
                  for g, r in enumerate(ATT_DILATIONS)]
            m = functools.reduce(jnp.maximum, ls)
            es = [jnp.exp(lv - m) for lv in ls]
            tot = functools.reduce(lambda u, v: u + v, es)
            acc = (es[0] / tot) * ov[0]
            for g in range(1, n_pat):
                acc = acc + (es[g] / tot) * ov[g]
            att_ref[:, sl] = acc
            joint = m + jnp.log(tot)
            for g, r in enumerate(ATT_DILATIONS):
                def store(ri, rows, out=lse_outs[g]):
                    out[ri, :, sl] = rows
                _to_residues(joint, stage, r, store)

    parts = [_by_residue_spec(r, gw) for r in ATT_DILATIONS]
    return pl.pallas_call(
        body, name=name, grid=(b, s // ts), in_specs=parts * 2,
        out_specs=[pl.BlockSpec((None, ts, gw), lambda bi, i: (bi, i, 0))] + parts,
        out_shape=[jax.ShapeDtypeStruct((b, s, gw), F32)]
        + [jax.ShapeDtypeStruct((b, r, s // r, gw), F32) for r in ATT_DILATIONS],
        scratch_shapes=[pltpu.VMEM((ts, d), F32)],
        compiler_params=_params("parallel", "parallel"),
    )(*os_, *lses)


def _att_delta(att, datt, name):
    b, s, gw = att.shape
    ts, d = ATT_TILE, ATT_HEAD_DIM
    n_pat = len(ATT_DILATIONS)

    def body(a_ref, d_ref, *rest):
        do_outs, dl_outs, stage = rest[:n_pat], rest[n_pat:2 * n_pat], rest[-1]
        for j in range(ATT_HEADS_PER_GROUP):
            sl = slice(d * j, d * (j + 1))
            dv = d_ref[:, sl]
            delta = jnp.broadcast_to(jnp.sum(a_ref[:, sl] * dv, axis=-1, keepdims=True), (ts, d))
            for g, r in enumerate(ATT_DILATIONS):
                def store_do(ri, rows, out=do_outs[g]):
                    out[ri, :, sl] = rows.astype(BF16)

                def store_dl(ri, rows, out=dl_outs[g]):
                    out[ri, :, sl] = rows

                _to_residues(dv, stage, r, store_do)
                _to_residues(delta, stage, r, store_dl)

    row = pl.BlockSpec((None, ts, gw), lambda bi, i: (bi, i, 0))
    parts = [_by_residue_spec(r, gw) for r in ATT_DILATIONS]
    outs = pl.pallas_call(
        body, name=name, grid=(b, s // ts), in_specs=[row, row], out_specs=parts * 2,
        out_shape=[jax.ShapeDtypeStruct((b, r, s // r, gw), BF16) for r in ATT_DILATIONS]
        + [jax.ShapeDtypeStruct((b, r, s // r, gw), F32) for r in ATT_DILATIONS],
        scratch_shapes=[pltpu.VMEM((ts, d), F32)],
        compiler_params=_params("parallel", "parallel"),
    )(att, datt)
    return outs[:n_pat], outs[n_pat:]


def _att_bwd_q(qkr, datt, lse, delta, name):
    b, r, l, _ = qkr.shape
    nb = l // ATT_STEP
    d = ATT_HEAD_DIM

    def body(q_ref, kp_ref, k_ref, vp_ref, v_ref, do_ref, lse_ref, dl_ref, dq_ref):
        mask = _band_mask((ATT_STEP, ATT_BLOCK + ATT_STEP), 0, pl.program_id(2) > 0)
        heads = [slice(d * j, d * (j + 1)) for j in range(ATT_HEADS_PER_GROUP)]
        kcats = [jnp.concatenate([kp_ref[:, sl], k_ref[:, sl]], axis=0) for sl in heads]
        scores = [_nt(q_ref[:, sl], kcat) for sl, kcat in zip(heads, kcats)]
        dps = [_nt(do_ref[:, sl], jnp.concatenate([vp_ref[:, sl], v_ref[:, sl]], axis=0)) for sl in heads]
        probs = [jnp.exp(jnp.where(mask, sc * ATT_SCALE - lse_ref[:, sl.start:sl.start + 1], NEG_INF))
                 for sl, sc in zip(heads, scores)]
        dscs = [pr * (dp - dl_ref[:, sl.start:sl.start + 1]) for sl, pr, dp in zip(heads, probs, dps)]
        for sl, dsc, kcat in zip(heads, dscs, kcats):
            dq_ref[:, sl] = _nn(dsc, kcat) * ATT_SCALE

    tok = _att_spec(0)
    return pl.pallas_call(
        body, name=name, grid=(b, r, nb),
        in_specs=[_att_spec(0), _att_edge_spec(1, -1, nb), _att_spec(1), _att_edge_spec(2, -1, nb), _att_spec(2),
                  tok, tok, tok],
        out_specs=tok,
        out_shape=jax.ShapeDtypeStruct((b, r, l, ATT_OUT_DIM), F32),
        compiler_params=_params("parallel", "parallel", "parallel"),
    )(qkr, qkr, qkr, qkr, qkr, datt, lse, delta)


def _att_bwd_kv(qkr, datt, lse, delta, name):
    b, r, l, _ = qkr.shape
    nb = l // ATT_STEP
    d = ATT_HEAD_DIM

    def body(k_ref, v_ref, q_ref, qn_ref, do_ref, don_ref, lse_ref, lsen_ref, dl_ref, dln_ref, dk_ref, dv_ref):
        shape = (ATT_STEP, ATT_STEP + ATT_BLOCK)
        kj = lax.broadcasted_iota(jnp.int32, shape, 0)
        qi = lax.broadcasted_iota(jnp.int32, shape, 1)
        dist = qi - kj
        has_next = pl.program_id(2) < nb - 1
        mask = (dist >= 0) & (dist <= ATT_BLOCK) & (has_next | (qi < ATT_STEP))
        def per_query(own_ref, next_ref, sl):
            return jnp.tile(jnp.concatenate([own_ref[:, sl], next_ref[:, sl]], axis=0).T, (ATT_STEP // d, 1))

        heads = [slice(d * j, d * (j + 1)) for j in range(ATT_HEADS_PER_GROUP)]
        qcats = [jnp.concatenate([q_ref[:, sl], qn_ref[:, sl]], axis=0) for sl in heads]
        docats = [jnp.concatenate([do_ref[:, sl], don_ref[:, sl]], axis=0) for sl in heads]
        scores = [_nt(k_ref[:, sl], qcat) for sl, qcat in zip(heads, qcats)]
        dps = [_nt(v_ref[:, sl], docat) for sl, docat in zip(heads, docats)]
        probs = [jnp.exp(jnp.where(mask, sc * ATT_SCALE - per_query(lse_ref, lsen_ref, sl), NEG_INF))
                 for sl, sc in zip(heads, scores)]
        for sl, pr, docat in zip(heads, probs, docats):
            dv_ref[:, sl] = _nn(pr, docat)
        dscs = [pr * (dp - per_query(dl_ref, dln_ref, sl)) for sl, pr, dp in zip(heads, probs, dps)]
        for sl, dsc, qcat in zip(heads, dscs, qcats):
            dk_ref[:, sl] = _nn(dsc, qcat) * ATT_SCALE

    tok, tok_n = _att_spec(0), _att_edge_spec(0, 1, nb)
    return pl.pallas_call(
        body, name=name, grid=(b, r, nb),
        in_specs=[_att_spec(1), _att_spec(2), _att_spec(0), _att_edge_spec(0, 1, nb),
                  tok, tok_n, tok, tok_n, tok, tok_n],
        out_specs=[tok, tok],
        out_shape=[jax.ShapeDtypeStruct((b, r, l, ATT_OUT_DIM), F32)] * 2,
        compiler_params=_params("parallel", "parallel", "parallel"),
    )(qkr, qkr, qkr, qkr, datt, datt, lse, lse, delta, delta)


def _att_out_proj_mix(att, w_att_t, gl, bg, ys, name):
    t, k = att.shape
    d = w_att_t.shape[0]
    tm = _pick(t, (512, 256, 128))

    def body(a_ref, w_ref, gl_ref, bg_ref, ys_ref, ya_ref, o_ref):
        ya = _nt(a_ref[...], w_ref[...])
        ya_ref[...] = ya
        g0 = _sigmoid(gl_ref[:, :d] + bg_ref[:, :d])
        g1 = _sigmoid(gl_ref[:, d:] + bg_ref[:, d:])
        o_ref[...] = (g0 * ys_ref[...] + g1 * ya).astype(BF16)

    row = pl.BlockSpec((tm, d), lambda i: (i, 0))
    return pl.pallas_call(
        body, name=name, grid=(t // tm,),
        in_specs=[pl.BlockSpec((tm, k), lambda i: (i, 0)), pl.BlockSpec((d, k), lambda i: (0, 0)),
                  pl.BlockSpec((tm, 2 * d), lambda i: (i, 0)), pl.BlockSpec((1, 2 * d), lambda i: (0, 0)), row],
        out_specs=[row, row],
        out_shape=[jax.ShapeDtypeStruct((t, d), F32), jax.ShapeDtypeStruct((t, d), BF16)],
        compiler_params=_params("parallel"),
    )(att, w_att_t, gl, bg, ys)


def _mix_out_dx_mix_bwd(dx, w_mix, gl, bg, ys, ya, name):
    t, d = ys.shape
    tm = _pick(t, (512, 256, 128))

    def body(dx_ref, w_ref, gl_ref, bg_ref, ys_ref, ya_ref, dys_ref, dya_ref, dgl_ref, dbg_ref):
        @pl.when(pl.program_id(0) == 0)
        def _():
            dbg_ref[...] = jnp.zeros_like(dbg_ref)

        dm = _nt(dx_ref[...], w_ref[...])
        g0 = _sigmoid(gl_ref[:, :d] + bg_ref[:, :d])
        g1 = _sigmoid(gl_ref[:, d:] + bg_ref[:, d:])
        dys_ref[...] = (dm * g0).astype(BF16)
        dya_ref[...] = (dm * g1).astype(BF16)
        d0 = dm * ys_ref[...] * (g0 * (1.0 - g0))
        d1 = dm * ya_ref[...] * (g1 * (1.0 - g1))
        dgl_ref[:, :d] = d0.astype(BF16)
        dgl_ref[:, d:] = d1.astype(BF16)
        dbg_ref[:, :d] += jnp.sum(d0, axis=0, keepdims=True)
        dbg_ref[:, d:] += jnp.sum(d1, axis=0, keepdims=True)

    row = pl.BlockSpec((tm, d), lambda i: (i, 0))
    wide = pl.BlockSpec((tm, 2 * d), lambda i: (i, 0))
    vec = pl.BlockSpec((1, 2 * d), lambda i: (0, 0))
    gate_cols = pl.BlockSpec((tm, 2 * d), lambda i: (i, DPROJ_COLS["gate"] // (2 * d)))
    return pl.pallas_call(
        body, name=name, grid=(t // tm,),
        in_specs=[row, pl.BlockSpec((d, d), lambda i: (0, 0)), wide, vec, row, row],
        out_specs=[row, row, gate_cols, vec],
        out_shape=[jax.ShapeDtypeStruct((t, d), BF16), jax.ShapeDtypeStruct((t, d), BF16),
                   jax.ShapeDtypeStruct((t, DPROJ_WIDTH), BF16), jax.ShapeDtypeStruct((1, 2 * d), F32)],
        compiler_params=_params("arbitrary"),
    )(dx, w_mix, gl, bg, ys, ya)


def _gate_up_proj_swiglu(h, w_gate_t, w_up_t, name):
    t, k = h.shape
    f = w_up_t.shape[0]
    tm, tn, _ = _mm_tiles(t, f, k, h.dtype.itemsize, w_gate_t.dtype.itemsize + w_up_t.dtype.itemsize, 4 + 4 + 2, False)

    def body(h_ref, wg_ref, wu_ref, gt_ref, up_ref, act_ref):
        hv = h_ref[...]
        gv = _nt(hv, wg_ref[...])
        gt_ref[...] = gv
        up = _nt(hv, wu_ref[...])
        up_ref[...] = up
        act_ref[...] = ((gv * _sigmoid(gv)) * up).astype(BF16)

    tile = pl.BlockSpec((tm, tn), lambda i, j: (i, j))
    w_tile = pl.BlockSpec((tn, k), lambda i, j: (j, 0))
    return pl.pallas_call(
        body, name=name, grid=(t // tm, f // tn),
        in_specs=[pl.BlockSpec((tm, k), lambda i, j: (i, 0)), w_tile, w_tile],
        out_specs=[tile, tile, tile],
        out_shape=[jax.ShapeDtypeStruct((t, f), F32), jax.ShapeDtypeStruct((t, f), F32),
                   jax.ShapeDtypeStruct((t, f), BF16)],
        compiler_params=_params("parallel", "parallel"),
    )(h, w_gate_t, w_up_t)


def _down_dx_swiglu_bwd(dx, w_down, gt, up, name):
    t, k = dx.shape
    f = w_down.shape[0]
    tm, tn, _ = _mm_tiles(t, f, k, dx.dtype.itemsize, w_down.dtype.itemsize, 2 + 2, True)
    tm = min(tm, 512)

    def body(d_ref, w_ref, g_ref, u_ref, dg_ref, du_ref):
        dact = _nt(d_ref[...], w_ref[...])
        gv = g_ref[...]
        sg = _sigmoid(gv)
        dg_ref[...] = (dact * u_ref[...] * (sg * (1.0 + gv * (1.0 - sg)))).astype(BF16)
        du_ref[...] = (dact * (gv * sg)).astype(BF16)

    tile = pl.BlockSpec((tm, tn), lambda i, j: (i, j))
    return pl.pallas_call(
        body, name=name, grid=(t // tm, f // tn),
        in_specs=[pl.BlockSpec((tm, k), lambda i, j: (i, 0)), pl.BlockSpec((tn, k), lambda i, j: (j, 0)), tile, tile],
        out_specs=[tile, tile], out_shape=[jax.ShapeDtypeStruct((t, f), BF16)] * 2,
        compiler_params=_params("parallel", "parallel"),
    )(dx, w_down, gt, up)


def _peer(k):
    x, y, c = lax.axis_index("x"), lax.axis_index("y"), lax.axis_index("c")
    px, py, pc = x ^ ((k >> 2) & 1), y ^ ((k >> 1) & 1), c ^ (k & 1)
    return (px, py, pc), 4 * px + 2 * py + pc


def _my_index():
    return 4 * lax.axis_index("x") + 2 * lax.axis_index("y") + lax.axis_index("c")


def _all_gather(parts, name):
    n_parts = len(parts)

    def body(*refs):
        ins, outs = refs[:n_parts], refs[n_parts:2 * n_parts]
        send_sems, recv_sems, local_sems = refs[2 * n_parts:]
        here, me = _peer(0)
        sibling, sib_idx = _peer(1)
        chips = [_peer(2 * q) for q in range(1, N_CHIPS)]

        def copy(i, k, block, to, src=None):
            return pltpu.make_async_remote_copy(
                src_ref=outs[i].at[block] if src is None else src, dst_ref=outs[i].at[block],
                send_sem=send_sems.at[i * (N_DEV - 1) + k], recv_sem=recv_sems.at[i * (N_DEV - 1) + k],
                device_id=to, device_id_type=MESH)

        local = [pltpu.make_async_copy(ins[i], outs[i].at[me], local_sems.at[i]) for i in range(n_parts)]
        for cp in local:
            cp.start()
        sends = []
        for i in range(n_parts):
            sends.append(copy(i, 0, me, sibling, src=ins[i]))
            sends += [copy(i, q, me, chip, src=ins[i]) for q, (chip, _) in enumerate(chips, start=1)]
        for cp in sends:
            cp.start()
        for q, (chip, chip_idx) in enumerate(chips, start=1):
            for i in range(n_parts):
                copy(i, q, chip_idx, here).wait_recv()
                fwd = copy(i, N_CHIPS - 1 + q, chip_idx, sibling)
                fwd.start()
                sends.append(fwd)
        for i in range(n_parts):
            copy(i, 0, sib_idx, here).wait_recv()
        for q, (_, chip_idx) in enumerate(chips, start=1):
            for i in range(n_parts):
                copy(i, N_CHIPS - 1 + q, chip_idx ^ 1, here).wait_recv()
        for cp in sends:
            cp.wait_send()
        for cp in local:
            cp.wait()

    hbm = pl.BlockSpec(memory_space=pl.ANY)
    return pl.pallas_call(
        body, name=name, in_specs=[hbm] * n_parts, out_specs=[hbm] * n_parts,
        out_shape=[jax.ShapeDtypeStruct((N_DEV,) + p_.shape, p_.dtype) for p_ in parts],
        scratch_shapes=[pltpu.SemaphoreType.DMA((n_parts * (N_DEV - 1),)),
                        pltpu.SemaphoreType.DMA((n_parts * (N_DEV - 1),)),
                        pltpu.SemaphoreType.DMA((n_parts,))],
        compiler_params=pltpu.CompilerParams(has_side_effects=True),
    )(*parts)


HBM_SPEC = pl.BlockSpec(memory_space=pltpu.HBM)
SEM_SPEC = pl.BlockSpec(memory_space=pltpu.SEMAPHORE)
DATAFLOW = pltpu.SideEffectType.DATAFLOW_SIDE_EFFECTING


def _gather_start(block, after, name):
    per_peer = block.ndim == 3

    def body(v_ref, land_ref, after_ref, send_sems, recv_sems, v_thru, land_thru, token):
        me = _my_index()
        for k in range(1, N_DEV):
            peer, pidx = _peer(k)
            pltpu.make_async_remote_copy(
                src_ref=v_ref.at[pidx] if per_peer else v_ref, dst_ref=land_ref.at[me],
                send_sem=send_sems.at[k - 1], recv_sem=recv_sems.at[k - 1],
                device_id=peer, device_id_type=MESH).start()
        token[...] = jnp.zeros_like(token)

    land_shape = (N_DEV,) + block.shape[-2:]
    return pl.pallas_call(
        body, name=name,
        out_shape=(pltpu.SemaphoreType.DMA((N_DEV - 1,)), pltpu.SemaphoreType.DMA((N_DEV - 1,)),
                   pltpu.HBM(block.shape, block.dtype), pltpu.HBM(land_shape, block.dtype),
                   jax.ShapeDtypeStruct((8, LANES), F32)),
        in_specs=(HBM_SPEC, HBM_SPEC, pl.BlockSpec(memory_space=pl.ANY)),
        out_specs=(SEM_SPEC, SEM_SPEC, HBM_SPEC, HBM_SPEC, pl.BlockSpec(memory_space=pltpu.VMEM)),
        input_output_aliases={0: 2, 1: 3},
        compiler_params=pltpu.CompilerParams(has_side_effects=DATAFLOW),
    )(pltpu.with_memory_space_constraint(block, pltpu.HBM),
      pltpu.with_memory_space_constraint(lax.empty(land_shape, block.dtype), pltpu.HBM), after)


def _gather_wait(send_sems, recv_sems, block, landing, after, name):
    per_peer = block.ndim == 3

    def body(v_ref, land_ref, send_sems, recv_sems, after_ref, v_dead, got_ref):
        for k in range(1, N_DEV):
            peer, pidx = _peer(k)
            copy = pltpu.make_async_remote_copy(
                src_ref=v_ref.at[pidx] if per_peer else v_ref, dst_ref=land_ref.at[pidx],
                send_sem=send_sems.at[k - 1], recv_sem=recv_sems.at[k - 1],
                device_id=peer, device_id_type=MESH)
            copy.wait_send()
            copy.wait_recv()

    return pl.pallas_call(
        body, name=name,
        out_shape=(pltpu.HBM(block.shape, block.dtype), pltpu.HBM(landing.shape, landing.dtype)),
        in_specs=(HBM_SPEC, HBM_SPEC, SEM_SPEC, SEM_SPEC, pl.BlockSpec(memory_space=pl.ANY)),
        out_specs=(HBM_SPEC, HBM_SPEC), input_output_aliases={0: 0, 1: 1},
        compiler_params=pltpu.CompilerParams(has_side_effects=DATAFLOW),
    )(block, landing, send_sems, recv_sems, after)


TILE_ELEMS = 1024 * 1024


def _shared_exchange(shared, name):
    def body(sh_ref, gsh_ref, send_sems, recv_sems, local_sem):
        me = _my_index()
        local = pltpu.make_async_copy(sh_ref, gsh_ref.at[me], local_sem)
        local.start()
        sends = []
        for k in range(1, N_DEV):
            peer, _ = _peer(k)
            cp = pltpu.make_async_remote_copy(
                src_ref=sh_ref, dst_ref=gsh_ref.at[me], send_sem=send_sems.at[k - 1],
                recv_sem=recv_sems.at[k - 1], device_id=peer, device_id_type=MESH)
            cp.start()
            sends.append(cp)
        for k in range(1, N_DEV):
            peer, pidx = _peer(k)
            pltpu.make_async_remote_copy(
                src_ref=sh_ref, dst_ref=gsh_ref.at[pidx], send_sem=send_sems.at[k - 1],
                recv_sem=recv_sems.at[k - 1], device_id=peer, device_id_type=MESH).wait_recv()
        for cp in sends:
            cp.wait_send()
        local.wait()

    hbm = pl.BlockSpec(memory_space=pl.ANY)
    return pl.pallas_call(
        body, name=name, in_specs=[hbm], out_specs=hbm,
        out_shape=jax.ShapeDtypeStruct((N_DEV,) + shared.shape, shared.dtype),
        scratch_shapes=[pltpu.SemaphoreType.DMA((N_DEV - 1,)), pltpu.SemaphoreType.DMA((N_DEV - 1,)),
                        pltpu.SemaphoreType.DMA],
        compiler_params=pltpu.CompilerParams(has_side_effects=True),
    )(shared)


def _adamw(parts, w, m, v, name, row0=0, own=None):
    n_parts, rows, lanes = parts.shape
    tr = rows if rows * lanes <= TILE_ELEMS // 2 else _tile_rows(math.gcd(rows, row0), TILE_ELEMS // 4 // lanes, 8)
    c1 = 1.0 - ADAM_B1 ** ADAM_STEP
    c2 = 1.0 - ADAM_B2 ** ADAM_STEP

    def body(*refs):
        if own is None:
            p_ref, w_ref, m_ref, v_ref, g_ref, d_ref, nm_ref, nv_ref = refs
            terms = [p_ref[j].astype(F32) for j in range(n_parts)]
        else:
            me_ref, p_ref, own_ref, w_ref, m_ref, v_ref, g_ref, d_ref, nm_ref, nv_ref = refs
            terms = [jnp.where(me_ref[0] == j, own_ref[...], p_ref[j]).astype(F32) for j in range(n_parts)]
        g = terms[0]
        for term in terms[1:]:
            g = g + term
        nm = ADAM_B1 * m_ref[...] + (1.0 - ADAM_B1) * g
        nv = ADAM_B2 * v_ref[...] + (1.0 - ADAM_B2) * (g * g)
        g_ref[...] = g
        nm_ref[...] = nm
        nv_ref[...] = nv
        d_ref[...] = -ADAM_LR * ((nm / c1) / (jnp.sqrt(nv / c2) + ADAM_EPS) + ADAM_WD * w_ref[...])

    row = pl.BlockSpec((tr, lanes), lambda i, *_: (i, 0))
    state = pl.BlockSpec((tr, lanes), lambda i, *_: (row0 // tr + i, 0))
    in_specs = [pl.BlockSpec((n_parts, tr, lanes), lambda i, *_: (0, i, 0)), state, state, state]
    args, n_prefetch = (parts, w, m, v), 0
    if own is not None:
        slabs, me = own
        in_specs.insert(1, pl.BlockSpec((None, tr, lanes), lambda i, me_ref: (me_ref[0], i, 0)))
        args, n_prefetch = (me, parts, slabs, w, m, v), 1
    return pl.pallas_call(
        body, name=name,
        grid_spec=pltpu.PrefetchScalarGridSpec(num_scalar_prefetch=n_prefetch, grid=(rows // tr,),
                                               in_specs=in_specs, out_specs=[row] * 4),
        out_shape=[jax.ShapeDtypeStruct((rows, lanes), F32)] * 4,
        compiler_params=_params("parallel"),
    )(*args)


MATRIX_SHARDS = (
    ("w_in", (D_MODEL, IN_PROJ_DIM // N_DEV), True),
    ("w_ssm_out", (SSM_D_INNER // N_DEV, D_MODEL), False),
    ("w_att_out", (ATT_OUT_DIM, D_MODEL // N_DEV), True),
    ("w_mix_out", (D_MODEL // N_DEV, D_MODEL), False),
    ("w_ffn_gate", (D_MODEL, D_FF // N_DEV), True),
    ("w_ffn_up", (D_MODEL, D_FF // N_DEV), True),
    ("w_ffn_down", (D_FF // N_DEV, D_MODEL), False),
)
CONV_SHARD = ("conv_w", (SSM_CONV, SSM_CONV_DIM // N_DEV), True)
SHARDED = MATRIX_SHARDS + (CONV_SHARD,)
REPLICATED = (("norm_mix", D_MODEL), ("b_gate", 2 * D_MODEL), ("conv_b", SSM_CONV_DIM), ("dt_bias", SSM_N_HEADS),
              ("a_log", SSM_N_HEADS), ("d_skip", SSM_N_HEADS), ("ssm_norm", SSM_D_INNER), ("norm_ffn", D_MODEL),
              ("norm_final", D_MODEL))


def _round_up(n, mult):
    return -(-n // mult) * mult


def _pack_rows(flat, row_mult):
    rows = _round_up(-(-flat.shape[0] // LANES), row_mult)
    return jnp.pad(flat, (0, rows * LANES - flat.shape[0])).reshape(rows, LANES)


def _stacking(specs):
    return tuple((name, (shape[1], shape[0]) if by_cols else shape, by_cols) for name, shape, by_cols in specs)


def _to_stacking(vals, specs):
    return {name: (vals[name].T if by_cols else vals[name]) for name, _, by_cols in specs}


STACK_WIDTH = D_MODEL
STACK_ALIGN = 16
STACK_ORDER = ("w_ssm_out", "w_mix_out", "w_ffn_gate", "w_ffn_up", "w_ffn_down", "w_att_out", "conv_w", "w_in")
GATHER_LATER = STACK_ORDER[:-1]
REDUCE_EARLY = STACK_ORDER[:5]
REDUCE_LATE = STACK_ORDER[5:]


def _stack_layout():
    shapes = {name: shape for name, shape, _ in _stacking(SHARDED)}
    layout, off = {}, 0
    for name in STACK_ORDER:
        r, c = shapes[name]
        rows = r if c == STACK_WIDTH else _round_up(-(-(r * c) // STACK_WIDTH), STACK_ALIGN)
        layout[name] = (off, rows, (r, c))
        off = _round_up(off + rows, STACK_ALIGN)
    return layout, _round_up(off, 1024)


def _to_stack_rows(v, rows):
    if v.shape[-1] == STACK_WIDTH:
        return v
    lead = v.shape[:-2]
    flat = v.reshape(lead + (-1,))
    flat = jnp.pad(flat, [(0, 0)] * len(lead) + [(0, rows * STACK_WIDTH - flat.shape[-1])])
    return flat.reshape(lead + (rows, STACK_WIDTH))


def _from_stack_rows(block, shape):
    r, c = shape
    if c == STACK_WIDTH:
        return block
    lead = block.shape[:-2]
    return block.reshape(lead + (-1,))[..., :r * c].reshape(lead + (r, c))


def _stack(vals, dtype, skip=(), names=STACK_ORDER):
    layout, total = _stack_layout()
    order = names
    after = STACK_ORDER.index(order[-1]) + 1
    if after < len(STACK_ORDER):
        total = layout[STACK_ORDER[after]][0]
    lead = next(iter(vals.values())).shape[:-2]
    pieces = []
    for i, name in enumerate(order):
        off, rows, _ = layout[name]
        until = layout[order[i + 1]][0] if i + 1 < len(order) else total
        piece = jnp.zeros(lead + (rows, STACK_WIDTH), dtype) if name in skip else _to_stack_rows(vals[name], rows)
        pieces.append(jnp.pad(piece.astype(dtype), [(0, 0)] * len(lead) + [(0, until - off - rows), (0, 0)]))
    return jnp.concatenate(pieces, axis=-2)


def _unstack(stacked, names):
    layout, _ = _stack_layout()
    row0 = layout[names[0]][0]
    return {name: _from_stack_rows(stacked[..., layout[name][0] - row0:layout[name][0] - row0 + layout[name][1], :],
                                   layout[name][2]) for name in names}


W_IN_SHARD_ROWS = IN_PROJ_DIM // N_DEV


def _w_in_row_moves():
    moves, orig = [], 0
    for name, size in IN_SPLIT:
        for j in range(N_DEV):
            lo, hi = max(orig, W_IN_SHARD_ROWS * j), min(orig + size, W_IN_SHARD_ROWS * (j + 1))
            if lo < hi:
                moves.append((j, lo - W_IN_SHARD_ROWS * j, DPROJ_COLS[name] + lo - orig, hi - lo))
        orig += size
    return moves


def _w_in_from_shards(shards, name):
    total, base = shards.shape[1], 0
    pad_lo, pad_hi = DPROJ_COLS["dt"] + _round_up(SSM_N_HEADS, STACK_ALIGN), DPROJ_COLS["dt"] + DPROJ_DT_WIDTH

    def body(x_ref, o_ref):
        o_ref[pad_lo:pad_hi, :] = jnp.zeros((pad_hi - pad_lo, LANES), x_ref.dtype)
        for j, r, at, n in _w_in_row_moves():
            o_ref[at:at + n, :] = x_ref[j, base + r:base + r + n, :]

    return pl.pallas_call(
        body, name=name, grid=(STACK_WIDTH // LANES,),
        in_specs=[pl.BlockSpec((N_DEV, total, LANES), lambda c: (0, 0, c))],
        out_specs=pl.BlockSpec((DPROJ_WIDTH, LANES), lambda c: (0, c)),
        out_shape=jax.ShapeDtypeStruct((DPROJ_WIDTH, STACK_WIDTH), shards.dtype),
        compiler_params=_params("parallel"),
    )(shards)


def _w_in_to_shards(dw_all, head, name):
    layout, total = _stack_layout()
    total -= layout[REDUCE_LATE[0]][0]
    base = head.shape[1]
    end = base + W_IN_SHARD_ROWS

    def body(x_ref, h_ref, o_ref):
        o_ref[:, 0:base, :] = h_ref[...]
        for j, r, at, n in _w_in_row_moves():
            o_ref[j, base + r:base + r + n, :] = x_ref[at:at + n, :]
        o_ref[:, end:total, :] = jnp.zeros((N_DEV, total - end, LANES), o_ref.dtype)

    return pl.pallas_call(
        body, name=name, grid=(STACK_WIDTH // LANES,),
        in_specs=[pl.BlockSpec((DPROJ_WIDTH, LANES), lambda c: (0, c)),
                  pl.BlockSpec((N_DEV, base, LANES), lambda c: (0, 0, c))],
        out_specs=pl.BlockSpec((N_DEV, total, LANES), lambda c: (0, 0, c)),
        out_shape=jax.ShapeDtypeStruct((N_DEV, total, STACK_WIDTH), dw_all.dtype),
        compiler_params=_params("parallel"),
    )(dw_all, head)


REPLICATED_ROWS = sum(-(-size // LANES) for _, size in REPLICATED)
LOSS_ROW = REPLICATED_ROWS


def _pack_replicated(vals):
    rows = []
    for name, size in REPLICATED:
        v = vals[name].reshape(-1).astype(F32)
        rows.append(jnp.pad(v, (0, _round_up(size, LANES) - size)))
    return _pack_rows(jnp.concatenate(rows), 8)


def _unpack_replicated(packed, shapes):
    flat = packed.reshape(-1)
    out, off = {}, 0
    for name, size in REPLICATED:
        out[name] = flat[off:off + size].reshape(shapes[name])
        off += _round_up(size, LANES)
    return out


def _lane_row(v):
    v = v.reshape(-1).astype(F32)
    return jnp.pad(v, (0, LANES - v.shape[0])).reshape(1, LANES)


IN_SPLIT = (("z", SSM_D_INNER), ("xbc", SSM_CONV_DIM), ("dt", SSM_N_HEADS), ("qkv", ATT_QKV_DIM), ("gate", 2 * D_MODEL))


def kernel(x, norm_mix, w_in, b_gate, conv_w, conv_b, dt_bias, a_log, d_skip, ssm_norm, w_ssm_out, w_att_out, w_mix_out, norm_ffn, w_ffn_gate, w_ffn_up, w_ffn_down, norm_final, loss_target, m_norm_mix, m_w_in, m_b_gate, m_conv_w, m_conv_b, m_dt_bias, m_a_log, m_d_skip, m_ssm_norm, m_w_ssm_out, m_w_att_out, m_w_mix_out, m_norm_ffn, m_w_ffn_gate, m_w_ffn_up, m_w_ffn_down, m_norm_final, v_norm_mix, v_w_in, v_b_gate, v_conv_w, v_conv_b, v_dt_bias, v_a_log, v_d_skip, v_ssm_norm, v_w_ssm_out, v_w_att_out, v_w_mix_out, v_norm_ffn, v_w_ffn_gate, v_w_ffn_up, v_w_ffn_down, v_norm_final):
    given = dict(locals())
    weights = {name: given[name][0] for name, _, _ in SHARDED}
    b, s, d = x.shape
    t = b * s

    stacking = _to_stacking(weights, SHARDED)
    conv_shape = dict((name, shape) for name, shape, _ in _stacking(SHARDED))["conv_w"]
    w_in_local = jnp.pad(stacking["w_in"].astype(BF16), ((0, -W_IN_SHARD_ROWS % STACK_ALIGN), (0, 0)))
    conv_local = _pack_rows(stacking["conv_w"].reshape(-1), 8)
    w_in_shards, conv_all = _all_gather([w_in_local, conv_local], "w_in_all_gather")
    head_local = _stack(stacking, BF16, skip=("conv_w",), names=GATHER_LATER)
    in_flight = _gather_start(head_local, conv_all, "weights_gather_start")
    w_in_all = _w_in_from_shards(w_in_shards, "w_in_from_shards")
    w_sec = {name: w_in_all[DPROJ_COLS[name]:DPROJ_COLS[name] + _round_up(size, LANES)] for name, size in IN_SPLIT}
    conv_size = conv_shape[0] * conv_shape[1]
    conv_taps = conv_all.reshape(N_DEV, -1)[:, :conv_size].reshape(N_DEV * conv_shape[0], conv_shape[1]).T

    g_mix, g_ffn, g_fin = norm_mix.reshape(1, d), norm_ffn.reshape(1, d), norm_final.reshape(1, d)
    g_mix = g_mix + in_flight[4][:1, :1]
    bg_row = b_gate.reshape(1, 2 * d)
    convb_row = conv_b.reshape(1, SSM_CONV_DIM)
    ssmn_row = ssm_norm.reshape(1, SSM_D_INNER)
    dtb_row, alog_row = _lane_row(dt_bias), _lane_row(a_log)
    cosf, sinf = _rope_tables(s)

    x2d = x.reshape(t, d)
    h1 = _rmsnorm_fwd(x2d, g_mix, "norm_mix_fwd")
    proj = {name: _mm(h1, w_sec[name], mode="nt", name="in_proj_" + name) for name, _ in IN_SPLIT if name != "qkv"}
    xbc3 = proj["xbc"].reshape(b, s, SSM_CONV_DIM)
    xc = _conv_fwd(xbc3, conv_taps, convb_row, "conv_fwd")
    dtr3 = proj["dt"].reshape(b, s, DT_PAD)
    to_channels, to_heads = _head_masks()
    dskx = jnp.repeat(d_skip.reshape(-1).astype(F32), SSM_HEAD_DIM).reshape(1, SSM_D_INNER)
    y_ssd, h_states = _ssd_fwd(xc, dtr3, dtb_row, alog_row, dskx, to_channels, "ssd_fwd")
    y_ssd2 = y_ssd.reshape(t, SSM_D_INNER)
    ynorm = _gate_norm_fwd(y_ssd2, proj["z"], ssmn_row, "ssd_gate_norm_fwd")
    head_local, landed = _gather_wait(*in_flight[:4], ynorm, "weights_gather_wait")
    head_all = lax.dynamic_update_slice(landed, head_local[None], (_my_index(), 0, 0))
    full = {name: v.reshape((-1,) + v.shape[2:]) for name, v in _unstack(head_all, STACK_ORDER[:-2]).items()}
    y_ssm = _mm(ynorm, full["w_ssm_out"], mode="nn", name="ssm_out_proj")

    qk_parts = _qkv_proj_rope(h1, w_sec["qkv"], cosf, sinf, b, s, "in_proj_qkv_rope")
    att_parts = [_att_fwd(qk_parts[gi], "att_fwd_%d" % r) for gi, r in enumerate(ATT_DILATIONS)]
    att, *lse_parts = _att_merge([o for o, _ in att_parts], [l_ for _, l_ in att_parts], "att_merge")
    att2 = att.reshape(t, ATT_OUT_DIM)
    y_att, mixed = _att_out_proj_mix(att2, full["w_att_out"], proj["gate"], bg_row, y_ssm, "att_out_proj_mix")
    x2, h2 = _proj_residual_norm(mixed, full["w_mix_out"], x2d, g_ffn, "mix_out_proj_norm")
    gt, up, act = _gate_up_proj_swiglu(h2, full["w_ffn_gate"], full["w_ffn_up"], "ffn_gate_up_proj_swiglu")

    loss_row, dx3, dg_fin, dx3b = _down_proj_loss_head(act, full["w_ffn_down"], x2, g_fin, loss_target.reshape(t, d),
                                                       "ffn_down_proj_loss_head")
    grads = {}
    grads["w_ffn_down"] = _mm(act, dx3b, mode="tn", name="ffn_down_dw", out_dtype=BF16)
    dgt, dup = _down_dx_swiglu_bwd(dx3b, full["w_ffn_down"], gt, up, "ffn_down_dx_swiglu_bwd")
    grads["w_ffn_gate"] = _mm(dgt, h2, mode="tn", name="ffn_gate_dw", out_dtype=BF16)
    grads["w_ffn_up"] = _mm(dup, h2, mode="tn", name="ffn_up_dw", out_dtype=BF16)
    dx2, dg_ffn, dx2b = _proj_norm_bwd(dup, full["w_ffn_up"], x2, g_ffn, dx3, "ffn_gate_up_dx_norm_bwd",
                                       second=(dgt, full["w_ffn_gate"]), with_bf16=True)

    grads["w_mix_out"] = _mm(mixed, dx2b, mode="tn", name="mix_out_dw", out_dtype=BF16)
    dys, dya, dproj, dbg = _mix_out_dx_mix_bwd(dx2b, full["w_mix_out"], proj["gate"], bg_row, y_ssm, y_att,
                                               "mix_out_dx_mix_bwd")

    grads["w_ssm_out"] = _mm(ynorm, dys, mode="tn", name="ssm_out_dw", out_dtype=BF16)
    early = _stack({name: grads[name].reshape((N_DEV, -1, STACK_WIDTH)) for name in REDUCE_EARLY}, BF16,
                   names=REDUCE_EARLY)
    early_flight = _gather_start(early, dys, "grads_scatter_start")
    ssmn_row = ssmn_row + early_flight[4][:1, :1]
    dy_ssd, dproj, dssmn = _ssm_out_dx_gate_norm_bwd(dys, full["w_ssm_out"], y_ssd2, proj["z"], ssmn_row, dproj,
                                                     "ssm_out_dx_gate_norm_bwd")
    dxc, dproj, dalog, ddsk, ddtb = _ssd_bwd(xc, dtr3, dy_ssd.reshape(b, s, SSM_D_INNER), h_states, dtb_row, alog_row,
                                             dskx, to_channels, to_heads, dproj.reshape(b, s, DPROJ_WIDTH), "ssd_bwd")
    dproj, dconvw, dconvb = _conv_bwd(xbc3, dxc, conv_taps, convb_row, dproj, "conv_bwd")
    grads["conv_w"] = dconvw.T.astype(BF16)

    grads["w_att_out"] = _mm(dya, att2, mode="tn", name="att_out_dw", out_dtype=BF16)
    datt = _mm(dya, full["w_att_out"], mode="nn", name="att_out_dx").reshape(b, s, ATT_OUT_DIM)
    do_parts, dl_parts = _att_delta(att, datt, "att_delta")
    dqs, dks, dvs = [], [], []
    for gi, r in enumerate(ATT_DILATIONS):
        operands = (qk_parts[gi], do_parts[gi], lse_parts[gi], dl_parts[gi])
        dqs.append(_att_bwd_q(*operands, "att_bwd_q_%d" % r))
        dk_g, dv_g = _att_bwd_kv(*operands, "att_bwd_kv_%d" % r)
        dks.append(dk_g)
        dvs.append(dv_g)
    dproj = _rope_bwd(dqs, dks, dvs, cosf, sinf, dproj, "rope_bwd").reshape(t, DPROJ_WIDTH)

    dw_all = _mm(dproj, h1, mode="tn", name="in_proj_dw", out_dtype=BF16)
    head = _stack({name: grads[name].reshape((N_DEV, -1, grads[name].shape[-1])) for name in REDUCE_LATE[:-1]}, BF16,
                  names=REDUCE_LATE[:-1])
    late = _w_in_to_shards(dw_all, head, "grad_stacks")
    late_flight = _gather_start(late, dw_all, "grads_late_scatter_start")
    dh1 = _mm(dproj, w_in_all, mode="nn", name="in_proj_dx", after=late_flight[4])
    grad_x, dg_mix = _rmsnorm_bwd(x2d, g_mix, dh1, dx2, "norm_mix_bwd")

    small = {"norm_mix": dg_mix, "b_gate": dbg, "conv_b": dconvb, "dt_bias": ddtb[:, :SSM_N_HEADS],
             "a_log": dalog[:, :SSM_N_HEADS], "d_skip": ddsk[:, :SSM_N_HEADS], "ssm_norm": dssmn,
             "norm_ffn": dg_ffn, "norm_final": dg_fin}
    shared = _pack_replicated(small)
    shared = shared.at[LOSS_ROW, 0].set(loss_row[0, 0])
    got_small = _shared_exchange(shared, "shared_grads_exchange")

    def packed(prefix):
        vals = _to_stacking({name: given[prefix + name][0] for name, _, _ in SHARDED}, SHARDED)
        rep = {name: given[prefix + name] for name, _ in REPLICATED}
        return _stack(vals, F32), _pack_replicated(rep)

    (w_big, w_small), (m_big, m_small), (v_big, v_small) = packed(""), packed("m_"), packed("v_")
    me = _my_index().astype(jnp.int32).reshape(1)
    early, early_landed = _gather_wait(*early_flight[:4], got_small, "grads_scatter_wait")
    late, late_landed = _gather_wait(*late_flight[:4], got_small, "grads_late_scatter_wait")
    big_early = _adamw(early_landed, w_big, m_big, v_big, "adamw_early", own=(early, me))
    big_late = _adamw(late_landed, w_big, m_big, v_big, "adamw_late", row0=early.shape[1], own=(late, me))
    sml = _adamw(got_small, w_small, m_small, v_small, "adamw_replicated")

    outs = [sml[0][LOSS_ROW, 0], grad_x.reshape(b, s, d)]
    rep_shapes = {name: given[name].shape for name, _ in REPLICATED}
    order = ["norm_mix", "w_in", "b_gate", "conv_w", "conv_b", "dt_bias", "a_log", "d_skip", "ssm_norm", "w_ssm_out",
             "w_att_out", "w_mix_out", "norm_ffn", "w_ffn_gate", "w_ffn_up", "w_ffn_down", "norm_final"]
    for early_k, late_k, sml_k in zip(big_early, big_late, sml):
        stacks = dict(_unstack(early_k, REDUCE_EARLY), **_unstack(late_k, REDUCE_LATE))
        sharded = _to_stacking(stacks, SHARDED)
        rep = _unpack_replicated(sml_k, rep_shapes)
        for name in order:
            outs.append(sharded[name][None] if name in sharded else rep[name])
    return tuple(outs)
```
